```python
import math
import jax, jax.numpy as jnp
from jax import lax
import numpy as np

D_MODEL = 1024
BATCH = 8
SEQ = 8192
DEPTH = 4

N_MIXERS = 2
BRANCH = D_MODEL
SSM_GROUP = 16
SSM_GROUPS = BRANCH // SSM_GROUP
SSM_STATE = 64
SSM_CHUNK = 128
DT_MIN = 1e-3
DT_MAX = 1e-1
HEAD_DIM = 64
N_Q_HEADS = BRANCH // HEAD_DIM
N_KV_HEADS = 2
GQA_GROUP = N_Q_HEADS // N_KV_HEADS
WINDOW = 128
ATTN_BLOCK = 128
Q_DIM = N_Q_HEADS * HEAD_DIM
KV_DIM = N_KV_HEADS * HEAD_DIM
ROPE_THETA = 10000.0
NORM_EPS = 1e-5
NEG_INF = -1e30

kernel_name = "hybrid_s5_swa_sink_trunk"


def _rmsnorm(x, g):
    xf = x.astype(jnp.float32)
    y = xf * lax.rsqrt(jnp.mean(xf * xf, axis=-1, keepdims=True) + NORM_EPS)
    return (y * g.astype(jnp.float32)).astype(x.dtype)


def _ssm_combine(e1, e2):
    a1, b1 = e1
    a2, b2 = e2
    return a1 * a2, a2 * b1 + b2


def _s5_scan(u, a_re, a_im, log_step, b_re, b_im, c_re, c_im, d):
    bsz, seq, _ = u.shape
    f32 = jnp.float32
    u = u.astype(f32)
    lam = lax.complex(a_re.astype(f32), a_im.astype(f32))
    step = jnp.exp(log_step.astype(f32))[:, None]
    a_bar = jnp.exp(lam * step)
    b = lax.complex(b_re.astype(f32), b_im.astype(f32))
    b_bar = ((a_bar - 1.0) / lam)[..., None] * b
    c = lax.complex(c_re.astype(f32), c_im.astype(f32))
    n_chunks = seq // SSM_CHUNK
    u_c = u.reshape(bsz, n_chunks, SSM_CHUNK, SSM_GROUPS, SSM_GROUP).transpose(1, 2, 0, 3, 4)

    def step_fn(h_prev, u_blk):
        bu = jnp.einsum("tbgc,gpc->tbgp", u_blk.astype(b_bar.dtype), b_bar)
        a = jnp.broadcast_to(a_bar, bu.shape)
        a_cum, h_loc = lax.associative_scan(_ssm_combine, (a, bu), axis=0)
        h = h_loc + a_cum * h_prev[None]
        y = jnp.real(jnp.einsum("tbgp,gcp->tbgc", h, c))
        return h[-1], y

    h0 = jnp.zeros((bsz, SSM_GROUPS, SSM_STATE), dtype=b_bar.dtype)
    _, y = lax.scan(step_fn, h0, u_c)
    y = y.transpose(2, 0, 1, 3, 4).reshape(bsz, seq, BRANCH)
    return y + d.astype(f32) * u


def _ssm_layer(x, norm, w_in, a_re, a_im, log_step, b_re, b_im, c_re, c_im, d, w_glu, b_glu, w_out):
    f32 = jnp.float32
    h = _rmsnorm(x, norm)
    proj = h @ w_in
    u, gate = jnp.split(proj, [BRANCH], axis=-1)
    y = _s5_scan(u, a_re, a_im, log_step, b_re, b_im, c_re, c_im, d)
    z = jax.nn.gelu(y)
    z = z * jax.nn.sigmoid(z @ w_glu.astype(f32) + b_glu.astype(f32))
    out = (z * jax.nn.silu(gate.astype(f32))).astype(x.dtype) @ w_out
    return x + out


def _rope(t, cos, sin):
    t1, t2 = jnp.split(t, 2, axis=-1)
    return jnp.concatenate([t1 * cos - t2 * sin, t2 * cos + t1 * sin], axis=-1)


def _swa_sinks(q, k, v, sinks):
    f32 = jnp.float32
    bsz, seq = q.shape[:2]
    nb = seq // ATTN_BLOCK
    qb = q.reshape(bsz, nb, ATTN_BLOCK, N_KV_HEADS, GQA_GROUP, HEAD_DIM)
    kb = k.reshape(bsz, nb, ATTN_BLOCK, N_KV_HEADS, HEAD_DIM)
    vb = v.reshape(bsz, nb, ATTN_BLOCK, N_KV_HEADS, HEAD_DIM)

    def with_prev(t):
        prev = jnp.concatenate([jnp.zeros_like(t[:, :1]), t[:, :-1]], axis=1)
        return jnp.concatenate([prev, t], axis=2)

    kk = with_prev(kb)
    vv = with_prev(vb)
    s = jnp.einsum("bnqhgd,bnkhd->bnhgqk", qb, kk) * (HEAD_DIM ** -0.5)
    qi = jnp.arange(ATTN_BLOCK)[:, None]
    kj = jnp.arange(2 * ATTN_BLOCK)[None, :]
    dist = qi + ATTN_BLOCK - kj
    band = (dist >= 0) & (dist < WINDOW)
    blk = jnp.arange(nb)[:, None, None]
    valid = band[None] & ((blk > 0) | (kj[None] >= ATTN_BLOCK))
    s = jnp.where(valid[None, :, None, None], s, NEG_INF)
    sink = sinks.astype(f32).reshape(N_KV_HEADS, GQA_GROUP)[None, None, :, :, None, None]
    m = jnp.maximum(jnp.max(s, axis=-1, keepdims=True), sink)
    p = jnp.exp(s - m)
    denom = jnp.sum(p, axis=-1, keepdims=True) + jnp.exp(sink - m)
    o = jnp.einsum("bnhgqk,bnkhd->bnqhgd", p / denom, vv)
    return o.reshape(bsz, seq, Q_DIM)


def _attn_layer(x, norm, w_in, sinks, w_out):
    f32 = jnp.float32
    bsz, seq, _ = x.shape
    h = _rmsnorm(x, norm)
    proj = (h @ w_in).astype(f32)
    q, k, v, gate = jnp.split(proj, [Q_DIM, Q_DIM + KV_DIM, Q_DIM + 2 * KV_DIM], axis=-1)
    q = q.reshape(bsz, seq, N_Q_HEADS, HEAD_DIM)
    k = k.reshape(bsz, seq, N_KV_HEADS, HEAD_DIM)
    v = v.reshape(bsz, seq, N_KV_HEADS, HEAD_DIM)
    pos = jnp.arange(seq, dtype=f32)
    inv_freq = ROPE_THETA ** (-jnp.arange(0, HEAD_DIM, 2, dtype=f32) / HEAD_DIM)
    ang = pos[:, None] * inv_freq[None, :]
    cos = jnp.cos(ang)[None, :, None, :]
    sin = jnp.sin(ang)[None, :, None, :]
    o = _swa_sinks(_rope(q, cos, sin), _rope(k, cos, sin), v, sinks)
    out = (o * jax.nn.silu(gate)).astype(x.dtype) @ w_out
    return x + out


def _fwd_setup_inputs(seed: int = 0) -> dict:
    key = jax.random.key(seed)
    keys = iter(jax.random.split(key, 64))
    f32 = jnp.float32

    def nrm(shape, scale):
        return jax.random.normal(next(keys), shape, f32) * scale

    inputs = {"x": nrm((BATCH, SEQ, D_MODEL), 1.0)}
    for i in range(DEPTH):
        p = "l%d_" % i
        inputs[p + "norm"] = 1.0 + nrm((D_MODEL,), 0.05)
        if i % N_MIXERS == 0:
            inputs[p + "w_in"] = nrm((D_MODEL, 2 * BRANCH), D_MODEL ** -0.5)
            inputs[p + "a_re"] = -0.5 + nrm((SSM_GROUPS, SSM_STATE), 0.01)
            inputs[p + "a_im"] = math.pi * jnp.arange(SSM_STATE, dtype=f32)[None, :] + nrm((SSM_GROUPS, SSM_STATE), 0.01)
            inputs[p + "log_step"] = jax.random.uniform(next(keys), (SSM_GROUPS,), f32, math.log(DT_MIN), math.log(DT_MAX))
            inputs[p + "b_re"] = nrm((SSM_GROUPS, SSM_STATE, SSM_GROUP), (2 * SSM_GROUP) ** -0.5)
            inputs[p + "b_im"] = nrm((SSM_GROUPS, SSM_STATE, SSM_GROUP), (2 * SSM_GROUP) ** -0.5)
            inputs[p + "c_re"] = nrm((SSM_GROUPS, SSM_GROUP, SSM_STATE), SSM_STATE ** -0.5)
            inputs[p + "c_im"] = nrm((SSM_GROUPS, SSM_GROUP, SSM_STATE), SSM_STATE ** -0.5)
            inputs[p + "d"] = nrm((BRANCH,), 1.0)
            inputs[p + "w_glu"] = nrm((BRANCH, BRANCH), BRANCH ** -0.5)
            inputs[p + "b_glu"] = nrm((BRANCH,), 0.01)
            inputs[p + "w_out"] = nrm((BRANCH, D_MODEL), BRANCH ** -0.5)
        else:
            inputs[p + "w_in"] = nrm((D_MODEL, Q_DIM + 2 * KV_DIM + BRANCH), D_MODEL ** -0.5)
            inputs[p + "sinks"] = nrm((N_Q_HEADS,), 1.0)
            inputs[p + "w_out"] = nrm((Q_DIM, D_MODEL), Q_DIM ** -0.5)
    inputs["final_norm"] = 1.0 + nrm((D_MODEL,), 0.05)
    return inputs


def _fwd_reference(x,
              l0_norm, l0_w_in, l0_a_re, l0_a_im, l0_log_step, l0_b_re, l0_b_im, l0_c_re, l0_c_im, l0_d, l0_w_glu, l0_b_glu, l0_w_out,
              l1_norm, l1_w_in, l1_sinks, l1_w_out,
              l2_norm, l2_w_in, l2_a_re, l2_a_im, l2_log_step, l2_b_re, l2_b_im, l2_c_re, l2_c_im, l2_d, l2_w_glu, l2_b_glu, l2_w_out,
              l3_norm, l3_w_in, l3_sinks, l3_w_out,
              final_norm):
    ssm_params = [
        (l0_norm, l0_w_in, l0_a_re, l0_a_im, l0_log_step, l0_b_re, l0_b_im, l0_c_re, l0_c_im, l0_d, l0_w_glu, l0_b_glu, l0_w_out),
        (l2_norm, l2_w_in, l2_a_re, l2_a_im, l2_log_step, l2_b_re, l2_b_im, l2_c_re, l2_c_im, l2_d, l2_w_glu, l2_b_glu, l2_w_out),
    ]
    attn_params = [
        (l1_norm, l1_w_in, l1_sinks, l1_w_out),
        (l3_norm, l3_w_in, l3_sinks, l3_w_out),
    ]
    for i in range(DEPTH):
        if i % N_MIXERS == 0:
            x = _ssm_layer(x, *ssm_params[i // N_MIXERS])
        else:
            x = _attn_layer(x, *attn_params[i // N_MIXERS])
    return _rmsnorm(x, final_norm)


import jax as _jax
import jax.numpy as _jnp

TWIN_FORMAT = 'train_step'
FWD_PARAMS = ['x', 'l0_norm', 'l0_w_in', 'l0_a_re', 'l0_a_im', 'l0_log_step', 'l0_b_re', 'l0_b_im', 'l0_c_re', 'l0_c_im', 'l0_d', 'l0_w_glu', 'l0_b_glu', 'l0_w_out', 'l1_norm', 'l1_w_in', 'l1_sinks', 'l1_w_out', 'l2_norm', 'l2_w_in', 'l2_a_re', 'l2_a_im', 'l2_log_step', 'l2_b_re', 'l2_b_im', 'l2_c_re', 'l2_c_im', 'l2_d', 'l2_w_glu', 'l2_b_glu', 'l2_w_out', 'l3_norm', 'l3_w_in', 'l3_sinks', 'l3_w_out', 'final_norm']
TWIN_WEIGHTS = ['l0_norm', 'l0_w_in', 'l0_a_re', 'l0_a_im', 'l0_log_step', 'l0_b_re', 'l0_b_im', 'l0_c_re', 'l0_c_im', 'l0_d', 'l0_w_glu', 'l0_b_glu', 'l0_w_out', 'l1_norm', 'l1_w_in', 'l1_sinks', 'l1_w_out', 'l2_norm', 'l2_w_in', 'l2_a_re', 'l2_a_im', 'l2_log_step', 'l2_b_re', 'l2_b_im', 'l2_c_re', 'l2_c_im', 'l2_d', 'l2_w_glu', 'l2_b_glu', 'l2_w_out', 'l3_norm', 'l3_w_in', 'l3_sinks', 'l3_w_out', 'final_norm']
TWIN_DIFF_INPUT = 'x'
TWIN_INPUTS = ['x', 'l0_norm', 'l0_w_in', 'l0_a_re', 'l0_a_im', 'l0_log_step', 'l0_b_re', 'l0_b_im', 'l0_c_re', 'l0_c_im', 'l0_d', 'l0_w_glu', 'l0_b_glu', 'l0_w_out', 'l1_norm', 'l1_w_in', 'l1_sinks', 'l1_w_out', 'l2_norm', 'l2_w_in', 'l2_a_re', 'l2_a_im', 'l2_log_step', 'l2_b_re', 'l2_b_im', 'l2_c_re', 'l2_c_im', 'l2_d', 'l2_w_glu', 'l2_b_glu', 'l2_w_out', 'l3_norm', 'l3_w_in', 'l3_sinks', 'l3_w_out', 'final_norm', 'loss_target', 'm_l0_norm', 'm_l0_w_in', 'm_l0_a_re', 'm_l0_a_im', 'm_l0_log_step', 'm_l0_b_re', 'm_l0_b_im', 'm_l0_c_re', 'm_l0_c_im', 'm_l0_d', 'm_l0_w_glu', 'm_l0_b_glu', 'm_l0_w_out', 'm_l1_norm', 'm_l1_w_in', 'm_l1_sinks', 'm_l1_w_out', 'm_l2_norm', 'm_l2_w_in', 'm_l2_a_re', 'm_l2_a_im', 'm_l2_log_step', 'm_l2_b_re', 'm_l2_b_im', 'm_l2_c_re', 'm_l2_c_im', 'm_l2_d', 'm_l2_w_glu', 'm_l2_b_glu', 'm_l2_w_out', 'm_l3_norm', 'm_l3_w_in', 'm_l3_sinks', 'm_l3_w_out', 'm_final_norm', 'v_l0_norm', 'v_l0_w_in', 'v_l0_a_re', 'v_l0_a_im', 'v_l0_log_step', 'v_l0_b_re', 'v_l0_b_im', 'v_l0_c_re', 'v_l0_c_im', 'v_l0_d', 'v_l0_w_glu', 'v_l0_b_glu', 'v_l0_w_out', 'v_l1_norm', 'v_l1_w_in', 'v_l1_sinks', 'v_l1_w_out', 'v_l2_norm', 'v_l2_w_in', 'v_l2_a_re', 'v_l2_a_im', 'v_l2_log_step', 'v_l2_b_re', 'v_l2_b_im', 'v_l2_c_re', 'v_l2_c_im', 'v_l2_d', 'v_l2_w_glu', 'v_l2_b_glu', 'v_l2_w_out', 'v_l3_norm', 'v_l3_w_in', 'v_l3_sinks', 'v_l3_w_out', 'v_final_norm']
TWIN_OUTPUTS = ['loss', 'grad_x', 'grad_l0_norm', 'grad_l0_w_in', 'grad_l0_a_re', 'grad_l0_a_im', 'grad_l0_log_step', 'grad_l0_b_re', 'grad_l0_b_im', 'grad_l0_c_re', 'grad_l0_c_im', 'grad_l0_d', 'grad_l0_w_glu', 'grad_l0_b_glu', 'grad_l0_w_out', 'grad_l1_norm', 'grad_l1_w_in', 'grad_l1_sinks', 'grad_l1_w_out', 'grad_l2_norm', 'grad_l2_w_in', 'grad_l2_a_re', 'grad_l2_a_im', 'grad_l2_log_step', 'grad_l2_b_re', 'grad_l2_b_im', 'grad_l2_c_re', 'grad_l2_c_im', 'grad_l2_d', 'grad_l2_w_glu', 'grad_l2_b_glu', 'grad_l2_w_out', 'grad_l3_norm', 'grad_l3_w_in', 'grad_l3_sinks', 'grad_l3_w_out', 'grad_final_norm', 'delta_l0_norm', 'delta_l0_w_in', 'delta_l0_a_re', 'delta_l0_a_im', 'delta_l0_log_step', 'delta_l0_b_re', 'delta_l0_b_im', 'delta_l0_c_re', 'delta_l0_c_im', 'delta_l0_d', 'delta_l0_w_glu', 'delta_l0_b_glu', 'delta_l0_w_out', 'delta_l1_norm', 'delta_l1_w_in', 'delta_l1_sinks', 'delta_l1_w_out', 'delta_l2_norm', 'delta_l2_w_in', 'delta_l2_a_re', 'delta_l2_a_im', 'delta_l2_log_step', 'delta_l2_b_re', 'delta_l2_b_im', 'delta_l2_c_re', 'delta_l2_c_im', 'delta_l2_d', 'delta_l2_w_glu', 'delta_l2_b_glu', 'delta_l2_w_out', 'delta_l3_norm', 'delta_l3_w_in', 'delta_l3_sinks', 'delta_l3_w_out', 'delta_final_norm', 'new_m_l0_norm', 'new_m_l0_w_in', 'new_m_l0_a_re', 'new_m_l0_a_im', 'new_m_l0_log_step', 'new_m_l0_b_re', 'new_m_l0_b_im', 'new_m_l0_c_re', 'new_m_l0_c_im', 'new_m_l0_d', 'new_m_l0_w_glu', 'new_m_l0_b_glu', 'new_m_l0_w_out', 'new_m_l1_norm', 'new_m_l1_w_in', 'new_m_l1_sinks', 'new_m_l1_w_out', 'new_m_l2_norm', 'new_m_l2_w_in', 'new_m_l2_a_re', 'new_m_l2_a_im', 'new_m_l2_log_step', 'new_m_l2_b_re', 'new_m_l2_b_im', 'new_m_l2_c_re', 'new_m_l2_c_im', 'new_m_l2_d', 'new_m_l2_w_glu', 'new_m_l2_b_glu', 'new_m_l2_w_out', 'new_m_l3_norm', 'new_m_l3_w_in', 'new_m_l3_sinks', 'new_m_l3_w_out', 'new_m_final_norm', 'new_v_l0_norm', 'new_v_l0_w_in', 'new_v_l0_a_re', 'new_v_l0_a_im', 'new_v_l0_log_step', 'new_v_l0_b_re', 'new_v_l0_b_im', 'new_v_l0_c_re', 'new_v_l0_c_im', 'new_v_l0_d', 'new_v_l0_w_glu', 'new_v_l0_b_glu', 'new_v_l0_w_out', 'new_v_l1_norm', 'new_v_l1_w_in', 'new_v_l1_sinks', 'new_v_l1_w_out', 'new_v_l2_norm', 'new_v_l2_w_in', 'new_v_l2_a_re', 'new_v_l2_a_im', 'new_v_l2_log_step', 'new_v_l2_b_re', 'new_v_l2_b_im', 'new_v_l2_c_re', 'new_v_l2_c_im', 'new_v_l2_d', 'new_v_l2_w_glu', 'new_v_l2_b_glu', 'new_v_l2_w_out', 'new_v_l3_norm', 'new_v_l3_w_in', 'new_v_l3_sinks', 'new_v_l3_w_out', 'new_v_final_norm']
TWIN_LEAF_KINDS = {'loss': 'loss', 'grad_x': 'grad_x', 'grad_l0_norm': 'grad_w', 'grad_l0_w_in': 'grad_w', 'grad_l0_a_re': 'grad_w', 'grad_l0_a_im': 'grad_w', 'grad_l0_log_step': 'grad_w', 'grad_l0_b_re': 'grad_w', 'grad_l0_b_im': 'grad_w', 'grad_l0_c_re': 'grad_w', 'grad_l0_c_im': 'grad_w', 'grad_l0_d': 'grad_w', 'grad_l0_w_glu': 'grad_w', 'grad_l0_b_glu': 'grad_w', 'grad_l0_w_out': 'grad_w', 'grad_l1_norm': 'grad_w', 'grad_l1_w_in': 'grad_w', 'grad_l1_sinks': 'grad_w', 'grad_l1_w_out': 'grad_w', 'grad_l2_norm': 'grad_w', 'grad_l2_w_in': 'grad_w', 'grad_l2_a_re': 'grad_w', 'grad_l2_a_im': 'grad_w', 'grad_l2_log_step': 'grad_w', 'grad_l2_b_re': 'grad_w', 'grad_l2_b_im': 'grad_w', 'grad_l2_c_re': 'grad_w', 'grad_l2_c_im': 'grad_w', 'grad_l2_d': 'grad_w', 'grad_l2_w_glu': 'grad_w', 'grad_l2_b_glu': 'grad_w', 'grad_l2_w_out': 'grad_w', 'grad_l3_norm': 'grad_w', 'grad_l3_w_in': 'grad_w', 'grad_l3_sinks': 'grad_w', 'grad_l3_w_out': 'grad_w', 'grad_final_norm': 'grad_w', 'delta_l0_norm': 'delta_w', 'delta_l0_w_in': 'delta_w', 'delta_l0_a_re': 'delta_w', 'delta_l0_a_im': 'delta_w', 'delta_l0_log_step': 'delta_w', 'delta_l0_b_re': 'delta_w', 'delta_l0_b_im': 'delta_w', 'delta_l0_c_re': 'delta_w', 'delta_l0_c_im': 'delta_w', 'delta_l0_d': 'delta_w', 'delta_l0_w_glu': 'delta_w', 'delta_l0_b_glu': 'delta_w', 'delta_l0_w_out': 'delta_w', 'delta_l1_norm': 'delta_w', 'delta_l1_w_in': 'delta_w', 'delta_l1_sinks': 'delta_w', 'delta_l1_w_out': 'delta_w', 'delta_l2_norm': 'delta_w', 'delta_l2_w_in': 'delta_w', 'delta_l2_a_re': 'delta_w', 'delta_l2_a_im': 'delta_w', 'delta_l2_log_step': 'delta_w', 'delta_l2_b_re': 'delta_w', 'delta_l2_b_im': 'delta_w', 'delta_l2_c_re': 'delta_w', 'delta_l2_c_im': 'delta_w', 'delta_l2_d': 'delta_w', 'delta_l2_w_glu': 'delta_w', 'delta_l2_b_glu': 'delta_w', 'delta_l2_w_out': 'delta_w', 'delta_l3_norm': 'delta_w', 'delta_l3_w_in': 'delta_w', 'delta_l3_sinks': 'delta_w', 'delta_l3_w_out': 'delta_w', 'delta_final_norm': 'delta_w', 'new_m_l0_norm': 'new_m', 'new_m_l0_w_in': 'new_m', 'new_m_l0_a_re': 'new_m', 'new_m_l0_a_im': 'new_m', 'new_m_l0_log_step': 'new_m', 'new_m_l0_b_re': 'new_m', 'new_m_l0_b_im': 'new_m', 'new_m_l0_c_re': 'new_m', 'new_m_l0_c_im': 'new_m', 'new_m_l0_d': 'new_m', 'new_m_l0_w_glu': 'new_m', 'new_m_l0_b_glu': 'new_m', 'new_m_l0_w_out': 'new_m', 'new_m_l1_norm': 'new_m', 'new_m_l1_w_in': 'new_m', 'new_m_l1_sinks': 'new_m', 'new_m_l1_w_out': 'new_m', 'new_m_l2_norm': 'new_m', 'new_m_l2_w_in': 'new_m', 'new_m_l2_a_re': 'new_m', 'new_m_l2_a_im': 'new_m', 'new_m_l2_log_step': 'new_m', 'new_m_l2_b_re': 'new_m', 'new_m_l2_b_im': 'new_m', 'new_m_l2_c_re': 'new_m', 'new_m_l2_c_im': 'new_m', 'new_m_l2_d': 'new_m', 'new_m_l2_w_glu': 'new_m', 'new_m_l2_b_glu': 'new_m', 'new_m_l2_w_out': 'new_m', 'new_m_l3_norm': 'new_m', 'new_m_l3_w_in': 'new_m', 'new_m_l3_sinks': 'new_m', 'new_m_l3_w_out': 'new_m', 'new_m_final_norm': 'new_m', 'new_v_l0_norm': 'new_v', 'new_v_l0_w_in': 'new_v', 'new_v_l0_a_re': 'new_v', 'new_v_l0_a_im': 'new_v', 'new_v_l0_log_step': 'new_v', 'new_v_l0_b_re': 'new_v', 'new_v_l0_b_im': 'new_v', 'new_v_l0_c_re': 'new_v', 'new_v_l0_c_im': 'new_v', 'new_v_l0_d': 'new_v', 'new_v_l0_w_glu': 'new_v', 'new_v_l0_b_glu': 'new_v', 'new_v_l0_w_out': 'new_v', 'new_v_l1_norm': 'new_v', 'new_v_l1_w_in': 'new_v', 'new_v_l1_sinks': 'new_v', 'new_v_l1_w_out': 'new_v', 'new_v_l2_norm': 'new_v', 'new_v_l2_w_in': 'new_v', 'new_v_l2_a_re': 'new_v', 'new_v_l2_a_im': 'new_v', 'new_v_l2_log_step': 'new_v', 'new_v_l2_b_re': 'new_v', 'new_v_l2_b_im': 'new_v', 'new_v_l2_c_re': 'new_v', 'new_v_l2_c_im': 'new_v', 'new_v_l2_d': 'new_v', 'new_v_l2_w_glu': 'new_v', 'new_v_l2_b_glu': 'new_v', 'new_v_l2_w_out': 'new_v', 'new_v_l3_norm': 'new_v', 'new_v_l3_w_in': 'new_v', 'new_v_l3_sinks': 'new_v', 'new_v_l3_w_out': 'new_v', 'new_v_final_norm': 'new_v'}


def _forward(args):
    return _fwd_reference(*[args[k] for k in FWD_PARAMS])


def _output_shape():
    out = _jax.eval_shape(lambda: _forward(_fwd_setup_inputs(0)))
    return out.shape, out.dtype

N_MICROBATCH = 1
ADAM_LR = 0.001
ADAM_B1 = 0.9
ADAM_B2 = 0.999
ADAM_EPS = 1e-08
ADAM_WD = 0.01
ADAM_STEP = 10
PER_EXAMPLE_BATCH_AXIS = {'x': 0, 'loss_target': 0}
SHARED_INPUTS = []
_WEIGHT_DTYPES = {'l0_norm': _jnp.float32, 'l0_w_in': _jnp.float32, 'l0_a_re': _jnp.float32, 'l0_a_im': _jnp.float32, 'l0_log_step': _jnp.float32, 'l0_b_re': _jnp.float32, 'l0_b_im': _jnp.float32, 'l0_c_re': _jnp.float32, 'l0_c_im': _jnp.float32, 'l0_d': _jnp.float32, 'l0_w_glu': _jnp.float32, 'l0_b_glu': _jnp.float32, 'l0_w_out': _jnp.float32, 'l1_norm': _jnp.float32, 'l1_w_in': _jnp.float32, 'l1_sinks': _jnp.float32, 'l1_w_out': _jnp.float32, 'l2_norm': _jnp.float32, 'l2_w_in': _jnp.float32, 'l2_a_re': _jnp.float32, 'l2_a_im': _jnp.float32, 'l2_log_step': _jnp.float32, 'l2_b_re': _jnp.float32, 'l2_b_im': _jnp.float32, 'l2_c_re': _jnp.float32, 'l2_c_im': _jnp.float32, 'l2_d': _jnp.float32, 'l2_w_glu': _jnp.float32, 'l2_b_glu': _jnp.float32, 'l2_w_out': _jnp.float32, 'l3_norm': _jnp.float32, 'l3_w_in': _jnp.float32, 'l3_sinks': _jnp.float32, 'l3_w_out': _jnp.float32, 'final_norm': _jnp.float32}
MOMENT_SCALE = {'l0_norm': 8.140539e-02, 'l0_w_in': 5.690790e-02, 'l0_a_re': 4.911624e-03, 'l0_a_im': 4.827816e-03, 'l0_log_step': 1.979970e+00, 'l0_b_re': 2.745182e-03, 'l0_b_im': 2.774903e-03, 'l0_c_re': 4.140473e-03, 'l0_c_im': 4.030641e-03, 'l0_d': 6.183264e-02, 'l0_w_glu': 1.636117e-02, 'l0_b_glu': 2.749175e-02, 'l0_w_out': 5.439130e-02, 'l1_norm': 6.147718e-02, 'l1_w_in': 4.034186e-02, 'l1_sinks': 2.742613e-02, 'l1_w_out': 3.195630e-02, 'l2_norm': 7.294461e-02, 'l2_w_in': 5.178785e-02, 'l2_a_re': 5.191075e-03, 'l2_a_im': 4.520650e-03, 'l2_log_step': 6.526003e+00, 'l2_b_re': 2.677687e-03, 'l2_b_im': 2.641005e-03, 'l2_c_re': 3.770319e-03, 'l2_c_im': 3.645860e-03, 'l2_d': 5.590213e-02, 'l2_w_glu': 1.365442e-02, 'l2_b_glu': 2.174752e-02, 'l2_w_out': 5.055036e-02, 'l3_norm': 5.322910e-02, 'l3_w_in': 3.696152e-02, 'l3_sinks': 2.157409e-02, 'l3_w_out': 2.961170e-02, 'final_norm': 6.402724e+01}


def _to_microbatches(a, axis):
    t = _jnp.moveaxis(a, axis, 0)
    t = t.reshape((N_MICROBATCH, t.shape[0] // N_MICROBATCH) + t.shape[1:])
    return _jnp.moveaxis(t, 1, axis + 1)


def setup_inputs(seed: int = 0) -> dict:
    inp = _fwd_setup_inputs(seed)
    key = _jax.random.fold_in(_jax.random.key(seed), 7919)
    shape, _ = _output_shape()
    out = dict(inp)
    out["loss_target"] = _jax.random.normal(_jax.random.fold_in(key, 0), shape, _jnp.float32)
    for i, name in enumerate(TWIN_WEIGHTS):
        w = inp[name].astype(_jnp.float32)
        if MOMENT_SCALE is None:
            s = _jnp.sqrt(_jnp.mean(_jnp.square(w)) + 1e-30)
        else:
            s = MOMENT_SCALE[name]
        km, kv = _jax.random.split(_jax.random.fold_in(key, i + 1))
        out[name] = w
        out["m_" + name] = s * _jax.random.normal(km, w.shape, _jnp.float32)
        out["v_" + name] = (s * s) * _jax.random.uniform(kv, w.shape, _jnp.float32, 0.5, 1.5)
    if N_MICROBATCH > 1:
        for name, axis in PER_EXAMPLE_BATCH_AXIS.items():
            out[name] = _to_microbatches(out[name], axis)
    return {'x': out['x'], 'l0_norm': out['l0_norm'], 'l0_w_in': out['l0_w_in'], 'l0_a_re': out['l0_a_re'], 'l0_a_im': out['l0_a_im'], 'l0_log_step': out['l0_log_step'], 'l0_b_re': out['l0_b_re'], 'l0_b_im': out['l0_b_im'], 'l0_c_re': out['l0_c_re'], 'l0_c_im': out['l0_c_im'], 'l0_d': out['l0_d'], 'l0_w_glu': out['l0_w_glu'], 'l0_b_glu': out['l0_b_glu'], 'l0_w_out': out['l0_w_out'], 'l1_norm': out['l1_norm'], 'l1_w_in': out['l1_w_in'], 'l1_sinks': out['l1_sinks'], 'l1_w_out': out['l1_w_out'], 'l2_norm': out['l2_norm'], 'l2_w_in': out['l2_w_in'], 'l2_a_re': out['l2_a_re'], 'l2_a_im': out['l2_a_im'], 'l2_log_step': out['l2_log_step'], 'l2_b_re': out['l2_b_re'], 'l2_b_im': out['l2_b_im'], 'l2_c_re': out['l2_c_re'], 'l2_c_im': out['l2_c_im'], 'l2_d': out['l2_d'], 'l2_w_glu': out['l2_w_glu'], 'l2_b_glu': out['l2_b_glu'], 'l2_w_out': out['l2_w_out'], 'l3_norm': out['l3_norm'], 'l3_w_in': out['l3_w_in'], 'l3_sinks': out['l3_sinks'], 'l3_w_out': out['l3_w_out'], 'final_norm': out['final_norm'], 'loss_target': out['loss_target'], 'm_l0_norm': out['m_l0_norm'], 'm_l0_w_in': out['m_l0_w_in'], 'm_l0_a_re': out['m_l0_a_re'], 'm_l0_a_im': out['m_l0_a_im'], 'm_l0_log_step': out['m_l0_log_step'], 'm_l0_b_re': out['m_l0_b_re'], 'm_l0_b_im': out['m_l0_b_im'], 'm_l0_c_re': out['m_l0_c_re'], 'm_l0_c_im': out['m_l0_c_im'], 'm_l0_d': out['m_l0_d'], 'm_l0_w_glu': out['m_l0_w_glu'], 'm_l0_b_glu': out['m_l0_b_glu'], 'm_l0_w_out': out['m_l0_w_out'], 'm_l1_norm': out['m_l1_norm'], 'm_l1_w_in': out['m_l1_w_in'], 'm_l1_sinks': out['m_l1_sinks'], 'm_l1_w_out': out['m_l1_w_out'], 'm_l2_norm': out['m_l2_norm'], 'm_l2_w_in': out['m_l2_w_in'], 'm_l2_a_re': out['m_l2_a_re'], 'm_l2_a_im': out['m_l2_a_im'], 'm_l2_log_step': out['m_l2_log_step'], 'm_l2_b_re': out['m_l2_b_re'], 'm_l2_b_im': out['m_l2_b_im'], 'm_l2_c_re': out['m_l2_c_re'], 'm_l2_c_im': out['m_l2_c_im'], 'm_l2_d': out['m_l2_d'], 'm_l2_w_glu': out['m_l2_w_glu'], 'm_l2_b_glu': out['m_l2_b_glu'], 'm_l2_w_out': out['m_l2_w_out'], 'm_l3_norm': out['m_l3_norm'], 'm_l3_w_in': out['m_l3_w_in'], 'm_l3_sinks': out['m_l3_sinks'], 'm_l3_w_out': out['m_l3_w_out'], 'm_final_norm': out['m_final_norm'], 'v_l0_norm': out['v_l0_norm'], 'v_l0_w_in': out['v_l0_w_in'], 'v_l0_a_re': out['v_l0_a_re'], 'v_l0_a_im': out['v_l0_a_im'], 'v_l0_log_step': out['v_l0_log_step'], 'v_l0_b_re': out['v_l0_b_re'], 'v_l0_b_im': out['v_l0_b_im'], 'v_l0_c_re': out['v_l0_c_re'], 'v_l0_c_im': out['v_l0_c_im'], 'v_l0_d': out['v_l0_d'], 'v_l0_w_glu': out['v_l0_w_glu'], 'v_l0_b_glu': out['v_l0_b_glu'], 'v_l0_w_out': out['v_l0_w_out'], 'v_l1_norm': out['v_l1_norm'], 'v_l1_w_in': out['v_l1_w_in'], 'v_l1_sinks': out['v_l1_sinks'], 'v_l1_w_out': out['v_l1_w_out'], 'v_l2_norm': out['v_l2_norm'], 'v_l2_w_in': out['v_l2_w_in'], 'v_l2_a_re': out['v_l2_a_re'], 'v_l2_a_im': out['v_l2_a_im'], 'v_l2_log_step': out['v_l2_log_step'], 'v_l2_b_re': out['v_l2_b_re'], 'v_l2_b_im': out['v_l2_b_im'], 'v_l2_c_re': out['v_l2_c_re'], 'v_l2_c_im': out['v_l2_c_im'], 'v_l2_d': out['v_l2_d'], 'v_l2_w_glu': out['v_l2_w_glu'], 'v_l2_b_glu': out['v_l2_b_glu'], 'v_l2_w_out': out['v_l2_w_out'], 'v_l3_norm': out['v_l3_norm'], 'v_l3_w_in': out['v_l3_w_in'], 'v_l3_sinks': out['v_l3_sinks'], 'v_l3_w_out': out['v_l3_w_out'], 'v_final_norm': out['v_final_norm']}


def _loss(weights, diff, rest, loss_target):
    with _jax.named_scope("forward"):
        args = {**rest, TWIN_DIFF_INPUT: diff, **{k: w.astype(_WEIGHT_DTYPES[k]) for k, w in weights.items()}}
        y = _forward(args)
    with _jax.named_scope("loss_head"):
        err = _jnp.square(y.astype(_jnp.float32) - loss_target)
        return 0.5 * _jnp.sum(_jnp.mean(err, axis=-1)) if err.ndim else 0.5 * err


def _adamw(w, g, m, v):
    m = ADAM_B1 * m + (1.0 - ADAM_B1) * g
    v = ADAM_B2 * v + (1.0 - ADAM_B2) * _jnp.square(g)
    m_hat = m / (1.0 - ADAM_B1 ** ADAM_STEP)
    v_hat = v / (1.0 - ADAM_B2 ** ADAM_STEP)
    delta = -ADAM_LR * (m_hat / (_jnp.sqrt(v_hat) + ADAM_EPS) + ADAM_WD * w)
    return delta, m, v


def reference(x, l0_norm, l0_w_in, l0_a_re, l0_a_im, l0_log_step, l0_b_re, l0_b_im, l0_c_re, l0_c_im, l0_d, l0_w_glu, l0_b_glu, l0_w_out, l1_norm, l1_w_in, l1_sinks, l1_w_out, l2_norm, l2_w_in, l2_a_re, l2_a_im, l2_log_step, l2_b_re, l2_b_im, l2_c_re, l2_c_im, l2_d, l2_w_glu, l2_b_glu, l2_w_out, l3_norm, l3_w_in, l3_sinks, l3_w_out, final_norm, loss_target, m_l0_norm, m_l0_w_in, m_l0_a_re, m_l0_a_im, m_l0_log_step, m_l0_b_re, m_l0_b_im, m_l0_c_re, m_l0_c_im, m_l0_d, m_l0_w_glu, m_l0_b_glu, m_l0_w_out, m_l1_norm, m_l1_w_in, m_l1_sinks, m_l1_w_out, m_l2_norm, m_l2_w_in, m_l2_a_re, m_l2_a_im, m_l2_log_step, m_l2_b_re, m_l2_b_im, m_l2_c_re, m_l2_c_im, m_l2_d, m_l2_w_glu, m_l2_b_glu, m_l2_w_out, m_l3_norm, m_l3_w_in, m_l3_sinks, m_l3_w_out, m_final_norm, v_l0_norm, v_l0_w_in, v_l0_a_re, v_l0_a_im, v_l0_log_step, v_l0_b_re, v_l0_b_im, v_l0_c_re, v_l0_c_im, v_l0_d, v_l0_w_glu, v_l0_b_glu, v_l0_w_out, v_l1_norm, v_l1_w_in, v_l1_sinks, v_l1_w_out, v_l2_norm, v_l2_w_in, v_l2_a_re, v_l2_a_im, v_l2_log_step, v_l2_b_re, v_l2_b_im, v_l2_c_re, v_l2_c_im, v_l2_d, v_l2_w_glu, v_l2_b_glu, v_l2_w_out, v_l3_norm, v_l3_w_in, v_l3_sinks, v_l3_w_out, v_final_norm):
    given = dict(x=x, l0_norm=l0_norm, l0_w_in=l0_w_in, l0_a_re=l0_a_re, l0_a_im=l0_a_im, l0_log_step=l0_log_step, l0_b_re=l0_b_re, l0_b_im=l0_b_im, l0_c_re=l0_c_re, l0_c_im=l0_c_im, l0_d=l0_d, l0_w_glu=l0_w_glu, l0_b_glu=l0_b_glu, l0_w_out=l0_w_out, l1_norm=l1_norm, l1_w_in=l1_w_in, l1_sinks=l1_sinks, l1_w_out=l1_w_out, l2_norm=l2_norm, l2_w_in=l2_w_in, l2_a_re=l2_a_re, l2_a_im=l2_a_im, l2_log_step=l2_log_step, l2_b_re=l2_b_re, l2_b_im=l2_b_im, l2_c_re=l2_c_re, l2_c_im=l2_c_im, l2_d=l2_d, l2_w_glu=l2_w_glu, l2_b_glu=l2_b_glu, l2_w_out=l2_w_out, l3_norm=l3_norm, l3_w_in=l3_w_in, l3_sinks=l3_sinks, l3_w_out=l3_w_out, final_norm=final_norm, loss_target=loss_target, m_l0_norm=m_l0_norm, m_l0_w_in=m_l0_w_in, m_l0_a_re=m_l0_a_re, m_l0_a_im=m_l0_a_im, m_l0_log_step=m_l0_log_step, m_l0_b_re=m_l0_b_re, m_l0_b_im=m_l0_b_im, m_l0_c_re=m_l0_c_re, m_l0_c_im=m_l0_c_im, m_l0_d=m_l0_d, m_l0_w_glu=m_l0_w_glu, m_l0_b_glu=m_l0_b_glu, m_l0_w_out=m_l0_w_out, m_l1_norm=m_l1_norm, m_l1_w_in=m_l1_w_in, m_l1_sinks=m_l1_sinks, m_l1_w_out=m_l1_w_out, m_l2_norm=m_l2_norm, m_l2_w_in=m_l2_w_in, m_l2_a_re=m_l2_a_re, m_l2_a_im=m_l2_a_im, m_l2_log_step=m_l2_log_step, m_l2_b_re=m_l2_b_re, m_l2_b_im=m_l2_b_im, m_l2_c_re=m_l2_c_re, m_l2_c_im=m_l2_c_im, m_l2_d=m_l2_d, m_l2_w_glu=m_l2_w_glu, m_l2_b_glu=m_l2_b_glu, m_l2_w_out=m_l2_w_out, m_l3_norm=m_l3_norm, m_l3_w_in=m_l3_w_in, m_l3_sinks=m_l3_sinks, m_l3_w_out=m_l3_w_out, m_final_norm=m_final_norm, v_l0_norm=v_l0_norm, v_l0_w_in=v_l0_w_in, v_l0_a_re=v_l0_a_re, v_l0_a_im=v_l0_a_im, v_l0_log_step=v_l0_log_step, v_l0_b_re=v_l0_b_re, v_l0_b_im=v_l0_b_im, v_l0_c_re=v_l0_c_re, v_l0_c_im=v_l0_c_im, v_l0_d=v_l0_d, v_l0_w_glu=v_l0_w_glu, v_l0_b_glu=v_l0_b_glu, v_l0_w_out=v_l0_w_out, v_l1_norm=v_l1_norm, v_l1_w_in=v_l1_w_in, v_l1_sinks=v_l1_sinks, v_l1_w_out=v_l1_w_out, v_l2_norm=v_l2_norm, v_l2_w_in=v_l2_w_in, v_l2_a_re=v_l2_a_re, v_l2_a_im=v_l2_a_im, v_l2_log_step=v_l2_log_step, v_l2_b_re=v_l2_b_re, v_l2_b_im=v_l2_b_im, v_l2_c_re=v_l2_c_re, v_l2_c_im=v_l2_c_im, v_l2_d=v_l2_d, v_l2_w_glu=v_l2_w_glu, v_l2_b_glu=v_l2_b_glu, v_l2_w_out=v_l2_w_out, v_l3_norm=v_l3_norm, v_l3_w_in=v_l3_w_in, v_l3_sinks=v_l3_sinks, v_l3_w_out=v_l3_w_out, v_final_norm=v_final_norm)
    weights = {n: given[n] for n in TWIN_WEIGHTS}
    shared = {n: given[n] for n in SHARED_INPUTS}
    per_example = {n: given[n] for n in ['x']}
    grad_fn = _jax.value_and_grad(_loss, argnums=(0, 1))

    def one_microbatch(ex, loss_target):
        ex = dict(ex)
        diff = ex.pop(TWIN_DIFF_INPUT)
        return grad_fn(weights, diff, {**shared, **ex}, loss_target)

    if N_MICROBATCH == 1:
        loss, (grad_w, grad_x) = one_microbatch(per_example, given["loss_target"])
    else:
        def body(carry, xs):
            loss_sum, grad_sum = carry
            l_k, (gw_k, gx_k) = one_microbatch(xs[0], xs[1])
            with _jax.named_scope("update"):
                return (loss_sum + l_k, _jax.tree.map(_jnp.add, grad_sum, gw_k)), gx_k

        init = (_jnp.zeros((), _jnp.float32), _jax.tree.map(_jnp.zeros_like, weights))
        (loss, grad_w), grad_x = _jax.lax.scan(body, init, (per_example, given["loss_target"]))
    with _jax.named_scope("update"):
        delta_w, new_m, new_v = {}, {}, {}
        for n in TWIN_WEIGHTS:
            delta_w[n], new_m[n], new_v[n] = _adamw(weights[n], grad_w[n], given["m_" + n], given["v_" + n])
    return (loss, grad_x, *[grad_w[n] for n in TWIN_WEIGHTS], *[delta_w[n] for n in TWIN_WEIGHTS],
            *[new_m[n] for n in TWIN_WEIGHTS], *[new_v[n] for n in TWIN_WEIGHTS])
```

```python
import functools
import math

import jax
import jax.numpy as jnp
from jax import lax
from jax.experimental import pallas as pl
from jax.experimental.pallas import tpu as pltpu

F32 = jnp.float32
MXU_DTYPE = jnp.bfloat16
MESH = pl.DeviceIdType.MESH

D_MODEL = 1024
BRANCH = 1024
NORM_EPS = 1e-5
SSM_GROUPS = 64
SSM_GROUP = 16
SSM_STATE = 64
S5_CHUNK = 16
LANES = 128
S5_OCT = LANES // SSM_GROUP
S5_OCTETS = SSM_GROUPS // S5_OCT
S5_OCT_IN = S5_CHUNK * LANES
S5_OCT_STATE = S5_OCT * SSM_STATE
S5_STATES = SSM_GROUPS * SSM_STATE
HEAD_DIM = 64
N_Q_HEADS = 16
N_KV_HEADS = 2
GQA_GROUP = N_Q_HEADS // N_KV_HEADS
ATTN_BLOCK = 128
Q_DIM = N_Q_HEADS * HEAD_DIM
KV_DIM = N_KV_HEADS * HEAD_DIM
ROPE_THETA = 10000.0
NEG_INF = -1e30
ADAM_LR = 0.001
ADAM_B1 = 0.9
ADAM_B2 = 0.999
ADAM_EPS = 1e-08
ADAM_WD = 0.01
ADAM_STEP = 10

VMEM_LIMIT_V7X = 56 * 1024 * 1024
ROW_TILE_FWD = 512
ROW_TILE_BWD = 256

SSM_NAMES = ("norm", "w_in", "a_re", "a_im", "log_step", "b_re", "b_im", "c_re", "c_im", "d", "w_glu", "b_glu", "w_out")
ATTN_NAMES = ("norm", "w_in", "sinks", "w_out")


def _weight_names():
    names = []
    for i in range(4):
        for n in (SSM_NAMES if i % 2 == 0 else ATTN_NAMES):
            names.append("l%d_%s" % (i, n))
    names.append("final_norm")
    return names


WEIGHT_NAMES = _weight_names()
BIG_NAMES = [n for n in WEIGHT_NAMES if n.endswith(("w_in", "w_glu", "w_out"))]
SMALL_NAMES = [n for n in WEIGHT_NAMES if n not in BIG_NAMES]


def _params(semantics=None):
    return pltpu.CompilerParams(dimension_semantics=semantics, vmem_limit_bytes=VMEM_LIMIT_V7X)


def _rows(tm, n):
    return pl.BlockSpec((tm, n), lambda i: (i, 0))


def _whole(shape):
    return pl.BlockSpec(shape, lambda i: (0,) * len(shape))


def _sds(shape, dtype=F32):
    return jax.ShapeDtypeStruct(shape, dtype)


def _mm(a, b):
    return jnp.dot(a.astype(MXU_DTYPE), b.astype(MXU_DTYPE), preferred_element_type=F32)


def _mm_tn(a, b):
    return lax.dot_general(a.astype(MXU_DTYPE), b.astype(MXU_DTYPE), (((0,), (0,)), ((), ())), preferred_element_type=F32)


def _mm_nt(a, b):
    return lax.dot_general(a.astype(MXU_DTYPE), b.astype(MXU_DTYPE), (((1,), (1,)), ((), ())), preferred_element_type=F32)


def _sigmoid(x):
    return 1.0 / (1.0 + jnp.exp(-x))


def _silu(x):
    return x * _sigmoid(x)


def _silu_grad(x):
    s = _sigmoid(x)
    return s * (1.0 + x * (1.0 - s))


GELU_C0 = math.sqrt(2.0 / math.pi)
GELU_C1 = 0.044715


def _gelu(x):
    return 0.5 * x * (1.0 + jnp.tanh(GELU_C0 * (x + GELU_C1 * x * x * x)))


def _gelu_grad(x):
    th = jnp.tanh(GELU_C0 * (x + GELU_C1 * x * x * x))
    return 0.5 * (1.0 + th) + 0.5 * x * (1.0 - th * th) * GELU_C0 * (1.0 + 3.0 * GELU_C1 * x * x)


def _rms(x, g):
    r = lax.rsqrt(jnp.mean(x * x, axis=-1, keepdims=True) + NORM_EPS)
    xhat = x * r
    return r, xhat, xhat * g


def _rms_bwd(dh, g, r, xhat):
    dxhat = dh * g
    dx = r * (dxhat - xhat * jnp.mean(dxhat * xhat, axis=-1, keepdims=True))
    return dx, jnp.sum(dh * xhat, axis=0, keepdims=True)


def _swap_half_heads(x):
    n = x.shape[-1]
    lane = lax.broadcasted_iota(jnp.int32, x.shape, x.ndim - 1)
    first = (lane % HEAD_DIM) < (HEAD_DIM // 2)
    return jnp.where(first, pltpu.roll(x, n - HEAD_DIM // 2, x.ndim - 1), pltpu.roll(x, HEAD_DIM // 2, x.ndim - 1))


def _tile_lanes(t, reps):
    return jnp.concatenate([t] * reps, axis=1)


def ssm_proj_fwd(x, norm, w_in):
    t = x.shape[0]
    tm = min(ROW_TILE_FWD, t)

    def body(x_ref, g_ref, w_ref, u_ref, gate_ref):
        _, _, h = _rms(x_ref[...], g_ref[...])
        p = _mm(h, w_ref[...])
        u_ref[...] = p[:, :BRANCH]
        gate_ref[...] = p[:, BRANCH:]

    return pl.pallas_call(
        body, name="ssm_proj_fwd", grid=(t // tm,),
        in_specs=[_rows(tm, D_MODEL), _whole((1, D_MODEL)), _whole((D_MODEL, 2 * BRANCH))],
        out_specs=[_rows(tm, BRANCH), _rows(tm, BRANCH)],
        out_shape=[_sds((t, BRANCH)), _sds((t, BRANCH))],
        compiler_params=_params(("parallel",)),
    )(x, norm, w_in)


def _chunk_rows(ref, nk, dtype=None):
    rows = jnp.concatenate([ref[pl.ds(s, nk, stride=S5_CHUNK), :] for s in range(S5_CHUNK)], axis=1)
    return rows.astype(MXU_DTYPE if dtype is None else dtype)


def _store_chunk_rows(ref, val, nk):
    for s in range(S5_CHUNK):
        ref[pl.ds(s, nk, stride=S5_CHUNK), :] = val[:, s * LANES:(s + 1) * LANES]


def _own_group_mask():
    row = lax.broadcasted_iota(jnp.int32, (S5_OCT_IN, S5_OCT_STATE), 0)
    col = lax.broadcasted_iota(jnp.int32, (S5_OCT_IN, S5_OCT_STATE), 1)
    return ((row % LANES) // SSM_GROUP) == (col // SSM_STATE)


def _spread_groups(w):
    return jnp.where(_own_group_mask(), jnp.concatenate([w] * (S5_OCT_STATE // LANES), axis=1), 0.0).astype(MXU_DTYPE)


def _fold_groups(p):
    p = jnp.where(_own_group_mask(), p, 0.0)
    return sum(p[:, q * LANES:(q + 1) * LANES] for q in range(S5_OCT_STATE // LANES))


def _fill_toeplitz(win_ref, kd_ref):
    win_ref[...] = jnp.zeros_like(win_ref)
    for s in range(S5_CHUNK):
        for t in range(s, S5_CHUNK):
            win_ref[s * LANES:(s + 1) * LANES, t * LANES:(t + 1) * LANES] = kd_ref[t - s].astype(MXU_DTYPE)


def _strip(t):
    return pl.BlockSpec((t, LANES), lambda b: (0, b))


def _oct_states(nk):
    return pl.BlockSpec((nk, S5_OCT_STATE), lambda b: (0, b))


OCT_W = pl.BlockSpec((None, S5_OCT_IN, LANES), lambda b: (b, 0, 0))
OCT_KD = pl.BlockSpec((None, S5_CHUNK, LANES, LANES), lambda b: (b, 0, 0, 0))


def s5_chunk_states(u, ws_re, ws_im):
    t = u.shape[0]
    nk = t // S5_CHUNK

    def body(u_ref, wr_ref, wi_ref, re_ref, im_ref):
        uc = _chunk_rows(u_ref, nk)
        re_ref[...] = _mm(uc, _spread_groups(wr_ref[...]))
        im_ref[...] = _mm(uc, _spread_groups(wi_ref[...]))

    return pl.pallas_call(
        body, name="s5_chunk_states", grid=(S5_OCTETS,),
        in_specs=[_strip(t), OCT_W, OCT_W], out_specs=[_oct_states(nk), _oct_states(nk)],
        out_shape=[_sds((nk, S5_STATES)), _sds((nk, S5_STATES))],
        compiler_params=_params(("parallel",)),
    )(u, ws_re, ws_im)


def s5_scan_fwd(s_re, s_im, a_re, a_im):
    nk = s_re.shape[0]

    def body(sre_ref, sim_ref, ar_ref, ai_ref, hre_ref, him_ref):
        ar = ar_ref[...]
        ai = ai_ref[...]

        def step(k, carry):
            hr, hi = carry
            hre_ref[pl.ds(k, 1), :] = hr
            him_ref[pl.ds(k, 1), :] = hi
            sr = sre_ref[pl.ds(k, 1), :]
            si = sim_ref[pl.ds(k, 1), :]
            return ar * hr - ai * hi + sr, ai * hr + ar * hi + si

        zero = jnp.zeros((1, S5_STATES), F32)
        lax.fori_loop(0, nk, step, (zero, zero))

    vm = pl.BlockSpec(memory_space=pltpu.VMEM)
    return pl.pallas_call(
        body, name="s5_scan_fwd", in_specs=[vm, vm, vm, vm], out_specs=[vm, vm],
        out_shape=[_sds((nk, S5_STATES)), _sds((nk, S5_STATES))],
        compiler_params=_params(),
    )(s_re, s_im, a_re, a_im)


def s5_outputs(u, h_re, h_im, kd, wo_re, wo_im):
    t = u.shape[0]
    nk = t // S5_CHUNK

    def body(u_ref, hre_ref, him_ref, kd_ref, wor_ref, woi_ref, y_ref, win_ref):
        _fill_toeplitz(win_ref, kd_ref)
        y = _mm(_chunk_rows(u_ref, nk), win_ref[...])
        y = y + _mm_nt(hre_ref[...], _spread_groups(wor_ref[...])) + _mm_nt(him_ref[...], _spread_groups(woi_ref[...]))
        _store_chunk_rows(y_ref, y, nk)

    return pl.pallas_call(
        body, name="s5_outputs", grid=(S5_OCTETS,),
        in_specs=[_strip(t), _oct_states(nk), _oct_states(nk), OCT_KD, OCT_W, OCT_W],
        out_specs=_strip(t), out_shape=_sds((t, BRANCH)),
        scratch_shapes=[pltpu.VMEM((S5_OCT_IN, S5_OCT_IN), MXU_DTYPE)],
        compiler_params=_params(("parallel",)),
    )(u, h_re, h_im, kd, wo_re, wo_im)


def s5_state_grads(dy, wo_re, wo_im):
    t = dy.shape[0]
    nk = t // S5_CHUNK

    def body(dy_ref, wor_ref, woi_ref, re_ref, im_ref):
        dyc = _chunk_rows(dy_ref, nk)
        re_ref[...] = _mm(dyc, _spread_groups(wor_ref[...]))
        im_ref[...] = _mm(dyc, _spread_groups(woi_ref[...]))

    return pl.pallas_call(
        body, name="s5_state_grads", grid=(S5_OCTETS,),
        in_specs=[_strip(t), OCT_W, OCT_W], out_specs=[_oct_states(nk), _oct_states(nk)],
        out_shape=[_sds((nk, S5_STATES)), _sds((nk, S5_STATES))],
        compiler_params=_params(("parallel",)),
    )(dy, wo_re, wo_im)


def s5_scan_bwd(dh_re, dh_im, h_re, h_im, a_re, a_im):
    nk = dh_re.shape[0]

    def body(dhr_ref, dhi_ref, hr_ref, hi_ref, ar_ref, ai_ref, dsr_ref, dsi_ref, dar_ref, dai_ref):
        ar = ar_ref[...]
        ai = ai_ref[...]

        dar_ref[...] = jnp.zeros_like(dar_ref)
        dai_ref[...] = jnp.zeros_like(dai_ref)

        def step(i, carry):
            gr, gi = carry
            k = nk - 1 - i
            dhr = dhr_ref[pl.ds(k, 1), :]
            dhi = dhi_ref[pl.ds(k, 1), :]
            dsr_ref[pl.ds(k, 1), :] = gr
            dsi_ref[pl.ds(k, 1), :] = gi
            hr = hr_ref[pl.ds(k, 1), :]
            hi = hi_ref[pl.ds(k, 1), :]
            dar_ref[...] += gr * hr + gi * hi
            dai_ref[...] += gi * hr - gr * hi
            return dhr + ar * gr + ai * gi, dhi - ai * gr + ar * gi

        zero = jnp.zeros((1, S5_STATES), F32)
        lax.fori_loop(0, nk, step, (zero, zero))

    vm = pl.BlockSpec(memory_space=pltpu.VMEM)
    return pl.pallas_call(
        body, name="s5_scan_bwd", in_specs=[vm] * 6, out_specs=[vm] * 4,
        out_shape=[_sds((nk, S5_STATES)), _sds((nk, S5_STATES)), _sds((1, S5_STATES)), _sds((1, S5_STATES))],
        input_output_aliases={0: 0, 1: 1}, compiler_params=_params(),
    )(dh_re, dh_im, h_re, h_im, a_re, a_im)


def s5_input_grads(dy, ds_re, ds_im, kd, ws_re, ws_im):
    t = dy.shape[0]
    nk = t // S5_CHUNK

    def body(dy_ref, dsr_ref, dsi_ref, kd_ref, wsr_ref, wsi_ref, du_ref, win_ref):
        _fill_toeplitz(win_ref, kd_ref)
        du = _mm_nt(_chunk_rows(dy_ref, nk), win_ref[...])
        du = du + _mm_nt(dsr_ref[...], _spread_groups(wsr_ref[...])) + _mm_nt(dsi_ref[...], _spread_groups(wsi_ref[...]))
        _store_chunk_rows(du_ref, du, nk)

    return pl.pallas_call(
        body, name="s5_input_grads", grid=(S5_OCTETS,),
        in_specs=[_strip(t), _oct_states(nk), _oct_states(nk), OCT_KD, OCT_W, OCT_W],
        out_specs=_strip(t), out_shape=_sds((t, BRANCH)),
        scratch_shapes=[pltpu.VMEM((S5_OCT_IN, S5_OCT_IN), MXU_DTYPE)],
        compiler_params=_params(("parallel",)),
    )(dy, ds_re, ds_im, kd, ws_re, ws_im)


def s5_weight_grads(u, dy, h_re, h_im, ds_re, ds_im):
    t = u.shape[0]
    nk = t // S5_CHUNK

    def body(u_ref, dy_ref, hre_ref, him_ref, dsr_ref, dsi_ref, dkd_ref, dwsr_ref, dwsi_ref, dwor_ref, dwoi_ref):
        dyc = _chunk_rows(dy_ref, nk, F32)
        uct = _chunk_rows(u_ref, nk, F32).T.astype(MXU_DTYPE)
        dyct = dyc.T.astype(MXU_DTYPE)
        dyc = dyc.astype(MXU_DTYPE)
        dwsr_ref[...] = _fold_groups(_mm(uct, dsr_ref[...]))
        dwsi_ref[...] = _fold_groups(_mm(uct, dsi_ref[...]))
        dwor_ref[...] = _fold_groups(_mm(dyct, hre_ref[...]))
        dwoi_ref[...] = _fold_groups(_mm(dyct, him_ref[...]))
        dkd_ref[...] = jnp.zeros_like(dkd_ref)
        for tt in range(S5_CHUNK):
            p = _mm(uct, dyc[:, tt * LANES:(tt + 1) * LANES])
            for s in range(tt + 1):
                dkd_ref[tt - s] += p[s * LANES:(s + 1) * LANES]

    return pl.pallas_call(
        body, name="s5_weight_grads", grid=(S5_OCTETS,),
        in_specs=[_strip(t), _strip(t)] + [_oct_states(nk)] * 4,
        out_specs=[OCT_KD, OCT_W, OCT_W, OCT_W, OCT_W],
        out_shape=[_sds((S5_OCTETS, S5_CHUNK, LANES, LANES))] + [_sds((S5_OCTETS, S5_OCT_IN, LANES))] * 4,
        compiler_params=_params(("parallel",)),
    )(u, dy, h_re, h_im, ds_re, ds_im)


def ssm_mix_fwd(x, u, gate, y_scan, d, w_glu, b_glu, w_out):
    t = x.shape[0]
    tm = min(ROW_TILE_FWD, t)

    def body(x_ref, u_ref, gate_ref, ys_ref, d_ref, wg_ref, bg_ref, wo_ref, y_ref, g2_ref, xo_ref):
        y = ys_ref[...] + d_ref[...] * u_ref[...]
        z0 = _gelu(y)
        g2 = _mm(z0, wg_ref[...]) + bg_ref[...]
        a = z0 * _sigmoid(g2) * _silu(gate_ref[...])
        y_ref[...] = y
        g2_ref[...] = g2
        xo_ref[...] = x_ref[...] + _mm(a, wo_ref[...])

    row = _rows(tm, BRANCH)
    vec = _whole((1, BRANCH))
    mat = _whole((BRANCH, BRANCH))
    return pl.pallas_call(
        body, name="ssm_mix_fwd", grid=(t // tm,),
        in_specs=[row, row, row, row, vec, mat, vec, mat],
        out_specs=[row, row, row],
        out_shape=[_sds((t, BRANCH))] * 3,
        compiler_params=_params(("parallel",)),
    )(x, u, gate, y_scan, d, w_glu, b_glu, w_out)


def ssm_mix_bwd(dxo, u, gate, y, g2, w_glu_t, w_out_t):
    t = dxo.shape[0]
    tm = min(ROW_TILE_BWD, t)

    def body(dxo_ref, u_ref, gate_ref, y_ref, g2_ref, wgt_ref, wot_ref, dy_ref, dgate_ref, dwo_ref, dwg_ref, dbg_ref, dd_ref):
        @pl.when(pl.program_id(0) == 0)
        def _():
            dwo_ref[...] = jnp.zeros_like(dwo_ref)
            dwg_ref[...] = jnp.zeros_like(dwg_ref)
            dbg_ref[...] = jnp.zeros_like(dbg_ref)
            dd_ref[...] = jnp.zeros_like(dd_ref)

        dxo = dxo_ref[...]
        gate = gate_ref[...]
        y = y_ref[...]
        z0 = _gelu(y)
        sg = _sigmoid(g2_ref[...])
        z = z0 * sg
        sgate = _silu(gate)
        da = _mm(dxo, wot_ref[...])
        dwo_ref[...] += _mm_tn(z * sgate, dxo)
        dz = da * sgate
        dgate_ref[...] = da * z * _silu_grad(gate)
        dg2 = dz * z0 * sg * (1.0 - sg)
        dbg_ref[...] += jnp.sum(dg2, axis=0, keepdims=True)
        dwg_ref[...] += _mm_tn(z0, dg2)
        dz0 = dz * sg + _mm(dg2, wgt_ref[...])
        dy = dz0 * _gelu_grad(y)
        dd_ref[...] += jnp.sum(dy * u_ref[...], axis=0, keepdims=True)
        dy_ref[...] = dy

    row = _rows(tm, BRANCH)
    vec = _whole((1, BRANCH))
    mat = _whole((BRANCH, BRANCH))
    return pl.pallas_call(
        body, name="ssm_mix_bwd", grid=(t // tm,),
        in_specs=[row, row, row, row, row, mat, mat],
        out_specs=[row, row, mat, mat, vec, vec],
        out_shape=[_sds((t, BRANCH)), _sds((t, BRANCH)), _sds((BRANCH, D_MODEL)), _sds((BRANCH, BRANCH)),
                   _sds((1, BRANCH)), _sds((1, BRANCH))],
        compiler_params=_params(("arbitrary",)),
    )(dxo, u, gate, y, g2, w_glu_t, w_out_t)


def ssm_proj_bwd(x, norm, dxo, dy, du_scan, dgate, d, w_in_t):
    t = x.shape[0]
    tm = min(ROW_TILE_BWD, t)
    n = 2 * BRANCH

    def body(x_ref, g_ref, dxo_ref, dy_ref, dus_ref, dgate_ref, d_ref, wt_ref, dx_ref, dw_ref, dg_ref):
        @pl.when(pl.program_id(0) == 0)
        def _():
            dw_ref[...] = jnp.zeros_like(dw_ref)
            dg_ref[...] = jnp.zeros_like(dg_ref)

        g = g_ref[...]
        r, xhat, h = _rms(x_ref[...], g)
        du = dus_ref[...] + d_ref[...] * dy_ref[...]
        dproj = jnp.concatenate([du, dgate_ref[...]], axis=1)
        dh = _mm(dproj, wt_ref[...])
        dw_ref[...] += _mm_tn(h, dproj)
        dx, dg = _rms_bwd(dh, g, r, xhat)
        dg_ref[...] += dg
        dx_ref[...] = dxo_ref[...] + dx

    row = _rows(tm, D_MODEL)
    vec = _whole((1, D_MODEL))
    return pl.pallas_call(
        body, name="ssm_proj_bwd", grid=(t // tm,),
        in_specs=[row, vec, row, row, row, row, vec, _whole((n, D_MODEL))],
        out_specs=[row, _whole((D_MODEL, n)), vec],
        out_shape=[_sds((t, D_MODEL)), _sds((D_MODEL, n)), _sds((1, D_MODEL))],
        compiler_params=_params(("arbitrary",)),
    )(x, norm, dxo, dy, du_scan, dgate, d, w_in_t)


ATTN_N = Q_DIM + 2 * KV_DIM + BRANCH


def attn_proj_fwd(x, norm, w_in, cos2, sin2):
    t = x.shape[0]
    tm = min(ROW_TILE_FWD, t)

    def body(x_ref, g_ref, w_ref, cos_ref, sin_ref, q_ref, k_ref, v_ref, gate_ref):
        _, _, h = _rms(x_ref[...], g_ref[...])
        p = _mm(h, w_ref[...])
        cs = cos_ref[...]
        sn = sin_ref[...]
        q = p[:, :Q_DIM]
        k = p[:, Q_DIM:Q_DIM + KV_DIM]
        q_ref[...] = q * _tile_lanes(cs, Q_DIM // LANES) + _swap_half_heads(q) * _tile_lanes(sn, Q_DIM // LANES)
        k_ref[...] = k * cs + _swap_half_heads(k) * sn
        v_ref[...] = p[:, Q_DIM + KV_DIM:Q_DIM + 2 * KV_DIM]
        gate_ref[...] = p[:, Q_DIM + 2 * KV_DIM:]

    return pl.pallas_call(
        body, name="attn_proj_fwd", grid=(t // tm,),
        in_specs=[_rows(tm, D_MODEL), _whole((1, D_MODEL)), _whole((D_MODEL, ATTN_N)), _rows(tm, LANES), _rows(tm, LANES)],
        out_specs=[_rows(tm, Q_DIM), _rows(tm, KV_DIM), _rows(tm, KV_DIM), _rows(tm, BRANCH)],
        out_shape=[_sds((t, Q_DIM)), _sds((t, KV_DIM)), _sds((t, KV_DIM)), _sds((t, BRANCH))],
        compiler_params=_params(("parallel",)),
    )(x, norm, w_in, cos2, sin2)


def _band_mask(first_block):
    qi = lax.broadcasted_iota(jnp.int32, (ATTN_BLOCK, 2 * ATTN_BLOCK), 0)
    kj = lax.broadcasted_iota(jnp.int32, (ATTN_BLOCK, 2 * ATTN_BLOCK), 1)
    dist = qi + ATTN_BLOCK - kj
    first_key = jnp.where(first_block, ATTN_BLOCK, 0)
    return (dist >= 0) & (dist < ATTN_BLOCK) & (kj >= first_key)


def _lane_is(h):
    return lax.broadcasted_iota(jnp.int32, (1, LANES), 1) == h


def attn_fwd(q, k, v, sinks):
    t = q.shape[0]
    nb = t // ATTN_BLOCK
    scale = HEAD_DIM ** -0.5

    def body(sink_ref, q_ref, kc_ref, kp_ref, vc_ref, vp_ref, o_ref, lse_ref):
        i = pl.program_id(0)
        keys = jnp.concatenate([kp_ref[...], kc_ref[...]], axis=0).astype(MXU_DTYPE)
        vals = jnp.concatenate([vp_ref[...], vc_ref[...]], axis=0).astype(MXU_DTYPE)
        valid = _band_mask(i == 0)
        lse = jnp.zeros((ATTN_BLOCK, LANES), F32)
        for h in range(N_Q_HEADS):
            hk = h // GQA_GROUP
            kh = keys[:, hk * HEAD_DIM:(hk + 1) * HEAD_DIM]
            vh = vals[:, hk * HEAD_DIM:(hk + 1) * HEAD_DIM]
            qh = q_ref[:, h * HEAD_DIM:(h + 1) * HEAD_DIM]
            s = jnp.where(valid, _mm_nt(qh, kh) * scale, NEG_INF)
            sink = sink_ref[h]
            m = jnp.maximum(jnp.max(s, axis=-1, keepdims=True), sink)
            p = jnp.exp(s - m)
            den = jnp.sum(p, axis=-1, keepdims=True) + jnp.exp(sink - m)
            o_ref[:, h * HEAD_DIM:(h + 1) * HEAD_DIM] = _mm(p, vh) / den
            lse = jnp.where(_lane_is(h), m + jnp.log(den), lse)
        lse_ref[...] = lse

    cur = lambda n: pl.BlockSpec((ATTN_BLOCK, n), lambda i: (i, 0))
    prev = lambda n: pl.BlockSpec((ATTN_BLOCK, n), lambda i: (jnp.maximum(i - 1, 0), 0))
    return pl.pallas_call(
        body, name="attn_fwd", grid=(nb,),
        in_specs=[pl.BlockSpec(memory_space=pltpu.SMEM), cur(Q_DIM), cur(KV_DIM), prev(KV_DIM), cur(KV_DIM), prev(KV_DIM)],
        out_specs=[cur(Q_DIM), cur(LANES)],
        out_shape=[_sds((t, Q_DIM)), _sds((t, LANES))],
        compiler_params=_params(("parallel",)),
    )(sinks, q, k, k, v, v)


def attn_bwd(q, k, v, sinks, o, lse, do):
    t = q.shape[0]
    nb = t // ATTN_BLOCK
    scale = HEAD_DIM ** -0.5

    def body(sink_ref, q_ref, kc_ref, kp_ref, vc_ref, vp_ref, o_ref, lse_ref, do_ref,
             dq_ref, dk_ref, dv_ref, dsink_ref, dk_carry, dv_carry):
        i = pl.program_id(0)

        @pl.when(i == 0)
        def _():
            dsink_ref[...] = jnp.zeros_like(dsink_ref)
            dk_carry[...] = jnp.zeros_like(dk_carry)
            dv_carry[...] = jnp.zeros_like(dv_carry)

        @pl.when(i < nb)
        def _():
            keys = jnp.concatenate([kp_ref[...], kc_ref[...]], axis=0).astype(MXU_DTYPE)
            vals = jnp.concatenate([vp_ref[...], vc_ref[...]], axis=0).astype(MXU_DTYPE)
            valid = _band_mask(i == 0)
            lse_all = lse_ref[...]
            dsink = jnp.zeros((1, LANES), F32)
            dk_heads = []
            dv_heads = []
            for hk in range(N_KV_HEADS):
                kh = keys[:, hk * HEAD_DIM:(hk + 1) * HEAD_DIM]
                vh = vals[:, hk * HEAD_DIM:(hk + 1) * HEAD_DIM]
                dkk = jnp.zeros((2 * ATTN_BLOCK, HEAD_DIM), F32)
                dvv = jnp.zeros((2 * ATTN_BLOCK, HEAD_DIM), F32)
                for hq in range(GQA_GROUP):
                    h = hk * GQA_GROUP + hq
                    sl = slice(h * HEAD_DIM, (h + 1) * HEAD_DIM)
                    qh = q_ref[:, sl]
                    doh = do_ref[:, sl]
                    lse_h = jnp.sum(jnp.where(_lane_is(h), lse_all, 0.0), axis=-1, keepdims=True)
                    s = jnp.where(valid, _mm_nt(qh, kh) * scale, NEG_INF)
                    p = jnp.exp(s - lse_h)
                    delta = jnp.sum(doh * o_ref[:, sl], axis=-1, keepdims=True)
                    dvv = dvv + _mm_tn(p, doh)
                    ds = p * (_mm_nt(doh, vh) - delta)
                    dq_ref[:, sl] = _mm(ds, kh) * scale
                    dkk = dkk + _mm_tn(ds, qh) * scale
                    dsink = dsink + jnp.where(_lane_is(h), -jnp.sum(jnp.exp(sink_ref[h] - lse_h) * delta), 0.0)
                dk_heads.append(dkk)
                dv_heads.append(dvv)
            dkk = jnp.concatenate(dk_heads, axis=1)
            dvv = jnp.concatenate(dv_heads, axis=1)
            dsink_ref[...] += dsink
            dk_ref[...] = dk_carry[...] + dkk[:ATTN_BLOCK]
            dv_ref[...] = dv_carry[...] + dvv[:ATTN_BLOCK]
            dk_carry[...] = dkk[ATTN_BLOCK:]
            dv_carry[...] = dvv[ATTN_BLOCK:]

        @pl.when(i == nb)
        def _():
            dk_ref[...] = dk_carry[...]
            dv_ref[...] = dv_carry[...]

    last = nb - 1
    cur = lambda n: pl.BlockSpec((ATTN_BLOCK, n), lambda i: (jnp.minimum(i, last), 0))
    prev = lambda n: pl.BlockSpec((ATTN_BLOCK, n), lambda i: (jnp.clip(i - 1, 0, last), 0))
    late = lambda n: pl.BlockSpec((ATTN_BLOCK, n), lambda i: (i, 0))
    dq, dk_late, dv_late, dsinks = pl.pallas_call(
        body, name="attn_bwd", grid=(nb + 1,),
        in_specs=[pl.BlockSpec(memory_space=pltpu.SMEM), cur(Q_DIM), cur(KV_DIM), prev(KV_DIM), cur(KV_DIM), prev(KV_DIM),
                  cur(Q_DIM), cur(LANES), cur(Q_DIM)],
        out_specs=[cur(Q_DIM), late(KV_DIM), late(KV_DIM), _whole((1, LANES))],
        out_shape=[_sds((t, Q_DIM)), _sds((t + ATTN_BLOCK, KV_DIM)), _sds((t + ATTN_BLOCK, KV_DIM)), _sds((1, LANES))],
        scratch_shapes=[pltpu.VMEM((ATTN_BLOCK, KV_DIM), F32), pltpu.VMEM((ATTN_BLOCK, KV_DIM), F32)],
        compiler_params=_params(("arbitrary",)),
    )(sinks, q, k, k, v, v, o, lse, do)
    return dq, dk_late[ATTN_BLOCK:], dv_late[ATTN_BLOCK:], dsinks


def attn_out_fwd(x, o, gate, w_out):
    t = x.shape[0]
    tm = min(ROW_TILE_FWD, t)

    def body(x_ref, o_ref, gate_ref, w_ref, xo_ref):
        xo_ref[...] = x_ref[...] + _mm(o_ref[...] * _silu(gate_ref[...]), w_ref[...])

    row = _rows(tm, D_MODEL)
    return pl.pallas_call(
        body, name="attn_out_fwd", grid=(t // tm,),
        in_specs=[row, row, row, _whole((Q_DIM, D_MODEL))], out_specs=row, out_shape=_sds((t, D_MODEL)),
        compiler_params=_params(("parallel",)),
    )(x, o, gate, w_out)


def attn_out_bwd(dxo, o, gate, w_out_t):
    t = dxo.shape[0]
    tm = min(ROW_TILE_BWD, t)

    def body(dxo_ref, o_ref, gate_ref, wt_ref, do_ref, dgate_ref, dw_ref):
        @pl.when(pl.program_id(0) == 0)
        def _():
            dw_ref[...] = jnp.zeros_like(dw_ref)

        dxo = dxo_ref[...]
        o = o_ref[...]
        gate = gate_ref[...]
        sgate = _silu(gate)
        da = _mm(dxo, wt_ref[...])
        dw_ref[...] += _mm_tn(o * sgate, dxo)
        do_ref[...] = da * sgate
        dgate_ref[...] = da * o * _silu_grad(gate)

    row = _rows(tm, D_MODEL)
    mat = _whole((Q_DIM, D_MODEL))
    return pl.pallas_call(
        body, name="attn_out_bwd", grid=(t // tm,),
        in_specs=[row, row, row, mat], out_specs=[row, row, mat],
        out_shape=[_sds((t, Q_DIM)), _sds((t, BRANCH)), _sds((Q_DIM, D_MODEL))],
        compiler_params=_params(("arbitrary",)),
    )(dxo, o, gate, w_out_t)


def attn_proj_bwd(x, norm, dxo, dq, dk, dv, dgate, cos2, sin2, w_in_t):
    t = x.shape[0]
    tm = min(ROW_TILE_BWD, t)

    def body(x_ref, g_ref, dxo_ref, dq_ref, dk_ref, dv_ref, dgate_ref, cos_ref, sin_ref, wt_ref, dx_ref, dw_ref, dg_ref):
        @pl.when(pl.program_id(0) == 0)
        def _():
            dw_ref[...] = jnp.zeros_like(dw_ref)
            dg_ref[...] = jnp.zeros_like(dg_ref)

        g = g_ref[...]
        r, xhat, h = _rms(x_ref[...], g)
        cs = cos_ref[...]
        sn = sin_ref[...]
        dqr = dq_ref[...]
        dkr = dk_ref[...]
        dq = dqr * _tile_lanes(cs, Q_DIM // LANES) + _swap_half_heads(dqr * _tile_lanes(sn, Q_DIM // LANES))
        dk = dkr * cs + _swap_half_heads(dkr * sn)
        dproj = jnp.concatenate([dq, dk, dv_ref[...], dgate_ref[...]], axis=1)
        dh = _mm(dproj, wt_ref[...])
        dw_ref[...] += _mm_tn(h, dproj)
        dx, dg = _rms_bwd(dh, g, r, xhat)
        dg_ref[...] += dg
        dx_ref[...] = dxo_ref[...] + dx

    row = _rows(tm, D_MODEL)
    vec = _whole((1, D_MODEL))
    return pl.pallas_call(
        body, name="attn_proj_bwd", grid=(t // tm,),
        in_specs=[row, vec, row, _rows(tm, Q_DIM), _rows(tm, KV_DIM), _rows(tm, KV_DIM), _rows(tm, BRANCH),
                  _rows(tm, LANES), _rows(tm, LANES), _whole((ATTN_N, D_MODEL))],
        out_specs=[row, _whole((D_MODEL, ATTN_N)), vec],
        out_shape=[_sds((t, D_MODEL)), _sds((D_MODEL, ATTN_N)), _sds((1, D_MODEL))],
        compiler_params=_params(("arbitrary",)),
    )(x, norm, dxo, dq, dk, dv, dgate, cos2, sin2, w_in_t)


def loss_head(x, norm, target):
    t = x.shape[0]
    tm = min(ROW_TILE_FWD, t)

    def body(x_ref, g_ref, tgt_ref, loss_ref, dx_ref, dg_ref):
        @pl.when(pl.program_id(0) == 0)
        def _():
            loss_ref[...] = jnp.zeros_like(loss_ref)
            dg_ref[...] = jnp.zeros_like(dg_ref)

        g = g_ref[...]
        r, xhat, y = _rms(x_ref[...], g)
        err = y - tgt_ref[...]
        loss_ref[...] += 0.5 * jnp.sum(jnp.mean(err * err, axis=-1, keepdims=True), axis=0, keepdims=True)
        dx, dg = _rms_bwd(err * (1.0 / D_MODEL), g, r, xhat)
        dg_ref[...] += dg
        dx_ref[...] = dx

    row = _rows(tm, D_MODEL)
    vec = _whole((1, D_MODEL))
    return pl.pallas_call(
        body, name="loss_head", grid=(t // tm,),
        in_specs=[row, vec, row], out_specs=[_whole((1, 1)), row, vec],
        out_shape=[_sds((1, 1)), _sds((t, D_MODEL)), _sds((1, D_MODEL))],
        compiler_params=_params(("arbitrary",)),
    )(x, norm, target)


def _s5_matrices(a_re, a_im, log_step, b_re, b_im, c_re, c_im):
    hi = lax.Precision.HIGHEST
    g, p, c = SSM_GROUPS, SSM_STATE, SSM_GROUP
    step = jnp.exp(log_step)[:, None]
    xr = a_re * step
    xi = a_im * step
    lag = jnp.arange(S5_CHUNK + 1, dtype=F32)[None, :, None]
    mag = jnp.exp(lag * xr[:, None, :])
    ang = lag * xi[:, None, :]
    er = mag * jnp.cos(ang)
    ei = mag * jnp.sin(ang)
    nr = er[:, 1] - 1.0
    ni = ei[:, 1]
    den = a_re * a_re + a_im * a_im
    fr = (nr * a_re + ni * a_im) / den
    fi = (ni * a_re - nr * a_im) / den
    bbr = fr[..., None] * b_re - fi[..., None] * b_im
    bbi = fr[..., None] * b_im + fi[..., None] * b_re
    xr_ = er[:, :, :, None] * bbr[:, None] - ei[:, :, :, None] * bbi[:, None]
    xi_ = er[:, :, :, None] * bbi[:, None] + ei[:, :, :, None] * bbr[:, None]
    def rows_by_step(v):
        v = v.reshape(S5_OCTETS, S5_OCT, S5_CHUNK, c, p).transpose(0, 2, 1, 3, 4).reshape(S5_OCTETS, S5_OCT_IN, p)
        return jnp.concatenate([v, v], axis=-1)

    ws_re = rows_by_step(xr_[:, S5_CHUNK - 1::-1].transpose(0, 1, 3, 2))
    ws_im = rows_by_step(xi_[:, S5_CHUNK - 1::-1].transpose(0, 1, 3, 2))
    m = (jnp.einsum("glpd,gcp->gldc", xr_[:, :S5_CHUNK], c_re, precision=hi)
         - jnp.einsum("glpd,gcp->gldc", xi_[:, :S5_CHUNK], c_im, precision=hi))
    kd = m.reshape(S5_OCTETS, S5_OCT, S5_CHUNK, c, c).transpose(0, 2, 1, 3, 4)
    kd = kd[:, :, :, :, None, :] * jnp.eye(S5_OCT, dtype=F32)[None, None, :, None, :, None]
    kd = kd.reshape(S5_OCTETS, S5_CHUNK, LANES, LANES)
    c_re_t = c_re[:, None]
    c_im_t = c_im[:, None]
    e_r = er[:, 1:, None, :]
    e_i = ei[:, 1:, None, :]
    wo_re = rows_by_step(e_r * c_re_t - e_i * c_im_t)
    wo_im = rows_by_step(-(e_r * c_im_t + e_i * c_re_t))
    return dict(kd=kd, ws_re=ws_re, ws_im=ws_im, wo_re=wo_re, wo_im=wo_im,
                a_re=er[:, S5_CHUNK].reshape(1, S5_STATES), a_im=ei[:, S5_CHUNK].reshape(1, S5_STATES))


def _rope_tables(t):
    pos = jnp.arange(t, dtype=F32)
    inv_freq = ROPE_THETA ** (-jnp.arange(0, HEAD_DIM, 2, dtype=F32) / HEAD_DIM)
    ang = pos[:, None] * inv_freq[None, :]
    cos = jnp.cos(ang)
    sin = jnp.sin(ang)
    cos64 = jnp.concatenate([cos, cos], axis=1)
    sin64 = jnp.concatenate([-sin, sin], axis=1)
    return jnp.concatenate([cos64, cos64], axis=1), jnp.concatenate([sin64, sin64], axis=1)


def _row(v):
    return v.reshape(1, -1)


def _ssm_forward(x, w):
    mats, mats_vjp = jax.vjp(_s5_matrices, w["a_re"], w["a_im"], w["log_step"], w["b_re"], w["b_im"], w["c_re"], w["c_im"])
    u, gate = ssm_proj_fwd(x, _row(w["norm"]), w["w_in"])
    s_re, s_im = s5_chunk_states(u, mats["ws_re"], mats["ws_im"])
    h_re, h_im = s5_scan_fwd(s_re, s_im, mats["a_re"], mats["a_im"])
    y_scan = s5_outputs(u, h_re, h_im, mats["kd"], mats["wo_re"], mats["wo_im"])
    y, g2, x_new = ssm_mix_fwd(x, u, gate, y_scan, _row(w["d"]), w["w_glu"], _row(w["b_glu"]), w["w_out"])
    saved = dict(x=x, u=u, gate=gate, y=y, g2=g2, h_re=h_re, h_im=h_im, mats=mats, mats_vjp=mats_vjp)
    return x_new, saved


def _ssm_backward(dxo, w, s):
    dy, dgate, dw_out, dw_glu, db_glu, dd = ssm_mix_bwd(dxo, s["u"], s["gate"], s["y"], s["g2"], w["w_glu_t"], w["w_out_t"])
    mats = s["mats"]
    dh_re, dh_im = s5_state_grads(dy, mats["wo_re"], mats["wo_im"])
    ds_re, ds_im, da_re, da_im = s5_scan_bwd(dh_re, dh_im, s["h_re"], s["h_im"], mats["a_re"], mats["a_im"])
    du_scan = s5_input_grads(dy, ds_re, ds_im, mats["kd"], mats["ws_re"], mats["ws_im"])
    dkd, dws_re, dws_im, dwo_re, dwo_im = s5_weight_grads(s["u"], dy, s["h_re"], s["h_im"], ds_re, ds_im)
    dparams = s["mats_vjp"](dict(kd=dkd, ws_re=dws_re, ws_im=dws_im, wo_re=dwo_re, wo_im=dwo_im, a_re=da_re, a_im=da_im))
    dx, dw_in, dnorm = ssm_proj_bwd(s["x"], _row(w["norm"]), dxo, dy, du_scan, dgate, _row(w["d"]), w["w_in_t"])
    grads = dict(norm=dnorm, w_in=dw_in, d=dd, w_glu=dw_glu, b_glu=db_glu, w_out=dw_out)
    for name, val in zip(("a_re", "a_im", "log_step", "b_re", "b_im", "c_re", "c_im"), dparams):
        grads[name] = val
    return dx, grads


def _attn_forward(x, w, cos2, sin2):
    q, k, v, gate = attn_proj_fwd(x, _row(w["norm"]), w["w_in"], cos2, sin2)
    o, lse = attn_fwd(q, k, v, w["sinks"])
    x_new = attn_out_fwd(x, o, gate, w["w_out"])
    return x_new, dict(x=x, q=q, k=k, v=v, gate=gate, o=o, lse=lse)


def _attn_backward(dxo, w, s, cos2, sin2):
    do, dgate, dw_out = attn_out_bwd(dxo, s["o"], s["gate"], w["w_out_t"])
    dq, dk, dv, dsinks = attn_bwd(s["q"], s["k"], s["v"], w["sinks"], s["o"], s["lse"], do)
    dx, dw_in, dnorm = attn_proj_bwd(s["x"], _row(w["norm"]), dxo, dq, dk, dv, dgate, cos2, sin2, w["w_in_t"])
    return dx, dict(norm=dnorm, w_in=dw_in, sinks=dsinks[0, :N_Q_HEADS], w_out=dw_out)


def _sequence_step(x, target, layers, final_norm):
    cos2, sin2 = _rope_tables(x.shape[0])
    saved = []
    for i, w in enumerate(layers):
        if i % 2 == 0:
            x, s = _ssm_forward(x, w)
        else:
            x, s = _attn_forward(x, w, cos2, sin2)
        saved.append(s)
    loss, dx, dfinal = loss_head(x, _row(final_norm), target)
    grads = {"final_norm": dfinal}
    for i in reversed(range(len(layers))):
        if i % 2 == 0:
            dx, g = _ssm_backward(dx, layers[i], saved[i])
        else:
            dx, g = _attn_backward(dx, layers[i], saved[i], cos2, sin2)
        for name, val in g.items():
            grads["l%d_%s" % (i, name)] = val
    return loss[0, 0], dx, grads


ANY = pl.BlockSpec(memory_space=pl.ANY)


def _place():
    return lax.axis_index("x"), lax.axis_index("y"), lax.axis_index("c")


def _other_chips(x, y):
    return [(1 - x, y), (x, 1 - y), (1 - x, 1 - y)]


def gather_weight_shards(shards):
    n = len(shards)

    def body(*refs):
        ins, outs = refs[:n], refs[n:2 * n]
        send_sems, recv_sems, local_sems = refs[2 * n:]
        x, y, c = _place()
        me = 2 * x + y
        started = []
        for i in range(n):
            local = pltpu.make_async_copy(ins[i], outs[i].at[me], local_sems.at[i])
            local.start()
            started.append(local)
        sends = []
        for i in range(n):
            for k, (tx, ty) in enumerate(_other_chips(x, y)):
                cp = pltpu.make_async_remote_copy(src_ref=ins[i], dst_ref=outs[i].at[me], send_sem=send_sems.at[i, k],
                                                  recv_sem=recv_sems.at[i, k], device_id=(tx, ty, c), device_id_type=MESH)
                cp.start()
                sends.append(cp)
        for i in range(n):
            for k, (tx, ty) in enumerate(_other_chips(x, y)):
                pltpu.make_async_remote_copy(src_ref=ins[i], dst_ref=outs[i].at[2 * tx + ty], send_sem=send_sems.at[i, k],
                                             recv_sem=recv_sems.at[i, k], device_id=(tx, ty, c), device_id_type=MESH).wait_recv()
        for cp in sends:
            cp.wait_send()
        for cp in started:
            cp.wait()

    return pl.pallas_call(
        body, name="gather_weight_shards",
        in_specs=[ANY] * n, out_specs=[ANY] * n,
        out_shape=[_sds((4,) + s.shape, s.dtype) for s in shards],
        scratch_shapes=[pltpu.SemaphoreType.DMA((n, 3)), pltpu.SemaphoreType.DMA((n, 3)), pltpu.SemaphoreType.DMA((n,))],
    )(*shards)


def exchange_halves_with_sibling(grads):
    n = len(grads)

    def body(*refs):
        ins, outs = refs[:n], refs[n:2 * n]
        send_sems, recv_sems = refs[2 * n:]
        x, y, c = _place()
        copies = []
        for i in range(n):
            half = ins[i].shape[1] // 2
            src = ins[i].at[:, pl.ds((1 - c) * half, half), :]
            cp = pltpu.make_async_remote_copy(src_ref=src, dst_ref=outs[i], send_sem=send_sems.at[i], recv_sem=recv_sems.at[i],
                                              device_id=(x, y, 1 - c), device_id_type=MESH)
            cp.start()
            copies.append(cp)
        for cp in copies:
            cp.wait()

    return pl.pallas_call(
        body, name="exchange_halves_with_sibling",
        in_specs=[ANY] * n, out_specs=[ANY] * n,
        out_shape=[_sds((4, g.shape[1] // 2, g.shape[2])) for g in grads],
        scratch_shapes=[pltpu.SemaphoreType.DMA((n,)), pltpu.SemaphoreType.DMA((n,))],
    )(*grads)


def scatter_blocks_to_chips(sums):
    n = len(sums)

    def body(*refs):
        ins, outs = refs[:n], refs[n:2 * n]
        send_sems, recv_sems, local_sems = refs[2 * n:]
        x, y, c = _place()
        me = 2 * x + y
        local = []
        for i in range(n):
            cp = pltpu.make_async_copy(ins[i].at[me], outs[i].at[me], local_sems.at[i])
            cp.start()
            local.append(cp)
        sends = []
        for i in range(n):
            for k, (tx, ty) in enumerate(_other_chips(x, y)):
                cp = pltpu.make_async_remote_copy(src_ref=ins[i].at[2 * tx + ty], dst_ref=outs[i].at[me], send_sem=send_sems.at[i, k],
                                                  recv_sem=recv_sems.at[i, k], device_id=(tx, ty, c), device_id_type=MESH)
                cp.start()
                sends.append(cp)
        for i in range(n):
            for k, (tx, ty) in enumerate(_other_chips(x, y)):
                pltpu.make_async_remote_copy(src_ref=ins[i].at[me], dst_ref=outs[i].at[2 * tx + ty], send_sem=send_sems.at[i, k],
                                             recv_sem=recv_sems.at[i, k], device_id=(tx, ty, c), device_id_type=MESH).wait_recv()
        for cp in sends:
            cp.wait_send()
        for cp in local:
            cp.wait()

    return pl.pallas_call(
        body, name="scatter_blocks_to_chips",
        in_specs=[ANY] * n, out_specs=[ANY] * n,
        out_shape=[_sds(s.shape) for s in sums],
        scratch_shapes=[pltpu.SemaphoreType.DMA((n, 3)), pltpu.SemaphoreType.DMA((n, 3)), pltpu.SemaphoreType.DMA((n,))],
    )(*sums)


def share_reduced_pieces(pieces, small_piece):
    n = len(pieces)

    def body(*refs):
        ins, small_in = refs[:n], refs[n]
        outs, small_out = refs[n + 1:2 * n + 1], refs[2 * n + 1]
        send_sems, recv_sems, local_sems, small_send, small_recv = refs[2 * n + 2:]
        x, y, c = _place()
        started = []
        for i in range(n):
            cp = pltpu.make_async_copy(ins[i], outs[i].at[c], local_sems.at[i])
            cp.start()
            started.append(cp)
        cp = pltpu.make_async_copy(small_in, small_out.at[4 * x + 2 * y + c], local_sems.at[n])
        cp.start()
        started.append(cp)
        swaps = []
        for i in range(n):
            cp = pltpu.make_async_remote_copy(src_ref=ins[i], dst_ref=outs[i].at[c], send_sem=send_sems.at[i], recv_sem=recv_sems.at[i],
                                              device_id=(x, y, 1 - c), device_id_type=MESH)
            cp.start()
            swaps.append(cp)
        sends = []
        for k in range(1, 8):
            tx, ty, tc = x ^ ((k >> 2) & 1), y ^ ((k >> 1) & 1), c ^ (k & 1)
            cp = pltpu.make_async_remote_copy(src_ref=small_in, dst_ref=small_out.at[4 * x + 2 * y + c], send_sem=small_send.at[k - 1],
                                              recv_sem=small_recv.at[k - 1], device_id=(tx, ty, tc), device_id_type=MESH)
            cp.start()
            sends.append(cp)
        for i in range(n):
            pltpu.make_async_remote_copy(src_ref=ins[i], dst_ref=outs[i].at[1 - c], send_sem=send_sems.at[i], recv_sem=recv_sems.at[i],
                                         device_id=(x, y, 1 - c), device_id_type=MESH).wait_recv()
        for k in range(1, 8):
            tx, ty, tc = x ^ ((k >> 2) & 1), y ^ ((k >> 1) & 1), c ^ (k & 1)
            pltpu.make_async_remote_copy(src_ref=small_in, dst_ref=small_out.at[4 * tx + 2 * ty + tc], send_sem=small_send.at[k - 1],
                                         recv_sem=small_recv.at[k - 1], device_id=(tx, ty, tc), device_id_type=MESH).wait_recv()
        for cp in swaps + sends:
            cp.wait_send()
        for cp in started:
            cp.wait()

    return pl.pallas_call(
        body, name="share_reduced_pieces",
        in_specs=[ANY] * (n + 1), out_specs=[ANY] * (n + 1),
        out_shape=[_sds((2,) + p.shape) for p in pieces] + [_sds((8,) + small_piece.shape)],
        scratch_shapes=[pltpu.SemaphoreType.DMA((n,)), pltpu.SemaphoreType.DMA((n,)), pltpu.SemaphoreType.DMA((n + 1,)),
                        pltpu.SemaphoreType.DMA((7,)), pltpu.SemaphoreType.DMA((7,))],
    )(*pieces, small_piece)


def _row_tile(rows, cols):
    tm = rows
    while tm * cols * 4 > (2 << 20) and tm % 16 == 0:
        tm //= 2
    return tm


def add_pair(a, b):
    nb, rows, cols = a.shape
    tm = _row_tile(rows, cols)

    def body(a_ref, b_ref, o_ref):
        o_ref[...] = a_ref[...] + b_ref[...]

    spec = pl.BlockSpec((None, tm, cols), lambda j, i: (j, i, 0))
    return pl.pallas_call(
        body, name="add_pair", grid=(nb, rows // tm), in_specs=[spec, spec], out_specs=spec, out_shape=_sds(a.shape),
        compiler_params=_params(("parallel", "parallel")),
    )(a, b)


def sum_four(a):
    _, rows, cols = a.shape
    tm = _row_tile(rows, cols)

    def body(a_ref, o_ref):
        o_ref[...] = ((a_ref[0] + a_ref[1]) + a_ref[2]) + a_ref[3]

    return pl.pallas_call(
        body, name="sum_four", grid=(rows // tm,),
        in_specs=[pl.BlockSpec((4, tm, cols), lambda i: (0, i, 0))], out_specs=pl.BlockSpec((tm, cols), lambda i: (i, 0)),
        out_shape=_sds((rows, cols)), compiler_params=_params(("parallel",)),
    )(a)


def adamw(w, g, m, v):
    rows, cols = w.shape
    tm = _row_tile(rows, cols)
    c1 = 1.0 - ADAM_B1 ** ADAM_STEP
    c2 = 1.0 - ADAM_B2 ** ADAM_STEP

    def body(w_ref, g_ref, m_ref, v_ref, d_ref, nm_ref, nv_ref):
        g = g_ref[...]
        nm = ADAM_B1 * m_ref[...] + (1.0 - ADAM_B1) * g
        nv = ADAM_B2 * v_ref[...] + (1.0 - ADAM_B2) * (g * g)
        d_ref[...] = -ADAM_LR * ((nm / c1) / (jnp.sqrt(nv / c2) + ADAM_EPS) + ADAM_WD * w_ref[...])
        nm_ref[...] = nm
        nv_ref[...] = nv

    spec = pl.BlockSpec((tm, cols), lambda i: (i, 0))
    return pl.pallas_call(
        body, name="adamw", grid=(rows // tm,), in_specs=[spec] * 4, out_specs=[spec] * 3,
        out_shape=[_sds(w.shape)] * 3, compiler_params=_params(("parallel",)),
    )(w, g, m, v)


PACK_TILE = 8 * LANES
PACK_PIECES = 8
PACK_ALIGN = PACK_PIECES * 16


def _pack_small(values):
    parts = []
    for name in SMALL_NAMES:
        flat = values[name].reshape(-1)
        pad = (-flat.shape[0]) % PACK_TILE
        if pad:
            flat = jnp.concatenate([flat, jnp.zeros((pad,), F32)])
        parts.append(flat.reshape(-1, LANES))
    rows = sum(p.shape[0] for p in parts)
    pad = (-rows) % PACK_ALIGN
    if pad:
        parts.append(jnp.zeros((pad, LANES), F32))
    return jnp.concatenate(parts, axis=0)


def _unpack_small(pack, like):
    out = {}
    row = 0
    for name in SMALL_NAMES:
        size = math.prod(like[name].shape)
        rows = -(-size // PACK_TILE) * 8
        out[name] = pack[row:row + rows].reshape(-1)[:size].reshape(like[name].shape)
        row += rows
    return out


def _is_column_sharded(name):
    return name.endswith("w_in")


def _to_blocks(name, full):
    if _is_column_sharded(name):
        rows, cols = full.shape
        return full.reshape(rows, 4, cols // 4).transpose(1, 0, 2)
    return full.reshape(4, full.shape[0] // 4, full.shape[1])


def _from_blocks(name, stacked):
    if _is_column_sharded(name):
        return stacked.transpose(1, 0, 2).reshape(stacked.shape[1], 4 * stacked.shape[2])
    return stacked.reshape(4 * stacked.shape[1], stacked.shape[2])


def _train_step(x, loss_target, weights, moments_m, moments_v):
    c = lax.axis_index("c")
    gathered = gather_weight_shards([weights[n].astype(MXU_DTYPE) for n in BIG_NAMES])
    full = {n: _from_blocks(n, g) for n, g in zip(BIG_NAMES, gathered)}
    layers = []
    for i in range(4):
        names = SSM_NAMES if i % 2 == 0 else ATTN_NAMES
        w = {}
        for n in names:
            key = "l%d_%s" % (i, n)
            if key in full:
                w[n] = full[key]
                w[n + "_t"] = full[key].T
            else:
                w[n] = weights[key]
        layers.append(w)
    loss, dx, grads = _sequence_step(x[0], loss_target[0], layers, weights["final_norm"])
    loss = lax.psum(loss, ("x", "y", "c"))
    small_pack = _pack_small({n: grads[n] for n in SMALL_NAMES})
    blocks = [_to_blocks(n, grads[n]) for n in BIG_NAMES] + [small_pack.reshape(4, -1, LANES)]
    from_sibling = exchange_halves_with_sibling(blocks)
    chip_sums = []
    for b, r in zip(blocks, from_sibling):
        half = b.shape[1] // 2
        chip_sums.append(add_pair(lax.dynamic_slice_in_dim(b, c * half, half, axis=1), r))
    contributions = scatter_blocks_to_chips(chip_sums)
    reduced = [sum_four(a) for a in contributions]
    shared = share_reduced_pieces(reduced[:-1], reduced[-1])
    big_grads = {n: s.reshape(2 * s.shape[1], s.shape[2]) for n, s in zip(BIG_NAMES, shared[:-1])}
    small_grad_pack = shared[-1].reshape(-1, LANES)
    out_grad, out_delta, out_m, out_v = {}, {}, {}, {}
    for n in BIG_NAMES:
        out_grad[n] = big_grads[n]
        out_delta[n], out_m[n], out_v[n] = adamw(weights[n], big_grads[n], moments_m[n], moments_v[n])
    small_like = {n: weights[n] for n in SMALL_NAMES}
    d_pack, m_pack, v_pack = adamw(_pack_small(small_like), small_grad_pack, _pack_small({n: moments_m[n] for n in SMALL_NAMES}),
                                   _pack_small({n: moments_v[n] for n in SMALL_NAMES}))
    out_grad.update(_unpack_small(small_grad_pack, small_like))
    out_delta.update(_unpack_small(d_pack, small_like))
    out_m.update(_unpack_small(m_pack, small_like))
    out_v.update(_unpack_small(v_pack, small_like))
    outs = [loss, dx[None]]
    for group in (out_grad, out_delta, out_m, out_v):
        outs.extend(group[n] for n in WEIGHT_NAMES)
    return tuple(outs)


def kernel(x, l0_norm, l0_w_in, l0_a_re, l0_a_im, l0_log_step, l0_b_re, l0_b_im, l0_c_re, l0_c_im, l0_d, l0_w_glu, l0_b_glu, l0_w_out, l1_norm, l1_w_in, l1_sinks, l1_w_out, l2_norm, l2_w_in, l2_a_re, l2_a_im, l2_log_step, l2_b_re, l2_b_im, l2_c_re, l2_c_im, l2_d, l2_w_glu, l2_b_glu, l2_w_out, l3_norm, l3_w_in, l3_sinks, l3_w_out, final_norm, loss_target, m_l0_norm, m_l0_w_in, m_l0_a_re, m_l0_a_im, m_l0_log_step, m_l0_b_re, m_l0_b_im, m_l0_c_re, m_l0_c_im, m_l0_d, m_l0_w_glu, m_l0_b_glu, m_l0_w_out, m_l1_norm, m_l1_w_in, m_l1_sinks, m_l1_w_out, m_l2_norm, m_l2_w_in, m_l2_a_re, m_l2_a_im, m_l2_log_step, m_l2_b_re, m_l2_b_im, m_l2_c_re, m_l2_c_im, m_l2_d, m_l2_w_glu, m_l2_b_glu, m_l2_w_out, m_l3_norm, m_l3_w_in, m_l3_sinks, m_l3_w_out, m_final_norm, v_l0_norm, v_l0_w_in, v_l0_a_re, v_l0_a_im, v_l0_log_step, v_l0_b_re, v_l0_b_im, v_l0_c_re, v_l0_c_im, v_l0_d, v_l0_w_glu, v_l0_b_glu, v_l0_w_out, v_l1_norm, v_l1_w_in, v_l1_sinks, v_l1_w_out, v_l2_norm, v_l2_w_in, v_l2_a_re, v_l2_a_im, v_l2_log_step, v_l2_b_re, v_l2_b_im, v_l2_c_re, v_l2_c_im, v_l2_d, v_l2_w_glu, v_l2_b_glu, v_l2_w_out, v_l3_norm, v_l3_w_in, v_l3_sinks, v_l3_w_out, v_final_norm):
    args = locals()
    weights = {n: args[n] for n in WEIGHT_NAMES}
    moments_m = {n: args["m_" + n] for n in WEIGHT_NAMES}
    moments_v = {n: args["v_" + n] for n in WEIGHT_NAMES}
    return _train_step(x, loss_target, weights, moments_m, moments_v)
```

```python
import functools
import math

import jax
import jax.numpy as jnp
from jax import lax
from jax.experimental import pallas as pl
from jax.experimental.pallas import tpu as pltpu

F32 = jnp.float32
MXU_DTYPE = jnp.bfloat16
WIRE_DTYPE = jnp.bfloat16
MESH = pl.DeviceIdType.MESH

D_MODEL = 1024
BRANCH = 1024
NORM_EPS = 1e-5
SSM_GROUPS = 64
SSM_GROUP = 16
SSM_STATE = 64
S5_CHUNK = 16
LANES = 128
S5_OCT = LANES // SSM_GROUP
S5_OCTETS = SSM_GROUPS // S5_OCT
S5_OCT_IN = S5_CHUNK * LANES
S5_OCT_STATE = S5_OCT * SSM_STATE
S5_STATES = SSM_GROUPS * SSM_STATE
HEAD_DIM = 64
N_Q_HEADS = 16
N_KV_HEADS = 2
GQA_GROUP = N_Q_HEADS // N_KV_HEADS
ATTN_BLOCK = 128
Q_DIM = N_Q_HEADS * HEAD_DIM
KV_DIM = N_KV_HEADS * HEAD_DIM
ROPE_THETA = 10000.0
NEG_INF = -1e30
ADAM_LR = 0.001
ADAM_B1 = 0.9
ADAM_B2 = 0.999
ADAM_EPS = 1e-08
ADAM_WD = 0.01
ADAM_STEP = 10

VMEM_LIMIT_V7X = 56 * 1024 * 1024
ROW_TILE_FWD = 512
ROW_TILE_BWD = 256

SSM_NAMES = ("norm", "w_in", "a_re", "a_im", "log_step", "b_re", "b_im", "c_re", "c_im", "d", "w_glu", "b_glu", "w_out")
ATTN_NAMES = ("norm", "w_in", "sinks", "w_out")


def _weight_names():
    names = []
    for i in range(4):
        for n in (SSM_NAMES if i % 2 == 0 else ATTN_NAMES):
            names.append("l%d_%s" % (i, n))
    names.append("final_norm")
    return names


WEIGHT_NAMES = _weight_names()
BIG_NAMES = [n for n in WEIGHT_NAMES if n.endswith(("w_in", "w_glu", "w_out"))]
SMALL_NAMES = [n for n in WEIGHT_NAMES if n not in BIG_NAMES]


def _params(semantics=None):
    return pltpu.CompilerParams(dimension_semantics=semantics, vmem_limit_bytes=VMEM_LIMIT_V7X)


def _rows(tm, n):
    return pl.BlockSpec((tm, n), lambda i: (i, 0))


def _whole(shape):
    return pl.BlockSpec(shape, lambda i: (0,) * len(shape))


def _sds(shape, dtype=F32):
    return jax.ShapeDtypeStruct(shape, dtype)


def _mm(a, b):
    return jnp.dot(a.astype(MXU_DTYPE), b.astype(MXU_DTYPE), preferred_element_type=F32)


def _mm_tn(a, b):
    return lax.dot_general(a.astype(MXU_DTYPE), b.astype(MXU_DTYPE), (((0,), (0,)), ((), ())), preferred_element_type=F32)


def _mm_nt(a, b):
    return lax.dot_general(a.astype(MXU_DTYPE), b.astype(MXU_DTYPE), (((1,), (1,)), ((), ())), preferred_element_type=F32)


def _sigmoid(x):
    return 1.0 / (1.0 + jnp.exp(-x))


def _silu(x):
    return x * _sigmoid(x)


def _silu_grad(x):
    s = _sigmoid(x)
    return s * (1.0 + x * (1.0 - s))


GELU_C0 = math.sqrt(2.0 / math.pi)
GELU_C1 = 0.044715


def _gelu(x):
    return 0.5 * x * (1.0 + jnp.tanh(GELU_C0 * (x + GELU_C1 * x * x * x)))


def _gelu_grad(x):
    th = jnp.tanh(GELU_C0 * (x + GELU_C1 * x * x * x))
    return 0.5 * (1.0 + th) + 0.5 * x * (1.0 - th * th) * GELU_C0 * (1.0 + 3.0 * GELU_C1 * x * x)


def _rms(x, g):
    r = lax.rsqrt(jnp.mean(x * x, axis=-1, keepdims=True) + NORM_EPS)
    xhat = x * r
    return r, xhat, xhat * g


def _rms_bwd(dh, g, r, xhat):
    dxhat = dh * g
    dx = r * (dxhat - xhat * jnp.mean(dxhat * xhat, axis=-1, keepdims=True))
    return dx, jnp.sum(dh * xhat, axis=0, keepdims=True)


def _swap_half_heads(x):
    n = x.shape[-1]
    lane = lax.broadcasted_iota(jnp.int32, x.shape, x.ndim - 1)
    first = (lane % HEAD_DIM) < (HEAD_DIM // 2)
    return jnp.where(first, pltpu.roll(x, n - HEAD_DIM // 2, x.ndim - 1), pltpu.roll(x, HEAD_DIM // 2, x.ndim - 1))


def _tile_lanes(t, reps):
    return jnp.concatenate([t] * reps, axis=1)


def ssm_proj_fwd(x, norm, w_in):
    t = x.shape[0]
    tm = min(ROW_TILE_FWD, t)

    def body(x_ref, g_ref, w_ref, u_ref, gate_ref):
        _, _, h = _rms(x_ref[...], g_ref[...])
        p = _mm(h, w_ref[...])
        u_ref[...] = p[:, :BRANCH]
        gate_ref[...] = p[:, BRANCH:]

    return pl.pallas_call(
        body, name="ssm_proj_fwd", grid=(t // tm,),
        in_specs=[_rows(tm, D_MODEL), _whole((1, D_MODEL)), _whole((D_MODEL, 2 * BRANCH))],
        out_specs=[_rows(tm, BRANCH), _rows(tm, BRANCH)],
        out_shape=[_sds((t, BRANCH)), _sds((t, BRANCH))],
        compiler_params=_params(("parallel",)),
    )(x, norm, w_in)


def _chunk_rows(ref, nk, dtype=None):
    rows = jnp.concatenate([ref[pl.ds(s, nk, stride=S5_CHUNK), :] for s in range(S5_CHUNK)], axis=1)
    return rows.astype(MXU_DTYPE if dtype is None else dtype)


def _store_chunk_rows(ref, val, nk):
    for s in range(S5_CHUNK):
        ref[pl.ds(s, nk, stride=S5_CHUNK), :] = val[:, s * LANES:(s + 1) * LANES]


def _own_group_mask():
    row = lax.broadcasted_iota(jnp.int32, (S5_OCT_IN, S5_OCT_STATE), 0)
    col = lax.broadcasted_iota(jnp.int32, (S5_OCT_IN, S5_OCT_STATE), 1)
    return ((row % LANES) // SSM_GROUP) == (col // SSM_STATE)


def _spread_groups(w):
    return jnp.where(_own_group_mask(), jnp.concatenate([w] * (S5_OCT_STATE // LANES), axis=1), 0.0).astype(MXU_DTYPE)


def _fold_groups(p):
    p = jnp.where(_own_group_mask(), p, 0.0)
    return sum(p[:, q * LANES:(q + 1) * LANES] for q in range(S5_OCT_STATE // LANES))


def _fill_toeplitz(win_ref, kd_ref):
    win_ref[...] = jnp.zeros_like(win_ref)
    for s in range(S5_CHUNK):
        for t in range(s, S5_CHUNK):
            win_ref[s * LANES:(s + 1) * LANES, t * LANES:(t + 1) * LANES] = kd_ref[t - s].astype(MXU_DTYPE)


def _strip(t):
    return pl.BlockSpec((t, LANES), lambda b: (0, b))


def _oct_states(nk):
    return pl.BlockSpec((nk, S5_OCT_STATE), lambda b: (0, b))


OCT_W = pl.BlockSpec((None, S5_OCT_IN, LANES), lambda b: (b, 0, 0))
OCT_KD = pl.BlockSpec((None, S5_CHUNK, LANES, LANES), lambda b: (b, 0, 0, 0))


def s5_chunk_states(u, ws_re, ws_im):
    t = u.shape[0]
    nk = t // S5_CHUNK

    def body(u_ref, wr_ref, wi_ref, re_ref, im_ref):
        uc = _chunk_rows(u_ref, nk)
        re_ref[...] = _mm(uc, _spread_groups(wr_ref[...]))
        im_ref[...] = _mm(uc, _spread_groups(wi_ref[...]))

    return pl.pallas_call(
        body, name="s5_chunk_states", grid=(S5_OCTETS,),
        in_specs=[_strip(t), OCT_W, OCT_W], out_specs=[_oct_states(nk), _oct_states(nk)],
        out_shape=[_sds((nk, S5_STATES)), _sds((nk, S5_STATES))],
        compiler_params=_params(("parallel",)),
    )(u, ws_re, ws_im)


def s5_scan_fwd(s_re, s_im, a_re, a_im):
    nk = s_re.shape[0]

    def body(sre_ref, sim_ref, ar_ref, ai_ref, hre_ref, him_ref):
        ar = ar_ref[...]
        ai = ai_ref[...]

        def step(k, carry):
            hr, hi = carry
            hre_ref[pl.ds(k, 1), :] = hr
            him_ref[pl.ds(k, 1), :] = hi
            sr = sre_ref[pl.ds(k, 1), :]
            si = sim_ref[pl.ds(k, 1), :]
            return ar * hr - ai * hi + sr, ai * hr + ar * hi + si

        zero = jnp.zeros((1, S5_STATES), F32)
        lax.fori_loop(0, nk, step, (zero, zero))

    vm = pl.BlockSpec(memory_space=pltpu.VMEM)
    return pl.pallas_call(
        body, name="s5_scan_fwd", in_specs=[vm, vm, vm, vm], out_specs=[vm, vm],
        out_shape=[_sds((nk, S5_STATES)), _sds((nk, S5_STATES))],
        compiler_params=_params(),
    )(s_re, s_im, a_re, a_im)


def s5_outputs(u, h_re, h_im, kd, wo_re, wo_im):
    t = u.shape[0]
    nk = t // S5_CHUNK

    def body(u_ref, hre_ref, him_ref, kd_ref, wor_ref, woi_ref, y_ref, win_ref):
        _fill_toeplitz(win_ref, kd_ref)
        y = _mm(_chunk_rows(u_ref, nk), win_ref[...])
        y = y + _mm_nt(hre_ref[...], _spread_groups(wor_ref[...])) + _mm_nt(him_ref[...], _spread_groups(woi_ref[...]))
        _store_chunk_rows(y_ref, y, nk)

    return pl.pallas_call(
        body, name="s5_outputs", grid=(S5_OCTETS,),
        in_specs=[_strip(t), _oct_states(nk), _oct_states(nk), OCT_KD, OCT_W, OCT_W],
        out_specs=_strip(t), out_shape=_sds((t, BRANCH)),
        scratch_shapes=[pltpu.VMEM((S5_OCT_IN, S5_OCT_IN), MXU_DTYPE)],
        compiler_params=_params(("parallel",)),
    )(u, h_re, h_im, kd, wo_re, wo_im)


def s5_state_grads(dy, wo_re, wo_im):
    t = dy.shape[0]
    nk = t // S5_CHUNK

    def body(dy_ref, wor_ref, woi_ref, re_ref, im_ref):
        dyc = _chunk_rows(dy_ref, nk)
        re_ref[...] = _mm(dyc, _spread_groups(wor_ref[...]))
        im_ref[...] = _mm(dyc, _spread_groups(woi_ref[...]))

    return pl.pallas_call(
        body, name="s5_state_grads", grid=(S5_OCTETS,),
        in_specs=[_strip(t), OCT_W, OCT_W], out_specs=[_oct_states(nk), _oct_states(nk)],
        out_shape=[_sds((nk, S5_STATES)), _sds((nk, S5_STATES))],
        compiler_params=_params(("parallel",)),
    )(dy, wo_re, wo_im)


def s5_scan_bwd(dh_re, dh_im, h_re, h_im, a_re, a_im):
    nk = dh_re.shape[0]

    def body(dhr_ref, dhi_ref, hr_ref, hi_ref, ar_ref, ai_ref, dsr_ref, dsi_ref, dar_ref, dai_ref):
        ar = ar_ref[...]
        ai = ai_ref[...]

        dar_ref[...] = jnp.zeros_like(dar_ref)
        dai_ref[...] = jnp.zeros_like(dai_ref)

        def step(i, carry):
            gr, gi = carry
            k = nk - 1 - i
            dhr = dhr_ref[pl.ds(k, 1), :]
            dhi = dhi_ref[pl.ds(k, 1), :]
            dsr_ref[pl.ds(k, 1), :] = gr
            dsi_ref[pl.ds(k, 1), :] = gi
            hr = hr_ref[pl.ds(k, 1), :]
            hi = hi_ref[pl.ds(k, 1), :]
            dar_ref[...] += gr * hr + gi * hi
            dai_ref[...] += gi * hr - gr * hi
            return dhr + ar * gr + ai * gi, dhi - ai * gr + ar * gi

        zero = jnp.zeros((1, S5_STATES), F32)
        lax.fori_loop(0, nk, step, (zero, zero))

    vm = pl.BlockSpec(memory_space=pltpu.VMEM)
    return pl.pallas_call(
        body, name="s5_scan_bwd", in_specs=[vm] * 6, out_specs=[vm] * 4,
        out_shape=[_sds((nk, S5_STATES)), _sds((nk, S5_STATES)), _sds((1, S5_STATES)), _sds((1, S5_STATES))],
        input_output_aliases={0: 0, 1: 1}, compiler_params=_params(),
    )(dh_re, dh_im, h_re, h_im, a_re, a_im)


def s5_input_grads(dy, ds_re, ds_im, kd, ws_re, ws_im):
    t = dy.shape[0]
    nk = t // S5_CHUNK

    def body(dy_ref, dsr_ref, dsi_ref, kd_ref, wsr_ref, wsi_ref, du_ref, win_ref):
        _fill_toeplitz(win_ref, kd_ref)
        du = _mm_nt(_chunk_rows(dy_ref, nk), win_ref[...])
        du = du + _mm_nt(dsr_ref[...], _spread_groups(wsr_ref[...])) + _mm_nt(dsi_ref[...], _spread_groups(wsi_ref[...]))
        _store_chunk_rows(du_ref, du, nk)

    return pl.pallas_call(
        body, name="s5_input_grads", grid=(S5_OCTETS,),
        in_specs=[_strip(t), _oct_states(nk), _oct_states(nk), OCT_KD, OCT_W, OCT_W],
        out_specs=_strip(t), out_shape=_sds((t, BRANCH)),
        scratch_shapes=[pltpu.VMEM((S5_OCT_IN, S5_OCT_IN), MXU_DTYPE)],
        compiler_params=_params(("parallel",)),
    )(dy, ds_re, ds_im, kd, ws_re, ws_im)


def s5_weight_grads(u, dy, h_re, h_im, ds_re, ds_im):
    t = u.shape[0]
    nk = t // S5_CHUNK

    def body(u_ref, dy_ref, hre_ref, him_ref, dsr_ref, dsi_ref, dkd_ref, dwsr_ref, dwsi_ref, dwor_ref, dwoi_ref):
        dyc = _chunk_rows(dy_ref, nk, F32)
        uct = _chunk_rows(u_ref, nk, F32).T.astype(MXU_DTYPE)
        dyct = dyc.T.astype(MXU_DTYPE)
        dyc = dyc.astype(MXU_DTYPE)
        dwsr_ref[...] = _fold_groups(_mm(uct, dsr_ref[...]))
        dwsi_ref[...] = _fold_groups(_mm(uct, dsi_ref[...]))
        dwor_ref[...] = _fold_groups(_mm(dyct, hre_ref[...]))
        dwoi_ref[...] = _fold_groups(_mm(dyct, him_ref[...]))
        dkd_ref[...] = jnp.zeros_like(dkd_ref)
        for tt in range(S5_CHUNK):
            p = _mm(uct, dyc[:, tt * LANES:(tt + 1) * LANES])
            for s in range(tt + 1):
                dkd_ref[tt - s] += p[s * LANES:(s + 1) * LANES]

    return pl.pallas_call(
        body, name="s5_weight_grads", grid=(S5_OCTETS,),
        in_specs=[_strip(t), _strip(t)] + [_oct_states(nk)] * 4,
        out_specs=[OCT_KD, OCT_W, OCT_W, OCT_W, OCT_W],
        out_shape=[_sds((S5_OCTETS, S5_CHUNK, LANES, LANES))] + [_sds((S5_OCTETS, S5_OCT_IN, LANES))] * 4,
        compiler_params=_params(("parallel",)),
    )(u, dy, h_re, h_im, ds_re, ds_im)


def ssm_mix_fwd(x, u, gate, y_scan, d, w_glu, b_glu, w_out):
    t = x.shape[0]
    tm = min(ROW_TILE_FWD, t)

    def body(x_ref, u_ref, gate_ref, ys_ref, d_ref, wg_ref, bg_ref, wo_ref, y_ref, g2_ref, xo_ref):
        y = ys_ref[...] + d_ref[...] * u_ref[...]
        z0 = _gelu(y)
        g2 = _mm(z0, wg_ref[...]) + bg_ref[...]
        a = z0 * _sigmoid(g2) * _silu(gate_ref[...])
        y_ref[...] = y
        g2_ref[...] = g2
        xo_ref[...] = x_ref[...] + _mm(a, wo_ref[...])

    row = _rows(tm, BRANCH)
    vec = _whole((1, BRANCH))
    mat = _whole((BRANCH, BRANCH))
    return pl.pallas_call(
        body, name="ssm_mix_fwd", grid=(t // tm,),
        in_specs=[row, row, row, row, vec, mat, vec, mat],
        out_specs=[row, row, row],
        out_shape=[_sds((t, BRANCH))] * 3,
        compiler_params=_params(("parallel",)),
    )(x, u, gate, y_scan, d, w_glu, b_glu, w_out)


def ssm_mix_bwd(dxo, u, gate, y, g2, w_glu_t, w_out_t):
    t = dxo.shape[0]
    tm = min(ROW_TILE_BWD, t)

    def body(dxo_ref, u_ref, gate_ref, y_ref, g2_ref, wgt_ref, wot_ref, dy_ref, dgate_ref, dwo_ref, dwg_ref, dbg_ref, dd_ref):
        @pl.when(pl.program_id(0) == 0)
        def _():
            dwo_ref[...] = jnp.zeros_like(dwo_ref)
            dwg_ref[...] = jnp.zeros_like(dwg_ref)
            dbg_ref[...] = jnp.zeros_like(dbg_ref)
            dd_ref[...] = jnp.zeros_like(dd_ref)

        dxo = dxo_ref[...]
        gate = gate_ref[...]
        y = y_ref[...]
        z0 = _gelu(y)
        sg = _sigmoid(g2_ref[...])
        z = z0 * sg
        sgate = _silu(gate)
        da = _mm(dxo, wot_ref[...])
        dwo_ref[...] += _mm_tn(z * sgate, dxo)
        dz = da * sgate
        dgate_ref[...] = da * z * _silu_grad(gate)
        dg2 = dz * z0 * sg * (1.0 - sg)
        dbg_ref[...] += jnp.sum(dg2, axis=0, keepdims=True)
        dwg_ref[...] += _mm_tn(z0, dg2)
        dz0 = dz * sg + _mm(dg2, wgt_ref[...])
        dy = dz0 * _gelu_grad(y)
        dd_ref[...] += jnp.sum(dy * u_ref[...], axis=0, keepdims=True)
        dy_ref[...] = dy

    row = _rows(tm, BRANCH)
    vec = _whole((1, BRANCH))
    mat = _whole((BRANCH, BRANCH))
    return pl.pallas_call(
        body, name="ssm_mix_bwd", grid=(t // tm,),
        in_specs=[row, row, row, row, row, mat, mat],
        out_specs=[row, row, mat, mat, vec, vec],
        out_shape=[_sds((t, BRANCH)), _sds((t, BRANCH)), _sds((BRANCH, D_MODEL)), _sds((BRANCH, BRANCH)),
                   _sds((1, BRANCH)), _sds((1, BRANCH))],
        compiler_params=_params(("arbitrary",)),
    )(dxo, u, gate, y, g2, w_glu_t, w_out_t)


def ssm_proj_bwd(x, norm, dxo, dy, du_scan, dgate, d, w_in_t):
    t = x.shape[0]
    tm = min(ROW_TILE_BWD, t)
    n = 2 * BRANCH

    def body(x_ref, g_ref, dxo_ref, dy_ref, dus_ref, dgate_ref, d_ref, wt_ref, dx_ref, dw_ref, dg_ref):
        @pl.when(pl.program_id(0) == 0)
        def _():
            dw_ref[...] = jnp.zeros_like(dw_ref)
            dg_ref[...] = jnp.zeros_like(dg_ref)

        g = g_ref[...]
        r, xhat, h = _rms(x_ref[...], g)
        du = dus_ref[...] + d_ref[...] * dy_ref[...]
        dproj = jnp.concatenate([du, dgate_ref[...]], axis=1)
        dh = _mm(dproj, wt_ref[...])
        dw_ref[...] += _mm_tn(h, dproj)
        dx, dg = _rms_bwd(dh, g, r, xhat)
        dg_ref[...] += dg
        dx_ref[...] = dxo_ref[...] + dx

    row = _rows(tm, D_MODEL)
    vec = _whole((1, D_MODEL))
    return pl.pallas_call(
        body, name="ssm_proj_bwd", grid=(t // tm,),
        in_specs=[row, vec, row, row, row, row, vec, _whole((n, D_MODEL))],
        out_specs=[row, _whole((D_MODEL, n)), vec],
        out_shape=[_sds((t, D_MODEL)), _sds((D_MODEL, n)), _sds((1, D_MODEL))],
        compiler_params=_params(("arbitrary",)),
    )(x, norm, dxo, dy, du_scan, dgate, d, w_in_t)


ATTN_N = Q_DIM + 2 * KV_DIM + BRANCH


def attn_proj_fwd(x, norm, w_in, cos2, sin2):
    t = x.shape[0]
    tm = min(ROW_TILE_FWD, t)

    def body(x_ref, g_ref, w_ref, cos_ref, sin_ref, q_ref, k_ref, v_ref, gate_ref):
        _, _, h = _rms(x_ref[...], g_ref[...])
        p = _mm(h, w_ref[...])
        cs = cos_ref[...]
        sn = sin_ref[...]
        q = p[:, :Q_DIM]
        k = p[:, Q_DIM:Q_DIM + KV_DIM]
        q_ref[...] = q * _tile_lanes(cs, Q_DIM // LANES) + _swap_half_heads(q) * _tile_lanes(sn, Q_DIM // LANES)
        k_ref[...] = k * cs + _swap_half_heads(k) * sn
        v_ref[...] = p[:, Q_DIM + KV_DIM:Q_DIM + 2 * KV_DIM]
        gate_ref[...] = p[:, Q_DIM + 2 * KV_DIM:]

    return pl.pallas_call(
        body, name="attn_proj_fwd", grid=(t // tm,),
        in_specs=[_rows(tm, D_MODEL), _whole((1, D_MODEL)), _whole((D_MODEL, ATTN_N)), _rows(tm, LANES), _rows(tm, LANES)],
        out_specs=[_rows(tm, Q_DIM), _rows(tm, KV_DIM), _rows(tm, KV_DIM), _rows(tm, BRANCH)],
        out_shape=[_sds((t, Q_DIM)), _sds((t, KV_DIM)), _sds((t, KV_DIM)), _sds((t, BRANCH))],
        compiler_params=_params(("parallel",)),
    )(x, norm, w_in, cos2, sin2)


def _band_mask(first_block):
    qi = lax.broadcasted_iota(jnp.int32, (ATTN_BLOCK, 2 * ATTN_BLOCK), 0)
    kj = lax.broadcasted_iota(jnp.int32, (ATTN_BLOCK, 2 * ATTN_BLOCK), 1)
    dist = qi + ATTN_BLOCK - kj
    first_key = jnp.where(first_block, ATTN_BLOCK, 0)
    return (dist >= 0) & (dist < ATTN_BLOCK) & (kj >= first_key)


def _lane_is(h):
    return lax.broadcasted_iota(jnp.int32, (1, LANES), 1) == h


def attn_fwd(q, k, v, sinks):
    t = q.shape[0]
    nb = t // ATTN_BLOCK
    scale = HEAD_DIM ** -0.5

    def body(sink_ref, q_ref, kc_ref, kp_ref, vc_ref, vp_ref, o_ref, lse_ref):
        i = pl.program_id(0)
        keys = jnp.concatenate([kp_ref[...], kc_ref[...]], axis=0).astype(MXU_DTYPE)
        vals = jnp.concatenate([vp_ref[...], vc_ref[...]], axis=0).astype(MXU_DTYPE)
        valid = _band_mask(i == 0)
        lse = jnp.zeros((ATTN_BLOCK, LANES), F32)
        for h in range(N_Q_HEADS):
            hk = h // GQA_GROUP
            kh = keys[:, hk * HEAD_DIM:(hk + 1) * HEAD_DIM]
            vh = vals[:, hk * HEAD_DIM:(hk + 1) * HEAD_DIM]
            qh = q_ref[:, h * HEAD_DIM:(h + 1) * HEAD_DIM]
            s = jnp.where(valid, _mm_nt(qh, kh) * scale, NEG_INF)
            sink = sink_ref[h]
            m = jnp.maximum(jnp.max(s, axis=-1, keepdims=True), sink)
            p = jnp.exp(s - m)
            den = jnp.sum(p, axis=-1, keepdims=True) + jnp.exp(sink - m)
            o_ref[:, h * HEAD_DIM:(h + 1) * HEAD_DIM] = _mm(p, vh) / den
            lse = jnp.where(_lane_is(h), m + jnp.log(den), lse)
        lse_ref[...] = lse

    cur = lambda n: pl.BlockSpec((ATTN_BLOCK, n), lambda i: (i, 0))
    prev = lambda n: pl.BlockSpec((ATTN_BLOCK, n), lambda i: (jnp.maximum(i - 1, 0), 0))
    return pl.pallas_call(
        body, name="attn_fwd", grid=(nb,),
        in_specs=[pl.BlockSpec(memory_space=pltpu.SMEM), cur(Q_DIM), cur(KV_DIM), prev(KV_DIM), cur(KV_DIM), prev(KV_DIM)],
        out_specs=[cur(Q_DIM), cur(LANES)],
        out_shape=[_sds((t, Q_DIM)), _sds((t, LANES))],
        compiler_params=_params(("parallel",)),
    )(sinks, q, k, k, v, v)


def attn_bwd(q, k, v, sinks, o, lse, do):
    t = q.shape[0]
    nb = t // ATTN_BLOCK
    scale = HEAD_DIM ** -0.5

    def body(sink_ref, q_ref, kc_ref, kp_ref, vc_ref, vp_ref, o_ref, lse_ref, do_ref,
             dq_ref, dk_ref, dv_ref, dsink_ref, dk_carry, dv_carry):
        i = pl.program_id(0)

        @pl.when(i == 0)
        def _():
            dsink_ref[...] = jnp.zeros_like(dsink_ref)
            dk_carry[...] = jnp.zeros_like(dk_carry)
            dv_carry[...] = jnp.zeros_like(dv_carry)

        @pl.when(i < nb)
        def _():
            keys = jnp.concatenate([kp_ref[...], kc_ref[...]], axis=0).astype(MXU_DTYPE)
            vals = jnp.concatenate([vp_ref[...], vc_ref[...]], axis=0).astype(MXU_DTYPE)
            valid = _band_mask(i == 0)
            lse_all = lse_ref[...]
            dsink = jnp.zeros((1, LANES), F32)
            dk_heads = []
            dv_heads = []
            for hk in range(N_KV_HEADS):
                kh = keys[:, hk * HEAD_DIM:(hk + 1) * HEAD_DIM]
                vh = vals[:, hk * HEAD_DIM:(hk + 1) * HEAD_DIM]
                dkk = jnp.zeros((2 * ATTN_BLOCK, HEAD_DIM), F32)
                dvv = jnp.zeros((2 * ATTN_BLOCK, HEAD_DIM), F32)
                for hq in range(GQA_GROUP):
                    h = hk * GQA_GROUP + hq
                    sl = slice(h * HEAD_DIM, (h + 1) * HEAD_DIM)
                    qh = q_ref[:, sl]
                    doh = do_ref[:, sl]
                    lse_h = jnp.sum(jnp.where(_lane_is(h), lse_all, 0.0), axis=-1, keepdims=True)
                    s = jnp.where(valid, _mm_nt(qh, kh) * scale, NEG_INF)
                    p = jnp.exp(s - lse_h)
                    delta = jnp.sum(doh * o_ref[:, sl], axis=-1, keepdims=True)
                    dvv = dvv + _mm_tn(p, doh)
                    ds = p * (_mm_nt(doh, vh) - delta)
                    dq_ref[:, sl] = _mm(ds, kh) * scale
                    dkk = dkk + _mm_tn(ds, qh) * scale
                    dsink = dsink + jnp.where(_lane_is(h), -jnp.sum(jnp.exp(sink_ref[h] - lse_h) * delta), 0.0)
                dk_heads.append(dkk)
                dv_heads.append(dvv)
            dkk = jnp.concatenate(dk_heads, axis=1)
            dvv = jnp.concatenate(dv_heads, axis=1)
            dsink_ref[...] += dsink
            dk_ref[...] = dk_carry[...] + dkk[:ATTN_BLOCK]
            dv_ref[...] = dv_carry[...] + dvv[:ATTN_BLOCK]
            dk_carry[...] = dkk[ATTN_BLOCK:]
            dv_carry[...] = dvv[ATTN_BLOCK:]

        @pl.when(i == nb)
        def _():
            dk_ref[...] = dk_carry[...]
            dv_ref[...] = dv_carry[...]

    last = nb - 1
    cur = lambda n: pl.BlockSpec((ATTN_BLOCK, n), lambda i: (jnp.minimum(i, last), 0))
    prev = lambda n: pl.BlockSpec((ATTN_BLOCK, n), lambda i: (jnp.clip(i - 1, 0, last), 0))
    late = lambda n: pl.BlockSpec((ATTN_BLOCK, n), lambda i: (i, 0))
    dq, dk_late, dv_late, dsinks = pl.pallas_call(
        body, name="attn_bwd", grid=(nb + 1,),
        in_specs=[pl.BlockSpec(memory_space=pltpu.SMEM), cur(Q_DIM), cur(KV_DIM), prev(KV_DIM), cur(KV_DIM), prev(KV_DIM),
                  cur(Q_DIM), cur(LANES), cur(Q_DIM)],
        out_specs=[cur(Q_DIM), late(KV_DIM), late(KV_DIM), _whole((1, LANES))],
        out_shape=[_sds((t, Q_DIM)), _sds((t + ATTN_BLOCK, KV_DIM)), _sds((t + ATTN_BLOCK, KV_DIM)), _sds((1, LANES))],
        scratch_shapes=[pltpu.VMEM((ATTN_BLOCK, KV_DIM), F32), pltpu.VMEM((ATTN_BLOCK, KV_DIM), F32)],
        compiler_params=_params(("arbitrary",)),
    )(sinks, q, k, k, v, v, o, lse, do)
    return dq, dk_late[ATTN_BLOCK:], dv_late[ATTN_BLOCK:], dsinks


def attn_out_fwd(x, o, gate, w_out):
    t = x.shape[0]
    tm = min(ROW_TILE_FWD, t)

    def body(x_ref, o_ref, gate_ref, w_ref, xo_ref):
        xo_ref[...] = x_ref[...] + _mm(o_ref[...] * _silu(gate_ref[...]), w_ref[...])

    row = _rows(tm, D_MODEL)
    return pl.pallas_call(
        body, name="attn_out_fwd", grid=(t // tm,),
        in_specs=[row, row, row, _whole((Q_DIM, D_MODEL))], out_specs=row, out_shape=_sds((t, D_MODEL)),
        compiler_params=_params(("parallel",)),
    )(x, o, gate, w_out)


def attn_out_bwd(dxo, o, gate, w_out_t):
    t = dxo.shape[0]
    tm = min(ROW_TILE_BWD, t)

    def body(dxo_ref, o_ref, gate_ref, wt_ref, do_ref, dgate_ref, dw_ref):
        @pl.when(pl.program_id(0) == 0)
        def _():
            dw_ref[...] = jnp.zeros_like(dw_ref)

        dxo = dxo_ref[...]
        o = o_ref[...]
        gate = gate_ref[...]
        sgate = _silu(gate)
        da = _mm(dxo, wt_ref[...])
        dw_ref[...] += _mm_tn(o * sgate, dxo)
        do_ref[...] = da * sgate
        dgate_ref[...] = da * o * _silu_grad(gate)

    row = _rows(tm, D_MODEL)
    mat = _whole((Q_DIM, D_MODEL))
    return pl.pallas_call(
        body, name="attn_out_bwd", grid=(t // tm,),
        in_specs=[row, row, row, mat], out_specs=[row, row, mat],
        out_shape=[_sds((t, Q_DIM)), _sds((t, BRANCH)), _sds((Q_DIM, D_MODEL))],
        compiler_params=_params(("arbitrary",)),
    )(dxo, o, gate, w_out_t)


def attn_proj_bwd(x, norm, dxo, dq, dk, dv, dgate, cos2, sin2, w_in_t):
    t = x.shape[0]
    tm = min(ROW_TILE_BWD, t)

    def body(x_ref, g_ref, dxo_ref, dq_ref, dk_ref, dv_ref, dgate_ref, cos_ref, sin_ref, wt_ref, dx_ref, dw_ref, dg_ref):
        @pl.when(pl.program_id(0) == 0)
        def _():
            dw_ref[...] = jnp.zeros_like(dw_ref)
            dg_ref[...] = jnp.zeros_like(dg_ref)

        g = g_ref[...]
        r, xhat, h = _rms(x_ref[...], g)
        cs = cos_ref[...]
        sn = sin_ref[...]
        dqr = dq_ref[...]
        dkr = dk_ref[...]
        dq = dqr * _tile_lanes(cs, Q_DIM // LANES) + _swap_half_heads(dqr * _tile_lanes(sn, Q_DIM // LANES))
        dk = dkr * cs + _swap_half_heads(dkr * sn)
        dproj = jnp.concatenate([dq, dk, dv_ref[...], dgate_ref[...]], axis=1)
        dh = _mm(dproj, wt_ref[...])
        dw_ref[...] += _mm_tn(h, dproj)
        dx, dg = _rms_bwd(dh, g, r, xhat)
        dg_ref[...] += dg
        dx_ref[...] = dxo_ref[...] + dx

    row = _rows(tm, D_MODEL)
    vec = _whole((1, D_MODEL))
    return pl.pallas_call(
        body, name="attn_proj_bwd", grid=(t // tm,),
        in_specs=[row, vec, row, _rows(tm, Q_DIM), _rows(tm, KV_DIM), _rows(tm, KV_DIM), _rows(tm, BRANCH),
                  _rows(tm, LANES), _rows(tm, LANES), _whole((ATTN_N, D_MODEL))],
        out_specs=[row, _whole((D_MODEL, ATTN_N)), vec],
        out_shape=[_sds((t, D_MODEL)), _sds((D_MODEL, ATTN_N)), _sds((1, D_MODEL))],
        compiler_params=_params(("arbitrary",)),
    )(x, norm, dxo, dq, dk, dv, dgate, cos2, sin2, w_in_t)


def loss_head(x, norm, target):
    t = x.shape[0]
    tm = min(ROW_TILE_FWD, t)

    def body(x_ref, g_ref, tgt_ref, loss_ref, dx_ref, dg_ref):
        @pl.when(pl.program_id(0) == 0)
        def _():
            loss_ref[...] = jnp.zeros_like(loss_ref)
            dg_ref[...] = jnp.zeros_like(dg_ref)

        g = g_ref[...]
        r, xhat, y = _rms(x_ref[...], g)
        err = y - tgt_ref[...]
        loss_ref[...] += 0.5 * jnp.sum(jnp.mean(err * err, axis=-1, keepdims=True), axis=0, keepdims=True)
        dx, dg = _rms_bwd(err * (1.0 / D_MODEL), g, r, xhat)
        dg_ref[...] += dg
        dx_ref[...] = dx

    row = _rows(tm, D_MODEL)
    vec = _whole((1, D_MODEL))
    return pl.pallas_call(
        body, name="loss_head", grid=(t // tm,),
        in_specs=[row, vec, row], out_specs=[_whole((1, 1)), row, vec],
        out_shape=[_sds((1, 1)), _sds((t, D_MODEL)), _sds((1, D_MODEL))],
        compiler_params=_params(("arbitrary",)),
    )(x, norm, target)


OCT_TILE = pl.BlockSpec((None, LANES, LANES), lambda b: (b, 0, 0))
N_LAGS = S5_CHUNK + 1


def _cmul(ar, ai, br, bi):
    return ar * br - ai * bi, ar * bi + ai * br


def _cmul_conj(ar, ai, br, bi):
    return ar * br + ai * bi, ar * bi - ai * br


def _mm_f32(a, b, dims):
    return lax.dot_general(a, b, (dims, ((), ())), precision=lax.Precision.HIGHEST, preferred_element_type=F32)


def _s5_discretise(ar, ai, ls, br, bi):
    dt = jnp.exp(ls)
    xr = ar * dt
    xi = ai * dt
    powers = []
    for lag in range(N_LAGS):
        mag = jnp.exp(lag * xr)
        powers.append((mag * jnp.cos(lag * xi), mag * jnp.sin(lag * xi)))
    den = ar * ar + ai * ai
    nr = powers[1][0] - 1.0
    ni = powers[1][1]
    fr = (nr * ar + ni * ai) / den
    fi = (ni * ar - nr * ai) / den
    bbr, bbi = _cmul(fr, fi, br, bi)
    return dt, powers, (fr, fi), (bbr, bbi), den


def _same_group_tile():
    row = lax.broadcasted_iota(jnp.int32, (LANES, LANES), 0)
    col = lax.broadcasted_iota(jnp.int32, (LANES, LANES), 1)
    return (row // SSM_GROUP) == (col // SSM_GROUP)


def _first_copy_lanes():
    return lax.broadcasted_iota(jnp.int32, (LANES, LANES), 1) < SSM_STATE


def s5_param_fwd(tiles):
    def body(ar_ref, ai_ref, ls_ref, br_ref, bi_ref, cr_ref, ci_ref, kd_ref, wsr_ref, wsi_ref, wor_ref, woi_ref, pr_ref, pi_ref):
        cr = cr_ref[...]
        ci = ci_ref[...]
        _, powers, _, (bbr, bbi), _ = _s5_discretise(ar_ref[...], ai_ref[...], ls_ref[...], br_ref[...], bi_ref[...])
        once = _first_copy_lanes()
        crm = jnp.where(once, cr, 0.0)
        cim = jnp.where(once, ci, 0.0)
        same = _same_group_tile()
        for lag in range(S5_CHUNK):
            er, ei = powers[lag]
            xr, xi = _cmul(er, ei, bbr, bbi)
            rows = pl.ds((S5_CHUNK - 1 - lag) * LANES, LANES)
            wsr_ref[rows, :] = xr
            wsi_ref[rows, :] = xi
            k = _mm_f32(xr, crm, ((1,), (1,))) - _mm_f32(xi, cim, ((1,), (1,)))
            kd_ref[lag] = jnp.where(same, k, 0.0)
        for t in range(S5_CHUNK):
            er, ei = powers[t + 1]
            zr, zi = _cmul(er, ei, cr, ci)
            wor_ref[pl.ds(t * LANES, LANES), :] = zr
            woi_ref[pl.ds(t * LANES, LANES), :] = -zi
        pr_ref[...] = powers[S5_CHUNK][0]
        pi_ref[...] = powers[S5_CHUNK][1]

    return pl.pallas_call(
        body, name="s5_param_fwd", grid=(S5_OCTETS,),
        in_specs=[OCT_TILE] * 7, out_specs=[OCT_KD, OCT_W, OCT_W, OCT_W, OCT_W, OCT_TILE, OCT_TILE],
        out_shape=[_sds((S5_OCTETS, S5_CHUNK, LANES, LANES))] + [_sds((S5_OCTETS, S5_OCT_IN, LANES))] * 4
                  + [_sds((S5_OCTETS, LANES, LANES))] * 2,
        compiler_params=_params(("parallel",)),
    )(*tiles)


def s5_param_bwd(tiles, dkd, dws_re, dws_im, dwo_re, dwo_im, dp_re, dp_im):
    def body(ar_ref, ai_ref, ls_ref, br_ref, bi_ref, cr_ref, ci_ref, dkd_ref, dwsr_ref, dwsi_ref, dwor_ref, dwoi_ref, dpr_ref, dpi_ref,
             dar_ref, dai_ref, dls_ref, dbr_ref, dbi_ref, dcr_ref, dci_ref):
        ar = ar_ref[...]
        ai = ai_ref[...]
        br = br_ref[...]
        bi = bi_ref[...]
        cr = cr_ref[...]
        ci = ci_ref[...]
        dt, powers, (fr, fi), (bbr, bbi), den = _s5_discretise(ar, ai, ls_ref[...], br, bi)
        once = _first_copy_lanes()
        crm = jnp.where(once, cr, 0.0)
        cim = jnp.where(once, ci, 0.0)
        same = _same_group_tile()
        zero = jnp.zeros((LANES, LANES), F32)
        dpow = [[zero, zero] for _ in range(N_LAGS)]
        dbbr, dbbi, dcr, dci = zero, zero, zero, zero
        for lag in range(S5_CHUNK):
            er, ei = powers[lag]
            xr, xi = _cmul(er, ei, bbr, bbi)
            rows = pl.ds((S5_CHUNK - 1 - lag) * LANES, LANES)
            g = jnp.where(same, dkd_ref[lag], 0.0)
            dxr = dwsr_ref[rows, :] + _mm_f32(g, crm, ((1,), (0,)))
            dxi = dwsi_ref[rows, :] - _mm_f32(g, cim, ((1,), (0,)))
            dcr = dcr + jnp.where(once, _mm_f32(g, xr, ((0,), (0,))), 0.0)
            dci = dci - jnp.where(once, _mm_f32(g, xi, ((0,), (0,))), 0.0)
            a, b = _cmul_conj(bbr, bbi, dxr, dxi)
            dpow[lag][0] = dpow[lag][0] + a
            dpow[lag][1] = dpow[lag][1] + b
            a, b = _cmul_conj(er, ei, dxr, dxi)
            dbbr = dbbr + a
            dbbi = dbbi + b
        for t in range(S5_CHUNK):
            er, ei = powers[t + 1]
            dzr = dwor_ref[pl.ds(t * LANES, LANES), :]
            dzi = -dwoi_ref[pl.ds(t * LANES, LANES), :]
            a, b = _cmul_conj(cr, ci, dzr, dzi)
            dpow[t + 1][0] = dpow[t + 1][0] + a
            dpow[t + 1][1] = dpow[t + 1][1] + b
            a, b = _cmul_conj(er, ei, dzr, dzi)
            dcr = dcr + a
            dci = dci + b
        dpow[S5_CHUNK][0] = dpow[S5_CHUNK][0] + dpr_ref[...]
        dpow[S5_CHUNK][1] = dpow[S5_CHUNK][1] + dpi_ref[...]
        dfr, dfi = _cmul_conj(br, bi, dbbr, dbbi)
        dbr, dbi = _cmul_conj(fr, fi, dbbr, dbbi)
        dnr, dni = _cmul(ar / den, ai / den, dfr, dfi)
        qr = (fr * ar + fi * ai) / den
        qi = (fi * ar - fr * ai) / den
        dlr, dli = _cmul(-qr, qi, dfr, dfi)
        dpow[1][0] = dpow[1][0] + dnr
        dpow[1][1] = dpow[1][1] + dni
        dxr, dxi = zero, zero
        for lag in range(1, N_LAGS):
            a, b = _cmul_conj(powers[lag][0], powers[lag][1], dpow[lag][0], dpow[lag][1])
            dxr = dxr + lag * a
            dxi = dxi + lag * b
        dar_ref[...] = dlr + dt * dxr
        dai_ref[...] = dli + dt * dxi
        dls_ref[...] = dt * (ar * dxr + ai * dxi)
        dbr_ref[...] = dbr
        dbi_ref[...] = dbi
        dcr_ref[...] = dcr
        dci_ref[...] = dci

    return pl.pallas_call(
        body, name="s5_param_bwd", grid=(S5_OCTETS,),
        in_specs=[OCT_TILE] * 7 + [OCT_KD, OCT_W, OCT_W, OCT_W, OCT_W, OCT_TILE, OCT_TILE], out_specs=[OCT_TILE] * 7,
        out_shape=[_sds((S5_OCTETS, LANES, LANES))] * 7,
        compiler_params=_params(("parallel",)),
    )(*tiles, dkd, dws_re, dws_im, dwo_re, dwo_im, dp_re, dp_im)


def _doubled(v):
    return jnp.concatenate([v, v], axis=-1)


def _s5_param_tiles(a_re, a_im, log_step, b_re, b_im, c_re, c_im):
    def per_group(a):
        return _doubled(jnp.broadcast_to(a.reshape(S5_OCTETS, S5_OCT, 1, SSM_STATE),
                                         (S5_OCTETS, S5_OCT, SSM_GROUP, SSM_STATE)).reshape(S5_OCTETS, LANES, SSM_STATE))

    ls = jnp.broadcast_to(log_step.reshape(S5_OCTETS, S5_OCT, 1, 1), (S5_OCTETS, S5_OCT, SSM_GROUP, LANES)).reshape(S5_OCTETS, LANES, LANES)
    bt = lambda b: _doubled(b.transpose(0, 2, 1).reshape(S5_OCTETS, LANES, SSM_STATE))
    ct = lambda c: _doubled(c.reshape(S5_OCTETS, LANES, SSM_STATE))
    return [per_group(a_re), per_group(a_im), ls, bt(b_re), bt(b_im), ct(c_re), ct(c_im)]


def _s5_param_grads(dtiles):
    dar, dai, dls, dbr, dbi, dcr, dci = dtiles
    halves = lambda d: d[..., :SSM_STATE] + d[..., SSM_STATE:]
    per_group = lambda d: halves(d).reshape(SSM_GROUPS, SSM_GROUP, SSM_STATE).sum(axis=1)
    per_row = lambda d: halves(d).reshape(SSM_GROUPS, SSM_GROUP, SSM_STATE)
    return (per_group(dar), per_group(dai), dls.reshape(SSM_GROUPS, SSM_GROUP * LANES).sum(axis=1),
            per_row(dbr).transpose(0, 2, 1), per_row(dbi).transpose(0, 2, 1), per_row(dcr), per_row(dci))


def _group_power_rows(tile):
    return tile[:, ::SSM_GROUP, :SSM_STATE].reshape(1, S5_STATES)


def _group_power_tiles(row):
    t = jnp.pad(row.reshape(S5_OCTETS, S5_OCT, 1, SSM_STATE), ((0, 0), (0, 0), (0, SSM_GROUP - 1), (0, LANES - SSM_STATE)))
    return t.reshape(S5_OCTETS, LANES, LANES)


def _rope_tables(t):
    pos = jnp.arange(t, dtype=F32)
    inv_freq = ROPE_THETA ** (-jnp.arange(0, HEAD_DIM, 2, dtype=F32) / HEAD_DIM)
    ang = pos[:, None] * inv_freq[None, :]
    cos = jnp.cos(ang)
    sin = jnp.sin(ang)
    cos64 = jnp.concatenate([cos, cos], axis=1)
    sin64 = jnp.concatenate([-sin, sin], axis=1)
    return jnp.concatenate([cos64, cos64], axis=1), jnp.concatenate([sin64, sin64], axis=1)


def _row(v):
    return v.reshape(1, -1)


def _ssm_forward(x, w):
    tiles = _s5_param_tiles(w["a_re"], w["a_im"], w["log_step"], w["b_re"], w["b_im"], w["c_re"], w["c_im"])
    kd, ws_re, ws_im, wo_re, wo_im, p_re, p_im = s5_param_fwd(tiles)
    mats = dict(kd=kd, ws_re=ws_re, ws_im=ws_im, wo_re=wo_re, wo_im=wo_im, a_re=_group_power_rows(p_re), a_im=_group_power_rows(p_im))
    u, gate = ssm_proj_fwd(x, _row(w["norm"]), w["w_in"])
    s_re, s_im = s5_chunk_states(u, mats["ws_re"], mats["ws_im"])
    h_re, h_im = s5_scan_fwd(s_re, s_im, mats["a_re"], mats["a_im"])
    y_scan = s5_outputs(u, h_re, h_im, mats["kd"], mats["wo_re"], mats["wo_im"])
    y, g2, x_new = ssm_mix_fwd(x, u, gate, y_scan, _row(w["d"]), w["w_glu"], _row(w["b_glu"]), w["w_out"])
    saved = dict(x=x, u=u, gate=gate, y=y, g2=g2, h_re=h_re, h_im=h_im, mats=mats, tiles=tiles)
    return x_new, saved


def _ssm_backward(dxo, w, s):
    dy, dgate, dw_out, dw_glu, db_glu, dd = ssm_mix_bwd(dxo, s["u"], s["gate"], s["y"], s["g2"], w["w_glu_t"], w["w_out_t"])
    mats = s["mats"]
    dh_re, dh_im = s5_state_grads(dy, mats["wo_re"], mats["wo_im"])
    ds_re, ds_im, da_re, da_im = s5_scan_bwd(dh_re, dh_im, s["h_re"], s["h_im"], mats["a_re"], mats["a_im"])
    du_scan = s5_input_grads(dy, ds_re, ds_im, mats["kd"], mats["ws_re"], mats["ws_im"])
    dkd, dws_re, dws_im, dwo_re, dwo_im = s5_weight_grads(s["u"], dy, s["h_re"], s["h_im"], ds_re, ds_im)
    dparams = _s5_param_grads(s5_param_bwd(s["tiles"], dkd, dws_re, dws_im, dwo_re, dwo_im,
                                           _group_power_tiles(da_re), _group_power_tiles(da_im)))
    dx, dw_in, dnorm = ssm_proj_bwd(s["x"], _row(w["norm"]), dxo, dy, du_scan, dgate, _row(w["d"]), w["w_in_t"])
    grads = dict(norm=dnorm, w_in=dw_in, d=dd, w_glu=dw_glu, b_glu=db_glu, w_out=dw_out)
    for name, val in zip(("a_re", "a_im", "log_step", "b_re", "b_im", "c_re", "c_im"), dparams):
        grads[name] = val
    return dx, grads


def _attn_forward(x, w, cos2, sin2):
    q, k, v, gate = attn_proj_fwd(x, _row(w["norm"]), w["w_in"], cos2, sin2)
    o, lse = attn_fwd(q, k, v, w["sinks"])
    x_new = attn_out_fwd(x, o, gate, w["w_out"])
    return x_new, dict(x=x, q=q, k=k, v=v, gate=gate, o=o, lse=lse)


def _attn_backward(dxo, w, s, cos2, sin2):
    do, dgate, dw_out = attn_out_bwd(dxo, s["o"], s["gate"], w["w_out_t"])
    dq, dk, dv, dsinks = attn_bwd(s["q"], s["k"], s["v"], w["sinks"], s["o"], s["lse"], do)
    dx, dw_in, dnorm = attn_proj_bwd(s["x"], _row(w["norm"]), dxo, dq, dk, dv, dgate, cos2, sin2, w["w_in_t"])
    return dx, dict(norm=dnorm, w_in=dw_in, sinks=dsinks[0, :N_Q_HEADS], w_out=dw_out)


def _sequence_step(x, target, layers, final_norm):
    cos2, sin2 = _rope_tables(x.shape[0])
    saved = []
    for i, w in enumerate(layers):
        if i % 2 == 0:
            x, s = _ssm_forward(x, w)
        else:
            x, s = _attn_forward(x, w, cos2, sin2)
        saved.append(s)
    loss, dx, dfinal = loss_head(x, _row(final_norm), target)
    grads = {"final_norm": dfinal}
    for i in reversed(range(len(layers))):
        if i % 2 == 0:
            dx, g = _ssm_backward(dx, layers[i], saved[i])
        else:
            dx, g = _attn_backward(dx, layers[i], saved[i], cos2, sin2)
        for name, val in g.items():
            grads["l%d_%s" % (i, name)] = val
    return loss[0, 0], dx, grads


ANY = pl.BlockSpec(memory_space=pl.ANY)


def _place():
    return lax.axis_index("x"), lax.axis_index("y"), lax.axis_index("c")


def _other_chips(x, y):
    return [(1 - x, y), (x, 1 - y), (1 - x, 1 - y)]


def gather_weight_shards(shards):
    n = len(shards)

    def body(*refs):
        ins, outs = refs[:n], refs[n:2 * n]
        send_sems, recv_sems, pass_send_sems, pass_recv_sems, local_sems = refs[2 * n:]
        x, y, c = _place()
        me = 2 * x + y
        chips = _other_chips(x, y)

        def half(i, block, which):
            rows = ins[i].shape[0] // 2
            return outs[i].at[block, pl.ds(which * rows, rows), :]

        def my_half(i):
            rows = ins[i].shape[0] // 2
            return ins[i].at[pl.ds(c * rows, rows), :]

        started = []
        for i in range(n):
            local = pltpu.make_async_copy(ins[i], outs[i].at[me], local_sems.at[i])
            local.start()
            started.append(local)
        sends = []
        for i in range(n):
            for k, (tx, ty) in enumerate(chips):
                cp = pltpu.make_async_remote_copy(src_ref=my_half(i), dst_ref=half(i, me, c), send_sem=send_sems.at[i, k],
                                                  recv_sem=recv_sems.at[i, k], device_id=(tx, ty, c), device_id_type=MESH)
                cp.start()
                sends.append(cp)
        for i in range(n):
            for k, (tx, ty) in enumerate(chips):
                landed = half(i, 2 * tx + ty, c)
                pltpu.make_async_remote_copy(src_ref=my_half(i), dst_ref=landed, send_sem=send_sems.at[i, k],
                                             recv_sem=recv_sems.at[i, k], device_id=(tx, ty, c), device_id_type=MESH).wait_recv()
                cp = pltpu.make_async_remote_copy(src_ref=landed, dst_ref=landed, send_sem=pass_send_sems.at[i, k],
                                                  recv_sem=pass_recv_sems.at[i, k], device_id=(x, y, 1 - c), device_id_type=MESH)
                cp.start()
                sends.append(cp)
        for i in range(n):
            for k, (tx, ty) in enumerate(chips):
                missing = half(i, 2 * tx + ty, 1 - c)
                pltpu.make_async_remote_copy(src_ref=missing, dst_ref=missing, send_sem=pass_send_sems.at[i, k],
                                             recv_sem=pass_recv_sems.at[i, k], device_id=(x, y, 1 - c), device_id_type=MESH).wait_recv()
        for cp in sends:
            cp.wait_send()
        for cp in started:
            cp.wait()

    sems = pltpu.SemaphoreType.DMA((n, 3))
    return pl.pallas_call(
        body, name="gather_weight_shards",
        in_specs=[ANY] * n, out_specs=[ANY] * n,
        out_shape=[_sds((4,) + s.shape, s.dtype) for s in shards],
        scratch_shapes=[sems, sems, sems, sems, pltpu.SemaphoreType.DMA((n,))],
    )(*shards)


def exchange_halves_with_sibling(grads):
    n = len(grads)

    def body(*refs):
        ins, outs = refs[:n], refs[n:2 * n]
        send_sems, recv_sems = refs[2 * n:]
        x, y, c = _place()
        copies = []
        for i in range(n):
            half = ins[i].shape[1] // 2
            src = ins[i].at[:, pl.ds((1 - c) * half, half), :]
            cp = pltpu.make_async_remote_copy(src_ref=src, dst_ref=outs[i], send_sem=send_sems.at[i], recv_sem=recv_sems.at[i],
                                              device_id=(x, y, 1 - c), device_id_type=MESH)
            cp.start()
            copies.append(cp)
        for cp in copies:
            cp.wait()

    return pl.pallas_call(
        body, name="exchange_halves_with_sibling",
        in_specs=[ANY] * n, out_specs=[ANY] * n,
        out_shape=[_sds((4, g.shape[1] // 2, g.shape[2])) for g in grads],
        scratch_shapes=[pltpu.SemaphoreType.DMA((n,)), pltpu.SemaphoreType.DMA((n,))],
    )(*grads)


def scatter_blocks_to_chips(sums):
    n = len(sums)

    def body(*refs):
        ins, outs = refs[:n], refs[n:2 * n]
        send_sems, recv_sems, local_sems = refs[2 * n:]
        x, y, c = _place()
        me = 2 * x + y
        local = []
        for i in range(n):
            cp = pltpu.make_async_copy(ins[i].at[me], outs[i].at[me], local_sems.at[i])
            cp.start()
            local.append(cp)
        sends = []
        for i in range(n):
            for k, (tx, ty) in enumerate(_other_chips(x, y)):
                cp = pltpu.make_async_remote_copy(src_ref=ins[i].at[2 * tx + ty], dst_ref=outs[i].at[me], send_sem=send_sems.at[i, k],
                                                  recv_sem=recv_sems.at[i, k], device_id=(tx, ty, c), device_id_type=MESH)
                cp.start()
                sends.append(cp)
        for i in range(n):
            for k, (tx, ty) in enumerate(_other_chips(x, y)):
                pltpu.make_async_remote_copy(src_ref=ins[i].at[me], dst_ref=outs[i].at[2 * tx + ty], send_sem=send_sems.at[i, k],
                                             recv_sem=recv_sems.at[i, k], device_id=(tx, ty, c), device_id_type=MESH).wait_recv()
        for cp in sends:
            cp.wait_send()
        for cp in local:
            cp.wait()

    return pl.pallas_call(
        body, name="scatter_blocks_to_chips",
        in_specs=[ANY] * n, out_specs=[ANY] * n,
        out_shape=[_sds(s.shape, s.dtype) for s in sums],
        scratch_shapes=[pltpu.SemaphoreType.DMA((n, 3)), pltpu.SemaphoreType.DMA((n, 3)), pltpu.SemaphoreType.DMA((n,))],
    )(*sums)


def share_reduced_pieces(pieces, small_piece):
    n = len(pieces)

    def body(*refs):
        ins, small_in = refs[:n], refs[n]
        outs, small_out = refs[n + 1:2 * n + 1], refs[2 * n + 1]
        send_sems, recv_sems, local_sems, small_send, small_recv = refs[2 * n + 2:]
        x, y, c = _place()
        started = []
        for i in range(n):
            cp = pltpu.make_async_copy(ins[i], outs[i].at[c], local_sems.at[i])
            cp.start()
            started.append(cp)
        cp = pltpu.make_async_copy(small_in, small_out.at[4 * x + 2 * y + c], local_sems.at[n])
        cp.start()
        started.append(cp)
        swaps = []
        for i in range(n):
            cp = pltpu.make_async_remote_copy(src_ref=ins[i], dst_ref=outs[i].at[c], send_sem=send_sems.at[i], recv_sem=recv_sems.at[i],
                                              device_id=(x, y, 1 - c), device_id_type=MESH)
            cp.start()
            swaps.append(cp)
        sends = []
        for k in range(1, 8):
            tx, ty, tc = x ^ ((k >> 2) & 1), y ^ ((k >> 1) & 1), c ^ (k & 1)
            cp = pltpu.make_async_remote_copy(src_ref=small_in, dst_ref=small_out.at[4 * x + 2 * y + c], send_sem=small_send.at[k - 1],
                                              recv_sem=small_recv.at[k - 1], device_id=(tx, ty, tc), device_id_type=MESH)
            cp.start()
            sends.append(cp)
        for i in range(n):
            pltpu.make_async_remote_copy(src_ref=ins[i], dst_ref=outs[i].at[1 - c], send_sem=send_sems.at[i], recv_sem=recv_sems.at[i],
                                         device_id=(x, y, 1 - c), device_id_type=MESH).wait_recv()
        for k in range(1, 8):
            tx, ty, tc = x ^ ((k >> 2) & 1), y ^ ((k >> 1) & 1), c ^ (k & 1)
            pltpu.make_async_remote_copy(src_ref=small_in, dst_ref=small_out.at[4 * tx + 2 * ty + tc], send_sem=small_send.at[k - 1],
                                         recv_sem=small_recv.at[k - 1], device_id=(tx, ty, tc), device_id_type=MESH).wait_recv()
        for cp in swaps + sends:
            cp.wait_send()
        for cp in started:
            cp.wait()

    return pl.pallas_call(
        body, name="share_reduced_pieces",
        in_specs=[ANY] * (n + 1), out_specs=[ANY] * (n + 1),
        out_shape=[_sds((2,) + p.shape) for p in pieces] + [_sds((8,) + small_piece.shape)],
        scratch_shapes=[pltpu.SemaphoreType.DMA((n,)), pltpu.SemaphoreType.DMA((n,)), pltpu.SemaphoreType.DMA((n + 1,)),
                        pltpu.SemaphoreType.DMA((7,)), pltpu.SemaphoreType.DMA((7,))],
    )(*pieces, small_piece)


def _row_tile(rows, cols):
    tm = rows
    while tm * cols * 4 > (2 << 20) and tm % 16 == 0:
        tm //= 2
    return tm


def add_pair(a, b, out_dtype):
    nb, rows, cols = a.shape
    tm = _row_tile(rows, cols)

    def body(a_ref, b_ref, o_ref):
        o_ref[...] = (a_ref[...] + b_ref[...]).astype(out_dtype)

    spec = pl.BlockSpec((None, tm, cols), lambda j, i: (j, i, 0))
    return pl.pallas_call(
        body, name="add_pair", grid=(nb, rows // tm), in_specs=[spec, spec], out_specs=spec, out_shape=_sds(a.shape, out_dtype),
        compiler_params=_params(("parallel", "parallel")),
    )(a, b)


def sum_four(a):
    _, rows, cols = a.shape
    tm = _row_tile(rows, cols)

    def body(a_ref, o_ref):
        o_ref[...] = ((a_ref[0].astype(F32) + a_ref[1].astype(F32)) + a_ref[2].astype(F32)) + a_ref[3].astype(F32)

    return pl.pallas_call(
        body, name="sum_four", grid=(rows // tm,),
        in_specs=[pl.BlockSpec((4, tm, cols), lambda i: (0, i, 0))], out_specs=pl.BlockSpec((tm, cols), lambda i: (i, 0)),
        out_shape=_sds((rows, cols)), compiler_params=_params(("parallel",)),
    )(a)


def adamw(w, g, m, v):
    rows, cols = w.shape
    tm = _row_tile(rows, cols)
    c1 = 1.0 - ADAM_B1 ** ADAM_STEP
    c2 = 1.0 - ADAM_B2 ** ADAM_STEP

    def body(w_ref, g_ref, m_ref, v_ref, d_ref, nm_ref, nv_ref):
        g = g_ref[...]
        nm = ADAM_B1 * m_ref[...] + (1.0 - ADAM_B1) * g
        nv = ADAM_B2 * v_ref[...] + (1.0 - ADAM_B2) * (g * g)
        d_ref[...] = -ADAM_LR * ((nm / c1) / (jnp.sqrt(nv / c2) + ADAM_EPS) + ADAM_WD * w_ref[...])
        nm_ref[...] = nm
        nv_ref[...] = nv

    spec = pl.BlockSpec((tm, cols), lambda i: (i, 0))
    return pl.pallas_call(
        body, name="adamw", grid=(rows // tm,), in_specs=[spec] * 4, out_specs=[spec] * 3,
        out_shape=[_sds(w.shape)] * 3, compiler_params=_params(("parallel",)),
    )(w, g, m, v)


PACK_TILE = 8 * LANES
PACK_PIECES = 8
PACK_ALIGN = PACK_PIECES * 16


def _pack_small(values):
    parts = []
    for name in SMALL_NAMES:
        flat = values[name].reshape(-1)
        pad = (-flat.shape[0]) % PACK_TILE
        if pad:
            flat = jnp.concatenate([flat, jnp.zeros((pad,), F32)])
        parts.append(flat.reshape(-1, LANES))
    rows = sum(p.shape[0] for p in parts)
    pad = (-rows) % PACK_ALIGN
    if pad:
        parts.append(jnp.zeros((pad, LANES), F32))
    return jnp.concatenate(parts, axis=0)


def _unpack_small(pack, like):
    out = {}
    row = 0
    for name in SMALL_NAMES:
        size = math.prod(like[name].shape)
        rows = -(-size // PACK_TILE) * 8
        out[name] = pack[row:row + rows].reshape(-1)[:size].reshape(like[name].shape)
        row += rows
    return out


def _is_column_sharded(name):
    return name.endswith("w_in")


def _to_blocks(name, full):
    if _is_column_sharded(name):
        rows, cols = full.shape
        return full.reshape(rows, 4, cols // 4).transpose(1, 0, 2)
    return full.reshape(4, full.shape[0] // 4, full.shape[1])


def _from_blocks(name, stacked):
    if _is_column_sharded(name):
        return stacked.transpose(1, 0, 2).reshape(stacked.shape[1], 4 * stacked.shape[2])
    return stacked.reshape(4 * stacked.shape[1], stacked.shape[2])


def _train_step(x, loss_target, weights, moments_m, moments_v):
    c = lax.axis_index("c")
    gathered = gather_weight_shards([weights[n].astype(MXU_DTYPE) for n in BIG_NAMES])
    full = {n: _from_blocks(n, g) for n, g in zip(BIG_NAMES, gathered)}
    layers = []
    for i in range(4):
        names = SSM_NAMES if i % 2 == 0 else ATTN_NAMES
        w = {}
        for n in names:
            key = "l%d_%s" % (i, n)
            if key in full:
                w[n] = full[key]
                w[n + "_t"] = full[key].T
            else:
                w[n] = weights[key]
        layers.append(w)
    loss, dx, grads = _sequence_step(x[0], loss_target[0], layers, weights["final_norm"])
    loss = lax.psum(loss, ("x", "y", "c"))
    small_pack = _pack_small({n: grads[n] for n in SMALL_NAMES})
    blocks = [_to_blocks(n, grads[n]) for n in BIG_NAMES] + [small_pack.reshape(4, -1, LANES)]
    from_sibling = exchange_halves_with_sibling(blocks)
    chip_sums = []
    for i, (b, r) in enumerate(zip(blocks, from_sibling)):
        half = b.shape[1] // 2
        dtype = WIRE_DTYPE if i < len(BIG_NAMES) else F32
        chip_sums.append(add_pair(lax.dynamic_slice_in_dim(b, c * half, half, axis=1), r, dtype))
    contributions = scatter_blocks_to_chips(chip_sums)
    reduced = [sum_four(a) for a in contributions]
    shared = share_reduced_pieces(reduced[:-1], reduced[-1])
    big_grads = {n: s.reshape(2 * s.shape[1], s.shape[2]) for n, s in zip(BIG_NAMES, shared[:-1])}
    small_grad_pack = shared[-1].reshape(-1, LANES)
    out_grad, out_delta, out_m, out_v = {}, {}, {}, {}
    for n in BIG_NAMES:
        out_grad[n] = big_grads[n]
        out_delta[n], out_m[n], out_v[n] = adamw(weights[n], big_grads[n], moments_m[n], moments_v[n])
    small_like = {n: weights[n] for n in SMALL_NAMES}
    d_pack, m_pack, v_pack = adamw(_pack_small(small_like), small_grad_pack, _pack_small({n: moments_m[n] for n in SMALL_NAMES}),
                                   _pack_small({n: moments_v[n] for n in SMALL_NAMES}))
    out_grad.update(_unpack_small(small_grad_pack, small_like))
    out_delta.update(_unpack_small(d_pack, small_like))
    out_m.update(_unpack_small(m_pack, small_like))
    out_v.update(_unpack_small(v_pack, small_like))
    outs = [loss, dx[None]]
    for group in (out_grad, out_delta, out_m, out_v):
        outs.extend(group[n] for n in WEIGHT_NAMES)
    return tuple(outs)


def kernel(x, l0_norm, l0_w_in, l0_a_re, l0_a_im, l0_log_step, l0_b_re, l0_b_im, l0_c_re, l0_c_im, l0_d, l0_w_glu, l0_b_glu, l0_w_out, l1_norm, l1_w_in, l1_sinks, l1_w_out, l2_norm, l2_w_in, l2_a_re, l2_a_im, l2_log_step, l2_b_re, l2_b_im, l2_c_re, l2_c_im, l2_d, l2_w_glu, l2_b_glu, l2_w_out, l3_norm, l3_w_in, l3_sinks, l3_w_out, final_norm, loss_target, m_l0_norm, m_l0_w_in, m_l0_a_re, m_l0_a_im, m_l0_log_step, m_l0_b_re, m_l0_b_im, m_l0_c_re, m_l0_c_im, m_l0_d, m_l0_w_glu, m_l0_b_glu, m_l0_w_out, m_l1_norm, m_l1_w_in, m_l1_sinks, m_l1_w_out, m_l2_norm, m_l2_w_in, m_l2_a_re, m_l2_a_im, m_l2_log_step, m_l2_b_re, m_l2_b_im, m_l2_c_re, m_l2_c_im, m_l2_d, m_l2_w_glu, m_l2_b_glu, m_l2_w_out, m_l3_norm, m_l3_w_in, m_l3_sinks, m_l3_w_out, m_final_norm, v_l0_norm, v_l0_w_in, v_l0_a_re, v_l0_a_im, v_l0_log_step, v_l0_b_re, v_l0_b_im, v_l0_c_re, v_l0_c_im, v_l0_d, v_l0_w_glu, v_l0_b_glu, v_l0_w_out, v_l1_norm, v_l1_w_in, v_l1_sinks, v_l1_w_out, v_l2_norm, v_l2_w_in, v_l2_a_re, v_l2_a_im, v_l2_log_step, v_l2_b_re, v_l2_b_im, v_l2_c_re, v_l2_c_im, v_l2_d, v_l2_w_glu, v_l2_b_glu, v_l2_w_out, v_l3_norm, v_l3_w_in, v_l3_sinks, v_l3_w_out, v_final_norm):
    args = locals()
    weights = {n: args[n] for n in WEIGHT_NAMES}
    moments_m = {n: args["m_" + n] for n in WEIGHT_NAMES}
    moments_v = {n: args["v_" + n] for n in WEIGHT_NAMES}
    return _train_step(x, loss_target, weights, moments_m, moments_v)
```

```python
import functools
import math

import jax
import jax.numpy as jnp
from jax import lax
from jax.experimental import pallas as pl
from jax.experimental.pallas import tpu as pltpu

F32 = jnp.float32
MXU_DTYPE = jnp.bfloat16
WIRE_DTYPE = jnp.bfloat16
MESH = pl.DeviceIdType.MESH

D_MODEL = 1024
BRANCH = 1024
NORM_EPS = 1e-5
SSM_GROUPS = 64
SSM_GROUP = 16
SSM_STATE = 64
S5_CHUNK = 16
LANES = 128
S5_OCT = LANES // SSM_GROUP
S5_OCTETS = SSM_GROUPS // S5_OCT
S5_OCT_IN = S5_CHUNK * LANES
S5_OCT_STATE = S5_OCT * SSM_STATE
S5_STATES = SSM_GROUPS * SSM_STATE
HEAD_DIM = 64
N_Q_HEADS = 16
N_KV_HEADS = 2
GQA_GROUP = N_Q_HEADS // N_KV_HEADS
ATTN_BLOCK = 128
Q_DIM = N_Q_HEADS * HEAD_DIM
KV_DIM = N_KV_HEADS * HEAD_DIM
ROPE_THETA = 10000.0
NEG_INF = -1e30
ADAM_LR = 0.001
ADAM_B1 = 0.9
ADAM_B2 = 0.999
ADAM_EPS = 1e-08
ADAM_WD = 0.01
ADAM_STEP = 10

VMEM_LIMIT_V7X = 56 * 1024 * 1024
ROW_TILE_FWD = 512
ROW_TILE_BWD = 256

SSM_NAMES = ("norm", "w_in", "a_re", "a_im", "log_step", "b_re", "b_im", "c_re", "c_im", "d", "w_glu", "b_glu", "w_out")
ATTN_NAMES = ("norm", "w_in", "sinks", "w_out")


def _weight_names():
    names = []
    for i in range(4):
        for n in (SSM_NAMES if i % 2 == 0 else ATTN_NAMES):
            names.append("l%d_%s" % (i, n))
    names.append("final_norm")
    return names


WEIGHT_NAMES = _weight_names()
BIG_NAMES = [n for n in WEIGHT_NAMES if n.endswith(("w_in", "w_glu", "w_out"))]
SMALL_NAMES = [n for n in WEIGHT_NAMES if n not in BIG_NAMES]


def _params(semantics=None):
    return pltpu.CompilerParams(dimension_semantics=semantics, vmem_limit_bytes=VMEM_LIMIT_V7X)


def _rows(tm, n):
    return pl.BlockSpec((tm, n), lambda i: (i, 0))


def _whole(shape):
    return pl.BlockSpec(shape, lambda i: (0,) * len(shape))


def _sds(shape, dtype=F32):
    return jax.ShapeDtypeStruct(shape, dtype)


def _mm(a, b):
    return jnp.dot(a.astype(MXU_DTYPE), b.astype(MXU_DTYPE), preferred_element_type=F32)


def _mm_tn(a, b):
    return lax.dot_general(a.astype(MXU_DTYPE), b.astype(MXU_DTYPE), (((0,), (0,)), ((), ())), preferred_element_type=F32)


def _mm_nt(a, b):
    return lax.dot_general(a.astype(MXU_DTYPE), b.astype(MXU_DTYPE), (((1,), (1,)), ((), ())), preferred_element_type=F32)


def _sigmoid(x):
    return 1.0 / (1.0 + jnp.exp(-x))


def _silu(x):
    return x * _sigmoid(x)


def _silu_grad(x):
    s = _sigmoid(x)
    return s * (1.0 + x * (1.0 - s))


GELU_C0 = math.sqrt(2.0 / math.pi)
GELU_C1 = 0.044715


def _gelu(x):
    return 0.5 * x * (1.0 + jnp.tanh(GELU_C0 * (x + GELU_C1 * x * x * x)))


def _gelu_grad(x):
    th = jnp.tanh(GELU_C0 * (x + GELU_C1 * x * x * x))
    return 0.5 * (1.0 + th) + 0.5 * x * (1.0 - th * th) * GELU_C0 * (1.0 + 3.0 * GELU_C1 * x * x)


def _rms(x, g):
    r = lax.rsqrt(jnp.mean(x * x, axis=-1, keepdims=True) + NORM_EPS)
    xhat = x * r
    return r, xhat, xhat * g


def _rms_bwd(dh, g, r, xhat):
    dxhat = dh * g
    dx = r * (dxhat - xhat * jnp.mean(dxhat * xhat, axis=-1, keepdims=True))
    return dx, jnp.sum(dh * xhat, axis=0, keepdims=True)


def _swap_half_heads(x):
    n = x.shape[-1]
    lane = lax.broadcasted_iota(jnp.int32, x.shape, x.ndim - 1)
    first = (lane % HEAD_DIM) < (HEAD_DIM // 2)
    return jnp.where(first, pltpu.roll(x, n - HEAD_DIM // 2, x.ndim - 1), pltpu.roll(x, HEAD_DIM // 2, x.ndim - 1))


def _tile_lanes(t, reps):
    return jnp.concatenate([t] * reps, axis=1)


def ssm_proj_fwd(x, norm, w_in):
    t = x.shape[0]
    tm = min(ROW_TILE_FWD, t)

    def body(x_ref, g_ref, w_ref, u_ref, gate_ref):
        _, _, h = _rms(x_ref[...], g_ref[...])
        p = _mm(h, w_ref[...])
        u_ref[...] = p[:, :BRANCH]
        gate_ref[...] = p[:, BRANCH:]

    return pl.pallas_call(
        body, name="ssm_proj_fwd", grid=(t // tm,),
        in_specs=[_rows(tm, D_MODEL), _whole((1, D_MODEL)), _whole((D_MODEL, 2 * BRANCH))],
        out_specs=[_rows(tm, BRANCH), _rows(tm, BRANCH)],
        out_shape=[_sds((t, BRANCH)), _sds((t, BRANCH))],
        compiler_params=_params(("parallel",)),
    )(x, norm, w_in)


def _chunk_rows(ref, nk, dtype=None):
    rows = jnp.concatenate([ref[pl.ds(s, nk, stride=S5_CHUNK), :] for s in range(S5_CHUNK)], axis=1)
    return rows.astype(MXU_DTYPE if dtype is None else dtype)


def _store_chunk_rows(ref, val, nk):
    for s in range(S5_CHUNK):
        ref[pl.ds(s, nk, stride=S5_CHUNK), :] = val[:, s * LANES:(s + 1) * LANES]


def _own_group_mask():
    row = lax.broadcasted_iota(jnp.int32, (S5_OCT_IN, S5_OCT_STATE), 0)
    col = lax.broadcasted_iota(jnp.int32, (S5_OCT_IN, S5_OCT_STATE), 1)
    return ((row % LANES) // SSM_GROUP) == (col // SSM_STATE)


def _spread_groups(w):
    return jnp.where(_own_group_mask(), jnp.concatenate([w] * (S5_OCT_STATE // LANES), axis=1), 0.0).astype(MXU_DTYPE)


def _fold_groups(p):
    p = jnp.where(_own_group_mask(), p, 0.0)
    return sum(p[:, q * LANES:(q + 1) * LANES] for q in range(S5_OCT_STATE // LANES))


def _fill_toeplitz(win_ref, kd_ref):
    win_ref[...] = jnp.zeros_like(win_ref)
    for s in range(S5_CHUNK):
        for t in range(s, S5_CHUNK):
            win_ref[s * LANES:(s + 1) * LANES, t * LANES:(t + 1) * LANES] = kd_ref[t - s].astype(MXU_DTYPE)


def _strip(t):
    return pl.BlockSpec((t, LANES), lambda b: (0, b))


def _oct_states(nk):
    return pl.BlockSpec((nk, S5_OCT_STATE), lambda b: (0, b))


OCT_W = pl.BlockSpec((None, S5_OCT_IN, LANES), lambda b: (b, 0, 0))
OCT_KD = pl.BlockSpec((None, S5_CHUNK, LANES, LANES), lambda b: (b, 0, 0, 0))


def s5_chunk_states(u, ws_re, ws_im):
    t = u.shape[0]
    nk = t // S5_CHUNK

    def body(u_ref, wr_ref, wi_ref, re_ref, im_ref):
        uc = _chunk_rows(u_ref, nk)
        re_ref[...] = _mm(uc, _spread_groups(wr_ref[...]))
        im_ref[...] = _mm(uc, _spread_groups(wi_ref[...]))

    return pl.pallas_call(
        body, name="s5_chunk_states", grid=(S5_OCTETS,),
        in_specs=[_strip(t), OCT_W, OCT_W], out_specs=[_oct_states(nk), _oct_states(nk)],
        out_shape=[_sds((nk, S5_STATES)), _sds((nk, S5_STATES))],
        compiler_params=_params(("parallel",)),
    )(u, ws_re, ws_im)


def s5_scan_fwd(s_re, s_im, a_re, a_im):
    nk = s_re.shape[0]

    def body(sre_ref, sim_ref, ar_ref, ai_ref, hre_ref, him_ref):
        ar = ar_ref[...]
        ai = ai_ref[...]

        def step(k, carry):
            hr, hi = carry
            hre_ref[pl.ds(k, 1), :] = hr
            him_ref[pl.ds(k, 1), :] = hi
            sr = sre_ref[pl.ds(k, 1), :]
            si = sim_ref[pl.ds(k, 1), :]
            return ar * hr - ai * hi + sr, ai * hr + ar * hi + si

        zero = jnp.zeros((1, S5_STATES), F32)
        lax.fori_loop(0, nk, step, (zero, zero))

    vm = pl.BlockSpec(memory_space=pltpu.VMEM)
    return pl.pallas_call(
        body, name="s5_scan_fwd", in_specs=[vm, vm, vm, vm], out_specs=[vm, vm],
        out_shape=[_sds((nk, S5_STATES)), _sds((nk, S5_STATES))],
        compiler_params=_params(),
    )(s_re, s_im, a_re, a_im)


def s5_outputs(u, h_re, h_im, kd, wo_re, wo_im):
    t = u.shape[0]
    nk = t // S5_CHUNK

    def body(u_ref, hre_ref, him_ref, kd_ref, wor_ref, woi_ref, y_ref, win_ref):
        _fill_toeplitz(win_ref, kd_ref)
        y = _mm(_chunk_rows(u_ref, nk), win_ref[...])
        y = y + _mm_nt(hre_ref[...], _spread_groups(wor_ref[...])) + _mm_nt(him_ref[...], _spread_groups(woi_ref[...]))
        _store_chunk_rows(y_ref, y, nk)

    return pl.pallas_call(
        body, name="s5_outputs", grid=(S5_OCTETS,),
        in_specs=[_strip(t), _oct_states(nk), _oct_states(nk), OCT_KD, OCT_W, OCT_W],
        out_specs=_strip(t), out_shape=_sds((t, BRANCH)),
        scratch_shapes=[pltpu.VMEM((S5_OCT_IN, S5_OCT_IN), MXU_DTYPE)],
        compiler_params=_params(("parallel",)),
    )(u, h_re, h_im, kd, wo_re, wo_im)


def s5_state_grads(dy, wo_re, wo_im):
    t = dy.shape[0]
    nk = t // S5_CHUNK

    def body(dy_ref, wor_ref, woi_ref, re_ref, im_ref):
        dyc = _chunk_rows(dy_ref, nk)
        re_ref[...] = _mm(dyc, _spread_groups(wor_ref[...]))
        im_ref[...] = _mm(dyc, _spread_groups(woi_ref[...]))

    return pl.pallas_call(
        body, name="s5_state_grads", grid=(S5_OCTETS,),
        in_specs=[_strip(t), OCT_W, OCT_W], out_specs=[_oct_states(nk), _oct_states(nk)],
        out_shape=[_sds((nk, S5_STATES)), _sds((nk, S5_STATES))],
        compiler_params=_params(("parallel",)),
    )(dy, wo_re, wo_im)


def s5_scan_bwd(dh_re, dh_im, h_re, h_im, a_re, a_im):
    nk = dh_re.shape[0]

    def body(dhr_ref, dhi_ref, hr_ref, hi_ref, ar_ref, ai_ref, dsr_ref, dsi_ref, dar_ref, dai_ref):
        ar = ar_ref[...]
        ai = ai_ref[...]

        dar_ref[...] = jnp.zeros_like(dar_ref)
        dai_ref[...] = jnp.zeros_like(dai_ref)

        def step(i, carry):
            gr, gi = carry
            k = nk - 1 - i
            dhr = dhr_ref[pl.ds(k, 1), :]
            dhi = dhi_ref[pl.ds(k, 1), :]
            dsr_ref[pl.ds(k, 1), :] = gr
            dsi_ref[pl.ds(k, 1), :] = gi
            hr = hr_ref[pl.ds(k, 1), :]
            hi = hi_ref[pl.ds(k, 1), :]
            dar_ref[...] += gr * hr + gi * hi
            dai_ref[...] += gi * hr - gr * hi
            return dhr + ar * gr + ai * gi, dhi - ai * gr + ar * gi

        zero = jnp.zeros((1, S5_STATES), F32)
        lax.fori_loop(0, nk, step, (zero, zero))

    vm = pl.BlockSpec(memory_space=pltpu.VMEM)
    return pl.pallas_call(
        body, name="s5_scan_bwd", in_specs=[vm] * 6, out_specs=[vm] * 4,
        out_shape=[_sds((nk, S5_STATES)), _sds((nk, S5_STATES)), _sds((1, S5_STATES)), _sds((1, S5_STATES))],
        input_output_aliases={0: 0, 1: 1}, compiler_params=_params(),
    )(dh_re, dh_im, h_re, h_im, a_re, a_im)


def s5_input_grads(dy, ds_re, ds_im, kd, ws_re, ws_im):
    t = dy.shape[0]
    nk = t // S5_CHUNK

    def body(dy_ref, dsr_ref, dsi_ref, kd_ref, wsr_ref, wsi_ref, du_ref, win_ref):
        _fill_toeplitz(win_ref, kd_ref)
        du = _mm_nt(_chunk_rows(dy_ref, nk), win_ref[...])
        du = du + _mm_nt(dsr_ref[...], _spread_groups(wsr_ref[...])) + _mm_nt(dsi_ref[...], _spread_groups(wsi_ref[...]))
        _store_chunk_rows(du_ref, du, nk)

    return pl.pallas_call(
        body, name="s5_input_grads", grid=(S5_OCTETS,),
        in_specs=[_strip(t), _oct_states(nk), _oct_states(nk), OCT_KD, OCT_W, OCT_W],
        out_specs=_strip(t), out_shape=_sds((t, BRANCH)),
        scratch_shapes=[pltpu.VMEM((S5_OCT_IN, S5_OCT_IN), MXU_DTYPE)],
        compiler_params=_params(("parallel",)),
    )(dy, ds_re, ds_im, kd, ws_re, ws_im)


def s5_weight_grads(u, dy, h_re, h_im, ds_re, ds_im):
    t = u.shape[0]
    nk = t // S5_CHUNK

    def body(u_ref, dy_ref, hre_ref, him_ref, dsr_ref, dsi_ref, dkd_ref, dwsr_ref, dwsi_ref, dwor_ref, dwoi_ref):
        dyc = _chunk_rows(dy_ref, nk, F32)
        uct = _chunk_rows(u_ref, nk, F32).T.astype(MXU_DTYPE)
        dyct = dyc.T.astype(MXU_DTYPE)
        dyc = dyc.astype(MXU_DTYPE)
        dwsr_ref[...] = _fold_groups(_mm(uct, dsr_ref[...]))
        dwsi_ref[...] = _fold_groups(_mm(uct, dsi_ref[...]))
        dwor_ref[...] = _fold_groups(_mm(dyct, hre_ref[...]))
        dwoi_ref[...] = _fold_groups(_mm(dyct, him_ref[...]))
        dkd_ref[...] = jnp.zeros_like(dkd_ref)
        for tt in range(S5_CHUNK):
            p = _mm(uct, dyc[:, tt * LANES:(tt + 1) * LANES])
            for s in range(tt + 1):
                dkd_ref[tt - s] += p[s * LANES:(s + 1) * LANES]

    return pl.pallas_call(
        body, name="s5_weight_grads", grid=(S5_OCTETS,),
        in_specs=[_strip(t), _strip(t)] + [_oct_states(nk)] * 4,
        out_specs=[OCT_KD, OCT_W, OCT_W, OCT_W, OCT_W],
        out_shape=[_sds((S5_OCTETS, S5_CHUNK, LANES, LANES))] + [_sds((S5_OCTETS, S5_OCT_IN, LANES))] * 4,
        compiler_params=_params(("parallel",)),
    )(u, dy, h_re, h_im, ds_re, ds_im)


def ssm_mix_fwd(x, u, gate, y_scan, d, w_glu, b_glu, w_out):
    t = x.shape[0]
    tm = min(ROW_TILE_FWD, t)

    def body(x_ref, u_ref, gate_ref, ys_ref, d_ref, wg_ref, bg_ref, wo_ref, y_ref, g2_ref, xo_ref):
        y = ys_ref[...] + d_ref[...] * u_ref[...]
        z0 = _gelu(y)
        g2 = _mm(z0, wg_ref[...]) + bg_ref[...]
        a = z0 * _sigmoid(g2) * _silu(gate_ref[...])
        y_ref[...] = y
        g2_ref[...] = g2
        xo_ref[...] = x_ref[...] + _mm(a, wo_ref[...])

    row = _rows(tm, BRANCH)
    vec = _whole((1, BRANCH))
    mat = _whole((BRANCH, BRANCH))
    return pl.pallas_call(
        body, name="ssm_mix_fwd", grid=(t // tm,),
        in_specs=[row, row, row, row, vec, mat, vec, mat],
        out_specs=[row, row, row],
        out_shape=[_sds((t, BRANCH))] * 3,
        compiler_params=_params(("parallel",)),
    )(x, u, gate, y_scan, d, w_glu, b_glu, w_out)


def ssm_mix_bwd(dxo, u, gate, y, g2, w_glu, w_out):
    t = dxo.shape[0]
    tm = min(ROW_TILE_BWD, t)

    def body(dxo_ref, u_ref, gate_ref, y_ref, g2_ref, wgt_ref, wot_ref, dy_ref, dgate_ref, dwo_ref, dwg_ref, dbg_ref, dd_ref):
        @pl.when(pl.program_id(0) == 0)
        def _():
            dwo_ref[...] = jnp.zeros_like(dwo_ref)
            dwg_ref[...] = jnp.zeros_like(dwg_ref)
            dbg_ref[...] = jnp.zeros_like(dbg_ref)
            dd_ref[...] = jnp.zeros_like(dd_ref)

        dxo = dxo_ref[...]
        gate = gate_ref[...]
        y = y_ref[...]
        z0 = _gelu(y)
        sg = _sigmoid(g2_ref[...])
        z = z0 * sg
        sgate = _silu(gate)
        da = _mm_nt(dxo, wot_ref[...])
        dwo_ref[...] += _mm_tn(z * sgate, dxo)
        dz = da * sgate
        dgate_ref[...] = da * z * _silu_grad(gate)
        dg2 = dz * z0 * sg * (1.0 - sg)
        dbg_ref[...] += jnp.sum(dg2, axis=0, keepdims=True)
        dwg_ref[...] += _mm_tn(z0, dg2)
        dz0 = dz * sg + _mm_nt(dg2, wgt_ref[...])
        dy = dz0 * _gelu_grad(y)
        dd_ref[...] += jnp.sum(dy * u_ref[...], axis=0, keepdims=True)
        dy_ref[...] = dy

    row = _rows(tm, BRANCH)
    vec = _whole((1, BRANCH))
    mat = _whole((BRANCH, BRANCH))
    return pl.pallas_call(
        body, name="ssm_mix_bwd", grid=(t // tm,),
        in_specs=[row, row, row, row, row, mat, mat],
        out_specs=[row, row, mat, mat, vec, vec],
        out_shape=[_sds((t, BRANCH)), _sds((t, BRANCH)), _sds((BRANCH, D_MODEL)), _sds((BRANCH, BRANCH)),
                   _sds((1, BRANCH)), _sds((1, BRANCH))],
        compiler_params=_params(("arbitrary",)),
    )(dxo, u, gate, y, g2, w_glu, w_out)


def ssm_proj_bwd(x, norm, dxo, dy, du_scan, dgate, d, w_in):
    t = x.shape[0]
    tm = min(ROW_TILE_BWD, t)
    n = 2 * BRANCH

    def body(x_ref, g_ref, dxo_ref, dy_ref, dus_ref, dgate_ref, d_ref, wt_ref, dx_ref, dw_ref, dg_ref):
        @pl.when(pl.program_id(0) == 0)
        def _():
            dw_ref[...] = jnp.zeros_like(dw_ref)
            dg_ref[...] = jnp.zeros_like(dg_ref)

        g = g_ref[...]
        r, xhat, h = _rms(x_ref[...], g)
        du = dus_ref[...] + d_ref[...] * dy_ref[...]
        dproj = jnp.concatenate([du, dgate_ref[...]], axis=1)
        dh = _mm_nt(dproj, wt_ref[...])
        dw_ref[...] += _mm_tn(h, dproj)
        dx, dg = _rms_bwd(dh, g, r, xhat)
        dg_ref[...] += dg
        dx_ref[...] = dxo_ref[...] + dx

    row = _rows(tm, D_MODEL)
    vec = _whole((1, D_MODEL))
    return pl.pallas_call(
        body, name="ssm_proj_bwd", grid=(t // tm,),
        in_specs=[row, vec, row, row, row, row, vec, _whole((D_MODEL, n))],
        out_specs=[row, _whole((D_MODEL, n)), vec],
        out_shape=[_sds((t, D_MODEL)), _sds((D_MODEL, n)), _sds((1, D_MODEL))],
        compiler_params=_params(("arbitrary",)),
    )(x, norm, dxo, dy, du_scan, dgate, d, w_in)


ATTN_N = Q_DIM + 2 * KV_DIM + BRANCH


def attn_proj_fwd(x, norm, w_in, cos2, sin2):
    t = x.shape[0]
    tm = min(ROW_TILE_FWD, t)

    def body(x_ref, g_ref, w_ref, cos_ref, sin_ref, q_ref, k_ref, v_ref, gate_ref):
        _, _, h = _rms(x_ref[...], g_ref[...])
        p = _mm(h, w_ref[...])
        cs = cos_ref[...]
        sn = sin_ref[...]
        q = p[:, :Q_DIM]
        k = p[:, Q_DIM:Q_DIM + KV_DIM]
        q_ref[...] = q * _tile_lanes(cs, Q_DIM // LANES) + _swap_half_heads(q) * _tile_lanes(sn, Q_DIM // LANES)
        k_ref[...] = k * cs + _swap_half_heads(k) * sn
        v_ref[...] = p[:, Q_DIM + KV_DIM:Q_DIM + 2 * KV_DIM]
        gate_ref[...] = p[:, Q_DIM + 2 * KV_DIM:]

    return pl.pallas_call(
        body, name="attn_proj_fwd", grid=(t // tm,),
        in_specs=[_rows(tm, D_MODEL), _whole((1, D_MODEL)), _whole((D_MODEL, ATTN_N)), _rows(tm, LANES), _rows(tm, LANES)],
        out_specs=[_rows(tm, Q_DIM), _rows(tm, KV_DIM), _rows(tm, KV_DIM), _rows(tm, BRANCH)],
        out_shape=[_sds((t, Q_DIM)), _sds((t, KV_DIM)), _sds((t, KV_DIM)), _sds((t, BRANCH))],
        compiler_params=_params(("parallel",)),
    )(x, norm, w_in, cos2, sin2)


def _band_mask(first_block):
    qi = lax.broadcasted_iota(jnp.int32, (ATTN_BLOCK, 2 * ATTN_BLOCK), 0)
    kj = lax.broadcasted_iota(jnp.int32, (ATTN_BLOCK, 2 * ATTN_BLOCK), 1)
    dist = qi + ATTN_BLOCK - kj
    first_key = jnp.where(first_block, ATTN_BLOCK, 0)
    return (dist >= 0) & (dist < ATTN_BLOCK) & (kj >= first_key)


def _lane_is(h):
    return lax.broadcasted_iota(jnp.int32, (1, LANES), 1) == h


def attn_fwd(q, k, v, sinks):
    t = q.shape[0]
    nb = t // ATTN_BLOCK
    scale = HEAD_DIM ** -0.5

    def body(sink_ref, q_ref, kc_ref, kp_ref, vc_ref, vp_ref, o_ref, lse_ref):
        i = pl.program_id(0)
        keys = jnp.concatenate([kp_ref[...], kc_ref[...]], axis=0).astype(MXU_DTYPE)
        vals = jnp.concatenate([vp_ref[...], vc_ref[...]], axis=0).astype(MXU_DTYPE)
        valid = _band_mask(i == 0)
        lse = jnp.zeros((ATTN_BLOCK, LANES), F32)
        for h in range(N_Q_HEADS):
            hk = h // GQA_GROUP
            kh = keys[:, hk * HEAD_DIM:(hk + 1) * HEAD_DIM]
            vh = vals[:, hk * HEAD_DIM:(hk + 1) * HEAD_DIM]
            qh = q_ref[:, h * HEAD_DIM:(h + 1) * HEAD_DIM]
            s = jnp.where(valid, _mm_nt(qh, kh) * scale, NEG_INF)
            sink = sink_ref[h]
            m = jnp.maximum(jnp.max(s, axis=-1, keepdims=True), sink)
            p = jnp.exp(s - m)
            den = jnp.sum(p, axis=-1, keepdims=True) + jnp.exp(sink - m)
            o_ref[:, h * HEAD_DIM:(h + 1) * HEAD_DIM] = _mm(p, vh) / den
            lse = jnp.where(_lane_is(h), m + jnp.log(den), lse)
        lse_ref[...] = lse

    cur = lambda n: pl.BlockSpec((ATTN_BLOCK, n), lambda i: (i, 0))
    prev = lambda n: pl.BlockSpec((ATTN_BLOCK, n), lambda i: (jnp.maximum(i - 1, 0), 0))
    return pl.pallas_call(
        body, name="attn_fwd", grid=(nb,),
        in_specs=[pl.BlockSpec(memory_space=pltpu.SMEM), cur(Q_DIM), cur(KV_DIM), prev(KV_DIM), cur(KV_DIM), prev(KV_DIM)],
        out_specs=[cur(Q_DIM), cur(LANES)],
        out_shape=[_sds((t, Q_DIM)), _sds((t, LANES))],
        compiler_params=_params(("parallel",)),
    )(sinks, q, k, k, v, v)


def attn_bwd(q, k, v, sinks, o, lse, do):
    t = q.shape[0]
    nb = t // ATTN_BLOCK
    scale = HEAD_DIM ** -0.5

    def body(sink_ref, q_ref, kc_ref, kp_ref, vc_ref, vp_ref, o_ref, lse_ref, do_ref,
             dq_ref, dk_ref, dv_ref, dsink_ref, dk_carry, dv_carry):
        i = pl.program_id(0)

        @pl.when(i == 0)
        def _():
            dsink_ref[...] = jnp.zeros_like(dsink_ref)
            dk_carry[...] = jnp.zeros_like(dk_carry)
            dv_carry[...] = jnp.zeros_like(dv_carry)

        @pl.when(i < nb)
        def _():
            keys = jnp.concatenate([kp_ref[...], kc_ref[...]], axis=0).astype(MXU_DTYPE)
            vals = jnp.concatenate([vp_ref[...], vc_ref[...]], axis=0).astype(MXU_DTYPE)
            valid = _band_mask(i == 0)
            lse_all = lse_ref[...]
            dsink = jnp.zeros((1, LANES), F32)
            dk_heads = []
            dv_heads = []
            for hk in range(N_KV_HEADS):
                kh = keys[:, hk * HEAD_DIM:(hk + 1) * HEAD_DIM]
                vh = vals[:, hk * HEAD_DIM:(hk + 1) * HEAD_DIM]
                dkk = jnp.zeros((2 * ATTN_BLOCK, HEAD_DIM), F32)
                dvv = jnp.zeros((2 * ATTN_BLOCK, HEAD_DIM), F32)
                for hq in range(GQA_GROUP):
                    h = hk * GQA_GROUP + hq
                    sl = slice(h * HEAD_DIM, (h + 1) * HEAD_DIM)
                    qh = q_ref[:, sl]
                    doh = do_ref[:, sl]
                    lse_h = jnp.sum(jnp.where(_lane_is(h), lse_all, 0.0), axis=-1, keepdims=True)
                    s = jnp.where(valid, _mm_nt(qh, kh) * scale, NEG_INF)
                    p = jnp.exp(s - lse_h)
                    delta = jnp.sum(doh * o_ref[:, sl], axis=-1, keepdims=True)
                    dvv = dvv + _mm_tn(p, doh)
                    ds = p * (_mm_nt(doh, vh) - delta)
                    dq_ref[:, sl] = _mm(ds, kh) * scale
                    dkk = dkk + _mm_tn(ds, qh) * scale
                    dsink = dsink + jnp.where(_lane_is(h), -jnp.sum(jnp.exp(sink_ref[h] - lse_h) * delta), 0.0)
                dk_heads.append(dkk)
                dv_heads.append(dvv)
            dkk = jnp.concatenate(dk_heads, axis=1)
            dvv = jnp.concatenate(dv_heads, axis=1)
            dsink_ref[...] += dsink
            dk_ref[...] = dk_carry[...] + dkk[:ATTN_BLOCK]
            dv_ref[...] = dv_carry[...] + dvv[:ATTN_BLOCK]
            dk_carry[...] = dkk[ATTN_BLOCK:]
            dv_carry[...] = dvv[ATTN_BLOCK:]

        @pl.when(i == nb)
        def _():
            dk_ref[...] = dk_carry[...]
            dv_ref[...] = dv_carry[...]

    last = nb - 1
    cur = lambda n: pl.BlockSpec((ATTN_BLOCK, n), lambda i: (jnp.minimum(i, last), 0))
    prev = lambda n: pl.BlockSpec((ATTN_BLOCK, n), lambda i: (jnp.clip(i - 1, 0, last), 0))
    late = lambda n: pl.BlockSpec((ATTN_BLOCK, n), lambda i: (i, 0))
    dq, dk_late, dv_late, dsinks = pl.pallas_call(
        body, name="attn_bwd", grid=(nb + 1,),
        in_specs=[pl.BlockSpec(memory_space=pltpu.SMEM), cur(Q_DIM), cur(KV_DIM), prev(KV_DIM), cur(KV_DIM), prev(KV_DIM),
                  cur(Q_DIM), cur(LANES), cur(Q_DIM)],
        out_specs=[cur(Q_DIM), late(KV_DIM), late(KV_DIM), _whole((1, LANES))],
        out_shape=[_sds((t, Q_DIM)), _sds((t + ATTN_BLOCK, KV_DIM)), _sds((t + ATTN_BLOCK, KV_DIM)), _sds((1, LANES))],
        scratch_shapes=[pltpu.VMEM((ATTN_BLOCK, KV_DIM), F32), pltpu.VMEM((ATTN_BLOCK, KV_DIM), F32)],
        compiler_params=_params(("arbitrary",)),
    )(sinks, q, k, k, v, v, o, lse, do)
    return dq, dk_late[ATTN_BLOCK:], dv_late[ATTN_BLOCK:], dsinks


def attn_out_fwd(x, o, gate, w_out):
    t = x.shape[0]
    tm = min(ROW_TILE_FWD, t)

    def body(x_ref, o_ref, gate_ref, w_ref, xo_ref):
        xo_ref[...] = x_ref[...] + _mm(o_ref[...] * _silu(gate_ref[...]), w_ref[...])

    row = _rows(tm, D_MODEL)
    return pl.pallas_call(
        body, name="attn_out_fwd", grid=(t // tm,),
        in_specs=[row, row, row, _whole((Q_DIM, D_MODEL))], out_specs=row, out_shape=_sds((t, D_MODEL)),
        compiler_params=_params(("parallel",)),
    )(x, o, gate, w_out)


def attn_out_bwd(dxo, o, gate, w_out):
    t = dxo.shape[0]
    tm = min(ROW_TILE_BWD, t)

    def body(dxo_ref, o_ref, gate_ref, wt_ref, do_ref, dgate_ref, dw_ref):
        @pl.when(pl.program_id(0) == 0)
        def _():
            dw_ref[...] = jnp.zeros_like(dw_ref)

        dxo = dxo_ref[...]
        o = o_ref[...]
        gate = gate_ref[...]
        sgate = _silu(gate)
        da = _mm_nt(dxo, wt_ref[...])
        dw_ref[...] += _mm_tn(o * sgate, dxo)
        do_ref[...] = da * sgate
        dgate_ref[...] = da * o * _silu_grad(gate)

    row = _rows(tm, D_MODEL)
    mat = _whole((Q_DIM, D_MODEL))
    return pl.pallas_call(
        body, name="attn_out_bwd", grid=(t // tm,),
        in_specs=[row, row, row, mat], out_specs=[row, row, mat],
        out_shape=[_sds((t, Q_DIM)), _sds((t, BRANCH)), _sds((Q_DIM, D_MODEL))],
        compiler_params=_params(("arbitrary",)),
    )(dxo, o, gate, w_out)


def attn_proj_bwd(x, norm, dxo, dq, dk, dv, dgate, cos2, sin2, w_in):
    t = x.shape[0]
    tm = min(ROW_TILE_BWD, t)

    def body(x_ref, g_ref, dxo_ref, dq_ref, dk_ref, dv_ref, dgate_ref, cos_ref, sin_ref, wt_ref, dx_ref, dw_ref, dg_ref):
        @pl.when(pl.program_id(0) == 0)
        def _():
            dw_ref[...] = jnp.zeros_like(dw_ref)
            dg_ref[...] = jnp.zeros_like(dg_ref)

        g = g_ref[...]
        r, xhat, h = _rms(x_ref[...], g)
        cs = cos_ref[...]
        sn = sin_ref[...]
        dqr = dq_ref[...]
        dkr = dk_ref[...]
        dq = dqr * _tile_lanes(cs, Q_DIM // LANES) + _swap_half_heads(dqr * _tile_lanes(sn, Q_DIM // LANES))
        dk = dkr * cs + _swap_half_heads(dkr * sn)
        dproj = jnp.concatenate([dq, dk, dv_ref[...], dgate_ref[...]], axis=1)
        dh = _mm_nt(dproj, wt_ref[...])
        dw_ref[...] += _mm_tn(h, dproj)
        dx, dg = _rms_bwd(dh, g, r, xhat)
        dg_ref[...] += dg
        dx_ref[...] = dxo_ref[...] + dx

    row = _rows(tm, D_MODEL)
    vec = _whole((1, D_MODEL))
    return pl.pallas_call(
        body, name="attn_proj_bwd", grid=(t // tm,),
        in_specs=[row, vec, row, _rows(tm, Q_DIM), _rows(tm, KV_DIM), _rows(tm, KV_DIM), _rows(tm, BRANCH),
                  _rows(tm, LANES), _rows(tm, LANES), _whole((D_MODEL, ATTN_N))],
        out_specs=[row, _whole((D_MODEL, ATTN_N)), vec],
        out_shape=[_sds((t, D_MODEL)), _sds((D_MODEL, ATTN_N)), _sds((1, D_MODEL))],
        compiler_params=_params(("arbitrary",)),
    )(x, norm, dxo, dq, dk, dv, dgate, cos2, sin2, w_in)


def loss_head(x, norm, target):
    t = x.shape[0]
    tm = min(ROW_TILE_FWD, t)

    def body(x_ref, g_ref, tgt_ref, loss_ref, dx_ref, dg_ref):
        @pl.when(pl.program_id(0) == 0)
        def _():
            loss_ref[...] = jnp.zeros_like(loss_ref)
            dg_ref[...] = jnp.zeros_like(dg_ref)

        g = g_ref[...]
        r, xhat, y = _rms(x_ref[...], g)
        err = y - tgt_ref[...]
        loss_ref[...] += 0.5 * jnp.sum(jnp.mean(err * err, axis=-1, keepdims=True), axis=0, keepdims=True)
        dx, dg = _rms_bwd(err * (1.0 / D_MODEL), g, r, xhat)
        dg_ref[...] += dg
        dx_ref[...] = dx

    row = _rows(tm, D_MODEL)
    vec = _whole((1, D_MODEL))
    return pl.pallas_call(
        body, name="loss_head", grid=(t // tm,),
        in_specs=[row, vec, row], out_specs=[_whole((1, 1)), row, vec],
        out_shape=[_sds((1, 1)), _sds((t, D_MODEL)), _sds((1, D_MODEL))],
        compiler_params=_params(("arbitrary",)),
    )(x, norm, target)


OCT_TILE = pl.BlockSpec((None, LANES, LANES), lambda b: (b, 0, 0))
N_LAGS = S5_CHUNK + 1


def _cmul(ar, ai, br, bi):
    return ar * br - ai * bi, ar * bi + ai * br


def _cmul_conj(ar, ai, br, bi):
    return ar * br + ai * bi, ar * bi - ai * br


def _mm_f32(a, b, dims):
    return lax.dot_general(a, b, (dims, ((), ())), precision=lax.Precision.HIGHEST, preferred_element_type=F32)


def _s5_discretise(ar, ai, ls, br, bi):
    dt = jnp.exp(ls)
    xr = ar * dt
    xi = ai * dt
    mag = jnp.exp(xr)
    first = (mag * jnp.cos(xi), mag * jnp.sin(xi))
    powers = [(jnp.ones_like(xr), jnp.zeros_like(xr)), first]
    for _ in range(2, N_LAGS):
        powers.append(_cmul(*powers[-1], *first))
    den = ar * ar + ai * ai
    nr = powers[1][0] - 1.0
    ni = powers[1][1]
    fr = (nr * ar + ni * ai) / den
    fi = (ni * ar - nr * ai) / den
    bbr, bbi = _cmul(fr, fi, br, bi)
    return dt, powers, (fr, fi), (bbr, bbi), den


def _same_group_tile():
    row = lax.broadcasted_iota(jnp.int32, (LANES, LANES), 0)
    col = lax.broadcasted_iota(jnp.int32, (LANES, LANES), 1)
    return (row // SSM_GROUP) == (col // SSM_GROUP)


def _first_copy_lanes():
    return lax.broadcasted_iota(jnp.int32, (LANES, LANES), 1) < SSM_STATE


def s5_param_fwd(tiles):
    def body(ar_ref, ai_ref, ls_ref, br_ref, bi_ref, cr_ref, ci_ref, kd_ref, wsr_ref, wsi_ref, wor_ref, woi_ref, pr_ref, pi_ref):
        cr = cr_ref[...]
        ci = ci_ref[...]
        _, powers, _, (bbr, bbi), _ = _s5_discretise(ar_ref[...], ai_ref[...], ls_ref[...], br_ref[...], bi_ref[...])
        once = _first_copy_lanes()
        crm = jnp.where(once, cr, 0.0)
        cim = jnp.where(once, ci, 0.0)
        same = _same_group_tile()
        for lag in range(S5_CHUNK):
            er, ei = powers[lag]
            xr, xi = _cmul(er, ei, bbr, bbi)
            rows = pl.ds((S5_CHUNK - 1 - lag) * LANES, LANES)
            wsr_ref[rows, :] = xr
            wsi_ref[rows, :] = xi
            k = _mm_f32(xr, crm, ((1,), (1,))) - _mm_f32(xi, cim, ((1,), (1,)))
            kd_ref[lag] = jnp.where(same, k, 0.0)
        for t in range(S5_CHUNK):
            er, ei = powers[t + 1]
            zr, zi = _cmul(er, ei, cr, ci)
            wor_ref[pl.ds(t * LANES, LANES), :] = zr
            woi_ref[pl.ds(t * LANES, LANES), :] = -zi
        pr_ref[...] = powers[S5_CHUNK][0]
        pi_ref[...] = powers[S5_CHUNK][1]

    return pl.pallas_call(
        body, name="s5_param_fwd", grid=(S5_OCTETS,),
        in_specs=[OCT_TILE] * 7, out_specs=[OCT_KD, OCT_W, OCT_W, OCT_W, OCT_W, OCT_TILE, OCT_TILE],
        out_shape=[_sds((S5_OCTETS, S5_CHUNK, LANES, LANES))] + [_sds((S5_OCTETS, S5_OCT_IN, LANES))] * 4
                  + [_sds((S5_OCTETS, LANES, LANES))] * 2,
        compiler_params=_params(("parallel",)),
    )(*tiles)


def s5_param_bwd(tiles, dkd, dws_re, dws_im, dwo_re, dwo_im, dp_re, dp_im):
    def body(ar_ref, ai_ref, ls_ref, br_ref, bi_ref, cr_ref, ci_ref, dkd_ref, dwsr_ref, dwsi_ref, dwor_ref, dwoi_ref, dpr_ref, dpi_ref,
             dar_ref, dai_ref, dls_ref, dbr_ref, dbi_ref, dcr_ref, dci_ref):
        ar = ar_ref[...]
        ai = ai_ref[...]
        br = br_ref[...]
        bi = bi_ref[...]
        cr = cr_ref[...]
        ci = ci_ref[...]
        dt, powers, (fr, fi), (bbr, bbi), den = _s5_discretise(ar, ai, ls_ref[...], br, bi)
        once = _first_copy_lanes()
        crm = jnp.where(once, cr, 0.0)
        cim = jnp.where(once, ci, 0.0)
        same = _same_group_tile()
        zero = jnp.zeros((LANES, LANES), F32)
        dpow = [[zero, zero] for _ in range(N_LAGS)]
        dbbr, dbbi, dcr, dci = zero, zero, zero, zero
        for lag in range(S5_CHUNK):
            er, ei = powers[lag]
            xr, xi = _cmul(er, ei, bbr, bbi)
            rows = pl.ds((S5_CHUNK - 1 - lag) * LANES, LANES)
            g = jnp.where(same, dkd_ref[lag], 0.0)
            dxr = dwsr_ref[rows, :] + _mm_f32(g, crm, ((1,), (0,)))
            dxi = dwsi_ref[rows, :] - _mm_f32(g, cim, ((1,), (0,)))
            dcr = dcr + jnp.where(once, _mm_f32(g, xr, ((0,), (0,))), 0.0)
            dci = dci - jnp.where(once, _mm_f32(g, xi, ((0,), (0,))), 0.0)
            a, b = _cmul_conj(bbr, bbi, dxr, dxi)
            dpow[lag][0] = dpow[lag][0] + a
            dpow[lag][1] = dpow[lag][1] + b
            a, b = _cmul_conj(er, ei, dxr, dxi)
            dbbr = dbbr + a
            dbbi = dbbi + b
        for t in range(S5_CHUNK):
            er, ei = powers[t + 1]
            dzr = dwor_ref[pl.ds(t * LANES, LANES), :]
            dzi = -dwoi_ref[pl.ds(t * LANES, LANES), :]
            a, b = _cmul_conj(cr, ci, dzr, dzi)
            dpow[t + 1][0] = dpow[t + 1][0] + a
            dpow[t + 1][1] = dpow[t + 1][1] + b
            a, b = _cmul_conj(er, ei, dzr, dzi)
            dcr = dcr + a
            dci = dci + b
        dpow[S5_CHUNK][0] = dpow[S5_CHUNK][0] + dpr_ref[...]
        dpow[S5_CHUNK][1] = dpow[S5_CHUNK][1] + dpi_ref[...]
        dfr, dfi = _cmul_conj(br, bi, dbbr, dbbi)
        dbr, dbi = _cmul_conj(fr, fi, dbbr, dbbi)
        dnr, dni = _cmul(ar / den, ai / den, dfr, dfi)
        qr = (fr * ar + fi * ai) / den
        qi = (fi * ar - fr * ai) / den
        dlr, dli = _cmul(-qr, qi, dfr, dfi)
        dpow[1][0] = dpow[1][0] + dnr
        dpow[1][1] = dpow[1][1] + dni
        dxr, dxi = zero, zero
        for lag in range(1, N_LAGS):
            a, b = _cmul_conj(powers[lag][0], powers[lag][1], dpow[lag][0], dpow[lag][1])
            dxr = dxr + lag * a
            dxi = dxi + lag * b
        dar_ref[...] = dlr + dt * dxr
        dai_ref[...] = dli + dt * dxi
        dls_ref[...] = dt * (ar * dxr + ai * dxi)
        dbr_ref[...] = dbr
        dbi_ref[...] = dbi
        dcr_ref[...] = dcr
        dci_ref[...] = dci

    return pl.pallas_call(
        body, name="s5_param_bwd", grid=(S5_OCTETS,),
        in_specs=[OCT_TILE] * 7 + [OCT_KD, OCT_W, OCT_W, OCT_W, OCT_W, OCT_TILE, OCT_TILE], out_specs=[OCT_TILE] * 7,
        out_shape=[_sds((S5_OCTETS, LANES, LANES))] * 7,
        compiler_params=_params(("parallel",)),
    )(*tiles, dkd, dws_re, dws_im, dwo_re, dwo_im, dp_re, dp_im)


def _doubled(v):
    return jnp.concatenate([v, v], axis=-1)


def _s5_param_tiles(a_re, a_im, log_step, b_re, b_im, c_re, c_im):
    def per_group(a):
        return _doubled(jnp.broadcast_to(a.reshape(S5_OCTETS, S5_OCT, 1, SSM_STATE),
                                         (S5_OCTETS, S5_OCT, SSM_GROUP, SSM_STATE)).reshape(S5_OCTETS, LANES, SSM_STATE))

    ls = jnp.broadcast_to(log_step.reshape(S5_OCTETS, S5_OCT, 1, 1), (S5_OCTETS, S5_OCT, SSM_GROUP, LANES)).reshape(S5_OCTETS, LANES, LANES)
    bt = lambda b: _doubled(b.transpose(0, 2, 1).reshape(S5_OCTETS, LANES, SSM_STATE))
    ct = lambda c: _doubled(c.reshape(S5_OCTETS, LANES, SSM_STATE))
    return [per_group(a_re), per_group(a_im), ls, bt(b_re), bt(b_im), ct(c_re), ct(c_im)]


def _s5_param_grads(dtiles):
    dar, dai, dls, dbr, dbi, dcr, dci = dtiles
    halves = lambda d: d[..., :SSM_STATE] + d[..., SSM_STATE:]
    per_group = lambda d: halves(d).reshape(SSM_GROUPS, SSM_GROUP, SSM_STATE).sum(axis=1)
    per_row = lambda d: halves(d).reshape(SSM_GROUPS, SSM_GROUP, SSM_STATE)
    return (per_group(dar), per_group(dai), dls.reshape(SSM_GROUPS, SSM_GROUP * LANES).sum(axis=1),
            per_row(dbr).transpose(0, 2, 1), per_row(dbi).transpose(0, 2, 1), per_row(dcr), per_row(dci))


def _group_power_rows(tile):
    return tile[:, ::SSM_GROUP, :SSM_STATE].reshape(1, S5_STATES)


def _group_power_tiles(row):
    t = jnp.pad(row.reshape(S5_OCTETS, S5_OCT, 1, SSM_STATE), ((0, 0), (0, 0), (0, SSM_GROUP - 1), (0, LANES - SSM_STATE)))
    return t.reshape(S5_OCTETS, LANES, LANES)


def _rope_tables(t):
    pos = jnp.arange(t, dtype=F32)
    inv_freq = ROPE_THETA ** (-jnp.arange(0, HEAD_DIM, 2, dtype=F32) / HEAD_DIM)
    ang = pos[:, None] * inv_freq[None, :]
    cos = jnp.cos(ang)
    sin = jnp.sin(ang)
    cos64 = jnp.concatenate([cos, cos], axis=1)
    sin64 = jnp.concatenate([-sin, sin], axis=1)
    return jnp.concatenate([cos64, cos64], axis=1), jnp.concatenate([sin64, sin64], axis=1)


def _row(v):
    return v.reshape(1, -1)


def _ssm_forward(x, w):
    tiles = _s5_param_tiles(w["a_re"], w["a_im"], w["log_step"], w["b_re"], w["b_im"], w["c_re"], w["c_im"])
    kd, ws_re, ws_im, wo_re, wo_im, p_re, p_im = s5_param_fwd(tiles)
    mats = dict(kd=kd, ws_re=ws_re, ws_im=ws_im, wo_re=wo_re, wo_im=wo_im, a_re=_group_power_rows(p_re), a_im=_group_power_rows(p_im))
    u, gate = ssm_proj_fwd(x, _row(w["norm"]), w["w_in"])
    s_re, s_im = s5_chunk_states(u, mats["ws_re"], mats["ws_im"])
    h_re, h_im = s5_scan_fwd(s_re, s_im, mats["a_re"], mats["a_im"])
    y_scan = s5_outputs(u, h_re, h_im, mats["kd"], mats["wo_re"], mats["wo_im"])
    y, g2, x_new = ssm_mix_fwd(x, u, gate, y_scan, _row(w["d"]), w["w_glu"], _row(w["b_glu"]), w["w_out"])
    saved = dict(x=x, u=u, gate=gate, y=y, g2=g2, h_re=h_re, h_im=h_im, mats=mats, tiles=tiles)
    return x_new, saved


def _ssm_backward(dxo, w, s):
    dy, dgate, dw_out, dw_glu, db_glu, dd = ssm_mix_bwd(dxo, s["u"], s["gate"], s["y"], s["g2"], w["w_glu"], w["w_out"])
    mats = s["mats"]
    dh_re, dh_im = s5_state_grads(dy, mats["wo_re"], mats["wo_im"])
    ds_re, ds_im, da_re, da_im = s5_scan_bwd(dh_re, dh_im, s["h_re"], s["h_im"], mats["a_re"], mats["a_im"])
    du_scan = s5_input_grads(dy, ds_re, ds_im, mats["kd"], mats["ws_re"], mats["ws_im"])
    dkd, dws_re, dws_im, dwo_re, dwo_im = s5_weight_grads(s["u"], dy, s["h_re"], s["h_im"], ds_re, ds_im)
    dparams = _s5_param_grads(s5_param_bwd(s["tiles"], dkd, dws_re, dws_im, dwo_re, dwo_im,
                                           _group_power_tiles(da_re), _group_power_tiles(da_im)))
    dx, dw_in, dnorm = ssm_proj_bwd(s["x"], _row(w["norm"]), dxo, dy, du_scan, dgate, _row(w["d"]), w["w_in"])
    grads = dict(norm=dnorm, w_in=dw_in, d=dd, w_glu=dw_glu, b_glu=db_glu, w_out=dw_out)
    for name, val in zip(("a_re", "a_im", "log_step", "b_re", "b_im", "c_re", "c_im"), dparams):
        grads[name] = val
    return dx, grads


def _attn_forward(x, w, cos2, sin2):
    q, k, v, gate = attn_proj_fwd(x, _row(w["norm"]), w["w_in"], cos2, sin2)
    o, lse = attn_fwd(q, k, v, w["sinks"])
    x_new = attn_out_fwd(x, o, gate, w["w_out"])
    return x_new, dict(x=x, q=q, k=k, v=v, gate=gate, o=o, lse=lse)


def _attn_backward(dxo, w, s, cos2, sin2):
    do, dgate, dw_out = attn_out_bwd(dxo, s["o"], s["gate"], w["w_out"])
    dq, dk, dv, dsinks = attn_bwd(s["q"], s["k"], s["v"], w["sinks"], s["o"], s["lse"], do)
    dx, dw_in, dnorm = attn_proj_bwd(s["x"], _row(w["norm"]), dxo, dq, dk, dv, dgate, cos2, sin2, w["w_in"])
    return dx, dict(norm=dnorm, w_in=dw_in, sinks=dsinks[0, :N_Q_HEADS], w_out=dw_out)


def _sequence_step(x, target, layers, final_norm):
    cos2, sin2 = _rope_tables(x.shape[0])
    saved = []
    for i, w in enumerate(layers):
        if i % 2 == 0:
            x, s = _ssm_forward(x, w)
        else:
            x, s = _attn_forward(x, w, cos2, sin2)
        saved.append(s)
    loss, dx, dfinal = loss_head(x, _row(final_norm), target)
    grads = {"final_norm": dfinal}
    for i in reversed(range(len(layers))):
        if i % 2 == 0:
            dx, g = _ssm_backward(dx, layers[i], saved[i])
        else:
            dx, g = _attn_backward(dx, layers[i], saved[i], cos2, sin2)
        for name, val in g.items():
            grads["l%d_%s" % (i, name)] = val
    return loss[0, 0], dx, grads


ANY = pl.BlockSpec(memory_space=pl.ANY)


def _place():
    return lax.axis_index("x"), lax.axis_index("y"), lax.axis_index("c")


def _other_chips(x, y):
    return [(1 - x, y), (x, 1 - y), (1 - x, 1 - y)]


def gather_weight_shards(shards):
    n = len(shards)

    def body(*refs):
        ins, outs = refs[:n], refs[n:2 * n]
        send_sems, recv_sems, pass_send_sems, pass_recv_sems, local_sems = refs[2 * n:]
        x, y, c = _place()
        me = 2 * x + y
        chips = _other_chips(x, y)

        def half(i, block, which):
            rows = ins[i].shape[0] // 2
            return outs[i].at[block, pl.ds(which * rows, rows), :]

        def my_half(i):
            rows = ins[i].shape[0] // 2
            return ins[i].at[pl.ds(c * rows, rows), :]

        started = []
        for i in range(n):
            local = pltpu.make_async_copy(ins[i], outs[i].at[me], local_sems.at[i])
            local.start()
            started.append(local)
        sends = []
        for i in range(n):
            for k, (tx, ty) in enumerate(chips):
                cp = pltpu.make_async_remote_copy(src_ref=my_half(i), dst_ref=half(i, me, c), send_sem=send_sems.at[i, k],
                                                  recv_sem=recv_sems.at[i, k], device_id=(tx, ty, c), device_id_type=MESH)
                cp.start()
                sends.append(cp)
        for i in range(n):
            for k, (tx, ty) in enumerate(chips):
                landed = half(i, 2 * tx + ty, c)
                pltpu.make_async_remote_copy(src_ref=my_half(i), dst_ref=landed, send_sem=send_sems.at[i, k],
                                             recv_sem=recv_sems.at[i, k], device_id=(tx, ty, c), device_id_type=MESH).wait_recv()
                cp = pltpu.make_async_remote_copy(src_ref=landed, dst_ref=landed, send_sem=pass_send_sems.at[i, k],
                                                  recv_sem=pass_recv_sems.at[i, k], device_id=(x, y, 1 - c), device_id_type=MESH)
                cp.start()
                sends.append(cp)
        for i in range(n):
            for k, (tx, ty) in enumerate(chips):
                missing = half(i, 2 * tx + ty, 1 - c)
                pltpu.make_async_remote_copy(src_ref=missing, dst_ref=missing, send_sem=pass_send_sems.at[i, k],
                                             recv_sem=pass_recv_sems.at[i, k], device_id=(x, y, 1 - c), device_id_type=MESH).wait_recv()
        for cp in sends:
            cp.wait_send()
        for cp in started:
            cp.wait()

    sems = pltpu.SemaphoreType.DMA((n, 3))
    return pl.pallas_call(
        body, name="gather_weight_shards",
        in_specs=[ANY] * n, out_specs=[ANY] * n,
        out_shape=[_sds((4,) + s.shape, s.dtype) for s in shards],
        scratch_shapes=[sems, sems, sems, sems, pltpu.SemaphoreType.DMA((n,))],
    )(*shards)


def exchange_halves_with_sibling(grads):
    n = len(grads)

    def body(*refs):
        ins, outs = refs[:n], refs[n:2 * n]
        send_sems, recv_sems = refs[2 * n:]
        x, y, c = _place()
        copies = []
        for i in range(n):
            half = ins[i].shape[1] // 2
            src = ins[i].at[:, pl.ds((1 - c) * half, half), :]
            cp = pltpu.make_async_remote_copy(src_ref=src, dst_ref=outs[i], send_sem=send_sems.at[i], recv_sem=recv_sems.at[i],
                                              device_id=(x, y, 1 - c), device_id_type=MESH)
            cp.start()
            copies.append(cp)
        for cp in copies:
            cp.wait()

    return pl.pallas_call(
        body, name="exchange_halves_with_sibling",
        in_specs=[ANY] * n, out_specs=[ANY] * n,
        out_shape=[_sds((g.shape[0], g.shape[1] // 2, g.shape[2])) for g in grads],
        scratch_shapes=[pltpu.SemaphoreType.DMA((n,)), pltpu.SemaphoreType.DMA((n,))],
    )(*grads)


def scatter_blocks_to_chips(sums):
    n = len(sums)

    def body(*refs):
        ins, outs = refs[:n], refs[n:2 * n]
        send_sems, recv_sems, local_sems = refs[2 * n:]
        x, y, c = _place()
        me = 2 * x + y

        def block_for(i, chip):
            return ins[i].at[chip] if ins[i].shape[0] == 4 else ins[i].at[0]

        local = []
        for i in range(n):
            cp = pltpu.make_async_copy(block_for(i, me), outs[i].at[me], local_sems.at[i])
            cp.start()
            local.append(cp)
        sends = []
        for i in range(n):
            for k, (tx, ty) in enumerate(_other_chips(x, y)):
                cp = pltpu.make_async_remote_copy(src_ref=block_for(i, 2 * tx + ty), dst_ref=outs[i].at[me], send_sem=send_sems.at[i, k],
                                                  recv_sem=recv_sems.at[i, k], device_id=(tx, ty, c), device_id_type=MESH)
                cp.start()
                sends.append(cp)
        for i in range(n):
            for k, (tx, ty) in enumerate(_other_chips(x, y)):
                pltpu.make_async_remote_copy(src_ref=block_for(i, me), dst_ref=outs[i].at[2 * tx + ty], send_sem=send_sems.at[i, k],
                                             recv_sem=recv_sems.at[i, k], device_id=(tx, ty, c), device_id_type=MESH).wait_recv()
        for cp in sends:
            cp.wait_send()
        for cp in local:
            cp.wait()

    return pl.pallas_call(
        body, name="scatter_blocks_to_chips",
        in_specs=[ANY] * n, out_specs=[ANY] * n,
        out_shape=[_sds((4,) + s.shape[1:], s.dtype) for s in sums],
        scratch_shapes=[pltpu.SemaphoreType.DMA((n, 3)), pltpu.SemaphoreType.DMA((n, 3)), pltpu.SemaphoreType.DMA((n,))],
    )(*sums)


def swap_halves_with_sibling(pieces):
    n = len(pieces)

    def body(*refs):
        ins, outs = refs[:n], refs[n:2 * n]
        send_sems, recv_sems, local_sems = refs[2 * n:]
        x, y, c = _place()
        started = []
        for i in range(n):
            cp = pltpu.make_async_copy(ins[i], outs[i].at[c], local_sems.at[i])
            cp.start()
            started.append(cp)
        swaps = []
        for i in range(n):
            cp = pltpu.make_async_remote_copy(src_ref=ins[i], dst_ref=outs[i].at[c], send_sem=send_sems.at[i], recv_sem=recv_sems.at[i],
                                              device_id=(x, y, 1 - c), device_id_type=MESH)
            cp.start()
            swaps.append(cp)
        for i in range(n):
            pltpu.make_async_remote_copy(src_ref=ins[i], dst_ref=outs[i].at[1 - c], send_sem=send_sems.at[i], recv_sem=recv_sems.at[i],
                                         device_id=(x, y, 1 - c), device_id_type=MESH).wait_recv()
        for cp in swaps:
            cp.wait_send()
        for cp in started:
            cp.wait()

    return pl.pallas_call(
        body, name="swap_halves_with_sibling",
        in_specs=[ANY] * n, out_specs=[ANY] * n,
        out_shape=[_sds((2,) + p.shape) for p in pieces],
        scratch_shapes=[pltpu.SemaphoreType.DMA((n,)), pltpu.SemaphoreType.DMA((n,)), pltpu.SemaphoreType.DMA((n,))],
    )(*pieces)


def _row_tile(rows, cols):
    tm = rows
    while tm * cols * 4 > (2 << 20) and tm % 16 == 0:
        tm //= 2
    return tm


def add_pair(a, b, out_dtype):
    nb, rows, cols = a.shape
    tm = _row_tile(rows, cols)

    def body(a_ref, b_ref, o_ref):
        o_ref[...] = (a_ref[...] + b_ref[...]).astype(out_dtype)

    spec = pl.BlockSpec((None, tm, cols), lambda j, i: (j, i, 0))
    return pl.pallas_call(
        body, name="add_pair", grid=(nb, rows // tm), in_specs=[spec, spec], out_specs=spec, out_shape=_sds(a.shape, out_dtype),
        compiler_params=_params(("parallel", "parallel")),
    )(a, b)


def sum_four(a):
    _, rows, cols = a.shape
    tm = _row_tile(rows, cols)

    def body(a_ref, o_ref):
        o_ref[...] = ((a_ref[0].astype(F32) + a_ref[1].astype(F32)) + a_ref[2].astype(F32)) + a_ref[3].astype(F32)

    return pl.pallas_call(
        body, name="sum_four", grid=(rows // tm,),
        in_specs=[pl.BlockSpec((4, tm, cols), lambda i: (0, i, 0))], out_specs=pl.BlockSpec((tm, cols), lambda i: (i, 0)),
        out_shape=_sds((rows, cols)), compiler_params=_params(("parallel",)),
    )(a)


def adamw(w, g, m, v):
    rows, cols = w.shape
    tm = _row_tile(rows, cols)
    c1 = 1.0 - ADAM_B1 ** ADAM_STEP
    c2 = 1.0 - ADAM_B2 ** ADAM_STEP

    def body(w_ref, g_ref, m_ref, v_ref, d_ref, nm_ref, nv_ref):
        g = g_ref[...]
        nm = ADAM_B1 * m_ref[...] + (1.0 - ADAM_B1) * g
        nv = ADAM_B2 * v_ref[...] + (1.0 - ADAM_B2) * (g * g)
        d_ref[...] = -ADAM_LR * ((nm / c1) / (jnp.sqrt(nv / c2) + ADAM_EPS) + ADAM_WD * w_ref[...])
        nm_ref[...] = nm
        nv_ref[...] = nv

    spec = pl.BlockSpec((tm, cols), lambda i: (i, 0))
    return pl.pallas_call(
        body, name="adamw", grid=(rows // tm,), in_specs=[spec] * 4, out_specs=[spec] * 3,
        out_shape=[_sds(w.shape)] * 3, compiler_params=_params(("parallel",)),
    )(w, g, m, v)


PACK_TILE = 8 * LANES
PACK_PIECES = 8
PACK_ALIGN = PACK_PIECES * 16


def _pack_small(values):
    parts = []
    for name in SMALL_NAMES:
        flat = values[name].reshape(-1)
        pad = (-flat.shape[0]) % PACK_TILE
        if pad:
            flat = jnp.concatenate([flat, jnp.zeros((pad,), F32)])
        parts.append(flat.reshape(-1, LANES))
    rows = sum(p.shape[0] for p in parts)
    pad = (-rows) % PACK_ALIGN
    if pad:
        parts.append(jnp.zeros((pad, LANES), F32))
    return jnp.concatenate(parts, axis=0)


def _unpack_small(pack, like):
    out = {}
    row = 0
    for name in SMALL_NAMES:
        size = math.prod(like[name].shape)
        rows = -(-size // PACK_TILE) * 8
        out[name] = pack[row:row + rows].reshape(-1)[:size].reshape(like[name].shape)
        row += rows
    return out


def _is_column_sharded(name):
    return name.endswith("w_in")


def _to_blocks(name, full):
    if _is_column_sharded(name):
        rows, cols = full.shape
        return full.reshape(rows, 4, cols // 4).transpose(1, 0, 2)
    return full.reshape(4, full.shape[0] // 4, full.shape[1])


def _from_blocks(name, stacked):
    if _is_column_sharded(name):
        return stacked.transpose(1, 0, 2).reshape(stacked.shape[1], 4 * stacked.shape[2])
    return stacked.reshape(4 * stacked.shape[1], stacked.shape[2])


def _train_step(x, loss_target, weights, moments_m, moments_v):
    c = lax.axis_index("c")
    gathered = gather_weight_shards([weights[n].astype(MXU_DTYPE) for n in BIG_NAMES])
    full = {n: _from_blocks(n, g) for n, g in zip(BIG_NAMES, gathered)}
    layers = []
    for i in range(4):
        names = SSM_NAMES if i % 2 == 0 else ATTN_NAMES
        w = {}
        for n in names:
            key = "l%d_%s" % (i, n)
            if key in full:
                w[n] = full[key]
            else:
                w[n] = weights[key]
        layers.append(w)
    loss, dx, grads = _sequence_step(x[0], loss_target[0], layers, weights["final_norm"])
    loss = lax.psum(loss, ("x", "y", "c"))
    small_pack = _pack_small({n: grads[n] for n in SMALL_NAMES})
    blocks = [_to_blocks(n, grads[n]) for n in BIG_NAMES] + [small_pack[None]]
    from_sibling = exchange_halves_with_sibling(blocks)
    chip_sums = []
    for i, (b, r) in enumerate(zip(blocks, from_sibling)):
        half = b.shape[1] // 2
        dtype = WIRE_DTYPE if i < len(BIG_NAMES) else F32
        chip_sums.append(add_pair(lax.dynamic_slice_in_dim(b, c * half, half, axis=1), r, dtype))
    contributions = scatter_blocks_to_chips(chip_sums)
    reduced = [sum_four(a) for a in contributions]
    shared = swap_halves_with_sibling(reduced)
    big_grads = {n: s.reshape(2 * s.shape[1], s.shape[2]) for n, s in zip(BIG_NAMES, shared[:-1])}
    small_grad_pack = shared[-1].reshape(-1, LANES)
    out_grad, out_delta, out_m, out_v = {}, {}, {}, {}
    for n in BIG_NAMES:
        out_grad[n] = big_grads[n]
        out_delta[n], out_m[n], out_v[n] = adamw(weights[n], big_grads[n], moments_m[n], moments_v[n])
    small_like = {n: weights[n] for n in SMALL_NAMES}
    d_pack, m_pack, v_pack = adamw(_pack_small(small_like), small_grad_pack, _pack_small({n: moments_m[n] for n in SMALL_NAMES}),
                                   _pack_small({n: moments_v[n] for n in SMALL_NAMES}))
    out_grad.update(_unpack_small(small_grad_pack, small_like))
    out_delta.update(_unpack_small(d_pack, small_like))
    out_m.update(_unpack_small(m_pack, small_like))
    out_v.update(_unpack_small(v_pack, small_like))
    outs = [loss, dx[None]]
    for group in (out_grad, out_delta, out_m, out_v):
        outs.extend(group[n] for n in WEIGHT_NAMES)
    return tuple(outs)


def kernel(x, l0_norm, l0_w_in, l0_a_re, l0_a_im, l0_log_step, l0_b_re, l0_b_im, l0_c_re, l0_c_im, l0_d, l0_w_glu, l0_b_glu, l0_w_out, l1_norm, l1_w_in, l1_sinks, l1_w_out, l2_norm, l2_w_in, l2_a_re, l2_a_im, l2_log_step, l2_b_re, l2_b_im, l2_c_re, l2_c_im, l2_d, l2_w_glu, l2_b_glu, l2_w_out, l3_norm, l3_w_in, l3_sinks, l3_w_out, final_norm, loss_target, m_l0_norm, m_l0_w_in, m_l0_a_re, m_l0_a_im, m_l0_log_step, m_l0_b_re, m_l0_b_im, m_l0_c_re, m_l0_c_im, m_l0_d, m_l0_w_glu, m_l0_b_glu, m_l0_w_out, m_l1_norm, m_l1_w_in, m_l1_sinks, m_l1_w_out, m_l2_norm, m_l2_w_in, m_l2_a_re, m_l2_a_im, m_l2_log_step, m_l2_b_re, m_l2_b_im, m_l2_c_re, m_l2_c_im, m_l2_d, m_l2_w_glu, m_l2_b_glu, m_l2_w_out, m_l3_norm, m_l3_w_in, m_l3_sinks, m_l3_w_out, m_final_norm, v_l0_norm, v_l0_w_in, v_l0_a_re, v_l0_a_im, v_l0_log_step, v_l0_b_re, v_l0_b_im, v_l0_c_re, v_l0_c_im, v_l0_d, v_l0_w_glu, v_l0_b_glu, v_l0_w_out, v_l1_norm, v_l1_w_in, v_l1_sinks, v_l1_w_out, v_l2_norm, v_l2_w_in, v_l2_a_re, v_l2_a_im, v_l2_log_step, v_l2_b_re, v_l2_b_im, v_l2_c_re, v_l2_c_im, v_l2_d, v_l2_w_glu, v_l2_b_glu, v_l2_w_out, v_l3_norm, v_l3_w_in, v_l3_sinks, v_l3_w_out, v_final_norm):
    args = locals()
    weights = {n: args[n] for n in WEIGHT_NAMES}
    moments_m = {n: args["m_" + n] for n in WEIGHT_NAMES}
    moments_v = {n: args["v_" + n] for n in WEIGHT_NAMES}
    return _train_step(x, loss_target, weights, moments_m, moments_v)
```

```python
import functools
import math

import jax
import jax.numpy as jnp
from jax import lax
from jax.experimental import pallas as pl
from jax.experimental.pallas import tpu as pltpu

F32 = jnp.float32
MXU_DTYPE = jnp.bfloat16
WIRE_DTYPE = jnp.bfloat16
MESH = pl.DeviceIdType.MESH

D_MODEL = 1024
BRANCH = 1024
NORM_EPS = 1e-5
SSM_GROUPS = 64
SSM_GROUP = 16
SSM_STATE = 64
S5_CHUNK = 16
LANES = 128
S5_OCT = LANES // SSM_GROUP
S5_OCTETS = SSM_GROUPS // S5_OCT
S5_OCT_IN = S5_CHUNK * LANES
S5_OCT_STATE = S5_OCT * SSM_STATE
S5_STATES = SSM_GROUPS * SSM_STATE
HEAD_DIM = 64
N_Q_HEADS = 16
N_KV_HEADS = 2
GQA_GROUP = N_Q_HEADS // N_KV_HEADS
ATTN_BLOCK = 128
Q_DIM = N_Q_HEADS * HEAD_DIM
KV_DIM = N_KV_HEADS * HEAD_DIM
ROPE_THETA = 10000.0
NEG_INF = -1e30
ADAM_LR = 0.001
ADAM_B1 = 0.9
ADAM_B2 = 0.999
ADAM_EPS = 1e-08
ADAM_WD = 0.01
ADAM_STEP = 10

VMEM_LIMIT_V7X = 56 * 1024 * 1024
ROW_TILE_FWD = 512
ROW_TILE_BWD = 512

SSM_NAMES = ("norm", "w_in", "a_re", "a_im", "log_step", "b_re", "b_im", "c_re", "c_im", "d", "w_glu", "b_glu", "w_out")
ATTN_NAMES = ("norm", "w_in", "sinks", "w_out")


def _weight_names():
    names = []
    for i in range(4):
        for n in (SSM_NAMES if i % 2 == 0 else ATTN_NAMES):
            names.append("l%d_%s" % (i, n))
    names.append("final_norm")
    return names


WEIGHT_NAMES = _weight_names()
BIG_NAMES = [n for n in WEIGHT_NAMES if n.endswith(("w_in", "w_glu", "w_out"))]
SMALL_NAMES = [n for n in WEIGHT_NAMES if n not in BIG_NAMES]


def _params(semantics=None):
    return pltpu.CompilerParams(dimension_semantics=semantics, vmem_limit_bytes=VMEM_LIMIT_V7X)


def _rows(tm, n):
    return pl.BlockSpec((tm, n), lambda i: (i, 0))


def _whole(shape):
    return pl.BlockSpec(shape, lambda i: (0,) * len(shape), pipeline_mode=pl.Buffered(1))


def _sds(shape, dtype=F32):
    return jax.ShapeDtypeStruct(shape, dtype)


def _mm(a, b):
    return jnp.dot(a.astype(MXU_DTYPE), b.astype(MXU_DTYPE), preferred_element_type=F32)


def _mm_tn(a, b):
    return lax.dot_general(a.astype(MXU_DTYPE), b.astype(MXU_DTYPE), (((0,), (0,)), ((), ())), preferred_element_type=F32)


def _mm_nt(a, b):
    return lax.dot_general(a.astype(MXU_DTYPE), b.astype(MXU_DTYPE), (((1,), (1,)), ((), ())), preferred_element_type=F32)


def _sigmoid(x):
    return 1.0 / (1.0 + jnp.exp(-x))


def _silu(x):
    return x * _sigmoid(x)


def _silu_grad(x):
    s = _sigmoid(x)
    return s * (1.0 + x * (1.0 - s))


GELU_C0 = math.sqrt(2.0 / math.pi)
GELU_C1 = 0.044715


def _gelu(x):
    return 0.5 * x * (1.0 + jnp.tanh(GELU_C0 * (x + GELU_C1 * x * x * x)))


def _gelu_grad(x):
    th = jnp.tanh(GELU_C0 * (x + GELU_C1 * x * x * x))
    return 0.5 * (1.0 + th) + 0.5 * x * (1.0 - th * th) * GELU_C0 * (1.0 + 3.0 * GELU_C1 * x * x)


def _rms(x, g):
    r = lax.rsqrt(jnp.mean(x * x, axis=-1, keepdims=True) + NORM_EPS)
    xhat = x * r
    return r, xhat, xhat * g


def _rms_bwd(dh, g, r, xhat):
    dxhat = dh * g
    dx = r * (dxhat - xhat * jnp.mean(dxhat * xhat, axis=-1, keepdims=True))
    return dx, jnp.sum(dh * xhat, axis=0, keepdims=True)


def _swap_half_heads(x):
    n = x.shape[-1]
    lane = lax.broadcasted_iota(jnp.int32, x.shape, x.ndim - 1)
    first = (lane % HEAD_DIM) < (HEAD_DIM // 2)
    return jnp.where(first, pltpu.roll(x, n - HEAD_DIM // 2, x.ndim - 1), pltpu.roll(x, HEAD_DIM // 2, x.ndim - 1))


def _tile_lanes(t, reps):
    return jnp.concatenate([t] * reps, axis=1)


def ssm_proj_fwd(x, norm, w_in):
    t = x.shape[0]
    tm = min(ROW_TILE_FWD, t)

    def body(x_ref, g_ref, w_ref, u_ref, gate_ref):
        _, _, h = _rms(x_ref[...], g_ref[...])
        p = _mm(h, w_ref[...])
        u_ref[...] = p[:, :BRANCH]
        gate_ref[...] = p[:, BRANCH:]

    return pl.pallas_call(
        body, name="ssm_proj_fwd", grid=(t // tm,),
        in_specs=[_rows(tm, D_MODEL), _whole((1, D_MODEL)), _whole((D_MODEL, 2 * BRANCH))],
        out_specs=[_rows(tm, BRANCH), _rows(tm, BRANCH)],
        out_shape=[_sds((t, BRANCH)), _sds((t, BRANCH))],
        compiler_params=_params(("parallel",)),
    )(x, norm, w_in)


def _chunk_rows(ref, nk, dtype=None):
    rows = jnp.concatenate([ref[pl.ds(s, nk, stride=S5_CHUNK), :] for s in range(S5_CHUNK)], axis=1)
    return rows.astype(MXU_DTYPE if dtype is None else dtype)


def _store_chunk_rows(ref, val, nk):
    for s in range(S5_CHUNK):
        ref[pl.ds(s, nk, stride=S5_CHUNK), :] = val[:, s * LANES:(s + 1) * LANES]


def _own_group_mask():
    row = lax.broadcasted_iota(jnp.int32, (S5_OCT_IN, S5_OCT_STATE), 0)
    col = lax.broadcasted_iota(jnp.int32, (S5_OCT_IN, S5_OCT_STATE), 1)
    return ((row % LANES) // SSM_GROUP) == (col // SSM_STATE)


def _spread_groups(w):
    return jnp.where(_own_group_mask(), jnp.concatenate([w] * (S5_OCT_STATE // LANES), axis=1), 0.0).astype(MXU_DTYPE)


def _fold_groups(p):
    p = jnp.where(_own_group_mask(), p, 0.0)
    return sum(p[:, q * LANES:(q + 1) * LANES] for q in range(S5_OCT_STATE // LANES))


def _fill_toeplitz(win_ref, kd_ref):
    win_ref[...] = jnp.zeros_like(win_ref)
    for s in range(S5_CHUNK):
        for t in range(s, S5_CHUNK):
            win_ref[s * LANES:(s + 1) * LANES, t * LANES:(t + 1) * LANES] = kd_ref[t - s].astype(MXU_DTYPE)


def _strip(t):
    return pl.BlockSpec((t, LANES), lambda b: (0, b))


def _oct_states(nk):
    return pl.BlockSpec((nk, S5_OCT_STATE), lambda b: (0, b))


OCT_W = pl.BlockSpec((None, S5_OCT_IN, LANES), lambda b: (b, 0, 0))
OCT_KD = pl.BlockSpec((None, S5_CHUNK, LANES, LANES), lambda b: (b, 0, 0, 0))


def s5_chunk_states(u, ws_re, ws_im):
    t = u.shape[0]
    nk = t // S5_CHUNK

    def body(u_ref, wr_ref, wi_ref, re_ref, im_ref):
        uc = _chunk_rows(u_ref, nk)
        re_ref[...] = _mm(uc, _spread_groups(wr_ref[...]))
        im_ref[...] = _mm(uc, _spread_groups(wi_ref[...]))

    return pl.pallas_call(
        body, name="s5_chunk_states", grid=(S5_OCTETS,),
        in_specs=[_strip(t), OCT_W, OCT_W], out_specs=[_oct_states(nk), _oct_states(nk)],
        out_shape=[_sds((nk, S5_STATES)), _sds((nk, S5_STATES))],
        compiler_params=_params(("parallel",)),
    )(u, ws_re, ws_im)


def s5_scan_fwd(s_re, s_im, a_re, a_im):
    nk = s_re.shape[0]

    def body(sre_ref, sim_ref, ar_ref, ai_ref, hre_ref, him_ref):
        ar = ar_ref[...]
        ai = ai_ref[...]

        def step(k, carry):
            hr, hi = carry
            hre_ref[pl.ds(k, 1), :] = hr
            him_ref[pl.ds(k, 1), :] = hi
            sr = sre_ref[pl.ds(k, 1), :]
            si = sim_ref[pl.ds(k, 1), :]
            return ar * hr - ai * hi + sr, ai * hr + ar * hi + si

        zero = jnp.zeros((1, S5_STATES), F32)
        lax.fori_loop(0, nk, step, (zero, zero))

    vm = pl.BlockSpec(memory_space=pltpu.VMEM)
    return pl.pallas_call(
        body, name="s5_scan_fwd", in_specs=[vm, vm, vm, vm], out_specs=[vm, vm],
        out_shape=[_sds((nk, S5_STATES)), _sds((nk, S5_STATES))],
        compiler_params=_params(),
    )(s_re, s_im, a_re, a_im)


def s5_outputs(u, h_re, h_im, kd, wo_re, wo_im):
    t = u.shape[0]
    nk = t // S5_CHUNK

    def body(u_ref, hre_ref, him_ref, kd_ref, wor_ref, woi_ref, y_ref, win_ref):
        _fill_toeplitz(win_ref, kd_ref)
        y = _mm(_chunk_rows(u_ref, nk), win_ref[...])
        y = y + _mm_nt(hre_ref[...], _spread_groups(wor_ref[...])) + _mm_nt(him_ref[...], _spread_groups(woi_ref[...]))
        _store_chunk_rows(y_ref, y, nk)

    return pl.pallas_call(
        body, name="s5_outputs", grid=(S5_OCTETS,),
        in_specs=[_strip(t), _oct_states(nk), _oct_states(nk), OCT_KD, OCT_W, OCT_W],
        out_specs=_strip(t), out_shape=_sds((t, BRANCH)),
        scratch_shapes=[pltpu.VMEM((S5_OCT_IN, S5_OCT_IN), MXU_DTYPE)],
        compiler_params=_params(("parallel",)),
    )(u, h_re, h_im, kd, wo_re, wo_im)


def s5_state_grads(dy, wo_re, wo_im):
    t = dy.shape[0]
    nk = t // S5_CHUNK

    def body(dy_ref, wor_ref, woi_ref, re_ref, im_ref):
        dyc = _chunk_rows(dy_ref, nk)
        re_ref[...] = _mm(dyc, _spread_groups(wor_ref[...]))
        im_ref[...] = _mm(dyc, _spread_groups(woi_ref[...]))

    return pl.pallas_call(
        body, name="s5_state_grads", grid=(S5_OCTETS,),
        in_specs=[_strip(t), OCT_W, OCT_W], out_specs=[_oct_states(nk), _oct_states(nk)],
        out_shape=[_sds((nk, S5_STATES)), _sds((nk, S5_STATES))],
        compiler_params=_params(("parallel",)),
    )(dy, wo_re, wo_im)


def s5_scan_bwd(dh_re, dh_im, h_re, h_im, a_re, a_im):
    nk = dh_re.shape[0]

    def body(dhr_ref, dhi_ref, hr_ref, hi_ref, ar_ref, ai_ref, dsr_ref, dsi_ref, dar_ref, dai_ref):
        ar = ar_ref[...]
        ai = ai_ref[...]

        dar_ref[...] = jnp.zeros_like(dar_ref)
        dai_ref[...] = jnp.zeros_like(dai_ref)

        def step(i, carry):
            gr, gi = carry
            k = nk - 1 - i
            dhr = dhr_ref[pl.ds(k, 1), :]
            dhi = dhi_ref[pl.ds(k, 1), :]
            dsr_ref[pl.ds(k, 1), :] = gr
            dsi_ref[pl.ds(k, 1), :] = gi
            hr = hr_ref[pl.ds(k, 1), :]
            hi = hi_ref[pl.ds(k, 1), :]
            dar_ref[...] += gr * hr + gi * hi
            dai_ref[...] += gi * hr - gr * hi
            return dhr + ar * gr + ai * gi, dhi - ai * gr + ar * gi

        zero = jnp.zeros((1, S5_STATES), F32)
        lax.fori_loop(0, nk, step, (zero, zero))

    vm = pl.BlockSpec(memory_space=pltpu.VMEM)
    return pl.pallas_call(
        body, name="s5_scan_bwd", in_specs=[vm] * 6, out_specs=[vm] * 4,
        out_shape=[_sds((nk, S5_STATES)), _sds((nk, S5_STATES)), _sds((1, S5_STATES)), _sds((1, S5_STATES))],
        input_output_aliases={0: 0, 1: 1}, compiler_params=_params(),
    )(dh_re, dh_im, h_re, h_im, a_re, a_im)


def s5_input_grads(dy, ds_re, ds_im, kd, ws_re, ws_im):
    t = dy.shape[0]
    nk = t // S5_CHUNK

    def body(dy_ref, dsr_ref, dsi_ref, kd_ref, wsr_ref, wsi_ref, du_ref, win_ref):
        _fill_toeplitz(win_ref, kd_ref)
        du = _mm_nt(_chunk_rows(dy_ref, nk), win_ref[...])
        du = du + _mm_nt(dsr_ref[...], _spread_groups(wsr_ref[...])) + _mm_nt(dsi_ref[...], _spread_groups(wsi_ref[...]))
        _store_chunk_rows(du_ref, du, nk)

    return pl.pallas_call(
        body, name="s5_input_grads", grid=(S5_OCTETS,),
        in_specs=[_strip(t), _oct_states(nk), _oct_states(nk), OCT_KD, OCT_W, OCT_W],
        out_specs=_strip(t), out_shape=_sds((t, BRANCH)),
        scratch_shapes=[pltpu.VMEM((S5_OCT_IN, S5_OCT_IN), MXU_DTYPE)],
        compiler_params=_params(("parallel",)),
    )(dy, ds_re, ds_im, kd, ws_re, ws_im)


def s5_weight_grads(u, dy, h_re, h_im, ds_re, ds_im):
    t = u.shape[0]
    nk = t // S5_CHUNK

    def body(u_ref, dy_ref, hre_ref, him_ref, dsr_ref, dsi_ref, dkd_ref, dwsr_ref, dwsi_ref, dwor_ref, dwoi_ref):
        dyc = _chunk_rows(dy_ref, nk, F32)
        uct = _chunk_rows(u_ref, nk, F32).T.astype(MXU_DTYPE)
        dyct = dyc.T.astype(MXU_DTYPE)
        dyc = dyc.astype(MXU_DTYPE)
        dwsr_ref[...] = _fold_groups(_mm(uct, dsr_ref[...]))
        dwsi_ref[...] = _fold_groups(_mm(uct, dsi_ref[...]))
        dwor_ref[...] = _fold_groups(_mm(dyct, hre_ref[...]))
        dwoi_ref[...] = _fold_groups(_mm(dyct, him_ref[...]))
        dkd_ref[...] = jnp.zeros_like(dkd_ref)
        for tt in range(S5_CHUNK):
            p = _mm(uct, dyc[:, tt * LANES:(tt + 1) * LANES])
            for s in range(tt + 1):
                dkd_ref[tt - s] += p[s * LANES:(s + 1) * LANES]

    return pl.pallas_call(
        body, name="s5_weight_grads", grid=(S5_OCTETS,),
        in_specs=[_strip(t), _strip(t)] + [_oct_states(nk)] * 4,
        out_specs=[OCT_KD, OCT_W, OCT_W, OCT_W, OCT_W],
        out_shape=[_sds((S5_OCTETS, S5_CHUNK, LANES, LANES))] + [_sds((S5_OCTETS, S5_OCT_IN, LANES))] * 4,
        compiler_params=_params(("parallel",)),
    )(u, dy, h_re, h_im, ds_re, ds_im)


def ssm_mix_fwd(x, u, gate, y_scan, d, w_glu, b_glu, w_out):
    t = x.shape[0]
    tm = min(ROW_TILE_FWD, t)

    def body(x_ref, u_ref, gate_ref, ys_ref, d_ref, wg_ref, bg_ref, wo_ref, y_ref, g2_ref, xo_ref):
        y = ys_ref[...] + d_ref[...] * u_ref[...]
        z0 = _gelu(y)
        g2 = _mm(z0, wg_ref[...]) + bg_ref[...]
        a = z0 * _sigmoid(g2) * _silu(gate_ref[...])
        y_ref[...] = y
        g2_ref[...] = g2
        xo_ref[...] = x_ref[...] + _mm(a, wo_ref[...])

    row = _rows(tm, BRANCH)
    vec = _whole((1, BRANCH))
    mat = _whole((BRANCH, BRANCH))
    return pl.pallas_call(
        body, name="ssm_mix_fwd", grid=(t // tm,),
        in_specs=[row, row, row, row, vec, mat, vec, mat],
        out_specs=[row, row, row],
        out_shape=[_sds((t, BRANCH))] * 3,
        compiler_params=_params(("parallel",)),
    )(x, u, gate, y_scan, d, w_glu, b_glu, w_out)


def ssm_mix_bwd(dxo, u, gate, y, g2, w_glu, w_out):
    t = dxo.shape[0]
    tm = min(ROW_TILE_BWD, t)

    def body(dxo_ref, u_ref, gate_ref, y_ref, g2_ref, wgt_ref, wot_ref, dy_ref, dgate_ref, dwo_ref, dwg_ref, dbg_ref, dd_ref):
        @pl.when(pl.program_id(0) == 0)
        def _():
            dwo_ref[...] = jnp.zeros_like(dwo_ref)
            dwg_ref[...] = jnp.zeros_like(dwg_ref)
            dbg_ref[...] = jnp.zeros_like(dbg_ref)
            dd_ref[...] = jnp.zeros_like(dd_ref)

        dxo = dxo_ref[...]
        gate = gate_ref[...]
        y = y_ref[...]
        z0 = _gelu(y)
        sg = _sigmoid(g2_ref[...])
        z = z0 * sg
        sgate = _silu(gate)
        da = _mm_nt(dxo, wot_ref[...])
        dwo_ref[...] += _mm_tn(z * sgate, dxo)
        dz = da * sgate
        dgate_ref[...] = da * z * _silu_grad(gate)
        dg2 = dz * z0 * sg * (1.0 - sg)
        dbg_ref[...] += jnp.sum(dg2, axis=0, keepdims=True)
        dwg_ref[...] += _mm_tn(z0, dg2)
        dz0 = dz * sg + _mm_nt(dg2, wgt_ref[...])
        dy = dz0 * _gelu_grad(y)
        dd_ref[...] += jnp.sum(dy * u_ref[...], axis=0, keepdims=True)
        dy_ref[...] = dy

    row = _rows(tm, BRANCH)
    vec = _whole((1, BRANCH))
    mat = _whole((BRANCH, BRANCH))
    return pl.pallas_call(
        body, name="ssm_mix_bwd", grid=(t // tm,),
        in_specs=[row, row, row, row, row, mat, mat],
        out_specs=[row, row, mat, mat, vec, vec],
        out_shape=[_sds((t, BRANCH)), _sds((t, BRANCH)), _sds((BRANCH, D_MODEL)), _sds((BRANCH, BRANCH)),
                   _sds((1, BRANCH)), _sds((1, BRANCH))],
        compiler_params=_params(("arbitrary",)),
    )(dxo, u, gate, y, g2, w_glu, w_out)


def ssm_proj_bwd(x, norm, dxo, dy, du_scan, dgate, d, w_in):
    t = x.shape[0]
    tm = min(ROW_TILE_BWD, t)
    n = 2 * BRANCH

    def body(x_ref, g_ref, dxo_ref, dy_ref, dus_ref, dgate_ref, d_ref, wt_ref, dx_ref, dw_ref, dg_ref):
        @pl.when(pl.program_id(0) == 0)
        def _():
            dw_ref[...] = jnp.zeros_like(dw_ref)
            dg_ref[...] = jnp.zeros_like(dg_ref)

        g = g_ref[...]
        r, xhat, h = _rms(x_ref[...], g)
        du = dus_ref[...] + d_ref[...] * dy_ref[...]
        dproj = jnp.concatenate([du, dgate_ref[...]], axis=1)
        dh = _mm_nt(dproj, wt_ref[...])
        dw_ref[...] += _mm_tn(h, dproj)
        dx, dg = _rms_bwd(dh, g, r, xhat)
        dg_ref[...] += dg
        dx_ref[...] = dxo_ref[...] + dx

    row = _rows(tm, D_MODEL)
    vec = _whole((1, D_MODEL))
    return pl.pallas_call(
        body, name="ssm_proj_bwd", grid=(t // tm,),
        in_specs=[row, vec, row, row, row, row, vec, _whole((D_MODEL, n))],
        out_specs=[row, _whole((D_MODEL, n)), vec],
        out_shape=[_sds((t, D_MODEL)), _sds((D_MODEL, n)), _sds((1, D_MODEL))],
        compiler_params=_params(("arbitrary",)),
    )(x, norm, dxo, dy, du_scan, dgate, d, w_in)


ATTN_N = Q_DIM + 2 * KV_DIM + BRANCH


def attn_proj_fwd(x, norm, w_in, cos2, sin2):
    t = x.shape[0]
    tm = min(ROW_TILE_FWD, t)

    def body(x_ref, g_ref, w_ref, cos_ref, sin_ref, q_ref, k_ref, v_ref, gate_ref):
        _, _, h = _rms(x_ref[...], g_ref[...])
        p = _mm(h, w_ref[...])
        cs = cos_ref[...]
        sn = sin_ref[...]
        q = p[:, :Q_DIM]
        k = p[:, Q_DIM:Q_DIM + KV_DIM]
        q_ref[...] = q * _tile_lanes(cs, Q_DIM // LANES) + _swap_half_heads(q) * _tile_lanes(sn, Q_DIM // LANES)
        k_ref[...] = k * cs + _swap_half_heads(k) * sn
        v_ref[...] = p[:, Q_DIM + KV_DIM:Q_DIM + 2 * KV_DIM]
        gate_ref[...] = p[:, Q_DIM + 2 * KV_DIM:]

    return pl.pallas_call(
        body, name="attn_proj_fwd", grid=(t // tm,),
        in_specs=[_rows(tm, D_MODEL), _whole((1, D_MODEL)), _whole((D_MODEL, ATTN_N)), _rows(tm, LANES), _rows(tm, LANES)],
        out_specs=[_rows(tm, Q_DIM), _rows(tm, KV_DIM), _rows(tm, KV_DIM), _rows(tm, BRANCH)],
        out_shape=[_sds((t, Q_DIM)), _sds((t, KV_DIM)), _sds((t, KV_DIM)), _sds((t, BRANCH))],
        compiler_params=_params(("parallel",)),
    )(x, norm, w_in, cos2, sin2)


def _band_mask(first_block):
    qi = lax.broadcasted_iota(jnp.int32, (ATTN_BLOCK, 2 * ATTN_BLOCK), 0)
    kj = lax.broadcasted_iota(jnp.int32, (ATTN_BLOCK, 2 * ATTN_BLOCK), 1)
    dist = qi + ATTN_BLOCK - kj
    first_key = jnp.where(first_block, ATTN_BLOCK, 0)
    return (dist >= 0) & (dist < ATTN_BLOCK) & (kj >= first_key)


def _lane_is(h):
    return lax.broadcasted_iota(jnp.int32, (1, LANES), 1) == h


def attn_fwd(q, k, v, sinks):
    t = q.shape[0]
    nb = t // ATTN_BLOCK
    scale = HEAD_DIM ** -0.5

    def body(sink_ref, q_ref, kc_ref, kp_ref, vc_ref, vp_ref, o_ref, lse_ref):
        i = pl.program_id(0)
        keys = jnp.concatenate([kp_ref[...], kc_ref[...]], axis=0).astype(MXU_DTYPE)
        vals = jnp.concatenate([vp_ref[...], vc_ref[...]], axis=0).astype(MXU_DTYPE)
        valid = _band_mask(i == 0)
        lse = jnp.zeros((ATTN_BLOCK, LANES), F32)
        for h in range(N_Q_HEADS):
            hk = h // GQA_GROUP
            kh = keys[:, hk * HEAD_DIM:(hk + 1) * HEAD_DIM]
            vh = vals[:, hk * HEAD_DIM:(hk + 1) * HEAD_DIM]
            qh = q_ref[:, h * HEAD_DIM:(h + 1) * HEAD_DIM]
            s = jnp.where(valid, _mm_nt(qh, kh) * scale, NEG_INF)
            sink = sink_ref[h]
            m = jnp.maximum(jnp.max(s, axis=-1, keepdims=True), sink)
            p = jnp.exp(s - m)
            den = jnp.sum(p, axis=-1, keepdims=True) + jnp.exp(sink - m)
            o_ref[:, h * HEAD_DIM:(h + 1) * HEAD_DIM] = _mm(p, vh) / den
            lse = jnp.where(_lane_is(h), m + jnp.log(den), lse)
        lse_ref[...] = lse

    cur = lambda n: pl.BlockSpec((ATTN_BLOCK, n), lambda i: (i, 0))
    prev = lambda n: pl.BlockSpec((ATTN_BLOCK, n), lambda i: (jnp.maximum(i - 1, 0), 0))
    return pl.pallas_call(
        body, name="attn_fwd", grid=(nb,),
        in_specs=[pl.BlockSpec(memory_space=pltpu.SMEM), cur(Q_DIM), cur(KV_DIM), prev(KV_DIM), cur(KV_DIM), prev(KV_DIM)],
        out_specs=[cur(Q_DIM), cur(LANES)],
        out_shape=[_sds((t, Q_DIM)), _sds((t, LANES))],
        compiler_params=_params(("parallel",)),
    )(sinks, q, k, k, v, v)


def attn_bwd(q, k, v, sinks, o, lse, do):
    t = q.shape[0]
    nb = t // ATTN_BLOCK
    scale = HEAD_DIM ** -0.5

    def body(sink_ref, q_ref, kc_ref, kp_ref, vc_ref, vp_ref, o_ref, lse_ref, do_ref,
             dq_ref, dk_ref, dv_ref, dsink_ref, dk_carry, dv_carry):
        i = pl.program_id(0)

        @pl.when(i == 0)
        def _():
            dsink_ref[...] = jnp.zeros_like(dsink_ref)
            dk_carry[...] = jnp.zeros_like(dk_carry)
            dv_carry[...] = jnp.zeros_like(dv_carry)

        @pl.when(i < nb)
        def _():
            keys = jnp.concatenate([kp_ref[...], kc_ref[...]], axis=0).astype(MXU_DTYPE)
            vals = jnp.concatenate([vp_ref[...], vc_ref[...]], axis=0).astype(MXU_DTYPE)
            valid = _band_mask(i == 0)
            lse_all = lse_ref[...]
            dsink = jnp.zeros((1, LANES), F32)
            dk_heads = []
            dv_heads = []
            for hk in range(N_KV_HEADS):
                kh = keys[:, hk * HEAD_DIM:(hk + 1) * HEAD_DIM]
                vh = vals[:, hk * HEAD_DIM:(hk + 1) * HEAD_DIM]
                dkk = jnp.zeros((2 * ATTN_BLOCK, HEAD_DIM), F32)
                dvv = jnp.zeros((2 * ATTN_BLOCK, HEAD_DIM), F32)
                for hq in range(GQA_GROUP):
                    h = hk * GQA_GROUP + hq
                    sl = slice(h * HEAD_DIM, (h + 1) * HEAD_DIM)
                    qh = q_ref[:, sl]
                    doh = do_ref[:, sl]
                    lse_h = jnp.sum(jnp.where(_lane_is(h), lse_all, 0.0), axis=-1, keepdims=True)
                    s = jnp.where(valid, _mm_nt(qh, kh) * scale, NEG_INF)
                    p = jnp.exp(s - lse_h)
                    delta = jnp.sum(doh * o_ref[:, sl], axis=-1, keepdims=True)
                    dvv = dvv + _mm_tn(p, doh)
                    ds = p * (_mm_nt(doh, vh) - delta)
                    dq_ref[:, sl] = _mm(ds, kh) * scale
                    dkk = dkk + _mm_tn(ds, qh) * scale
                    dsink = dsink + jnp.where(_lane_is(h), -jnp.sum(jnp.exp(sink_ref[h] - lse_h) * delta), 0.0)
                dk_heads.append(dkk)
                dv_heads.append(dvv)
            dkk = jnp.concatenate(dk_heads, axis=1)
            dvv = jnp.concatenate(dv_heads, axis=1)
            dsink_ref[...] += dsink
            dk_ref[...] = dk_carry[...] + dkk[:ATTN_BLOCK]
            dv_ref[...] = dv_carry[...] + dvv[:ATTN_BLOCK]
            dk_carry[...] = dkk[ATTN_BLOCK:]
            dv_carry[...] = dvv[ATTN_BLOCK:]

        @pl.when(i == nb)
        def _():
            dk_ref[...] = dk_carry[...]
            dv_ref[...] = dv_carry[...]

    last = nb - 1
    cur = lambda n: pl.BlockSpec((ATTN_BLOCK, n), lambda i: (jnp.minimum(i, last), 0))
    prev = lambda n: pl.BlockSpec((ATTN_BLOCK, n), lambda i: (jnp.clip(i - 1, 0, last), 0))
    late = lambda n: pl.BlockSpec((ATTN_BLOCK, n), lambda i: (i, 0))
    dq, dk_late, dv_late, dsinks = pl.pallas_call(
        body, name="attn_bwd", grid=(nb + 1,),
        in_specs=[pl.BlockSpec(memory_space=pltpu.SMEM), cur(Q_DIM), cur(KV_DIM), prev(KV_DIM), cur(KV_DIM), prev(KV_DIM),
                  cur(Q_DIM), cur(LANES), cur(Q_DIM)],
        out_specs=[cur(Q_DIM), late(KV_DIM), late(KV_DIM), _whole((1, LANES))],
        out_shape=[_sds((t, Q_DIM)), _sds((t + ATTN_BLOCK, KV_DIM)), _sds((t + ATTN_BLOCK, KV_DIM)), _sds((1, LANES))],
        scratch_shapes=[pltpu.VMEM((ATTN_BLOCK, KV_DIM), F32), pltpu.VMEM((ATTN_BLOCK, KV_DIM), F32)],
        compiler_params=_params(("arbitrary",)),
    )(sinks, q, k, k, v, v, o, lse, do)
    return dq, dk_late[ATTN_BLOCK:], dv_late[ATTN_BLOCK:], dsinks


def attn_out_fwd(x, o, gate, w_out):
    t = x.shape[0]
    tm = min(ROW_TILE_FWD, t)

    def body(x_ref, o_ref, gate_ref, w_ref, xo_ref):
        xo_ref[...] = x_ref[...] + _mm(o_ref[...] * _silu(gate_ref[...]), w_ref[...])

    row = _rows(tm, D_MODEL)
    return pl.pallas_call(
        body, name="attn_out_fwd", grid=(t // tm,),
        in_specs=[row, row, row, _whole((Q_DIM, D_MODEL))], out_specs=row, out_shape=_sds((t, D_MODEL)),
        compiler_params=_params(("parallel",)),
    )(x, o, gate, w_out)


def attn_out_bwd(dxo, o, gate, w_out):
    t = dxo.shape[0]
    tm = min(ROW_TILE_BWD, t)

    def body(dxo_ref, o_ref, gate_ref, wt_ref, do_ref, dgate_ref, dw_ref):
        @pl.when(pl.program_id(0) == 0)
        def _():
            dw_ref[...] = jnp.zeros_like(dw_ref)

        dxo = dxo_ref[...]
        o = o_ref[...]
        gate = gate_ref[...]
        sgate = _silu(gate)
        da = _mm_nt(dxo, wt_ref[...])
        dw_ref[...] += _mm_tn(o * sgate, dxo)
        do_ref[...] = da * sgate
        dgate_ref[...] = da * o * _silu_grad(gate)

    row = _rows(tm, D_MODEL)
    mat = _whole((Q_DIM, D_MODEL))
    return pl.pallas_call(
        body, name="attn_out_bwd", grid=(t // tm,),
        in_specs=[row, row, row, mat], out_specs=[row, row, mat],
        out_shape=[_sds((t, Q_DIM)), _sds((t, BRANCH)), _sds((Q_DIM, D_MODEL))],
        compiler_params=_params(("arbitrary",)),
    )(dxo, o, gate, w_out)


def attn_proj_bwd(x, norm, dxo, dq, dk, dv, dgate, cos2, sin2, w_in):
    t = x.shape[0]
    tm = min(ROW_TILE_BWD, t)

    def body(x_ref, g_ref, dxo_ref, dq_ref, dk_ref, dv_ref, dgate_ref, cos_ref, sin_ref, wt_ref, dx_ref, dw_ref, dg_ref):
        @pl.when(pl.program_id(0) == 0)
        def _():
            dw_ref[...] = jnp.zeros_like(dw_ref)
            dg_ref[...] = jnp.zeros_like(dg_ref)

        g = g_ref[...]
        r, xhat, h = _rms(x_ref[...], g)
        cs = cos_ref[...]
        sn = sin_ref[...]
        dqr = dq_ref[...]
        dkr = dk_ref[...]
        dq = dqr * _tile_lanes(cs, Q_DIM // LANES) + _swap_half_heads(dqr * _tile_lanes(sn, Q_DIM // LANES))
        dk = dkr * cs + _swap_half_heads(dkr * sn)
        dproj = jnp.concatenate([dq, dk, dv_ref[...], dgate_ref[...]], axis=1)
        dh = _mm_nt(dproj, wt_ref[...])
        dw_ref[...] += _mm_tn(h, dproj)
        dx, dg = _rms_bwd(dh, g, r, xhat)
        dg_ref[...] += dg
        dx_ref[...] = dxo_ref[...] + dx

    row = _rows(tm, D_MODEL)
    vec = _whole((1, D_MODEL))
    return pl.pallas_call(
        body, name="attn_proj_bwd", grid=(t // tm,),
        in_specs=[row, vec, row, _rows(tm, Q_DIM), _rows(tm, KV_DIM), _rows(tm, KV_DIM), _rows(tm, BRANCH),
                  _rows(tm, LANES), _rows(tm, LANES), _whole((D_MODEL, ATTN_N))],
        out_specs=[row, _whole((D_MODEL, ATTN_N)), vec],
        out_shape=[_sds((t, D_MODEL)), _sds((D_MODEL, ATTN_N)), _sds((1, D_MODEL))],
        compiler_params=_params(("arbitrary",)),
    )(x, norm, dxo, dq, dk, dv, dgate, cos2, sin2, w_in)


def loss_head(x, norm, target):
    t = x.shape[0]
    tm = min(ROW_TILE_FWD, t)

    def body(x_ref, g_ref, tgt_ref, loss_ref, dx_ref, dg_ref):
        @pl.when(pl.program_id(0) == 0)
        def _():
            loss_ref[...] = jnp.zeros_like(loss_ref)
            dg_ref[...] = jnp.zeros_like(dg_ref)

        g = g_ref[...]
        r, xhat, y = _rms(x_ref[...], g)
        err = y - tgt_ref[...]
        loss_ref[...] += 0.5 * jnp.sum(jnp.mean(err * err, axis=-1, keepdims=True), axis=0, keepdims=True)
        dx, dg = _rms_bwd(err * (1.0 / D_MODEL), g, r, xhat)
        dg_ref[...] += dg
        dx_ref[...] = dx

    row = _rows(tm, D_MODEL)
    vec = _whole((1, D_MODEL))
    return pl.pallas_call(
        body, name="loss_head", grid=(t // tm,),
        in_specs=[row, vec, row], out_specs=[_whole((1, 1)), row, vec],
        out_shape=[_sds((1, 1)), _sds((t, D_MODEL)), _sds((1, D_MODEL))],
        compiler_params=_params(("arbitrary",)),
    )(x, norm, target)


OCT_TILE = pl.BlockSpec((None, LANES, LANES), lambda b: (b, 0, 0))
N_LAGS = S5_CHUNK + 1


def _cmul(ar, ai, br, bi):
    return ar * br - ai * bi, ar * bi + ai * br


def _cmul_conj(ar, ai, br, bi):
    return ar * br + ai * bi, ar * bi - ai * br


def _mm_f32(a, b, dims):
    return lax.dot_general(a, b, (dims, ((), ())), precision=lax.Precision.HIGHEST, preferred_element_type=F32)


def _s5_discretise(ar, ai, ls, br, bi):
    dt = jnp.exp(ls)
    xr = ar * dt
    xi = ai * dt
    mag = jnp.exp(xr)
    first = (mag * jnp.cos(xi), mag * jnp.sin(xi))
    powers = [(jnp.ones_like(xr), jnp.zeros_like(xr)), first]
    for _ in range(2, N_LAGS):
        powers.append(_cmul(*powers[-1], *first))
    den = ar * ar + ai * ai
    nr = powers[1][0] - 1.0
    ni = powers[1][1]
    fr = (nr * ar + ni * ai) / den
    fi = (ni * ar - nr * ai) / den
    bbr, bbi = _cmul(fr, fi, br, bi)
    return dt, powers, (fr, fi), (bbr, bbi), den


def _same_group_tile():
    row = lax.broadcasted_iota(jnp.int32, (LANES, LANES), 0)
    col = lax.broadcasted_iota(jnp.int32, (LANES, LANES), 1)
    return (row // SSM_GROUP) == (col // SSM_GROUP)


def _first_copy_lanes():
    return lax.broadcasted_iota(jnp.int32, (LANES, LANES), 1) < SSM_STATE


def s5_param_fwd(tiles):
    def body(ar_ref, ai_ref, ls_ref, br_ref, bi_ref, cr_ref, ci_ref, kd_ref, wsr_ref, wsi_ref, wor_ref, woi_ref, pr_ref, pi_ref):
        cr = cr_ref[...]
        ci = ci_ref[...]
        _, powers, _, (bbr, bbi), _ = _s5_discretise(ar_ref[...], ai_ref[...], ls_ref[...], br_ref[...], bi_ref[...])
        once = _first_copy_lanes()
        crm = jnp.where(once, cr, 0.0)
        cim = jnp.where(once, ci, 0.0)
        same = _same_group_tile()
        for lag in range(S5_CHUNK):
            er, ei = powers[lag]
            xr, xi = _cmul(er, ei, bbr, bbi)
            rows = pl.ds((S5_CHUNK - 1 - lag) * LANES, LANES)
            wsr_ref[rows, :] = xr
            wsi_ref[rows, :] = xi
            k = _mm_f32(xr, crm, ((1,), (1,))) - _mm_f32(xi, cim, ((1,), (1,)))
            kd_ref[lag] = jnp.where(same, k, 0.0)
        for t in range(S5_CHUNK):
            er, ei = powers[t + 1]
            zr, zi = _cmul(er, ei, cr, ci)
            wor_ref[pl.ds(t * LANES, LANES), :] = zr
            woi_ref[pl.ds(t * LANES, LANES), :] = -zi
        pr_ref[...] = powers[S5_CHUNK][0]
        pi_ref[...] = powers[S5_CHUNK][1]

    return pl.pallas_call(
        body, name="s5_param_fwd", grid=(S5_OCTETS,),
        in_specs=[OCT_TILE] * 7, out_specs=[OCT_KD, OCT_W, OCT_W, OCT_W, OCT_W, OCT_TILE, OCT_TILE],
        out_shape=[_sds((S5_OCTETS, S5_CHUNK, LANES, LANES))] + [_sds((S5_OCTETS, S5_OCT_IN, LANES))] * 4
                  + [_sds((S5_OCTETS, LANES, LANES))] * 2,
        compiler_params=_params(("parallel",)),
    )(*tiles)


def s5_param_bwd(tiles, dkd, dws_re, dws_im, dwo_re, dwo_im, dp_re, dp_im):
    def body(ar_ref, ai_ref, ls_ref, br_ref, bi_ref, cr_ref, ci_ref, dkd_ref, dwsr_ref, dwsi_ref, dwor_ref, dwoi_ref, dpr_ref, dpi_ref,
             dar_ref, dai_ref, dls_ref, dbr_ref, dbi_ref, dcr_ref, dci_ref):
        ar = ar_ref[...]
        ai = ai_ref[...]
        br = br_ref[...]
        bi = bi_ref[...]
        cr = cr_ref[...]
        ci = ci_ref[...]
        dt, powers, (fr, fi), (bbr, bbi), den = _s5_discretise(ar, ai, ls_ref[...], br, bi)
        once = _first_copy_lanes()
        crm = jnp.where(once, cr, 0.0)
        cim = jnp.where(once, ci, 0.0)
        same = _same_group_tile()
        zero = jnp.zeros((LANES, LANES), F32)
        dpow = [[zero, zero] for _ in range(N_LAGS)]
        dbbr, dbbi, dcr, dci = zero, zero, zero, zero
        for lag in range(S5_CHUNK):
            er, ei = powers[lag]
            xr, xi = _cmul(er, ei, bbr, bbi)
            rows = pl.ds((S5_CHUNK - 1 - lag) * LANES, LANES)
            g = jnp.where(same, dkd_ref[lag], 0.0)
            dxr = dwsr_ref[rows, :] + _mm_f32(g, crm, ((1,), (0,)))
            dxi = dwsi_ref[rows, :] - _mm_f32(g, cim, ((1,), (0,)))
            dcr = dcr + jnp.where(once, _mm_f32(g, xr, ((0,), (0,))), 0.0)
            dci = dci - jnp.where(once, _mm_f32(g, xi, ((0,), (0,))), 0.0)
            a, b = _cmul_conj(bbr, bbi, dxr, dxi)
            dpow[lag][0] = dpow[lag][0] + a
            dpow[lag][1] = dpow[lag][1] + b
            a, b = _cmul_conj(er, ei, dxr, dxi)
            dbbr = dbbr + a
            dbbi = dbbi + b
        for t in range(S5_CHUNK):
            er, ei = powers[t + 1]
            dzr = dwor_ref[pl.ds(t * LANES, LANES), :]
            dzi = -dwoi_ref[pl.ds(t * LANES, LANES), :]
            a, b = _cmul_conj(cr, ci, dzr, dzi)
            dpow[t + 1][0] = dpow[t + 1][0] + a
            dpow[t + 1][1] = dpow[t + 1][1] + b
            a, b = _cmul_conj(er, ei, dzr, dzi)
            dcr = dcr + a
            dci = dci + b
        dpow[S5_CHUNK][0] = dpow[S5_CHUNK][0] + dpr_ref[...]
        dpow[S5_CHUNK][1] = dpow[S5_CHUNK][1] + dpi_ref[...]
        dfr, dfi = _cmul_conj(br, bi, dbbr, dbbi)
        dbr, dbi = _cmul_conj(fr, fi, dbbr, dbbi)
        dnr, dni = _cmul(ar / den, ai / den, dfr, dfi)
        qr = (fr * ar + fi * ai) / den
        qi = (fi * ar - fr * ai) / den
        dlr, dli = _cmul(-qr, qi, dfr, dfi)
        dpow[1][0] = dpow[1][0] + dnr
        dpow[1][1] = dpow[1][1] + dni
        dxr, dxi = zero, zero
        for lag in range(1, N_LAGS):
            a, b = _cmul_conj(powers[lag][0], powers[lag][1], dpow[lag][0], dpow[lag][1])
            dxr = dxr + lag * a
            dxi = dxi + lag * b
        dar_ref[...] = dlr + dt * dxr
        dai_ref[...] = dli + dt * dxi
        dls_ref[...] = dt * (ar * dxr + ai * dxi)
        dbr_ref[...] = dbr
        dbi_ref[...] = dbi
        dcr_ref[...] = dcr
        dci_ref[...] = dci

    return pl.pallas_call(
        body, name="s5_param_bwd", grid=(S5_OCTETS,),
        in_specs=[OCT_TILE] * 7 + [OCT_KD, OCT_W, OCT_W, OCT_W, OCT_W, OCT_TILE, OCT_TILE], out_specs=[OCT_TILE] * 7,
        out_shape=[_sds((S5_OCTETS, LANES, LANES))] * 7,
        compiler_params=_params(("parallel",)),
    )(*tiles, dkd, dws_re, dws_im, dwo_re, dwo_im, dp_re, dp_im)


def _doubled(v):
    return jnp.concatenate([v, v], axis=-1)


def _s5_param_tiles(a_re, a_im, log_step, b_re, b_im, c_re, c_im):
    def per_group(a):
        return _doubled(jnp.broadcast_to(a.reshape(S5_OCTETS, S5_OCT, 1, SSM_STATE),
                                         (S5_OCTETS, S5_OCT, SSM_GROUP, SSM_STATE)).reshape(S5_OCTETS, LANES, SSM_STATE))

    ls = jnp.broadcast_to(log_step.reshape(S5_OCTETS, S5_OCT, 1, 1), (S5_OCTETS, S5_OCT, SSM_GROUP, LANES)).reshape(S5_OCTETS, LANES, LANES)
    bt = lambda b: _doubled(b.transpose(0, 2, 1).reshape(S5_OCTETS, LANES, SSM_STATE))
    ct = lambda c: _doubled(c.reshape(S5_OCTETS, LANES, SSM_STATE))
    return [per_group(a_re), per_group(a_im), ls, bt(b_re), bt(b_im), ct(c_re), ct(c_im)]


def _s5_param_grads(dtiles):
    dar, dai, dls, dbr, dbi, dcr, dci = dtiles
    halves = lambda d: d[..., :SSM_STATE] + d[..., SSM_STATE:]
    per_group = lambda d: halves(d).reshape(SSM_GROUPS, SSM_GROUP, SSM_STATE).sum(axis=1)
    per_row = lambda d: halves(d).reshape(SSM_GROUPS, SSM_GROUP, SSM_STATE)
    return (per_group(dar), per_group(dai), dls.reshape(SSM_GROUPS, SSM_GROUP * LANES).sum(axis=1),
            per_row(dbr).transpose(0, 2, 1), per_row(dbi).transpose(0, 2, 1), per_row(dcr), per_row(dci))


def _group_power_rows(tile):
    return tile[:, ::SSM_GROUP, :SSM_STATE].reshape(1, S5_STATES)


def _group_power_tiles(row):
    t = jnp.pad(row.reshape(S5_OCTETS, S5_OCT, 1, SSM_STATE), ((0, 0), (0, 0), (0, SSM_GROUP - 1), (0, LANES - SSM_STATE)))
    return t.reshape(S5_OCTETS, LANES, LANES)


def _rope_tables(t):
    pos = jnp.arange(t, dtype=F32)
    inv_freq = ROPE_THETA ** (-jnp.arange(0, HEAD_DIM, 2, dtype=F32) / HEAD_DIM)
    ang = pos[:, None] * inv_freq[None, :]
    cos = jnp.cos(ang)
    sin = jnp.sin(ang)
    cos64 = jnp.concatenate([cos, cos], axis=1)
    sin64 = jnp.concatenate([-sin, sin], axis=1)
    return jnp.concatenate([cos64, cos64], axis=1), jnp.concatenate([sin64, sin64], axis=1)


def _row(v):
    return v.reshape(1, -1)


def _ssm_forward(x, w):
    tiles = _s5_param_tiles(w["a_re"], w["a_im"], w["log_step"], w["b_re"], w["b_im"], w["c_re"], w["c_im"])
    kd, ws_re, ws_im, wo_re, wo_im, p_re, p_im = s5_param_fwd(tiles)
    mats = dict(kd=kd, ws_re=ws_re, ws_im=ws_im, wo_re=wo_re, wo_im=wo_im, a_re=_group_power_rows(p_re), a_im=_group_power_rows(p_im))
    u, gate = ssm_proj_fwd(x, _row(w["norm"]), w["w_in"])
    s_re, s_im = s5_chunk_states(u, mats["ws_re"], mats["ws_im"])
    h_re, h_im = s5_scan_fwd(s_re, s_im, mats["a_re"], mats["a_im"])
    y_scan = s5_outputs(u, h_re, h_im, mats["kd"], mats["wo_re"], mats["wo_im"])
    y, g2, x_new = ssm_mix_fwd(x, u, gate, y_scan, _row(w["d"]), w["w_glu"], _row(w["b_glu"]), w["w_out"])
    saved = dict(x=x, u=u, gate=gate, y=y, g2=g2, h_re=h_re, h_im=h_im, mats=mats, tiles=tiles)
    return x_new, saved


def _ssm_backward(dxo, w, s):
    dy, dgate, dw_out, dw_glu, db_glu, dd = ssm_mix_bwd(dxo, s["u"], s["gate"], s["y"], s["g2"], w["w_glu"], w["w_out"])
    mats = s["mats"]
    dh_re, dh_im = s5_state_grads(dy, mats["wo_re"], mats["wo_im"])
    ds_re, ds_im, da_re, da_im = s5_scan_bwd(dh_re, dh_im, s["h_re"], s["h_im"], mats["a_re"], mats["a_im"])
    du_scan = s5_input_grads(dy, ds_re, ds_im, mats["kd"], mats["ws_re"], mats["ws_im"])
    dkd, dws_re, dws_im, dwo_re, dwo_im = s5_weight_grads(s["u"], dy, s["h_re"], s["h_im"], ds_re, ds_im)
    dparams = _s5_param_grads(s5_param_bwd(s["tiles"], dkd, dws_re, dws_im, dwo_re, dwo_im,
                                           _group_power_tiles(da_re), _group_power_tiles(da_im)))
    dx, dw_in, dnorm = ssm_proj_bwd(s["x"], _row(w["norm"]), dxo, dy, du_scan, dgate, _row(w["d"]), w["w_in"])
    grads = dict(norm=dnorm, w_in=dw_in, d=dd, w_glu=dw_glu, b_glu=db_glu, w_out=dw_out)
    for name, val in zip(("a_re", "a_im", "log_step", "b_re", "b_im", "c_re", "c_im"), dparams):
        grads[name] = val
    return dx, grads


def _attn_forward(x, w, cos2, sin2):
    q, k, v, gate = attn_proj_fwd(x, _row(w["norm"]), w["w_in"], cos2, sin2)
    o, lse = attn_fwd(q, k, v, w["sinks"])
    x_new = attn_out_fwd(x, o, gate, w["w_out"])
    return x_new, dict(x=x, q=q, k=k, v=v, gate=gate, o=o, lse=lse)


def _attn_backward(dxo, w, s, cos2, sin2):
    do, dgate, dw_out = attn_out_bwd(dxo, s["o"], s["gate"], w["w_out"])
    dq, dk, dv, dsinks = attn_bwd(s["q"], s["k"], s["v"], w["sinks"], s["o"], s["lse"], do)
    dx, dw_in, dnorm = attn_proj_bwd(s["x"], _row(w["norm"]), dxo, dq, dk, dv, dgate, cos2, sin2, w["w_in"])
    return dx, dict(norm=dnorm, w_in=dw_in, sinks=dsinks[0, :N_Q_HEADS], w_out=dw_out)


def _sequence_step(x, target, layers, final_norm):
    cos2, sin2 = _rope_tables(x.shape[0])
    saved = []
    for i, w in enumerate(layers):
        if i % 2 == 0:
            x, s = _ssm_forward(x, w)
        else:
            x, s = _attn_forward(x, w, cos2, sin2)
        saved.append(s)
    loss, dx, dfinal = loss_head(x, _row(final_norm), target)
    grads = {"final_norm": dfinal}
    for i in reversed(range(len(layers))):
        if i % 2 == 0:
            dx, g = _ssm_backward(dx, layers[i], saved[i])
        else:
            dx, g = _attn_backward(dx, layers[i], saved[i], cos2, sin2)
        for name, val in g.items():
            grads["l%d_%s" % (i, name)] = val
    return loss[0, 0], dx, grads


ANY = pl.BlockSpec(memory_space=pl.ANY)


def _place():
    return lax.axis_index("x"), lax.axis_index("y"), lax.axis_index("c")


def _other_chips(x, y):
    return [(1 - x, y), (x, 1 - y), (1 - x, 1 - y)]


class _StagedCopies:
    def __init__(self, bufs, load_sems, store_sems):
        self.bufs, self.load_sems, self.store_sems = bufs, load_sems, store_sems
        self.loads, self.stores = [], []

    def load(self, i, src):
        cp = pltpu.make_async_copy(src, self.bufs[i], self.load_sems.at[i])
        cp.start()
        self.loads.append(cp)

    def store(self, i, dst):
        self.loads[i].wait()
        cp = pltpu.make_async_copy(self.bufs[i], dst, self.store_sems.at[i])
        cp.start()
        self.stores.append(cp)

    def finish(self):
        for cp in self.stores:
            cp.wait()


def _staging(blocks):
    n = len(blocks)
    return [pltpu.VMEM(b.shape, b.dtype) for b in blocks] + [pltpu.SemaphoreType.DMA((n,)), pltpu.SemaphoreType.DMA((n,))]


def gather_weight_shards(shards):
    n = len(shards)

    def body(*refs):
        ins, outs = refs[:n], refs[n:2 * n]
        send_sems, recv_sems, pass_send_sems, pass_recv_sems = refs[2 * n:2 * n + 4]
        own = _StagedCopies(refs[2 * n + 4:3 * n + 4], *refs[3 * n + 4:])
        x, y, c = _place()
        me = 2 * x + y
        chips = _other_chips(x, y)

        def half(i, block, which):
            rows = ins[i].shape[0] // 2
            return outs[i].at[block, pl.ds(which * rows, rows), :]

        def my_half(i):
            rows = ins[i].shape[0] // 2
            return ins[i].at[pl.ds(c * rows, rows), :]

        for i in range(n):
            own.load(i, ins[i])
        sends = []
        for i in range(n):
            for k, (tx, ty) in enumerate(chips):
                cp = pltpu.make_async_remote_copy(src_ref=my_half(i), dst_ref=half(i, me, c), send_sem=send_sems.at[i, k],
                                                  recv_sem=recv_sems.at[i, k], device_id=(tx, ty, c), device_id_type=MESH)
                cp.start()
                sends.append(cp)
        for i in range(n):
            own.store(i, outs[i].at[me])
        for i in range(n):
            for k, (tx, ty) in enumerate(chips):
                landed = half(i, 2 * tx + ty, c)
                pltpu.make_async_remote_copy(src_ref=my_half(i), dst_ref=landed, send_sem=send_sems.at[i, k],
                                             recv_sem=recv_sems.at[i, k], device_id=(tx, ty, c), device_id_type=MESH).wait_recv()
                cp = pltpu.make_async_remote_copy(src_ref=landed, dst_ref=landed, send_sem=pass_send_sems.at[i, k],
                                                  recv_sem=pass_recv_sems.at[i, k], device_id=(x, y, 1 - c), device_id_type=MESH)
                cp.start()
                sends.append(cp)
        for i in range(n):
            for k, (tx, ty) in enumerate(chips):
                missing = half(i, 2 * tx + ty, 1 - c)
                pltpu.make_async_remote_copy(src_ref=missing, dst_ref=missing, send_sem=pass_send_sems.at[i, k],
                                             recv_sem=pass_recv_sems.at[i, k], device_id=(x, y, 1 - c), device_id_type=MESH).wait_recv()
        for cp in sends:
            cp.wait_send()
        own.finish()

    sems = pltpu.SemaphoreType.DMA((n, 3))
    return pl.pallas_call(
        body, name="gather_weight_shards",
        in_specs=[ANY] * n, out_specs=[ANY] * n,
        out_shape=[_sds((4,) + s.shape, s.dtype) for s in shards],
        scratch_shapes=[sems, sems, sems, sems] + _staging(shards),
        compiler_params=_params(),
    )(*shards)


def exchange_halves_with_sibling(grads):
    n = len(grads)

    def body(*refs):
        ins, outs = refs[:n], refs[n:2 * n]
        send_sems, recv_sems = refs[2 * n:]
        x, y, c = _place()
        copies = []
        for i in range(n):
            half = ins[i].shape[1] // 2
            src = ins[i].at[:, pl.ds((1 - c) * half, half), :]
            cp = pltpu.make_async_remote_copy(src_ref=src, dst_ref=outs[i], send_sem=send_sems.at[i], recv_sem=recv_sems.at[i],
                                              device_id=(x, y, 1 - c), device_id_type=MESH)
            cp.start()
            copies.append(cp)
        for cp in copies:
            cp.wait()

    return pl.pallas_call(
        body, name="exchange_halves_with_sibling",
        in_specs=[ANY] * n, out_specs=[ANY] * n,
        out_shape=[_sds((g.shape[0], g.shape[1] // 2, g.shape[2])) for g in grads],
        scratch_shapes=[pltpu.SemaphoreType.DMA((n,)), pltpu.SemaphoreType.DMA((n,))],
    )(*grads)


def scatter_blocks_to_chips(sums):
    n = len(sums)

    def body(*refs):
        ins, outs = refs[:n], refs[n:2 * n]
        send_sems, recv_sems = refs[2 * n:2 * n + 2]
        own = _StagedCopies(refs[2 * n + 2:3 * n + 2], *refs[3 * n + 2:])
        x, y, c = _place()
        me = 2 * x + y

        def block_for(i, chip):
            return ins[i].at[chip] if ins[i].shape[0] == 4 else ins[i].at[0]

        for i in range(n):
            own.load(i, block_for(i, me))
        sends = []
        for i in range(n):
            for k, (tx, ty) in enumerate(_other_chips(x, y)):
                cp = pltpu.make_async_remote_copy(src_ref=block_for(i, 2 * tx + ty), dst_ref=outs[i].at[me], send_sem=send_sems.at[i, k],
                                                  recv_sem=recv_sems.at[i, k], device_id=(tx, ty, c), device_id_type=MESH)
                cp.start()
                sends.append(cp)
        for i in range(n):
            own.store(i, outs[i].at[me])
        for i in range(n):
            for k, (tx, ty) in enumerate(_other_chips(x, y)):
                pltpu.make_async_remote_copy(src_ref=block_for(i, me), dst_ref=outs[i].at[2 * tx + ty], send_sem=send_sems.at[i, k],
                                             recv_sem=recv_sems.at[i, k], device_id=(tx, ty, c), device_id_type=MESH).wait_recv()
        for cp in sends:
            cp.wait_send()
        own.finish()

    return pl.pallas_call(
        body, name="scatter_blocks_to_chips",
        in_specs=[ANY] * n, out_specs=[ANY] * n,
        out_shape=[_sds((4,) + s.shape[1:], s.dtype) for s in sums],
        scratch_shapes=[pltpu.SemaphoreType.DMA((n, 3)), pltpu.SemaphoreType.DMA((n, 3))] + _staging([_sds(s.shape[1:], s.dtype) for s in sums]),
        compiler_params=_params(),
    )(*sums)


def swap_halves_with_sibling(pieces):
    n = len(pieces)

    def body(*refs):
        ins, outs = refs[:n], refs[n:2 * n]
        send_sems, recv_sems = refs[2 * n:2 * n + 2]
        own = _StagedCopies(refs[2 * n + 2:3 * n + 2], *refs[3 * n + 2:])
        x, y, c = _place()
        for i in range(n):
            own.load(i, ins[i])
        swaps = []
        for i in range(n):
            cp = pltpu.make_async_remote_copy(src_ref=ins[i], dst_ref=outs[i].at[c], send_sem=send_sems.at[i], recv_sem=recv_sems.at[i],
                                              device_id=(x, y, 1 - c), device_id_type=MESH)
            cp.start()
            swaps.append(cp)
        for i in range(n):
            own.store(i, outs[i].at[c])
        for i in range(n):
            pltpu.make_async_remote_copy(src_ref=ins[i], dst_ref=outs[i].at[1 - c], send_sem=send_sems.at[i], recv_sem=recv_sems.at[i],
                                         device_id=(x, y, 1 - c), device_id_type=MESH).wait_recv()
        for cp in swaps:
            cp.wait_send()
        own.finish()

    return pl.pallas_call(
        body, name="swap_halves_with_sibling",
        in_specs=[ANY] * n, out_specs=[ANY] * n,
        out_shape=[_sds((2,) + p.shape) for p in pieces],
        scratch_shapes=[pltpu.SemaphoreType.DMA((n,)), pltpu.SemaphoreType.DMA((n,))] + _staging(pieces),
        compiler_params=_params(),
    )(*pieces)


def _row_tile(rows, cols):
    tm = rows
    while tm * cols * 4 > (2 << 20) and tm % 16 == 0:
        tm //= 2
    return tm


def add_pair(a, b, out_dtype):
    nb, rows, cols = a.shape
    tm = _row_tile(rows, cols)

    def body(a_ref, b_ref, o_ref):
        o_ref[...] = (a_ref[...] + b_ref[...]).astype(out_dtype)

    spec = pl.BlockSpec((None, tm, cols), lambda j, i: (j, i, 0))
    return pl.pallas_call(
        body, name="add_pair", grid=(nb, rows // tm), in_specs=[spec, spec], out_specs=spec, out_shape=_sds(a.shape, out_dtype),
        compiler_params=_params(("parallel", "parallel")),
    )(a, b)


def sum_four(a):
    _, rows, cols = a.shape
    tm = _row_tile(rows, cols)

    def body(a_ref, o_ref):
        o_ref[...] = ((a_ref[0].astype(F32) + a_ref[1].astype(F32)) + a_ref[2].astype(F32)) + a_ref[3].astype(F32)

    return pl.pallas_call(
        body, name="sum_four", grid=(rows // tm,),
        in_specs=[pl.BlockSpec((4, tm, cols), lambda i: (0, i, 0))], out_specs=pl.BlockSpec((tm, cols), lambda i: (i, 0)),
        out_shape=_sds((rows, cols)), compiler_params=_params(("parallel",)),
    )(a)


def adamw(w, g, m, v):
    rows, cols = w.shape
    tm = _row_tile(rows, cols)
    c1 = 1.0 - ADAM_B1 ** ADAM_STEP
    c2 = 1.0 - ADAM_B2 ** ADAM_STEP

    def body(w_ref, g_ref, m_ref, v_ref, d_ref, nm_ref, nv_ref):
        g = g_ref[...]
        nm = ADAM_B1 * m_ref[...] + (1.0 - ADAM_B1) * g
        nv = ADAM_B2 * v_ref[...] + (1.0 - ADAM_B2) * (g * g)
        d_ref[...] = -ADAM_LR * ((nm / c1) / (jnp.sqrt(nv / c2) + ADAM_EPS) + ADAM_WD * w_ref[...])
        nm_ref[...] = nm
        nv_ref[...] = nv

    spec = pl.BlockSpec((tm, cols), lambda i: (i, 0))
    return pl.pallas_call(
        body, name="adamw", grid=(rows // tm,), in_specs=[spec] * 4, out_specs=[spec] * 3,
        out_shape=[_sds(w.shape)] * 3, compiler_params=_params(("parallel",)),
    )(w, g, m, v)


PACK_TILE = 8 * LANES
PACK_PIECES = 8
PACK_ALIGN = PACK_PIECES * 16


def _pack_small(values):
    parts = []
    for name in SMALL_NAMES:
        flat = values[name].reshape(-1)
        pad = (-flat.shape[0]) % PACK_TILE
        if pad:
            flat = jnp.concatenate([flat, jnp.zeros((pad,), F32)])
        parts.append(flat.reshape(-1, LANES))
    rows = sum(p.shape[0] for p in parts)
    pad = (-rows) % PACK_ALIGN
    if pad:
        parts.append(jnp.zeros((pad, LANES), F32))
    return jnp.concatenate(parts, axis=0)


def _unpack_small(pack, like):
    out = {}
    row = 0
    for name in SMALL_NAMES:
        size = math.prod(like[name].shape)
        rows = -(-size // PACK_TILE) * 8
        out[name] = pack[row:row + rows].reshape(-1)[:size].reshape(like[name].shape)
        row += rows
    return out


def _is_column_sharded(name):
    return name.endswith("w_in")


def _to_blocks(name, full):
    if _is_column_sharded(name):
        rows, cols = full.shape
        return full.reshape(rows, 4, cols // 4).transpose(1, 0, 2)
    return full.reshape(4, full.shape[0] // 4, full.shape[1])


def _from_blocks(name, stacked):
    if _is_column_sharded(name):
        return stacked.transpose(1, 0, 2).reshape(stacked.shape[1], 4 * stacked.shape[2])
    return stacked.reshape(4 * stacked.shape[1], stacked.shape[2])


def _train_step(x, loss_target, weights, moments_m, moments_v):
    c = lax.axis_index("c")
    gathered = gather_weight_shards([weights[n].astype(MXU_DTYPE) for n in BIG_NAMES])
    full = {n: _from_blocks(n, g) for n, g in zip(BIG_NAMES, gathered)}
    layers = []
    for i in range(4):
        names = SSM_NAMES if i % 2 == 0 else ATTN_NAMES
        w = {}
        for n in names:
            key = "l%d_%s" % (i, n)
            if key in full:
                w[n] = full[key]
            else:
                w[n] = weights[key]
        layers.append(w)
    loss, dx, grads = _sequence_step(x[0], loss_target[0], layers, weights["final_norm"])
    loss = lax.psum(loss, ("x", "y", "c"))
    small_pack = _pack_small({n: grads[n] for n in SMALL_NAMES})
    blocks = [_to_blocks(n, grads[n]) for n in BIG_NAMES] + [small_pack[None]]
    from_sibling = exchange_halves_with_sibling(blocks)
    chip_sums = []
    for i, (b, r) in enumerate(zip(blocks, from_sibling)):
        half = b.shape[1] // 2
        dtype = WIRE_DTYPE if i < len(BIG_NAMES) else F32
        chip_sums.append(add_pair(lax.dynamic_slice_in_dim(b, c * half, half, axis=1), r, dtype))
    contributions = scatter_blocks_to_chips(chip_sums)
    reduced = [sum_four(a) for a in contributions]
    shared = swap_halves_with_sibling(reduced)
    big_grads = {n: s.reshape(2 * s.shape[1], s.shape[2]) for n, s in zip(BIG_NAMES, shared[:-1])}
    small_grad_pack = shared[-1].reshape(-1, LANES)
    out_grad, out_delta, out_m, out_v = {}, {}, {}, {}
    for n in BIG_NAMES:
        out_grad[n] = big_grads[n]
        out_delta[n], out_m[n], out_v[n] = adamw(weights[n], big_grads[n], moments_m[n], moments_v[n])
    small_like = {n: weights[n] for n in SMALL_NAMES}
    d_pack, m_pack, v_pack = adamw(_pack_small(small_like), small_grad_pack, _pack_small({n: moments_m[n] for n in SMALL_NAMES}),
                                   _pack_small({n: moments_v[n] for n in SMALL_NAMES}))
    out_grad.update(_unpack_small(small_grad_pack, small_like))
    out_delta.update(_unpack_small(d_pack, small_like))
    out_m.update(_unpack_small(m_pack, small_like))
    out_v.update(_unpack_small(v_pack, small_like))
    outs = [loss, dx[None]]
    for group in (out_grad, out_delta, out_m, out_v):
        outs.extend(group[n] for n in WEIGHT_NAMES)
    return tuple(outs)


def kernel(x, l0_norm, l0_w_in, l0_a_re, l0_a_im, l0_log_step, l0_b_re, l0_b_im, l0_c_re, l0_c_im, l0_d, l0_w_glu, l0_b_glu, l0_w_out, l1_norm, l1_w_in, l1_sinks, l1_w_out, l2_norm, l2_w_in, l2_a_re, l2_a_im, l2_log_step, l2_b_re, l2_b_im, l2_c_re, l2_c_im, l2_d, l2_w_glu, l2_b_glu, l2_w_out, l3_norm, l3_w_in, l3_sinks, l3_w_out, final_norm, loss_target, m_l0_norm, m_l0_w_in, m_l0_a_re, m_l0_a_im, m_l0_log_step, m_l0_b_re, m_l0_b_im, m_l0_c_re, m_l0_c_im, m_l0_d, m_l0_w_glu, m_l0_b_glu, m_l0_w_out, m_l1_norm, m_l1_w_in, m_l1_sinks, m_l1_w_out, m_l2_norm, m_l2_w_in, m_l2_a_re, m_l2_a_im, m_l2_log_step, m_l2_b_re, m_l2_b_im, m_l2_c_re, m_l2_c_im, m_l2_d, m_l2_w_glu, m_l2_b_glu, m_l2_w_out, m_l3_norm, m_l3_w_in, m_l3_sinks, m_l3_w_out, m_final_norm, v_l0_norm, v_l0_w_in, v_l0_a_re, v_l0_a_im, v_l0_log_step, v_l0_b_re, v_l0_b_im, v_l0_c_re, v_l0_c_im, v_l0_d, v_l0_w_glu, v_l0_b_glu, v_l0_w_out, v_l1_norm, v_l1_w_in, v_l1_sinks, v_l1_w_out, v_l2_norm, v_l2_w_in, v_l2_a_re, v_l2_a_im, v_l2_log_step, v_l2_b_re, v_l2_b_im, v_l2_c_re, v_l2_c_im, v_l2_d, v_l2_w_glu, v_l2_b_glu, v_l2_w_out, v_l3_norm, v_l3_w_in, v_l3_sinks, v_l3_w_out, v_final_norm):
    args = locals()
    weights = {n: args[n] for n in WEIGHT_NAMES}
    moments_m = {n: args["m_" + n] for n in WEIGHT_NAMES}
    moments_v = {n: args["v_" + n] for n in WEIGHT_NAMES}
    return _train_step(x, loss_target, weights, moments_m, moments_v)
```

```python
import functools
import math

import jax
import jax.numpy as jnp
from jax import lax
from jax.experimental import pallas as pl
from jax.experimental.pallas import tpu as pltpu

F32 = jnp.float32
MXU_DTYPE = jnp.bfloat16
WIRE_DTYPE = jnp.bfloat16
MESH = pl.DeviceIdType.MESH

D_MODEL = 1024
BRANCH = 1024
NORM_EPS = 1e-5
SSM_GROUPS = 64
SSM_GROUP = 16
SSM_STATE = 64
S5_CHUNK = 16
LANES = 128
S5_OCT = LANES // SSM_GROUP
S5_OCTETS = SSM_GROUPS // S5_OCT
S5_OCT_IN = S5_CHUNK * LANES
S5_OCT_STATE = S5_OCT * SSM_STATE
S5_STATES = SSM_GROUPS * SSM_STATE
HEAD_DIM = 64
N_Q_HEADS = 16
N_KV_HEADS = 2
GQA_GROUP = N_Q_HEADS // N_KV_HEADS
ATTN_BLOCK = 128
Q_DIM = N_Q_HEADS * HEAD_DIM
KV_DIM = N_KV_HEADS * HEAD_DIM
ROPE_THETA = 10000.0
NEG_INF = -1e30
ADAM_LR = 0.001
ADAM_B1 = 0.9
ADAM_B2 = 0.999
ADAM_EPS = 1e-08
ADAM_WD = 0.01
ADAM_STEP = 10

VMEM_LIMIT_V7X = 56 * 1024 * 1024
ROW_TILE_FWD = 512
ROW_TILE_BWD = 512

SSM_NAMES = ("norm", "w_in", "a_re", "a_im", "log_step", "b_re", "b_im", "c_re", "c_im", "d", "w_glu", "b_glu", "w_out")
ATTN_NAMES = ("norm", "w_in", "sinks", "w_out")


def _weight_names():
    names = []
    for i in range(4):
        for n in (SSM_NAMES if i % 2 == 0 else ATTN_NAMES):
            names.append("l%d_%s" % (i, n))
    names.append("final_norm")
    return names


WEIGHT_NAMES = _weight_names()
BIG_NAMES = [n for n in WEIGHT_NAMES if n.endswith(("w_in", "w_glu", "w_out"))]
SMALL_NAMES = [n for n in WEIGHT_NAMES if n not in BIG_NAMES]


def _params(semantics=None):
    return pltpu.CompilerParams(dimension_semantics=semantics, vmem_limit_bytes=VMEM_LIMIT_V7X)


def _rows(tm, n):
    return pl.BlockSpec((tm, n), lambda i: (i, 0))


def _whole(shape):
    return pl.BlockSpec(shape, lambda i: (0,) * len(shape), pipeline_mode=pl.Buffered(1))


def _sds(shape, dtype=F32):
    return jax.ShapeDtypeStruct(shape, dtype)


def _mm(a, b):
    return jnp.dot(a.astype(MXU_DTYPE), b.astype(MXU_DTYPE), preferred_element_type=F32)


def _mm_tn(a, b):
    return lax.dot_general(a.astype(MXU_DTYPE), b.astype(MXU_DTYPE), (((0,), (0,)), ((), ())), preferred_element_type=F32)


def _mm_nt(a, b):
    return lax.dot_general(a.astype(MXU_DTYPE), b.astype(MXU_DTYPE), (((1,), (1,)), ((), ())), preferred_element_type=F32)


def _sigmoid(x):
    return 1.0 / (1.0 + jnp.exp(-x))


def _silu(x):
    return x * _sigmoid(x)


def _silu_grad(x):
    s = _sigmoid(x)
    return s * (1.0 + x * (1.0 - s))


GELU_C0 = math.sqrt(2.0 / math.pi)
GELU_C1 = 0.044715


def _gelu(x):
    return 0.5 * x * (1.0 + jnp.tanh(GELU_C0 * (x + GELU_C1 * x * x * x)))


def _gelu_grad(x):
    th = jnp.tanh(GELU_C0 * (x + GELU_C1 * x * x * x))
    return 0.5 * (1.0 + th) + 0.5 * x * (1.0 - th * th) * GELU_C0 * (1.0 + 3.0 * GELU_C1 * x * x)


def _rms(x, g):
    r = lax.rsqrt(jnp.mean(x * x, axis=-1, keepdims=True) + NORM_EPS)
    xhat = x * r
    return r, xhat, xhat * g


def _rms_bwd(dh, g, r, xhat):
    dxhat = dh * g
    dx = r * (dxhat - xhat * jnp.mean(dxhat * xhat, axis=-1, keepdims=True))
    return dx, jnp.sum(dh * xhat, axis=0, keepdims=True)


def _swap_half_heads(x):
    n = x.shape[-1]
    lane = lax.broadcasted_iota(jnp.int32, x.shape, x.ndim - 1)
    first = (lane % HEAD_DIM) < (HEAD_DIM // 2)
    return jnp.where(first, pltpu.roll(x, n - HEAD_DIM // 2, x.ndim - 1), pltpu.roll(x, HEAD_DIM // 2, x.ndim - 1))


def _tile_lanes(t, reps):
    return jnp.concatenate([t] * reps, axis=1)


def ssm_proj_fwd(x, norm, w_in):
    t = x.shape[0]
    tm = min(ROW_TILE_FWD, t)

    def body(x_ref, g_ref, w_ref, u_ref, gate_ref):
        _, _, h = _rms(x_ref[...], g_ref[...])
        p = _mm(h, w_ref[...])
        u_ref[...] = p[:, :BRANCH]
        gate_ref[...] = p[:, BRANCH:]

    return pl.pallas_call(
        body, name="ssm_proj_fwd", grid=(t // tm,),
        in_specs=[_rows(tm, D_MODEL), _whole((1, D_MODEL)), _whole((D_MODEL, 2 * BRANCH))],
        out_specs=[_rows(tm, BRANCH), _rows(tm, BRANCH)],
        out_shape=[_sds((t, BRANCH)), _sds((t, BRANCH))],
        compiler_params=_params(("parallel",)),
    )(x, norm, w_in)


def _chunk_rows(ref, nk, dtype=None):
    rows = jnp.concatenate([ref[pl.ds(s, nk, stride=S5_CHUNK), :] for s in range(S5_CHUNK)], axis=1)
    return rows.astype(MXU_DTYPE if dtype is None else dtype)


def _store_chunk_rows(ref, val, nk):
    for s in range(S5_CHUNK):
        ref[pl.ds(s, nk, stride=S5_CHUNK), :] = val[:, s * LANES:(s + 1) * LANES]


def _own_group_mask():
    row = lax.broadcasted_iota(jnp.int32, (S5_OCT_IN, S5_OCT_STATE), 0)
    col = lax.broadcasted_iota(jnp.int32, (S5_OCT_IN, S5_OCT_STATE), 1)
    return ((row % LANES) // SSM_GROUP) == (col // SSM_STATE)


def _spread_groups(w):
    return jnp.where(_own_group_mask(), jnp.concatenate([w] * (S5_OCT_STATE // LANES), axis=1), 0.0).astype(MXU_DTYPE)


def _fold_groups(p):
    p = jnp.where(_own_group_mask(), p, 0.0)
    return sum(p[:, q * LANES:(q + 1) * LANES] for q in range(S5_OCT_STATE // LANES))


def _fill_toeplitz(win_ref, kd_ref):
    win_ref[...] = jnp.zeros_like(win_ref)
    for s in range(S5_CHUNK):
        for t in range(s, S5_CHUNK):
            win_ref[s * LANES:(s + 1) * LANES, t * LANES:(t + 1) * LANES] = kd_ref[t - s].astype(MXU_DTYPE)


TOEPLITZ_BLOCK = 512
_TOEPLITZ_BLOCKS = [(lo, lo + TOEPLITZ_BLOCK) for lo in range(0, S5_OCT_IN, TOEPLITZ_BLOCK)]


def _strip(t):
    return pl.BlockSpec((t, LANES), lambda b: (0, b))


def _oct_states(nk):
    return pl.BlockSpec((nk, S5_OCT_STATE), lambda b: (0, b))


OCT_W = pl.BlockSpec((None, S5_OCT_IN, LANES), lambda b: (b, 0, 0))
OCT_KD = pl.BlockSpec((None, S5_CHUNK, LANES, LANES), lambda b: (b, 0, 0, 0))


def s5_chunk_states(u, ws_re, ws_im):
    t = u.shape[0]
    nk = t // S5_CHUNK

    def body(u_ref, wr_ref, wi_ref, re_ref, im_ref):
        uc = _chunk_rows(u_ref, nk)
        re_ref[...] = _mm(uc, _spread_groups(wr_ref[...]))
        im_ref[...] = _mm(uc, _spread_groups(wi_ref[...]))

    return pl.pallas_call(
        body, name="s5_chunk_states", grid=(S5_OCTETS,),
        in_specs=[_strip(t), OCT_W, OCT_W], out_specs=[_oct_states(nk), _oct_states(nk)],
        out_shape=[_sds((nk, S5_STATES)), _sds((nk, S5_STATES))],
        compiler_params=_params(("parallel",)),
    )(u, ws_re, ws_im)


def s5_scan_fwd(s_re, s_im, a_re, a_im):
    nk = s_re.shape[0]

    def body(sre_ref, sim_ref, ar_ref, ai_ref, hre_ref, him_ref):
        ar = ar_ref[...]
        ai = ai_ref[...]

        def step(k, carry):
            hr, hi = carry
            hre_ref[pl.ds(k, 1), :] = hr
            him_ref[pl.ds(k, 1), :] = hi
            sr = sre_ref[pl.ds(k, 1), :]
            si = sim_ref[pl.ds(k, 1), :]
            return ar * hr - ai * hi + sr, ai * hr + ar * hi + si

        zero = jnp.zeros((1, S5_STATES), F32)
        lax.fori_loop(0, nk, step, (zero, zero))

    vm = pl.BlockSpec(memory_space=pltpu.VMEM)
    return pl.pallas_call(
        body, name="s5_scan_fwd", in_specs=[vm, vm, vm, vm], out_specs=[vm, vm],
        out_shape=[_sds((nk, S5_STATES)), _sds((nk, S5_STATES))],
        compiler_params=_params(),
    )(s_re, s_im, a_re, a_im)


def s5_outputs(u, h_re, h_im, kd, wo_re, wo_im):
    t = u.shape[0]
    nk = t // S5_CHUNK

    def body(u_ref, hre_ref, him_ref, kd_ref, wor_ref, woi_ref, y_ref, win_ref):
        _fill_toeplitz(win_ref, kd_ref)
        uc = _chunk_rows(u_ref, nk)
        y = jnp.concatenate([_mm(uc[:, :hi], win_ref[:hi, lo:hi]) for lo, hi in _TOEPLITZ_BLOCKS], axis=1)
        y = y + _mm_nt(hre_ref[...], _spread_groups(wor_ref[...])) + _mm_nt(him_ref[...], _spread_groups(woi_ref[...]))
        _store_chunk_rows(y_ref, y, nk)

    return pl.pallas_call(
        body, name="s5_outputs", grid=(S5_OCTETS,),
        in_specs=[_strip(t), _oct_states(nk), _oct_states(nk), OCT_KD, OCT_W, OCT_W],
        out_specs=_strip(t), out_shape=_sds((t, BRANCH)),
        scratch_shapes=[pltpu.VMEM((S5_OCT_IN, S5_OCT_IN), MXU_DTYPE)],
        compiler_params=_params(("parallel",)),
    )(u, h_re, h_im, kd, wo_re, wo_im)


def s5_state_grads(dy, wo_re, wo_im):
    t = dy.shape[0]
    nk = t // S5_CHUNK

    def body(dy_ref, wor_ref, woi_ref, re_ref, im_ref):
        dyc = _chunk_rows(dy_ref, nk)
        re_ref[...] = _mm(dyc, _spread_groups(wor_ref[...]))
        im_ref[...] = _mm(dyc, _spread_groups(woi_ref[...]))

    return pl.pallas_call(
        body, name="s5_state_grads", grid=(S5_OCTETS,),
        in_specs=[_strip(t), OCT_W, OCT_W], out_specs=[_oct_states(nk), _oct_states(nk)],
        out_shape=[_sds((nk, S5_STATES)), _sds((nk, S5_STATES))],
        compiler_params=_params(("parallel",)),
    )(dy, wo_re, wo_im)


def s5_scan_bwd(dh_re, dh_im, h_re, h_im, a_re, a_im):
    nk = dh_re.shape[0]

    def body(dhr_ref, dhi_ref, hr_ref, hi_ref, ar_ref, ai_ref, dsr_ref, dsi_ref, dar_ref, dai_ref):
        ar = ar_ref[...]
        ai = ai_ref[...]

        dar_ref[...] = jnp.zeros_like(dar_ref)
        dai_ref[...] = jnp.zeros_like(dai_ref)

        def step(i, carry):
            gr, gi = carry
            k = nk - 1 - i
            dhr = dhr_ref[pl.ds(k, 1), :]
            dhi = dhi_ref[pl.ds(k, 1), :]
            dsr_ref[pl.ds(k, 1), :] = gr
            dsi_ref[pl.ds(k, 1), :] = gi
            hr = hr_ref[pl.ds(k, 1), :]
            hi = hi_ref[pl.ds(k, 1), :]
            dar_ref[...] += gr * hr + gi * hi
            dai_ref[...] += gi * hr - gr * hi
            return dhr + ar * gr + ai * gi, dhi - ai * gr + ar * gi

        zero = jnp.zeros((1, S5_STATES), F32)
        lax.fori_loop(0, nk, step, (zero, zero))

    vm = pl.BlockSpec(memory_space=pltpu.VMEM)
    return pl.pallas_call(
        body, name="s5_scan_bwd", in_specs=[vm] * 6, out_specs=[vm] * 4,
        out_shape=[_sds((nk, S5_STATES)), _sds((nk, S5_STATES)), _sds((1, S5_STATES)), _sds((1, S5_STATES))],
        input_output_aliases={0: 0, 1: 1}, compiler_params=_params(),
    )(dh_re, dh_im, h_re, h_im, a_re, a_im)


def s5_input_grads(dy, ds_re, ds_im, kd, ws_re, ws_im):
    t = dy.shape[0]
    nk = t // S5_CHUNK

    def body(dy_ref, dsr_ref, dsi_ref, kd_ref, wsr_ref, wsi_ref, du_ref, win_ref):
        _fill_toeplitz(win_ref, kd_ref)
        dyc = _chunk_rows(dy_ref, nk)
        du = jnp.concatenate([_mm_nt(dyc[:, lo:], win_ref[lo:hi, lo:]) for lo, hi in _TOEPLITZ_BLOCKS], axis=1)
        du = du + _mm_nt(dsr_ref[...], _spread_groups(wsr_ref[...])) + _mm_nt(dsi_ref[...], _spread_groups(wsi_ref[...]))
        _store_chunk_rows(du_ref, du, nk)

    return pl.pallas_call(
        body, name="s5_input_grads", grid=(S5_OCTETS,),
        in_specs=[_strip(t), _oct_states(nk), _oct_states(nk), OCT_KD, OCT_W, OCT_W],
        out_specs=_strip(t), out_shape=_sds((t, BRANCH)),
        scratch_shapes=[pltpu.VMEM((S5_OCT_IN, S5_OCT_IN), MXU_DTYPE)],
        compiler_params=_params(("parallel",)),
    )(dy, ds_re, ds_im, kd, ws_re, ws_im)


def s5_weight_grads(u, dy, h_re, h_im, ds_re, ds_im):
    t = u.shape[0]
    nk = t // S5_CHUNK

    def body(u_ref, dy_ref, hre_ref, him_ref, dsr_ref, dsi_ref, dkd_ref, dwsr_ref, dwsi_ref, dwor_ref, dwoi_ref):
        dyc = _chunk_rows(dy_ref, nk, F32)
        uct = _chunk_rows(u_ref, nk, F32).T.astype(MXU_DTYPE)
        dyct = dyc.T.astype(MXU_DTYPE)
        dyc = dyc.astype(MXU_DTYPE)
        dwsr_ref[...] = _fold_groups(_mm(uct, dsr_ref[...]))
        dwsi_ref[...] = _fold_groups(_mm(uct, dsi_ref[...]))
        dwor_ref[...] = _fold_groups(_mm(dyct, hre_ref[...]))
        dwoi_ref[...] = _fold_groups(_mm(dyct, him_ref[...]))
        dkd_ref[...] = jnp.zeros_like(dkd_ref)
        for tt in range(0, S5_CHUNK, 2):
            p = _mm(uct[:(tt + 2) * LANES], dyc[:, tt * LANES:(tt + 2) * LANES])
            for s in range(tt + 2):
                rows = p[s * LANES:(s + 1) * LANES]
                if s <= tt:
                    dkd_ref[tt - s] += rows[:, :LANES]
                dkd_ref[tt + 1 - s] += rows[:, LANES:]

    return pl.pallas_call(
        body, name="s5_weight_grads", grid=(S5_OCTETS,),
        in_specs=[_strip(t), _strip(t)] + [_oct_states(nk)] * 4,
        out_specs=[OCT_KD, OCT_W, OCT_W, OCT_W, OCT_W],
        out_shape=[_sds((S5_OCTETS, S5_CHUNK, LANES, LANES))] + [_sds((S5_OCTETS, S5_OCT_IN, LANES))] * 4,
        compiler_params=_params(("parallel",)),
    )(u, dy, h_re, h_im, ds_re, ds_im)


def ssm_mix_fwd(x, u, gate, y_scan, d, w_glu, b_glu, w_out):
    t = x.shape[0]
    tm = min(ROW_TILE_FWD, t)

    def body(x_ref, u_ref, gate_ref, ys_ref, d_ref, wg_ref, bg_ref, wo_ref, y_ref, g2_ref, xo_ref):
        y = ys_ref[...] + d_ref[...] * u_ref[...]
        z0 = _gelu(y)
        g2 = _mm(z0, wg_ref[...]) + bg_ref[...]
        a = z0 * _sigmoid(g2) * _silu(gate_ref[...])
        y_ref[...] = y
        g2_ref[...] = g2
        xo_ref[...] = x_ref[...] + _mm(a, wo_ref[...])

    row = _rows(tm, BRANCH)
    vec = _whole((1, BRANCH))
    mat = _whole((BRANCH, BRANCH))
    return pl.pallas_call(
        body, name="ssm_mix_fwd", grid=(t // tm,),
        in_specs=[row, row, row, row, vec, mat, vec, mat],
        out_specs=[row, row, row],
        out_shape=[_sds((t, BRANCH))] * 3,
        compiler_params=_params(("parallel",)),
    )(x, u, gate, y_scan, d, w_glu, b_glu, w_out)


def ssm_mix_bwd(dxo, u, gate, y, g2, w_glu, w_out):
    t = dxo.shape[0]
    tm = min(ROW_TILE_BWD, t)

    def body(dxo_ref, u_ref, gate_ref, y_ref, g2_ref, wgt_ref, wot_ref, dy_ref, dgate_ref, dwo_ref, dwg_ref, dbg_ref, dd_ref):
        @pl.when(pl.program_id(0) == 0)
        def _():
            dwo_ref[...] = jnp.zeros_like(dwo_ref)
            dwg_ref[...] = jnp.zeros_like(dwg_ref)
            dbg_ref[...] = jnp.zeros_like(dbg_ref)
            dd_ref[...] = jnp.zeros_like(dd_ref)

        dxo = dxo_ref[...]
        gate = gate_ref[...]
        y = y_ref[...]
        z0 = _gelu(y)
        sg = _sigmoid(g2_ref[...])
        z = z0 * sg
        sgate = _silu(gate)
        da = _mm_nt(dxo, wot_ref[...])
        dwo_ref[...] += _mm_tn(z * sgate, dxo)
        dz = da * sgate
        dgate_ref[...] = da * z * _silu_grad(gate)
        dg2 = dz * z0 * sg * (1.0 - sg)
        dbg_ref[...] += jnp.sum(dg2, axis=0, keepdims=True)
        dwg_ref[...] += _mm_tn(z0, dg2)
        dz0 = dz * sg + _mm_nt(dg2, wgt_ref[...])
        dy = dz0 * _gelu_grad(y)
        dd_ref[...] += jnp.sum(dy * u_ref[...], axis=0, keepdims=True)
        dy_ref[...] = dy

    row = _rows(tm, BRANCH)
    vec = _whole((1, BRANCH))
    mat = _whole((BRANCH, BRANCH))
    return pl.pallas_call(
        body, name="ssm_mix_bwd", grid=(t // tm,),
        in_specs=[row, row, row, row, row, mat, mat],
        out_specs=[row, row, mat, mat, vec, vec],
        out_shape=[_sds((t, BRANCH)), _sds((t, BRANCH)), _sds((BRANCH, D_MODEL)), _sds((BRANCH, BRANCH)),
                   _sds((1, BRANCH)), _sds((1, BRANCH))],
        compiler_params=_params(("arbitrary",)),
    )(dxo, u, gate, y, g2, w_glu, w_out)


def ssm_proj_bwd(x, norm, dxo, dy, du_scan, dgate, d, w_in):
    t = x.shape[0]
    tm = min(ROW_TILE_BWD, t)
    n = 2 * BRANCH

    def body(x_ref, g_ref, dxo_ref, dy_ref, dus_ref, dgate_ref, d_ref, wt_ref, dx_ref, dw_ref, dg_ref):
        @pl.when(pl.program_id(0) == 0)
        def _():
            dw_ref[...] = jnp.zeros_like(dw_ref)
            dg_ref[...] = jnp.zeros_like(dg_ref)

        g = g_ref[...]
        r, xhat, h = _rms(x_ref[...], g)
        du = dus_ref[...] + d_ref[...] * dy_ref[...]
        dproj = jnp.concatenate([du, dgate_ref[...]], axis=1)
        dh = _mm_nt(dproj, wt_ref[...])
        dw_ref[...] += _mm_tn(h, dproj)
        dx, dg = _rms_bwd(dh, g, r, xhat)
        dg_ref[...] += dg
        dx_ref[...] = dxo_ref[...] + dx

    row = _rows(tm, D_MODEL)
    vec = _whole((1, D_MODEL))
    return pl.pallas_call(
        body, name="ssm_proj_bwd", grid=(t // tm,),
        in_specs=[row, vec, row, row, row, row, vec, _whole((D_MODEL, n))],
        out_specs=[row, _whole((D_MODEL, n)), vec],
        out_shape=[_sds((t, D_MODEL)), _sds((D_MODEL, n)), _sds((1, D_MODEL))],
        compiler_params=_params(("arbitrary",)),
    )(x, norm, dxo, dy, du_scan, dgate, d, w_in)


ATTN_N = Q_DIM + 2 * KV_DIM + BRANCH


def attn_proj_fwd(x, norm, w_in, cos2, sin2):
    t = x.shape[0]
    tm = min(ROW_TILE_FWD, t)

    def body(x_ref, g_ref, w_ref, cos_ref, sin_ref, q_ref, k_ref, v_ref, gate_ref):
        _, _, h = _rms(x_ref[...], g_ref[...])
        p = _mm(h, w_ref[...])
        cs = cos_ref[...]
        sn = sin_ref[...]
        q = p[:, :Q_DIM]
        k = p[:, Q_DIM:Q_DIM + KV_DIM]
        q_ref[...] = q * _tile_lanes(cs, Q_DIM // LANES) + _swap_half_heads(q) * _tile_lanes(sn, Q_DIM // LANES)
        k_ref[...] = k * cs + _swap_half_heads(k) * sn
        v_ref[...] = p[:, Q_DIM + KV_DIM:Q_DIM + 2 * KV_DIM]
        gate_ref[...] = p[:, Q_DIM + 2 * KV_DIM:]

    return pl.pallas_call(
        body, name="attn_proj_fwd", grid=(t // tm,),
        in_specs=[_rows(tm, D_MODEL), _whole((1, D_MODEL)), _whole((D_MODEL, ATTN_N)), _rows(tm, LANES), _rows(tm, LANES)],
        out_specs=[_rows(tm, Q_DIM), _rows(tm, KV_DIM), _rows(tm, KV_DIM), _rows(tm, BRANCH)],
        out_shape=[_sds((t, Q_DIM)), _sds((t, KV_DIM)), _sds((t, KV_DIM)), _sds((t, BRANCH))],
        compiler_params=_params(("parallel",)),
    )(x, norm, w_in, cos2, sin2)


def _band_mask(first_block):
    qi = lax.broadcasted_iota(jnp.int32, (ATTN_BLOCK, 2 * ATTN_BLOCK), 0)
    kj = lax.broadcasted_iota(jnp.int32, (ATTN_BLOCK, 2 * ATTN_BLOCK), 1)
    dist = qi + ATTN_BLOCK - kj
    first_key = jnp.where(first_block, ATTN_BLOCK, 0)
    return (dist >= 0) & (dist < ATTN_BLOCK) & (kj >= first_key)


def _lane_is(h):
    return lax.broadcasted_iota(jnp.int32, (1, LANES), 1) == h


def attn_fwd(q, k, v, sinks):
    t = q.shape[0]
    nb = t // ATTN_BLOCK
    scale = HEAD_DIM ** -0.5

    def body(sink_ref, q_ref, kc_ref, kp_ref, vc_ref, vp_ref, o_ref, lse_ref):
        i = pl.program_id(0)
        keys = jnp.concatenate([kp_ref[...], kc_ref[...]], axis=0).astype(MXU_DTYPE)
        vals = jnp.concatenate([vp_ref[...], vc_ref[...]], axis=0).astype(MXU_DTYPE)
        valid = _band_mask(i == 0)
        lse = jnp.zeros((ATTN_BLOCK, LANES), F32)
        for h in range(N_Q_HEADS):
            hk = h // GQA_GROUP
            kh = keys[:, hk * HEAD_DIM:(hk + 1) * HEAD_DIM]
            vh = vals[:, hk * HEAD_DIM:(hk + 1) * HEAD_DIM]
            qh = q_ref[:, h * HEAD_DIM:(h + 1) * HEAD_DIM]
            s = jnp.where(valid, _mm_nt(qh, kh) * scale, NEG_INF)
            sink = sink_ref[h]
            m = jnp.maximum(jnp.max(s, axis=-1, keepdims=True), sink)
            p = jnp.exp(s - m)
            den = jnp.sum(p, axis=-1, keepdims=True) + jnp.exp(sink - m)
            o_ref[:, h * HEAD_DIM:(h + 1) * HEAD_DIM] = _mm(p, vh) / den
            lse = jnp.where(_lane_is(h), m + jnp.log(den), lse)
        lse_ref[...] = lse

    cur = lambda n: pl.BlockSpec((ATTN_BLOCK, n), lambda i: (i, 0))
    prev = lambda n: pl.BlockSpec((ATTN_BLOCK, n), lambda i: (jnp.maximum(i - 1, 0), 0))
    return pl.pallas_call(
        body, name="attn_fwd", grid=(nb,),
        in_specs=[pl.BlockSpec(memory_space=pltpu.SMEM), cur(Q_DIM), cur(KV_DIM), prev(KV_DIM), cur(KV_DIM), prev(KV_DIM)],
        out_specs=[cur(Q_DIM), cur(LANES)],
        out_shape=[_sds((t, Q_DIM)), _sds((t, LANES))],
        compiler_params=_params(("parallel",)),
    )(sinks, q, k, k, v, v)


def attn_bwd(q, k, v, sinks, o, lse, do):
    t = q.shape[0]
    nb = t // ATTN_BLOCK
    scale = HEAD_DIM ** -0.5

    def body(sink_ref, q_ref, kc_ref, kp_ref, vc_ref, vp_ref, o_ref, lse_ref, do_ref,
             dq_ref, dk_ref, dv_ref, dsink_ref, dk_carry, dv_carry):
        i = pl.program_id(0)

        @pl.when(i == 0)
        def _():
            dsink_ref[...] = jnp.zeros_like(dsink_ref)
            dk_carry[...] = jnp.zeros_like(dk_carry)
            dv_carry[...] = jnp.zeros_like(dv_carry)

        @pl.when(i < nb)
        def _():
            keys = jnp.concatenate([kp_ref[...], kc_ref[...]], axis=0).astype(MXU_DTYPE)
            vals = jnp.concatenate([vp_ref[...], vc_ref[...]], axis=0).astype(MXU_DTYPE)
            valid = _band_mask(i == 0)
            lse_all = lse_ref[...]
            dsink = jnp.zeros((1, LANES), F32)
            dk_heads = []
            dv_heads = []
            for hk in range(N_KV_HEADS):
                kh = keys[:, hk * HEAD_DIM:(hk + 1) * HEAD_DIM]
                vh = vals[:, hk * HEAD_DIM:(hk + 1) * HEAD_DIM]
                dkk = jnp.zeros((2 * ATTN_BLOCK, HEAD_DIM), F32)
                dvv = jnp.zeros((2 * ATTN_BLOCK, HEAD_DIM), F32)
                for hq in range(GQA_GROUP):
                    h = hk * GQA_GROUP + hq
                    sl = slice(h * HEAD_DIM, (h + 1) * HEAD_DIM)
                    qh = q_ref[:, sl]
                    doh = do_ref[:, sl]
                    lse_h = jnp.sum(jnp.where(_lane_is(h), lse_all, 0.0), axis=-1, keepdims=True)
                    s = jnp.where(valid, _mm_nt(qh, kh) * scale, NEG_INF)
                    p = jnp.exp(s - lse_h)
                    delta = jnp.sum(doh * o_ref[:, sl], axis=-1, keepdims=True)
                    dvv = dvv + _mm_tn(p, doh)
                    ds = p * (_mm_nt(doh, vh) - delta)
                    dq_ref[:, sl] = _mm(ds, kh) * scale
                    dkk = dkk + _mm_tn(ds, qh) * scale
                    dsink = dsink + jnp.where(_lane_is(h), -jnp.sum(jnp.exp(sink_ref[h] - lse_h) * delta), 0.0)
                dk_heads.append(dkk)
                dv_heads.append(dvv)
            dkk = jnp.concatenate(dk_heads, axis=1)
            dvv = jnp.concatenate(dv_heads, axis=1)
            dsink_ref[...] += dsink
            dk_ref[...] = dk_carry[...] + dkk[:ATTN_BLOCK]
            dv_ref[...] = dv_carry[...] + dvv[:ATTN_BLOCK]
            dk_carry[...] = dkk[ATTN_BLOCK:]
            dv_carry[...] = dvv[ATTN_BLOCK:]

        @pl.when(i == nb)
        def _():
            dk_ref[...] = dk_carry[...]
            dv_ref[...] = dv_carry[...]

    last = nb - 1
    cur = lambda n: pl.BlockSpec((ATTN_BLOCK, n), lambda i: (jnp.minimum(i, last), 0))
    prev = lambda n: pl.BlockSpec((ATTN_BLOCK, n), lambda i: (jnp.clip(i - 1, 0, last), 0))
    late = lambda n: pl.BlockSpec((ATTN_BLOCK, n), lambda i: (i, 0))
    dq, dk_late, dv_late, dsinks = pl.pallas_call(
        body, name="attn_bwd", grid=(nb + 1,),
        in_specs=[pl.BlockSpec(memory_space=pltpu.SMEM), cur(Q_DIM), cur(KV_DIM), prev(KV_DIM), cur(KV_DIM), prev(KV_DIM),
                  cur(Q_DIM), cur(LANES), cur(Q_DIM)],
        out_specs=[cur(Q_DIM), late(KV_DIM), late(KV_DIM), _whole((1, LANES))],
        out_shape=[_sds((t, Q_DIM)), _sds((t + ATTN_BLOCK, KV_DIM)), _sds((t + ATTN_BLOCK, KV_DIM)), _sds((1, LANES))],
        scratch_shapes=[pltpu.VMEM((ATTN_BLOCK, KV_DIM), F32), pltpu.VMEM((ATTN_BLOCK, KV_DIM), F32)],
        compiler_params=_params(("arbitrary",)),
    )(sinks, q, k, k, v, v, o, lse, do)
    return dq, dk_late[ATTN_BLOCK:], dv_late[ATTN_BLOCK:], dsinks


def attn_out_fwd(x, o, gate, w_out):
    t = x.shape[0]
    tm = min(ROW_TILE_FWD, t)

    def body(x_ref, o_ref, gate_ref, w_ref, xo_ref):
        xo_ref[...] = x_ref[...] + _mm(o_ref[...] * _silu(gate_ref[...]), w_ref[...])

    row = _rows(tm, D_MODEL)
    return pl.pallas_call(
        body, name="attn_out_fwd", grid=(t // tm,),
        in_specs=[row, row, row, _whole((Q_DIM, D_MODEL))], out_specs=row, out_shape=_sds((t, D_MODEL)),
        compiler_params=_params(("parallel",)),
    )(x, o, gate, w_out)


def attn_out_bwd(dxo, o, gate, w_out):
    t = dxo.shape[0]
    tm = min(ROW_TILE_BWD, t)

    def body(dxo_ref, o_ref, gate_ref, wt_ref, do_ref, dgate_ref, dw_ref):
        @pl.when(pl.program_id(0) == 0)
        def _():
            dw_ref[...] = jnp.zeros_like(dw_ref)

        dxo = dxo_ref[...]
        o = o_ref[...]
        gate = gate_ref[...]
        sgate = _silu(gate)
        da = _mm_nt(dxo, wt_ref[...])
        dw_ref[...] += _mm_tn(o * sgate, dxo)
        do_ref[...] = da * sgate
        dgate_ref[...] = da * o * _silu_grad(gate)

    row = _rows(tm, D_MODEL)
    mat = _whole((Q_DIM, D_MODEL))
    return pl.pallas_call(
        body, name="attn_out_bwd", grid=(t // tm,),
        in_specs=[row, row, row, mat], out_specs=[row, row, mat],
        out_shape=[_sds((t, Q_DIM)), _sds((t, BRANCH)), _sds((Q_DIM, D_MODEL))],
        compiler_params=_params(("arbitrary",)),
    )(dxo, o, gate, w_out)


def attn_proj_bwd(x, norm, dxo, dq, dk, dv, dgate, cos2, sin2, w_in):
    t = x.shape[0]
    tm = min(ROW_TILE_BWD, t)

    def body(x_ref, g_ref, dxo_ref, dq_ref, dk_ref, dv_ref, dgate_ref, cos_ref, sin_ref, wt_ref, dx_ref, dw_ref, dg_ref):
        @pl.when(pl.program_id(0) == 0)
        def _():
            dw_ref[...] = jnp.zeros_like(dw_ref)
            dg_ref[...] = jnp.zeros_like(dg_ref)

        g = g_ref[...]
        r, xhat, h = _rms(x_ref[...], g)
        cs = cos_ref[...]
        sn = sin_ref[...]
        dqr = dq_ref[...]
        dkr = dk_ref[...]
        dq = dqr * _tile_lanes(cs, Q_DIM // LANES) + _swap_half_heads(dqr * _tile_lanes(sn, Q_DIM // LANES))
        dk = dkr * cs + _swap_half_heads(dkr * sn)
        dproj = jnp.concatenate([dq, dk, dv_ref[...], dgate_ref[...]], axis=1)
        dh = _mm_nt(dproj, wt_ref[...])
        dw_ref[...] += _mm_tn(h, dproj)
        dx, dg = _rms_bwd(dh, g, r, xhat)
        dg_ref[...] += dg
        dx_ref[...] = dxo_ref[...] + dx

    row = _rows(tm, D_MODEL)
    vec = _whole((1, D_MODEL))
    return pl.pallas_call(
        body, name="attn_proj_bwd", grid=(t // tm,),
        in_specs=[row, vec, row, _rows(tm, Q_DIM), _rows(tm, KV_DIM), _rows(tm, KV_DIM), _rows(tm, BRANCH),
                  _rows(tm, LANES), _rows(tm, LANES), _whole((D_MODEL, ATTN_N))],
        out_specs=[row, _whole((D_MODEL, ATTN_N)), vec],
        out_shape=[_sds((t, D_MODEL)), _sds((D_MODEL, ATTN_N)), _sds((1, D_MODEL))],
        compiler_params=_params(("arbitrary",)),
    )(x, norm, dxo, dq, dk, dv, dgate, cos2, sin2, w_in)


def loss_head(x, norm, target):
    t = x.shape[0]
    tm = min(ROW_TILE_FWD, t)

    def body(x_ref, g_ref, tgt_ref, loss_ref, dx_ref, dg_ref):
        @pl.when(pl.program_id(0) == 0)
        def _():
            loss_ref[...] = jnp.zeros_like(loss_ref)
            dg_ref[...] = jnp.zeros_like(dg_ref)

        g = g_ref[...]
        r, xhat, y = _rms(x_ref[...], g)
        err = y - tgt_ref[...]
        loss_ref[...] += 0.5 * jnp.sum(jnp.mean(err * err, axis=-1, keepdims=True), axis=0, keepdims=True)
        dx, dg = _rms_bwd(err * (1.0 / D_MODEL), g, r, xhat)
        dg_ref[...] += dg
        dx_ref[...] = dx

    row = _rows(tm, D_MODEL)
    vec = _whole((1, D_MODEL))
    return pl.pallas_call(
        body, name="loss_head", grid=(t // tm,),
        in_specs=[row, vec, row], out_specs=[_whole((1, 1)), row, vec],
        out_shape=[_sds((1, 1)), _sds((t, D_MODEL)), _sds((1, D_MODEL))],
        compiler_params=_params(("arbitrary",)),
    )(x, norm, target)


OCT_TILE = pl.BlockSpec((None, LANES, LANES), lambda b: (b, 0, 0))
N_LAGS = S5_CHUNK + 1


def _cmul(ar, ai, br, bi):
    return ar * br - ai * bi, ar * bi + ai * br


def _cmul_conj(ar, ai, br, bi):
    return ar * br + ai * bi, ar * bi - ai * br


def _mm_f32(a, b, dims):
    return lax.dot_general(a, b, (dims, ((), ())), precision=lax.Precision.HIGHEST, preferred_element_type=F32)


def _s5_discretise(ar, ai, ls, br, bi):
    dt = jnp.exp(ls)
    xr = ar * dt
    xi = ai * dt
    mag = jnp.exp(xr)
    first = (mag * jnp.cos(xi), mag * jnp.sin(xi))
    powers = [(jnp.ones_like(xr), jnp.zeros_like(xr)), first]
    for _ in range(2, N_LAGS):
        powers.append(_cmul(*powers[-1], *first))
    den = ar * ar + ai * ai
    nr = powers[1][0] - 1.0
    ni = powers[1][1]
    fr = (nr * ar + ni * ai) / den
    fi = (ni * ar - nr * ai) / den
    bbr, bbi = _cmul(fr, fi, br, bi)
    return dt, powers, (fr, fi), (bbr, bbi), den


def _same_group_tile():
    row = lax.broadcasted_iota(jnp.int32, (LANES, LANES), 0)
    col = lax.broadcasted_iota(jnp.int32, (LANES, LANES), 1)
    return (row // SSM_GROUP) == (col // SSM_GROUP)


def _first_copy_lanes():
    return lax.broadcasted_iota(jnp.int32, (LANES, LANES), 1) < SSM_STATE


def s5_param_fwd(tiles):
    def body(ar_ref, ai_ref, ls_ref, br_ref, bi_ref, cr_ref, ci_ref, kd_ref, wsr_ref, wsi_ref, wor_ref, woi_ref, pr_ref, pi_ref):
        cr = cr_ref[...]
        ci = ci_ref[...]
        _, powers, _, (bbr, bbi), _ = _s5_discretise(ar_ref[...], ai_ref[...], ls_ref[...], br_ref[...], bi_ref[...])
        once = _first_copy_lanes()
        crm = jnp.where(once, cr, 0.0)
        cim = jnp.where(once, ci, 0.0)
        same = _same_group_tile()
        for lag in range(S5_CHUNK):
            er, ei = powers[lag]
            xr, xi = _cmul(er, ei, bbr, bbi)
            rows = pl.ds((S5_CHUNK - 1 - lag) * LANES, LANES)
            wsr_ref[rows, :] = xr
            wsi_ref[rows, :] = xi
            k = _mm_f32(xr, crm, ((1,), (1,))) - _mm_f32(xi, cim, ((1,), (1,)))
            kd_ref[lag] = jnp.where(same, k, 0.0)
        for t in range(S5_CHUNK):
            er, ei = powers[t + 1]
            zr, zi = _cmul(er, ei, cr, ci)
            wor_ref[pl.ds(t * LANES, LANES), :] = zr
            woi_ref[pl.ds(t * LANES, LANES), :] = -zi
        pr_ref[...] = powers[S5_CHUNK][0]
        pi_ref[...] = powers[S5_CHUNK][1]

    return pl.pallas_call(
        body, name="s5_param_fwd", grid=(S5_OCTETS,),
        in_specs=[OCT_TILE] * 7, out_specs=[OCT_KD, OCT_W, OCT_W, OCT_W, OCT_W, OCT_TILE, OCT_TILE],
        out_shape=[_sds((S5_OCTETS, S5_CHUNK, LANES, LANES))] + [_sds((S5_OCTETS, S5_OCT_IN, LANES))] * 4
                  + [_sds((S5_OCTETS, LANES, LANES))] * 2,
        compiler_params=_params(("parallel",)),
    )(*tiles)


def s5_param_bwd(tiles, dkd, dws_re, dws_im, dwo_re, dwo_im, dp_re, dp_im):
    def body(ar_ref, ai_ref, ls_ref, br_ref, bi_ref, cr_ref, ci_ref, dkd_ref, dwsr_ref, dwsi_ref, dwor_ref, dwoi_ref, dpr_ref, dpi_ref,
             dar_ref, dai_ref, dls_ref, dbr_ref, dbi_ref, dcr_ref, dci_ref):
        ar = ar_ref[...]
        ai = ai_ref[...]
        br = br_ref[...]
        bi = bi_ref[...]
        cr = cr_ref[...]
        ci = ci_ref[...]
        dt, powers, (fr, fi), (bbr, bbi), den = _s5_discretise(ar, ai, ls_ref[...], br, bi)
        once = _first_copy_lanes()
        crm = jnp.where(once, cr, 0.0)
        cim = jnp.where(once, ci, 0.0)
        same = _same_group_tile()
        zero = jnp.zeros((LANES, LANES), F32)
        dpow = [[zero, zero] for _ in range(N_LAGS)]
        dbbr, dbbi, dcr, dci = zero, zero, zero, zero
        for lag in range(S5_CHUNK):
            er, ei = powers[lag]
            xr, xi = _cmul(er, ei, bbr, bbi)
            rows = pl.ds((S5_CHUNK - 1 - lag) * LANES, LANES)
            g = jnp.where(same, dkd_ref[lag], 0.0)
            dxr = dwsr_ref[rows, :] + _mm_f32(g, crm, ((1,), (0,)))
            dxi = dwsi_ref[rows, :] - _mm_f32(g, cim, ((1,), (0,)))
            dcr = dcr + jnp.where(once, _mm_f32(g, xr, ((0,), (0,))), 0.0)
            dci = dci - jnp.where(once, _mm_f32(g, xi, ((0,), (0,))), 0.0)
            a, b = _cmul_conj(bbr, bbi, dxr, dxi)
            dpow[lag][0] = dpow[lag][0] + a
            dpow[lag][1] = dpow[lag][1] + b
            a, b = _cmul_conj(er, ei, dxr, dxi)
            dbbr = dbbr + a
            dbbi = dbbi + b
        for t in range(S5_CHUNK):
            er, ei = powers[t + 1]
            dzr = dwor_ref[pl.ds(t * LANES, LANES), :]
            dzi = -dwoi_ref[pl.ds(t * LANES, LANES), :]
            a, b = _cmul_conj(cr, ci, dzr, dzi)
            dpow[t + 1][0] = dpow[t + 1][0] + a
            dpow[t + 1][1] = dpow[t + 1][1] + b
            a, b = _cmul_conj(er, ei, dzr, dzi)
            dcr = dcr + a
            dci = dci + b
        dpow[S5_CHUNK][0] = dpow[S5_CHUNK][0] + dpr_ref[...]
        dpow[S5_CHUNK][1] = dpow[S5_CHUNK][1] + dpi_ref[...]
        dfr, dfi = _cmul_conj(br, bi, dbbr, dbbi)
        dbr, dbi = _cmul_conj(fr, fi, dbbr, dbbi)
        dnr, dni = _cmul(ar / den, ai / den, dfr, dfi)
        qr = (fr * ar + fi * ai) / den
        qi = (fi * ar - fr * ai) / den
        dlr, dli = _cmul(-qr, qi, dfr, dfi)
        dpow[1][0] = dpow[1][0] + dnr
        dpow[1][1] = dpow[1][1] + dni
        dxr, dxi = zero, zero
        for lag in range(1, N_LAGS):
            a, b = _cmul_conj(powers[lag][0], powers[lag][1], dpow[lag][0], dpow[lag][1])
            dxr = dxr + lag * a
            dxi = dxi + lag * b
        dar_ref[...] = dlr + dt * dxr
        dai_ref[...] = dli + dt * dxi
        dls_ref[...] = dt * (ar * dxr + ai * dxi)
        dbr_ref[...] = dbr
        dbi_ref[...] = dbi
        dcr_ref[...] = dcr
        dci_ref[...] = dci

    return pl.pallas_call(
        body, name="s5_param_bwd", grid=(S5_OCTETS,),
        in_specs=[OCT_TILE] * 7 + [OCT_KD, OCT_W, OCT_W, OCT_W, OCT_W, OCT_TILE, OCT_TILE], out_specs=[OCT_TILE] * 7,
        out_shape=[_sds((S5_OCTETS, LANES, LANES))] * 7,
        compiler_params=_params(("parallel",)),
    )(*tiles, dkd, dws_re, dws_im, dwo_re, dwo_im, dp_re, dp_im)


def _doubled(v):
    return jnp.concatenate([v, v], axis=-1)


def _s5_param_tiles(a_re, a_im, log_step, b_re, b_im, c_re, c_im):
    def per_group(a):
        return _doubled(jnp.broadcast_to(a.reshape(S5_OCTETS, S5_OCT, 1, SSM_STATE),
                                         (S5_OCTETS, S5_OCT, SSM_GROUP, SSM_STATE)).reshape(S5_OCTETS, LANES, SSM_STATE))

    ls = jnp.broadcast_to(log_step.reshape(S5_OCTETS, S5_OCT, 1, 1), (S5_OCTETS, S5_OCT, SSM_GROUP, LANES)).reshape(S5_OCTETS, LANES, LANES)
    bt = lambda b: _doubled(b.transpose(0, 2, 1).reshape(S5_OCTETS, LANES, SSM_STATE))
    ct = lambda c: _doubled(c.reshape(S5_OCTETS, LANES, SSM_STATE))
    return [per_group(a_re), per_group(a_im), ls, bt(b_re), bt(b_im), ct(c_re), ct(c_im)]


def _s5_param_grads(dtiles):
    dar, dai, dls, dbr, dbi, dcr, dci = dtiles
    halves = lambda d: d[..., :SSM_STATE] + d[..., SSM_STATE:]
    per_group = lambda d: halves(d).reshape(SSM_GROUPS, SSM_GROUP, SSM_STATE).sum(axis=1)
    per_row = lambda d: halves(d).reshape(SSM_GROUPS, SSM_GROUP, SSM_STATE)
    return (per_group(dar), per_group(dai), dls.reshape(SSM_GROUPS, SSM_GROUP * LANES).sum(axis=1),
            per_row(dbr).transpose(0, 2, 1), per_row(dbi).transpose(0, 2, 1), per_row(dcr), per_row(dci))


def _group_power_rows(tile):
    return tile[:, ::SSM_GROUP, :SSM_STATE].reshape(1, S5_STATES)


def _group_power_tiles(row):
    t = jnp.pad(row.reshape(S5_OCTETS, S5_OCT, 1, SSM_STATE), ((0, 0), (0, 0), (0, SSM_GROUP - 1), (0, LANES - SSM_STATE)))
    return t.reshape(S5_OCTETS, LANES, LANES)


def _rope_tables(t):
    pos = jnp.arange(t, dtype=F32)
    inv_freq = ROPE_THETA ** (-jnp.arange(0, HEAD_DIM, 2, dtype=F32) / HEAD_DIM)
    ang = pos[:, None] * inv_freq[None, :]
    cos = jnp.cos(ang)
    sin = jnp.sin(ang)
    cos64 = jnp.concatenate([cos, cos], axis=1)
    sin64 = jnp.concatenate([-sin, sin], axis=1)
    return jnp.concatenate([cos64, cos64], axis=1), jnp.concatenate([sin64, sin64], axis=1)


def _row(v):
    return v.reshape(1, -1)


def _ssm_forward(x, w):
    tiles = _s5_param_tiles(w["a_re"], w["a_im"], w["log_step"], w["b_re"], w["b_im"], w["c_re"], w["c_im"])
    kd, ws_re, ws_im, wo_re, wo_im, p_re, p_im = s5_param_fwd(tiles)
    mats = dict(kd=kd, ws_re=ws_re, ws_im=ws_im, wo_re=wo_re, wo_im=wo_im, a_re=_group_power_rows(p_re), a_im=_group_power_rows(p_im))
    u, gate = ssm_proj_fwd(x, _row(w["norm"]), w["w_in"])
    s_re, s_im = s5_chunk_states(u, mats["ws_re"], mats["ws_im"])
    h_re, h_im = s5_scan_fwd(s_re, s_im, mats["a_re"], mats["a_im"])
    y_scan = s5_outputs(u, h_re, h_im, mats["kd"], mats["wo_re"], mats["wo_im"])
    y, g2, x_new = ssm_mix_fwd(x, u, gate, y_scan, _row(w["d"]), w["w_glu"], _row(w["b_glu"]), w["w_out"])
    saved = dict(x=x, u=u, gate=gate, y=y, g2=g2, h_re=h_re, h_im=h_im, mats=mats, tiles=tiles)
    return x_new, saved


def _ssm_backward(dxo, w, s):
    dy, dgate, dw_out, dw_glu, db_glu, dd = ssm_mix_bwd(dxo, s["u"], s["gate"], s["y"], s["g2"], w["w_glu"], w["w_out"])
    mats = s["mats"]
    dh_re, dh_im = s5_state_grads(dy, mats["wo_re"], mats["wo_im"])
    ds_re, ds_im, da_re, da_im = s5_scan_bwd(dh_re, dh_im, s["h_re"], s["h_im"], mats["a_re"], mats["a_im"])
    du_scan = s5_input_grads(dy, ds_re, ds_im, mats["kd"], mats["ws_re"], mats["ws_im"])
    dkd, dws_re, dws_im, dwo_re, dwo_im = s5_weight_grads(s["u"], dy, s["h_re"], s["h_im"], ds_re, ds_im)
    dparams = _s5_param_grads(s5_param_bwd(s["tiles"], dkd, dws_re, dws_im, dwo_re, dwo_im,
                                           _group_power_tiles(da_re), _group_power_tiles(da_im)))
    dx, dw_in, dnorm = ssm_proj_bwd(s["x"], _row(w["norm"]), dxo, dy, du_scan, dgate, _row(w["d"]), w["w_in"])
    grads = dict(norm=dnorm, w_in=dw_in, d=dd, w_glu=dw_glu, b_glu=db_glu, w_out=dw_out)
    for name, val in zip(("a_re", "a_im", "log_step", "b_re", "b_im", "c_re", "c_im"), dparams):
        grads[name] = val
    return dx, grads


def _attn_forward(x, w, cos2, sin2):
    q, k, v, gate = attn_proj_fwd(x, _row(w["norm"]), w["w_in"], cos2, sin2)
    o, lse = attn_fwd(q, k, v, w["sinks"])
    x_new = attn_out_fwd(x, o, gate, w["w_out"])
    return x_new, dict(x=x, q=q, k=k, v=v, gate=gate, o=o, lse=lse)


def _attn_backward(dxo, w, s, cos2, sin2):
    do, dgate, dw_out = attn_out_bwd(dxo, s["o"], s["gate"], w["w_out"])
    dq, dk, dv, dsinks = attn_bwd(s["q"], s["k"], s["v"], w["sinks"], s["o"], s["lse"], do)
    dx, dw_in, dnorm = attn_proj_bwd(s["x"], _row(w["norm"]), dxo, dq, dk, dv, dgate, cos2, sin2, w["w_in"])
    return dx, dict(norm=dnorm, w_in=dw_in, sinks=dsinks[0, :N_Q_HEADS], w_out=dw_out)


def _sequence_step(x, target, layers, final_norm):
    cos2, sin2 = _rope_tables(x.shape[0])
    saved = []
    for i, w in enumerate(layers):
        if i % 2 == 0:
            x, s = _ssm_forward(x, w)
        else:
            x, s = _attn_forward(x, w, cos2, sin2)
        saved.append(s)
    loss, dx, dfinal = loss_head(x, _row(final_norm), target)
    grads = {"final_norm": dfinal}
    for i in reversed(range(len(layers))):
        if i % 2 == 0:
            dx, g = _ssm_backward(dx, layers[i], saved[i])
        else:
            dx, g = _attn_backward(dx, layers[i], saved[i], cos2, sin2)
        for name, val in g.items():
            grads["l%d_%s" % (i, name)] = val
    return loss[0, 0], dx, grads


ANY = pl.BlockSpec(memory_space=pl.ANY)


def _place():
    return lax.axis_index("x"), lax.axis_index("y"), lax.axis_index("c")


def _other_chips(x, y):
    return [(1 - x, y), (x, 1 - y), (1 - x, 1 - y)]


class _StagedCopies:
    def __init__(self, bufs, load_sems, store_sems):
        self.bufs, self.load_sems, self.store_sems = bufs, load_sems, store_sems
        self.loads, self.stores = [], []

    def load(self, i, src):
        cp = pltpu.make_async_copy(src, self.bufs[i], self.load_sems.at[i])
        cp.start()
        self.loads.append(cp)

    def store(self, i, dst):
        self.loads[i].wait()
        cp = pltpu.make_async_copy(self.bufs[i], dst, self.store_sems.at[i])
        cp.start()
        self.stores.append(cp)

    def finish(self):
        for cp in self.stores:
            cp.wait()


def _staging(blocks):
    n = len(blocks)
    return [pltpu.VMEM(b.shape, b.dtype) for b in blocks] + [pltpu.SemaphoreType.DMA((n,)), pltpu.SemaphoreType.DMA((n,))]


def gather_weight_shards(shards):
    n = len(shards)

    def body(*refs):
        ins, outs = refs[:n], refs[n:2 * n]
        send_sems, recv_sems, pass_send_sems, pass_recv_sems = refs[2 * n:2 * n + 4]
        own = _StagedCopies(refs[2 * n + 4:3 * n + 4], *refs[3 * n + 4:])
        x, y, c = _place()
        me = 2 * x + y
        chips = _other_chips(x, y)

        def half(i, block, which):
            rows = ins[i].shape[0] // 2
            return outs[i].at[block, pl.ds(which * rows, rows), :]

        def my_half(i):
            rows = ins[i].shape[0] // 2
            return ins[i].at[pl.ds(c * rows, rows), :]

        for i in range(n):
            own.load(i, ins[i])
        sends = []
        for i in range(n):
            for k, (tx, ty) in enumerate(chips):
                cp = pltpu.make_async_remote_copy(src_ref=my_half(i), dst_ref=half(i, me, c), send_sem=send_sems.at[i, k],
                                                  recv_sem=recv_sems.at[i, k], device_id=(tx, ty, c), device_id_type=MESH)
                cp.start()
                sends.append(cp)
        for i in range(n):
            own.store(i, outs[i].at[me])
        for i in range(n):
            for k, (tx, ty) in enumerate(chips):
                landed = half(i, 2 * tx + ty, c)
                pltpu.make_async_remote_copy(src_ref=my_half(i), dst_ref=landed, send_sem=send_sems.at[i, k],
                                             recv_sem=recv_sems.at[i, k], device_id=(tx, ty, c), device_id_type=MESH).wait_recv()
                cp = pltpu.make_async_remote_copy(src_ref=landed, dst_ref=landed, send_sem=pass_send_sems.at[i, k],
                                                  recv_sem=pass_recv_sems.at[i, k], device_id=(x, y, 1 - c), device_id_type=MESH)
                cp.start()
                sends.append(cp)
        for i in range(n):
            for k, (tx, ty) in enumerate(chips):
                missing = half(i, 2 * tx + ty, 1 - c)
                pltpu.make_async_remote_copy(src_ref=missing, dst_ref=missing, send_sem=pass_send_sems.at[i, k],
                                             recv_sem=pass_recv_sems.at[i, k], device_id=(x, y, 1 - c), device_id_type=MESH).wait_recv()
        for cp in sends:
            cp.wait_send()
        own.finish()

    sems = pltpu.SemaphoreType.DMA((n, 3))
    return pl.pallas_call(
        body, name="gather_weight_shards",
        in_specs=[ANY] * n, out_specs=[ANY] * n,
        out_shape=[_sds((4,) + s.shape, s.dtype) for s in shards],
        scratch_shapes=[sems, sems, sems, sems] + _staging(shards),
        compiler_params=_params(),
    )(*shards)


def exchange_halves_with_sibling(grads):
    n = len(grads)

    def body(*refs):
        ins, outs = refs[:n], refs[n:2 * n]
        send_sems, recv_sems = refs[2 * n:]
        x, y, c = _place()
        copies = []
        for i in range(n):
            half = ins[i].shape[1] // 2
            src = ins[i].at[:, pl.ds((1 - c) * half, half), :]
            cp = pltpu.make_async_remote_copy(src_ref=src, dst_ref=outs[i], send_sem=send_sems.at[i], recv_sem=recv_sems.at[i],
                                              device_id=(x, y, 1 - c), device_id_type=MESH)
            cp.start()
            copies.append(cp)
        for cp in copies:
            cp.wait()

    return pl.pallas_call(
        body, name="exchange_halves_with_sibling",
        in_specs=[ANY] * n, out_specs=[ANY] * n,
        out_shape=[_sds((g.shape[0], g.shape[1] // 2, g.shape[2])) for g in grads],
        scratch_shapes=[pltpu.SemaphoreType.DMA((n,)), pltpu.SemaphoreType.DMA((n,))],
    )(*grads)


def scatter_blocks_to_chips(sums):
    n = len(sums)

    def body(*refs):
        ins, outs = refs[:n], refs[n:2 * n]
        send_sems, recv_sems = refs[2 * n:2 * n + 2]
        own = _StagedCopies(refs[2 * n + 2:3 * n + 2], *refs[3 * n + 2:])
        x, y, c = _place()
        me = 2 * x + y

        def block_for(i, chip):
            return ins[i].at[chip] if ins[i].shape[0] == 4 else ins[i].at[0]

        for i in range(n):
            own.load(i, block_for(i, me))
        sends = []
        for i in range(n):
            for k, (tx, ty) in enumerate(_other_chips(x, y)):
                cp = pltpu.make_async_remote_copy(src_ref=block_for(i, 2 * tx + ty), dst_ref=outs[i].at[me], send_sem=send_sems.at[i, k],
                                                  recv_sem=recv_sems.at[i, k], device_id=(tx, ty, c), device_id_type=MESH)
                cp.start()
                sends.append(cp)
        for i in range(n):
            own.store(i, outs[i].at[me])
        for i in range(n):
            for k, (tx, ty) in enumerate(_other_chips(x, y)):
                pltpu.make_async_remote_copy(src_ref=block_for(i, me), dst_ref=outs[i].at[2 * tx + ty], send_sem=send_sems.at[i, k],
                                             recv_sem=recv_sems.at[i, k], device_id=(tx, ty, c), device_id_type=MESH).wait_recv()
        for cp in sends:
            cp.wait_send()
        own.finish()

    return pl.pallas_call(
        body, name="scatter_blocks_to_chips",
        in_specs=[ANY] * n, out_specs=[ANY] * n,
        out_shape=[_sds((4,) + s.shape[1:], s.dtype) for s in sums],
        scratch_shapes=[pltpu.SemaphoreType.DMA((n, 3)), pltpu.SemaphoreType.DMA((n, 3))] + _staging([_sds(s.shape[1:], s.dtype) for s in sums]),
        compiler_params=_params(),
    )(*sums)


def swap_halves_with_sibling(pieces):
    n = len(pieces)

    def body(*refs):
        ins, outs = refs[:n], refs[n:2 * n]
        send_sems, recv_sems = refs[2 * n:2 * n + 2]
        own = _StagedCopies(refs[2 * n + 2:3 * n + 2], *refs[3 * n + 2:])
        x, y, c = _place()
        for i in range(n):
            own.load(i, ins[i])
        swaps = []
        for i in range(n):
            cp = pltpu.make_async_remote_copy(src_ref=ins[i], dst_ref=outs[i].at[c], send_sem=send_sems.at[i], recv_sem=recv_sems.at[i],
                                              device_id=(x, y, 1 - c), device_id_type=MESH)
            cp.start()
            swaps.append(cp)
        for i in range(n):
            own.store(i, outs[i].at[c])
        for i in range(n):
            pltpu.make_async_remote_copy(src_ref=ins[i], dst_ref=outs[i].at[1 - c], send_sem=send_sems.at[i], recv_sem=recv_sems.at[i],
                                         device_id=(x, y, 1 - c), device_id_type=MESH).wait_recv()
        for cp in swaps:
            cp.wait_send()
        own.finish()

    return pl.pallas_call(
        body, name="swap_halves_with_sibling",
        in_specs=[ANY] * n, out_specs=[ANY] * n,
        out_shape=[_sds((2,) + p.shape) for p in pieces],
        scratch_shapes=[pltpu.SemaphoreType.DMA((n,)), pltpu.SemaphoreType.DMA((n,))] + _staging(pieces),
        compiler_params=_params(),
    )(*pieces)


def _row_tile(rows, cols):
    tm = rows
    while tm * cols * 4 > (2 << 20) and tm % 16 == 0:
        tm //= 2
    return tm


def add_pair(a, b, out_dtype):
    nb, rows, cols = a.shape
    tm = _row_tile(rows, cols)

    def body(a_ref, b_ref, o_ref):
        o_ref[...] = (a_ref[...] + b_ref[...]).astype(out_dtype)

    spec = pl.BlockSpec((None, tm, cols), lambda j, i: (j, i, 0))
    return pl.pallas_call(
        body, name="add_pair", grid=(nb, rows // tm), in_specs=[spec, spec], out_specs=spec, out_shape=_sds(a.shape, out_dtype),
        compiler_params=_params(("parallel", "parallel")),
    )(a, b)


def sum_four(a):
    _, rows, cols = a.shape
    tm = _row_tile(rows, cols)

    def body(a_ref, o_ref):
        o_ref[...] = ((a_ref[0].astype(F32) + a_ref[1].astype(F32)) + a_ref[2].astype(F32)) + a_ref[3].astype(F32)

    return pl.pallas_call(
        body, name="sum_four", grid=(rows // tm,),
        in_specs=[pl.BlockSpec((4, tm, cols), lambda i: (0, i, 0))], out_specs=pl.BlockSpec((tm, cols), lambda i: (i, 0)),
        out_shape=_sds((rows, cols)), compiler_params=_params(("parallel",)),
    )(a)


def adamw(w, g, m, v):
    rows, cols = w.shape
    tm = _row_tile(rows, cols)
    c1 = 1.0 - ADAM_B1 ** ADAM_STEP
    c2 = 1.0 - ADAM_B2 ** ADAM_STEP

    def body(w_ref, g_ref, m_ref, v_ref, d_ref, nm_ref, nv_ref):
        g = g_ref[...]
        nm = ADAM_B1 * m_ref[...] + (1.0 - ADAM_B1) * g
        nv = ADAM_B2 * v_ref[...] + (1.0 - ADAM_B2) * (g * g)
        d_ref[...] = -ADAM_LR * ((nm / c1) / (jnp.sqrt(nv / c2) + ADAM_EPS) + ADAM_WD * w_ref[...])
        nm_ref[...] = nm
        nv_ref[...] = nv

    spec = pl.BlockSpec((tm, cols), lambda i: (i, 0))
    return pl.pallas_call(
        body, name="adamw", grid=(rows // tm,), in_specs=[spec] * 4, out_specs=[spec] * 3,
        out_shape=[_sds(w.shape)] * 3, compiler_params=_params(("parallel",)),
    )(w, g, m, v)


PACK_TILE = 8 * LANES
PACK_PIECES = 8
PACK_ALIGN = PACK_PIECES * 16


def _pack_small(values, scalar=None):
    parts = []
    for name in SMALL_NAMES:
        flat = values[name].reshape(-1)
        pad = (-flat.shape[0]) % PACK_TILE
        if pad:
            flat = jnp.concatenate([flat, jnp.zeros((pad,), F32)])
        parts.append(flat.reshape(-1, LANES))
    rows = sum(p.shape[0] for p in parts) + 8
    parts.append(jnp.zeros(((-rows) % PACK_ALIGN, LANES), F32))
    last = jnp.zeros((8, LANES), F32)
    parts.append(last if scalar is None else jnp.broadcast_to(scalar.astype(F32), (8, LANES)))
    return jnp.concatenate(parts, axis=0)


def _unpack_small(pack, like):
    out = {}
    row = 0
    for name in SMALL_NAMES:
        size = math.prod(like[name].shape)
        rows = -(-size // PACK_TILE) * 8
        out[name] = pack[row:row + rows].reshape(-1)[:size].reshape(like[name].shape)
        row += rows
    return out


def _is_column_sharded(name):
    return name.endswith("w_in")


def _to_blocks(name, full):
    if _is_column_sharded(name):
        rows, cols = full.shape
        return full.reshape(rows, 4, cols // 4).transpose(1, 0, 2)
    return full.reshape(4, full.shape[0] // 4, full.shape[1])


def _from_blocks(name, stacked):
    if _is_column_sharded(name):
        return stacked.transpose(1, 0, 2).reshape(stacked.shape[1], 4 * stacked.shape[2])
    return stacked.reshape(4 * stacked.shape[1], stacked.shape[2])


def _train_step(x, loss_target, weights, moments_m, moments_v):
    c = lax.axis_index("c")
    gathered = gather_weight_shards([weights[n].astype(MXU_DTYPE) for n in BIG_NAMES])
    full = {n: _from_blocks(n, g) for n, g in zip(BIG_NAMES, gathered)}
    layers = []
    for i in range(4):
        names = SSM_NAMES if i % 2 == 0 else ATTN_NAMES
        w = {}
        for n in names:
            key = "l%d_%s" % (i, n)
            if key in full:
                w[n] = full[key]
            else:
                w[n] = weights[key]
        layers.append(w)
    loss, dx, grads = _sequence_step(x[0], loss_target[0], layers, weights["final_norm"])
    small_pack = _pack_small({n: grads[n] for n in SMALL_NAMES}, scalar=loss)
    blocks = [_to_blocks(n, grads[n]) for n in BIG_NAMES] + [small_pack[None]]
    from_sibling = exchange_halves_with_sibling(blocks)
    chip_sums = []
    for i, (b, r) in enumerate(zip(blocks, from_sibling)):
        half = b.shape[1] // 2
        dtype = WIRE_DTYPE if i < len(BIG_NAMES) else F32
        chip_sums.append(add_pair(lax.dynamic_slice_in_dim(b, c * half, half, axis=1), r, dtype))
    contributions = scatter_blocks_to_chips(chip_sums)
    reduced = [sum_four(a) for a in contributions]
    shared = swap_halves_with_sibling(reduced)
    big_grads = {n: s.reshape(2 * s.shape[1], s.shape[2]) for n, s in zip(BIG_NAMES, shared[:-1])}
    small_grad_pack = shared[-1].reshape(-1, LANES)
    loss = small_grad_pack[-8, 0]
    out_grad, out_delta, out_m, out_v = {}, {}, {}, {}
    for n in BIG_NAMES:
        out_grad[n] = big_grads[n]
        out_delta[n], out_m[n], out_v[n] = adamw(weights[n], big_grads[n], moments_m[n], moments_v[n])
    small_like = {n: weights[n] for n in SMALL_NAMES}
    d_pack, m_pack, v_pack = adamw(_pack_small(small_like), small_grad_pack, _pack_small({n: moments_m[n] for n in SMALL_NAMES}),
                                   _pack_small({n: moments_v[n] for n in SMALL_NAMES}))
    out_grad.update(_unpack_small(small_grad_pack, small_like))
    out_delta.update(_unpack_small(d_pack, small_like))
    out_m.update(_unpack_small(m_pack, small_like))
    out_v.update(_unpack_small(v_pack, small_like))
    outs = [loss, dx[None]]
    for group in (out_grad, out_delta, out_m, out_v):
        outs.extend(group[n] for n in WEIGHT_NAMES)
    return tuple(outs)


def kernel(x, l0_norm, l0_w_in, l0_a_re, l0_a_im, l0_log_step, l0_b_re, l0_b_im, l0_c_re, l0_c_im, l0_d, l0_w_glu, l0_b_glu, l0_w_out, l1_norm, l1_w_in, l1_sinks, l1_w_out, l2_norm, l2_w_in, l2_a_re, l2_a_im, l2_log_step, l2_b_re, l2_b_im, l2_c_re, l2_c_im, l2_d, l2_w_glu, l2_b_glu, l2_w_out, l3_norm, l3_w_in, l3_sinks, l3_w_out, final_norm, loss_target, m_l0_norm, m_l0_w_in, m_l0_a_re, m_l0_a_im, m_l0_log_step, m_l0_b_re, m_l0_b_im, m_l0_c_re, m_l0_c_im, m_l0_d, m_l0_w_glu, m_l0_b_glu, m_l0_w_out, m_l1_norm, m_l1_w_in, m_l1_sinks, m_l1_w_out, m_l2_norm, m_l2_w_in, m_l2_a_re, m_l2_a_im, m_l2_log_step, m_l2_b_re, m_l2_b_im, m_l2_c_re, m_l2_c_im, m_l2_d, m_l2_w_glu, m_l2_b_glu, m_l2_w_out, m_l3_norm, m_l3_w_in, m_l3_sinks, m_l3_w_out, m_final_norm, v_l0_norm, v_l0_w_in, v_l0_a_re, v_l0_a_im, v_l0_log_step, v_l0_b_re, v_l0_b_im, v_l0_c_re, v_l0_c_im, v_l0_d, v_l0_w_glu, v_l0_b_glu, v_l0_w_out, v_l1_norm, v_l1_w_in, v_l1_sinks, v_l1_w_out, v_l2_norm, v_l2_w_in, v_l2_a_re, v_l2_a_im, v_l2_log_step, v_l2_b_re, v_l2_b_im, v_l2_c_re, v_l2_c_im, v_l2_d, v_l2_w_glu, v_l2_b_glu, v_l2_w_out, v_l3_norm, v_l3_w_in, v_l3_sinks, v_l3_w_out, v_final_norm):
    args = locals()
    weights = {n: args[n] for n in WEIGHT_NAMES}
    moments_m = {n: args["m_" + n] for n in WEIGHT_NAMES}
    moments_v = {n: args["v_" + n] for n in WEIGHT_NAMES}
    return _train_step(x, loss_target, weights, moments_m, moments_v)
```

```python
import functools
import math

import jax
import jax.numpy as jnp
from jax import lax
from jax.experimental import pallas as pl
from jax.experimental.pallas import tpu as pltpu

F32 = jnp.float32
MXU_DTYPE = jnp.bfloat16
WIRE_DTYPE = jnp.bfloat16
MESH = pl.DeviceIdType.MESH

D_MODEL = 1024
BRANCH = 1024
NORM_EPS = 1e-5
SSM_GROUPS = 64
SSM_GROUP = 16
SSM_STATE = 64
S5_CHUNK = 16
LANES = 128
S5_OCT = LANES // SSM_GROUP
S5_OCTETS = SSM_GROUPS // S5_OCT
S5_OCT_IN = S5_CHUNK * LANES
S5_OCT_STATE = S5_OCT * SSM_STATE
S5_STATES = SSM_GROUPS * SSM_STATE
HEAD_DIM = 64
N_Q_HEADS = 16
N_KV_HEADS = 2
GQA_GROUP = N_Q_HEADS // N_KV_HEADS
ATTN_BLOCK = 128
Q_DIM = N_Q_HEADS * HEAD_DIM
KV_DIM = N_KV_HEADS * HEAD_DIM
ROPE_THETA = 10000.0
NEG_INF = -1e30
ADAM_LR = 0.001
ADAM_B1 = 0.9
ADAM_B2 = 0.999
ADAM_EPS = 1e-08
ADAM_WD = 0.01
ADAM_STEP = 10

VMEM_LIMIT_V7X = 56 * 1024 * 1024
ROW_TILE_FWD = 512
ROW_TILE_BWD = 512

SSM_NAMES = ("norm", "w_in", "a_re", "a_im", "log_step", "b_re", "b_im", "c_re", "c_im", "d", "w_glu", "b_glu", "w_out")
ATTN_NAMES = ("norm", "w_in", "sinks", "w_out")


def _weight_names():
    names = []
    for i in range(4):
        for n in (SSM_NAMES if i % 2 == 0 else ATTN_NAMES):
            names.append("l%d_%s" % (i, n))
    names.append("final_norm")
    return names


WEIGHT_NAMES = _weight_names()
BIG_NAMES = [n for n in WEIGHT_NAMES if n.endswith(("w_in", "w_glu", "w_out"))]
SMALL_NAMES = [n for n in WEIGHT_NAMES if n not in BIG_NAMES]


def _params(semantics=None):
    return pltpu.CompilerParams(dimension_semantics=semantics, vmem_limit_bytes=VMEM_LIMIT_V7X)


def _rows(tm, n):
    return pl.BlockSpec((tm, n), lambda i: (i, 0))


def _whole(shape):
    return pl.BlockSpec(shape, lambda i: (0,) * len(shape), pipeline_mode=pl.Buffered(1))


def _sds(shape, dtype=F32):
    return jax.ShapeDtypeStruct(shape, dtype)


def _mm(a, b):
    return jnp.dot(a.astype(MXU_DTYPE), b.astype(MXU_DTYPE), preferred_element_type=F32)


def _mm_tn(a, b):
    return lax.dot_general(a.astype(MXU_DTYPE), b.astype(MXU_DTYPE), (((0,), (0,)), ((), ())), preferred_element_type=F32)


def _mm_nt(a, b):
    return lax.dot_general(a.astype(MXU_DTYPE), b.astype(MXU_DTYPE), (((1,), (1,)), ((), ())), preferred_element_type=F32)


def _sigmoid(x):
    return 1.0 / (1.0 + jnp.exp(-x))


def _silu(x):
    return x * _sigmoid(x)


def _silu_grad(x):
    s = _sigmoid(x)
    return s * (1.0 + x * (1.0 - s))


GELU_C0 = math.sqrt(2.0 / math.pi)
GELU_C1 = 0.044715


def _gelu(x):
    return 0.5 * x * (1.0 + jnp.tanh(GELU_C0 * (x + GELU_C1 * x * x * x)))


def _gelu_grad(x):
    th = jnp.tanh(GELU_C0 * (x + GELU_C1 * x * x * x))
    return 0.5 * (1.0 + th) + 0.5 * x * (1.0 - th * th) * GELU_C0 * (1.0 + 3.0 * GELU_C1 * x * x)


def _rms(x, g):
    r = lax.rsqrt(jnp.mean(x * x, axis=-1, keepdims=True) + NORM_EPS)
    xhat = x * r
    return r, xhat, xhat * g


def _rms_bwd(dh, g, r, xhat):
    dxhat = dh * g
    dx = r * (dxhat - xhat * jnp.mean(dxhat * xhat, axis=-1, keepdims=True))
    return dx, jnp.sum(dh * xhat, axis=0, keepdims=True)


def _swap_half_heads(x):
    n = x.shape[-1]
    lane = lax.broadcasted_iota(jnp.int32, x.shape, x.ndim - 1)
    first = (lane % HEAD_DIM) < (HEAD_DIM // 2)
    return jnp.where(first, pltpu.roll(x, n - HEAD_DIM // 2, x.ndim - 1), pltpu.roll(x, HEAD_DIM // 2, x.ndim - 1))


def _tile_lanes(t, reps):
    return jnp.concatenate([t] * reps, axis=1)


def ssm_proj_fwd(x, norm, w_in):
    t = x.shape[0]
    tm = min(ROW_TILE_FWD, t)

    def body(x_ref, g_ref, w_ref, u_ref, gate_ref):
        _, _, h = _rms(x_ref[...], g_ref[...])
        p = _mm(h, w_ref[...])
        u_ref[...] = p[:, :BRANCH]
        gate_ref[...] = p[:, BRANCH:]

    return pl.pallas_call(
        body, name="ssm_proj_fwd", grid=(t // tm,),
        in_specs=[_rows(tm, D_MODEL), _whole((1, D_MODEL)), _whole((D_MODEL, 2 * BRANCH))],
        out_specs=[_rows(tm, BRANCH), _rows(tm, BRANCH)],
        out_shape=[_sds((t, BRANCH)), _sds((t, BRANCH))],
        compiler_params=_params(("parallel",)),
    )(x, norm, w_in)


def _chunk_rows(ref, nk, dtype=None):
    rows = jnp.concatenate([ref[pl.ds(s, nk, stride=S5_CHUNK), :] for s in range(S5_CHUNK)], axis=1)
    return rows.astype(MXU_DTYPE if dtype is None else dtype)


def _store_chunk_rows(ref, val, nk):
    for s in range(S5_CHUNK):
        ref[pl.ds(s, nk, stride=S5_CHUNK), :] = val[:, s * LANES:(s + 1) * LANES]


def _own_group_mask():
    row = lax.broadcasted_iota(jnp.int32, (S5_OCT_IN, S5_OCT_STATE), 0)
    col = lax.broadcasted_iota(jnp.int32, (S5_OCT_IN, S5_OCT_STATE), 1)
    return ((row % LANES) // SSM_GROUP) == (col // SSM_STATE)


def _spread_groups(w):
    return jnp.where(_own_group_mask(), jnp.concatenate([w] * (S5_OCT_STATE // LANES), axis=1), 0.0).astype(MXU_DTYPE)


def _fold_groups(p):
    p = jnp.where(_own_group_mask(), p, 0.0)
    return sum(p[:, q * LANES:(q + 1) * LANES] for q in range(S5_OCT_STATE // LANES))


def _fill_toeplitz(win_ref, kd_ref):
    win_ref[...] = jnp.zeros_like(win_ref)
    for s in range(S5_CHUNK):
        for t in range(s, S5_CHUNK):
            win_ref[s * LANES:(s + 1) * LANES, t * LANES:(t + 1) * LANES] = kd_ref[t - s].astype(MXU_DTYPE)


TOEPLITZ_BLOCK = 512
_TOEPLITZ_BLOCKS = [(lo, lo + TOEPLITZ_BLOCK) for lo in range(0, S5_OCT_IN, TOEPLITZ_BLOCK)]


def _strip(t):
    return pl.BlockSpec((t, LANES), lambda b: (0, b))


def _oct_states(nk):
    return pl.BlockSpec((nk, S5_OCT_STATE), lambda b: (0, b))


OCT_W = pl.BlockSpec((None, S5_OCT_IN, LANES), lambda b: (b, 0, 0))
OCT_KD = pl.BlockSpec((None, S5_CHUNK, LANES, LANES), lambda b: (b, 0, 0, 0))


def s5_chunk_states(u, ws_re, ws_im):
    t = u.shape[0]
    nk = t // S5_CHUNK

    def body(u_ref, wr_ref, wi_ref, re_ref, im_ref):
        uc = _chunk_rows(u_ref, nk)
        re_ref[...] = _mm(uc, _spread_groups(wr_ref[...]))
        im_ref[...] = _mm(uc, _spread_groups(wi_ref[...]))

    return pl.pallas_call(
        body, name="s5_chunk_states", grid=(S5_OCTETS,),
        in_specs=[_strip(t), OCT_W, OCT_W], out_specs=[_oct_states(nk), _oct_states(nk)],
        out_shape=[_sds((nk, S5_STATES)), _sds((nk, S5_STATES))],
        compiler_params=_params(("parallel",)),
    )(u, ws_re, ws_im)


def s5_scan_fwd(s_re, s_im, a_re, a_im):
    nk = s_re.shape[0]

    def body(sre_ref, sim_ref, ar_ref, ai_ref, hre_ref, him_ref):
        ar = ar_ref[...]
        ai = ai_ref[...]

        def step(k, carry):
            hr, hi = carry
            hre_ref[pl.ds(k, 1), :] = hr
            him_ref[pl.ds(k, 1), :] = hi
            sr = sre_ref[pl.ds(k, 1), :]
            si = sim_ref[pl.ds(k, 1), :]
            return ar * hr - ai * hi + sr, ai * hr + ar * hi + si

        zero = jnp.zeros((1, S5_STATES), F32)
        lax.fori_loop(0, nk, step, (zero, zero))

    vm = pl.BlockSpec(memory_space=pltpu.VMEM)
    return pl.pallas_call(
        body, name="s5_scan_fwd", in_specs=[vm, vm, vm, vm], out_specs=[vm, vm],
        out_shape=[_sds((nk, S5_STATES)), _sds((nk, S5_STATES))],
        compiler_params=_params(),
    )(s_re, s_im, a_re, a_im)


def s5_outputs(u, h_re, h_im, kd, wo_re, wo_im):
    t = u.shape[0]
    nk = t // S5_CHUNK

    def body(u_ref, hre_ref, him_ref, kd_ref, wor_ref, woi_ref, y_ref, win_ref):
        _fill_toeplitz(win_ref, kd_ref)
        uc = _chunk_rows(u_ref, nk)
        y = jnp.concatenate([_mm(uc[:, :hi], win_ref[:hi, lo:hi]) for lo, hi in _TOEPLITZ_BLOCKS], axis=1)
        y = y + _mm_nt(hre_ref[...], _spread_groups(wor_ref[...])) + _mm_nt(him_ref[...], _spread_groups(woi_ref[...]))
        _store_chunk_rows(y_ref, y, nk)

    return pl.pallas_call(
        body, name="s5_outputs", grid=(S5_OCTETS,),
        in_specs=[_strip(t), _oct_states(nk), _oct_states(nk), OCT_KD, OCT_W, OCT_W],
        out_specs=_strip(t), out_shape=_sds((t, BRANCH)),
        scratch_shapes=[pltpu.VMEM((S5_OCT_IN, S5_OCT_IN), MXU_DTYPE)],
        compiler_params=_params(("parallel",)),
    )(u, h_re, h_im, kd, wo_re, wo_im)


def s5_state_grads(dy, wo_re, wo_im):
    t = dy.shape[0]
    nk = t // S5_CHUNK

    def body(dy_ref, wor_ref, woi_ref, re_ref, im_ref):
        dyc = _chunk_rows(dy_ref, nk)
        re_ref[...] = _mm(dyc, _spread_groups(wor_ref[...]))
        im_ref[...] = _mm(dyc, _spread_groups(woi_ref[...]))

    return pl.pallas_call(
        body, name="s5_state_grads", grid=(S5_OCTETS,),
        in_specs=[_strip(t), OCT_W, OCT_W], out_specs=[_oct_states(nk), _oct_states(nk)],
        out_shape=[_sds((nk, S5_STATES)), _sds((nk, S5_STATES))],
        compiler_params=_params(("parallel",)),
    )(dy, wo_re, wo_im)


def s5_scan_bwd(dh_re, dh_im, h_re, h_im, a_re, a_im):
    nk = dh_re.shape[0]

    def body(dhr_ref, dhi_ref, hr_ref, hi_ref, ar_ref, ai_ref, dsr_ref, dsi_ref, dar_ref, dai_ref):
        ar = ar_ref[...]
        ai = ai_ref[...]

        dar_ref[...] = jnp.zeros_like(dar_ref)
        dai_ref[...] = jnp.zeros_like(dai_ref)

        def step(i, carry):
            gr, gi = carry
            k = nk - 1 - i
            dhr = dhr_ref[pl.ds(k, 1), :]
            dhi = dhi_ref[pl.ds(k, 1), :]
            dsr_ref[pl.ds(k, 1), :] = gr
            dsi_ref[pl.ds(k, 1), :] = gi
            hr = hr_ref[pl.ds(k, 1), :]
            hi = hi_ref[pl.ds(k, 1), :]
            dar_ref[...] += gr * hr + gi * hi
            dai_ref[...] += gi * hr - gr * hi
            return dhr + ar * gr + ai * gi, dhi - ai * gr + ar * gi

        zero = jnp.zeros((1, S5_STATES), F32)
        lax.fori_loop(0, nk, step, (zero, zero))

    vm = pl.BlockSpec(memory_space=pltpu.VMEM)
    return pl.pallas_call(
        body, name="s5_scan_bwd", in_specs=[vm] * 6, out_specs=[vm] * 4,
        out_shape=[_sds((nk, S5_STATES)), _sds((nk, S5_STATES)), _sds((1, S5_STATES)), _sds((1, S5_STATES))],
        input_output_aliases={0: 0, 1: 1}, compiler_params=_params(),
    )(dh_re, dh_im, h_re, h_im, a_re, a_im)


def s5_input_grads(dy, ds_re, ds_im, kd, ws_re, ws_im):
    t = dy.shape[0]
    nk = t // S5_CHUNK

    def body(dy_ref, dsr_ref, dsi_ref, kd_ref, wsr_ref, wsi_ref, du_ref, win_ref):
        _fill_toeplitz(win_ref, kd_ref)
        dyc = _chunk_rows(dy_ref, nk)
        du = jnp.concatenate([_mm_nt(dyc[:, lo:], win_ref[lo:hi, lo:]) for lo, hi in _TOEPLITZ_BLOCKS], axis=1)
        du = du + _mm_nt(dsr_ref[...], _spread_groups(wsr_ref[...])) + _mm_nt(dsi_ref[...], _spread_groups(wsi_ref[...]))
        _store_chunk_rows(du_ref, du, nk)

    return pl.pallas_call(
        body, name="s5_input_grads", grid=(S5_OCTETS,),
        in_specs=[_strip(t), _oct_states(nk), _oct_states(nk), OCT_KD, OCT_W, OCT_W],
        out_specs=_strip(t), out_shape=_sds((t, BRANCH)),
        scratch_shapes=[pltpu.VMEM((S5_OCT_IN, S5_OCT_IN), MXU_DTYPE)],
        compiler_params=_params(("parallel",)),
    )(dy, ds_re, ds_im, kd, ws_re, ws_im)


def s5_weight_grads(u, dy, h_re, h_im, ds_re, ds_im):
    t = u.shape[0]
    nk = t // S5_CHUNK

    def body(u_ref, dy_ref, hre_ref, him_ref, dsr_ref, dsi_ref, dkd_ref, dwsr_ref, dwsi_ref, dwor_ref, dwoi_ref):
        dyc = _chunk_rows(dy_ref, nk, F32)
        uct = _chunk_rows(u_ref, nk, F32).T.astype(MXU_DTYPE)
        dyct = dyc.T.astype(MXU_DTYPE)
        dyc = dyc.astype(MXU_DTYPE)
        dwsr_ref[...] = _fold_groups(_mm(uct, dsr_ref[...]))
        dwsi_ref[...] = _fold_groups(_mm(uct, dsi_ref[...]))
        dwor_ref[...] = _fold_groups(_mm(dyct, hre_ref[...]))
        dwoi_ref[...] = _fold_groups(_mm(dyct, him_ref[...]))
        dkd_ref[...] = jnp.zeros_like(dkd_ref)
        for tt in range(0, S5_CHUNK, 2):
            p = _mm(uct[:(tt + 2) * LANES], dyc[:, tt * LANES:(tt + 2) * LANES])
            for s in range(tt + 2):
                rows = p[s * LANES:(s + 1) * LANES]
                if s <= tt:
                    dkd_ref[tt - s] += rows[:, :LANES]
                dkd_ref[tt + 1 - s] += rows[:, LANES:]

    return pl.pallas_call(
        body, name="s5_weight_grads", grid=(S5_OCTETS,),
        in_specs=[_strip(t), _strip(t)] + [_oct_states(nk)] * 4,
        out_specs=[OCT_KD, OCT_W, OCT_W, OCT_W, OCT_W],
        out_shape=[_sds((S5_OCTETS, S5_CHUNK, LANES, LANES))] + [_sds((S5_OCTETS, S5_OCT_IN, LANES))] * 4,
        compiler_params=_params(("parallel",)),
    )(u, dy, h_re, h_im, ds_re, ds_im)


def ssm_mix_fwd(x, u, gate, y_scan, d, w_glu, b_glu, w_out):
    t = x.shape[0]
    tm = min(ROW_TILE_FWD, t)

    def body(x_ref, u_ref, gate_ref, ys_ref, d_ref, wg_ref, bg_ref, wo_ref, y_ref, g2_ref, xo_ref):
        y = ys_ref[...] + d_ref[...] * u_ref[...]
        z0 = _gelu(y)
        g2 = _mm(z0, wg_ref[...]) + bg_ref[...]
        a = z0 * _sigmoid(g2) * _silu(gate_ref[...])
        y_ref[...] = y
        g2_ref[...] = g2
        xo_ref[...] = x_ref[...] + _mm(a, wo_ref[...])

    row = _rows(tm, BRANCH)
    vec = _whole((1, BRANCH))
    mat = _whole((BRANCH, BRANCH))
    return pl.pallas_call(
        body, name="ssm_mix_fwd", grid=(t // tm,),
        in_specs=[row, row, row, row, vec, mat, vec, mat],
        out_specs=[row, row, row],
        out_shape=[_sds((t, BRANCH))] * 3,
        compiler_params=_params(("parallel",)),
    )(x, u, gate, y_scan, d, w_glu, b_glu, w_out)


def ssm_mix_bwd(dxo, u, gate, y, g2, w_glu, w_out):
    t = dxo.shape[0]
    tm = min(ROW_TILE_BWD, t)

    def body(dxo_ref, u_ref, gate_ref, y_ref, g2_ref, wgt_ref, wot_ref, dy_ref, dgate_ref, dwo_ref, dwg_ref, dbg_ref, dd_ref):
        @pl.when(pl.program_id(0) == 0)
        def _():
            dwo_ref[...] = jnp.zeros_like(dwo_ref)
            dwg_ref[...] = jnp.zeros_like(dwg_ref)
            dbg_ref[...] = jnp.zeros_like(dbg_ref)
            dd_ref[...] = jnp.zeros_like(dd_ref)

        dxo = dxo_ref[...]
        gate = gate_ref[...]
        y = y_ref[...]
        z0 = _gelu(y)
        sg = _sigmoid(g2_ref[...])
        z = z0 * sg
        sgate = _silu(gate)
        da = _mm_nt(dxo, wot_ref[...])
        dwo_ref[...] += _mm_tn(z * sgate, dxo)
        dz = da * sgate
        dgate_ref[...] = da * z * _silu_grad(gate)
        dg2 = dz * z0 * sg * (1.0 - sg)
        dbg_ref[...] += jnp.sum(dg2, axis=0, keepdims=True)
        dwg_ref[...] += _mm_tn(z0, dg2)
        dz0 = dz * sg + _mm_nt(dg2, wgt_ref[...])
        dy = dz0 * _gelu_grad(y)
        dd_ref[...] += jnp.sum(dy * u_ref[...], axis=0, keepdims=True)
        dy_ref[...] = dy

    row = _rows(tm, BRANCH)
    vec = _whole((1, BRANCH))
    mat = _whole((BRANCH, BRANCH))
    return pl.pallas_call(
        body, name="ssm_mix_bwd", grid=(t // tm,),
        in_specs=[row, row, row, row, row, mat, mat],
        out_specs=[row, row, mat, mat, vec, vec],
        out_shape=[_sds((t, BRANCH)), _sds((t, BRANCH)), _sds((BRANCH, D_MODEL)), _sds((BRANCH, BRANCH)),
                   _sds((1, BRANCH)), _sds((1, BRANCH))],
        compiler_params=_params(("arbitrary",)),
    )(dxo, u, gate, y, g2, w_glu, w_out)


def ssm_proj_bwd(x, norm, dxo, dy, du_scan, dgate, d, w_in):
    t = x.shape[0]
    tm = min(ROW_TILE_BWD, t)
    n = 2 * BRANCH

    def body(x_ref, g_ref, dxo_ref, dy_ref, dus_ref, dgate_ref, d_ref, wt_ref, dx_ref, dw_ref, dg_ref):
        @pl.when(pl.program_id(0) == 0)
        def _():
            dw_ref[...] = jnp.zeros_like(dw_ref)
            dg_ref[...] = jnp.zeros_like(dg_ref)

        g = g_ref[...]
        r, xhat, h = _rms(x_ref[...], g)
        du = dus_ref[...] + d_ref[...] * dy_ref[...]
        dproj = jnp.concatenate([du, dgate_ref[...]], axis=1)
        dh = _mm_nt(dproj, wt_ref[...])
        dw_ref[...] += _mm_tn(h, dproj)
        dx, dg = _rms_bwd(dh, g, r, xhat)
        dg_ref[...] += dg
        dx_ref[...] = dxo_ref[...] + dx

    row = _rows(tm, D_MODEL)
    vec = _whole((1, D_MODEL))
    return pl.pallas_call(
        body, name="ssm_proj_bwd", grid=(t // tm,),
        in_specs=[row, vec, row, row, row, row, vec, _whole((D_MODEL, n))],
        out_specs=[row, _whole((D_MODEL, n)), vec],
        out_shape=[_sds((t, D_MODEL)), _sds((D_MODEL, n)), _sds((1, D_MODEL))],
        compiler_params=_params(("arbitrary",)),
    )(x, norm, dxo, dy, du_scan, dgate, d, w_in)


ATTN_N = Q_DIM + 2 * KV_DIM + BRANCH


def attn_proj_fwd(x, norm, w_in, cos2, sin2):
    t = x.shape[0]
    tm = min(ROW_TILE_FWD, t)

    def body(x_ref, g_ref, w_ref, cos_ref, sin_ref, q_ref, k_ref, v_ref, gate_ref):
        _, _, h = _rms(x_ref[...], g_ref[...])
        p = _mm(h, w_ref[...])
        cs = cos_ref[...]
        sn = sin_ref[...]
        q = p[:, :Q_DIM]
        k = p[:, Q_DIM:Q_DIM + KV_DIM]
        q_ref[...] = q * _tile_lanes(cs, Q_DIM // LANES) + _swap_half_heads(q) * _tile_lanes(sn, Q_DIM // LANES)
        k_ref[...] = k * cs + _swap_half_heads(k) * sn
        v_ref[...] = p[:, Q_DIM + KV_DIM:Q_DIM + 2 * KV_DIM]
        gate_ref[...] = p[:, Q_DIM + 2 * KV_DIM:]

    return pl.pallas_call(
        body, name="attn_proj_fwd", grid=(t // tm,),
        in_specs=[_rows(tm, D_MODEL), _whole((1, D_MODEL)), _whole((D_MODEL, ATTN_N)), _rows(tm, LANES), _rows(tm, LANES)],
        out_specs=[_rows(tm, Q_DIM), _rows(tm, KV_DIM), _rows(tm, KV_DIM), _rows(tm, BRANCH)],
        out_shape=[_sds((t, Q_DIM)), _sds((t, KV_DIM)), _sds((t, KV_DIM)), _sds((t, BRANCH))],
        compiler_params=_params(("parallel",)),
    )(x, norm, w_in, cos2, sin2)


GQA_LANES = GQA_GROUP * ATTN_BLOCK


def _window_masks(first_block):
    kj = lax.broadcasted_iota(jnp.int32, (ATTN_BLOCK, GQA_LANES), 0)
    qi = lax.broadcasted_iota(jnp.int32, (ATTN_BLOCK, GQA_LANES), 1) % ATTN_BLOCK
    return kj > qi, kj > jnp.where(first_block, qi, ATTN_BLOCK)


def _fold(upper, both):
    return jnp.where(upper, both[:ATTN_BLOCK], both[ATTN_BLOCK:])


def _unfold(upper, tile):
    return jnp.concatenate([jnp.where(upper, tile, 0.0), jnp.where(upper, 0.0, tile)], axis=0).astype(MXU_DTYPE)


def _stack_heads(ref, group):
    return jnp.concatenate([ref[:, h * HEAD_DIM:(h + 1) * HEAD_DIM] for h in range(group * GQA_GROUP, (group + 1) * GQA_GROUP)], axis=0)


def _unstack_heads(ref, group, stacked):
    for n in range(GQA_GROUP):
        h = group * GQA_GROUP + n
        ref[:, h * HEAD_DIM:(h + 1) * HEAD_DIM] = stacked[n * ATTN_BLOCK:(n + 1) * ATTN_BLOCK]


def _sink_row(sink_ref, group):
    return jnp.concatenate([jnp.full((1, ATTN_BLOCK), sink_ref[group * GQA_GROUP + n], F32) for n in range(GQA_GROUP)], axis=1)


def _lane_is(h):
    return lax.broadcasted_iota(jnp.int32, (1, LANES), 1) == h


def attn_fwd(q, k, v, sinks):
    t = q.shape[0]
    nb = t // ATTN_BLOCK
    scale = HEAD_DIM ** -0.5

    def body(sink_ref, q_ref, kc_ref, kp_ref, vc_ref, vp_ref, o_ref, lse_ref):
        keys = jnp.concatenate([kp_ref[...], kc_ref[...]], axis=0).astype(MXU_DTYPE)
        vals = jnp.concatenate([vp_ref[...], vc_ref[...]], axis=0).astype(MXU_DTYPE)
        upper, dead = _window_masks(pl.program_id(0) == 0)
        for g in range(N_KV_HEADS):
            kv = slice(g * HEAD_DIM, (g + 1) * HEAD_DIM)
            qs = _stack_heads(q_ref, g) * scale
            s = jnp.where(dead, NEG_INF, _fold(upper, _mm_nt(keys[:, kv], qs)))
            sink = _sink_row(sink_ref, g)
            m = jnp.maximum(jnp.max(s, axis=0, keepdims=True), sink)
            p = jnp.exp(s - m)
            den = jnp.sum(p, axis=0, keepdims=True) + jnp.exp(sink - m)
            _unstack_heads(o_ref, g, _mm_tn(_unfold(upper, p * (1.0 / den)), vals[:, kv]))
            lse = m + jnp.log(den)
            for n in range(GQA_GROUP):
                lse_ref[pl.ds(g * GQA_GROUP + n, 1), :] = lse[:, n * ATTN_BLOCK:(n + 1) * ATTN_BLOCK]

    cur = lambda n: pl.BlockSpec((ATTN_BLOCK, n), lambda i: (i, 0))
    prev = lambda n: pl.BlockSpec((ATTN_BLOCK, n), lambda i: (jnp.maximum(i - 1, 0), 0))
    return pl.pallas_call(
        body, name="attn_fwd", grid=(nb,),
        in_specs=[pl.BlockSpec(memory_space=pltpu.SMEM), cur(Q_DIM), cur(KV_DIM), prev(KV_DIM), cur(KV_DIM), prev(KV_DIM)],
        out_specs=[cur(Q_DIM), pl.BlockSpec((N_Q_HEADS, ATTN_BLOCK), lambda i: (0, i))],
        out_shape=[_sds((t, Q_DIM)), _sds((N_Q_HEADS, t))],
        compiler_params=_params(("parallel",)),
    )(sinks, q, k, k, v, v)


def attn_bwd(q, k, v, sinks, o, lse, do):
    t = q.shape[0]
    nb = t // ATTN_BLOCK
    scale = HEAD_DIM ** -0.5

    def body(sink_ref, q_ref, kc_ref, kp_ref, vc_ref, vp_ref, o_ref, lse_ref, do_ref,
             dq_ref, dk_ref, dv_ref, dsink_ref, dk_carry, dv_carry):
        i = pl.program_id(0)

        @pl.when(i == 0)
        def _():
            dsink_ref[...] = jnp.zeros_like(dsink_ref)
            dk_carry[...] = jnp.zeros_like(dk_carry)
            dv_carry[...] = jnp.zeros_like(dv_carry)

        @pl.when(i < nb)
        def _():
            keys = jnp.concatenate([kp_ref[...], kc_ref[...]], axis=0).astype(MXU_DTYPE)
            vals = jnp.concatenate([vp_ref[...], vc_ref[...]], axis=0).astype(MXU_DTYPE)
            upper, dead = _window_masks(i == 0)
            dsink = jnp.zeros((1, LANES), F32)
            dk_heads = []
            dv_heads = []
            for g in range(N_KV_HEADS):
                kv = slice(g * HEAD_DIM, (g + 1) * HEAD_DIM)
                qs = (_stack_heads(q_ref, g) * scale).astype(MXU_DTYPE)
                dos = _stack_heads(do_ref, g)
                lse = jnp.concatenate([lse_ref[pl.ds(g * GQA_GROUP + n, 1), :] for n in range(GQA_GROUP)], axis=1)
                s = jnp.where(dead, NEG_INF, _fold(upper, _mm_nt(keys[:, kv], qs)))
                p = jnp.exp(s - lse)
                delta = _mm_f32(jnp.ones((8, HEAD_DIM), F32), dos * _stack_heads(o_ref, g), ((1,), (1,)))[:1]
                dos = dos.astype(MXU_DTYPE)
                ds = _unfold(upper, p * (_fold(upper, _mm_nt(vals[:, kv], dos)) - delta))
                _unstack_heads(dq_ref, g, _mm_tn(ds, keys[:, kv]) * scale)
                dk_heads.append(_mm(ds, qs))
                dv_heads.append(_mm(_unfold(upper, p), dos))
                at_sink = jnp.exp(_sink_row(sink_ref, g) - lse) * delta
                for n in range(GQA_GROUP):
                    dsink = dsink + jnp.where(_lane_is(g * GQA_GROUP + n), -jnp.sum(at_sink[:, n * ATTN_BLOCK:(n + 1) * ATTN_BLOCK]), 0.0)
            dkk = jnp.concatenate(dk_heads, axis=1)
            dvv = jnp.concatenate(dv_heads, axis=1)
            dsink_ref[...] += dsink
            dk_ref[...] = dk_carry[...] + dkk[:ATTN_BLOCK]
            dv_ref[...] = dv_carry[...] + dvv[:ATTN_BLOCK]
            dk_carry[...] = dkk[ATTN_BLOCK:]
            dv_carry[...] = dvv[ATTN_BLOCK:]

        @pl.when(i == nb)
        def _():
            dk_ref[...] = dk_carry[...]
            dv_ref[...] = dv_carry[...]

    last = nb - 1
    cur = lambda n: pl.BlockSpec((ATTN_BLOCK, n), lambda i: (jnp.minimum(i, last), 0))
    prev = lambda n: pl.BlockSpec((ATTN_BLOCK, n), lambda i: (jnp.clip(i - 1, 0, last), 0))
    late = lambda n: pl.BlockSpec((ATTN_BLOCK, n), lambda i: (i, 0))
    dq, dk_late, dv_late, dsinks = pl.pallas_call(
        body, name="attn_bwd", grid=(nb + 1,),
        in_specs=[pl.BlockSpec(memory_space=pltpu.SMEM), cur(Q_DIM), cur(KV_DIM), prev(KV_DIM), cur(KV_DIM), prev(KV_DIM),
                  cur(Q_DIM), pl.BlockSpec((N_Q_HEADS, ATTN_BLOCK), lambda i: (0, jnp.minimum(i, last))), cur(Q_DIM)],
        out_specs=[cur(Q_DIM), late(KV_DIM), late(KV_DIM), _whole((1, LANES))],
        out_shape=[_sds((t, Q_DIM)), _sds((t + ATTN_BLOCK, KV_DIM)), _sds((t + ATTN_BLOCK, KV_DIM)), _sds((1, LANES))],
        scratch_shapes=[pltpu.VMEM((ATTN_BLOCK, KV_DIM), F32), pltpu.VMEM((ATTN_BLOCK, KV_DIM), F32)],
        compiler_params=_params(("arbitrary",)),
    )(sinks, q, k, k, v, v, o, lse, do)
    return dq, dk_late[ATTN_BLOCK:], dv_late[ATTN_BLOCK:], dsinks


def attn_out_fwd(x, o, gate, w_out):
    t = x.shape[0]
    tm = min(ROW_TILE_FWD, t)

    def body(x_ref, o_ref, gate_ref, w_ref, xo_ref):
        xo_ref[...] = x_ref[...] + _mm(o_ref[...] * _silu(gate_ref[...]), w_ref[...])

    row = _rows(tm, D_MODEL)
    return pl.pallas_call(
        body, name="attn_out_fwd", grid=(t // tm,),
        in_specs=[row, row, row, _whole((Q_DIM, D_MODEL))], out_specs=row, out_shape=_sds((t, D_MODEL)),
        compiler_params=_params(("parallel",)),
    )(x, o, gate, w_out)


def attn_out_bwd(dxo, o, gate, w_out):
    t = dxo.shape[0]
    tm = min(ROW_TILE_BWD, t)

    def body(dxo_ref, o_ref, gate_ref, wt_ref, do_ref, dgate_ref, dw_ref):
        @pl.when(pl.program_id(0) == 0)
        def _():
            dw_ref[...] = jnp.zeros_like(dw_ref)

        dxo = dxo_ref[...]
        o = o_ref[...]
        gate = gate_ref[...]
        sgate = _silu(gate)
        da = _mm_nt(dxo, wt_ref[...])
        dw_ref[...] += _mm_tn(o * sgate, dxo)
        do_ref[...] = da * sgate
        dgate_ref[...] = da * o * _silu_grad(gate)

    row = _rows(tm, D_MODEL)
    mat = _whole((Q_DIM, D_MODEL))
    return pl.pallas_call(
        body, name="attn_out_bwd", grid=(t // tm,),
        in_specs=[row, row, row, mat], out_specs=[row, row, mat],
        out_shape=[_sds((t, Q_DIM)), _sds((t, BRANCH)), _sds((Q_DIM, D_MODEL))],
        compiler_params=_params(("arbitrary",)),
    )(dxo, o, gate, w_out)


def attn_proj_bwd(x, norm, dxo, dq, dk, dv, dgate, cos2, sin2, w_in):
    t = x.shape[0]
    tm = min(ROW_TILE_BWD, t)

    def body(x_ref, g_ref, dxo_ref, dq_ref, dk_ref, dv_ref, dgate_ref, cos_ref, sin_ref, wt_ref, dx_ref, dw_ref, dg_ref):
        @pl.when(pl.program_id(0) == 0)
        def _():
            dw_ref[...] = jnp.zeros_like(dw_ref)
            dg_ref[...] = jnp.zeros_like(dg_ref)

        g = g_ref[...]
        r, xhat, h = _rms(x_ref[...], g)
        cs = cos_ref[...]
        sn = sin_ref[...]
        dqr = dq_ref[...]
        dkr = dk_ref[...]
        dq = dqr * _tile_lanes(cs, Q_DIM // LANES) + _swap_half_heads(dqr * _tile_lanes(sn, Q_DIM // LANES))
        dk = dkr * cs + _swap_half_heads(dkr * sn)
        dproj = jnp.concatenate([dq, dk, dv_ref[...], dgate_ref[...]], axis=1)
        dh = _mm_nt(dproj, wt_ref[...])
        dw_ref[...] += _mm_tn(h, dproj)
        dx, dg = _rms_bwd(dh, g, r, xhat)
        dg_ref[...] += dg
        dx_ref[...] = dxo_ref[...] + dx

    row = _rows(tm, D_MODEL)
    vec = _whole((1, D_MODEL))
    return pl.pallas_call(
        body, name="attn_proj_bwd", grid=(t // tm,),
        in_specs=[row, vec, row, _rows(tm, Q_DIM), _rows(tm, KV_DIM), _rows(tm, KV_DIM), _rows(tm, BRANCH),
                  _rows(tm, LANES), _rows(tm, LANES), _whole((D_MODEL, ATTN_N))],
        out_specs=[row, _whole((D_MODEL, ATTN_N)), vec],
        out_shape=[_sds((t, D_MODEL)), _sds((D_MODEL, ATTN_N)), _sds((1, D_MODEL))],
        compiler_params=_params(("arbitrary",)),
    )(x, norm, dxo, dq, dk, dv, dgate, cos2, sin2, w_in)


def loss_head(x, norm, target):
    t = x.shape[0]
    tm = min(ROW_TILE_FWD, t)

    def body(x_ref, g_ref, tgt_ref, loss_ref, dx_ref, dg_ref):
        @pl.when(pl.program_id(0) == 0)
        def _():
            loss_ref[...] = jnp.zeros_like(loss_ref)
            dg_ref[...] = jnp.zeros_like(dg_ref)

        g = g_ref[...]
        r, xhat, y = _rms(x_ref[...], g)
        err = y - tgt_ref[...]
        loss_ref[...] += 0.5 * jnp.sum(jnp.mean(err * err, axis=-1, keepdims=True), axis=0, keepdims=True)
        dx, dg = _rms_bwd(err * (1.0 / D_MODEL), g, r, xhat)
        dg_ref[...] += dg
        dx_ref[...] = dx

    row = _rows(tm, D_MODEL)
    vec = _whole((1, D_MODEL))
    return pl.pallas_call(
        body, name="loss_head", grid=(t // tm,),
        in_specs=[row, vec, row], out_specs=[_whole((1, 1)), row, vec],
        out_shape=[_sds((1, 1)), _sds((t, D_MODEL)), _sds((1, D_MODEL))],
        compiler_params=_params(("arbitrary",)),
    )(x, norm, target)


OCT_TILE = pl.BlockSpec((None, LANES, LANES), lambda b: (b, 0, 0))
N_LAGS = S5_CHUNK + 1


def _cmul(ar, ai, br, bi):
    return ar * br - ai * bi, ar * bi + ai * br


def _cmul_conj(ar, ai, br, bi):
    return ar * br + ai * bi, ar * bi - ai * br


def _mm_f32(a, b, dims):
    return lax.dot_general(a, b, (dims, ((), ())), precision=lax.Precision.HIGHEST, preferred_element_type=F32)


def _s5_discretise(ar, ai, ls, br, bi):
    dt = jnp.exp(ls)
    xr = ar * dt
    xi = ai * dt
    mag = jnp.exp(xr)
    first = (mag * jnp.cos(xi), mag * jnp.sin(xi))
    powers = [(jnp.ones_like(xr), jnp.zeros_like(xr)), first]
    for _ in range(2, N_LAGS):
        powers.append(_cmul(*powers[-1], *first))
    den = ar * ar + ai * ai
    nr = powers[1][0] - 1.0
    ni = powers[1][1]
    fr = (nr * ar + ni * ai) / den
    fi = (ni * ar - nr * ai) / den
    bbr, bbi = _cmul(fr, fi, br, bi)
    return dt, powers, (fr, fi), (bbr, bbi), den


def _same_group_tile():
    row = lax.broadcasted_iota(jnp.int32, (LANES, LANES), 0)
    col = lax.broadcasted_iota(jnp.int32, (LANES, LANES), 1)
    return (row // SSM_GROUP) == (col // SSM_GROUP)


def _first_copy_lanes():
    return lax.broadcasted_iota(jnp.int32, (LANES, LANES), 1) < SSM_STATE


def s5_param_fwd(tiles):
    def body(ar_ref, ai_ref, ls_ref, br_ref, bi_ref, cr_ref, ci_ref, kd_ref, wsr_ref, wsi_ref, wor_ref, woi_ref, pr_ref, pi_ref):
        cr = cr_ref[...]
        ci = ci_ref[...]
        _, powers, _, (bbr, bbi), _ = _s5_discretise(ar_ref[...], ai_ref[...], ls_ref[...], br_ref[...], bi_ref[...])
        once = _first_copy_lanes()
        crm = jnp.where(once, cr, 0.0)
        cim = jnp.where(once, ci, 0.0)
        same = _same_group_tile()
        for lag in range(S5_CHUNK):
            er, ei = powers[lag]
            xr, xi = _cmul(er, ei, bbr, bbi)
            rows = pl.ds((S5_CHUNK - 1 - lag) * LANES, LANES)
            wsr_ref[rows, :] = xr
            wsi_ref[rows, :] = xi
            k = _mm_f32(xr, crm, ((1,), (1,))) - _mm_f32(xi, cim, ((1,), (1,)))
            kd_ref[lag] = jnp.where(same, k, 0.0)
        for t in range(S5_CHUNK):
            er, ei = powers[t + 1]
            zr, zi = _cmul(er, ei, cr, ci)
            wor_ref[pl.ds(t * LANES, LANES), :] = zr
            woi_ref[pl.ds(t * LANES, LANES), :] = -zi
        pr_ref[...] = powers[S5_CHUNK][0]
        pi_ref[...] = powers[S5_CHUNK][1]

    return pl.pallas_call(
        body, name="s5_param_fwd", grid=(S5_OCTETS,),
        in_specs=[OCT_TILE] * 7, out_specs=[OCT_KD, OCT_W, OCT_W, OCT_W, OCT_W, OCT_TILE, OCT_TILE],
        out_shape=[_sds((S5_OCTETS, S5_CHUNK, LANES, LANES))] + [_sds((S5_OCTETS, S5_OCT_IN, LANES))] * 4
                  + [_sds((S5_OCTETS, LANES, LANES))] * 2,
        compiler_params=_params(("parallel",)),
    )(*tiles)


def s5_param_bwd(tiles, dkd, dws_re, dws_im, dwo_re, dwo_im, dp_re, dp_im):
    def body(ar_ref, ai_ref, ls_ref, br_ref, bi_ref, cr_ref, ci_ref, dkd_ref, dwsr_ref, dwsi_ref, dwor_ref, dwoi_ref, dpr_ref, dpi_ref,
             dar_ref, dai_ref, dls_ref, dbr_ref, dbi_ref, dcr_ref, dci_ref):
        ar = ar_ref[...]
        ai = ai_ref[...]
        br = br_ref[...]
        bi = bi_ref[...]
        cr = cr_ref[...]
        ci = ci_ref[...]
        dt, powers, (fr, fi), (bbr, bbi), den = _s5_discretise(ar, ai, ls_ref[...], br, bi)
        once = _first_copy_lanes()
        crm = jnp.where(once, cr, 0.0)
        cim = jnp.where(once, ci, 0.0)
        same = _same_group_tile()
        zero = jnp.zeros((LANES, LANES), F32)
        dpow = [[zero, zero] for _ in range(N_LAGS)]
        dbbr, dbbi, dcr, dci = zero, zero, zero, zero
        for lag in range(S5_CHUNK):
            er, ei = powers[lag]
            xr, xi = _cmul(er, ei, bbr, bbi)
            rows = pl.ds((S5_CHUNK - 1 - lag) * LANES, LANES)
            g = jnp.where(same, dkd_ref[lag], 0.0)
            dxr = dwsr_ref[rows, :] + _mm_f32(g, crm, ((1,), (0,)))
            dxi = dwsi_ref[rows, :] - _mm_f32(g, cim, ((1,), (0,)))
            dcr = dcr + jnp.where(once, _mm_f32(g, xr, ((0,), (0,))), 0.0)
            dci = dci - jnp.where(once, _mm_f32(g, xi, ((0,), (0,))), 0.0)
            a, b = _cmul_conj(bbr, bbi, dxr, dxi)
            dpow[lag][0] = dpow[lag][0] + a
            dpow[lag][1] = dpow[lag][1] + b
            a, b = _cmul_conj(er, ei, dxr, dxi)
            dbbr = dbbr + a
            dbbi = dbbi + b
        for t in range(S5_CHUNK):
            er, ei = powers[t + 1]
            dzr = dwor_ref[pl.ds(t * LANES, LANES), :]
            dzi = -dwoi_ref[pl.ds(t * LANES, LANES), :]
            a, b = _cmul_conj(cr, ci, dzr, dzi)
            dpow[t + 1][0] = dpow[t + 1][0] + a
            dpow[t + 1][1] = dpow[t + 1][1] + b
            a, b = _cmul_conj(er, ei, dzr, dzi)
            dcr = dcr + a
            dci = dci + b
        dpow[S5_CHUNK][0] = dpow[S5_CHUNK][0] + dpr_ref[...]
        dpow[S5_CHUNK][1] = dpow[S5_CHUNK][1] + dpi_ref[...]
        dfr, dfi = _cmul_conj(br, bi, dbbr, dbbi)
        dbr, dbi = _cmul_conj(fr, fi, dbbr, dbbi)
        dnr, dni = _cmul(ar / den, ai / den, dfr, dfi)
        qr = (fr * ar + fi * ai) / den
        qi = (fi * ar - fr * ai) / den
        dlr, dli = _cmul(-qr, qi, dfr, dfi)
        dpow[1][0] = dpow[1][0] + dnr
        dpow[1][1] = dpow[1][1] + dni
        dxr, dxi = zero, zero
        for lag in range(1, N_LAGS):
            a, b = _cmul_conj(powers[lag][0], powers[lag][1], dpow[lag][0], dpow[lag][1])
            dxr = dxr + lag * a
            dxi = dxi + lag * b
        dar_ref[...] = dlr + dt * dxr
        dai_ref[...] = dli + dt * dxi
        dls_ref[...] = dt * (ar * dxr + ai * dxi)
        dbr_ref[...] = dbr
        dbi_ref[...] = dbi
        dcr_ref[...] = dcr
        dci_ref[...] = dci

    return pl.pallas_call(
        body, name="s5_param_bwd", grid=(S5_OCTETS,),
        in_specs=[OCT_TILE] * 7 + [OCT_KD, OCT_W, OCT_W, OCT_W, OCT_W, OCT_TILE, OCT_TILE], out_specs=[OCT_TILE] * 7,
        out_shape=[_sds((S5_OCTETS, LANES, LANES))] * 7,
        compiler_params=_params(("parallel",)),
    )(*tiles, dkd, dws_re, dws_im, dwo_re, dwo_im, dp_re, dp_im)


def _doubled(v):
    return jnp.concatenate([v, v], axis=-1)


def _s5_param_tiles(a_re, a_im, log_step, b_re, b_im, c_re, c_im):
    def per_group(a):
        return _doubled(jnp.broadcast_to(a.reshape(S5_OCTETS, S5_OCT, 1, SSM_STATE),
                                         (S5_OCTETS, S5_OCT, SSM_GROUP, SSM_STATE)).reshape(S5_OCTETS, LANES, SSM_STATE))

    ls = jnp.broadcast_to(log_step.reshape(S5_OCTETS, S5_OCT, 1, 1), (S5_OCTETS, S5_OCT, SSM_GROUP, LANES)).reshape(S5_OCTETS, LANES, LANES)
    bt = lambda b: _doubled(b.transpose(0, 2, 1).reshape(S5_OCTETS, LANES, SSM_STATE))
    ct = lambda c: _doubled(c.reshape(S5_OCTETS, LANES, SSM_STATE))
    return [per_group(a_re), per_group(a_im), ls, bt(b_re), bt(b_im), ct(c_re), ct(c_im)]


def _s5_param_grads(dtiles):
    dar, dai, dls, dbr, dbi, dcr, dci = dtiles
    halves = lambda d: d[..., :SSM_STATE] + d[..., SSM_STATE:]
    per_group = lambda d: halves(d).reshape(SSM_GROUPS, SSM_GROUP, SSM_STATE).sum(axis=1)
    per_row = lambda d: halves(d).reshape(SSM_GROUPS, SSM_GROUP, SSM_STATE)
    return (per_group(dar), per_group(dai), dls.reshape(SSM_GROUPS, SSM_GROUP * LANES).sum(axis=1),
            per_row(dbr).transpose(0, 2, 1), per_row(dbi).transpose(0, 2, 1), per_row(dcr), per_row(dci))


def _group_power_rows(tile):
    return tile[:, ::SSM_GROUP, :SSM_STATE].reshape(1, S5_STATES)


def _group_power_tiles(row):
    t = jnp.pad(row.reshape(S5_OCTETS, S5_OCT, 1, SSM_STATE), ((0, 0), (0, 0), (0, SSM_GROUP - 1), (0, LANES - SSM_STATE)))
    return t.reshape(S5_OCTETS, LANES, LANES)


def _rope_tables(t):
    pos = jnp.arange(t, dtype=F32)
    inv_freq = ROPE_THETA ** (-jnp.arange(0, HEAD_DIM, 2, dtype=F32) / HEAD_DIM)
    ang = pos[:, None] * inv_freq[None, :]
    cos = jnp.cos(ang)
    sin = jnp.sin(ang)
    cos64 = jnp.concatenate([cos, cos], axis=1)
    sin64 = jnp.concatenate([-sin, sin], axis=1)
    return jnp.concatenate([cos64, cos64], axis=1), jnp.concatenate([sin64, sin64], axis=1)


def _row(v):
    return v.reshape(1, -1)


def _ssm_forward(x, w):
    tiles = _s5_param_tiles(w["a_re"], w["a_im"], w["log_step"], w["b_re"], w["b_im"], w["c_re"], w["c_im"])
    kd, ws_re, ws_im, wo_re, wo_im, p_re, p_im = s5_param_fwd(tiles)
    mats = dict(kd=kd, ws_re=ws_re, ws_im=ws_im, wo_re=wo_re, wo_im=wo_im, a_re=_group_power_rows(p_re), a_im=_group_power_rows(p_im))
    u, gate = ssm_proj_fwd(x, _row(w["norm"]), w["w_in"])
    s_re, s_im = s5_chunk_states(u, mats["ws_re"], mats["ws_im"])
    h_re, h_im = s5_scan_fwd(s_re, s_im, mats["a_re"], mats["a_im"])
    y_scan = s5_outputs(u, h_re, h_im, mats["kd"], mats["wo_re"], mats["wo_im"])
    y, g2, x_new = ssm_mix_fwd(x, u, gate, y_scan, _row(w["d"]), w["w_glu"], _row(w["b_glu"]), w["w_out"])
    saved = dict(x=x, u=u, gate=gate, y=y, g2=g2, h_re=h_re, h_im=h_im, mats=mats, tiles=tiles)
    return x_new, saved


def _ssm_backward(dxo, w, s):
    dy, dgate, dw_out, dw_glu, db_glu, dd = ssm_mix_bwd(dxo, s["u"], s["gate"], s["y"], s["g2"], w["w_glu"], w["w_out"])
    mats = s["mats"]
    dh_re, dh_im = s5_state_grads(dy, mats["wo_re"], mats["wo_im"])
    ds_re, ds_im, da_re, da_im = s5_scan_bwd(dh_re, dh_im, s["h_re"], s["h_im"], mats["a_re"], mats["a_im"])
    du_scan = s5_input_grads(dy, ds_re, ds_im, mats["kd"], mats["ws_re"], mats["ws_im"])
    dkd, dws_re, dws_im, dwo_re, dwo_im = s5_weight_grads(s["u"], dy, s["h_re"], s["h_im"], ds_re, ds_im)
    dparams = _s5_param_grads(s5_param_bwd(s["tiles"], dkd, dws_re, dws_im, dwo_re, dwo_im,
                                           _group_power_tiles(da_re), _group_power_tiles(da_im)))
    dx, dw_in, dnorm = ssm_proj_bwd(s["x"], _row(w["norm"]), dxo, dy, du_scan, dgate, _row(w["d"]), w["w_in"])
    grads = dict(norm=dnorm, w_in=dw_in, d=dd, w_glu=dw_glu, b_glu=db_glu, w_out=dw_out)
    for name, val in zip(("a_re", "a_im", "log_step", "b_re", "b_im", "c_re", "c_im"), dparams):
        grads[name] = val
    return dx, grads


def _attn_forward(x, w, cos2, sin2):
    q, k, v, gate = attn_proj_fwd(x, _row(w["norm"]), w["w_in"], cos2, sin2)
    o, lse = attn_fwd(q, k, v, w["sinks"])
    x_new = attn_out_fwd(x, o, gate, w["w_out"])
    return x_new, dict(x=x, q=q, k=k, v=v, gate=gate, o=o, lse=lse)


def _attn_backward(dxo, w, s, cos2, sin2):
    do, dgate, dw_out = attn_out_bwd(dxo, s["o"], s["gate"], w["w_out"])
    dq, dk, dv, dsinks = attn_bwd(s["q"], s["k"], s["v"], w["sinks"], s["o"], s["lse"], do)
    dx, dw_in, dnorm = attn_proj_bwd(s["x"], _row(w["norm"]), dxo, dq, dk, dv, dgate, cos2, sin2, w["w_in"])
    return dx, dict(norm=dnorm, w_in=dw_in, sinks=dsinks[0, :N_Q_HEADS], w_out=dw_out)


def _sequence_step(x, target, layers, final_norm):
    cos2, sin2 = _rope_tables(x.shape[0])
    saved = []
    for i, w in enumerate(layers):
        if i % 2 == 0:
            x, s = _ssm_forward(x, w)
        else:
            x, s = _attn_forward(x, w, cos2, sin2)
        saved.append(s)
    loss, dx, dfinal = loss_head(x, _row(final_norm), target)
    grads = {"final_norm": dfinal}
    for i in reversed(range(len(layers))):
        if i % 2 == 0:
            dx, g = _ssm_backward(dx, layers[i], saved[i])
        else:
            dx, g = _attn_backward(dx, layers[i], saved[i], cos2, sin2)
        for name, val in g.items():
            grads["l%d_%s" % (i, name)] = val
    return loss[0, 0], dx, grads


ANY = pl.BlockSpec(memory_space=pl.ANY)


def _place():
    return lax.axis_index("x"), lax.axis_index("y"), lax.axis_index("c")


def _other_chips(x, y):
    return [(1 - x, y), (x, 1 - y), (1 - x, 1 - y)]


class _StagedCopies:
    def __init__(self, bufs, load_sems, store_sems):
        self.bufs, self.load_sems, self.store_sems = bufs, load_sems, store_sems
        self.loads, self.stores = [], []

    def load(self, i, src):
        cp = pltpu.make_async_copy(src, self.bufs[i], self.load_sems.at[i])
        cp.start()
        self.loads.append(cp)

    def store(self, i, dst):
        self.loads[i].wait()
        cp = pltpu.make_async_copy(self.bufs[i], dst, self.store_sems.at[i])
        cp.start()
        self.stores.append(cp)

    def finish(self):
        for cp in self.stores:
            cp.wait()


def _staging(blocks):
    n = len(blocks)
    return [pltpu.VMEM(b.shape, b.dtype) for b in blocks] + [pltpu.SemaphoreType.DMA((n,)), pltpu.SemaphoreType.DMA((n,))]


def gather_weight_shards(shards):
    n = len(shards)

    def body(*refs):
        ins, outs = refs[:n], refs[n:2 * n]
        send_sems, recv_sems, pass_send_sems, pass_recv_sems = refs[2 * n:2 * n + 4]
        own = _StagedCopies(refs[2 * n + 4:3 * n + 4], *refs[3 * n + 4:])
        x, y, c = _place()
        me = 2 * x + y
        chips = _other_chips(x, y)

        def half(i, block, which):
            rows = ins[i].shape[0] // 2
            return outs[i].at[block, pl.ds(which * rows, rows), :]

        def my_half(i):
            rows = ins[i].shape[0] // 2
            return ins[i].at[pl.ds(c * rows, rows), :]

        for i in range(n):
            own.load(i, ins[i])
        sends = []
        for i in range(n):
            for k, (tx, ty) in enumerate(chips):
                cp = pltpu.make_async_remote_copy(src_ref=my_half(i), dst_ref=half(i, me, c), send_sem=send_sems.at[i, k],
                                                  recv_sem=recv_sems.at[i, k], device_id=(tx, ty, c), device_id_type=MESH)
                cp.start()
                sends.append(cp)
        for i in range(n):
            own.store(i, outs[i].at[me])
        for i in range(n):
            for k, (tx, ty) in enumerate(chips):
                landed = half(i, 2 * tx + ty, c)
                pltpu.make_async_remote_copy(src_ref=my_half(i), dst_ref=landed, send_sem=send_sems.at[i, k],
                                             recv_sem=recv_sems.at[i, k], device_id=(tx, ty, c), device_id_type=MESH).wait_recv()
                cp = pltpu.make_async_remote_copy(src_ref=landed, dst_ref=landed, send_sem=pass_send_sems.at[i, k],
                                                  recv_sem=pass_recv_sems.at[i, k], device_id=(x, y, 1 - c), device_id_type=MESH)
                cp.start()
                sends.append(cp)
        for i in range(n):
            for k, (tx, ty) in enumerate(chips):
                missing = half(i, 2 * tx + ty, 1 - c)
                pltpu.make_async_remote_copy(src_ref=missing, dst_ref=missing, send_sem=pass_send_sems.at[i, k],
                                             recv_sem=pass_recv_sems.at[i, k], device_id=(x, y, 1 - c), device_id_type=MESH).wait_recv()
        for cp in sends:
            cp.wait_send()
        own.finish()

    sems = pltpu.SemaphoreType.DMA((n, 3))
    return pl.pallas_call(
        body, name="gather_weight_shards",
        in_specs=[ANY] * n, out_specs=[ANY] * n,
        out_shape=[_sds((4,) + s.shape, s.dtype) for s in shards],
        scratch_shapes=[sems, sems, sems, sems] + _staging(shards),
        compiler_params=_params(),
    )(*shards)


def exchange_halves_with_sibling(grads):
    n = len(grads)

    def body(*refs):
        ins, outs = refs[:n], refs[n:2 * n]
        send_sems, recv_sems = refs[2 * n:]
        x, y, c = _place()
        copies = []
        for i in range(n):
            half = ins[i].shape[1] // 2
            src = ins[i].at[:, pl.ds((1 - c) * half, half), :]
            cp = pltpu.make_async_remote_copy(src_ref=src, dst_ref=outs[i], send_sem=send_sems.at[i], recv_sem=recv_sems.at[i],
                                              device_id=(x, y, 1 - c), device_id_type=MESH)
            cp.start()
            copies.append(cp)
        for cp in copies:
            cp.wait()

    return pl.pallas_call(
        body, name="exchange_halves_with_sibling",
        in_specs=[ANY] * n, out_specs=[ANY] * n,
        out_shape=[_sds((g.shape[0], g.shape[1] // 2, g.shape[2])) for g in grads],
        scratch_shapes=[pltpu.SemaphoreType.DMA((n,)), pltpu.SemaphoreType.DMA((n,))],
    )(*grads)


def scatter_blocks_to_chips(sums):
    n = len(sums)

    def body(*refs):
        ins, outs = refs[:n], refs[n:2 * n]
        send_sems, recv_sems = refs[2 * n:2 * n + 2]
        own = _StagedCopies(refs[2 * n + 2:3 * n + 2], *refs[3 * n + 2:])
        x, y, c = _place()
        me = 2 * x + y

        def block_for(i, chip):
            return ins[i].at[chip] if ins[i].shape[0] == 4 else ins[i].at[0]

        for i in range(n):
            own.load(i, block_for(i, me))
        sends = []
        for i in range(n):
            for k, (tx, ty) in enumerate(_other_chips(x, y)):
                cp = pltpu.make_async_remote_copy(src_ref=block_for(i, 2 * tx + ty), dst_ref=outs[i].at[me], send_sem=send_sems.at[i, k],
                                                  recv_sem=recv_sems.at[i, k], device_id=(tx, ty, c), device_id_type=MESH)
                cp.start()
                sends.append(cp)
        for i in range(n):
            own.store(i, outs[i].at[me])
        for i in range(n):
            for k, (tx, ty) in enumerate(_other_chips(x, y)):
                pltpu.make_async_remote_copy(src_ref=block_for(i, me), dst_ref=outs[i].at[2 * tx + ty], send_sem=send_sems.at[i, k],
                                             recv_sem=recv_sems.at[i, k], device_id=(tx, ty, c), device_id_type=MESH).wait_recv()
        for cp in sends:
            cp.wait_send()
        own.finish()

    return pl.pallas_call(
        body, name="scatter_blocks_to_chips",
        in_specs=[ANY] * n, out_specs=[ANY] * n,
        out_shape=[_sds((4,) + s.shape[1:], s.dtype) for s in sums],
        scratch_shapes=[pltpu.SemaphoreType.DMA((n, 3)), pltpu.SemaphoreType.DMA((n, 3))] + _staging([_sds(s.shape[1:], s.dtype) for s in sums]),
        compiler_params=_params(),
    )(*sums)


def swap_halves_with_sibling(pieces):
    n = len(pieces)

    def body(*refs):
        ins, outs = refs[:n], refs[n:2 * n]
        send_sems, recv_sems = refs[2 * n:2 * n + 2]
        own = _StagedCopies(refs[2 * n + 2:3 * n + 2], *refs[3 * n + 2:])
        x, y, c = _place()
        for i in range(n):
            own.load(i, ins[i])
        swaps = []
        for i in range(n):
            cp = pltpu.make_async_remote_copy(src_ref=ins[i], dst_ref=outs[i].at[c], send_sem=send_sems.at[i], recv_sem=recv_sems.at[i],
                                              device_id=(x, y, 1 - c), device_id_type=MESH)
            cp.start()
            swaps.append(cp)
        for i in range(n):
            own.store(i, outs[i].at[c])
        for i in range(n):
            pltpu.make_async_remote_copy(src_ref=ins[i], dst_ref=outs[i].at[1 - c], send_sem=send_sems.at[i], recv_sem=recv_sems.at[i],
                                         device_id=(x, y, 1 - c), device_id_type=MESH).wait_recv()
        for cp in swaps:
            cp.wait_send()
        own.finish()

    return pl.pallas_call(
        body, name="swap_halves_with_sibling",
        in_specs=[ANY] * n, out_specs=[ANY] * n,
        out_shape=[_sds((2,) + p.shape) for p in pieces],
        scratch_shapes=[pltpu.SemaphoreType.DMA((n,)), pltpu.SemaphoreType.DMA((n,))] + _staging(pieces),
        compiler_params=_params(),
    )(*pieces)


def _row_tile(rows, cols):
    tm = rows
    while tm * cols * 4 > (2 << 20) and tm % 16 == 0:
        tm //= 2
    return tm


def add_pair(a, b, out_dtype):
    nb, rows, cols = a.shape
    tm = _row_tile(rows, cols)

    def body(a_ref, b_ref, o_ref):
        o_ref[...] = (a_ref[...] + b_ref[...]).astype(out_dtype)

    spec = pl.BlockSpec((None, tm, cols), lambda j, i: (j, i, 0))
    return pl.pallas_call(
        body, name="add_pair", grid=(nb, rows // tm), in_specs=[spec, spec], out_specs=spec, out_shape=_sds(a.shape, out_dtype),
        compiler_params=_params(("parallel", "parallel")),
    )(a, b)


def sum_four(a):
    _, rows, cols = a.shape
    tm = _row_tile(rows, cols)

    def body(a_ref, o_ref):
        o_ref[...] = ((a_ref[0].astype(F32) + a_ref[1].astype(F32)) + a_ref[2].astype(F32)) + a_ref[3].astype(F32)

    return pl.pallas_call(
        body, name="sum_four", grid=(rows // tm,),
        in_specs=[pl.BlockSpec((4, tm, cols), lambda i: (0, i, 0))], out_specs=pl.BlockSpec((tm, cols), lambda i: (i, 0)),
        out_shape=_sds((rows, cols)), compiler_params=_params(("parallel",)),
    )(a)


def adamw(w, g, m, v):
    rows, cols = w.shape
    tm = _row_tile(rows, cols)
    c1 = 1.0 - ADAM_B1 ** ADAM_STEP
    c2 = 1.0 - ADAM_B2 ** ADAM_STEP

    def body(w_ref, g_ref, m_ref, v_ref, d_ref, nm_ref, nv_ref):
        g = g_ref[...]
        nm = ADAM_B1 * m_ref[...] + (1.0 - ADAM_B1) * g
        nv = ADAM_B2 * v_ref[...] + (1.0 - ADAM_B2) * (g * g)
        d_ref[...] = -ADAM_LR * ((nm / c1) / (jnp.sqrt(nv / c2) + ADAM_EPS) + ADAM_WD * w_ref[...])
        nm_ref[...] = nm
        nv_ref[...] = nv

    spec = pl.BlockSpec((tm, cols), lambda i: (i, 0))
    return pl.pallas_call(
        body, name="adamw", grid=(rows // tm,), in_specs=[spec] * 4, out_specs=[spec] * 3,
        out_shape=[_sds(w.shape)] * 3, compiler_params=_params(("parallel",)),
    )(w, g, m, v)


PACK_TILE = 8 * LANES
PACK_PIECES = 8
PACK_ALIGN = PACK_PIECES * 16


def _pack_small(values, scalar=None):
    parts = []
    for name in SMALL_NAMES:
        flat = values[name].reshape(-1)
        pad = (-flat.shape[0]) % PACK_TILE
        if pad:
            flat = jnp.concatenate([flat, jnp.zeros((pad,), F32)])
        parts.append(flat.reshape(-1, LANES))
    rows = sum(p.shape[0] for p in parts) + 8
    parts.append(jnp.zeros(((-rows) % PACK_ALIGN, LANES), F32))
    last = jnp.zeros((8, LANES), F32)
    parts.append(last if scalar is None else jnp.broadcast_to(scalar.astype(F32), (8, LANES)))
    return jnp.concatenate(parts, axis=0)


def _unpack_small(pack, like):
    out = {}
    row = 0
    for name in SMALL_NAMES:
        size = math.prod(like[name].shape)
        rows = -(-size // PACK_TILE) * 8
        out[name] = pack[row:row + rows].reshape(-1)[:size].reshape(like[name].shape)
        row += rows
    return out


def _is_column_sharded(name):
    return name.endswith("w_in")


def _to_blocks(name, full):
    if _is_column_sharded(name):
        rows, cols = full.shape
        return full.reshape(rows, 4, cols // 4).transpose(1, 0, 2)
    return full.reshape(4, full.shape[0] // 4, full.shape[1])


def _from_blocks(name, stacked):
    if _is_column_sharded(name):
        return stacked.transpose(1, 0, 2).reshape(stacked.shape[1], 4 * stacked.shape[2])
    return stacked.reshape(4 * stacked.shape[1], stacked.shape[2])


def _train_step(x, loss_target, weights, moments_m, moments_v):
    c = lax.axis_index("c")
    gathered = gather_weight_shards([weights[n].astype(MXU_DTYPE) for n in BIG_NAMES])
    full = {n: _from_blocks(n, g) for n, g in zip(BIG_NAMES, gathered)}
    layers = []
    for i in range(4):
        names = SSM_NAMES if i % 2 == 0 else ATTN_NAMES
        w = {}
        for n in names:
            key = "l%d_%s" % (i, n)
            if key in full:
                w[n] = full[key]
            else:
                w[n] = weights[key]
        layers.append(w)
    loss, dx, grads = _sequence_step(x[0], loss_target[0], layers, weights["final_norm"])
    small_pack = _pack_small({n: grads[n] for n in SMALL_NAMES}, scalar=loss)
    blocks = [_to_blocks(n, grads[n]) for n in BIG_NAMES] + [small_pack[None]]
    from_sibling = exchange_halves_with_sibling(blocks)
    chip_sums = []
    for i, (b, r) in enumerate(zip(blocks, from_sibling)):
        half = b.shape[1] // 2
        dtype = WIRE_DTYPE if i < len(BIG_NAMES) else F32
        chip_sums.append(add_pair(lax.dynamic_slice_in_dim(b, c * half, half, axis=1), r, dtype))
    contributions = scatter_blocks_to_chips(chip_sums)
    reduced = [sum_four(a) for a in contributions]
    shared = swap_halves_with_sibling(reduced)
    big_grads = {n: s.reshape(2 * s.shape[1], s.shape[2]) for n, s in zip(BIG_NAMES, shared[:-1])}
    small_grad_pack = shared[-1].reshape(-1, LANES)
    loss = small_grad_pack[-8, 0]
    out_grad, out_delta, out_m, out_v = {}, {}, {}, {}
    for n in BIG_NAMES:
        out_grad[n] = big_grads[n]
        out_delta[n], out_m[n], out_v[n] = adamw(weights[n], big_grads[n], moments_m[n], moments_v[n])
    small_like = {n: weights[n] for n in SMALL_NAMES}
    d_pack, m_pack, v_pack = adamw(_pack_small(small_like), small_grad_pack, _pack_small({n: moments_m[n] for n in SMALL_NAMES}),
                                   _pack_small({n: moments_v[n] for n in SMALL_NAMES}))
    out_grad.update(_unpack_small(small_grad_pack, small_like))
    out_delta.update(_unpack_small(d_pack, small_like))
    out_m.update(_unpack_small(m_pack, small_like))
    out_v.update(_unpack_small(v_pack, small_like))
    outs = [loss, dx[None]]
    for group in (out_grad, out_delta, out_m, out_v):
        outs.extend(group[n] for n in WEIGHT_NAMES)
    return tuple(outs)


def kernel(x, l0_norm, l0_w_in, l0_a_re, l0_a_im, l0_log_step, l0_b_re, l0_b_im, l0_c_re, l0_c_im, l0_d, l0_w_glu, l0_b_glu, l0_w_out, l1_norm, l1_w_in, l1_sinks, l1_w_out, l2_norm, l2_w_in, l2_a_re, l2_a_im, l2_log_step, l2_b_re, l2_b_im, l2_c_re, l2_c_im, l2_d, l2_w_glu, l2_b_glu, l2_w_out, l3_norm, l3_w_in, l3_sinks, l3_w_out, final_norm, loss_target, m_l0_norm, m_l0_w_in, m_l0_a_re, m_l0_a_im, m_l0_log_step, m_l0_b_re, m_l0_b_im, m_l0_c_re, m_l0_c_im, m_l0_d, m_l0_w_glu, m_l0_b_glu, m_l0_w_out, m_l1_norm, m_l1_w_in, m_l1_sinks, m_l1_w_out, m_l2_norm, m_l2_w_in, m_l2_a_re, m_l2_a_im, m_l2_log_step, m_l2_b_re, m_l2_b_im, m_l2_c_re, m_l2_c_im, m_l2_d, m_l2_w_glu, m_l2_b_glu, m_l2_w_out, m_l3_norm, m_l3_w_in, m_l3_sinks, m_l3_w_out, m_final_norm, v_l0_norm, v_l0_w_in, v_l0_a_re, v_l0_a_im, v_l0_log_step, v_l0_b_re, v_l0_b_im, v_l0_c_re, v_l0_c_im, v_l0_d, v_l0_w_glu, v_l0_b_glu, v_l0_w_out, v_l1_norm, v_l1_w_in, v_l1_sinks, v_l1_w_out, v_l2_norm, v_l2_w_in, v_l2_a_re, v_l2_a_im, v_l2_log_step, v_l2_b_re, v_l2_b_im, v_l2_c_re, v_l2_c_im, v_l2_d, v_l2_w_glu, v_l2_b_glu, v_l2_w_out, v_l3_norm, v_l3_w_in, v_l3_sinks, v_l3_w_out, v_final_norm):
    args = locals()
    weights = {n: args[n] for n in WEIGHT_NAMES}
    moments_m = {n: args["m_" + n] for n in WEIGHT_NAMES}
    moments_v = {n: args["v_" + n] for n in WEIGHT_NAMES}
    return _train_step(x, loss_target, weights, moments_m, moments_v)
```

```python
import functools
import math

import jax
import jax.numpy as jnp
from jax import lax
from jax.experimental import pallas as pl
from jax.experimental.pallas import tpu as pltpu

F32 = jnp.float32
MXU_DTYPE = jnp.bfloat16
WIRE_DTYPE = jnp.bfloat16
MESH = pl.DeviceIdType.MESH

D_MODEL = 1024
BRANCH = 1024
NORM_EPS = 1e-5
SSM_GROUPS = 64
SSM_GROUP = 16
SSM_STATE = 64
S5_CHUNK = 16
LANES = 128
S5_OCT = LANES // SSM_GROUP
S5_OCTETS = SSM_GROUPS // S5_OCT
S5_OCT_IN = S5_CHUNK * LANES
S5_OCT_STATE = S5_OCT * SSM_STATE
S5_STATES = SSM_GROUPS * SSM_STATE
HEAD_DIM = 64
N_Q_HEADS = 16
N_KV_HEADS = 2
GQA_GROUP = N_Q_HEADS // N_KV_HEADS
ATTN_BLOCK = 128
Q_DIM = N_Q_HEADS * HEAD_DIM
KV_DIM = N_KV_HEADS * HEAD_DIM
ROPE_THETA = 10000.0
NEG_INF = -1e30
ADAM_LR = 0.001
ADAM_B1 = 0.9
ADAM_B2 = 0.999
ADAM_EPS = 1e-08
ADAM_WD = 0.01
ADAM_STEP = 10

VMEM_LIMIT_V7X = 56 * 1024 * 1024
ROW_TILE_FWD = 512
ROW_TILE_BWD = 512

SSM_NAMES = ("norm", "w_in", "a_re", "a_im", "log_step", "b_re", "b_im", "c_re", "c_im", "d", "w_glu", "b_glu", "w_out")
ATTN_NAMES = ("norm", "w_in", "sinks", "w_out")


def _weight_names():
    names = []
    for i in range(4):
        for n in (SSM_NAMES if i % 2 == 0 else ATTN_NAMES):
            names.append("l%d_%s" % (i, n))
    names.append("final_norm")
    return names


WEIGHT_NAMES = _weight_names()
BIG_NAMES = [n for n in WEIGHT_NAMES if n.endswith(("w_in", "w_glu", "w_out"))]
SMALL_NAMES = [n for n in WEIGHT_NAMES if n not in BIG_NAMES]


def _params(semantics=None):
    return pltpu.CompilerParams(dimension_semantics=semantics, vmem_limit_bytes=VMEM_LIMIT_V7X)


def _rows(tm, n):
    return pl.BlockSpec((tm, n), lambda i: (i, 0))


def _whole(shape):
    return pl.BlockSpec(shape, lambda i: (0,) * len(shape), pipeline_mode=pl.Buffered(1))


def _sds(shape, dtype=F32):
    return jax.ShapeDtypeStruct(shape, dtype)


def _mm(a, b):
    return jnp.dot(a.astype(MXU_DTYPE), b.astype(MXU_DTYPE), preferred_element_type=F32)


def _mm_tn(a, b):
    return lax.dot_general(a.astype(MXU_DTYPE), b.astype(MXU_DTYPE), (((0,), (0,)), ((), ())), preferred_element_type=F32)


def _mm_nt(a, b):
    return lax.dot_general(a.astype(MXU_DTYPE), b.astype(MXU_DTYPE), (((1,), (1,)), ((), ())), preferred_element_type=F32)


def _sigmoid(x):
    return 1.0 / (1.0 + jnp.exp(-x))


def _silu(x):
    return x * _sigmoid(x)


def _silu_grad(x):
    s = _sigmoid(x)
    return s * (1.0 + x * (1.0 - s))


GELU_C0 = math.sqrt(2.0 / math.pi)
GELU_C1 = 0.044715


def _gelu(x):
    return 0.5 * x * (1.0 + jnp.tanh(GELU_C0 * (x + GELU_C1 * x * x * x)))


def _gelu_grad(x):
    th = jnp.tanh(GELU_C0 * (x + GELU_C1 * x * x * x))
    return 0.5 * (1.0 + th) + 0.5 * x * (1.0 - th * th) * GELU_C0 * (1.0 + 3.0 * GELU_C1 * x * x)


def _rms(x, g):
    r = lax.rsqrt(jnp.mean(x * x, axis=-1, keepdims=True) + NORM_EPS)
    xhat = x * r
    return r, xhat, xhat * g


def _rms_bwd(dh, g, r, xhat):
    dxhat = dh * g
    dx = r * (dxhat - xhat * jnp.mean(dxhat * xhat, axis=-1, keepdims=True))
    return dx, jnp.sum(dh * xhat, axis=0, keepdims=True)


def _swap_half_heads(x):
    n = x.shape[-1]
    lane = lax.broadcasted_iota(jnp.int32, x.shape, x.ndim - 1)
    first = (lane % HEAD_DIM) < (HEAD_DIM // 2)
    return jnp.where(first, pltpu.roll(x, n - HEAD_DIM // 2, x.ndim - 1), pltpu.roll(x, HEAD_DIM // 2, x.ndim - 1))


def _tile_lanes(t, reps):
    return jnp.concatenate([t] * reps, axis=1)


TOKEN_SHAPE = (8, LANES)


def _after(token):
    return ([], []) if token is None else ([token], [_whole(TOKEN_SHAPE)])


def ssm_proj_fwd(x, norm, w_in, token=None):
    t = x.shape[0]
    tm = min(ROW_TILE_FWD, t)
    extra, extra_specs = _after(token)

    def body(x_ref, g_ref, w_ref, *rest):
        u_ref, gate_ref = rest[-2:]
        _, _, h = _rms(x_ref[...], g_ref[...])
        p = _mm(h, w_ref[...])
        u_ref[...] = p[:, :BRANCH]
        gate_ref[...] = p[:, BRANCH:]

    return pl.pallas_call(
        body, name="ssm_proj_fwd", grid=(t // tm,),
        in_specs=[_rows(tm, D_MODEL), _whole((1, D_MODEL)), _whole((D_MODEL, 2 * BRANCH))] + extra_specs,
        out_specs=[_rows(tm, BRANCH), _rows(tm, BRANCH)],
        out_shape=[_sds((t, BRANCH)), _sds((t, BRANCH))],
        compiler_params=_params(("parallel",)),
    )(x, norm, w_in, *extra)


def _chunk_rows(ref, nk, dtype=None):
    rows = jnp.concatenate([ref[pl.ds(s, nk, stride=S5_CHUNK), :] for s in range(S5_CHUNK)], axis=1)
    return rows.astype(MXU_DTYPE if dtype is None else dtype)


def _store_chunk_rows(ref, val, nk):
    for s in range(S5_CHUNK):
        ref[pl.ds(s, nk, stride=S5_CHUNK), :] = val[:, s * LANES:(s + 1) * LANES]


def _own_group_mask():
    row = lax.broadcasted_iota(jnp.int32, (S5_OCT_IN, S5_OCT_STATE), 0)
    col = lax.broadcasted_iota(jnp.int32, (S5_OCT_IN, S5_OCT_STATE), 1)
    return ((row % LANES) // SSM_GROUP) == (col // SSM_STATE)


def _spread_groups(w):
    return jnp.where(_own_group_mask(), jnp.concatenate([w] * (S5_OCT_STATE // LANES), axis=1), 0.0).astype(MXU_DTYPE)


def _fold_groups(p):
    p = jnp.where(_own_group_mask(), p, 0.0)
    return sum(p[:, q * LANES:(q + 1) * LANES] for q in range(S5_OCT_STATE // LANES))


def _fill_toeplitz(win_ref, kd_ref):
    win_ref[...] = jnp.zeros_like(win_ref)
    for s in range(S5_CHUNK):
        for t in range(s, S5_CHUNK):
            win_ref[s * LANES:(s + 1) * LANES, t * LANES:(t + 1) * LANES] = kd_ref[t - s].astype(MXU_DTYPE)


TOEPLITZ_BLOCK = 512
_TOEPLITZ_BLOCKS = [(lo, lo + TOEPLITZ_BLOCK) for lo in range(0, S5_OCT_IN, TOEPLITZ_BLOCK)]


def _strip(t):
    return pl.BlockSpec((t, LANES), lambda b: (0, b))


def _oct_states(nk):
    return pl.BlockSpec((nk, S5_OCT_STATE), lambda b: (0, b))


OCT_W = pl.BlockSpec((None, S5_OCT_IN, LANES), lambda b: (b, 0, 0))
OCT_KD = pl.BlockSpec((None, S5_CHUNK, LANES, LANES), lambda b: (b, 0, 0, 0))


def s5_chunk_states(u, ws_re, ws_im):
    t = u.shape[0]
    nk = t // S5_CHUNK

    def body(u_ref, wr_ref, wi_ref, re_ref, im_ref):
        uc = _chunk_rows(u_ref, nk)
        re_ref[...] = _mm(uc, _spread_groups(wr_ref[...]))
        im_ref[...] = _mm(uc, _spread_groups(wi_ref[...]))

    return pl.pallas_call(
        body, name="s5_chunk_states", grid=(S5_OCTETS,),
        in_specs=[_strip(t), OCT_W, OCT_W], out_specs=[_oct_states(nk), _oct_states(nk)],
        out_shape=[_sds((nk, S5_STATES)), _sds((nk, S5_STATES))],
        compiler_params=_params(("parallel",)),
    )(u, ws_re, ws_im)


def s5_scan_fwd(s_re, s_im, a_re, a_im):
    nk = s_re.shape[0]

    def body(sre_ref, sim_ref, ar_ref, ai_ref, hre_ref, him_ref):
        ar = ar_ref[...]
        ai = ai_ref[...]

        def step(k, carry):
            hr, hi = carry
            hre_ref[pl.ds(k, 1), :] = hr
            him_ref[pl.ds(k, 1), :] = hi
            sr = sre_ref[pl.ds(k, 1), :]
            si = sim_ref[pl.ds(k, 1), :]
            return ar * hr - ai * hi + sr, ai * hr + ar * hi + si

        zero = jnp.zeros((1, S5_STATES), F32)
        lax.fori_loop(0, nk, step, (zero, zero))

    vm = pl.BlockSpec(memory_space=pltpu.VMEM)
    return pl.pallas_call(
        body, name="s5_scan_fwd", in_specs=[vm, vm, vm, vm], out_specs=[vm, vm],
        out_shape=[_sds((nk, S5_STATES)), _sds((nk, S5_STATES))],
        compiler_params=_params(),
    )(s_re, s_im, a_re, a_im)


def s5_outputs(u, h_re, h_im, kd, wo_re, wo_im):
    t = u.shape[0]
    nk = t // S5_CHUNK

    def body(u_ref, hre_ref, him_ref, kd_ref, wor_ref, woi_ref, y_ref, win_ref):
        _fill_toeplitz(win_ref, kd_ref)
        uc = _chunk_rows(u_ref, nk)
        y = jnp.concatenate([_mm(uc[:, :hi], win_ref[:hi, lo:hi]) for lo, hi in _TOEPLITZ_BLOCKS], axis=1)
        y = y + _mm_nt(hre_ref[...], _spread_groups(wor_ref[...])) + _mm_nt(him_ref[...], _spread_groups(woi_ref[...]))
        _store_chunk_rows(y_ref, y, nk)

    return pl.pallas_call(
        body, name="s5_outputs", grid=(S5_OCTETS,),
        in_specs=[_strip(t), _oct_states(nk), _oct_states(nk), OCT_KD, OCT_W, OCT_W],
        out_specs=_strip(t), out_shape=_sds((t, BRANCH)),
        scratch_shapes=[pltpu.VMEM((S5_OCT_IN, S5_OCT_IN), MXU_DTYPE)],
        compiler_params=_params(("parallel",)),
    )(u, h_re, h_im, kd, wo_re, wo_im)


def s5_state_grads(dy, wo_re, wo_im):
    t = dy.shape[0]
    nk = t // S5_CHUNK

    def body(dy_ref, wor_ref, woi_ref, re_ref, im_ref):
        dyc = _chunk_rows(dy_ref, nk)
        re_ref[...] = _mm(dyc, _spread_groups(wor_ref[...]))
        im_ref[...] = _mm(dyc, _spread_groups(woi_ref[...]))

    return pl.pallas_call(
        body, name="s5_state_grads", grid=(S5_OCTETS,),
        in_specs=[_strip(t), OCT_W, OCT_W], out_specs=[_oct_states(nk), _oct_states(nk)],
        out_shape=[_sds((nk, S5_STATES)), _sds((nk, S5_STATES))],
        compiler_params=_params(("parallel",)),
    )(dy, wo_re, wo_im)


def s5_scan_bwd(dh_re, dh_im, h_re, h_im, a_re, a_im):
    nk = dh_re.shape[0]

    def body(dhr_ref, dhi_ref, hr_ref, hi_ref, ar_ref, ai_ref, dsr_ref, dsi_ref, dar_ref, dai_ref):
        ar = ar_ref[...]
        ai = ai_ref[...]

        dar_ref[...] = jnp.zeros_like(dar_ref)
        dai_ref[...] = jnp.zeros_like(dai_ref)

        def step(i, carry):
            gr, gi = carry
            k = nk - 1 - i
            dhr = dhr_ref[pl.ds(k, 1), :]
            dhi = dhi_ref[pl.ds(k, 1), :]
            dsr_ref[pl.ds(k, 1), :] = gr
            dsi_ref[pl.ds(k, 1), :] = gi
            hr = hr_ref[pl.ds(k, 1), :]
            hi = hi_ref[pl.ds(k, 1), :]
            dar_ref[...] += gr * hr + gi * hi
            dai_ref[...] += gi * hr - gr * hi
            return dhr + ar * gr + ai * gi, dhi - ai * gr + ar * gi

        zero = jnp.zeros((1, S5_STATES), F32)
        lax.fori_loop(0, nk, step, (zero, zero))

    vm = pl.BlockSpec(memory_space=pltpu.VMEM)
    return pl.pallas_call(
        body, name="s5_scan_bwd", in_specs=[vm] * 6, out_specs=[vm] * 4,
        out_shape=[_sds((nk, S5_STATES)), _sds((nk, S5_STATES)), _sds((1, S5_STATES)), _sds((1, S5_STATES))],
        input_output_aliases={0: 0, 1: 1}, compiler_params=_params(),
    )(dh_re, dh_im, h_re, h_im, a_re, a_im)


def s5_input_grads(dy, ds_re, ds_im, kd, ws_re, ws_im):
    t = dy.shape[0]
    nk = t // S5_CHUNK

    def body(dy_ref, dsr_ref, dsi_ref, kd_ref, wsr_ref, wsi_ref, du_ref, win_ref):
        _fill_toeplitz(win_ref, kd_ref)
        dyc = _chunk_rows(dy_ref, nk)
        du = jnp.concatenate([_mm_nt(dyc[:, lo:], win_ref[lo:hi, lo:]) for lo, hi in _TOEPLITZ_BLOCKS], axis=1)
        du = du + _mm_nt(dsr_ref[...], _spread_groups(wsr_ref[...])) + _mm_nt(dsi_ref[...], _spread_groups(wsi_ref[...]))
        _store_chunk_rows(du_ref, du, nk)

    return pl.pallas_call(
        body, name="s5_input_grads", grid=(S5_OCTETS,),
        in_specs=[_strip(t), _oct_states(nk), _oct_states(nk), OCT_KD, OCT_W, OCT_W],
        out_specs=_strip(t), out_shape=_sds((t, BRANCH)),
        scratch_shapes=[pltpu.VMEM((S5_OCT_IN, S5_OCT_IN), MXU_DTYPE)],
        compiler_params=_params(("parallel",)),
    )(dy, ds_re, ds_im, kd, ws_re, ws_im)


def s5_weight_grads(u, dy, h_re, h_im, ds_re, ds_im):
    t = u.shape[0]
    nk = t // S5_CHUNK

    def body(u_ref, dy_ref, hre_ref, him_ref, dsr_ref, dsi_ref, dkd_ref, dwsr_ref, dwsi_ref, dwor_ref, dwoi_ref):
        dyc = _chunk_rows(dy_ref, nk, F32)
        uct = _chunk_rows(u_ref, nk, F32).T.astype(MXU_DTYPE)
        dyct = dyc.T.astype(MXU_DTYPE)
        dyc = dyc.astype(MXU_DTYPE)
        dwsr_ref[...] = _fold_groups(_mm(uct, dsr_ref[...]))
        dwsi_ref[...] = _fold_groups(_mm(uct, dsi_ref[...]))
        dwor_ref[...] = _fold_groups(_mm(dyct, hre_ref[...]))
        dwoi_ref[...] = _fold_groups(_mm(dyct, him_ref[...]))
        dkd_ref[...] = jnp.zeros_like(dkd_ref)
        for tt in range(0, S5_CHUNK, 2):
            p = _mm(uct[:(tt + 2) * LANES], dyc[:, tt * LANES:(tt + 2) * LANES])
            for s in range(tt + 2):
                rows = p[s * LANES:(s + 1) * LANES]
                if s <= tt:
                    dkd_ref[tt - s] += rows[:, :LANES]
                dkd_ref[tt + 1 - s] += rows[:, LANES:]

    return pl.pallas_call(
        body, name="s5_weight_grads", grid=(S5_OCTETS,),
        in_specs=[_strip(t), _strip(t)] + [_oct_states(nk)] * 4,
        out_specs=[OCT_KD, OCT_W, OCT_W, OCT_W, OCT_W],
        out_shape=[_sds((S5_OCTETS, S5_CHUNK, LANES, LANES))] + [_sds((S5_OCTETS, S5_OCT_IN, LANES))] * 4,
        compiler_params=_params(("parallel",)),
    )(u, dy, h_re, h_im, ds_re, ds_im)


def ssm_mix_fwd(x, u, gate, y_scan, d, w_glu, b_glu, w_out):
    t = x.shape[0]
    tm = min(ROW_TILE_FWD, t)

    def body(x_ref, u_ref, gate_ref, ys_ref, d_ref, wg_ref, bg_ref, wo_ref, y_ref, g2_ref, xo_ref):
        y = ys_ref[...] + d_ref[...] * u_ref[...]
        z0 = _gelu(y)
        g2 = _mm(z0, wg_ref[...]) + bg_ref[...]
        a = z0 * _sigmoid(g2) * _silu(gate_ref[...])
        y_ref[...] = y
        g2_ref[...] = g2
        xo_ref[...] = x_ref[...] + _mm(a, wo_ref[...])

    row = _rows(tm, BRANCH)
    vec = _whole((1, BRANCH))
    mat = _whole((BRANCH, BRANCH))
    return pl.pallas_call(
        body, name="ssm_mix_fwd", grid=(t // tm,),
        in_specs=[row, row, row, row, vec, mat, vec, mat],
        out_specs=[row, row, row],
        out_shape=[_sds((t, BRANCH))] * 3,
        compiler_params=_params(("parallel",)),
    )(x, u, gate, y_scan, d, w_glu, b_glu, w_out)


def ssm_mix_bwd(dxo, u, gate, y, g2, w_glu, w_out, token=None):
    t = dxo.shape[0]
    tm = min(ROW_TILE_BWD, t)
    extra, extra_specs = _after(token)

    def body(dxo_ref, u_ref, gate_ref, y_ref, g2_ref, wgt_ref, wot_ref, *rest):
        dy_ref, dgate_ref, dwo_ref, dwg_ref, dbg_ref, dd_ref = rest[-6:]

        @pl.when(pl.program_id(0) == 0)
        def _():
            dwo_ref[...] = jnp.zeros_like(dwo_ref)
            dwg_ref[...] = jnp.zeros_like(dwg_ref)
            dbg_ref[...] = jnp.zeros_like(dbg_ref)
            dd_ref[...] = jnp.zeros_like(dd_ref)

        dxo = dxo_ref[...]
        gate = gate_ref[...]
        y = y_ref[...]
        z0 = _gelu(y)
        sg = _sigmoid(g2_ref[...])
        z = z0 * sg
        sgate = _silu(gate)
        da = _mm_nt(dxo, wot_ref[...])
        dwo_ref[...] += _mm_tn(z * sgate, dxo)
        dz = da * sgate
        dgate_ref[...] = da * z * _silu_grad(gate)
        dg2 = dz * z0 * sg * (1.0 - sg)
        dbg_ref[...] += jnp.sum(dg2, axis=0, keepdims=True)
        dwg_ref[...] += _mm_tn(z0, dg2)
        dz0 = dz * sg + _mm_nt(dg2, wgt_ref[...])
        dy = dz0 * _gelu_grad(y)
        dd_ref[...] += jnp.sum(dy * u_ref[...], axis=0, keepdims=True)
        dy_ref[...] = dy

    row = _rows(tm, BRANCH)
    vec = _whole((1, BRANCH))
    mat = _whole((BRANCH, BRANCH))
    return pl.pallas_call(
        body, name="ssm_mix_bwd", grid=(t // tm,),
        in_specs=[row, row, row, row, row, mat, mat] + extra_specs,
        out_specs=[row, row, mat, mat, vec, vec],
        out_shape=[_sds((t, BRANCH)), _sds((t, BRANCH)), _sds((BRANCH, D_MODEL)), _sds((BRANCH, BRANCH)),
                   _sds((1, BRANCH)), _sds((1, BRANCH))],
        compiler_params=_params(("arbitrary",)),
    )(dxo, u, gate, y, g2, w_glu, w_out, *extra)


def ssm_proj_bwd(x, norm, dxo, dy, du_scan, dgate, d, w_in):
    t = x.shape[0]
    tm = min(ROW_TILE_BWD, t)
    n = 2 * BRANCH

    def body(x_ref, g_ref, dxo_ref, dy_ref, dus_ref, dgate_ref, d_ref, wt_ref, dx_ref, dw_ref, dg_ref):
        @pl.when(pl.program_id(0) == 0)
        def _():
            dw_ref[...] = jnp.zeros_like(dw_ref)
            dg_ref[...] = jnp.zeros_like(dg_ref)

        g = g_ref[...]
        r, xhat, h = _rms(x_ref[...], g)
        du = dus_ref[...] + d_ref[...] * dy_ref[...]
        dproj = jnp.concatenate([du, dgate_ref[...]], axis=1)
        dh = _mm_nt(dproj, wt_ref[...])
        dw_ref[...] += _mm_tn(h, dproj)
        dx, dg = _rms_bwd(dh, g, r, xhat)
        dg_ref[...] += dg
        dx_ref[...] = dxo_ref[...] + dx

    row = _rows(tm, D_MODEL)
    vec = _whole((1, D_MODEL))
    return pl.pallas_call(
        body, name="ssm_proj_bwd", grid=(t // tm,),
        in_specs=[row, vec, row, row, row, row, vec, _whole((D_MODEL, n))],
        out_specs=[row, _whole((D_MODEL, n)), vec],
        out_shape=[_sds((t, D_MODEL)), _sds((D_MODEL, n)), _sds((1, D_MODEL))],
        compiler_params=_params(("arbitrary",)),
    )(x, norm, dxo, dy, du_scan, dgate, d, w_in)


ATTN_N = Q_DIM + 2 * KV_DIM + BRANCH


def attn_proj_fwd(x, norm, w_in, cos2, sin2):
    t = x.shape[0]
    tm = min(ROW_TILE_FWD, t)

    def body(x_ref, g_ref, w_ref, cos_ref, sin_ref, q_ref, k_ref, v_ref, gate_ref):
        _, _, h = _rms(x_ref[...], g_ref[...])
        p = _mm(h, w_ref[...])
        cs = cos_ref[...]
        sn = sin_ref[...]
        q = p[:, :Q_DIM]
        k = p[:, Q_DIM:Q_DIM + KV_DIM]
        q_ref[...] = q * _tile_lanes(cs, Q_DIM // LANES) + _swap_half_heads(q) * _tile_lanes(sn, Q_DIM // LANES)
        k_ref[...] = k * cs + _swap_half_heads(k) * sn
        v_ref[...] = p[:, Q_DIM + KV_DIM:Q_DIM + 2 * KV_DIM]
        gate_ref[...] = p[:, Q_DIM + 2 * KV_DIM:]

    return pl.pallas_call(
        body, name="attn_proj_fwd", grid=(t // tm,),
        in_specs=[_rows(tm, D_MODEL), _whole((1, D_MODEL)), _whole((D_MODEL, ATTN_N)), _rows(tm, LANES), _rows(tm, LANES)],
        out_specs=[_rows(tm, Q_DIM), _rows(tm, KV_DIM), _rows(tm, KV_DIM), _rows(tm, BRANCH)],
        out_shape=[_sds((t, Q_DIM)), _sds((t, KV_DIM)), _sds((t, KV_DIM)), _sds((t, BRANCH))],
        compiler_params=_params(("parallel",)),
    )(x, norm, w_in, cos2, sin2)


GQA_LANES = GQA_GROUP * ATTN_BLOCK


def _window_masks(first_block):
    kj = lax.broadcasted_iota(jnp.int32, (ATTN_BLOCK, GQA_LANES), 0)
    qi = lax.broadcasted_iota(jnp.int32, (ATTN_BLOCK, GQA_LANES), 1) % ATTN_BLOCK
    return kj > qi, kj > jnp.where(first_block, qi, ATTN_BLOCK)


def _fold(upper, both):
    return jnp.where(upper, both[:ATTN_BLOCK], both[ATTN_BLOCK:])


def _unfold(upper, tile):
    return jnp.concatenate([jnp.where(upper, tile, 0.0), jnp.where(upper, 0.0, tile)], axis=0).astype(MXU_DTYPE)


def _stack_heads(ref, group):
    return jnp.concatenate([ref[:, h * HEAD_DIM:(h + 1) * HEAD_DIM] for h in range(group * GQA_GROUP, (group + 1) * GQA_GROUP)], axis=0)


def _unstack_heads(ref, group, stacked):
    for n in range(GQA_GROUP):
        h = group * GQA_GROUP + n
        ref[:, h * HEAD_DIM:(h + 1) * HEAD_DIM] = stacked[n * ATTN_BLOCK:(n + 1) * ATTN_BLOCK]


def _sink_row(sink_ref, group):
    return jnp.concatenate([jnp.full((1, ATTN_BLOCK), sink_ref[group * GQA_GROUP + n], F32) for n in range(GQA_GROUP)], axis=1)


def _lane_is(h):
    return lax.broadcasted_iota(jnp.int32, (1, LANES), 1) == h


def attn_fwd(q, k, v, sinks):
    t = q.shape[0]
    nb = t // ATTN_BLOCK
    scale = HEAD_DIM ** -0.5

    def body(sink_ref, q_ref, kc_ref, kp_ref, vc_ref, vp_ref, o_ref, lse_ref):
        keys = jnp.concatenate([kp_ref[...], kc_ref[...]], axis=0).astype(MXU_DTYPE)
        vals = jnp.concatenate([vp_ref[...], vc_ref[...]], axis=0).astype(MXU_DTYPE)
        upper, dead = _window_masks(pl.program_id(0) == 0)
        for g in range(N_KV_HEADS):
            kv = slice(g * HEAD_DIM, (g + 1) * HEAD_DIM)
            qs = _stack_heads(q_ref, g) * scale
            s = jnp.where(dead, NEG_INF, _fold(upper, _mm_nt(keys[:, kv], qs)))
            sink = _sink_row(sink_ref, g)
            m = jnp.maximum(jnp.max(s, axis=0, keepdims=True), sink)
            p = jnp.exp(s - m)
            den = jnp.sum(p, axis=0, keepdims=True) + jnp.exp(sink - m)
            _unstack_heads(o_ref, g, _mm_tn(_unfold(upper, p * (1.0 / den)), vals[:, kv]))
            lse = m + jnp.log(den)
            for n in range(GQA_GROUP):
                lse_ref[pl.ds(g * GQA_GROUP + n, 1), :] = lse[:, n * ATTN_BLOCK:(n + 1) * ATTN_BLOCK]

    cur = lambda n: pl.BlockSpec((ATTN_BLOCK, n), lambda i: (i, 0))
    prev = lambda n: pl.BlockSpec((ATTN_BLOCK, n), lambda i: (jnp.maximum(i - 1, 0), 0))
    return pl.pallas_call(
        body, name="attn_fwd", grid=(nb,),
        in_specs=[pl.BlockSpec(memory_space=pltpu.SMEM), cur(Q_DIM), cur(KV_DIM), prev(KV_DIM), cur(KV_DIM), prev(KV_DIM)],
        out_specs=[cur(Q_DIM), pl.BlockSpec((N_Q_HEADS, ATTN_BLOCK), lambda i: (0, i))],
        out_shape=[_sds((t, Q_DIM)), _sds((N_Q_HEADS, t))],
        compiler_params=_params(("parallel",)),
    )(sinks, q, k, k, v, v)


def attn_bwd(q, k, v, sinks, o, lse, do):
    t = q.shape[0]
    nb = t // ATTN_BLOCK
    scale = HEAD_DIM ** -0.5

    def body(sink_ref, q_ref, kc_ref, kp_ref, vc_ref, vp_ref, o_ref, lse_ref, do_ref,
             dq_ref, dk_ref, dv_ref, dsink_ref, dk_carry, dv_carry):
        i = pl.program_id(0)

        @pl.when(i == 0)
        def _():
            dsink_ref[...] = jnp.zeros_like(dsink_ref)
            dk_carry[...] = jnp.zeros_like(dk_carry)
            dv_carry[...] = jnp.zeros_like(dv_carry)

        @pl.when(i < nb)
        def _():
            keys = jnp.concatenate([kp_ref[...], kc_ref[...]], axis=0).astype(MXU_DTYPE)
            vals = jnp.concatenate([vp_ref[...], vc_ref[...]], axis=0).astype(MXU_DTYPE)
            upper, dead = _window_masks(i == 0)
            dsink = jnp.zeros((1, LANES), F32)
            dk_heads = []
            dv_heads = []
            for g in range(N_KV_HEADS):
                kv = slice(g * HEAD_DIM, (g + 1) * HEAD_DIM)
                qs = (_stack_heads(q_ref, g) * scale).astype(MXU_DTYPE)
                dos = _stack_heads(do_ref, g)
                lse = jnp.concatenate([lse_ref[pl.ds(g * GQA_GROUP + n, 1), :] for n in range(GQA_GROUP)], axis=1)
                s = jnp.where(dead, NEG_INF, _fold(upper, _mm_nt(keys[:, kv], qs)))
                p = jnp.exp(s - lse)
                delta = _mm_f32(jnp.ones((8, HEAD_DIM), F32), dos * _stack_heads(o_ref, g), ((1,), (1,)))[:1]
                dos = dos.astype(MXU_DTYPE)
                ds = _unfold(upper, p * (_fold(upper, _mm_nt(vals[:, kv], dos)) - delta))
                _unstack_heads(dq_ref, g, _mm_tn(ds, keys[:, kv]) * scale)
                dk_heads.append(_mm(ds, qs))
                dv_heads.append(_mm(_unfold(upper, p), dos))
                at_sink = jnp.exp(_sink_row(sink_ref, g) - lse) * delta
                for n in range(GQA_GROUP):
                    dsink = dsink + jnp.where(_lane_is(g * GQA_GROUP + n), -jnp.sum(at_sink[:, n * ATTN_BLOCK:(n + 1) * ATTN_BLOCK]), 0.0)
            dkk = jnp.concatenate(dk_heads, axis=1)
            dvv = jnp.concatenate(dv_heads, axis=1)
            dsink_ref[...] += dsink
            dk_ref[...] = dk_carry[...] + dkk[:ATTN_BLOCK]
            dv_ref[...] = dv_carry[...] + dvv[:ATTN_BLOCK]
            dk_carry[...] = dkk[ATTN_BLOCK:]
            dv_carry[...] = dvv[ATTN_BLOCK:]

        @pl.when(i == nb)
        def _():
            dk_ref[...] = dk_carry[...]
            dv_ref[...] = dv_carry[...]

    last = nb - 1
    cur = lambda n: pl.BlockSpec((ATTN_BLOCK, n), lambda i: (jnp.minimum(i, last), 0))
    prev = lambda n: pl.BlockSpec((ATTN_BLOCK, n), lambda i: (jnp.clip(i - 1, 0, last), 0))
    late = lambda n: pl.BlockSpec((ATTN_BLOCK, n), lambda i: (i, 0))
    dq, dk_late, dv_late, dsinks = pl.pallas_call(
        body, name="attn_bwd", grid=(nb + 1,),
        in_specs=[pl.BlockSpec(memory_space=pltpu.SMEM), cur(Q_DIM), cur(KV_DIM), prev(KV_DIM), cur(KV_DIM), prev(KV_DIM),
                  cur(Q_DIM), pl.BlockSpec((N_Q_HEADS, ATTN_BLOCK), lambda i: (0, jnp.minimum(i, last))), cur(Q_DIM)],
        out_specs=[cur(Q_DIM), late(KV_DIM), late(KV_DIM), _whole((1, LANES))],
        out_shape=[_sds((t, Q_DIM)), _sds((t + ATTN_BLOCK, KV_DIM)), _sds((t + ATTN_BLOCK, KV_DIM)), _sds((1, LANES))],
        scratch_shapes=[pltpu.VMEM((ATTN_BLOCK, KV_DIM), F32), pltpu.VMEM((ATTN_BLOCK, KV_DIM), F32)],
        compiler_params=_params(("arbitrary",)),
    )(sinks, q, k, k, v, v, o, lse, do)
    return dq, dk_late[ATTN_BLOCK:], dv_late[ATTN_BLOCK:], dsinks


def attn_out_fwd(x, o, gate, w_out):
    t = x.shape[0]
    tm = min(ROW_TILE_FWD, t)

    def body(x_ref, o_ref, gate_ref, w_ref, xo_ref):
        xo_ref[...] = x_ref[...] + _mm(o_ref[...] * _silu(gate_ref[...]), w_ref[...])

    row = _rows(tm, D_MODEL)
    return pl.pallas_call(
        body, name="attn_out_fwd", grid=(t // tm,),
        in_specs=[row, row, row, _whole((Q_DIM, D_MODEL))], out_specs=row, out_shape=_sds((t, D_MODEL)),
        compiler_params=_params(("parallel",)),
    )(x, o, gate, w_out)


def attn_out_bwd(dxo, o, gate, w_out, token=None):
    t = dxo.shape[0]
    tm = min(ROW_TILE_BWD, t)
    extra, extra_specs = _after(token)

    def body(dxo_ref, o_ref, gate_ref, wt_ref, *rest):
        do_ref, dgate_ref, dw_ref = rest[-3:]

        @pl.when(pl.program_id(0) == 0)
        def _():
            dw_ref[...] = jnp.zeros_like(dw_ref)

        dxo = dxo_ref[...]
        o = o_ref[...]
        gate = gate_ref[...]
        sgate = _silu(gate)
        da = _mm_nt(dxo, wt_ref[...])
        dw_ref[...] += _mm_tn(o * sgate, dxo)
        do_ref[...] = da * sgate
        dgate_ref[...] = da * o * _silu_grad(gate)

    row = _rows(tm, D_MODEL)
    mat = _whole((Q_DIM, D_MODEL))
    return pl.pallas_call(
        body, name="attn_out_bwd", grid=(t // tm,),
        in_specs=[row, row, row, mat] + extra_specs, out_specs=[row, row, mat],
        out_shape=[_sds((t, Q_DIM)), _sds((t, BRANCH)), _sds((Q_DIM, D_MODEL))],
        compiler_params=_params(("arbitrary",)),
    )(dxo, o, gate, w_out, *extra)


def attn_proj_bwd(x, norm, dxo, dq, dk, dv, dgate, cos2, sin2, w_in):
    t = x.shape[0]
    tm = min(ROW_TILE_BWD, t)

    def body(x_ref, g_ref, dxo_ref, dq_ref, dk_ref, dv_ref, dgate_ref, cos_ref, sin_ref, wt_ref, dx_ref, dw_ref, dg_ref):
        @pl.when(pl.program_id(0) == 0)
        def _():
            dw_ref[...] = jnp.zeros_like(dw_ref)
            dg_ref[...] = jnp.zeros_like(dg_ref)

        g = g_ref[...]
        r, xhat, h = _rms(x_ref[...], g)
        cs = cos_ref[...]
        sn = sin_ref[...]
        dqr = dq_ref[...]
        dkr = dk_ref[...]
        dq = dqr * _tile_lanes(cs, Q_DIM // LANES) + _swap_half_heads(dqr * _tile_lanes(sn, Q_DIM // LANES))
        dk = dkr * cs + _swap_half_heads(dkr * sn)
        dproj = jnp.concatenate([dq, dk, dv_ref[...], dgate_ref[...]], axis=1)
        dh = _mm_nt(dproj, wt_ref[...])
        dw_ref[...] += _mm_tn(h, dproj)
        dx, dg = _rms_bwd(dh, g, r, xhat)
        dg_ref[...] += dg
        dx_ref[...] = dxo_ref[...] + dx

    row = _rows(tm, D_MODEL)
    vec = _whole((1, D_MODEL))
    return pl.pallas_call(
        body, name="attn_proj_bwd", grid=(t // tm,),
        in_specs=[row, vec, row, _rows(tm, Q_DIM), _rows(tm, KV_DIM), _rows(tm, KV_DIM), _rows(tm, BRANCH),
                  _rows(tm, LANES), _rows(tm, LANES), _whole((D_MODEL, ATTN_N))],
        out_specs=[row, _whole((D_MODEL, ATTN_N)), vec],
        out_shape=[_sds((t, D_MODEL)), _sds((D_MODEL, ATTN_N)), _sds((1, D_MODEL))],
        compiler_params=_params(("arbitrary",)),
    )(x, norm, dxo, dq, dk, dv, dgate, cos2, sin2, w_in)


def loss_head(x, norm, target):
    t = x.shape[0]
    tm = min(ROW_TILE_FWD, t)

    def body(x_ref, g_ref, tgt_ref, loss_ref, dx_ref, dg_ref):
        @pl.when(pl.program_id(0) == 0)
        def _():
            loss_ref[...] = jnp.zeros_like(loss_ref)
            dg_ref[...] = jnp.zeros_like(dg_ref)

        g = g_ref[...]
        r, xhat, y = _rms(x_ref[...], g)
        err = y - tgt_ref[...]
        loss_ref[...] += 0.5 * jnp.sum(jnp.mean(err * err, axis=-1, keepdims=True), axis=0, keepdims=True)
        dx, dg = _rms_bwd(err * (1.0 / D_MODEL), g, r, xhat)
        dg_ref[...] += dg
        dx_ref[...] = dx

    row = _rows(tm, D_MODEL)
    vec = _whole((1, D_MODEL))
    return pl.pallas_call(
        body, name="loss_head", grid=(t // tm,),
        in_specs=[row, vec, row], out_specs=[_whole((1, 1)), row, vec],
        out_shape=[_sds((1, 1)), _sds((t, D_MODEL)), _sds((1, D_MODEL))],
        compiler_params=_params(("arbitrary",)),
    )(x, norm, target)


OCT_TILE = pl.BlockSpec((None, LANES, LANES), lambda b: (b, 0, 0))
N_LAGS = S5_CHUNK + 1


def _cmul(ar, ai, br, bi):
    return ar * br - ai * bi, ar * bi + ai * br


def _cmul_conj(ar, ai, br, bi):
    return ar * br + ai * bi, ar * bi - ai * br


def _mm_f32(a, b, dims):
    return lax.dot_general(a, b, (dims, ((), ())), precision=lax.Precision.HIGHEST, preferred_element_type=F32)


def _s5_discretise(ar, ai, ls, br, bi):
    dt = jnp.exp(ls)
    xr = ar * dt
    xi = ai * dt
    mag = jnp.exp(xr)
    first = (mag * jnp.cos(xi), mag * jnp.sin(xi))
    powers = [(jnp.ones_like(xr), jnp.zeros_like(xr)), first]
    for _ in range(2, N_LAGS):
        powers.append(_cmul(*powers[-1], *first))
    den = ar * ar + ai * ai
    nr = powers[1][0] - 1.0
    ni = powers[1][1]
    fr = (nr * ar + ni * ai) / den
    fi = (ni * ar - nr * ai) / den
    bbr, bbi = _cmul(fr, fi, br, bi)
    return dt, powers, (fr, fi), (bbr, bbi), den


def _same_group_tile():
    row = lax.broadcasted_iota(jnp.int32, (LANES, LANES), 0)
    col = lax.broadcasted_iota(jnp.int32, (LANES, LANES), 1)
    return (row // SSM_GROUP) == (col // SSM_GROUP)


def _first_copy_lanes():
    return lax.broadcasted_iota(jnp.int32, (LANES, LANES), 1) < SSM_STATE


def s5_param_fwd(tiles):
    def body(ar_ref, ai_ref, ls_ref, br_ref, bi_ref, cr_ref, ci_ref, kd_ref, wsr_ref, wsi_ref, wor_ref, woi_ref, pr_ref, pi_ref):
        cr = cr_ref[...]
        ci = ci_ref[...]
        _, powers, _, (bbr, bbi), _ = _s5_discretise(ar_ref[...], ai_ref[...], ls_ref[...], br_ref[...], bi_ref[...])
        once = _first_copy_lanes()
        crm = jnp.where(once, cr, 0.0)
        cim = jnp.where(once, ci, 0.0)
        same = _same_group_tile()
        for lag in range(S5_CHUNK):
            er, ei = powers[lag]
            xr, xi = _cmul(er, ei, bbr, bbi)
            rows = pl.ds((S5_CHUNK - 1 - lag) * LANES, LANES)
            wsr_ref[rows, :] = xr
            wsi_ref[rows, :] = xi
            k = _mm_f32(xr, crm, ((1,), (1,))) - _mm_f32(xi, cim, ((1,), (1,)))
            kd_ref[lag] = jnp.where(same, k, 0.0)
        for t in range(S5_CHUNK):
            er, ei = powers[t + 1]
            zr, zi = _cmul(er, ei, cr, ci)
            wor_ref[pl.ds(t * LANES, LANES), :] = zr
            woi_ref[pl.ds(t * LANES, LANES), :] = -zi
        pr_ref[...] = powers[S5_CHUNK][0]
        pi_ref[...] = powers[S5_CHUNK][1]

    return pl.pallas_call(
        body, name="s5_param_fwd", grid=(S5_OCTETS,),
        in_specs=[OCT_TILE] * 7, out_specs=[OCT_KD, OCT_W, OCT_W, OCT_W, OCT_W, OCT_TILE, OCT_TILE],
        out_shape=[_sds((S5_OCTETS, S5_CHUNK, LANES, LANES))] + [_sds((S5_OCTETS, S5_OCT_IN, LANES))] * 4
                  + [_sds((S5_OCTETS, LANES, LANES))] * 2,
        compiler_params=_params(("parallel",)),
    )(*tiles)


def s5_param_bwd(tiles, dkd, dws_re, dws_im, dwo_re, dwo_im, dp_re, dp_im):
    def body(ar_ref, ai_ref, ls_ref, br_ref, bi_ref, cr_ref, ci_ref, dkd_ref, dwsr_ref, dwsi_ref, dwor_ref, dwoi_ref, dpr_ref, dpi_ref,
             dar_ref, dai_ref, dls_ref, dbr_ref, dbi_ref, dcr_ref, dci_ref):
        ar = ar_ref[...]
        ai = ai_ref[...]
        br = br_ref[...]
        bi = bi_ref[...]
        cr = cr_ref[...]
        ci = ci_ref[...]
        dt, powers, (fr, fi), (bbr, bbi), den = _s5_discretise(ar, ai, ls_ref[...], br, bi)
        once = _first_copy_lanes()
        crm = jnp.where(once, cr, 0.0)
        cim = jnp.where(once, ci, 0.0)
        same = _same_group_tile()
        zero = jnp.zeros((LANES, LANES), F32)
        dpow = [[zero, zero] for _ in range(N_LAGS)]
        dbbr, dbbi, dcr, dci = zero, zero, zero, zero
        for lag in range(S5_CHUNK):
            er, ei = powers[lag]
            xr, xi = _cmul(er, ei, bbr, bbi)
            rows = pl.ds((S5_CHUNK - 1 - lag) * LANES, LANES)
            g = jnp.where(same, dkd_ref[lag], 0.0)
            dxr = dwsr_ref[rows, :] + _mm_f32(g, crm, ((1,), (0,)))
            dxi = dwsi_ref[rows, :] - _mm_f32(g, cim, ((1,), (0,)))
            dcr = dcr + jnp.where(once, _mm_f32(g, xr, ((0,), (0,))), 0.0)
            dci = dci - jnp.where(once, _mm_f32(g, xi, ((0,), (0,))), 0.0)
            a, b = _cmul_conj(bbr, bbi, dxr, dxi)
            dpow[lag][0] = dpow[lag][0] + a
            dpow[lag][1] = dpow[lag][1] + b
            a, b = _cmul_conj(er, ei, dxr, dxi)
            dbbr = dbbr + a
            dbbi = dbbi + b
        for t in range(S5_CHUNK):
            er, ei = powers[t + 1]
            dzr = dwor_ref[pl.ds(t * LANES, LANES), :]
            dzi = -dwoi_ref[pl.ds(t * LANES, LANES), :]
            a, b = _cmul_conj(cr, ci, dzr, dzi)
            dpow[t + 1][0] = dpow[t + 1][0] + a
            dpow[t + 1][1] = dpow[t + 1][1] + b
            a, b = _cmul_conj(er, ei, dzr, dzi)
            dcr = dcr + a
            dci = dci + b
        dpow[S5_CHUNK][0] = dpow[S5_CHUNK][0] + dpr_ref[...]
        dpow[S5_CHUNK][1] = dpow[S5_CHUNK][1] + dpi_ref[...]
        dfr, dfi = _cmul_conj(br, bi, dbbr, dbbi)
        dbr, dbi = _cmul_conj(fr, fi, dbbr, dbbi)
        dnr, dni = _cmul(ar / den, ai / den, dfr, dfi)
        qr = (fr * ar + fi * ai) / den
        qi = (fi * ar - fr * ai) / den
        dlr, dli = _cmul(-qr, qi, dfr, dfi)
        dpow[1][0] = dpow[1][0] + dnr
        dpow[1][1] = dpow[1][1] + dni
        dxr, dxi = zero, zero
        for lag in range(1, N_LAGS):
            a, b = _cmul_conj(powers[lag][0], powers[lag][1], dpow[lag][0], dpow[lag][1])
            dxr = dxr + lag * a
            dxi = dxi + lag * b
        dar_ref[...] = dlr + dt * dxr
        dai_ref[...] = dli + dt * dxi
        dls_ref[...] = dt * (ar * dxr + ai * dxi)
        dbr_ref[...] = dbr
        dbi_ref[...] = dbi
        dcr_ref[...] = dcr
        dci_ref[...] = dci

    return pl.pallas_call(
        body, name="s5_param_bwd", grid=(S5_OCTETS,),
        in_specs=[OCT_TILE] * 7 + [OCT_KD, OCT_W, OCT_W, OCT_W, OCT_W, OCT_TILE, OCT_TILE], out_specs=[OCT_TILE] * 7,
        out_shape=[_sds((S5_OCTETS, LANES, LANES))] * 7,
        compiler_params=_params(("parallel",)),
    )(*tiles, dkd, dws_re, dws_im, dwo_re, dwo_im, dp_re, dp_im)


def _doubled(v):
    return jnp.concatenate([v, v], axis=-1)


def _s5_param_tiles(a_re, a_im, log_step, b_re, b_im, c_re, c_im):
    def per_group(a):
        return _doubled(jnp.broadcast_to(a.reshape(S5_OCTETS, S5_OCT, 1, SSM_STATE),
                                         (S5_OCTETS, S5_OCT, SSM_GROUP, SSM_STATE)).reshape(S5_OCTETS, LANES, SSM_STATE))

    ls = jnp.broadcast_to(log_step.reshape(S5_OCTETS, S5_OCT, 1, 1), (S5_OCTETS, S5_OCT, SSM_GROUP, LANES)).reshape(S5_OCTETS, LANES, LANES)
    bt = lambda b: _doubled(b.transpose(0, 2, 1).reshape(S5_OCTETS, LANES, SSM_STATE))
    ct = lambda c: _doubled(c.reshape(S5_OCTETS, LANES, SSM_STATE))
    return [per_group(a_re), per_group(a_im), ls, bt(b_re), bt(b_im), ct(c_re), ct(c_im)]


def _s5_param_grads(dtiles):
    dar, dai, dls, dbr, dbi, dcr, dci = dtiles
    halves = lambda d: d[..., :SSM_STATE] + d[..., SSM_STATE:]
    per_group = lambda d: halves(d).reshape(SSM_GROUPS, SSM_GROUP, SSM_STATE).sum(axis=1)
    per_row = lambda d: halves(d).reshape(SSM_GROUPS, SSM_GROUP, SSM_STATE)
    return (per_group(dar), per_group(dai), dls.reshape(SSM_GROUPS, SSM_GROUP * LANES).sum(axis=1),
            per_row(dbr).transpose(0, 2, 1), per_row(dbi).transpose(0, 2, 1), per_row(dcr), per_row(dci))


def _group_power_rows(tile):
    return tile[:, ::SSM_GROUP, :SSM_STATE].reshape(1, S5_STATES)


def _group_power_tiles(row):
    t = jnp.pad(row.reshape(S5_OCTETS, S5_OCT, 1, SSM_STATE), ((0, 0), (0, 0), (0, SSM_GROUP - 1), (0, LANES - SSM_STATE)))
    return t.reshape(S5_OCTETS, LANES, LANES)


def _rope_tables(t):
    pos = jnp.arange(t, dtype=F32)
    inv_freq = ROPE_THETA ** (-jnp.arange(0, HEAD_DIM, 2, dtype=F32) / HEAD_DIM)
    ang = pos[:, None] * inv_freq[None, :]
    cos = jnp.cos(ang)
    sin = jnp.sin(ang)
    cos64 = jnp.concatenate([cos, cos], axis=1)
    sin64 = jnp.concatenate([-sin, sin], axis=1)
    return jnp.concatenate([cos64, cos64], axis=1), jnp.concatenate([sin64, sin64], axis=1)


def _row(v):
    return v.reshape(1, -1)


def _ssm_forward(x, w, token=None):
    tiles = _s5_param_tiles(w["a_re"], w["a_im"], w["log_step"], w["b_re"], w["b_im"], w["c_re"], w["c_im"])
    kd, ws_re, ws_im, wo_re, wo_im, p_re, p_im = s5_param_fwd(tiles)
    mats = dict(kd=kd, ws_re=ws_re, ws_im=ws_im, wo_re=wo_re, wo_im=wo_im, a_re=_group_power_rows(p_re), a_im=_group_power_rows(p_im))
    u, gate = ssm_proj_fwd(x, _row(w["norm"]), w["w_in"], token)
    s_re, s_im = s5_chunk_states(u, mats["ws_re"], mats["ws_im"])
    h_re, h_im = s5_scan_fwd(s_re, s_im, mats["a_re"], mats["a_im"])
    y_scan = s5_outputs(u, h_re, h_im, mats["kd"], mats["wo_re"], mats["wo_im"])
    y, g2, x_new = ssm_mix_fwd(x, u, gate, y_scan, _row(w["d"]), w["w_glu"], _row(w["b_glu"]), w["w_out"])
    saved = dict(x=x, u=u, gate=gate, y=y, g2=g2, h_re=h_re, h_im=h_im, mats=mats, tiles=tiles)
    return x_new, saved


def _ssm_backward(dxo, w, s, token=None):
    dy, dgate, dw_out, dw_glu, db_glu, dd = ssm_mix_bwd(dxo, s["u"], s["gate"], s["y"], s["g2"], w["w_glu"], w["w_out"], token)
    mats = s["mats"]
    dh_re, dh_im = s5_state_grads(dy, mats["wo_re"], mats["wo_im"])
    ds_re, ds_im, da_re, da_im = s5_scan_bwd(dh_re, dh_im, s["h_re"], s["h_im"], mats["a_re"], mats["a_im"])
    du_scan = s5_input_grads(dy, ds_re, ds_im, mats["kd"], mats["ws_re"], mats["ws_im"])
    dkd, dws_re, dws_im, dwo_re, dwo_im = s5_weight_grads(s["u"], dy, s["h_re"], s["h_im"], ds_re, ds_im)
    dparams = _s5_param_grads(s5_param_bwd(s["tiles"], dkd, dws_re, dws_im, dwo_re, dwo_im,
                                           _group_power_tiles(da_re), _group_power_tiles(da_im)))
    dx, dw_in, dnorm = ssm_proj_bwd(s["x"], _row(w["norm"]), dxo, dy, du_scan, dgate, _row(w["d"]), w["w_in"])
    grads = dict(norm=dnorm, w_in=dw_in, d=dd, w_glu=dw_glu, b_glu=db_glu, w_out=dw_out)
    for name, val in zip(("a_re", "a_im", "log_step", "b_re", "b_im", "c_re", "c_im"), dparams):
        grads[name] = val
    return dx, grads


def _attn_forward(x, w, cos2, sin2):
    q, k, v, gate = attn_proj_fwd(x, _row(w["norm"]), w["w_in"], cos2, sin2)
    o, lse = attn_fwd(q, k, v, w["sinks"])
    x_new = attn_out_fwd(x, o, gate, w["w_out"])
    return x_new, dict(x=x, q=q, k=k, v=v, gate=gate, o=o, lse=lse)


def _attn_backward(dxo, w, s, cos2, sin2, token=None):
    do, dgate, dw_out = attn_out_bwd(dxo, s["o"], s["gate"], w["w_out"], token)
    dq, dk, dv, dsinks = attn_bwd(s["q"], s["k"], s["v"], w["sinks"], s["o"], s["lse"], do)
    dx, dw_in, dnorm = attn_proj_bwd(s["x"], _row(w["norm"]), dxo, dq, dk, dv, dgate, cos2, sin2, w["w_in"])
    return dx, dict(norm=dnorm, w_in=dw_in, sinks=dsinks[0, :N_Q_HEADS], w_out=dw_out)


class _NoExchanges:
    def __init__(self, layers):
        self.layers = layers

    def first_token(self):
        return None

    def layer(self, i, x):
        return self.layers[i]

    def layer_done(self, i, grads, dx):
        return None


def _sequence_step(x, target, final_norm, hooks, depth=4):
    cos2, sin2 = _rope_tables(x.shape[0])
    saved, layers = [], []
    for i in range(depth):
        w = hooks.layer(i, x)
        layers.append(w)
        if i % 2 == 0:
            x, s = _ssm_forward(x, w, hooks.first_token() if i == 0 else None)
        else:
            x, s = _attn_forward(x, w, cos2, sin2)
        saved.append(s)
    loss, dx, dfinal = loss_head(x, _row(final_norm), target)
    grads = {"final_norm": dfinal}
    token = None
    for i in reversed(range(depth)):
        if i % 2 == 0:
            dx, g = _ssm_backward(dx, layers[i], saved[i], token)
        else:
            dx, g = _attn_backward(dx, layers[i], saved[i], cos2, sin2, token)
        g = {"l%d_%s" % (i, name): val for name, val in g.items()}
        grads.update(g)
        token = hooks.layer_done(i, g, dx)
    return loss[0, 0], dx, grads


ANY = pl.BlockSpec(memory_space=pl.ANY)


def _place():
    return lax.axis_index("x"), lax.axis_index("y"), lax.axis_index("c")


def _other_chips(x, y):
    return [(1 - x, y), (x, 1 - y), (1 - x, 1 - y)]


class _StagedCopies:
    def __init__(self, bufs, load_sems, store_sems):
        self.bufs, self.load_sems, self.store_sems = bufs, load_sems, store_sems
        self.loads, self.stores = [], []

    def load(self, i, src):
        cp = pltpu.make_async_copy(src, self.bufs[i], self.load_sems.at[i])
        cp.start()
        self.loads.append(cp)

    def store(self, i, dst):
        self.loads[i].wait()
        cp = pltpu.make_async_copy(self.bufs[i], dst, self.store_sems.at[i])
        cp.start()
        self.stores.append(cp)

    def finish(self):
        for cp in self.stores:
            cp.wait()


def _staging(blocks):
    n = len(blocks)
    return [pltpu.VMEM(b.shape, b.dtype) for b in blocks] + [pltpu.SemaphoreType.DMA((n,)), pltpu.SemaphoreType.DMA((n,))]


def gather_weight_shards(shards):
    n = len(shards)

    def body(*refs):
        ins, outs = refs[:n], refs[n:2 * n]
        send_sems, recv_sems, pass_send_sems, pass_recv_sems = refs[2 * n:2 * n + 4]
        own = _StagedCopies(refs[2 * n + 4:3 * n + 4], *refs[3 * n + 4:])
        x, y, c = _place()
        me = 2 * x + y
        chips = _other_chips(x, y)

        def half(i, block, which):
            rows = ins[i].shape[0] // 2
            return outs[i].at[block, pl.ds(which * rows, rows), :]

        def my_half(i):
            rows = ins[i].shape[0] // 2
            return ins[i].at[pl.ds(c * rows, rows), :]

        for i in range(n):
            own.load(i, ins[i])
        sends = []
        for i in range(n):
            for k, (tx, ty) in enumerate(chips):
                cp = pltpu.make_async_remote_copy(src_ref=my_half(i), dst_ref=half(i, me, c), send_sem=send_sems.at[i, k],
                                                  recv_sem=recv_sems.at[i, k], device_id=(tx, ty, c), device_id_type=MESH)
                cp.start()
                sends.append(cp)
        for i in range(n):
            own.store(i, outs[i].at[me])
        for i in range(n):
            for k, (tx, ty) in enumerate(chips):
                landed = half(i, 2 * tx + ty, c)
                pltpu.make_async_remote_copy(src_ref=my_half(i), dst_ref=landed, send_sem=send_sems.at[i, k],
                                             recv_sem=recv_sems.at[i, k], device_id=(tx, ty, c), device_id_type=MESH).wait_recv()
                cp = pltpu.make_async_remote_copy(src_ref=landed, dst_ref=landed, send_sem=pass_send_sems.at[i, k],
                                                  recv_sem=pass_recv_sems.at[i, k], device_id=(x, y, 1 - c), device_id_type=MESH)
                cp.start()
                sends.append(cp)
        for i in range(n):
            for k, (tx, ty) in enumerate(chips):
                missing = half(i, 2 * tx + ty, 1 - c)
                pltpu.make_async_remote_copy(src_ref=missing, dst_ref=missing, send_sem=pass_send_sems.at[i, k],
                                             recv_sem=pass_recv_sems.at[i, k], device_id=(x, y, 1 - c), device_id_type=MESH).wait_recv()
        for cp in sends:
            cp.wait_send()
        own.finish()

    sems = pltpu.SemaphoreType.DMA((n, 3))
    return pl.pallas_call(
        body, name="gather_weight_shards",
        in_specs=[ANY] * n, out_specs=[ANY] * n,
        out_shape=[_sds((4,) + s.shape, s.dtype) for s in shards],
        scratch_shapes=[sems, sems, sems, sems] + _staging(shards),
        compiler_params=_params(),
    )(*shards)


def exchange_halves_with_sibling(grads):
    n = len(grads)

    def body(*refs):
        ins, outs = refs[:n], refs[n:2 * n]
        send_sems, recv_sems = refs[2 * n:]
        x, y, c = _place()
        copies = []
        for i in range(n):
            half = ins[i].shape[1] // 2
            src = ins[i].at[:, pl.ds((1 - c) * half, half), :]
            cp = pltpu.make_async_remote_copy(src_ref=src, dst_ref=outs[i], send_sem=send_sems.at[i], recv_sem=recv_sems.at[i],
                                              device_id=(x, y, 1 - c), device_id_type=MESH)
            cp.start()
            copies.append(cp)
        for cp in copies:
            cp.wait()

    return pl.pallas_call(
        body, name="exchange_halves_with_sibling",
        in_specs=[ANY] * n, out_specs=[ANY] * n,
        out_shape=[_sds((g.shape[0], g.shape[1] // 2, g.shape[2])) for g in grads],
        scratch_shapes=[pltpu.SemaphoreType.DMA((n,)), pltpu.SemaphoreType.DMA((n,))],
    )(*grads)


def scatter_blocks_to_chips(sums):
    n = len(sums)

    def body(*refs):
        ins, outs = refs[:n], refs[n:2 * n]
        send_sems, recv_sems = refs[2 * n:2 * n + 2]
        own = _StagedCopies(refs[2 * n + 2:3 * n + 2], *refs[3 * n + 2:])
        x, y, c = _place()
        me = 2 * x + y

        def block_for(i, chip):
            return ins[i].at[chip] if ins[i].shape[0] == 4 else ins[i].at[0]

        for i in range(n):
            own.load(i, block_for(i, me))
        sends = []
        for i in range(n):
            for k, (tx, ty) in enumerate(_other_chips(x, y)):
                cp = pltpu.make_async_remote_copy(src_ref=block_for(i, 2 * tx + ty), dst_ref=outs[i].at[me], send_sem=send_sems.at[i, k],
                                                  recv_sem=recv_sems.at[i, k], device_id=(tx, ty, c), device_id_type=MESH)
                cp.start()
                sends.append(cp)
        for i in range(n):
            own.store(i, outs[i].at[me])
        for i in range(n):
            for k, (tx, ty) in enumerate(_other_chips(x, y)):
                pltpu.make_async_remote_copy(src_ref=block_for(i, me), dst_ref=outs[i].at[2 * tx + ty], send_sem=send_sems.at[i, k],
                                             recv_sem=recv_sems.at[i, k], device_id=(tx, ty, c), device_id_type=MESH).wait_recv()
        for cp in sends:
            cp.wait_send()
        own.finish()

    return pl.pallas_call(
        body, name="scatter_blocks_to_chips",
        in_specs=[ANY] * n, out_specs=[ANY] * n,
        out_shape=[_sds((4,) + s.shape[1:], s.dtype) for s in sums],
        scratch_shapes=[pltpu.SemaphoreType.DMA((n, 3)), pltpu.SemaphoreType.DMA((n, 3))] + _staging([_sds(s.shape[1:], s.dtype) for s in sums]),
        compiler_params=_params(),
    )(*sums)


def swap_halves_with_sibling(pieces):
    n = len(pieces)

    def body(*refs):
        ins, outs = refs[:n], refs[n:2 * n]
        send_sems, recv_sems = refs[2 * n:2 * n + 2]
        own = _StagedCopies(refs[2 * n + 2:3 * n + 2], *refs[3 * n + 2:])
        x, y, c = _place()
        for i in range(n):
            own.load(i, ins[i])
        swaps = []
        for i in range(n):
            cp = pltpu.make_async_remote_copy(src_ref=ins[i], dst_ref=outs[i].at[c], send_sem=send_sems.at[i], recv_sem=recv_sems.at[i],
                                              device_id=(x, y, 1 - c), device_id_type=MESH)
            cp.start()
            swaps.append(cp)
        for i in range(n):
            own.store(i, outs[i].at[c])
        for i in range(n):
            pltpu.make_async_remote_copy(src_ref=ins[i], dst_ref=outs[i].at[1 - c], send_sem=send_sems.at[i], recv_sem=recv_sems.at[i],
                                         device_id=(x, y, 1 - c), device_id_type=MESH).wait_recv()
        for cp in swaps:
            cp.wait_send()
        own.finish()

    return pl.pallas_call(
        body, name="swap_halves_with_sibling",
        in_specs=[ANY] * n, out_specs=[ANY] * n,
        out_shape=[_sds((2,) + p.shape) for p in pieces],
        scratch_shapes=[pltpu.SemaphoreType.DMA((n,)), pltpu.SemaphoreType.DMA((n,))] + _staging(pieces),
        compiler_params=_params(),
    )(*pieces)


IN_HBM = pl.BlockSpec(memory_space=pltpu.HBM)
SEMAPHORES = pl.BlockSpec(memory_space=pltpu.SEMAPHORE)
DATAFLOW = pltpu.SideEffectType.DATAFLOW_SIDE_EFFECTING


def _hbm(a):
    return pltpu.with_memory_space_constraint(a, pltpu.HBM)


def place_own_blocks(shards):
    n = len(shards)

    def body(*refs):
        ins, outs = refs[:n], refs[n:2 * n]
        own = _StagedCopies(refs[2 * n:3 * n], *refs[3 * n:])
        x, y, _ = _place()
        for i in range(n):
            own.load(i, ins[i])
        for i in range(n):
            own.store(i, outs[i].at[2 * x + y])
        own.finish()

    return pl.pallas_call(
        body, name="place_own_blocks", in_specs=[ANY] * n, out_specs=[ANY] * n,
        out_shape=[_sds((4,) + s.shape, s.dtype) for s in shards],
        scratch_shapes=_staging(shards), compiler_params=_params(),
    )(*shards)


def start_sends_to_chips(name, sources, landings, per_target, after):
    n = len(sources)
    n_sems = 2 * 3 * n

    def body(*refs):
        srcs = refs[:n]
        sems = refs[2 * n + 1:2 * n + 1 + n_sems]
        lands = refs[2 * n + 1 + n_sems:3 * n + 1 + n_sems]
        token = refs[3 * n + 1 + n_sems]
        x, y, c = _place()
        me = 2 * x + y
        for i in range(n):
            for k, (tx, ty) in enumerate(_other_chips(x, y)):
                src = srcs[i].at[2 * tx + ty] if per_target else srcs[i]
                pltpu.make_async_remote_copy(src_ref=src, dst_ref=lands[i].at[me], send_sem=sems[2 * (3 * i + k)], recv_sem=sems[2 * (3 * i + k) + 1],
                                             device_id=(tx, ty, c), device_id_type=MESH).start()
        token[...] = jnp.zeros_like(token)

    outs = pl.pallas_call(
        body, name=name,
        in_specs=[IN_HBM] * (2 * n) + [ANY],
        out_specs=[SEMAPHORES] * n_sems + [IN_HBM] * n + [pl.BlockSpec(memory_space=pltpu.VMEM)],
        out_shape=[pltpu.SemaphoreType.DMA(())] * n_sems + [pltpu.HBM(l.shape, l.dtype) for l in landings] + [_sds(TOKEN_SHAPE)],
        input_output_aliases={n + i: n_sems + i for i in range(n)},
        compiler_params=pltpu.CompilerParams(has_side_effects=DATAFLOW),
    )(*[_hbm(s) for s in sources], *[_hbm(l) for l in landings], after)
    return list(outs[:n_sems]), list(outs[n_sems:n_sems + n]), outs[n_sems + n]


def wait_sends_to_chips(name, sources, landings, sems, per_target, after):
    n = len(sources)
    n_sems = len(sems)

    def body(*refs):
        srcs = refs[:n]
        sem_refs = refs[2 * n:2 * n + n_sems]
        lands = refs[2 * n + n_sems + 1:]
        x, y, c = _place()
        me = 2 * x + y
        for i in range(n):
            for k, (tx, ty) in enumerate(_other_chips(x, y)):
                src = srcs[i].at[me] if per_target else srcs[i]
                cp = pltpu.make_async_remote_copy(src_ref=src, dst_ref=lands[i].at[2 * tx + ty], send_sem=sem_refs[2 * (3 * i + k)],
                                                  recv_sem=sem_refs[2 * (3 * i + k) + 1], device_id=(tx, ty, c), device_id_type=MESH)
                cp.wait_send()
                cp.wait_recv()

    return pl.pallas_call(
        body, name=name,
        in_specs=[IN_HBM] * (2 * n) + [SEMAPHORES] * n_sems + [ANY],
        out_specs=[IN_HBM] * n,
        out_shape=[pltpu.HBM(l.shape, l.dtype) for l in landings],
        input_output_aliases={n + i: i for i in range(n)},
        compiler_params=pltpu.CompilerParams(has_side_effects=DATAFLOW),
    )(*[_hbm(s) for s in sources], *landings, *sems, after)


def _row_tile(rows, cols):
    tm = rows
    while tm * cols * 4 > (2 << 20) and tm % 16 == 0:
        tm //= 2
    return tm


def add_pair(a, b, out_dtype, copies=1):
    nb, rows, cols = a.shape
    tm = _row_tile(rows, cols)

    def body(a_ref, b_ref, *o_refs):
        total = (a_ref[...] + b_ref[...]).astype(out_dtype)
        for o_ref in o_refs:
            o_ref[...] = total

    spec = pl.BlockSpec((None, tm, cols), lambda j, i: (j, i, 0))
    outs = pl.pallas_call(
        body, name="add_pair", grid=(nb, rows // tm), in_specs=[spec, spec], out_specs=[spec] * copies,
        out_shape=[_sds(a.shape, out_dtype)] * copies, compiler_params=_params(("parallel", "parallel")),
    )(a, b)
    return outs[0] if copies == 1 else outs


def sum_four(a):
    _, rows, cols = a.shape
    tm = _row_tile(rows, cols)

    def body(a_ref, o_ref):
        o_ref[...] = ((a_ref[0].astype(F32) + a_ref[1].astype(F32)) + a_ref[2].astype(F32)) + a_ref[3].astype(F32)

    return pl.pallas_call(
        body, name="sum_four", grid=(rows // tm,),
        in_specs=[pl.BlockSpec((4, tm, cols), lambda i: (0, i, 0))], out_specs=pl.BlockSpec((tm, cols), lambda i: (i, 0)),
        out_shape=_sds((rows, cols)), compiler_params=_params(("parallel",)),
    )(a)


def adamw(w, g, m, v):
    rows, cols = w.shape
    tm = _row_tile(rows, cols)
    c1 = 1.0 - ADAM_B1 ** ADAM_STEP
    c2 = 1.0 - ADAM_B2 ** ADAM_STEP

    def body(w_ref, g_ref, m_ref, v_ref, d_ref, nm_ref, nv_ref):
        g = g_ref[...]
        nm = ADAM_B1 * m_ref[...] + (1.0 - ADAM_B1) * g
        nv = ADAM_B2 * v_ref[...] + (1.0 - ADAM_B2) * (g * g)
        d_ref[...] = -ADAM_LR * ((nm / c1) / (jnp.sqrt(nv / c2) + ADAM_EPS) + ADAM_WD * w_ref[...])
        nm_ref[...] = nm
        nv_ref[...] = nv

    spec = pl.BlockSpec((tm, cols), lambda i: (i, 0))
    return pl.pallas_call(
        body, name="adamw", grid=(rows // tm,), in_specs=[spec] * 4, out_specs=[spec] * 3,
        out_shape=[_sds(w.shape)] * 3, compiler_params=_params(("parallel",)),
    )(w, g, m, v)


PACK_TILE = 8 * LANES
PACK_PIECES = 8
PACK_ALIGN = PACK_PIECES * 16


def _pack_small(values, scalar=None):
    parts = []
    for name in SMALL_NAMES:
        flat = values[name].reshape(-1)
        pad = (-flat.shape[0]) % PACK_TILE
        if pad:
            flat = jnp.concatenate([flat, jnp.zeros((pad,), F32)])
        parts.append(flat.reshape(-1, LANES))
    rows = sum(p.shape[0] for p in parts) + 8
    parts.append(jnp.zeros(((-rows) % PACK_ALIGN, LANES), F32))
    last = jnp.zeros((8, LANES), F32)
    parts.append(last if scalar is None else jnp.broadcast_to(scalar.astype(F32), (8, LANES)))
    return jnp.concatenate(parts, axis=0)


def _unpack_small(pack, like):
    out = {}
    row = 0
    for name in SMALL_NAMES:
        size = math.prod(like[name].shape)
        rows = -(-size // PACK_TILE) * 8
        out[name] = pack[row:row + rows].reshape(-1)[:size].reshape(like[name].shape)
        row += rows
    return out


def _is_column_sharded(name):
    return name.endswith("w_in")


def _to_blocks(name, full):
    if _is_column_sharded(name):
        rows, cols = full.shape
        return full.reshape(rows, 4, cols // 4).transpose(1, 0, 2)
    return full.reshape(4, full.shape[0] // 4, full.shape[1])


def _from_blocks(name, stacked):
    if _is_column_sharded(name):
        return stacked.transpose(1, 0, 2).reshape(stacked.shape[1], 4 * stacked.shape[2])
    return stacked.reshape(4 * stacked.shape[1], stacked.shape[2])


def _layer_big_names(i):
    return [n for n in BIG_NAMES if n.startswith("l%d_" % i)]


class _OverlappedExchanges:
    def __init__(self, weights):
        self.weights = weights
        self.c = lax.axis_index("c")
        first = _layer_big_names(0)
        self.later = [n for n in BIG_NAMES if n not in first]
        gathered = gather_weight_shards([weights[n].astype(MXU_DTYPE) for n in first])
        self.full = {n: _from_blocks(n, g) for n, g in zip(first, gathered)}
        shards = [weights[n].astype(MXU_DTYPE) for n in self.later]
        self.gather = (shards,) + start_sends_to_chips("gather_later_start", shards, place_own_blocks(shards), False, gathered[0])
        self.in_flight = {}
        self.contributions = {}

    def first_token(self):
        return self.gather[3]

    def layer(self, i, x):
        if i == 1:
            shards, sems, stacks, _ = self.gather
            stacks = wait_sends_to_chips("gather_later_wait", shards, stacks, sems, False, x)
            self.full.update({n: _from_blocks(n, g) for n, g in zip(self.later, stacks)})
        names = SSM_NAMES if i % 2 == 0 else ATTN_NAMES
        return {n: self.full.get("l%d_%s" % (i, n), self.weights.get("l%d_%s" % (i, n))) for n in names}

    def chip_sums(self, names, grads, extra_blocks=(), copies=1):
        blocks = [_to_blocks(n, grads[n]) for n in names] + list(extra_blocks)
        from_sibling = exchange_halves_with_sibling(blocks)
        sums = []
        for i, (b, r) in enumerate(zip(blocks, from_sibling)):
            half = b.shape[1] // 2
            mine = lax.dynamic_slice_in_dim(b, self.c * half, half, axis=1)
            sums.append(add_pair(mine, r, WIRE_DTYPE if i < len(names) else F32, copies))
        return sums

    def layer_done(self, i, grads, dx):
        if i + 1 in self.in_flight:
            names, sums, sems, landings = self.in_flight.pop(i + 1)
            done = wait_sends_to_chips("scatter_wait_l%d" % (i + 1), sums, landings, sems, True, dx)
            self.contributions.update(zip(names, done))
        if i == 0:
            return None
        names = _layer_big_names(i)
        pairs = self.chip_sums(names, grads, copies=2)
        sums = [p[0] for p in pairs]
        sems, landings, token = start_sends_to_chips("scatter_start_l%d" % i, sums, [p[1] for p in pairs], True, sums[0])
        self.in_flight[i] = (names, sums, sems, landings)
        return token


def _train_step(x, loss_target, weights, moments_m, moments_v):
    hooks = _OverlappedExchanges(weights)
    loss, dx, grads = _sequence_step(x[0], loss_target[0], weights["final_norm"], hooks)
    small_pack = _pack_small({n: grads[n] for n in SMALL_NAMES}, scalar=loss)
    last = _layer_big_names(0)
    arrived = scatter_blocks_to_chips(hooks.chip_sums(last, grads, extra_blocks=[small_pack[None]]))
    hooks.contributions.update(zip(last + ["small"], arrived))
    reduced = [sum_four(hooks.contributions[n]) for n in BIG_NAMES + ["small"]]
    shared = swap_halves_with_sibling(reduced)
    big_grads = {n: s.reshape(2 * s.shape[1], s.shape[2]) for n, s in zip(BIG_NAMES, shared[:-1])}
    small_grad_pack = shared[-1].reshape(-1, LANES)
    loss = small_grad_pack[-8, 0]
    out_grad, out_delta, out_m, out_v = {}, {}, {}, {}
    for n in BIG_NAMES:
        out_grad[n] = big_grads[n]
        out_delta[n], out_m[n], out_v[n] = adamw(weights[n], big_grads[n], moments_m[n], moments_v[n])
    small_like = {n: weights[n] for n in SMALL_NAMES}
    d_pack, m_pack, v_pack = adamw(_pack_small(small_like), small_grad_pack, _pack_small({n: moments_m[n] for n in SMALL_NAMES}),
                                   _pack_small({n: moments_v[n] for n in SMALL_NAMES}))
    out_grad.update(_unpack_small(small_grad_pack, small_like))
    out_delta.update(_unpack_small(d_pack, small_like))
    out_m.update(_unpack_small(m_pack, small_like))
    out_v.update(_unpack_small(v_pack, small_like))
    outs = [loss, dx[None]]
    for group in (out_grad, out_delta, out_m, out_v):
        outs.extend(group[n] for n in WEIGHT_NAMES)
    return tuple(outs)


def kernel(x, l0_norm, l0_w_in, l0_a_re, l0_a_im, l0_log_step, l0_b_re, l0_b_im, l0_c_re, l0_c_im, l0_d, l0_w_glu, l0_b_glu, l0_w_out, l1_norm, l1_w_in, l1_sinks, l1_w_out, l2_norm, l2_w_in, l2_a_re, l2_a_im, l2_log_step, l2_b_re, l2_b_im, l2_c_re, l2_c_im, l2_d, l2_w_glu, l2_b_glu, l2_w_out, l3_norm, l3_w_in, l3_sinks, l3_w_out, final_norm, loss_target, m_l0_norm, m_l0_w_in, m_l0_a_re, m_l0_a_im, m_l0_log_step, m_l0_b_re, m_l0_b_im, m_l0_c_re, m_l0_c_im, m_l0_d, m_l0_w_glu, m_l0_b_glu, m_l0_w_out, m_l1_norm, m_l1_w_in, m_l1_sinks, m_l1_w_out, m_l2_norm, m_l2_w_in, m_l2_a_re, m_l2_a_im, m_l2_log_step, m_l2_b_re, m_l2_b_im, m_l2_c_re, m_l2_c_im, m_l2_d, m_l2_w_glu, m_l2_b_glu, m_l2_w_out, m_l3_norm, m_l3_w_in, m_l3_sinks, m_l3_w_out, m_final_norm, v_l0_norm, v_l0_w_in, v_l0_a_re, v_l0_a_im, v_l0_log_step, v_l0_b_re, v_l0_b_im, v_l0_c_re, v_l0_c_im, v_l0_d, v_l0_w_glu, v_l0_b_glu, v_l0_w_out, v_l1_norm, v_l1_w_in, v_l1_sinks, v_l1_w_out, v_l2_norm, v_l2_w_in, v_l2_a_re, v_l2_a_im, v_l2_log_step, v_l2_b_re, v_l2_b_im, v_l2_c_re, v_l2_c_im, v_l2_d, v_l2_w_glu, v_l2_b_glu, v_l2_w_out, v_l3_norm, v_l3_w_in, v_l3_sinks, v_l3_w_out, v_final_norm):
    args = locals()
    weights = {n: args[n] for n in WEIGHT_NAMES}
    moments_m = {n: args["m_" + n] for n in WEIGHT_NAMES}
    moments_v = {n: args["v_" + n] for n in WEIGHT_NAMES}
    return _train_step(x, loss_target, weights, moments_m, moments_v)
```

```python
import functools
import math

import jax
import jax.numpy as jnp
from jax import lax
from jax.experimental import pallas as pl
from jax.experimental.pallas import tpu as pltpu

F32 = jnp.float32
MXU_DTYPE = jnp.bfloat16
WIRE_DTYPE = jnp.bfloat16
MESH = pl.DeviceIdType.MESH

D_MODEL = 1024
BRANCH = 1024
NORM_EPS = 1e-5
SSM_GROUPS = 64
SSM_GROUP = 16
SSM_STATE = 64
S5_CHUNK = 16
LANES = 128
S5_OCT = LANES // SSM_GROUP
S5_OCTETS = SSM_GROUPS // S5_OCT
S5_OCT_IN = S5_CHUNK * LANES
S5_OCT_STATE = S5_OCT * SSM_STATE
S5_STATES = SSM_GROUPS * SSM_STATE
HEAD_DIM = 64
N_Q_HEADS = 16
N_KV_HEADS = 2
GQA_GROUP = N_Q_HEADS // N_KV_HEADS
ATTN_BLOCK = 128
Q_DIM = N_Q_HEADS * HEAD_DIM
KV_DIM = N_KV_HEADS * HEAD_DIM
ROPE_THETA = 10000.0
NEG_INF = -1e30
ADAM_LR = 0.001
ADAM_B1 = 0.9
ADAM_B2 = 0.999
ADAM_EPS = 1e-08
ADAM_WD = 0.01
ADAM_STEP = 10

VMEM_LIMIT_V7X = 56 * 1024 * 1024
ROW_TILE_FWD = 512
ROW_TILE_BWD = 512

SSM_NAMES = ("norm", "w_in", "a_re", "a_im", "log_step", "b_re", "b_im", "c_re", "c_im", "d", "w_glu", "b_glu", "w_out")
ATTN_NAMES = ("norm", "w_in", "sinks", "w_out")


def _weight_names():
    names = []
    for i in range(4):
        for n in (SSM_NAMES if i % 2 == 0 else ATTN_NAMES):
            names.append("l%d_%s" % (i, n))
    names.append("final_norm")
    return names


WEIGHT_NAMES = _weight_names()
BIG_NAMES = [n for n in WEIGHT_NAMES if n.endswith(("w_in", "w_glu", "w_out"))]
SMALL_NAMES = [n for n in WEIGHT_NAMES if n not in BIG_NAMES]


def _params(semantics=None):
    return pltpu.CompilerParams(dimension_semantics=semantics, vmem_limit_bytes=VMEM_LIMIT_V7X)


def _rows(tm, n):
    return pl.BlockSpec((tm, n), lambda i: (i, 0))


def _whole(shape):
    return pl.BlockSpec(shape, lambda i: (0,) * len(shape), pipeline_mode=pl.Buffered(1))


def _sds(shape, dtype=F32):
    return jax.ShapeDtypeStruct(shape, dtype)


def _mm(a, b):
    return jnp.dot(a.astype(MXU_DTYPE), b.astype(MXU_DTYPE), preferred_element_type=F32)


def _mm_tn(a, b):
    return lax.dot_general(a.astype(MXU_DTYPE), b.astype(MXU_DTYPE), (((0,), (0,)), ((), ())), preferred_element_type=F32)


def _mm_nt(a, b):
    return lax.dot_general(a.astype(MXU_DTYPE), b.astype(MXU_DTYPE), (((1,), (1,)), ((), ())), preferred_element_type=F32)


def _sigmoid(x):
    return 0.5 + 0.5 * jnp.tanh(0.5 * x)


def _silu(x):
    return x * _sigmoid(x)


def _silu_and_grad(x):
    s = _sigmoid(x)
    return x * s, s * (1.0 + x * (1.0 - s))


GELU_C0 = math.sqrt(2.0 / math.pi)
GELU_C1 = 0.044715


def _gelu(x):
    return 0.5 * x * (1.0 + jnp.tanh(GELU_C0 * (x + GELU_C1 * x * x * x)))


def _gelu_and_grad(x):
    x2 = x * x
    th = jnp.tanh(GELU_C0 * x * (1.0 + GELU_C1 * x2))
    half = 0.5 + 0.5 * th
    return x * half, half + 0.5 * x * (1.0 - th * th) * (GELU_C0 + 3.0 * GELU_C0 * GELU_C1 * x2)


def _rms(x, g):
    r = lax.rsqrt(jnp.mean(x * x, axis=-1, keepdims=True) + NORM_EPS)
    xhat = x * r
    return r, xhat, xhat * g


def _rms_bwd(dh, g, r, xhat):
    dxhat = dh * g
    dx = r * (dxhat - xhat * jnp.mean(dxhat * xhat, axis=-1, keepdims=True))
    return dx, jnp.sum(dh * xhat, axis=0, keepdims=True)


def _swap_half_heads(x):
    n = x.shape[-1]
    lane = lax.broadcasted_iota(jnp.int32, x.shape, x.ndim - 1)
    first = (lane % HEAD_DIM) < (HEAD_DIM // 2)
    return jnp.where(first, pltpu.roll(x, n - HEAD_DIM // 2, x.ndim - 1), pltpu.roll(x, HEAD_DIM // 2, x.ndim - 1))


def _tile_lanes(t, reps):
    return jnp.concatenate([t] * reps, axis=1)


TOKEN_SHAPE = (8, LANES)


def _after(token):
    return ([], []) if token is None else ([token], [_whole(TOKEN_SHAPE)])


def ssm_proj_fwd(x, norm, w_in, token=None):
    t = x.shape[0]
    tm = min(ROW_TILE_FWD, t)
    extra, extra_specs = _after(token)

    def body(x_ref, g_ref, w_ref, *rest):
        u_ref, gate_ref = rest[-2:]
        _, _, h = _rms(x_ref[...], g_ref[...])
        p = _mm(h, w_ref[...])
        u_ref[...] = p[:, :BRANCH]
        gate_ref[...] = p[:, BRANCH:]

    return pl.pallas_call(
        body, name="ssm_proj_fwd", grid=(t // tm,),
        in_specs=[_rows(tm, D_MODEL), _whole((1, D_MODEL)), _whole((D_MODEL, 2 * BRANCH))] + extra_specs,
        out_specs=[_rows(tm, BRANCH), _rows(tm, BRANCH)],
        out_shape=[_sds((t, BRANCH)), _sds((t, BRANCH))],
        compiler_params=_params(("parallel",)),
    )(x, norm, w_in, *extra)


def _chunk_rows(ref, nk, dtype=None):
    rows = jnp.concatenate([ref[pl.ds(s, nk, stride=S5_CHUNK), :] for s in range(S5_CHUNK)], axis=1)
    return rows.astype(MXU_DTYPE if dtype is None else dtype)


def _store_chunk_rows(ref, val, nk):
    for s in range(S5_CHUNK):
        ref[pl.ds(s, nk, stride=S5_CHUNK), :] = val[:, s * LANES:(s + 1) * LANES]


def _own_group_mask():
    row = lax.broadcasted_iota(jnp.int32, (S5_OCT_IN, S5_OCT_STATE), 0)
    col = lax.broadcasted_iota(jnp.int32, (S5_OCT_IN, S5_OCT_STATE), 1)
    return ((row % LANES) // SSM_GROUP) == (col // SSM_STATE)


def _spread_groups(w):
    return jnp.where(_own_group_mask(), jnp.concatenate([w] * (S5_OCT_STATE // LANES), axis=1), 0.0).astype(MXU_DTYPE)


def _fold_groups(p):
    p = jnp.where(_own_group_mask(), p, 0.0)
    return sum(p[:, q * LANES:(q + 1) * LANES] for q in range(S5_OCT_STATE // LANES))


def _fill_toeplitz(win_ref, kd_ref):
    win_ref[...] = jnp.zeros_like(win_ref)
    for s in range(S5_CHUNK):
        for t in range(s, S5_CHUNK):
            win_ref[s * LANES:(s + 1) * LANES, t * LANES:(t + 1) * LANES] = kd_ref[t - s].astype(MXU_DTYPE)


TOEPLITZ_BLOCK = 512
_TOEPLITZ_BLOCKS = [(lo, lo + TOEPLITZ_BLOCK) for lo in range(0, S5_OCT_IN, TOEPLITZ_BLOCK)]


def _strip(t):
    return pl.BlockSpec((t, LANES), lambda b: (0, b))


def _oct_states(nk):
    return pl.BlockSpec((nk, S5_OCT_STATE), lambda b: (0, b))


OCT_W = pl.BlockSpec((None, S5_OCT_IN, LANES), lambda b: (b, 0, 0))
OCT_KD = pl.BlockSpec((None, S5_CHUNK, LANES, LANES), lambda b: (b, 0, 0, 0))


def s5_chunk_states(u, ws_re, ws_im):
    t = u.shape[0]
    nk = t // S5_CHUNK

    def body(u_ref, wr_ref, wi_ref, re_ref, im_ref):
        uc = _chunk_rows(u_ref, nk)
        re_ref[...] = _mm(uc, _spread_groups(wr_ref[...]))
        im_ref[...] = _mm(uc, _spread_groups(wi_ref[...]))

    return pl.pallas_call(
        body, name="s5_chunk_states", grid=(S5_OCTETS,),
        in_specs=[_strip(t), OCT_W, OCT_W], out_specs=[_oct_states(nk), _oct_states(nk)],
        out_shape=[_sds((nk, S5_STATES)), _sds((nk, S5_STATES))],
        compiler_params=_params(("parallel",)),
    )(u, ws_re, ws_im)


def s5_scan_fwd(s_re, s_im, a_re, a_im):
    nk = s_re.shape[0]

    def body(sre_ref, sim_ref, ar_ref, ai_ref, hre_ref, him_ref):
        ar = ar_ref[...]
        ai = ai_ref[...]

        def step(k, carry):
            hr, hi = carry
            hre_ref[pl.ds(k, 1), :] = hr
            him_ref[pl.ds(k, 1), :] = hi
            sr = sre_ref[pl.ds(k, 1), :]
            si = sim_ref[pl.ds(k, 1), :]
            return ar * hr - ai * hi + sr, ai * hr + ar * hi + si

        zero = jnp.zeros((1, S5_STATES), F32)
        lax.fori_loop(0, nk, step, (zero, zero))

    vm = pl.BlockSpec(memory_space=pltpu.VMEM)
    return pl.pallas_call(
        body, name="s5_scan_fwd", in_specs=[vm, vm, vm, vm], out_specs=[vm, vm],
        out_shape=[_sds((nk, S5_STATES)), _sds((nk, S5_STATES))],
        compiler_params=_params(),
    )(s_re, s_im, a_re, a_im)


def s5_outputs(u, h_re, h_im, kd, wo_re, wo_im):
    t = u.shape[0]
    nk = t // S5_CHUNK

    def body(u_ref, hre_ref, him_ref, kd_ref, wor_ref, woi_ref, y_ref, win_ref):
        _fill_toeplitz(win_ref, kd_ref)
        uc = _chunk_rows(u_ref, nk)
        y = jnp.concatenate([_mm(uc[:, :hi], win_ref[:hi, lo:hi]) for lo, hi in _TOEPLITZ_BLOCKS], axis=1)
        y = y + _mm_nt(hre_ref[...], _spread_groups(wor_ref[...])) + _mm_nt(him_ref[...], _spread_groups(woi_ref[...]))
        _store_chunk_rows(y_ref, y, nk)

    return pl.pallas_call(
        body, name="s5_outputs", grid=(S5_OCTETS,),
        in_specs=[_strip(t), _oct_states(nk), _oct_states(nk), OCT_KD, OCT_W, OCT_W],
        out_specs=_strip(t), out_shape=_sds((t, BRANCH)),
        scratch_shapes=[pltpu.VMEM((S5_OCT_IN, S5_OCT_IN), MXU_DTYPE)],
        compiler_params=_params(("parallel",)),
    )(u, h_re, h_im, kd, wo_re, wo_im)


def s5_state_grads(dy, wo_re, wo_im):
    t = dy.shape[0]
    nk = t // S5_CHUNK

    def body(dy_ref, wor_ref, woi_ref, re_ref, im_ref):
        dyc = _chunk_rows(dy_ref, nk)
        re_ref[...] = _mm(dyc, _spread_groups(wor_ref[...]))
        im_ref[...] = _mm(dyc, _spread_groups(woi_ref[...]))

    return pl.pallas_call(
        body, name="s5_state_grads", grid=(S5_OCTETS,),
        in_specs=[_strip(t), OCT_W, OCT_W], out_specs=[_oct_states(nk), _oct_states(nk)],
        out_shape=[_sds((nk, S5_STATES)), _sds((nk, S5_STATES))],
        compiler_params=_params(("parallel",)),
    )(dy, wo_re, wo_im)


def s5_scan_bwd(dh_re, dh_im, h_re, h_im, a_re, a_im):
    nk = dh_re.shape[0]

    def body(dhr_ref, dhi_ref, hr_ref, hi_ref, ar_ref, ai_ref, dsr_ref, dsi_ref, dar_ref, dai_ref):
        ar = ar_ref[...]
        ai = ai_ref[...]

        dar_ref[...] = jnp.zeros_like(dar_ref)
        dai_ref[...] = jnp.zeros_like(dai_ref)

        def step(i, carry):
            gr, gi = carry
            k = nk - 1 - i
            dhr = dhr_ref[pl.ds(k, 1), :]
            dhi = dhi_ref[pl.ds(k, 1), :]
            dsr_ref[pl.ds(k, 1), :] = gr
            dsi_ref[pl.ds(k, 1), :] = gi
            hr = hr_ref[pl.ds(k, 1), :]
            hi = hi_ref[pl.ds(k, 1), :]
            dar_ref[...] += gr * hr + gi * hi
            dai_ref[...] += gi * hr - gr * hi
            return dhr + ar * gr + ai * gi, dhi - ai * gr + ar * gi

        zero = jnp.zeros((1, S5_STATES), F32)
        lax.fori_loop(0, nk, step, (zero, zero))

    vm = pl.BlockSpec(memory_space=pltpu.VMEM)
    return pl.pallas_call(
        body, name="s5_scan_bwd", in_specs=[vm] * 6, out_specs=[vm] * 4,
        out_shape=[_sds((nk, S5_STATES)), _sds((nk, S5_STATES)), _sds((1, S5_STATES)), _sds((1, S5_STATES))],
        input_output_aliases={0: 0, 1: 1}, compiler_params=_params(),
    )(dh_re, dh_im, h_re, h_im, a_re, a_im)


def s5_input_grads(dy, ds_re, ds_im, kd, ws_re, ws_im):
    t = dy.shape[0]
    nk = t // S5_CHUNK

    def body(dy_ref, dsr_ref, dsi_ref, kd_ref, wsr_ref, wsi_ref, du_ref, win_ref):
        _fill_toeplitz(win_ref, kd_ref)
        dyc = _chunk_rows(dy_ref, nk)
        du = jnp.concatenate([_mm_nt(dyc[:, lo:], win_ref[lo:hi, lo:]) for lo, hi in _TOEPLITZ_BLOCKS], axis=1)
        du = du + _mm_nt(dsr_ref[...], _spread_groups(wsr_ref[...])) + _mm_nt(dsi_ref[...], _spread_groups(wsi_ref[...]))
        _store_chunk_rows(du_ref, du, nk)

    return pl.pallas_call(
        body, name="s5_input_grads", grid=(S5_OCTETS,),
        in_specs=[_strip(t), _oct_states(nk), _oct_states(nk), OCT_KD, OCT_W, OCT_W],
        out_specs=_strip(t), out_shape=_sds((t, BRANCH)),
        scratch_shapes=[pltpu.VMEM((S5_OCT_IN, S5_OCT_IN), MXU_DTYPE)],
        compiler_params=_params(("parallel",)),
    )(dy, ds_re, ds_im, kd, ws_re, ws_im)


def s5_weight_grads(u, dy, h_re, h_im, ds_re, ds_im):
    t = u.shape[0]
    nk = t // S5_CHUNK

    def body(u_ref, dy_ref, hre_ref, him_ref, dsr_ref, dsi_ref, dkd_ref, dwsr_ref, dwsi_ref, dwor_ref, dwoi_ref):
        dyc = _chunk_rows(dy_ref, nk, F32)
        uct = _chunk_rows(u_ref, nk, F32).T.astype(MXU_DTYPE)
        dyct = dyc.T.astype(MXU_DTYPE)
        dyc = dyc.astype(MXU_DTYPE)
        dwsr_ref[...] = _fold_groups(_mm(uct, dsr_ref[...]))
        dwsi_ref[...] = _fold_groups(_mm(uct, dsi_ref[...]))
        dwor_ref[...] = _fold_groups(_mm(dyct, hre_ref[...]))
        dwoi_ref[...] = _fold_groups(_mm(dyct, him_ref[...]))
        dkd_ref[...] = jnp.zeros_like(dkd_ref)
        for tt in range(0, S5_CHUNK, 2):
            p = _mm(uct[:(tt + 2) * LANES], dyc[:, tt * LANES:(tt + 2) * LANES])
            for s in range(tt + 2):
                rows = p[s * LANES:(s + 1) * LANES]
                if s <= tt:
                    dkd_ref[tt - s] += rows[:, :LANES]
                dkd_ref[tt + 1 - s] += rows[:, LANES:]

    return pl.pallas_call(
        body, name="s5_weight_grads", grid=(S5_OCTETS,),
        in_specs=[_strip(t), _strip(t)] + [_oct_states(nk)] * 4,
        out_specs=[OCT_KD, OCT_W, OCT_W, OCT_W, OCT_W],
        out_shape=[_sds((S5_OCTETS, S5_CHUNK, LANES, LANES))] + [_sds((S5_OCTETS, S5_OCT_IN, LANES))] * 4,
        compiler_params=_params(("parallel",)),
    )(u, dy, h_re, h_im, ds_re, ds_im)


def ssm_mix_fwd(x, u, gate, y_scan, d, w_glu, b_glu, w_out):
    t = x.shape[0]
    tm = min(ROW_TILE_FWD, t)

    def body(x_ref, u_ref, gate_ref, ys_ref, d_ref, wg_ref, bg_ref, wo_ref, y_ref, g2_ref, xo_ref):
        y = ys_ref[...] + d_ref[...] * u_ref[...]
        z0 = _gelu(y)
        g2 = _mm(z0, wg_ref[...]) + bg_ref[...]
        a = z0 * _sigmoid(g2) * _silu(gate_ref[...])
        y_ref[...] = y
        g2_ref[...] = g2
        xo_ref[...] = x_ref[...] + _mm(a, wo_ref[...])

    row = _rows(tm, BRANCH)
    vec = _whole((1, BRANCH))
    mat = _whole((BRANCH, BRANCH))
    return pl.pallas_call(
        body, name="ssm_mix_fwd", grid=(t // tm,),
        in_specs=[row, row, row, row, vec, mat, vec, mat],
        out_specs=[row, row, row],
        out_shape=[_sds((t, BRANCH))] * 3,
        compiler_params=_params(("parallel",)),
    )(x, u, gate, y_scan, d, w_glu, b_glu, w_out)


def ssm_mix_bwd(dxo, u, gate, y, g2, w_glu, w_out, token=None):
    t = dxo.shape[0]
    tm = min(ROW_TILE_BWD, t)
    extra, extra_specs = _after(token)

    def body(dxo_ref, u_ref, gate_ref, y_ref, g2_ref, wgt_ref, wot_ref, *rest):
        dy_ref, dgate_ref, dwo_ref, dwg_ref, dbg_ref, dd_ref = rest[-6:]

        @pl.when(pl.program_id(0) == 0)
        def _():
            dwo_ref[...] = jnp.zeros_like(dwo_ref)
            dwg_ref[...] = jnp.zeros_like(dwg_ref)
            dbg_ref[...] = jnp.zeros_like(dbg_ref)
            dd_ref[...] = jnp.zeros_like(dd_ref)

        dxo = dxo_ref[...]
        gate = gate_ref[...]
        y = y_ref[...]
        z0, z0_grad = _gelu_and_grad(y)
        sg = _sigmoid(g2_ref[...])
        z = z0 * sg
        sgate, sgate_grad = _silu_and_grad(gate)
        da = _mm_nt(dxo, wot_ref[...])
        dwo_ref[...] += _mm_tn(z * sgate, dxo)
        dz = da * sgate
        dgate_ref[...] = da * z * sgate_grad
        dg2 = dz * z0 * sg * (1.0 - sg)
        dbg_ref[...] += jnp.sum(dg2, axis=0, keepdims=True)
        dwg_ref[...] += _mm_tn(z0, dg2)
        dz0 = dz * sg + _mm_nt(dg2, wgt_ref[...])
        dy = dz0 * z0_grad
        dd_ref[...] += jnp.sum(dy * u_ref[...], axis=0, keepdims=True)
        dy_ref[...] = dy

    row = _rows(tm, BRANCH)
    vec = _whole((1, BRANCH))
    mat = _whole((BRANCH, BRANCH))
    return pl.pallas_call(
        body, name="ssm_mix_bwd", grid=(t // tm,),
        in_specs=[row, row, row, row, row, mat, mat] + extra_specs,
        out_specs=[row, row, mat, mat, vec, vec],
        out_shape=[_sds((t, BRANCH)), _sds((t, BRANCH)), _sds((BRANCH, D_MODEL)), _sds((BRANCH, BRANCH)),
                   _sds((1, BRANCH)), _sds((1, BRANCH))],
        compiler_params=_params(("arbitrary",)),
    )(dxo, u, gate, y, g2, w_glu, w_out, *extra)


def ssm_proj_bwd(x, norm, dxo, dy, du_scan, dgate, d, w_in):
    t = x.shape[0]
    tm = min(ROW_TILE_BWD, t)
    n = 2 * BRANCH

    def body(x_ref, g_ref, dxo_ref, dy_ref, dus_ref, dgate_ref, d_ref, wt_ref, dx_ref, dw_ref, dg_ref):
        @pl.when(pl.program_id(0) == 0)
        def _():
            dw_ref[...] = jnp.zeros_like(dw_ref)
            dg_ref[...] = jnp.zeros_like(dg_ref)

        g = g_ref[...]
        r, xhat, h = _rms(x_ref[...], g)
        du = dus_ref[...] + d_ref[...] * dy_ref[...]
        dproj = jnp.concatenate([du, dgate_ref[...]], axis=1)
        dh = _mm_nt(dproj, wt_ref[...])
        dw_ref[...] += _mm_tn(h, dproj)
        dx, dg = _rms_bwd(dh, g, r, xhat)
        dg_ref[...] += dg
        dx_ref[...] = dxo_ref[...] + dx

    row = _rows(tm, D_MODEL)
    vec = _whole((1, D_MODEL))
    return pl.pallas_call(
        body, name="ssm_proj_bwd", grid=(t // tm,),
        in_specs=[row, vec, row, row, row, row, vec, _whole((D_MODEL, n))],
        out_specs=[row, _whole((D_MODEL, n)), vec],
        out_shape=[_sds((t, D_MODEL)), _sds((D_MODEL, n)), _sds((1, D_MODEL))],
        compiler_params=_params(("arbitrary",)),
    )(x, norm, dxo, dy, du_scan, dgate, d, w_in)


ATTN_N = Q_DIM + 2 * KV_DIM + BRANCH


def attn_proj_fwd(x, norm, w_in, cos2, sin2):
    t = x.shape[0]
    tm = min(ROW_TILE_FWD, t)

    def body(x_ref, g_ref, w_ref, cos_ref, sin_ref, q_ref, k_ref, v_ref, gate_ref):
        _, _, h = _rms(x_ref[...], g_ref[...])
        p = _mm(h, w_ref[...])
        cs = cos_ref[...]
        sn = sin_ref[...]
        q = p[:, :Q_DIM]
        k = p[:, Q_DIM:Q_DIM + KV_DIM]
        q_ref[...] = q * _tile_lanes(cs, Q_DIM // LANES) + _swap_half_heads(q) * _tile_lanes(sn, Q_DIM // LANES)
        k_ref[...] = k * cs + _swap_half_heads(k) * sn
        v_ref[...] = p[:, Q_DIM + KV_DIM:Q_DIM + 2 * KV_DIM]
        gate_ref[...] = p[:, Q_DIM + 2 * KV_DIM:]

    return pl.pallas_call(
        body, name="attn_proj_fwd", grid=(t // tm,),
        in_specs=[_rows(tm, D_MODEL), _whole((1, D_MODEL)), _whole((D_MODEL, ATTN_N)), _rows(tm, LANES), _rows(tm, LANES)],
        out_specs=[_rows(tm, Q_DIM), _rows(tm, KV_DIM), _rows(tm, KV_DIM), _rows(tm, BRANCH)],
        out_shape=[_sds((t, Q_DIM)), _sds((t, KV_DIM)), _sds((t, KV_DIM)), _sds((t, BRANCH))],
        compiler_params=_params(("parallel",)),
    )(x, norm, w_in, cos2, sin2)


GQA_LANES = GQA_GROUP * ATTN_BLOCK


def _window_masks(first_block):
    kj = lax.broadcasted_iota(jnp.int32, (ATTN_BLOCK, GQA_LANES), 0)
    qi = lax.broadcasted_iota(jnp.int32, (ATTN_BLOCK, GQA_LANES), 1) % ATTN_BLOCK
    return kj > qi, kj > jnp.where(first_block, qi, ATTN_BLOCK)


def _fold(upper, both):
    return jnp.where(upper, both[:ATTN_BLOCK], both[ATTN_BLOCK:])


def _unfold(upper, tile):
    return jnp.concatenate([jnp.where(upper, tile, 0.0), jnp.where(upper, 0.0, tile)], axis=0).astype(MXU_DTYPE)


def _stack_heads(ref, group):
    return jnp.concatenate([ref[:, h * HEAD_DIM:(h + 1) * HEAD_DIM] for h in range(group * GQA_GROUP, (group + 1) * GQA_GROUP)], axis=0)


def _unstack_heads(ref, group, stacked):
    for n in range(GQA_GROUP):
        h = group * GQA_GROUP + n
        ref[:, h * HEAD_DIM:(h + 1) * HEAD_DIM] = stacked[n * ATTN_BLOCK:(n + 1) * ATTN_BLOCK]


def _sink_row(sink_ref, group):
    return jnp.concatenate([jnp.full((1, ATTN_BLOCK), sink_ref[group * GQA_GROUP + n], F32) for n in range(GQA_GROUP)], axis=1)


def _lane_is(h):
    return lax.broadcasted_iota(jnp.int32, (1, LANES), 1) == h


def attn_fwd(q, k, v, sinks):
    t = q.shape[0]
    nb = t // ATTN_BLOCK
    scale = HEAD_DIM ** -0.5

    def body(sink_ref, q_ref, kc_ref, kp_ref, vc_ref, vp_ref, o_ref, lse_ref):
        keys = jnp.concatenate([kp_ref[...], kc_ref[...]], axis=0).astype(MXU_DTYPE)
        vals = jnp.concatenate([vp_ref[...], vc_ref[...]], axis=0).astype(MXU_DTYPE)
        upper, dead = _window_masks(pl.program_id(0) == 0)
        for g in range(N_KV_HEADS):
            kv = slice(g * HEAD_DIM, (g + 1) * HEAD_DIM)
            qs = _stack_heads(q_ref, g) * scale
            s = jnp.where(dead, NEG_INF, _fold(upper, _mm_nt(keys[:, kv], qs)))
            sink = _sink_row(sink_ref, g)
            m = jnp.maximum(jnp.max(s, axis=0, keepdims=True), sink)
            p = jnp.exp(s - m)
            den = jnp.sum(p, axis=0, keepdims=True) + jnp.exp(sink - m)
            _unstack_heads(o_ref, g, _mm_tn(_unfold(upper, p * (1.0 / den)), vals[:, kv]))
            lse = m + jnp.log(den)
            for n in range(GQA_GROUP):
                lse_ref[pl.ds(g * GQA_GROUP + n, 1), :] = lse[:, n * ATTN_BLOCK:(n + 1) * ATTN_BLOCK]

    cur = lambda n: pl.BlockSpec((ATTN_BLOCK, n), lambda i: (i, 0))
    prev = lambda n: pl.BlockSpec((ATTN_BLOCK, n), lambda i: (jnp.maximum(i - 1, 0), 0))
    return pl.pallas_call(
        body, name="attn_fwd", grid=(nb,),
        in_specs=[pl.BlockSpec(memory_space=pltpu.SMEM), cur(Q_DIM), cur(KV_DIM), prev(KV_DIM), cur(KV_DIM), prev(KV_DIM)],
        out_specs=[cur(Q_DIM), pl.BlockSpec((N_Q_HEADS, ATTN_BLOCK), lambda i: (0, i))],
        out_shape=[_sds((t, Q_DIM)), _sds((N_Q_HEADS, t))],
        compiler_params=_params(("parallel",)),
    )(sinks, q, k, k, v, v)


def attn_bwd(q, k, v, sinks, o, lse, do):
    t = q.shape[0]
    nb = t // ATTN_BLOCK
    scale = HEAD_DIM ** -0.5

    def body(sink_ref, q_ref, kc_ref, kp_ref, vc_ref, vp_ref, o_ref, lse_ref, do_ref,
             dq_ref, dk_ref, dv_ref, dsink_ref, dk_carry, dv_carry):
        i = pl.program_id(0)

        @pl.when(i == 0)
        def _():
            dsink_ref[...] = jnp.zeros_like(dsink_ref)
            dk_carry[...] = jnp.zeros_like(dk_carry)
            dv_carry[...] = jnp.zeros_like(dv_carry)

        @pl.when(i < nb)
        def _():
            keys = jnp.concatenate([kp_ref[...], kc_ref[...]], axis=0).astype(MXU_DTYPE)
            vals = jnp.concatenate([vp_ref[...], vc_ref[...]], axis=0).astype(MXU_DTYPE)
            upper, dead = _window_masks(i == 0)
            dsink = jnp.zeros((1, LANES), F32)
            dk_heads = []
            dv_heads = []
            for g in range(N_KV_HEADS):
                kv = slice(g * HEAD_DIM, (g + 1) * HEAD_DIM)
                qs = (_stack_heads(q_ref, g) * scale).astype(MXU_DTYPE)
                dos = _stack_heads(do_ref, g)
                lse = jnp.concatenate([lse_ref[pl.ds(g * GQA_GROUP + n, 1), :] for n in range(GQA_GROUP)], axis=1)
                s = jnp.where(dead, NEG_INF, _fold(upper, _mm_nt(keys[:, kv], qs)))
                p = jnp.exp(s - lse)
                delta = _mm_f32(jnp.ones((8, HEAD_DIM), F32), dos * _stack_heads(o_ref, g), ((1,), (1,)))[:1]
                dos = dos.astype(MXU_DTYPE)
                ds = _unfold(upper, p * (_fold(upper, _mm_nt(vals[:, kv], dos)) - delta))
                _unstack_heads(dq_ref, g, _mm_tn(ds, keys[:, kv]) * scale)
                dk_heads.append(_mm(ds, qs))
                dv_heads.append(_mm(_unfold(upper, p), dos))
                at_sink = jnp.exp(_sink_row(sink_ref, g) - lse) * delta
                for n in range(GQA_GROUP):
                    dsink = dsink + jnp.where(_lane_is(g * GQA_GROUP + n), -jnp.sum(at_sink[:, n * ATTN_BLOCK:(n + 1) * ATTN_BLOCK]), 0.0)
            dkk = jnp.concatenate(dk_heads, axis=1)
            dvv = jnp.concatenate(dv_heads, axis=1)
            dsink_ref[...] += dsink
            dk_ref[...] = dk_carry[...] + dkk[:ATTN_BLOCK]
            dv_ref[...] = dv_carry[...] + dvv[:ATTN_BLOCK]
            dk_carry[...] = dkk[ATTN_BLOCK:]
            dv_carry[...] = dvv[ATTN_BLOCK:]

        @pl.when(i == nb)
        def _():
            dk_ref[...] = dk_carry[...]
            dv_ref[...] = dv_carry[...]

    last = nb - 1
    cur = lambda n: pl.BlockSpec((ATTN_BLOCK, n), lambda i: (jnp.minimum(i, last), 0))
    prev = lambda n: pl.BlockSpec((ATTN_BLOCK, n), lambda i: (jnp.clip(i - 1, 0, last), 0))
    late = lambda n: pl.BlockSpec((ATTN_BLOCK, n), lambda i: (i, 0))
    dq, dk_late, dv_late, dsinks = pl.pallas_call(
        body, name="attn_bwd", grid=(nb + 1,),
        in_specs=[pl.BlockSpec(memory_space=pltpu.SMEM), cur(Q_DIM), cur(KV_DIM), prev(KV_DIM), cur(KV_DIM), prev(KV_DIM),
                  cur(Q_DIM), pl.BlockSpec((N_Q_HEADS, ATTN_BLOCK), lambda i: (0, jnp.minimum(i, last))), cur(Q_DIM)],
        out_specs=[cur(Q_DIM), late(KV_DIM), late(KV_DIM), _whole((1, LANES))],
        out_shape=[_sds((t, Q_DIM)), _sds((t + ATTN_BLOCK, KV_DIM)), _sds((t + ATTN_BLOCK, KV_DIM)), _sds((1, LANES))],
        scratch_shapes=[pltpu.VMEM((ATTN_BLOCK, KV_DIM), F32), pltpu.VMEM((ATTN_BLOCK, KV_DIM), F32)],
        compiler_params=_params(("arbitrary",)),
    )(sinks, q, k, k, v, v, o, lse, do)
    return dq, dk_late[ATTN_BLOCK:], dv_late[ATTN_BLOCK:], dsinks


def attn_out_fwd(x, o, gate, w_out):
    t = x.shape[0]
    tm = min(ROW_TILE_FWD, t)

    def body(x_ref, o_ref, gate_ref, w_ref, xo_ref):
        xo_ref[...] = x_ref[...] + _mm(o_ref[...] * _silu(gate_ref[...]), w_ref[...])

    row = _rows(tm, D_MODEL)
    return pl.pallas_call(
        body, name="attn_out_fwd", grid=(t // tm,),
        in_specs=[row, row, row, _whole((Q_DIM, D_MODEL))], out_specs=row, out_shape=_sds((t, D_MODEL)),
        compiler_params=_params(("parallel",)),
    )(x, o, gate, w_out)


def attn_out_bwd(dxo, o, gate, w_out, token=None):
    t = dxo.shape[0]
    tm = min(ROW_TILE_BWD, t)
    extra, extra_specs = _after(token)

    def body(dxo_ref, o_ref, gate_ref, wt_ref, *rest):
        do_ref, dgate_ref, dw_ref = rest[-3:]

        @pl.when(pl.program_id(0) == 0)
        def _():
            dw_ref[...] = jnp.zeros_like(dw_ref)

        dxo = dxo_ref[...]
        o = o_ref[...]
        gate = gate_ref[...]
        sgate, sgate_grad = _silu_and_grad(gate)
        da = _mm_nt(dxo, wt_ref[...])
        dw_ref[...] += _mm_tn(o * sgate, dxo)
        do_ref[...] = da * sgate
        dgate_ref[...] = da * o * sgate_grad

    row = _rows(tm, D_MODEL)
    mat = _whole((Q_DIM, D_MODEL))
    return pl.pallas_call(
        body, name="attn_out_bwd", grid=(t // tm,),
        in_specs=[row, row, row, mat] + extra_specs, out_specs=[row, row, mat],
        out_shape=[_sds((t, Q_DIM)), _sds((t, BRANCH)), _sds((Q_DIM, D_MODEL))],
        compiler_params=_params(("arbitrary",)),
    )(dxo, o, gate, w_out, *extra)


def attn_proj_bwd(x, norm, dxo, dq, dk, dv, dgate, cos2, sin2, w_in):
    t = x.shape[0]
    tm = min(ROW_TILE_BWD, t)

    def body(x_ref, g_ref, dxo_ref, dq_ref, dk_ref, dv_ref, dgate_ref, cos_ref, sin_ref, wt_ref, dx_ref, dw_ref, dg_ref):
        @pl.when(pl.program_id(0) == 0)
        def _():
            dw_ref[...] = jnp.zeros_like(dw_ref)
            dg_ref[...] = jnp.zeros_like(dg_ref)

        g = g_ref[...]
        r, xhat, h = _rms(x_ref[...], g)
        cs = cos_ref[...]
        sn = sin_ref[...]
        dqr = dq_ref[...]
        dkr = dk_ref[...]
        dq = dqr * _tile_lanes(cs, Q_DIM // LANES) + _swap_half_heads(dqr * _tile_lanes(sn, Q_DIM // LANES))
        dk = dkr * cs + _swap_half_heads(dkr * sn)
        dproj = jnp.concatenate([dq, dk, dv_ref[...], dgate_ref[...]], axis=1)
        dh = _mm_nt(dproj, wt_ref[...])
        dw_ref[...] += _mm_tn(h, dproj)
        dx, dg = _rms_bwd(dh, g, r, xhat)
        dg_ref[...] += dg
        dx_ref[...] = dxo_ref[...] + dx

    row = _rows(tm, D_MODEL)
    vec = _whole((1, D_MODEL))
    return pl.pallas_call(
        body, name="attn_proj_bwd", grid=(t // tm,),
        in_specs=[row, vec, row, _rows(tm, Q_DIM), _rows(tm, KV_DIM), _rows(tm, KV_DIM), _rows(tm, BRANCH),
                  _rows(tm, LANES), _rows(tm, LANES), _whole((D_MODEL, ATTN_N))],
        out_specs=[row, _whole((D_MODEL, ATTN_N)), vec],
        out_shape=[_sds((t, D_MODEL)), _sds((D_MODEL, ATTN_N)), _sds((1, D_MODEL))],
        compiler_params=_params(("arbitrary",)),
    )(x, norm, dxo, dq, dk, dv, dgate, cos2, sin2, w_in)


def loss_head(x, norm, target):
    t = x.shape[0]
    tm = min(ROW_TILE_FWD, t)

    def body(x_ref, g_ref, tgt_ref, loss_ref, dx_ref, dg_ref):
        @pl.when(pl.program_id(0) == 0)
        def _():
            loss_ref[...] = jnp.zeros_like(loss_ref)
            dg_ref[...] = jnp.zeros_like(dg_ref)

        g = g_ref[...]
        r, xhat, y = _rms(x_ref[...], g)
        err = y - tgt_ref[...]
        loss_ref[...] += 0.5 * jnp.sum(jnp.mean(err * err, axis=-1, keepdims=True), axis=0, keepdims=True)
        dx, dg = _rms_bwd(err * (1.0 / D_MODEL), g, r, xhat)
        dg_ref[...] += dg
        dx_ref[...] = dx

    row = _rows(tm, D_MODEL)
    vec = _whole((1, D_MODEL))
    return pl.pallas_call(
        body, name="loss_head", grid=(t // tm,),
        in_specs=[row, vec, row], out_specs=[_whole((1, 1)), row, vec],
        out_shape=[_sds((1, 1)), _sds((t, D_MODEL)), _sds((1, D_MODEL))],
        compiler_params=_params(("arbitrary",)),
    )(x, norm, target)


OCT_TILE = pl.BlockSpec((None, LANES, LANES), lambda b: (b, 0, 0))
N_LAGS = S5_CHUNK + 1


def _cmul(ar, ai, br, bi):
    return ar * br - ai * bi, ar * bi + ai * br


def _cmul_conj(ar, ai, br, bi):
    return ar * br + ai * bi, ar * bi - ai * br


def _mm_f32(a, b, dims):
    return lax.dot_general(a, b, (dims, ((), ())), precision=lax.Precision.HIGHEST, preferred_element_type=F32)


def _s5_discretise(ar, ai, ls, br, bi):
    dt = jnp.exp(ls)
    xr = ar * dt
    xi = ai * dt
    mag = jnp.exp(xr)
    first = (mag * jnp.cos(xi), mag * jnp.sin(xi))
    powers = [(jnp.ones_like(xr), jnp.zeros_like(xr)), first]
    for _ in range(2, N_LAGS):
        powers.append(_cmul(*powers[-1], *first))
    den = ar * ar + ai * ai
    nr = powers[1][0] - 1.0
    ni = powers[1][1]
    fr = (nr * ar + ni * ai) / den
    fi = (ni * ar - nr * ai) / den
    bbr, bbi = _cmul(fr, fi, br, bi)
    return dt, powers, (fr, fi), (bbr, bbi), den


def _same_group_tile():
    row = lax.broadcasted_iota(jnp.int32, (LANES, LANES), 0)
    col = lax.broadcasted_iota(jnp.int32, (LANES, LANES), 1)
    return (row // SSM_GROUP) == (col // SSM_GROUP)


def _first_copy_lanes():
    return lax.broadcasted_iota(jnp.int32, (LANES, LANES), 1) < SSM_STATE


def s5_param_fwd(tiles):
    def body(ar_ref, ai_ref, ls_ref, br_ref, bi_ref, cr_ref, ci_ref, kd_ref, wsr_ref, wsi_ref, wor_ref, woi_ref, pr_ref, pi_ref):
        cr = cr_ref[...]
        ci = ci_ref[...]
        _, powers, _, (bbr, bbi), _ = _s5_discretise(ar_ref[...], ai_ref[...], ls_ref[...], br_ref[...], bi_ref[...])
        once = _first_copy_lanes()
        crm = jnp.where(once, cr, 0.0)
        cim = jnp.where(once, ci, 0.0)
        same = _same_group_tile()
        for lag in range(S5_CHUNK):
            er, ei = powers[lag]
            xr, xi = _cmul(er, ei, bbr, bbi)
            rows = pl.ds((S5_CHUNK - 1 - lag) * LANES, LANES)
            wsr_ref[rows, :] = xr
            wsi_ref[rows, :] = xi
        k = _mm_f32(wsr_ref[...], crm, ((1,), (1,))) - _mm_f32(wsi_ref[...], cim, ((1,), (1,)))
        for lag in range(S5_CHUNK):
            kd_ref[lag] = jnp.where(same, k[(S5_CHUNK - 1 - lag) * LANES:(S5_CHUNK - lag) * LANES], 0.0)
        for t in range(S5_CHUNK):
            er, ei = powers[t + 1]
            zr, zi = _cmul(er, ei, cr, ci)
            wor_ref[pl.ds(t * LANES, LANES), :] = zr
            woi_ref[pl.ds(t * LANES, LANES), :] = -zi
        pr_ref[...] = powers[S5_CHUNK][0]
        pi_ref[...] = powers[S5_CHUNK][1]

    return pl.pallas_call(
        body, name="s5_param_fwd", grid=(S5_OCTETS,),
        in_specs=[OCT_TILE] * 7, out_specs=[OCT_KD, OCT_W, OCT_W, OCT_W, OCT_W, OCT_TILE, OCT_TILE],
        out_shape=[_sds((S5_OCTETS, S5_CHUNK, LANES, LANES))] + [_sds((S5_OCTETS, S5_OCT_IN, LANES))] * 4
                  + [_sds((S5_OCTETS, LANES, LANES))] * 2,
        compiler_params=_params(("parallel",)),
    )(*tiles)


def s5_param_bwd(tiles, dkd, dws_re, dws_im, dwo_re, dwo_im, dp_re, dp_im):
    def body(ar_ref, ai_ref, ls_ref, br_ref, bi_ref, cr_ref, ci_ref, dkd_ref, dwsr_ref, dwsi_ref, dwor_ref, dwoi_ref, dpr_ref, dpi_ref,
             dar_ref, dai_ref, dls_ref, dbr_ref, dbi_ref, dcr_ref, dci_ref):
        ar = ar_ref[...]
        ai = ai_ref[...]
        br = br_ref[...]
        bi = bi_ref[...]
        cr = cr_ref[...]
        ci = ci_ref[...]
        dt, powers, (fr, fi), (bbr, bbi), den = _s5_discretise(ar, ai, ls_ref[...], br, bi)
        once = _first_copy_lanes()
        crm = jnp.where(once, cr, 0.0)
        cim = jnp.where(once, ci, 0.0)
        same = _same_group_tile()
        zero = jnp.zeros((LANES, LANES), F32)
        dpow = [[zero, zero] for _ in range(N_LAGS)]
        dbbr, dbbi = zero, zero
        by_step = [S5_CHUNK - 1 - s for s in range(S5_CHUNK)]
        x_all = [_cmul(*powers[lag], bbr, bbi) for lag in by_step]
        xr_all = jnp.concatenate([x[0] for x in x_all], axis=0)
        xi_all = jnp.concatenate([x[1] for x in x_all], axis=0)
        g_all = jnp.concatenate([jnp.where(same, dkd_ref[lag], 0.0) for lag in by_step], axis=0)
        dxr_all = dwsr_ref[...] + _mm_f32(g_all, crm, ((1,), (0,)))
        dxi_all = dwsi_ref[...] - _mm_f32(g_all, cim, ((1,), (0,)))
        dcr = jnp.where(once, _mm_f32(g_all, xr_all, ((0,), (0,))), 0.0)
        dci = -jnp.where(once, _mm_f32(g_all, xi_all, ((0,), (0,))), 0.0)
        for lag in range(S5_CHUNK):
            er, ei = powers[lag]
            rows = slice((S5_CHUNK - 1 - lag) * LANES, (S5_CHUNK - lag) * LANES)
            dxr = dxr_all[rows]
            dxi = dxi_all[rows]
            a, b = _cmul_conj(bbr, bbi, dxr, dxi)
            dpow[lag][0] = dpow[lag][0] + a
            dpow[lag][1] = dpow[lag][1] + b
            a, b = _cmul_conj(er, ei, dxr, dxi)
            dbbr = dbbr + a
            dbbi = dbbi + b
        for t in range(S5_CHUNK):
            er, ei = powers[t + 1]
            dzr = dwor_ref[pl.ds(t * LANES, LANES), :]
            dzi = -dwoi_ref[pl.ds(t * LANES, LANES), :]
            a, b = _cmul_conj(cr, ci, dzr, dzi)
            dpow[t + 1][0] = dpow[t + 1][0] + a
            dpow[t + 1][1] = dpow[t + 1][1] + b
            a, b = _cmul_conj(er, ei, dzr, dzi)
            dcr = dcr + a
            dci = dci + b
        dpow[S5_CHUNK][0] = dpow[S5_CHUNK][0] + dpr_ref[...]
        dpow[S5_CHUNK][1] = dpow[S5_CHUNK][1] + dpi_ref[...]
        dfr, dfi = _cmul_conj(br, bi, dbbr, dbbi)
        dbr, dbi = _cmul_conj(fr, fi, dbbr, dbbi)
        dnr, dni = _cmul(ar / den, ai / den, dfr, dfi)
        qr = (fr * ar + fi * ai) / den
        qi = (fi * ar - fr * ai) / den
        dlr, dli = _cmul(-qr, qi, dfr, dfi)
        dpow[1][0] = dpow[1][0] + dnr
        dpow[1][1] = dpow[1][1] + dni
        dxr, dxi = zero, zero
        for lag in range(1, N_LAGS):
            a, b = _cmul_conj(powers[lag][0], powers[lag][1], dpow[lag][0], dpow[lag][1])
            dxr = dxr + lag * a
            dxi = dxi + lag * b
        dar_ref[...] = dlr + dt * dxr
        dai_ref[...] = dli + dt * dxi
        dls_ref[...] = dt * (ar * dxr + ai * dxi)
        dbr_ref[...] = dbr
        dbi_ref[...] = dbi
        dcr_ref[...] = dcr
        dci_ref[...] = dci

    return pl.pallas_call(
        body, name="s5_param_bwd", grid=(S5_OCTETS,),
        in_specs=[OCT_TILE] * 7 + [OCT_KD, OCT_W, OCT_W, OCT_W, OCT_W, OCT_TILE, OCT_TILE], out_specs=[OCT_TILE] * 7,
        out_shape=[_sds((S5_OCTETS, LANES, LANES))] * 7,
        compiler_params=_params(("parallel",)),
    )(*tiles, dkd, dws_re, dws_im, dwo_re, dwo_im, dp_re, dp_im)


def _doubled(v):
    return jnp.concatenate([v, v], axis=-1)


def _s5_param_tiles(a_re, a_im, log_step, b_re, b_im, c_re, c_im):
    def per_group(a):
        return _doubled(jnp.broadcast_to(a.reshape(S5_OCTETS, S5_OCT, 1, SSM_STATE),
                                         (S5_OCTETS, S5_OCT, SSM_GROUP, SSM_STATE)).reshape(S5_OCTETS, LANES, SSM_STATE))

    ls = jnp.broadcast_to(log_step.reshape(S5_OCTETS, S5_OCT, 1, 1), (S5_OCTETS, S5_OCT, SSM_GROUP, LANES)).reshape(S5_OCTETS, LANES, LANES)
    bt = lambda b: _doubled(b.transpose(0, 2, 1).reshape(S5_OCTETS, LANES, SSM_STATE))
    ct = lambda c: _doubled(c.reshape(S5_OCTETS, LANES, SSM_STATE))
    return [per_group(a_re), per_group(a_im), ls, bt(b_re), bt(b_im), ct(c_re), ct(c_im)]


def _s5_param_grads(dtiles):
    dar, dai, dls, dbr, dbi, dcr, dci = dtiles
    halves = lambda d: d[..., :SSM_STATE] + d[..., SSM_STATE:]
    per_group = lambda d: halves(d).reshape(SSM_GROUPS, SSM_GROUP, SSM_STATE).sum(axis=1)
    per_row = lambda d: halves(d).reshape(SSM_GROUPS, SSM_GROUP, SSM_STATE)
    return (per_group(dar), per_group(dai), dls.reshape(SSM_GROUPS, SSM_GROUP * LANES).sum(axis=1),
            per_row(dbr).transpose(0, 2, 1), per_row(dbi).transpose(0, 2, 1), per_row(dcr), per_row(dci))


def _group_power_rows(tile):
    return tile[:, ::SSM_GROUP, :SSM_STATE].reshape(1, S5_STATES)


def _group_power_tiles(row):
    t = jnp.pad(row.reshape(S5_OCTETS, S5_OCT, 1, SSM_STATE), ((0, 0), (0, 0), (0, SSM_GROUP - 1), (0, LANES - SSM_STATE)))
    return t.reshape(S5_OCTETS, LANES, LANES)


def _rope_tables(t):
    pos = jnp.arange(t, dtype=F32)
    inv_freq = ROPE_THETA ** (-jnp.arange(0, HEAD_DIM, 2, dtype=F32) / HEAD_DIM)
    ang = pos[:, None] * inv_freq[None, :]
    cos = jnp.cos(ang)
    sin = jnp.sin(ang)
    cos64 = jnp.concatenate([cos, cos], axis=1)
    sin64 = jnp.concatenate([-sin, sin], axis=1)
    return jnp.concatenate([cos64, cos64], axis=1), jnp.concatenate([sin64, sin64], axis=1)


def _row(v):
    return v.reshape(1, -1)


def _ssm_forward(x, w, token=None):
    tiles = _s5_param_tiles(w["a_re"], w["a_im"], w["log_step"], w["b_re"], w["b_im"], w["c_re"], w["c_im"])
    kd, ws_re, ws_im, wo_re, wo_im, p_re, p_im = s5_param_fwd(tiles)
    mats = dict(kd=kd, ws_re=ws_re, ws_im=ws_im, wo_re=wo_re, wo_im=wo_im, a_re=_group_power_rows(p_re), a_im=_group_power_rows(p_im))
    u, gate = ssm_proj_fwd(x, _row(w["norm"]), w["w_in"], token)
    s_re, s_im = s5_chunk_states(u, mats["ws_re"], mats["ws_im"])
    h_re, h_im = s5_scan_fwd(s_re, s_im, mats["a_re"], mats["a_im"])
    y_scan = s5_outputs(u, h_re, h_im, mats["kd"], mats["wo_re"], mats["wo_im"])
    y, g2, x_new = ssm_mix_fwd(x, u, gate, y_scan, _row(w["d"]), w["w_glu"], _row(w["b_glu"]), w["w_out"])
    saved = dict(x=x, u=u, gate=gate, y=y, g2=g2, h_re=h_re, h_im=h_im, mats=mats, tiles=tiles)
    return x_new, saved


def _ssm_backward(dxo, w, s, token=None):
    dy, dgate, dw_out, dw_glu, db_glu, dd = ssm_mix_bwd(dxo, s["u"], s["gate"], s["y"], s["g2"], w["w_glu"], w["w_out"], token)
    mats = s["mats"]
    dh_re, dh_im = s5_state_grads(dy, mats["wo_re"], mats["wo_im"])
    ds_re, ds_im, da_re, da_im = s5_scan_bwd(dh_re, dh_im, s["h_re"], s["h_im"], mats["a_re"], mats["a_im"])
    du_scan = s5_input_grads(dy, ds_re, ds_im, mats["kd"], mats["ws_re"], mats["ws_im"])
    dkd, dws_re, dws_im, dwo_re, dwo_im = s5_weight_grads(s["u"], dy, s["h_re"], s["h_im"], ds_re, ds_im)
    dparams = _s5_param_grads(s5_param_bwd(s["tiles"], dkd, dws_re, dws_im, dwo_re, dwo_im,
                                           _group_power_tiles(da_re), _group_power_tiles(da_im)))
    dx, dw_in, dnorm = ssm_proj_bwd(s["x"], _row(w["norm"]), dxo, dy, du_scan, dgate, _row(w["d"]), w["w_in"])
    grads = dict(norm=dnorm, w_in=dw_in, d=dd, w_glu=dw_glu, b_glu=db_glu, w_out=dw_out)
    for name, val in zip(("a_re", "a_im", "log_step", "b_re", "b_im", "c_re", "c_im"), dparams):
        grads[name] = val
    return dx, grads


def _attn_forward(x, w, cos2, sin2):
    q, k, v, gate = attn_proj_fwd(x, _row(w["norm"]), w["w_in"], cos2, sin2)
    o, lse = attn_fwd(q, k, v, w["sinks"])
    x_new = attn_out_fwd(x, o, gate, w["w_out"])
    return x_new, dict(x=x, q=q, k=k, v=v, gate=gate, o=o, lse=lse)


def _attn_backward(dxo, w, s, cos2, sin2, token=None):
    do, dgate, dw_out = attn_out_bwd(dxo, s["o"], s["gate"], w["w_out"], token)
    dq, dk, dv, dsinks = attn_bwd(s["q"], s["k"], s["v"], w["sinks"], s["o"], s["lse"], do)
    dx, dw_in, dnorm = attn_proj_bwd(s["x"], _row(w["norm"]), dxo, dq, dk, dv, dgate, cos2, sin2, w["w_in"])
    return dx, dict(norm=dnorm, w_in=dw_in, sinks=dsinks[0, :N_Q_HEADS], w_out=dw_out)


class _NoExchanges:
    def __init__(self, layers):
        self.layers = layers

    def first_token(self):
        return None

    def layer(self, i, x):
        return self.layers[i]

    def layer_done(self, i, grads, dx):
        return None


def _sequence_step(x, target, final_norm, hooks, depth=4):
    cos2, sin2 = _rope_tables(x.shape[0])
    saved, layers = [], []
    for i in range(depth):
        w = hooks.layer(i, x)
        layers.append(w)
        if i % 2 == 0:
            x, s = _ssm_forward(x, w, hooks.first_token() if i == 0 else None)
        else:
            x, s = _attn_forward(x, w, cos2, sin2)
        saved.append(s)
    loss, dx, dfinal = loss_head(x, _row(final_norm), target)
    grads = {"final_norm": dfinal}
    token = None
    for i in reversed(range(depth)):
        if i % 2 == 0:
            dx, g = _ssm_backward(dx, layers[i], saved[i], token)
        else:
            dx, g = _attn_backward(dx, layers[i], saved[i], cos2, sin2, token)
        g = {"l%d_%s" % (i, name): val for name, val in g.items()}
        grads.update(g)
        token = hooks.layer_done(i, g, dx)
    return loss[0, 0], dx, grads


ANY = pl.BlockSpec(memory_space=pl.ANY)


def _place():
    return lax.axis_index("x"), lax.axis_index("y"), lax.axis_index("c")


def _other_chips(x, y):
    return [(1 - x, y), (x, 1 - y), (1 - x, 1 - y)]


class _StagedCopies:
    def __init__(self, bufs, load_sems, store_sems):
        self.bufs, self.load_sems, self.store_sems = bufs, load_sems, store_sems
        self.loads, self.stores = [], []

    def load(self, i, src):
        cp = pltpu.make_async_copy(src, self.bufs[i], self.load_sems.at[i])
        cp.start()
        self.loads.append(cp)

    def store(self, i, dst):
        self.loads[i].wait()
        cp = pltpu.make_async_copy(self.bufs[i], dst, self.store_sems.at[i])
        cp.start()
        self.stores.append(cp)

    def finish(self):
        for cp in self.stores:
            cp.wait()


def _staging(blocks):
    n = len(blocks)
    return [pltpu.VMEM(b.shape, b.dtype) for b in blocks] + [pltpu.SemaphoreType.DMA((n,)), pltpu.SemaphoreType.DMA((n,))]


def gather_weight_shards(shards):
    n = len(shards)

    def body(*refs):
        ins, outs = refs[:n], refs[n:2 * n]
        send_sems, recv_sems, pass_send_sems, pass_recv_sems = refs[2 * n:2 * n + 4]
        own = _StagedCopies(refs[2 * n + 4:3 * n + 4], *refs[3 * n + 4:])
        x, y, c = _place()
        me = 2 * x + y
        chips = _other_chips(x, y)

        def half(i, block, which):
            rows = ins[i].shape[0] // 2
            return outs[i].at[block, pl.ds(which * rows, rows), :]

        def my_half(i):
            rows = ins[i].shape[0] // 2
            return ins[i].at[pl.ds(c * rows, rows), :]

        for i in range(n):
            own.load(i, ins[i])
        sends = []
        for i in range(n):
            for k, (tx, ty) in enumerate(chips):
                cp = pltpu.make_async_remote_copy(src_ref=my_half(i), dst_ref=half(i, me, c), send_sem=send_sems.at[i, k],
                                                  recv_sem=recv_sems.at[i, k], device_id=(tx, ty, c), device_id_type=MESH)
                cp.start()
                sends.append(cp)
        for i in range(n):
            own.store(i, outs[i].at[me])
        for i in range(n):
            for k, (tx, ty) in enumerate(chips):
                landed = half(i, 2 * tx + ty, c)
                pltpu.make_async_remote_copy(src_ref=my_half(i), dst_ref=landed, send_sem=send_sems.at[i, k],
                                             recv_sem=recv_sems.at[i, k], device_id=(tx, ty, c), device_id_type=MESH).wait_recv()
                cp = pltpu.make_async_remote_copy(src_ref=landed, dst_ref=landed, send_sem=pass_send_sems.at[i, k],
                                                  recv_sem=pass_recv_sems.at[i, k], device_id=(x, y, 1 - c), device_id_type=MESH)
                cp.start()
                sends.append(cp)
        for i in range(n):
            for k, (tx, ty) in enumerate(chips):
                missing = half(i, 2 * tx + ty, 1 - c)
                pltpu.make_async_remote_copy(src_ref=missing, dst_ref=missing, send_sem=pass_send_sems.at[i, k],
                                             recv_sem=pass_recv_sems.at[i, k], device_id=(x, y, 1 - c), device_id_type=MESH).wait_recv()
        for cp in sends:
            cp.wait_send()
        own.finish()

    sems = pltpu.SemaphoreType.DMA((n, 3))
    return pl.pallas_call(
        body, name="gather_weight_shards",
        in_specs=[ANY] * n, out_specs=[ANY] * n,
        out_shape=[_sds((4,) + s.shape, s.dtype) for s in shards],
        scratch_shapes=[sems, sems, sems, sems] + _staging(shards),
        compiler_params=_params(),
    )(*shards)


def exchange_halves_with_sibling(grads):
    n = len(grads)

    def body(*refs):
        ins, outs = refs[:n], refs[n:2 * n]
        send_sems, recv_sems = refs[2 * n:]
        x, y, c = _place()
        copies = []
        for i in range(n):
            half = ins[i].shape[1] // 2
            src = ins[i].at[:, pl.ds((1 - c) * half, half), :]
            cp = pltpu.make_async_remote_copy(src_ref=src, dst_ref=outs[i], send_sem=send_sems.at[i], recv_sem=recv_sems.at[i],
                                              device_id=(x, y, 1 - c), device_id_type=MESH)
            cp.start()
            copies.append(cp)
        for cp in copies:
            cp.wait()

    return pl.pallas_call(
        body, name="exchange_halves_with_sibling",
        in_specs=[ANY] * n, out_specs=[ANY] * n,
        out_shape=[_sds((g.shape[0], g.shape[1] // 2, g.shape[2])) for g in grads],
        scratch_shapes=[pltpu.SemaphoreType.DMA((n,)), pltpu.SemaphoreType.DMA((n,))],
    )(*grads)


def swap_halves_with_sibling(pieces):
    n = len(pieces)

    def body(*refs):
        ins, outs = refs[:n], refs[n:2 * n]
        send_sems, recv_sems = refs[2 * n:2 * n + 2]
        own = _StagedCopies(refs[2 * n + 2:3 * n + 2], *refs[3 * n + 2:])
        x, y, c = _place()
        for i in range(n):
            own.load(i, ins[i])
        swaps = []
        for i in range(n):
            cp = pltpu.make_async_remote_copy(src_ref=ins[i], dst_ref=outs[i].at[c], send_sem=send_sems.at[i], recv_sem=recv_sems.at[i],
                                              device_id=(x, y, 1 - c), device_id_type=MESH)
            cp.start()
            swaps.append(cp)
        for i in range(n):
            own.store(i, outs[i].at[c])
        for i in range(n):
            pltpu.make_async_remote_copy(src_ref=ins[i], dst_ref=outs[i].at[1 - c], send_sem=send_sems.at[i], recv_sem=recv_sems.at[i],
                                         device_id=(x, y, 1 - c), device_id_type=MESH).wait_recv()
        for cp in swaps:
            cp.wait_send()
        own.finish()

    return pl.pallas_call(
        body, name="swap_halves_with_sibling",
        in_specs=[ANY] * n, out_specs=[ANY] * n,
        out_shape=[_sds((2,) + p.shape) for p in pieces],
        scratch_shapes=[pltpu.SemaphoreType.DMA((n,)), pltpu.SemaphoreType.DMA((n,))] + _staging(pieces),
        compiler_params=_params(),
    )(*pieces)


IN_HBM = pl.BlockSpec(memory_space=pltpu.HBM)
SEMAPHORES = pl.BlockSpec(memory_space=pltpu.SEMAPHORE)
DATAFLOW = pltpu.SideEffectType.DATAFLOW_SIDE_EFFECTING


def _hbm(a):
    return pltpu.with_memory_space_constraint(a, pltpu.HBM)


def place_own_blocks(shards):
    n = len(shards)

    def body(*refs):
        ins, outs = refs[:n], refs[n:2 * n]
        own = _StagedCopies(refs[2 * n:3 * n], *refs[3 * n:])
        x, y, _ = _place()
        for i in range(n):
            own.load(i, ins[i])
        for i in range(n):
            own.store(i, outs[i].at[2 * x + y])
        own.finish()

    return pl.pallas_call(
        body, name="place_own_blocks", in_specs=[ANY] * n, out_specs=[ANY] * n,
        out_shape=[_sds((4,) + s.shape, s.dtype) for s in shards],
        scratch_shapes=_staging(shards), compiler_params=_params(),
    )(*shards)


def _block_to_send(ref, chip, per_target):
    if not per_target:
        return ref
    return ref.at[chip] if ref.shape[0] == 4 else ref.at[0]


def start_sends_to_chips(name, sources, landings, per_target, after):
    n = len(sources)
    n_sems = 2 * 3 * n

    def body(*refs):
        srcs = refs[:n]
        sems = refs[2 * n + 1:2 * n + 1 + n_sems]
        lands = refs[2 * n + 1 + n_sems:3 * n + 1 + n_sems]
        token = refs[3 * n + 1 + n_sems]
        x, y, c = _place()
        me = 2 * x + y
        for i in range(n):
            for k, (tx, ty) in enumerate(_other_chips(x, y)):
                src = _block_to_send(srcs[i], 2 * tx + ty, per_target)
                pltpu.make_async_remote_copy(src_ref=src, dst_ref=lands[i].at[me], send_sem=sems[2 * (3 * i + k)], recv_sem=sems[2 * (3 * i + k) + 1],
                                             device_id=(tx, ty, c), device_id_type=MESH).start()
        token[...] = jnp.zeros_like(token)

    outs = pl.pallas_call(
        body, name=name,
        in_specs=[IN_HBM] * (2 * n) + [ANY],
        out_specs=[SEMAPHORES] * n_sems + [IN_HBM] * n + [pl.BlockSpec(memory_space=pltpu.VMEM)],
        out_shape=[pltpu.SemaphoreType.DMA(())] * n_sems + [pltpu.HBM(l.shape, l.dtype) for l in landings] + [_sds(TOKEN_SHAPE)],
        input_output_aliases={n + i: n_sems + i for i in range(n)},
        compiler_params=pltpu.CompilerParams(has_side_effects=DATAFLOW),
    )(*[_hbm(s) for s in sources], *[_hbm(l) for l in landings], after)
    return list(outs[:n_sems]), list(outs[n_sems:n_sems + n]), outs[n_sems + n]


def wait_sends_to_chips(name, sources, landings, sems, per_target, after):
    n = len(sources)
    n_sems = len(sems)

    def body(*refs):
        srcs = refs[:n]
        sem_refs = refs[2 * n:2 * n + n_sems]
        lands = refs[2 * n + n_sems + 1:]
        x, y, c = _place()
        me = 2 * x + y
        for i in range(n):
            for k, (tx, ty) in enumerate(_other_chips(x, y)):
                src = _block_to_send(srcs[i], me, per_target)
                cp = pltpu.make_async_remote_copy(src_ref=src, dst_ref=lands[i].at[2 * tx + ty], send_sem=sem_refs[2 * (3 * i + k)],
                                                  recv_sem=sem_refs[2 * (3 * i + k) + 1], device_id=(tx, ty, c), device_id_type=MESH)
                cp.wait_send()
                cp.wait_recv()

    return pl.pallas_call(
        body, name=name,
        in_specs=[IN_HBM] * (2 * n) + [SEMAPHORES] * n_sems + [ANY],
        out_specs=[IN_HBM] * n,
        out_shape=[pltpu.HBM(l.shape, l.dtype) for l in landings],
        input_output_aliases={n + i: i for i in range(n)},
        compiler_params=pltpu.CompilerParams(has_side_effects=DATAFLOW),
    )(*[_hbm(s) for s in sources], *landings, *sems, after)


def _row_tile(rows, cols):
    tm = rows
    while tm * cols * 4 > (2 << 20) and tm % 16 == 0:
        tm //= 2
    return tm


def add_pair(a, b, out_dtype, copies=1):
    nb, rows, cols = a.shape
    tm = _row_tile(rows, cols)

    def body(a_ref, b_ref, *o_refs):
        total = (a_ref[...] + b_ref[...]).astype(out_dtype)
        for o_ref in o_refs:
            o_ref[...] = total

    spec = pl.BlockSpec((None, tm, cols), lambda j, i: (j, i, 0))
    outs = pl.pallas_call(
        body, name="add_pair", grid=(nb, rows // tm), in_specs=[spec, spec], out_specs=[spec] * copies,
        out_shape=[_sds(a.shape, out_dtype)] * copies, compiler_params=_params(("parallel", "parallel")),
    )(a, b)
    return outs[0] if copies == 1 else outs


def sum_four(a, token=None):
    _, rows, cols = a.shape
    tm = _row_tile(rows, cols)
    extra, extra_specs = _after(token)

    def body(a_ref, *rest):
        o_ref = rest[-1]
        o_ref[...] = ((a_ref[0].astype(F32) + a_ref[1].astype(F32)) + a_ref[2].astype(F32)) + a_ref[3].astype(F32)

    return pl.pallas_call(
        body, name="sum_four", grid=(rows // tm,),
        in_specs=[pl.BlockSpec((4, tm, cols), lambda i: (0, i, 0))] + extra_specs, out_specs=pl.BlockSpec((tm, cols), lambda i: (i, 0)),
        out_shape=_sds((rows, cols)), compiler_params=_params(("parallel",)),
    )(a, *extra)


def adamw(w, g, m, v):
    rows, cols = w.shape
    tm = _row_tile(rows, cols)
    c1 = 1.0 - ADAM_B1 ** ADAM_STEP
    c2 = 1.0 - ADAM_B2 ** ADAM_STEP

    def body(w_ref, g_ref, m_ref, v_ref, d_ref, nm_ref, nv_ref):
        g = g_ref[...]
        nm = ADAM_B1 * m_ref[...] + (1.0 - ADAM_B1) * g
        nv = ADAM_B2 * v_ref[...] + (1.0 - ADAM_B2) * (g * g)
        d_ref[...] = -ADAM_LR * ((nm / c1) / (jnp.sqrt(nv / c2) + ADAM_EPS) + ADAM_WD * w_ref[...])
        nm_ref[...] = nm
        nv_ref[...] = nv

    spec = pl.BlockSpec((tm, cols), lambda i: (i, 0))
    return pl.pallas_call(
        body, name="adamw", grid=(rows // tm,), in_specs=[spec] * 4, out_specs=[spec] * 3,
        out_shape=[_sds(w.shape)] * 3, compiler_params=_params(("parallel",)),
    )(w, g, m, v)


PACK_TILE = 8 * LANES
PACK_PIECES = 8
PACK_ALIGN = PACK_PIECES * 16


def _pack_small(values, scalar=None):
    parts = []
    for name in SMALL_NAMES:
        flat = values[name].reshape(-1)
        pad = (-flat.shape[0]) % PACK_TILE
        if pad:
            flat = jnp.concatenate([flat, jnp.zeros((pad,), F32)])
        parts.append(flat.reshape(-1, LANES))
    rows = sum(p.shape[0] for p in parts) + 8
    parts.append(jnp.zeros(((-rows) % PACK_ALIGN, LANES), F32))
    last = jnp.zeros((8, LANES), F32)
    parts.append(last if scalar is None else jnp.broadcast_to(scalar.astype(F32), (8, LANES)))
    return jnp.concatenate(parts, axis=0)


def _unpack_small(pack, like):
    out = {}
    row = 0
    for name in SMALL_NAMES:
        size = math.prod(like[name].shape)
        rows = -(-size // PACK_TILE) * 8
        out[name] = pack[row:row + rows].reshape(-1)[:size].reshape(like[name].shape)
        row += rows
    return out


def _is_column_sharded(name):
    return name.endswith("w_in")


def _to_blocks(name, full):
    if _is_column_sharded(name):
        rows, cols = full.shape
        return full.reshape(rows, 4, cols // 4).transpose(1, 0, 2)
    return full.reshape(4, full.shape[0] // 4, full.shape[1])


def _from_blocks(name, stacked):
    if _is_column_sharded(name):
        return stacked.transpose(1, 0, 2).reshape(stacked.shape[1], 4 * stacked.shape[2])
    return stacked.reshape(4 * stacked.shape[1], stacked.shape[2])


def _layer_big_names(i):
    return [n for n in BIG_NAMES if n.startswith("l%d_" % i)]


class _OverlappedExchanges:
    def __init__(self, weights):
        self.weights = weights
        self.c = lax.axis_index("c")
        first = _layer_big_names(0)
        self.later = [n for n in BIG_NAMES if n not in first]
        gathered = gather_weight_shards([weights[n].astype(MXU_DTYPE) for n in first])
        self.full = {n: _from_blocks(n, g) for n, g in zip(first, gathered)}
        shards = [weights[n].astype(MXU_DTYPE) for n in self.later]
        self.gather = (shards,) + start_sends_to_chips("gather_later_start", shards, place_own_blocks(shards), False, gathered[0])
        self.in_flight = {}
        self.contributions = {}

    def first_token(self):
        return self.gather[3]

    def layer(self, i, x):
        if i == 1:
            shards, sems, stacks, _ = self.gather
            stacks = wait_sends_to_chips("gather_later_wait", shards, stacks, sems, False, x)
            self.full.update({n: _from_blocks(n, g) for n, g in zip(self.later, stacks)})
        names = SSM_NAMES if i % 2 == 0 else ATTN_NAMES
        return {n: self.full.get("l%d_%s" % (i, n), self.weights.get("l%d_%s" % (i, n))) for n in names}

    def chip_sums(self, names, grads, extra_blocks=(), copies=1):
        blocks = [_to_blocks(n, grads[n]) for n in names] + list(extra_blocks)
        from_sibling = exchange_halves_with_sibling(blocks)
        sums = []
        for i, (b, r) in enumerate(zip(blocks, from_sibling)):
            half = b.shape[1] // 2
            mine = lax.dynamic_slice_in_dim(b, self.c * half, half, axis=1)
            sums.append(add_pair(mine, r, WIRE_DTYPE if i < len(names) else F32, copies))
        return sums

    def layer_done(self, i, grads, dx):
        if i + 1 in self.in_flight:
            names, sums, sems, landings = self.in_flight.pop(i + 1)
            done = wait_sends_to_chips("scatter_wait_l%d" % (i + 1), sums, landings, sems, True, dx)
            self.contributions.update(zip(names, done))
        if i == 0:
            return None
        names = _layer_big_names(i)
        pairs = self.chip_sums(names, grads, copies=2)
        sums = [p[0] for p in pairs]
        sems, landings, token = start_sends_to_chips("scatter_start_l%d" % i, sums, [p[1] for p in pairs], True, sums[0])
        self.in_flight[i] = (names, sums, sems, landings)
        return token


def _train_step(x, loss_target, weights, moments_m, moments_v):
    hooks = _OverlappedExchanges(weights)
    loss, dx, grads = _sequence_step(x[0], loss_target[0], weights["final_norm"], hooks)
    small_pack = _pack_small({n: grads[n] for n in SMALL_NAMES}, scalar=loss)
    last = _layer_big_names(0)
    pairs = hooks.chip_sums(last, grads, extra_blocks=[small_pack[None]], copies=2)
    sums = [p[0] for p in pairs]
    landings = [p[1] for p in pairs[:-1]] + [jnp.broadcast_to(sums[-1], (4,) + sums[-1].shape[1:])]
    sems, landings, token = start_sends_to_chips("scatter_start_l0", sums, landings, True, sums[0])
    out_grad, out_delta, out_m, out_v = {}, {}, {}, {}

    def finish(names, arrays, token=None):
        shared = swap_halves_with_sibling([sum_four(a, token) for a in arrays])
        for n, s in zip(names, shared):
            if n == "small":
                return s.reshape(-1, LANES)
            out_grad[n] = s.reshape(2 * s.shape[1], s.shape[2])
            out_delta[n], out_m[n], out_v[n] = adamw(weights[n], out_grad[n], moments_m[n], moments_v[n])

    others = [n for n in BIG_NAMES if n not in last]
    finish(others, [hooks.contributions[n] for n in others], token)
    arrived = wait_sends_to_chips("scatter_wait_l0", sums, landings, sems, True, out_v[others[-1]])
    small_grad_pack = finish(last + ["small"], arrived)
    loss = small_grad_pack[-8, 0]
    small_like = {n: weights[n] for n in SMALL_NAMES}
    d_pack, m_pack, v_pack = adamw(_pack_small(small_like), small_grad_pack, _pack_small({n: moments_m[n] for n in SMALL_NAMES}),
                                   _pack_small({n: moments_v[n] for n in SMALL_NAMES}))
    out_grad.update(_unpack_small(small_grad_pack, small_like))
    out_delta.update(_unpack_small(d_pack, small_like))
    out_m.update(_unpack_small(m_pack, small_like))
    out_v.update(_unpack_small(v_pack, small_like))
    outs = [loss, dx[None]]
    for group in (out_grad, out_delta, out_m, out_v):
        outs.extend(group[n] for n in WEIGHT_NAMES)
    return tuple(outs)


def kernel(x, l0_norm, l0_w_in, l0_a_re, l0_a_im, l0_log_step, l0_b_re, l0_b_im, l0_c_re, l0_c_im, l0_d, l0_w_glu, l0_b_glu, l0_w_out, l1_norm, l1_w_in, l1_sinks, l1_w_out, l2_norm, l2_w_in, l2_a_re, l2_a_im, l2_log_step, l2_b_re, l2_b_im, l2_c_re, l2_c_im, l2_d, l2_w_glu, l2_b_glu, l2_w_out, l3_norm, l3_w_in, l3_sinks, l3_w_out, final_norm, loss_target, m_l0_norm, m_l0_w_in, m_l0_a_re, m_l0_a_im, m_l0_log_step, m_l0_b_re, m_l0_b_im, m_l0_c_re, m_l0_c_im, m_l0_d, m_l0_w_glu, m_l0_b_glu, m_l0_w_out, m_l1_norm, m_l1_w_in, m_l1_sinks, m_l1_w_out, m_l2_norm, m_l2_w_in, m_l2_a_re, m_l2_a_im, m_l2_log_step, m_l2_b_re, m_l2_b_im, m_l2_c_re, m_l2_c_im, m_l2_d, m_l2_w_glu, m_l2_b_glu, m_l2_w_out, m_l3_norm, m_l3_w_in, m_l3_sinks, m_l3_w_out, m_final_norm, v_l0_norm, v_l0_w_in, v_l0_a_re, v_l0_a_im, v_l0_log_step, v_l0_b_re, v_l0_b_im, v_l0_c_re, v_l0_c_im, v_l0_d, v_l0_w_glu, v_l0_b_glu, v_l0_w_out, v_l1_norm, v_l1_w_in, v_l1_sinks, v_l1_w_out, v_l2_norm, v_l2_w_in, v_l2_a_re, v_l2_a_im, v_l2_log_step, v_l2_b_re, v_l2_b_im, v_l2_c_re, v_l2_c_im, v_l2_d, v_l2_w_glu, v_l2_b_glu, v_l2_w_out, v_l3_norm, v_l3_w_in, v_l3_sinks, v_l3_w_out, v_final_norm):
    args = locals()
    weights = {n: args[n] for n in WEIGHT_NAMES}
    moments_m = {n: args["m_" + n] for n in WEIGHT_NAMES}
    moments_v = {n: args["v_" + n] for n in WEIGHT_NAMES}
    return _train_step(x, loss_target, weights, moments_m, moments_v)
```

```python
import functools
import math

import jax
import jax.numpy as jnp
from jax import lax
from jax.experimental import pallas as pl
from jax.experimental.pallas import tpu as pltpu

F32 = jnp.float32
MXU_DTYPE = jnp.bfloat16
WIRE_DTYPE = jnp.bfloat16
MESH = pl.DeviceIdType.MESH

D_MODEL = 1024
BRANCH = 1024
NORM_EPS = 1e-5
SSM_GROUPS = 64
SSM_GROUP = 16
SSM_STATE = 64
S5_CHUNK = 16
LANES = 128
S5_OCT = LANES // SSM_GROUP
S5_OCTETS = SSM_GROUPS // S5_OCT
S5_OCT_IN = S5_CHUNK * LANES
S5_OCT_STATE = S5_OCT * SSM_STATE
S5_STATES = SSM_GROUPS * SSM_STATE
HEAD_DIM = 64
N_Q_HEADS = 16
N_KV_HEADS = 2
GQA_GROUP = N_Q_HEADS // N_KV_HEADS
ATTN_BLOCK = 128
Q_DIM = N_Q_HEADS * HEAD_DIM
KV_DIM = N_KV_HEADS * HEAD_DIM
ROPE_THETA = 10000.0
NEG_INF = -1e30
ADAM_LR = 0.001
ADAM_B1 = 0.9
ADAM_B2 = 0.999
ADAM_EPS = 1e-08
ADAM_WD = 0.01
ADAM_STEP = 10

VMEM_LIMIT_V7X = 56 * 1024 * 1024
ROW_TILE_FWD = 512
ROW_TILE_BWD = 512

SSM_NAMES = ("norm", "w_in", "a_re", "a_im", "log_step", "b_re", "b_im", "c_re", "c_im", "d", "w_glu", "b_glu", "w_out")
ATTN_NAMES = ("norm", "w_in", "sinks", "w_out")


def _weight_names():
    names = []
    for i in range(4):
        for n in (SSM_NAMES if i % 2 == 0 else ATTN_NAMES):
            names.append("l%d_%s" % (i, n))
    names.append("final_norm")
    return names


WEIGHT_NAMES = _weight_names()
BIG_NAMES = [n for n in WEIGHT_NAMES if n.endswith(("w_in", "w_glu", "w_out"))]
SMALL_NAMES = [n for n in WEIGHT_NAMES if n not in BIG_NAMES]


def _params(semantics=None):
    return pltpu.CompilerParams(dimension_semantics=semantics, vmem_limit_bytes=VMEM_LIMIT_V7X)


def _rows(tm, n):
    return pl.BlockSpec((tm, n), lambda i: (i, 0))


def _whole(shape):
    return pl.BlockSpec(shape, lambda i: (0,) * len(shape), pipeline_mode=pl.Buffered(1))


def _sds(shape, dtype=F32):
    return jax.ShapeDtypeStruct(shape, dtype)


def _mm(a, b):
    return jnp.dot(a.astype(MXU_DTYPE), b.astype(MXU_DTYPE), preferred_element_type=F32)


def _mm_tn(a, b):
    return lax.dot_general(a.astype(MXU_DTYPE), b.astype(MXU_DTYPE), (((0,), (0,)), ((), ())), preferred_element_type=F32)


def _mm_nt(a, b):
    return lax.dot_general(a.astype(MXU_DTYPE), b.astype(MXU_DTYPE), (((1,), (1,)), ((), ())), preferred_element_type=F32)


def _sigmoid(x):
    return 0.5 + 0.5 * jnp.tanh(0.5 * x)


def _silu(x):
    return x * _sigmoid(x)


def _silu_and_grad(x):
    s = _sigmoid(x)
    return x * s, s * (1.0 + x * (1.0 - s))


GELU_C0 = math.sqrt(2.0 / math.pi)
GELU_C1 = 0.044715


def _gelu(x):
    return 0.5 * x * (1.0 + jnp.tanh(GELU_C0 * (x + GELU_C1 * x * x * x)))


def _gelu_and_grad(x):
    x2 = x * x
    th = jnp.tanh(GELU_C0 * x * (1.0 + GELU_C1 * x2))
    half = 0.5 + 0.5 * th
    return x * half, half + 0.5 * x * (1.0 - th * th) * (GELU_C0 + 3.0 * GELU_C0 * GELU_C1 * x2)


def _rms(x, g):
    r = lax.rsqrt(jnp.mean(x * x, axis=-1, keepdims=True) + NORM_EPS)
    xhat = x * r
    return r, xhat, xhat * g


def _rms_bwd(dh, g, r, xhat):
    dxhat = dh * g
    dx = r * (dxhat - xhat * jnp.mean(dxhat * xhat, axis=-1, keepdims=True))
    return dx, jnp.sum(dh * xhat, axis=0, keepdims=True)


def _swap_half_heads(x):
    n = x.shape[-1]
    lane = lax.broadcasted_iota(jnp.int32, x.shape, x.ndim - 1)
    first = (lane % HEAD_DIM) < (HEAD_DIM // 2)
    return jnp.where(first, pltpu.roll(x, n - HEAD_DIM // 2, x.ndim - 1), pltpu.roll(x, HEAD_DIM // 2, x.ndim - 1))


def _tile_lanes(t, reps):
    return jnp.concatenate([t] * reps, axis=1)


TOKEN_SHAPE = (8, LANES)


def _after(token):
    return ([], []) if token is None else ([token], [_whole(TOKEN_SHAPE)])


def ssm_proj_fwd(x, norm, w_in, token=None):
    t = x.shape[0]
    tm = min(ROW_TILE_FWD, t)
    extra, extra_specs = _after(token)

    def body(x_ref, g_ref, w_ref, *rest):
        u_ref, gate_ref = rest[-2:]
        _, _, h = _rms(x_ref[...], g_ref[...])
        h = h.astype(MXU_DTYPE)
        half = BRANCH // 2
        for j in range(2):
            u_ref[:, j * half:(j + 1) * half] = _mm(h, w_ref[j])
            gate_ref[:, j * half:(j + 1) * half] = _mm(h, w_ref[2 + j])

    return pl.pallas_call(
        body, name="ssm_proj_fwd", grid=(t // tm,),
        in_specs=[_rows(tm, D_MODEL), _whole((1, D_MODEL)), _whole((4, D_MODEL, BRANCH // 2))] + extra_specs,
        out_specs=[_rows(tm, BRANCH), _rows(tm, BRANCH)],
        out_shape=[_sds((t, BRANCH)), _sds((t, BRANCH))],
        compiler_params=_params(("parallel",)),
    )(x, norm, w_in, *extra)


def _chunk_rows(ref, nk, dtype=None):
    rows = jnp.concatenate([ref[pl.ds(s, nk, stride=S5_CHUNK), :] for s in range(S5_CHUNK)], axis=1)
    return rows.astype(MXU_DTYPE if dtype is None else dtype)


def _store_chunk_rows(ref, val, nk):
    for s in range(S5_CHUNK):
        ref[pl.ds(s, nk, stride=S5_CHUNK), :] = val[:, s * LANES:(s + 1) * LANES]


def _own_group_mask():
    row = lax.broadcasted_iota(jnp.int32, (S5_OCT_IN, S5_OCT_STATE), 0)
    col = lax.broadcasted_iota(jnp.int32, (S5_OCT_IN, S5_OCT_STATE), 1)
    return ((row % LANES) // SSM_GROUP) == (col // SSM_STATE)


def _spread_groups(w):
    return jnp.where(_own_group_mask(), jnp.concatenate([w] * (S5_OCT_STATE // LANES), axis=1), 0.0).astype(MXU_DTYPE)


def _fold_groups(p):
    p = jnp.where(_own_group_mask(), p, 0.0)
    return sum(p[:, q * LANES:(q + 1) * LANES] for q in range(S5_OCT_STATE // LANES))


def _fill_toeplitz(win_ref, kd_ref):
    win_ref[...] = jnp.zeros_like(win_ref)
    for s in range(S5_CHUNK):
        for t in range(s, S5_CHUNK):
            win_ref[s * LANES:(s + 1) * LANES, t * LANES:(t + 1) * LANES] = kd_ref[t - s].astype(MXU_DTYPE)


TOEPLITZ_BLOCK = 512
_TOEPLITZ_BLOCKS = [(lo, lo + TOEPLITZ_BLOCK) for lo in range(0, S5_OCT_IN, TOEPLITZ_BLOCK)]


def _strip(t):
    return pl.BlockSpec((t, LANES), lambda b: (0, b))


def _oct_states(nk):
    return pl.BlockSpec((nk, S5_OCT_STATE), lambda b: (0, b))


OCT_W = pl.BlockSpec((None, S5_OCT_IN, LANES), lambda b: (b, 0, 0))
OCT_KD = pl.BlockSpec((None, S5_CHUNK, LANES, LANES), lambda b: (b, 0, 0, 0))


def s5_chunk_states(u, ws_re, ws_im):
    t = u.shape[0]
    nk = t // S5_CHUNK

    def body(u_ref, wr_ref, wi_ref, re_ref, im_ref):
        uc = _chunk_rows(u_ref, nk)
        re_ref[...] = _mm(uc, _spread_groups(wr_ref[...]))
        im_ref[...] = _mm(uc, _spread_groups(wi_ref[...]))

    return pl.pallas_call(
        body, name="s5_chunk_states", grid=(S5_OCTETS,),
        in_specs=[_strip(t), OCT_W, OCT_W], out_specs=[_oct_states(nk), _oct_states(nk)],
        out_shape=[_sds((nk, S5_STATES)), _sds((nk, S5_STATES))],
        compiler_params=_params(("parallel",)),
    )(u, ws_re, ws_im)


def s5_scan_fwd(s_re, s_im, a_re, a_im):
    nk = s_re.shape[0]

    def body(sre_ref, sim_ref, ar_ref, ai_ref, hre_ref, him_ref):
        ar = ar_ref[...]
        ai = ai_ref[...]

        def step(k, carry):
            hr, hi = carry
            hre_ref[pl.ds(k, 1), :] = hr
            him_ref[pl.ds(k, 1), :] = hi
            sr = sre_ref[pl.ds(k, 1), :]
            si = sim_ref[pl.ds(k, 1), :]
            return ar * hr - ai * hi + sr, ai * hr + ar * hi + si

        zero = jnp.zeros((1, S5_STATES), F32)
        lax.fori_loop(0, nk, step, (zero, zero))

    vm = pl.BlockSpec(memory_space=pltpu.VMEM)
    return pl.pallas_call(
        body, name="s5_scan_fwd", in_specs=[vm, vm, vm, vm], out_specs=[vm, vm],
        out_shape=[_sds((nk, S5_STATES)), _sds((nk, S5_STATES))],
        compiler_params=_params(),
    )(s_re, s_im, a_re, a_im)


def s5_outputs(u, h_re, h_im, kd, wo_re, wo_im):
    t = u.shape[0]
    nk = t // S5_CHUNK

    def body(u_ref, hre_ref, him_ref, kd_ref, wor_ref, woi_ref, y_ref, win_ref):
        _fill_toeplitz(win_ref, kd_ref)
        uc = _chunk_rows(u_ref, nk)
        y = jnp.concatenate([_mm(uc[:, :hi], win_ref[:hi, lo:hi]) for lo, hi in _TOEPLITZ_BLOCKS], axis=1)
        y = y + _mm_nt(hre_ref[...], _spread_groups(wor_ref[...])) + _mm_nt(him_ref[...], _spread_groups(woi_ref[...]))
        _store_chunk_rows(y_ref, y, nk)

    return pl.pallas_call(
        body, name="s5_outputs", grid=(S5_OCTETS,),
        in_specs=[_strip(t), _oct_states(nk), _oct_states(nk), OCT_KD, OCT_W, OCT_W],
        out_specs=_strip(t), out_shape=_sds((t, BRANCH)),
        scratch_shapes=[pltpu.VMEM((S5_OCT_IN, S5_OCT_IN), MXU_DTYPE)],
        compiler_params=_params(("parallel",)),
    )(u, h_re, h_im, kd, wo_re, wo_im)


def s5_state_grads(dy, wo_re, wo_im):
    t = dy.shape[0]
    nk = t // S5_CHUNK

    def body(dy_ref, wor_ref, woi_ref, re_ref, im_ref):
        dyc = _chunk_rows(dy_ref, nk)
        re_ref[...] = _mm(dyc, _spread_groups(wor_ref[...]))
        im_ref[...] = _mm(dyc, _spread_groups(woi_ref[...]))

    return pl.pallas_call(
        body, name="s5_state_grads", grid=(S5_OCTETS,),
        in_specs=[_strip(t), OCT_W, OCT_W], out_specs=[_oct_states(nk), _oct_states(nk)],
        out_shape=[_sds((nk, S5_STATES)), _sds((nk, S5_STATES))],
        compiler_params=_params(("parallel",)),
    )(dy, wo_re, wo_im)


def s5_scan_bwd(dh_re, dh_im, h_re, h_im, a_re, a_im):
    nk = dh_re.shape[0]

    def body(dhr_ref, dhi_ref, hr_ref, hi_ref, ar_ref, ai_ref, dsr_ref, dsi_ref, dar_ref, dai_ref):
        ar = ar_ref[...]
        ai = ai_ref[...]

        dar_ref[...] = jnp.zeros_like(dar_ref)
        dai_ref[...] = jnp.zeros_like(dai_ref)

        def step(i, carry):
            gr, gi = carry
            k = nk - 1 - i
            dhr = dhr_ref[pl.ds(k, 1), :]
            dhi = dhi_ref[pl.ds(k, 1), :]
            dsr_ref[pl.ds(k, 1), :] = gr
            dsi_ref[pl.ds(k, 1), :] = gi
            hr = hr_ref[pl.ds(k, 1), :]
            hi = hi_ref[pl.ds(k, 1), :]
            dar_ref[...] += gr * hr + gi * hi
            dai_ref[...] += gi * hr - gr * hi
            return dhr + ar * gr + ai * gi, dhi - ai * gr + ar * gi

        zero = jnp.zeros((1, S5_STATES), F32)
        lax.fori_loop(0, nk, step, (zero, zero))

    vm = pl.BlockSpec(memory_space=pltpu.VMEM)
    return pl.pallas_call(
        body, name="s5_scan_bwd", in_specs=[vm] * 6, out_specs=[vm] * 4,
        out_shape=[_sds((nk, S5_STATES)), _sds((nk, S5_STATES)), _sds((1, S5_STATES)), _sds((1, S5_STATES))],
        input_output_aliases={0: 0, 1: 1}, compiler_params=_params(),
    )(dh_re, dh_im, h_re, h_im, a_re, a_im)


def s5_input_grads(dy, ds_re, ds_im, kd, ws_re, ws_im):
    t = dy.shape[0]
    nk = t // S5_CHUNK

    def body(dy_ref, dsr_ref, dsi_ref, kd_ref, wsr_ref, wsi_ref, du_ref, win_ref):
        _fill_toeplitz(win_ref, kd_ref)
        dyc = _chunk_rows(dy_ref, nk)
        du = jnp.concatenate([_mm_nt(dyc[:, lo:], win_ref[lo:hi, lo:]) for lo, hi in _TOEPLITZ_BLOCKS], axis=1)
        du = du + _mm_nt(dsr_ref[...], _spread_groups(wsr_ref[...])) + _mm_nt(dsi_ref[...], _spread_groups(wsi_ref[...]))
        _store_chunk_rows(du_ref, du, nk)

    return pl.pallas_call(
        body, name="s5_input_grads", grid=(S5_OCTETS,),
        in_specs=[_strip(t), _oct_states(nk), _oct_states(nk), OCT_KD, OCT_W, OCT_W],
        out_specs=_strip(t), out_shape=_sds((t, BRANCH)),
        scratch_shapes=[pltpu.VMEM((S5_OCT_IN, S5_OCT_IN), MXU_DTYPE)],
        compiler_params=_params(("parallel",)),
    )(dy, ds_re, ds_im, kd, ws_re, ws_im)


def s5_weight_grads(u, dy, h_re, h_im, ds_re, ds_im):
    t = u.shape[0]
    nk = t // S5_CHUNK

    def body(u_ref, dy_ref, hre_ref, him_ref, dsr_ref, dsi_ref, dkd_ref, dwsr_ref, dwsi_ref, dwor_ref, dwoi_ref):
        dyc = _chunk_rows(dy_ref, nk, F32)
        uct = _chunk_rows(u_ref, nk, F32).T.astype(MXU_DTYPE)
        dyct = dyc.T.astype(MXU_DTYPE)
        dyc = dyc.astype(MXU_DTYPE)
        dwsr_ref[...] = _fold_groups(_mm(uct, dsr_ref[...]))
        dwsi_ref[...] = _fold_groups(_mm(uct, dsi_ref[...]))
        dwor_ref[...] = _fold_groups(_mm(dyct, hre_ref[...]))
        dwoi_ref[...] = _fold_groups(_mm(dyct, him_ref[...]))
        dkd_ref[...] = jnp.zeros_like(dkd_ref)
        for tt in range(0, S5_CHUNK, 2):
            p = _mm(uct[:(tt + 2) * LANES], dyc[:, tt * LANES:(tt + 2) * LANES])
            for s in range(tt + 2):
                rows = p[s * LANES:(s + 1) * LANES]
                if s <= tt:
                    dkd_ref[tt - s] += rows[:, :LANES]
                dkd_ref[tt + 1 - s] += rows[:, LANES:]

    return pl.pallas_call(
        body, name="s5_weight_grads", grid=(S5_OCTETS,),
        in_specs=[_strip(t), _strip(t)] + [_oct_states(nk)] * 4,
        out_specs=[OCT_KD, OCT_W, OCT_W, OCT_W, OCT_W],
        out_shape=[_sds((S5_OCTETS, S5_CHUNK, LANES, LANES))] + [_sds((S5_OCTETS, S5_OCT_IN, LANES))] * 4,
        compiler_params=_params(("parallel",)),
    )(u, dy, h_re, h_im, ds_re, ds_im)


def ssm_mix_fwd(x, u, gate, y_scan, d, w_glu, b_glu, w_out):
    t = x.shape[0]
    tm = min(ROW_TILE_FWD, t)

    def body(x_ref, u_ref, gate_ref, ys_ref, d_ref, wg_ref, bg_ref, wo_ref, y_ref, g2_ref, xo_ref):
        y = ys_ref[...] + d_ref[...] * u_ref[...]
        z0 = _gelu(y)
        g2 = _mm(z0, wg_ref[...]) + bg_ref[...]
        a = z0 * _sigmoid(g2) * _silu(gate_ref[...])
        y_ref[...] = y
        g2_ref[...] = g2
        xo_ref[...] = x_ref[...] + _mm(a, wo_ref[...])

    row = _rows(tm, BRANCH)
    vec = _whole((1, BRANCH))
    mat = _whole((BRANCH, BRANCH))
    return pl.pallas_call(
        body, name="ssm_mix_fwd", grid=(t // tm,),
        in_specs=[row, row, row, row, vec, mat, vec, mat],
        out_specs=[row, row, row],
        out_shape=[_sds((t, BRANCH))] * 3,
        compiler_params=_params(("parallel",)),
    )(x, u, gate, y_scan, d, w_glu, b_glu, w_out)


def ssm_mix_bwd(dxo, u, gate, y, g2, w_glu, w_out, token=None):
    t = dxo.shape[0]
    tm = min(ROW_TILE_BWD, t)
    extra, extra_specs = _after(token)

    def body(dxo_ref, u_ref, gate_ref, y_ref, g2_ref, wgt_ref, wot_ref, *rest):
        dy_ref, dgate_ref, dwo_ref, dwg_ref, dbg_ref, dd_ref = rest[-6:]

        @pl.when(pl.program_id(0) == 0)
        def _():
            dwo_ref[...] = jnp.zeros_like(dwo_ref)
            dwg_ref[...] = jnp.zeros_like(dwg_ref)
            dbg_ref[...] = jnp.zeros_like(dbg_ref)
            dd_ref[...] = jnp.zeros_like(dd_ref)

        dxo = dxo_ref[...]
        gate = gate_ref[...]
        y = y_ref[...]
        z0, z0_grad = _gelu_and_grad(y)
        sg = _sigmoid(g2_ref[...])
        z = z0 * sg
        sgate, sgate_grad = _silu_and_grad(gate)
        da = _mm_nt(dxo, wot_ref[...])
        dwo_ref[...] += _mm_tn(z * sgate, dxo)
        dz = da * sgate
        dgate_ref[...] = da * z * sgate_grad
        dg2 = dz * z0 * sg * (1.0 - sg)
        dbg_ref[...] += jnp.sum(dg2, axis=0, keepdims=True)
        dwg_ref[...] += _mm_tn(z0, dg2)
        dz0 = dz * sg + _mm_nt(dg2, wgt_ref[...])
        dy = dz0 * z0_grad
        dd_ref[...] += jnp.sum(dy * u_ref[...], axis=0, keepdims=True)
        dy_ref[...] = dy

    row = _rows(tm, BRANCH)
    vec = _whole((1, BRANCH))
    mat = _whole((BRANCH, BRANCH))
    return pl.pallas_call(
        body, name="ssm_mix_bwd", grid=(t // tm,),
        in_specs=[row, row, row, row, row, mat, mat] + extra_specs,
        out_specs=[row, row, mat, mat, vec, vec],
        out_shape=[_sds((t, BRANCH)), _sds((t, BRANCH)), _sds((BRANCH, D_MODEL)), _sds((BRANCH, BRANCH)),
                   _sds((1, BRANCH)), _sds((1, BRANCH))],
        compiler_params=_params(("arbitrary",)),
    )(dxo, u, gate, y, g2, w_glu, w_out, *extra)


def ssm_proj_bwd(x, norm, dxo, dy, du_scan, dgate, d, w_in):
    t = x.shape[0]
    tm = min(ROW_TILE_BWD, t)
    n = 2 * BRANCH

    def body(x_ref, g_ref, dxo_ref, dy_ref, dus_ref, dgate_ref, d_ref, wt_ref, dx_ref, dw_ref, dg_ref):
        @pl.when(pl.program_id(0) == 0)
        def _():
            dw_ref[...] = jnp.zeros_like(dw_ref)
            dg_ref[...] = jnp.zeros_like(dg_ref)

        g = g_ref[...]
        r, xhat, h = _rms(x_ref[...], g)
        h = h.astype(MXU_DTYPE)
        du = dus_ref[...] + d_ref[...] * dy_ref[...]
        dproj = jnp.concatenate([du, dgate_ref[...]], axis=1).astype(MXU_DTYPE)
        dh = jnp.zeros((tm, D_MODEL), F32)
        for j in range(4):
            cols = dproj[:, j * (n // 4):(j + 1) * (n // 4)]
            dh = dh + _mm_nt(cols, wt_ref[j])
            dw_ref[j] += _mm_tn(h, cols)
        dx, dg = _rms_bwd(dh, g, r, xhat)
        dg_ref[...] += dg
        dx_ref[...] = dxo_ref[...] + dx

    row = _rows(tm, D_MODEL)
    vec = _whole((1, D_MODEL))
    blocks = _whole((4, D_MODEL, n // 4))
    return pl.pallas_call(
        body, name="ssm_proj_bwd", grid=(t // tm,),
        in_specs=[row, vec, row, row, row, row, vec, blocks],
        out_specs=[row, blocks, vec],
        out_shape=[_sds((t, D_MODEL)), _sds((4, D_MODEL, n // 4)), _sds((1, D_MODEL))],
        compiler_params=_params(("arbitrary",)),
    )(x, norm, dxo, dy, du_scan, dgate, d, w_in)


ATTN_N = Q_DIM + 2 * KV_DIM + BRANCH


def attn_proj_fwd(x, norm, w_in, cos2, sin2):
    t = x.shape[0]
    tm = min(ROW_TILE_FWD, t)

    def body(x_ref, g_ref, w_ref, cos_ref, sin_ref, q_ref, k_ref, v_ref, gate_ref):
        _, _, h = _rms(x_ref[...], g_ref[...])
        p = _mm(h, w_ref[...])
        cs = cos_ref[...]
        sn = sin_ref[...]
        q = p[:, :Q_DIM]
        k = p[:, Q_DIM:Q_DIM + KV_DIM]
        q_ref[...] = q * _tile_lanes(cs, Q_DIM // LANES) + _swap_half_heads(q) * _tile_lanes(sn, Q_DIM // LANES)
        k_ref[...] = k * cs + _swap_half_heads(k) * sn
        v_ref[...] = p[:, Q_DIM + KV_DIM:Q_DIM + 2 * KV_DIM]
        gate_ref[...] = p[:, Q_DIM + 2 * KV_DIM:]

    return pl.pallas_call(
        body, name="attn_proj_fwd", grid=(t // tm,),
        in_specs=[_rows(tm, D_MODEL), _whole((1, D_MODEL)), _whole((D_MODEL, ATTN_N)), _rows(tm, LANES), _rows(tm, LANES)],
        out_specs=[_rows(tm, Q_DIM), _rows(tm, KV_DIM), _rows(tm, KV_DIM), _rows(tm, BRANCH)],
        out_shape=[_sds((t, Q_DIM)), _sds((t, KV_DIM)), _sds((t, KV_DIM)), _sds((t, BRANCH))],
        compiler_params=_params(("parallel",)),
    )(x, norm, w_in, cos2, sin2)


GQA_LANES = GQA_GROUP * ATTN_BLOCK


def _window_masks(first_block):
    kj = lax.broadcasted_iota(jnp.int32, (ATTN_BLOCK, GQA_LANES), 0)
    qi = lax.broadcasted_iota(jnp.int32, (ATTN_BLOCK, GQA_LANES), 1) % ATTN_BLOCK
    return kj > qi, kj > jnp.where(first_block, qi, ATTN_BLOCK)


def _fold(upper, both):
    return jnp.where(upper, both[:ATTN_BLOCK], both[ATTN_BLOCK:])


def _unfold(upper, tile):
    return jnp.concatenate([jnp.where(upper, tile, 0.0), jnp.where(upper, 0.0, tile)], axis=0).astype(MXU_DTYPE)


def _stack_heads(ref, group):
    return jnp.concatenate([ref[:, h * HEAD_DIM:(h + 1) * HEAD_DIM] for h in range(group * GQA_GROUP, (group + 1) * GQA_GROUP)], axis=0)


def _unstack_heads(ref, group, stacked):
    for n in range(GQA_GROUP):
        h = group * GQA_GROUP + n
        ref[:, h * HEAD_DIM:(h + 1) * HEAD_DIM] = stacked[n * ATTN_BLOCK:(n + 1) * ATTN_BLOCK]


def _sink_row(sink_ref, group):
    return jnp.concatenate([jnp.full((1, ATTN_BLOCK), sink_ref[group * GQA_GROUP + n], F32) for n in range(GQA_GROUP)], axis=1)


def _lane_is(h):
    return lax.broadcasted_iota(jnp.int32, (1, LANES), 1) == h


def attn_fwd(q, k, v, sinks):
    t = q.shape[0]
    nb = t // ATTN_BLOCK
    scale = HEAD_DIM ** -0.5

    def body(sink_ref, q_ref, kc_ref, kp_ref, vc_ref, vp_ref, o_ref, lse_ref):
        keys = jnp.concatenate([kp_ref[...], kc_ref[...]], axis=0).astype(MXU_DTYPE)
        vals = jnp.concatenate([vp_ref[...], vc_ref[...]], axis=0).astype(MXU_DTYPE)
        upper, dead = _window_masks(pl.program_id(0) == 0)
        for g in range(N_KV_HEADS):
            kv = slice(g * HEAD_DIM, (g + 1) * HEAD_DIM)
            qs = _stack_heads(q_ref, g) * scale
            s = jnp.where(dead, NEG_INF, _fold(upper, _mm_nt(keys[:, kv], qs)))
            sink = _sink_row(sink_ref, g)
            m = jnp.maximum(jnp.max(s, axis=0, keepdims=True), sink)
            p = jnp.exp(s - m)
            den = jnp.sum(p, axis=0, keepdims=True) + jnp.exp(sink - m)
            _unstack_heads(o_ref, g, _mm_tn(_unfold(upper, p * (1.0 / den)), vals[:, kv]))
            lse = m + jnp.log(den)
            for n in range(GQA_GROUP):
                lse_ref[pl.ds(g * GQA_GROUP + n, 1), :] = lse[:, n * ATTN_BLOCK:(n + 1) * ATTN_BLOCK]

    cur = lambda n: pl.BlockSpec((ATTN_BLOCK, n), lambda i: (i, 0))
    prev = lambda n: pl.BlockSpec((ATTN_BLOCK, n), lambda i: (jnp.maximum(i - 1, 0), 0))
    return pl.pallas_call(
        body, name="attn_fwd", grid=(nb,),
        in_specs=[pl.BlockSpec(memory_space=pltpu.SMEM), cur(Q_DIM), cur(KV_DIM), prev(KV_DIM), cur(KV_DIM), prev(KV_DIM)],
        out_specs=[cur(Q_DIM), pl.BlockSpec((N_Q_HEADS, ATTN_BLOCK), lambda i: (0, i))],
        out_shape=[_sds((t, Q_DIM)), _sds((N_Q_HEADS, t))],
        compiler_params=_params(("parallel",)),
    )(sinks, q, k, k, v, v)


def attn_bwd(q, k, v, sinks, o, lse, do):
    t = q.shape[0]
    nb = t // ATTN_BLOCK
    scale = HEAD_DIM ** -0.5

    def body(sink_ref, q_ref, kc_ref, kp_ref, vc_ref, vp_ref, o_ref, lse_ref, do_ref,
             dq_ref, dk_ref, dv_ref, dsink_ref, dk_carry, dv_carry):
        i = pl.program_id(0)

        @pl.when(i == 0)
        def _():
            dsink_ref[...] = jnp.zeros_like(dsink_ref)
            dk_carry[...] = jnp.zeros_like(dk_carry)
            dv_carry[...] = jnp.zeros_like(dv_carry)

        @pl.when(i < nb)
        def _():
            keys = jnp.concatenate([kp_ref[...], kc_ref[...]], axis=0).astype(MXU_DTYPE)
            vals = jnp.concatenate([vp_ref[...], vc_ref[...]], axis=0).astype(MXU_DTYPE)
            upper, dead = _window_masks(i == 0)
            dsink = jnp.zeros((1, LANES), F32)
            dk_heads = []
            dv_heads = []
            for g in range(N_KV_HEADS):
                kv = slice(g * HEAD_DIM, (g + 1) * HEAD_DIM)
                qs = (_stack_heads(q_ref, g) * scale).astype(MXU_DTYPE)
                dos = _stack_heads(do_ref, g)
                lse = jnp.concatenate([lse_ref[pl.ds(g * GQA_GROUP + n, 1), :] for n in range(GQA_GROUP)], axis=1)
                s = jnp.where(dead, NEG_INF, _fold(upper, _mm_nt(keys[:, kv], qs)))
                p = jnp.exp(s - lse)
                delta = _mm_f32(jnp.ones((8, HEAD_DIM), F32), dos * _stack_heads(o_ref, g), ((1,), (1,)))[:1]
                dos = dos.astype(MXU_DTYPE)
                ds = _unfold(upper, p * (_fold(upper, _mm_nt(vals[:, kv], dos)) - delta))
                _unstack_heads(dq_ref, g, _mm_tn(ds, keys[:, kv]) * scale)
                dk_heads.append(_mm(ds, qs))
                dv_heads.append(_mm(_unfold(upper, p), dos))
                at_sink = jnp.exp(_sink_row(sink_ref, g) - lse) * delta
                for n in range(GQA_GROUP):
                    dsink = dsink + jnp.where(_lane_is(g * GQA_GROUP + n), -jnp.sum(at_sink[:, n * ATTN_BLOCK:(n + 1) * ATTN_BLOCK]), 0.0)
            dkk = jnp.concatenate(dk_heads, axis=1)
            dvv = jnp.concatenate(dv_heads, axis=1)
            dsink_ref[...] += dsink
            dk_ref[...] = dk_carry[...] + dkk[:ATTN_BLOCK]
            dv_ref[...] = dv_carry[...] + dvv[:ATTN_BLOCK]
            dk_carry[...] = dkk[ATTN_BLOCK:]
            dv_carry[...] = dvv[ATTN_BLOCK:]

        @pl.when(i == nb)
        def _():
            dk_ref[...] = dk_carry[...]
            dv_ref[...] = dv_carry[...]

    last = nb - 1
    cur = lambda n: pl.BlockSpec((ATTN_BLOCK, n), lambda i: (jnp.minimum(i, last), 0))
    prev = lambda n: pl.BlockSpec((ATTN_BLOCK, n), lambda i: (jnp.clip(i - 1, 0, last), 0))
    late = lambda n: pl.BlockSpec((ATTN_BLOCK, n), lambda i: (i, 0))
    dq, dk_late, dv_late, dsinks = pl.pallas_call(
        body, name="attn_bwd", grid=(nb + 1,),
        in_specs=[pl.BlockSpec(memory_space=pltpu.SMEM), cur(Q_DIM), cur(KV_DIM), prev(KV_DIM), cur(KV_DIM), prev(KV_DIM),
                  cur(Q_DIM), pl.BlockSpec((N_Q_HEADS, ATTN_BLOCK), lambda i: (0, jnp.minimum(i, last))), cur(Q_DIM)],
        out_specs=[cur(Q_DIM), late(KV_DIM), late(KV_DIM), _whole((1, LANES))],
        out_shape=[_sds((t, Q_DIM)), _sds((t + ATTN_BLOCK, KV_DIM)), _sds((t + ATTN_BLOCK, KV_DIM)), _sds((1, LANES))],
        scratch_shapes=[pltpu.VMEM((ATTN_BLOCK, KV_DIM), F32), pltpu.VMEM((ATTN_BLOCK, KV_DIM), F32)],
        compiler_params=_params(("arbitrary",)),
    )(sinks, q, k, k, v, v, o, lse, do)
    return dq, dk_late[ATTN_BLOCK:], dv_late[ATTN_BLOCK:], dsinks


def attn_out_fwd(x, o, gate, w_out):
    t = x.shape[0]
    tm = min(ROW_TILE_FWD, t)

    def body(x_ref, o_ref, gate_ref, w_ref, xo_ref):
        xo_ref[...] = x_ref[...] + _mm(o_ref[...] * _silu(gate_ref[...]), w_ref[...])

    row = _rows(tm, D_MODEL)
    return pl.pallas_call(
        body, name="attn_out_fwd", grid=(t // tm,),
        in_specs=[row, row, row, _whole((Q_DIM, D_MODEL))], out_specs=row, out_shape=_sds((t, D_MODEL)),
        compiler_params=_params(("parallel",)),
    )(x, o, gate, w_out)


def attn_out_bwd(dxo, o, gate, w_out, token=None):
    t = dxo.shape[0]
    tm = min(ROW_TILE_BWD, t)
    extra, extra_specs = _after(token)

    def body(dxo_ref, o_ref, gate_ref, wt_ref, *rest):
        do_ref, dgate_ref, dw_ref = rest[-3:]

        @pl.when(pl.program_id(0) == 0)
        def _():
            dw_ref[...] = jnp.zeros_like(dw_ref)

        dxo = dxo_ref[...]
        o = o_ref[...]
        gate = gate_ref[...]
        sgate, sgate_grad = _silu_and_grad(gate)
        da = _mm_nt(dxo, wt_ref[...])
        dw_ref[...] += _mm_tn(o * sgate, dxo)
        do_ref[...] = da * sgate
        dgate_ref[...] = da * o * sgate_grad

    row = _rows(tm, D_MODEL)
    mat = _whole((Q_DIM, D_MODEL))
    return pl.pallas_call(
        body, name="attn_out_bwd", grid=(t // tm,),
        in_specs=[row, row, row, mat] + extra_specs, out_specs=[row, row, mat],
        out_shape=[_sds((t, Q_DIM)), _sds((t, BRANCH)), _sds((Q_DIM, D_MODEL))],
        compiler_params=_params(("arbitrary",)),
    )(dxo, o, gate, w_out, *extra)


def attn_proj_bwd(x, norm, dxo, dq, dk, dv, dgate, cos2, sin2, w_in):
    t = x.shape[0]
    tm = min(ROW_TILE_BWD, t)

    def body(x_ref, g_ref, dxo_ref, dq_ref, dk_ref, dv_ref, dgate_ref, cos_ref, sin_ref, wt_ref, dx_ref, dw_ref, dg_ref):
        @pl.when(pl.program_id(0) == 0)
        def _():
            dw_ref[...] = jnp.zeros_like(dw_ref)
            dg_ref[...] = jnp.zeros_like(dg_ref)

        g = g_ref[...]
        r, xhat, h = _rms(x_ref[...], g)
        cs = cos_ref[...]
        sn = sin_ref[...]
        dqr = dq_ref[...]
        dkr = dk_ref[...]
        dq = dqr * _tile_lanes(cs, Q_DIM // LANES) + _swap_half_heads(dqr * _tile_lanes(sn, Q_DIM // LANES))
        dk = dkr * cs + _swap_half_heads(dkr * sn)
        dproj = jnp.concatenate([dq, dk, dv_ref[...], dgate_ref[...]], axis=1)
        dh = _mm_nt(dproj, wt_ref[...])
        dw_ref[...] += _mm_tn(h, dproj)
        dx, dg = _rms_bwd(dh, g, r, xhat)
        dg_ref[...] += dg
        dx_ref[...] = dxo_ref[...] + dx

    row = _rows(tm, D_MODEL)
    vec = _whole((1, D_MODEL))
    return pl.pallas_call(
        body, name="attn_proj_bwd", grid=(t // tm,),
        in_specs=[row, vec, row, _rows(tm, Q_DIM), _rows(tm, KV_DIM), _rows(tm, KV_DIM), _rows(tm, BRANCH),
                  _rows(tm, LANES), _rows(tm, LANES), _whole((D_MODEL, ATTN_N))],
        out_specs=[row, _whole((D_MODEL, ATTN_N)), vec],
        out_shape=[_sds((t, D_MODEL)), _sds((D_MODEL, ATTN_N)), _sds((1, D_MODEL))],
        compiler_params=_params(("arbitrary",)),
    )(x, norm, dxo, dq, dk, dv, dgate, cos2, sin2, w_in)


def loss_head(x, norm, target):
    t = x.shape[0]
    tm = min(ROW_TILE_FWD, t)

    def body(x_ref, g_ref, tgt_ref, loss_ref, dx_ref, dg_ref):
        @pl.when(pl.program_id(0) == 0)
        def _():
            loss_ref[...] = jnp.zeros_like(loss_ref)
            dg_ref[...] = jnp.zeros_like(dg_ref)

        g = g_ref[...]
        r, xhat, y = _rms(x_ref[...], g)
        err = y - tgt_ref[...]
        loss_ref[...] += 0.5 * jnp.sum(jnp.mean(err * err, axis=-1, keepdims=True), axis=0, keepdims=True)
        dx, dg = _rms_bwd(err * (1.0 / D_MODEL), g, r, xhat)
        dg_ref[...] += dg
        dx_ref[...] = dx

    row = _rows(tm, D_MODEL)
    vec = _whole((1, D_MODEL))
    return pl.pallas_call(
        body, name="loss_head", grid=(t // tm,),
        in_specs=[row, vec, row], out_specs=[_whole((1, 1)), row, vec],
        out_shape=[_sds((1, 1)), _sds((t, D_MODEL)), _sds((1, D_MODEL))],
        compiler_params=_params(("arbitrary",)),
    )(x, norm, target)


OCT_TILE = pl.BlockSpec((None, LANES, LANES), lambda b: (b, 0, 0))
N_LAGS = S5_CHUNK + 1


def _cmul(ar, ai, br, bi):
    return ar * br - ai * bi, ar * bi + ai * br


def _cmul_conj(ar, ai, br, bi):
    return ar * br + ai * bi, ar * bi - ai * br


def _mm_f32(a, b, dims):
    return lax.dot_general(a, b, (dims, ((), ())), precision=lax.Precision.HIGHEST, preferred_element_type=F32)


def _s5_discretise(ar, ai, ls, br, bi):
    dt = jnp.exp(ls)
    xr = ar * dt
    xi = ai * dt
    mag = jnp.exp(xr)
    first = (mag * jnp.cos(xi), mag * jnp.sin(xi))
    powers = [(jnp.ones_like(xr), jnp.zeros_like(xr)), first]
    for _ in range(2, N_LAGS):
        powers.append(_cmul(*powers[-1], *first))
    den = ar * ar + ai * ai
    nr = powers[1][0] - 1.0
    ni = powers[1][1]
    fr = (nr * ar + ni * ai) / den
    fi = (ni * ar - nr * ai) / den
    bbr, bbi = _cmul(fr, fi, br, bi)
    return dt, powers, (fr, fi), (bbr, bbi), den


def _same_group_tile():
    row = lax.broadcasted_iota(jnp.int32, (LANES, LANES), 0)
    col = lax.broadcasted_iota(jnp.int32, (LANES, LANES), 1)
    return (row // SSM_GROUP) == (col // SSM_GROUP)


def _first_copy_lanes():
    return lax.broadcasted_iota(jnp.int32, (LANES, LANES), 1) < SSM_STATE


def s5_param_fwd(tiles):
    def body(ar_ref, ai_ref, ls_ref, br_ref, bi_ref, cr_ref, ci_ref, kd_ref, wsr_ref, wsi_ref, wor_ref, woi_ref, pr_ref, pi_ref):
        cr = cr_ref[...]
        ci = ci_ref[...]
        _, powers, _, (bbr, bbi), _ = _s5_discretise(ar_ref[...], ai_ref[...], ls_ref[...], br_ref[...], bi_ref[...])
        once = _first_copy_lanes()
        crm = jnp.where(once, cr, 0.0)
        cim = jnp.where(once, ci, 0.0)
        same = _same_group_tile()
        for lag in range(S5_CHUNK):
            er, ei = powers[lag]
            xr, xi = _cmul(er, ei, bbr, bbi)
            rows = pl.ds((S5_CHUNK - 1 - lag) * LANES, LANES)
            wsr_ref[rows, :] = xr
            wsi_ref[rows, :] = xi
        k = _mm_f32(wsr_ref[...], crm, ((1,), (1,))) - _mm_f32(wsi_ref[...], cim, ((1,), (1,)))
        for lag in range(S5_CHUNK):
            kd_ref[lag] = jnp.where(same, k[(S5_CHUNK - 1 - lag) * LANES:(S5_CHUNK - lag) * LANES], 0.0)
        for t in range(S5_CHUNK):
            er, ei = powers[t + 1]
            zr, zi = _cmul(er, ei, cr, ci)
            wor_ref[pl.ds(t * LANES, LANES), :] = zr
            woi_ref[pl.ds(t * LANES, LANES), :] = -zi
        pr_ref[...] = powers[S5_CHUNK][0]
        pi_ref[...] = powers[S5_CHUNK][1]

    return pl.pallas_call(
        body, name="s5_param_fwd", grid=(S5_OCTETS,),
        in_specs=[OCT_TILE] * 7, out_specs=[OCT_KD, OCT_W, OCT_W, OCT_W, OCT_W, OCT_TILE, OCT_TILE],
        out_shape=[_sds((S5_OCTETS, S5_CHUNK, LANES, LANES))] + [_sds((S5_OCTETS, S5_OCT_IN, LANES))] * 4
                  + [_sds((S5_OCTETS, LANES, LANES))] * 2,
        compiler_params=_params(("parallel",)),
    )(*tiles)


def s5_param_bwd(tiles, dkd, dws_re, dws_im, dwo_re, dwo_im, dp_re, dp_im):
    def body(ar_ref, ai_ref, ls_ref, br_ref, bi_ref, cr_ref, ci_ref, dkd_ref, dwsr_ref, dwsi_ref, dwor_ref, dwoi_ref, dpr_ref, dpi_ref,
             dar_ref, dai_ref, dls_ref, dbr_ref, dbi_ref, dcr_ref, dci_ref):
        ar = ar_ref[...]
        ai = ai_ref[...]
        br = br_ref[...]
        bi = bi_ref[...]
        cr = cr_ref[...]
        ci = ci_ref[...]
        dt, powers, (fr, fi), (bbr, bbi), den = _s5_discretise(ar, ai, ls_ref[...], br, bi)
        once = _first_copy_lanes()
        crm = jnp.where(once, cr, 0.0)
        cim = jnp.where(once, ci, 0.0)
        same = _same_group_tile()
        zero = jnp.zeros((LANES, LANES), F32)
        dpow = [[zero, zero] for _ in range(N_LAGS)]
        dbbr, dbbi = zero, zero
        by_step = [S5_CHUNK - 1 - s for s in range(S5_CHUNK)]
        x_all = [_cmul(*powers[lag], bbr, bbi) for lag in by_step]
        xr_all = jnp.concatenate([x[0] for x in x_all], axis=0)
        xi_all = jnp.concatenate([x[1] for x in x_all], axis=0)
        g_all = jnp.concatenate([jnp.where(same, dkd_ref[lag], 0.0) for lag in by_step], axis=0)
        dxr_all = dwsr_ref[...] + _mm_f32(g_all, crm, ((1,), (0,)))
        dxi_all = dwsi_ref[...] - _mm_f32(g_all, cim, ((1,), (0,)))
        dcr = jnp.where(once, _mm_f32(g_all, xr_all, ((0,), (0,))), 0.0)
        dci = -jnp.where(once, _mm_f32(g_all, xi_all, ((0,), (0,))), 0.0)
        for lag in range(S5_CHUNK):
            er, ei = powers[lag]
            rows = slice((S5_CHUNK - 1 - lag) * LANES, (S5_CHUNK - lag) * LANES)
            dxr = dxr_all[rows]
            dxi = dxi_all[rows]
            a, b = _cmul_conj(bbr, bbi, dxr, dxi)
            dpow[lag][0] = dpow[lag][0] + a
            dpow[lag][1] = dpow[lag][1] + b
            a, b = _cmul_conj(er, ei, dxr, dxi)
            dbbr = dbbr + a
            dbbi = dbbi + b
        for t in range(S5_CHUNK):
            er, ei = powers[t + 1]
            dzr = dwor_ref[pl.ds(t * LANES, LANES), :]
            dzi = -dwoi_ref[pl.ds(t * LANES, LANES), :]
            a, b = _cmul_conj(cr, ci, dzr, dzi)
            dpow[t + 1][0] = dpow[t + 1][0] + a
            dpow[t + 1][1] = dpow[t + 1][1] + b
            a, b = _cmul_conj(er, ei, dzr, dzi)
            dcr = dcr + a
            dci = dci + b
        dpow[S5_CHUNK][0] = dpow[S5_CHUNK][0] + dpr_ref[...]
        dpow[S5_CHUNK][1] = dpow[S5_CHUNK][1] + dpi_ref[...]
        dfr, dfi = _cmul_conj(br, bi, dbbr, dbbi)
        dbr, dbi = _cmul_conj(fr, fi, dbbr, dbbi)
        dnr, dni = _cmul(ar / den, ai / den, dfr, dfi)
        qr = (fr * ar + fi * ai) / den
        qi = (fi * ar - fr * ai) / den
        dlr, dli = _cmul(-qr, qi, dfr, dfi)
        dpow[1][0] = dpow[1][0] + dnr
        dpow[1][1] = dpow[1][1] + dni
        dxr, dxi = zero, zero
        for lag in range(1, N_LAGS):
            a, b = _cmul_conj(powers[lag][0], powers[lag][1], dpow[lag][0], dpow[lag][1])
            dxr = dxr + lag * a
            dxi = dxi + lag * b
        dar_ref[...] = dlr + dt * dxr
        dai_ref[...] = dli + dt * dxi
        dls_ref[...] = dt * (ar * dxr + ai * dxi)
        dbr_ref[...] = dbr
        dbi_ref[...] = dbi
        dcr_ref[...] = dcr
        dci_ref[...] = dci

    return pl.pallas_call(
        body, name="s5_param_bwd", grid=(S5_OCTETS,),
        in_specs=[OCT_TILE] * 7 + [OCT_KD, OCT_W, OCT_W, OCT_W, OCT_W, OCT_TILE, OCT_TILE], out_specs=[OCT_TILE] * 7,
        out_shape=[_sds((S5_OCTETS, LANES, LANES))] * 7,
        compiler_params=_params(("parallel",)),
    )(*tiles, dkd, dws_re, dws_im, dwo_re, dwo_im, dp_re, dp_im)


def _doubled(v):
    return jnp.concatenate([v, v], axis=-1)


def _s5_param_tiles(a_re, a_im, log_step, b_re, b_im, c_re, c_im):
    def per_group(a):
        return _doubled(jnp.broadcast_to(a.reshape(S5_OCTETS, S5_OCT, 1, SSM_STATE),
                                         (S5_OCTETS, S5_OCT, SSM_GROUP, SSM_STATE)).reshape(S5_OCTETS, LANES, SSM_STATE))

    ls = jnp.broadcast_to(log_step.reshape(S5_OCTETS, S5_OCT, 1, 1), (S5_OCTETS, S5_OCT, SSM_GROUP, LANES)).reshape(S5_OCTETS, LANES, LANES)
    bt = lambda b: _doubled(b.transpose(0, 2, 1).reshape(S5_OCTETS, LANES, SSM_STATE))
    ct = lambda c: _doubled(c.reshape(S5_OCTETS, LANES, SSM_STATE))
    return [per_group(a_re), per_group(a_im), ls, bt(b_re), bt(b_im), ct(c_re), ct(c_im)]


def _s5_param_grads(dtiles):
    dar, dai, dls, dbr, dbi, dcr, dci = dtiles
    halves = lambda d: d[..., :SSM_STATE] + d[..., SSM_STATE:]
    per_group = lambda d: halves(d).reshape(SSM_GROUPS, SSM_GROUP, SSM_STATE).sum(axis=1)
    per_row = lambda d: halves(d).reshape(SSM_GROUPS, SSM_GROUP, SSM_STATE)
    return (per_group(dar), per_group(dai), dls.reshape(SSM_GROUPS, SSM_GROUP * LANES).sum(axis=1),
            per_row(dbr).transpose(0, 2, 1), per_row(dbi).transpose(0, 2, 1), per_row(dcr), per_row(dci))


def _group_power_rows(tile):
    return tile[:, ::SSM_GROUP, :SSM_STATE].reshape(1, S5_STATES)


def _group_power_tiles(row):
    t = jnp.pad(row.reshape(S5_OCTETS, S5_OCT, 1, SSM_STATE), ((0, 0), (0, 0), (0, SSM_GROUP - 1), (0, LANES - SSM_STATE)))
    return t.reshape(S5_OCTETS, LANES, LANES)


def _rope_tables(t):
    pos = jnp.arange(t, dtype=F32)
    inv_freq = ROPE_THETA ** (-jnp.arange(0, HEAD_DIM, 2, dtype=F32) / HEAD_DIM)
    ang = pos[:, None] * inv_freq[None, :]
    cos = jnp.cos(ang)
    sin = jnp.sin(ang)
    cos64 = jnp.concatenate([cos, cos], axis=1)
    sin64 = jnp.concatenate([-sin, sin], axis=1)
    return jnp.concatenate([cos64, cos64], axis=1), jnp.concatenate([sin64, sin64], axis=1)


def _row(v):
    return v.reshape(1, -1)


def _ssm_forward(x, w, token=None):
    tiles = _s5_param_tiles(w["a_re"], w["a_im"], w["log_step"], w["b_re"], w["b_im"], w["c_re"], w["c_im"])
    kd, ws_re, ws_im, wo_re, wo_im, p_re, p_im = s5_param_fwd(tiles)
    mats = dict(kd=kd, ws_re=ws_re, ws_im=ws_im, wo_re=wo_re, wo_im=wo_im, a_re=_group_power_rows(p_re), a_im=_group_power_rows(p_im))
    u, gate = ssm_proj_fwd(x, _row(w["norm"]), w["w_in"], token)
    s_re, s_im = s5_chunk_states(u, mats["ws_re"], mats["ws_im"])
    h_re, h_im = s5_scan_fwd(s_re, s_im, mats["a_re"], mats["a_im"])
    y_scan = s5_outputs(u, h_re, h_im, mats["kd"], mats["wo_re"], mats["wo_im"])
    y, g2, x_new = ssm_mix_fwd(x, u, gate, y_scan, _row(w["d"]), w["w_glu"], _row(w["b_glu"]), w["w_out"])
    saved = dict(x=x, u=u, gate=gate, y=y, g2=g2, h_re=h_re, h_im=h_im, mats=mats, tiles=tiles)
    return x_new, saved


def _ssm_backward(dxo, w, s, token=None):
    dy, dgate, dw_out, dw_glu, db_glu, dd = ssm_mix_bwd(dxo, s["u"], s["gate"], s["y"], s["g2"], w["w_glu"], w["w_out"], token)
    mats = s["mats"]
    dh_re, dh_im = s5_state_grads(dy, mats["wo_re"], mats["wo_im"])
    ds_re, ds_im, da_re, da_im = s5_scan_bwd(dh_re, dh_im, s["h_re"], s["h_im"], mats["a_re"], mats["a_im"])
    du_scan = s5_input_grads(dy, ds_re, ds_im, mats["kd"], mats["ws_re"], mats["ws_im"])
    dkd, dws_re, dws_im, dwo_re, dwo_im = s5_weight_grads(s["u"], dy, s["h_re"], s["h_im"], ds_re, ds_im)
    dparams = _s5_param_grads(s5_param_bwd(s["tiles"], dkd, dws_re, dws_im, dwo_re, dwo_im,
                                           _group_power_tiles(da_re), _group_power_tiles(da_im)))
    dx, dw_in, dnorm = ssm_proj_bwd(s["x"], _row(w["norm"]), dxo, dy, du_scan, dgate, _row(w["d"]), w["w_in"])
    grads = dict(norm=dnorm, w_in=dw_in, d=dd, w_glu=dw_glu, b_glu=db_glu, w_out=dw_out)
    for name, val in zip(("a_re", "a_im", "log_step", "b_re", "b_im", "c_re", "c_im"), dparams):
        grads[name] = val
    return dx, grads


def _attn_forward(x, w, cos2, sin2):
    q, k, v, gate = attn_proj_fwd(x, _row(w["norm"]), w["w_in"], cos2, sin2)
    o, lse = attn_fwd(q, k, v, w["sinks"])
    x_new = attn_out_fwd(x, o, gate, w["w_out"])
    return x_new, dict(x=x, q=q, k=k, v=v, gate=gate, o=o, lse=lse)


def _attn_backward(dxo, w, s, cos2, sin2, token=None):
    do, dgate, dw_out = attn_out_bwd(dxo, s["o"], s["gate"], w["w_out"], token)
    dq, dk, dv, dsinks = attn_bwd(s["q"], s["k"], s["v"], w["sinks"], s["o"], s["lse"], do)
    dx, dw_in, dnorm = attn_proj_bwd(s["x"], _row(w["norm"]), dxo, dq, dk, dv, dgate, cos2, sin2, w["w_in"])
    return dx, dict(norm=dnorm, w_in=dw_in, sinks=dsinks[0, :N_Q_HEADS], w_out=dw_out)


class _NoExchanges:
    def __init__(self, layers):
        self.layers = layers

    def first_token(self):
        return None

    def layer(self, i, x):
        return self.layers[i]

    def layer_done(self, i, grads, dx):
        return None


def _sequence_step(x, target, final_norm, hooks, depth=4):
    cos2, sin2 = _rope_tables(x.shape[0])
    saved, layers = [], []
    for i in range(depth):
        w = hooks.layer(i, x)
        layers.append(w)
        if i % 2 == 0:
            x, s = _ssm_forward(x, w, hooks.first_token() if i == 0 else None)
        else:
            x, s = _attn_forward(x, w, cos2, sin2)
        saved.append(s)
    loss, dx, dfinal = loss_head(x, _row(final_norm), target)
    grads = {"final_norm": dfinal}
    token = None
    for i in reversed(range(depth)):
        if i % 2 == 0:
            dx, g = _ssm_backward(dx, layers[i], saved[i], token)
        else:
            dx, g = _attn_backward(dx, layers[i], saved[i], cos2, sin2, token)
        g = {"l%d_%s" % (i, name): val for name, val in g.items()}
        grads.update(g)
        token = hooks.layer_done(i, g, dx)
    return loss[0, 0], dx, grads


ANY = pl.BlockSpec(memory_space=pl.ANY)


def _place():
    return lax.axis_index("x"), lax.axis_index("y"), lax.axis_index("c")


def _other_chips(x, y):
    return [(1 - x, y), (x, 1 - y), (1 - x, 1 - y)]


class _StagedCopies:
    def __init__(self, bufs, load_sems, store_sems):
        self.bufs, self.load_sems, self.store_sems = bufs, load_sems, store_sems
        self.loads, self.stores = [], []

    def load(self, i, src):
        cp = pltpu.make_async_copy(src, self.bufs[i], self.load_sems.at[i])
        cp.start()
        self.loads.append(cp)

    def store(self, i, dst):
        self.loads[i].wait()
        cp = pltpu.make_async_copy(self.bufs[i], dst, self.store_sems.at[i])
        cp.start()
        self.stores.append(cp)

    def finish(self):
        for cp in self.stores:
            cp.wait()


def _staging(blocks):
    n = len(blocks)
    return [pltpu.VMEM(b.shape, b.dtype) for b in blocks] + [pltpu.SemaphoreType.DMA((n,)), pltpu.SemaphoreType.DMA((n,))]


def gather_weight_shards(shards):
    n = len(shards)

    def body(*refs):
        ins, outs = refs[:n], refs[n:2 * n]
        send_sems, recv_sems, pass_send_sems, pass_recv_sems = refs[2 * n:2 * n + 4]
        own = _StagedCopies(refs[2 * n + 4:3 * n + 4], *refs[3 * n + 4:])
        x, y, c = _place()
        me = 2 * x + y
        chips = _other_chips(x, y)

        def half(i, block, which):
            rows = ins[i].shape[0] // 2
            return outs[i].at[block, pl.ds(which * rows, rows), :]

        def my_half(i):
            rows = ins[i].shape[0] // 2
            return ins[i].at[pl.ds(c * rows, rows), :]

        for i in range(n):
            own.load(i, ins[i])
        sends = []
        for i in range(n):
            for k, (tx, ty) in enumerate(chips):
                cp = pltpu.make_async_remote_copy(src_ref=my_half(i), dst_ref=half(i, me, c), send_sem=send_sems.at[i, k],
                                                  recv_sem=recv_sems.at[i, k], device_id=(tx, ty, c), device_id_type=MESH)
                cp.start()
                sends.append(cp)
        for i in range(n):
            own.store(i, outs[i].at[me])
        for i in range(n):
            for k, (tx, ty) in enumerate(chips):
                landed = half(i, 2 * tx + ty, c)
                pltpu.make_async_remote_copy(src_ref=my_half(i), dst_ref=landed, send_sem=send_sems.at[i, k],
                                             recv_sem=recv_sems.at[i, k], device_id=(tx, ty, c), device_id_type=MESH).wait_recv()
                cp = pltpu.make_async_remote_copy(src_ref=landed, dst_ref=landed, send_sem=pass_send_sems.at[i, k],
                                                  recv_sem=pass_recv_sems.at[i, k], device_id=(x, y, 1 - c), device_id_type=MESH)
                cp.start()
                sends.append(cp)
        for i in range(n):
            for k, (tx, ty) in enumerate(chips):
                missing = half(i, 2 * tx + ty, 1 - c)
                pltpu.make_async_remote_copy(src_ref=missing, dst_ref=missing, send_sem=pass_send_sems.at[i, k],
                                             recv_sem=pass_recv_sems.at[i, k], device_id=(x, y, 1 - c), device_id_type=MESH).wait_recv()
        for cp in sends:
            cp.wait_send()
        own.finish()

    sems = pltpu.SemaphoreType.DMA((n, 3))
    return pl.pallas_call(
        body, name="gather_weight_shards",
        in_specs=[ANY] * n, out_specs=[ANY] * n,
        out_shape=[_sds((4,) + s.shape, s.dtype) for s in shards],
        scratch_shapes=[sems, sems, sems, sems] + _staging(shards),
        compiler_params=_params(),
    )(*shards)


def exchange_halves_with_sibling(grads):
    n = len(grads)

    def body(*refs):
        ins, outs = refs[:n], refs[n:2 * n]
        send_sems, recv_sems = refs[2 * n:]
        x, y, c = _place()
        copies = []
        for i in range(n):
            half = ins[i].shape[1] // 2
            src = ins[i].at[:, pl.ds((1 - c) * half, half), :]
            cp = pltpu.make_async_remote_copy(src_ref=src, dst_ref=outs[i], send_sem=send_sems.at[i], recv_sem=recv_sems.at[i],
                                              device_id=(x, y, 1 - c), device_id_type=MESH)
            cp.start()
            copies.append(cp)
        for cp in copies:
            cp.wait()

    return pl.pallas_call(
        body, name="exchange_halves_with_sibling",
        in_specs=[ANY] * n, out_specs=[ANY] * n,
        out_shape=[_sds((g.shape[0], g.shape[1] // 2, g.shape[2])) for g in grads],
        scratch_shapes=[pltpu.SemaphoreType.DMA((n,)), pltpu.SemaphoreType.DMA((n,))],
    )(*grads)


def swap_halves_with_sibling(pieces):
    n = len(pieces)

    def body(*refs):
        ins, outs = refs[:n], refs[n:2 * n]
        send_sems, recv_sems = refs[2 * n:2 * n + 2]
        own = _StagedCopies(refs[2 * n + 2:3 * n + 2], *refs[3 * n + 2:])
        x, y, c = _place()
        for i in range(n):
            own.load(i, ins[i])
        swaps = []
        for i in range(n):
            cp = pltpu.make_async_remote_copy(src_ref=ins[i], dst_ref=outs[i].at[c], send_sem=send_sems.at[i], recv_sem=recv_sems.at[i],
                                              device_id=(x, y, 1 - c), device_id_type=MESH)
            cp.start()
            swaps.append(cp)
        for i in range(n):
            own.store(i, outs[i].at[c])
        for i in range(n):
            pltpu.make_async_remote_copy(src_ref=ins[i], dst_ref=outs[i].at[1 - c], send_sem=send_sems.at[i], recv_sem=recv_sems.at[i],
                                         device_id=(x, y, 1 - c), device_id_type=MESH).wait_recv()
        for cp in swaps:
            cp.wait_send()
        own.finish()

    return pl.pallas_call(
        body, name="swap_halves_with_sibling",
        in_specs=[ANY] * n, out_specs=[ANY] * n,
        out_shape=[_sds((2,) + p.shape) for p in pieces],
        scratch_shapes=[pltpu.SemaphoreType.DMA((n,)), pltpu.SemaphoreType.DMA((n,))] + _staging(pieces),
        compiler_params=_params(),
    )(*pieces)


IN_HBM = pl.BlockSpec(memory_space=pltpu.HBM)
SEMAPHORES = pl.BlockSpec(memory_space=pltpu.SEMAPHORE)
DATAFLOW = pltpu.SideEffectType.DATAFLOW_SIDE_EFFECTING


def _hbm(a):
    return pltpu.with_memory_space_constraint(a, pltpu.HBM)


def place_own_blocks(shards):
    n = len(shards)

    def body(*refs):
        ins, outs = refs[:n], refs[n:2 * n]
        own = _StagedCopies(refs[2 * n:3 * n], *refs[3 * n:])
        x, y, _ = _place()
        for i in range(n):
            own.load(i, ins[i])
        for i in range(n):
            own.store(i, outs[i].at[2 * x + y])
        own.finish()

    return pl.pallas_call(
        body, name="place_own_blocks", in_specs=[ANY] * n, out_specs=[ANY] * n,
        out_shape=[_sds((4,) + s.shape, s.dtype) for s in shards],
        scratch_shapes=_staging(shards), compiler_params=_params(),
    )(*shards)


def _block_to_send(ref, chip, per_target):
    if not per_target:
        return ref
    return ref.at[chip] if ref.shape[0] == 4 else ref.at[0]


def start_sends_to_chips(name, sources, landings, per_target, after):
    n = len(sources)
    n_sems = 2 * 3 * n

    def body(*refs):
        srcs = refs[:n]
        sems = refs[2 * n + 1:2 * n + 1 + n_sems]
        lands = refs[2 * n + 1 + n_sems:3 * n + 1 + n_sems]
        token = refs[3 * n + 1 + n_sems]
        x, y, c = _place()
        me = 2 * x + y
        for i in range(n):
            for k, (tx, ty) in enumerate(_other_chips(x, y)):
                src = _block_to_send(srcs[i], 2 * tx + ty, per_target)
                pltpu.make_async_remote_copy(src_ref=src, dst_ref=lands[i].at[me], send_sem=sems[2 * (3 * i + k)], recv_sem=sems[2 * (3 * i + k) + 1],
                                             device_id=(tx, ty, c), device_id_type=MESH).start()
        token[...] = jnp.zeros_like(token)

    outs = pl.pallas_call(
        body, name=name,
        in_specs=[IN_HBM] * (2 * n) + [ANY],
        out_specs=[SEMAPHORES] * n_sems + [IN_HBM] * n + [pl.BlockSpec(memory_space=pltpu.VMEM)],
        out_shape=[pltpu.SemaphoreType.DMA(())] * n_sems + [pltpu.HBM(l.shape, l.dtype) for l in landings] + [_sds(TOKEN_SHAPE)],
        input_output_aliases={n + i: n_sems + i for i in range(n)},
        compiler_params=pltpu.CompilerParams(has_side_effects=DATAFLOW),
    )(*[_hbm(s) for s in sources], *[_hbm(l) for l in landings], after)
    return list(outs[:n_sems]), list(outs[n_sems:n_sems + n]), outs[n_sems + n]


def wait_sends_to_chips(name, sources, landings, sems, per_target, after):
    n = len(sources)
    n_sems = len(sems)

    def body(*refs):
        srcs = refs[:n]
        sem_refs = refs[2 * n:2 * n + n_sems]
        lands = refs[2 * n + n_sems + 1:]
        x, y, c = _place()
        me = 2 * x + y
        for i in range(n):
            for k, (tx, ty) in enumerate(_other_chips(x, y)):
                src = _block_to_send(srcs[i], me, per_target)
                cp = pltpu.make_async_remote_copy(src_ref=src, dst_ref=lands[i].at[2 * tx + ty], send_sem=sem_refs[2 * (3 * i + k)],
                                                  recv_sem=sem_refs[2 * (3 * i + k) + 1], device_id=(tx, ty, c), device_id_type=MESH)
                cp.wait_send()
                cp.wait_recv()

    return pl.pallas_call(
        body, name=name,
        in_specs=[IN_HBM] * (2 * n) + [SEMAPHORES] * n_sems + [ANY],
        out_specs=[IN_HBM] * n,
        out_shape=[pltpu.HBM(l.shape, l.dtype) for l in landings],
        input_output_aliases={n + i: i for i in range(n)},
        compiler_params=pltpu.CompilerParams(has_side_effects=DATAFLOW),
    )(*[_hbm(s) for s in sources], *landings, *sems, after)


def _row_tile(rows, cols):
    tm = rows
    while tm * cols * 4 > (2 << 20) and tm % 16 == 0:
        tm //= 2
    return tm


def add_pair(a, b, out_dtype, copies=1):
    nb, rows, cols = a.shape
    tm = _row_tile(rows, cols)

    def body(a_ref, b_ref, *o_refs):
        total = (a_ref[...] + b_ref[...]).astype(out_dtype)
        for o_ref in o_refs:
            o_ref[...] = total

    spec = pl.BlockSpec((None, tm, cols), lambda j, i: (j, i, 0))
    outs = pl.pallas_call(
        body, name="add_pair", grid=(nb, rows // tm), in_specs=[spec, spec], out_specs=[spec] * copies,
        out_shape=[_sds(a.shape, out_dtype)] * copies, compiler_params=_params(("parallel", "parallel")),
    )(a, b)
    return outs[0] if copies == 1 else outs


def sum_four(a, token=None):
    _, rows, cols = a.shape
    tm = _row_tile(rows, cols)
    extra, extra_specs = _after(token)

    def body(a_ref, *rest):
        o_ref = rest[-1]
        o_ref[...] = ((a_ref[0].astype(F32) + a_ref[1].astype(F32)) + a_ref[2].astype(F32)) + a_ref[3].astype(F32)

    return pl.pallas_call(
        body, name="sum_four", grid=(rows // tm,),
        in_specs=[pl.BlockSpec((4, tm, cols), lambda i: (0, i, 0))] + extra_specs, out_specs=pl.BlockSpec((tm, cols), lambda i: (i, 0)),
        out_shape=_sds((rows, cols)), compiler_params=_params(("parallel",)),
    )(a, *extra)


def _adamw_update(w_ref, g_ref, m_ref, v_ref, d_ref, nm_ref, nv_ref):
    g = g_ref[...]
    nm = ADAM_B1 * m_ref[...] + (1.0 - ADAM_B1) * g
    nv = ADAM_B2 * v_ref[...] + (1.0 - ADAM_B2) * (g * g)
    d_ref[...] = -ADAM_LR * ((nm / (1.0 - ADAM_B1 ** ADAM_STEP)) / (jnp.sqrt(nv / (1.0 - ADAM_B2 ** ADAM_STEP)) + ADAM_EPS) + ADAM_WD * w_ref[...])
    nm_ref[...] = nm
    nv_ref[...] = nv


def adamw(w, g, m, v):
    rows, cols = w.shape
    tm = _row_tile(rows, cols)

    def body(*refs):
        _adamw_update(*refs)

    spec = pl.BlockSpec((tm, cols), lambda i: (i, 0))
    return pl.pallas_call(
        body, name="adamw", grid=(rows // tm,), in_specs=[spec] * 4, out_specs=[spec] * 3,
        out_shape=[_sds(w.shape)] * 3, compiler_params=_params(("parallel",)),
    )(w, g, m, v)


def adamw_small(ws, gs, ms, vs, slabs=None):
    n = len(ws)

    def body(*refs):
        for i in range(n):
            _adamw_update(refs[i], refs[n + i], refs[2 * n + i], refs[3 * n + i], refs[4 * n + i], refs[5 * n + i], refs[6 * n + i])

    if slabs is None:
        grid = ()
        specs = [pl.BlockSpec(memory_space=pltpu.VMEM)] * n
    else:
        grid = (slabs,)
        specs = [pl.BlockSpec((w.shape[0] // slabs,) + w.shape[1:], lambda i: (i, 0, 0)) for w in ws]
    outs = pl.pallas_call(
        body, name="adamw_small", grid=grid, in_specs=specs * 4, out_specs=specs * 3,
        out_shape=[_sds(w.shape) for w in ws] * 3, compiler_params=_params(("parallel",) if slabs else None),
    )(*ws, *gs, *ms, *vs)
    return outs[:n], outs[n:2 * n], outs[2 * n:]


PACK_TILE = 8 * LANES
PACK_PIECES = 8
PACK_ALIGN = PACK_PIECES * 16


def _pack_small(values, scalar=None):
    parts = []
    for name in SMALL_NAMES:
        flat = values[name].reshape(-1)
        pad = (-flat.shape[0]) % PACK_TILE
        if pad:
            flat = jnp.concatenate([flat, jnp.zeros((pad,), F32)])
        parts.append(flat.reshape(-1, LANES))
    rows = sum(p.shape[0] for p in parts) + 8
    parts.append(jnp.zeros(((-rows) % PACK_ALIGN, LANES), F32))
    last = jnp.zeros((8, LANES), F32)
    parts.append(last if scalar is None else jnp.broadcast_to(scalar.astype(F32), (8, LANES)))
    return jnp.concatenate(parts, axis=0)


def _unpack_small(pack, like):
    out = {}
    row = 0
    for name in SMALL_NAMES:
        size = math.prod(like[name].shape)
        rows = -(-size // PACK_TILE) * 8
        out[name] = pack[row:row + rows].reshape(-1)[:size].reshape(like[name].shape)
        row += rows
    return out


def _is_column_sharded(name):
    return name.endswith("w_in")


def _to_blocks(name, full):
    if full.ndim == 3:
        return full
    if _is_column_sharded(name):
        rows, cols = full.shape
        return full.reshape(rows, 4, cols // 4).transpose(1, 0, 2)
    return full.reshape(4, full.shape[0] // 4, full.shape[1])


def _from_blocks(name, stacked):
    if _is_column_sharded(name):
        if stacked.shape[2] % LANES == 0:
            return stacked
        return stacked.transpose(1, 0, 2).reshape(stacked.shape[1], 4 * stacked.shape[2])
    return stacked.reshape(4 * stacked.shape[1], stacked.shape[2])


def _layer_big_names(i):
    return [n for n in BIG_NAMES if n.startswith("l%d_" % i)]


class _OverlappedExchanges:
    def __init__(self, weights):
        self.weights = weights
        self.c = lax.axis_index("c")
        first = _layer_big_names(0)
        self.later = [n for n in BIG_NAMES if n not in first]
        gathered = gather_weight_shards([weights[n].astype(MXU_DTYPE) for n in first])
        self.full = {n: _from_blocks(n, g) for n, g in zip(first, gathered)}
        shards = [weights[n].astype(MXU_DTYPE) for n in self.later]
        self.gather = (shards,) + start_sends_to_chips("gather_later_start", shards, place_own_blocks(shards), False, gathered[0])
        self.in_flight = {}
        self.contributions = {}

    def first_token(self):
        return self.gather[3]

    def layer(self, i, x):
        if i == 1:
            shards, sems, stacks, _ = self.gather
            stacks = wait_sends_to_chips("gather_later_wait", shards, stacks, sems, False, x)
            self.full.update({n: _from_blocks(n, g) for n, g in zip(self.later, stacks)})
        names = SSM_NAMES if i % 2 == 0 else ATTN_NAMES
        return {n: self.full.get("l%d_%s" % (i, n), self.weights.get("l%d_%s" % (i, n))) for n in names}

    def chip_sums(self, names, grads, extra_blocks=(), copies=1):
        blocks = [_to_blocks(n, grads[n]) for n in names] + list(extra_blocks)
        from_sibling = exchange_halves_with_sibling(blocks)
        sums = []
        for i, (b, r) in enumerate(zip(blocks, from_sibling)):
            half = b.shape[1] // 2
            mine = lax.dynamic_slice_in_dim(b, self.c * half, half, axis=1)
            sums.append(add_pair(mine, r, WIRE_DTYPE if i < len(names) else F32, copies))
        return sums

    def layer_done(self, i, grads, dx):
        if i + 1 in self.in_flight:
            names, sums, sems, landings = self.in_flight.pop(i + 1)
            done = wait_sends_to_chips("scatter_wait_l%d" % (i + 1), sums, landings, sems, True, dx)
            self.contributions.update(zip(names, done))
        if i == 0:
            return None
        names = _layer_big_names(i)
        pairs = self.chip_sums(names, grads, copies=2)
        sums = [p[0] for p in pairs]
        sems, landings, token = start_sends_to_chips("scatter_start_l%d" % i, sums, [p[1] for p in pairs], True, sums[0])
        self.in_flight[i] = (names, sums, sems, landings)
        return token


def _train_step(x, loss_target, weights, moments_m, moments_v):
    hooks = _OverlappedExchanges(weights)
    loss, dx, grads = _sequence_step(x[0], loss_target[0], weights["final_norm"], hooks)
    small_pack = _pack_small({n: grads[n] for n in SMALL_NAMES}, scalar=loss)
    last = _layer_big_names(0)
    pairs = hooks.chip_sums(last, grads, extra_blocks=[small_pack[None]], copies=2)
    sums = [p[0] for p in pairs]
    landings = [p[1] for p in pairs[:-1]] + [jnp.broadcast_to(sums[-1], (4,) + sums[-1].shape[1:])]
    sems, landings, token = start_sends_to_chips("scatter_start_l0", sums, landings, True, sums[0])
    out_grad, out_delta, out_m, out_v = {}, {}, {}, {}

    def finish(names, arrays, token=None):
        shared = swap_halves_with_sibling([sum_four(a, token) for a in arrays])
        for n, s in zip(names, shared):
            if n == "small":
                return s.reshape(-1, LANES)
            out_grad[n] = s.reshape(2 * s.shape[1], s.shape[2])
            out_delta[n], out_m[n], out_v[n] = adamw(weights[n], out_grad[n], moments_m[n], moments_v[n])

    others = [n for n in BIG_NAMES if n not in last]
    finish(others, [hooks.contributions[n] for n in others], token)
    arrived = wait_sends_to_chips("scatter_wait_l0", sums, landings, sems, True, out_v[others[-1]])
    small_grad_pack = finish(last + ["small"], arrived)
    loss = small_grad_pack[-8, 0]
    out_grad.update(_unpack_small(small_grad_pack, {n: weights[n] for n in SMALL_NAMES}))
    cubes = [n for n in SMALL_NAMES if weights[n].ndim == 3]
    for names, slabs in ((cubes, 8), ([n for n in SMALL_NAMES if n not in cubes], None)):
        deltas, new_ms, new_vs = adamw_small(*[[group[n] for n in names] for group in (weights, out_grad, moments_m, moments_v)], slabs=slabs)
        out_delta.update(zip(names, deltas))
        out_m.update(zip(names, new_ms))
        out_v.update(zip(names, new_vs))
    outs = [loss, dx[None]]
    for group in (out_grad, out_delta, out_m, out_v):
        outs.extend(group[n] for n in WEIGHT_NAMES)
    return tuple(outs)


def kernel(x, l0_norm, l0_w_in, l0_a_re, l0_a_im, l0_log_step, l0_b_re, l0_b_im, l0_c_re, l0_c_im, l0_d, l0_w_glu, l0_b_glu, l0_w_out, l1_norm, l1_w_in, l1_sinks, l1_w_out, l2_norm, l2_w_in, l2_a_re, l2_a_im, l2_log_step, l2_b_re, l2_b_im, l2_c_re, l2_c_im, l2_d, l2_w_glu, l2_b_glu, l2_w_out, l3_norm, l3_w_in, l3_sinks, l3_w_out, final_norm, loss_target, m_l0_norm, m_l0_w_in, m_l0_a_re, m_l0_a_im, m_l0_log_step, m_l0_b_re, m_l0_b_im, m_l0_c_re, m_l0_c_im, m_l0_d, m_l0_w_glu, m_l0_b_glu, m_l0_w_out, m_l1_norm, m_l1_w_in, m_l1_sinks, m_l1_w_out, m_l2_norm, m_l2_w_in, m_l2_a_re, m_l2_a_im, m_l2_log_step, m_l2_b_re, m_l2_b_im, m_l2_c_re, m_l2_c_im, m_l2_d, m_l2_w_glu, m_l2_b_glu, m_l2_w_out, m_l3_norm, m_l3_w_in, m_l3_sinks, m_l3_w_out, m_final_norm, v_l0_norm, v_l0_w_in, v_l0_a_re, v_l0_a_im, v_l0_log_step, v_l0_b_re, v_l0_b_im, v_l0_c_re, v_l0_c_im, v_l0_d, v_l0_w_glu, v_l0_b_glu, v_l0_w_out, v_l1_norm, v_l1_w_in, v_l1_sinks, v_l1_w_out, v_l2_norm, v_l2_w_in, v_l2_a_re, v_l2_a_im, v_l2_log_step, v_l2_b_re, v_l2_b_im, v_l2_c_re, v_l2_c_im, v_l2_d, v_l2_w_glu, v_l2_b_glu, v_l2_w_out, v_l3_norm, v_l3_w_in, v_l3_sinks, v_l3_w_out, v_final_norm):
    args = locals()
    weights = {n: args[n] for n in WEIGHT_NAMES}
    moments_m = {n: args["m_" + n] for n in WEIGHT_NAMES}
    moments_v = {n: args["v_" + n] for n in WEIGHT_NAMES}
    return _train_step(x, loss_target, weights, moments_m, moments_v)
```

```python
import functools
import math

import jax
import jax.numpy as jnp
from jax import lax
from jax.experimental import pallas as pl
from jax.experimental.pallas import tpu as pltpu

F32 = jnp.float32
MXU_DTYPE = jnp.bfloat16
WIRE_DTYPE = jnp.bfloat16
MESH = pl.DeviceIdType.MESH

D_MODEL = 1024
BRANCH = 1024
NORM_EPS = 1e-5
SSM_GROUPS = 64
SSM_GROUP = 16
SSM_STATE = 64
S5_CHUNK = 16
LANES = 128
S5_OCT = LANES // SSM_GROUP
S5_OCTETS = SSM_GROUPS // S5_OCT
S5_OCT_IN = S5_CHUNK * LANES
S5_OCT_STATE = S5_OCT * SSM_STATE
S5_STATES = SSM_GROUPS * SSM_STATE
HEAD_DIM = 64
N_Q_HEADS = 16
N_KV_HEADS = 2
GQA_GROUP = N_Q_HEADS // N_KV_HEADS
ATTN_BLOCK = 128
Q_DIM = N_Q_HEADS * HEAD_DIM
KV_DIM = N_KV_HEADS * HEAD_DIM
ROPE_THETA = 10000.0
NEG_INF = -1e30
ADAM_LR = 0.001
ADAM_B1 = 0.9
ADAM_B2 = 0.999
ADAM_EPS = 1e-08
ADAM_WD = 0.01
ADAM_STEP = 10

VMEM_LIMIT_V7X = 56 * 1024 * 1024
ROW_TILE_FWD = 512
ROW_TILE_BWD = 512

SSM_NAMES = ("norm", "w_in", "a_re", "a_im", "log_step", "b_re", "b_im", "c_re", "c_im", "d", "w_glu", "b_glu", "w_out")
ATTN_NAMES = ("norm", "w_in", "sinks", "w_out")


def _weight_names():
    names = []
    for i in range(4):
        for n in (SSM_NAMES if i % 2 == 0 else ATTN_NAMES):
            names.append("l%d_%s" % (i, n))
    names.append("final_norm")
    return names


WEIGHT_NAMES = _weight_names()
BIG_NAMES = [n for n in WEIGHT_NAMES if n.endswith(("w_in", "w_glu", "w_out"))]
SMALL_NAMES = [n for n in WEIGHT_NAMES if n not in BIG_NAMES]


def _params(semantics=None):
    return pltpu.CompilerParams(dimension_semantics=semantics, vmem_limit_bytes=VMEM_LIMIT_V7X)


def _rows(tm, n):
    return pl.BlockSpec((tm, n), lambda i: (i, 0))


def _whole(shape):
    return pl.BlockSpec(shape, lambda i: (0,) * len(shape), pipeline_mode=pl.Buffered(1))


def _sds(shape, dtype=F32):
    return jax.ShapeDtypeStruct(shape, dtype)


def _mm(a, b):
    return jnp.dot(a.astype(MXU_DTYPE), b.astype(MXU_DTYPE), preferred_element_type=F32)


def _mm_tn(a, b):
    return lax.dot_general(a.astype(MXU_DTYPE), b.astype(MXU_DTYPE), (((0,), (0,)), ((), ())), preferred_element_type=F32)


def _mm_nt(a, b):
    return lax.dot_general(a.astype(MXU_DTYPE), b.astype(MXU_DTYPE), (((1,), (1,)), ((), ())), preferred_element_type=F32)


def _sigmoid(x):
    return 0.5 + 0.5 * jnp.tanh(0.5 * x)


def _silu(x):
    return x * _sigmoid(x)


def _silu_and_grad(x):
    s = _sigmoid(x)
    return x * s, s * (1.0 + x * (1.0 - s))


GELU_C0 = math.sqrt(2.0 / math.pi)
GELU_C1 = 0.044715


def _gelu(x):
    return 0.5 * x * (1.0 + jnp.tanh(GELU_C0 * (x + GELU_C1 * x * x * x)))


def _gelu_and_grad(x):
    x2 = x * x
    th = jnp.tanh(GELU_C0 * x * (1.0 + GELU_C1 * x2))
    half = 0.5 + 0.5 * th
    return x * half, half + 0.5 * x * (1.0 - th * th) * (GELU_C0 + 3.0 * GELU_C0 * GELU_C1 * x2)


def _rms(x, g):
    r = lax.rsqrt(jnp.mean(x * x, axis=-1, keepdims=True) + NORM_EPS)
    xhat = x * r
    return r, xhat, xhat * g


def _rms_bwd(dh, g, r, xhat):
    dxhat = dh * g
    dx = r * (dxhat - xhat * jnp.mean(dxhat * xhat, axis=-1, keepdims=True))
    return dx, jnp.sum(dh * xhat, axis=0, keepdims=True)


def _swap_half_heads(x):
    n = x.shape[-1]
    lane = lax.broadcasted_iota(jnp.int32, x.shape, x.ndim - 1)
    first = (lane % HEAD_DIM) < (HEAD_DIM // 2)
    return jnp.where(first, pltpu.roll(x, n - HEAD_DIM // 2, x.ndim - 1), pltpu.roll(x, HEAD_DIM // 2, x.ndim - 1))


def _tile_lanes(t, reps):
    return jnp.concatenate([t] * reps, axis=1)


TOKEN_SHAPE = (8, LANES)


def _after(token):
    return ([], []) if token is None else ([token], [_whole(TOKEN_SHAPE)])


def ssm_proj_fwd(x, norm, w_in, token=None):
    t = x.shape[0]
    tm = min(ROW_TILE_FWD, t)
    extra, extra_specs = _after(token)

    def body(x_ref, g_ref, w_ref, *rest):
        u_ref, gate_ref = rest[-2:]
        _, _, h = _rms(x_ref[...], g_ref[...])
        h = h.astype(MXU_DTYPE)
        half = BRANCH // 2
        for j in range(2):
            u_ref[:, j * half:(j + 1) * half] = _mm(h, w_ref[j])
            gate_ref[:, j * half:(j + 1) * half] = _mm(h, w_ref[2 + j])

    return pl.pallas_call(
        body, name="ssm_proj_fwd", grid=(t // tm,),
        in_specs=[_rows(tm, D_MODEL), _whole((1, D_MODEL)), _whole((4, D_MODEL, BRANCH // 2))] + extra_specs,
        out_specs=[_rows(tm, BRANCH), _rows(tm, BRANCH)],
        out_shape=[_sds((t, BRANCH)), _sds((t, BRANCH))],
        compiler_params=_params(("parallel",)),
    )(x, norm, w_in, *extra)


def _chunk_rows(ref, nk, dtype=None):
    rows = jnp.concatenate([ref[pl.ds(s, nk, stride=S5_CHUNK), :] for s in range(S5_CHUNK)], axis=1)
    return rows.astype(MXU_DTYPE if dtype is None else dtype)


def _store_chunk_rows(ref, val, nk):
    for s in range(S5_CHUNK):
        ref[pl.ds(s, nk, stride=S5_CHUNK), :] = val[:, s * LANES:(s + 1) * LANES]


def _own_group_mask():
    row = lax.broadcasted_iota(jnp.int32, (S5_OCT_IN, S5_OCT_STATE), 0)
    col = lax.broadcasted_iota(jnp.int32, (S5_OCT_IN, S5_OCT_STATE), 1)
    return ((row % LANES) // SSM_GROUP) == (col // SSM_STATE)


def _spread_groups(w):
    return jnp.where(_own_group_mask(), jnp.concatenate([w] * (S5_OCT_STATE // LANES), axis=1), 0.0).astype(MXU_DTYPE)


def _fold_groups(p):
    p = jnp.where(_own_group_mask(), p, 0.0)
    return sum(p[:, q * LANES:(q + 1) * LANES] for q in range(S5_OCT_STATE // LANES))


def _fill_toeplitz(win_ref, kd_ref):
    win_ref[...] = jnp.zeros_like(win_ref)
    for s in range(S5_CHUNK):
        for t in range(s, S5_CHUNK):
            win_ref[s * LANES:(s + 1) * LANES, t * LANES:(t + 1) * LANES] = kd_ref[t - s].astype(MXU_DTYPE)


TOEPLITZ_BLOCK = 512
_TOEPLITZ_BLOCKS = [(lo, lo + TOEPLITZ_BLOCK) for lo in range(0, S5_OCT_IN, TOEPLITZ_BLOCK)]


def _strip(t):
    return pl.BlockSpec((t, LANES), lambda b: (0, b))


def _oct_states(nk):
    return pl.BlockSpec((nk, S5_OCT_STATE), lambda b: (0, b))


OCT_W = pl.BlockSpec((None, S5_OCT_IN, LANES), lambda b: (b, 0, 0))
OCT_KD = pl.BlockSpec((None, S5_CHUNK, LANES, LANES), lambda b: (b, 0, 0, 0))


def s5_chunk_states(u, ws_re, ws_im):
    t = u.shape[0]
    nk = t // S5_CHUNK

    def body(u_ref, wr_ref, wi_ref, re_ref, im_ref):
        uc = _chunk_rows(u_ref, nk)
        re_ref[...] = _mm(uc, _spread_groups(wr_ref[...]))
        im_ref[...] = _mm(uc, _spread_groups(wi_ref[...]))

    return pl.pallas_call(
        body, name="s5_chunk_states", grid=(S5_OCTETS,),
        in_specs=[_strip(t), OCT_W, OCT_W], out_specs=[_oct_states(nk), _oct_states(nk)],
        out_shape=[_sds((nk, S5_STATES)), _sds((nk, S5_STATES))],
        compiler_params=_params(("parallel",)),
    )(u, ws_re, ws_im)


def s5_scan_fwd(s_re, s_im, a_re, a_im):
    nk = s_re.shape[0]

    def body(sre_ref, sim_ref, ar_ref, ai_ref, hre_ref, him_ref):
        ar = ar_ref[...]
        ai = ai_ref[...]

        def step(k, carry):
            hr, hi = carry
            hre_ref[pl.ds(k, 1), :] = hr
            him_ref[pl.ds(k, 1), :] = hi
            sr = sre_ref[pl.ds(k, 1), :]
            si = sim_ref[pl.ds(k, 1), :]
            return ar * hr - ai * hi + sr, ai * hr + ar * hi + si

        zero = jnp.zeros((1, S5_STATES), F32)
        lax.fori_loop(0, nk, step, (zero, zero))

    vm = pl.BlockSpec(memory_space=pltpu.VMEM)
    return pl.pallas_call(
        body, name="s5_scan_fwd", in_specs=[vm, vm, vm, vm], out_specs=[vm, vm],
        out_shape=[_sds((nk, S5_STATES)), _sds((nk, S5_STATES))],
        compiler_params=_params(),
    )(s_re, s_im, a_re, a_im)


def s5_outputs(u, h_re, h_im, kd, wo_re, wo_im):
    t = u.shape[0]
    nk = t // S5_CHUNK

    def body(u_ref, hre_ref, him_ref, kd_ref, wor_ref, woi_ref, y_ref, win_ref):
        _fill_toeplitz(win_ref, kd_ref)
        uc = _chunk_rows(u_ref, nk)
        y = jnp.concatenate([_mm(uc[:, :hi], win_ref[:hi, lo:hi]) for lo, hi in _TOEPLITZ_BLOCKS], axis=1)
        y = y + _mm_nt(hre_ref[...], _spread_groups(wor_ref[...])) + _mm_nt(him_ref[...], _spread_groups(woi_ref[...]))
        _store_chunk_rows(y_ref, y, nk)

    return pl.pallas_call(
        body, name="s5_outputs", grid=(S5_OCTETS,),
        in_specs=[_strip(t), _oct_states(nk), _oct_states(nk), OCT_KD, OCT_W, OCT_W],
        out_specs=_strip(t), out_shape=_sds((t, BRANCH)),
        scratch_shapes=[pltpu.VMEM((S5_OCT_IN, S5_OCT_IN), MXU_DTYPE)],
        compiler_params=_params(("parallel",)),
    )(u, h_re, h_im, kd, wo_re, wo_im)


def s5_state_grads(dy, wo_re, wo_im):
    t = dy.shape[0]
    nk = t // S5_CHUNK

    def body(dy_ref, wor_ref, woi_ref, re_ref, im_ref):
        dyc = _chunk_rows(dy_ref, nk)
        re_ref[...] = _mm(dyc, _spread_groups(wor_ref[...]))
        im_ref[...] = _mm(dyc, _spread_groups(woi_ref[...]))

    return pl.pallas_call(
        body, name="s5_state_grads", grid=(S5_OCTETS,),
        in_specs=[_strip(t), OCT_W, OCT_W], out_specs=[_oct_states(nk), _oct_states(nk)],
        out_shape=[_sds((nk, S5_STATES)), _sds((nk, S5_STATES))],
        compiler_params=_params(("parallel",)),
    )(dy, wo_re, wo_im)


def s5_scan_bwd(dh_re, dh_im, h_re, h_im, a_re, a_im):
    nk = dh_re.shape[0]

    def body(dhr_ref, dhi_ref, hr_ref, hi_ref, ar_ref, ai_ref, dsr_ref, dsi_ref, dar_ref, dai_ref):
        ar = ar_ref[...]
        ai = ai_ref[...]

        dar_ref[...] = jnp.zeros_like(dar_ref)
        dai_ref[...] = jnp.zeros_like(dai_ref)

        def step(i, carry):
            gr, gi = carry
            k = nk - 1 - i
            dhr = dhr_ref[pl.ds(k, 1), :]
            dhi = dhi_ref[pl.ds(k, 1), :]
            dsr_ref[pl.ds(k, 1), :] = gr
            dsi_ref[pl.ds(k, 1), :] = gi
            hr = hr_ref[pl.ds(k, 1), :]
            hi = hi_ref[pl.ds(k, 1), :]
            dar_ref[...] += gr * hr + gi * hi
            dai_ref[...] += gi * hr - gr * hi
            return dhr + ar * gr + ai * gi, dhi - ai * gr + ar * gi

        zero = jnp.zeros((1, S5_STATES), F32)
        lax.fori_loop(0, nk, step, (zero, zero))

    vm = pl.BlockSpec(memory_space=pltpu.VMEM)
    return pl.pallas_call(
        body, name="s5_scan_bwd", in_specs=[vm] * 6, out_specs=[vm] * 4,
        out_shape=[_sds((nk, S5_STATES)), _sds((nk, S5_STATES)), _sds((1, S5_STATES)), _sds((1, S5_STATES))],
        input_output_aliases={0: 0, 1: 1}, compiler_params=_params(),
    )(dh_re, dh_im, h_re, h_im, a_re, a_im)


def s5_input_grads(dy, ds_re, ds_im, kd, ws_re, ws_im):
    t = dy.shape[0]
    nk = t // S5_CHUNK

    def body(dy_ref, dsr_ref, dsi_ref, kd_ref, wsr_ref, wsi_ref, du_ref, win_ref):
        _fill_toeplitz(win_ref, kd_ref)
        dyc = _chunk_rows(dy_ref, nk)
        du = jnp.concatenate([_mm_nt(dyc[:, lo:], win_ref[lo:hi, lo:]) for lo, hi in _TOEPLITZ_BLOCKS], axis=1)
        du = du + _mm_nt(dsr_ref[...], _spread_groups(wsr_ref[...])) + _mm_nt(dsi_ref[...], _spread_groups(wsi_ref[...]))
        _store_chunk_rows(du_ref, du, nk)

    return pl.pallas_call(
        body, name="s5_input_grads", grid=(S5_OCTETS,),
        in_specs=[_strip(t), _oct_states(nk), _oct_states(nk), OCT_KD, OCT_W, OCT_W],
        out_specs=_strip(t), out_shape=_sds((t, BRANCH)),
        scratch_shapes=[pltpu.VMEM((S5_OCT_IN, S5_OCT_IN), MXU_DTYPE)],
        compiler_params=_params(("parallel",)),
    )(dy, ds_re, ds_im, kd, ws_re, ws_im)


def s5_weight_grads(u, dy, h_re, h_im, ds_re, ds_im):
    t = u.shape[0]
    nk = t // S5_CHUNK

    def body(u_ref, dy_ref, hre_ref, him_ref, dsr_ref, dsi_ref, dkd_ref, dwsr_ref, dwsi_ref, dwor_ref, dwoi_ref):
        dyc = _chunk_rows(dy_ref, nk, F32)
        uct = _chunk_rows(u_ref, nk, F32).T.astype(MXU_DTYPE)
        dyct = dyc.T.astype(MXU_DTYPE)
        dyc = dyc.astype(MXU_DTYPE)
        dwsr_ref[...] = _fold_groups(_mm(uct, dsr_ref[...]))
        dwsi_ref[...] = _fold_groups(_mm(uct, dsi_ref[...]))
        dwor_ref[...] = _fold_groups(_mm(dyct, hre_ref[...]))
        dwoi_ref[...] = _fold_groups(_mm(dyct, him_ref[...]))
        dkd_ref[...] = jnp.zeros_like(dkd_ref)
        for tt in range(0, S5_CHUNK, 2):
            p = _mm(uct[:(tt + 2) * LANES], dyc[:, tt * LANES:(tt + 2) * LANES])
            for s in range(tt + 2):
                rows = p[s * LANES:(s + 1) * LANES]
                if s <= tt:
                    dkd_ref[tt - s] += rows[:, :LANES]
                dkd_ref[tt + 1 - s] += rows[:, LANES:]

    return pl.pallas_call(
        body, name="s5_weight_grads", grid=(S5_OCTETS,),
        in_specs=[_strip(t), _strip(t)] + [_oct_states(nk)] * 4,
        out_specs=[OCT_KD, OCT_W, OCT_W, OCT_W, OCT_W],
        out_shape=[_sds((S5_OCTETS, S5_CHUNK, LANES, LANES))] + [_sds((S5_OCTETS, S5_OCT_IN, LANES))] * 4,
        compiler_params=_params(("parallel",)),
    )(u, dy, h_re, h_im, ds_re, ds_im)


def ssm_mix_fwd(x, u, gate, y_scan, d, w_glu, b_glu, w_out):
    t = x.shape[0]
    tm = min(ROW_TILE_FWD, t)

    def body(x_ref, u_ref, gate_ref, ys_ref, d_ref, wg_ref, bg_ref, wo_ref, y_ref, g2_ref, xo_ref):
        y = ys_ref[...] + d_ref[...] * u_ref[...]
        z0 = _gelu(y)
        g2 = _mm(z0, wg_ref[...]) + bg_ref[...]
        a = z0 * _sigmoid(g2) * _silu(gate_ref[...])
        y_ref[...] = y
        g2_ref[...] = g2
        xo_ref[...] = x_ref[...] + _mm(a, wo_ref[...])

    row = _rows(tm, BRANCH)
    vec = _whole((1, BRANCH))
    mat = _whole((BRANCH, BRANCH))
    return pl.pallas_call(
        body, name="ssm_mix_fwd", grid=(t // tm,),
        in_specs=[row, row, row, row, vec, mat, vec, mat],
        out_specs=[row, row, row],
        out_shape=[_sds((t, BRANCH))] * 3,
        compiler_params=_params(("parallel",)),
    )(x, u, gate, y_scan, d, w_glu, b_glu, w_out)


def ssm_mix_bwd(dxo, u, gate, y, g2, w_glu, w_out, token=None):
    t = dxo.shape[0]
    tm = min(ROW_TILE_BWD, t)
    extra, extra_specs = _after(token)

    def body(dxo_ref, u_ref, gate_ref, y_ref, g2_ref, wgt_ref, wot_ref, *rest):
        dy_ref, dgate_ref, dwo_ref, dwg_ref, dbg_ref, dd_ref = rest[-6:]

        @pl.when(pl.program_id(0) == 0)
        def _():
            dwo_ref[...] = jnp.zeros_like(dwo_ref)
            dwg_ref[...] = jnp.zeros_like(dwg_ref)
            dbg_ref[...] = jnp.zeros_like(dbg_ref)
            dd_ref[...] = jnp.zeros_like(dd_ref)

        dxo = dxo_ref[...]
        gate = gate_ref[...]
        y = y_ref[...]
        z0, z0_grad = _gelu_and_grad(y)
        sg = _sigmoid(g2_ref[...])
        z = z0 * sg
        sgate, sgate_grad = _silu_and_grad(gate)
        da = _mm_nt(dxo, wot_ref[...])
        dwo_ref[...] += _mm_tn(z * sgate, dxo)
        dz = da * sgate
        dgate_ref[...] = da * z * sgate_grad
        dg2 = dz * z0 * sg * (1.0 - sg)
        dbg_ref[...] += jnp.sum(dg2, axis=0, keepdims=True)
        dwg_ref[...] += _mm_tn(z0, dg2)
        dz0 = dz * sg + _mm_nt(dg2, wgt_ref[...])
        dy = dz0 * z0_grad
        dd_ref[...] += jnp.sum(dy * u_ref[...], axis=0, keepdims=True)
        dy_ref[...] = dy

    row = _rows(tm, BRANCH)
    vec = _whole((1, BRANCH))
    mat = _whole((BRANCH, BRANCH))
    return pl.pallas_call(
        body, name="ssm_mix_bwd", grid=(t // tm,),
        in_specs=[row, row, row, row, row, mat, mat] + extra_specs,
        out_specs=[row, row, mat, mat, vec, vec],
        out_shape=[_sds((t, BRANCH)), _sds((t, BRANCH)), _sds((BRANCH, D_MODEL)), _sds((BRANCH, BRANCH)),
                   _sds((1, BRANCH)), _sds((1, BRANCH))],
        compiler_params=_params(("arbitrary",)),
    )(dxo, u, gate, y, g2, w_glu, w_out, *extra)


def ssm_proj_bwd(x, norm, dxo, dy, du_scan, dgate, d, w_in):
    t = x.shape[0]
    tm = min(ROW_TILE_BWD, t)
    n = 2 * BRANCH

    def body(x_ref, g_ref, dxo_ref, dy_ref, dus_ref, dgate_ref, d_ref, wt_ref, dx_ref, dw_ref, dg_ref):
        @pl.when(pl.program_id(0) == 0)
        def _():
            dw_ref[...] = jnp.zeros_like(dw_ref)
            dg_ref[...] = jnp.zeros_like(dg_ref)

        g = g_ref[...]
        r, xhat, h = _rms(x_ref[...], g)
        h = h.astype(MXU_DTYPE)
        du = dus_ref[...] + d_ref[...] * dy_ref[...]
        dproj = jnp.concatenate([du, dgate_ref[...]], axis=1).astype(MXU_DTYPE)
        dh = jnp.zeros((tm, D_MODEL), F32)
        for j in range(4):
            cols = dproj[:, j * (n // 4):(j + 1) * (n // 4)]
            dh = dh + _mm_nt(cols, wt_ref[j])
            dw_ref[j] += _mm_tn(h, cols)
        dx, dg = _rms_bwd(dh, g, r, xhat)
        dg_ref[...] += dg
        dx_ref[...] = dxo_ref[...] + dx

    row = _rows(tm, D_MODEL)
    vec = _whole((1, D_MODEL))
    blocks = _whole((4, D_MODEL, n // 4))
    return pl.pallas_call(
        body, name="ssm_proj_bwd", grid=(t // tm,),
        in_specs=[row, vec, row, row, row, row, vec, blocks],
        out_specs=[row, blocks, vec],
        out_shape=[_sds((t, D_MODEL)), _sds((4, D_MODEL, n // 4)), _sds((1, D_MODEL))],
        compiler_params=_params(("arbitrary",)),
    )(x, norm, dxo, dy, du_scan, dgate, d, w_in)


ATTN_N = Q_DIM + 2 * KV_DIM + BRANCH


def attn_proj_fwd(x, norm, w_in, cos2, sin2):
    t = x.shape[0]
    tm = min(ROW_TILE_FWD, t)

    def body(x_ref, g_ref, w_ref, cos_ref, sin_ref, q_ref, k_ref, v_ref, gate_ref):
        _, _, h = _rms(x_ref[...], g_ref[...])
        p = _mm(h, w_ref[...])
        cs = cos_ref[...]
        sn = sin_ref[...]
        q = p[:, :Q_DIM]
        k = p[:, Q_DIM:Q_DIM + KV_DIM]
        q_ref[...] = q * _tile_lanes(cs, Q_DIM // LANES) + _swap_half_heads(q) * _tile_lanes(sn, Q_DIM // LANES)
        k_ref[...] = k * cs + _swap_half_heads(k) * sn
        v_ref[...] = p[:, Q_DIM + KV_DIM:Q_DIM + 2 * KV_DIM]
        gate_ref[...] = p[:, Q_DIM + 2 * KV_DIM:]

    return pl.pallas_call(
        body, name="attn_proj_fwd", grid=(t // tm,),
        in_specs=[_rows(tm, D_MODEL), _whole((1, D_MODEL)), _whole((D_MODEL, ATTN_N)), _rows(tm, LANES), _rows(tm, LANES)],
        out_specs=[_rows(tm, Q_DIM), _rows(tm, KV_DIM), _rows(tm, KV_DIM), _rows(tm, BRANCH)],
        out_shape=[_sds((t, Q_DIM)), _sds((t, KV_DIM)), _sds((t, KV_DIM)), _sds((t, BRANCH))],
        compiler_params=_params(("parallel",)),
    )(x, norm, w_in, cos2, sin2)


GQA_LANES = GQA_GROUP * ATTN_BLOCK


def _window_masks(first_block):
    kj = lax.broadcasted_iota(jnp.int32, (ATTN_BLOCK, GQA_LANES), 0)
    qi = lax.broadcasted_iota(jnp.int32, (ATTN_BLOCK, GQA_LANES), 1) % ATTN_BLOCK
    return kj > qi, kj > jnp.where(first_block, qi, ATTN_BLOCK)


def _fold(upper, both):
    return jnp.where(upper, both[:ATTN_BLOCK], both[ATTN_BLOCK:])


def _unfold(upper, tile):
    return jnp.concatenate([jnp.where(upper, tile, 0.0), jnp.where(upper, 0.0, tile)], axis=0).astype(MXU_DTYPE)


def _stack_heads(ref, group):
    return jnp.concatenate([ref[:, h * HEAD_DIM:(h + 1) * HEAD_DIM] for h in range(group * GQA_GROUP, (group + 1) * GQA_GROUP)], axis=0)


def _unstack_heads(ref, group, stacked):
    for n in range(GQA_GROUP):
        h = group * GQA_GROUP + n
        ref[:, h * HEAD_DIM:(h + 1) * HEAD_DIM] = stacked[n * ATTN_BLOCK:(n + 1) * ATTN_BLOCK]


def _sink_row(sink_ref, group):
    return jnp.concatenate([jnp.full((1, ATTN_BLOCK), sink_ref[group * GQA_GROUP + n], F32) for n in range(GQA_GROUP)], axis=1)


def _lane_is(h):
    return lax.broadcasted_iota(jnp.int32, (1, LANES), 1) == h


def attn_fwd(q, k, v, sinks):
    t = q.shape[0]
    nb = t // ATTN_BLOCK
    scale = HEAD_DIM ** -0.5

    def body(sink_ref, q_ref, kc_ref, kp_ref, vc_ref, vp_ref, o_ref, lse_ref):
        keys = jnp.concatenate([kp_ref[...], kc_ref[...]], axis=0).astype(MXU_DTYPE)
        vals = jnp.concatenate([vp_ref[...], vc_ref[...]], axis=0).astype(MXU_DTYPE)
        upper, dead = _window_masks(pl.program_id(0) == 0)
        for g in range(N_KV_HEADS):
            kv = slice(g * HEAD_DIM, (g + 1) * HEAD_DIM)
            qs = _stack_heads(q_ref, g) * scale
            s = jnp.where(dead, NEG_INF, _fold(upper, _mm_nt(keys[:, kv], qs)))
            sink = _sink_row(sink_ref, g)
            m = jnp.maximum(jnp.max(s, axis=0, keepdims=True), sink)
            p = jnp.exp(s - m)
            den = jnp.sum(p, axis=0, keepdims=True) + jnp.exp(sink - m)
            _unstack_heads(o_ref, g, _mm_tn(_unfold(upper, p * (1.0 / den)), vals[:, kv]))
            lse = m + jnp.log(den)
            for n in range(GQA_GROUP):
                lse_ref[pl.ds(g * GQA_GROUP + n, 1), :] = lse[:, n * ATTN_BLOCK:(n + 1) * ATTN_BLOCK]

    cur = lambda n: pl.BlockSpec((ATTN_BLOCK, n), lambda i: (i, 0))
    prev = lambda n: pl.BlockSpec((ATTN_BLOCK, n), lambda i: (jnp.maximum(i - 1, 0), 0))
    return pl.pallas_call(
        body, name="attn_fwd", grid=(nb,),
        in_specs=[pl.BlockSpec(memory_space=pltpu.SMEM), cur(Q_DIM), cur(KV_DIM), prev(KV_DIM), cur(KV_DIM), prev(KV_DIM)],
        out_specs=[cur(Q_DIM), pl.BlockSpec((N_Q_HEADS, ATTN_BLOCK), lambda i: (0, i))],
        out_shape=[_sds((t, Q_DIM)), _sds((N_Q_HEADS, t))],
        compiler_params=_params(("parallel",)),
    )(sinks, q, k, k, v, v)


def attn_bwd(q, k, v, sinks, o, lse, do):
    t = q.shape[0]
    nb = t // ATTN_BLOCK
    scale = HEAD_DIM ** -0.5

    def body(sink_ref, q_ref, kc_ref, kp_ref, vc_ref, vp_ref, o_ref, lse_ref, do_ref,
             dq_ref, dk_ref, dv_ref, dsink_ref, dk_carry, dv_carry):
        i = pl.program_id(0)

        @pl.when(i == 0)
        def _():
            dsink_ref[...] = jnp.zeros_like(dsink_ref)
            dk_carry[...] = jnp.zeros_like(dk_carry)
            dv_carry[...] = jnp.zeros_like(dv_carry)

        @pl.when(i < nb)
        def _():
            keys = jnp.concatenate([kp_ref[...], kc_ref[...]], axis=0).astype(MXU_DTYPE)
            vals = jnp.concatenate([vp_ref[...], vc_ref[...]], axis=0).astype(MXU_DTYPE)
            upper, dead = _window_masks(i == 0)
            dsink = jnp.zeros((1, LANES), F32)
            dk_heads = []
            dv_heads = []
            for g in range(N_KV_HEADS):
                kv = slice(g * HEAD_DIM, (g + 1) * HEAD_DIM)
                qs = (_stack_heads(q_ref, g) * scale).astype(MXU_DTYPE)
                dos = _stack_heads(do_ref, g)
                lse = jnp.concatenate([lse_ref[pl.ds(g * GQA_GROUP + n, 1), :] for n in range(GQA_GROUP)], axis=1)
                s = jnp.where(dead, NEG_INF, _fold(upper, _mm_nt(keys[:, kv], qs)))
                p = jnp.exp(s - lse)
                delta = _mm_f32(jnp.ones((8, HEAD_DIM), F32), dos * _stack_heads(o_ref, g), ((1,), (1,)))[:1]
                dos = dos.astype(MXU_DTYPE)
                ds = _unfold(upper, p * (_fold(upper, _mm_nt(vals[:, kv], dos)) - delta))
                _unstack_heads(dq_ref, g, _mm_tn(ds, keys[:, kv]) * scale)
                dk_heads.append(_mm(ds, qs))
                dv_heads.append(_mm(_unfold(upper, p), dos))
                at_sink = jnp.exp(_sink_row(sink_ref, g) - lse) * delta
                for n in range(GQA_GROUP):
                    dsink = dsink + jnp.where(_lane_is(g * GQA_GROUP + n), -jnp.sum(at_sink[:, n * ATTN_BLOCK:(n + 1) * ATTN_BLOCK]), 0.0)
            dkk = jnp.concatenate(dk_heads, axis=1)
            dvv = jnp.concatenate(dv_heads, axis=1)
            dsink_ref[...] += dsink
            dk_ref[...] = dk_carry[...] + dkk[:ATTN_BLOCK]
            dv_ref[...] = dv_carry[...] + dvv[:ATTN_BLOCK]
            dk_carry[...] = dkk[ATTN_BLOCK:]
            dv_carry[...] = dvv[ATTN_BLOCK:]

        @pl.when(i == nb)
        def _():
            dk_ref[...] = dk_carry[...]
            dv_ref[...] = dv_carry[...]

    last = nb - 1
    cur = lambda n: pl.BlockSpec((ATTN_BLOCK, n), lambda i: (jnp.minimum(i, last), 0))
    prev = lambda n: pl.BlockSpec((ATTN_BLOCK, n), lambda i: (jnp.clip(i - 1, 0, last), 0))
    late = lambda n: pl.BlockSpec((ATTN_BLOCK, n), lambda i: (i, 0))
    dq, dk_late, dv_late, dsinks = pl.pallas_call(
        body, name="attn_bwd", grid=(nb + 1,),
        in_specs=[pl.BlockSpec(memory_space=pltpu.SMEM), cur(Q_DIM), cur(KV_DIM), prev(KV_DIM), cur(KV_DIM), prev(KV_DIM),
                  cur(Q_DIM), pl.BlockSpec((N_Q_HEADS, ATTN_BLOCK), lambda i: (0, jnp.minimum(i, last))), cur(Q_DIM)],
        out_specs=[cur(Q_DIM), late(KV_DIM), late(KV_DIM), _whole((1, LANES))],
        out_shape=[_sds((t, Q_DIM)), _sds((t + ATTN_BLOCK, KV_DIM)), _sds((t + ATTN_BLOCK, KV_DIM)), _sds((1, LANES))],
        scratch_shapes=[pltpu.VMEM((ATTN_BLOCK, KV_DIM), F32), pltpu.VMEM((ATTN_BLOCK, KV_DIM), F32)],
        compiler_params=_params(("arbitrary",)),
    )(sinks, q, k, k, v, v, o, lse, do)
    return dq, dk_late[ATTN_BLOCK:], dv_late[ATTN_BLOCK:], dsinks


def attn_out_fwd(x, o, gate, w_out):
    t = x.shape[0]
    tm = min(ROW_TILE_FWD, t)

    def body(x_ref, o_ref, gate_ref, w_ref, xo_ref):
        xo_ref[...] = x_ref[...] + _mm(o_ref[...] * _silu(gate_ref[...]), w_ref[...])

    row = _rows(tm, D_MODEL)
    return pl.pallas_call(
        body, name="attn_out_fwd", grid=(t // tm,),
        in_specs=[row, row, row, _whole((Q_DIM, D_MODEL))], out_specs=row, out_shape=_sds((t, D_MODEL)),
        compiler_params=_params(("parallel",)),
    )(x, o, gate, w_out)


def attn_out_bwd(dxo, o, gate, w_out, token=None):
    t = dxo.shape[0]
    tm = min(ROW_TILE_BWD, t)
    extra, extra_specs = _after(token)

    def body(dxo_ref, o_ref, gate_ref, wt_ref, *rest):
        do_ref, dgate_ref, dw_ref = rest[-3:]

        @pl.when(pl.program_id(0) == 0)
        def _():
            dw_ref[...] = jnp.zeros_like(dw_ref)

        dxo = dxo_ref[...]
        o = o_ref[...]
        gate = gate_ref[...]
        sgate, sgate_grad = _silu_and_grad(gate)
        da = _mm_nt(dxo, wt_ref[...])
        dw_ref[...] += _mm_tn(o * sgate, dxo)
        do_ref[...] = da * sgate
        dgate_ref[...] = da * o * sgate_grad

    row = _rows(tm, D_MODEL)
    mat = _whole((Q_DIM, D_MODEL))
    return pl.pallas_call(
        body, name="attn_out_bwd", grid=(t // tm,),
        in_specs=[row, row, row, mat] + extra_specs, out_specs=[row, row, mat],
        out_shape=[_sds((t, Q_DIM)), _sds((t, BRANCH)), _sds((Q_DIM, D_MODEL))],
        compiler_params=_params(("arbitrary",)),
    )(dxo, o, gate, w_out, *extra)


def attn_proj_bwd(x, norm, dxo, dq, dk, dv, dgate, cos2, sin2, w_in):
    t = x.shape[0]
    tm = min(ROW_TILE_BWD, t)

    def body(x_ref, g_ref, dxo_ref, dq_ref, dk_ref, dv_ref, dgate_ref, cos_ref, sin_ref, wt_ref, dx_ref, dw_ref, dg_ref):
        @pl.when(pl.program_id(0) == 0)
        def _():
            dw_ref[...] = jnp.zeros_like(dw_ref)
            dg_ref[...] = jnp.zeros_like(dg_ref)

        g = g_ref[...]
        r, xhat, h = _rms(x_ref[...], g)
        cs = cos_ref[...]
        sn = sin_ref[...]
        dqr = dq_ref[...]
        dkr = dk_ref[...]
        dq = dqr * _tile_lanes(cs, Q_DIM // LANES) + _swap_half_heads(dqr * _tile_lanes(sn, Q_DIM // LANES))
        dk = dkr * cs + _swap_half_heads(dkr * sn)
        dproj = jnp.concatenate([dq, dk, dv_ref[...], dgate_ref[...]], axis=1)
        dh = _mm_nt(dproj, wt_ref[...])
        dw_ref[...] += _mm_tn(h, dproj)
        dx, dg = _rms_bwd(dh, g, r, xhat)
        dg_ref[...] += dg
        dx_ref[...] = dxo_ref[...] + dx

    row = _rows(tm, D_MODEL)
    vec = _whole((1, D_MODEL))
    return pl.pallas_call(
        body, name="attn_proj_bwd", grid=(t // tm,),
        in_specs=[row, vec, row, _rows(tm, Q_DIM), _rows(tm, KV_DIM), _rows(tm, KV_DIM), _rows(tm, BRANCH),
                  _rows(tm, LANES), _rows(tm, LANES), _whole((D_MODEL, ATTN_N))],
        out_specs=[row, _whole((D_MODEL, ATTN_N)), vec],
        out_shape=[_sds((t, D_MODEL)), _sds((D_MODEL, ATTN_N)), _sds((1, D_MODEL))],
        compiler_params=_params(("arbitrary",)),
    )(x, norm, dxo, dq, dk, dv, dgate, cos2, sin2, w_in)


def loss_head(x, norm, target):
    t = x.shape[0]
    tm = min(ROW_TILE_FWD, t)

    def body(x_ref, g_ref, tgt_ref, loss_ref, dx_ref, dg_ref):
        @pl.when(pl.program_id(0) == 0)
        def _():
            loss_ref[...] = jnp.zeros_like(loss_ref)
            dg_ref[...] = jnp.zeros_like(dg_ref)

        g = g_ref[...]
        r, xhat, y = _rms(x_ref[...], g)
        err = y - tgt_ref[...]
        loss_ref[...] += 0.5 * jnp.sum(jnp.mean(err * err, axis=-1, keepdims=True), axis=0, keepdims=True)
        dx, dg = _rms_bwd(err * (1.0 / D_MODEL), g, r, xhat)
        dg_ref[...] += dg
        dx_ref[...] = dx

    row = _rows(tm, D_MODEL)
    vec = _whole((1, D_MODEL))
    return pl.pallas_call(
        body, name="loss_head", grid=(t // tm,),
        in_specs=[row, vec, row], out_specs=[_whole((1, 1)), row, vec],
        out_shape=[_sds((1, 1)), _sds((t, D_MODEL)), _sds((1, D_MODEL))],
        compiler_params=_params(("arbitrary",)),
    )(x, norm, target)


OCT_TILE = pl.BlockSpec((None, LANES, LANES), lambda b: (b, 0, 0))
N_LAGS = S5_CHUNK + 1


def _cmul(ar, ai, br, bi):
    return ar * br - ai * bi, ar * bi + ai * br


def _cmul_conj(ar, ai, br, bi):
    return ar * br + ai * bi, ar * bi - ai * br


def _mm_f32(a, b, dims):
    return lax.dot_general(a, b, (dims, ((), ())), precision=lax.Precision.HIGHEST, preferred_element_type=F32)


def _s5_discretise(ar, ai, ls, br, bi):
    dt = jnp.exp(ls)
    xr = ar * dt
    xi = ai * dt
    mag = jnp.exp(xr)
    first = (mag * jnp.cos(xi), mag * jnp.sin(xi))
    powers = [(jnp.ones_like(xr), jnp.zeros_like(xr)), first]
    for _ in range(2, N_LAGS):
        powers.append(_cmul(*powers[-1], *first))
    den = ar * ar + ai * ai
    nr = powers[1][0] - 1.0
    ni = powers[1][1]
    fr = (nr * ar + ni * ai) / den
    fi = (ni * ar - nr * ai) / den
    bbr, bbi = _cmul(fr, fi, br, bi)
    return dt, powers, (fr, fi), (bbr, bbi), den


def _same_group_tile():
    row = lax.broadcasted_iota(jnp.int32, (LANES, LANES), 0)
    col = lax.broadcasted_iota(jnp.int32, (LANES, LANES), 1)
    return (row // SSM_GROUP) == (col // SSM_GROUP)


def _first_copy_lanes():
    return lax.broadcasted_iota(jnp.int32, (LANES, LANES), 1) < SSM_STATE


def s5_param_fwd(tiles, token=None):
    extra, extra_specs = _after(token)

    def body(ar_ref, ai_ref, ls_ref, br_ref, bi_ref, cr_ref, ci_ref, *rest):
        kd_ref, wsr_ref, wsi_ref, wor_ref, woi_ref, pr_ref, pi_ref = rest[-7:]
        cr = cr_ref[...]
        ci = ci_ref[...]
        _, powers, _, (bbr, bbi), _ = _s5_discretise(ar_ref[...], ai_ref[...], ls_ref[...], br_ref[...], bi_ref[...])
        once = _first_copy_lanes()
        crm = jnp.where(once, cr, 0.0)
        cim = jnp.where(once, ci, 0.0)
        same = _same_group_tile()
        for lag in range(S5_CHUNK):
            er, ei = powers[lag]
            xr, xi = _cmul(er, ei, bbr, bbi)
            rows = pl.ds((S5_CHUNK - 1 - lag) * LANES, LANES)
            wsr_ref[rows, :] = xr
            wsi_ref[rows, :] = xi
        k = _mm_f32(wsr_ref[...], crm, ((1,), (1,))) - _mm_f32(wsi_ref[...], cim, ((1,), (1,)))
        for lag in range(S5_CHUNK):
            kd_ref[lag] = jnp.where(same, k[(S5_CHUNK - 1 - lag) * LANES:(S5_CHUNK - lag) * LANES], 0.0)
        for t in range(S5_CHUNK):
            er, ei = powers[t + 1]
            zr, zi = _cmul(er, ei, cr, ci)
            wor_ref[pl.ds(t * LANES, LANES), :] = zr
            woi_ref[pl.ds(t * LANES, LANES), :] = -zi
        pr_ref[...] = powers[S5_CHUNK][0]
        pi_ref[...] = powers[S5_CHUNK][1]

    return pl.pallas_call(
        body, name="s5_param_fwd", grid=(S5_OCTETS,),
        in_specs=[OCT_TILE] * 7 + [ANY] * len(extra),
        out_specs=[OCT_KD, OCT_W, OCT_W, OCT_W, OCT_W, OCT_TILE, OCT_TILE],
        out_shape=[_sds((S5_OCTETS, S5_CHUNK, LANES, LANES))] + [_sds((S5_OCTETS, S5_OCT_IN, LANES))] * 4
                  + [_sds((S5_OCTETS, LANES, LANES))] * 2,
        compiler_params=_params(("parallel",)),
    )(*tiles, *extra)


def s5_param_bwd(tiles, dkd, dws_re, dws_im, dwo_re, dwo_im, dp_re, dp_im):
    def body(ar_ref, ai_ref, ls_ref, br_ref, bi_ref, cr_ref, ci_ref, dkd_ref, dwsr_ref, dwsi_ref, dwor_ref, dwoi_ref, dpr_ref, dpi_ref,
             dar_ref, dai_ref, dls_ref, dbr_ref, dbi_ref, dcr_ref, dci_ref):
        ar = ar_ref[...]
        ai = ai_ref[...]
        br = br_ref[...]
        bi = bi_ref[...]
        cr = cr_ref[...]
        ci = ci_ref[...]
        dt, powers, (fr, fi), (bbr, bbi), den = _s5_discretise(ar, ai, ls_ref[...], br, bi)
        once = _first_copy_lanes()
        crm = jnp.where(once, cr, 0.0)
        cim = jnp.where(once, ci, 0.0)
        same = _same_group_tile()
        zero = jnp.zeros((LANES, LANES), F32)
        dpow = [[zero, zero] for _ in range(N_LAGS)]
        dbbr, dbbi = zero, zero
        by_step = [S5_CHUNK - 1 - s for s in range(S5_CHUNK)]
        x_all = [_cmul(*powers[lag], bbr, bbi) for lag in by_step]
        xr_all = jnp.concatenate([x[0] for x in x_all], axis=0)
        xi_all = jnp.concatenate([x[1] for x in x_all], axis=0)
        g_all = jnp.concatenate([jnp.where(same, dkd_ref[lag], 0.0) for lag in by_step], axis=0)
        dxr_all = dwsr_ref[...] + _mm_f32(g_all, crm, ((1,), (0,)))
        dxi_all = dwsi_ref[...] - _mm_f32(g_all, cim, ((1,), (0,)))
        dcr = jnp.where(once, _mm_f32(g_all, xr_all, ((0,), (0,))), 0.0)
        dci = -jnp.where(once, _mm_f32(g_all, xi_all, ((0,), (0,))), 0.0)
        for lag in range(S5_CHUNK):
            er, ei = powers[lag]
            rows = slice((S5_CHUNK - 1 - lag) * LANES, (S5_CHUNK - lag) * LANES)
            dxr = dxr_all[rows]
            dxi = dxi_all[rows]
            a, b = _cmul_conj(bbr, bbi, dxr, dxi)
            dpow[lag][0] = dpow[lag][0] + a
            dpow[lag][1] = dpow[lag][1] + b
            a, b = _cmul_conj(er, ei, dxr, dxi)
            dbbr = dbbr + a
            dbbi = dbbi + b
        for t in range(S5_CHUNK):
            er, ei = powers[t + 1]
            dzr = dwor_ref[pl.ds(t * LANES, LANES), :]
            dzi = -dwoi_ref[pl.ds(t * LANES, LANES), :]
            a, b = _cmul_conj(cr, ci, dzr, dzi)
            dpow[t + 1][0] = dpow[t + 1][0] + a
            dpow[t + 1][1] = dpow[t + 1][1] + b
            a, b = _cmul_conj(er, ei, dzr, dzi)
            dcr = dcr + a
            dci = dci + b
        dpow[S5_CHUNK][0] = dpow[S5_CHUNK][0] + dpr_ref[...]
        dpow[S5_CHUNK][1] = dpow[S5_CHUNK][1] + dpi_ref[...]
        dfr, dfi = _cmul_conj(br, bi, dbbr, dbbi)
        dbr, dbi = _cmul_conj(fr, fi, dbbr, dbbi)
        dnr, dni = _cmul(ar / den, ai / den, dfr, dfi)
        qr = (fr * ar + fi * ai) / den
        qi = (fi * ar - fr * ai) / den
        dlr, dli = _cmul(-qr, qi, dfr, dfi)
        dpow[1][0] = dpow[1][0] + dnr
        dpow[1][1] = dpow[1][1] + dni
        dxr, dxi = zero, zero
        for lag in range(1, N_LAGS):
            a, b = _cmul_conj(powers[lag][0], powers[lag][1], dpow[lag][0], dpow[lag][1])
            dxr = dxr + lag * a
            dxi = dxi + lag * b
        dar_ref[...] = dlr + dt * dxr
        dai_ref[...] = dli + dt * dxi
        dls_ref[...] = dt * (ar * dxr + ai * dxi)
        dbr_ref[...] = dbr
        dbi_ref[...] = dbi
        dcr_ref[...] = dcr
        dci_ref[...] = dci

    return pl.pallas_call(
        body, name="s5_param_bwd", grid=(S5_OCTETS,),
        in_specs=[OCT_TILE] * 7 + [OCT_KD, OCT_W, OCT_W, OCT_W, OCT_W, OCT_TILE, OCT_TILE], out_specs=[OCT_TILE] * 7,
        out_shape=[_sds((S5_OCTETS, LANES, LANES))] * 7,
        compiler_params=_params(("parallel",)),
    )(*tiles, dkd, dws_re, dws_im, dwo_re, dwo_im, dp_re, dp_im)


def _doubled(v):
    return jnp.concatenate([v, v], axis=-1)


def _s5_param_tiles(a_re, a_im, log_step, b_re, b_im, c_re, c_im):
    def per_group(a):
        return _doubled(jnp.broadcast_to(a.reshape(S5_OCTETS, S5_OCT, 1, SSM_STATE),
                                         (S5_OCTETS, S5_OCT, SSM_GROUP, SSM_STATE)).reshape(S5_OCTETS, LANES, SSM_STATE))

    ls = jnp.broadcast_to(log_step.reshape(S5_OCTETS, S5_OCT, 1, 1), (S5_OCTETS, S5_OCT, SSM_GROUP, LANES)).reshape(S5_OCTETS, LANES, LANES)
    bt = lambda b: _doubled(b.transpose(0, 2, 1).reshape(S5_OCTETS, LANES, SSM_STATE))
    ct = lambda c: _doubled(c.reshape(S5_OCTETS, LANES, SSM_STATE))
    return [per_group(a_re), per_group(a_im), ls, bt(b_re), bt(b_im), ct(c_re), ct(c_im)]


def _s5_param_grads(dtiles):
    dar, dai, dls, dbr, dbi, dcr, dci = dtiles
    halves = lambda d: d[..., :SSM_STATE] + d[..., SSM_STATE:]
    per_group = lambda d: halves(d).reshape(SSM_GROUPS, SSM_GROUP, SSM_STATE).sum(axis=1)
    per_row = lambda d: halves(d).reshape(SSM_GROUPS, SSM_GROUP, SSM_STATE)
    return (per_group(dar), per_group(dai), dls.reshape(SSM_GROUPS, SSM_GROUP * LANES).sum(axis=1),
            per_row(dbr).transpose(0, 2, 1), per_row(dbi).transpose(0, 2, 1), per_row(dcr), per_row(dci))


def _group_power_rows(tile):
    return tile[:, ::SSM_GROUP, :SSM_STATE].reshape(1, S5_STATES)


def _group_power_tiles(row):
    t = jnp.pad(row.reshape(S5_OCTETS, S5_OCT, 1, SSM_STATE), ((0, 0), (0, 0), (0, SSM_GROUP - 1), (0, LANES - SSM_STATE)))
    return t.reshape(S5_OCTETS, LANES, LANES)


def _rope_tables(t):
    pos = jnp.arange(t, dtype=F32)
    inv_freq = ROPE_THETA ** (-jnp.arange(0, HEAD_DIM, 2, dtype=F32) / HEAD_DIM)
    ang = pos[:, None] * inv_freq[None, :]
    cos = jnp.cos(ang)
    sin = jnp.sin(ang)
    cos64 = jnp.concatenate([cos, cos], axis=1)
    sin64 = jnp.concatenate([-sin, sin], axis=1)
    return jnp.concatenate([cos64, cos64], axis=1), jnp.concatenate([sin64, sin64], axis=1)


def _row(v):
    return v.reshape(1, -1)


def _s5_matrices(w, token=None):
    tiles = _s5_param_tiles(w["a_re"], w["a_im"], w["log_step"], w["b_re"], w["b_im"], w["c_re"], w["c_im"])
    kd, ws_re, ws_im, wo_re, wo_im, p_re, p_im = s5_param_fwd(tiles, token)
    return tiles, dict(kd=kd, ws_re=ws_re, ws_im=ws_im, wo_re=wo_re, wo_im=wo_im, a_re=_group_power_rows(p_re), a_im=_group_power_rows(p_im))


def _ssm_forward(x, w, token=None):
    tiles, mats = w["s5"] if "s5" in w else _s5_matrices(w)
    u, gate = ssm_proj_fwd(x, _row(w["norm"]), w["w_in"], token)
    s_re, s_im = s5_chunk_states(u, mats["ws_re"], mats["ws_im"])
    h_re, h_im = s5_scan_fwd(s_re, s_im, mats["a_re"], mats["a_im"])
    y_scan = s5_outputs(u, h_re, h_im, mats["kd"], mats["wo_re"], mats["wo_im"])
    y, g2, x_new = ssm_mix_fwd(x, u, gate, y_scan, _row(w["d"]), w["w_glu"], _row(w["b_glu"]), w["w_out"])
    saved = dict(x=x, u=u, gate=gate, y=y, g2=g2, h_re=h_re, h_im=h_im, mats=mats, tiles=tiles)
    return x_new, saved


def _ssm_backward(dxo, w, s, token=None):
    dy, dgate, dw_out, dw_glu, db_glu, dd = ssm_mix_bwd(dxo, s["u"], s["gate"], s["y"], s["g2"], w["w_glu"], w["w_out"], token)
    mats = s["mats"]
    dh_re, dh_im = s5_state_grads(dy, mats["wo_re"], mats["wo_im"])
    ds_re, ds_im, da_re, da_im = s5_scan_bwd(dh_re, dh_im, s["h_re"], s["h_im"], mats["a_re"], mats["a_im"])
    du_scan = s5_input_grads(dy, ds_re, ds_im, mats["kd"], mats["ws_re"], mats["ws_im"])
    dkd, dws_re, dws_im, dwo_re, dwo_im = s5_weight_grads(s["u"], dy, s["h_re"], s["h_im"], ds_re, ds_im)
    dparams = _s5_param_grads(s5_param_bwd(s["tiles"], dkd, dws_re, dws_im, dwo_re, dwo_im,
                                           _group_power_tiles(da_re), _group_power_tiles(da_im)))
    dx, dw_in, dnorm = ssm_proj_bwd(s["x"], _row(w["norm"]), dxo, dy, du_scan, dgate, _row(w["d"]), w["w_in"])
    grads = dict(norm=dnorm, w_in=dw_in, d=dd, w_glu=dw_glu, b_glu=db_glu, w_out=dw_out)
    for name, val in zip(("a_re", "a_im", "log_step", "b_re", "b_im", "c_re", "c_im"), dparams):
        grads[name] = val
    return dx, grads


def _attn_forward(x, w, cos2, sin2):
    q, k, v, gate = attn_proj_fwd(x, _row(w["norm"]), w["w_in"], cos2, sin2)
    o, lse = attn_fwd(q, k, v, w["sinks"])
    x_new = attn_out_fwd(x, o, gate, w["w_out"])
    return x_new, dict(x=x, q=q, k=k, v=v, gate=gate, o=o, lse=lse)


def _attn_backward(dxo, w, s, cos2, sin2, token=None):
    do, dgate, dw_out = attn_out_bwd(dxo, s["o"], s["gate"], w["w_out"], token)
    dq, dk, dv, dsinks = attn_bwd(s["q"], s["k"], s["v"], w["sinks"], s["o"], s["lse"], do)
    dx, dw_in, dnorm = attn_proj_bwd(s["x"], _row(w["norm"]), dxo, dq, dk, dv, dgate, cos2, sin2, w["w_in"])
    return dx, dict(norm=dnorm, w_in=dw_in, sinks=dsinks[0, :N_Q_HEADS], w_out=dw_out)


class _NoExchanges:
    def __init__(self, layers):
        self.layers = layers

    def first_token(self):
        return None

    def layer(self, i, x):
        return self.layers[i]

    def layer_done(self, i, grads, dx):
        return None


def _sequence_step(x, target, final_norm, hooks, depth=4):
    cos2, sin2 = _rope_tables(x.shape[0])
    saved, layers = [], []
    for i in range(depth):
        w = hooks.layer(i, x)
        layers.append(w)
        if i % 2 == 0:
            x, s = _ssm_forward(x, w, hooks.first_token() if i == 0 else None)
        else:
            x, s = _attn_forward(x, w, cos2, sin2)
        saved.append(s)
    loss, dx, dfinal = loss_head(x, _row(final_norm), target)
    grads = {"final_norm": dfinal}
    token = None
    for i in reversed(range(depth)):
        if i % 2 == 0:
            dx, g = _ssm_backward(dx, layers[i], saved[i], token)
        else:
            dx, g = _attn_backward(dx, layers[i], saved[i], cos2, sin2, token)
        g = {"l%d_%s" % (i, name): val for name, val in g.items()}
        grads.update(g)
        token = hooks.layer_done(i, g, dx)
    return loss[0, 0], dx, grads


ANY = pl.BlockSpec(memory_space=pl.ANY)


def _place():
    return lax.axis_index("x"), lax.axis_index("y"), lax.axis_index("c")


def _other_chips(x, y):
    return [(1 - x, y), (x, 1 - y), (1 - x, 1 - y)]


class _StagedCopies:
    def __init__(self, bufs, load_sems, store_sems):
        self.bufs, self.load_sems, self.store_sems = bufs, load_sems, store_sems
        self.loads, self.stores = [], []

    def load(self, i, src):
        cp = pltpu.make_async_copy(src, self.bufs[i], self.load_sems.at[i])
        cp.start()
        self.loads.append(cp)

    def store(self, i, dst):
        self.loads[i].wait()
        cp = pltpu.make_async_copy(self.bufs[i], dst, self.store_sems.at[i])
        cp.start()
        self.stores.append(cp)

    def finish(self):
        for cp in self.stores:
            cp.wait()


def _staging(blocks):
    n = len(blocks)
    return [pltpu.VMEM(b.shape, b.dtype) for b in blocks] + [pltpu.SemaphoreType.DMA((n,)), pltpu.SemaphoreType.DMA((n,))]


def gather_weight_shards(shards):
    n = len(shards)

    def body(*refs):
        ins, outs = refs[:n], refs[n:2 * n]
        send_sems, recv_sems, pass_send_sems, pass_recv_sems = refs[2 * n:2 * n + 4]
        own = _StagedCopies(refs[2 * n + 4:3 * n + 4], *refs[3 * n + 4:])
        x, y, c = _place()
        me = 2 * x + y
        chips = _other_chips(x, y)

        def half(i, block, which):
            rows = ins[i].shape[0] // 2
            return outs[i].at[block, pl.ds(which * rows, rows), :]

        def my_half(i):
            rows = ins[i].shape[0] // 2
            return ins[i].at[pl.ds(c * rows, rows), :]

        for i in range(n):
            own.load(i, ins[i])
        sends = []
        for i in range(n):
            for k, (tx, ty) in enumerate(chips):
                cp = pltpu.make_async_remote_copy(src_ref=my_half(i), dst_ref=half(i, me, c), send_sem=send_sems.at[i, k],
                                                  recv_sem=recv_sems.at[i, k], device_id=(tx, ty, c), device_id_type=MESH)
                cp.start()
                sends.append(cp)
        for i in range(n):
            own.store(i, outs[i].at[me])
        for i in range(n):
            for k, (tx, ty) in enumerate(chips):
                landed = half(i, 2 * tx + ty, c)
                pltpu.make_async_remote_copy(src_ref=my_half(i), dst_ref=landed, send_sem=send_sems.at[i, k],
                                             recv_sem=recv_sems.at[i, k], device_id=(tx, ty, c), device_id_type=MESH).wait_recv()
                cp = pltpu.make_async_remote_copy(src_ref=landed, dst_ref=landed, send_sem=pass_send_sems.at[i, k],
                                                  recv_sem=pass_recv_sems.at[i, k], device_id=(x, y, 1 - c), device_id_type=MESH)
                cp.start()
                sends.append(cp)
        for i in range(n):
            for k, (tx, ty) in enumerate(chips):
                missing = half(i, 2 * tx + ty, 1 - c)
                pltpu.make_async_remote_copy(src_ref=missing, dst_ref=missing, send_sem=pass_send_sems.at[i, k],
                                             recv_sem=pass_recv_sems.at[i, k], device_id=(x, y, 1 - c), device_id_type=MESH).wait_recv()
        for cp in sends:
            cp.wait_send()
        own.finish()

    sems = pltpu.SemaphoreType.DMA((n, 3))
    return pl.pallas_call(
        body, name="gather_weight_shards",
        in_specs=[ANY] * n, out_specs=[ANY] * n,
        out_shape=[_sds((4,) + s.shape, s.dtype) for s in shards],
        scratch_shapes=[sems, sems, sems, sems] + _staging(shards),
        compiler_params=_params(),
    )(*shards)


def exchange_halves_with_sibling(grads):
    n = len(grads)

    def body(*refs):
        ins, outs = refs[:n], refs[n:2 * n]
        send_sems, recv_sems = refs[2 * n:]
        x, y, c = _place()
        copies = []
        for i in range(n):
            half = ins[i].shape[1] // 2
            src = ins[i].at[:, pl.ds((1 - c) * half, half), :]
            cp = pltpu.make_async_remote_copy(src_ref=src, dst_ref=outs[i], send_sem=send_sems.at[i], recv_sem=recv_sems.at[i],
                                              device_id=(x, y, 1 - c), device_id_type=MESH)
            cp.start()
            copies.append(cp)
        for cp in copies:
            cp.wait()

    return pl.pallas_call(
        body, name="exchange_halves_with_sibling",
        in_specs=[ANY] * n, out_specs=[ANY] * n,
        out_shape=[_sds((g.shape[0], g.shape[1] // 2, g.shape[2])) for g in grads],
        scratch_shapes=[pltpu.SemaphoreType.DMA((n,)), pltpu.SemaphoreType.DMA((n,))],
    )(*grads)


def swap_halves_with_sibling(pieces):
    n = len(pieces)

    def body(*refs):
        ins, outs = refs[:n], refs[n:2 * n]
        send_sems, recv_sems = refs[2 * n:2 * n + 2]
        own = _StagedCopies(refs[2 * n + 2:3 * n + 2], *refs[3 * n + 2:])
        x, y, c = _place()
        for i in range(n):
            own.load(i, ins[i])
        swaps = []
        for i in range(n):
            cp = pltpu.make_async_remote_copy(src_ref=ins[i], dst_ref=outs[i].at[c], send_sem=send_sems.at[i], recv_sem=recv_sems.at[i],
                                              device_id=(x, y, 1 - c), device_id_type=MESH)
            cp.start()
            swaps.append(cp)
        for i in range(n):
            own.store(i, outs[i].at[c])
        for i in range(n):
            pltpu.make_async_remote_copy(src_ref=ins[i], dst_ref=outs[i].at[1 - c], send_sem=send_sems.at[i], recv_sem=recv_sems.at[i],
                                         device_id=(x, y, 1 - c), device_id_type=MESH).wait_recv()
        for cp in swaps:
            cp.wait_send()
        own.finish()

    return pl.pallas_call(
        body, name="swap_halves_with_sibling",
        in_specs=[ANY] * n, out_specs=[ANY] * n,
        out_shape=[_sds((2,) + p.shape) for p in pieces],
        scratch_shapes=[pltpu.SemaphoreType.DMA((n,)), pltpu.SemaphoreType.DMA((n,))] + _staging(pieces),
        compiler_params=_params(),
    )(*pieces)


IN_HBM = pl.BlockSpec(memory_space=pltpu.HBM)
SEMAPHORES = pl.BlockSpec(memory_space=pltpu.SEMAPHORE)
DATAFLOW = pltpu.SideEffectType.DATAFLOW_SIDE_EFFECTING


def _hbm(a):
    return pltpu.with_memory_space_constraint(a, pltpu.HBM)


def place_own_blocks(shards):
    n = len(shards)

    def body(*refs):
        ins, outs = refs[:n], refs[n:2 * n]
        own = _StagedCopies(refs[2 * n:3 * n], *refs[3 * n:])
        x, y, _ = _place()
        for i in range(n):
            own.load(i, ins[i])
        for i in range(n):
            own.store(i, outs[i].at[2 * x + y])
        own.finish()

    return pl.pallas_call(
        body, name="place_own_blocks", in_specs=[ANY] * n, out_specs=[ANY] * n,
        out_shape=[_sds((4,) + s.shape, s.dtype) for s in shards],
        scratch_shapes=_staging(shards), compiler_params=_params(),
    )(*shards)


def _block_to_send(ref, chip, per_target):
    if not per_target:
        return ref
    return ref.at[chip] if ref.shape[0] == 4 else ref.at[0]


def start_sends_to_chips(name, sources, landings, per_target, after):
    n = len(sources)
    n_sems = 2 * 3 * n

    def body(*refs):
        srcs = refs[:n]
        sems = refs[2 * n + 1:2 * n + 1 + n_sems]
        lands = refs[2 * n + 1 + n_sems:3 * n + 1 + n_sems]
        token = refs[3 * n + 1 + n_sems]
        x, y, c = _place()
        me = 2 * x + y
        for i in range(n):
            for k, (tx, ty) in enumerate(_other_chips(x, y)):
                src = _block_to_send(srcs[i], 2 * tx + ty, per_target)
                pltpu.make_async_remote_copy(src_ref=src, dst_ref=lands[i].at[me], send_sem=sems[2 * (3 * i + k)], recv_sem=sems[2 * (3 * i + k) + 1],
                                             device_id=(tx, ty, c), device_id_type=MESH).start()
        token[...] = jnp.zeros_like(token)

    outs = pl.pallas_call(
        body, name=name,
        in_specs=[IN_HBM] * (2 * n) + [ANY],
        out_specs=[SEMAPHORES] * n_sems + [IN_HBM] * n + [pl.BlockSpec(memory_space=pltpu.VMEM)],
        out_shape=[pltpu.SemaphoreType.DMA(())] * n_sems + [pltpu.HBM(l.shape, l.dtype) for l in landings] + [_sds(TOKEN_SHAPE)],
        input_output_aliases={n + i: n_sems + i for i in range(n)},
        compiler_params=pltpu.CompilerParams(has_side_effects=DATAFLOW),
    )(*[_hbm(s) for s in sources], *[_hbm(l) for l in landings], after)
    return list(outs[:n_sems]), list(outs[n_sems:n_sems + n]), outs[n_sems + n]


def wait_sends_to_chips(name, sources, landings, sems, per_target, after):
    n = len(sources)
    n_sems = len(sems)

    def body(*refs):
        srcs = refs[:n]
        sem_refs = refs[2 * n:2 * n + n_sems]
        lands = refs[2 * n + n_sems + 1:]
        x, y, c = _place()
        me = 2 * x + y
        for i in range(n):
            for k, (tx, ty) in enumerate(_other_chips(x, y)):
                src = _block_to_send(srcs[i], me, per_target)
                cp = pltpu.make_async_remote_copy(src_ref=src, dst_ref=lands[i].at[2 * tx + ty], send_sem=sem_refs[2 * (3 * i + k)],
                                                  recv_sem=sem_refs[2 * (3 * i + k) + 1], device_id=(tx, ty, c), device_id_type=MESH)
                cp.wait_send()
                cp.wait_recv()

    return pl.pallas_call(
        body, name=name,
        in_specs=[IN_HBM] * (2 * n) + [SEMAPHORES] * n_sems + [ANY],
        out_specs=[IN_HBM] * n,
        out_shape=[pltpu.HBM(l.shape, l.dtype) for l in landings],
        input_output_aliases={n + i: i for i in range(n)},
        compiler_params=pltpu.CompilerParams(has_side_effects=DATAFLOW),
    )(*[_hbm(s) for s in sources], *landings, *sems, after)


def _row_tile(rows, cols):
    tm = rows
    while tm * cols * 4 > (2 << 20) and tm % 16 == 0:
        tm //= 2
    return tm


def add_pairs(a_list, b_list, out_dtypes, copies=1):
    n = len(a_list)
    nb = a_list[0].shape[0]

    def body(*refs):
        for i in range(n):
            total = (refs[i][...] + refs[n + i][...]).astype(out_dtypes[i])
            for o_ref in refs[2 * n + i * copies:2 * n + (i + 1) * copies]:
                o_ref[...] = total

    specs = [pl.BlockSpec((None,) + a.shape[1:], lambda j: (j, 0, 0)) for a in a_list]
    outs = pl.pallas_call(
        body, name="add_pairs", grid=(nb,), in_specs=specs * 2, out_specs=[s for s in specs for _ in range(copies)],
        out_shape=[_sds(a.shape, dt) for a, dt in zip(a_list, out_dtypes) for _ in range(copies)],
        compiler_params=_params(("parallel",)),
    )(*a_list, *b_list)
    return [tuple(outs[i * copies:(i + 1) * copies]) for i in range(n)]


def sum_fours(arrays, token=None):
    n = len(arrays)
    extra, extra_specs = _after(token)

    def body(*refs):
        outs = refs[-n:]
        for a_ref, o_ref in zip(refs[:n], outs):
            o_ref[...] = ((a_ref[0].astype(F32) + a_ref[1].astype(F32)) + a_ref[2].astype(F32)) + a_ref[3].astype(F32)

    return pl.pallas_call(
        body, name="sum_fours", grid=(2,),
        in_specs=[pl.BlockSpec((4, a.shape[1] // 2, a.shape[2]), lambda i: (0, i, 0)) for a in arrays] + extra_specs,
        out_specs=[pl.BlockSpec((a.shape[1] // 2, a.shape[2]), lambda i: (i, 0)) for a in arrays],
        out_shape=[_sds(a.shape[1:]) for a in arrays], compiler_params=_params(("parallel",)),
    )(*arrays, *extra)


def _adamw_update(w_ref, g_ref, m_ref, v_ref, d_ref, nm_ref, nv_ref):
    g = g_ref[...]
    nm = ADAM_B1 * m_ref[...] + (1.0 - ADAM_B1) * g
    nv = ADAM_B2 * v_ref[...] + (1.0 - ADAM_B2) * (g * g)
    d_ref[...] = -ADAM_LR * ((nm / (1.0 - ADAM_B1 ** ADAM_STEP)) / (jnp.sqrt(nv / (1.0 - ADAM_B2 ** ADAM_STEP)) + ADAM_EPS) + ADAM_WD * w_ref[...])
    nm_ref[...] = nm
    nv_ref[...] = nv


def adamw(w, g, m, v):
    rows, cols = w.shape
    tm = _row_tile(rows, cols)

    def body(*refs):
        _adamw_update(*refs)

    spec = pl.BlockSpec((tm, cols), lambda i: (i, 0))
    return pl.pallas_call(
        body, name="adamw", grid=(rows // tm,), in_specs=[spec] * 4, out_specs=[spec] * 3,
        out_shape=[_sds(w.shape)] * 3, compiler_params=_params(("parallel",)),
    )(w, g, m, v)


def adamw_small(ws, gs, ms, vs, slabs=None):
    n = len(ws)

    def body(*refs):
        for i in range(n):
            _adamw_update(refs[i], refs[n + i], refs[2 * n + i], refs[3 * n + i], refs[4 * n + i], refs[5 * n + i], refs[6 * n + i])

    if slabs is None:
        grid = ()
        specs = [pl.BlockSpec(memory_space=pltpu.VMEM)] * n
    else:
        grid = (slabs,)
        specs = [pl.BlockSpec((w.shape[0] // slabs,) + w.shape[1:], lambda i: (i, 0, 0)) for w in ws]
    outs = pl.pallas_call(
        body, name="adamw_small", grid=grid, in_specs=specs * 4, out_specs=specs * 3,
        out_shape=[_sds(w.shape) for w in ws] * 3, compiler_params=_params(("parallel",) if slabs else None),
    )(*ws, *gs, *ms, *vs)
    return outs[:n], outs[n:2 * n], outs[2 * n:]


PACK_TILE = 8 * LANES
PACK_PIECES = 8
PACK_ALIGN = PACK_PIECES * 16


def _pack_small(values, scalar=None):
    parts = []
    for name in SMALL_NAMES:
        flat = values[name].reshape(-1)
        pad = (-flat.shape[0]) % PACK_TILE
        if pad:
            flat = jnp.concatenate([flat, jnp.zeros((pad,), F32)])
        parts.append(flat.reshape(-1, LANES))
    rows = sum(p.shape[0] for p in parts) + 8
    parts.append(jnp.zeros(((-rows) % PACK_ALIGN, LANES), F32))
    last = jnp.zeros((8, LANES), F32)
    parts.append(last if scalar is None else jnp.broadcast_to(scalar.astype(F32), (8, LANES)))
    return jnp.concatenate(parts, axis=0)


def _unpack_small(pack, like):
    out = {}
    row = 0
    for name in SMALL_NAMES:
        size = math.prod(like[name].shape)
        rows = -(-size // PACK_TILE) * 8
        out[name] = pack[row:row + rows].reshape(-1)[:size].reshape(like[name].shape)
        row += rows
    return out


def _is_column_sharded(name):
    return name.endswith("w_in")


def _to_blocks(name, full):
    if full.ndim == 3:
        return full
    if _is_column_sharded(name):
        rows, cols = full.shape
        return full.reshape(rows, 4, cols // 4).transpose(1, 0, 2)
    return full.reshape(4, full.shape[0] // 4, full.shape[1])


def _from_blocks(name, stacked):
    if _is_column_sharded(name):
        if stacked.shape[2] % LANES == 0:
            return stacked
        return stacked.transpose(1, 0, 2).reshape(stacked.shape[1], 4 * stacked.shape[2])
    return stacked.reshape(4 * stacked.shape[1], stacked.shape[2])


def _layer_big_names(i):
    return [n for n in BIG_NAMES if n.startswith("l%d_" % i)]


class _OverlappedExchanges:
    def __init__(self, weights):
        self.weights = weights
        self.c = lax.axis_index("c")
        self.first = _layer_big_names(0)
        self.later = [n for n in BIG_NAMES if n not in self.first]
        shards = [weights[n].astype(MXU_DTYPE) for n in self.first + self.later]
        placed = place_own_blocks(shards)
        k = len(self.first)
        sems, stacks, token = start_sends_to_chips("gather_first_start", shards[:k], placed[:k], False, shards[0])
        self.gather_first = (shards[:k], sems, stacks)
        sems, stacks, token = start_sends_to_chips("gather_later_start", shards[k:], placed[k:], False, token)
        self.gather_later = (shards[k:], sems, stacks)
        self.s5 = {}
        for i in (0, 2):
            self.s5[i] = _s5_matrices({n: weights["l%d_%s" % (i, n)] for n in SSM_NAMES if "l%d_%s" % (i, n) in SMALL_NAMES}, token)
            token = self.s5[i][1]["kd"]
        self.full = {}
        self.in_flight = {}
        self.contributions = {}

    def first_token(self):
        return None

    def layer(self, i, x):
        if i == 0:
            shards, sems, stacks = self.gather_first
            stacks = wait_sends_to_chips("gather_first_wait", shards, stacks, sems, False, self.s5[2][1]["kd"])
            self.full.update({n: _from_blocks(n, g) for n, g in zip(self.first, stacks)})
        if i == 1:
            shards, sems, stacks = self.gather_later
            stacks = wait_sends_to_chips("gather_later_wait", shards, stacks, sems, False, x)
            self.full.update({n: _from_blocks(n, g) for n, g in zip(self.later, stacks)})
        names = SSM_NAMES if i % 2 == 0 else ATTN_NAMES
        w = {n: self.full.get("l%d_%s" % (i, n), self.weights.get("l%d_%s" % (i, n))) for n in names}
        if i in self.s5:
            w["s5"] = self.s5[i]
        return w

    def chip_sums(self, names, grads, extra_blocks=(), copies=1):
        blocks = [_to_blocks(n, grads[n]) for n in names] + list(extra_blocks)
        from_sibling = exchange_halves_with_sibling(blocks)
        mine = [lax.dynamic_slice_in_dim(b, self.c * (b.shape[1] // 2), b.shape[1] // 2, axis=1) for b in blocks]
        k = len(names)
        sums = add_pairs(mine[:k], from_sibling[:k], [WIRE_DTYPE] * k, copies)
        if extra_blocks:
            sums += add_pairs(mine[k:], from_sibling[k:], [F32] * len(extra_blocks), copies)
        return sums

    def layer_done(self, i, grads, dx):
        if i + 1 in self.in_flight:
            names, sums, sems, landings = self.in_flight.pop(i + 1)
            done = wait_sends_to_chips("scatter_wait_l%d" % (i + 1), sums, landings, sems, True, dx)
            self.contributions.update(zip(names, done))
        if i == 0:
            return None
        names = _layer_big_names(i)
        pairs = self.chip_sums(names, grads, copies=2)
        sums = [p[0] for p in pairs]
        sems, landings, token = start_sends_to_chips("scatter_start_l%d" % i, sums, [p[1] for p in pairs], True, sums[0])
        self.in_flight[i] = (names, sums, sems, landings)
        return token


def _train_step(x, loss_target, weights, moments_m, moments_v):
    hooks = _OverlappedExchanges(weights)
    loss, dx, grads = _sequence_step(x[0], loss_target[0], weights["final_norm"], hooks)
    small_pack = _pack_small({n: grads[n] for n in SMALL_NAMES}, scalar=loss)
    last = _layer_big_names(0)
    pairs = hooks.chip_sums(last, grads, extra_blocks=[small_pack[None]], copies=2)
    sums = [p[0] for p in pairs]
    landings = [p[1] for p in pairs[:-1]] + [jnp.broadcast_to(sums[-1], (4,) + sums[-1].shape[1:])]
    sems, landings, token = start_sends_to_chips("scatter_start_l0", sums, landings, True, sums[0])
    out_grad, out_delta, out_m, out_v = {}, {}, {}, {}

    def finish(names, arrays, token=None):
        shared = swap_halves_with_sibling(sum_fours(arrays, token))
        for n, s in zip(names, shared):
            if n == "small":
                return s.reshape(-1, LANES)
            out_grad[n] = s.reshape(2 * s.shape[1], s.shape[2])
            out_delta[n], out_m[n], out_v[n] = adamw(weights[n], out_grad[n], moments_m[n], moments_v[n])

    others = [n for n in BIG_NAMES if n not in last]
    finish(others, [hooks.contributions[n] for n in others], token)
    arrived = wait_sends_to_chips("scatter_wait_l0", sums, landings, sems, True, out_v[others[-1]])
    small_grad_pack = finish(last + ["small"], arrived)
    loss = small_grad_pack[-8, 0]
    out_grad.update(_unpack_small(small_grad_pack, {n: weights[n] for n in SMALL_NAMES}))
    cubes = [n for n in SMALL_NAMES if weights[n].ndim == 3]
    for names, slabs in ((cubes, 8), ([n for n in SMALL_NAMES if n not in cubes], None)):
        deltas, new_ms, new_vs = adamw_small(*[[group[n] for n in names] for group in (weights, out_grad, moments_m, moments_v)], slabs=slabs)
        out_delta.update(zip(names, deltas))
        out_m.update(zip(names, new_ms))
        out_v.update(zip(names, new_vs))
    outs = [loss, dx[None]]
    for group in (out_grad, out_delta, out_m, out_v):
        outs.extend(group[n] for n in WEIGHT_NAMES)
    return tuple(outs)


def kernel(x, l0_norm, l0_w_in, l0_a_re, l0_a_im, l0_log_step, l0_b_re, l0_b_im, l0_c_re, l0_c_im, l0_d, l0_w_glu, l0_b_glu, l0_w_out, l1_norm, l1_w_in, l1_sinks, l1_w_out, l2_norm, l2_w_in, l2_a_re, l2_a_im, l2_log_step, l2_b_re, l2_b_im, l2_c_re, l2_c_im, l2_d, l2_w_glu, l2_b_glu, l2_w_out, l3_norm, l3_w_in, l3_sinks, l3_w_out, final_norm, loss_target, m_l0_norm, m_l0_w_in, m_l0_a_re, m_l0_a_im, m_l0_log_step, m_l0_b_re, m_l0_b_im, m_l0_c_re, m_l0_c_im, m_l0_d, m_l0_w_glu, m_l0_b_glu, m_l0_w_out, m_l1_norm, m_l1_w_in, m_l1_sinks, m_l1_w_out, m_l2_norm, m_l2_w_in, m_l2_a_re, m_l2_a_im, m_l2_log_step, m_l2_b_re, m_l2_b_im, m_l2_c_re, m_l2_c_im, m_l2_d, m_l2_w_glu, m_l2_b_glu, m_l2_w_out, m_l3_norm, m_l3_w_in, m_l3_sinks, m_l3_w_out, m_final_norm, v_l0_norm, v_l0_w_in, v_l0_a_re, v_l0_a_im, v_l0_log_step, v_l0_b_re, v_l0_b_im, v_l0_c_re, v_l0_c_im, v_l0_d, v_l0_w_glu, v_l0_b_glu, v_l0_w_out, v_l1_norm, v_l1_w_in, v_l1_sinks, v_l1_w_out, v_l2_norm, v_l2_w_in, v_l2_a_re, v_l2_a_im, v_l2_log_step, v_l2_b_re, v_l2_b_im, v_l2_c_re, v_l2_c_im, v_l2_d, v_l2_w_glu, v_l2_b_glu, v_l2_w_out, v_l3_norm, v_l3_w_in, v_l3_sinks, v_l3_w_out, v_final_norm):
    args = locals()
    weights = {n: args[n] for n in WEIGHT_NAMES}
    moments_m = {n: args["m_" + n] for n in WEIGHT_NAMES}
    moments_v = {n: args["v_" + n] for n in WEIGHT_NAMES}
    return _train_step(x, loss_target, weights, moments_m, moments_v)
```

```python
import functools
import math

import jax
import jax.numpy as jnp
from jax import lax
from jax.experimental import pallas as pl
from jax.experimental.pallas import tpu as pltpu

F32 = jnp.float32
MXU_DTYPE = jnp.bfloat16
WIRE_DTYPE = jnp.bfloat16
MESH = pl.DeviceIdType.MESH

D_MODEL = 1024
BRANCH = 1024
NORM_EPS = 1e-5
SSM_GROUPS = 64
SSM_GROUP = 16
SSM_STATE = 64
S5_CHUNK = 16
LANES = 128
S5_OCT = LANES // SSM_GROUP
S5_OCTETS = SSM_GROUPS // S5_OCT
S5_OCT_IN = S5_CHUNK * LANES
S5_OCT_STATE = S5_OCT * SSM_STATE
S5_STATES = SSM_GROUPS * SSM_STATE
HEAD_DIM = 64
N_Q_HEADS = 16
N_KV_HEADS = 2
GQA_GROUP = N_Q_HEADS // N_KV_HEADS
ATTN_BLOCK = 128
Q_DIM = N_Q_HEADS * HEAD_DIM
KV_DIM = N_KV_HEADS * HEAD_DIM
ROPE_THETA = 10000.0
NEG_INF = -1e30
ADAM_LR = 0.001
ADAM_B1 = 0.9
ADAM_B2 = 0.999
ADAM_EPS = 1e-08
ADAM_WD = 0.01
ADAM_STEP = 10

VMEM_LIMIT_V7X = 56 * 1024 * 1024
ROW_TILE_FWD = 512
ROW_TILE_BWD = 512

SSM_NAMES = ("norm", "w_in", "a_re", "a_im", "log_step", "b_re", "b_im", "c_re", "c_im", "d", "w_glu", "b_glu", "w_out")
ATTN_NAMES = ("norm", "w_in", "sinks", "w_out")


def _weight_names():
    names = []
    for i in range(4):
        for n in (SSM_NAMES if i % 2 == 0 else ATTN_NAMES):
            names.append("l%d_%s" % (i, n))
    names.append("final_norm")
    return names


WEIGHT_NAMES = _weight_names()
BIG_NAMES = [n for n in WEIGHT_NAMES if n.endswith(("w_in", "w_glu", "w_out"))]
SMALL_NAMES = [n for n in WEIGHT_NAMES if n not in BIG_NAMES]


def _params(semantics=None):
    return pltpu.CompilerParams(dimension_semantics=semantics, vmem_limit_bytes=VMEM_LIMIT_V7X)


def _rows(tm, n):
    return pl.BlockSpec((tm, n), lambda i: (i, 0))


def _whole(shape):
    return pl.BlockSpec(shape, lambda i: (0,) * len(shape), pipeline_mode=pl.Buffered(1))


def _sds(shape, dtype=F32):
    return jax.ShapeDtypeStruct(shape, dtype)


def _mm(a, b):
    return jnp.dot(a.astype(MXU_DTYPE), b.astype(MXU_DTYPE), preferred_element_type=F32)


def _mm_tn(a, b):
    return lax.dot_general(a.astype(MXU_DTYPE), b.astype(MXU_DTYPE), (((0,), (0,)), ((), ())), preferred_element_type=F32)


def _mm_nt(a, b):
    return lax.dot_general(a.astype(MXU_DTYPE), b.astype(MXU_DTYPE), (((1,), (1,)), ((), ())), preferred_element_type=F32)


def _sigmoid(x):
    return 0.5 + 0.5 * jnp.tanh(0.5 * x)


def _silu(x):
    return x * _sigmoid(x)


def _silu_and_grad(x):
    s = _sigmoid(x)
    return x * s, s * (1.0 + x * (1.0 - s))


GELU_C0 = math.sqrt(2.0 / math.pi)
GELU_C1 = 0.044715


def _gelu(x):
    return 0.5 * x * (1.0 + jnp.tanh(GELU_C0 * (x + GELU_C1 * x * x * x)))


def _gelu_and_grad(x):
    x2 = x * x
    th = jnp.tanh(GELU_C0 * x * (1.0 + GELU_C1 * x2))
    half = 0.5 + 0.5 * th
    return x * half, half + 0.5 * x * (1.0 - th * th) * (GELU_C0 + 3.0 * GELU_C0 * GELU_C1 * x2)


def _rms(x, g):
    r = lax.rsqrt(jnp.mean(x * x, axis=-1, keepdims=True) + NORM_EPS)
    xhat = x * r
    return r, xhat, xhat * g


def _rms_bwd(dh, g, r, xhat):
    dxhat = dh * g
    dx = r * (dxhat - xhat * jnp.mean(dxhat * xhat, axis=-1, keepdims=True))
    return dx, jnp.sum(dh * xhat, axis=0, keepdims=True)


def _swap_half_heads(x):
    n = x.shape[-1]
    lane = lax.broadcasted_iota(jnp.int32, x.shape, x.ndim - 1)
    first = (lane % HEAD_DIM) < (HEAD_DIM // 2)
    return jnp.where(first, pltpu.roll(x, n - HEAD_DIM // 2, x.ndim - 1), pltpu.roll(x, HEAD_DIM // 2, x.ndim - 1))


def _tile_lanes(t, reps):
    return jnp.concatenate([t] * reps, axis=1)


TOKEN_SHAPE = (8, LANES)


def _after(token):
    return ([], []) if token is None else ([token], [_whole(TOKEN_SHAPE)])


def ssm_proj_fwd(x, norm, w_in):
    t = x.shape[0]
    tm = min(ROW_TILE_FWD, t)

    def body(x_ref, g_ref, w_ref, u_ref, gate_ref):
        _, _, h = _rms(x_ref[...], g_ref[...])
        h = h.astype(MXU_DTYPE)
        half = BRANCH // 2
        for j in range(2):
            u_ref[:, j * half:(j + 1) * half] = _mm(h, w_ref[j])
            gate_ref[:, j * half:(j + 1) * half] = _mm(h, w_ref[2 + j])

    return pl.pallas_call(
        body, name="ssm_proj_fwd", grid=(t // tm,),
        in_specs=[_rows(tm, D_MODEL), _whole((1, D_MODEL)), _whole((4, D_MODEL, BRANCH // 2))],
        out_specs=[_rows(tm, BRANCH), _rows(tm, BRANCH)],
        out_shape=[_sds((t, BRANCH)), _sds((t, BRANCH))],
        compiler_params=_params(("parallel",)),
    )(x, norm, w_in)


def _chunk_rows(ref, nk, dtype=None):
    rows = jnp.concatenate([ref[pl.ds(s, nk, stride=S5_CHUNK), :] for s in range(S5_CHUNK)], axis=1)
    return rows.astype(MXU_DTYPE if dtype is None else dtype)


def _store_chunk_rows(ref, val, nk):
    for s in range(S5_CHUNK):
        ref[pl.ds(s, nk, stride=S5_CHUNK), :] = val[:, s * LANES:(s + 1) * LANES]


def _own_group_mask():
    row = lax.broadcasted_iota(jnp.int32, (S5_OCT_IN, S5_OCT_STATE), 0)
    col = lax.broadcasted_iota(jnp.int32, (S5_OCT_IN, S5_OCT_STATE), 1)
    return ((row % LANES) // SSM_GROUP) == (col // SSM_STATE)


def _spread_groups(w):
    return jnp.where(_own_group_mask(), jnp.concatenate([w] * (S5_OCT_STATE // LANES), axis=1), 0.0).astype(MXU_DTYPE)


def _fold_groups(p):
    p = jnp.where(_own_group_mask(), p, 0.0)
    return sum(p[:, q * LANES:(q + 1) * LANES] for q in range(S5_OCT_STATE // LANES))


def _fill_toeplitz(win_ref, kd_ref):
    win_ref[...] = jnp.zeros_like(win_ref)
    for s in range(S5_CHUNK):
        for t in range(s, S5_CHUNK):
            win_ref[s * LANES:(s + 1) * LANES, t * LANES:(t + 1) * LANES] = kd_ref[t - s].astype(MXU_DTYPE)


TOEPLITZ_BLOCK = 512
_TOEPLITZ_BLOCKS = [(lo, lo + TOEPLITZ_BLOCK) for lo in range(0, S5_OCT_IN, TOEPLITZ_BLOCK)]


def _strip(t):
    return pl.BlockSpec((t, LANES), lambda b: (0, b))


def _oct_states(nk):
    return pl.BlockSpec((nk, S5_OCT_STATE), lambda b: (0, b))


OCT_W = pl.BlockSpec((None, S5_OCT_IN, LANES), lambda b: (b, 0, 0))
OCT_KD = pl.BlockSpec((None, S5_CHUNK, LANES, LANES), lambda b: (b, 0, 0, 0))


def s5_chunk_states(u, ws_re, ws_im):
    t = u.shape[0]
    nk = t // S5_CHUNK

    def body(u_ref, wr_ref, wi_ref, re_ref, im_ref):
        uc = _chunk_rows(u_ref, nk)
        re_ref[...] = _mm(uc, _spread_groups(wr_ref[...]))
        im_ref[...] = _mm(uc, _spread_groups(wi_ref[...]))

    return pl.pallas_call(
        body, name="s5_chunk_states", grid=(S5_OCTETS,),
        in_specs=[_strip(t), OCT_W, OCT_W], out_specs=[_oct_states(nk), _oct_states(nk)],
        out_shape=[_sds((nk, S5_STATES)), _sds((nk, S5_STATES))],
        compiler_params=_params(("parallel",)),
    )(u, ws_re, ws_im)


def s5_scan_fwd(s_re, s_im, a_re, a_im):
    nk = s_re.shape[0]

    def body(sre_ref, sim_ref, ar_ref, ai_ref, hre_ref, him_ref):
        ar = ar_ref[...]
        ai = ai_ref[...]

        def step(k, carry):
            hr, hi = carry
            hre_ref[pl.ds(k, 1), :] = hr
            him_ref[pl.ds(k, 1), :] = hi
            sr = sre_ref[pl.ds(k, 1), :]
            si = sim_ref[pl.ds(k, 1), :]
            return ar * hr - ai * hi + sr, ai * hr + ar * hi + si

        zero = jnp.zeros((1, S5_STATES), F32)
        lax.fori_loop(0, nk, step, (zero, zero))

    vm = pl.BlockSpec(memory_space=pltpu.VMEM)
    return pl.pallas_call(
        body, name="s5_scan_fwd", in_specs=[vm, vm, vm, vm], out_specs=[vm, vm],
        out_shape=[_sds((nk, S5_STATES)), _sds((nk, S5_STATES))],
        compiler_params=_params(),
    )(s_re, s_im, a_re, a_im)


def s5_outputs(u, h_re, h_im, kd, wo_re, wo_im):
    t = u.shape[0]
    nk = t // S5_CHUNK

    def body(u_ref, hre_ref, him_ref, kd_ref, wor_ref, woi_ref, y_ref, win_ref):
        _fill_toeplitz(win_ref, kd_ref)
        uc = _chunk_rows(u_ref, nk)
        y = jnp.concatenate([_mm(uc[:, :hi], win_ref[:hi, lo:hi]) for lo, hi in _TOEPLITZ_BLOCKS], axis=1)
        y = y + _mm_nt(hre_ref[...], _spread_groups(wor_ref[...])) + _mm_nt(him_ref[...], _spread_groups(woi_ref[...]))
        _store_chunk_rows(y_ref, y, nk)

    return pl.pallas_call(
        body, name="s5_outputs", grid=(S5_OCTETS,),
        in_specs=[_strip(t), _oct_states(nk), _oct_states(nk), OCT_KD, OCT_W, OCT_W],
        out_specs=_strip(t), out_shape=_sds((t, BRANCH)),
        scratch_shapes=[pltpu.VMEM((S5_OCT_IN, S5_OCT_IN), MXU_DTYPE)],
        compiler_params=_params(("parallel",)),
    )(u, h_re, h_im, kd, wo_re, wo_im)


def s5_state_grads(dy, wo_re, wo_im):
    t = dy.shape[0]
    nk = t // S5_CHUNK

    def body(dy_ref, wor_ref, woi_ref, re_ref, im_ref):
        dyc = _chunk_rows(dy_ref, nk)
        re_ref[...] = _mm(dyc, _spread_groups(wor_ref[...]))
        im_ref[...] = _mm(dyc, _spread_groups(woi_ref[...]))

    return pl.pallas_call(
        body, name="s5_state_grads", grid=(S5_OCTETS,),
        in_specs=[_strip(t), OCT_W, OCT_W], out_specs=[_oct_states(nk), _oct_states(nk)],
        out_shape=[_sds((nk, S5_STATES)), _sds((nk, S5_STATES))],
        compiler_params=_params(("parallel",)),
    )(dy, wo_re, wo_im)


def s5_scan_bwd(dh_re, dh_im, h_re, h_im, a_re, a_im):
    nk = dh_re.shape[0]

    def body(dhr_ref, dhi_ref, hr_ref, hi_ref, ar_ref, ai_ref, dsr_ref, dsi_ref, dar_ref, dai_ref):
        ar = ar_ref[...]
        ai = ai_ref[...]

        dar_ref[...] = jnp.zeros_like(dar_ref)
        dai_ref[...] = jnp.zeros_like(dai_ref)

        def step(i, carry):
            gr, gi = carry
            k = nk - 1 - i
            dhr = dhr_ref[pl.ds(k, 1), :]
            dhi = dhi_ref[pl.ds(k, 1), :]
            dsr_ref[pl.ds(k, 1), :] = gr
            dsi_ref[pl.ds(k, 1), :] = gi
            hr = hr_ref[pl.ds(k, 1), :]
            hi = hi_ref[pl.ds(k, 1), :]
            dar_ref[...] += gr * hr + gi * hi
            dai_ref[...] += gi * hr - gr * hi
            return dhr + ar * gr + ai * gi, dhi - ai * gr + ar * gi

        zero = jnp.zeros((1, S5_STATES), F32)
        lax.fori_loop(0, nk, step, (zero, zero))

    vm = pl.BlockSpec(memory_space=pltpu.VMEM)
    return pl.pallas_call(
        body, name="s5_scan_bwd", in_specs=[vm] * 6, out_specs=[vm] * 4,
        out_shape=[_sds((nk, S5_STATES)), _sds((nk, S5_STATES)), _sds((1, S5_STATES)), _sds((1, S5_STATES))],
        input_output_aliases={0: 0, 1: 1}, compiler_params=_params(),
    )(dh_re, dh_im, h_re, h_im, a_re, a_im)


def s5_input_grads(dy, ds_re, ds_im, kd, ws_re, ws_im):
    t = dy.shape[0]
    nk = t // S5_CHUNK

    def body(dy_ref, dsr_ref, dsi_ref, kd_ref, wsr_ref, wsi_ref, du_ref, win_ref):
        _fill_toeplitz(win_ref, kd_ref)
        dyc = _chunk_rows(dy_ref, nk)
        du = jnp.concatenate([_mm_nt(dyc[:, lo:], win_ref[lo:hi, lo:]) for lo, hi in _TOEPLITZ_BLOCKS], axis=1)
        du = du + _mm_nt(dsr_ref[...], _spread_groups(wsr_ref[...])) + _mm_nt(dsi_ref[...], _spread_groups(wsi_ref[...]))
        _store_chunk_rows(du_ref, du, nk)

    return pl.pallas_call(
        body, name="s5_input_grads", grid=(S5_OCTETS,),
        in_specs=[_strip(t), _oct_states(nk), _oct_states(nk), OCT_KD, OCT_W, OCT_W],
        out_specs=_strip(t), out_shape=_sds((t, BRANCH)),
        scratch_shapes=[pltpu.VMEM((S5_OCT_IN, S5_OCT_IN), MXU_DTYPE)],
        compiler_params=_params(("parallel",)),
    )(dy, ds_re, ds_im, kd, ws_re, ws_im)


def s5_weight_grads(u, dy, h_re, h_im, ds_re, ds_im):
    t = u.shape[0]
    nk = t // S5_CHUNK

    def body(u_ref, dy_ref, hre_ref, him_ref, dsr_ref, dsi_ref, dkd_ref, dwsr_ref, dwsi_ref, dwor_ref, dwoi_ref):
        dyc = _chunk_rows(dy_ref, nk, F32)
        uct = _chunk_rows(u_ref, nk, F32).T.astype(MXU_DTYPE)
        dyct = dyc.T.astype(MXU_DTYPE)
        dyc = dyc.astype(MXU_DTYPE)
        dwsr_ref[...] = _fold_groups(_mm(uct, dsr_ref[...]))
        dwsi_ref[...] = _fold_groups(_mm(uct, dsi_ref[...]))
        dwor_ref[...] = _fold_groups(_mm(dyct, hre_ref[...]))
        dwoi_ref[...] = _fold_groups(_mm(dyct, him_ref[...]))
        dkd_ref[...] = jnp.zeros_like(dkd_ref)
        for tt in range(0, S5_CHUNK, 2):
            p = _mm(uct[:(tt + 2) * LANES], dyc[:, tt * LANES:(tt + 2) * LANES])
            for s in range(tt + 2):
                rows = p[s * LANES:(s + 1) * LANES]
                if s <= tt:
                    dkd_ref[tt - s] += rows[:, :LANES]
                dkd_ref[tt + 1 - s] += rows[:, LANES:]

    return pl.pallas_call(
        body, name="s5_weight_grads", grid=(S5_OCTETS,),
        in_specs=[_strip(t), _strip(t)] + [_oct_states(nk)] * 4,
        out_specs=[OCT_KD, OCT_W, OCT_W, OCT_W, OCT_W],
        out_shape=[_sds((S5_OCTETS, S5_CHUNK, LANES, LANES))] + [_sds((S5_OCTETS, S5_OCT_IN, LANES))] * 4,
        compiler_params=_params(("parallel",)),
    )(u, dy, h_re, h_im, ds_re, ds_im)


def ssm_mix_fwd(x, u, gate, y_scan, d, w_glu, b_glu, w_out):
    t = x.shape[0]
    tm = min(ROW_TILE_FWD, t)

    def body(x_ref, u_ref, gate_ref, ys_ref, d_ref, wg_ref, bg_ref, wo_ref, y_ref, g2_ref, xo_ref):
        y = ys_ref[...] + d_ref[...] * u_ref[...]
        z0 = _gelu(y)
        g2 = _mm(z0, wg_ref[...]) + bg_ref[...]
        a = z0 * _sigmoid(g2) * _silu(gate_ref[...])
        y_ref[...] = y
        g2_ref[...] = g2
        xo_ref[...] = x_ref[...] + _mm(a, wo_ref[...])

    row = _rows(tm, BRANCH)
    vec = _whole((1, BRANCH))
    mat = _whole((BRANCH, BRANCH))
    return pl.pallas_call(
        body, name="ssm_mix_fwd", grid=(t // tm,),
        in_specs=[row, row, row, row, vec, mat, vec, mat],
        out_specs=[row, row, row],
        out_shape=[_sds((t, BRANCH))] * 3,
        compiler_params=_params(("parallel",)),
    )(x, u, gate, y_scan, d, w_glu, b_glu, w_out)


def ssm_mix_bwd(dxo, u, gate, y, g2, w_glu, w_out, token=None):
    t = dxo.shape[0]
    tm = min(ROW_TILE_BWD, t)
    extra, extra_specs = _after(token)

    def body(dxo_ref, u_ref, gate_ref, y_ref, g2_ref, wgt_ref, wot_ref, *rest):
        dy_ref, dgate_ref, dwo_ref, dwg_ref, dbg_ref, dd_ref = rest[-6:]

        @pl.when(pl.program_id(0) == 0)
        def _():
            dwo_ref[...] = jnp.zeros_like(dwo_ref)
            dwg_ref[...] = jnp.zeros_like(dwg_ref)
            dbg_ref[...] = jnp.zeros_like(dbg_ref)
            dd_ref[...] = jnp.zeros_like(dd_ref)

        dxo = dxo_ref[...]
        gate = gate_ref[...]
        y = y_ref[...]
        z0, z0_grad = _gelu_and_grad(y)
        sg = _sigmoid(g2_ref[...])
        z = z0 * sg
        sgate, sgate_grad = _silu_and_grad(gate)
        da = _mm_nt(dxo, wot_ref[...])
        dwo_ref[...] += _mm_tn(z * sgate, dxo)
        dz = da * sgate
        dgate_ref[...] = da * z * sgate_grad
        dg2 = dz * z0 * sg * (1.0 - sg)
        dbg_ref[...] += jnp.sum(dg2, axis=0, keepdims=True)
        dwg_ref[...] += _mm_tn(z0, dg2)
        dz0 = dz * sg + _mm_nt(dg2, wgt_ref[...])
        dy = dz0 * z0_grad
        dd_ref[...] += jnp.sum(dy * u_ref[...], axis=0, keepdims=True)
        dy_ref[...] = dy

    row = _rows(tm, BRANCH)
    vec = _whole((1, BRANCH))
    mat = _whole((BRANCH, BRANCH))
    return pl.pallas_call(
        body, name="ssm_mix_bwd", grid=(t // tm,),
        in_specs=[row, row, row, row, row, mat, mat] + extra_specs,
        out_specs=[row, row, mat, mat, vec, vec],
        out_shape=[_sds((t, BRANCH)), _sds((t, BRANCH)), _sds((BRANCH, D_MODEL)), _sds((BRANCH, BRANCH)),
                   _sds((1, BRANCH)), _sds((1, BRANCH))],
        compiler_params=_params(("arbitrary",)),
    )(dxo, u, gate, y, g2, w_glu, w_out, *extra)


def ssm_proj_bwd(x, norm, dxo, dy, du_scan, dgate, d, w_in):
    t = x.shape[0]
    tm = min(ROW_TILE_BWD, t)
    n = 2 * BRANCH

    def body(x_ref, g_ref, dxo_ref, dy_ref, dus_ref, dgate_ref, d_ref, wt_ref, dx_ref, dw_ref, dg_ref):
        @pl.when(pl.program_id(0) == 0)
        def _():
            dw_ref[...] = jnp.zeros_like(dw_ref)
            dg_ref[...] = jnp.zeros_like(dg_ref)

        g = g_ref[...]
        r, xhat, h = _rms(x_ref[...], g)
        h = h.astype(MXU_DTYPE)
        du = dus_ref[...] + d_ref[...] * dy_ref[...]
        dproj = jnp.concatenate([du, dgate_ref[...]], axis=1).astype(MXU_DTYPE)
        dh = jnp.zeros((tm, D_MODEL), F32)
        for j in range(4):
            cols = dproj[:, j * (n // 4):(j + 1) * (n // 4)]
            dh = dh + _mm_nt(cols, wt_ref[j])
            dw_ref[j] += _mm_tn(h, cols)
        dx, dg = _rms_bwd(dh, g, r, xhat)
        dg_ref[...] += dg
        dx_ref[...] = dxo_ref[...] + dx

    row = _rows(tm, D_MODEL)
    vec = _whole((1, D_MODEL))
    blocks = _whole((4, D_MODEL, n // 4))
    return pl.pallas_call(
        body, name="ssm_proj_bwd", grid=(t // tm,),
        in_specs=[row, vec, row, row, row, row, vec, blocks],
        out_specs=[row, blocks, vec],
        out_shape=[_sds((t, D_MODEL)), _sds((4, D_MODEL, n // 4)), _sds((1, D_MODEL))],
        compiler_params=_params(("arbitrary",)),
    )(x, norm, dxo, dy, du_scan, dgate, d, w_in)


ATTN_N = Q_DIM + 2 * KV_DIM + BRANCH


def attn_proj_fwd(x, norm, w_in, cos2, sin2):
    t = x.shape[0]
    tm = min(ROW_TILE_FWD, t)

    def body(x_ref, g_ref, w_ref, cos_ref, sin_ref, q_ref, k_ref, v_ref, gate_ref):
        _, _, h = _rms(x_ref[...], g_ref[...])
        p = _mm(h, w_ref[...])
        cs = cos_ref[...]
        sn = sin_ref[...]
        q = p[:, :Q_DIM]
        k = p[:, Q_DIM:Q_DIM + KV_DIM]
        q_ref[...] = q * _tile_lanes(cs, Q_DIM // LANES) + _swap_half_heads(q) * _tile_lanes(sn, Q_DIM // LANES)
        k_ref[...] = k * cs + _swap_half_heads(k) * sn
        v_ref[...] = p[:, Q_DIM + KV_DIM:Q_DIM + 2 * KV_DIM]
        gate_ref[...] = p[:, Q_DIM + 2 * KV_DIM:]

    return pl.pallas_call(
        body, name="attn_proj_fwd", grid=(t // tm,),
        in_specs=[_rows(tm, D_MODEL), _whole((1, D_MODEL)), _whole((D_MODEL, ATTN_N)), _rows(tm, LANES), _rows(tm, LANES)],
        out_specs=[_rows(tm, Q_DIM), _rows(tm, KV_DIM), _rows(tm, KV_DIM), _rows(tm, BRANCH)],
        out_shape=[_sds((t, Q_DIM)), _sds((t, KV_DIM)), _sds((t, KV_DIM)), _sds((t, BRANCH))],
        compiler_params=_params(("parallel",)),
    )(x, norm, w_in, cos2, sin2)


GQA_LANES = GQA_GROUP * ATTN_BLOCK


def _window_masks(first_block):
    kj = lax.broadcasted_iota(jnp.int32, (ATTN_BLOCK, GQA_LANES), 0)
    qi = lax.broadcasted_iota(jnp.int32, (ATTN_BLOCK, GQA_LANES), 1) % ATTN_BLOCK
    return kj > qi, kj > jnp.where(first_block, qi, ATTN_BLOCK)


def _fold(upper, both):
    return jnp.where(upper, both[:ATTN_BLOCK], both[ATTN_BLOCK:])


def _unfold(upper, tile):
    return jnp.concatenate([jnp.where(upper, tile, 0.0), jnp.where(upper, 0.0, tile)], axis=0).astype(MXU_DTYPE)


def _stack_heads(ref, group):
    return jnp.concatenate([ref[:, h * HEAD_DIM:(h + 1) * HEAD_DIM] for h in range(group * GQA_GROUP, (group + 1) * GQA_GROUP)], axis=0)


def _unstack_heads(ref, group, stacked):
    for n in range(GQA_GROUP):
        h = group * GQA_GROUP + n
        ref[:, h * HEAD_DIM:(h + 1) * HEAD_DIM] = stacked[n * ATTN_BLOCK:(n + 1) * ATTN_BLOCK]


def _sink_row(sink_ref, group):
    return jnp.concatenate([jnp.full((1, ATTN_BLOCK), sink_ref[group * GQA_GROUP + n], F32) for n in range(GQA_GROUP)], axis=1)


def _lane_is(h):
    return lax.broadcasted_iota(jnp.int32, (1, LANES), 1) == h


def attn_fwd(q, k, v, sinks):
    t = q.shape[0]
    nb = t // ATTN_BLOCK
    scale = HEAD_DIM ** -0.5

    def body(sink_ref, q_ref, kc_ref, kp_ref, vc_ref, vp_ref, o_ref, lse_ref):
        keys = jnp.concatenate([kp_ref[...], kc_ref[...]], axis=0).astype(MXU_DTYPE)
        vals = jnp.concatenate([vp_ref[...], vc_ref[...]], axis=0).astype(MXU_DTYPE)
        upper, dead = _window_masks(pl.program_id(0) == 0)
        for g in range(N_KV_HEADS):
            kv = slice(g * HEAD_DIM, (g + 1) * HEAD_DIM)
            qs = _stack_heads(q_ref, g) * scale
            s = jnp.where(dead, NEG_INF, _fold(upper, _mm_nt(keys[:, kv], qs)))
            sink = _sink_row(sink_ref, g)
            m = jnp.maximum(jnp.max(s, axis=0, keepdims=True), sink)
            p = jnp.exp(s - m)
            den = jnp.sum(p, axis=0, keepdims=True) + jnp.exp(sink - m)
            _unstack_heads(o_ref, g, _mm_tn(_unfold(upper, p * (1.0 / den)), vals[:, kv]))
            lse = m + jnp.log(den)
            for n in range(GQA_GROUP):
                lse_ref[pl.ds(g * GQA_GROUP + n, 1), :] = lse[:, n * ATTN_BLOCK:(n + 1) * ATTN_BLOCK]

    cur = lambda n: pl.BlockSpec((ATTN_BLOCK, n), lambda i: (i, 0))
    prev = lambda n: pl.BlockSpec((ATTN_BLOCK, n), lambda i: (jnp.maximum(i - 1, 0), 0))
    return pl.pallas_call(
        body, name="attn_fwd", grid=(nb,),
        in_specs=[pl.BlockSpec(memory_space=pltpu.SMEM), cur(Q_DIM), cur(KV_DIM), prev(KV_DIM), cur(KV_DIM), prev(KV_DIM)],
        out_specs=[cur(Q_DIM), pl.BlockSpec((N_Q_HEADS, ATTN_BLOCK), lambda i: (0, i))],
        out_shape=[_sds((t, Q_DIM)), _sds((N_Q_HEADS, t))],
        compiler_params=_params(("parallel",)),
    )(sinks, q, k, k, v, v)


def attn_bwd(q, k, v, sinks, o, lse, do):
    t = q.shape[0]
    nb = t // ATTN_BLOCK
    scale = HEAD_DIM ** -0.5

    def body(sink_ref, q_ref, kc_ref, kp_ref, vc_ref, vp_ref, o_ref, lse_ref, do_ref,
             dq_ref, dk_ref, dv_ref, dsink_ref, dk_carry, dv_carry):
        i = pl.program_id(0)

        @pl.when(i == 0)
        def _():
            dsink_ref[...] = jnp.zeros_like(dsink_ref)
            dk_carry[...] = jnp.zeros_like(dk_carry)
            dv_carry[...] = jnp.zeros_like(dv_carry)

        @pl.when(i < nb)
        def _():
            keys = jnp.concatenate([kp_ref[...], kc_ref[...]], axis=0).astype(MXU_DTYPE)
            vals = jnp.concatenate([vp_ref[...], vc_ref[...]], axis=0).astype(MXU_DTYPE)
            upper, dead = _window_masks(i == 0)
            dsink = jnp.zeros((1, LANES), F32)
            dk_heads = []
            dv_heads = []
            for g in range(N_KV_HEADS):
                kv = slice(g * HEAD_DIM, (g + 1) * HEAD_DIM)
                qs = (_stack_heads(q_ref, g) * scale).astype(MXU_DTYPE)
                dos = _stack_heads(do_ref, g)
                lse = jnp.concatenate([lse_ref[pl.ds(g * GQA_GROUP + n, 1), :] for n in range(GQA_GROUP)], axis=1)
                s = jnp.where(dead, NEG_INF, _fold(upper, _mm_nt(keys[:, kv], qs)))
                p = jnp.exp(s - lse)
                delta = _mm_f32(jnp.ones((8, HEAD_DIM), F32), dos * _stack_heads(o_ref, g), ((1,), (1,)))[:1]
                dos = dos.astype(MXU_DTYPE)
                ds = _unfold(upper, p * (_fold(upper, _mm_nt(vals[:, kv], dos)) - delta))
                _unstack_heads(dq_ref, g, _mm_tn(ds, keys[:, kv]) * scale)
                dk_heads.append(_mm(ds, qs))
                dv_heads.append(_mm(_unfold(upper, p), dos))
                at_sink = jnp.exp(_sink_row(sink_ref, g) - lse) * delta
                for n in range(GQA_GROUP):
                    dsink = dsink + jnp.where(_lane_is(g * GQA_GROUP + n), -jnp.sum(at_sink[:, n * ATTN_BLOCK:(n + 1) * ATTN_BLOCK]), 0.0)
            dkk = jnp.concatenate(dk_heads, axis=1)
            dvv = jnp.concatenate(dv_heads, axis=1)
            dsink_ref[...] += dsink
            dk_ref[...] = dk_carry[...] + dkk[:ATTN_BLOCK]
            dv_ref[...] = dv_carry[...] + dvv[:ATTN_BLOCK]
            dk_carry[...] = dkk[ATTN_BLOCK:]
            dv_carry[...] = dvv[ATTN_BLOCK:]

        @pl.when(i == nb)
        def _():
            dk_ref[...] = dk_carry[...]
            dv_ref[...] = dv_carry[...]

    last = nb - 1
    cur = lambda n: pl.BlockSpec((ATTN_BLOCK, n), lambda i: (jnp.minimum(i, last), 0))
    prev = lambda n: pl.BlockSpec((ATTN_BLOCK, n), lambda i: (jnp.clip(i - 1, 0, last), 0))
    late = lambda n: pl.BlockSpec((ATTN_BLOCK, n), lambda i: (i, 0))
    dq, dk_late, dv_late, dsinks = pl.pallas_call(
        body, name="attn_bwd", grid=(nb + 1,),
        in_specs=[pl.BlockSpec(memory_space=pltpu.SMEM), cur(Q_DIM), cur(KV_DIM), prev(KV_DIM), cur(KV_DIM), prev(KV_DIM),
                  cur(Q_DIM), pl.BlockSpec((N_Q_HEADS, ATTN_BLOCK), lambda i: (0, jnp.minimum(i, last))), cur(Q_DIM)],
        out_specs=[cur(Q_DIM), late(KV_DIM), late(KV_DIM), _whole((1, LANES))],
        out_shape=[_sds((t, Q_DIM)), _sds((t + ATTN_BLOCK, KV_DIM)), _sds((t + ATTN_BLOCK, KV_DIM)), _sds((1, LANES))],
        scratch_shapes=[pltpu.VMEM((ATTN_BLOCK, KV_DIM), F32), pltpu.VMEM((ATTN_BLOCK, KV_DIM), F32)],
        compiler_params=_params(("arbitrary",)),
    )(sinks, q, k, k, v, v, o, lse, do)
    return dq, dk_late[ATTN_BLOCK:], dv_late[ATTN_BLOCK:], dsinks


def attn_out_fwd(x, o, gate, w_out):
    t = x.shape[0]
    tm = min(ROW_TILE_FWD, t)

    def body(x_ref, o_ref, gate_ref, w_ref, xo_ref):
        xo_ref[...] = x_ref[...] + _mm(o_ref[...] * _silu(gate_ref[...]), w_ref[...])

    row = _rows(tm, D_MODEL)
    return pl.pallas_call(
        body, name="attn_out_fwd", grid=(t // tm,),
        in_specs=[row, row, row, _whole((Q_DIM, D_MODEL))], out_specs=row, out_shape=_sds((t, D_MODEL)),
        compiler_params=_params(("parallel",)),
    )(x, o, gate, w_out)


def attn_out_bwd(dxo, o, gate, w_out, token=None):
    t = dxo.shape[0]
    tm = min(ROW_TILE_BWD, t)
    extra, extra_specs = _after(token)

    def body(dxo_ref, o_ref, gate_ref, wt_ref, *rest):
        do_ref, dgate_ref, dw_ref = rest[-3:]

        @pl.when(pl.program_id(0) == 0)
        def _():
            dw_ref[...] = jnp.zeros_like(dw_ref)

        dxo = dxo_ref[...]
        o = o_ref[...]
        gate = gate_ref[...]
        sgate, sgate_grad = _silu_and_grad(gate)
        da = _mm_nt(dxo, wt_ref[...])
        dw_ref[...] += _mm_tn(o * sgate, dxo)
        do_ref[...] = da * sgate
        dgate_ref[...] = da * o * sgate_grad

    row = _rows(tm, D_MODEL)
    mat = _whole((Q_DIM, D_MODEL))
    return pl.pallas_call(
        body, name="attn_out_bwd", grid=(t // tm,),
        in_specs=[row, row, row, mat] + extra_specs, out_specs=[row, row, mat],
        out_shape=[_sds((t, Q_DIM)), _sds((t, BRANCH)), _sds((Q_DIM, D_MODEL))],
        compiler_params=_params(("arbitrary",)),
    )(dxo, o, gate, w_out, *extra)


def attn_proj_bwd(x, norm, dxo, dq, dk, dv, dgate, cos2, sin2, w_in):
    t = x.shape[0]
    tm = min(ROW_TILE_BWD, t)

    def body(x_ref, g_ref, dxo_ref, dq_ref, dk_ref, dv_ref, dgate_ref, cos_ref, sin_ref, wt_ref, dx_ref, dw_ref, dg_ref):
        @pl.when(pl.program_id(0) == 0)
        def _():
            dw_ref[...] = jnp.zeros_like(dw_ref)
            dg_ref[...] = jnp.zeros_like(dg_ref)

        g = g_ref[...]
        r, xhat, h = _rms(x_ref[...], g)
        cs = cos_ref[...]
        sn = sin_ref[...]
        dqr = dq_ref[...]
        dkr = dk_ref[...]
        dq = dqr * _tile_lanes(cs, Q_DIM // LANES) + _swap_half_heads(dqr * _tile_lanes(sn, Q_DIM // LANES))
        dk = dkr * cs + _swap_half_heads(dkr * sn)
        dproj = jnp.concatenate([dq, dk, dv_ref[...], dgate_ref[...]], axis=1)
        dh = _mm_nt(dproj, wt_ref[...])
        dw_ref[...] += _mm_tn(h, dproj)
        dx, dg = _rms_bwd(dh, g, r, xhat)
        dg_ref[...] += dg
        dx_ref[...] = dxo_ref[...] + dx

    row = _rows(tm, D_MODEL)
    vec = _whole((1, D_MODEL))
    return pl.pallas_call(
        body, name="attn_proj_bwd", grid=(t // tm,),
        in_specs=[row, vec, row, _rows(tm, Q_DIM), _rows(tm, KV_DIM), _rows(tm, KV_DIM), _rows(tm, BRANCH),
                  _rows(tm, LANES), _rows(tm, LANES), _whole((D_MODEL, ATTN_N))],
        out_specs=[row, _whole((D_MODEL, ATTN_N)), vec],
        out_shape=[_sds((t, D_MODEL)), _sds((D_MODEL, ATTN_N)), _sds((1, D_MODEL))],
        compiler_params=_params(("arbitrary",)),
    )(x, norm, dxo, dq, dk, dv, dgate, cos2, sin2, w_in)


def attn_out_loss(x, o, gate, w_out, norm, target):
    t = x.shape[0]
    tm = min(ROW_TILE_FWD, t)

    def body(x_ref, o_ref, gate_ref, w_ref, g_ref, tgt_ref, loss_ref, dx_ref, dg_ref):
        @pl.when(pl.program_id(0) == 0)
        def _():
            loss_ref[...] = jnp.zeros_like(loss_ref)
            dg_ref[...] = jnp.zeros_like(dg_ref)

        out = x_ref[...] + _mm(o_ref[...] * _silu(gate_ref[...]), w_ref[...])
        g = g_ref[...]
        r, xhat, y = _rms(out, g)
        err = y - tgt_ref[...]
        loss_ref[...] += 0.5 * jnp.sum(jnp.mean(err * err, axis=-1, keepdims=True), axis=0, keepdims=True)
        dx, dg = _rms_bwd(err * (1.0 / D_MODEL), g, r, xhat)
        dg_ref[...] += dg
        dx_ref[...] = dx

    row = _rows(tm, D_MODEL)
    vec = _whole((1, D_MODEL))
    return pl.pallas_call(
        body, name="attn_out_loss", grid=(t // tm,),
        in_specs=[row, row, row, _whole((Q_DIM, D_MODEL)), vec, row], out_specs=[_whole((1, 1)), row, vec],
        out_shape=[_sds((1, 1)), _sds((t, D_MODEL)), _sds((1, D_MODEL))],
        compiler_params=_params(("arbitrary",)),
    )(x, o, gate, w_out, norm, target)


OCT_TILE = pl.BlockSpec((None, LANES, LANES), lambda b: (b, 0, 0))
N_LAGS = S5_CHUNK + 1


def _cmul(ar, ai, br, bi):
    return ar * br - ai * bi, ar * bi + ai * br


def _cmul_conj(ar, ai, br, bi):
    return ar * br + ai * bi, ar * bi - ai * br


def _mm_f32(a, b, dims):
    return lax.dot_general(a, b, (dims, ((), ())), precision=lax.Precision.HIGH, preferred_element_type=F32)


def _s5_discretise(ar, ai, ls, br, bi):
    dt = jnp.exp(ls)
    xr = ar * dt
    xi = ai * dt
    mag = jnp.exp(xr)
    first = (mag * jnp.cos(xi), mag * jnp.sin(xi))
    powers = [(jnp.ones_like(xr), jnp.zeros_like(xr)), first]
    for _ in range(2, N_LAGS):
        powers.append(_cmul(*powers[-1], *first))
    den = ar * ar + ai * ai
    nr = powers[1][0] - 1.0
    ni = powers[1][1]
    fr = (nr * ar + ni * ai) / den
    fi = (ni * ar - nr * ai) / den
    bbr, bbi = _cmul(fr, fi, br, bi)
    return dt, powers, (fr, fi), (bbr, bbi), den


def _same_group_tile():
    row = lax.broadcasted_iota(jnp.int32, (LANES, LANES), 0)
    col = lax.broadcasted_iota(jnp.int32, (LANES, LANES), 1)
    return (row // SSM_GROUP) == (col // SSM_GROUP)


def _first_copy_lanes():
    return lax.broadcasted_iota(jnp.int32, (LANES, LANES), 1) < SSM_STATE


def s5_param_fwd(tiles, token=None):
    extra, extra_specs = _after(token)

    def body(ar_ref, ai_ref, ls_ref, br_ref, bi_ref, cr_ref, ci_ref, *rest):
        kd_ref, wsr_ref, wsi_ref, wor_ref, woi_ref, pr_ref, pi_ref = rest[-7:]
        cr = cr_ref[...]
        ci = ci_ref[...]
        _, powers, _, (bbr, bbi), _ = _s5_discretise(ar_ref[...], ai_ref[...], ls_ref[...], br_ref[...], bi_ref[...])
        once = _first_copy_lanes()
        crm = jnp.where(once, cr, 0.0)
        cim = jnp.where(once, ci, 0.0)
        same = _same_group_tile()
        for lag in range(S5_CHUNK):
            er, ei = powers[lag]
            xr, xi = _cmul(er, ei, bbr, bbi)
            rows = pl.ds((S5_CHUNK - 1 - lag) * LANES, LANES)
            wsr_ref[rows, :] = xr
            wsi_ref[rows, :] = xi
        k = _mm_f32(wsr_ref[...], crm, ((1,), (1,))) - _mm_f32(wsi_ref[...], cim, ((1,), (1,)))
        for lag in range(S5_CHUNK):
            kd_ref[lag] = jnp.where(same, k[(S5_CHUNK - 1 - lag) * LANES:(S5_CHUNK - lag) * LANES], 0.0)
        for t in range(S5_CHUNK):
            er, ei = powers[t + 1]
            zr, zi = _cmul(er, ei, cr, ci)
            wor_ref[pl.ds(t * LANES, LANES), :] = zr
            woi_ref[pl.ds(t * LANES, LANES), :] = -zi
        pr_ref[...] = powers[S5_CHUNK][0]
        pi_ref[...] = powers[S5_CHUNK][1]

    return pl.pallas_call(
        body, name="s5_param_fwd", grid=(S5_OCTETS,),
        in_specs=[OCT_TILE] * 7 + [ANY] * len(extra),
        out_specs=[OCT_KD, OCT_W, OCT_W, OCT_W, OCT_W, OCT_TILE, OCT_TILE],
        out_shape=[_sds((S5_OCTETS, S5_CHUNK, LANES, LANES))] + [_sds((S5_OCTETS, S5_OCT_IN, LANES))] * 4
                  + [_sds((S5_OCTETS, LANES, LANES))] * 2,
        compiler_params=_params(("parallel",)),
    )(*tiles, *extra)


def s5_param_bwd(tiles, dkd, dws_re, dws_im, dwo_re, dwo_im, dp_re, dp_im):
    def body(ar_ref, ai_ref, ls_ref, br_ref, bi_ref, cr_ref, ci_ref, dkd_ref, dwsr_ref, dwsi_ref, dwor_ref, dwoi_ref, dpr_ref, dpi_ref,
             dar_ref, dai_ref, dls_ref, dbr_ref, dbi_ref, dcr_ref, dci_ref):
        ar = ar_ref[...]
        ai = ai_ref[...]
        br = br_ref[...]
        bi = bi_ref[...]
        cr = cr_ref[...]
        ci = ci_ref[...]
        dt, powers, (fr, fi), (bbr, bbi), den = _s5_discretise(ar, ai, ls_ref[...], br, bi)
        once = _first_copy_lanes()
        crm = jnp.where(once, cr, 0.0)
        cim = jnp.where(once, ci, 0.0)
        same = _same_group_tile()
        zero = jnp.zeros((LANES, LANES), F32)
        dpow = [[zero, zero] for _ in range(N_LAGS)]
        dbbr, dbbi = zero, zero
        by_step = [S5_CHUNK - 1 - s for s in range(S5_CHUNK)]
        x_all = [_cmul(*powers[lag], bbr, bbi) for lag in by_step]
        xr_all = jnp.concatenate([x[0] for x in x_all], axis=0)
        xi_all = jnp.concatenate([x[1] for x in x_all], axis=0)
        g_all = jnp.concatenate([jnp.where(same, dkd_ref[lag], 0.0) for lag in by_step], axis=0)
        dxr_all = dwsr_ref[...] + _mm_f32(g_all, crm, ((1,), (0,)))
        dxi_all = dwsi_ref[...] - _mm_f32(g_all, cim, ((1,), (0,)))
        dcr = jnp.where(once, _mm_f32(g_all, xr_all, ((0,), (0,))), 0.0)
        dci = -jnp.where(once, _mm_f32(g_all, xi_all, ((0,), (0,))), 0.0)
        for lag in range(S5_CHUNK):
            er, ei = powers[lag]
            rows = slice((S5_CHUNK - 1 - lag) * LANES, (S5_CHUNK - lag) * LANES)
            dxr = dxr_all[rows]
            dxi = dxi_all[rows]
            a, b = _cmul_conj(bbr, bbi, dxr, dxi)
            dpow[lag][0] = dpow[lag][0] + a
            dpow[lag][1] = dpow[lag][1] + b
            a, b = _cmul_conj(er, ei, dxr, dxi)
            dbbr = dbbr + a
            dbbi = dbbi + b
        for t in range(S5_CHUNK):
            er, ei = powers[t + 1]
            dzr = dwor_ref[pl.ds(t * LANES, LANES), :]
            dzi = -dwoi_ref[pl.ds(t * LANES, LANES), :]
            a, b = _cmul_conj(cr, ci, dzr, dzi)
            dpow[t + 1][0] = dpow[t + 1][0] + a
            dpow[t + 1][1] = dpow[t + 1][1] + b
            a, b = _cmul_conj(er, ei, dzr, dzi)
            dcr = dcr + a
            dci = dci + b
        dpow[S5_CHUNK][0] = dpow[S5_CHUNK][0] + dpr_ref[...]
        dpow[S5_CHUNK][1] = dpow[S5_CHUNK][1] + dpi_ref[...]
        dfr, dfi = _cmul_conj(br, bi, dbbr, dbbi)
        dbr, dbi = _cmul_conj(fr, fi, dbbr, dbbi)
        dnr, dni = _cmul(ar / den, ai / den, dfr, dfi)
        qr = (fr * ar + fi * ai) / den
        qi = (fi * ar - fr * ai) / den
        dlr, dli = _cmul(-qr, qi, dfr, dfi)
        dpow[1][0] = dpow[1][0] + dnr
        dpow[1][1] = dpow[1][1] + dni
        dxr, dxi = zero, zero
        for lag in range(1, N_LAGS):
            a, b = _cmul_conj(powers[lag][0], powers[lag][1], dpow[lag][0], dpow[lag][1])
            dxr = dxr + lag * a
            dxi = dxi + lag * b
        dar_ref[...] = dlr + dt * dxr
        dai_ref[...] = dli + dt * dxi
        dls_ref[...] = dt * (ar * dxr + ai * dxi)
        dbr_ref[...] = dbr
        dbi_ref[...] = dbi
        dcr_ref[...] = dcr
        dci_ref[...] = dci

    return pl.pallas_call(
        body, name="s5_param_bwd", grid=(S5_OCTETS,),
        in_specs=[OCT_TILE] * 7 + [OCT_KD, OCT_W, OCT_W, OCT_W, OCT_W, OCT_TILE, OCT_TILE], out_specs=[OCT_TILE] * 7,
        out_shape=[_sds((S5_OCTETS, LANES, LANES))] * 7,
        compiler_params=_params(("parallel",)),
    )(*tiles, dkd, dws_re, dws_im, dwo_re, dwo_im, dp_re, dp_im)


def _doubled(v):
    return jnp.concatenate([v, v], axis=-1)


def _s5_param_tiles(a_re, a_im, log_step, b_re, b_im, c_re, c_im):
    def per_group(a):
        return _doubled(jnp.broadcast_to(a.reshape(S5_OCTETS, S5_OCT, 1, SSM_STATE),
                                         (S5_OCTETS, S5_OCT, SSM_GROUP, SSM_STATE)).reshape(S5_OCTETS, LANES, SSM_STATE))

    ls = jnp.broadcast_to(log_step.reshape(S5_OCTETS, S5_OCT, 1, 1), (S5_OCTETS, S5_OCT, SSM_GROUP, LANES)).reshape(S5_OCTETS, LANES, LANES)
    bt = lambda b: _doubled(b.transpose(0, 2, 1).reshape(S5_OCTETS, LANES, SSM_STATE))
    ct = lambda c: _doubled(c.reshape(S5_OCTETS, LANES, SSM_STATE))
    return [per_group(a_re), per_group(a_im), ls, bt(b_re), bt(b_im), ct(c_re), ct(c_im)]


def _s5_param_grads(dtiles):
    dar, dai, dls, dbr, dbi, dcr, dci = dtiles
    halves = lambda d: d[..., :SSM_STATE] + d[..., SSM_STATE:]
    per_group = lambda d: halves(d).reshape(SSM_GROUPS, SSM_GROUP, SSM_STATE).sum(axis=1)
    per_row = lambda d: halves(d).reshape(SSM_GROUPS, SSM_GROUP, SSM_STATE)
    return (per_group(dar), per_group(dai), dls.reshape(SSM_GROUPS, SSM_GROUP * LANES).sum(axis=1),
            per_row(dbr).transpose(0, 2, 1), per_row(dbi).transpose(0, 2, 1), per_row(dcr), per_row(dci))


def _group_power_rows(tile):
    return tile[:, ::SSM_GROUP, :SSM_STATE].reshape(1, S5_STATES)


def _group_power_tiles(row):
    t = jnp.pad(row.reshape(S5_OCTETS, S5_OCT, 1, SSM_STATE), ((0, 0), (0, 0), (0, SSM_GROUP - 1), (0, LANES - SSM_STATE)))
    return t.reshape(S5_OCTETS, LANES, LANES)


def _rope_tables(t):
    pos = jnp.arange(t, dtype=F32)
    inv_freq = ROPE_THETA ** (-jnp.arange(0, HEAD_DIM, 2, dtype=F32) / HEAD_DIM)
    ang = pos[:, None] * inv_freq[None, :]
    cos = jnp.cos(ang)
    sin = jnp.sin(ang)
    cos64 = jnp.concatenate([cos, cos], axis=1)
    sin64 = jnp.concatenate([-sin, sin], axis=1)
    return jnp.concatenate([cos64, cos64], axis=1), jnp.concatenate([sin64, sin64], axis=1)


def _row(v):
    return v.reshape(1, -1)


def _s5_matrices(w, token=None):
    tiles = _s5_param_tiles(w["a_re"], w["a_im"], w["log_step"], w["b_re"], w["b_im"], w["c_re"], w["c_im"])
    kd, ws_re, ws_im, wo_re, wo_im, p_re, p_im = s5_param_fwd(tiles, token)
    return tiles, dict(kd=kd, ws_re=ws_re, ws_im=ws_im, wo_re=wo_re, wo_im=wo_im, a_re=_group_power_rows(p_re), a_im=_group_power_rows(p_im))


def _ssm_forward(x, w):
    tiles, mats = w["s5"] if "s5" in w else _s5_matrices(w)
    u, gate = ssm_proj_fwd(x, _row(w["norm"]), w["w_in"])
    s_re, s_im = s5_chunk_states(u, mats["ws_re"], mats["ws_im"])
    h_re, h_im = s5_scan_fwd(s_re, s_im, mats["a_re"], mats["a_im"])
    y_scan = s5_outputs(u, h_re, h_im, mats["kd"], mats["wo_re"], mats["wo_im"])
    y, g2, x_new = ssm_mix_fwd(x, u, gate, y_scan, _row(w["d"]), w["w_glu"], _row(w["b_glu"]), w["w_out"])
    saved = dict(x=x, u=u, gate=gate, y=y, g2=g2, h_re=h_re, h_im=h_im, mats=mats, tiles=tiles)
    return x_new, saved


def _ssm_backward(dxo, w, s, token=None):
    dy, dgate, dw_out, dw_glu, db_glu, dd = ssm_mix_bwd(dxo, s["u"], s["gate"], s["y"], s["g2"], w["w_glu"], w["w_out"], token)
    mats = s["mats"]
    dh_re, dh_im = s5_state_grads(dy, mats["wo_re"], mats["wo_im"])
    ds_re, ds_im, da_re, da_im = s5_scan_bwd(dh_re, dh_im, s["h_re"], s["h_im"], mats["a_re"], mats["a_im"])
    du_scan = s5_input_grads(dy, ds_re, ds_im, mats["kd"], mats["ws_re"], mats["ws_im"])
    dkd, dws_re, dws_im, dwo_re, dwo_im = s5_weight_grads(s["u"], dy, s["h_re"], s["h_im"], ds_re, ds_im)
    dparams = _s5_param_grads(s5_param_bwd(s["tiles"], dkd, dws_re, dws_im, dwo_re, dwo_im,
                                           _group_power_tiles(da_re), _group_power_tiles(da_im)))
    dx, dw_in, dnorm = ssm_proj_bwd(s["x"], _row(w["norm"]), dxo, dy, du_scan, dgate, _row(w["d"]), w["w_in"])
    grads = dict(norm=dnorm, w_in=dw_in, d=dd, w_glu=dw_glu, b_glu=db_glu, w_out=dw_out)
    for name, val in zip(("a_re", "a_im", "log_step", "b_re", "b_im", "c_re", "c_im"), dparams):
        grads[name] = val
    return dx, grads


def _attn_forward(x, w, cos2, sin2, loss_head=None):
    q, k, v, gate = attn_proj_fwd(x, _row(w["norm"]), w["w_in"], cos2, sin2)
    o, lse = attn_fwd(q, k, v, w["sinks"])
    if loss_head is None:
        result = attn_out_fwd(x, o, gate, w["w_out"])
    else:
        result = attn_out_loss(x, o, gate, w["w_out"], _row(loss_head[0]), loss_head[1])
    return result, dict(x=x, q=q, k=k, v=v, gate=gate, o=o, lse=lse)


def _attn_backward(dxo, w, s, cos2, sin2, token=None):
    do, dgate, dw_out = attn_out_bwd(dxo, s["o"], s["gate"], w["w_out"], token)
    dq, dk, dv, dsinks = attn_bwd(s["q"], s["k"], s["v"], w["sinks"], s["o"], s["lse"], do)
    dx, dw_in, dnorm = attn_proj_bwd(s["x"], _row(w["norm"]), dxo, dq, dk, dv, dgate, cos2, sin2, w["w_in"])
    return dx, dict(norm=dnorm, w_in=dw_in, sinks=dsinks[0, :N_Q_HEADS], w_out=dw_out)


class _NoExchanges:
    def __init__(self, layers):
        self.layers = layers

    def layer(self, i, x):
        return self.layers[i]

    def layer_done(self, i, grads, dx):
        return None


def _sequence_step(x, target, final_norm, hooks, depth=4):
    cos2, sin2 = _rope_tables(x.shape[0])
    saved, layers = [], []
    for i in range(depth):
        w = hooks.layer(i, x)
        layers.append(w)
        if i % 2 == 0:
            x, s = _ssm_forward(x, w)
        else:
            x, s = _attn_forward(x, w, cos2, sin2, (final_norm, target) if i == depth - 1 else None)
        saved.append(s)
    loss, dx, dfinal = x
    grads = {"final_norm": dfinal}
    token = None
    for i in reversed(range(depth)):
        if i % 2 == 0:
            dx, g = _ssm_backward(dx, layers[i], saved[i], token)
        else:
            dx, g = _attn_backward(dx, layers[i], saved[i], cos2, sin2, token)
        g = {"l%d_%s" % (i, name): val for name, val in g.items()}
        grads.update(g)
        token = hooks.layer_done(i, g, dx)
    return loss[0, 0], dx, grads


ANY = pl.BlockSpec(memory_space=pl.ANY)


def _place():
    return lax.axis_index("x"), lax.axis_index("y"), lax.axis_index("c")


def _other_chips(x, y):
    return [(1 - x, y), (x, 1 - y), (1 - x, 1 - y)]


class _StagedCopies:
    def __init__(self, bufs, load_sems, store_sems):
        self.bufs, self.load_sems, self.store_sems = bufs, load_sems, store_sems
        self.loads, self.stores = [], []

    def load(self, i, src):
        cp = pltpu.make_async_copy(src, self.bufs[i], self.load_sems.at[i])
        cp.start()
        self.loads.append(cp)

    def store(self, i, dst):
        self.loads[i].wait()
        cp = pltpu.make_async_copy(self.bufs[i], dst, self.store_sems.at[i])
        cp.start()
        self.stores.append(cp)

    def finish(self):
        for cp in self.stores:
            cp.wait()


def _staging(blocks):
    n = len(blocks)
    return [pltpu.VMEM(b.shape, b.dtype) for b in blocks] + [pltpu.SemaphoreType.DMA((n,)), pltpu.SemaphoreType.DMA((n,))]


def exchange_halves_with_sibling(grads):
    n = len(grads)

    def body(*refs):
        ins, outs = refs[:n], refs[n:2 * n]
        send_sems, recv_sems = refs[2 * n:]
        x, y, c = _place()
        copies = []
        for i in range(n):
            half = ins[i].shape[1] // 2
            src = ins[i].at[:, pl.ds((1 - c) * half, half), :]
            cp = pltpu.make_async_remote_copy(src_ref=src, dst_ref=outs[i], send_sem=send_sems.at[i], recv_sem=recv_sems.at[i],
                                              device_id=(x, y, 1 - c), device_id_type=MESH)
            cp.start()
            copies.append(cp)
        for cp in copies:
            cp.wait()

    return pl.pallas_call(
        body, name="exchange_halves_with_sibling",
        in_specs=[ANY] * n, out_specs=[ANY] * n,
        out_shape=[_sds((g.shape[0], g.shape[1] // 2, g.shape[2])) for g in grads],
        scratch_shapes=[pltpu.SemaphoreType.DMA((n,)), pltpu.SemaphoreType.DMA((n,))],
    )(*grads)


def swap_halves_with_sibling(pieces):
    n = len(pieces)

    def body(*refs):
        ins, outs = refs[:n], refs[n:2 * n]
        send_sems, recv_sems = refs[2 * n:2 * n + 2]
        own = _StagedCopies(refs[2 * n + 2:3 * n + 2], *refs[3 * n + 2:])
        x, y, c = _place()
        for i in range(n):
            own.load(i, ins[i])
        swaps = []
        for i in range(n):
            cp = pltpu.make_async_remote_copy(src_ref=ins[i], dst_ref=outs[i].at[c], send_sem=send_sems.at[i], recv_sem=recv_sems.at[i],
                                              device_id=(x, y, 1 - c), device_id_type=MESH)
            cp.start()
            swaps.append(cp)
        for i in range(n):
            own.store(i, outs[i].at[c])
        for i in range(n):
            pltpu.make_async_remote_copy(src_ref=ins[i], dst_ref=outs[i].at[1 - c], send_sem=send_sems.at[i], recv_sem=recv_sems.at[i],
                                         device_id=(x, y, 1 - c), device_id_type=MESH).wait_recv()
        for cp in swaps:
            cp.wait_send()
        own.finish()

    return pl.pallas_call(
        body, name="swap_halves_with_sibling",
        in_specs=[ANY] * n, out_specs=[ANY] * n,
        out_shape=[_sds((2,) + p.shape) for p in pieces],
        scratch_shapes=[pltpu.SemaphoreType.DMA((n,)), pltpu.SemaphoreType.DMA((n,))] + _staging(pieces),
        compiler_params=_params(),
    )(*pieces)


IN_HBM = pl.BlockSpec(memory_space=pltpu.HBM)
SEMAPHORES = pl.BlockSpec(memory_space=pltpu.SEMAPHORE)
DATAFLOW = pltpu.SideEffectType.DATAFLOW_SIDE_EFFECTING


def _hbm(a):
    return pltpu.with_memory_space_constraint(a, pltpu.HBM)


def place_own_blocks(shards):
    n = len(shards)

    def body(*refs):
        ins, outs = refs[:n], refs[n:2 * n]
        own = _StagedCopies(refs[2 * n:3 * n], *refs[3 * n:])
        x, y, _ = _place()
        for i in range(n):
            own.load(i, ins[i])
        for i in range(n):
            own.store(i, outs[i].at[2 * x + y])
        own.finish()

    return pl.pallas_call(
        body, name="place_own_blocks", in_specs=[ANY] * n, out_specs=[ANY] * n,
        out_shape=[_sds((4,) + s.shape, s.dtype) for s in shards],
        scratch_shapes=_staging(shards), compiler_params=_params(),
    )(*shards)


def _block_to_send(ref, chip, per_target):
    if not per_target:
        return ref
    return ref.at[chip] if ref.shape[0] == 4 else ref.at[0]


def start_sends_to_chips(name, sources, landings, per_target, after):
    n = len(sources)
    n_sems = 2 * 3 * n

    def body(*refs):
        srcs = refs[:n]
        sems = refs[2 * n + 1:2 * n + 1 + n_sems]
        lands = refs[2 * n + 1 + n_sems:3 * n + 1 + n_sems]
        token = refs[3 * n + 1 + n_sems]
        x, y, c = _place()
        me = 2 * x + y
        for i in range(n):
            for k, (tx, ty) in enumerate(_other_chips(x, y)):
                src = _block_to_send(srcs[i], 2 * tx + ty, per_target)
                pltpu.make_async_remote_copy(src_ref=src, dst_ref=lands[i].at[me], send_sem=sems[2 * (3 * i + k)], recv_sem=sems[2 * (3 * i + k) + 1],
                                             device_id=(tx, ty, c), device_id_type=MESH).start()
        token[...] = jnp.zeros_like(token)

    outs = pl.pallas_call(
        body, name=name,
        in_specs=[IN_HBM] * (2 * n) + [ANY],
        out_specs=[SEMAPHORES] * n_sems + [IN_HBM] * n + [pl.BlockSpec(memory_space=pltpu.VMEM)],
        out_shape=[pltpu.SemaphoreType.DMA(())] * n_sems + [pltpu.HBM(l.shape, l.dtype) for l in landings] + [_sds(TOKEN_SHAPE)],
        input_output_aliases={n + i: n_sems + i for i in range(n)},
        compiler_params=pltpu.CompilerParams(has_side_effects=DATAFLOW),
    )(*[_hbm(s) for s in sources], *[_hbm(l) for l in landings], after)
    return list(outs[:n_sems]), list(outs[n_sems:n_sems + n]), outs[n_sems + n]


def wait_sends_to_chips(name, sources, landings, sems, per_target, after):
    n = len(sources)
    n_sems = len(sems)

    def body(*refs):
        srcs = refs[:n]
        sem_refs = refs[2 * n:2 * n + n_sems]
        lands = refs[2 * n + n_sems + 1:]
        x, y, c = _place()
        me = 2 * x + y
        for i in range(n):
            for k, (tx, ty) in enumerate(_other_chips(x, y)):
                src = _block_to_send(srcs[i], me, per_target)
                cp = pltpu.make_async_remote_copy(src_ref=src, dst_ref=lands[i].at[2 * tx + ty], send_sem=sem_refs[2 * (3 * i + k)],
                                                  recv_sem=sem_refs[2 * (3 * i + k) + 1], device_id=(tx, ty, c), device_id_type=MESH)
                cp.wait_send()
                cp.wait_recv()

    return pl.pallas_call(
        body, name=name,
        in_specs=[IN_HBM] * (2 * n) + [SEMAPHORES] * n_sems + [ANY],
        out_specs=[IN_HBM] * n,
        out_shape=[pltpu.HBM(l.shape, l.dtype) for l in landings],
        input_output_aliases={n + i: i for i in range(n)},
        compiler_params=pltpu.CompilerParams(has_side_effects=DATAFLOW),
    )(*[_hbm(s) for s in sources], *landings, *sems, after)


def _row_tile(rows, cols):
    tm = rows
    while tm * cols * 4 > (2 << 20) and tm % 16 == 0:
        tm //= 2
    return tm


def add_pairs(a_list, b_list, out_dtypes, copies=1):
    n = len(a_list)
    nb = a_list[0].shape[0]

    def body(*refs):
        for i in range(n):
            total = (refs[i][...] + refs[n + i][...]).astype(out_dtypes[i])
            for o_ref in refs[2 * n + i * copies:2 * n + (i + 1) * copies]:
                o_ref[...] = total

    specs = [pl.BlockSpec((None,) + a.shape[1:], lambda j: (j, 0, 0)) for a in a_list]
    outs = pl.pallas_call(
        body, name="add_pairs", grid=(nb,), in_specs=specs * 2, out_specs=[s for s in specs for _ in range(copies)],
        out_shape=[_sds(a.shape, dt) for a, dt in zip(a_list, out_dtypes) for _ in range(copies)],
        compiler_params=_params(("parallel",)),
    )(*a_list, *b_list)
    return [tuple(outs[i * copies:(i + 1) * copies]) for i in range(n)]


def sum_fours(arrays, token=None):
    n = len(arrays)
    extra, extra_specs = _after(token)

    def body(*refs):
        outs = refs[-n:]
        for a_ref, o_ref in zip(refs[:n], outs):
            o_ref[...] = ((a_ref[0].astype(F32) + a_ref[1].astype(F32)) + a_ref[2].astype(F32)) + a_ref[3].astype(F32)

    return pl.pallas_call(
        body, name="sum_fours", grid=(2,),
        in_specs=[pl.BlockSpec((4, a.shape[1] // 2, a.shape[2]), lambda i: (0, i, 0)) for a in arrays] + extra_specs,
        out_specs=[pl.BlockSpec((a.shape[1] // 2, a.shape[2]), lambda i: (i, 0)) for a in arrays],
        out_shape=[_sds(a.shape[1:]) for a in arrays], compiler_params=_params(("parallel",)),
    )(*arrays, *extra)


def _adamw_update(w_ref, g_ref, m_ref, v_ref, d_ref, nm_ref, nv_ref):
    g = g_ref[...]
    nm = ADAM_B1 * m_ref[...] + (1.0 - ADAM_B1) * g
    nv = ADAM_B2 * v_ref[...] + (1.0 - ADAM_B2) * (g * g)
    d_ref[...] = -ADAM_LR * ((nm / (1.0 - ADAM_B1 ** ADAM_STEP)) / (jnp.sqrt(nv / (1.0 - ADAM_B2 ** ADAM_STEP)) + ADAM_EPS) + ADAM_WD * w_ref[...])
    nm_ref[...] = nm
    nv_ref[...] = nv


def adamw(w, g, m, v):
    rows, cols = w.shape
    tm = _row_tile(rows, cols)

    def body(*refs):
        _adamw_update(*refs)

    spec = pl.BlockSpec((tm, cols), lambda i: (i, 0))
    return pl.pallas_call(
        body, name="adamw", grid=(rows // tm,), in_specs=[spec] * 4, out_specs=[spec] * 3,
        out_shape=[_sds(w.shape)] * 3, compiler_params=_params(("parallel",)),
    )(w, g, m, v)


def adamw_small(ws, gs, ms, vs, slabs=None):
    n = len(ws)

    def body(*refs):
        for i in range(n):
            _adamw_update(refs[i], refs[n + i], refs[2 * n + i], refs[3 * n + i], refs[4 * n + i], refs[5 * n + i], refs[6 * n + i])

    if slabs is None:
        grid = ()
        specs = [pl.BlockSpec(memory_space=pltpu.VMEM)] * n
    else:
        grid = (slabs,)
        specs = [pl.BlockSpec((w.shape[0] // slabs,) + w.shape[1:], lambda i: (i, 0, 0)) for w in ws]
    outs = pl.pallas_call(
        body, name="adamw_small", grid=grid, in_specs=specs * 4, out_specs=specs * 3,
        out_shape=[_sds(w.shape) for w in ws] * 3, compiler_params=_params(("parallel",) if slabs else None),
    )(*ws, *gs, *ms, *vs)
    return outs[:n], outs[n:2 * n], outs[2 * n:]


PACK_TILE = 8 * LANES
PACK_PIECES = 8
PACK_ALIGN = PACK_PIECES * 16


def _pack_small(values, scalar=None):
    parts = []
    for name in SMALL_NAMES:
        flat = values[name].reshape(-1)
        pad = (-flat.shape[0]) % PACK_TILE
        if pad:
            flat = jnp.concatenate([flat, jnp.zeros((pad,), F32)])
        parts.append(flat.reshape(-1, LANES))
    rows = sum(p.shape[0] for p in parts) + 8
    parts.append(jnp.zeros(((-rows) % PACK_ALIGN, LANES), F32))
    last = jnp.zeros((8, LANES), F32)
    parts.append(last if scalar is None else jnp.broadcast_to(scalar.astype(F32), (8, LANES)))
    return jnp.concatenate(parts, axis=0)


def _unpack_small(pack, like):
    out = {}
    row = 0
    for name in SMALL_NAMES:
        size = math.prod(like[name].shape)
        rows = -(-size // PACK_TILE) * 8
        out[name] = pack[row:row + rows].reshape(-1)[:size].reshape(like[name].shape)
        row += rows
    return out


def _is_column_sharded(name):
    return name.endswith("w_in")


def _to_blocks(name, full):
    if full.ndim == 3:
        return full
    if _is_column_sharded(name):
        rows, cols = full.shape
        return full.reshape(rows, 4, cols // 4).transpose(1, 0, 2)
    return full.reshape(4, full.shape[0] // 4, full.shape[1])


def _from_blocks(name, stacked):
    if _is_column_sharded(name):
        if stacked.shape[2] % LANES == 0:
            return stacked
        return stacked.transpose(1, 0, 2).reshape(stacked.shape[1], 4 * stacked.shape[2])
    return stacked.reshape(4 * stacked.shape[1], stacked.shape[2])


def _layer_big_names(i):
    return [n for n in BIG_NAMES if n.startswith("l%d_" % i)]


class _OverlappedExchanges:
    def __init__(self, weights):
        self.weights = weights
        self.c = lax.axis_index("c")
        self.first = _layer_big_names(0)
        self.later = [n for n in BIG_NAMES if n not in self.first]
        shards = [weights[n].astype(MXU_DTYPE) for n in self.first + self.later]
        placed = place_own_blocks(shards)
        k = len(self.first)
        sems, stacks, token = start_sends_to_chips("gather_first_start", shards[:k], placed[:k], False, shards[0])
        self.gather_first = (shards[:k], sems, stacks)
        sems, stacks, token = start_sends_to_chips("gather_later_start", shards[k:], placed[k:], False, token)
        self.gather_later = (shards[k:], sems, stacks)
        self.s5 = {}
        for i in (0, 2):
            self.s5[i] = _s5_matrices({n: weights["l%d_%s" % (i, n)] for n in SSM_NAMES if "l%d_%s" % (i, n) in SMALL_NAMES}, token)
            token = self.s5[i][1]["kd"]
        self.full = {}
        self.in_flight = {}
        self.contributions = {}

    def layer(self, i, x):
        if i == 0:
            shards, sems, stacks = self.gather_first
            stacks = wait_sends_to_chips("gather_first_wait", shards, stacks, sems, False, self.s5[2][1]["kd"])
            self.full.update({n: _from_blocks(n, g) for n, g in zip(self.first, stacks)})
        if i == 1:
            shards, sems, stacks = self.gather_later
            stacks = wait_sends_to_chips("gather_later_wait", shards, stacks, sems, False, x)
            self.full.update({n: _from_blocks(n, g) for n, g in zip(self.later, stacks)})
        names = SSM_NAMES if i % 2 == 0 else ATTN_NAMES
        w = {n: self.full.get("l%d_%s" % (i, n), self.weights.get("l%d_%s" % (i, n))) for n in names}
        if i in self.s5:
            w["s5"] = self.s5[i]
        return w

    def chip_sums(self, names, grads, extra_blocks=(), copies=1):
        blocks = [_to_blocks(n, grads[n]) for n in names] + list(extra_blocks)
        from_sibling = exchange_halves_with_sibling(blocks)
        mine = [lax.dynamic_slice_in_dim(b, self.c * (b.shape[1] // 2), b.shape[1] // 2, axis=1) for b in blocks]
        k = len(names)
        sums = add_pairs(mine[:k], from_sibling[:k], [WIRE_DTYPE] * k, copies)
        if extra_blocks:
            sums += add_pairs(mine[k:], from_sibling[k:], [F32] * len(extra_blocks), copies)
        return sums

    def layer_done(self, i, grads, dx):
        if i + 1 in self.in_flight:
            names, sums, sems, landings = self.in_flight.pop(i + 1)
            done = wait_sends_to_chips("scatter_wait_l%d" % (i + 1), sums, landings, sems, True, dx)
            self.contributions.update(zip(names, done))
        if i == 0:
            return None
        names = _layer_big_names(i)
        pairs = self.chip_sums(names, grads, copies=2)
        sums = [p[0] for p in pairs]
        sems, landings, token = start_sends_to_chips("scatter_start_l%d" % i, sums, [p[1] for p in pairs], True, sums[0])
        self.in_flight[i] = (names, sums, sems, landings)
        return token


def _train_step(x, loss_target, weights, moments_m, moments_v):
    hooks = _OverlappedExchanges(weights)
    loss, dx, grads = _sequence_step(x[0], loss_target[0], weights["final_norm"], hooks)
    small_pack = _pack_small({n: grads[n] for n in SMALL_NAMES}, scalar=loss)
    last = _layer_big_names(0)
    pairs = hooks.chip_sums(last, grads, extra_blocks=[small_pack[None]], copies=2)
    sums = [p[0] for p in pairs]
    landings = [p[1] for p in pairs[:-1]] + [jnp.broadcast_to(sums[-1], (4,) + sums[-1].shape[1:])]
    sems, landings, token = start_sends_to_chips("scatter_start_l0", sums, landings, True, sums[0])
    out_grad, out_delta, out_m, out_v = {}, {}, {}, {}

    def finish(names, arrays, token=None):
        shared = swap_halves_with_sibling(sum_fours(arrays, token))
        for n, s in zip(names, shared):
            if n == "small":
                return s.reshape(-1, LANES)
            out_grad[n] = s.reshape(2 * s.shape[1], s.shape[2])
            out_delta[n], out_m[n], out_v[n] = adamw(weights[n], out_grad[n], moments_m[n], moments_v[n])

    others = [n for n in BIG_NAMES if n not in last]
    finish(others, [hooks.contributions[n] for n in others], token)
    arrived = wait_sends_to_chips("scatter_wait_l0", sums, landings, sems, True, out_v[others[-1]])
    small_grad_pack = finish(last + ["small"], arrived)
    loss = small_grad_pack[-8, 0]
    out_grad.update(_unpack_small(small_grad_pack, {n: weights[n] for n in SMALL_NAMES}))
    cubes = [n for n in SMALL_NAMES if weights[n].ndim == 3]
    for names, slabs in ((cubes, 8), ([n for n in SMALL_NAMES if n not in cubes], None)):
        deltas, new_ms, new_vs = adamw_small(*[[group[n] for n in names] for group in (weights, out_grad, moments_m, moments_v)], slabs=slabs)
        out_delta.update(zip(names, deltas))
        out_m.update(zip(names, new_ms))
        out_v.update(zip(names, new_vs))
    outs = [loss, dx[None]]
    for group in (out_grad, out_delta, out_m, out_v):
        outs.extend(group[n] for n in WEIGHT_NAMES)
    return tuple(outs)


def kernel(x, l0_norm, l0_w_in, l0_a_re, l0_a_im, l0_log_step, l0_b_re, l0_b_im, l0_c_re, l0_c_im, l0_d, l0_w_glu, l0_b_glu, l0_w_out, l1_norm, l1_w_in, l1_sinks, l1_w_out, l2_norm, l2_w_in, l2_a_re, l2_a_im, l2_log_step, l2_b_re, l2_b_im, l2_c_re, l2_c_im, l2_d, l2_w_glu, l2_b_glu, l2_w_out, l3_norm, l3_w_in, l3_sinks, l3_w_out, final_norm, loss_target, m_l0_norm, m_l0_w_in, m_l0_a_re, m_l0_a_im, m_l0_log_step, m_l0_b_re, m_l0_b_im, m_l0_c_re, m_l0_c_im, m_l0_d, m_l0_w_glu, m_l0_b_glu, m_l0_w_out, m_l1_norm, m_l1_w_in, m_l1_sinks, m_l1_w_out, m_l2_norm, m_l2_w_in, m_l2_a_re, m_l2_a_im, m_l2_log_step, m_l2_b_re, m_l2_b_im, m_l2_c_re, m_l2_c_im, m_l2_d, m_l2_w_glu, m_l2_b_glu, m_l2_w_out, m_l3_norm, m_l3_w_in, m_l3_sinks, m_l3_w_out, m_final_norm, v_l0_norm, v_l0_w_in, v_l0_a_re, v_l0_a_im, v_l0_log_step, v_l0_b_re, v_l0_b_im, v_l0_c_re, v_l0_c_im, v_l0_d, v_l0_w_glu, v_l0_b_glu, v_l0_w_out, v_l1_norm, v_l1_w_in, v_l1_sinks, v_l1_w_out, v_l2_norm, v_l2_w_in, v_l2_a_re, v_l2_a_im, v_l2_log_step, v_l2_b_re, v_l2_b_im, v_l2_c_re, v_l2_c_im, v_l2_d, v_l2_w_glu, v_l2_b_glu, v_l2_w_out, v_l3_norm, v_l3_w_in, v_l3_sinks, v_l3_w_out, v_final_norm):
    args = locals()
    weights = {n: args[n] for n in WEIGHT_NAMES}
    moments_m = {n: args["m_" + n] for n in WEIGHT_NAMES}
    moments_v = {n: args["v_" + n] for n in WEIGHT_NAMES}
    return _train_step(x, loss_target, weights, moments_m, moments_v)
```

```python
import functools
import math

import jax
import jax.numpy as jnp
from jax import lax
from jax.experimental import pallas as pl
from jax.experimental.pallas import tpu as pltpu

F32 = jnp.float32
MXU_DTYPE = jnp.bfloat16
WIRE_DTYPE = jnp.bfloat16
MESH = pl.DeviceIdType.MESH

D_MODEL = 1024
BRANCH = 1024
NORM_EPS = 1e-5
SSM_GROUPS = 64
SSM_GROUP = 16
SSM_STATE = 64
S5_CHUNK = 16
LANES = 128
S5_OCT = LANES // SSM_GROUP
S5_OCTETS = SSM_GROUPS // S5_OCT
S5_OCT_IN = S5_CHUNK * LANES
S5_OCT_STATE = S5_OCT * SSM_STATE
S5_STATES = SSM_GROUPS * SSM_STATE
HEAD_DIM = 64
N_Q_HEADS = 16
N_KV_HEADS = 2
GQA_GROUP = N_Q_HEADS // N_KV_HEADS
ATTN_BLOCK = 128
Q_DIM = N_Q_HEADS * HEAD_DIM
KV_DIM = N_KV_HEADS * HEAD_DIM
ROPE_THETA = 10000.0
NEG_INF = -1e30
ADAM_LR = 0.001
ADAM_B1 = 0.9
ADAM_B2 = 0.999
ADAM_EPS = 1e-08
ADAM_WD = 0.01
ADAM_STEP = 10

VMEM_LIMIT_V7X = 56 * 1024 * 1024
ROW_TILE_FWD = 512
ROW_TILE_BWD = 512

SSM_NAMES = ("norm", "w_in", "a_re", "a_im", "log_step", "b_re", "b_im", "c_re", "c_im", "d", "w_glu", "b_glu", "w_out")
ATTN_NAMES = ("norm", "w_in", "sinks", "w_out")


def _weight_names():
    names = []
    for i in range(4):
        for n in (SSM_NAMES if i % 2 == 0 else ATTN_NAMES):
            names.append("l%d_%s" % (i, n))
    names.append("final_norm")
    return names


WEIGHT_NAMES = _weight_names()
BIG_NAMES = [n for n in WEIGHT_NAMES if n.endswith(("w_in", "w_glu", "w_out"))]
SMALL_NAMES = [n for n in WEIGHT_NAMES if n not in BIG_NAMES]


def _params(semantics=None):
    return pltpu.CompilerParams(dimension_semantics=semantics, vmem_limit_bytes=VMEM_LIMIT_V7X)


def _rows(tm, n):
    return pl.BlockSpec((tm, n), lambda i: (i, 0))


def _whole(shape):
    return pl.BlockSpec(shape, lambda i: (0,) * len(shape), pipeline_mode=pl.Buffered(1))


def _sds(shape, dtype=F32):
    return jax.ShapeDtypeStruct(shape, dtype)


def _mm(a, b):
    return jnp.dot(a.astype(MXU_DTYPE), b.astype(MXU_DTYPE), preferred_element_type=F32)


def _mm_tn(a, b):
    return lax.dot_general(a.astype(MXU_DTYPE), b.astype(MXU_DTYPE), (((0,), (0,)), ((), ())), preferred_element_type=F32)


def _mm_nt(a, b):
    return lax.dot_general(a.astype(MXU_DTYPE), b.astype(MXU_DTYPE), (((1,), (1,)), ((), ())), preferred_element_type=F32)


def _sigmoid(x):
    return 0.5 + 0.5 * jnp.tanh(0.5 * x)


def _silu(x):
    return x * _sigmoid(x)


def _silu_and_grad(x):
    s = _sigmoid(x)
    return x * s, s * (1.0 + x * (1.0 - s))


GELU_C0 = math.sqrt(2.0 / math.pi)
GELU_C1 = 0.044715


def _gelu(x):
    return 0.5 * x * (1.0 + jnp.tanh(GELU_C0 * (x + GELU_C1 * x * x * x)))


def _gelu_and_grad(x):
    x2 = x * x
    th = jnp.tanh(GELU_C0 * x * (1.0 + GELU_C1 * x2))
    half = 0.5 + 0.5 * th
    return x * half, half + 0.5 * x * (1.0 - th * th) * (GELU_C0 + 3.0 * GELU_C0 * GELU_C1 * x2)


def _rms(x, g):
    r = lax.rsqrt(jnp.mean(x * x, axis=-1, keepdims=True) + NORM_EPS)
    xhat = x * r
    return r, xhat, xhat * g


def _rms_bwd(dh, g, r, xhat):
    dxhat = dh * g
    dx = r * (dxhat - xhat * jnp.mean(dxhat * xhat, axis=-1, keepdims=True))
    return dx, jnp.sum(dh * xhat, axis=0, keepdims=True)


def _swap_half_heads(x):
    n = x.shape[-1]
    lane = lax.broadcasted_iota(jnp.int32, x.shape, x.ndim - 1)
    first = (lane % HEAD_DIM) < (HEAD_DIM // 2)
    return jnp.where(first, pltpu.roll(x, n - HEAD_DIM // 2, x.ndim - 1), pltpu.roll(x, HEAD_DIM // 2, x.ndim - 1))


def _tile_lanes(t, reps):
    return jnp.concatenate([t] * reps, axis=1)


TOKEN_SHAPE = (8, LANES)


def _after(token):
    return ([], []) if token is None else ([token], [_whole(TOKEN_SHAPE)])


def ssm_proj_fwd(x, norm, w_in):
    t = x.shape[0]
    tm = min(ROW_TILE_FWD, t)

    def body(x_ref, g_ref, w_ref, u_ref, gate_ref):
        _, _, h = _rms(x_ref[...], g_ref[...])
        h = h.astype(MXU_DTYPE)
        half = BRANCH // 2
        for j in range(2):
            u_ref[:, j * half:(j + 1) * half] = _mm(h, w_ref[j])
            gate_ref[:, j * half:(j + 1) * half] = _mm(h, w_ref[2 + j])

    return pl.pallas_call(
        body, name="ssm_proj_fwd", grid=(t // tm,),
        in_specs=[_rows(tm, D_MODEL), _whole((1, D_MODEL)), _whole((4, D_MODEL, BRANCH // 2))],
        out_specs=[_rows(tm, BRANCH), _rows(tm, BRANCH)],
        out_shape=[_sds((t, BRANCH)), _sds((t, BRANCH))],
        compiler_params=_params(("parallel",)),
    )(x, norm, w_in)


def _chunk_rows(ref, nk, dtype=None):
    rows = jnp.concatenate([ref[pl.ds(s, nk, stride=S5_CHUNK), :] for s in range(S5_CHUNK)], axis=1)
    return rows.astype(MXU_DTYPE if dtype is None else dtype)


def _store_chunk_rows(ref, val, nk):
    for s in range(S5_CHUNK):
        ref[pl.ds(s, nk, stride=S5_CHUNK), :] = val[:, s * LANES:(s + 1) * LANES]


def _own_group_mask():
    row = lax.broadcasted_iota(jnp.int32, (S5_OCT_IN, S5_OCT_STATE), 0)
    col = lax.broadcasted_iota(jnp.int32, (S5_OCT_IN, S5_OCT_STATE), 1)
    return ((row % LANES) // SSM_GROUP) == (col // SSM_STATE)


def _spread_groups(w):
    return jnp.where(_own_group_mask(), jnp.concatenate([w] * (S5_OCT_STATE // LANES), axis=1), 0.0).astype(MXU_DTYPE)


def _fold_groups(p):
    p = jnp.where(_own_group_mask(), p, 0.0)
    return sum(p[:, q * LANES:(q + 1) * LANES] for q in range(S5_OCT_STATE // LANES))


def _fill_toeplitz(win_ref, kd_ref):
    win_ref[...] = jnp.zeros_like(win_ref)
    for s in range(S5_CHUNK):
        for t in range(s, S5_CHUNK):
            win_ref[s * LANES:(s + 1) * LANES, t * LANES:(t + 1) * LANES] = kd_ref[t - s].astype(MXU_DTYPE)


TOEPLITZ_BLOCK = 512
_TOEPLITZ_BLOCKS = [(lo, lo + TOEPLITZ_BLOCK) for lo in range(0, S5_OCT_IN, TOEPLITZ_BLOCK)]


def _strip(t):
    return pl.BlockSpec((t, LANES), lambda b: (0, b))


def _oct_states(nk):
    return pl.BlockSpec((nk, S5_OCT_STATE), lambda b: (0, b))


OCT_W = pl.BlockSpec((None, S5_OCT_IN, LANES), lambda b: (b, 0, 0))
OCT_KD = pl.BlockSpec((None, S5_CHUNK, LANES, LANES), lambda b: (b, 0, 0, 0))


def s5_chunk_states(u, ws_re, ws_im):
    t = u.shape[0]
    nk = t // S5_CHUNK

    def body(u_ref, wr_ref, wi_ref, re_ref, im_ref):
        uc = _chunk_rows(u_ref, nk)
        re_ref[...] = _mm(uc, _spread_groups(wr_ref[...]))
        im_ref[...] = _mm(uc, _spread_groups(wi_ref[...]))

    return pl.pallas_call(
        body, name="s5_chunk_states", grid=(S5_OCTETS,),
        in_specs=[_strip(t), OCT_W, OCT_W], out_specs=[_oct_states(nk), _oct_states(nk)],
        out_shape=[_sds((nk, S5_STATES)), _sds((nk, S5_STATES))],
        compiler_params=_params(("parallel",)),
    )(u, ws_re, ws_im)


def s5_scan_fwd(s_re, s_im, a_re, a_im):
    nk = s_re.shape[0]

    def body(sre_ref, sim_ref, ar_ref, ai_ref, hre_ref, him_ref):
        ar = ar_ref[...]
        ai = ai_ref[...]

        def step(k, carry):
            hr, hi = carry
            hre_ref[pl.ds(k, 1), :] = hr
            him_ref[pl.ds(k, 1), :] = hi
            sr = sre_ref[pl.ds(k, 1), :]
            si = sim_ref[pl.ds(k, 1), :]
            return ar * hr - ai * hi + sr, ai * hr + ar * hi + si

        zero = jnp.zeros((1, S5_STATES), F32)
        lax.fori_loop(0, nk, step, (zero, zero))

    vm = pl.BlockSpec(memory_space=pltpu.VMEM)
    return pl.pallas_call(
        body, name="s5_scan_fwd", in_specs=[vm, vm, vm, vm], out_specs=[vm, vm],
        out_shape=[_sds((nk, S5_STATES)), _sds((nk, S5_STATES))],
        compiler_params=_params(),
    )(s_re, s_im, a_re, a_im)


def s5_outputs(u, h_re, h_im, kd, wo_re, wo_im):
    t = u.shape[0]
    nk = t // S5_CHUNK

    def body(u_ref, hre_ref, him_ref, kd_ref, wor_ref, woi_ref, y_ref, win_ref):
        _fill_toeplitz(win_ref, kd_ref)
        uc = _chunk_rows(u_ref, nk)
        y = jnp.concatenate([_mm(uc[:, :hi], win_ref[:hi, lo:hi]) for lo, hi in _TOEPLITZ_BLOCKS], axis=1)
        y = y + _mm_nt(hre_ref[...], _spread_groups(wor_ref[...])) + _mm_nt(him_ref[...], _spread_groups(woi_ref[...]))
        _store_chunk_rows(y_ref, y, nk)

    return pl.pallas_call(
        body, name="s5_outputs", grid=(S5_OCTETS,),
        in_specs=[_strip(t), _oct_states(nk), _oct_states(nk), OCT_KD, OCT_W, OCT_W],
        out_specs=_strip(t), out_shape=_sds((t, BRANCH)),
        scratch_shapes=[pltpu.VMEM((S5_OCT_IN, S5_OCT_IN), MXU_DTYPE)],
        compiler_params=_params(("parallel",)),
    )(u, h_re, h_im, kd, wo_re, wo_im)


def s5_state_grads(dy, wo_re, wo_im):
    t = dy.shape[0]
    nk = t // S5_CHUNK

    def body(dy_ref, wor_ref, woi_ref, re_ref, im_ref):
        dyc = _chunk_rows(dy_ref, nk)
        re_ref[...] = _mm(dyc, _spread_groups(wor_ref[...]))
        im_ref[...] = _mm(dyc, _spread_groups(woi_ref[...]))

    return pl.pallas_call(
        body, name="s5_state_grads", grid=(S5_OCTETS,),
        in_specs=[_strip(t), OCT_W, OCT_W], out_specs=[_oct_states(nk), _oct_states(nk)],
        out_shape=[_sds((nk, S5_STATES)), _sds((nk, S5_STATES))],
        compiler_params=_params(("parallel",)),
    )(dy, wo_re, wo_im)


def s5_scan_bwd(dh_re, dh_im, h_re, h_im, a_re, a_im):
    nk = dh_re.shape[0]

    def body(dhr_ref, dhi_ref, hr_ref, hi_ref, ar_ref, ai_ref, dsr_ref, dsi_ref, dar_ref, dai_ref):
        ar = ar_ref[...]
        ai = ai_ref[...]

        dar_ref[...] = jnp.zeros_like(dar_ref)
        dai_ref[...] = jnp.zeros_like(dai_ref)

        def step(i, carry):
            gr, gi = carry
            k = nk - 1 - i
            dhr = dhr_ref[pl.ds(k, 1), :]
            dhi = dhi_ref[pl.ds(k, 1), :]
            dsr_ref[pl.ds(k, 1), :] = gr
            dsi_ref[pl.ds(k, 1), :] = gi
            hr = hr_ref[pl.ds(k, 1), :]
            hi = hi_ref[pl.ds(k, 1), :]
            dar_ref[...] += gr * hr + gi * hi
            dai_ref[...] += gi * hr - gr * hi
            return dhr + ar * gr + ai * gi, dhi - ai * gr + ar * gi

        zero = jnp.zeros((1, S5_STATES), F32)
        lax.fori_loop(0, nk, step, (zero, zero))

    vm = pl.BlockSpec(memory_space=pltpu.VMEM)
    return pl.pallas_call(
        body, name="s5_scan_bwd", in_specs=[vm] * 6, out_specs=[vm] * 4,
        out_shape=[_sds((nk, S5_STATES)), _sds((nk, S5_STATES)), _sds((1, S5_STATES)), _sds((1, S5_STATES))],
        input_output_aliases={0: 0, 1: 1}, compiler_params=_params(),
    )(dh_re, dh_im, h_re, h_im, a_re, a_im)


def s5_input_grads(dy, ds_re, ds_im, kd, ws_re, ws_im):
    t = dy.shape[0]
    nk = t // S5_CHUNK

    def body(dy_ref, dsr_ref, dsi_ref, kd_ref, wsr_ref, wsi_ref, du_ref, win_ref):
        _fill_toeplitz(win_ref, kd_ref)
        dyc = _chunk_rows(dy_ref, nk)
        du = jnp.concatenate([_mm_nt(dyc[:, lo:], win_ref[lo:hi, lo:]) for lo, hi in _TOEPLITZ_BLOCKS], axis=1)
        du = du + _mm_nt(dsr_ref[...], _spread_groups(wsr_ref[...])) + _mm_nt(dsi_ref[...], _spread_groups(wsi_ref[...]))
        _store_chunk_rows(du_ref, du, nk)

    return pl.pallas_call(
        body, name="s5_input_grads", grid=(S5_OCTETS,),
        in_specs=[_strip(t), _oct_states(nk), _oct_states(nk), OCT_KD, OCT_W, OCT_W],
        out_specs=_strip(t), out_shape=_sds((t, BRANCH)),
        scratch_shapes=[pltpu.VMEM((S5_OCT_IN, S5_OCT_IN), MXU_DTYPE)],
        compiler_params=_params(("parallel",)),
    )(dy, ds_re, ds_im, kd, ws_re, ws_im)


def s5_weight_grads(u, dy, h_re, h_im, ds_re, ds_im):
    t = u.shape[0]
    nk = t // S5_CHUNK

    def body(u_ref, dy_ref, hre_ref, him_ref, dsr_ref, dsi_ref, dkd_ref, dwsr_ref, dwsi_ref, dwor_ref, dwoi_ref):
        dyc = _chunk_rows(dy_ref, nk, F32)
        uct = _chunk_rows(u_ref, nk, F32).T.astype(MXU_DTYPE)
        dyct = dyc.T.astype(MXU_DTYPE)
        dyc = dyc.astype(MXU_DTYPE)
        dwsr_ref[...] = _fold_groups(_mm(uct, dsr_ref[...]))
        dwsi_ref[...] = _fold_groups(_mm(uct, dsi_ref[...]))
        dwor_ref[...] = _fold_groups(_mm(dyct, hre_ref[...]))
        dwoi_ref[...] = _fold_groups(_mm(dyct, him_ref[...]))
        dkd_ref[...] = jnp.zeros_like(dkd_ref)
        for tt in range(0, S5_CHUNK, 2):
            p = _mm(uct[:(tt + 2) * LANES], dyc[:, tt * LANES:(tt + 2) * LANES])
            for s in range(tt + 2):
                rows = p[s * LANES:(s + 1) * LANES]
                if s <= tt:
                    dkd_ref[tt - s] += rows[:, :LANES]
                dkd_ref[tt + 1 - s] += rows[:, LANES:]

    return pl.pallas_call(
        body, name="s5_weight_grads", grid=(S5_OCTETS,),
        in_specs=[_strip(t), _strip(t)] + [_oct_states(nk)] * 4,
        out_specs=[OCT_KD, OCT_W, OCT_W, OCT_W, OCT_W],
        out_shape=[_sds((S5_OCTETS, S5_CHUNK, LANES, LANES))] + [_sds((S5_OCTETS, S5_OCT_IN, LANES))] * 4,
        compiler_params=_params(("parallel",)),
    )(u, dy, h_re, h_im, ds_re, ds_im)


def ssm_mix_fwd(x, u, gate, y_scan, d, w_glu, b_glu, w_out):
    t = x.shape[0]
    tm = min(ROW_TILE_FWD, t)

    def body(x_ref, u_ref, gate_ref, ys_ref, d_ref, wg_ref, bg_ref, wo_ref, y_ref, g2_ref, xo_ref):
        y = ys_ref[...] + d_ref[...] * u_ref[...]
        z0 = _gelu(y)
        g2 = _mm(z0, wg_ref[...]) + bg_ref[...]
        a = z0 * _sigmoid(g2) * _silu(gate_ref[...])
        y_ref[...] = y
        g2_ref[...] = g2
        xo_ref[...] = x_ref[...] + _mm(a, wo_ref[...])

    row = _rows(tm, BRANCH)
    vec = _whole((1, BRANCH))
    mat = _whole((BRANCH, BRANCH))
    return pl.pallas_call(
        body, name="ssm_mix_fwd", grid=(t // tm,),
        in_specs=[row, row, row, row, vec, mat, vec, mat],
        out_specs=[row, row, row],
        out_shape=[_sds((t, BRANCH))] * 3,
        compiler_params=_params(("parallel",)),
    )(x, u, gate, y_scan, d, w_glu, b_glu, w_out)


def ssm_mix_bwd(dxo, u, gate, y, g2, w_glu, w_out, token=None):
    t = dxo.shape[0]
    tm = min(ROW_TILE_BWD, t)
    extra, extra_specs = _after(token)

    def body(dxo_ref, u_ref, gate_ref, y_ref, g2_ref, wgt_ref, wot_ref, *rest):
        dy_ref, dgate_ref, dwo_ref, dwg_ref, dbg_ref, dd_ref = rest[-6:]

        @pl.when(pl.program_id(0) == 0)
        def _():
            dwo_ref[...] = jnp.zeros_like(dwo_ref)
            dwg_ref[...] = jnp.zeros_like(dwg_ref)
            dbg_ref[...] = jnp.zeros_like(dbg_ref)
            dd_ref[...] = jnp.zeros_like(dd_ref)

        dxo = dxo_ref[...]
        gate = gate_ref[...]
        y = y_ref[...]
        z0, z0_grad = _gelu_and_grad(y)
        sg = _sigmoid(g2_ref[...])
        z = z0 * sg
        sgate, sgate_grad = _silu_and_grad(gate)
        da = _mm_nt(dxo, wot_ref[...])
        dwo_ref[...] += _mm_tn(z * sgate, dxo)
        dz = da * sgate
        dgate_ref[...] = da * z * sgate_grad
        dg2 = dz * z0 * sg * (1.0 - sg)
        dbg_ref[...] += jnp.sum(dg2, axis=0, keepdims=True)
        dwg_ref[...] += _mm_tn(z0, dg2)
        dz0 = dz * sg + _mm_nt(dg2, wgt_ref[...])
        dy = dz0 * z0_grad
        dd_ref[...] += jnp.sum(dy * u_ref[...], axis=0, keepdims=True)
        dy_ref[...] = dy

    row = _rows(tm, BRANCH)
    vec = _whole((1, BRANCH))
    mat = _whole((BRANCH, BRANCH))
    return pl.pallas_call(
        body, name="ssm_mix_bwd", grid=(t // tm,),
        in_specs=[row, row, row, row, row, mat, mat] + extra_specs,
        out_specs=[row, row, mat, mat, vec, vec],
        out_shape=[_sds((t, BRANCH)), _sds((t, BRANCH)), _sds((BRANCH, D_MODEL)), _sds((BRANCH, BRANCH)),
                   _sds((1, BRANCH)), _sds((1, BRANCH))],
        compiler_params=_params(("arbitrary",)),
    )(dxo, u, gate, y, g2, w_glu, w_out, *extra)


def ssm_proj_bwd(x, norm, dxo, dy, du_scan, dgate, d, w_in):
    t = x.shape[0]
    tm = min(ROW_TILE_BWD, t)
    n = 2 * BRANCH

    def body(x_ref, g_ref, dxo_ref, dy_ref, dus_ref, dgate_ref, d_ref, wt_ref, dx_ref, dw_ref, dg_ref):
        @pl.when(pl.program_id(0) == 0)
        def _():
            dw_ref[...] = jnp.zeros_like(dw_ref)
            dg_ref[...] = jnp.zeros_like(dg_ref)

        g = g_ref[...]
        r, xhat, h = _rms(x_ref[...], g)
        h = h.astype(MXU_DTYPE)
        du = dus_ref[...] + d_ref[...] * dy_ref[...]
        dproj = jnp.concatenate([du, dgate_ref[...]], axis=1).astype(MXU_DTYPE)
        dh = jnp.zeros((tm, D_MODEL), F32)
        for j in range(4):
            cols = dproj[:, j * (n // 4):(j + 1) * (n // 4)]
            dh = dh + _mm_nt(cols, wt_ref[j])
            dw_ref[j] += _mm_tn(h, cols)
        dx, dg = _rms_bwd(dh, g, r, xhat)
        dg_ref[...] += dg
        dx_ref[...] = dxo_ref[...] + dx

    row = _rows(tm, D_MODEL)
    vec = _whole((1, D_MODEL))
    blocks = _whole((4, D_MODEL, n // 4))
    return pl.pallas_call(
        body, name="ssm_proj_bwd", grid=(t // tm,),
        in_specs=[row, vec, row, row, row, row, vec, blocks],
        out_specs=[row, blocks, vec],
        out_shape=[_sds((t, D_MODEL)), _sds((4, D_MODEL, n // 4)), _sds((1, D_MODEL))],
        compiler_params=_params(("arbitrary",)),
    )(x, norm, dxo, dy, du_scan, dgate, d, w_in)


ATTN_N = Q_DIM + 2 * KV_DIM + BRANCH


def attn_proj_fwd(x, norm, w_in_t, cos2, sin2):
    t = x.shape[0]
    tm = min(ROW_TILE_FWD, t)

    def body(x_ref, g_ref, w_ref, cos_ref, sin_ref, q_ref, k_ref, v_ref, gate_ref):
        _, _, h = _rms(x_ref[...], g_ref[...])
        p = _mm_nt(h, w_ref[...])
        cs = cos_ref[...]
        sn = sin_ref[...]
        q = p[:, :Q_DIM]
        k = p[:, Q_DIM:Q_DIM + KV_DIM]
        q_ref[...] = q * _tile_lanes(cs, Q_DIM // LANES) + _swap_half_heads(q) * _tile_lanes(sn, Q_DIM // LANES)
        k_ref[...] = k * cs + _swap_half_heads(k) * sn
        v_ref[...] = p[:, Q_DIM + KV_DIM:Q_DIM + 2 * KV_DIM]
        gate_ref[...] = p[:, Q_DIM + 2 * KV_DIM:]

    return pl.pallas_call(
        body, name="attn_proj_fwd", grid=(t // tm,),
        in_specs=[_rows(tm, D_MODEL), _whole((1, D_MODEL)), _whole((ATTN_N, D_MODEL)), _rows(tm, LANES), _rows(tm, LANES)],
        out_specs=[_rows(tm, Q_DIM), _rows(tm, KV_DIM), _rows(tm, KV_DIM), _rows(tm, BRANCH)],
        out_shape=[_sds((t, Q_DIM)), _sds((t, KV_DIM)), _sds((t, KV_DIM)), _sds((t, BRANCH))],
        compiler_params=_params(("parallel",)),
    )(x, norm, w_in_t, cos2, sin2)


GQA_LANES = GQA_GROUP * ATTN_BLOCK


def _window_masks(first_block):
    kj = lax.broadcasted_iota(jnp.int32, (ATTN_BLOCK, GQA_LANES), 0)
    qi = lax.broadcasted_iota(jnp.int32, (ATTN_BLOCK, GQA_LANES), 1) % ATTN_BLOCK
    return kj > qi, kj > jnp.where(first_block, qi, ATTN_BLOCK)


def _fold(upper, both):
    return jnp.where(upper, both[:ATTN_BLOCK], both[ATTN_BLOCK:])


def _unfold(upper, tile):
    return jnp.concatenate([jnp.where(upper, tile, 0.0), jnp.where(upper, 0.0, tile)], axis=0).astype(MXU_DTYPE)


def _stack_heads(ref, group):
    return jnp.concatenate([ref[:, h * HEAD_DIM:(h + 1) * HEAD_DIM] for h in range(group * GQA_GROUP, (group + 1) * GQA_GROUP)], axis=0)


def _unstack_heads(ref, group, stacked):
    for n in range(GQA_GROUP):
        h = group * GQA_GROUP + n
        ref[:, h * HEAD_DIM:(h + 1) * HEAD_DIM] = stacked[n * ATTN_BLOCK:(n + 1) * ATTN_BLOCK]


def _sink_row(sink_ref, group):
    return jnp.concatenate([jnp.full((1, ATTN_BLOCK), sink_ref[group * GQA_GROUP + n], F32) for n in range(GQA_GROUP)], axis=1)


def _lane_is(h):
    return lax.broadcasted_iota(jnp.int32, (1, LANES), 1) == h


def attn_fwd(q, k, v, sinks):
    t = q.shape[0]
    nb = t // ATTN_BLOCK
    scale = HEAD_DIM ** -0.5

    def body(sink_ref, q_ref, kc_ref, kp_ref, vc_ref, vp_ref, o_ref, lse_ref):
        keys = jnp.concatenate([kp_ref[...], kc_ref[...]], axis=0).astype(MXU_DTYPE)
        vals = jnp.concatenate([vp_ref[...], vc_ref[...]], axis=0).astype(MXU_DTYPE)
        upper, dead = _window_masks(pl.program_id(0) == 0)
        for g in range(N_KV_HEADS):
            kv = slice(g * HEAD_DIM, (g + 1) * HEAD_DIM)
            qs = _stack_heads(q_ref, g) * scale
            s = jnp.where(dead, NEG_INF, _fold(upper, _mm_nt(keys[:, kv], qs)))
            sink = _sink_row(sink_ref, g)
            m = jnp.maximum(jnp.max(s, axis=0, keepdims=True), sink)
            p = jnp.exp(s - m)
            den = jnp.sum(p, axis=0, keepdims=True) + jnp.exp(sink - m)
            _unstack_heads(o_ref, g, _mm_tn(_unfold(upper, p * (1.0 / den)), vals[:, kv]))
            lse = m + jnp.log(den)
            for n in range(GQA_GROUP):
                lse_ref[pl.ds(g * GQA_GROUP + n, 1), :] = lse[:, n * ATTN_BLOCK:(n + 1) * ATTN_BLOCK]

    cur = lambda n: pl.BlockSpec((ATTN_BLOCK, n), lambda i: (i, 0))
    prev = lambda n: pl.BlockSpec((ATTN_BLOCK, n), lambda i: (jnp.maximum(i - 1, 0), 0))
    return pl.pallas_call(
        body, name="attn_fwd", grid=(nb,),
        in_specs=[pl.BlockSpec(memory_space=pltpu.SMEM), cur(Q_DIM), cur(KV_DIM), prev(KV_DIM), cur(KV_DIM), prev(KV_DIM)],
        out_specs=[cur(Q_DIM), pl.BlockSpec((N_Q_HEADS, ATTN_BLOCK), lambda i: (0, i))],
        out_shape=[_sds((t, Q_DIM)), _sds((N_Q_HEADS, t))],
        compiler_params=_params(("parallel",)),
    )(sinks, q, k, k, v, v)


def attn_bwd(q, k, v, sinks, o, lse, do):
    t = q.shape[0]
    nb = t // ATTN_BLOCK
    scale = HEAD_DIM ** -0.5

    def body(sink_ref, q_ref, kc_ref, kp_ref, vc_ref, vp_ref, o_ref, lse_ref, do_ref,
             dq_ref, dk_ref, dv_ref, dsink_ref, dk_carry, dv_carry):
        i = pl.program_id(0)

        @pl.when(i == 0)
        def _():
            dsink_ref[...] = jnp.zeros_like(dsink_ref)
            dk_carry[...] = jnp.zeros_like(dk_carry)
            dv_carry[...] = jnp.zeros_like(dv_carry)

        @pl.when(i < nb)
        def _():
            keys = jnp.concatenate([kp_ref[...], kc_ref[...]], axis=0).astype(MXU_DTYPE)
            vals = jnp.concatenate([vp_ref[...], vc_ref[...]], axis=0).astype(MXU_DTYPE)
            upper, dead = _window_masks(i == 0)
            dsink = jnp.zeros((1, LANES), F32)
            dk_heads = []
            dv_heads = []
            for g in range(N_KV_HEADS):
                kv = slice(g * HEAD_DIM, (g + 1) * HEAD_DIM)
                qs = (_stack_heads(q_ref, g) * scale).astype(MXU_DTYPE)
                dos = _stack_heads(do_ref, g)
                lse = jnp.concatenate([lse_ref[pl.ds(g * GQA_GROUP + n, 1), :] for n in range(GQA_GROUP)], axis=1)
                s = jnp.where(dead, NEG_INF, _fold(upper, _mm_nt(keys[:, kv], qs)))
                p = jnp.exp(s - lse)
                delta = _mm_f32(jnp.ones((8, HEAD_DIM), F32), dos * _stack_heads(o_ref, g), ((1,), (1,)))[:1]
                dos = dos.astype(MXU_DTYPE)
                ds = _unfold(upper, p * (_fold(upper, _mm_nt(vals[:, kv], dos)) - delta))
                _unstack_heads(dq_ref, g, _mm_tn(ds, keys[:, kv]) * scale)
                dk_heads.append(_mm(ds, qs))
                dv_heads.append(_mm(_unfold(upper, p), dos))
                at_sink = jnp.exp(_sink_row(sink_ref, g) - lse) * delta
                for n in range(GQA_GROUP):
                    dsink = dsink + jnp.where(_lane_is(g * GQA_GROUP + n), -jnp.sum(at_sink[:, n * ATTN_BLOCK:(n + 1) * ATTN_BLOCK]), 0.0)
            dkk = jnp.concatenate(dk_heads, axis=1)
            dvv = jnp.concatenate(dv_heads, axis=1)
            dsink_ref[...] += dsink
            dk_ref[...] = dk_carry[...] + dkk[:ATTN_BLOCK]
            dv_ref[...] = dv_carry[...] + dvv[:ATTN_BLOCK]
            dk_carry[...] = dkk[ATTN_BLOCK:]
            dv_carry[...] = dvv[ATTN_BLOCK:]

        @pl.when(i == nb)
        def _():
            dk_ref[...] = dk_carry[...]
            dv_ref[...] = dv_carry[...]

    last = nb - 1
    cur = lambda n: pl.BlockSpec((ATTN_BLOCK, n), lambda i: (jnp.minimum(i, last), 0))
    prev = lambda n: pl.BlockSpec((ATTN_BLOCK, n), lambda i: (jnp.clip(i - 1, 0, last), 0))
    late = lambda n: pl.BlockSpec((ATTN_BLOCK, n), lambda i: (i, 0))
    dq, dk_late, dv_late, dsinks = pl.pallas_call(
        body, name="attn_bwd", grid=(nb + 1,),
        in_specs=[pl.BlockSpec(memory_space=pltpu.SMEM), cur(Q_DIM), cur(KV_DIM), prev(KV_DIM), cur(KV_DIM), prev(KV_DIM),
                  cur(Q_DIM), pl.BlockSpec((N_Q_HEADS, ATTN_BLOCK), lambda i: (0, jnp.minimum(i, last))), cur(Q_DIM)],
        out_specs=[cur(Q_DIM), late(KV_DIM), late(KV_DIM), _whole((1, LANES))],
        out_shape=[_sds((t, Q_DIM)), _sds((t + ATTN_BLOCK, KV_DIM)), _sds((t + ATTN_BLOCK, KV_DIM)), _sds((1, LANES))],
        scratch_shapes=[pltpu.VMEM((ATTN_BLOCK, KV_DIM), F32), pltpu.VMEM((ATTN_BLOCK, KV_DIM), F32)],
        compiler_params=_params(("arbitrary",)),
    )(sinks, q, k, k, v, v, o, lse, do)
    return dq, dk_late[ATTN_BLOCK:], dv_late[ATTN_BLOCK:], dsinks


def attn_out_fwd(x, o, gate, w_out):
    t = x.shape[0]
    tm = min(ROW_TILE_FWD, t)

    def body(x_ref, o_ref, gate_ref, w_ref, xo_ref):
        xo_ref[...] = x_ref[...] + _mm(o_ref[...] * _silu(gate_ref[...]), w_ref[...])

    row = _rows(tm, D_MODEL)
    return pl.pallas_call(
        body, name="attn_out_fwd", grid=(t // tm,),
        in_specs=[row, row, row, _whole((Q_DIM, D_MODEL))], out_specs=row, out_shape=_sds((t, D_MODEL)),
        compiler_params=_params(("parallel",)),
    )(x, o, gate, w_out)


def attn_out_bwd(dxo, o, gate, w_out, token=None):
    t = dxo.shape[0]
    tm = min(ROW_TILE_BWD, t)
    extra, extra_specs = _after(token)

    def body(dxo_ref, o_ref, gate_ref, wt_ref, *rest):
        do_ref, dgate_ref, dw_ref = rest[-3:]

        @pl.when(pl.program_id(0) == 0)
        def _():
            dw_ref[...] = jnp.zeros_like(dw_ref)

        dxo = dxo_ref[...]
        o = o_ref[...]
        gate = gate_ref[...]
        sgate, sgate_grad = _silu_and_grad(gate)
        da = _mm_nt(dxo, wt_ref[...])
        dw_ref[...] += _mm_tn(o * sgate, dxo)
        do_ref[...] = da * sgate
        dgate_ref[...] = da * o * sgate_grad

    row = _rows(tm, D_MODEL)
    mat = _whole((Q_DIM, D_MODEL))
    return pl.pallas_call(
        body, name="attn_out_bwd", grid=(t // tm,),
        in_specs=[row, row, row, mat] + extra_specs, out_specs=[row, row, mat],
        out_shape=[_sds((t, Q_DIM)), _sds((t, BRANCH)), _sds((Q_DIM, D_MODEL))],
        compiler_params=_params(("arbitrary",)),
    )(dxo, o, gate, w_out, *extra)


def attn_proj_bwd(x, norm, dxo, dq, dk, dv, dgate, cos2, sin2, w_in_t):
    t = x.shape[0]
    tm = min(ROW_TILE_BWD, t)

    def body(x_ref, g_ref, dxo_ref, dq_ref, dk_ref, dv_ref, dgate_ref, cos_ref, sin_ref, wt_ref, dx_ref, dw_ref, dg_ref):
        @pl.when(pl.program_id(0) == 0)
        def _():
            dw_ref[...] = jnp.zeros_like(dw_ref)
            dg_ref[...] = jnp.zeros_like(dg_ref)

        g = g_ref[...]
        r, xhat, h = _rms(x_ref[...], g)
        cs = cos_ref[...]
        sn = sin_ref[...]
        dqr = dq_ref[...]
        dkr = dk_ref[...]
        dq = dqr * _tile_lanes(cs, Q_DIM // LANES) + _swap_half_heads(dqr * _tile_lanes(sn, Q_DIM // LANES))
        dk = dkr * cs + _swap_half_heads(dkr * sn)
        dproj = jnp.concatenate([dq, dk, dv_ref[...], dgate_ref[...]], axis=1)
        dh = _mm(dproj, wt_ref[...])
        dw_ref[...] += _mm_tn(dproj, h)
        dx, dg = _rms_bwd(dh, g, r, xhat)
        dg_ref[...] += dg
        dx_ref[...] = dxo_ref[...] + dx

    row = _rows(tm, D_MODEL)
    vec = _whole((1, D_MODEL))
    return pl.pallas_call(
        body, name="attn_proj_bwd", grid=(t // tm,),
        in_specs=[row, vec, row, _rows(tm, Q_DIM), _rows(tm, KV_DIM), _rows(tm, KV_DIM), _rows(tm, BRANCH),
                  _rows(tm, LANES), _rows(tm, LANES), _whole((ATTN_N, D_MODEL))],
        out_specs=[row, _whole((ATTN_N, D_MODEL)), vec],
        out_shape=[_sds((t, D_MODEL)), _sds((ATTN_N, D_MODEL)), _sds((1, D_MODEL))],
        compiler_params=_params(("arbitrary",)),
    )(x, norm, dxo, dq, dk, dv, dgate, cos2, sin2, w_in_t)


def attn_out_loss(x, o, gate, w_out, norm, target):
    t = x.shape[0]
    tm = min(ROW_TILE_FWD, t)

    def body(x_ref, o_ref, gate_ref, w_ref, g_ref, tgt_ref, loss_ref, dx_ref, dg_ref):
        @pl.when(pl.program_id(0) == 0)
        def _():
            loss_ref[...] = jnp.zeros_like(loss_ref)
            dg_ref[...] = jnp.zeros_like(dg_ref)

        out = x_ref[...] + _mm(o_ref[...] * _silu(gate_ref[...]), w_ref[...])
        g = g_ref[...]
        r, xhat, y = _rms(out, g)
        err = y - tgt_ref[...]
        loss_ref[...] += 0.5 * jnp.sum(jnp.mean(err * err, axis=-1, keepdims=True), axis=0, keepdims=True)
        dx, dg = _rms_bwd(err * (1.0 / D_MODEL), g, r, xhat)
        dg_ref[...] += dg
        dx_ref[...] = dx

    row = _rows(tm, D_MODEL)
    vec = _whole((1, D_MODEL))
    return pl.pallas_call(
        body, name="attn_out_loss", grid=(t // tm,),
        in_specs=[row, row, row, _whole((Q_DIM, D_MODEL)), vec, row], out_specs=[_whole((1, 1)), row, vec],
        out_shape=[_sds((1, 1)), _sds((t, D_MODEL)), _sds((1, D_MODEL))],
        compiler_params=_params(("arbitrary",)),
    )(x, o, gate, w_out, norm, target)


OCT_TILE = pl.BlockSpec((None, LANES, LANES), lambda b: (b, 0, 0))
N_LAGS = S5_CHUNK + 1


def _cmul(ar, ai, br, bi):
    return ar * br - ai * bi, ar * bi + ai * br


def _cmul_conj(ar, ai, br, bi):
    return ar * br + ai * bi, ar * bi - ai * br


def _mm_f32(a, b, dims):
    return lax.dot_general(a, b, (dims, ((), ())), precision=lax.Precision.HIGH, preferred_element_type=F32)


def _s5_discretise(ar, ai, ls, br, bi):
    dt = jnp.exp(ls)
    xr = ar * dt
    xi = ai * dt
    mag = jnp.exp(xr)
    first = (mag * jnp.cos(xi), mag * jnp.sin(xi))
    powers = [(jnp.ones_like(xr), jnp.zeros_like(xr)), first]
    for _ in range(2, N_LAGS):
        powers.append(_cmul(*powers[-1], *first))
    den = ar * ar + ai * ai
    nr = powers[1][0] - 1.0
    ni = powers[1][1]
    fr = (nr * ar + ni * ai) / den
    fi = (ni * ar - nr * ai) / den
    bbr, bbi = _cmul(fr, fi, br, bi)
    return dt, powers, (fr, fi), (bbr, bbi), den


def _same_group_tile():
    row = lax.broadcasted_iota(jnp.int32, (LANES, LANES), 0)
    col = lax.broadcasted_iota(jnp.int32, (LANES, LANES), 1)
    return (row // SSM_GROUP) == (col // SSM_GROUP)


def _first_copy_lanes():
    return lax.broadcasted_iota(jnp.int32, (LANES, LANES), 1) < SSM_STATE


def s5_param_fwd(tiles, token=None):
    extra, extra_specs = _after(token)

    def body(ar_ref, ai_ref, ls_ref, br_ref, bi_ref, cr_ref, ci_ref, *rest):
        kd_ref, wsr_ref, wsi_ref, wor_ref, woi_ref, pr_ref, pi_ref = rest[-7:]
        cr = cr_ref[...]
        ci = ci_ref[...]
        _, powers, _, (bbr, bbi), _ = _s5_discretise(ar_ref[...], ai_ref[...], ls_ref[...], br_ref[...], bi_ref[...])
        once = _first_copy_lanes()
        crm = jnp.where(once, cr, 0.0)
        cim = jnp.where(once, ci, 0.0)
        same = _same_group_tile()
        for lag in range(S5_CHUNK):
            er, ei = powers[lag]
            xr, xi = _cmul(er, ei, bbr, bbi)
            rows = pl.ds((S5_CHUNK - 1 - lag) * LANES, LANES)
            wsr_ref[rows, :] = xr
            wsi_ref[rows, :] = xi
        k = _mm_f32(wsr_ref[...], crm, ((1,), (1,))) - _mm_f32(wsi_ref[...], cim, ((1,), (1,)))
        for lag in range(S5_CHUNK):
            kd_ref[lag] = jnp.where(same, k[(S5_CHUNK - 1 - lag) * LANES:(S5_CHUNK - lag) * LANES], 0.0)
        for t in range(S5_CHUNK):
            er, ei = powers[t + 1]
            zr, zi = _cmul(er, ei, cr, ci)
            wor_ref[pl.ds(t * LANES, LANES), :] = zr
            woi_ref[pl.ds(t * LANES, LANES), :] = -zi
        pr_ref[...] = powers[S5_CHUNK][0]
        pi_ref[...] = powers[S5_CHUNK][1]

    return pl.pallas_call(
        body, name="s5_param_fwd", grid=(S5_OCTETS,),
        in_specs=[OCT_TILE] * 7 + [ANY] * len(extra),
        out_specs=[OCT_KD, OCT_W, OCT_W, OCT_W, OCT_W, OCT_TILE, OCT_TILE],
        out_shape=[_sds((S5_OCTETS, S5_CHUNK, LANES, LANES))] + [_sds((S5_OCTETS, S5_OCT_IN, LANES))] * 4
                  + [_sds((S5_OCTETS, LANES, LANES))] * 2,
        compiler_params=_params(("parallel",)),
    )(*tiles, *extra)


def s5_param_bwd(tiles, dkd, dws_re, dws_im, dwo_re, dwo_im, dp_re, dp_im):
    def body(ar_ref, ai_ref, ls_ref, br_ref, bi_ref, cr_ref, ci_ref, dkd_ref, dwsr_ref, dwsi_ref, dwor_ref, dwoi_ref, dpr_ref, dpi_ref,
             dar_ref, dai_ref, dls_ref, dbr_ref, dbi_ref, dcr_ref, dci_ref):
        ar = ar_ref[...]
        ai = ai_ref[...]
        br = br_ref[...]
        bi = bi_ref[...]
        cr = cr_ref[...]
        ci = ci_ref[...]
        dt, powers, (fr, fi), (bbr, bbi), den = _s5_discretise(ar, ai, ls_ref[...], br, bi)
        once = _first_copy_lanes()
        crm = jnp.where(once, cr, 0.0)
        cim = jnp.where(once, ci, 0.0)
        same = _same_group_tile()
        zero = jnp.zeros((LANES, LANES), F32)
        dpow = [[zero, zero] for _ in range(N_LAGS)]
        dbbr, dbbi = zero, zero
        by_step = [S5_CHUNK - 1 - s for s in range(S5_CHUNK)]
        x_all = [_cmul(*powers[lag], bbr, bbi) for lag in by_step]
        xr_all = jnp.concatenate([x[0] for x in x_all], axis=0)
        xi_all = jnp.concatenate([x[1] for x in x_all], axis=0)
        g_all = jnp.concatenate([jnp.where(same, dkd_ref[lag], 0.0) for lag in by_step], axis=0)
        dxr_all = dwsr_ref[...] + _mm_f32(g_all, crm, ((1,), (0,)))
        dxi_all = dwsi_ref[...] - _mm_f32(g_all, cim, ((1,), (0,)))
        dcr = jnp.where(once, _mm_f32(g_all, xr_all, ((0,), (0,))), 0.0)
        dci = -jnp.where(once, _mm_f32(g_all, xi_all, ((0,), (0,))), 0.0)
        for lag in range(S5_CHUNK):
            er, ei = powers[lag]
            rows = slice((S5_CHUNK - 1 - lag) * LANES, (S5_CHUNK - lag) * LANES)
            dxr = dxr_all[rows]
            dxi = dxi_all[rows]
            a, b = _cmul_conj(bbr, bbi, dxr, dxi)
            dpow[lag][0] = dpow[lag][0] + a
            dpow[lag][1] = dpow[lag][1] + b
            a, b = _cmul_conj(er, ei, dxr, dxi)
            dbbr = dbbr + a
            dbbi = dbbi + b
        for t in range(S5_CHUNK):
            er, ei = powers[t + 1]
            dzr = dwor_ref[pl.ds(t * LANES, LANES), :]
            dzi = -dwoi_ref[pl.ds(t * LANES, LANES), :]
            a, b = _cmul_conj(cr, ci, dzr, dzi)
            dpow[t + 1][0] = dpow[t + 1][0] + a
            dpow[t + 1][1] = dpow[t + 1][1] + b
            a, b = _cmul_conj(er, ei, dzr, dzi)
            dcr = dcr + a
            dci = dci + b
        dpow[S5_CHUNK][0] = dpow[S5_CHUNK][0] + dpr_ref[...]
        dpow[S5_CHUNK][1] = dpow[S5_CHUNK][1] + dpi_ref[...]
        dfr, dfi = _cmul_conj(br, bi, dbbr, dbbi)
        dbr, dbi = _cmul_conj(fr, fi, dbbr, dbbi)
        dnr, dni = _cmul(ar / den, ai / den, dfr, dfi)
        qr = (fr * ar + fi * ai) / den
        qi = (fi * ar - fr * ai) / den
        dlr, dli = _cmul(-qr, qi, dfr, dfi)
        dpow[1][0] = dpow[1][0] + dnr
        dpow[1][1] = dpow[1][1] + dni
        dxr, dxi = zero, zero
        for lag in range(1, N_LAGS):
            a, b = _cmul_conj(powers[lag][0], powers[lag][1], dpow[lag][0], dpow[lag][1])
            dxr = dxr + lag * a
            dxi = dxi + lag * b
        dar_ref[...] = dlr + dt * dxr
        dai_ref[...] = dli + dt * dxi
        dls_ref[...] = dt * (ar * dxr + ai * dxi)
        dbr_ref[...] = dbr
        dbi_ref[...] = dbi
        dcr_ref[...] = dcr
        dci_ref[...] = dci

    return pl.pallas_call(
        body, name="s5_param_bwd", grid=(S5_OCTETS,),
        in_specs=[OCT_TILE] * 7 + [OCT_KD, OCT_W, OCT_W, OCT_W, OCT_W, OCT_TILE, OCT_TILE], out_specs=[OCT_TILE] * 7,
        out_shape=[_sds((S5_OCTETS, LANES, LANES))] * 7,
        compiler_params=_params(("parallel",)),
    )(*tiles, dkd, dws_re, dws_im, dwo_re, dwo_im, dp_re, dp_im)


def _doubled(v):
    return jnp.concatenate([v, v], axis=-1)


def _s5_param_tiles(a_re, a_im, log_step, b_re, b_im, c_re, c_im):
    def per_group(a):
        return _doubled(jnp.broadcast_to(a.reshape(S5_OCTETS, S5_OCT, 1, SSM_STATE),
                                         (S5_OCTETS, S5_OCT, SSM_GROUP, SSM_STATE)).reshape(S5_OCTETS, LANES, SSM_STATE))

    ls = jnp.broadcast_to(log_step.reshape(S5_OCTETS, S5_OCT, 1, 1), (S5_OCTETS, S5_OCT, SSM_GROUP, LANES)).reshape(S5_OCTETS, LANES, LANES)
    bt = lambda b: _doubled(b.transpose(0, 2, 1).reshape(S5_OCTETS, LANES, SSM_STATE))
    ct = lambda c: _doubled(c.reshape(S5_OCTETS, LANES, SSM_STATE))
    return [per_group(a_re), per_group(a_im), ls, bt(b_re), bt(b_im), ct(c_re), ct(c_im)]


def _s5_param_grads(dtiles):
    dar, dai, dls, dbr, dbi, dcr, dci = dtiles
    halves = lambda d: d[..., :SSM_STATE] + d[..., SSM_STATE:]
    per_group = lambda d: halves(d).reshape(SSM_GROUPS, SSM_GROUP, SSM_STATE).sum(axis=1)
    per_row = lambda d: halves(d).reshape(SSM_GROUPS, SSM_GROUP, SSM_STATE)
    return (per_group(dar), per_group(dai), dls.reshape(SSM_GROUPS, SSM_GROUP * LANES).sum(axis=1),
            per_row(dbr).transpose(0, 2, 1), per_row(dbi).transpose(0, 2, 1), per_row(dcr), per_row(dci))


def _group_power_rows(tile):
    return tile[:, ::SSM_GROUP, :SSM_STATE].reshape(1, S5_STATES)


def _group_power_tiles(row):
    t = jnp.pad(row.reshape(S5_OCTETS, S5_OCT, 1, SSM_STATE), ((0, 0), (0, 0), (0, SSM_GROUP - 1), (0, LANES - SSM_STATE)))
    return t.reshape(S5_OCTETS, LANES, LANES)


def _rope_tables(t):
    pos = jnp.arange(t, dtype=F32)
    inv_freq = ROPE_THETA ** (-jnp.arange(0, HEAD_DIM, 2, dtype=F32) / HEAD_DIM)
    ang = pos[:, None] * inv_freq[None, :]
    cos = jnp.cos(ang)
    sin = jnp.sin(ang)
    cos64 = jnp.concatenate([cos, cos], axis=1)
    sin64 = jnp.concatenate([-sin, sin], axis=1)
    return jnp.concatenate([cos64, cos64], axis=1), jnp.concatenate([sin64, sin64], axis=1)


def _row(v):
    return v.reshape(1, -1)


def _s5_matrices(w, token=None):
    tiles = _s5_param_tiles(w["a_re"], w["a_im"], w["log_step"], w["b_re"], w["b_im"], w["c_re"], w["c_im"])
    kd, ws_re, ws_im, wo_re, wo_im, p_re, p_im = s5_param_fwd(tiles, token)
    return tiles, dict(kd=kd, ws_re=ws_re, ws_im=ws_im, wo_re=wo_re, wo_im=wo_im, a_re=_group_power_rows(p_re), a_im=_group_power_rows(p_im))


def _ssm_forward(x, w):
    tiles, mats = w["s5"] if "s5" in w else _s5_matrices(w)
    u, gate = ssm_proj_fwd(x, _row(w["norm"]), w["w_in"])
    s_re, s_im = s5_chunk_states(u, mats["ws_re"], mats["ws_im"])
    h_re, h_im = s5_scan_fwd(s_re, s_im, mats["a_re"], mats["a_im"])
    y_scan = s5_outputs(u, h_re, h_im, mats["kd"], mats["wo_re"], mats["wo_im"])
    y, g2, x_new = ssm_mix_fwd(x, u, gate, y_scan, _row(w["d"]), w["w_glu"], _row(w["b_glu"]), w["w_out"])
    saved = dict(x=x, u=u, gate=gate, y=y, g2=g2, h_re=h_re, h_im=h_im, mats=mats, tiles=tiles)
    return x_new, saved


def _ssm_backward(dxo, w, s, token=None):
    dy, dgate, dw_out, dw_glu, db_glu, dd = ssm_mix_bwd(dxo, s["u"], s["gate"], s["y"], s["g2"], w["w_glu"], w["w_out"], token)
    mats = s["mats"]
    dh_re, dh_im = s5_state_grads(dy, mats["wo_re"], mats["wo_im"])
    ds_re, ds_im, da_re, da_im = s5_scan_bwd(dh_re, dh_im, s["h_re"], s["h_im"], mats["a_re"], mats["a_im"])
    du_scan = s5_input_grads(dy, ds_re, ds_im, mats["kd"], mats["ws_re"], mats["ws_im"])
    dkd, dws_re, dws_im, dwo_re, dwo_im = s5_weight_grads(s["u"], dy, s["h_re"], s["h_im"], ds_re, ds_im)
    dparams = _s5_param_grads(s5_param_bwd(s["tiles"], dkd, dws_re, dws_im, dwo_re, dwo_im,
                                           _group_power_tiles(da_re), _group_power_tiles(da_im)))
    dx, dw_in, dnorm = ssm_proj_bwd(s["x"], _row(w["norm"]), dxo, dy, du_scan, dgate, _row(w["d"]), w["w_in"])
    grads = dict(norm=dnorm, w_in=dw_in, d=dd, w_glu=dw_glu, b_glu=db_glu, w_out=dw_out)
    for name, val in zip(("a_re", "a_im", "log_step", "b_re", "b_im", "c_re", "c_im"), dparams):
        grads[name] = val
    return dx, grads


def _attn_forward(x, w, cos2, sin2, loss_head=None):
    q, k, v, gate = attn_proj_fwd(x, _row(w["norm"]), w["w_in"], cos2, sin2)
    o, lse = attn_fwd(q, k, v, w["sinks"])
    if loss_head is None:
        result = attn_out_fwd(x, o, gate, w["w_out"])
    else:
        result = attn_out_loss(x, o, gate, w["w_out"], _row(loss_head[0]), loss_head[1])
    return result, dict(x=x, q=q, k=k, v=v, gate=gate, o=o, lse=lse)


def _attn_backward(dxo, w, s, cos2, sin2, token=None):
    do, dgate, dw_out = attn_out_bwd(dxo, s["o"], s["gate"], w["w_out"], token)
    dq, dk, dv, dsinks = attn_bwd(s["q"], s["k"], s["v"], w["sinks"], s["o"], s["lse"], do)
    dx, dw_in, dnorm = attn_proj_bwd(s["x"], _row(w["norm"]), dxo, dq, dk, dv, dgate, cos2, sin2, w["w_in"])
    return dx, dict(norm=dnorm, w_in=dw_in, sinks=dsinks[0, :N_Q_HEADS], w_out=dw_out)


class _NoExchanges:
    def __init__(self, layers):
        self.layers = layers

    def layer(self, i, x):
        return self.layers[i]

    def layer_done(self, i, grads, dx):
        return None


def _sequence_step(x, target, final_norm, hooks, depth=4):
    cos2, sin2 = _rope_tables(x.shape[0])
    saved, layers = [], []
    for i in range(depth):
        w = hooks.layer(i, x)
        layers.append(w)
        if i % 2 == 0:
            x, s = _ssm_forward(x, w)
        else:
            x, s = _attn_forward(x, w, cos2, sin2, (final_norm, target) if i == depth - 1 else None)
        saved.append(s)
    loss, dx, dfinal = x
    grads = {"final_norm": dfinal}
    token = None
    for i in reversed(range(depth)):
        if i % 2 == 0:
            dx, g = _ssm_backward(dx, layers[i], saved[i], token)
        else:
            dx, g = _attn_backward(dx, layers[i], saved[i], cos2, sin2, token)
        g = {"l%d_%s" % (i, name): val for name, val in g.items()}
        grads.update(g)
        token = hooks.layer_done(i, g, dx)
    return loss[0, 0], dx, grads


ANY = pl.BlockSpec(memory_space=pl.ANY)


def _place():
    return lax.axis_index("x"), lax.axis_index("y"), lax.axis_index("c")


def _other_chips(x, y):
    return [(1 - x, y), (x, 1 - y), (1 - x, 1 - y)]


class _StagedCopies:
    def __init__(self, bufs, load_sems, store_sems):
        self.bufs, self.load_sems, self.store_sems = bufs, load_sems, store_sems
        self.loads, self.stores = [], []

    def load(self, i, src):
        cp = pltpu.make_async_copy(src, self.bufs[i], self.load_sems.at[i])
        cp.start()
        self.loads.append(cp)

    def store(self, i, dst):
        self.loads[i].wait()
        cp = pltpu.make_async_copy(self.bufs[i], dst, self.store_sems.at[i])
        cp.start()
        self.stores.append(cp)

    def finish(self):
        for cp in self.stores:
            cp.wait()


def _staging(blocks):
    n = len(blocks)
    return [pltpu.VMEM(b.shape, b.dtype) for b in blocks] + [pltpu.SemaphoreType.DMA((n,)), pltpu.SemaphoreType.DMA((n,))]


def exchange_halves_with_sibling(grads):
    n = len(grads)

    def body(*refs):
        ins, outs = refs[:n], refs[n:2 * n]
        send_sems, recv_sems = refs[2 * n:]
        x, y, c = _place()
        copies = []
        for i in range(n):
            half = ins[i].shape[1] // 2
            src = ins[i].at[:, pl.ds((1 - c) * half, half), :]
            cp = pltpu.make_async_remote_copy(src_ref=src, dst_ref=outs[i], send_sem=send_sems.at[i], recv_sem=recv_sems.at[i],
                                              device_id=(x, y, 1 - c), device_id_type=MESH)
            cp.start()
            copies.append(cp)
        for cp in copies:
            cp.wait()

    return pl.pallas_call(
        body, name="exchange_halves_with_sibling",
        in_specs=[ANY] * n, out_specs=[ANY] * n,
        out_shape=[_sds((g.shape[0], g.shape[1] // 2, g.shape[2])) for g in grads],
        scratch_shapes=[pltpu.SemaphoreType.DMA((n,)), pltpu.SemaphoreType.DMA((n,))],
    )(*grads)


def swap_halves_with_sibling(pieces):
    n = len(pieces)

    def body(*refs):
        ins, outs = refs[:n], refs[n:2 * n]
        send_sems, recv_sems = refs[2 * n:2 * n + 2]
        own = _StagedCopies(refs[2 * n + 2:3 * n + 2], *refs[3 * n + 2:])
        x, y, c = _place()
        for i in range(n):
            own.load(i, ins[i])
        swaps = []
        for i in range(n):
            cp = pltpu.make_async_remote_copy(src_ref=ins[i], dst_ref=outs[i].at[c], send_sem=send_sems.at[i], recv_sem=recv_sems.at[i],
                                              device_id=(x, y, 1 - c), device_id_type=MESH)
            cp.start()
            swaps.append(cp)
        for i in range(n):
            own.store(i, outs[i].at[c])
        for i in range(n):
            pltpu.make_async_remote_copy(src_ref=ins[i], dst_ref=outs[i].at[1 - c], send_sem=send_sems.at[i], recv_sem=recv_sems.at[i],
                                         device_id=(x, y, 1 - c), device_id_type=MESH).wait_recv()
        for cp in swaps:
            cp.wait_send()
        own.finish()

    return pl.pallas_call(
        body, name="swap_halves_with_sibling",
        in_specs=[ANY] * n, out_specs=[ANY] * n,
        out_shape=[_sds((2,) + p.shape) for p in pieces],
        scratch_shapes=[pltpu.SemaphoreType.DMA((n,)), pltpu.SemaphoreType.DMA((n,))] + _staging(pieces),
        compiler_params=_params(),
    )(*pieces)


IN_HBM = pl.BlockSpec(memory_space=pltpu.HBM)
SEMAPHORES = pl.BlockSpec(memory_space=pltpu.SEMAPHORE)
DATAFLOW = pltpu.SideEffectType.DATAFLOW_SIDE_EFFECTING


def _hbm(a):
    return pltpu.with_memory_space_constraint(a, pltpu.HBM)


def place_own_blocks(shards):
    n = len(shards)

    def body(*refs):
        ins, outs = refs[:n], refs[n:2 * n]
        own = _StagedCopies(refs[2 * n:3 * n], *refs[3 * n:])
        x, y, _ = _place()
        for i in range(n):
            own.load(i, ins[i])
        for i in range(n):
            own.store(i, outs[i].at[2 * x + y])
        own.finish()

    return pl.pallas_call(
        body, name="place_own_blocks", in_specs=[ANY] * n, out_specs=[ANY] * n,
        out_shape=[_sds((4,) + s.shape, s.dtype) for s in shards],
        scratch_shapes=_staging(shards), compiler_params=_params(),
    )(*shards)


def _block_to_send(ref, chip, per_target):
    if not per_target:
        return ref
    return ref.at[chip] if ref.shape[0] == 4 else ref.at[0]


def start_sends_to_chips(name, sources, landings, per_target, after):
    n = len(sources)
    n_sems = 2 * 3 * n

    def body(*refs):
        srcs = refs[:n]
        sems = refs[2 * n + 1:2 * n + 1 + n_sems]
        lands = refs[2 * n + 1 + n_sems:3 * n + 1 + n_sems]
        token = refs[3 * n + 1 + n_sems]
        x, y, c = _place()
        me = 2 * x + y
        for i in range(n):
            for k, (tx, ty) in enumerate(_other_chips(x, y)):
                src = _block_to_send(srcs[i], 2 * tx + ty, per_target)
                pltpu.make_async_remote_copy(src_ref=src, dst_ref=lands[i].at[me], send_sem=sems[2 * (3 * i + k)], recv_sem=sems[2 * (3 * i + k) + 1],
                                             device_id=(tx, ty, c), device_id_type=MESH).start()
        token[...] = jnp.zeros_like(token)

    outs = pl.pallas_call(
        body, name=name,
        in_specs=[IN_HBM] * (2 * n) + [ANY],
        out_specs=[SEMAPHORES] * n_sems + [IN_HBM] * n + [pl.BlockSpec(memory_space=pltpu.VMEM)],
        out_shape=[pltpu.SemaphoreType.DMA(())] * n_sems + [pltpu.HBM(l.shape, l.dtype) for l in landings] + [_sds(TOKEN_SHAPE)],
        input_output_aliases={n + i: n_sems + i for i in range(n)},
        compiler_params=pltpu.CompilerParams(has_side_effects=DATAFLOW),
    )(*[_hbm(s) for s in sources], *[_hbm(l) for l in landings], after)
    return list(outs[:n_sems]), list(outs[n_sems:n_sems + n]), outs[n_sems + n]


def wait_sends_to_chips(name, sources, landings, sems, per_target, after):
    n = len(sources)
    n_sems = len(sems)

    def body(*refs):
        srcs = refs[:n]
        sem_refs = refs[2 * n:2 * n + n_sems]
        lands = refs[2 * n + n_sems + 1:]
        x, y, c = _place()
        me = 2 * x + y
        for i in range(n):
            for k, (tx, ty) in enumerate(_other_chips(x, y)):
                src = _block_to_send(srcs[i], me, per_target)
                cp = pltpu.make_async_remote_copy(src_ref=src, dst_ref=lands[i].at[2 * tx + ty], send_sem=sem_refs[2 * (3 * i + k)],
                                                  recv_sem=sem_refs[2 * (3 * i + k) + 1], device_id=(tx, ty, c), device_id_type=MESH)
                cp.wait_send()
                cp.wait_recv()

    return pl.pallas_call(
        body, name=name,
        in_specs=[IN_HBM] * (2 * n) + [SEMAPHORES] * n_sems + [ANY],
        out_specs=[IN_HBM] * n,
        out_shape=[pltpu.HBM(l.shape, l.dtype) for l in landings],
        input_output_aliases={n + i: i for i in range(n)},
        compiler_params=pltpu.CompilerParams(has_side_effects=DATAFLOW),
    )(*[_hbm(s) for s in sources], *landings, *sems, after)


def _row_tile(rows, cols):
    tm = rows
    while tm * cols * 4 > (2 << 20) and tm % 16 == 0:
        tm //= 2
    return tm


def add_pairs(a_list, b_list, out_dtypes, copies=1):
    n = len(a_list)
    nb = a_list[0].shape[0]

    def body(*refs):
        for i in range(n):
            total = (refs[i][...] + refs[n + i][...]).astype(out_dtypes[i])
            for o_ref in refs[2 * n + i * copies:2 * n + (i + 1) * copies]:
                o_ref[...] = total

    specs = [pl.BlockSpec((None,) + a.shape[1:], lambda j: (j, 0, 0)) for a in a_list]
    outs = pl.pallas_call(
        body, name="add_pairs", grid=(nb,), in_specs=specs * 2, out_specs=[s for s in specs for _ in range(copies)],
        out_shape=[_sds(a.shape, dt) for a, dt in zip(a_list, out_dtypes) for _ in range(copies)],
        compiler_params=_params(("parallel",)),
    )(*a_list, *b_list)
    return [tuple(outs[i * copies:(i + 1) * copies]) for i in range(n)]


def sum_fours(arrays, token=None):
    n = len(arrays)
    extra, extra_specs = _after(token)

    def body(*refs):
        outs = refs[-n:]
        for a_ref, o_ref in zip(refs[:n], outs):
            o_ref[...] = ((a_ref[0].astype(F32) + a_ref[1].astype(F32)) + a_ref[2].astype(F32)) + a_ref[3].astype(F32)

    return pl.pallas_call(
        body, name="sum_fours", grid=(2,),
        in_specs=[pl.BlockSpec((4, a.shape[1] // 2, a.shape[2]), lambda i: (0, i, 0)) for a in arrays] + extra_specs,
        out_specs=[pl.BlockSpec((a.shape[1] // 2, a.shape[2]), lambda i: (i, 0)) for a in arrays],
        out_shape=[_sds(a.shape[1:]) for a in arrays], compiler_params=_params(("parallel",)),
    )(*arrays, *extra)


def _adamw_update(w_ref, g_ref, m_ref, v_ref, d_ref, nm_ref, nv_ref):
    g = g_ref[...]
    nm = ADAM_B1 * m_ref[...] + (1.0 - ADAM_B1) * g
    nv = ADAM_B2 * v_ref[...] + (1.0 - ADAM_B2) * (g * g)
    d_ref[...] = -ADAM_LR * ((nm / (1.0 - ADAM_B1 ** ADAM_STEP)) / (jnp.sqrt(nv / (1.0 - ADAM_B2 ** ADAM_STEP)) + ADAM_EPS) + ADAM_WD * w_ref[...])
    nm_ref[...] = nm
    nv_ref[...] = nv


def adamw(w, g, m, v):
    rows, cols = w.shape
    tm = _row_tile(rows, cols)

    def body(*refs):
        _adamw_update(*refs)

    spec = pl.BlockSpec((tm, cols), lambda i: (i, 0))
    return pl.pallas_call(
        body, name="adamw", grid=(rows // tm,), in_specs=[spec] * 4, out_specs=[spec] * 3,
        out_shape=[_sds(w.shape)] * 3, compiler_params=_params(("parallel",)),
    )(w, g, m, v)


def adamw_small(ws, gs, ms, vs, slabs=None):
    n = len(ws)

    def body(*refs):
        for i in range(n):
            _adamw_update(refs[i], refs[n + i], refs[2 * n + i], refs[3 * n + i], refs[4 * n + i], refs[5 * n + i], refs[6 * n + i])

    if slabs is None:
        grid = ()
        specs = [pl.BlockSpec(memory_space=pltpu.VMEM)] * n
    else:
        grid = (slabs,)
        specs = [pl.BlockSpec((w.shape[0] // slabs,) + w.shape[1:], lambda i: (i, 0, 0)) for w in ws]
    outs = pl.pallas_call(
        body, name="adamw_small", grid=grid, in_specs=specs * 4, out_specs=specs * 3,
        out_shape=[_sds(w.shape) for w in ws] * 3, compiler_params=_params(("parallel",) if slabs else None),
    )(*ws, *gs, *ms, *vs)
    return outs[:n], outs[n:2 * n], outs[2 * n:]


PACK_TILE = 8 * LANES
PACK_PIECES = 8
PACK_ALIGN = PACK_PIECES * 16


def _pack_small(values, scalar=None):
    parts = []
    for name in SMALL_NAMES:
        flat = values[name].reshape(-1)
        pad = (-flat.shape[0]) % PACK_TILE
        if pad:
            flat = jnp.concatenate([flat, jnp.zeros((pad,), F32)])
        parts.append(flat.reshape(-1, LANES))
    rows = sum(p.shape[0] for p in parts) + 8
    parts.append(jnp.zeros(((-rows) % PACK_ALIGN, LANES), F32))
    last = jnp.zeros((8, LANES), F32)
    parts.append(last if scalar is None else jnp.broadcast_to(scalar.astype(F32), (8, LANES)))
    return jnp.concatenate(parts, axis=0)


def _unpack_small(pack, like):
    out = {}
    row = 0
    for name in SMALL_NAMES:
        size = math.prod(like[name].shape)
        rows = -(-size // PACK_TILE) * 8
        out[name] = pack[row:row + rows].reshape(-1)[:size].reshape(like[name].shape)
        row += rows
    return out


def _travels_transposed(name, shard):
    return name.endswith("w_in") and shard.shape[-1] % LANES != 0


def _to_blocks(name, full):
    if full.ndim == 3:
        return full
    return full.reshape(4, full.shape[0] // 4, full.shape[1])


def _from_blocks(name, stacked):
    if name.endswith("w_in") and stacked.shape[2] % LANES == 0 and stacked.shape[1] == D_MODEL:
        return stacked
    return stacked.reshape(4 * stacked.shape[1], stacked.shape[2])


def _layer_big_names(i):
    return [n for n in BIG_NAMES if n.startswith("l%d_" % i)]


class _OverlappedExchanges:
    def __init__(self, weights):
        self.weights = weights
        self.c = lax.axis_index("c")
        self.first = _layer_big_names(0)
        self.later = [n for n in BIG_NAMES if n not in self.first]
        shards = [weights[n].astype(MXU_DTYPE) for n in self.first + self.later]
        shards = [s.T if _travels_transposed(n, s) else s for n, s in zip(self.first + self.later, shards)]
        placed = place_own_blocks(shards)
        k = len(self.first)
        sems, stacks, token = start_sends_to_chips("gather_first_start", shards[:k], placed[:k], False, shards[0])
        self.gather_first = (shards[:k], sems, stacks)
        sems, stacks, token = start_sends_to_chips("gather_later_start", shards[k:], placed[k:], False, token)
        self.gather_later = (shards[k:], sems, stacks)
        self.s5 = {}
        for i in (0, 2):
            self.s5[i] = _s5_matrices({n: weights["l%d_%s" % (i, n)] for n in SSM_NAMES if "l%d_%s" % (i, n) in SMALL_NAMES}, token)
            token = self.s5[i][1]["kd"]
        self.full = {}
        self.in_flight = {}
        self.contributions = {}

    def layer(self, i, x):
        if i == 0:
            shards, sems, stacks = self.gather_first
            stacks = wait_sends_to_chips("gather_first_wait", shards, stacks, sems, False, self.s5[2][1]["kd"])
            self.full.update({n: _from_blocks(n, g) for n, g in zip(self.first, stacks)})
        if i == 1:
            shards, sems, stacks = self.gather_later
            stacks = wait_sends_to_chips("gather_later_wait", shards, stacks, sems, False, x)
            self.full.update({n: _from_blocks(n, g) for n, g in zip(self.later, stacks)})
        names = SSM_NAMES if i % 2 == 0 else ATTN_NAMES
        w = {n: self.full.get("l%d_%s" % (i, n), self.weights.get("l%d_%s" % (i, n))) for n in names}
        if i in self.s5:
            w["s5"] = self.s5[i]
        return w

    def chip_sums(self, names, grads, extra_blocks=(), copies=1):
        blocks = [_to_blocks(n, grads[n]) for n in names] + list(extra_blocks)
        from_sibling = exchange_halves_with_sibling(blocks)
        mine = [lax.dynamic_slice_in_dim(b, self.c * (b.shape[1] // 2), b.shape[1] // 2, axis=1) for b in blocks]
        k = len(names)
        sums = add_pairs(mine[:k], from_sibling[:k], [WIRE_DTYPE] * k, copies)
        if extra_blocks:
            sums += add_pairs(mine[k:], from_sibling[k:], [F32] * len(extra_blocks), copies)
        return sums

    def layer_done(self, i, grads, dx):
        if i + 1 in self.in_flight:
            names, sums, sems, landings = self.in_flight.pop(i + 1)
            done = wait_sends_to_chips("scatter_wait_l%d" % (i + 1), sums, landings, sems, True, dx)
            self.contributions.update(zip(names, done))
        if i == 0:
            return None
        names = _layer_big_names(i)
        pairs = self.chip_sums(names, grads, copies=2)
        sums = [p[0] for p in pairs]
        sems, landings, token = start_sends_to_chips("scatter_start_l%d" % i, sums, [p[1] for p in pairs], True, sums[0])
        self.in_flight[i] = (names, sums, sems, landings)
        return token


def _train_step(x, loss_target, weights, moments_m, moments_v):
    hooks = _OverlappedExchanges(weights)
    loss, dx, grads = _sequence_step(x[0], loss_target[0], weights["final_norm"], hooks)
    small_pack = _pack_small({n: grads[n] for n in SMALL_NAMES}, scalar=loss)
    last = _layer_big_names(0)
    pairs = hooks.chip_sums(last, grads, extra_blocks=[small_pack[None]], copies=2)
    sums = [p[0] for p in pairs]
    landings = [p[1] for p in pairs[:-1]] + [jnp.broadcast_to(sums[-1], (4,) + sums[-1].shape[1:])]
    sems, landings, token = start_sends_to_chips("scatter_start_l0", sums, landings, True, sums[0])
    out_grad, out_delta, out_m, out_v = {}, {}, {}, {}

    def finish(names, arrays, token=None):
        shared = swap_halves_with_sibling(sum_fours(arrays, token))
        for n, s in zip(names, shared):
            if n == "small":
                return s.reshape(-1, LANES)
            out_grad[n] = s.reshape(2 * s.shape[1], s.shape[2])
            if _travels_transposed(n, weights[n]):
                out_grad[n] = out_grad[n].T
            out_delta[n], out_m[n], out_v[n] = adamw(weights[n], out_grad[n], moments_m[n], moments_v[n])

    others = [n for n in BIG_NAMES if n not in last]
    finish(others, [hooks.contributions[n] for n in others], token)
    arrived = wait_sends_to_chips("scatter_wait_l0", sums, landings, sems, True, out_v[others[-1]])
    small_grad_pack = finish(last + ["small"], arrived)
    loss = small_grad_pack[-8, 0]
    out_grad.update(_unpack_small(small_grad_pack, {n: weights[n] for n in SMALL_NAMES}))
    cubes = [n for n in SMALL_NAMES if weights[n].ndim == 3]
    for names, slabs in ((cubes, 8), ([n for n in SMALL_NAMES if n not in cubes], None)):
        deltas, new_ms, new_vs = adamw_small(*[[group[n] for n in names] for group in (weights, out_grad, moments_m, moments_v)], slabs=slabs)
        out_delta.update(zip(names, deltas))
        out_m.update(zip(names, new_ms))
        out_v.update(zip(names, new_vs))
    outs = [loss, dx[None]]
    for group in (out_grad, out_delta, out_m, out_v):
        outs.extend(group[n] for n in WEIGHT_NAMES)
    return tuple(outs)


def kernel(x, l0_norm, l0_w_in, l0_a_re, l0_a_im, l0_log_step, l0_b_re, l0_b_im, l0_c_re, l0_c_im, l0_d, l0_w_glu, l0_b_glu, l0_w_out, l1_norm, l1_w_in, l1_sinks, l1_w_out, l2_norm, l2_w_in, l2_a_re, l2_a_im, l2_log_step, l2_b_re, l2_b_im, l2_c_re, l2_c_im, l2_d, l2_w_glu, l2_b_glu, l2_w_out, l3_norm, l3_w_in, l3_sinks, l3_w_out, final_norm, loss_target, m_l0_norm, m_l0_w_in, m_l0_a_re, m_l0_a_im, m_l0_log_step, m_l0_b_re, m_l0_b_im, m_l0_c_re, m_l0_c_im, m_l0_d, m_l0_w_glu, m_l0_b_glu, m_l0_w_out, m_l1_norm, m_l1_w_in, m_l1_sinks, m_l1_w_out, m_l2_norm, m_l2_w_in, m_l2_a_re, m_l2_a_im, m_l2_log_step, m_l2_b_re, m_l2_b_im, m_l2_c_re, m_l2_c_im, m_l2_d, m_l2_w_glu, m_l2_b_glu, m_l2_w_out, m_l3_norm, m_l3_w_in, m_l3_sinks, m_l3_w_out, m_final_norm, v_l0_norm, v_l0_w_in, v_l0_a_re, v_l0_a_im, v_l0_log_step, v_l0_b_re, v_l0_b_im, v_l0_c_re, v_l0_c_im, v_l0_d, v_l0_w_glu, v_l0_b_glu, v_l0_w_out, v_l1_norm, v_l1_w_in, v_l1_sinks, v_l1_w_out, v_l2_norm, v_l2_w_in, v_l2_a_re, v_l2_a_im, v_l2_log_step, v_l2_b_re, v_l2_b_im, v_l2_c_re, v_l2_c_im, v_l2_d, v_l2_w_glu, v_l2_b_glu, v_l2_w_out, v_l3_norm, v_l3_w_in, v_l3_sinks, v_l3_w_out, v_final_norm):
    args = locals()
    weights = {n: args[n] for n in WEIGHT_NAMES}
    moments_m = {n: args["m_" + n] for n in WEIGHT_NAMES}
    moments_v = {n: args["v_" + n] for n in WEIGHT_NAMES}
    return _train_step(x, loss_target, weights, moments_m, moments_v)
```

```python
import functools
import math

import jax
import jax.numpy as jnp
from jax import lax
from jax.experimental import pallas as pl
from jax.experimental.pallas import tpu as pltpu

F32 = jnp.float32
MXU_DTYPE = jnp.bfloat16
WIRE_DTYPE = jnp.bfloat16
MESH = pl.DeviceIdType.MESH

D_MODEL = 1024
BRANCH = 1024
NORM_EPS = 1e-5
SSM_GROUPS = 64
SSM_GROUP = 16
SSM_STATE = 64
S5_CHUNK = 16
LANES = 128
S5_OCT = LANES // SSM_GROUP
S5_OCTETS = SSM_GROUPS // S5_OCT
S5_OCT_IN = S5_CHUNK * LANES
S5_OCT_STATE = S5_OCT * SSM_STATE
S5_STATES = SSM_GROUPS * SSM_STATE
HEAD_DIM = 64
N_Q_HEADS = 16
N_KV_HEADS = 2
GQA_GROUP = N_Q_HEADS // N_KV_HEADS
ATTN_BLOCK = 128
Q_DIM = N_Q_HEADS * HEAD_DIM
KV_DIM = N_KV_HEADS * HEAD_DIM
ROPE_THETA = 10000.0
NEG_INF = -1e30
ADAM_LR = 0.001
ADAM_B1 = 0.9
ADAM_B2 = 0.999
ADAM_EPS = 1e-08
ADAM_WD = 0.01
ADAM_STEP = 10

VMEM_LIMIT_V7X = 56 * 1024 * 1024
ROW_TILE_FWD = 512
ROW_TILE_BWD = 512

SSM_NAMES = ("norm", "w_in", "a_re", "a_im", "log_step", "b_re", "b_im", "c_re", "c_im", "d", "w_glu", "b_glu", "w_out")
ATTN_NAMES = ("norm", "w_in", "sinks", "w_out")


def _weight_names():
    names = []
    for i in range(4):
        for n in (SSM_NAMES if i % 2 == 0 else ATTN_NAMES):
            names.append("l%d_%s" % (i, n))
    names.append("final_norm")
    return names


WEIGHT_NAMES = _weight_names()
BIG_NAMES = [n for n in WEIGHT_NAMES if n.endswith(("w_in", "w_glu", "w_out"))]
SMALL_NAMES = [n for n in WEIGHT_NAMES if n not in BIG_NAMES]


def _params(semantics=None):
    return pltpu.CompilerParams(dimension_semantics=semantics, vmem_limit_bytes=VMEM_LIMIT_V7X)


def _rows(tm, n):
    return pl.BlockSpec((tm, n), lambda i: (i, 0))


def _whole(shape):
    return pl.BlockSpec(shape, lambda i: (0,) * len(shape), pipeline_mode=pl.Buffered(1))


def _sds(shape, dtype=F32):
    return jax.ShapeDtypeStruct(shape, dtype)


def _mm(a, b):
    return jnp.dot(a.astype(MXU_DTYPE), b.astype(MXU_DTYPE), preferred_element_type=F32)


def _mm_tn(a, b):
    return lax.dot_general(a.astype(MXU_DTYPE), b.astype(MXU_DTYPE), (((0,), (0,)), ((), ())), preferred_element_type=F32)


def _mm_nt(a, b):
    return lax.dot_general(a.astype(MXU_DTYPE), b.astype(MXU_DTYPE), (((1,), (1,)), ((), ())), preferred_element_type=F32)


def _sigmoid(x):
    return 0.5 + 0.5 * jnp.tanh(0.5 * x)


def _silu(x):
    return x * _sigmoid(x)


def _silu_and_grad(x):
    s = _sigmoid(x)
    return x * s, s * (1.0 + x * (1.0 - s))


GELU_C0 = math.sqrt(2.0 / math.pi)
GELU_C1 = 0.044715


def _gelu(x):
    return 0.5 * x * (1.0 + jnp.tanh(GELU_C0 * (x + GELU_C1 * x * x * x)))


def _gelu_and_grad(x):
    x2 = x * x
    th = jnp.tanh(GELU_C0 * x * (1.0 + GELU_C1 * x2))
    half = 0.5 + 0.5 * th
    return x * half, half + 0.5 * x * (1.0 - th * th) * (GELU_C0 + 3.0 * GELU_C0 * GELU_C1 * x2)


def _rms(x, g):
    r = lax.rsqrt(jnp.mean(x * x, axis=-1, keepdims=True) + NORM_EPS)
    xhat = x * r
    return r, xhat, xhat * g


def _rms_bwd(dh, g, r, xhat):
    dxhat = dh * g
    dx = r * (dxhat - xhat * jnp.mean(dxhat * xhat, axis=-1, keepdims=True))
    return dx, jnp.sum(dh * xhat, axis=0, keepdims=True)


def _swap_half_heads(x):
    n = x.shape[-1]
    lane = lax.broadcasted_iota(jnp.int32, x.shape, x.ndim - 1)
    first = (lane % HEAD_DIM) < (HEAD_DIM // 2)
    return jnp.where(first, pltpu.roll(x, n - HEAD_DIM // 2, x.ndim - 1), pltpu.roll(x, HEAD_DIM // 2, x.ndim - 1))


def _tile_lanes(t, reps):
    return jnp.concatenate([t] * reps, axis=1)


TOKEN_SHAPE = (8, LANES)


def _after(token):
    return ([], []) if token is None else ([token], [_whole(TOKEN_SHAPE)])


def ssm_proj_fwd(x, norm, w_in):
    t = x.shape[0]
    tm = min(ROW_TILE_FWD, t)

    def body(x_ref, g_ref, w_ref, u_ref, gate_ref):
        _, _, h = _rms(x_ref[...], g_ref[...])
        h = h.astype(MXU_DTYPE)
        half = BRANCH // 2
        for j in range(2):
            u_ref[:, j * half:(j + 1) * half] = _mm(h, w_ref[j])
            gate_ref[:, j * half:(j + 1) * half] = _mm(h, w_ref[2 + j])

    return pl.pallas_call(
        body, name="ssm_proj_fwd", grid=(t // tm,),
        in_specs=[_rows(tm, D_MODEL), _whole((1, D_MODEL)), _whole((4, D_MODEL, BRANCH // 2))],
        out_specs=[_rows(tm, BRANCH), _rows(tm, BRANCH)],
        out_shape=[_sds((t, BRANCH)), _sds((t, BRANCH))],
        compiler_params=_params(("parallel",)),
    )(x, norm, w_in)


def _chunk_rows(ref, nk, dtype=None):
    rows = jnp.concatenate([ref[pl.ds(s, nk, stride=S5_CHUNK), :] for s in range(S5_CHUNK)], axis=1)
    return rows.astype(MXU_DTYPE if dtype is None else dtype)


def _store_chunk_rows(ref, val, nk):
    for s in range(S5_CHUNK):
        ref[pl.ds(s, nk, stride=S5_CHUNK), :] = val[:, s * LANES:(s + 1) * LANES]


def _own_group_mask():
    row = lax.broadcasted_iota(jnp.int32, (S5_OCT_IN, S5_OCT_STATE), 0)
    col = lax.broadcasted_iota(jnp.int32, (S5_OCT_IN, S5_OCT_STATE), 1)
    return ((row % LANES) // SSM_GROUP) == (col // SSM_STATE)


def _spread_groups(w):
    return jnp.where(_own_group_mask(), jnp.concatenate([w] * (S5_OCT_STATE // LANES), axis=1), 0.0).astype(MXU_DTYPE)


def _fold_groups(p):
    p = jnp.where(_own_group_mask(), p, 0.0)
    return sum(p[:, q * LANES:(q + 1) * LANES] for q in range(S5_OCT_STATE // LANES))


def _fill_toeplitz(win_ref, kd_ref):
    win_ref[...] = jnp.zeros_like(win_ref)
    for s in range(S5_CHUNK):
        for t in range(s, S5_CHUNK):
            win_ref[s * LANES:(s + 1) * LANES, t * LANES:(t + 1) * LANES] = kd_ref[t - s].astype(MXU_DTYPE)


TOEPLITZ_BLOCK = 512
_TOEPLITZ_BLOCKS = [(lo, lo + TOEPLITZ_BLOCK) for lo in range(0, S5_OCT_IN, TOEPLITZ_BLOCK)]


def _strip(t):
    return pl.BlockSpec((t, LANES), lambda b: (0, b))


def _oct_states(nk):
    return pl.BlockSpec((nk, S5_OCT_STATE), lambda b: (0, b))


OCT_W = pl.BlockSpec((None, S5_OCT_IN, LANES), lambda b: (b, 0, 0))
OCT_KD = pl.BlockSpec((None, S5_CHUNK, LANES, LANES), lambda b: (b, 0, 0, 0))


def s5_chunk_states(u, ws_re, ws_im):
    t = u.shape[0]
    nk = t // S5_CHUNK

    def body(u_ref, wr_ref, wi_ref, re_ref, im_ref):
        uc = _chunk_rows(u_ref, nk)
        re_ref[...] = _mm(uc, _spread_groups(wr_ref[...]))
        im_ref[...] = _mm(uc, _spread_groups(wi_ref[...]))

    return pl.pallas_call(
        body, name="s5_chunk_states", grid=(S5_OCTETS,),
        in_specs=[_strip(t), OCT_W, OCT_W], out_specs=[_oct_states(nk), _oct_states(nk)],
        out_shape=[_sds((nk, S5_STATES)), _sds((nk, S5_STATES))],
        compiler_params=_params(("parallel",)),
    )(u, ws_re, ws_im)


def s5_scan_fwd(s_re, s_im, a_re, a_im):
    nk = s_re.shape[0]

    def body(sre_ref, sim_ref, ar_ref, ai_ref, hre_ref, him_ref):
        ar = ar_ref[...]
        ai = ai_ref[...]

        def step(k, carry):
            hr, hi = carry
            hre_ref[pl.ds(k, 1), :] = hr
            him_ref[pl.ds(k, 1), :] = hi
            sr = sre_ref[pl.ds(k, 1), :]
            si = sim_ref[pl.ds(k, 1), :]
            return ar * hr - ai * hi + sr, ai * hr + ar * hi + si

        zero = jnp.zeros((1, S5_STATES), F32)
        lax.fori_loop(0, nk, step, (zero, zero))

    vm = pl.BlockSpec(memory_space=pltpu.VMEM)
    return pl.pallas_call(
        body, name="s5_scan_fwd", in_specs=[vm, vm, vm, vm], out_specs=[vm, vm],
        out_shape=[_sds((nk, S5_STATES)), _sds((nk, S5_STATES))],
        compiler_params=_params(),
    )(s_re, s_im, a_re, a_im)


def s5_outputs(u, h_re, h_im, kd, wo_re, wo_im):
    t = u.shape[0]
    nk = t // S5_CHUNK

    def body(u_ref, hre_ref, him_ref, kd_ref, wor_ref, woi_ref, y_ref, win_ref):
        _fill_toeplitz(win_ref, kd_ref)
        uc = _chunk_rows(u_ref, nk)
        y = jnp.concatenate([_mm(uc[:, :hi], win_ref[:hi, lo:hi]) for lo, hi in _TOEPLITZ_BLOCKS], axis=1)
        y = y + _mm_nt(hre_ref[...], _spread_groups(wor_ref[...])) + _mm_nt(him_ref[...], _spread_groups(woi_ref[...]))
        _store_chunk_rows(y_ref, y, nk)

    return pl.pallas_call(
        body, name="s5_outputs", grid=(S5_OCTETS,),
        in_specs=[_strip(t), _oct_states(nk), _oct_states(nk), OCT_KD, OCT_W, OCT_W],
        out_specs=_strip(t), out_shape=_sds((t, BRANCH)),
        scratch_shapes=[pltpu.VMEM((S5_OCT_IN, S5_OCT_IN), MXU_DTYPE)],
        compiler_params=_params(("parallel",)),
    )(u, h_re, h_im, kd, wo_re, wo_im)


def s5_state_grads(dy, wo_re, wo_im):
    t = dy.shape[0]
    nk = t // S5_CHUNK

    def body(dy_ref, wor_ref, woi_ref, re_ref, im_ref):
        dyc = _chunk_rows(dy_ref, nk)
        re_ref[...] = _mm(dyc, _spread_groups(wor_ref[...]))
        im_ref[...] = _mm(dyc, _spread_groups(woi_ref[...]))

    return pl.pallas_call(
        body, name="s5_state_grads", grid=(S5_OCTETS,),
        in_specs=[_strip(t), OCT_W, OCT_W], out_specs=[_oct_states(nk), _oct_states(nk)],
        out_shape=[_sds((nk, S5_STATES)), _sds((nk, S5_STATES))],
        compiler_params=_params(("parallel",)),
    )(dy, wo_re, wo_im)


def s5_scan_bwd(dh_re, dh_im, h_re, h_im, a_re, a_im):
    nk = dh_re.shape[0]

    def body(dhr_ref, dhi_ref, hr_ref, hi_ref, ar_ref, ai_ref, dsr_ref, dsi_ref, dar_ref, dai_ref):
        ar = ar_ref[...]
        ai = ai_ref[...]

        dar_ref[...] = jnp.zeros_like(dar_ref)
        dai_ref[...] = jnp.zeros_like(dai_ref)

        def step(i, carry):
            gr, gi = carry
            k = nk - 1 - i
            dhr = dhr_ref[pl.ds(k, 1), :]
            dhi = dhi_ref[pl.ds(k, 1), :]
            dsr_ref[pl.ds(k, 1), :] = gr
            dsi_ref[pl.ds(k, 1), :] = gi
            hr = hr_ref[pl.ds(k, 1), :]
            hi = hi_ref[pl.ds(k, 1), :]
            dar_ref[...] += gr * hr + gi * hi
            dai_ref[...] += gi * hr - gr * hi
            return dhr + ar * gr + ai * gi, dhi - ai * gr + ar * gi

        zero = jnp.zeros((1, S5_STATES), F32)
        lax.fori_loop(0, nk, step, (zero, zero))

    vm = pl.BlockSpec(memory_space=pltpu.VMEM)
    return pl.pallas_call(
        body, name="s5_scan_bwd", in_specs=[vm] * 6, out_specs=[vm] * 4,
        out_shape=[_sds((nk, S5_STATES)), _sds((nk, S5_STATES)), _sds((1, S5_STATES)), _sds((1, S5_STATES))],
        input_output_aliases={0: 0, 1: 1}, compiler_params=_params(),
    )(dh_re, dh_im, h_re, h_im, a_re, a_im)


def s5_input_grads(dy, ds_re, ds_im, kd, ws_re, ws_im):
    t = dy.shape[0]
    nk = t // S5_CHUNK

    def body(dy_ref, dsr_ref, dsi_ref, kd_ref, wsr_ref, wsi_ref, du_ref, win_ref):
        _fill_toeplitz(win_ref, kd_ref)
        dyc = _chunk_rows(dy_ref, nk)
        du = jnp.concatenate([_mm_nt(dyc[:, lo:], win_ref[lo:hi, lo:]) for lo, hi in _TOEPLITZ_BLOCKS], axis=1)
        du = du + _mm_nt(dsr_ref[...], _spread_groups(wsr_ref[...])) + _mm_nt(dsi_ref[...], _spread_groups(wsi_ref[...]))
        _store_chunk_rows(du_ref, du, nk)

    return pl.pallas_call(
        body, name="s5_input_grads", grid=(S5_OCTETS,),
        in_specs=[_strip(t), _oct_states(nk), _oct_states(nk), OCT_KD, OCT_W, OCT_W],
        out_specs=_strip(t), out_shape=_sds((t, BRANCH)),
        scratch_shapes=[pltpu.VMEM((S5_OCT_IN, S5_OCT_IN), MXU_DTYPE)],
        compiler_params=_params(("parallel",)),
    )(dy, ds_re, ds_im, kd, ws_re, ws_im)


def s5_weight_grads(u, dy, h_re, h_im, ds_re, ds_im):
    t = u.shape[0]
    nk = t // S5_CHUNK

    def body(u_ref, dy_ref, hre_ref, him_ref, dsr_ref, dsi_ref, dkd_ref, dwsr_ref, dwsi_ref, dwor_ref, dwoi_ref):
        dyc = _chunk_rows(dy_ref, nk, F32)
        uct = _chunk_rows(u_ref, nk, F32).T.astype(MXU_DTYPE)
        dyct = dyc.T.astype(MXU_DTYPE)
        dyc = dyc.astype(MXU_DTYPE)
        dwsr_ref[...] = _fold_groups(_mm(uct, dsr_ref[...]))
        dwsi_ref[...] = _fold_groups(_mm(uct, dsi_ref[...]))
        dwor_ref[...] = _fold_groups(_mm(dyct, hre_ref[...]))
        dwoi_ref[...] = _fold_groups(_mm(dyct, him_ref[...]))
        dkd_ref[...] = jnp.zeros_like(dkd_ref)
        for tt in range(0, S5_CHUNK, 2):
            p = _mm(uct[:(tt + 2) * LANES], dyc[:, tt * LANES:(tt + 2) * LANES])
            for s in range(tt + 2):
                rows = p[s * LANES:(s + 1) * LANES]
                if s <= tt:
                    dkd_ref[tt - s] += rows[:, :LANES]
                dkd_ref[tt + 1 - s] += rows[:, LANES:]

    return pl.pallas_call(
        body, name="s5_weight_grads", grid=(S5_OCTETS,),
        in_specs=[_strip(t), _strip(t)] + [_oct_states(nk)] * 4,
        out_specs=[OCT_KD, OCT_W, OCT_W, OCT_W, OCT_W],
        out_shape=[_sds((S5_OCTETS, S5_CHUNK, LANES, LANES))] + [_sds((S5_OCTETS, S5_OCT_IN, LANES))] * 4,
        compiler_params=_params(("parallel",)),
    )(u, dy, h_re, h_im, ds_re, ds_im)


def ssm_mix_fwd(x, u, gate, y_scan, d, w_glu, b_glu, w_out):
    t = x.shape[0]
    tm = min(ROW_TILE_FWD, t)

    def body(x_ref, u_ref, gate_ref, ys_ref, d_ref, wg_ref, bg_ref, wo_ref, y_ref, g2_ref, xo_ref):
        y = ys_ref[...] + d_ref[...] * u_ref[...]
        z0 = _gelu(y)
        g2 = _mm(z0, wg_ref[...]) + bg_ref[...]
        a = z0 * _sigmoid(g2) * _silu(gate_ref[...])
        y_ref[...] = y
        g2_ref[...] = g2
        xo_ref[...] = x_ref[...] + _mm(a, wo_ref[...])

    row = _rows(tm, BRANCH)
    vec = _whole((1, BRANCH))
    mat = _whole((BRANCH, BRANCH))
    return pl.pallas_call(
        body, name="ssm_mix_fwd", grid=(t // tm,),
        in_specs=[row, row, row, row, vec, mat, vec, mat],
        out_specs=[row, row, row],
        out_shape=[_sds((t, BRANCH))] * 3,
        compiler_params=_params(("parallel",)),
    )(x, u, gate, y_scan, d, w_glu, b_glu, w_out)


def ssm_mix_bwd(dxo, u, gate, y, g2, w_glu, w_out, token=None):
    t = dxo.shape[0]
    tm = min(ROW_TILE_BWD, t)
    extra, extra_specs = _after(token)

    def body(dxo_ref, u_ref, gate_ref, y_ref, g2_ref, wgt_ref, wot_ref, *rest):
        dy_ref, dgate_ref, dwo_ref, dwg_ref, dbg_ref, dd_ref = rest[-6:]

        @pl.when(pl.program_id(0) == 0)
        def _():
            dwo_ref[...] = jnp.zeros_like(dwo_ref)
            dwg_ref[...] = jnp.zeros_like(dwg_ref)
            dbg_ref[...] = jnp.zeros_like(dbg_ref)
            dd_ref[...] = jnp.zeros_like(dd_ref)

        dxo = dxo_ref[...]
        gate = gate_ref[...]
        y = y_ref[...]
        z0, z0_grad = _gelu_and_grad(y)
        sg = _sigmoid(g2_ref[...])
        z = z0 * sg
        sgate, sgate_grad = _silu_and_grad(gate)
        da = _mm_nt(dxo, wot_ref[...])
        dwo_ref[...] += _mm_tn(z * sgate, dxo)
        dz = da * sgate
        dgate_ref[...] = da * z * sgate_grad
        dg2 = dz * z0 * sg * (1.0 - sg)
        dbg_ref[...] += jnp.sum(dg2, axis=0, keepdims=True)
        dwg_ref[...] += _mm_tn(z0, dg2)
        dz0 = dz * sg + _mm_nt(dg2, wgt_ref[...])
        dy = dz0 * z0_grad
        dd_ref[...] += jnp.sum(dy * u_ref[...], axis=0, keepdims=True)
        dy_ref[...] = dy

    row = _rows(tm, BRANCH)
    vec = _whole((1, BRANCH))
    mat = _whole((BRANCH, BRANCH))
    return pl.pallas_call(
        body, name="ssm_mix_bwd", grid=(t // tm,),
        in_specs=[row, row, row, row, row, mat, mat] + extra_specs,
        out_specs=[row, row, mat, mat, vec, vec],
        out_shape=[_sds((t, BRANCH)), _sds((t, BRANCH)), _sds((BRANCH, D_MODEL)), _sds((BRANCH, BRANCH)),
                   _sds((1, BRANCH)), _sds((1, BRANCH))],
        compiler_params=_params(("arbitrary",)),
    )(dxo, u, gate, y, g2, w_glu, w_out, *extra)


def ssm_proj_bwd(x, norm, dxo, dy, du_scan, dgate, d, w_in):
    t = x.shape[0]
    tm = min(ROW_TILE_BWD, t)
    n = 2 * BRANCH

    def body(x_ref, g_ref, dxo_ref, dy_ref, dus_ref, dgate_ref, d_ref, wt_ref, dx_ref, dw_ref, dg_ref):
        @pl.when(pl.program_id(0) == 0)
        def _():
            dw_ref[...] = jnp.zeros_like(dw_ref)
            dg_ref[...] = jnp.zeros_like(dg_ref)

        g = g_ref[...]
        r, xhat, h = _rms(x_ref[...], g)
        h = h.astype(MXU_DTYPE)
        du = dus_ref[...] + d_ref[...] * dy_ref[...]
        dproj = jnp.concatenate([du, dgate_ref[...]], axis=1).astype(MXU_DTYPE)
        dh = jnp.zeros((tm, D_MODEL), F32)
        for j in range(4):
            cols = dproj[:, j * (n // 4):(j + 1) * (n // 4)]
            dh = dh + _mm_nt(cols, wt_ref[j])
            dw_ref[j] += _mm_tn(h, cols)
        dx, dg = _rms_bwd(dh, g, r, xhat)
        dg_ref[...] += dg
        dx_ref[...] = dxo_ref[...] + dx

    row = _rows(tm, D_MODEL)
    vec = _whole((1, D_MODEL))
    blocks = _whole((4, D_MODEL, n // 4))
    return pl.pallas_call(
        body, name="ssm_proj_bwd", grid=(t // tm,),
        in_specs=[row, vec, row, row, row, row, vec, blocks],
        out_specs=[row, blocks, vec],
        out_shape=[_sds((t, D_MODEL)), _sds((4, D_MODEL, n // 4)), _sds((1, D_MODEL))],
        compiler_params=_params(("arbitrary",)),
    )(x, norm, dxo, dy, du_scan, dgate, d, w_in)


ATTN_N = Q_DIM + 2 * KV_DIM + BRANCH


def attn_proj_fwd(x, norm, w_in_t, cos2, sin2):
    t = x.shape[0]
    tm = min(ROW_TILE_FWD, t)

    def body(x_ref, g_ref, w_ref, cos_ref, sin_ref, q_ref, k_ref, v_ref, gate_ref):
        _, _, h = _rms(x_ref[...], g_ref[...])
        p = _mm_nt(h, w_ref[...])
        cs = cos_ref[...]
        sn = sin_ref[...]
        q = p[:, :Q_DIM]
        k = p[:, Q_DIM:Q_DIM + KV_DIM]
        q_ref[...] = q * _tile_lanes(cs, Q_DIM // LANES) + _swap_half_heads(q) * _tile_lanes(sn, Q_DIM // LANES)
        k_ref[...] = k * cs + _swap_half_heads(k) * sn
        v_ref[...] = p[:, Q_DIM + KV_DIM:Q_DIM + 2 * KV_DIM]
        gate_ref[...] = p[:, Q_DIM + 2 * KV_DIM:]

    return pl.pallas_call(
        body, name="attn_proj_fwd", grid=(t // tm,),
        in_specs=[_rows(tm, D_MODEL), _whole((1, D_MODEL)), _whole((ATTN_N, D_MODEL)), _rows(tm, LANES), _rows(tm, LANES)],
        out_specs=[_rows(tm, Q_DIM), _rows(tm, KV_DIM), _rows(tm, KV_DIM), _rows(tm, BRANCH)],
        out_shape=[_sds((t, Q_DIM)), _sds((t, KV_DIM)), _sds((t, KV_DIM)), _sds((t, BRANCH))],
        compiler_params=_params(("parallel",)),
    )(x, norm, w_in_t, cos2, sin2)


GQA_LANES = GQA_GROUP * ATTN_BLOCK


def _window_masks(first_block):
    kj = lax.broadcasted_iota(jnp.int32, (ATTN_BLOCK, GQA_LANES), 0)
    qi = lax.broadcasted_iota(jnp.int32, (ATTN_BLOCK, GQA_LANES), 1) % ATTN_BLOCK
    return kj > qi, kj > jnp.where(first_block, qi, ATTN_BLOCK)


def _fold(upper, both):
    return jnp.where(upper, both[:ATTN_BLOCK], both[ATTN_BLOCK:])


def _unfold(upper, tile):
    return jnp.concatenate([jnp.where(upper, tile, 0.0), jnp.where(upper, 0.0, tile)], axis=0).astype(MXU_DTYPE)


def _stack_heads(ref, group):
    return jnp.concatenate([ref[:, h * HEAD_DIM:(h + 1) * HEAD_DIM] for h in range(group * GQA_GROUP, (group + 1) * GQA_GROUP)], axis=0)


def _unstack_heads(ref, group, stacked):
    for n in range(GQA_GROUP):
        h = group * GQA_GROUP + n
        ref[:, h * HEAD_DIM:(h + 1) * HEAD_DIM] = stacked[n * ATTN_BLOCK:(n + 1) * ATTN_BLOCK]


def _sink_row(sink_ref, group):
    return jnp.concatenate([jnp.full((1, ATTN_BLOCK), sink_ref[group * GQA_GROUP + n], F32) for n in range(GQA_GROUP)], axis=1)


def _lane_is(h):
    return lax.broadcasted_iota(jnp.int32, (1, LANES), 1) == h


def attn_fwd(q, k, v, sinks):
    t = q.shape[0]
    nb = t // ATTN_BLOCK
    scale = HEAD_DIM ** -0.5

    def body(sink_ref, q_ref, kc_ref, kp_ref, vc_ref, vp_ref, o_ref, lse_ref):
        keys = jnp.concatenate([kp_ref[...], kc_ref[...]], axis=0).astype(MXU_DTYPE)
        vals = jnp.concatenate([vp_ref[...], vc_ref[...]], axis=0).astype(MXU_DTYPE)
        upper, dead = _window_masks(pl.program_id(0) == 0)
        for g in range(N_KV_HEADS):
            kv = slice(g * HEAD_DIM, (g + 1) * HEAD_DIM)
            qs = _stack_heads(q_ref, g) * scale
            s = jnp.where(dead, NEG_INF, _fold(upper, _mm_nt(keys[:, kv], qs)))
            sink = _sink_row(sink_ref, g)
            m = jnp.maximum(jnp.max(s, axis=0, keepdims=True), sink)
            p = jnp.exp(s - m)
            den = jnp.sum(p, axis=0, keepdims=True) + jnp.exp(sink - m)
            _unstack_heads(o_ref, g, _mm_tn(_unfold(upper, p * (1.0 / den)), vals[:, kv]))
            lse = m + jnp.log(den)
            for n in range(GQA_GROUP):
                lse_ref[pl.ds(g * GQA_GROUP + n, 1), :] = lse[:, n * ATTN_BLOCK:(n + 1) * ATTN_BLOCK]

    cur = lambda n: pl.BlockSpec((ATTN_BLOCK, n), lambda i: (i, 0))
    prev = lambda n: pl.BlockSpec((ATTN_BLOCK, n), lambda i: (jnp.maximum(i - 1, 0), 0))
    return pl.pallas_call(
        body, name="attn_fwd", grid=(nb,),
        in_specs=[pl.BlockSpec(memory_space=pltpu.SMEM), cur(Q_DIM), cur(KV_DIM), prev(KV_DIM), cur(KV_DIM), prev(KV_DIM)],
        out_specs=[cur(Q_DIM), pl.BlockSpec((N_Q_HEADS, ATTN_BLOCK), lambda i: (0, i))],
        out_shape=[_sds((t, Q_DIM)), _sds((N_Q_HEADS, t))],
        compiler_params=_params(("parallel",)),
    )(sinks, q, k, k, v, v)


def attn_bwd(q, k, v, sinks, o, lse, do):
    t = q.shape[0]
    nb = t // ATTN_BLOCK
    scale = HEAD_DIM ** -0.5

    def body(sink_ref, q_ref, kc_ref, kp_ref, vc_ref, vp_ref, o_ref, lse_ref, do_ref,
             dq_ref, dk_ref, dv_ref, dsink_ref, dk_carry, dv_carry):
        i = pl.program_id(0)

        @pl.when(i == 0)
        def _():
            dsink_ref[...] = jnp.zeros_like(dsink_ref)
            dk_carry[...] = jnp.zeros_like(dk_carry)
            dv_carry[...] = jnp.zeros_like(dv_carry)

        @pl.when(i < nb)
        def _():
            keys = jnp.concatenate([kp_ref[...], kc_ref[...]], axis=0).astype(MXU_DTYPE)
            vals = jnp.concatenate([vp_ref[...], vc_ref[...]], axis=0).astype(MXU_DTYPE)
            upper, dead = _window_masks(i == 0)
            dsink = jnp.zeros((1, LANES), F32)
            dk_heads = []
            dv_heads = []
            for g in range(N_KV_HEADS):
                kv = slice(g * HEAD_DIM, (g + 1) * HEAD_DIM)
                qs = (_stack_heads(q_ref, g) * scale).astype(MXU_DTYPE)
                dos = _stack_heads(do_ref, g)
                lse = jnp.concatenate([lse_ref[pl.ds(g * GQA_GROUP + n, 1), :] for n in range(GQA_GROUP)], axis=1)
                s = jnp.where(dead, NEG_INF, _fold(upper, _mm_nt(keys[:, kv], qs)))
                p = jnp.exp(s - lse)
                delta = _mm_f32(jnp.ones((8, HEAD_DIM), F32), dos * _stack_heads(o_ref, g), ((1,), (1,)))[:1]
                dos = dos.astype(MXU_DTYPE)
                ds = _unfold(upper, p * (_fold(upper, _mm_nt(vals[:, kv], dos)) - delta))
                _unstack_heads(dq_ref, g, _mm_tn(ds, keys[:, kv]) * scale)
                dk_heads.append(_mm(ds, qs))
                dv_heads.append(_mm(_unfold(upper, p), dos))
                at_sink = jnp.exp(_sink_row(sink_ref, g) - lse) * delta
                for n in range(GQA_GROUP):
                    dsink = dsink + jnp.where(_lane_is(g * GQA_GROUP + n), -jnp.sum(at_sink[:, n * ATTN_BLOCK:(n + 1) * ATTN_BLOCK]), 0.0)
            dkk = jnp.concatenate(dk_heads, axis=1)
            dvv = jnp.concatenate(dv_heads, axis=1)
            dsink_ref[...] += dsink
            dk_ref[...] = dk_carry[...] + dkk[:ATTN_BLOCK]
            dv_ref[...] = dv_carry[...] + dvv[:ATTN_BLOCK]
            dk_carry[...] = dkk[ATTN_BLOCK:]
            dv_carry[...] = dvv[ATTN_BLOCK:]

        @pl.when(i == nb)
        def _():
            dk_ref[...] = dk_carry[...]
            dv_ref[...] = dv_carry[...]

    last = nb - 1
    cur = lambda n: pl.BlockSpec((ATTN_BLOCK, n), lambda i: (jnp.minimum(i, last), 0))
    prev = lambda n: pl.BlockSpec((ATTN_BLOCK, n), lambda i: (jnp.clip(i - 1, 0, last), 0))
    late = lambda n: pl.BlockSpec((ATTN_BLOCK, n), lambda i: (i, 0))
    dq, dk_late, dv_late, dsinks = pl.pallas_call(
        body, name="attn_bwd", grid=(nb + 1,),
        in_specs=[pl.BlockSpec(memory_space=pltpu.SMEM), cur(Q_DIM), cur(KV_DIM), prev(KV_DIM), cur(KV_DIM), prev(KV_DIM),
                  cur(Q_DIM), pl.BlockSpec((N_Q_HEADS, ATTN_BLOCK), lambda i: (0, jnp.minimum(i, last))), cur(Q_DIM)],
        out_specs=[cur(Q_DIM), late(KV_DIM), late(KV_DIM), _whole((1, LANES))],
        out_shape=[_sds((t, Q_DIM)), _sds((t + ATTN_BLOCK, KV_DIM)), _sds((t + ATTN_BLOCK, KV_DIM)), _sds((1, LANES))],
        scratch_shapes=[pltpu.VMEM((ATTN_BLOCK, KV_DIM), F32), pltpu.VMEM((ATTN_BLOCK, KV_DIM), F32)],
        compiler_params=_params(("arbitrary",)),
    )(sinks, q, k, k, v, v, o, lse, do)
    return dq, dk_late[ATTN_BLOCK:], dv_late[ATTN_BLOCK:], dsinks


def attn_out_fwd(x, o, gate, w_out):
    t = x.shape[0]
    tm = min(ROW_TILE_FWD, t)

    def body(x_ref, o_ref, gate_ref, w_ref, xo_ref):
        xo_ref[...] = x_ref[...] + _mm(o_ref[...] * _silu(gate_ref[...]), w_ref[...])

    row = _rows(tm, D_MODEL)
    return pl.pallas_call(
        body, name="attn_out_fwd", grid=(t // tm,),
        in_specs=[row, row, row, _whole((Q_DIM, D_MODEL))], out_specs=row, out_shape=_sds((t, D_MODEL)),
        compiler_params=_params(("parallel",)),
    )(x, o, gate, w_out)


def attn_out_bwd(dxo, o, gate, w_out, token=None):
    t = dxo.shape[0]
    tm = min(ROW_TILE_BWD, t)
    extra, extra_specs = _after(token)

    def body(dxo_ref, o_ref, gate_ref, wt_ref, *rest):
        do_ref, dgate_ref, dw_ref = rest[-3:]

        @pl.when(pl.program_id(0) == 0)
        def _():
            dw_ref[...] = jnp.zeros_like(dw_ref)

        dxo = dxo_ref[...]
        o = o_ref[...]
        gate = gate_ref[...]
        sgate, sgate_grad = _silu_and_grad(gate)
        da = _mm_nt(dxo, wt_ref[...])
        dw_ref[...] += _mm_tn(o * sgate, dxo)
        do_ref[...] = da * sgate
        dgate_ref[...] = da * o * sgate_grad

    row = _rows(tm, D_MODEL)
    mat = _whole((Q_DIM, D_MODEL))
    return pl.pallas_call(
        body, name="attn_out_bwd", grid=(t // tm,),
        in_specs=[row, row, row, mat] + extra_specs, out_specs=[row, row, mat],
        out_shape=[_sds((t, Q_DIM)), _sds((t, BRANCH)), _sds((Q_DIM, D_MODEL))],
        compiler_params=_params(("arbitrary",)),
    )(dxo, o, gate, w_out, *extra)


def attn_proj_bwd(x, norm, dxo, dq, dk, dv, dgate, cos2, sin2, w_in_t):
    t = x.shape[0]
    tm = min(ROW_TILE_BWD, t)

    def body(x_ref, g_ref, dxo_ref, dq_ref, dk_ref, dv_ref, dgate_ref, cos_ref, sin_ref, wt_ref, dx_ref, dw_ref, dg_ref):
        @pl.when(pl.program_id(0) == 0)
        def _():
            dw_ref[...] = jnp.zeros_like(dw_ref)
            dg_ref[...] = jnp.zeros_like(dg_ref)

        g = g_ref[...]
        r, xhat, h = _rms(x_ref[...], g)
        cs = cos_ref[...]
        sn = sin_ref[...]
        dqr = dq_ref[...]
        dkr = dk_ref[...]
        dq = dqr * _tile_lanes(cs, Q_DIM // LANES) + _swap_half_heads(dqr * _tile_lanes(sn, Q_DIM // LANES))
        dk = dkr * cs + _swap_half_heads(dkr * sn)
        dproj = jnp.concatenate([dq, dk, dv_ref[...], dgate_ref[...]], axis=1)
        dh = _mm(dproj, wt_ref[...])
        dw_ref[...] += _mm_tn(dproj, h)
        dx, dg = _rms_bwd(dh, g, r, xhat)
        dg_ref[...] += dg
        dx_ref[...] = dxo_ref[...] + dx

    row = _rows(tm, D_MODEL)
    vec = _whole((1, D_MODEL))
    return pl.pallas_call(
        body, name="attn_proj_bwd", grid=(t // tm,),
        in_specs=[row, vec, row, _rows(tm, Q_DIM), _rows(tm, KV_DIM), _rows(tm, KV_DIM), _rows(tm, BRANCH),
                  _rows(tm, LANES), _rows(tm, LANES), _whole((ATTN_N, D_MODEL))],
        out_specs=[row, _whole((ATTN_N, D_MODEL)), vec],
        out_shape=[_sds((t, D_MODEL)), _sds((ATTN_N, D_MODEL)), _sds((1, D_MODEL))],
        compiler_params=_params(("arbitrary",)),
    )(x, norm, dxo, dq, dk, dv, dgate, cos2, sin2, w_in_t)


def attn_out_loss(x, o, gate, w_out, norm, target):
    t = x.shape[0]
    tm = min(ROW_TILE_FWD, t)

    def body(x_ref, o_ref, gate_ref, w_ref, g_ref, tgt_ref, loss_ref, dx_ref, dg_ref):
        @pl.when(pl.program_id(0) == 0)
        def _():
            loss_ref[...] = jnp.zeros_like(loss_ref)
            dg_ref[...] = jnp.zeros_like(dg_ref)

        out = x_ref[...] + _mm(o_ref[...] * _silu(gate_ref[...]), w_ref[...])
        g = g_ref[...]
        r, xhat, y = _rms(out, g)
        err = y - tgt_ref[...]
        loss_ref[...] += 0.5 * jnp.sum(jnp.mean(err * err, axis=-1, keepdims=True), axis=0, keepdims=True)
        dx, dg = _rms_bwd(err * (1.0 / D_MODEL), g, r, xhat)
        dg_ref[...] += dg
        dx_ref[...] = dx

    row = _rows(tm, D_MODEL)
    vec = _whole((1, D_MODEL))
    return pl.pallas_call(
        body, name="attn_out_loss", grid=(t // tm,),
        in_specs=[row, row, row, _whole((Q_DIM, D_MODEL)), vec, row], out_specs=[_whole((1, 1)), row, vec],
        out_shape=[_sds((1, 1)), _sds((t, D_MODEL)), _sds((1, D_MODEL))],
        compiler_params=_params(("arbitrary",)),
    )(x, o, gate, w_out, norm, target)


OCT_TILE = pl.BlockSpec((None, LANES, LANES), lambda b: (b, 0, 0))
N_LAGS = S5_CHUNK + 1


def _cmul(ar, ai, br, bi):
    return ar * br - ai * bi, ar * bi + ai * br


def _cmul_conj(ar, ai, br, bi):
    return ar * br + ai * bi, ar * bi - ai * br


def _mm_f32(a, b, dims):
    return lax.dot_general(a, b, (dims, ((), ())), precision=lax.Precision.HIGH, preferred_element_type=F32)


def _s5_discretise(ar, ai, ls, br, bi):
    dt = jnp.exp(ls)
    xr = ar * dt
    xi = ai * dt
    mag = jnp.exp(xr)
    first = (mag * jnp.cos(xi), mag * jnp.sin(xi))
    powers = [(jnp.ones_like(xr), jnp.zeros_like(xr)), first]
    for _ in range(2, N_LAGS):
        powers.append(_cmul(*powers[-1], *first))
    den = ar * ar + ai * ai
    nr = powers[1][0] - 1.0
    ni = powers[1][1]
    fr = (nr * ar + ni * ai) / den
    fi = (ni * ar - nr * ai) / den
    bbr, bbi = _cmul(fr, fi, br, bi)
    return dt, powers, (fr, fi), (bbr, bbi), den


def _same_group_tile():
    row = lax.broadcasted_iota(jnp.int32, (LANES, LANES), 0)
    col = lax.broadcasted_iota(jnp.int32, (LANES, LANES), 1)
    return (row // SSM_GROUP) == (col // SSM_GROUP)


def _first_copy_lanes():
    return lax.broadcasted_iota(jnp.int32, (LANES, LANES), 1) < SSM_STATE


def s5_param_fwd(tiles, token=None):
    extra, extra_specs = _after(token)

    def body(ar_ref, ai_ref, ls_ref, br_ref, bi_ref, cr_ref, ci_ref, *rest):
        kd_ref, wsr_ref, wsi_ref, wor_ref, woi_ref, pr_ref, pi_ref = rest[-7:]
        cr = cr_ref[...]
        ci = ci_ref[...]
        _, powers, _, (bbr, bbi), _ = _s5_discretise(ar_ref[...], ai_ref[...], ls_ref[...], br_ref[...], bi_ref[...])
        once = _first_copy_lanes()
        crm = jnp.where(once, cr, 0.0)
        cim = jnp.where(once, ci, 0.0)
        same = _same_group_tile()
        for lag in range(S5_CHUNK):
            er, ei = powers[lag]
            xr, xi = _cmul(er, ei, bbr, bbi)
            rows = pl.ds((S5_CHUNK - 1 - lag) * LANES, LANES)
            wsr_ref[rows, :] = xr
            wsi_ref[rows, :] = xi
        k = _mm_f32(wsr_ref[...], crm, ((1,), (1,))) - _mm_f32(wsi_ref[...], cim, ((1,), (1,)))
        for lag in range(S5_CHUNK):
            kd_ref[lag] = jnp.where(same, k[(S5_CHUNK - 1 - lag) * LANES:(S5_CHUNK - lag) * LANES], 0.0)
        for t in range(S5_CHUNK):
            er, ei = powers[t + 1]
            zr, zi = _cmul(er, ei, cr, ci)
            wor_ref[pl.ds(t * LANES, LANES), :] = zr
            woi_ref[pl.ds(t * LANES, LANES), :] = -zi
        pr_ref[...] = powers[S5_CHUNK][0]
        pi_ref[...] = powers[S5_CHUNK][1]

    return pl.pallas_call(
        body, name="s5_param_fwd", grid=(S5_OCTETS,),
        in_specs=[OCT_TILE] * 7 + [ANY] * len(extra),
        out_specs=[OCT_KD, OCT_W, OCT_W, OCT_W, OCT_W, OCT_TILE, OCT_TILE],
        out_shape=[_sds((S5_OCTETS, S5_CHUNK, LANES, LANES))] + [_sds((S5_OCTETS, S5_OCT_IN, LANES))] * 4
                  + [_sds((S5_OCTETS, LANES, LANES))] * 2,
        compiler_params=_params(("parallel",)),
    )(*tiles, *extra)


def s5_param_bwd(tiles, dkd, dws_re, dws_im, dwo_re, dwo_im, dp_re, dp_im):
    def body(ar_ref, ai_ref, ls_ref, br_ref, bi_ref, cr_ref, ci_ref, dkd_ref, dwsr_ref, dwsi_ref, dwor_ref, dwoi_ref, dpr_ref, dpi_ref,
             dar_ref, dai_ref, dls_ref, dbr_ref, dbi_ref, dcr_ref, dci_ref):
        ar = ar_ref[...]
        ai = ai_ref[...]
        br = br_ref[...]
        bi = bi_ref[...]
        cr = cr_ref[...]
        ci = ci_ref[...]
        dt, powers, (fr, fi), (bbr, bbi), den = _s5_discretise(ar, ai, ls_ref[...], br, bi)
        once = _first_copy_lanes()
        crm = jnp.where(once, cr, 0.0)
        cim = jnp.where(once, ci, 0.0)
        same = _same_group_tile()
        zero = jnp.zeros((LANES, LANES), F32)
        dpow = [[zero, zero] for _ in range(N_LAGS)]
        dbbr, dbbi = zero, zero
        by_step = [S5_CHUNK - 1 - s for s in range(S5_CHUNK)]
        x_all = [_cmul(*powers[lag], bbr, bbi) for lag in by_step]
        xr_all = jnp.concatenate([x[0] for x in x_all], axis=0)
        xi_all = jnp.concatenate([x[1] for x in x_all], axis=0)
        g_all = jnp.concatenate([jnp.where(same, dkd_ref[lag], 0.0) for lag in by_step], axis=0)
        dxr_all = dwsr_ref[...] + _mm_f32(g_all, crm, ((1,), (0,)))
        dxi_all = dwsi_ref[...] - _mm_f32(g_all, cim, ((1,), (0,)))
        dcr = jnp.where(once, _mm_f32(g_all, xr_all, ((0,), (0,))), 0.0)
        dci = -jnp.where(once, _mm_f32(g_all, xi_all, ((0,), (0,))), 0.0)
        for lag in range(S5_CHUNK):
            er, ei = powers[lag]
            rows = slice((S5_CHUNK - 1 - lag) * LANES, (S5_CHUNK - lag) * LANES)
            dxr = dxr_all[rows]
            dxi = dxi_all[rows]
            a, b = _cmul_conj(bbr, bbi, dxr, dxi)
            dpow[lag][0] = dpow[lag][0] + a
            dpow[lag][1] = dpow[lag][1] + b
            a, b = _cmul_conj(er, ei, dxr, dxi)
            dbbr = dbbr + a
            dbbi = dbbi + b
        for t in range(S5_CHUNK):
            er, ei = powers[t + 1]
            dzr = dwor_ref[pl.ds(t * LANES, LANES), :]
            dzi = -dwoi_ref[pl.ds(t * LANES, LANES), :]
            a, b = _cmul_conj(cr, ci, dzr, dzi)
            dpow[t + 1][0] = dpow[t + 1][0] + a
            dpow[t + 1][1] = dpow[t + 1][1] + b
            a, b = _cmul_conj(er, ei, dzr, dzi)
            dcr = dcr + a
            dci = dci + b
        dpow[S5_CHUNK][0] = dpow[S5_CHUNK][0] + dpr_ref[...]
        dpow[S5_CHUNK][1] = dpow[S5_CHUNK][1] + dpi_ref[...]
        dfr, dfi = _cmul_conj(br, bi, dbbr, dbbi)
        dbr, dbi = _cmul_conj(fr, fi, dbbr, dbbi)
        dnr, dni = _cmul(ar / den, ai / den, dfr, dfi)
        qr = (fr * ar + fi * ai) / den
        qi = (fi * ar - fr * ai) / den
        dlr, dli = _cmul(-qr, qi, dfr, dfi)
        dpow[1][0] = dpow[1][0] + dnr
        dpow[1][1] = dpow[1][1] + dni
        dxr, dxi = zero, zero
        for lag in range(1, N_LAGS):
            a, b = _cmul_conj(powers[lag][0], powers[lag][1], dpow[lag][0], dpow[lag][1])
            dxr = dxr + lag * a
            dxi = dxi + lag * b
        dar_ref[...] = dlr + dt * dxr
        dai_ref[...] = dli + dt * dxi
        dls_ref[...] = dt * (ar * dxr + ai * dxi)
        dbr_ref[...] = dbr
        dbi_ref[...] = dbi
        dcr_ref[...] = dcr
        dci_ref[...] = dci

    return pl.pallas_call(
        body, name="s5_param_bwd", grid=(S5_OCTETS,),
        in_specs=[OCT_TILE] * 7 + [OCT_KD, OCT_W, OCT_W, OCT_W, OCT_W, OCT_TILE, OCT_TILE], out_specs=[OCT_TILE] * 7,
        out_shape=[_sds((S5_OCTETS, LANES, LANES))] * 7,
        compiler_params=_params(("parallel",)),
    )(*tiles, dkd, dws_re, dws_im, dwo_re, dwo_im, dp_re, dp_im)


def _doubled(v):
    return jnp.concatenate([v, v], axis=-1)


def _s5_param_tiles(a_re, a_im, log_step, b_re, b_im, c_re, c_im):
    def per_group(a):
        return _doubled(jnp.broadcast_to(a.reshape(S5_OCTETS, S5_OCT, 1, SSM_STATE),
                                         (S5_OCTETS, S5_OCT, SSM_GROUP, SSM_STATE)).reshape(S5_OCTETS, LANES, SSM_STATE))

    ls = jnp.broadcast_to(log_step.reshape(S5_OCTETS, S5_OCT, 1, 1), (S5_OCTETS, S5_OCT, SSM_GROUP, LANES)).reshape(S5_OCTETS, LANES, LANES)
    bt = lambda b: _doubled(b.transpose(0, 2, 1).reshape(S5_OCTETS, LANES, SSM_STATE))
    ct = lambda c: _doubled(c.reshape(S5_OCTETS, LANES, SSM_STATE))
    return [per_group(a_re), per_group(a_im), ls, bt(b_re), bt(b_im), ct(c_re), ct(c_im)]


def _s5_param_grads(dtiles):
    dar, dai, dls, dbr, dbi, dcr, dci = dtiles
    halves = lambda d: d[..., :SSM_STATE] + d[..., SSM_STATE:]
    per_group = lambda d: halves(d).reshape(SSM_GROUPS, SSM_GROUP, SSM_STATE).sum(axis=1)
    per_row = lambda d: halves(d).reshape(SSM_GROUPS, SSM_GROUP, SSM_STATE)
    return (per_group(dar), per_group(dai), dls.reshape(SSM_GROUPS, SSM_GROUP * LANES).sum(axis=1),
            per_row(dbr).transpose(0, 2, 1), per_row(dbi).transpose(0, 2, 1), per_row(dcr), per_row(dci))


def _group_power_rows(tile):
    return tile[:, ::SSM_GROUP, :SSM_STATE].reshape(1, S5_STATES)


def _group_power_tiles(row):
    t = jnp.pad(row.reshape(S5_OCTETS, S5_OCT, 1, SSM_STATE), ((0, 0), (0, 0), (0, SSM_GROUP - 1), (0, LANES - SSM_STATE)))
    return t.reshape(S5_OCTETS, LANES, LANES)


def _rope_tables(t):
    pos = jnp.arange(t, dtype=F32)
    inv_freq = ROPE_THETA ** (-jnp.arange(0, HEAD_DIM, 2, dtype=F32) / HEAD_DIM)
    ang = pos[:, None] * inv_freq[None, :]
    cos = jnp.cos(ang)
    sin = jnp.sin(ang)
    cos64 = jnp.concatenate([cos, cos], axis=1)
    sin64 = jnp.concatenate([-sin, sin], axis=1)
    return jnp.concatenate([cos64, cos64], axis=1), jnp.concatenate([sin64, sin64], axis=1)


def _row(v):
    return v.reshape(1, -1)


def _s5_matrices(w, token=None):
    tiles = _s5_param_tiles(w["a_re"], w["a_im"], w["log_step"], w["b_re"], w["b_im"], w["c_re"], w["c_im"])
    kd, ws_re, ws_im, wo_re, wo_im, p_re, p_im = s5_param_fwd(tiles, token)
    return tiles, dict(kd=kd, ws_re=ws_re, ws_im=ws_im, wo_re=wo_re, wo_im=wo_im, a_re=_group_power_rows(p_re), a_im=_group_power_rows(p_im))


def _ssm_forward(x, w):
    tiles, mats = w["s5"] if "s5" in w else _s5_matrices(w)
    u, gate = ssm_proj_fwd(x, _row(w["norm"]), w["w_in"])
    s_re, s_im = s5_chunk_states(u, mats["ws_re"], mats["ws_im"])
    h_re, h_im = s5_scan_fwd(s_re, s_im, mats["a_re"], mats["a_im"])
    y_scan = s5_outputs(u, h_re, h_im, mats["kd"], mats["wo_re"], mats["wo_im"])
    y, g2, x_new = ssm_mix_fwd(x, u, gate, y_scan, _row(w["d"]), w["w_glu"], _row(w["b_glu"]), w["w_out"])
    saved = dict(x=x, u=u, gate=gate, y=y, g2=g2, h_re=h_re, h_im=h_im, mats=mats, tiles=tiles)
    return x_new, saved


def _ssm_backward(dxo, w, s, token=None):
    dy, dgate, dw_out, dw_glu, db_glu, dd = ssm_mix_bwd(dxo, s["u"], s["gate"], s["y"], s["g2"], w["w_glu"], w["w_out"], token)
    mats = s["mats"]
    dh_re, dh_im = s5_state_grads(dy, mats["wo_re"], mats["wo_im"])
    ds_re, ds_im, da_re, da_im = s5_scan_bwd(dh_re, dh_im, s["h_re"], s["h_im"], mats["a_re"], mats["a_im"])
    du_scan = s5_input_grads(dy, ds_re, ds_im, mats["kd"], mats["ws_re"], mats["ws_im"])
    dkd, dws_re, dws_im, dwo_re, dwo_im = s5_weight_grads(s["u"], dy, s["h_re"], s["h_im"], ds_re, ds_im)
    dparams = _s5_param_grads(s5_param_bwd(s["tiles"], dkd, dws_re, dws_im, dwo_re, dwo_im,
                                           _group_power_tiles(da_re), _group_power_tiles(da_im)))
    dx, dw_in, dnorm = ssm_proj_bwd(s["x"], _row(w["norm"]), dxo, dy, du_scan, dgate, _row(w["d"]), w["w_in"])
    grads = dict(norm=dnorm, w_in=dw_in, d=dd, w_glu=dw_glu, b_glu=db_glu, w_out=dw_out)
    for name, val in zip(("a_re", "a_im", "log_step", "b_re", "b_im", "c_re", "c_im"), dparams):
        grads[name] = val
    return dx, grads


def _attn_forward(x, w, cos2, sin2, loss_head=None):
    q, k, v, gate = attn_proj_fwd(x, _row(w["norm"]), w["w_in"], cos2, sin2)
    o, lse = attn_fwd(q, k, v, w["sinks"])
    if loss_head is None:
        result = attn_out_fwd(x, o, gate, w["w_out"])
    else:
        result = attn_out_loss(x, o, gate, w["w_out"], _row(loss_head[0]), loss_head[1])
    return result, dict(x=x, q=q, k=k, v=v, gate=gate, o=o, lse=lse)


def _attn_backward(dxo, w, s, cos2, sin2, token=None):
    do, dgate, dw_out = attn_out_bwd(dxo, s["o"], s["gate"], w["w_out"], token)
    dq, dk, dv, dsinks = attn_bwd(s["q"], s["k"], s["v"], w["sinks"], s["o"], s["lse"], do)
    dx, dw_in, dnorm = attn_proj_bwd(s["x"], _row(w["norm"]), dxo, dq, dk, dv, dgate, cos2, sin2, w["w_in"])
    return dx, dict(norm=dnorm, w_in=dw_in, sinks=dsinks[0, :N_Q_HEADS], w_out=dw_out)


class _NoExchanges:
    def __init__(self, layers):
        self.layers = layers

    def layer(self, i, x):
        return self.layers[i]

    def layer_done(self, i, grads, dx):
        return None


def _sequence_step(x, target, final_norm, hooks, depth=4):
    cos2, sin2 = _rope_tables(x.shape[0])
    saved, layers = [], []
    for i in range(depth):
        w = hooks.layer(i, x)
        layers.append(w)
        if i % 2 == 0:
            x, s = _ssm_forward(x, w)
        else:
            x, s = _attn_forward(x, w, cos2, sin2, (final_norm, target) if i == depth - 1 else None)
        saved.append(s)
    loss, dx, dfinal = x
    grads = {"final_norm": dfinal}
    token = None
    for i in reversed(range(depth)):
        if i % 2 == 0:
            dx, g = _ssm_backward(dx, layers[i], saved[i], token)
        else:
            dx, g = _attn_backward(dx, layers[i], saved[i], cos2, sin2, token)
        g = {"l%d_%s" % (i, name): val for name, val in g.items()}
        grads.update(g)
        token = hooks.layer_done(i, g, dx)
    return loss[0, 0], dx, grads


ANY = pl.BlockSpec(memory_space=pl.ANY)


def _place():
    return lax.axis_index("x"), lax.axis_index("y"), lax.axis_index("c")


def _other_chips(x, y):
    return [(1 - x, y), (x, 1 - y), (1 - x, 1 - y)]


class _StagedCopies:
    def __init__(self, bufs, load_sems, store_sems):
        self.bufs, self.load_sems, self.store_sems = bufs, load_sems, store_sems
        self.loads, self.stores = [], []

    def load(self, i, src):
        cp = pltpu.make_async_copy(src, self.bufs[i], self.load_sems.at[i])
        cp.start()
        self.loads.append(cp)

    def store(self, i, dst):
        self.loads[i].wait()
        cp = pltpu.make_async_copy(self.bufs[i], dst, self.store_sems.at[i])
        cp.start()
        self.stores.append(cp)

    def finish(self):
        for cp in self.stores:
            cp.wait()


def _staging(blocks):
    n = len(blocks)
    return [pltpu.VMEM(b.shape, b.dtype) for b in blocks] + [pltpu.SemaphoreType.DMA((n,)), pltpu.SemaphoreType.DMA((n,))]


def exchange_halves_with_sibling(grads):
    n = len(grads)

    def body(*refs):
        ins, outs = refs[:n], refs[n:2 * n]
        send_sems, recv_sems = refs[2 * n:]
        x, y, c = _place()
        copies = []
        for i in range(n):
            half = ins[i].shape[1] // 2
            src = ins[i].at[:, pl.ds((1 - c) * half, half), :]
            cp = pltpu.make_async_remote_copy(src_ref=src, dst_ref=outs[i], send_sem=send_sems.at[i], recv_sem=recv_sems.at[i],
                                              device_id=(x, y, 1 - c), device_id_type=MESH)
            cp.start()
            copies.append(cp)
        for cp in copies:
            cp.wait()

    return pl.pallas_call(
        body, name="exchange_halves_with_sibling",
        in_specs=[ANY] * n, out_specs=[ANY] * n,
        out_shape=[_sds((g.shape[0], g.shape[1] // 2, g.shape[2])) for g in grads],
        scratch_shapes=[pltpu.SemaphoreType.DMA((n,)), pltpu.SemaphoreType.DMA((n,))],
    )(*grads)


def swap_halves_with_sibling(pieces):
    n = len(pieces)

    def body(*refs):
        ins, outs = refs[:n], refs[n:2 * n]
        send_sems, recv_sems = refs[2 * n:2 * n + 2]
        own = _StagedCopies(refs[2 * n + 2:3 * n + 2], *refs[3 * n + 2:])
        x, y, c = _place()
        for i in range(n):
            own.load(i, ins[i])
        swaps = []
        for i in range(n):
            cp = pltpu.make_async_remote_copy(src_ref=ins[i], dst_ref=outs[i].at[c], send_sem=send_sems.at[i], recv_sem=recv_sems.at[i],
                                              device_id=(x, y, 1 - c), device_id_type=MESH)
            cp.start()
            swaps.append(cp)
        for i in range(n):
            own.store(i, outs[i].at[c])
        for i in range(n):
            pltpu.make_async_remote_copy(src_ref=ins[i], dst_ref=outs[i].at[1 - c], send_sem=send_sems.at[i], recv_sem=recv_sems.at[i],
                                         device_id=(x, y, 1 - c), device_id_type=MESH).wait_recv()
        for cp in swaps:
            cp.wait_send()
        own.finish()

    return pl.pallas_call(
        body, name="swap_halves_with_sibling",
        in_specs=[ANY] * n, out_specs=[ANY] * n,
        out_shape=[_sds((2,) + p.shape) for p in pieces],
        scratch_shapes=[pltpu.SemaphoreType.DMA((n,)), pltpu.SemaphoreType.DMA((n,))] + _staging(pieces),
        compiler_params=_params(),
    )(*pieces)


IN_HBM = pl.BlockSpec(memory_space=pltpu.HBM)
SEMAPHORES = pl.BlockSpec(memory_space=pltpu.SEMAPHORE)
DATAFLOW = pltpu.SideEffectType.DATAFLOW_SIDE_EFFECTING


def _hbm(a):
    return pltpu.with_memory_space_constraint(a, pltpu.HBM)


def place_own_blocks(shards):
    n = len(shards)

    def body(*refs):
        ins, outs = refs[:n], refs[n:2 * n]
        own = _StagedCopies(refs[2 * n:3 * n], *refs[3 * n:])
        x, y, _ = _place()
        for i in range(n):
            own.load(i, ins[i])
        for i in range(n):
            own.store(i, outs[i].at[2 * x + y])
        own.finish()

    return pl.pallas_call(
        body, name="place_own_blocks", in_specs=[ANY] * n, out_specs=[ANY] * n,
        out_shape=[_sds((4,) + s.shape, s.dtype) for s in shards],
        scratch_shapes=_staging(shards), compiler_params=_params(),
    )(*shards)


def _block_to_send(ref, chip, per_target):
    if not per_target:
        return ref
    return ref.at[chip] if ref.shape[0] == 4 else ref.at[0]


def start_sends_to_chips(name, sources, landings, per_target, after):
    n = len(sources)
    n_sems = 2 * 3 * n

    def body(*refs):
        srcs = refs[:n]
        sems = refs[2 * n + 1:2 * n + 1 + n_sems]
        lands = refs[2 * n + 1 + n_sems:3 * n + 1 + n_sems]
        token = refs[3 * n + 1 + n_sems]
        x, y, c = _place()
        me = 2 * x + y
        for i in range(n):
            for k, (tx, ty) in enumerate(_other_chips(x, y)):
                src = _block_to_send(srcs[i], 2 * tx + ty, per_target)
                pltpu.make_async_remote_copy(src_ref=src, dst_ref=lands[i].at[me], send_sem=sems[2 * (3 * i + k)], recv_sem=sems[2 * (3 * i + k) + 1],
                                             device_id=(tx, ty, c), device_id_type=MESH).start()
        token[...] = jnp.zeros_like(token)

    outs = pl.pallas_call(
        body, name=name,
        in_specs=[IN_HBM] * (2 * n) + [ANY],
        out_specs=[SEMAPHORES] * n_sems + [IN_HBM] * n + [pl.BlockSpec(memory_space=pltpu.VMEM)],
        out_shape=[pltpu.SemaphoreType.DMA(())] * n_sems + [pltpu.HBM(l.shape, l.dtype) for l in landings] + [_sds(TOKEN_SHAPE)],
        input_output_aliases={n + i: n_sems + i for i in range(n)},
        compiler_params=pltpu.CompilerParams(has_side_effects=DATAFLOW),
    )(*[_hbm(s) for s in sources], *[_hbm(l) for l in landings], after)
    return list(outs[:n_sems]), list(outs[n_sems:n_sems + n]), outs[n_sems + n]


def wait_sends_to_chips(name, sources, landings, sems, per_target, after):
    n = len(sources)
    n_sems = len(sems)

    def body(*refs):
        srcs = refs[:n]
        sem_refs = refs[2 * n:2 * n + n_sems]
        lands = refs[2 * n + n_sems + 1:]
        x, y, c = _place()
        me = 2 * x + y
        for i in range(n):
            for k, (tx, ty) in enumerate(_other_chips(x, y)):
                src = _block_to_send(srcs[i], me, per_target)
                cp = pltpu.make_async_remote_copy(src_ref=src, dst_ref=lands[i].at[2 * tx + ty], send_sem=sem_refs[2 * (3 * i + k)],
                                                  recv_sem=sem_refs[2 * (3 * i + k) + 1], device_id=(tx, ty, c), device_id_type=MESH)
                cp.wait_send()
                cp.wait_recv()

    return pl.pallas_call(
        body, name=name,
        in_specs=[IN_HBM] * (2 * n) + [SEMAPHORES] * n_sems + [ANY],
        out_specs=[IN_HBM] * n,
        out_shape=[pltpu.HBM(l.shape, l.dtype) for l in landings],
        input_output_aliases={n + i: i for i in range(n)},
        compiler_params=pltpu.CompilerParams(has_side_effects=DATAFLOW),
    )(*[_hbm(s) for s in sources], *landings, *sems, after)


def _row_tile(rows, cols):
    tm = rows
    while tm * cols * 4 > (2 << 20) and tm % 16 == 0:
        tm //= 2
    return tm


def add_pairs(half, a_list, b_list, out_dtypes, copies=1):
    n = len(a_list)
    nb = a_list[0].shape[0]

    def body(half_ref, *refs):
        for i in range(n):
            total = (refs[i][...] + refs[n + i][...]).astype(out_dtypes[i])
            for o_ref in refs[2 * n + i * copies:2 * n + (i + 1) * copies]:
                o_ref[...] = total

    halves = [pl.BlockSpec((None,) + b.shape[1:], lambda j, h: (j, h[0], 0)) for b in b_list]
    whole = [pl.BlockSpec((None,) + b.shape[1:], lambda j, h: (j, 0, 0)) for b in b_list]
    outs = pl.pallas_call(
        body, name="add_pairs",
        grid_spec=pltpu.PrefetchScalarGridSpec(num_scalar_prefetch=1, grid=(nb,), in_specs=halves + whole,
                                               out_specs=[s for s in whole for _ in range(copies)]),
        out_shape=[_sds(b.shape, dt) for b, dt in zip(b_list, out_dtypes) for _ in range(copies)],
        compiler_params=_params(("parallel",)),
    )(half, *a_list, *b_list)
    return [tuple(outs[i * copies:(i + 1) * copies]) for i in range(n)]


def sum_fours(arrays, token=None):
    n = len(arrays)
    extra, extra_specs = _after(token)

    def body(*refs):
        outs = refs[-n:]
        for a_ref, o_ref in zip(refs[:n], outs):
            o_ref[...] = ((a_ref[0].astype(F32) + a_ref[1].astype(F32)) + a_ref[2].astype(F32)) + a_ref[3].astype(F32)

    return pl.pallas_call(
        body, name="sum_fours", grid=(2,),
        in_specs=[pl.BlockSpec((4, a.shape[1] // 2, a.shape[2]), lambda i: (0, i, 0)) for a in arrays] + extra_specs,
        out_specs=[pl.BlockSpec((a.shape[1] // 2, a.shape[2]), lambda i: (i, 0)) for a in arrays],
        out_shape=[_sds(a.shape[1:]) for a in arrays], compiler_params=_params(("parallel",)),
    )(*arrays, *extra)


def _adamw_update(w_ref, g_ref, m_ref, v_ref, d_ref, nm_ref, nv_ref):
    g = g_ref[...]
    nm = ADAM_B1 * m_ref[...] + (1.0 - ADAM_B1) * g
    nv = ADAM_B2 * v_ref[...] + (1.0 - ADAM_B2) * (g * g)
    d_ref[...] = -ADAM_LR * ((nm / (1.0 - ADAM_B1 ** ADAM_STEP)) / (jnp.sqrt(nv / (1.0 - ADAM_B2 ** ADAM_STEP)) + ADAM_EPS) + ADAM_WD * w_ref[...])
    nm_ref[...] = nm
    nv_ref[...] = nv


def adamw(w, g, m, v):
    rows, cols = w.shape
    tm = _row_tile(rows, cols)

    def body(*refs):
        _adamw_update(*refs)

    spec = pl.BlockSpec((tm, cols), lambda i: (i, 0))
    return pl.pallas_call(
        body, name="adamw", grid=(rows // tm,), in_specs=[spec] * 4, out_specs=[spec] * 3,
        out_shape=[_sds(w.shape)] * 3, compiler_params=_params(("parallel",)),
    )(w, g, m, v)


def adamw_small(ws, gs, ms, vs, slabs=None):
    n = len(ws)

    def body(*refs):
        for i in range(n):
            _adamw_update(refs[i], refs[n + i], refs[2 * n + i], refs[3 * n + i], refs[4 * n + i], refs[5 * n + i], refs[6 * n + i])

    if slabs is None:
        grid = ()
        specs = [pl.BlockSpec(memory_space=pltpu.VMEM)] * n
    else:
        grid = (slabs,)
        specs = [pl.BlockSpec((w.shape[0] // slabs,) + w.shape[1:], lambda i: (i, 0, 0)) for w in ws]
    outs = pl.pallas_call(
        body, name="adamw_small", grid=grid, in_specs=specs * 4, out_specs=specs * 3,
        out_shape=[_sds(w.shape) for w in ws] * 3, compiler_params=_params(("parallel",) if slabs else None),
    )(*ws, *gs, *ms, *vs)
    return outs[:n], outs[n:2 * n], outs[2 * n:]


PACK_TILE = 8 * LANES
PACK_PIECES = 8
PACK_ALIGN = PACK_PIECES * 16


def _pack_small(values, scalar=None):
    parts = []
    for name in SMALL_NAMES:
        flat = values[name].reshape(-1)
        pad = (-flat.shape[0]) % PACK_TILE
        if pad:
            flat = jnp.concatenate([flat, jnp.zeros((pad,), F32)])
        parts.append(flat.reshape(-1, LANES))
    rows = sum(p.shape[0] for p in parts) + 8
    parts.append(jnp.zeros(((-rows) % PACK_ALIGN, LANES), F32))
    last = jnp.zeros((8, LANES), F32)
    parts.append(last if scalar is None else jnp.broadcast_to(scalar.astype(F32), (8, LANES)))
    return jnp.concatenate(parts, axis=0)


def _unpack_small(pack, like):
    out = {}
    row = 0
    for name in SMALL_NAMES:
        size = math.prod(like[name].shape)
        rows = -(-size // PACK_TILE) * 8
        out[name] = pack[row:row + rows].reshape(-1)[:size].reshape(like[name].shape)
        row += rows
    return out


def _travels_transposed(name, shard):
    return name.endswith("w_in") and shard.shape[-1] % LANES != 0


def _to_blocks(name, full):
    if full.ndim == 3:
        return full
    return full.reshape(4, full.shape[0] // 4, full.shape[1])


def _from_blocks(name, stacked):
    if name.endswith("w_in") and stacked.shape[2] % LANES == 0 and stacked.shape[1] == D_MODEL:
        return stacked
    return stacked.reshape(4 * stacked.shape[1], stacked.shape[2])


def _layer_big_names(i):
    return [n for n in BIG_NAMES if n.startswith("l%d_" % i)]


class _OverlappedExchanges:
    def __init__(self, weights):
        self.weights = weights
        self.c = lax.axis_index("c")
        self.first = _layer_big_names(0)
        self.later = [n for n in BIG_NAMES if n not in self.first]
        shards = [weights[n].astype(MXU_DTYPE) for n in self.first + self.later]
        shards = [s.T if _travels_transposed(n, s) else s for n, s in zip(self.first + self.later, shards)]
        placed = place_own_blocks(shards)
        k = len(self.first)
        sems, stacks, token = start_sends_to_chips("gather_first_start", shards[:k], placed[:k], False, shards[0])
        self.gather_first = (shards[:k], sems, stacks)
        sems, stacks, token = start_sends_to_chips("gather_later_start", shards[k:], placed[k:], False, token)
        self.gather_later = (shards[k:], sems, stacks)
        self.s5 = {}
        for i in (0, 2):
            self.s5[i] = _s5_matrices({n: weights["l%d_%s" % (i, n)] for n in SSM_NAMES if "l%d_%s" % (i, n) in SMALL_NAMES}, token)
            token = self.s5[i][1]["kd"]
        self.full = {}
        self.in_flight = {}
        self.contributions = {}

    def layer(self, i, x):
        if i == 0:
            shards, sems, stacks = self.gather_first
            stacks = wait_sends_to_chips("gather_first_wait", shards, stacks, sems, False, self.s5[2][1]["kd"])
            self.full.update({n: _from_blocks(n, g) for n, g in zip(self.first, stacks)})
        if i == 1:
            shards, sems, stacks = self.gather_later
            stacks = wait_sends_to_chips("gather_later_wait", shards, stacks, sems, False, x)
            self.full.update({n: _from_blocks(n, g) for n, g in zip(self.later, stacks)})
        names = SSM_NAMES if i % 2 == 0 else ATTN_NAMES
        w = {n: self.full.get("l%d_%s" % (i, n), self.weights.get("l%d_%s" % (i, n))) for n in names}
        if i in self.s5:
            w["s5"] = self.s5[i]
        return w

    def chip_sums(self, names, grads, extra_blocks=(), copies=1):
        blocks = [_to_blocks(n, grads[n]) for n in names] + list(extra_blocks)
        from_sibling = exchange_halves_with_sibling(blocks)
        k = len(names)
        half = self.c.reshape(1).astype(jnp.int32)
        sums = add_pairs(half, blocks[:k], from_sibling[:k], [WIRE_DTYPE] * k, copies)
        if extra_blocks:
            sums += add_pairs(half, blocks[k:], from_sibling[k:], [F32] * len(extra_blocks), copies)
        return sums

    def layer_done(self, i, grads, dx):
        if i + 1 in self.in_flight:
            names, sums, sems, landings = self.in_flight.pop(i + 1)
            done = wait_sends_to_chips("scatter_wait_l%d" % (i + 1), sums, landings, sems, True, dx)
            self.contributions.update(zip(names, done))
        if i == 0:
            return None
        names = _layer_big_names(i)
        pairs = self.chip_sums(names, grads, copies=2)
        sums = [p[0] for p in pairs]
        sems, landings, token = start_sends_to_chips("scatter_start_l%d" % i, sums, [p[1] for p in pairs], True, sums[0])
        self.in_flight[i] = (names, sums, sems, landings)
        return token


def _train_step(x, loss_target, weights, moments_m, moments_v):
    hooks = _OverlappedExchanges(weights)
    loss, dx, grads = _sequence_step(x[0], loss_target[0], weights["final_norm"], hooks)
    small_pack = _pack_small({n: grads[n] for n in SMALL_NAMES}, scalar=loss)
    last = _layer_big_names(0)
    pairs = hooks.chip_sums(last, grads, extra_blocks=[small_pack[None]], copies=2)
    sums = [p[0] for p in pairs]
    landings = [p[1] for p in pairs[:-1]] + [jnp.broadcast_to(sums[-1], (4,) + sums[-1].shape[1:])]
    sems, landings, token = start_sends_to_chips("scatter_start_l0", sums, landings, True, sums[0])
    out_grad, out_delta, out_m, out_v = {}, {}, {}, {}

    def finish(names, arrays, token=None):
        shared = swap_halves_with_sibling(sum_fours(arrays, token))
        for n, s in zip(names, shared):
            if n == "small":
                return s.reshape(-1, LANES)
            out_grad[n] = s.reshape(2 * s.shape[1], s.shape[2])
            if _travels_transposed(n, weights[n]):
                out_grad[n] = out_grad[n].T
            out_delta[n], out_m[n], out_v[n] = adamw(weights[n], out_grad[n], moments_m[n], moments_v[n])

    others = [n for n in BIG_NAMES if n not in last]
    finish(others, [hooks.contributions[n] for n in others], token)
    arrived = wait_sends_to_chips("scatter_wait_l0", sums, landings, sems, True, out_v[others[-1]])
    small_grad_pack = finish(last + ["small"], arrived)
    loss = small_grad_pack[-8, 0]
    out_grad.update(_unpack_small(small_grad_pack, {n: weights[n] for n in SMALL_NAMES}))
    cubes = [n for n in SMALL_NAMES if weights[n].ndim == 3]
    for names, slabs in ((cubes, 8), ([n for n in SMALL_NAMES if n not in cubes], None)):
        deltas, new_ms, new_vs = adamw_small(*[[group[n] for n in names] for group in (weights, out_grad, moments_m, moments_v)], slabs=slabs)
        out_delta.update(zip(names, deltas))
        out_m.update(zip(names, new_ms))
        out_v.update(zip(names, new_vs))
    outs = [loss, dx[None]]
    for group in (out_grad, out_delta, out_m, out_v):
        outs.extend(group[n] for n in WEIGHT_NAMES)
    return tuple(outs)


def kernel(x, l0_norm, l0_w_in, l0_a_re, l0_a_im, l0_log_step, l0_b_re, l0_b_im, l0_c_re, l0_c_im, l0_d, l0_w_glu, l0_b_glu, l0_w_out, l1_norm, l1_w_in, l1_sinks, l1_w_out, l2_norm, l2_w_in, l2_a_re, l2_a_im, l2_log_step, l2_b_re, l2_b_im, l2_c_re, l2_c_im, l2_d, l2_w_glu, l2_b_glu, l2_w_out, l3_norm, l3_w_in, l3_sinks, l3_w_out, final_norm, loss_target, m_l0_norm, m_l0_w_in, m_l0_a_re, m_l0_a_im, m_l0_log_step, m_l0_b_re, m_l0_b_im, m_l0_c_re, m_l0_c_im, m_l0_d, m_l0_w_glu, m_l0_b_glu, m_l0_w_out, m_l1_norm, m_l1_w_in, m_l1_sinks, m_l1_w_out, m_l2_norm, m_l2_w_in, m_l2_a_re, m_l2_a_im, m_l2_log_step, m_l2_b_re, m_l2_b_im, m_l2_c_re, m_l2_c_im, m_l2_d, m_l2_w_glu, m_l2_b_glu, m_l2_w_out, m_l3_norm, m_l3_w_in, m_l3_sinks, m_l3_w_out, m_final_norm, v_l0_norm, v_l0_w_in, v_l0_a_re, v_l0_a_im, v_l0_log_step, v_l0_b_re, v_l0_b_im, v_l0_c_re, v_l0_c_im, v_l0_d, v_l0_w_glu, v_l0_b_glu, v_l0_w_out, v_l1_norm, v_l1_w_in, v_l1_sinks, v_l1_w_out, v_l2_norm, v_l2_w_in, v_l2_a_re, v_l2_a_im, v_l2_log_step, v_l2_b_re, v_l2_b_im, v_l2_c_re, v_l2_c_im, v_l2_d, v_l2_w_glu, v_l2_b_glu, v_l2_w_out, v_l3_norm, v_l3_w_in, v_l3_sinks, v_l3_w_out, v_final_norm):
    args = locals()
    weights = {n: args[n] for n in WEIGHT_NAMES}
    moments_m = {n: args["m_" + n] for n in WEIGHT_NAMES}
    moments_v = {n: args["v_" + n] for n in WEIGHT_NAMES}
    return _train_step(x, loss_target, weights, moments_m, moments_v)
```

```python
import functools
import math

import jax
import jax.numpy as jnp
from jax import lax
from jax.experimental import pallas as pl
from jax.experimental.pallas import tpu as pltpu

F32 = jnp.float32
MXU_DTYPE = jnp.bfloat16
WIRE_DTYPE = jnp.bfloat16
MESH = pl.DeviceIdType.MESH

D_MODEL = 1024
BRANCH = 1024
NORM_EPS = 1e-5
SSM_GROUPS = 64
SSM_GROUP = 16
SSM_STATE = 64
S5_CHUNK = 16
LANES = 128
S5_OCT = LANES // SSM_GROUP
S5_OCTETS = SSM_GROUPS // S5_OCT
S5_OCT_IN = S5_CHUNK * LANES
S5_OCT_STATE = S5_OCT * SSM_STATE
S5_STATES = SSM_GROUPS * SSM_STATE
HEAD_DIM = 64
N_Q_HEADS = 16
N_KV_HEADS = 2
GQA_GROUP = N_Q_HEADS // N_KV_HEADS
ATTN_BLOCK = 128
Q_DIM = N_Q_HEADS * HEAD_DIM
KV_DIM = N_KV_HEADS * HEAD_DIM
ROPE_THETA = 10000.0
NEG_INF = -1e30
ADAM_LR = 0.001
ADAM_B1 = 0.9
ADAM_B2 = 0.999
ADAM_EPS = 1e-08
ADAM_WD = 0.01
ADAM_STEP = 10

VMEM_LIMIT_V7X = 56 * 1024 * 1024
ROW_TILE_FWD = 512
ROW_TILE_BWD = 512

SSM_NAMES = ("norm", "w_in", "a_re", "a_im", "log_step", "b_re", "b_im", "c_re", "c_im", "d", "w_glu", "b_glu", "w_out")
ATTN_NAMES = ("norm", "w_in", "sinks", "w_out")


def _weight_names():
    names = []
    for i in range(4):
        for n in (SSM_NAMES if i % 2 == 0 else ATTN_NAMES):
            names.append("l%d_%s" % (i, n))
    names.append("final_norm")
    return names


WEIGHT_NAMES = _weight_names()
BIG_NAMES = [n for n in WEIGHT_NAMES if n.endswith(("w_in", "w_glu", "w_out"))]
SMALL_NAMES = [n for n in WEIGHT_NAMES if n not in BIG_NAMES]


def _params(semantics=None):
    return pltpu.CompilerParams(dimension_semantics=semantics, vmem_limit_bytes=VMEM_LIMIT_V7X)


def _rows(tm, n):
    return pl.BlockSpec((tm, n), lambda i: (i, 0))


def _whole(shape):
    return pl.BlockSpec(shape, lambda i: (0,) * len(shape), pipeline_mode=pl.Buffered(1))


def _sds(shape, dtype=F32):
    return jax.ShapeDtypeStruct(shape, dtype)


def _mm(a, b):
    return jnp.dot(a.astype(MXU_DTYPE), b.astype(MXU_DTYPE), preferred_element_type=F32)


def _mm_tn(a, b):
    return lax.dot_general(a.astype(MXU_DTYPE), b.astype(MXU_DTYPE), (((0,), (0,)), ((), ())), preferred_element_type=F32)


def _mm_nt(a, b):
    return lax.dot_general(a.astype(MXU_DTYPE), b.astype(MXU_DTYPE), (((1,), (1,)), ((), ())), preferred_element_type=F32)


def _sigmoid(x):
    return 0.5 + 0.5 * jnp.tanh(0.5 * x)


def _silu(x):
    return x * _sigmoid(x)


def _silu_and_grad(x):
    s = _sigmoid(x)
    return x * s, s * (1.0 + x * (1.0 - s))


GELU_C0 = math.sqrt(2.0 / math.pi)
GELU_C1 = 0.044715


def _gelu(x):
    return 0.5 * x * (1.0 + jnp.tanh(GELU_C0 * (x + GELU_C1 * x * x * x)))


def _gelu_and_grad(x):
    x2 = x * x
    th = jnp.tanh(GELU_C0 * x * (1.0 + GELU_C1 * x2))
    half = 0.5 + 0.5 * th
    return x * half, half + 0.5 * x * (1.0 - th * th) * (GELU_C0 + 3.0 * GELU_C0 * GELU_C1 * x2)


def _rms(x, g):
    r = lax.rsqrt(jnp.mean(x * x, axis=-1, keepdims=True) + NORM_EPS)
    xhat = x * r
    return r, xhat, xhat * g


def _rms_bwd(dh, g, r, xhat):
    dxhat = dh * g
    dx = r * (dxhat - xhat * jnp.mean(dxhat * xhat, axis=-1, keepdims=True))
    return dx, jnp.sum(dh * xhat, axis=0, keepdims=True)


def _swap_half_heads(x):
    n = x.shape[-1]
    lane = lax.broadcasted_iota(jnp.int32, x.shape, x.ndim - 1)
    first = (lane % HEAD_DIM) < (HEAD_DIM // 2)
    return jnp.where(first, pltpu.roll(x, n - HEAD_DIM // 2, x.ndim - 1), pltpu.roll(x, HEAD_DIM // 2, x.ndim - 1))


def _tile_lanes(t, reps):
    return jnp.concatenate([t] * reps, axis=1)


TOKEN_SHAPE = (8, LANES)


def _after(token):
    return ([], []) if token is None else ([token], [_whole(TOKEN_SHAPE)])


def ssm_proj_fwd(x, norm, w_in):
    t = x.shape[0]
    tm = min(ROW_TILE_FWD, t)

    def body(x_ref, g_ref, w_ref, u_ref, gate_ref):
        _, _, h = _rms(x_ref[...], g_ref[...])
        h = h.astype(MXU_DTYPE)
        half = BRANCH // 2
        for j in range(2):
            u_ref[:, j * half:(j + 1) * half] = _mm(h, w_ref[j])
            gate_ref[:, j * half:(j + 1) * half] = _mm(h, w_ref[2 + j])

    return pl.pallas_call(
        body, name="ssm_proj_fwd", grid=(t // tm,),
        in_specs=[_rows(tm, D_MODEL), _whole((1, D_MODEL)), _whole((4, D_MODEL, BRANCH // 2))],
        out_specs=[_rows(tm, BRANCH), _rows(tm, BRANCH)],
        out_shape=[_sds((t, BRANCH)), _sds((t, BRANCH))],
        compiler_params=_params(("parallel",)),
    )(x, norm, w_in)


def _chunk_rows(ref, nk, dtype=None):
    rows = jnp.concatenate([ref[pl.ds(s, nk, stride=S5_CHUNK), :] for s in range(S5_CHUNK)], axis=1)
    return rows.astype(MXU_DTYPE if dtype is None else dtype)


def _store_chunk_rows(ref, val, nk):
    for s in range(S5_CHUNK):
        ref[pl.ds(s, nk, stride=S5_CHUNK), :] = val[:, s * LANES:(s + 1) * LANES]


def _own_group_mask():
    row = lax.broadcasted_iota(jnp.int32, (S5_OCT_IN, S5_OCT_STATE), 0)
    col = lax.broadcasted_iota(jnp.int32, (S5_OCT_IN, S5_OCT_STATE), 1)
    return ((row % LANES) // SSM_GROUP) == (col // SSM_STATE)


def _spread_groups(w):
    return jnp.where(_own_group_mask(), jnp.concatenate([w] * (S5_OCT_STATE // LANES), axis=1), 0.0).astype(MXU_DTYPE)


def _fold_groups(p):
    p = jnp.where(_own_group_mask(), p, 0.0)
    return sum(p[:, q * LANES:(q + 1) * LANES] for q in range(S5_OCT_STATE // LANES))


def _fill_toeplitz(win_ref, kd_ref):
    win_ref[...] = jnp.zeros_like(win_ref)
    for s in range(S5_CHUNK):
        for t in range(s, S5_CHUNK):
            win_ref[s * LANES:(s + 1) * LANES, t * LANES:(t + 1) * LANES] = kd_ref[t - s].astype(MXU_DTYPE)


TOEPLITZ_BLOCK = 512
_TOEPLITZ_BLOCKS = [(lo, lo + TOEPLITZ_BLOCK) for lo in range(0, S5_OCT_IN, TOEPLITZ_BLOCK)]


def _strip(t):
    return pl.BlockSpec((t, LANES), lambda b: (0, b))


def _oct_states(nk):
    return pl.BlockSpec((nk, S5_OCT_STATE), lambda b: (0, b))


OCT_W = pl.BlockSpec((None, S5_OCT_IN, LANES), lambda b: (b, 0, 0))
OCT_KD = pl.BlockSpec((None, S5_CHUNK, LANES, LANES), lambda b: (b, 0, 0, 0))


def s5_chunk_states(u, ws_re, ws_im):
    t = u.shape[0]
    nk = t // S5_CHUNK

    def body(u_ref, wr_ref, wi_ref, re_ref, im_ref):
        uc = _chunk_rows(u_ref, nk)
        re_ref[...] = _mm(uc, _spread_groups(wr_ref[...]))
        im_ref[...] = _mm(uc, _spread_groups(wi_ref[...]))

    return pl.pallas_call(
        body, name="s5_chunk_states", grid=(S5_OCTETS,),
        in_specs=[_strip(t), OCT_W, OCT_W], out_specs=[_oct_states(nk), _oct_states(nk)],
        out_shape=[_sds((nk, S5_STATES)), _sds((nk, S5_STATES))],
        compiler_params=_params(("parallel",)),
    )(u, ws_re, ws_im)


def s5_scan_fwd(s_re, s_im, a_re, a_im):
    nk = s_re.shape[0]

    def body(sre_ref, sim_ref, ar_ref, ai_ref, hre_ref, him_ref):
        ar = ar_ref[...]
        ai = ai_ref[...]

        def step(k, carry):
            hr, hi = carry
            hre_ref[pl.ds(k, 1), :] = hr
            him_ref[pl.ds(k, 1), :] = hi
            sr = sre_ref[pl.ds(k, 1), :]
            si = sim_ref[pl.ds(k, 1), :]
            return ar * hr - ai * hi + sr, ai * hr + ar * hi + si

        zero = jnp.zeros((1, S5_STATES), F32)
        lax.fori_loop(0, nk, step, (zero, zero))

    vm = pl.BlockSpec(memory_space=pltpu.VMEM)
    return pl.pallas_call(
        body, name="s5_scan_fwd", in_specs=[vm, vm, vm, vm], out_specs=[vm, vm],
        out_shape=[_sds((nk, S5_STATES)), _sds((nk, S5_STATES))],
        compiler_params=_params(),
    )(s_re, s_im, a_re, a_im)


def s5_outputs(u, h_re, h_im, kd, wo_re, wo_im):
    t = u.shape[0]
    nk = t // S5_CHUNK

    def body(u_ref, hre_ref, him_ref, kd_ref, wor_ref, woi_ref, y_ref, win_ref):
        _fill_toeplitz(win_ref, kd_ref)
        uc = _chunk_rows(u_ref, nk)
        y = jnp.concatenate([_mm(uc[:, :hi], win_ref[:hi, lo:hi]) for lo, hi in _TOEPLITZ_BLOCKS], axis=1)
        y = y + _mm_nt(hre_ref[...], _spread_groups(wor_ref[...])) + _mm_nt(him_ref[...], _spread_groups(woi_ref[...]))
        _store_chunk_rows(y_ref, y, nk)

    return pl.pallas_call(
        body, name="s5_outputs", grid=(S5_OCTETS,),
        in_specs=[_strip(t), _oct_states(nk), _oct_states(nk), OCT_KD, OCT_W, OCT_W],
        out_specs=_strip(t), out_shape=_sds((t, BRANCH)),
        scratch_shapes=[pltpu.VMEM((S5_OCT_IN, S5_OCT_IN), MXU_DTYPE)],
        compiler_params=_params(("parallel",)),
    )(u, h_re, h_im, kd, wo_re, wo_im)


def s5_state_grads(dy, wo_re, wo_im):
    t = dy.shape[0]
    nk = t // S5_CHUNK

    def body(dy_ref, wor_ref, woi_ref, re_ref, im_ref):
        dyc = _chunk_rows(dy_ref, nk)
        re_ref[...] = _mm(dyc, _spread_groups(wor_ref[...]))
        im_ref[...] = _mm(dyc, _spread_groups(woi_ref[...]))

    return pl.pallas_call(
        body, name="s5_state_grads", grid=(S5_OCTETS,),
        in_specs=[_strip(t), OCT_W, OCT_W], out_specs=[_oct_states(nk), _oct_states(nk)],
        out_shape=[_sds((nk, S5_STATES)), _sds((nk, S5_STATES))],
        compiler_params=_params(("parallel",)),
    )(dy, wo_re, wo_im)


def s5_scan_bwd(dh_re, dh_im, h_re, h_im, a_re, a_im):
    nk = dh_re.shape[0]

    def body(dhr_ref, dhi_ref, hr_ref, hi_ref, ar_ref, ai_ref, dsr_ref, dsi_ref, dar_ref, dai_ref):
        ar = ar_ref[...]
        ai = ai_ref[...]

        def step(i, carry):
            gr, gi = carry
            k = nk - 1 - i
            dhr = dhr_ref[pl.ds(k, 1), :]
            dhi = dhi_ref[pl.ds(k, 1), :]
            dsr_ref[pl.ds(k, 1), :] = gr
            dsi_ref[pl.ds(k, 1), :] = gi
            return dhr + ar * gr + ai * gi, dhi - ai * gr + ar * gi

        zero = jnp.zeros((1, S5_STATES), F32)
        lax.fori_loop(0, nk, step, (zero, zero))
        dsr, dsi, hr, hi = dsr_ref[...], dsi_ref[...], hr_ref[...], hi_ref[...]
        dar_ref[...] = jnp.sum(dsr * hr + dsi * hi, axis=0, keepdims=True)
        dai_ref[...] = jnp.sum(dsi * hr - dsr * hi, axis=0, keepdims=True)

    vm = pl.BlockSpec(memory_space=pltpu.VMEM)
    return pl.pallas_call(
        body, name="s5_scan_bwd", in_specs=[vm] * 6, out_specs=[vm] * 4,
        out_shape=[_sds((nk, S5_STATES)), _sds((nk, S5_STATES)), _sds((1, S5_STATES)), _sds((1, S5_STATES))],
        input_output_aliases={0: 0, 1: 1}, compiler_params=_params(),
    )(dh_re, dh_im, h_re, h_im, a_re, a_im)


def s5_input_grads(dy, ds_re, ds_im, kd, ws_re, ws_im):
    t = dy.shape[0]
    nk = t // S5_CHUNK

    def body(dy_ref, dsr_ref, dsi_ref, kd_ref, wsr_ref, wsi_ref, du_ref, win_ref):
        _fill_toeplitz(win_ref, kd_ref)
        dyc = _chunk_rows(dy_ref, nk)
        du = jnp.concatenate([_mm_nt(dyc[:, lo:], win_ref[lo:hi, lo:]) for lo, hi in _TOEPLITZ_BLOCKS], axis=1)
        du = du + _mm_nt(dsr_ref[...], _spread_groups(wsr_ref[...])) + _mm_nt(dsi_ref[...], _spread_groups(wsi_ref[...]))
        _store_chunk_rows(du_ref, du, nk)

    return pl.pallas_call(
        body, name="s5_input_grads", grid=(S5_OCTETS,),
        in_specs=[_strip(t), _oct_states(nk), _oct_states(nk), OCT_KD, OCT_W, OCT_W],
        out_specs=_strip(t), out_shape=_sds((t, BRANCH)),
        scratch_shapes=[pltpu.VMEM((S5_OCT_IN, S5_OCT_IN), MXU_DTYPE)],
        compiler_params=_params(("parallel",)),
    )(dy, ds_re, ds_im, kd, ws_re, ws_im)


def s5_weight_grads(u, dy, h_re, h_im, ds_re, ds_im):
    t = u.shape[0]
    nk = t // S5_CHUNK

    def body(u_ref, dy_ref, hre_ref, him_ref, dsr_ref, dsi_ref, dkd_ref, dwsr_ref, dwsi_ref, dwor_ref, dwoi_ref):
        dyc = _chunk_rows(dy_ref, nk, F32)
        uct = _chunk_rows(u_ref, nk, F32).T.astype(MXU_DTYPE)
        dyct = dyc.T.astype(MXU_DTYPE)
        dyc = dyc.astype(MXU_DTYPE)
        dwsr_ref[...] = _fold_groups(_mm(uct, dsr_ref[...]))
        dwsi_ref[...] = _fold_groups(_mm(uct, dsi_ref[...]))
        dwor_ref[...] = _fold_groups(_mm(dyct, hre_ref[...]))
        dwoi_ref[...] = _fold_groups(_mm(dyct, him_ref[...]))
        dkd_ref[...] = jnp.zeros_like(dkd_ref)
        for tt in range(0, S5_CHUNK, 2):
            p = _mm(uct[:(tt + 2) * LANES], dyc[:, tt * LANES:(tt + 2) * LANES])
            for s in range(tt + 2):
                rows = p[s * LANES:(s + 1) * LANES]
                if s <= tt:
                    dkd_ref[tt - s] += rows[:, :LANES]
                dkd_ref[tt + 1 - s] += rows[:, LANES:]

    return pl.pallas_call(
        body, name="s5_weight_grads", grid=(S5_OCTETS,),
        in_specs=[_strip(t), _strip(t)] + [_oct_states(nk)] * 4,
        out_specs=[OCT_KD, OCT_W, OCT_W, OCT_W, OCT_W],
        out_shape=[_sds((S5_OCTETS, S5_CHUNK, LANES, LANES))] + [_sds((S5_OCTETS, S5_OCT_IN, LANES))] * 4,
        compiler_params=_params(("parallel",)),
    )(u, dy, h_re, h_im, ds_re, ds_im)


def ssm_mix_fwd(x, u, gate, y_scan, d, w_glu, b_glu, w_out):
    t = x.shape[0]
    tm = min(ROW_TILE_FWD, t)

    def body(x_ref, u_ref, gate_ref, ys_ref, d_ref, wg_ref, bg_ref, wo_ref, y_ref, g2_ref, xo_ref):
        y = ys_ref[...] + d_ref[...] * u_ref[...]
        z0 = _gelu(y)
        g2 = _mm(z0, wg_ref[...]) + bg_ref[...]
        a = z0 * _sigmoid(g2) * _silu(gate_ref[...])
        y_ref[...] = y
        g2_ref[...] = g2
        xo_ref[...] = x_ref[...] + _mm(a, wo_ref[...])

    row = _rows(tm, BRANCH)
    vec = _whole((1, BRANCH))
    mat = _whole((BRANCH, BRANCH))
    return pl.pallas_call(
        body, name="ssm_mix_fwd", grid=(t // tm,),
        in_specs=[row, row, row, row, vec, mat, vec, mat],
        out_specs=[row, row, row],
        out_shape=[_sds((t, BRANCH))] * 3,
        compiler_params=_params(("parallel",)),
    )(x, u, gate, y_scan, d, w_glu, b_glu, w_out)


def ssm_mix_bwd(dxo, u, gate, y, g2, w_glu, w_out, token=None):
    t = dxo.shape[0]
    tm = min(ROW_TILE_BWD, t)
    extra, extra_specs = _after(token)

    def body(dxo_ref, u_ref, gate_ref, y_ref, g2_ref, wgt_ref, wot_ref, *rest):
        dy_ref, dgate_ref, dwo_ref, dwg_ref, dbg_ref, dd_ref = rest[-6:]

        @pl.when(pl.program_id(0) == 0)
        def _():
            dwo_ref[...] = jnp.zeros_like(dwo_ref)
            dwg_ref[...] = jnp.zeros_like(dwg_ref)
            dbg_ref[...] = jnp.zeros_like(dbg_ref)
            dd_ref[...] = jnp.zeros_like(dd_ref)

        dxo = dxo_ref[...]
        gate = gate_ref[...]
        y = y_ref[...]
        z0, z0_grad = _gelu_and_grad(y)
        sg = _sigmoid(g2_ref[...])
        z = z0 * sg
        sgate, sgate_grad = _silu_and_grad(gate)
        da = _mm_nt(dxo, wot_ref[...])
        dwo_ref[...] += _mm_tn(z * sgate, dxo)
        dz = da * sgate
        dgate_ref[...] = da * z * sgate_grad
        dg2 = dz * z0 * sg * (1.0 - sg)
        dbg_ref[...] += jnp.sum(dg2, axis=0, keepdims=True)
        dwg_ref[...] += _mm_tn(z0, dg2)
        dz0 = dz * sg + _mm_nt(dg2, wgt_ref[...])
        dy = dz0 * z0_grad
        dd_ref[...] += jnp.sum(dy * u_ref[...], axis=0, keepdims=True)
        dy_ref[...] = dy

    row = _rows(tm, BRANCH)
    vec = _whole((1, BRANCH))
    mat = _whole((BRANCH, BRANCH))
    return pl.pallas_call(
        body, name="ssm_mix_bwd", grid=(t // tm,),
        in_specs=[row, row, row, row, row, mat, mat] + extra_specs,
        out_specs=[row, row, mat, mat, vec, vec],
        out_shape=[_sds((t, BRANCH)), _sds((t, BRANCH)), _sds((BRANCH, D_MODEL)), _sds((BRANCH, BRANCH)),
                   _sds((1, BRANCH)), _sds((1, BRANCH))],
        compiler_params=_params(("arbitrary",)),
    )(dxo, u, gate, y, g2, w_glu, w_out, *extra)


def ssm_proj_bwd(x, norm, dxo, dy, du_scan, dgate, d, w_in):
    t = x.shape[0]
    tm = min(ROW_TILE_BWD, t)
    n = 2 * BRANCH

    def body(x_ref, g_ref, dxo_ref, dy_ref, dus_ref, dgate_ref, d_ref, wt_ref, dx_ref, dw_ref, dg_ref):
        @pl.when(pl.program_id(0) == 0)
        def _():
            dw_ref[...] = jnp.zeros_like(dw_ref)
            dg_ref[...] = jnp.zeros_like(dg_ref)

        g = g_ref[...]
        r, xhat, h = _rms(x_ref[...], g)
        h = h.astype(MXU_DTYPE)
        du = dus_ref[...] + d_ref[...] * dy_ref[...]
        dproj = jnp.concatenate([du, dgate_ref[...]], axis=1).astype(MXU_DTYPE)
        dh = jnp.zeros((tm, D_MODEL), F32)
        for j in range(4):
            cols = dproj[:, j * (n // 4):(j + 1) * (n // 4)]
            dh = dh + _mm_nt(cols, wt_ref[j])
            dw_ref[j] += _mm_tn(h, cols)
        dx, dg = _rms_bwd(dh, g, r, xhat)
        dg_ref[...] += dg
        dx_ref[...] = dxo_ref[...] + dx

    row = _rows(tm, D_MODEL)
    vec = _whole((1, D_MODEL))
    blocks = _whole((4, D_MODEL, n // 4))
    return pl.pallas_call(
        body, name="ssm_proj_bwd", grid=(t // tm,),
        in_specs=[row, vec, row, row, row, row, vec, blocks],
        out_specs=[row, blocks, vec],
        out_shape=[_sds((t, D_MODEL)), _sds((4, D_MODEL, n // 4)), _sds((1, D_MODEL))],
        compiler_params=_params(("arbitrary",)),
    )(x, norm, dxo, dy, du_scan, dgate, d, w_in)


ATTN_N = Q_DIM + 2 * KV_DIM + BRANCH


def attn_proj_fwd(x, norm, w_in_t, cos2, sin2):
    t = x.shape[0]
    tm = min(ROW_TILE_FWD, t)

    def body(x_ref, g_ref, w_ref, cos_ref, sin_ref, q_ref, k_ref, v_ref, gate_ref):
        _, _, h = _rms(x_ref[...], g_ref[...])
        p = _mm_nt(h, w_ref[...])
        cs = cos_ref[...]
        sn = sin_ref[...]
        q = p[:, :Q_DIM]
        k = p[:, Q_DIM:Q_DIM + KV_DIM]
        q_ref[...] = q * _tile_lanes(cs, Q_DIM // LANES) + _swap_half_heads(q) * _tile_lanes(sn, Q_DIM // LANES)
        k_ref[...] = k * cs + _swap_half_heads(k) * sn
        v_ref[...] = p[:, Q_DIM + KV_DIM:Q_DIM + 2 * KV_DIM]
        gate_ref[...] = p[:, Q_DIM + 2 * KV_DIM:]

    return pl.pallas_call(
        body, name="attn_proj_fwd", grid=(t // tm,),
        in_specs=[_rows(tm, D_MODEL), _whole((1, D_MODEL)), _whole((ATTN_N, D_MODEL)), _rows(tm, LANES), _rows(tm, LANES)],
        out_specs=[_rows(tm, Q_DIM), _rows(tm, KV_DIM), _rows(tm, KV_DIM), _rows(tm, BRANCH)],
        out_shape=[_sds((t, Q_DIM)), _sds((t, KV_DIM)), _sds((t, KV_DIM)), _sds((t, BRANCH))],
        compiler_params=_params(("parallel",)),
    )(x, norm, w_in_t, cos2, sin2)


GQA_LANES = GQA_GROUP * ATTN_BLOCK


def _window_masks(first_block):
    kj = lax.broadcasted_iota(jnp.int32, (ATTN_BLOCK, GQA_LANES), 0)
    qi = lax.broadcasted_iota(jnp.int32, (ATTN_BLOCK, GQA_LANES), 1) % ATTN_BLOCK
    return kj > qi, kj > jnp.where(first_block, qi, ATTN_BLOCK)


def _fold(upper, both):
    return jnp.where(upper, both[:ATTN_BLOCK], both[ATTN_BLOCK:])


def _unfold(upper, tile):
    return jnp.concatenate([jnp.where(upper, tile, 0.0), jnp.where(upper, 0.0, tile)], axis=0).astype(MXU_DTYPE)


def _stack_heads(ref, group):
    return jnp.concatenate([ref[:, h * HEAD_DIM:(h + 1) * HEAD_DIM] for h in range(group * GQA_GROUP, (group + 1) * GQA_GROUP)], axis=0)


def _unstack_heads(ref, group, stacked):
    for n in range(GQA_GROUP):
        h = group * GQA_GROUP + n
        ref[:, h * HEAD_DIM:(h + 1) * HEAD_DIM] = stacked[n * ATTN_BLOCK:(n + 1) * ATTN_BLOCK]


def _sink_row(sink_ref, group):
    return jnp.concatenate([jnp.full((1, ATTN_BLOCK), sink_ref[group * GQA_GROUP + n], F32) for n in range(GQA_GROUP)], axis=1)


def _lane_is(h):
    return lax.broadcasted_iota(jnp.int32, (1, LANES), 1) == h


def attn_fwd(q, k, v, sinks):
    t = q.shape[0]
    nb = t // ATTN_BLOCK
    scale = HEAD_DIM ** -0.5

    def body(sink_ref, q_ref, kc_ref, kp_ref, vc_ref, vp_ref, o_ref, lse_ref):
        keys = jnp.concatenate([kp_ref[...], kc_ref[...]], axis=0).astype(MXU_DTYPE)
        vals = jnp.concatenate([vp_ref[...], vc_ref[...]], axis=0).astype(MXU_DTYPE)
        upper, dead = _window_masks(pl.program_id(0) == 0)
        for g in range(N_KV_HEADS):
            kv = slice(g * HEAD_DIM, (g + 1) * HEAD_DIM)
            qs = _stack_heads(q_ref, g) * scale
            s = jnp.where(dead, NEG_INF, _fold(upper, _mm_nt(keys[:, kv], qs)))
            sink = _sink_row(sink_ref, g)
            m = jnp.maximum(jnp.max(s, axis=0, keepdims=True), sink)
            p = jnp.exp(s - m)
            den = jnp.sum(p, axis=0, keepdims=True) + jnp.exp(sink - m)
            _unstack_heads(o_ref, g, _mm_tn(_unfold(upper, p * (1.0 / den)), vals[:, kv]))
            lse = m + jnp.log(den)
            for n in range(GQA_GROUP):
                lse_ref[pl.ds(g * GQA_GROUP + n, 1), :] = lse[:, n * ATTN_BLOCK:(n + 1) * ATTN_BLOCK]

    cur = lambda n: pl.BlockSpec((ATTN_BLOCK, n), lambda i: (i, 0))
    prev = lambda n: pl.BlockSpec((ATTN_BLOCK, n), lambda i: (jnp.maximum(i - 1, 0), 0))
    return pl.pallas_call(
        body, name="attn_fwd", grid=(nb,),
        in_specs=[pl.BlockSpec(memory_space=pltpu.SMEM), cur(Q_DIM), cur(KV_DIM), prev(KV_DIM), cur(KV_DIM), prev(KV_DIM)],
        out_specs=[cur(Q_DIM), pl.BlockSpec((N_Q_HEADS, ATTN_BLOCK), lambda i: (0, i))],
        out_shape=[_sds((t, Q_DIM)), _sds((N_Q_HEADS, t))],
        compiler_params=_params(("parallel",)),
    )(sinks, q, k, k, v, v)


def attn_bwd(q, k, v, sinks, o, lse, do):
    t = q.shape[0]
    nb = t // ATTN_BLOCK
    scale = HEAD_DIM ** -0.5

    def body(sink_ref, q_ref, kc_ref, kp_ref, vc_ref, vp_ref, o_ref, lse_ref, do_ref,
             dq_ref, dk_ref, dv_ref, dsink_ref, dk_carry, dv_carry):
        i = pl.program_id(0)

        @pl.when(i == 0)
        def _():
            dsink_ref[...] = jnp.zeros_like(dsink_ref)
            dk_carry[...] = jnp.zeros_like(dk_carry)
            dv_carry[...] = jnp.zeros_like(dv_carry)

        @pl.when(i < nb)
        def _():
            keys = jnp.concatenate([kp_ref[...], kc_ref[...]], axis=0).astype(MXU_DTYPE)
            vals = jnp.concatenate([vp_ref[...], vc_ref[...]], axis=0).astype(MXU_DTYPE)
            upper, dead = _window_masks(i == 0)
            dsink = jnp.zeros((1, LANES), F32)
            dk_heads = []
            dv_heads = []
            for g in range(N_KV_HEADS):
                kv = slice(g * HEAD_DIM, (g + 1) * HEAD_DIM)
                qs = (_stack_heads(q_ref, g) * scale).astype(MXU_DTYPE)
                dos = _stack_heads(do_ref, g)
                lse = jnp.concatenate([lse_ref[pl.ds(g * GQA_GROUP + n, 1), :] for n in range(GQA_GROUP)], axis=1)
                s = jnp.where(dead, NEG_INF, _fold(upper, _mm_nt(keys[:, kv], qs)))
                p = jnp.exp(s - lse)
                delta = _mm_f32(jnp.ones((8, HEAD_DIM), F32), dos * _stack_heads(o_ref, g), ((1,), (1,)))[:1]
                dos = dos.astype(MXU_DTYPE)
                ds = _unfold(upper, p * (_fold(upper, _mm_nt(vals[:, kv], dos)) - delta))
                _unstack_heads(dq_ref, g, _mm_tn(ds, keys[:, kv]) * scale)
                dk_heads.append(_mm(ds, qs))
                dv_heads.append(_mm(_unfold(upper, p), dos))
                at_sink = jnp.exp(_sink_row(sink_ref, g) - lse) * delta
                for n in range(GQA_GROUP):
                    dsink = dsink + jnp.where(_lane_is(g * GQA_GROUP + n), -jnp.sum(at_sink[:, n * ATTN_BLOCK:(n + 1) * ATTN_BLOCK]), 0.0)
            dkk = jnp.concatenate(dk_heads, axis=1)
            dvv = jnp.concatenate(dv_heads, axis=1)
            dsink_ref[...] += dsink
            dk_ref[...] = dk_carry[...] + dkk[:ATTN_BLOCK]
            dv_ref[...] = dv_carry[...] + dvv[:ATTN_BLOCK]
            dk_carry[...] = dkk[ATTN_BLOCK:]
            dv_carry[...] = dvv[ATTN_BLOCK:]

        @pl.when(i == nb)
        def _():
            dk_ref[...] = dk_carry[...]
            dv_ref[...] = dv_carry[...]

    last = nb - 1
    cur = lambda n: pl.BlockSpec((ATTN_BLOCK, n), lambda i: (jnp.minimum(i, last), 0))
    prev = lambda n: pl.BlockSpec((ATTN_BLOCK, n), lambda i: (jnp.clip(i - 1, 0, last), 0))
    late = lambda n: pl.BlockSpec((ATTN_BLOCK, n), lambda i: (i, 0))
    dq, dk_late, dv_late, dsinks = pl.pallas_call(
        body, name="attn_bwd", grid=(nb + 1,),
        in_specs=[pl.BlockSpec(memory_space=pltpu.SMEM), cur(Q_DIM), cur(KV_DIM), prev(KV_DIM), cur(KV_DIM), prev(KV_DIM),
                  cur(Q_DIM), pl.BlockSpec((N_Q_HEADS, ATTN_BLOCK), lambda i: (0, jnp.minimum(i, last))), cur(Q_DIM)],
        out_specs=[cur(Q_DIM), late(KV_DIM), late(KV_DIM), _whole((1, LANES))],
        out_shape=[_sds((t, Q_DIM)), _sds((t + ATTN_BLOCK, KV_DIM)), _sds((t + ATTN_BLOCK, KV_DIM)), _sds((1, LANES))],
        scratch_shapes=[pltpu.VMEM((ATTN_BLOCK, KV_DIM), F32), pltpu.VMEM((ATTN_BLOCK, KV_DIM), F32)],
        compiler_params=_params(("arbitrary",)),
    )(sinks, q, k, k, v, v, o, lse, do)
    return dq, dk_late[ATTN_BLOCK:], dv_late[ATTN_BLOCK:], dsinks


def attn_out_fwd(x, o, gate, w_out):
    t = x.shape[0]
    tm = min(ROW_TILE_FWD, t)

    def body(x_ref, o_ref, gate_ref, w_ref, xo_ref):
        xo_ref[...] = x_ref[...] + _mm(o_ref[...] * _silu(gate_ref[...]), w_ref[...])

    row = _rows(tm, D_MODEL)
    return pl.pallas_call(
        body, name="attn_out_fwd", grid=(t // tm,),
        in_specs=[row, row, row, _whole((Q_DIM, D_MODEL))], out_specs=row, out_shape=_sds((t, D_MODEL)),
        compiler_params=_params(("parallel",)),
    )(x, o, gate, w_out)


def attn_out_bwd(dxo, o, gate, w_out, token=None):
    t = dxo.shape[0]
    tm = min(ROW_TILE_BWD, t)
    extra, extra_specs = _after(token)

    def body(dxo_ref, o_ref, gate_ref, wt_ref, *rest):
        do_ref, dgate_ref, dw_ref = rest[-3:]

        @pl.when(pl.program_id(0) == 0)
        def _():
            dw_ref[...] = jnp.zeros_like(dw_ref)

        dxo = dxo_ref[...]
        o = o_ref[...]
        gate = gate_ref[...]
        sgate, sgate_grad = _silu_and_grad(gate)
        da = _mm_nt(dxo, wt_ref[...])
        dw_ref[...] += _mm_tn(o * sgate, dxo)
        do_ref[...] = da * sgate
        dgate_ref[...] = da * o * sgate_grad

    row = _rows(tm, D_MODEL)
    mat = _whole((Q_DIM, D_MODEL))
    return pl.pallas_call(
        body, name="attn_out_bwd", grid=(t // tm,),
        in_specs=[row, row, row, mat] + extra_specs, out_specs=[row, row, mat],
        out_shape=[_sds((t, Q_DIM)), _sds((t, BRANCH)), _sds((Q_DIM, D_MODEL))],
        compiler_params=_params(("arbitrary",)),
    )(dxo, o, gate, w_out, *extra)


def attn_proj_bwd(x, norm, dxo, dq, dk, dv, dgate, cos2, sin2, w_in_t):
    t = x.shape[0]
    tm = min(ROW_TILE_BWD, t)

    def body(x_ref, g_ref, dxo_ref, dq_ref, dk_ref, dv_ref, dgate_ref, cos_ref, sin_ref, wt_ref, dx_ref, dw_ref, dg_ref):
        @pl.when(pl.program_id(0) == 0)
        def _():
            dw_ref[...] = jnp.zeros_like(dw_ref)
            dg_ref[...] = jnp.zeros_like(dg_ref)

        g = g_ref[...]
        r, xhat, h = _rms(x_ref[...], g)
        cs = cos_ref[...]
        sn = sin_ref[...]
        dqr = dq_ref[...]
        dkr = dk_ref[...]
        dq = dqr * _tile_lanes(cs, Q_DIM // LANES) + _swap_half_heads(dqr * _tile_lanes(sn, Q_DIM // LANES))
        dk = dkr * cs + _swap_half_heads(dkr * sn)
        dproj = jnp.concatenate([dq, dk, dv_ref[...], dgate_ref[...]], axis=1)
        dh = _mm(dproj, wt_ref[...])
        dw_ref[...] += _mm_tn(dproj, h)
        dx, dg = _rms_bwd(dh, g, r, xhat)
        dg_ref[...] += dg
        dx_ref[...] = dxo_ref[...] + dx

    row = _rows(tm, D_MODEL)
    vec = _whole((1, D_MODEL))
    return pl.pallas_call(
        body, name="attn_proj_bwd", grid=(t // tm,),
        in_specs=[row, vec, row, _rows(tm, Q_DIM), _rows(tm, KV_DIM), _rows(tm, KV_DIM), _rows(tm, BRANCH),
                  _rows(tm, LANES), _rows(tm, LANES), _whole((ATTN_N, D_MODEL))],
        out_specs=[row, _whole((ATTN_N, D_MODEL)), vec],
        out_shape=[_sds((t, D_MODEL)), _sds((ATTN_N, D_MODEL)), _sds((1, D_MODEL))],
        compiler_params=_params(("arbitrary",)),
    )(x, norm, dxo, dq, dk, dv, dgate, cos2, sin2, w_in_t)


def attn_out_loss(x, o, gate, w_out, norm, target):
    t = x.shape[0]
    tm = min(ROW_TILE_FWD, t)

    def body(x_ref, o_ref, gate_ref, w_ref, g_ref, tgt_ref, loss_ref, dx_ref, dg_ref):
        @pl.when(pl.program_id(0) == 0)
        def _():
            loss_ref[...] = jnp.zeros_like(loss_ref)
            dg_ref[...] = jnp.zeros_like(dg_ref)

        out = x_ref[...] + _mm(o_ref[...] * _silu(gate_ref[...]), w_ref[...])
        g = g_ref[...]
        r, xhat, y = _rms(out, g)
        err = y - tgt_ref[...]
        loss_ref[...] += 0.5 * jnp.sum(jnp.mean(err * err, axis=-1, keepdims=True), axis=0, keepdims=True)
        dx, dg = _rms_bwd(err * (1.0 / D_MODEL), g, r, xhat)
        dg_ref[...] += dg
        dx_ref[...] = dx

    row = _rows(tm, D_MODEL)
    vec = _whole((1, D_MODEL))
    return pl.pallas_call(
        body, name="attn_out_loss", grid=(t // tm,),
        in_specs=[row, row, row, _whole((Q_DIM, D_MODEL)), vec, row], out_specs=[_whole((1, 1)), row, vec],
        out_shape=[_sds((1, 1)), _sds((t, D_MODEL)), _sds((1, D_MODEL))],
        compiler_params=_params(("arbitrary",)),
    )(x, o, gate, w_out, norm, target)


OCT_TILE = pl.BlockSpec((None, LANES, LANES), lambda b: (b, 0, 0))
N_LAGS = S5_CHUNK + 1


def _cmul(ar, ai, br, bi):
    return ar * br - ai * bi, ar * bi + ai * br


def _cmul_conj(ar, ai, br, bi):
    return ar * br + ai * bi, ar * bi - ai * br


def _mm_f32(a, b, dims):
    return lax.dot_general(a, b, (dims, ((), ())), precision=lax.Precision.HIGH, preferred_element_type=F32)


def _s5_discretise(ar, ai, ls, br, bi):
    dt = jnp.exp(ls)
    xr = ar * dt
    xi = ai * dt
    mag = jnp.exp(xr)
    first = (mag * jnp.cos(xi), mag * jnp.sin(xi))
    powers = [(jnp.ones_like(xr), jnp.zeros_like(xr)), first]
    for _ in range(2, N_LAGS):
        powers.append(_cmul(*powers[-1], *first))
    den = ar * ar + ai * ai
    nr = powers[1][0] - 1.0
    ni = powers[1][1]
    fr = (nr * ar + ni * ai) / den
    fi = (ni * ar - nr * ai) / den
    bbr, bbi = _cmul(fr, fi, br, bi)
    return dt, powers, (fr, fi), (bbr, bbi), den


def _same_group_tile():
    row = lax.broadcasted_iota(jnp.int32, (LANES, LANES), 0)
    col = lax.broadcasted_iota(jnp.int32, (LANES, LANES), 1)
    return (row // SSM_GROUP) == (col // SSM_GROUP)


def _first_copy_lanes():
    return lax.broadcasted_iota(jnp.int32, (LANES, LANES), 1) < SSM_STATE


def s5_param_fwd(tiles, token=None):
    extra, extra_specs = _after(token)

    def body(ar_ref, ai_ref, ls_ref, br_ref, bi_ref, cr_ref, ci_ref, *rest):
        kd_ref, wsr_ref, wsi_ref, wor_ref, woi_ref, pr_ref, pi_ref = rest[-7:]
        cr = cr_ref[...]
        ci = ci_ref[...]
        _, powers, _, (bbr, bbi), _ = _s5_discretise(ar_ref[...], ai_ref[...], ls_ref[...], br_ref[...], bi_ref[...])
        once = _first_copy_lanes()
        crm = jnp.where(once, cr, 0.0)
        cim = jnp.where(once, ci, 0.0)
        same = _same_group_tile()
        for lag in range(S5_CHUNK):
            er, ei = powers[lag]
            xr, xi = _cmul(er, ei, bbr, bbi)
            rows = pl.ds((S5_CHUNK - 1 - lag) * LANES, LANES)
            wsr_ref[rows, :] = xr
            wsi_ref[rows, :] = xi
        k = _mm_f32(wsr_ref[...], crm, ((1,), (1,))) - _mm_f32(wsi_ref[...], cim, ((1,), (1,)))
        for lag in range(S5_CHUNK):
            kd_ref[lag] = jnp.where(same, k[(S5_CHUNK - 1 - lag) * LANES:(S5_CHUNK - lag) * LANES], 0.0)
        for t in range(S5_CHUNK):
            er, ei = powers[t + 1]
            zr, zi = _cmul(er, ei, cr, ci)
            wor_ref[pl.ds(t * LANES, LANES), :] = zr
            woi_ref[pl.ds(t * LANES, LANES), :] = -zi
        pr_ref[...] = powers[S5_CHUNK][0]
        pi_ref[...] = powers[S5_CHUNK][1]

    return pl.pallas_call(
        body, name="s5_param_fwd", grid=(S5_OCTETS,),
        in_specs=[OCT_TILE] * 7 + [ANY] * len(extra),
        out_specs=[OCT_KD, OCT_W, OCT_W, OCT_W, OCT_W, OCT_TILE, OCT_TILE],
        out_shape=[_sds((S5_OCTETS, S5_CHUNK, LANES, LANES))] + [_sds((S5_OCTETS, S5_OCT_IN, LANES))] * 4
                  + [_sds((S5_OCTETS, LANES, LANES))] * 2,
        compiler_params=_params(("parallel",)),
    )(*tiles, *extra)


def s5_param_bwd(tiles, dkd, dws_re, dws_im, dwo_re, dwo_im, dp_re, dp_im):
    def body(ar_ref, ai_ref, ls_ref, br_ref, bi_ref, cr_ref, ci_ref, dkd_ref, dwsr_ref, dwsi_ref, dwor_ref, dwoi_ref, dpr_ref, dpi_ref,
             dar_ref, dai_ref, dls_ref, dbr_ref, dbi_ref, dcr_ref, dci_ref):
        ar = ar_ref[...]
        ai = ai_ref[...]
        br = br_ref[...]
        bi = bi_ref[...]
        cr = cr_ref[...]
        ci = ci_ref[...]
        dt, powers, (fr, fi), (bbr, bbi), den = _s5_discretise(ar, ai, ls_ref[...], br, bi)
        once = _first_copy_lanes()
        crm = jnp.where(once, cr, 0.0)
        cim = jnp.where(once, ci, 0.0)
        same = _same_group_tile()
        zero = jnp.zeros((LANES, LANES), F32)
        dpow = [[zero, zero] for _ in range(N_LAGS)]
        dbbr, dbbi = zero, zero
        by_step = [S5_CHUNK - 1 - s for s in range(S5_CHUNK)]
        x_all = [_cmul(*powers[lag], bbr, bbi) for lag in by_step]
        xr_all = jnp.concatenate([x[0] for x in x_all], axis=0)
        xi_all = jnp.concatenate([x[1] for x in x_all], axis=0)
        g_all = jnp.concatenate([jnp.where(same, dkd_ref[lag], 0.0) for lag in by_step], axis=0)
        dxr_all = dwsr_ref[...] + _mm_f32(g_all, crm, ((1,), (0,)))
        dxi_all = dwsi_ref[...] - _mm_f32(g_all, cim, ((1,), (0,)))
        dcr = jnp.where(once, _mm_f32(g_all, xr_all, ((0,), (0,))), 0.0)
        dci = -jnp.where(once, _mm_f32(g_all, xi_all, ((0,), (0,))), 0.0)
        for lag in range(S5_CHUNK):
            er, ei = powers[lag]
            rows = slice((S5_CHUNK - 1 - lag) * LANES, (S5_CHUNK - lag) * LANES)
            dxr = dxr_all[rows]
            dxi = dxi_all[rows]
            a, b = _cmul_conj(bbr, bbi, dxr, dxi)
            dpow[lag][0] = dpow[lag][0] + a
            dpow[lag][1] = dpow[lag][1] + b
            a, b = _cmul_conj(er, ei, dxr, dxi)
            dbbr = dbbr + a
            dbbi = dbbi + b
        for t in range(S5_CHUNK):
            er, ei = powers[t + 1]
            dzr = dwor_ref[pl.ds(t * LANES, LANES), :]
            dzi = -dwoi_ref[pl.ds(t * LANES, LANES), :]
            a, b = _cmul_conj(cr, ci, dzr, dzi)
            dpow[t + 1][0] = dpow[t + 1][0] + a
            dpow[t + 1][1] = dpow[t + 1][1] + b
            a, b = _cmul_conj(er, ei, dzr, dzi)
            dcr = dcr + a
            dci = dci + b
        dpow[S5_CHUNK][0] = dpow[S5_CHUNK][0] + dpr_ref[...]
        dpow[S5_CHUNK][1] = dpow[S5_CHUNK][1] + dpi_ref[...]
        dfr, dfi = _cmul_conj(br, bi, dbbr, dbbi)
        dbr, dbi = _cmul_conj(fr, fi, dbbr, dbbi)
        dnr, dni = _cmul(ar / den, ai / den, dfr, dfi)
        qr = (fr * ar + fi * ai) / den
        qi = (fi * ar - fr * ai) / den
        dlr, dli = _cmul(-qr, qi, dfr, dfi)
        dpow[1][0] = dpow[1][0] + dnr
        dpow[1][1] = dpow[1][1] + dni
        dxr, dxi = zero, zero
        for lag in range(1, N_LAGS):
            a, b = _cmul_conj(powers[lag][0], powers[lag][1], dpow[lag][0], dpow[lag][1])
            dxr = dxr + lag * a
            dxi = dxi + lag * b
        dar_ref[...] = dlr + dt * dxr
        dai_ref[...] = dli + dt * dxi
        dls_ref[...] = dt * (ar * dxr + ai * dxi)
        dbr_ref[...] = dbr
        dbi_ref[...] = dbi
        dcr_ref[...] = dcr
        dci_ref[...] = dci

    return pl.pallas_call(
        body, name="s5_param_bwd", grid=(S5_OCTETS,),
        in_specs=[OCT_TILE] * 7 + [OCT_KD, OCT_W, OCT_W, OCT_W, OCT_W, OCT_TILE, OCT_TILE], out_specs=[OCT_TILE] * 7,
        out_shape=[_sds((S5_OCTETS, LANES, LANES))] * 7,
        compiler_params=_params(("parallel",)),
    )(*tiles, dkd, dws_re, dws_im, dwo_re, dwo_im, dp_re, dp_im)


def _doubled(v):
    return jnp.concatenate([v, v], axis=-1)


def _s5_param_tiles(a_re, a_im, log_step, b_re, b_im, c_re, c_im):
    def per_group(a):
        return _doubled(jnp.broadcast_to(a.reshape(S5_OCTETS, S5_OCT, 1, SSM_STATE),
                                         (S5_OCTETS, S5_OCT, SSM_GROUP, SSM_STATE)).reshape(S5_OCTETS, LANES, SSM_STATE))

    ls = jnp.broadcast_to(log_step.reshape(S5_OCTETS, S5_OCT, 1, 1), (S5_OCTETS, S5_OCT, SSM_GROUP, LANES)).reshape(S5_OCTETS, LANES, LANES)
    bt = lambda b: _doubled(b.transpose(0, 2, 1).reshape(S5_OCTETS, LANES, SSM_STATE))
    ct = lambda c: _doubled(c.reshape(S5_OCTETS, LANES, SSM_STATE))
    return [per_group(a_re), per_group(a_im), ls, bt(b_re), bt(b_im), ct(c_re), ct(c_im)]


def _s5_param_grads(dtiles):
    dar, dai, dls, dbr, dbi, dcr, dci = dtiles
    halves = lambda d: d[..., :SSM_STATE] + d[..., SSM_STATE:]
    per_group = lambda d: halves(d).reshape(SSM_GROUPS, SSM_GROUP, SSM_STATE).sum(axis=1)
    per_row = lambda d: halves(d).reshape(SSM_GROUPS, SSM_GROUP, SSM_STATE)
    return (per_group(dar), per_group(dai), dls.reshape(SSM_GROUPS, SSM_GROUP * LANES).sum(axis=1),
            per_row(dbr).transpose(0, 2, 1), per_row(dbi).transpose(0, 2, 1), per_row(dcr), per_row(dci))


def _group_power_rows(tile):
    return tile[:, ::SSM_GROUP, :SSM_STATE].reshape(1, S5_STATES)


def _group_power_tiles(row):
    t = jnp.pad(row.reshape(S5_OCTETS, S5_OCT, 1, SSM_STATE), ((0, 0), (0, 0), (0, SSM_GROUP - 1), (0, LANES - SSM_STATE)))
    return t.reshape(S5_OCTETS, LANES, LANES)


def _rope_tables(t):
    pos = jnp.arange(t, dtype=F32)
    inv_freq = ROPE_THETA ** (-jnp.arange(0, HEAD_DIM, 2, dtype=F32) / HEAD_DIM)
    ang = pos[:, None] * inv_freq[None, :]
    cos = jnp.cos(ang)
    sin = jnp.sin(ang)
    cos64 = jnp.concatenate([cos, cos], axis=1)
    sin64 = jnp.concatenate([-sin, sin], axis=1)
    return jnp.concatenate([cos64, cos64], axis=1), jnp.concatenate([sin64, sin64], axis=1)


def _row(v):
    return v.reshape(1, -1)


def _s5_matrices(w, token=None):
    tiles = _s5_param_tiles(w["a_re"], w["a_im"], w["log_step"], w["b_re"], w["b_im"], w["c_re"], w["c_im"])
    kd, ws_re, ws_im, wo_re, wo_im, p_re, p_im = s5_param_fwd(tiles, token)
    return tiles, dict(kd=kd, ws_re=ws_re, ws_im=ws_im, wo_re=wo_re, wo_im=wo_im, a_re=_group_power_rows(p_re), a_im=_group_power_rows(p_im))


def _ssm_forward(x, w):
    tiles, mats = w["s5"] if "s5" in w else _s5_matrices(w)
    u, gate = ssm_proj_fwd(x, _row(w["norm"]), w["w_in"])
    s_re, s_im = s5_chunk_states(u, mats["ws_re"], mats["ws_im"])
    h_re, h_im = s5_scan_fwd(s_re, s_im, mats["a_re"], mats["a_im"])
    y_scan = s5_outputs(u, h_re, h_im, mats["kd"], mats["wo_re"], mats["wo_im"])
    y, g2, x_new = ssm_mix_fwd(x, u, gate, y_scan, _row(w["d"]), w["w_glu"], _row(w["b_glu"]), w["w_out"])
    saved = dict(x=x, u=u, gate=gate, y=y, g2=g2, h_re=h_re, h_im=h_im, mats=mats, tiles=tiles)
    return x_new, saved


def _ssm_backward(dxo, w, s, token=None):
    dy, dgate, dw_out, dw_glu, db_glu, dd = ssm_mix_bwd(dxo, s["u"], s["gate"], s["y"], s["g2"], w["w_glu"], w["w_out"], token)
    mats = s["mats"]
    dh_re, dh_im = s5_state_grads(dy, mats["wo_re"], mats["wo_im"])
    ds_re, ds_im, da_re, da_im = s5_scan_bwd(dh_re, dh_im, s["h_re"], s["h_im"], mats["a_re"], mats["a_im"])
    du_scan = s5_input_grads(dy, ds_re, ds_im, mats["kd"], mats["ws_re"], mats["ws_im"])
    dkd, dws_re, dws_im, dwo_re, dwo_im = s5_weight_grads(s["u"], dy, s["h_re"], s["h_im"], ds_re, ds_im)
    dparams = _s5_param_grads(s5_param_bwd(s["tiles"], dkd, dws_re, dws_im, dwo_re, dwo_im,
                                           _group_power_tiles(da_re), _group_power_tiles(da_im)))
    dx, dw_in, dnorm = ssm_proj_bwd(s["x"], _row(w["norm"]), dxo, dy, du_scan, dgate, _row(w["d"]), w["w_in"])
    grads = dict(norm=dnorm, w_in=dw_in, d=dd, w_glu=dw_glu, b_glu=db_glu, w_out=dw_out)
    for name, val in zip(("a_re", "a_im", "log_step", "b_re", "b_im", "c_re", "c_im"), dparams):
        grads[name] = val
    return dx, grads


def _attn_forward(x, w, cos2, sin2, loss_head=None):
    q, k, v, gate = attn_proj_fwd(x, _row(w["norm"]), w["w_in"], cos2, sin2)
    o, lse = attn_fwd(q, k, v, w["sinks"])
    if loss_head is None:
        result = attn_out_fwd(x, o, gate, w["w_out"])
    else:
        result = attn_out_loss(x, o, gate, w["w_out"], _row(loss_head[0]), loss_head[1])
    return result, dict(x=x, q=q, k=k, v=v, gate=gate, o=o, lse=lse)


def _attn_backward(dxo, w, s, cos2, sin2, token=None):
    do, dgate, dw_out = attn_out_bwd(dxo, s["o"], s["gate"], w["w_out"], token)
    dq, dk, dv, dsinks = attn_bwd(s["q"], s["k"], s["v"], w["sinks"], s["o"], s["lse"], do)
    dx, dw_in, dnorm = attn_proj_bwd(s["x"], _row(w["norm"]), dxo, dq, dk, dv, dgate, cos2, sin2, w["w_in"])
    return dx, dict(norm=dnorm, w_in=dw_in, sinks=dsinks[0, :N_Q_HEADS], w_out=dw_out)


class _NoExchanges:
    def __init__(self, layers):
        self.layers = layers

    def layer(self, i, x):
        return self.layers[i]

    def layer_done(self, i, grads, dx):
        return None


def _sequence_step(x, target, final_norm, hooks, depth=4):
    cos2, sin2 = _rope_tables(x.shape[0])
    saved, layers = [], []
    for i in range(depth):
        w = hooks.layer(i, x)
        layers.append(w)
        if i % 2 == 0:
            x, s = _ssm_forward(x, w)
        else:
            x, s = _attn_forward(x, w, cos2, sin2, (final_norm, target) if i == depth - 1 else None)
        saved.append(s)
    loss, dx, dfinal = x
    grads = {"final_norm": dfinal}
    token = None
    for i in reversed(range(depth)):
        if i % 2 == 0:
            dx, g = _ssm_backward(dx, layers[i], saved[i], token)
        else:
            dx, g = _attn_backward(dx, layers[i], saved[i], cos2, sin2, token)
        g = {"l%d_%s" % (i, name): val for name, val in g.items()}
        grads.update(g)
        token = hooks.layer_done(i, g, dx)
    return loss[0, 0], dx, grads


ANY = pl.BlockSpec(memory_space=pl.ANY)


def _place():
    return lax.axis_index("x"), lax.axis_index("y"), lax.axis_index("c")


def _other_chips(x, y):
    return [(1 - x, y), (x, 1 - y), (1 - x, 1 - y)]


class _StagedCopies:
    def __init__(self, bufs, load_sems, store_sems):
        self.bufs, self.load_sems, self.store_sems = bufs, load_sems, store_sems
        self.loads, self.stores = [], []

    def load(self, i, src):
        cp = pltpu.make_async_copy(src, self.bufs[i], self.load_sems.at[i])
        cp.start()
        self.loads.append(cp)

    def store(self, i, dst):
        self.loads[i].wait()
        cp = pltpu.make_async_copy(self.bufs[i], dst, self.store_sems.at[i])
        cp.start()
        self.stores.append(cp)

    def finish(self):
        for cp in self.stores:
            cp.wait()


def _staging(blocks):
    n = len(blocks)
    return [pltpu.VMEM(b.shape, b.dtype) for b in blocks] + [pltpu.SemaphoreType.DMA((n,)), pltpu.SemaphoreType.DMA((n,))]


def exchange_halves_with_sibling(grads):
    n = len(grads)

    def body(*refs):
        ins, outs = refs[:n], refs[n:2 * n]
        send_sems, recv_sems = refs[2 * n:]
        x, y, c = _place()
        copies = []
        for i in range(n):
            half = ins[i].shape[1] // 2
            src = ins[i].at[:, pl.ds((1 - c) * half, half), :]
            cp = pltpu.make_async_remote_copy(src_ref=src, dst_ref=outs[i], send_sem=send_sems.at[i], recv_sem=recv_sems.at[i],
                                              device_id=(x, y, 1 - c), device_id_type=MESH)
            cp.start()
            copies.append(cp)
        for cp in copies:
            cp.wait()

    return pl.pallas_call(
        body, name="exchange_halves_with_sibling",
        in_specs=[ANY] * n, out_specs=[ANY] * n,
        out_shape=[_sds((g.shape[0], g.shape[1] // 2, g.shape[2])) for g in grads],
        scratch_shapes=[pltpu.SemaphoreType.DMA((n,)), pltpu.SemaphoreType.DMA((n,))],
    )(*grads)


def swap_halves_with_sibling(pieces):
    n = len(pieces)

    def body(*refs):
        ins, outs = refs[:n], refs[n:2 * n]
        send_sems, recv_sems = refs[2 * n:2 * n + 2]
        own = _StagedCopies(refs[2 * n + 2:3 * n + 2], *refs[3 * n + 2:])
        x, y, c = _place()
        for i in range(n):
            own.load(i, ins[i])
        swaps = []
        for i in range(n):
            cp = pltpu.make_async_remote_copy(src_ref=ins[i], dst_ref=outs[i].at[c], send_sem=send_sems.at[i], recv_sem=recv_sems.at[i],
                                              device_id=(x, y, 1 - c), device_id_type=MESH)
            cp.start()
            swaps.append(cp)
        for i in range(n):
            own.store(i, outs[i].at[c])
        for i in range(n):
            pltpu.make_async_remote_copy(src_ref=ins[i], dst_ref=outs[i].at[1 - c], send_sem=send_sems.at[i], recv_sem=recv_sems.at[i],
                                         device_id=(x, y, 1 - c), device_id_type=MESH).wait_recv()
        for cp in swaps:
            cp.wait_send()
        own.finish()

    return pl.pallas_call(
        body, name="swap_halves_with_sibling",
        in_specs=[ANY] * n, out_specs=[ANY] * n,
        out_shape=[_sds((2,) + p.shape) for p in pieces],
        scratch_shapes=[pltpu.SemaphoreType.DMA((n,)), pltpu.SemaphoreType.DMA((n,))] + _staging(pieces),
        compiler_params=_params(),
    )(*pieces)


IN_HBM = pl.BlockSpec(memory_space=pltpu.HBM)
SEMAPHORES = pl.BlockSpec(memory_space=pltpu.SEMAPHORE)
DATAFLOW = pltpu.SideEffectType.DATAFLOW_SIDE_EFFECTING


def _hbm(a):
    return pltpu.with_memory_space_constraint(a, pltpu.HBM)


def place_own_blocks(shards):
    n = len(shards)

    def body(*refs):
        ins, outs = refs[:n], refs[n:2 * n]
        own = _StagedCopies(refs[2 * n:3 * n], *refs[3 * n:])
        x, y, _ = _place()
        for i in range(n):
            own.load(i, ins[i])
        for i in range(n):
            own.store(i, outs[i].at[2 * x + y])
        own.finish()

    return pl.pallas_call(
        body, name="place_own_blocks", in_specs=[ANY] * n, out_specs=[ANY] * n,
        out_shape=[_sds((4,) + s.shape, s.dtype) for s in shards],
        scratch_shapes=_staging(shards), compiler_params=_params(),
    )(*shards)


def _block_to_send(ref, chip, per_target):
    if not per_target:
        return ref
    return ref.at[chip] if ref.shape[0] == 4 else ref.at[0]


def start_sends_to_chips(name, sources, landings, per_target, after):
    n = len(sources)
    n_sems = 2 * 3 * n

    def body(*refs):
        srcs = refs[:n]
        sems = refs[2 * n + 1:2 * n + 1 + n_sems]
        lands = refs[2 * n + 1 + n_sems:3 * n + 1 + n_sems]
        token = refs[3 * n + 1 + n_sems]
        x, y, c = _place()
        me = 2 * x + y
        for i in range(n):
            for k, (tx, ty) in enumerate(_other_chips(x, y)):
                src = _block_to_send(srcs[i], 2 * tx + ty, per_target)
                pltpu.make_async_remote_copy(src_ref=src, dst_ref=lands[i].at[me], send_sem=sems[2 * (3 * i + k)], recv_sem=sems[2 * (3 * i + k) + 1],
                                             device_id=(tx, ty, c), device_id_type=MESH).start()
        token[...] = jnp.zeros_like(token)

    outs = pl.pallas_call(
        body, name=name,
        in_specs=[IN_HBM] * (2 * n) + [ANY],
        out_specs=[SEMAPHORES] * n_sems + [IN_HBM] * n + [pl.BlockSpec(memory_space=pltpu.VMEM)],
        out_shape=[pltpu.SemaphoreType.DMA(())] * n_sems + [pltpu.HBM(l.shape, l.dtype) for l in landings] + [_sds(TOKEN_SHAPE)],
        input_output_aliases={n + i: n_sems + i for i in range(n)},
        compiler_params=pltpu.CompilerParams(has_side_effects=DATAFLOW),
    )(*[_hbm(s) for s in sources], *[_hbm(l) for l in landings], after)
    return list(outs[:n_sems]), list(outs[n_sems:n_sems + n]), outs[n_sems + n]


def wait_sends_to_chips(name, sources, landings, sems, per_target, after):
    n = len(sources)
    n_sems = len(sems)

    def body(*refs):
        srcs = refs[:n]
        sem_refs = refs[2 * n:2 * n + n_sems]
        lands = refs[2 * n + n_sems + 1:]
        x, y, c = _place()
        me = 2 * x + y
        for i in range(n):
            for k, (tx, ty) in enumerate(_other_chips(x, y)):
                src = _block_to_send(srcs[i], me, per_target)
                cp = pltpu.make_async_remote_copy(src_ref=src, dst_ref=lands[i].at[2 * tx + ty], send_sem=sem_refs[2 * (3 * i + k)],
                                                  recv_sem=sem_refs[2 * (3 * i + k) + 1], device_id=(tx, ty, c), device_id_type=MESH)
                cp.wait_send()
                cp.wait_recv()

    return pl.pallas_call(
        body, name=name,
        in_specs=[IN_HBM] * (2 * n) + [SEMAPHORES] * n_sems + [ANY],
        out_specs=[IN_HBM] * n,
        out_shape=[pltpu.HBM(l.shape, l.dtype) for l in landings],
        input_output_aliases={n + i: i for i in range(n)},
        compiler_params=pltpu.CompilerParams(has_side_effects=DATAFLOW),
    )(*[_hbm(s) for s in sources], *landings, *sems, after)


def _row_tile(rows, cols):
    tm = rows
    while tm * cols * 4 > (2 << 20) and tm % 16 == 0:
        tm //= 2
    return tm


def add_pairs(half, a_list, b_list, out_dtypes, copies=1):
    n = len(a_list)
    nb = a_list[0].shape[0]

    def body(half_ref, *refs):
        for i in range(n):
            total = (refs[i][...] + refs[n + i][...]).astype(out_dtypes[i])
            for o_ref in refs[2 * n + i * copies:2 * n + (i + 1) * copies]:
                o_ref[...] = total

    halves = [pl.BlockSpec((None,) + b.shape[1:], lambda j, h: (j, h[0], 0)) for b in b_list]
    whole = [pl.BlockSpec((None,) + b.shape[1:], lambda j, h: (j, 0, 0)) for b in b_list]
    outs = pl.pallas_call(
        body, name="add_pairs",
        grid_spec=pltpu.PrefetchScalarGridSpec(num_scalar_prefetch=1, grid=(nb,), in_specs=halves + whole,
                                               out_specs=[s for s in whole for _ in range(copies)]),
        out_shape=[_sds(b.shape, dt) for b, dt in zip(b_list, out_dtypes) for _ in range(copies)],
        compiler_params=_params(("parallel",)),
    )(half, *a_list, *b_list)
    return [tuple(outs[i * copies:(i + 1) * copies]) for i in range(n)]


def sum_fours(arrays, token=None):
    n = len(arrays)
    extra, extra_specs = _after(token)

    def body(*refs):
        outs = refs[-n:]
        for a_ref, o_ref in zip(refs[:n], outs):
            o_ref[...] = ((a_ref[0].astype(F32) + a_ref[1].astype(F32)) + a_ref[2].astype(F32)) + a_ref[3].astype(F32)

    return pl.pallas_call(
        body, name="sum_fours", grid=(2,),
        in_specs=[pl.BlockSpec((4, a.shape[1] // 2, a.shape[2]), lambda i: (0, i, 0)) for a in arrays] + extra_specs,
        out_specs=[pl.BlockSpec((a.shape[1] // 2, a.shape[2]), lambda i: (i, 0)) for a in arrays],
        out_shape=[_sds(a.shape[1:]) for a in arrays], compiler_params=_params(("parallel",)),
    )(*arrays, *extra)


def _adamw_update(w_ref, g_ref, m_ref, v_ref, d_ref, nm_ref, nv_ref):
    g = g_ref[...]
    nm = ADAM_B1 * m_ref[...] + (1.0 - ADAM_B1) * g
    nv = ADAM_B2 * v_ref[...] + (1.0 - ADAM_B2) * (g * g)
    d_ref[...] = -ADAM_LR * ((nm / (1.0 - ADAM_B1 ** ADAM_STEP)) / (jnp.sqrt(nv / (1.0 - ADAM_B2 ** ADAM_STEP)) + ADAM_EPS) + ADAM_WD * w_ref[...])
    nm_ref[...] = nm
    nv_ref[...] = nv


def adamw(w, g, m, v):
    rows, cols = w.shape
    tm = _row_tile(rows, cols)

    def body(*refs):
        _adamw_update(*refs)

    spec = pl.BlockSpec((tm, cols), lambda i: (i, 0))
    return pl.pallas_call(
        body, name="adamw", grid=(rows // tm,), in_specs=[spec] * 4, out_specs=[spec] * 3,
        out_shape=[_sds(w.shape)] * 3, compiler_params=_params(("parallel",)),
    )(w, g, m, v)


def adamw_small(ws, gs, ms, vs, slabs=None):
    n = len(ws)

    def body(*refs):
        for i in range(n):
            _adamw_update(refs[i], refs[n + i], refs[2 * n + i], refs[3 * n + i], refs[4 * n + i], refs[5 * n + i], refs[6 * n + i])

    if slabs is None:
        grid = ()
        specs = [pl.BlockSpec(memory_space=pltpu.VMEM)] * n
    else:
        grid = (slabs,)
        specs = [pl.BlockSpec((w.shape[0] // slabs,) + w.shape[1:], lambda i: (i, 0, 0)) for w in ws]
    outs = pl.pallas_call(
        body, name="adamw_small", grid=grid, in_specs=specs * 4, out_specs=specs * 3,
        out_shape=[_sds(w.shape) for w in ws] * 3, compiler_params=_params(("parallel",) if slabs else None),
    )(*ws, *gs, *ms, *vs)
    return outs[:n], outs[n:2 * n], outs[2 * n:]


PACK_TILE = 8 * LANES
PACK_PIECES = 8
PACK_ALIGN = PACK_PIECES * 16


def _pack_small(values, scalar=None):
    parts = []
    for name in SMALL_NAMES:
        flat = values[name].reshape(-1)
        pad = (-flat.shape[0]) % PACK_TILE
        if pad:
            flat = jnp.concatenate([flat, jnp.zeros((pad,), F32)])
        parts.append(flat.reshape(-1, LANES))
    rows = sum(p.shape[0] for p in parts) + 8
    parts.append(jnp.zeros(((-rows) % PACK_ALIGN, LANES), F32))
    last = jnp.zeros((8, LANES), F32)
    parts.append(last if scalar is None else jnp.broadcast_to(scalar.astype(F32), (8, LANES)))
    return jnp.concatenate(parts, axis=0)


def _unpack_small(pack, like):
    out = {}
    row = 0
    for name in SMALL_NAMES:
        size = math.prod(like[name].shape)
        rows = -(-size // PACK_TILE) * 8
        out[name] = pack[row:row + rows].reshape(-1)[:size].reshape(like[name].shape)
        row += rows
    return out


def _travels_transposed(name, shard):
    return name.endswith("w_in") and shard.shape[-1] % LANES != 0


def _to_blocks(name, full):
    if full.ndim == 3:
        return full
    return full.reshape(4, full.shape[0] // 4, full.shape[1])


def _from_blocks(name, stacked):
    if name.endswith("w_in") and stacked.shape[2] % LANES == 0 and stacked.shape[1] == D_MODEL:
        return stacked
    return stacked.reshape(4 * stacked.shape[1], stacked.shape[2])


def _layer_big_names(i):
    return [n for n in BIG_NAMES if n.startswith("l%d_" % i)]


class _OverlappedExchanges:
    def __init__(self, weights):
        self.weights = weights
        self.c = lax.axis_index("c")
        self.first = _layer_big_names(0)
        self.later = [n for n in BIG_NAMES if n not in self.first]
        shards = [weights[n].astype(MXU_DTYPE) for n in self.first + self.later]
        shards = [s.T if _travels_transposed(n, s) else s for n, s in zip(self.first + self.later, shards)]
        placed = place_own_blocks(shards)
        k = len(self.first)
        sems, stacks, token = start_sends_to_chips("gather_first_start", shards[:k], placed[:k], False, shards[0])
        self.gather_first = (shards[:k], sems, stacks)
        sems, stacks, token = start_sends_to_chips("gather_later_start", shards[k:], placed[k:], False, token)
        self.gather_later = (shards[k:], sems, stacks)
        self.s5 = {}
        for i in (0, 2):
            self.s5[i] = _s5_matrices({n: weights["l%d_%s" % (i, n)] for n in SSM_NAMES if "l%d_%s" % (i, n) in SMALL_NAMES}, token)
            token = self.s5[i][1]["kd"]
        self.full = {}
        self.in_flight = {}
        self.contributions = {}

    def layer(self, i, x):
        if i == 0:
            shards, sems, stacks = self.gather_first
            stacks = wait_sends_to_chips("gather_first_wait", shards, stacks, sems, False, self.s5[2][1]["kd"])
            self.full.update({n: _from_blocks(n, g) for n, g in zip(self.first, stacks)})
        if i == 1:
            shards, sems, stacks = self.gather_later
            stacks = wait_sends_to_chips("gather_later_wait", shards, stacks, sems, False, x)
            self.full.update({n: _from_blocks(n, g) for n, g in zip(self.later, stacks)})
        names = SSM_NAMES if i % 2 == 0 else ATTN_NAMES
        w = {n: self.full.get("l%d_%s" % (i, n), self.weights.get("l%d_%s" % (i, n))) for n in names}
        if i in self.s5:
            w["s5"] = self.s5[i]
        return w

    def chip_sums(self, names, grads, extra_blocks=(), copies=1):
        blocks = [_to_blocks(n, grads[n]) for n in names] + list(extra_blocks)
        from_sibling = exchange_halves_with_sibling(blocks)
        k = len(names)
        half = self.c.reshape(1).astype(jnp.int32)
        sums = add_pairs(half, blocks[:k], from_sibling[:k], [WIRE_DTYPE] * k, copies)
        if extra_blocks:
            sums += add_pairs(half, blocks[k:], from_sibling[k:], [F32] * len(extra_blocks), copies)
        return sums

    def layer_done(self, i, grads, dx):
        if i + 1 in self.in_flight:
            names, sums, sems, landings = self.in_flight.pop(i + 1)
            done = wait_sends_to_chips("scatter_wait_l%d" % (i + 1), sums, landings, sems, True, dx)
            self.contributions.update(zip(names, done))
        if i == 0:
            return None
        names = _layer_big_names(i)
        pairs = self.chip_sums(names, grads, copies=2)
        sums = [p[0] for p in pairs]
        sems, landings, token = start_sends_to_chips("scatter_start_l%d" % i, sums, [p[1] for p in pairs], True, sums[0])
        self.in_flight[i] = (names, sums, sems, landings)
        return token


def _train_step(x, loss_target, weights, moments_m, moments_v):
    hooks = _OverlappedExchanges(weights)
    loss, dx, grads = _sequence_step(x[0], loss_target[0], weights["final_norm"], hooks)
    small_pack = _pack_small({n: grads[n] for n in SMALL_NAMES}, scalar=loss)
    last = _layer_big_names(0)
    pairs = hooks.chip_sums(last, grads, extra_blocks=[small_pack[None]], copies=2)
    sums = [p[0] for p in pairs]
    landings = [p[1] for p in pairs[:-1]] + [jnp.broadcast_to(sums[-1], (4,) + sums[-1].shape[1:])]
    sems, landings, token = start_sends_to_chips("scatter_start_l0", sums, landings, True, sums[0])
    out_grad, out_delta, out_m, out_v = {}, {}, {}, {}

    def finish(names, arrays, token=None):
        shared = swap_halves_with_sibling(sum_fours(arrays, token))
        for n, s in zip(names, shared):
            if n == "small":
                return s.reshape(-1, LANES)
            out_grad[n] = s.reshape(2 * s.shape[1], s.shape[2])
            if _travels_transposed(n, weights[n]):
                out_grad[n] = out_grad[n].T
            out_delta[n], out_m[n], out_v[n] = adamw(weights[n], out_grad[n], moments_m[n], moments_v[n])

    others = [n for n in BIG_NAMES if n not in last]
    finish(others, [hooks.contributions[n] for n in others], token)
    arrived = wait_sends_to_chips("scatter_wait_l0", sums, landings, sems, True, out_v[others[-1]])
    small_grad_pack = finish(last + ["small"], arrived)
    loss = small_grad_pack[-8, 0]
    out_grad.update(_unpack_small(small_grad_pack, {n: weights[n] for n in SMALL_NAMES}))
    cubes = [n for n in SMALL_NAMES if weights[n].ndim == 3]
    for names, slabs in ((cubes, 8), ([n for n in SMALL_NAMES if n not in cubes], None)):
        deltas, new_ms, new_vs = adamw_small(*[[group[n] for n in names] for group in (weights, out_grad, moments_m, moments_v)], slabs=slabs)
        out_delta.update(zip(names, deltas))
        out_m.update(zip(names, new_ms))
        out_v.update(zip(names, new_vs))
    outs = [loss, dx[None]]
    for group in (out_grad, out_delta, out_m, out_v):
        outs.extend(group[n] for n in WEIGHT_NAMES)
    return tuple(outs)


def kernel(x, l0_norm, l0_w_in, l0_a_re, l0_a_im, l0_log_step, l0_b_re, l0_b_im, l0_c_re, l0_c_im, l0_d, l0_w_glu, l0_b_glu, l0_w_out, l1_norm, l1_w_in, l1_sinks, l1_w_out, l2_norm, l2_w_in, l2_a_re, l2_a_im, l2_log_step, l2_b_re, l2_b_im, l2_c_re, l2_c_im, l2_d, l2_w_glu, l2_b_glu, l2_w_out, l3_norm, l3_w_in, l3_sinks, l3_w_out, final_norm, loss_target, m_l0_norm, m_l0_w_in, m_l0_a_re, m_l0_a_im, m_l0_log_step, m_l0_b_re, m_l0_b_im, m_l0_c_re, m_l0_c_im, m_l0_d, m_l0_w_glu, m_l0_b_glu, m_l0_w_out, m_l1_norm, m_l1_w_in, m_l1_sinks, m_l1_w_out, m_l2_norm, m_l2_w_in, m_l2_a_re, m_l2_a_im, m_l2_log_step, m_l2_b_re, m_l2_b_im, m_l2_c_re, m_l2_c_im, m_l2_d, m_l2_w_glu, m_l2_b_glu, m_l2_w_out, m_l3_norm, m_l3_w_in, m_l3_sinks, m_l3_w_out, m_final_norm, v_l0_norm, v_l0_w_in, v_l0_a_re, v_l0_a_im, v_l0_log_step, v_l0_b_re, v_l0_b_im, v_l0_c_re, v_l0_c_im, v_l0_d, v_l0_w_glu, v_l0_b_glu, v_l0_w_out, v_l1_norm, v_l1_w_in, v_l1_sinks, v_l1_w_out, v_l2_norm, v_l2_w_in, v_l2_a_re, v_l2_a_im, v_l2_log_step, v_l2_b_re, v_l2_b_im, v_l2_c_re, v_l2_c_im, v_l2_d, v_l2_w_glu, v_l2_b_glu, v_l2_w_out, v_l3_norm, v_l3_w_in, v_l3_sinks, v_l3_w_out, v_final_norm):
    args = locals()
    weights = {n: args[n] for n in WEIGHT_NAMES}
    moments_m = {n: args["m_" + n] for n in WEIGHT_NAMES}
    moments_v = {n: args["v_" + n] for n in WEIGHT_NAMES}
    return _train_step(x, loss_target, weights, moments_m, moments_v)
```

```python
import functools
import math

import jax
import jax.numpy as jnp
from jax import lax
from jax.experimental import pallas as pl
from jax.experimental.pallas import tpu as pltpu

F32 = jnp.float32
MXU_DTYPE = jnp.bfloat16
WIRE_DTYPE = jnp.bfloat16
MESH = pl.DeviceIdType.MESH

D_MODEL = 1024
BRANCH = 1024
NORM_EPS = 1e-5
SSM_GROUPS = 64
SSM_GROUP = 16
SSM_STATE = 64
S5_CHUNK = 16
LANES = 128
S5_OCT = LANES // SSM_GROUP
S5_OCTETS = SSM_GROUPS // S5_OCT
S5_OCT_IN = S5_CHUNK * LANES
S5_OCT_STATE = S5_OCT * SSM_STATE
S5_STATES = SSM_GROUPS * SSM_STATE
HEAD_DIM = 64
N_Q_HEADS = 16
N_KV_HEADS = 2
GQA_GROUP = N_Q_HEADS // N_KV_HEADS
ATTN_BLOCK = 128
Q_DIM = N_Q_HEADS * HEAD_DIM
KV_DIM = N_KV_HEADS * HEAD_DIM
ROPE_THETA = 10000.0
NEG_INF = -1e30
ADAM_LR = 0.001
ADAM_B1 = 0.9
ADAM_B2 = 0.999
ADAM_EPS = 1e-08
ADAM_WD = 0.01
ADAM_STEP = 10

VMEM_LIMIT_V7X = 56 * 1024 * 1024
ROW_TILE_FWD = 512
ROW_TILE_BWD = 512

SSM_NAMES = ("norm", "w_in", "a_re", "a_im", "log_step", "b_re", "b_im", "c_re", "c_im", "d", "w_glu", "b_glu", "w_out")
ATTN_NAMES = ("norm", "w_in", "sinks", "w_out")


def _weight_names():
    names = []
    for i in range(4):
        for n in (SSM_NAMES if i % 2 == 0 else ATTN_NAMES):
            names.append("l%d_%s" % (i, n))
    names.append("final_norm")
    return names


WEIGHT_NAMES = _weight_names()
BIG_NAMES = [n for n in WEIGHT_NAMES if n.endswith(("w_in", "w_glu", "w_out"))]
SMALL_NAMES = [n for n in WEIGHT_NAMES if n not in BIG_NAMES]


def _params(semantics=None):
    return pltpu.CompilerParams(dimension_semantics=semantics, vmem_limit_bytes=VMEM_LIMIT_V7X)


def _rows(tm, n):
    return pl.BlockSpec((tm, n), lambda i: (i, 0))


def _whole(shape):
    return pl.BlockSpec(shape, lambda i: (0,) * len(shape), pipeline_mode=pl.Buffered(1))


def _sds(shape, dtype=F32):
    return jax.ShapeDtypeStruct(shape, dtype)


def _mm(a, b):
    return jnp.dot(a.astype(MXU_DTYPE), b.astype(MXU_DTYPE), preferred_element_type=F32)


def _mm_tn(a, b):
    return lax.dot_general(a.astype(MXU_DTYPE), b.astype(MXU_DTYPE), (((0,), (0,)), ((), ())), preferred_element_type=F32)


def _mm_nt(a, b):
    return lax.dot_general(a.astype(MXU_DTYPE), b.astype(MXU_DTYPE), (((1,), (1,)), ((), ())), preferred_element_type=F32)


def _sigmoid(x):
    return 0.5 + 0.5 * jnp.tanh(0.5 * x)


def _silu(x):
    return x * _sigmoid(x)


def _silu_and_grad(x):
    s = _sigmoid(x)
    return x * s, s * (1.0 + x * (1.0 - s))


GELU_C0 = math.sqrt(2.0 / math.pi)
GELU_C1 = 0.044715


def _gelu(x):
    return 0.5 * x * (1.0 + jnp.tanh(GELU_C0 * (x + GELU_C1 * x * x * x)))


def _gelu_and_grad(x):
    x2 = x * x
    th = jnp.tanh(GELU_C0 * x * (1.0 + GELU_C1 * x2))
    half = 0.5 + 0.5 * th
    return x * half, half + 0.5 * x * (1.0 - th * th) * (GELU_C0 + 3.0 * GELU_C0 * GELU_C1 * x2)


def _rms(x, g):
    r = lax.rsqrt(jnp.mean(x * x, axis=-1, keepdims=True) + NORM_EPS)
    xhat = x * r
    return r, xhat, xhat * g


def _rms_bwd(dh, g, r, xhat):
    dxhat = dh * g
    dx = r * (dxhat - xhat * jnp.mean(dxhat * xhat, axis=-1, keepdims=True))
    return dx, jnp.sum(dh * xhat, axis=0, keepdims=True)


def _swap_half_heads(x):
    n = x.shape[-1]
    lane = lax.broadcasted_iota(jnp.int32, x.shape, x.ndim - 1)
    first = (lane % HEAD_DIM) < (HEAD_DIM // 2)
    return jnp.where(first, pltpu.roll(x, n - HEAD_DIM // 2, x.ndim - 1), pltpu.roll(x, HEAD_DIM // 2, x.ndim - 1))


def _tile_lanes(t, reps):
    return jnp.concatenate([t] * reps, axis=1)


TOKEN_SHAPE = (8, LANES)


def _after(token):
    return ([], []) if token is None else ([token], [_whole(TOKEN_SHAPE)])


def ssm_proj_fwd(x, norm, w_in):
    t = x.shape[0]
    tm = min(ROW_TILE_FWD, t)

    def body(x_ref, g_ref, w_ref, u_ref, gate_ref):
        _, _, h = _rms(x_ref[...], g_ref[...])
        h = h.astype(MXU_DTYPE)
        half = BRANCH // 2
        for j in range(2):
            u_ref[:, j * half:(j + 1) * half] = _mm(h, w_ref[j])
            gate_ref[:, j * half:(j + 1) * half] = _mm(h, w_ref[2 + j])

    return pl.pallas_call(
        body, name="ssm_proj_fwd", grid=(t // tm,),
        in_specs=[_rows(tm, D_MODEL), _whole((1, D_MODEL)), _whole((4, D_MODEL, BRANCH // 2))],
        out_specs=[_rows(tm, BRANCH), _rows(tm, BRANCH)],
        out_shape=[_sds((t, BRANCH)), _sds((t, BRANCH))],
        compiler_params=_params(("parallel",)),
    )(x, norm, w_in)


def _chunk_rows(ref, nk, dtype=None):
    rows = jnp.concatenate([ref[pl.ds(s, nk, stride=S5_CHUNK), :] for s in range(S5_CHUNK)], axis=1)
    return rows.astype(MXU_DTYPE if dtype is None else dtype)


def _store_chunk_rows(ref, val, nk):
    for s in range(S5_CHUNK):
        ref[pl.ds(s, nk, stride=S5_CHUNK), :] = val[:, s * LANES:(s + 1) * LANES]


def _own_group_mask():
    row = lax.broadcasted_iota(jnp.int32, (S5_OCT_IN, S5_OCT_STATE), 0)
    col = lax.broadcasted_iota(jnp.int32, (S5_OCT_IN, S5_OCT_STATE), 1)
    return ((row % LANES) // SSM_GROUP) == (col // SSM_STATE)


def _spread_groups(w):
    return jnp.where(_own_group_mask(), jnp.concatenate([w] * (S5_OCT_STATE // LANES), axis=1), 0.0).astype(MXU_DTYPE)


def _fold_groups(p):
    p = jnp.where(_own_group_mask(), p, 0.0)
    return sum(p[:, q * LANES:(q + 1) * LANES] for q in range(S5_OCT_STATE // LANES))


def _fill_toeplitz(win_ref, kd_ref):
    win_ref[...] = jnp.zeros_like(win_ref)
    for s in range(S5_CHUNK):
        for t in range(s, S5_CHUNK):
            win_ref[s * LANES:(s + 1) * LANES, t * LANES:(t + 1) * LANES] = kd_ref[t - s].astype(MXU_DTYPE)


TOEPLITZ_BLOCK = 512
_TOEPLITZ_BLOCKS = [(lo, lo + TOEPLITZ_BLOCK) for lo in range(0, S5_OCT_IN, TOEPLITZ_BLOCK)]


def _strip(t):
    return pl.BlockSpec((t, LANES), lambda b: (0, b))


def _oct_states(nk):
    return pl.BlockSpec((nk, S5_OCT_STATE), lambda b: (0, b))


OCT_W = pl.BlockSpec((None, S5_OCT_IN, LANES), lambda b: (b, 0, 0))
OCT_KD = pl.BlockSpec((None, S5_CHUNK, LANES, LANES), lambda b: (b, 0, 0, 0))


def s5_chunk_states(u, ws_re, ws_im):
    t = u.shape[0]
    nk = t // S5_CHUNK

    def body(u_ref, wr_ref, wi_ref, re_ref, im_ref):
        uc = _chunk_rows(u_ref, nk)
        re_ref[...] = _mm(uc, _spread_groups(wr_ref[...]))
        im_ref[...] = _mm(uc, _spread_groups(wi_ref[...]))

    return pl.pallas_call(
        body, name="s5_chunk_states", grid=(S5_OCTETS,),
        in_specs=[_strip(t), OCT_W, OCT_W], out_specs=[_oct_states(nk), _oct_states(nk)],
        out_shape=[_sds((nk, S5_STATES)), _sds((nk, S5_STATES))],
        compiler_params=_params(("parallel",)),
    )(u, ws_re, ws_im)


def s5_scan_fwd(s_re, s_im, a_re, a_im):
    nk = s_re.shape[0]

    def body(sre_ref, sim_ref, ar_ref, ai_ref, hre_ref, him_ref):
        ar = ar_ref[...]
        ai = ai_ref[...]

        def step(k, carry):
            hr, hi = carry
            hre_ref[pl.ds(k, 1), :] = hr
            him_ref[pl.ds(k, 1), :] = hi
            sr = sre_ref[pl.ds(k, 1), :]
            si = sim_ref[pl.ds(k, 1), :]
            return ar * hr - ai * hi + sr, ai * hr + ar * hi + si

        zero = jnp.zeros((1, S5_STATES), F32)
        lax.fori_loop(0, nk, step, (zero, zero))

    vm = pl.BlockSpec(memory_space=pltpu.VMEM)
    return pl.pallas_call(
        body, name="s5_scan_fwd", in_specs=[vm, vm, vm, vm], out_specs=[vm, vm],
        out_shape=[_sds((nk, S5_STATES)), _sds((nk, S5_STATES))],
        compiler_params=_params(),
    )(s_re, s_im, a_re, a_im)


def s5_outputs(u, h_re, h_im, kd, wo_re, wo_im):
    t = u.shape[0]
    nk = t // S5_CHUNK

    def body(u_ref, hre_ref, him_ref, kd_ref, wor_ref, woi_ref, y_ref, win_ref):
        _fill_toeplitz(win_ref, kd_ref)
        uc = _chunk_rows(u_ref, nk)
        y = jnp.concatenate([_mm(uc[:, :hi], win_ref[:hi, lo:hi]) for lo, hi in _TOEPLITZ_BLOCKS], axis=1)
        y = y + _mm_nt(hre_ref[...], _spread_groups(wor_ref[...])) + _mm_nt(him_ref[...], _spread_groups(woi_ref[...]))
        _store_chunk_rows(y_ref, y, nk)

    return pl.pallas_call(
        body, name="s5_outputs", grid=(S5_OCTETS,),
        in_specs=[_strip(t), _oct_states(nk), _oct_states(nk), OCT_KD, OCT_W, OCT_W],
        out_specs=_strip(t), out_shape=_sds((t, BRANCH)),
        scratch_shapes=[pltpu.VMEM((S5_OCT_IN, S5_OCT_IN), MXU_DTYPE)],
        compiler_params=_params(("parallel",)),
    )(u, h_re, h_im, kd, wo_re, wo_im)


def s5_state_grads(dy, wo_re, wo_im):
    t = dy.shape[0]
    nk = t // S5_CHUNK

    def body(dy_ref, wor_ref, woi_ref, re_ref, im_ref):
        dyc = _chunk_rows(dy_ref, nk)
        re_ref[...] = _mm(dyc, _spread_groups(wor_ref[...]))
        im_ref[...] = _mm(dyc, _spread_groups(woi_ref[...]))

    return pl.pallas_call(
        body, name="s5_state_grads", grid=(S5_OCTETS,),
        in_specs=[_strip(t), OCT_W, OCT_W], out_specs=[_oct_states(nk), _oct_states(nk)],
        out_shape=[_sds((nk, S5_STATES)), _sds((nk, S5_STATES))],
        compiler_params=_params(("parallel",)),
    )(dy, wo_re, wo_im)


def s5_scan_bwd(dh_re, dh_im, h_re, h_im, a_re, a_im):
    nk = dh_re.shape[0]

    def body(dhr_ref, dhi_ref, hr_ref, hi_ref, ar_ref, ai_ref, dsr_ref, dsi_ref, dar_ref, dai_ref):
        ar = ar_ref[...]
        ai = ai_ref[...]

        def step(i, carry):
            gr, gi = carry
            k = nk - 1 - i
            dhr = dhr_ref[pl.ds(k, 1), :]
            dhi = dhi_ref[pl.ds(k, 1), :]
            dsr_ref[pl.ds(k, 1), :] = gr
            dsi_ref[pl.ds(k, 1), :] = gi
            return dhr + ar * gr + ai * gi, dhi - ai * gr + ar * gi

        zero = jnp.zeros((1, S5_STATES), F32)
        lax.fori_loop(0, nk, step, (zero, zero))
        dsr, dsi, hr, hi = dsr_ref[...], dsi_ref[...], hr_ref[...], hi_ref[...]
        dar_ref[...] = jnp.sum(dsr * hr + dsi * hi, axis=0, keepdims=True)
        dai_ref[...] = jnp.sum(dsi * hr - dsr * hi, axis=0, keepdims=True)

    vm = pl.BlockSpec(memory_space=pltpu.VMEM)
    return pl.pallas_call(
        body, name="s5_scan_bwd", in_specs=[vm] * 6, out_specs=[vm] * 4,
        out_shape=[_sds((nk, S5_STATES)), _sds((nk, S5_STATES)), _sds((1, S5_STATES)), _sds((1, S5_STATES))],
        input_output_aliases={0: 0, 1: 1}, compiler_params=_params(),
    )(dh_re, dh_im, h_re, h_im, a_re, a_im)


def s5_input_grads(dy, ds_re, ds_im, kd, ws_re, ws_im):
    t = dy.shape[0]
    nk = t // S5_CHUNK

    def body(dy_ref, dsr_ref, dsi_ref, kd_ref, wsr_ref, wsi_ref, du_ref, win_ref):
        _fill_toeplitz(win_ref, kd_ref)
        dyc = _chunk_rows(dy_ref, nk)
        du = jnp.concatenate([_mm_nt(dyc[:, lo:], win_ref[lo:hi, lo:]) for lo, hi in _TOEPLITZ_BLOCKS], axis=1)
        du = du + _mm_nt(dsr_ref[...], _spread_groups(wsr_ref[...])) + _mm_nt(dsi_ref[...], _spread_groups(wsi_ref[...]))
        _store_chunk_rows(du_ref, du, nk)

    return pl.pallas_call(
        body, name="s5_input_grads", grid=(S5_OCTETS,),
        in_specs=[_strip(t), _oct_states(nk), _oct_states(nk), OCT_KD, OCT_W, OCT_W],
        out_specs=_strip(t), out_shape=_sds((t, BRANCH)),
        scratch_shapes=[pltpu.VMEM((S5_OCT_IN, S5_OCT_IN), MXU_DTYPE)],
        compiler_params=_params(("parallel",)),
    )(dy, ds_re, ds_im, kd, ws_re, ws_im)


def s5_weight_grads(u, dy, h_re, h_im, ds_re, ds_im):
    t = u.shape[0]
    nk = t // S5_CHUNK

    def body(u_ref, dy_ref, hre_ref, him_ref, dsr_ref, dsi_ref, dkd_ref, dwsr_ref, dwsi_ref, dwor_ref, dwoi_ref):
        dyc = _chunk_rows(dy_ref, nk, F32)
        uct = _chunk_rows(u_ref, nk, F32).T.astype(MXU_DTYPE)
        dyct = dyc.T.astype(MXU_DTYPE)
        dyc = dyc.astype(MXU_DTYPE)
        dwsr_ref[...] = _fold_groups(_mm(uct, dsr_ref[...]))
        dwsi_ref[...] = _fold_groups(_mm(uct, dsi_ref[...]))
        dwor_ref[...] = _fold_groups(_mm(dyct, hre_ref[...]))
        dwoi_ref[...] = _fold_groups(_mm(dyct, him_ref[...]))
        dkd_ref[...] = jnp.zeros_like(dkd_ref)
        for tt in range(0, S5_CHUNK, 2):
            p = _mm(uct[:(tt + 2) * LANES], dyc[:, tt * LANES:(tt + 2) * LANES])
            for s in range(tt + 2):
                rows = p[s * LANES:(s + 1) * LANES]
                if s <= tt:
                    dkd_ref[tt - s] += rows[:, :LANES]
                dkd_ref[tt + 1 - s] += rows[:, LANES:]

    return pl.pallas_call(
        body, name="s5_weight_grads", grid=(S5_OCTETS,),
        in_specs=[_strip(t), _strip(t)] + [_oct_states(nk)] * 4,
        out_specs=[OCT_KD, OCT_W, OCT_W, OCT_W, OCT_W],
        out_shape=[_sds((S5_OCTETS, S5_CHUNK, LANES, LANES))] + [_sds((S5_OCTETS, S5_OCT_IN, LANES))] * 4,
        compiler_params=_params(("parallel",)),
    )(u, dy, h_re, h_im, ds_re, ds_im)


def ssm_mix_fwd(x, u, gate, y_scan, d, w_glu, b_glu, w_out):
    t = x.shape[0]
    tm = min(ROW_TILE_FWD, t)

    def body(x_ref, u_ref, gate_ref, ys_ref, d_ref, wg_ref, bg_ref, wo_ref, y_ref, g2_ref, xo_ref):
        y = ys_ref[...] + d_ref[...] * u_ref[...]
        z0 = _gelu(y)
        g2 = _mm(z0, wg_ref[...]) + bg_ref[...]
        a = z0 * _sigmoid(g2) * _silu(gate_ref[...])
        y_ref[...] = y
        g2_ref[...] = g2
        xo_ref[...] = x_ref[...] + _mm(a, wo_ref[...])

    row = _rows(tm, BRANCH)
    vec = _whole((1, BRANCH))
    mat = _whole((BRANCH, BRANCH))
    return pl.pallas_call(
        body, name="ssm_mix_fwd", grid=(t // tm,),
        in_specs=[row, row, row, row, vec, mat, vec, mat],
        out_specs=[row, row, row],
        out_shape=[_sds((t, BRANCH))] * 3,
        compiler_params=_params(("parallel",)),
    )(x, u, gate, y_scan, d, w_glu, b_glu, w_out)


def ssm_mix_bwd(dxo, u, gate, y, g2, w_glu, w_out, token=None):
    t = dxo.shape[0]
    tm = min(ROW_TILE_BWD, t)
    extra, extra_specs = _after(token)

    def body(dxo_ref, u_ref, gate_ref, y_ref, g2_ref, wgt_ref, wot_ref, *rest):
        dy_ref, dgate_ref, dwo_ref, dwg_ref, dbg_ref, dd_ref = rest[-6:]

        @pl.when(pl.program_id(0) == 0)
        def _():
            dwo_ref[...] = jnp.zeros_like(dwo_ref)
            dwg_ref[...] = jnp.zeros_like(dwg_ref)
            dbg_ref[...] = jnp.zeros_like(dbg_ref)
            dd_ref[...] = jnp.zeros_like(dd_ref)

        dxo = dxo_ref[...]
        gate = gate_ref[...]
        y = y_ref[...]
        z0, z0_grad = _gelu_and_grad(y)
        sg = _sigmoid(g2_ref[...])
        z = z0 * sg
        sgate, sgate_grad = _silu_and_grad(gate)
        da = _mm_nt(dxo, wot_ref[...])
        dwo_ref[...] += _mm_tn(z * sgate, dxo)
        dz = da * sgate
        dgate_ref[...] = da * z * sgate_grad
        dg2 = dz * z0 * sg * (1.0 - sg)
        dbg_ref[...] += jnp.sum(dg2, axis=0, keepdims=True)
        dwg_ref[...] += _mm_tn(z0, dg2)
        dz0 = dz * sg + _mm_nt(dg2, wgt_ref[...])
        dy = dz0 * z0_grad
        dd_ref[...] += jnp.sum(dy * u_ref[...], axis=0, keepdims=True)
        dy_ref[...] = dy

    row = _rows(tm, BRANCH)
    vec = _whole((1, BRANCH))
    mat = _whole((BRANCH, BRANCH))
    return pl.pallas_call(
        body, name="ssm_mix_bwd", grid=(t // tm,),
        in_specs=[row, row, row, row, row, mat, mat] + extra_specs,
        out_specs=[row, row, mat, mat, vec, vec],
        out_shape=[_sds((t, BRANCH)), _sds((t, BRANCH)), _sds((BRANCH, D_MODEL)), _sds((BRANCH, BRANCH)),
                   _sds((1, BRANCH)), _sds((1, BRANCH))],
        compiler_params=_params(("arbitrary",)),
    )(dxo, u, gate, y, g2, w_glu, w_out, *extra)


def ssm_proj_bwd(x, norm, dxo, dy, du_scan, dgate, d, w_in):
    t = x.shape[0]
    tm = min(ROW_TILE_BWD, t)
    n = 2 * BRANCH

    def body(x_ref, g_ref, dxo_ref, dy_ref, dus_ref, dgate_ref, d_ref, wt_ref, dx_ref, dw_ref, dg_ref):
        @pl.when(pl.program_id(0) == 0)
        def _():
            dw_ref[...] = jnp.zeros_like(dw_ref)
            dg_ref[...] = jnp.zeros_like(dg_ref)

        g = g_ref[...]
        r, xhat, h = _rms(x_ref[...], g)
        h = h.astype(MXU_DTYPE)
        du = dus_ref[...] + d_ref[...] * dy_ref[...]
        dproj = jnp.concatenate([du, dgate_ref[...]], axis=1).astype(MXU_DTYPE)
        dh = jnp.zeros((tm, D_MODEL), F32)
        for j in range(4):
            cols = dproj[:, j * (n // 4):(j + 1) * (n // 4)]
            dh = dh + _mm_nt(cols, wt_ref[j])
            dw_ref[j] += _mm_tn(h, cols)
        dx, dg = _rms_bwd(dh, g, r, xhat)
        dg_ref[...] += dg
        dx_ref[...] = dxo_ref[...] + dx

    row = _rows(tm, D_MODEL)
    vec = _whole((1, D_MODEL))
    blocks = _whole((4, D_MODEL, n // 4))
    return pl.pallas_call(
        body, name="ssm_proj_bwd", grid=(t // tm,),
        in_specs=[row, vec, row, row, row, row, vec, blocks],
        out_specs=[row, blocks, vec],
        out_shape=[_sds((t, D_MODEL)), _sds((4, D_MODEL, n // 4)), _sds((1, D_MODEL))],
        compiler_params=_params(("arbitrary",)),
    )(x, norm, dxo, dy, du_scan, dgate, d, w_in)


ATTN_N = Q_DIM + 2 * KV_DIM + BRANCH


def attn_proj_fwd(x, norm, w_in_t, cos2, sin2):
    t = x.shape[0]
    tm = min(ROW_TILE_FWD, t)

    def body(x_ref, g_ref, w_ref, cos_ref, sin_ref, q_ref, k_ref, v_ref, gate_ref):
        _, _, h = _rms(x_ref[...], g_ref[...])
        p = _mm_nt(h, w_ref[...])
        cs = cos_ref[...]
        sn = sin_ref[...]
        q = p[:, :Q_DIM]
        k = p[:, Q_DIM:Q_DIM + KV_DIM]
        q_ref[...] = q * _tile_lanes(cs, Q_DIM // LANES) + _swap_half_heads(q) * _tile_lanes(sn, Q_DIM // LANES)
        k_ref[...] = k * cs + _swap_half_heads(k) * sn
        v_ref[...] = p[:, Q_DIM + KV_DIM:Q_DIM + 2 * KV_DIM]
        gate_ref[...] = p[:, Q_DIM + 2 * KV_DIM:]

    return pl.pallas_call(
        body, name="attn_proj_fwd", grid=(t // tm,),
        in_specs=[_rows(tm, D_MODEL), _whole((1, D_MODEL)), _whole((ATTN_N, D_MODEL)), _rows(tm, LANES), _rows(tm, LANES)],
        out_specs=[_rows(tm, Q_DIM), _rows(tm, KV_DIM), _rows(tm, KV_DIM), _rows(tm, BRANCH)],
        out_shape=[_sds((t, Q_DIM)), _sds((t, KV_DIM)), _sds((t, KV_DIM)), _sds((t, BRANCH))],
        compiler_params=_params(("parallel",)),
    )(x, norm, w_in_t, cos2, sin2)


GQA_LANES = GQA_GROUP * ATTN_BLOCK


def _window_masks(first_block):
    kj = lax.broadcasted_iota(jnp.int32, (ATTN_BLOCK, GQA_LANES), 0)
    qi = lax.broadcasted_iota(jnp.int32, (ATTN_BLOCK, GQA_LANES), 1) % ATTN_BLOCK
    return kj > qi, kj > jnp.where(first_block, qi, ATTN_BLOCK)


def _fold(upper, both):
    return jnp.where(upper, both[:ATTN_BLOCK], both[ATTN_BLOCK:])


def _unfold(upper, tile):
    return jnp.concatenate([jnp.where(upper, tile, 0.0), jnp.where(upper, 0.0, tile)], axis=0).astype(MXU_DTYPE)


def _stack_heads(ref, group):
    return jnp.concatenate([ref[:, h * HEAD_DIM:(h + 1) * HEAD_DIM] for h in range(group * GQA_GROUP, (group + 1) * GQA_GROUP)], axis=0)


def _unstack_heads(ref, group, stacked):
    for n in range(GQA_GROUP):
        h = group * GQA_GROUP + n
        ref[:, h * HEAD_DIM:(h + 1) * HEAD_DIM] = stacked[n * ATTN_BLOCK:(n + 1) * ATTN_BLOCK]


def _sink_row(sink_ref, group):
    return jnp.concatenate([jnp.full((1, ATTN_BLOCK), sink_ref[group * GQA_GROUP + n], F32) for n in range(GQA_GROUP)], axis=1)


def _lane_is(h):
    return lax.broadcasted_iota(jnp.int32, (1, LANES), 1) == h


def attn_fwd(q, k, v, sinks):
    t = q.shape[0]
    nb = t // ATTN_BLOCK
    scale = HEAD_DIM ** -0.5

    def body(sink_ref, q_ref, kc_ref, kp_ref, vc_ref, vp_ref, o_ref, lse_ref):
        keys = jnp.concatenate([kp_ref[...], kc_ref[...]], axis=0).astype(MXU_DTYPE)
        vals = jnp.concatenate([vp_ref[...], vc_ref[...]], axis=0).astype(MXU_DTYPE)
        upper, dead = _window_masks(pl.program_id(0) == 0)
        for g in range(N_KV_HEADS):
            kv = slice(g * HEAD_DIM, (g + 1) * HEAD_DIM)
            qs = _stack_heads(q_ref, g) * scale
            s = jnp.where(dead, NEG_INF, _fold(upper, _mm_nt(keys[:, kv], qs)))
            sink = _sink_row(sink_ref, g)
            m = jnp.maximum(jnp.max(s, axis=0, keepdims=True), sink)
            p = jnp.exp(s - m)
            den = jnp.sum(p, axis=0, keepdims=True) + jnp.exp(sink - m)
            _unstack_heads(o_ref, g, _mm_tn(_unfold(upper, p * (1.0 / den)), vals[:, kv]))
            lse = m + jnp.log(den)
            for n in range(GQA_GROUP):
                lse_ref[pl.ds(g * GQA_GROUP + n, 1), :] = lse[:, n * ATTN_BLOCK:(n + 1) * ATTN_BLOCK]

    cur = lambda n: pl.BlockSpec((ATTN_BLOCK, n), lambda i: (i, 0))
    prev = lambda n: pl.BlockSpec((ATTN_BLOCK, n), lambda i: (jnp.maximum(i - 1, 0), 0))
    return pl.pallas_call(
        body, name="attn_fwd", grid=(nb,),
        in_specs=[pl.BlockSpec(memory_space=pltpu.SMEM), cur(Q_DIM), cur(KV_DIM), prev(KV_DIM), cur(KV_DIM), prev(KV_DIM)],
        out_specs=[cur(Q_DIM), pl.BlockSpec((N_Q_HEADS, ATTN_BLOCK), lambda i: (0, i))],
        out_shape=[_sds((t, Q_DIM)), _sds((N_Q_HEADS, t))],
        compiler_params=_params(("parallel",)),
    )(sinks, q, k, k, v, v)


def attn_bwd(q, k, v, sinks, o, lse, do):
    t = q.shape[0]
    nb = t // ATTN_BLOCK
    scale = HEAD_DIM ** -0.5

    def body(sink_ref, q_ref, kc_ref, kp_ref, vc_ref, vp_ref, o_ref, lse_ref, do_ref,
             dq_ref, dk_ref, dv_ref, dsink_ref, dk_carry, dv_carry):
        i = pl.program_id(0)

        @pl.when(i == 0)
        def _():
            dsink_ref[...] = jnp.zeros_like(dsink_ref)
            dk_carry[...] = jnp.zeros_like(dk_carry)
            dv_carry[...] = jnp.zeros_like(dv_carry)

        @pl.when(i < nb)
        def _():
            keys = jnp.concatenate([kp_ref[...], kc_ref[...]], axis=0).astype(MXU_DTYPE)
            vals = jnp.concatenate([vp_ref[...], vc_ref[...]], axis=0).astype(MXU_DTYPE)
            upper, dead = _window_masks(i == 0)
            dsink = jnp.zeros((1, LANES), F32)
            dk_heads = []
            dv_heads = []
            for g in range(N_KV_HEADS):
                kv = slice(g * HEAD_DIM, (g + 1) * HEAD_DIM)
                qs = (_stack_heads(q_ref, g) * scale).astype(MXU_DTYPE)
                dos = _stack_heads(do_ref, g)
                lse = jnp.concatenate([lse_ref[pl.ds(g * GQA_GROUP + n, 1), :] for n in range(GQA_GROUP)], axis=1)
                s = jnp.where(dead, NEG_INF, _fold(upper, _mm_nt(keys[:, kv], qs)))
                p = jnp.exp(s - lse)
                delta = _mm_f32(jnp.ones((8, HEAD_DIM), F32), dos * _stack_heads(o_ref, g), ((1,), (1,)))[:1]
                dos = dos.astype(MXU_DTYPE)
                ds = _unfold(upper, p * (_fold(upper, _mm_nt(vals[:, kv], dos)) - delta))
                _unstack_heads(dq_ref, g, _mm_tn(ds, keys[:, kv]) * scale)
                dk_heads.append(_mm(ds, qs))
                dv_heads.append(_mm(_unfold(upper, p), dos))
                at_sink = jnp.exp(_sink_row(sink_ref, g) - lse) * delta
                for n in range(GQA_GROUP):
                    dsink = dsink + jnp.where(_lane_is(g * GQA_GROUP + n), -jnp.sum(at_sink[:, n * ATTN_BLOCK:(n + 1) * ATTN_BLOCK]), 0.0)
            dkk = jnp.concatenate(dk_heads, axis=1)
            dvv = jnp.concatenate(dv_heads, axis=1)
            dsink_ref[...] += dsink
            dk_ref[...] = dk_carry[...] + dkk[:ATTN_BLOCK]
            dv_ref[...] = dv_carry[...] + dvv[:ATTN_BLOCK]
            dk_carry[...] = dkk[ATTN_BLOCK:]
            dv_carry[...] = dvv[ATTN_BLOCK:]

        @pl.when(i == nb)
        def _():
            dk_ref[...] = dk_carry[...]
            dv_ref[...] = dv_carry[...]

    last = nb - 1
    cur = lambda n: pl.BlockSpec((ATTN_BLOCK, n), lambda i: (jnp.minimum(i, last), 0))
    prev = lambda n: pl.BlockSpec((ATTN_BLOCK, n), lambda i: (jnp.clip(i - 1, 0, last), 0))
    late = lambda n: pl.BlockSpec((ATTN_BLOCK, n), lambda i: (i, 0))
    dq, dk_late, dv_late, dsinks = pl.pallas_call(
        body, name="attn_bwd", grid=(nb + 1,),
        in_specs=[pl.BlockSpec(memory_space=pltpu.SMEM), cur(Q_DIM), cur(KV_DIM), prev(KV_DIM), cur(KV_DIM), prev(KV_DIM),
                  cur(Q_DIM), pl.BlockSpec((N_Q_HEADS, ATTN_BLOCK), lambda i: (0, jnp.minimum(i, last))), cur(Q_DIM)],
        out_specs=[cur(Q_DIM), late(KV_DIM), late(KV_DIM), _whole((1, LANES))],
        out_shape=[_sds((t, Q_DIM)), _sds((t + ATTN_BLOCK, KV_DIM)), _sds((t + ATTN_BLOCK, KV_DIM)), _sds((1, LANES))],
        scratch_shapes=[pltpu.VMEM((ATTN_BLOCK, KV_DIM), F32), pltpu.VMEM((ATTN_BLOCK, KV_DIM), F32)],
        compiler_params=_params(("arbitrary",)),
    )(sinks, q, k, k, v, v, o, lse, do)
    return dq, dk_late[ATTN_BLOCK:], dv_late[ATTN_BLOCK:], dsinks


def attn_out_fwd(x, o, gate, w_out):
    t = x.shape[0]
    tm = min(ROW_TILE_FWD, t)

    def body(x_ref, o_ref, gate_ref, w_ref, xo_ref):
        xo_ref[...] = x_ref[...] + _mm(o_ref[...] * _silu(gate_ref[...]), w_ref[...])

    row = _rows(tm, D_MODEL)
    return pl.pallas_call(
        body, name="attn_out_fwd", grid=(t // tm,),
        in_specs=[row, row, row, _whole((Q_DIM, D_MODEL))], out_specs=row, out_shape=_sds((t, D_MODEL)),
        compiler_params=_params(("parallel",)),
    )(x, o, gate, w_out)


def attn_out_bwd(dxo, o, gate, w_out, token=None):
    t = dxo.shape[0]
    tm = min(ROW_TILE_BWD, t)
    extra, extra_specs = _after(token)

    def body(dxo_ref, o_ref, gate_ref, wt_ref, *rest):
        do_ref, dgate_ref, dw_ref = rest[-3:]

        @pl.when(pl.program_id(0) == 0)
        def _():
            dw_ref[...] = jnp.zeros_like(dw_ref)

        dxo = dxo_ref[...]
        o = o_ref[...]
        gate = gate_ref[...]
        sgate, sgate_grad = _silu_and_grad(gate)
        da = _mm_nt(dxo, wt_ref[...])
        dw_ref[...] += _mm_tn(o * sgate, dxo)
        do_ref[...] = da * sgate
        dgate_ref[...] = da * o * sgate_grad

    row = _rows(tm, D_MODEL)
    mat = _whole((Q_DIM, D_MODEL))
    return pl.pallas_call(
        body, name="attn_out_bwd", grid=(t // tm,),
        in_specs=[row, row, row, mat] + extra_specs, out_specs=[row, row, mat],
        out_shape=[_sds((t, Q_DIM)), _sds((t, BRANCH)), _sds((Q_DIM, D_MODEL))],
        compiler_params=_params(("arbitrary",)),
    )(dxo, o, gate, w_out, *extra)


def attn_proj_bwd(x, norm, dxo, dq, dk, dv, dgate, cos2, sin2, w_in_t):
    t = x.shape[0]
    tm = min(ROW_TILE_BWD, t)

    def body(x_ref, g_ref, dxo_ref, dq_ref, dk_ref, dv_ref, dgate_ref, cos_ref, sin_ref, wt_ref, dx_ref, dw_ref, dg_ref):
        @pl.when(pl.program_id(0) == 0)
        def _():
            dw_ref[...] = jnp.zeros_like(dw_ref)
            dg_ref[...] = jnp.zeros_like(dg_ref)

        g = g_ref[...]
        r, xhat, h = _rms(x_ref[...], g)
        cs = cos_ref[...]
        sn = sin_ref[...]
        dqr = dq_ref[...]
        dkr = dk_ref[...]
        dq = dqr * _tile_lanes(cs, Q_DIM // LANES) + _swap_half_heads(dqr * _tile_lanes(sn, Q_DIM // LANES))
        dk = dkr * cs + _swap_half_heads(dkr * sn)
        dproj = jnp.concatenate([dq, dk, dv_ref[...], dgate_ref[...]], axis=1)
        dh = _mm(dproj, wt_ref[...])
        dw_ref[...] += _mm_tn(dproj, h)
        dx, dg = _rms_bwd(dh, g, r, xhat)
        dg_ref[...] += dg
        dx_ref[...] = dxo_ref[...] + dx

    row = _rows(tm, D_MODEL)
    vec = _whole((1, D_MODEL))
    return pl.pallas_call(
        body, name="attn_proj_bwd", grid=(t // tm,),
        in_specs=[row, vec, row, _rows(tm, Q_DIM), _rows(tm, KV_DIM), _rows(tm, KV_DIM), _rows(tm, BRANCH),
                  _rows(tm, LANES), _rows(tm, LANES), _whole((ATTN_N, D_MODEL))],
        out_specs=[row, _whole((ATTN_N, D_MODEL)), vec],
        out_shape=[_sds((t, D_MODEL)), _sds((ATTN_N, D_MODEL)), _sds((1, D_MODEL))],
        compiler_params=_params(("arbitrary",)),
    )(x, norm, dxo, dq, dk, dv, dgate, cos2, sin2, w_in_t)


def attn_out_loss(x, o, gate, w_out, norm, target):
    t = x.shape[0]
    tm = min(ROW_TILE_FWD, t)

    def body(x_ref, o_ref, gate_ref, w_ref, g_ref, tgt_ref, loss_ref, dx_ref, dg_ref):
        @pl.when(pl.program_id(0) == 0)
        def _():
            loss_ref[...] = jnp.zeros_like(loss_ref)
            dg_ref[...] = jnp.zeros_like(dg_ref)

        out = x_ref[...] + _mm(o_ref[...] * _silu(gate_ref[...]), w_ref[...])
        g = g_ref[...]
        r, xhat, y = _rms(out, g)
        err = y - tgt_ref[...]
        loss_ref[...] += 0.5 * jnp.sum(jnp.mean(err * err, axis=-1, keepdims=True), axis=0, keepdims=True)
        dx, dg = _rms_bwd(err * (1.0 / D_MODEL), g, r, xhat)
        dg_ref[...] += dg
        dx_ref[...] = dx

    row = _rows(tm, D_MODEL)
    vec = _whole((1, D_MODEL))
    return pl.pallas_call(
        body, name="attn_out_loss", grid=(t // tm,),
        in_specs=[row, row, row, _whole((Q_DIM, D_MODEL)), vec, row], out_specs=[_whole((1, 1)), row, vec],
        out_shape=[_sds((1, 1)), _sds((t, D_MODEL)), _sds((1, D_MODEL))],
        compiler_params=_params(("arbitrary",)),
    )(x, o, gate, w_out, norm, target)


OCT_TILE = pl.BlockSpec((None, LANES, LANES), lambda b: (b, 0, 0))
N_LAGS = S5_CHUNK + 1


def _cmul(ar, ai, br, bi):
    return ar * br - ai * bi, ar * bi + ai * br


def _cmul_conj(ar, ai, br, bi):
    return ar * br + ai * bi, ar * bi - ai * br


def _mm_f32(a, b, dims):
    return lax.dot_general(a, b, (dims, ((), ())), precision=lax.Precision.HIGH, preferred_element_type=F32)


def _s5_discretise(ar, ai, ls, br, bi):
    dt = jnp.exp(ls)
    xr = ar * dt
    xi = ai * dt
    mag = jnp.exp(xr)
    first = (mag * jnp.cos(xi), mag * jnp.sin(xi))
    powers = [(jnp.ones_like(xr), jnp.zeros_like(xr)), first]
    for _ in range(2, N_LAGS):
        powers.append(_cmul(*powers[-1], *first))
    den = ar * ar + ai * ai
    nr = powers[1][0] - 1.0
    ni = powers[1][1]
    fr = (nr * ar + ni * ai) / den
    fi = (ni * ar - nr * ai) / den
    bbr, bbi = _cmul(fr, fi, br, bi)
    return dt, powers, (fr, fi), (bbr, bbi), den


def _same_group_tile():
    row = lax.broadcasted_iota(jnp.int32, (LANES, LANES), 0)
    col = lax.broadcasted_iota(jnp.int32, (LANES, LANES), 1)
    return (row // SSM_GROUP) == (col // SSM_GROUP)


def _first_copy_lanes():
    return lax.broadcasted_iota(jnp.int32, (LANES, LANES), 1) < SSM_STATE


def s5_param_fwd(tiles, token=None):
    extra, extra_specs = _after(token)

    def body(ar_ref, ai_ref, ls_ref, br_ref, bi_ref, cr_ref, ci_ref, *rest):
        kd_ref, wsr_ref, wsi_ref, wor_ref, woi_ref, pr_ref, pi_ref = rest[-7:]
        cr = cr_ref[...]
        ci = ci_ref[...]
        _, powers, _, (bbr, bbi), _ = _s5_discretise(ar_ref[...], ai_ref[...], ls_ref[...], br_ref[...], bi_ref[...])
        once = _first_copy_lanes()
        crm = jnp.where(once, cr, 0.0)
        cim = jnp.where(once, ci, 0.0)
        same = _same_group_tile()
        for lag in range(S5_CHUNK):
            er, ei = powers[lag]
            xr, xi = _cmul(er, ei, bbr, bbi)
            rows = pl.ds((S5_CHUNK - 1 - lag) * LANES, LANES)
            wsr_ref[rows, :] = xr
            wsi_ref[rows, :] = xi
        k = _mm_f32(wsr_ref[...], crm, ((1,), (1,))) - _mm_f32(wsi_ref[...], cim, ((1,), (1,)))
        for lag in range(S5_CHUNK):
            kd_ref[lag] = jnp.where(same, k[(S5_CHUNK - 1 - lag) * LANES:(S5_CHUNK - lag) * LANES], 0.0)
        for t in range(S5_CHUNK):
            er, ei = powers[t + 1]
            zr, zi = _cmul(er, ei, cr, ci)
            wor_ref[pl.ds(t * LANES, LANES), :] = zr
            woi_ref[pl.ds(t * LANES, LANES), :] = -zi
        pr_ref[...] = powers[S5_CHUNK][0]
        pi_ref[...] = powers[S5_CHUNK][1]

    return pl.pallas_call(
        body, name="s5_param_fwd", grid=(S5_OCTETS,),
        in_specs=[OCT_TILE] * 7 + [ANY] * len(extra),
        out_specs=[OCT_KD, OCT_W, OCT_W, OCT_W, OCT_W, OCT_TILE, OCT_TILE],
        out_shape=[_sds((S5_OCTETS, S5_CHUNK, LANES, LANES))] + [_sds((S5_OCTETS, S5_OCT_IN, LANES))] * 4
                  + [_sds((S5_OCTETS, LANES, LANES))] * 2,
        compiler_params=_params(("parallel",)),
    )(*tiles, *extra)


def s5_param_bwd(tiles, dkd, dws_re, dws_im, dwo_re, dwo_im, dp_re, dp_im):
    def body(ar_ref, ai_ref, ls_ref, br_ref, bi_ref, cr_ref, ci_ref, dkd_ref, dwsr_ref, dwsi_ref, dwor_ref, dwoi_ref, dpr_ref, dpi_ref,
             dar_ref, dai_ref, dls_ref, dbr_ref, dbi_ref, dcr_ref, dci_ref):
        ar = ar_ref[...]
        ai = ai_ref[...]
        br = br_ref[...]
        bi = bi_ref[...]
        cr = cr_ref[...]
        ci = ci_ref[...]
        dt, powers, (fr, fi), (bbr, bbi), den = _s5_discretise(ar, ai, ls_ref[...], br, bi)
        once = _first_copy_lanes()
        crm = jnp.where(once, cr, 0.0)
        cim = jnp.where(once, ci, 0.0)
        same = _same_group_tile()
        zero = jnp.zeros((LANES, LANES), F32)
        dpow = [[zero, zero] for _ in range(N_LAGS)]
        dbbr, dbbi = zero, zero
        by_step = [S5_CHUNK - 1 - s for s in range(S5_CHUNK)]
        x_all = [_cmul(*powers[lag], bbr, bbi) for lag in by_step]
        xr_all = jnp.concatenate([x[0] for x in x_all], axis=0)
        xi_all = jnp.concatenate([x[1] for x in x_all], axis=0)
        g_all = jnp.concatenate([jnp.where(same, dkd_ref[lag], 0.0) for lag in by_step], axis=0)
        dxr_all = dwsr_ref[...] + _mm_f32(g_all, crm, ((1,), (0,)))
        dxi_all = dwsi_ref[...] - _mm_f32(g_all, cim, ((1,), (0,)))
        dcr = jnp.where(once, _mm_f32(g_all, xr_all, ((0,), (0,))), 0.0)
        dci = -jnp.where(once, _mm_f32(g_all, xi_all, ((0,), (0,))), 0.0)
        for lag in range(S5_CHUNK):
            er, ei = powers[lag]
            rows = slice((S5_CHUNK - 1 - lag) * LANES, (S5_CHUNK - lag) * LANES)
            dxr = dxr_all[rows]
            dxi = dxi_all[rows]
            a, b = _cmul_conj(bbr, bbi, dxr, dxi)
            dpow[lag][0] = dpow[lag][0] + a
            dpow[lag][1] = dpow[lag][1] + b
            a, b = _cmul_conj(er, ei, dxr, dxi)
            dbbr = dbbr + a
            dbbi = dbbi + b
        for t in range(S5_CHUNK):
            er, ei = powers[t + 1]
            dzr = dwor_ref[pl.ds(t * LANES, LANES), :]
            dzi = -dwoi_ref[pl.ds(t * LANES, LANES), :]
            a, b = _cmul_conj(cr, ci, dzr, dzi)
            dpow[t + 1][0] = dpow[t + 1][0] + a
            dpow[t + 1][1] = dpow[t + 1][1] + b
            a, b = _cmul_conj(er, ei, dzr, dzi)
            dcr = dcr + a
            dci = dci + b
        dpow[S5_CHUNK][0] = dpow[S5_CHUNK][0] + dpr_ref[...]
        dpow[S5_CHUNK][1] = dpow[S5_CHUNK][1] + dpi_ref[...]
        dfr, dfi = _cmul_conj(br, bi, dbbr, dbbi)
        dbr, dbi = _cmul_conj(fr, fi, dbbr, dbbi)
        dnr, dni = _cmul(ar / den, ai / den, dfr, dfi)
        qr = (fr * ar + fi * ai) / den
        qi = (fi * ar - fr * ai) / den
        dlr, dli = _cmul(-qr, qi, dfr, dfi)
        dpow[1][0] = dpow[1][0] + dnr
        dpow[1][1] = dpow[1][1] + dni
        dxr, dxi = zero, zero
        for lag in range(1, N_LAGS):
            a, b = _cmul_conj(powers[lag][0], powers[lag][1], dpow[lag][0], dpow[lag][1])
            dxr = dxr + lag * a
            dxi = dxi + lag * b
        dar_ref[...] = dlr + dt * dxr
        dai_ref[...] = dli + dt * dxi
        dls_ref[...] = dt * (ar * dxr + ai * dxi)
        dbr_ref[...] = dbr
        dbi_ref[...] = dbi
        dcr_ref[...] = dcr
        dci_ref[...] = dci

    return pl.pallas_call(
        body, name="s5_param_bwd", grid=(S5_OCTETS,),
        in_specs=[OCT_TILE] * 7 + [OCT_KD, OCT_W, OCT_W, OCT_W, OCT_W, OCT_TILE, OCT_TILE], out_specs=[OCT_TILE] * 7,
        out_shape=[_sds((S5_OCTETS, LANES, LANES))] * 7,
        compiler_params=_params(("parallel",)),
    )(*tiles, dkd, dws_re, dws_im, dwo_re, dwo_im, dp_re, dp_im)


def _doubled(v):
    return jnp.concatenate([v, v], axis=-1)


def _s5_param_tiles(a_re, a_im, log_step, b_re, b_im, c_re, c_im):
    def per_group(a):
        return _doubled(jnp.broadcast_to(a.reshape(S5_OCTETS, S5_OCT, 1, SSM_STATE),
                                         (S5_OCTETS, S5_OCT, SSM_GROUP, SSM_STATE)).reshape(S5_OCTETS, LANES, SSM_STATE))

    ls = jnp.broadcast_to(log_step.reshape(S5_OCTETS, S5_OCT, 1, 1), (S5_OCTETS, S5_OCT, SSM_GROUP, LANES)).reshape(S5_OCTETS, LANES, LANES)
    bt = lambda b: _doubled(b.transpose(0, 2, 1).reshape(S5_OCTETS, LANES, SSM_STATE))
    ct = lambda c: _doubled(c.reshape(S5_OCTETS, LANES, SSM_STATE))
    return [per_group(a_re), per_group(a_im), ls, bt(b_re), bt(b_im), ct(c_re), ct(c_im)]


def _s5_param_grads(dtiles):
    dar, dai, dls, dbr, dbi, dcr, dci = dtiles
    halves = lambda d: d[..., :SSM_STATE] + d[..., SSM_STATE:]
    per_group = lambda d: halves(d).reshape(SSM_GROUPS, SSM_GROUP, SSM_STATE).sum(axis=1)
    per_row = lambda d: halves(d).reshape(SSM_GROUPS, SSM_GROUP, SSM_STATE)
    return (per_group(dar), per_group(dai), dls.reshape(SSM_GROUPS, SSM_GROUP * LANES).sum(axis=1),
            per_row(dbr).transpose(0, 2, 1), per_row(dbi).transpose(0, 2, 1), per_row(dcr), per_row(dci))


def _group_power_rows(tile):
    return tile[:, ::SSM_GROUP, :SSM_STATE].reshape(1, S5_STATES)


def _group_power_tiles(row):
    t = jnp.pad(row.reshape(S5_OCTETS, S5_OCT, 1, SSM_STATE), ((0, 0), (0, 0), (0, SSM_GROUP - 1), (0, LANES - SSM_STATE)))
    return t.reshape(S5_OCTETS, LANES, LANES)


def _rope_tables(t):
    pos = jnp.arange(t, dtype=F32)
    inv_freq = ROPE_THETA ** (-jnp.arange(0, HEAD_DIM, 2, dtype=F32) / HEAD_DIM)
    ang = pos[:, None] * inv_freq[None, :]
    cos = jnp.cos(ang)
    sin = jnp.sin(ang)
    cos64 = jnp.concatenate([cos, cos], axis=1)
    sin64 = jnp.concatenate([-sin, sin], axis=1)
    return jnp.concatenate([cos64, cos64], axis=1), jnp.concatenate([sin64, sin64], axis=1)


def _row(v):
    return v.reshape(1, -1)


def _s5_matrices(w, token=None):
    tiles = _s5_param_tiles(w["a_re"], w["a_im"], w["log_step"], w["b_re"], w["b_im"], w["c_re"], w["c_im"])
    kd, ws_re, ws_im, wo_re, wo_im, p_re, p_im = s5_param_fwd(tiles, token)
    return tiles, dict(kd=kd, ws_re=ws_re, ws_im=ws_im, wo_re=wo_re, wo_im=wo_im, a_re=_group_power_rows(p_re), a_im=_group_power_rows(p_im))


def _ssm_forward(x, w):
    tiles, mats = w["s5"] if "s5" in w else _s5_matrices(w)
    u, gate = ssm_proj_fwd(x, _row(w["norm"]), w["w_in"])
    s_re, s_im = s5_chunk_states(u, mats["ws_re"], mats["ws_im"])
    h_re, h_im = s5_scan_fwd(s_re, s_im, mats["a_re"], mats["a_im"])
    y_scan = s5_outputs(u, h_re, h_im, mats["kd"], mats["wo_re"], mats["wo_im"])
    y, g2, x_new = ssm_mix_fwd(x, u, gate, y_scan, _row(w["d"]), w["w_glu"], _row(w["b_glu"]), w["w_out"])
    saved = dict(x=x, u=u, gate=gate, y=y, g2=g2, h_re=h_re, h_im=h_im, mats=mats, tiles=tiles)
    return x_new, saved


def _ssm_backward(dxo, w, s, token=None):
    dy, dgate, dw_out, dw_glu, db_glu, dd = ssm_mix_bwd(dxo, s["u"], s["gate"], s["y"], s["g2"], w["w_glu"], w["w_out"], token)
    mats = s["mats"]
    dh_re, dh_im = s5_state_grads(dy, mats["wo_re"], mats["wo_im"])
    ds_re, ds_im, da_re, da_im = s5_scan_bwd(dh_re, dh_im, s["h_re"], s["h_im"], mats["a_re"], mats["a_im"])
    du_scan = s5_input_grads(dy, ds_re, ds_im, mats["kd"], mats["ws_re"], mats["ws_im"])
    dkd, dws_re, dws_im, dwo_re, dwo_im = s5_weight_grads(s["u"], dy, s["h_re"], s["h_im"], ds_re, ds_im)
    dparams = _s5_param_grads(s5_param_bwd(s["tiles"], dkd, dws_re, dws_im, dwo_re, dwo_im,
                                           _group_power_tiles(da_re), _group_power_tiles(da_im)))
    dx, dw_in, dnorm = ssm_proj_bwd(s["x"], _row(w["norm"]), dxo, dy, du_scan, dgate, _row(w["d"]), w["w_in"])
    grads = dict(norm=dnorm, w_in=dw_in, d=dd, w_glu=dw_glu, b_glu=db_glu, w_out=dw_out)
    for name, val in zip(("a_re", "a_im", "log_step", "b_re", "b_im", "c_re", "c_im"), dparams):
        grads[name] = val
    return dx, grads


def _attn_forward(x, w, cos2, sin2, loss_head=None):
    q, k, v, gate = attn_proj_fwd(x, _row(w["norm"]), w["w_in"], cos2, sin2)
    o, lse = attn_fwd(q, k, v, w["sinks"])
    if loss_head is None:
        result = attn_out_fwd(x, o, gate, w["w_out"])
    else:
        result = attn_out_loss(x, o, gate, w["w_out"], _row(loss_head[0]), loss_head[1])
    return result, dict(x=x, q=q, k=k, v=v, gate=gate, o=o, lse=lse)


def _attn_backward(dxo, w, s, cos2, sin2, token=None):
    do, dgate, dw_out = attn_out_bwd(dxo, s["o"], s["gate"], w["w_out"], token)
    dq, dk, dv, dsinks = attn_bwd(s["q"], s["k"], s["v"], w["sinks"], s["o"], s["lse"], do)
    dx, dw_in, dnorm = attn_proj_bwd(s["x"], _row(w["norm"]), dxo, dq, dk, dv, dgate, cos2, sin2, w["w_in"])
    return dx, dict(norm=dnorm, w_in=dw_in, sinks=dsinks[0, :N_Q_HEADS], w_out=dw_out)


class _NoExchanges:
    def __init__(self, layers):
        self.layers = layers

    def layer(self, i, x):
        return self.layers[i]

    def layer_done(self, i, grads, dx):
        return None


def _sequence_step(x, target, final_norm, hooks, depth=4):
    cos2, sin2 = _rope_tables(x.shape[0])
    saved, layers = [], []
    for i in range(depth):
        w = hooks.layer(i, x)
        layers.append(w)
        if i % 2 == 0:
            x, s = _ssm_forward(x, w)
        else:
            x, s = _attn_forward(x, w, cos2, sin2, (final_norm, target) if i == depth - 1 else None)
        saved.append(s)
    loss, dx, dfinal = x
    grads = {"final_norm": dfinal}
    token = None
    for i in reversed(range(depth)):
        if i % 2 == 0:
            dx, g = _ssm_backward(dx, layers[i], saved[i], token)
        else:
            dx, g = _attn_backward(dx, layers[i], saved[i], cos2, sin2, token)
        g = {"l%d_%s" % (i, name): val for name, val in g.items()}
        grads.update(g)
        token = hooks.layer_done(i, g, dx)
    return loss[0, 0], dx, grads


ANY = pl.BlockSpec(memory_space=pl.ANY)


def _place():
    return lax.axis_index("x"), lax.axis_index("y"), lax.axis_index("c")


def _other_chips(x, y):
    return [(1 - x, y), (x, 1 - y), (1 - x, 1 - y)]


class _StagedCopies:
    def __init__(self, bufs, load_sems, store_sems):
        self.bufs, self.load_sems, self.store_sems = bufs, load_sems, store_sems
        self.loads, self.stores = [], []

    def load(self, i, src):
        cp = pltpu.make_async_copy(src, self.bufs[i], self.load_sems.at[i])
        cp.start()
        self.loads.append(cp)

    def store(self, i, dst):
        self.loads[i].wait()
        cp = pltpu.make_async_copy(self.bufs[i], dst, self.store_sems.at[i])
        cp.start()
        self.stores.append(cp)

    def finish(self):
        for cp in self.stores:
            cp.wait()


def _staging(blocks):
    n = len(blocks)
    return [pltpu.VMEM(b.shape, b.dtype) for b in blocks] + [pltpu.SemaphoreType.DMA((n,)), pltpu.SemaphoreType.DMA((n,))]


def exchange_halves_with_sibling(grads):
    n = len(grads)

    def body(*refs):
        ins, outs = refs[:n], refs[n:2 * n]
        send_sems, recv_sems = refs[2 * n:]
        x, y, c = _place()
        copies = []
        for i in range(n):
            half = ins[i].shape[1] // 2
            src = ins[i].at[:, pl.ds((1 - c) * half, half), :]
            cp = pltpu.make_async_remote_copy(src_ref=src, dst_ref=outs[i], send_sem=send_sems.at[i], recv_sem=recv_sems.at[i],
                                              device_id=(x, y, 1 - c), device_id_type=MESH)
            cp.start()
            copies.append(cp)
        for cp in copies:
            cp.wait()

    return pl.pallas_call(
        body, name="exchange_halves_with_sibling",
        in_specs=[ANY] * n, out_specs=[ANY] * n,
        out_shape=[_sds((g.shape[0], g.shape[1] // 2, g.shape[2])) for g in grads],
        scratch_shapes=[pltpu.SemaphoreType.DMA((n,)), pltpu.SemaphoreType.DMA((n,))],
    )(*grads)


def swap_halves_with_sibling(pieces):
    n = len(pieces)

    def body(*refs):
        ins, outs = refs[:n], refs[n:2 * n]
        send_sems, recv_sems = refs[2 * n:2 * n + 2]
        own = _StagedCopies(refs[2 * n + 2:3 * n + 2], *refs[3 * n + 2:])
        x, y, c = _place()
        for i in range(n):
            own.load(i, ins[i])
        swaps = []
        for i in range(n):
            cp = pltpu.make_async_remote_copy(src_ref=ins[i], dst_ref=outs[i].at[c], send_sem=send_sems.at[i], recv_sem=recv_sems.at[i],
                                              device_id=(x, y, 1 - c), device_id_type=MESH)
            cp.start()
            swaps.append(cp)
        for i in range(n):
            own.store(i, outs[i].at[c])
        for i in range(n):
            pltpu.make_async_remote_copy(src_ref=ins[i], dst_ref=outs[i].at[1 - c], send_sem=send_sems.at[i], recv_sem=recv_sems.at[i],
                                         device_id=(x, y, 1 - c), device_id_type=MESH).wait_recv()
        for cp in swaps:
            cp.wait_send()
        own.finish()

    return pl.pallas_call(
        body, name="swap_halves_with_sibling",
        in_specs=[ANY] * n, out_specs=[ANY] * n,
        out_shape=[_sds((2,) + p.shape) for p in pieces],
        scratch_shapes=[pltpu.SemaphoreType.DMA((n,)), pltpu.SemaphoreType.DMA((n,))] + _staging(pieces),
        compiler_params=_params(),
    )(*pieces)


IN_HBM = pl.BlockSpec(memory_space=pltpu.HBM)
SEMAPHORES = pl.BlockSpec(memory_space=pltpu.SEMAPHORE)
DATAFLOW = pltpu.SideEffectType.DATAFLOW_SIDE_EFFECTING


def _hbm(a):
    return pltpu.with_memory_space_constraint(a, pltpu.HBM)


def place_own_blocks(shards):
    n = len(shards)

    def body(*refs):
        ins, outs = refs[:n], refs[n:2 * n]
        own = _StagedCopies(refs[2 * n:3 * n], *refs[3 * n:])
        x, y, _ = _place()
        for i in range(n):
            own.load(i, ins[i])
        for i in range(n):
            own.store(i, outs[i].at[2 * x + y])
        own.finish()

    return pl.pallas_call(
        body, name="place_own_blocks", in_specs=[ANY] * n, out_specs=[ANY] * n,
        out_shape=[_sds((4,) + s.shape, s.dtype) for s in shards],
        scratch_shapes=_staging(shards), compiler_params=_params(),
    )(*shards)


def _block_to_send(ref, chip, per_target):
    if not per_target:
        return ref
    return ref.at[chip] if ref.shape[0] == 4 else ref.at[0]


def start_sends_to_chips(name, sources, landings, per_target, after):
    n = len(sources)
    n_sems = 2 * 3 * n

    def body(*refs):
        srcs = refs[:n]
        sems = refs[2 * n + 1:2 * n + 1 + n_sems]
        lands = refs[2 * n + 1 + n_sems:3 * n + 1 + n_sems]
        token = refs[3 * n + 1 + n_sems]
        x, y, c = _place()
        me = 2 * x + y
        for i in range(n):
            for k, (tx, ty) in enumerate(_other_chips(x, y)):
                src = _block_to_send(srcs[i], 2 * tx + ty, per_target)
                pltpu.make_async_remote_copy(src_ref=src, dst_ref=lands[i].at[me], send_sem=sems[2 * (3 * i + k)], recv_sem=sems[2 * (3 * i + k) + 1],
                                             device_id=(tx, ty, c), device_id_type=MESH).start()
        token[...] = jnp.zeros_like(token)

    outs = pl.pallas_call(
        body, name=name,
        in_specs=[IN_HBM] * (2 * n) + [ANY],
        out_specs=[SEMAPHORES] * n_sems + [IN_HBM] * n + [pl.BlockSpec(memory_space=pltpu.VMEM)],
        out_shape=[pltpu.SemaphoreType.DMA(())] * n_sems + [pltpu.HBM(l.shape, l.dtype) for l in landings] + [_sds(TOKEN_SHAPE)],
        input_output_aliases={n + i: n_sems + i for i in range(n)},
        compiler_params=pltpu.CompilerParams(has_side_effects=DATAFLOW),
    )(*[_hbm(s) for s in sources], *[_hbm(l) for l in landings], after)
    return list(outs[:n_sems]), list(outs[n_sems:n_sems + n]), outs[n_sems + n]


def wait_sends_to_chips(name, sources, landings, sems, per_target, after):
    n = len(sources)
    n_sems = len(sems)

    def body(*refs):
        srcs = refs[:n]
        sem_refs = refs[2 * n:2 * n + n_sems]
        lands = refs[2 * n + n_sems + 1:]
        x, y, c = _place()
        me = 2 * x + y
        for i in range(n):
            for k, (tx, ty) in enumerate(_other_chips(x, y)):
                src = _block_to_send(srcs[i], me, per_target)
                cp = pltpu.make_async_remote_copy(src_ref=src, dst_ref=lands[i].at[2 * tx + ty], send_sem=sem_refs[2 * (3 * i + k)],
                                                  recv_sem=sem_refs[2 * (3 * i + k) + 1], device_id=(tx, ty, c), device_id_type=MESH)
                cp.wait_send()
                cp.wait_recv()

    return pl.pallas_call(
        body, name=name,
        in_specs=[IN_HBM] * (2 * n) + [SEMAPHORES] * n_sems + [ANY],
        out_specs=[IN_HBM] * n,
        out_shape=[pltpu.HBM(l.shape, l.dtype) for l in landings],
        input_output_aliases={n + i: i for i in range(n)},
        compiler_params=pltpu.CompilerParams(has_side_effects=DATAFLOW),
    )(*[_hbm(s) for s in sources], *landings, *sems, after)


def _row_tile(rows, cols):
    tm = rows
    while tm * cols * 4 > (2 << 20) and tm % 16 == 0:
        tm //= 2
    return tm


def add_pairs(half, a_list, b_list, out_dtypes, copies=1):
    n = len(a_list)
    nb = a_list[0].shape[0]

    def body(half_ref, *refs):
        for i in range(n):
            total = (refs[i][...] + refs[n + i][...]).astype(out_dtypes[i])
            for o_ref in refs[2 * n + i * copies:2 * n + (i + 1) * copies]:
                o_ref[...] = total

    halves = [pl.BlockSpec((None,) + b.shape[1:], lambda j, h: (j, h[0], 0)) for b in b_list]
    whole = [pl.BlockSpec((None,) + b.shape[1:], lambda j, h: (j, 0, 0)) for b in b_list]
    outs = pl.pallas_call(
        body, name="add_pairs",
        grid_spec=pltpu.PrefetchScalarGridSpec(num_scalar_prefetch=1, grid=(nb,), in_specs=halves + whole,
                                               out_specs=[s for s in whole for _ in range(copies)]),
        out_shape=[_sds(b.shape, dt) for b, dt in zip(b_list, out_dtypes) for _ in range(copies)],
        compiler_params=_params(("parallel",)),
    )(half, *a_list, *b_list)
    return [tuple(outs[i * copies:(i + 1) * copies]) for i in range(n)]


def sum_fours(arrays, token=None):
    n = len(arrays)
    extra, extra_specs = _after(token)
    steps = 2 if all(a.shape[1] % 32 == 0 for a in arrays) else 1

    def body(*refs):
        outs = refs[-n:]
        for a_ref, o_ref in zip(refs[:n], outs):
            o_ref[...] = ((a_ref[0].astype(F32) + a_ref[1].astype(F32)) + a_ref[2].astype(F32)) + a_ref[3].astype(F32)

    return pl.pallas_call(
        body, name="sum_fours", grid=(steps,),
        in_specs=[pl.BlockSpec((4, a.shape[1] // steps, a.shape[2]), lambda i: (0, i, 0)) for a in arrays] + extra_specs,
        out_specs=[pl.BlockSpec((a.shape[1] // steps, a.shape[2]), lambda i: (i, 0)) for a in arrays],
        out_shape=[_sds(a.shape[1:]) for a in arrays], compiler_params=_params(("parallel",)),
    )(*arrays, *extra)


def _adamw_update(w_ref, g_ref, m_ref, v_ref, d_ref, nm_ref, nv_ref):
    g = g_ref[...]
    nm = ADAM_B1 * m_ref[...] + (1.0 - ADAM_B1) * g
    nv = ADAM_B2 * v_ref[...] + (1.0 - ADAM_B2) * (g * g)
    d_ref[...] = -ADAM_LR * ((nm / (1.0 - ADAM_B1 ** ADAM_STEP)) / (jnp.sqrt(nv / (1.0 - ADAM_B2 ** ADAM_STEP)) + ADAM_EPS) + ADAM_WD * w_ref[...])
    nm_ref[...] = nm
    nv_ref[...] = nv


def adamw(w, g, m, v):
    rows, cols = w.shape
    tm = _row_tile(rows, cols)

    def body(*refs):
        _adamw_update(*refs)

    spec = pl.BlockSpec((tm, cols), lambda i: (i, 0))
    return pl.pallas_call(
        body, name="adamw", grid=(rows // tm,), in_specs=[spec] * 4, out_specs=[spec] * 3,
        out_shape=[_sds(w.shape)] * 3, compiler_params=_params(("parallel",)),
    )(w, g, m, v)


def adamw_small(ws, gs, ms, vs, slabs=None):
    n = len(ws)

    def body(*refs):
        for i in range(n):
            _adamw_update(refs[i], refs[n + i], refs[2 * n + i], refs[3 * n + i], refs[4 * n + i], refs[5 * n + i], refs[6 * n + i])

    if slabs is None:
        grid = ()
        specs = [pl.BlockSpec(memory_space=pltpu.VMEM)] * n
    else:
        grid = (slabs,)
        specs = [pl.BlockSpec((w.shape[0] // slabs,) + w.shape[1:], lambda i: (i, 0, 0)) for w in ws]
    outs = pl.pallas_call(
        body, name="adamw_small", grid=grid, in_specs=specs * 4, out_specs=specs * 3,
        out_shape=[_sds(w.shape) for w in ws] * 3, compiler_params=_params(("parallel",) if slabs else None),
    )(*ws, *gs, *ms, *vs)
    return outs[:n], outs[n:2 * n], outs[2 * n:]


PACK_TILE = 8 * LANES
PACK_PIECES = 8
PACK_ALIGN = PACK_PIECES * 16


def _pack_small(values, scalar=None):
    parts = []
    for name in SMALL_NAMES:
        flat = values[name].reshape(-1)
        pad = (-flat.shape[0]) % PACK_TILE
        if pad:
            flat = jnp.concatenate([flat, jnp.zeros((pad,), F32)])
        parts.append(flat.reshape(-1, LANES))
    rows = sum(p.shape[0] for p in parts) + 8
    parts.append(jnp.zeros(((-rows) % PACK_ALIGN, LANES), F32))
    last = jnp.zeros((8, LANES), F32)
    parts.append(last if scalar is None else jnp.broadcast_to(scalar.astype(F32), (8, LANES)))
    return jnp.concatenate(parts, axis=0)


def _pack_row_of(name, like):
    row = 0
    for other in SMALL_NAMES:
        if other == name:
            return row
        row += -(-math.prod(like[other].shape) // PACK_TILE) * 8
    raise KeyError(name)


def _unpack_small(pack, like):
    out = {}
    row = 0
    for name in SMALL_NAMES:
        size = math.prod(like[name].shape)
        rows = -(-size // PACK_TILE) * 8
        out[name] = pack[row:row + rows].reshape(-1)[:size].reshape(like[name].shape)
        row += rows
    return out


def _travels_transposed(name, shard):
    return name.endswith("w_in") and shard.shape[-1] % LANES != 0


def _to_blocks(name, full):
    if full.ndim == 3:
        return full
    return full.reshape(4, full.shape[0] // 4, full.shape[1])


def _from_blocks(name, stacked):
    if name.endswith("w_in") and stacked.shape[2] % LANES == 0 and stacked.shape[1] == D_MODEL:
        return stacked
    return stacked.reshape(4 * stacked.shape[1], stacked.shape[2])


def _layer_big_names(i):
    return [n for n in BIG_NAMES if n.startswith("l%d_" % i)]


class _OverlappedExchanges:
    def __init__(self, weights):
        self.weights = weights
        self.c = lax.axis_index("c")
        self.first = _layer_big_names(0)
        self.later = [n for n in BIG_NAMES if n not in self.first]
        shards = [weights[n].astype(MXU_DTYPE) for n in self.first + self.later]
        shards = [s.T if _travels_transposed(n, s) else s for n, s in zip(self.first + self.later, shards)]
        placed = place_own_blocks(shards)
        k = len(self.first)
        sems, stacks, token = start_sends_to_chips("gather_first_start", shards[:k], placed[:k], False, shards[0])
        self.gather_first = (shards[:k], sems, stacks)
        sems, stacks, token = start_sends_to_chips("gather_later_start", shards[k:], placed[k:], False, token)
        self.gather_later = (shards[k:], sems, stacks)
        self.s5 = {}
        for i in (0, 2):
            self.s5[i] = _s5_matrices({n: weights["l%d_%s" % (i, n)] for n in SSM_NAMES if "l%d_%s" % (i, n) in SMALL_NAMES}, token)
            token = self.s5[i][1]["kd"]
        self.full = {}
        self.in_flight = {}
        self.contributions = {}

    def layer(self, i, x):
        if i == 0:
            shards, sems, stacks = self.gather_first
            stacks = wait_sends_to_chips("gather_first_wait", shards, stacks, sems, False, self.s5[2][1]["kd"])
            self.full.update({n: _from_blocks(n, g) for n, g in zip(self.first, stacks)})
        if i == 1:
            shards, sems, stacks = self.gather_later
            stacks = wait_sends_to_chips("gather_later_wait", shards, stacks, sems, False, x)
            self.full.update({n: _from_blocks(n, g) for n, g in zip(self.later, stacks)})
        names = SSM_NAMES if i % 2 == 0 else ATTN_NAMES
        w = {n: self.full.get("l%d_%s" % (i, n), self.weights.get("l%d_%s" % (i, n))) for n in names}
        if i in self.s5:
            w["s5"] = self.s5[i]
        return w

    def chip_sums(self, names, grads, extra_blocks=(), extra_dtypes=(), copies=1):
        blocks = [_to_blocks(n, grads[n]) for n in names] + list(extra_blocks)
        from_sibling = exchange_halves_with_sibling(blocks)
        k = len(names)
        half = self.c.reshape(1).astype(jnp.int32)
        sums = add_pairs(half, blocks[:k], from_sibling[:k], [WIRE_DTYPE] * k, copies)
        if extra_blocks:
            sums += add_pairs(half, blocks[k:], from_sibling[k:], list(extra_dtypes), copies)
        return sums

    def layer_done(self, i, grads, dx):
        if i + 1 in self.in_flight:
            names, sums, sems, landings = self.in_flight.pop(i + 1)
            done = wait_sends_to_chips("scatter_wait_l%d" % (i + 1), sums, landings, sems, True, dx)
            self.contributions.update(zip(names, done))
        if i == 0:
            return None
        names = _layer_big_names(i)
        pairs = self.chip_sums(names, grads, copies=2)
        sums = [p[0] for p in pairs]
        sems, landings, token = start_sends_to_chips("scatter_start_l%d" % i, sums, [p[1] for p in pairs], True, sums[0])
        self.in_flight[i] = (names, sums, sems, landings)
        return token


def _train_step(x, loss_target, weights, moments_m, moments_v):
    hooks = _OverlappedExchanges(weights)
    loss, dx, grads = _sequence_step(x[0], loss_target[0], weights["final_norm"], hooks)
    small_pack = _pack_small({n: grads[n] for n in SMALL_NAMES}, scalar=loss)
    tail = small_pack.shape[0] - _pack_row_of("final_norm", grads)
    small = [small_pack[None, :-tail], small_pack[None, -tail:]]
    last = _layer_big_names(0)
    pairs = hooks.chip_sums(last, grads, extra_blocks=small, extra_dtypes=[WIRE_DTYPE, F32], copies=2)
    sums = [p[0] for p in pairs]
    landings = [p[1] for p in pairs[:-2]] + [jnp.broadcast_to(s, (4,) + s.shape[1:]) for s in sums[-2:]]
    sems, landings, token = start_sends_to_chips("scatter_start_l0", sums, landings, True, sums[0])
    out_grad, out_delta, out_m, out_v = {}, {}, {}, {}

    def finish(names, arrays, token=None):
        shared = swap_halves_with_sibling(sum_fours(arrays, token))
        rest = []
        for n, s in zip(names, shared):
            if n not in weights:
                rest.append(s.reshape(-1, LANES))
                continue
            out_grad[n] = s.reshape(2 * s.shape[1], s.shape[2])
            if _travels_transposed(n, weights[n]):
                out_grad[n] = out_grad[n].T
            out_delta[n], out_m[n], out_v[n] = adamw(weights[n], out_grad[n], moments_m[n], moments_v[n])
        return rest

    others = [n for n in BIG_NAMES if n not in last]
    finish(others, [hooks.contributions[n] for n in others], token)
    arrived = wait_sends_to_chips("scatter_wait_l0", sums, landings, sems, True, out_v[others[-1]])
    small_grad_pack = jnp.concatenate(finish(last + ["small", "small tail"], arrived), axis=0)
    loss = small_grad_pack[-8, 0]
    out_grad.update(_unpack_small(small_grad_pack, {n: weights[n] for n in SMALL_NAMES}))
    cubes = [n for n in SMALL_NAMES if weights[n].ndim == 3]
    for names, slabs in ((cubes, 8), ([n for n in SMALL_NAMES if n not in cubes], None)):
        deltas, new_ms, new_vs = adamw_small(*[[group[n] for n in names] for group in (weights, out_grad, moments_m, moments_v)], slabs=slabs)
        out_delta.update(zip(names, deltas))
        out_m.update(zip(names, new_ms))
        out_v.update(zip(names, new_vs))
    outs = [loss, dx[None]]
    for group in (out_grad, out_delta, out_m, out_v):
        outs.extend(group[n] for n in WEIGHT_NAMES)
    return tuple(outs)


def kernel(x, l0_norm, l0_w_in, l0_a_re, l0_a_im, l0_log_step, l0_b_re, l0_b_im, l0_c_re, l0_c_im, l0_d, l0_w_glu, l0_b_glu, l0_w_out, l1_norm, l1_w_in, l1_sinks, l1_w_out, l2_norm, l2_w_in, l2_a_re, l2_a_im, l2_log_step, l2_b_re, l2_b_im, l2_c_re, l2_c_im, l2_d, l2_w_glu, l2_b_glu, l2_w_out, l3_norm, l3_w_in, l3_sinks, l3_w_out, final_norm, loss_target, m_l0_norm, m_l0_w_in, m_l0_a_re, m_l0_a_im, m_l0_log_step, m_l0_b_re, m_l0_b_im, m_l0_c_re, m_l0_c_im, m_l0_d, m_l0_w_glu, m_l0_b_glu, m_l0_w_out, m_l1_norm, m_l1_w_in, m_l1_sinks, m_l1_w_out, m_l2_norm, m_l2_w_in, m_l2_a_re, m_l2_a_im, m_l2_log_step, m_l2_b_re, m_l2_b_im, m_l2_c_re, m_l2_c_im, m_l2_d, m_l2_w_glu, m_l2_b_glu, m_l2_w_out, m_l3_norm, m_l3_w_in, m_l3_sinks, m_l3_w_out, m_final_norm, v_l0_norm, v_l0_w_in, v_l0_a_re, v_l0_a_im, v_l0_log_step, v_l0_b_re, v_l0_b_im, v_l0_c_re, v_l0_c_im, v_l0_d, v_l0_w_glu, v_l0_b_glu, v_l0_w_out, v_l1_norm, v_l1_w_in, v_l1_sinks, v_l1_w_out, v_l2_norm, v_l2_w_in, v_l2_a_re, v_l2_a_im, v_l2_log_step, v_l2_b_re, v_l2_b_im, v_l2_c_re, v_l2_c_im, v_l2_d, v_l2_w_glu, v_l2_b_glu, v_l2_w_out, v_l3_norm, v_l3_w_in, v_l3_sinks, v_l3_w_out, v_final_norm):
    args = locals()
    weights = {n: args[n] for n in WEIGHT_NAMES}
    moments_m = {n: args["m_" + n] for n in WEIGHT_NAMES}
    moments_v = {n: args["v_" + n] for n in WEIGHT_NAMES}
    return _train_step(x, loss_target, weights, moments_m, moments_v)
```

```python
import functools
import math

import jax
import jax.numpy as jnp
from jax import lax
from jax.experimental import pallas as pl
from jax.experimental.pallas import tpu as pltpu

F32 = jnp.float32
MXU_DTYPE = jnp.bfloat16
WIRE_DTYPE = jnp.bfloat16
MESH = pl.DeviceIdType.MESH

D_MODEL = 1024
BRANCH = 1024
NORM_EPS = 1e-5
SSM_GROUPS = 64
SSM_GROUP = 16
SSM_STATE = 64
S5_CHUNK = 16
LANES = 128
S5_OCT = LANES // SSM_GROUP
S5_OCTETS = SSM_GROUPS // S5_OCT
S5_OCT_IN = S5_CHUNK * LANES
S5_OCT_STATE = S5_OCT * SSM_STATE
S5_STATES = SSM_GROUPS * SSM_STATE
HEAD_DIM = 64
N_Q_HEADS = 16
N_KV_HEADS = 2
GQA_GROUP = N_Q_HEADS // N_KV_HEADS
ATTN_BLOCK = 128
Q_DIM = N_Q_HEADS * HEAD_DIM
KV_DIM = N_KV_HEADS * HEAD_DIM
ROPE_THETA = 10000.0
NEG_INF = -1e30
ADAM_LR = 0.001
ADAM_B1 = 0.9
ADAM_B2 = 0.999
ADAM_EPS = 1e-08
ADAM_WD = 0.01
ADAM_STEP = 10

VMEM_LIMIT_V7X = 56 * 1024 * 1024
ROW_TILE_FWD = 512
ROW_TILE_BWD = 512

SSM_NAMES = ("norm", "w_in", "a_re", "a_im", "log_step", "b_re", "b_im", "c_re", "c_im", "d", "w_glu", "b_glu", "w_out")
ATTN_NAMES = ("norm", "w_in", "sinks", "w_out")


def _weight_names():
    names = []
    for i in range(4):
        for n in (SSM_NAMES if i % 2 == 0 else ATTN_NAMES):
            names.append("l%d_%s" % (i, n))
    names.append("final_norm")
    return names


WEIGHT_NAMES = _weight_names()
BIG_NAMES = [n for n in WEIGHT_NAMES if n.endswith(("w_in", "w_glu", "w_out"))]
SMALL_NAMES = [n for n in WEIGHT_NAMES if n not in BIG_NAMES]
EXACT_NAMES = [n for n in SMALL_NAMES if n.endswith(("log_step", "sinks")) or n == "final_norm"]
PACK_NAMES = [n for n in SMALL_NAMES if n not in EXACT_NAMES] + EXACT_NAMES


def _params(semantics=None):
    return pltpu.CompilerParams(dimension_semantics=semantics, vmem_limit_bytes=VMEM_LIMIT_V7X)


def _rows(tm, n):
    return pl.BlockSpec((tm, n), lambda i: (i, 0))


def _whole(shape):
    return pl.BlockSpec(shape, lambda i: (0,) * len(shape), pipeline_mode=pl.Buffered(1))


def _sds(shape, dtype=F32):
    return jax.ShapeDtypeStruct(shape, dtype)


def _mm(a, b):
    return jnp.dot(a.astype(MXU_DTYPE), b.astype(MXU_DTYPE), preferred_element_type=F32)


def _mm_tn(a, b):
    return lax.dot_general(a.astype(MXU_DTYPE), b.astype(MXU_DTYPE), (((0,), (0,)), ((), ())), preferred_element_type=F32)


def _mm_nt(a, b):
    return lax.dot_general(a.astype(MXU_DTYPE), b.astype(MXU_DTYPE), (((1,), (1,)), ((), ())), preferred_element_type=F32)


def _sigmoid(x):
    return 0.5 + 0.5 * jnp.tanh(0.5 * x)


def _silu(x):
    return x * _sigmoid(x)


def _silu_and_grad(x):
    s = _sigmoid(x)
    return x * s, s * (1.0 + x * (1.0 - s))


GELU_C0 = math.sqrt(2.0 / math.pi)
GELU_C1 = 0.044715


def _gelu(x):
    return 0.5 * x * (1.0 + jnp.tanh(GELU_C0 * (x + GELU_C1 * x * x * x)))


def _gelu_and_grad(x):
    x2 = x * x
    th = jnp.tanh(GELU_C0 * x * (1.0 + GELU_C1 * x2))
    half = 0.5 + 0.5 * th
    return x * half, half + 0.5 * x * (1.0 - th * th) * (GELU_C0 + 3.0 * GELU_C0 * GELU_C1 * x2)


def _rms(x, g):
    r = lax.rsqrt(jnp.mean(x * x, axis=-1, keepdims=True) + NORM_EPS)
    xhat = x * r
    return r, xhat, xhat * g


def _rms_bwd(dh, g, r, xhat):
    dxhat = dh * g
    dx = r * (dxhat - xhat * jnp.mean(dxhat * xhat, axis=-1, keepdims=True))
    return dx, jnp.sum(dh * xhat, axis=0, keepdims=True)


def _swap_half_heads(x):
    n = x.shape[-1]
    lane = lax.broadcasted_iota(jnp.int32, x.shape, x.ndim - 1)
    first = (lane % HEAD_DIM) < (HEAD_DIM // 2)
    return jnp.where(first, pltpu.roll(x, n - HEAD_DIM // 2, x.ndim - 1), pltpu.roll(x, HEAD_DIM // 2, x.ndim - 1))


def _tile_lanes(t, reps):
    return jnp.concatenate([t] * reps, axis=1)


TOKEN_SHAPE = (8, LANES)


def _after(token):
    return ([], []) if token is None else ([token], [_whole(TOKEN_SHAPE)])


def ssm_proj_fwd(x, norm, w_in):
    t = x.shape[0]
    tm = min(ROW_TILE_FWD, t)

    def body(x_ref, g_ref, w_ref, u_ref, gate_ref):
        _, _, h = _rms(x_ref[...], g_ref[...])
        h = h.astype(MXU_DTYPE)
        half = BRANCH // 2
        for j in range(2):
            u_ref[:, j * half:(j + 1) * half] = _mm(h, w_ref[j])
            gate_ref[:, j * half:(j + 1) * half] = _mm(h, w_ref[2 + j])

    return pl.pallas_call(
        body, name="ssm_proj_fwd", grid=(t // tm,),
        in_specs=[_rows(tm, D_MODEL), _whole((1, D_MODEL)), _whole((4, D_MODEL, BRANCH // 2))],
        out_specs=[_rows(tm, BRANCH), _rows(tm, BRANCH)],
        out_shape=[_sds((t, BRANCH)), _sds((t, BRANCH))],
        compiler_params=_params(("parallel",)),
    )(x, norm, w_in)


def _chunk_rows(ref, nk, dtype=None):
    rows = jnp.concatenate([ref[pl.ds(s, nk, stride=S5_CHUNK), :] for s in range(S5_CHUNK)], axis=1)
    return rows.astype(MXU_DTYPE if dtype is None else dtype)


def _store_chunk_rows(ref, val, nk):
    for s in range(S5_CHUNK):
        ref[pl.ds(s, nk, stride=S5_CHUNK), :] = val[:, s * LANES:(s + 1) * LANES]


def _own_group_mask():
    row = lax.broadcasted_iota(jnp.int32, (S5_OCT_IN, S5_OCT_STATE), 0)
    col = lax.broadcasted_iota(jnp.int32, (S5_OCT_IN, S5_OCT_STATE), 1)
    return ((row % LANES) // SSM_GROUP) == (col // SSM_STATE)


def _spread_groups(w):
    return jnp.where(_own_group_mask(), jnp.concatenate([w] * (S5_OCT_STATE // LANES), axis=1), 0.0).astype(MXU_DTYPE)


def _fold_groups(p):
    p = jnp.where(_own_group_mask(), p, 0.0)
    return sum(p[:, q * LANES:(q + 1) * LANES] for q in range(S5_OCT_STATE // LANES))


def _fill_toeplitz(win_ref, kd_ref):
    win_ref[...] = jnp.zeros_like(win_ref)
    for s in range(S5_CHUNK):
        for t in range(s, S5_CHUNK):
            win_ref[s * LANES:(s + 1) * LANES, t * LANES:(t + 1) * LANES] = kd_ref[t - s].astype(MXU_DTYPE)


TOEPLITZ_BLOCK = 512
_TOEPLITZ_BLOCKS = [(lo, lo + TOEPLITZ_BLOCK) for lo in range(0, S5_OCT_IN, TOEPLITZ_BLOCK)]


def _strip(t):
    return pl.BlockSpec((t, LANES), lambda b: (0, b))


def _oct_states(nk):
    return pl.BlockSpec((nk, S5_OCT_STATE), lambda b: (0, b))


OCT_W = pl.BlockSpec((None, S5_OCT_IN, LANES), lambda b: (b, 0, 0))
OCT_KD = pl.BlockSpec((None, S5_CHUNK, LANES, LANES), lambda b: (b, 0, 0, 0))


def s5_chunk_states(u, ws_re, ws_im):
    t = u.shape[0]
    nk = t // S5_CHUNK

    def body(u_ref, wr_ref, wi_ref, re_ref, im_ref):
        uc = _chunk_rows(u_ref, nk)
        re_ref[...] = _mm(uc, _spread_groups(wr_ref[...]))
        im_ref[...] = _mm(uc, _spread_groups(wi_ref[...]))

    return pl.pallas_call(
        body, name="s5_chunk_states", grid=(S5_OCTETS,),
        in_specs=[_strip(t), OCT_W, OCT_W], out_specs=[_oct_states(nk), _oct_states(nk)],
        out_shape=[_sds((nk, S5_STATES)), _sds((nk, S5_STATES))],
        compiler_params=_params(("parallel",)),
    )(u, ws_re, ws_im)


def s5_scan_fwd(s_re, s_im, a_re, a_im):
    nk = s_re.shape[0]

    def body(sre_ref, sim_ref, ar_ref, ai_ref, hre_ref, him_ref):
        ar = ar_ref[...]
        ai = ai_ref[...]

        def step(k, carry):
            hr, hi = carry
            hre_ref[pl.ds(k, 1), :] = hr
            him_ref[pl.ds(k, 1), :] = hi
            sr = sre_ref[pl.ds(k, 1), :]
            si = sim_ref[pl.ds(k, 1), :]
            return ar * hr - ai * hi + sr, ai * hr + ar * hi + si

        zero = jnp.zeros((1, S5_STATES), F32)
        lax.fori_loop(0, nk, step, (zero, zero))

    vm = pl.BlockSpec(memory_space=pltpu.VMEM)
    return pl.pallas_call(
        body, name="s5_scan_fwd", in_specs=[vm, vm, vm, vm], out_specs=[vm, vm],
        out_shape=[_sds((nk, S5_STATES)), _sds((nk, S5_STATES))],
        compiler_params=_params(),
    )(s_re, s_im, a_re, a_im)


def s5_outputs(u, h_re, h_im, kd, wo_re, wo_im):
    t = u.shape[0]
    nk = t // S5_CHUNK

    def body(u_ref, hre_ref, him_ref, kd_ref, wor_ref, woi_ref, y_ref, win_ref):
        _fill_toeplitz(win_ref, kd_ref)
        uc = _chunk_rows(u_ref, nk)
        y = jnp.concatenate([_mm(uc[:, :hi], win_ref[:hi, lo:hi]) for lo, hi in _TOEPLITZ_BLOCKS], axis=1)
        y = y + _mm_nt(hre_ref[...], _spread_groups(wor_ref[...])) + _mm_nt(him_ref[...], _spread_groups(woi_ref[...]))
        _store_chunk_rows(y_ref, y, nk)

    return pl.pallas_call(
        body, name="s5_outputs", grid=(S5_OCTETS,),
        in_specs=[_strip(t), _oct_states(nk), _oct_states(nk), OCT_KD, OCT_W, OCT_W],
        out_specs=_strip(t), out_shape=_sds((t, BRANCH)),
        scratch_shapes=[pltpu.VMEM((S5_OCT_IN, S5_OCT_IN), MXU_DTYPE)],
        compiler_params=_params(("parallel",)),
    )(u, h_re, h_im, kd, wo_re, wo_im)


def s5_state_grads(dy, wo_re, wo_im):
    t = dy.shape[0]
    nk = t // S5_CHUNK

    def body(dy_ref, wor_ref, woi_ref, re_ref, im_ref):
        dyc = _chunk_rows(dy_ref, nk)
        re_ref[...] = _mm(dyc, _spread_groups(wor_ref[...]))
        im_ref[...] = _mm(dyc, _spread_groups(woi_ref[...]))

    return pl.pallas_call(
        body, name="s5_state_grads", grid=(S5_OCTETS,),
        in_specs=[_strip(t), OCT_W, OCT_W], out_specs=[_oct_states(nk), _oct_states(nk)],
        out_shape=[_sds((nk, S5_STATES)), _sds((nk, S5_STATES))],
        compiler_params=_params(("parallel",)),
    )(dy, wo_re, wo_im)


def s5_scan_bwd(dh_re, dh_im, h_re, h_im, a_re, a_im):
    nk = dh_re.shape[0]

    def body(dhr_ref, dhi_ref, hr_ref, hi_ref, ar_ref, ai_ref, dsr_ref, dsi_ref, dar_ref, dai_ref):
        ar = ar_ref[...]
        ai = ai_ref[...]

        def step(i, carry):
            gr, gi = carry
            k = nk - 1 - i
            dhr = dhr_ref[pl.ds(k, 1), :]
            dhi = dhi_ref[pl.ds(k, 1), :]
            dsr_ref[pl.ds(k, 1), :] = gr
            dsi_ref[pl.ds(k, 1), :] = gi
            return dhr + ar * gr + ai * gi, dhi - ai * gr + ar * gi

        zero = jnp.zeros((1, S5_STATES), F32)
        lax.fori_loop(0, nk, step, (zero, zero))
        dsr, dsi, hr, hi = dsr_ref[...], dsi_ref[...], hr_ref[...], hi_ref[...]
        dar_ref[...] = jnp.sum(dsr * hr + dsi * hi, axis=0, keepdims=True)
        dai_ref[...] = jnp.sum(dsi * hr - dsr * hi, axis=0, keepdims=True)

    vm = pl.BlockSpec(memory_space=pltpu.VMEM)
    return pl.pallas_call(
        body, name="s5_scan_bwd", in_specs=[vm] * 6, out_specs=[vm] * 4,
        out_shape=[_sds((nk, S5_STATES)), _sds((nk, S5_STATES)), _sds((1, S5_STATES)), _sds((1, S5_STATES))],
        input_output_aliases={0: 0, 1: 1}, compiler_params=_params(),
    )(dh_re, dh_im, h_re, h_im, a_re, a_im)


def s5_input_grads(dy, ds_re, ds_im, kd, ws_re, ws_im):
    t = dy.shape[0]
    nk = t // S5_CHUNK

    def body(dy_ref, dsr_ref, dsi_ref, kd_ref, wsr_ref, wsi_ref, du_ref, win_ref):
        _fill_toeplitz(win_ref, kd_ref)
        dyc = _chunk_rows(dy_ref, nk)
        du = jnp.concatenate([_mm_nt(dyc[:, lo:], win_ref[lo:hi, lo:]) for lo, hi in _TOEPLITZ_BLOCKS], axis=1)
        du = du + _mm_nt(dsr_ref[...], _spread_groups(wsr_ref[...])) + _mm_nt(dsi_ref[...], _spread_groups(wsi_ref[...]))
        _store_chunk_rows(du_ref, du, nk)

    return pl.pallas_call(
        body, name="s5_input_grads", grid=(S5_OCTETS,),
        in_specs=[_strip(t), _oct_states(nk), _oct_states(nk), OCT_KD, OCT_W, OCT_W],
        out_specs=_strip(t), out_shape=_sds((t, BRANCH)),
        scratch_shapes=[pltpu.VMEM((S5_OCT_IN, S5_OCT_IN), MXU_DTYPE)],
        compiler_params=_params(("parallel",)),
    )(dy, ds_re, ds_im, kd, ws_re, ws_im)


def s5_weight_grads(u, dy, h_re, h_im, ds_re, ds_im):
    t = u.shape[0]
    nk = t // S5_CHUNK

    def body(u_ref, dy_ref, hre_ref, him_ref, dsr_ref, dsi_ref, dkd_ref, dwsr_ref, dwsi_ref, dwor_ref, dwoi_ref):
        dyc = _chunk_rows(dy_ref, nk, F32)
        uct = _chunk_rows(u_ref, nk, F32).T.astype(MXU_DTYPE)
        dyct = dyc.T.astype(MXU_DTYPE)
        dyc = dyc.astype(MXU_DTYPE)
        dwsr_ref[...] = _fold_groups(_mm(uct, dsr_ref[...]))
        dwsi_ref[...] = _fold_groups(_mm(uct, dsi_ref[...]))
        dwor_ref[...] = _fold_groups(_mm(dyct, hre_ref[...]))
        dwoi_ref[...] = _fold_groups(_mm(dyct, him_ref[...]))
        dkd_ref[...] = jnp.zeros_like(dkd_ref)
        for tt in range(0, S5_CHUNK, 2):
            p = _mm(uct[:(tt + 2) * LANES], dyc[:, tt * LANES:(tt + 2) * LANES])
            for s in range(tt + 2):
                rows = p[s * LANES:(s + 1) * LANES]
                if s <= tt:
                    dkd_ref[tt - s] += rows[:, :LANES]
                dkd_ref[tt + 1 - s] += rows[:, LANES:]

    return pl.pallas_call(
        body, name="s5_weight_grads", grid=(S5_OCTETS,),
        in_specs=[_strip(t), _strip(t)] + [_oct_states(nk)] * 4,
        out_specs=[OCT_KD, OCT_W, OCT_W, OCT_W, OCT_W],
        out_shape=[_sds((S5_OCTETS, S5_CHUNK, LANES, LANES))] + [_sds((S5_OCTETS, S5_OCT_IN, LANES))] * 4,
        compiler_params=_params(("parallel",)),
    )(u, dy, h_re, h_im, ds_re, ds_im)


def ssm_mix_fwd(x, u, gate, y_scan, d, w_glu, b_glu, w_out):
    t = x.shape[0]
    tm = min(ROW_TILE_FWD, t)

    def body(x_ref, u_ref, gate_ref, ys_ref, d_ref, wg_ref, bg_ref, wo_ref, y_ref, g2_ref, xo_ref):
        y = ys_ref[...] + d_ref[...] * u_ref[...]
        z0 = _gelu(y)
        g2 = _mm(z0, wg_ref[...]) + bg_ref[...]
        a = z0 * _sigmoid(g2) * _silu(gate_ref[...])
        y_ref[...] = y
        g2_ref[...] = g2
        xo_ref[...] = x_ref[...] + _mm(a, wo_ref[...])

    row = _rows(tm, BRANCH)
    vec = _whole((1, BRANCH))
    mat = _whole((BRANCH, BRANCH))
    return pl.pallas_call(
        body, name="ssm_mix_fwd", grid=(t // tm,),
        in_specs=[row, row, row, row, vec, mat, vec, mat],
        out_specs=[row, row, row],
        out_shape=[_sds((t, BRANCH))] * 3,
        compiler_params=_params(("parallel",)),
    )(x, u, gate, y_scan, d, w_glu, b_glu, w_out)


def ssm_mix_bwd(dxo, u, gate, y, g2, w_glu, w_out, token=None):
    t = dxo.shape[0]
    tm = min(ROW_TILE_BWD, t)
    extra, extra_specs = _after(token)

    def body(dxo_ref, u_ref, gate_ref, y_ref, g2_ref, wgt_ref, wot_ref, *rest):
        dy_ref, dgate_ref, dwo_ref, dwg_ref, dbg_ref, dd_ref = rest[-6:]

        @pl.when(pl.program_id(0) == 0)
        def _():
            dwo_ref[...] = jnp.zeros_like(dwo_ref)
            dwg_ref[...] = jnp.zeros_like(dwg_ref)
            dbg_ref[...] = jnp.zeros_like(dbg_ref)
            dd_ref[...] = jnp.zeros_like(dd_ref)

        dxo = dxo_ref[...]
        gate = gate_ref[...]
        y = y_ref[...]
        z0, z0_grad = _gelu_and_grad(y)
        sg = _sigmoid(g2_ref[...])
        z = z0 * sg
        sgate, sgate_grad = _silu_and_grad(gate)
        da = _mm_nt(dxo, wot_ref[...])
        dwo_ref[...] += _mm_tn(z * sgate, dxo)
        dz = da * sgate
        dgate_ref[...] = da * z * sgate_grad
        dg2 = dz * z0 * sg * (1.0 - sg)
        dbg_ref[...] += jnp.sum(dg2, axis=0, keepdims=True)
        dwg_ref[...] += _mm_tn(z0, dg2)
        dz0 = dz * sg + _mm_nt(dg2, wgt_ref[...])
        dy = dz0 * z0_grad
        dd_ref[...] += jnp.sum(dy * u_ref[...], axis=0, keepdims=True)
        dy_ref[...] = dy

    row = _rows(tm, BRANCH)
    vec = _whole((1, BRANCH))
    mat = _whole((BRANCH, BRANCH))
    return pl.pallas_call(
        body, name="ssm_mix_bwd", grid=(t // tm,),
        in_specs=[row, row, row, row, row, mat, mat] + extra_specs,
        out_specs=[row, row, mat, mat, vec, vec],
        out_shape=[_sds((t, BRANCH)), _sds((t, BRANCH)), _sds((BRANCH, D_MODEL)), _sds((BRANCH, BRANCH)),
                   _sds((1, BRANCH)), _sds((1, BRANCH))],
        compiler_params=_params(("arbitrary",)),
    )(dxo, u, gate, y, g2, w_glu, w_out, *extra)


def ssm_proj_bwd(x, norm, dxo, dy, du_scan, dgate, d, w_in):
    t = x.shape[0]
    tm = min(ROW_TILE_BWD, t)
    n = 2 * BRANCH

    def body(x_ref, g_ref, dxo_ref, dy_ref, dus_ref, dgate_ref, d_ref, wt_ref, dx_ref, dw_ref, dg_ref):
        @pl.when(pl.program_id(0) == 0)
        def _():
            dw_ref[...] = jnp.zeros_like(dw_ref)
            dg_ref[...] = jnp.zeros_like(dg_ref)

        g = g_ref[...]
        r, xhat, h = _rms(x_ref[...], g)
        h = h.astype(MXU_DTYPE)
        du = dus_ref[...] + d_ref[...] * dy_ref[...]
        dproj = jnp.concatenate([du, dgate_ref[...]], axis=1).astype(MXU_DTYPE)
        dh = jnp.zeros((tm, D_MODEL), F32)
        for j in range(4):
            cols = dproj[:, j * (n // 4):(j + 1) * (n // 4)]
            dh = dh + _mm_nt(cols, wt_ref[j])
            dw_ref[j] += _mm_tn(h, cols)
        dx, dg = _rms_bwd(dh, g, r, xhat)
        dg_ref[...] += dg
        dx_ref[...] = dxo_ref[...] + dx

    row = _rows(tm, D_MODEL)
    vec = _whole((1, D_MODEL))
    blocks = _whole((4, D_MODEL, n // 4))
    return pl.pallas_call(
        body, name="ssm_proj_bwd", grid=(t // tm,),
        in_specs=[row, vec, row, row, row, row, vec, blocks],
        out_specs=[row, blocks, vec],
        out_shape=[_sds((t, D_MODEL)), _sds((4, D_MODEL, n // 4)), _sds((1, D_MODEL))],
        compiler_params=_params(("arbitrary",)),
    )(x, norm, dxo, dy, du_scan, dgate, d, w_in)


ATTN_N = Q_DIM + 2 * KV_DIM + BRANCH


def attn_proj_fwd(x, norm, w_in_t, cos2, sin2):
    t = x.shape[0]
    tm = min(ROW_TILE_FWD, t)

    def body(x_ref, g_ref, w_ref, cos_ref, sin_ref, q_ref, k_ref, v_ref, gate_ref):
        _, _, h = _rms(x_ref[...], g_ref[...])
        p = _mm_nt(h, w_ref[...])
        cs = cos_ref[...]
        sn = sin_ref[...]
        q = p[:, :Q_DIM]
        k = p[:, Q_DIM:Q_DIM + KV_DIM]
        q_ref[...] = q * _tile_lanes(cs, Q_DIM // LANES) + _swap_half_heads(q) * _tile_lanes(sn, Q_DIM // LANES)
        k_ref[...] = k * cs + _swap_half_heads(k) * sn
        v_ref[...] = p[:, Q_DIM + KV_DIM:Q_DIM + 2 * KV_DIM]
        gate_ref[...] = p[:, Q_DIM + 2 * KV_DIM:]

    return pl.pallas_call(
        body, name="attn_proj_fwd", grid=(t // tm,),
        in_specs=[_rows(tm, D_MODEL), _whole((1, D_MODEL)), _whole((ATTN_N, D_MODEL)), _rows(tm, LANES), _rows(tm, LANES)],
        out_specs=[_rows(tm, Q_DIM), _rows(tm, KV_DIM), _rows(tm, KV_DIM), _rows(tm, BRANCH)],
        out_shape=[_sds((t, Q_DIM)), _sds((t, KV_DIM)), _sds((t, KV_DIM)), _sds((t, BRANCH))],
        compiler_params=_params(("parallel",)),
    )(x, norm, w_in_t, cos2, sin2)


GQA_LANES = GQA_GROUP * ATTN_BLOCK


def _window_masks(first_block):
    kj = lax.broadcasted_iota(jnp.int32, (ATTN_BLOCK, GQA_LANES), 0)
    qi = lax.broadcasted_iota(jnp.int32, (ATTN_BLOCK, GQA_LANES), 1) % ATTN_BLOCK
    return kj > qi, kj > jnp.where(first_block, qi, ATTN_BLOCK)


def _fold(upper, both):
    return jnp.where(upper, both[:ATTN_BLOCK], both[ATTN_BLOCK:])


def _unfold(upper, tile):
    return jnp.concatenate([jnp.where(upper, tile, 0.0), jnp.where(upper, 0.0, tile)], axis=0).astype(MXU_DTYPE)


def _stack_heads(ref, group):
    return jnp.concatenate([ref[:, h * HEAD_DIM:(h + 1) * HEAD_DIM] for h in range(group * GQA_GROUP, (group + 1) * GQA_GROUP)], axis=0)


def _unstack_heads(ref, group, stacked):
    for n in range(GQA_GROUP):
        h = group * GQA_GROUP + n
        ref[:, h * HEAD_DIM:(h + 1) * HEAD_DIM] = stacked[n * ATTN_BLOCK:(n + 1) * ATTN_BLOCK]


def _sink_row(sink_ref, group):
    return jnp.concatenate([jnp.full((1, ATTN_BLOCK), sink_ref[group * GQA_GROUP + n], F32) for n in range(GQA_GROUP)], axis=1)


def _lane_is(h):
    return lax.broadcasted_iota(jnp.int32, (1, LANES), 1) == h


def attn_fwd(q, k, v, sinks):
    t = q.shape[0]
    nb = t // ATTN_BLOCK
    scale = HEAD_DIM ** -0.5

    def body(sink_ref, q_ref, kc_ref, kp_ref, vc_ref, vp_ref, o_ref, lse_ref):
        keys = jnp.concatenate([kp_ref[...], kc_ref[...]], axis=0).astype(MXU_DTYPE)
        vals = jnp.concatenate([vp_ref[...], vc_ref[...]], axis=0).astype(MXU_DTYPE)
        upper, dead = _window_masks(pl.program_id(0) == 0)
        for g in range(N_KV_HEADS):
            kv = slice(g * HEAD_DIM, (g + 1) * HEAD_DIM)
            qs = _stack_heads(q_ref, g) * scale
            s = jnp.where(dead, NEG_INF, _fold(upper, _mm_nt(keys[:, kv], qs)))
            sink = _sink_row(sink_ref, g)
            m = jnp.maximum(jnp.max(s, axis=0, keepdims=True), sink)
            p = jnp.exp(s - m)
            den = jnp.sum(p, axis=0, keepdims=True) + jnp.exp(sink - m)
            _unstack_heads(o_ref, g, _mm_tn(_unfold(upper, p * (1.0 / den)), vals[:, kv]))
            lse = m + jnp.log(den)
            for n in range(GQA_GROUP):
                lse_ref[pl.ds(g * GQA_GROUP + n, 1), :] = lse[:, n * ATTN_BLOCK:(n + 1) * ATTN_BLOCK]

    cur = lambda n: pl.BlockSpec((ATTN_BLOCK, n), lambda i: (i, 0))
    prev = lambda n: pl.BlockSpec((ATTN_BLOCK, n), lambda i: (jnp.maximum(i - 1, 0), 0))
    return pl.pallas_call(
        body, name="attn_fwd", grid=(nb,),
        in_specs=[pl.BlockSpec(memory_space=pltpu.SMEM), cur(Q_DIM), cur(KV_DIM), prev(KV_DIM), cur(KV_DIM), prev(KV_DIM)],
        out_specs=[cur(Q_DIM), pl.BlockSpec((N_Q_HEADS, ATTN_BLOCK), lambda i: (0, i))],
        out_shape=[_sds((t, Q_DIM)), _sds((N_Q_HEADS, t))],
        compiler_params=_params(("parallel",)),
    )(sinks, q, k, k, v, v)


def attn_bwd(q, k, v, sinks, o, lse, do):
    t = q.shape[0]
    nb = t // ATTN_BLOCK
    scale = HEAD_DIM ** -0.5

    def body(sink_ref, q_ref, kc_ref, kp_ref, vc_ref, vp_ref, o_ref, lse_ref, do_ref,
             dq_ref, dk_ref, dv_ref, dsink_ref, dk_carry, dv_carry):
        i = pl.program_id(0)

        @pl.when(i == 0)
        def _():
            dsink_ref[...] = jnp.zeros_like(dsink_ref)
            dk_carry[...] = jnp.zeros_like(dk_carry)
            dv_carry[...] = jnp.zeros_like(dv_carry)

        @pl.when(i < nb)
        def _():
            keys = jnp.concatenate([kp_ref[...], kc_ref[...]], axis=0).astype(MXU_DTYPE)
            vals = jnp.concatenate([vp_ref[...], vc_ref[...]], axis=0).astype(MXU_DTYPE)
            upper, dead = _window_masks(i == 0)
            dsink = jnp.zeros((1, LANES), F32)
            dk_heads = []
            dv_heads = []
            for g in range(N_KV_HEADS):
                kv = slice(g * HEAD_DIM, (g + 1) * HEAD_DIM)
                qs = (_stack_heads(q_ref, g) * scale).astype(MXU_DTYPE)
                dos = _stack_heads(do_ref, g)
                lse = jnp.concatenate([lse_ref[pl.ds(g * GQA_GROUP + n, 1), :] for n in range(GQA_GROUP)], axis=1)
                s = jnp.where(dead, NEG_INF, _fold(upper, _mm_nt(keys[:, kv], qs)))
                p = jnp.exp(s - lse)
                delta = _mm_f32(jnp.ones((8, HEAD_DIM), F32), dos * _stack_heads(o_ref, g), ((1,), (1,)))[:1]
                dos = dos.astype(MXU_DTYPE)
                ds = _unfold(upper, p * (_fold(upper, _mm_nt(vals[:, kv], dos)) - delta))
                _unstack_heads(dq_ref, g, _mm_tn(ds, keys[:, kv]) * scale)
                dk_heads.append(_mm(ds, qs))
                dv_heads.append(_mm(_unfold(upper, p), dos))
                at_sink = jnp.exp(_sink_row(sink_ref, g) - lse) * delta
                for n in range(GQA_GROUP):
                    dsink = dsink + jnp.where(_lane_is(g * GQA_GROUP + n), -jnp.sum(at_sink[:, n * ATTN_BLOCK:(n + 1) * ATTN_BLOCK]), 0.0)
            dkk = jnp.concatenate(dk_heads, axis=1)
            dvv = jnp.concatenate(dv_heads, axis=1)
            dsink_ref[...] += dsink
            dk_ref[...] = dk_carry[...] + dkk[:ATTN_BLOCK]
            dv_ref[...] = dv_carry[...] + dvv[:ATTN_BLOCK]
            dk_carry[...] = dkk[ATTN_BLOCK:]
            dv_carry[...] = dvv[ATTN_BLOCK:]

        @pl.when(i == nb)
        def _():
            dk_ref[...] = dk_carry[...]
            dv_ref[...] = dv_carry[...]

    last = nb - 1
    cur = lambda n: pl.BlockSpec((ATTN_BLOCK, n), lambda i: (jnp.minimum(i, last), 0))
    prev = lambda n: pl.BlockSpec((ATTN_BLOCK, n), lambda i: (jnp.clip(i - 1, 0, last), 0))
    late = lambda n: pl.BlockSpec((ATTN_BLOCK, n), lambda i: (i, 0))
    dq, dk_late, dv_late, dsinks = pl.pallas_call(
        body, name="attn_bwd", grid=(nb + 1,),
        in_specs=[pl.BlockSpec(memory_space=pltpu.SMEM), cur(Q_DIM), cur(KV_DIM), prev(KV_DIM), cur(KV_DIM), prev(KV_DIM),
                  cur(Q_DIM), pl.BlockSpec((N_Q_HEADS, ATTN_BLOCK), lambda i: (0, jnp.minimum(i, last))), cur(Q_DIM)],
        out_specs=[cur(Q_DIM), late(KV_DIM), late(KV_DIM), _whole((1, LANES))],
        out_shape=[_sds((t, Q_DIM)), _sds((t + ATTN_BLOCK, KV_DIM)), _sds((t + ATTN_BLOCK, KV_DIM)), _sds((1, LANES))],
        scratch_shapes=[pltpu.VMEM((ATTN_BLOCK, KV_DIM), F32), pltpu.VMEM((ATTN_BLOCK, KV_DIM), F32)],
        compiler_params=_params(("arbitrary",)),
    )(sinks, q, k, k, v, v, o, lse, do)
    return dq, dk_late[ATTN_BLOCK:], dv_late[ATTN_BLOCK:], dsinks


def attn_out_fwd(x, o, gate, w_out):
    t = x.shape[0]
    tm = min(ROW_TILE_FWD, t)

    def body(x_ref, o_ref, gate_ref, w_ref, xo_ref):
        xo_ref[...] = x_ref[...] + _mm(o_ref[...] * _silu(gate_ref[...]), w_ref[...])

    row = _rows(tm, D_MODEL)
    return pl.pallas_call(
        body, name="attn_out_fwd", grid=(t // tm,),
        in_specs=[row, row, row, _whole((Q_DIM, D_MODEL))], out_specs=row, out_shape=_sds((t, D_MODEL)),
        compiler_params=_params(("parallel",)),
    )(x, o, gate, w_out)


def attn_out_bwd(dxo, o, gate, w_out, token=None):
    t = dxo.shape[0]
    tm = min(ROW_TILE_BWD, t)
    extra, extra_specs = _after(token)

    def body(dxo_ref, o_ref, gate_ref, wt_ref, *rest):
        do_ref, dgate_ref, dw_ref = rest[-3:]

        @pl.when(pl.program_id(0) == 0)
        def _():
            dw_ref[...] = jnp.zeros_like(dw_ref)

        dxo = dxo_ref[...]
        o = o_ref[...]
        gate = gate_ref[...]
        sgate, sgate_grad = _silu_and_grad(gate)
        da = _mm_nt(dxo, wt_ref[...])
        dw_ref[...] += _mm_tn(o * sgate, dxo)
        do_ref[...] = da * sgate
        dgate_ref[...] = da * o * sgate_grad

    row = _rows(tm, D_MODEL)
    mat = _whole((Q_DIM, D_MODEL))
    return pl.pallas_call(
        body, name="attn_out_bwd", grid=(t // tm,),
        in_specs=[row, row, row, mat] + extra_specs, out_specs=[row, row, mat],
        out_shape=[_sds((t, Q_DIM)), _sds((t, BRANCH)), _sds((Q_DIM, D_MODEL))],
        compiler_params=_params(("arbitrary",)),
    )(dxo, o, gate, w_out, *extra)


def attn_proj_bwd(x, norm, dxo, dq, dk, dv, dgate, cos2, sin2, w_in_t):
    t = x.shape[0]
    tm = min(ROW_TILE_BWD, t)

    def body(x_ref, g_ref, dxo_ref, dq_ref, dk_ref, dv_ref, dgate_ref, cos_ref, sin_ref, wt_ref, dx_ref, dw_ref, dg_ref):
        @pl.when(pl.program_id(0) == 0)
        def _():
            dw_ref[...] = jnp.zeros_like(dw_ref)
            dg_ref[...] = jnp.zeros_like(dg_ref)

        g = g_ref[...]
        r, xhat, h = _rms(x_ref[...], g)
        cs = cos_ref[...]
        sn = sin_ref[...]
        dqr = dq_ref[...]
        dkr = dk_ref[...]
        dq = dqr * _tile_lanes(cs, Q_DIM // LANES) + _swap_half_heads(dqr * _tile_lanes(sn, Q_DIM // LANES))
        dk = dkr * cs + _swap_half_heads(dkr * sn)
        dproj = jnp.concatenate([dq, dk, dv_ref[...], dgate_ref[...]], axis=1)
        dh = _mm(dproj, wt_ref[...])
        dw_ref[...] += _mm_tn(dproj, h)
        dx, dg = _rms_bwd(dh, g, r, xhat)
        dg_ref[...] += dg
        dx_ref[...] = dxo_ref[...] + dx

    row = _rows(tm, D_MODEL)
    vec = _whole((1, D_MODEL))
    return pl.pallas_call(
        body, name="attn_proj_bwd", grid=(t // tm,),
        in_specs=[row, vec, row, _rows(tm, Q_DIM), _rows(tm, KV_DIM), _rows(tm, KV_DIM), _rows(tm, BRANCH),
                  _rows(tm, LANES), _rows(tm, LANES), _whole((ATTN_N, D_MODEL))],
        out_specs=[row, _whole((ATTN_N, D_MODEL)), vec],
        out_shape=[_sds((t, D_MODEL)), _sds((ATTN_N, D_MODEL)), _sds((1, D_MODEL))],
        compiler_params=_params(("arbitrary",)),
    )(x, norm, dxo, dq, dk, dv, dgate, cos2, sin2, w_in_t)


def attn_out_loss(x, o, gate, w_out, norm, target):
    t = x.shape[0]
    tm = min(ROW_TILE_FWD, t)

    def body(x_ref, o_ref, gate_ref, w_ref, g_ref, tgt_ref, loss_ref, dx_ref, dg_ref):
        @pl.when(pl.program_id(0) == 0)
        def _():
            loss_ref[...] = jnp.zeros_like(loss_ref)
            dg_ref[...] = jnp.zeros_like(dg_ref)

        out = x_ref[...] + _mm(o_ref[...] * _silu(gate_ref[...]), w_ref[...])
        g = g_ref[...]
        r, xhat, y = _rms(out, g)
        err = y - tgt_ref[...]
        loss_ref[...] += 0.5 * jnp.sum(jnp.mean(err * err, axis=-1, keepdims=True), axis=0, keepdims=True)
        dx, dg = _rms_bwd(err * (1.0 / D_MODEL), g, r, xhat)
        dg_ref[...] += dg
        dx_ref[...] = dx

    row = _rows(tm, D_MODEL)
    vec = _whole((1, D_MODEL))
    return pl.pallas_call(
        body, name="attn_out_loss", grid=(t // tm,),
        in_specs=[row, row, row, _whole((Q_DIM, D_MODEL)), vec, row], out_specs=[_whole((1, 1)), row, vec],
        out_shape=[_sds((1, 1)), _sds((t, D_MODEL)), _sds((1, D_MODEL))],
        compiler_params=_params(("arbitrary",)),
    )(x, o, gate, w_out, norm, target)


OCT_TILE = pl.BlockSpec((None, LANES, LANES), lambda b: (b, 0, 0))
N_LAGS = S5_CHUNK + 1


def _cmul(ar, ai, br, bi):
    return ar * br - ai * bi, ar * bi + ai * br


def _cmul_conj(ar, ai, br, bi):
    return ar * br + ai * bi, ar * bi - ai * br


def _mm_f32(a, b, dims):
    return lax.dot_general(a, b, (dims, ((), ())), precision=lax.Precision.HIGH, preferred_element_type=F32)


def _s5_discretise(ar, ai, ls, br, bi):
    dt = jnp.exp(ls)
    xr = ar * dt
    xi = ai * dt
    mag = jnp.exp(xr)
    first = (mag * jnp.cos(xi), mag * jnp.sin(xi))
    powers = [(jnp.ones_like(xr), jnp.zeros_like(xr)), first]
    for _ in range(2, N_LAGS):
        powers.append(_cmul(*powers[-1], *first))
    den = ar * ar + ai * ai
    nr = powers[1][0] - 1.0
    ni = powers[1][1]
    fr = (nr * ar + ni * ai) / den
    fi = (ni * ar - nr * ai) / den
    bbr, bbi = _cmul(fr, fi, br, bi)
    return dt, powers, (fr, fi), (bbr, bbi), den


def _same_group_tile():
    row = lax.broadcasted_iota(jnp.int32, (LANES, LANES), 0)
    col = lax.broadcasted_iota(jnp.int32, (LANES, LANES), 1)
    return (row // SSM_GROUP) == (col // SSM_GROUP)


def _first_copy_lanes():
    return lax.broadcasted_iota(jnp.int32, (LANES, LANES), 1) < SSM_STATE


def s5_param_fwd(tiles, token=None):
    extra, extra_specs = _after(token)

    def body(ar_ref, ai_ref, ls_ref, br_ref, bi_ref, cr_ref, ci_ref, *rest):
        kd_ref, wsr_ref, wsi_ref, wor_ref, woi_ref, pr_ref, pi_ref = rest[-7:]
        cr = cr_ref[...]
        ci = ci_ref[...]
        _, powers, _, (bbr, bbi), _ = _s5_discretise(ar_ref[...], ai_ref[...], ls_ref[...], br_ref[...], bi_ref[...])
        once = _first_copy_lanes()
        crm = jnp.where(once, cr, 0.0)
        cim = jnp.where(once, ci, 0.0)
        same = _same_group_tile()
        for lag in range(S5_CHUNK):
            er, ei = powers[lag]
            xr, xi = _cmul(er, ei, bbr, bbi)
            rows = pl.ds((S5_CHUNK - 1 - lag) * LANES, LANES)
            wsr_ref[rows, :] = xr
            wsi_ref[rows, :] = xi
        k = _mm_f32(wsr_ref[...], crm, ((1,), (1,))) - _mm_f32(wsi_ref[...], cim, ((1,), (1,)))
        for lag in range(S5_CHUNK):
            kd_ref[lag] = jnp.where(same, k[(S5_CHUNK - 1 - lag) * LANES:(S5_CHUNK - lag) * LANES], 0.0)
        for t in range(S5_CHUNK):
            er, ei = powers[t + 1]
            zr, zi = _cmul(er, ei, cr, ci)
            wor_ref[pl.ds(t * LANES, LANES), :] = zr
            woi_ref[pl.ds(t * LANES, LANES), :] = -zi
        pr_ref[...] = powers[S5_CHUNK][0]
        pi_ref[...] = powers[S5_CHUNK][1]

    return pl.pallas_call(
        body, name="s5_param_fwd", grid=(S5_OCTETS,),
        in_specs=[OCT_TILE] * 7 + [ANY] * len(extra),
        out_specs=[OCT_KD, OCT_W, OCT_W, OCT_W, OCT_W, OCT_TILE, OCT_TILE],
        out_shape=[_sds((S5_OCTETS, S5_CHUNK, LANES, LANES))] + [_sds((S5_OCTETS, S5_OCT_IN, LANES))] * 4
                  + [_sds((S5_OCTETS, LANES, LANES))] * 2,
        compiler_params=_params(("parallel",)),
    )(*tiles, *extra)


def s5_param_bwd(tiles, dkd, dws_re, dws_im, dwo_re, dwo_im, dp_re, dp_im):
    def body(ar_ref, ai_ref, ls_ref, br_ref, bi_ref, cr_ref, ci_ref, dkd_ref, dwsr_ref, dwsi_ref, dwor_ref, dwoi_ref, dpr_ref, dpi_ref,
             dar_ref, dai_ref, dls_ref, dbr_ref, dbi_ref, dcr_ref, dci_ref):
        ar = ar_ref[...]
        ai = ai_ref[...]
        br = br_ref[...]
        bi = bi_ref[...]
        cr = cr_ref[...]
        ci = ci_ref[...]
        dt, powers, (fr, fi), (bbr, bbi), den = _s5_discretise(ar, ai, ls_ref[...], br, bi)
        once = _first_copy_lanes()
        crm = jnp.where(once, cr, 0.0)
        cim = jnp.where(once, ci, 0.0)
        same = _same_group_tile()
        zero = jnp.zeros((LANES, LANES), F32)
        dpow = [[zero, zero] for _ in range(N_LAGS)]
        dbbr, dbbi = zero, zero
        by_step = [S5_CHUNK - 1 - s for s in range(S5_CHUNK)]
        x_all = [_cmul(*powers[lag], bbr, bbi) for lag in by_step]
        xr_all = jnp.concatenate([x[0] for x in x_all], axis=0)
        xi_all = jnp.concatenate([x[1] for x in x_all], axis=0)
        g_all = jnp.concatenate([jnp.where(same, dkd_ref[lag], 0.0) for lag in by_step], axis=0)
        dxr_all = dwsr_ref[...] + _mm_f32(g_all, crm, ((1,), (0,)))
        dxi_all = dwsi_ref[...] - _mm_f32(g_all, cim, ((1,), (0,)))
        dcr = jnp.where(once, _mm_f32(g_all, xr_all, ((0,), (0,))), 0.0)
        dci = -jnp.where(once, _mm_f32(g_all, xi_all, ((0,), (0,))), 0.0)
        for lag in range(S5_CHUNK):
            er, ei = powers[lag]
            rows = slice((S5_CHUNK - 1 - lag) * LANES, (S5_CHUNK - lag) * LANES)
            dxr = dxr_all[rows]
            dxi = dxi_all[rows]
            a, b = _cmul_conj(bbr, bbi, dxr, dxi)
            dpow[lag][0] = dpow[lag][0] + a
            dpow[lag][1] = dpow[lag][1] + b
            a, b = _cmul_conj(er, ei, dxr, dxi)
            dbbr = dbbr + a
            dbbi = dbbi + b
        for t in range(S5_CHUNK):
            er, ei = powers[t + 1]
            dzr = dwor_ref[pl.ds(t * LANES, LANES), :]
            dzi = -dwoi_ref[pl.ds(t * LANES, LANES), :]
            a, b = _cmul_conj(cr, ci, dzr, dzi)
            dpow[t + 1][0] = dpow[t + 1][0] + a
            dpow[t + 1][1] = dpow[t + 1][1] + b
            a, b = _cmul_conj(er, ei, dzr, dzi)
            dcr = dcr + a
            dci = dci + b
        dpow[S5_CHUNK][0] = dpow[S5_CHUNK][0] + dpr_ref[...]
        dpow[S5_CHUNK][1] = dpow[S5_CHUNK][1] + dpi_ref[...]
        dfr, dfi = _cmul_conj(br, bi, dbbr, dbbi)
        dbr, dbi = _cmul_conj(fr, fi, dbbr, dbbi)
        dnr, dni = _cmul(ar / den, ai / den, dfr, dfi)
        qr = (fr * ar + fi * ai) / den
        qi = (fi * ar - fr * ai) / den
        dlr, dli = _cmul(-qr, qi, dfr, dfi)
        dpow[1][0] = dpow[1][0] + dnr
        dpow[1][1] = dpow[1][1] + dni
        dxr, dxi = zero, zero
        for lag in range(1, N_LAGS):
            a, b = _cmul_conj(powers[lag][0], powers[lag][1], dpow[lag][0], dpow[lag][1])
            dxr = dxr + lag * a
            dxi = dxi + lag * b
        dar_ref[...] = dlr + dt * dxr
        dai_ref[...] = dli + dt * dxi
        dls_ref[...] = dt * (ar * dxr + ai * dxi)
        dbr_ref[...] = dbr
        dbi_ref[...] = dbi
        dcr_ref[...] = dcr
        dci_ref[...] = dci

    return pl.pallas_call(
        body, name="s5_param_bwd", grid=(S5_OCTETS,),
        in_specs=[OCT_TILE] * 7 + [OCT_KD, OCT_W, OCT_W, OCT_W, OCT_W, OCT_TILE, OCT_TILE], out_specs=[OCT_TILE] * 7,
        out_shape=[_sds((S5_OCTETS, LANES, LANES))] * 7,
        compiler_params=_params(("parallel",)),
    )(*tiles, dkd, dws_re, dws_im, dwo_re, dwo_im, dp_re, dp_im)


def _doubled(v):
    return jnp.concatenate([v, v], axis=-1)


def _s5_param_tiles(a_re, a_im, log_step, b_re, b_im, c_re, c_im):
    def per_group(a):
        return _doubled(jnp.broadcast_to(a.reshape(S5_OCTETS, S5_OCT, 1, SSM_STATE),
                                         (S5_OCTETS, S5_OCT, SSM_GROUP, SSM_STATE)).reshape(S5_OCTETS, LANES, SSM_STATE))

    ls = jnp.broadcast_to(log_step.reshape(S5_OCTETS, S5_OCT, 1, 1), (S5_OCTETS, S5_OCT, SSM_GROUP, LANES)).reshape(S5_OCTETS, LANES, LANES)
    bt = lambda b: _doubled(b.transpose(0, 2, 1).reshape(S5_OCTETS, LANES, SSM_STATE))
    ct = lambda c: _doubled(c.reshape(S5_OCTETS, LANES, SSM_STATE))
    return [per_group(a_re), per_group(a_im), ls, bt(b_re), bt(b_im), ct(c_re), ct(c_im)]


def _s5_param_grads(dtiles):
    dar, dai, dls, dbr, dbi, dcr, dci = dtiles
    halves = lambda d: d[..., :SSM_STATE] + d[..., SSM_STATE:]
    per_group = lambda d: halves(d).reshape(SSM_GROUPS, SSM_GROUP, SSM_STATE).sum(axis=1)
    per_row = lambda d: halves(d).reshape(SSM_GROUPS, SSM_GROUP, SSM_STATE)
    return (per_group(dar), per_group(dai), dls.reshape(SSM_GROUPS, SSM_GROUP * LANES).sum(axis=1),
            per_row(dbr).transpose(0, 2, 1), per_row(dbi).transpose(0, 2, 1), per_row(dcr), per_row(dci))


def _group_power_rows(tile):
    return tile[:, ::SSM_GROUP, :SSM_STATE].reshape(1, S5_STATES)


def _group_power_tiles(row):
    t = jnp.pad(row.reshape(S5_OCTETS, S5_OCT, 1, SSM_STATE), ((0, 0), (0, 0), (0, SSM_GROUP - 1), (0, LANES - SSM_STATE)))
    return t.reshape(S5_OCTETS, LANES, LANES)


def _rope_tables(t):
    pos = jnp.arange(t, dtype=F32)
    inv_freq = ROPE_THETA ** (-jnp.arange(0, HEAD_DIM, 2, dtype=F32) / HEAD_DIM)
    ang = pos[:, None] * inv_freq[None, :]
    cos = jnp.cos(ang)
    sin = jnp.sin(ang)
    cos64 = jnp.concatenate([cos, cos], axis=1)
    sin64 = jnp.concatenate([-sin, sin], axis=1)
    return jnp.concatenate([cos64, cos64], axis=1), jnp.concatenate([sin64, sin64], axis=1)


def _row(v):
    return v.reshape(1, -1)


def _s5_matrices(w, token=None):
    tiles = _s5_param_tiles(w["a_re"], w["a_im"], w["log_step"], w["b_re"], w["b_im"], w["c_re"], w["c_im"])
    kd, ws_re, ws_im, wo_re, wo_im, p_re, p_im = s5_param_fwd(tiles, token)
    return tiles, dict(kd=kd, ws_re=ws_re, ws_im=ws_im, wo_re=wo_re, wo_im=wo_im, a_re=_group_power_rows(p_re), a_im=_group_power_rows(p_im))


def _ssm_forward(x, w):
    tiles, mats = w["s5"] if "s5" in w else _s5_matrices(w)
    u, gate = ssm_proj_fwd(x, _row(w["norm"]), w["w_in"])
    s_re, s_im = s5_chunk_states(u, mats["ws_re"], mats["ws_im"])
    h_re, h_im = s5_scan_fwd(s_re, s_im, mats["a_re"], mats["a_im"])
    y_scan = s5_outputs(u, h_re, h_im, mats["kd"], mats["wo_re"], mats["wo_im"])
    y, g2, x_new = ssm_mix_fwd(x, u, gate, y_scan, _row(w["d"]), w["w_glu"], _row(w["b_glu"]), w["w_out"])
    saved = dict(x=x, u=u, gate=gate, y=y, g2=g2, h_re=h_re, h_im=h_im, mats=mats, tiles=tiles)
    return x_new, saved


def _ssm_backward(dxo, w, s, token=None):
    dy, dgate, dw_out, dw_glu, db_glu, dd = ssm_mix_bwd(dxo, s["u"], s["gate"], s["y"], s["g2"], w["w_glu"], w["w_out"], token)
    mats = s["mats"]
    dh_re, dh_im = s5_state_grads(dy, mats["wo_re"], mats["wo_im"])
    ds_re, ds_im, da_re, da_im = s5_scan_bwd(dh_re, dh_im, s["h_re"], s["h_im"], mats["a_re"], mats["a_im"])
    du_scan = s5_input_grads(dy, ds_re, ds_im, mats["kd"], mats["ws_re"], mats["ws_im"])
    dkd, dws_re, dws_im, dwo_re, dwo_im = s5_weight_grads(s["u"], dy, s["h_re"], s["h_im"], ds_re, ds_im)
    dparams = _s5_param_grads(s5_param_bwd(s["tiles"], dkd, dws_re, dws_im, dwo_re, dwo_im,
                                           _group_power_tiles(da_re), _group_power_tiles(da_im)))
    dx, dw_in, dnorm = ssm_proj_bwd(s["x"], _row(w["norm"]), dxo, dy, du_scan, dgate, _row(w["d"]), w["w_in"])
    grads = dict(norm=dnorm, w_in=dw_in, d=dd, w_glu=dw_glu, b_glu=db_glu, w_out=dw_out)
    for name, val in zip(("a_re", "a_im", "log_step", "b_re", "b_im", "c_re", "c_im"), dparams):
        grads[name] = val
    return dx, grads


def _attn_forward(x, w, cos2, sin2, loss_head=None):
    q, k, v, gate = attn_proj_fwd(x, _row(w["norm"]), w["w_in"], cos2, sin2)
    o, lse = attn_fwd(q, k, v, w["sinks"])
    if loss_head is None:
        result = attn_out_fwd(x, o, gate, w["w_out"])
    else:
        result = attn_out_loss(x, o, gate, w["w_out"], _row(loss_head[0]), loss_head[1])
    return result, dict(x=x, q=q, k=k, v=v, gate=gate, o=o, lse=lse)


def _attn_backward(dxo, w, s, cos2, sin2, token=None):
    do, dgate, dw_out = attn_out_bwd(dxo, s["o"], s["gate"], w["w_out"], token)
    dq, dk, dv, dsinks = attn_bwd(s["q"], s["k"], s["v"], w["sinks"], s["o"], s["lse"], do)
    dx, dw_in, dnorm = attn_proj_bwd(s["x"], _row(w["norm"]), dxo, dq, dk, dv, dgate, cos2, sin2, w["w_in"])
    return dx, dict(norm=dnorm, w_in=dw_in, sinks=dsinks[0, :N_Q_HEADS], w_out=dw_out)


class _NoExchanges:
    def __init__(self, layers):
        self.layers = layers

    def layer(self, i, x):
        return self.layers[i]

    def layer_done(self, i, grads, dx):
        return None


def _sequence_step(x, target, final_norm, hooks, depth=4):
    cos2, sin2 = _rope_tables(x.shape[0])
    saved, layers = [], []
    for i in range(depth):
        w = hooks.layer(i, x)
        layers.append(w)
        if i % 2 == 0:
            x, s = _ssm_forward(x, w)
        else:
            x, s = _attn_forward(x, w, cos2, sin2, (final_norm, target) if i == depth - 1 else None)
        saved.append(s)
    loss, dx, dfinal = x
    grads = {"final_norm": dfinal}
    token = None
    for i in reversed(range(depth)):
        if i % 2 == 0:
            dx, g = _ssm_backward(dx, layers[i], saved[i], token)
        else:
            dx, g = _attn_backward(dx, layers[i], saved[i], cos2, sin2, token)
        g = {"l%d_%s" % (i, name): val for name, val in g.items()}
        grads.update(g)
        token = hooks.layer_done(i, g, dx)
    return loss[0, 0], dx, grads


ANY = pl.BlockSpec(memory_space=pl.ANY)


def _place():
    return lax.axis_index("x"), lax.axis_index("y"), lax.axis_index("c")


def _other_chips(x, y):
    return [(1 - x, y), (x, 1 - y), (1 - x, 1 - y)]


class _StagedCopies:
    def __init__(self, bufs, load_sems, store_sems):
        self.bufs, self.load_sems, self.store_sems = bufs, load_sems, store_sems
        self.loads, self.stores = [], []

    def load(self, i, src):
        cp = pltpu.make_async_copy(src, self.bufs[i], self.load_sems.at[i])
        cp.start()
        self.loads.append(cp)

    def store(self, i, dst):
        self.loads[i].wait()
        cp = pltpu.make_async_copy(self.bufs[i], dst, self.store_sems.at[i])
        cp.start()
        self.stores.append(cp)

    def finish(self):
        for cp in self.stores:
            cp.wait()


def _staging(blocks):
    n = len(blocks)
    return [pltpu.VMEM(b.shape, b.dtype) for b in blocks] + [pltpu.SemaphoreType.DMA((n,)), pltpu.SemaphoreType.DMA((n,))]


def exchange_halves_with_sibling(grads):
    n = len(grads)

    def body(*refs):
        ins, outs = refs[:n], refs[n:2 * n]
        send_sems, recv_sems = refs[2 * n:]
        x, y, c = _place()
        copies = []
        for i in range(n):
            half = ins[i].shape[1] // 2
            src = ins[i].at[:, pl.ds((1 - c) * half, half), :]
            cp = pltpu.make_async_remote_copy(src_ref=src, dst_ref=outs[i], send_sem=send_sems.at[i], recv_sem=recv_sems.at[i],
                                              device_id=(x, y, 1 - c), device_id_type=MESH)
            cp.start()
            copies.append(cp)
        for cp in copies:
            cp.wait()

    return pl.pallas_call(
        body, name="exchange_halves_with_sibling",
        in_specs=[ANY] * n, out_specs=[ANY] * n,
        out_shape=[_sds((g.shape[0], g.shape[1] // 2, g.shape[2])) for g in grads],
        scratch_shapes=[pltpu.SemaphoreType.DMA((n,)), pltpu.SemaphoreType.DMA((n,))],
    )(*grads)


def swap_halves_with_sibling(pieces):
    n = len(pieces)

    def body(*refs):
        ins, outs = refs[:n], refs[n:2 * n]
        send_sems, recv_sems = refs[2 * n:2 * n + 2]
        own = _StagedCopies(refs[2 * n + 2:3 * n + 2], *refs[3 * n + 2:])
        x, y, c = _place()
        for i in range(n):
            own.load(i, ins[i])
        swaps = []
        for i in range(n):
            cp = pltpu.make_async_remote_copy(src_ref=ins[i], dst_ref=outs[i].at[c], send_sem=send_sems.at[i], recv_sem=recv_sems.at[i],
                                              device_id=(x, y, 1 - c), device_id_type=MESH)
            cp.start()
            swaps.append(cp)
        for i in range(n):
            own.store(i, outs[i].at[c])
        for i in range(n):
            pltpu.make_async_remote_copy(src_ref=ins[i], dst_ref=outs[i].at[1 - c], send_sem=send_sems.at[i], recv_sem=recv_sems.at[i],
                                         device_id=(x, y, 1 - c), device_id_type=MESH).wait_recv()
        for cp in swaps:
            cp.wait_send()
        own.finish()

    return pl.pallas_call(
        body, name="swap_halves_with_sibling",
        in_specs=[ANY] * n, out_specs=[ANY] * n,
        out_shape=[_sds((2,) + p.shape) for p in pieces],
        scratch_shapes=[pltpu.SemaphoreType.DMA((n,)), pltpu.SemaphoreType.DMA((n,))] + _staging(pieces),
        compiler_params=_params(),
    )(*pieces)


IN_HBM = pl.BlockSpec(memory_space=pltpu.HBM)
SEMAPHORES = pl.BlockSpec(memory_space=pltpu.SEMAPHORE)
DATAFLOW = pltpu.SideEffectType.DATAFLOW_SIDE_EFFECTING


def _hbm(a):
    return pltpu.with_memory_space_constraint(a, pltpu.HBM)


def place_own_blocks(shards):
    n = len(shards)

    def body(*refs):
        ins, outs = refs[:n], refs[n:2 * n]
        own = _StagedCopies(refs[2 * n:3 * n], *refs[3 * n:])
        x, y, _ = _place()
        for i in range(n):
            own.load(i, ins[i])
        for i in range(n):
            own.store(i, outs[i].at[2 * x + y])
        own.finish()

    return pl.pallas_call(
        body, name="place_own_blocks", in_specs=[ANY] * n, out_specs=[ANY] * n,
        out_shape=[_sds((4,) + s.shape, s.dtype) for s in shards],
        scratch_shapes=_staging(shards), compiler_params=_params(),
    )(*shards)


def _block_to_send(ref, chip, per_target):
    if not per_target:
        return ref
    return ref.at[chip] if ref.shape[0] == 4 else ref.at[0]


def start_sends_to_chips(name, sources, landings, per_target, after):
    n = len(sources)
    n_sems = 2 * 3 * n

    def body(*refs):
        srcs = refs[:n]
        sems = refs[2 * n + 1:2 * n + 1 + n_sems]
        lands = refs[2 * n + 1 + n_sems:3 * n + 1 + n_sems]
        token = refs[3 * n + 1 + n_sems]
        x, y, c = _place()
        me = 2 * x + y
        for i in range(n):
            for k, (tx, ty) in enumerate(_other_chips(x, y)):
                src = _block_to_send(srcs[i], 2 * tx + ty, per_target)
                pltpu.make_async_remote_copy(src_ref=src, dst_ref=lands[i].at[me], send_sem=sems[2 * (3 * i + k)], recv_sem=sems[2 * (3 * i + k) + 1],
                                             device_id=(tx, ty, c), device_id_type=MESH).start()
        token[...] = jnp.zeros_like(token)

    outs = pl.pallas_call(
        body, name=name,
        in_specs=[IN_HBM] * (2 * n) + [ANY],
        out_specs=[SEMAPHORES] * n_sems + [IN_HBM] * n + [pl.BlockSpec(memory_space=pltpu.VMEM)],
        out_shape=[pltpu.SemaphoreType.DMA(())] * n_sems + [pltpu.HBM(l.shape, l.dtype) for l in landings] + [_sds(TOKEN_SHAPE)],
        input_output_aliases={n + i: n_sems + i for i in range(n)},
        compiler_params=pltpu.CompilerParams(has_side_effects=DATAFLOW),
    )(*[_hbm(s) for s in sources], *[_hbm(l) for l in landings], after)
    return list(outs[:n_sems]), list(outs[n_sems:n_sems + n]), outs[n_sems + n]


def wait_sends_to_chips(name, sources, landings, sems, per_target, after):
    n = len(sources)
    n_sems = len(sems)

    def body(*refs):
        srcs = refs[:n]
        sem_refs = refs[2 * n:2 * n + n_sems]
        lands = refs[2 * n + n_sems + 1:]
        x, y, c = _place()
        me = 2 * x + y
        for i in range(n):
            for k, (tx, ty) in enumerate(_other_chips(x, y)):
                src = _block_to_send(srcs[i], me, per_target)
                cp = pltpu.make_async_remote_copy(src_ref=src, dst_ref=lands[i].at[2 * tx + ty], send_sem=sem_refs[2 * (3 * i + k)],
                                                  recv_sem=sem_refs[2 * (3 * i + k) + 1], device_id=(tx, ty, c), device_id_type=MESH)
                cp.wait_send()
                cp.wait_recv()

    return pl.pallas_call(
        body, name=name,
        in_specs=[IN_HBM] * (2 * n) + [SEMAPHORES] * n_sems + [ANY],
        out_specs=[IN_HBM] * n,
        out_shape=[pltpu.HBM(l.shape, l.dtype) for l in landings],
        input_output_aliases={n + i: i for i in range(n)},
        compiler_params=pltpu.CompilerParams(has_side_effects=DATAFLOW),
    )(*[_hbm(s) for s in sources], *landings, *sems, after)


def _row_tile(rows, cols):
    tm = rows
    while tm * cols * 4 > (2 << 20) and tm % 16 == 0:
        tm //= 2
    return tm


def add_pairs(half, a_list, b_list, out_dtypes, copies=1):
    n = len(a_list)
    nb = a_list[0].shape[0]

    def body(half_ref, *refs):
        for i in range(n):
            total = (refs[i][...] + refs[n + i][...]).astype(out_dtypes[i])
            for o_ref in refs[2 * n + i * copies:2 * n + (i + 1) * copies]:
                o_ref[...] = total

    halves = [pl.BlockSpec((None,) + b.shape[1:], lambda j, h: (j, h[0], 0)) for b in b_list]
    whole = [pl.BlockSpec((None,) + b.shape[1:], lambda j, h: (j, 0, 0)) for b in b_list]
    outs = pl.pallas_call(
        body, name="add_pairs",
        grid_spec=pltpu.PrefetchScalarGridSpec(num_scalar_prefetch=1, grid=(nb,), in_specs=halves + whole,
                                               out_specs=[s for s in whole for _ in range(copies)]),
        out_shape=[_sds(b.shape, dt) for b, dt in zip(b_list, out_dtypes) for _ in range(copies)],
        compiler_params=_params(("parallel",)),
    )(half, *a_list, *b_list)
    return [tuple(outs[i * copies:(i + 1) * copies]) for i in range(n)]


def sum_fours(arrays, token=None):
    n = len(arrays)
    extra, extra_specs = _after(token)
    steps = 2 if all(a.shape[1] % 32 == 0 for a in arrays) else 1

    def body(*refs):
        outs = refs[-n:]
        for a_ref, o_ref in zip(refs[:n], outs):
            o_ref[...] = ((a_ref[0].astype(F32) + a_ref[1].astype(F32)) + a_ref[2].astype(F32)) + a_ref[3].astype(F32)

    return pl.pallas_call(
        body, name="sum_fours", grid=(steps,),
        in_specs=[pl.BlockSpec((4, a.shape[1] // steps, a.shape[2]), lambda i: (0, i, 0)) for a in arrays] + extra_specs,
        out_specs=[pl.BlockSpec((a.shape[1] // steps, a.shape[2]), lambda i: (i, 0)) for a in arrays],
        out_shape=[_sds(a.shape[1:]) for a in arrays], compiler_params=_params(("parallel",)),
    )(*arrays, *extra)


def _adamw_update(w_ref, g_ref, m_ref, v_ref, d_ref, nm_ref, nv_ref):
    g = g_ref[...]
    nm = ADAM_B1 * m_ref[...] + (1.0 - ADAM_B1) * g
    nv = ADAM_B2 * v_ref[...] + (1.0 - ADAM_B2) * (g * g)
    d_ref[...] = -ADAM_LR * ((nm / (1.0 - ADAM_B1 ** ADAM_STEP)) / (jnp.sqrt(nv / (1.0 - ADAM_B2 ** ADAM_STEP)) + ADAM_EPS) + ADAM_WD * w_ref[...])
    nm_ref[...] = nm
    nv_ref[...] = nv


def adamw(w, g, m, v):
    rows, cols = w.shape
    tm = _row_tile(rows, cols)

    def body(*refs):
        _adamw_update(*refs)

    spec = pl.BlockSpec((tm, cols), lambda i: (i, 0))
    return pl.pallas_call(
        body, name="adamw", grid=(rows // tm,), in_specs=[spec] * 4, out_specs=[spec] * 3,
        out_shape=[_sds(w.shape)] * 3, compiler_params=_params(("parallel",)),
    )(w, g, m, v)


def adamw_small(ws, gs, ms, vs, slabs=None):
    n = len(ws)

    def body(*refs):
        for i in range(n):
            _adamw_update(refs[i], refs[n + i], refs[2 * n + i], refs[3 * n + i], refs[4 * n + i], refs[5 * n + i], refs[6 * n + i])

    if slabs is None:
        grid = ()
        specs = [pl.BlockSpec(memory_space=pltpu.VMEM)] * n
    else:
        grid = (slabs,)
        specs = [pl.BlockSpec((w.shape[0] // slabs,) + w.shape[1:], lambda i: (i, 0, 0)) for w in ws]
    outs = pl.pallas_call(
        body, name="adamw_small", grid=grid, in_specs=specs * 4, out_specs=specs * 3,
        out_shape=[_sds(w.shape) for w in ws] * 3, compiler_params=_params(("parallel",) if slabs else None),
    )(*ws, *gs, *ms, *vs)
    return outs[:n], outs[n:2 * n], outs[2 * n:]


PACK_TILE = 8 * LANES
PACK_PIECES = 8
PACK_ALIGN = PACK_PIECES * 16


def _pack_small(values, scalar=None):
    parts = []
    for name in PACK_NAMES:
        flat = values[name].reshape(-1)
        pad = (-flat.shape[0]) % PACK_TILE
        if pad:
            flat = jnp.concatenate([flat, jnp.zeros((pad,), F32)])
        parts.append(flat.reshape(-1, LANES))
    rows = sum(p.shape[0] for p in parts) + 8
    parts.append(jnp.zeros(((-rows) % PACK_ALIGN, LANES), F32))
    last = jnp.zeros((8, LANES), F32)
    parts.append(last if scalar is None else jnp.broadcast_to(scalar.astype(F32), (8, LANES)))
    return jnp.concatenate(parts, axis=0)


def _pack_row_of(name, like):
    row = 0
    for other in PACK_NAMES:
        if other == name:
            return row
        row += -(-math.prod(like[other].shape) // PACK_TILE) * 8
    raise KeyError(name)


def _unpack_small(pack, like):
    out = {}
    row = 0
    for name in PACK_NAMES:
        size = math.prod(like[name].shape)
        rows = -(-size // PACK_TILE) * 8
        out[name] = pack[row:row + rows].reshape(-1)[:size].reshape(like[name].shape)
        row += rows
    return out


def _travels_transposed(name, shard):
    return name.endswith("w_in") and shard.shape[-1] % LANES != 0


def _to_blocks(name, full):
    if full.ndim == 3:
        return full
    return full.reshape(4, full.shape[0] // 4, full.shape[1])


def _from_blocks(name, stacked):
    if name.endswith("w_in") and stacked.shape[2] % LANES == 0 and stacked.shape[1] == D_MODEL:
        return stacked
    return stacked.reshape(4 * stacked.shape[1], stacked.shape[2])


def _layer_big_names(i):
    return [n for n in BIG_NAMES if n.startswith("l%d_" % i)]


class _OverlappedExchanges:
    def __init__(self, weights):
        self.weights = weights
        self.c = lax.axis_index("c")
        self.first = _layer_big_names(0)
        self.later = [n for n in BIG_NAMES if n not in self.first]
        shards = [weights[n].astype(MXU_DTYPE) for n in self.first + self.later]
        shards = [s.T if _travels_transposed(n, s) else s for n, s in zip(self.first + self.later, shards)]
        placed = place_own_blocks(shards)
        k = len(self.first)
        sems, stacks, token = start_sends_to_chips("gather_first_start", shards[:k], placed[:k], False, shards[0])
        self.gather_first = (shards[:k], sems, stacks)
        sems, stacks, token = start_sends_to_chips("gather_later_start", shards[k:], placed[k:], False, token)
        self.gather_later = (shards[k:], sems, stacks)
        self.s5 = {}
        for i in (0, 2):
            self.s5[i] = _s5_matrices({n: weights["l%d_%s" % (i, n)] for n in SSM_NAMES if "l%d_%s" % (i, n) in SMALL_NAMES}, token)
            token = self.s5[i][1]["kd"]
        self.full = {}
        self.in_flight = {}
        self.contributions = {}

    def layer(self, i, x):
        if i == 0:
            shards, sems, stacks = self.gather_first
            stacks = wait_sends_to_chips("gather_first_wait", shards, stacks, sems, False, self.s5[2][1]["kd"])
            self.full.update({n: _from_blocks(n, g) for n, g in zip(self.first, stacks)})
        if i == 1:
            shards, sems, stacks = self.gather_later
            stacks = wait_sends_to_chips("gather_later_wait", shards, stacks, sems, False, x)
            self.full.update({n: _from_blocks(n, g) for n, g in zip(self.later, stacks)})
        names = SSM_NAMES if i % 2 == 0 else ATTN_NAMES
        w = {n: self.full.get("l%d_%s" % (i, n), self.weights.get("l%d_%s" % (i, n))) for n in names}
        if i in self.s5:
            w["s5"] = self.s5[i]
        return w

    def chip_sums(self, names, grads, extra_blocks=(), extra_dtypes=(), copies=1):
        blocks = [_to_blocks(n, grads[n]) for n in names] + list(extra_blocks)
        from_sibling = exchange_halves_with_sibling(blocks)
        k = len(names)
        half = self.c.reshape(1).astype(jnp.int32)
        sums = add_pairs(half, blocks[:k], from_sibling[:k], [WIRE_DTYPE] * k, copies)
        if extra_blocks:
            sums += add_pairs(half, blocks[k:], from_sibling[k:], list(extra_dtypes), copies)
        return sums

    def layer_done(self, i, grads, dx):
        if i + 1 in self.in_flight:
            names, sums, sems, landings = self.in_flight.pop(i + 1)
            done = wait_sends_to_chips("scatter_wait_l%d" % (i + 1), sums, landings, sems, True, dx)
            self.contributions.update(zip(names, done))
        if i == 0:
            return None
        names = _layer_big_names(i)
        pairs = self.chip_sums(names, grads, copies=2)
        sums = [p[0] for p in pairs]
        sems, landings, token = start_sends_to_chips("scatter_start_l%d" % i, sums, [p[1] for p in pairs], True, sums[0])
        self.in_flight[i] = (names, sums, sems, landings)
        return token


def _train_step(x, loss_target, weights, moments_m, moments_v):
    hooks = _OverlappedExchanges(weights)
    loss, dx, grads = _sequence_step(x[0], loss_target[0], weights["final_norm"], hooks)
    small_pack = _pack_small({n: grads[n] for n in SMALL_NAMES}, scalar=loss)
    tail = small_pack.shape[0] - _pack_row_of(EXACT_NAMES[0], grads)
    small = [small_pack[None, :-tail], small_pack[None, -tail:]]
    last = _layer_big_names(0)
    pairs = hooks.chip_sums(last, grads, extra_blocks=small, extra_dtypes=[WIRE_DTYPE, F32], copies=2)
    sums = [p[0] for p in pairs]
    landings = [p[1] for p in pairs[:-2]] + [jnp.broadcast_to(s, (4,) + s.shape[1:]) for s in sums[-2:]]
    sems, landings, token = start_sends_to_chips("scatter_start_l0", sums, landings, True, sums[0])
    out_grad, out_delta, out_m, out_v = {}, {}, {}, {}

    def finish(names, arrays, token=None):
        shared = swap_halves_with_sibling(sum_fours(arrays, token))
        rest = []
        for n, s in zip(names, shared):
            if n not in weights:
                rest.append(s.reshape(-1, LANES))
                continue
            out_grad[n] = s.reshape(2 * s.shape[1], s.shape[2])
            if _travels_transposed(n, weights[n]):
                out_grad[n] = out_grad[n].T
            out_delta[n], out_m[n], out_v[n] = adamw(weights[n], out_grad[n], moments_m[n], moments_v[n])
        return rest

    others = [n for n in BIG_NAMES if n not in last]
    finish(others, [hooks.contributions[n] for n in others], token)
    arrived = wait_sends_to_chips("scatter_wait_l0", sums, landings, sems, True, out_v[others[-1]])
    small_grad_pack = jnp.concatenate(finish(last + ["small", "small tail"], arrived), axis=0)
    loss = small_grad_pack[-8, 0]
    out_grad.update(_unpack_small(small_grad_pack, {n: weights[n] for n in SMALL_NAMES}))
    cubes = [n for n in SMALL_NAMES if weights[n].ndim == 3]
    for names, slabs in ((cubes, 8), ([n for n in SMALL_NAMES if n not in cubes], None)):
        deltas, new_ms, new_vs = adamw_small(*[[group[n] for n in names] for group in (weights, out_grad, moments_m, moments_v)], slabs=slabs)
        out_delta.update(zip(names, deltas))
        out_m.update(zip(names, new_ms))
        out_v.update(zip(names, new_vs))
    outs = [loss, dx[None]]
    for group in (out_grad, out_delta, out_m, out_v):
        outs.extend(group[n] for n in WEIGHT_NAMES)
    return tuple(outs)


def kernel(x, l0_norm, l0_w_in, l0_a_re, l0_a_im, l0_log_step, l0_b_re, l0_b_im, l0_c_re, l0_c_im, l0_d, l0_w_glu, l0_b_glu, l0_w_out, l1_norm, l1_w_in, l1_sinks, l1_w_out, l2_norm, l2_w_in, l2_a_re, l2_a_im, l2_log_step, l2_b_re, l2_b_im, l2_c_re, l2_c_im, l2_d, l2_w_glu, l2_b_glu, l2_w_out, l3_norm, l3_w_in, l3_sinks, l3_w_out, final_norm, loss_target, m_l0_norm, m_l0_w_in, m_l0_a_re, m_l0_a_im, m_l0_log_step, m_l0_b_re, m_l0_b_im, m_l0_c_re, m_l0_c_im, m_l0_d, m_l0_w_glu, m_l0_b_glu, m_l0_w_out, m_l1_norm, m_l1_w_in, m_l1_sinks, m_l1_w_out, m_l2_norm, m_l2_w_in, m_l2_a_re, m_l2_a_im, m_l2_log_step, m_l2_b_re, m_l2_b_im, m_l2_c_re, m_l2_c_im, m_l2_d, m_l2_w_glu, m_l2_b_glu, m_l2_w_out, m_l3_norm, m_l3_w_in, m_l3_sinks, m_l3_w_out, m_final_norm, v_l0_norm, v_l0_w_in, v_l0_a_re, v_l0_a_im, v_l0_log_step, v_l0_b_re, v_l0_b_im, v_l0_c_re, v_l0_c_im, v_l0_d, v_l0_w_glu, v_l0_b_glu, v_l0_w_out, v_l1_norm, v_l1_w_in, v_l1_sinks, v_l1_w_out, v_l2_norm, v_l2_w_in, v_l2_a_re, v_l2_a_im, v_l2_log_step, v_l2_b_re, v_l2_b_im, v_l2_c_re, v_l2_c_im, v_l2_d, v_l2_w_glu, v_l2_b_glu, v_l2_w_out, v_l3_norm, v_l3_w_in, v_l3_sinks, v_l3_w_out, v_final_norm):
    args = locals()
    weights = {n: args[n] for n in WEIGHT_NAMES}
    moments_m = {n: args["m_" + n] for n in WEIGHT_NAMES}
    moments_v = {n: args["v_" + n] for n in WEIGHT_NAMES}
    return _train_step(x, loss_target, weights, moments_m, moments_v)
```

```python
import functools
import math

import jax
import jax.numpy as jnp
from jax import lax
from jax.experimental import pallas as pl
from jax.experimental.pallas import tpu as pltpu

F32 = jnp.float32
MXU_DTYPE = jnp.bfloat16
WIRE_DTYPE = jnp.bfloat16
MESH = pl.DeviceIdType.MESH

D_MODEL = 1024
BRANCH = 1024
NORM_EPS = 1e-5
SSM_GROUPS = 64
SSM_GROUP = 16
SSM_STATE = 64
S5_CHUNK = 16
LANES = 128
S5_OCT = LANES // SSM_GROUP
S5_OCTETS = SSM_GROUPS // S5_OCT
S5_OCT_IN = S5_CHUNK * LANES
S5_OCT_STATE = S5_OCT * SSM_STATE
S5_STATES = SSM_GROUPS * SSM_STATE
HEAD_DIM = 64
N_Q_HEADS = 16
N_KV_HEADS = 2
GQA_GROUP = N_Q_HEADS // N_KV_HEADS
ATTN_BLOCK = 128
Q_DIM = N_Q_HEADS * HEAD_DIM
KV_DIM = N_KV_HEADS * HEAD_DIM
ROPE_THETA = 10000.0
NEG_INF = -1e30
ADAM_LR = 0.001
ADAM_B1 = 0.9
ADAM_B2 = 0.999
ADAM_EPS = 1e-08
ADAM_WD = 0.01
ADAM_STEP = 10

VMEM_LIMIT_V7X = 56 * 1024 * 1024
ROW_TILE_FWD = 512
ROW_TILE_BWD = 512

SSM_NAMES = ("norm", "w_in", "a_re", "a_im", "log_step", "b_re", "b_im", "c_re", "c_im", "d", "w_glu", "b_glu", "w_out")
ATTN_NAMES = ("norm", "w_in", "sinks", "w_out")


def _weight_names():
    names = []
    for i in range(4):
        for n in (SSM_NAMES if i % 2 == 0 else ATTN_NAMES):
            names.append("l%d_%s" % (i, n))
    names.append("final_norm")
    return names


WEIGHT_NAMES = _weight_names()
BIG_NAMES = [n for n in WEIGHT_NAMES if n.endswith(("w_in", "w_glu", "w_out"))]
SMALL_NAMES = [n for n in WEIGHT_NAMES if n not in BIG_NAMES]
EXACT_NAMES = [n for n in SMALL_NAMES if n.endswith(("log_step", "sinks")) or n == "final_norm"]
PACK_NAMES = [n for n in SMALL_NAMES if n not in EXACT_NAMES] + EXACT_NAMES


def _params(semantics=None):
    return pltpu.CompilerParams(dimension_semantics=semantics, vmem_limit_bytes=VMEM_LIMIT_V7X)


def _rows(tm, n):
    return pl.BlockSpec((tm, n), lambda i: (i, 0))


def _whole(shape):
    return pl.BlockSpec(shape, lambda i: (0,) * len(shape), pipeline_mode=pl.Buffered(1))


def _sds(shape, dtype=F32):
    return jax.ShapeDtypeStruct(shape, dtype)


def _mm(a, b):
    return jnp.dot(a.astype(MXU_DTYPE), b.astype(MXU_DTYPE), preferred_element_type=F32)


def _mm_tn(a, b):
    return lax.dot_general(a.astype(MXU_DTYPE), b.astype(MXU_DTYPE), (((0,), (0,)), ((), ())), preferred_element_type=F32)


def _mm_nt(a, b):
    return lax.dot_general(a.astype(MXU_DTYPE), b.astype(MXU_DTYPE), (((1,), (1,)), ((), ())), preferred_element_type=F32)


def _sigmoid(x):
    return 0.5 + 0.5 * jnp.tanh(0.5 * x)


def _silu(x):
    return x * _sigmoid(x)


def _silu_and_grad(x):
    s = _sigmoid(x)
    return x * s, s * (1.0 + x * (1.0 - s))


GELU_C0 = math.sqrt(2.0 / math.pi)
GELU_C1 = 0.044715


def _gelu(x):
    return 0.5 * x * (1.0 + jnp.tanh(GELU_C0 * (x + GELU_C1 * x * x * x)))


def _gelu_and_grad(x):
    x2 = x * x
    th = jnp.tanh(GELU_C0 * x * (1.0 + GELU_C1 * x2))
    half = 0.5 + 0.5 * th
    return x * half, half + 0.5 * x * (1.0 - th * th) * (GELU_C0 + 3.0 * GELU_C0 * GELU_C1 * x2)


def _rms(x, g):
    r = lax.rsqrt(jnp.mean(x * x, axis=-1, keepdims=True) + NORM_EPS)
    xhat = x * r
    return r, xhat, xhat * g


def _rms_bwd(dh, g, r, xhat):
    dxhat = dh * g
    dx = r * (dxhat - xhat * jnp.mean(dxhat * xhat, axis=-1, keepdims=True))
    return dx, jnp.sum(dh * xhat, axis=0, keepdims=True)


def _swap_half_heads(x):
    n = x.shape[-1]
    lane = lax.broadcasted_iota(jnp.int32, x.shape, x.ndim - 1)
    first = (lane % HEAD_DIM) < (HEAD_DIM // 2)
    return jnp.where(first, pltpu.roll(x, n - HEAD_DIM // 2, x.ndim - 1), pltpu.roll(x, HEAD_DIM // 2, x.ndim - 1))


def _tile_lanes(t, reps):
    return jnp.concatenate([t] * reps, axis=1)


TOKEN_SHAPE = (8, LANES)


def _after(token):
    return ([], []) if token is None else ([token], [_whole(TOKEN_SHAPE)])


def ssm_proj_fwd(x, norm, w_in):
    t = x.shape[0]
    tm = min(ROW_TILE_FWD, t)

    def body(x_ref, g_ref, w_ref, u_ref, gate_ref):
        _, _, h = _rms(x_ref[...], g_ref[...])
        h = h.astype(MXU_DTYPE)
        half = BRANCH // 2
        for j in range(2):
            u_ref[:, j * half:(j + 1) * half] = _mm(h, w_ref[j])
            gate_ref[:, j * half:(j + 1) * half] = _mm(h, w_ref[2 + j])

    return pl.pallas_call(
        body, name="ssm_proj_fwd", grid=(t // tm,),
        in_specs=[_rows(tm, D_MODEL), _whole((1, D_MODEL)), _whole((4, D_MODEL, BRANCH // 2))],
        out_specs=[_rows(tm, BRANCH), _rows(tm, BRANCH)],
        out_shape=[_sds((t, BRANCH)), _sds((t, BRANCH))],
        compiler_params=_params(("parallel",)),
    )(x, norm, w_in)


def _chunk_rows(ref, nk, dtype=None):
    rows = jnp.concatenate([ref[pl.ds(s, nk, stride=S5_CHUNK), :] for s in range(S5_CHUNK)], axis=1)
    return rows.astype(MXU_DTYPE if dtype is None else dtype)


def _store_chunk_rows(ref, val, nk):
    for s in range(S5_CHUNK):
        ref[pl.ds(s, nk, stride=S5_CHUNK), :] = val[:, s * LANES:(s + 1) * LANES]


def _own_group_mask():
    row = lax.broadcasted_iota(jnp.int32, (S5_OCT_IN, S5_OCT_STATE), 0)
    col = lax.broadcasted_iota(jnp.int32, (S5_OCT_IN, S5_OCT_STATE), 1)
    return ((row % LANES) // SSM_GROUP) == (col // SSM_STATE)


def _spread_groups(w):
    return jnp.where(_own_group_mask(), jnp.concatenate([w] * (S5_OCT_STATE // LANES), axis=1), 0.0).astype(MXU_DTYPE)


def _fold_groups(p):
    p = jnp.where(_own_group_mask(), p, 0.0)
    return sum(p[:, q * LANES:(q + 1) * LANES] for q in range(S5_OCT_STATE // LANES))


def _fill_toeplitz(win_ref, kd_ref):
    win_ref[...] = jnp.zeros_like(win_ref)
    for s in range(S5_CHUNK):
        for t in range(s, S5_CHUNK):
            win_ref[s * LANES:(s + 1) * LANES, t * LANES:(t + 1) * LANES] = kd_ref[t - s].astype(MXU_DTYPE)


TOEPLITZ_BLOCK = 512
_TOEPLITZ_BLOCKS = [(lo, lo + TOEPLITZ_BLOCK) for lo in range(0, S5_OCT_IN, TOEPLITZ_BLOCK)]


def _strip(t):
    return pl.BlockSpec((t, LANES), lambda b: (0, b))


def _oct_states(nk):
    return pl.BlockSpec((nk, S5_OCT_STATE), lambda b: (0, b))


OCT_W = pl.BlockSpec((None, S5_OCT_IN, LANES), lambda b: (b, 0, 0))
OCT_KD = pl.BlockSpec((None, S5_CHUNK, LANES, LANES), lambda b: (b, 0, 0, 0))


def s5_chunk_states(u, ws_re, ws_im):
    t = u.shape[0]
    nk = t // S5_CHUNK

    def body(u_ref, wr_ref, wi_ref, re_ref, im_ref):
        uc = _chunk_rows(u_ref, nk)
        re_ref[...] = _mm(uc, _spread_groups(wr_ref[...]))
        im_ref[...] = _mm(uc, _spread_groups(wi_ref[...]))

    return pl.pallas_call(
        body, name="s5_chunk_states", grid=(S5_OCTETS,),
        in_specs=[_strip(t), OCT_W, OCT_W], out_specs=[_oct_states(nk), _oct_states(nk)],
        out_shape=[_sds((nk, S5_STATES)), _sds((nk, S5_STATES))],
        compiler_params=_params(("parallel",)),
    )(u, ws_re, ws_im)


def s5_scan_fwd(s_re, s_im, a_re, a_im):
    nk = s_re.shape[0]

    def body(sre_ref, sim_ref, ar_ref, ai_ref, hre_ref, him_ref):
        ar = ar_ref[...]
        ai = ai_ref[...]

        def step(k, carry):
            hr, hi = carry
            hre_ref[pl.ds(k, 1), :] = hr
            him_ref[pl.ds(k, 1), :] = hi
            sr = sre_ref[pl.ds(k, 1), :]
            si = sim_ref[pl.ds(k, 1), :]
            return ar * hr - ai * hi + sr, ai * hr + ar * hi + si

        zero = jnp.zeros((1, S5_STATES), F32)
        lax.fori_loop(0, nk, step, (zero, zero))

    vm = pl.BlockSpec(memory_space=pltpu.VMEM)
    return pl.pallas_call(
        body, name="s5_scan_fwd", in_specs=[vm, vm, vm, vm], out_specs=[vm, vm],
        out_shape=[_sds((nk, S5_STATES)), _sds((nk, S5_STATES))],
        compiler_params=_params(),
    )(s_re, s_im, a_re, a_im)


def s5_outputs(u, h_re, h_im, kd, wo_re, wo_im):
    t = u.shape[0]
    nk = t // S5_CHUNK

    def body(u_ref, hre_ref, him_ref, kd_ref, wor_ref, woi_ref, y_ref, win_ref):
        _fill_toeplitz(win_ref, kd_ref)
        uc = _chunk_rows(u_ref, nk)
        y = jnp.concatenate([_mm(uc[:, :hi], win_ref[:hi, lo:hi]) for lo, hi in _TOEPLITZ_BLOCKS], axis=1)
        y = y + _mm_nt(hre_ref[...], _spread_groups(wor_ref[...])) + _mm_nt(him_ref[...], _spread_groups(woi_ref[...]))
        _store_chunk_rows(y_ref, y, nk)

    return pl.pallas_call(
        body, name="s5_outputs", grid=(S5_OCTETS,),
        in_specs=[_strip(t), _oct_states(nk), _oct_states(nk), OCT_KD, OCT_W, OCT_W],
        out_specs=_strip(t), out_shape=_sds((t, BRANCH)),
        scratch_shapes=[pltpu.VMEM((S5_OCT_IN, S5_OCT_IN), MXU_DTYPE)],
        compiler_params=_params(("parallel",)),
    )(u, h_re, h_im, kd, wo_re, wo_im)


def s5_state_grads(dy, wo_re, wo_im):
    t = dy.shape[0]
    nk = t // S5_CHUNK

    def body(dy_ref, wor_ref, woi_ref, re_ref, im_ref):
        dyc = _chunk_rows(dy_ref, nk)
        re_ref[...] = _mm(dyc, _spread_groups(wor_ref[...]))
        im_ref[...] = _mm(dyc, _spread_groups(woi_ref[...]))

    return pl.pallas_call(
        body, name="s5_state_grads", grid=(S5_OCTETS,),
        in_specs=[_strip(t), OCT_W, OCT_W], out_specs=[_oct_states(nk), _oct_states(nk)],
        out_shape=[_sds((nk, S5_STATES)), _sds((nk, S5_STATES))],
        compiler_params=_params(("parallel",)),
    )(dy, wo_re, wo_im)


def s5_scan_bwd(dh_re, dh_im, h_re, h_im, a_re, a_im):
    nk = dh_re.shape[0]

    def body(dhr_ref, dhi_ref, hr_ref, hi_ref, ar_ref, ai_ref, dsr_ref, dsi_ref, dar_ref, dai_ref):
        ar = ar_ref[...]
        ai = ai_ref[...]

        def step(i, carry):
            gr, gi = carry
            k = nk - 1 - i
            dhr = dhr_ref[pl.ds(k, 1), :]
            dhi = dhi_ref[pl.ds(k, 1), :]
            dsr_ref[pl.ds(k, 1), :] = gr
            dsi_ref[pl.ds(k, 1), :] = gi
            return dhr + ar * gr + ai * gi, dhi - ai * gr + ar * gi

        zero = jnp.zeros((1, S5_STATES), F32)
        lax.fori_loop(0, nk, step, (zero, zero))
        dsr, dsi, hr, hi = dsr_ref[...], dsi_ref[...], hr_ref[...], hi_ref[...]
        dar_ref[...] = jnp.sum(dsr * hr + dsi * hi, axis=0, keepdims=True)
        dai_ref[...] = jnp.sum(dsi * hr - dsr * hi, axis=0, keepdims=True)

    vm = pl.BlockSpec(memory_space=pltpu.VMEM)
    return pl.pallas_call(
        body, name="s5_scan_bwd", in_specs=[vm] * 6, out_specs=[vm] * 4,
        out_shape=[_sds((nk, S5_STATES)), _sds((nk, S5_STATES)), _sds((1, S5_STATES)), _sds((1, S5_STATES))],
        input_output_aliases={0: 0, 1: 1}, compiler_params=_params(),
    )(dh_re, dh_im, h_re, h_im, a_re, a_im)


def s5_input_grads(dy, ds_re, ds_im, kd, ws_re, ws_im):
    t = dy.shape[0]
    nk = t // S5_CHUNK

    def body(dy_ref, dsr_ref, dsi_ref, kd_ref, wsr_ref, wsi_ref, du_ref, win_ref):
        _fill_toeplitz(win_ref, kd_ref)
        dyc = _chunk_rows(dy_ref, nk)
        du = jnp.concatenate([_mm_nt(dyc[:, lo:], win_ref[lo:hi, lo:]) for lo, hi in _TOEPLITZ_BLOCKS], axis=1)
        du = du + _mm_nt(dsr_ref[...], _spread_groups(wsr_ref[...])) + _mm_nt(dsi_ref[...], _spread_groups(wsi_ref[...]))
        _store_chunk_rows(du_ref, du, nk)

    return pl.pallas_call(
        body, name="s5_input_grads", grid=(S5_OCTETS,),
        in_specs=[_strip(t), _oct_states(nk), _oct_states(nk), OCT_KD, OCT_W, OCT_W],
        out_specs=_strip(t), out_shape=_sds((t, BRANCH)),
        scratch_shapes=[pltpu.VMEM((S5_OCT_IN, S5_OCT_IN), MXU_DTYPE)],
        compiler_params=_params(("parallel",)),
    )(dy, ds_re, ds_im, kd, ws_re, ws_im)


def s5_weight_grads(u, dy, h_re, h_im, ds_re, ds_im):
    t = u.shape[0]
    nk = t // S5_CHUNK

    def body(u_ref, dy_ref, hre_ref, him_ref, dsr_ref, dsi_ref, dkd_ref, dwsr_ref, dwsi_ref, dwor_ref, dwoi_ref):
        dyc = _chunk_rows(dy_ref, nk, F32)
        uct = _chunk_rows(u_ref, nk, F32).T.astype(MXU_DTYPE)
        dyct = dyc.T.astype(MXU_DTYPE)
        dyc = dyc.astype(MXU_DTYPE)
        dwsr_ref[...] = _fold_groups(_mm(uct, dsr_ref[...]))
        dwsi_ref[...] = _fold_groups(_mm(uct, dsi_ref[...]))
        dwor_ref[...] = _fold_groups(_mm(dyct, hre_ref[...]))
        dwoi_ref[...] = _fold_groups(_mm(dyct, him_ref[...]))
        dkd_ref[...] = jnp.zeros_like(dkd_ref)
        for tt in range(0, S5_CHUNK, 2):
            p = _mm(uct[:(tt + 2) * LANES], dyc[:, tt * LANES:(tt + 2) * LANES])
            for s in range(tt + 2):
                rows = p[s * LANES:(s + 1) * LANES]
                if s <= tt:
                    dkd_ref[tt - s] += rows[:, :LANES]
                dkd_ref[tt + 1 - s] += rows[:, LANES:]

    return pl.pallas_call(
        body, name="s5_weight_grads", grid=(S5_OCTETS,),
        in_specs=[_strip(t), _strip(t)] + [_oct_states(nk)] * 4,
        out_specs=[OCT_KD, OCT_W, OCT_W, OCT_W, OCT_W],
        out_shape=[_sds((S5_OCTETS, S5_CHUNK, LANES, LANES))] + [_sds((S5_OCTETS, S5_OCT_IN, LANES))] * 4,
        compiler_params=_params(("parallel",)),
    )(u, dy, h_re, h_im, ds_re, ds_im)


def ssm_mix_fwd(x, u, gate, y_scan, d, w_glu, b_glu, w_out):
    t = x.shape[0]
    tm = min(ROW_TILE_FWD, t)

    def body(x_ref, u_ref, gate_ref, ys_ref, d_ref, wg_ref, bg_ref, wo_ref, y_ref, g2_ref, xo_ref):
        y = ys_ref[...] + d_ref[...] * u_ref[...]
        z0 = _gelu(y)
        g2 = _mm(z0, wg_ref[...]) + bg_ref[...]
        a = z0 * _sigmoid(g2) * _silu(gate_ref[...])
        y_ref[...] = y
        g2_ref[...] = g2
        xo_ref[...] = x_ref[...] + _mm(a, wo_ref[...])

    row = _rows(tm, BRANCH)
    vec = _whole((1, BRANCH))
    mat = _whole((BRANCH, BRANCH))
    return pl.pallas_call(
        body, name="ssm_mix_fwd", grid=(t // tm,),
        in_specs=[row, row, row, row, vec, mat, vec, mat],
        out_specs=[row, row, row],
        out_shape=[_sds((t, BRANCH))] * 3,
        compiler_params=_params(("parallel",)),
    )(x, u, gate, y_scan, d, w_glu, b_glu, w_out)


def ssm_mix_bwd(dxo, u, gate, y, g2, w_glu, w_out, token=None):
    t = dxo.shape[0]
    tm = min(ROW_TILE_BWD, t)
    extra, extra_specs = _after(token)

    def body(dxo_ref, u_ref, gate_ref, y_ref, g2_ref, wgt_ref, wot_ref, *rest):
        dy_ref, dgate_ref, dwo_ref, dwg_ref, dbg_ref, dd_ref = rest[-6:]

        @pl.when(pl.program_id(0) == 0)
        def _():
            dwo_ref[...] = jnp.zeros_like(dwo_ref)
            dwg_ref[...] = jnp.zeros_like(dwg_ref)
            dbg_ref[...] = jnp.zeros_like(dbg_ref)
            dd_ref[...] = jnp.zeros_like(dd_ref)

        dxo = dxo_ref[...]
        gate = gate_ref[...]
        y = y_ref[...]
        z0, z0_grad = _gelu_and_grad(y)
        sg = _sigmoid(g2_ref[...])
        z = z0 * sg
        sgate, sgate_grad = _silu_and_grad(gate)
        da = _mm_nt(dxo, wot_ref[...])
        dwo_ref[...] += _mm_tn(z * sgate, dxo)
        dz = da * sgate
        dgate_ref[...] = da * z * sgate_grad
        dg2 = dz * z0 * sg * (1.0 - sg)
        dbg_ref[...] += jnp.sum(dg2, axis=0, keepdims=True)
        dwg_ref[...] += _mm_tn(z0, dg2)
        dz0 = dz * sg + _mm_nt(dg2, wgt_ref[...])
        dy = dz0 * z0_grad
        dd_ref[...] += jnp.sum(dy * u_ref[...], axis=0, keepdims=True)
        dy_ref[...] = dy

    row = _rows(tm, BRANCH)
    vec = _whole((1, BRANCH))
    mat = _whole((BRANCH, BRANCH))
    return pl.pallas_call(
        body, name="ssm_mix_bwd", grid=(t // tm,),
        in_specs=[row, row, row, row, row, mat, mat] + extra_specs,
        out_specs=[row, row, mat, mat, vec, vec],
        out_shape=[_sds((t, BRANCH)), _sds((t, BRANCH)), _sds((BRANCH, D_MODEL)), _sds((BRANCH, BRANCH)),
                   _sds((1, BRANCH)), _sds((1, BRANCH))],
        compiler_params=_params(("arbitrary",)),
    )(dxo, u, gate, y, g2, w_glu, w_out, *extra)


def ssm_proj_bwd(x, norm, dxo, dy, du_scan, dgate, d, w_in):
    t = x.shape[0]
    tm = min(ROW_TILE_BWD, t)
    n = 2 * BRANCH

    def body(x_ref, g_ref, dxo_ref, dy_ref, dus_ref, dgate_ref, d_ref, wt_ref, dx_ref, dw_ref, dg_ref):
        @pl.when(pl.program_id(0) == 0)
        def _():
            dw_ref[...] = jnp.zeros_like(dw_ref)
            dg_ref[...] = jnp.zeros_like(dg_ref)

        g = g_ref[...]
        r, xhat, h = _rms(x_ref[...], g)
        h = h.astype(MXU_DTYPE)
        du = dus_ref[...] + d_ref[...] * dy_ref[...]
        dproj = jnp.concatenate([du, dgate_ref[...]], axis=1).astype(MXU_DTYPE)
        dh = jnp.zeros((tm, D_MODEL), F32)
        for j in range(4):
            cols = dproj[:, j * (n // 4):(j + 1) * (n // 4)]
            dh = dh + _mm_nt(cols, wt_ref[j])
            dw_ref[j] += _mm_tn(h, cols)
        dx, dg = _rms_bwd(dh, g, r, xhat)
        dg_ref[...] += dg
        dx_ref[...] = dxo_ref[...] + dx

    row = _rows(tm, D_MODEL)
    vec = _whole((1, D_MODEL))
    blocks = _whole((4, D_MODEL, n // 4))
    return pl.pallas_call(
        body, name="ssm_proj_bwd", grid=(t // tm,),
        in_specs=[row, vec, row, row, row, row, vec, blocks],
        out_specs=[row, blocks, vec],
        out_shape=[_sds((t, D_MODEL)), _sds((4, D_MODEL, n // 4)), _sds((1, D_MODEL))],
        compiler_params=_params(("arbitrary",)),
    )(x, norm, dxo, dy, du_scan, dgate, d, w_in)


ATTN_N = Q_DIM + 2 * KV_DIM + BRANCH


def attn_proj_fwd(x, norm, w_in_t, cos2, sin2):
    t = x.shape[0]
    tm = min(ROW_TILE_FWD, t)

    def body(x_ref, g_ref, w_ref, cos_ref, sin_ref, q_ref, k_ref, v_ref, gate_ref):
        _, _, h = _rms(x_ref[...], g_ref[...])
        p = _mm_nt(h, w_ref[...])
        cs = cos_ref[...]
        sn = sin_ref[...]
        q = p[:, :Q_DIM]
        k = p[:, Q_DIM:Q_DIM + KV_DIM]
        q_ref[...] = q * _tile_lanes(cs, Q_DIM // LANES) + _swap_half_heads(q) * _tile_lanes(sn, Q_DIM // LANES)
        k_ref[...] = k * cs + _swap_half_heads(k) * sn
        v_ref[...] = p[:, Q_DIM + KV_DIM:Q_DIM + 2 * KV_DIM]
        gate_ref[...] = p[:, Q_DIM + 2 * KV_DIM:]

    return pl.pallas_call(
        body, name="attn_proj_fwd", grid=(t // tm,),
        in_specs=[_rows(tm, D_MODEL), _whole((1, D_MODEL)), _whole((ATTN_N, D_MODEL)), _rows(tm, LANES), _rows(tm, LANES)],
        out_specs=[_rows(tm, Q_DIM), _rows(tm, KV_DIM), _rows(tm, KV_DIM), _rows(tm, BRANCH)],
        out_shape=[_sds((t, Q_DIM)), _sds((t, KV_DIM)), _sds((t, KV_DIM)), _sds((t, BRANCH))],
        compiler_params=_params(("parallel",)),
    )(x, norm, w_in_t, cos2, sin2)


GQA_LANES = GQA_GROUP * ATTN_BLOCK


def _window_masks(first_block):
    kj = lax.broadcasted_iota(jnp.int32, (ATTN_BLOCK, GQA_LANES), 0)
    qi = lax.broadcasted_iota(jnp.int32, (ATTN_BLOCK, GQA_LANES), 1) % ATTN_BLOCK
    return kj > qi, kj > jnp.where(first_block, qi, ATTN_BLOCK)


def _fold(upper, both):
    return jnp.where(upper, both[:ATTN_BLOCK], both[ATTN_BLOCK:])


def _unfold(upper, tile):
    return jnp.concatenate([jnp.where(upper, tile, 0.0), jnp.where(upper, 0.0, tile)], axis=0).astype(MXU_DTYPE)


def _stack_heads(ref, group):
    return jnp.concatenate([ref[:, h * HEAD_DIM:(h + 1) * HEAD_DIM] for h in range(group * GQA_GROUP, (group + 1) * GQA_GROUP)], axis=0)


def _unstack_heads(ref, group, stacked):
    for n in range(GQA_GROUP):
        h = group * GQA_GROUP + n
        ref[:, h * HEAD_DIM:(h + 1) * HEAD_DIM] = stacked[n * ATTN_BLOCK:(n + 1) * ATTN_BLOCK]


def _sink_row(sink_ref, group):
    return jnp.concatenate([jnp.full((1, ATTN_BLOCK), sink_ref[group * GQA_GROUP + n], F32) for n in range(GQA_GROUP)], axis=1)


def _lane_is(h):
    return lax.broadcasted_iota(jnp.int32, (1, LANES), 1) == h


def attn_fwd(q, k, v, sinks):
    t = q.shape[0]
    nb = t // ATTN_BLOCK
    scale = HEAD_DIM ** -0.5

    def body(sink_ref, q_ref, kc_ref, kp_ref, vc_ref, vp_ref, o_ref, lse_ref):
        keys = jnp.concatenate([kp_ref[...], kc_ref[...]], axis=0).astype(MXU_DTYPE)
        vals = jnp.concatenate([vp_ref[...], vc_ref[...]], axis=0).astype(MXU_DTYPE)
        upper, dead = _window_masks(pl.program_id(0) == 0)
        for g in range(N_KV_HEADS):
            kv = slice(g * HEAD_DIM, (g + 1) * HEAD_DIM)
            qs = _stack_heads(q_ref, g) * scale
            s = jnp.where(dead, NEG_INF, _fold(upper, _mm_nt(keys[:, kv], qs)))
            sink = _sink_row(sink_ref, g)
            m = jnp.maximum(jnp.max(s, axis=0, keepdims=True), sink)
            p = jnp.exp(s - m)
            den = jnp.sum(p, axis=0, keepdims=True) + jnp.exp(sink - m)
            _unstack_heads(o_ref, g, _mm_tn(_unfold(upper, p * (1.0 / den)), vals[:, kv]))
            lse = m + jnp.log(den)
            for n in range(GQA_GROUP):
                lse_ref[pl.ds(g * GQA_GROUP + n, 1), :] = lse[:, n * ATTN_BLOCK:(n + 1) * ATTN_BLOCK]

    cur = lambda n: pl.BlockSpec((ATTN_BLOCK, n), lambda i: (i, 0))
    prev = lambda n: pl.BlockSpec((ATTN_BLOCK, n), lambda i: (jnp.maximum(i - 1, 0), 0))
    return pl.pallas_call(
        body, name="attn_fwd", grid=(nb,),
        in_specs=[pl.BlockSpec(memory_space=pltpu.SMEM), cur(Q_DIM), cur(KV_DIM), prev(KV_DIM), cur(KV_DIM), prev(KV_DIM)],
        out_specs=[cur(Q_DIM), pl.BlockSpec((N_Q_HEADS, ATTN_BLOCK), lambda i: (0, i))],
        out_shape=[_sds((t, Q_DIM)), _sds((N_Q_HEADS, t))],
        compiler_params=_params(("parallel",)),
    )(sinks, q, k, k, v, v)


def attn_bwd(q, k, v, sinks, o, lse, do):
    t = q.shape[0]
    nb = t // ATTN_BLOCK
    scale = HEAD_DIM ** -0.5

    def body(sink_ref, q_ref, kc_ref, kp_ref, vc_ref, vp_ref, o_ref, lse_ref, do_ref,
             dq_ref, dk_ref, dv_ref, dsink_ref, dk_carry, dv_carry):
        i = pl.program_id(0)

        @pl.when(i == 0)
        def _():
            dsink_ref[...] = jnp.zeros_like(dsink_ref)
            dk_carry[...] = jnp.zeros_like(dk_carry)
            dv_carry[...] = jnp.zeros_like(dv_carry)

        @pl.when(i < nb)
        def _():
            keys = jnp.concatenate([kp_ref[...], kc_ref[...]], axis=0).astype(MXU_DTYPE)
            vals = jnp.concatenate([vp_ref[...], vc_ref[...]], axis=0).astype(MXU_DTYPE)
            upper, dead = _window_masks(i == 0)
            dsink = jnp.zeros((1, LANES), F32)
            dk_heads = []
            dv_heads = []
            for g in range(N_KV_HEADS):
                kv = slice(g * HEAD_DIM, (g + 1) * HEAD_DIM)
                qs = (_stack_heads(q_ref, g) * scale).astype(MXU_DTYPE)
                dos = _stack_heads(do_ref, g)
                lse = jnp.concatenate([lse_ref[pl.ds(g * GQA_GROUP + n, 1), :] for n in range(GQA_GROUP)], axis=1)
                s = jnp.where(dead, NEG_INF, _fold(upper, _mm_nt(keys[:, kv], qs)))
                p = jnp.exp(s - lse)
                delta = _mm_f32(jnp.ones((8, HEAD_DIM), F32), dos * _stack_heads(o_ref, g), ((1,), (1,)))[:1]
                dos = dos.astype(MXU_DTYPE)
                ds = _unfold(upper, p * (_fold(upper, _mm_nt(vals[:, kv], dos)) - delta))
                _unstack_heads(dq_ref, g, _mm_tn(ds, keys[:, kv]) * scale)
                dk_heads.append(_mm(ds, qs))
                dv_heads.append(_mm(_unfold(upper, p), dos))
                at_sink = jnp.exp(_sink_row(sink_ref, g) - lse) * delta
                for n in range(GQA_GROUP):
                    dsink = dsink + jnp.where(_lane_is(g * GQA_GROUP + n), -jnp.sum(at_sink[:, n * ATTN_BLOCK:(n + 1) * ATTN_BLOCK]), 0.0)
            dkk = jnp.concatenate(dk_heads, axis=1)
            dvv = jnp.concatenate(dv_heads, axis=1)
            dsink_ref[...] += dsink
            dk_ref[...] = dk_carry[...] + dkk[:ATTN_BLOCK]
            dv_ref[...] = dv_carry[...] + dvv[:ATTN_BLOCK]
            dk_carry[...] = dkk[ATTN_BLOCK:]
            dv_carry[...] = dvv[ATTN_BLOCK:]

        @pl.when(i == nb)
        def _():
            dk_ref[...] = dk_carry[...]
            dv_ref[...] = dv_carry[...]

    last = nb - 1
    cur = lambda n: pl.BlockSpec((ATTN_BLOCK, n), lambda i: (jnp.minimum(i, last), 0))
    prev = lambda n: pl.BlockSpec((ATTN_BLOCK, n), lambda i: (jnp.clip(i - 1, 0, last), 0))
    late = lambda n: pl.BlockSpec((ATTN_BLOCK, n), lambda i: (i, 0))
    dq, dk_late, dv_late, dsinks = pl.pallas_call(
        body, name="attn_bwd", grid=(nb + 1,),
        in_specs=[pl.BlockSpec(memory_space=pltpu.SMEM), cur(Q_DIM), cur(KV_DIM), prev(KV_DIM), cur(KV_DIM), prev(KV_DIM),
                  cur(Q_DIM), pl.BlockSpec((N_Q_HEADS, ATTN_BLOCK), lambda i: (0, jnp.minimum(i, last))), cur(Q_DIM)],
        out_specs=[cur(Q_DIM), late(KV_DIM), late(KV_DIM), _whole((1, LANES))],
        out_shape=[_sds((t, Q_DIM)), _sds((t + ATTN_BLOCK, KV_DIM)), _sds((t + ATTN_BLOCK, KV_DIM)), _sds((1, LANES))],
        scratch_shapes=[pltpu.VMEM((ATTN_BLOCK, KV_DIM), F32), pltpu.VMEM((ATTN_BLOCK, KV_DIM), F32)],
        compiler_params=_params(("arbitrary",)),
    )(sinks, q, k, k, v, v, o, lse, do)
    return dq, dk_late[ATTN_BLOCK:], dv_late[ATTN_BLOCK:], dsinks


def attn_out_fwd(x, o, gate, w_out):
    t = x.shape[0]
    tm = min(ROW_TILE_FWD, t)

    def body(x_ref, o_ref, gate_ref, w_ref, xo_ref):
        xo_ref[...] = x_ref[...] + _mm(o_ref[...] * _silu(gate_ref[...]), w_ref[...])

    row = _rows(tm, D_MODEL)
    return pl.pallas_call(
        body, name="attn_out_fwd", grid=(t // tm,),
        in_specs=[row, row, row, _whole((Q_DIM, D_MODEL))], out_specs=row, out_shape=_sds((t, D_MODEL)),
        compiler_params=_params(("parallel",)),
    )(x, o, gate, w_out)


def attn_out_bwd(dxo, o, gate, w_out, token=None):
    t = dxo.shape[0]
    tm = min(ROW_TILE_BWD, t)
    extra, extra_specs = _after(token)

    def body(dxo_ref, o_ref, gate_ref, wt_ref, *rest):
        do_ref, dgate_ref, dw_ref = rest[-3:]

        @pl.when(pl.program_id(0) == 0)
        def _():
            dw_ref[...] = jnp.zeros_like(dw_ref)

        dxo = dxo_ref[...]
        o = o_ref[...]
        gate = gate_ref[...]
        sgate, sgate_grad = _silu_and_grad(gate)
        da = _mm_nt(dxo, wt_ref[...])
        dw_ref[...] += _mm_tn(o * sgate, dxo)
        do_ref[...] = da * sgate
        dgate_ref[...] = da * o * sgate_grad

    row = _rows(tm, D_MODEL)
    mat = _whole((Q_DIM, D_MODEL))
    return pl.pallas_call(
        body, name="attn_out_bwd", grid=(t // tm,),
        in_specs=[row, row, row, mat] + extra_specs, out_specs=[row, row, mat],
        out_shape=[_sds((t, Q_DIM)), _sds((t, BRANCH)), _sds((Q_DIM, D_MODEL))],
        compiler_params=_params(("arbitrary",)),
    )(dxo, o, gate, w_out, *extra)


def attn_proj_bwd(x, norm, dxo, dq, dk, dv, dgate, cos2, sin2, w_in_t):
    t = x.shape[0]
    tm = min(ROW_TILE_BWD, t)

    def body(x_ref, g_ref, dxo_ref, dq_ref, dk_ref, dv_ref, dgate_ref, cos_ref, sin_ref, wt_ref, dx_ref, dw_ref, dg_ref):
        @pl.when(pl.program_id(0) == 0)
        def _():
            dw_ref[...] = jnp.zeros_like(dw_ref)
            dg_ref[...] = jnp.zeros_like(dg_ref)

        g = g_ref[...]
        r, xhat, h = _rms(x_ref[...], g)
        cs = cos_ref[...]
        sn = sin_ref[...]
        dqr = dq_ref[...]
        dkr = dk_ref[...]
        dq = dqr * _tile_lanes(cs, Q_DIM // LANES) + _swap_half_heads(dqr * _tile_lanes(sn, Q_DIM // LANES))
        dk = dkr * cs + _swap_half_heads(dkr * sn)
        dproj = jnp.concatenate([dq, dk, dv_ref[...], dgate_ref[...]], axis=1)
        dh = _mm(dproj, wt_ref[...])
        dw_ref[...] += _mm_tn(dproj, h)
        dx, dg = _rms_bwd(dh, g, r, xhat)
        dg_ref[...] += dg
        dx_ref[...] = dxo_ref[...] + dx

    row = _rows(tm, D_MODEL)
    vec = _whole((1, D_MODEL))
    return pl.pallas_call(
        body, name="attn_proj_bwd", grid=(t // tm,),
        in_specs=[row, vec, row, _rows(tm, Q_DIM), _rows(tm, KV_DIM), _rows(tm, KV_DIM), _rows(tm, BRANCH),
                  _rows(tm, LANES), _rows(tm, LANES), _whole((ATTN_N, D_MODEL))],
        out_specs=[row, _whole((ATTN_N, D_MODEL)), vec],
        out_shape=[_sds((t, D_MODEL)), _sds((ATTN_N, D_MODEL)), _sds((1, D_MODEL))],
        compiler_params=_params(("arbitrary",)),
    )(x, norm, dxo, dq, dk, dv, dgate, cos2, sin2, w_in_t)


def attn_out_loss(x, o, gate, w_out, norm, target):
    t = x.shape[0]
    tm = min(ROW_TILE_FWD, t)

    def body(x_ref, o_ref, gate_ref, w_ref, g_ref, tgt_ref, loss_ref, dx_ref, dg_ref):
        @pl.when(pl.program_id(0) == 0)
        def _():
            loss_ref[...] = jnp.zeros_like(loss_ref)
            dg_ref[...] = jnp.zeros_like(dg_ref)

        out = x_ref[...] + _mm(o_ref[...] * _silu(gate_ref[...]), w_ref[...])
        g = g_ref[...]
        r, xhat, y = _rms(out, g)
        err = y - tgt_ref[...]
        loss_ref[...] += 0.5 * jnp.sum(jnp.mean(err * err, axis=-1, keepdims=True), axis=0, keepdims=True)
        dx, dg = _rms_bwd(err * (1.0 / D_MODEL), g, r, xhat)
        dg_ref[...] += dg
        dx_ref[...] = dx

    row = _rows(tm, D_MODEL)
    vec = _whole((1, D_MODEL))
    return pl.pallas_call(
        body, name="attn_out_loss", grid=(t // tm,),
        in_specs=[row, row, row, _whole((Q_DIM, D_MODEL)), vec, row], out_specs=[_whole((1, 1)), row, vec],
        out_shape=[_sds((1, 1)), _sds((t, D_MODEL)), _sds((1, D_MODEL))],
        compiler_params=_params(("arbitrary",)),
    )(x, o, gate, w_out, norm, target)


OCT_TILE = pl.BlockSpec((None, LANES, LANES), lambda b: (b, 0, 0))
N_LAGS = S5_CHUNK + 1


def _cmul(ar, ai, br, bi):
    return ar * br - ai * bi, ar * bi + ai * br


def _cmul_conj(ar, ai, br, bi):
    return ar * br + ai * bi, ar * bi - ai * br


def _mm_f32(a, b, dims):
    return lax.dot_general(a, b, (dims, ((), ())), precision=lax.Precision.HIGH, preferred_element_type=F32)


def _s5_discretise(ar, ai, ls, br, bi):
    dt = jnp.exp(ls)
    xr = ar * dt
    xi = ai * dt
    mag = jnp.exp(xr)
    first = (mag * jnp.cos(xi), mag * jnp.sin(xi))
    powers = [(jnp.ones_like(xr), jnp.zeros_like(xr)), first]
    for _ in range(2, N_LAGS):
        powers.append(_cmul(*powers[-1], *first))
    den = ar * ar + ai * ai
    nr = powers[1][0] - 1.0
    ni = powers[1][1]
    fr = (nr * ar + ni * ai) / den
    fi = (ni * ar - nr * ai) / den
    bbr, bbi = _cmul(fr, fi, br, bi)
    return dt, powers, (fr, fi), (bbr, bbi), den


def _same_group_tile():
    row = lax.broadcasted_iota(jnp.int32, (LANES, LANES), 0)
    col = lax.broadcasted_iota(jnp.int32, (LANES, LANES), 1)
    return (row // SSM_GROUP) == (col // SSM_GROUP)


def _first_copy_lanes():
    return lax.broadcasted_iota(jnp.int32, (LANES, LANES), 1) < SSM_STATE


def s5_param_fwd(tiles, token=None):
    extra, extra_specs = _after(token)

    def body(ar_ref, ai_ref, ls_ref, br_ref, bi_ref, cr_ref, ci_ref, *rest):
        kd_ref, wsr_ref, wsi_ref, wor_ref, woi_ref, pr_ref, pi_ref = rest[-7:]
        cr = cr_ref[...]
        ci = ci_ref[...]
        _, powers, _, (bbr, bbi), _ = _s5_discretise(ar_ref[...], ai_ref[...], ls_ref[...], br_ref[...], bi_ref[...])
        once = _first_copy_lanes()
        crm = jnp.where(once, cr, 0.0)
        cim = jnp.where(once, ci, 0.0)
        same = _same_group_tile()
        for lag in range(S5_CHUNK):
            er, ei = powers[lag]
            xr, xi = _cmul(er, ei, bbr, bbi)
            rows = pl.ds((S5_CHUNK - 1 - lag) * LANES, LANES)
            wsr_ref[rows, :] = xr
            wsi_ref[rows, :] = xi
        k = _mm_f32(wsr_ref[...], crm, ((1,), (1,))) - _mm_f32(wsi_ref[...], cim, ((1,), (1,)))
        for lag in range(S5_CHUNK):
            kd_ref[lag] = jnp.where(same, k[(S5_CHUNK - 1 - lag) * LANES:(S5_CHUNK - lag) * LANES], 0.0)
        for t in range(S5_CHUNK):
            er, ei = powers[t + 1]
            zr, zi = _cmul(er, ei, cr, ci)
            wor_ref[pl.ds(t * LANES, LANES), :] = zr
            woi_ref[pl.ds(t * LANES, LANES), :] = -zi
        pr_ref[...] = powers[S5_CHUNK][0]
        pi_ref[...] = powers[S5_CHUNK][1]

    return pl.pallas_call(
        body, name="s5_param_fwd", grid=(S5_OCTETS,),
        in_specs=[OCT_TILE] * 7 + [ANY] * len(extra),
        out_specs=[OCT_KD, OCT_W, OCT_W, OCT_W, OCT_W, OCT_TILE, OCT_TILE],
        out_shape=[_sds((S5_OCTETS, S5_CHUNK, LANES, LANES))] + [_sds((S5_OCTETS, S5_OCT_IN, LANES))] * 4
                  + [_sds((S5_OCTETS, LANES, LANES))] * 2,
        compiler_params=_params(("parallel",)),
    )(*tiles, *extra)


def s5_param_bwd(tiles, dkd, dws_re, dws_im, dwo_re, dwo_im, dp_re, dp_im):
    def body(ar_ref, ai_ref, ls_ref, br_ref, bi_ref, cr_ref, ci_ref, dkd_ref, dwsr_ref, dwsi_ref, dwor_ref, dwoi_ref, dpr_ref, dpi_ref,
             dar_ref, dai_ref, dls_ref, dbr_ref, dbi_ref, dcr_ref, dci_ref):
        ar = ar_ref[...]
        ai = ai_ref[...]
        br = br_ref[...]
        bi = bi_ref[...]
        cr = cr_ref[...]
        ci = ci_ref[...]
        dt, powers, (fr, fi), (bbr, bbi), den = _s5_discretise(ar, ai, ls_ref[...], br, bi)
        once = _first_copy_lanes()
        crm = jnp.where(once, cr, 0.0)
        cim = jnp.where(once, ci, 0.0)
        same = _same_group_tile()
        zero = jnp.zeros((LANES, LANES), F32)
        dpow = [[zero, zero] for _ in range(N_LAGS)]
        dbbr, dbbi = zero, zero
        by_step = [S5_CHUNK - 1 - s for s in range(S5_CHUNK)]
        x_all = [_cmul(*powers[lag], bbr, bbi) for lag in by_step]
        xr_all = jnp.concatenate([x[0] for x in x_all], axis=0)
        xi_all = jnp.concatenate([x[1] for x in x_all], axis=0)
        g_all = jnp.concatenate([jnp.where(same, dkd_ref[lag], 0.0) for lag in by_step], axis=0)
        dxr_all = dwsr_ref[...] + _mm_f32(g_all, crm, ((1,), (0,)))
        dxi_all = dwsi_ref[...] - _mm_f32(g_all, cim, ((1,), (0,)))
        dcr = jnp.where(once, _mm_f32(g_all, xr_all, ((0,), (0,))), 0.0)
        dci = -jnp.where(once, _mm_f32(g_all, xi_all, ((0,), (0,))), 0.0)
        for lag in range(S5_CHUNK):
            er, ei = powers[lag]
            rows = slice((S5_CHUNK - 1 - lag) * LANES, (S5_CHUNK - lag) * LANES)
            dxr = dxr_all[rows]
            dxi = dxi_all[rows]
            a, b = _cmul_conj(bbr, bbi, dxr, dxi)
            dpow[lag][0] = dpow[lag][0] + a
            dpow[lag][1] = dpow[lag][1] + b
            a, b = _cmul_conj(er, ei, dxr, dxi)
            dbbr = dbbr + a
            dbbi = dbbi + b
        for t in range(S5_CHUNK):
            er, ei = powers[t + 1]
            dzr = dwor_ref[pl.ds(t * LANES, LANES), :]
            dzi = -dwoi_ref[pl.ds(t * LANES, LANES), :]
            a, b = _cmul_conj(cr, ci, dzr, dzi)
            dpow[t + 1][0] = dpow[t + 1][0] + a
            dpow[t + 1][1] = dpow[t + 1][1] + b
            a, b = _cmul_conj(er, ei, dzr, dzi)
            dcr = dcr + a
            dci = dci + b
        dpow[S5_CHUNK][0] = dpow[S5_CHUNK][0] + dpr_ref[...]
        dpow[S5_CHUNK][1] = dpow[S5_CHUNK][1] + dpi_ref[...]
        dfr, dfi = _cmul_conj(br, bi, dbbr, dbbi)
        dbr, dbi = _cmul_conj(fr, fi, dbbr, dbbi)
        dnr, dni = _cmul(ar / den, ai / den, dfr, dfi)
        qr = (fr * ar + fi * ai) / den
        qi = (fi * ar - fr * ai) / den
        dlr, dli = _cmul(-qr, qi, dfr, dfi)
        dpow[1][0] = dpow[1][0] + dnr
        dpow[1][1] = dpow[1][1] + dni
        dxr, dxi = zero, zero
        for lag in range(1, N_LAGS):
            a, b = _cmul_conj(powers[lag][0], powers[lag][1], dpow[lag][0], dpow[lag][1])
            dxr = dxr + lag * a
            dxi = dxi + lag * b
        dar_ref[...] = dlr + dt * dxr
        dai_ref[...] = dli + dt * dxi
        dls_ref[...] = dt * (ar * dxr + ai * dxi)
        dbr_ref[...] = dbr
        dbi_ref[...] = dbi
        dcr_ref[...] = dcr
        dci_ref[...] = dci

    return pl.pallas_call(
        body, name="s5_param_bwd", grid=(S5_OCTETS,),
        in_specs=[OCT_TILE] * 7 + [OCT_KD, OCT_W, OCT_W, OCT_W, OCT_W, OCT_TILE, OCT_TILE], out_specs=[OCT_TILE] * 7,
        out_shape=[_sds((S5_OCTETS, LANES, LANES))] * 7,
        compiler_params=_params(("parallel",)),
    )(*tiles, dkd, dws_re, dws_im, dwo_re, dwo_im, dp_re, dp_im)


def _doubled(v):
    return jnp.concatenate([v, v], axis=-1)


def _s5_param_tiles(a_re, a_im, log_step, b_re, b_im, c_re, c_im):
    def per_group(a):
        return _doubled(jnp.broadcast_to(a.reshape(S5_OCTETS, S5_OCT, 1, SSM_STATE),
                                         (S5_OCTETS, S5_OCT, SSM_GROUP, SSM_STATE)).reshape(S5_OCTETS, LANES, SSM_STATE))

    ls = jnp.broadcast_to(log_step.reshape(S5_OCTETS, S5_OCT, 1, 1), (S5_OCTETS, S5_OCT, SSM_GROUP, LANES)).reshape(S5_OCTETS, LANES, LANES)
    bt = lambda b: _doubled(b.transpose(0, 2, 1).reshape(S5_OCTETS, LANES, SSM_STATE))
    ct = lambda c: _doubled(c.reshape(S5_OCTETS, LANES, SSM_STATE))
    return [per_group(a_re), per_group(a_im), ls, bt(b_re), bt(b_im), ct(c_re), ct(c_im)]


def _s5_param_grads(dtiles):
    dar, dai, dls, dbr, dbi, dcr, dci = dtiles
    halves = lambda d: d[..., :SSM_STATE] + d[..., SSM_STATE:]
    per_group = lambda d: halves(d).reshape(SSM_GROUPS, SSM_GROUP, SSM_STATE).sum(axis=1)
    per_row = lambda d: halves(d).reshape(SSM_GROUPS, SSM_GROUP, SSM_STATE)
    return (per_group(dar), per_group(dai), dls.reshape(SSM_GROUPS, SSM_GROUP * LANES).sum(axis=1),
            per_row(dbr).transpose(0, 2, 1), per_row(dbi).transpose(0, 2, 1), per_row(dcr), per_row(dci))


def _group_power_rows(tile):
    return tile[:, ::SSM_GROUP, :SSM_STATE].reshape(1, S5_STATES)


def _group_power_tiles(row):
    t = jnp.pad(row.reshape(S5_OCTETS, S5_OCT, 1, SSM_STATE), ((0, 0), (0, 0), (0, SSM_GROUP - 1), (0, LANES - SSM_STATE)))
    return t.reshape(S5_OCTETS, LANES, LANES)


def _rope_tables(t):
    pos = jnp.arange(t, dtype=F32)
    inv_freq = ROPE_THETA ** (-jnp.arange(0, HEAD_DIM, 2, dtype=F32) / HEAD_DIM)
    ang = pos[:, None] * inv_freq[None, :]
    cos = jnp.cos(ang)
    sin = jnp.sin(ang)
    cos64 = jnp.concatenate([cos, cos], axis=1)
    sin64 = jnp.concatenate([-sin, sin], axis=1)
    return jnp.concatenate([cos64, cos64], axis=1), jnp.concatenate([sin64, sin64], axis=1)


def _row(v):
    return v.reshape(1, -1)


def _s5_matrices(w, token=None):
    tiles = _s5_param_tiles(w["a_re"], w["a_im"], w["log_step"], w["b_re"], w["b_im"], w["c_re"], w["c_im"])
    kd, ws_re, ws_im, wo_re, wo_im, p_re, p_im = s5_param_fwd(tiles, token)
    return tiles, dict(kd=kd, ws_re=ws_re, ws_im=ws_im, wo_re=wo_re, wo_im=wo_im, a_re=_group_power_rows(p_re), a_im=_group_power_rows(p_im))


def _ssm_forward(x, w):
    tiles, mats = w["s5"] if "s5" in w else _s5_matrices(w)
    u, gate = ssm_proj_fwd(x, _row(w["norm"]), w["w_in"])
    s_re, s_im = s5_chunk_states(u, mats["ws_re"], mats["ws_im"])
    h_re, h_im = s5_scan_fwd(s_re, s_im, mats["a_re"], mats["a_im"])
    y_scan = s5_outputs(u, h_re, h_im, mats["kd"], mats["wo_re"], mats["wo_im"])
    y, g2, x_new = ssm_mix_fwd(x, u, gate, y_scan, _row(w["d"]), w["w_glu"], _row(w["b_glu"]), w["w_out"])
    saved = dict(x=x, u=u, gate=gate, y=y, g2=g2, h_re=h_re, h_im=h_im, mats=mats, tiles=tiles)
    return x_new, saved


def _ssm_backward(dxo, w, s, token=None):
    dy, dgate, dw_out, dw_glu, db_glu, dd = ssm_mix_bwd(dxo, s["u"], s["gate"], s["y"], s["g2"], w["w_glu"], w["w_out"], token)
    mats = s["mats"]
    dh_re, dh_im = s5_state_grads(dy, mats["wo_re"], mats["wo_im"])
    ds_re, ds_im, da_re, da_im = s5_scan_bwd(dh_re, dh_im, s["h_re"], s["h_im"], mats["a_re"], mats["a_im"])
    du_scan = s5_input_grads(dy, ds_re, ds_im, mats["kd"], mats["ws_re"], mats["ws_im"])
    dkd, dws_re, dws_im, dwo_re, dwo_im = s5_weight_grads(s["u"], dy, s["h_re"], s["h_im"], ds_re, ds_im)
    dparams = _s5_param_grads(s5_param_bwd(s["tiles"], dkd, dws_re, dws_im, dwo_re, dwo_im,
                                           _group_power_tiles(da_re), _group_power_tiles(da_im)))
    dx, dw_in, dnorm = ssm_proj_bwd(s["x"], _row(w["norm"]), dxo, dy, du_scan, dgate, _row(w["d"]), w["w_in"])
    grads = dict(norm=dnorm, w_in=dw_in, d=dd, w_glu=dw_glu, b_glu=db_glu, w_out=dw_out)
    for name, val in zip(("a_re", "a_im", "log_step", "b_re", "b_im", "c_re", "c_im"), dparams):
        grads[name] = val
    return dx, grads


def _attn_forward(x, w, cos2, sin2, loss_head=None):
    q, k, v, gate = attn_proj_fwd(x, _row(w["norm"]), w["w_in"], cos2, sin2)
    o, lse = attn_fwd(q, k, v, w["sinks"])
    if loss_head is None:
        result = attn_out_fwd(x, o, gate, w["w_out"])
    else:
        result = attn_out_loss(x, o, gate, w["w_out"], _row(loss_head[0]), loss_head[1])
    return result, dict(x=x, q=q, k=k, v=v, gate=gate, o=o, lse=lse)


def _attn_backward(dxo, w, s, cos2, sin2, token=None):
    do, dgate, dw_out = attn_out_bwd(dxo, s["o"], s["gate"], w["w_out"], token)
    dq, dk, dv, dsinks = attn_bwd(s["q"], s["k"], s["v"], w["sinks"], s["o"], s["lse"], do)
    dx, dw_in, dnorm = attn_proj_bwd(s["x"], _row(w["norm"]), dxo, dq, dk, dv, dgate, cos2, sin2, w["w_in"])
    return dx, dict(norm=dnorm, w_in=dw_in, sinks=dsinks[0, :N_Q_HEADS], w_out=dw_out)


class _NoExchanges:
    def __init__(self, layers):
        self.layers = layers

    def layer(self, i, x):
        return self.layers[i]

    def layer_done(self, i, grads, dx):
        return None


def _sequence_step(x, target, final_norm, hooks, depth=4):
    cos2, sin2 = _rope_tables(x.shape[0])
    saved, layers = [], []
    for i in range(depth):
        w = hooks.layer(i, x)
        layers.append(w)
        if i % 2 == 0:
            x, s = _ssm_forward(x, w)
        else:
            x, s = _attn_forward(x, w, cos2, sin2, (final_norm, target) if i == depth - 1 else None)
        saved.append(s)
    loss, dx, dfinal = x
    grads = {"final_norm": dfinal}
    token = None
    for i in reversed(range(depth)):
        if i % 2 == 0:
            dx, g = _ssm_backward(dx, layers[i], saved[i], token)
        else:
            dx, g = _attn_backward(dx, layers[i], saved[i], cos2, sin2, token)
        g = {"l%d_%s" % (i, name): val for name, val in g.items()}
        grads.update(g)
        token = hooks.layer_done(i, g, dx)
    return loss[0, 0], dx, grads


ANY = pl.BlockSpec(memory_space=pl.ANY)


def _place():
    return lax.axis_index("x"), lax.axis_index("y"), lax.axis_index("c")


def _other_chips(x, y):
    return [(1 - x, y), (x, 1 - y), (1 - x, 1 - y)]


class _StagedCopies:
    def __init__(self, bufs, load_sems, store_sems):
        self.bufs, self.load_sems, self.store_sems = bufs, load_sems, store_sems
        self.loads, self.stores = [], []

    def load(self, i, src):
        cp = pltpu.make_async_copy(src, self.bufs[i], self.load_sems.at[i])
        cp.start()
        self.loads.append(cp)

    def store(self, i, dst):
        self.loads[i].wait()
        cp = pltpu.make_async_copy(self.bufs[i], dst, self.store_sems.at[i])
        cp.start()
        self.stores.append(cp)

    def finish(self):
        for cp in self.stores:
            cp.wait()


def _staging(blocks):
    n = len(blocks)
    return [pltpu.VMEM(b.shape, b.dtype) for b in blocks] + [pltpu.SemaphoreType.DMA((n,)), pltpu.SemaphoreType.DMA((n,))]


def exchange_halves_with_sibling(grads):
    n = len(grads)

    def body(*refs):
        ins, outs = refs[:n], refs[n:2 * n]
        send_sems, recv_sems = refs[2 * n:]
        x, y, c = _place()
        copies = []
        for i in range(n):
            half = ins[i].shape[1] // 2
            src = ins[i].at[:, pl.ds((1 - c) * half, half), :]
            cp = pltpu.make_async_remote_copy(src_ref=src, dst_ref=outs[i], send_sem=send_sems.at[i], recv_sem=recv_sems.at[i],
                                              device_id=(x, y, 1 - c), device_id_type=MESH)
            cp.start()
            copies.append(cp)
        for cp in copies:
            cp.wait()

    return pl.pallas_call(
        body, name="exchange_halves_with_sibling",
        in_specs=[ANY] * n, out_specs=[ANY] * n,
        out_shape=[_sds((g.shape[0], g.shape[1] // 2, g.shape[2])) for g in grads],
        scratch_shapes=[pltpu.SemaphoreType.DMA((n,)), pltpu.SemaphoreType.DMA((n,))],
    )(*grads)


def swap_halves_with_sibling(pieces):
    n = len(pieces)

    def body(*refs):
        ins, outs = refs[:n], refs[n:2 * n]
        send_sems, recv_sems = refs[2 * n:2 * n + 2]
        own = _StagedCopies(refs[2 * n + 2:3 * n + 2], *refs[3 * n + 2:])
        x, y, c = _place()
        for i in range(n):
            own.load(i, ins[i])
        swaps = []
        for i in range(n):
            cp = pltpu.make_async_remote_copy(src_ref=ins[i], dst_ref=outs[i].at[c], send_sem=send_sems.at[i], recv_sem=recv_sems.at[i],
                                              device_id=(x, y, 1 - c), device_id_type=MESH)
            cp.start()
            swaps.append(cp)
        for i in range(n):
            own.store(i, outs[i].at[c])
        for i in range(n):
            pltpu.make_async_remote_copy(src_ref=ins[i], dst_ref=outs[i].at[1 - c], send_sem=send_sems.at[i], recv_sem=recv_sems.at[i],
                                         device_id=(x, y, 1 - c), device_id_type=MESH).wait_recv()
        for cp in swaps:
            cp.wait_send()
        own.finish()

    return pl.pallas_call(
        body, name="swap_halves_with_sibling",
        in_specs=[ANY] * n, out_specs=[ANY] * n,
        out_shape=[_sds((2,) + p.shape) for p in pieces],
        scratch_shapes=[pltpu.SemaphoreType.DMA((n,)), pltpu.SemaphoreType.DMA((n,))] + _staging(pieces),
        compiler_params=_params(),
    )(*pieces)


def pass_halves_to_sibling(stacks):
    n = len(stacks)

    def body(*refs):
        outs = refs[n:2 * n]
        send_sems, recv_sems = refs[2 * n:]
        x, y, c = _place()
        sends = []
        for i in range(n):
            for k, (tx, ty) in enumerate(_other_chips(x, y)):
                mine = _rows_of_core(outs[i].at[2 * tx + ty], c, True)
                cp = pltpu.make_async_remote_copy(src_ref=mine, dst_ref=mine, send_sem=send_sems.at[i, k], recv_sem=recv_sems.at[i, k],
                                                  device_id=(x, y, 1 - c), device_id_type=MESH)
                cp.start()
                sends.append(cp)
        for i in range(n):
            for k, (tx, ty) in enumerate(_other_chips(x, y)):
                missing = _rows_of_core(outs[i].at[2 * tx + ty], 1 - c, True)
                pltpu.make_async_remote_copy(src_ref=missing, dst_ref=missing, send_sem=send_sems.at[i, k], recv_sem=recv_sems.at[i, k],
                                             device_id=(x, y, 1 - c), device_id_type=MESH).wait_recv()
        for cp in sends:
            cp.wait_send()

    sems = pltpu.SemaphoreType.DMA((n, 3))
    return pl.pallas_call(
        body, name="pass_halves_to_sibling", in_specs=[ANY] * n, out_specs=[ANY] * n,
        out_shape=[_sds(s.shape, s.dtype) for s in stacks], input_output_aliases={i: i for i in range(n)},
        scratch_shapes=[sems, sems],
    )(*stacks)


IN_HBM = pl.BlockSpec(memory_space=pltpu.HBM)
SEMAPHORES = pl.BlockSpec(memory_space=pltpu.SEMAPHORE)
DATAFLOW = pltpu.SideEffectType.DATAFLOW_SIDE_EFFECTING


def _hbm(a):
    return pltpu.with_memory_space_constraint(a, pltpu.HBM)


def place_own_blocks(shards):
    n = len(shards)

    def body(*refs):
        ins, outs = refs[:n], refs[n:2 * n]
        own = _StagedCopies(refs[2 * n:3 * n], *refs[3 * n:])
        x, y, _ = _place()
        for i in range(n):
            own.load(i, ins[i])
        for i in range(n):
            own.store(i, outs[i].at[2 * x + y])
        own.finish()

    return pl.pallas_call(
        body, name="place_own_blocks", in_specs=[ANY] * n, out_specs=[ANY] * n,
        out_shape=[_sds((4,) + s.shape, s.dtype) for s in shards],
        scratch_shapes=_staging(shards), compiler_params=_params(),
    )(*shards)


def _block_to_send(ref, chip, per_target):
    if not per_target:
        return ref
    return ref.at[chip] if ref.shape[0] == 4 else ref.at[0]


def _rows_of_core(ref, c, core_half):
    if not core_half:
        return ref
    rows = ref.shape[0] // 2
    return ref.at[pl.ds(c * rows, rows), :]


def start_sends_to_chips(name, sources, landings, per_target, after, core_half=False):
    n = len(sources)
    n_sems = 2 * 3 * n

    def body(*refs):
        srcs = refs[:n]
        sems = refs[2 * n + 1:2 * n + 1 + n_sems]
        lands = refs[2 * n + 1 + n_sems:3 * n + 1 + n_sems]
        token = refs[3 * n + 1 + n_sems]
        x, y, c = _place()
        me = 2 * x + y
        for i in range(n):
            for k, (tx, ty) in enumerate(_other_chips(x, y)):
                src = _rows_of_core(_block_to_send(srcs[i], 2 * tx + ty, per_target), c, core_half)
                dst = _rows_of_core(lands[i].at[me], c, core_half)
                pltpu.make_async_remote_copy(src_ref=src, dst_ref=dst, send_sem=sems[2 * (3 * i + k)], recv_sem=sems[2 * (3 * i + k) + 1],
                                             device_id=(tx, ty, c), device_id_type=MESH).start()
        token[...] = jnp.zeros_like(token)

    outs = pl.pallas_call(
        body, name=name,
        in_specs=[IN_HBM] * (2 * n) + [ANY],
        out_specs=[SEMAPHORES] * n_sems + [IN_HBM] * n + [pl.BlockSpec(memory_space=pltpu.VMEM)],
        out_shape=[pltpu.SemaphoreType.DMA(())] * n_sems + [pltpu.HBM(l.shape, l.dtype) for l in landings] + [_sds(TOKEN_SHAPE)],
        input_output_aliases={n + i: n_sems + i for i in range(n)},
        compiler_params=pltpu.CompilerParams(has_side_effects=DATAFLOW),
    )(*[_hbm(s) for s in sources], *[_hbm(l) for l in landings], after)
    return list(outs[:n_sems]), list(outs[n_sems:n_sems + n]), outs[n_sems + n]


def wait_sends_to_chips(name, sources, landings, sems, per_target, after, core_half=False):
    n = len(sources)
    n_sems = len(sems)

    def body(*refs):
        srcs = refs[:n]
        sem_refs = refs[2 * n:2 * n + n_sems]
        lands = refs[2 * n + n_sems + 1:]
        x, y, c = _place()
        me = 2 * x + y
        for i in range(n):
            for k, (tx, ty) in enumerate(_other_chips(x, y)):
                src = _rows_of_core(_block_to_send(srcs[i], me, per_target), c, core_half)
                dst = _rows_of_core(lands[i].at[2 * tx + ty], c, core_half)
                cp = pltpu.make_async_remote_copy(src_ref=src, dst_ref=dst, send_sem=sem_refs[2 * (3 * i + k)],
                                                  recv_sem=sem_refs[2 * (3 * i + k) + 1], device_id=(tx, ty, c), device_id_type=MESH)
                cp.wait_send()
                cp.wait_recv()

    return pl.pallas_call(
        body, name=name,
        in_specs=[IN_HBM] * (2 * n) + [SEMAPHORES] * n_sems + [ANY],
        out_specs=[IN_HBM] * n,
        out_shape=[pltpu.HBM(l.shape, l.dtype) for l in landings],
        input_output_aliases={n + i: i for i in range(n)},
        compiler_params=pltpu.CompilerParams(has_side_effects=DATAFLOW),
    )(*[_hbm(s) for s in sources], *landings, *sems, after)


def _row_tile(rows, cols):
    tm = rows
    while tm * cols * 4 > (2 << 20) and tm % 16 == 0:
        tm //= 2
    return tm


def add_pairs(half, a_list, b_list, out_dtypes, copies=1):
    n = len(a_list)
    nb = a_list[0].shape[0]

    def body(half_ref, *refs):
        for i in range(n):
            total = (refs[i][...] + refs[n + i][...]).astype(out_dtypes[i])
            for o_ref in refs[2 * n + i * copies:2 * n + (i + 1) * copies]:
                o_ref[...] = total

    halves = [pl.BlockSpec((None,) + b.shape[1:], lambda j, h: (j, h[0], 0)) for b in b_list]
    whole = [pl.BlockSpec((None,) + b.shape[1:], lambda j, h: (j, 0, 0)) for b in b_list]
    outs = pl.pallas_call(
        body, name="add_pairs",
        grid_spec=pltpu.PrefetchScalarGridSpec(num_scalar_prefetch=1, grid=(nb,), in_specs=halves + whole,
                                               out_specs=[s for s in whole for _ in range(copies)]),
        out_shape=[_sds(b.shape, dt) for b, dt in zip(b_list, out_dtypes) for _ in range(copies)],
        compiler_params=_params(("parallel",)),
    )(half, *a_list, *b_list)
    return [tuple(outs[i * copies:(i + 1) * copies]) for i in range(n)]


def sum_fours(arrays, token=None):
    n = len(arrays)
    extra, extra_specs = _after(token)
    steps = 2 if all(a.shape[1] % 32 == 0 for a in arrays) else 1

    def body(*refs):
        outs = refs[-n:]
        for a_ref, o_ref in zip(refs[:n], outs):
            o_ref[...] = ((a_ref[0].astype(F32) + a_ref[1].astype(F32)) + a_ref[2].astype(F32)) + a_ref[3].astype(F32)

    return pl.pallas_call(
        body, name="sum_fours", grid=(steps,),
        in_specs=[pl.BlockSpec((4, a.shape[1] // steps, a.shape[2]), lambda i: (0, i, 0)) for a in arrays] + extra_specs,
        out_specs=[pl.BlockSpec((a.shape[1] // steps, a.shape[2]), lambda i: (i, 0)) for a in arrays],
        out_shape=[_sds(a.shape[1:]) for a in arrays], compiler_params=_params(("parallel",)),
    )(*arrays, *extra)


def _adamw_update(w_ref, g_ref, m_ref, v_ref, d_ref, nm_ref, nv_ref):
    g = g_ref[...]
    nm = ADAM_B1 * m_ref[...] + (1.0 - ADAM_B1) * g
    nv = ADAM_B2 * v_ref[...] + (1.0 - ADAM_B2) * (g * g)
    d_ref[...] = -ADAM_LR * ((nm / (1.0 - ADAM_B1 ** ADAM_STEP)) / (jnp.sqrt(nv / (1.0 - ADAM_B2 ** ADAM_STEP)) + ADAM_EPS) + ADAM_WD * w_ref[...])
    nm_ref[...] = nm
    nv_ref[...] = nv


def adamw(w, g, m, v):
    rows, cols = w.shape
    tm = _row_tile(rows, cols)

    def body(*refs):
        _adamw_update(*refs)

    spec = pl.BlockSpec((tm, cols), lambda i: (i, 0))
    return pl.pallas_call(
        body, name="adamw", grid=(rows // tm,), in_specs=[spec] * 4, out_specs=[spec] * 3,
        out_shape=[_sds(w.shape)] * 3, compiler_params=_params(("parallel",)),
    )(w, g, m, v)


def adamw_small(ws, gs, ms, vs, slabs=None):
    n = len(ws)

    def body(*refs):
        for i in range(n):
            _adamw_update(refs[i], refs[n + i], refs[2 * n + i], refs[3 * n + i], refs[4 * n + i], refs[5 * n + i], refs[6 * n + i])

    if slabs is None:
        grid = ()
        specs = [pl.BlockSpec(memory_space=pltpu.VMEM)] * n
    else:
        grid = (slabs,)
        specs = [pl.BlockSpec((w.shape[0] // slabs,) + w.shape[1:], lambda i: (i, 0, 0)) for w in ws]
    outs = pl.pallas_call(
        body, name="adamw_small", grid=grid, in_specs=specs * 4, out_specs=specs * 3,
        out_shape=[_sds(w.shape) for w in ws] * 3, compiler_params=_params(("parallel",) if slabs else None),
    )(*ws, *gs, *ms, *vs)
    return outs[:n], outs[n:2 * n], outs[2 * n:]


PACK_TILE = 8 * LANES
PACK_PIECES = 8
PACK_ALIGN = PACK_PIECES * 16


def _pack_small(values, scalar=None):
    parts = []
    for name in PACK_NAMES:
        flat = values[name].reshape(-1)
        pad = (-flat.shape[0]) % PACK_TILE
        if pad:
            flat = jnp.concatenate([flat, jnp.zeros((pad,), F32)])
        parts.append(flat.reshape(-1, LANES))
    rows = sum(p.shape[0] for p in parts) + 8
    parts.append(jnp.zeros(((-rows) % PACK_ALIGN, LANES), F32))
    last = jnp.zeros((8, LANES), F32)
    parts.append(last if scalar is None else jnp.broadcast_to(scalar.astype(F32), (8, LANES)))
    return jnp.concatenate(parts, axis=0)


def _pack_row_of(name, like):
    row = 0
    for other in PACK_NAMES:
        if other == name:
            return row
        row += -(-math.prod(like[other].shape) // PACK_TILE) * 8
    raise KeyError(name)


def _unpack_small(pack, like):
    out = {}
    row = 0
    for name in PACK_NAMES:
        size = math.prod(like[name].shape)
        rows = -(-size // PACK_TILE) * 8
        out[name] = pack[row:row + rows].reshape(-1)[:size].reshape(like[name].shape)
        row += rows
    return out


def _travels_transposed(name, shard):
    return name.endswith("w_in") and shard.shape[-1] % LANES != 0


def _to_blocks(name, full):
    if full.ndim == 3:
        return full
    return full.reshape(4, full.shape[0] // 4, full.shape[1])


def _from_blocks(name, stacked):
    if name.endswith("w_in") and stacked.shape[2] % LANES == 0 and stacked.shape[1] == D_MODEL:
        return stacked
    return stacked.reshape(4 * stacked.shape[1], stacked.shape[2])


def _layer_big_names(i):
    return [n for n in BIG_NAMES if n.startswith("l%d_" % i)]


class _OverlappedExchanges:
    def __init__(self, weights):
        self.weights = weights
        self.c = lax.axis_index("c")
        self.first = _layer_big_names(0)
        self.later = [n for n in BIG_NAMES if n not in self.first]
        shards = [weights[n].astype(MXU_DTYPE) for n in self.first + self.later]
        shards = [s.T if _travels_transposed(n, s) else s for n, s in zip(self.first + self.later, shards)]
        placed = place_own_blocks(shards)
        k = len(self.first)
        sems, stacks, token = start_sends_to_chips("gather_first_start", shards[:k], placed[:k], False, shards[0], core_half=True)
        self.gather_first = (shards[:k], sems, stacks)
        sems, stacks, token = start_sends_to_chips("gather_later_start", shards[k:], placed[k:], False, token)
        self.gather_later = (shards[k:], sems, stacks)
        self.s5 = {}
        for i in (0, 2):
            self.s5[i] = _s5_matrices({n: weights["l%d_%s" % (i, n)] for n in SSM_NAMES if "l%d_%s" % (i, n) in SMALL_NAMES}, token)
            token = self.s5[i][1]["kd"]
        self.full = {}
        self.in_flight = {}
        self.contributions = {}

    def layer(self, i, x):
        if i == 0:
            shards, sems, stacks = self.gather_first
            stacks = wait_sends_to_chips("gather_first_wait", shards, stacks, sems, False, self.s5[2][1]["kd"], core_half=True)
            stacks = pass_halves_to_sibling(stacks)
            self.full.update({n: _from_blocks(n, g) for n, g in zip(self.first, stacks)})
        if i == 1:
            shards, sems, stacks = self.gather_later
            stacks = wait_sends_to_chips("gather_later_wait", shards, stacks, sems, False, x)
            self.full.update({n: _from_blocks(n, g) for n, g in zip(self.later, stacks)})
        names = SSM_NAMES if i % 2 == 0 else ATTN_NAMES
        w = {n: self.full.get("l%d_%s" % (i, n), self.weights.get("l%d_%s" % (i, n))) for n in names}
        if i in self.s5:
            w["s5"] = self.s5[i]
        return w

    def chip_sums(self, names, grads, extra_blocks=(), extra_dtypes=(), copies=1):
        blocks = [_to_blocks(n, grads[n]) for n in names] + list(extra_blocks)
        from_sibling = exchange_halves_with_sibling(blocks)
        k = len(names)
        half = self.c.reshape(1).astype(jnp.int32)
        sums = add_pairs(half, blocks[:k], from_sibling[:k], [WIRE_DTYPE] * k, copies)
        if extra_blocks:
            sums += add_pairs(half, blocks[k:], from_sibling[k:], list(extra_dtypes), copies)
        return sums

    def layer_done(self, i, grads, dx):
        if i + 1 in self.in_flight:
            names, sums, sems, landings = self.in_flight.pop(i + 1)
            done = wait_sends_to_chips("scatter_wait_l%d" % (i + 1), sums, landings, sems, True, dx)
            self.contributions.update(zip(names, done))
        if i == 0:
            return None
        names = _layer_big_names(i)
        pairs = self.chip_sums(names, grads, copies=2)
        sums = [p[0] for p in pairs]
        sems, landings, token = start_sends_to_chips("scatter_start_l%d" % i, sums, [p[1] for p in pairs], True, sums[0])
        self.in_flight[i] = (names, sums, sems, landings)
        return token


def _train_step(x, loss_target, weights, moments_m, moments_v):
    hooks = _OverlappedExchanges(weights)
    loss, dx, grads = _sequence_step(x[0], loss_target[0], weights["final_norm"], hooks)
    small_pack = _pack_small({n: grads[n] for n in SMALL_NAMES}, scalar=loss)
    tail = small_pack.shape[0] - _pack_row_of(EXACT_NAMES[0], grads)
    small = [small_pack[None, :-tail], small_pack[None, -tail:]]
    last = _layer_big_names(0)
    pairs = hooks.chip_sums(last, grads, extra_blocks=small, extra_dtypes=[WIRE_DTYPE, F32], copies=2)
    sums = [p[0] for p in pairs]
    landings = [p[1] for p in pairs[:-2]] + [jnp.broadcast_to(s, (4,) + s.shape[1:]) for s in sums[-2:]]
    sems, landings, token = start_sends_to_chips("scatter_start_l0", sums, landings, True, sums[0])
    out_grad, out_delta, out_m, out_v = {}, {}, {}, {}

    def finish(names, arrays, token=None):
        shared = swap_halves_with_sibling(sum_fours(arrays, token))
        rest = []
        for n, s in zip(names, shared):
            if n not in weights:
                rest.append(s.reshape(-1, LANES))
                continue
            out_grad[n] = s.reshape(2 * s.shape[1], s.shape[2])
            if _travels_transposed(n, weights[n]):
                out_grad[n] = out_grad[n].T
            out_delta[n], out_m[n], out_v[n] = adamw(weights[n], out_grad[n], moments_m[n], moments_v[n])
        return rest

    others = [n for n in BIG_NAMES if n not in last]
    finish(others, [hooks.contributions[n] for n in others], token)
    arrived = wait_sends_to_chips("scatter_wait_l0", sums, landings, sems, True, out_v[others[-1]])
    small_grad_pack = jnp.concatenate(finish(last + ["small", "small tail"], arrived), axis=0)
    loss = small_grad_pack[-8, 0]
    out_grad.update(_unpack_small(small_grad_pack, {n: weights[n] for n in SMALL_NAMES}))
    cubes = [n for n in SMALL_NAMES if weights[n].ndim == 3]
    for names, slabs in ((cubes, 8), ([n for n in SMALL_NAMES if n not in cubes], None)):
        deltas, new_ms, new_vs = adamw_small(*[[group[n] for n in names] for group in (weights, out_grad, moments_m, moments_v)], slabs=slabs)
        out_delta.update(zip(names, deltas))
        out_m.update(zip(names, new_ms))
        out_v.update(zip(names, new_vs))
    outs = [loss, dx[None]]
    for group in (out_grad, out_delta, out_m, out_v):
        outs.extend(group[n] for n in WEIGHT_NAMES)
    return tuple(outs)


def kernel(x, l0_norm, l0_w_in, l0_a_re, l0_a_im, l0_log_step, l0_b_re, l0_b_im, l0_c_re, l0_c_im, l0_d, l0_w_glu, l0_b_glu, l0_w_out, l1_norm, l1_w_in, l1_sinks, l1_w_out, l2_norm, l2_w_in, l2_a_re, l2_a_im, l2_log_step, l2_b_re, l2_b_im, l2_c_re, l2_c_im, l2_d, l2_w_glu, l2_b_glu, l2_w_out, l3_norm, l3_w_in, l3_sinks, l3_w_out, final_norm, loss_target, m_l0_norm, m_l0_w_in, m_l0_a_re, m_l0_a_im, m_l0_log_step, m_l0_b_re, m_l0_b_im, m_l0_c_re, m_l0_c_im, m_l0_d, m_l0_w_glu, m_l0_b_glu, m_l0_w_out, m_l1_norm, m_l1_w_in, m_l1_sinks, m_l1_w_out, m_l2_norm, m_l2_w_in, m_l2_a_re, m_l2_a_im, m_l2_log_step, m_l2_b_re, m_l2_b_im, m_l2_c_re, m_l2_c_im, m_l2_d, m_l2_w_glu, m_l2_b_glu, m_l2_w_out, m_l3_norm, m_l3_w_in, m_l3_sinks, m_l3_w_out, m_final_norm, v_l0_norm, v_l0_w_in, v_l0_a_re, v_l0_a_im, v_l0_log_step, v_l0_b_re, v_l0_b_im, v_l0_c_re, v_l0_c_im, v_l0_d, v_l0_w_glu, v_l0_b_glu, v_l0_w_out, v_l1_norm, v_l1_w_in, v_l1_sinks, v_l1_w_out, v_l2_norm, v_l2_w_in, v_l2_a_re, v_l2_a_im, v_l2_log_step, v_l2_b_re, v_l2_b_im, v_l2_c_re, v_l2_c_im, v_l2_d, v_l2_w_glu, v_l2_b_glu, v_l2_w_out, v_l3_norm, v_l3_w_in, v_l3_sinks, v_l3_w_out, v_final_norm):
    args = locals()
    weights = {n: args[n] for n in WEIGHT_NAMES}
    moments_m = {n: args["m_" + n] for n in WEIGHT_NAMES}
    moments_v = {n: args["v_" + n] for n in WEIGHT_NAMES}
    return _train_step(x, loss_target, weights, moments_m, moments_v)
```

```python
import functools
import math

import jax
import jax.numpy as jnp
from jax import lax
from jax.experimental import pallas as pl
from jax.experimental.pallas import tpu as pltpu

F32 = jnp.float32
MXU_DTYPE = jnp.bfloat16
WIRE_DTYPE = jnp.bfloat16
MESH = pl.DeviceIdType.MESH

D_MODEL = 1024
BRANCH = 1024
NORM_EPS = 1e-5
SSM_GROUPS = 64
SSM_GROUP = 16
SSM_STATE = 64
S5_CHUNK = 16
LANES = 128
S5_OCT = LANES // SSM_GROUP
S5_OCTETS = SSM_GROUPS // S5_OCT
S5_OCT_IN = S5_CHUNK * LANES
S5_OCT_STATE = S5_OCT * SSM_STATE
S5_STATES = SSM_GROUPS * SSM_STATE
HEAD_DIM = 64
N_Q_HEADS = 16
N_KV_HEADS = 2
GQA_GROUP = N_Q_HEADS // N_KV_HEADS
ATTN_BLOCK = 128
Q_DIM = N_Q_HEADS * HEAD_DIM
KV_DIM = N_KV_HEADS * HEAD_DIM
ROPE_THETA = 10000.0
NEG_INF = -1e30
ADAM_LR = 0.001
ADAM_B1 = 0.9
ADAM_B2 = 0.999
ADAM_EPS = 1e-08
ADAM_WD = 0.01
ADAM_STEP = 10

VMEM_LIMIT_V7X = 56 * 1024 * 1024
ROW_TILE_FWD = 512
ROW_TILE_BWD = 512

SSM_NAMES = ("norm", "w_in", "a_re", "a_im", "log_step", "b_re", "b_im", "c_re", "c_im", "d", "w_glu", "b_glu", "w_out")
ATTN_NAMES = ("norm", "w_in", "sinks", "w_out")


def _weight_names():
    names = []
    for i in range(4):
        for n in (SSM_NAMES if i % 2 == 0 else ATTN_NAMES):
            names.append("l%d_%s" % (i, n))
    names.append("final_norm")
    return names


WEIGHT_NAMES = _weight_names()
BIG_NAMES = [n for n in WEIGHT_NAMES if n.endswith(("w_in", "w_glu", "w_out"))]
SMALL_NAMES = [n for n in WEIGHT_NAMES if n not in BIG_NAMES]
EXACT_NAMES = [n for n in SMALL_NAMES if n.endswith(("log_step", "sinks")) or n == "final_norm"]
PACK_NAMES = [n for n in SMALL_NAMES if n not in EXACT_NAMES] + EXACT_NAMES


def _params(semantics=None):
    return pltpu.CompilerParams(dimension_semantics=semantics, vmem_limit_bytes=VMEM_LIMIT_V7X)


def _rows(tm, n):
    return pl.BlockSpec((tm, n), lambda i: (i, 0))


def _whole(shape):
    return pl.BlockSpec(shape, lambda i: (0,) * len(shape), pipeline_mode=pl.Buffered(1))


def _sds(shape, dtype=F32):
    return jax.ShapeDtypeStruct(shape, dtype)


def _mm(a, b):
    return jnp.dot(a.astype(MXU_DTYPE), b.astype(MXU_DTYPE), preferred_element_type=F32)


def _mm_tn(a, b):
    return lax.dot_general(a.astype(MXU_DTYPE), b.astype(MXU_DTYPE), (((0,), (0,)), ((), ())), preferred_element_type=F32)


def _mm_nt(a, b):
    return lax.dot_general(a.astype(MXU_DTYPE), b.astype(MXU_DTYPE), (((1,), (1,)), ((), ())), preferred_element_type=F32)


def _sigmoid(x):
    return 0.5 + 0.5 * jnp.tanh(0.5 * x)


def _silu(x):
    return x * _sigmoid(x)


def _silu_and_grad(x):
    s = _sigmoid(x)
    return x * s, s * (1.0 + x * (1.0 - s))


GELU_C0 = math.sqrt(2.0 / math.pi)
GELU_C1 = 0.044715


def _gelu(x):
    return 0.5 * x * (1.0 + jnp.tanh(GELU_C0 * (x + GELU_C1 * x * x * x)))


def _gelu_and_grad(x):
    x2 = x * x
    th = jnp.tanh(GELU_C0 * x * (1.0 + GELU_C1 * x2))
    half = 0.5 + 0.5 * th
    return x * half, half + 0.5 * x * (1.0 - th * th) * (GELU_C0 + 3.0 * GELU_C0 * GELU_C1 * x2)


def _rms(x, g):
    r = lax.rsqrt(jnp.mean(x * x, axis=-1, keepdims=True) + NORM_EPS)
    xhat = x * r
    return r, xhat, xhat * g


def _rms_bwd(dh, g, r, xhat):
    dxhat = dh * g
    dx = r * (dxhat - xhat * jnp.mean(dxhat * xhat, axis=-1, keepdims=True))
    return dx, jnp.sum(dh * xhat, axis=0, keepdims=True)


def _swap_half_heads(x):
    n = x.shape[-1]
    lane = lax.broadcasted_iota(jnp.int32, x.shape, x.ndim - 1)
    first = (lane % HEAD_DIM) < (HEAD_DIM // 2)
    return jnp.where(first, pltpu.roll(x, n - HEAD_DIM // 2, x.ndim - 1), pltpu.roll(x, HEAD_DIM // 2, x.ndim - 1))


def _tile_lanes(t, reps):
    return jnp.concatenate([t] * reps, axis=1)


TOKEN_SHAPE = (8, LANES)


def _after(token):
    return ([], []) if token is None else ([token], [_whole(TOKEN_SHAPE)])


def ssm_proj_fwd(x, norm, w_in):
    t = x.shape[0]
    tm = min(ROW_TILE_FWD, t)

    def body(x_ref, g_ref, w_ref, u_ref, gate_ref):
        _, _, h = _rms(x_ref[...], g_ref[...])
        h = h.astype(MXU_DTYPE)
        half = BRANCH // 2
        for j in range(2):
            u_ref[:, j * half:(j + 1) * half] = _mm(h, w_ref[j])
            gate_ref[:, j * half:(j + 1) * half] = _mm(h, w_ref[2 + j])

    return pl.pallas_call(
        body, name="ssm_proj_fwd", grid=(t // tm,),
        in_specs=[_rows(tm, D_MODEL), _whole((1, D_MODEL)), _whole((4, D_MODEL, BRANCH // 2))],
        out_specs=[_rows(tm, BRANCH), _rows(tm, BRANCH)],
        out_shape=[_sds((t, BRANCH)), _sds((t, BRANCH))],
        compiler_params=_params(("parallel",)),
    )(x, norm, w_in)


def _chunk_rows(ref, nk, dtype=None):
    rows = jnp.concatenate([ref[pl.ds(s, nk, stride=S5_CHUNK), :] for s in range(S5_CHUNK)], axis=1)
    return rows.astype(MXU_DTYPE if dtype is None else dtype)


def _store_chunk_rows(ref, val, nk):
    for s in range(S5_CHUNK):
        ref[pl.ds(s, nk, stride=S5_CHUNK), :] = val[:, s * LANES:(s + 1) * LANES]


def _own_group_mask():
    row = lax.broadcasted_iota(jnp.int32, (S5_OCT_IN, S5_OCT_STATE), 0)
    col = lax.broadcasted_iota(jnp.int32, (S5_OCT_IN, S5_OCT_STATE), 1)
    return ((row % LANES) // SSM_GROUP) == (col // SSM_STATE)


def _spread_groups(w):
    return jnp.where(_own_group_mask(), jnp.concatenate([w] * (S5_OCT_STATE // LANES), axis=1), 0.0).astype(MXU_DTYPE)


def _fold_groups(p):
    p = jnp.where(_own_group_mask(), p, 0.0)
    return sum(p[:, q * LANES:(q + 1) * LANES] for q in range(S5_OCT_STATE // LANES))


def _fill_toeplitz(win_ref, kd_ref):
    win_ref[...] = jnp.zeros_like(win_ref)
    for s in range(S5_CHUNK):
        for t in range(s, S5_CHUNK):
            win_ref[s * LANES:(s + 1) * LANES, t * LANES:(t + 1) * LANES] = kd_ref[t - s].astype(MXU_DTYPE)


TOEPLITZ_BLOCK = 512
_TOEPLITZ_BLOCKS = [(lo, lo + TOEPLITZ_BLOCK) for lo in range(0, S5_OCT_IN, TOEPLITZ_BLOCK)]


def _strip(t):
    return pl.BlockSpec((t, LANES), lambda b: (0, b))


def _oct_states(nk):
    return pl.BlockSpec((nk, S5_OCT_STATE), lambda b: (0, b))


OCT_W = pl.BlockSpec((None, S5_OCT_IN, LANES), lambda b: (b, 0, 0))
OCT_KD = pl.BlockSpec((None, S5_CHUNK, LANES, LANES), lambda b: (b, 0, 0, 0))


def s5_chunk_states(u, ws_re, ws_im):
    t = u.shape[0]
    nk = t // S5_CHUNK

    def body(u_ref, wr_ref, wi_ref, re_ref, im_ref):
        uc = _chunk_rows(u_ref, nk)
        re_ref[...] = _mm(uc, _spread_groups(wr_ref[...]))
        im_ref[...] = _mm(uc, _spread_groups(wi_ref[...]))

    return pl.pallas_call(
        body, name="s5_chunk_states", grid=(S5_OCTETS,),
        in_specs=[_strip(t), OCT_W, OCT_W], out_specs=[_oct_states(nk), _oct_states(nk)],
        out_shape=[_sds((nk, S5_STATES)), _sds((nk, S5_STATES))],
        compiler_params=_params(("parallel",)),
    )(u, ws_re, ws_im)


def s5_scan_fwd(s_re, s_im, a_re, a_im):
    nk = s_re.shape[0]

    def body(sre_ref, sim_ref, ar_ref, ai_ref, hre_ref, him_ref):
        ar = ar_ref[...]
        ai = ai_ref[...]

        def step(k, carry):
            hr, hi = carry
            hre_ref[pl.ds(k, 1), :] = hr
            him_ref[pl.ds(k, 1), :] = hi
            sr = sre_ref[pl.ds(k, 1), :]
            si = sim_ref[pl.ds(k, 1), :]
            return ar * hr - ai * hi + sr, ai * hr + ar * hi + si

        zero = jnp.zeros((1, S5_STATES), F32)
        lax.fori_loop(0, nk, step, (zero, zero))

    vm = pl.BlockSpec(memory_space=pltpu.VMEM)
    return pl.pallas_call(
        body, name="s5_scan_fwd", in_specs=[vm, vm, vm, vm], out_specs=[vm, vm],
        out_shape=[_sds((nk, S5_STATES)), _sds((nk, S5_STATES))],
        compiler_params=_params(),
    )(s_re, s_im, a_re, a_im)


def s5_outputs(u, h_re, h_im, kd, wo_re, wo_im):
    t = u.shape[0]
    nk = t // S5_CHUNK

    def body(u_ref, hre_ref, him_ref, kd_ref, wor_ref, woi_ref, y_ref, win_ref):
        _fill_toeplitz(win_ref, kd_ref)
        uc = _chunk_rows(u_ref, nk)
        y = jnp.concatenate([_mm(uc[:, :hi], win_ref[:hi, lo:hi]) for lo, hi in _TOEPLITZ_BLOCKS], axis=1)
        y = y + _mm_nt(hre_ref[...], _spread_groups(wor_ref[...])) + _mm_nt(him_ref[...], _spread_groups(woi_ref[...]))
        _store_chunk_rows(y_ref, y, nk)

    return pl.pallas_call(
        body, name="s5_outputs", grid=(S5_OCTETS,),
        in_specs=[_strip(t), _oct_states(nk), _oct_states(nk), OCT_KD, OCT_W, OCT_W],
        out_specs=_strip(t), out_shape=_sds((t, BRANCH)),
        scratch_shapes=[pltpu.VMEM((S5_OCT_IN, S5_OCT_IN), MXU_DTYPE)],
        compiler_params=_params(("parallel",)),
    )(u, h_re, h_im, kd, wo_re, wo_im)


def s5_state_grads(dy, wo_re, wo_im, token=None):
    t = dy.shape[0]
    nk = t // S5_CHUNK
    extra, extra_specs = _after(token)

    def body(dy_ref, wor_ref, woi_ref, *rest):
        re_ref, im_ref = rest[-2:]
        dyc = _chunk_rows(dy_ref, nk)
        re_ref[...] = _mm(dyc, _spread_groups(wor_ref[...]))
        im_ref[...] = _mm(dyc, _spread_groups(woi_ref[...]))

    return pl.pallas_call(
        body, name="s5_state_grads", grid=(S5_OCTETS,),
        in_specs=[_strip(t), OCT_W, OCT_W] + extra_specs, out_specs=[_oct_states(nk), _oct_states(nk)],
        out_shape=[_sds((nk, S5_STATES)), _sds((nk, S5_STATES))],
        compiler_params=_params(("parallel",)),
    )(dy, wo_re, wo_im, *extra)


def s5_scan_bwd(dh_re, dh_im, h_re, h_im, a_re, a_im):
    nk = dh_re.shape[0]

    def body(dhr_ref, dhi_ref, hr_ref, hi_ref, ar_ref, ai_ref, dsr_ref, dsi_ref, dar_ref, dai_ref):
        ar = ar_ref[...]
        ai = ai_ref[...]

        def step(i, carry):
            gr, gi = carry
            k = nk - 1 - i
            dhr = dhr_ref[pl.ds(k, 1), :]
            dhi = dhi_ref[pl.ds(k, 1), :]
            dsr_ref[pl.ds(k, 1), :] = gr
            dsi_ref[pl.ds(k, 1), :] = gi
            return dhr + ar * gr + ai * gi, dhi - ai * gr + ar * gi

        zero = jnp.zeros((1, S5_STATES), F32)
        lax.fori_loop(0, nk, step, (zero, zero))
        dsr, dsi, hr, hi = dsr_ref[...], dsi_ref[...], hr_ref[...], hi_ref[...]
        dar_ref[...] = jnp.sum(dsr * hr + dsi * hi, axis=0, keepdims=True)
        dai_ref[...] = jnp.sum(dsi * hr - dsr * hi, axis=0, keepdims=True)

    vm = pl.BlockSpec(memory_space=pltpu.VMEM)
    return pl.pallas_call(
        body, name="s5_scan_bwd", in_specs=[vm] * 6, out_specs=[vm] * 4,
        out_shape=[_sds((nk, S5_STATES)), _sds((nk, S5_STATES)), _sds((1, S5_STATES)), _sds((1, S5_STATES))],
        input_output_aliases={0: 0, 1: 1}, compiler_params=_params(),
    )(dh_re, dh_im, h_re, h_im, a_re, a_im)


def s5_input_grads(dy, ds_re, ds_im, kd, ws_re, ws_im):
    t = dy.shape[0]
    nk = t // S5_CHUNK

    def body(dy_ref, dsr_ref, dsi_ref, kd_ref, wsr_ref, wsi_ref, du_ref, win_ref):
        _fill_toeplitz(win_ref, kd_ref)
        dyc = _chunk_rows(dy_ref, nk)
        du = jnp.concatenate([_mm_nt(dyc[:, lo:], win_ref[lo:hi, lo:]) for lo, hi in _TOEPLITZ_BLOCKS], axis=1)
        du = du + _mm_nt(dsr_ref[...], _spread_groups(wsr_ref[...])) + _mm_nt(dsi_ref[...], _spread_groups(wsi_ref[...]))
        _store_chunk_rows(du_ref, du, nk)

    return pl.pallas_call(
        body, name="s5_input_grads", grid=(S5_OCTETS,),
        in_specs=[_strip(t), _oct_states(nk), _oct_states(nk), OCT_KD, OCT_W, OCT_W],
        out_specs=_strip(t), out_shape=_sds((t, BRANCH)),
        scratch_shapes=[pltpu.VMEM((S5_OCT_IN, S5_OCT_IN), MXU_DTYPE)],
        compiler_params=_params(("parallel",)),
    )(dy, ds_re, ds_im, kd, ws_re, ws_im)


def s5_weight_grads(u, dy, h_re, h_im, ds_re, ds_im):
    t = u.shape[0]
    nk = t // S5_CHUNK

    def body(u_ref, dy_ref, hre_ref, him_ref, dsr_ref, dsi_ref, dkd_ref, dwsr_ref, dwsi_ref, dwor_ref, dwoi_ref):
        dyc = _chunk_rows(dy_ref, nk, F32)
        uct = _chunk_rows(u_ref, nk, F32).T.astype(MXU_DTYPE)
        dyct = dyc.T.astype(MXU_DTYPE)
        dyc = dyc.astype(MXU_DTYPE)
        dwsr_ref[...] = _fold_groups(_mm(uct, dsr_ref[...]))
        dwsi_ref[...] = _fold_groups(_mm(uct, dsi_ref[...]))
        dwor_ref[...] = _fold_groups(_mm(dyct, hre_ref[...]))
        dwoi_ref[...] = _fold_groups(_mm(dyct, him_ref[...]))
        dkd_ref[...] = jnp.zeros_like(dkd_ref)
        for tt in range(0, S5_CHUNK, 2):
            p = _mm(uct[:(tt + 2) * LANES], dyc[:, tt * LANES:(tt + 2) * LANES])
            for s in range(tt + 2):
                rows = p[s * LANES:(s + 1) * LANES]
                if s <= tt:
                    dkd_ref[tt - s] += rows[:, :LANES]
                dkd_ref[tt + 1 - s] += rows[:, LANES:]

    return pl.pallas_call(
        body, name="s5_weight_grads", grid=(S5_OCTETS,),
        in_specs=[_strip(t), _strip(t)] + [_oct_states(nk)] * 4,
        out_specs=[OCT_KD, OCT_W, OCT_W, OCT_W, OCT_W],
        out_shape=[_sds((S5_OCTETS, S5_CHUNK, LANES, LANES))] + [_sds((S5_OCTETS, S5_OCT_IN, LANES))] * 4,
        compiler_params=_params(("parallel",)),
    )(u, dy, h_re, h_im, ds_re, ds_im)


def ssm_mix_fwd(x, u, gate, y_scan, d, w_glu, b_glu, w_out):
    t = x.shape[0]
    tm = min(ROW_TILE_FWD, t)

    def body(x_ref, u_ref, gate_ref, ys_ref, d_ref, wg_ref, bg_ref, wo_ref, y_ref, g2_ref, xo_ref):
        y = ys_ref[...] + d_ref[...] * u_ref[...]
        z0 = _gelu(y)
        g2 = _mm(z0, wg_ref[...]) + bg_ref[...]
        a = z0 * _sigmoid(g2) * _silu(gate_ref[...])
        y_ref[...] = y
        g2_ref[...] = g2
        xo_ref[...] = x_ref[...] + _mm(a, wo_ref[...])

    row = _rows(tm, BRANCH)
    vec = _whole((1, BRANCH))
    mat = _whole((BRANCH, BRANCH))
    return pl.pallas_call(
        body, name="ssm_mix_fwd", grid=(t // tm,),
        in_specs=[row, row, row, row, vec, mat, vec, mat],
        out_specs=[row, row, row],
        out_shape=[_sds((t, BRANCH))] * 3,
        compiler_params=_params(("parallel",)),
    )(x, u, gate, y_scan, d, w_glu, b_glu, w_out)


def ssm_mix_bwd(dxo, u, gate, y, g2, w_glu, w_out, token=None):
    t = dxo.shape[0]
    tm = min(ROW_TILE_BWD, t)
    extra, extra_specs = _after(token)

    def body(dxo_ref, u_ref, gate_ref, y_ref, g2_ref, wgt_ref, wot_ref, *rest):
        dy_ref, dgate_ref, dwo_ref, dwg_ref, dbg_ref, dd_ref = rest[-6:]

        @pl.when(pl.program_id(0) == 0)
        def _():
            dwo_ref[...] = jnp.zeros_like(dwo_ref)
            dwg_ref[...] = jnp.zeros_like(dwg_ref)
            dbg_ref[...] = jnp.zeros_like(dbg_ref)
            dd_ref[...] = jnp.zeros_like(dd_ref)

        dxo = dxo_ref[...]
        gate = gate_ref[...]
        y = y_ref[...]
        z0, z0_grad = _gelu_and_grad(y)
        sg = _sigmoid(g2_ref[...])
        z = z0 * sg
        sgate, sgate_grad = _silu_and_grad(gate)
        da = _mm_nt(dxo, wot_ref[...])
        dwo_ref[...] += _mm_tn(z * sgate, dxo)
        dz = da * sgate
        dgate_ref[...] = da * z * sgate_grad
        dg2 = dz * z0 * sg * (1.0 - sg)
        dbg_ref[...] += jnp.sum(dg2, axis=0, keepdims=True)
        dwg_ref[...] += _mm_tn(z0, dg2)
        dz0 = dz * sg + _mm_nt(dg2, wgt_ref[...])
        dy = dz0 * z0_grad
        dd_ref[...] += jnp.sum(dy * u_ref[...], axis=0, keepdims=True)
        dy_ref[...] = dy

    row = _rows(tm, BRANCH)
    vec = _whole((1, BRANCH))
    mat = _whole((BRANCH, BRANCH))
    return pl.pallas_call(
        body, name="ssm_mix_bwd", grid=(t // tm,),
        in_specs=[row, row, row, row, row, mat, mat] + extra_specs,
        out_specs=[row, row, mat, mat, vec, vec],
        out_shape=[_sds((t, BRANCH)), _sds((t, BRANCH)), _sds((BRANCH, D_MODEL)), _sds((BRANCH, BRANCH)),
                   _sds((1, BRANCH)), _sds((1, BRANCH))],
        compiler_params=_params(("arbitrary",)),
    )(dxo, u, gate, y, g2, w_glu, w_out, *extra)


def ssm_proj_bwd(x, norm, dxo, dy, du_scan, dgate, d, w_in):
    t = x.shape[0]
    tm = min(ROW_TILE_BWD, t)
    n = 2 * BRANCH

    def body(x_ref, g_ref, dxo_ref, dy_ref, dus_ref, dgate_ref, d_ref, wt_ref, dx_ref, dw_ref, dg_ref):
        @pl.when(pl.program_id(0) == 0)
        def _():
            dw_ref[...] = jnp.zeros_like(dw_ref)
            dg_ref[...] = jnp.zeros_like(dg_ref)

        g = g_ref[...]
        r, xhat, h = _rms(x_ref[...], g)
        h = h.astype(MXU_DTYPE)
        du = dus_ref[...] + d_ref[...] * dy_ref[...]
        dproj = jnp.concatenate([du, dgate_ref[...]], axis=1).astype(MXU_DTYPE)
        dh = jnp.zeros((tm, D_MODEL), F32)
        for j in range(4):
            cols = dproj[:, j * (n // 4):(j + 1) * (n // 4)]
            dh = dh + _mm_nt(cols, wt_ref[j])
            dw_ref[j] += _mm_tn(h, cols)
        dx, dg = _rms_bwd(dh, g, r, xhat)
        dg_ref[...] += dg
        dx_ref[...] = dxo_ref[...] + dx

    row = _rows(tm, D_MODEL)
    vec = _whole((1, D_MODEL))
    blocks = _whole((4, D_MODEL, n // 4))
    return pl.pallas_call(
        body, name="ssm_proj_bwd", grid=(t // tm,),
        in_specs=[row, vec, row, row, row, row, vec, blocks],
        out_specs=[row, blocks, vec],
        out_shape=[_sds((t, D_MODEL)), _sds((4, D_MODEL, n // 4)), _sds((1, D_MODEL))],
        compiler_params=_params(("arbitrary",)),
    )(x, norm, dxo, dy, du_scan, dgate, d, w_in)


ATTN_N = Q_DIM + 2 * KV_DIM + BRANCH


def attn_proj_fwd(x, norm, w_in_t, cos2, sin2):
    t = x.shape[0]
    tm = min(ROW_TILE_FWD, t)

    def body(x_ref, g_ref, w_ref, cos_ref, sin_ref, q_ref, k_ref, v_ref, gate_ref):
        _, _, h = _rms(x_ref[...], g_ref[...])
        p = _mm_nt(h, w_ref[...])
        cs = cos_ref[...]
        sn = sin_ref[...]
        q = p[:, :Q_DIM]
        k = p[:, Q_DIM:Q_DIM + KV_DIM]
        q_ref[...] = q * _tile_lanes(cs, Q_DIM // LANES) + _swap_half_heads(q) * _tile_lanes(sn, Q_DIM // LANES)
        k_ref[...] = k * cs + _swap_half_heads(k) * sn
        v_ref[...] = p[:, Q_DIM + KV_DIM:Q_DIM + 2 * KV_DIM]
        gate_ref[...] = p[:, Q_DIM + 2 * KV_DIM:]

    return pl.pallas_call(
        body, name="attn_proj_fwd", grid=(t // tm,),
        in_specs=[_rows(tm, D_MODEL), _whole((1, D_MODEL)), _whole((ATTN_N, D_MODEL)), _rows(tm, LANES), _rows(tm, LANES)],
        out_specs=[_rows(tm, Q_DIM), _rows(tm, KV_DIM), _rows(tm, KV_DIM), _rows(tm, BRANCH)],
        out_shape=[_sds((t, Q_DIM)), _sds((t, KV_DIM)), _sds((t, KV_DIM)), _sds((t, BRANCH))],
        compiler_params=_params(("parallel",)),
    )(x, norm, w_in_t, cos2, sin2)


GQA_LANES = GQA_GROUP * ATTN_BLOCK


def _window_masks(first_block):
    kj = lax.broadcasted_iota(jnp.int32, (ATTN_BLOCK, GQA_LANES), 0)
    qi = lax.broadcasted_iota(jnp.int32, (ATTN_BLOCK, GQA_LANES), 1) % ATTN_BLOCK
    return kj > qi, kj > jnp.where(first_block, qi, ATTN_BLOCK)


def _fold(upper, both):
    return jnp.where(upper, both[:ATTN_BLOCK], both[ATTN_BLOCK:])


def _unfold(upper, tile):
    return jnp.concatenate([jnp.where(upper, tile, 0.0), jnp.where(upper, 0.0, tile)], axis=0).astype(MXU_DTYPE)


def _stack_heads(ref, group):
    return jnp.concatenate([ref[:, h * HEAD_DIM:(h + 1) * HEAD_DIM] for h in range(group * GQA_GROUP, (group + 1) * GQA_GROUP)], axis=0)


def _unstack_heads(ref, group, stacked):
    for n in range(GQA_GROUP):
        h = group * GQA_GROUP + n
        ref[:, h * HEAD_DIM:(h + 1) * HEAD_DIM] = stacked[n * ATTN_BLOCK:(n + 1) * ATTN_BLOCK]


def _sink_row(sink_ref, group):
    return jnp.concatenate([jnp.full((1, ATTN_BLOCK), sink_ref[group * GQA_GROUP + n], F32) for n in range(GQA_GROUP)], axis=1)


def _lane_is(h):
    return lax.broadcasted_iota(jnp.int32, (1, LANES), 1) == h


def attn_fwd(q, k, v, sinks):
    t = q.shape[0]
    nb = t // ATTN_BLOCK
    scale = HEAD_DIM ** -0.5

    def body(sink_ref, q_ref, kc_ref, kp_ref, vc_ref, vp_ref, o_ref, lse_ref):
        keys = jnp.concatenate([kp_ref[...], kc_ref[...]], axis=0).astype(MXU_DTYPE)
        vals = jnp.concatenate([vp_ref[...], vc_ref[...]], axis=0).astype(MXU_DTYPE)
        upper, dead = _window_masks(pl.program_id(0) == 0)
        for g in range(N_KV_HEADS):
            kv = slice(g * HEAD_DIM, (g + 1) * HEAD_DIM)
            qs = _stack_heads(q_ref, g) * scale
            s = jnp.where(dead, NEG_INF, _fold(upper, _mm_nt(keys[:, kv], qs)))
            sink = _sink_row(sink_ref, g)
            m = jnp.maximum(jnp.max(s, axis=0, keepdims=True), sink)
            p = jnp.exp(s - m)
            den = jnp.sum(p, axis=0, keepdims=True) + jnp.exp(sink - m)
            _unstack_heads(o_ref, g, _mm_tn(_unfold(upper, p * (1.0 / den)), vals[:, kv]))
            lse = m + jnp.log(den)
            for n in range(GQA_GROUP):
                lse_ref[pl.ds(g * GQA_GROUP + n, 1), :] = lse[:, n * ATTN_BLOCK:(n + 1) * ATTN_BLOCK]

    cur = lambda n: pl.BlockSpec((ATTN_BLOCK, n), lambda i: (i, 0))
    prev = lambda n: pl.BlockSpec((ATTN_BLOCK, n), lambda i: (jnp.maximum(i - 1, 0), 0))
    return pl.pallas_call(
        body, name="attn_fwd", grid=(nb,),
        in_specs=[pl.BlockSpec(memory_space=pltpu.SMEM), cur(Q_DIM), cur(KV_DIM), prev(KV_DIM), cur(KV_DIM), prev(KV_DIM)],
        out_specs=[cur(Q_DIM), pl.BlockSpec((N_Q_HEADS, ATTN_BLOCK), lambda i: (0, i))],
        out_shape=[_sds((t, Q_DIM)), _sds((N_Q_HEADS, t))],
        compiler_params=_params(("parallel",)),
    )(sinks, q, k, k, v, v)


def attn_bwd(q, k, v, sinks, o, lse, do):
    t = q.shape[0]
    nb = t // ATTN_BLOCK
    scale = HEAD_DIM ** -0.5

    def body(sink_ref, q_ref, kc_ref, kp_ref, vc_ref, vp_ref, o_ref, lse_ref, do_ref,
             dq_ref, dk_ref, dv_ref, dsink_ref, dk_carry, dv_carry):
        i = pl.program_id(0)

        @pl.when(i == 0)
        def _():
            dsink_ref[...] = jnp.zeros_like(dsink_ref)
            dk_carry[...] = jnp.zeros_like(dk_carry)
            dv_carry[...] = jnp.zeros_like(dv_carry)

        @pl.when(i < nb)
        def _():
            keys = jnp.concatenate([kp_ref[...], kc_ref[...]], axis=0).astype(MXU_DTYPE)
            vals = jnp.concatenate([vp_ref[...], vc_ref[...]], axis=0).astype(MXU_DTYPE)
            upper, dead = _window_masks(i == 0)
            dsink = jnp.zeros((1, LANES), F32)
            dk_heads = []
            dv_heads = []
            for g in range(N_KV_HEADS):
                kv = slice(g * HEAD_DIM, (g + 1) * HEAD_DIM)
                qs = (_stack_heads(q_ref, g) * scale).astype(MXU_DTYPE)
                dos = _stack_heads(do_ref, g)
                lse = jnp.concatenate([lse_ref[pl.ds(g * GQA_GROUP + n, 1), :] for n in range(GQA_GROUP)], axis=1)
                s = jnp.where(dead, NEG_INF, _fold(upper, _mm_nt(keys[:, kv], qs)))
                p = jnp.exp(s - lse)
                delta = _mm_f32(jnp.ones((8, HEAD_DIM), F32), dos * _stack_heads(o_ref, g), ((1,), (1,)))[:1]
                dos = dos.astype(MXU_DTYPE)
                ds = _unfold(upper, p * (_fold(upper, _mm_nt(vals[:, kv], dos)) - delta))
                _unstack_heads(dq_ref, g, _mm_tn(ds, keys[:, kv]) * scale)
                dk_heads.append(_mm(ds, qs))
                dv_heads.append(_mm(_unfold(upper, p), dos))
                at_sink = jnp.exp(_sink_row(sink_ref, g) - lse) * delta
                for n in range(GQA_GROUP):
                    dsink = dsink + jnp.where(_lane_is(g * GQA_GROUP + n), -jnp.sum(at_sink[:, n * ATTN_BLOCK:(n + 1) * ATTN_BLOCK]), 0.0)
            dkk = jnp.concatenate(dk_heads, axis=1)
            dvv = jnp.concatenate(dv_heads, axis=1)
            dsink_ref[...] += dsink
            dk_ref[...] = dk_carry[...] + dkk[:ATTN_BLOCK]
            dv_ref[...] = dv_carry[...] + dvv[:ATTN_BLOCK]
            dk_carry[...] = dkk[ATTN_BLOCK:]
            dv_carry[...] = dvv[ATTN_BLOCK:]

        @pl.when(i == nb)
        def _():
            dk_ref[...] = dk_carry[...]
            dv_ref[...] = dv_carry[...]

    last = nb - 1
    cur = lambda n: pl.BlockSpec((ATTN_BLOCK, n), lambda i: (jnp.minimum(i, last), 0))
    prev = lambda n: pl.BlockSpec((ATTN_BLOCK, n), lambda i: (jnp.clip(i - 1, 0, last), 0))
    late = lambda n: pl.BlockSpec((ATTN_BLOCK, n), lambda i: (i, 0))
    dq, dk_late, dv_late, dsinks = pl.pallas_call(
        body, name="attn_bwd", grid=(nb + 1,),
        in_specs=[pl.BlockSpec(memory_space=pltpu.SMEM), cur(Q_DIM), cur(KV_DIM), prev(KV_DIM), cur(KV_DIM), prev(KV_DIM),
                  cur(Q_DIM), pl.BlockSpec((N_Q_HEADS, ATTN_BLOCK), lambda i: (0, jnp.minimum(i, last))), cur(Q_DIM)],
        out_specs=[cur(Q_DIM), late(KV_DIM), late(KV_DIM), _whole((1, LANES))],
        out_shape=[_sds((t, Q_DIM)), _sds((t + ATTN_BLOCK, KV_DIM)), _sds((t + ATTN_BLOCK, KV_DIM)), _sds((1, LANES))],
        scratch_shapes=[pltpu.VMEM((ATTN_BLOCK, KV_DIM), F32), pltpu.VMEM((ATTN_BLOCK, KV_DIM), F32)],
        compiler_params=_params(("arbitrary",)),
    )(sinks, q, k, k, v, v, o, lse, do)
    return dq, dk_late[ATTN_BLOCK:], dv_late[ATTN_BLOCK:], dsinks


def attn_out_fwd(x, o, gate, w_out):
    t = x.shape[0]
    tm = min(ROW_TILE_FWD, t)

    def body(x_ref, o_ref, gate_ref, w_ref, xo_ref):
        xo_ref[...] = x_ref[...] + _mm(o_ref[...] * _silu(gate_ref[...]), w_ref[...])

    row = _rows(tm, D_MODEL)
    return pl.pallas_call(
        body, name="attn_out_fwd", grid=(t // tm,),
        in_specs=[row, row, row, _whole((Q_DIM, D_MODEL))], out_specs=row, out_shape=_sds((t, D_MODEL)),
        compiler_params=_params(("parallel",)),
    )(x, o, gate, w_out)


def attn_out_bwd(dxo, o, gate, w_out, token=None):
    t = dxo.shape[0]
    tm = min(ROW_TILE_BWD, t)
    extra, extra_specs = _after(token)

    def body(dxo_ref, o_ref, gate_ref, wt_ref, *rest):
        do_ref, dgate_ref, dw_ref = rest[-3:]

        @pl.when(pl.program_id(0) == 0)
        def _():
            dw_ref[...] = jnp.zeros_like(dw_ref)

        dxo = dxo_ref[...]
        o = o_ref[...]
        gate = gate_ref[...]
        sgate, sgate_grad = _silu_and_grad(gate)
        da = _mm_nt(dxo, wt_ref[...])
        dw_ref[...] += _mm_tn(o * sgate, dxo)
        do_ref[...] = da * sgate
        dgate_ref[...] = da * o * sgate_grad

    row = _rows(tm, D_MODEL)
    mat = _whole((Q_DIM, D_MODEL))
    return pl.pallas_call(
        body, name="attn_out_bwd", grid=(t // tm,),
        in_specs=[row, row, row, mat] + extra_specs, out_specs=[row, row, mat],
        out_shape=[_sds((t, Q_DIM)), _sds((t, BRANCH)), _sds((Q_DIM, D_MODEL))],
        compiler_params=_params(("arbitrary",)),
    )(dxo, o, gate, w_out, *extra)


def attn_proj_bwd(x, norm, dxo, dq, dk, dv, dgate, cos2, sin2, w_in_t):
    t = x.shape[0]
    tm = min(ROW_TILE_BWD, t)

    def body(x_ref, g_ref, dxo_ref, dq_ref, dk_ref, dv_ref, dgate_ref, cos_ref, sin_ref, wt_ref, dx_ref, dw_ref, dg_ref):
        @pl.when(pl.program_id(0) == 0)
        def _():
            dw_ref[...] = jnp.zeros_like(dw_ref)
            dg_ref[...] = jnp.zeros_like(dg_ref)

        g = g_ref[...]
        r, xhat, h = _rms(x_ref[...], g)
        cs = cos_ref[...]
        sn = sin_ref[...]
        dqr = dq_ref[...]
        dkr = dk_ref[...]
        dq = dqr * _tile_lanes(cs, Q_DIM // LANES) + _swap_half_heads(dqr * _tile_lanes(sn, Q_DIM // LANES))
        dk = dkr * cs + _swap_half_heads(dkr * sn)
        dproj = jnp.concatenate([dq, dk, dv_ref[...], dgate_ref[...]], axis=1)
        dh = _mm(dproj, wt_ref[...])
        dw_ref[...] += _mm_tn(dproj, h)
        dx, dg = _rms_bwd(dh, g, r, xhat)
        dg_ref[...] += dg
        dx_ref[...] = dxo_ref[...] + dx

    row = _rows(tm, D_MODEL)
    vec = _whole((1, D_MODEL))
    return pl.pallas_call(
        body, name="attn_proj_bwd", grid=(t // tm,),
        in_specs=[row, vec, row, _rows(tm, Q_DIM), _rows(tm, KV_DIM), _rows(tm, KV_DIM), _rows(tm, BRANCH),
                  _rows(tm, LANES), _rows(tm, LANES), _whole((ATTN_N, D_MODEL))],
        out_specs=[row, _whole((ATTN_N, D_MODEL)), vec],
        out_shape=[_sds((t, D_MODEL)), _sds((ATTN_N, D_MODEL)), _sds((1, D_MODEL))],
        compiler_params=_params(("arbitrary",)),
    )(x, norm, dxo, dq, dk, dv, dgate, cos2, sin2, w_in_t)


def attn_out_loss(x, o, gate, w_out, norm, target):
    t = x.shape[0]
    tm = min(ROW_TILE_FWD, t)

    def body(x_ref, o_ref, gate_ref, w_ref, g_ref, tgt_ref, loss_ref, dx_ref, dg_ref):
        @pl.when(pl.program_id(0) == 0)
        def _():
            loss_ref[...] = jnp.zeros_like(loss_ref)
            dg_ref[...] = jnp.zeros_like(dg_ref)

        out = x_ref[...] + _mm(o_ref[...] * _silu(gate_ref[...]), w_ref[...])
        g = g_ref[...]
        r, xhat, y = _rms(out, g)
        err = y - tgt_ref[...]
        loss_ref[...] += 0.5 * jnp.sum(jnp.mean(err * err, axis=-1, keepdims=True), axis=0, keepdims=True)
        dx, dg = _rms_bwd(err * (1.0 / D_MODEL), g, r, xhat)
        dg_ref[...] += dg
        dx_ref[...] = dx

    row = _rows(tm, D_MODEL)
    vec = _whole((1, D_MODEL))
    return pl.pallas_call(
        body, name="attn_out_loss", grid=(t // tm,),
        in_specs=[row, row, row, _whole((Q_DIM, D_MODEL)), vec, row], out_specs=[_whole((1, 1)), row, vec],
        out_shape=[_sds((1, 1)), _sds((t, D_MODEL)), _sds((1, D_MODEL))],
        compiler_params=_params(("arbitrary",)),
    )(x, o, gate, w_out, norm, target)


OCT_TILE = pl.BlockSpec((None, LANES, LANES), lambda b: (b, 0, 0))
N_LAGS = S5_CHUNK + 1


def _cmul(ar, ai, br, bi):
    return ar * br - ai * bi, ar * bi + ai * br


def _cmul_conj(ar, ai, br, bi):
    return ar * br + ai * bi, ar * bi - ai * br


def _mm_f32(a, b, dims):
    return lax.dot_general(a, b, (dims, ((), ())), precision=lax.Precision.HIGH, preferred_element_type=F32)


def _s5_discretise(ar, ai, ls, br, bi):
    dt = jnp.exp(ls)
    xr = ar * dt
    xi = ai * dt
    mag = jnp.exp(xr)
    first = (mag * jnp.cos(xi), mag * jnp.sin(xi))
    powers = [(jnp.ones_like(xr), jnp.zeros_like(xr)), first]
    for _ in range(2, N_LAGS):
        powers.append(_cmul(*powers[-1], *first))
    den = ar * ar + ai * ai
    nr = powers[1][0] - 1.0
    ni = powers[1][1]
    fr = (nr * ar + ni * ai) / den
    fi = (ni * ar - nr * ai) / den
    bbr, bbi = _cmul(fr, fi, br, bi)
    return dt, powers, (fr, fi), (bbr, bbi), den


def _same_group_tile():
    row = lax.broadcasted_iota(jnp.int32, (LANES, LANES), 0)
    col = lax.broadcasted_iota(jnp.int32, (LANES, LANES), 1)
    return (row // SSM_GROUP) == (col // SSM_GROUP)


def _first_copy_lanes():
    return lax.broadcasted_iota(jnp.int32, (LANES, LANES), 1) < SSM_STATE


def s5_param_fwd(tiles, token=None):
    extra, extra_specs = _after(token)

    def body(ar_ref, ai_ref, ls_ref, br_ref, bi_ref, cr_ref, ci_ref, *rest):
        kd_ref, wsr_ref, wsi_ref, wor_ref, woi_ref, pr_ref, pi_ref = rest[-7:]
        cr = cr_ref[...]
        ci = ci_ref[...]
        _, powers, _, (bbr, bbi), _ = _s5_discretise(ar_ref[...], ai_ref[...], ls_ref[...], br_ref[...], bi_ref[...])
        once = _first_copy_lanes()
        crm = jnp.where(once, cr, 0.0)
        cim = jnp.where(once, ci, 0.0)
        same = _same_group_tile()
        for lag in range(S5_CHUNK):
            er, ei = powers[lag]
            xr, xi = _cmul(er, ei, bbr, bbi)
            rows = pl.ds((S5_CHUNK - 1 - lag) * LANES, LANES)
            wsr_ref[rows, :] = xr
            wsi_ref[rows, :] = xi
        k = _mm_f32(wsr_ref[...], crm, ((1,), (1,))) - _mm_f32(wsi_ref[...], cim, ((1,), (1,)))
        for lag in range(S5_CHUNK):
            kd_ref[lag] = jnp.where(same, k[(S5_CHUNK - 1 - lag) * LANES:(S5_CHUNK - lag) * LANES], 0.0)
        for t in range(S5_CHUNK):
            er, ei = powers[t + 1]
            zr, zi = _cmul(er, ei, cr, ci)
            wor_ref[pl.ds(t * LANES, LANES), :] = zr
            woi_ref[pl.ds(t * LANES, LANES), :] = -zi
        pr_ref[...] = powers[S5_CHUNK][0]
        pi_ref[...] = powers[S5_CHUNK][1]

    return pl.pallas_call(
        body, name="s5_param_fwd", grid=(S5_OCTETS,),
        in_specs=[OCT_TILE] * 7 + [ANY] * len(extra),
        out_specs=[OCT_KD, OCT_W, OCT_W, OCT_W, OCT_W, OCT_TILE, OCT_TILE],
        out_shape=[_sds((S5_OCTETS, S5_CHUNK, LANES, LANES))] + [_sds((S5_OCTETS, S5_OCT_IN, LANES))] * 4
                  + [_sds((S5_OCTETS, LANES, LANES))] * 2,
        compiler_params=_params(("parallel",)),
    )(*tiles, *extra)


def s5_param_bwd(tiles, dkd, dws_re, dws_im, dwo_re, dwo_im, dp_re, dp_im):
    def body(ar_ref, ai_ref, ls_ref, br_ref, bi_ref, cr_ref, ci_ref, dkd_ref, dwsr_ref, dwsi_ref, dwor_ref, dwoi_ref, dpr_ref, dpi_ref,
             dar_ref, dai_ref, dls_ref, dbr_ref, dbi_ref, dcr_ref, dci_ref):
        ar = ar_ref[...]
        ai = ai_ref[...]
        br = br_ref[...]
        bi = bi_ref[...]
        cr = cr_ref[...]
        ci = ci_ref[...]
        dt, powers, (fr, fi), (bbr, bbi), den = _s5_discretise(ar, ai, ls_ref[...], br, bi)
        once = _first_copy_lanes()
        crm = jnp.where(once, cr, 0.0)
        cim = jnp.where(once, ci, 0.0)
        same = _same_group_tile()
        zero = jnp.zeros((LANES, LANES), F32)
        dpow = [[zero, zero] for _ in range(N_LAGS)]
        dbbr, dbbi = zero, zero
        by_step = [S5_CHUNK - 1 - s for s in range(S5_CHUNK)]
        x_all = [_cmul(*powers[lag], bbr, bbi) for lag in by_step]
        xr_all = jnp.concatenate([x[0] for x in x_all], axis=0)
        xi_all = jnp.concatenate([x[1] for x in x_all], axis=0)
        g_all = jnp.concatenate([jnp.where(same, dkd_ref[lag], 0.0) for lag in by_step], axis=0)
        dxr_all = dwsr_ref[...] + _mm_f32(g_all, crm, ((1,), (0,)))
        dxi_all = dwsi_ref[...] - _mm_f32(g_all, cim, ((1,), (0,)))
        dcr = jnp.where(once, _mm_f32(g_all, xr_all, ((0,), (0,))), 0.0)
        dci = -jnp.where(once, _mm_f32(g_all, xi_all, ((0,), (0,))), 0.0)
        for lag in range(S5_CHUNK):
            er, ei = powers[lag]
            rows = slice((S5_CHUNK - 1 - lag) * LANES, (S5_CHUNK - lag) * LANES)
            dxr = dxr_all[rows]
            dxi = dxi_all[rows]
            a, b = _cmul_conj(bbr, bbi, dxr, dxi)
            dpow[lag][0] = dpow[lag][0] + a
            dpow[lag][1] = dpow[lag][1] + b
            a, b = _cmul_conj(er, ei, dxr, dxi)
            dbbr = dbbr + a
            dbbi = dbbi + b
        for t in range(S5_CHUNK):
            er, ei = powers[t + 1]
            dzr = dwor_ref[pl.ds(t * LANES, LANES), :]
            dzi = -dwoi_ref[pl.ds(t * LANES, LANES), :]
            a, b = _cmul_conj(cr, ci, dzr, dzi)
            dpow[t + 1][0] = dpow[t + 1][0] + a
            dpow[t + 1][1] = dpow[t + 1][1] + b
            a, b = _cmul_conj(er, ei, dzr, dzi)
            dcr = dcr + a
            dci = dci + b
        dpow[S5_CHUNK][0] = dpow[S5_CHUNK][0] + dpr_ref[...]
        dpow[S5_CHUNK][1] = dpow[S5_CHUNK][1] + dpi_ref[...]
        dfr, dfi = _cmul_conj(br, bi, dbbr, dbbi)
        dbr, dbi = _cmul_conj(fr, fi, dbbr, dbbi)
        dnr, dni = _cmul(ar / den, ai / den, dfr, dfi)
        qr = (fr * ar + fi * ai) / den
        qi = (fi * ar - fr * ai) / den
        dlr, dli = _cmul(-qr, qi, dfr, dfi)
        dpow[1][0] = dpow[1][0] + dnr
        dpow[1][1] = dpow[1][1] + dni
        dxr, dxi = zero, zero
        for lag in range(1, N_LAGS):
            a, b = _cmul_conj(powers[lag][0], powers[lag][1], dpow[lag][0], dpow[lag][1])
            dxr = dxr + lag * a
            dxi = dxi + lag * b
        dar_ref[...] = dlr + dt * dxr
        dai_ref[...] = dli + dt * dxi
        dls_ref[...] = dt * (ar * dxr + ai * dxi)
        dbr_ref[...] = dbr
        dbi_ref[...] = dbi
        dcr_ref[...] = dcr
        dci_ref[...] = dci

    return pl.pallas_call(
        body, name="s5_param_bwd", grid=(S5_OCTETS,),
        in_specs=[OCT_TILE] * 7 + [OCT_KD, OCT_W, OCT_W, OCT_W, OCT_W, OCT_TILE, OCT_TILE], out_specs=[OCT_TILE] * 7,
        out_shape=[_sds((S5_OCTETS, LANES, LANES))] * 7,
        compiler_params=_params(("parallel",)),
    )(*tiles, dkd, dws_re, dws_im, dwo_re, dwo_im, dp_re, dp_im)


def _doubled(v):
    return jnp.concatenate([v, v], axis=-1)


def _s5_param_tiles(a_re, a_im, log_step, b_re, b_im, c_re, c_im):
    def per_group(a):
        return _doubled(jnp.broadcast_to(a.reshape(S5_OCTETS, S5_OCT, 1, SSM_STATE),
                                         (S5_OCTETS, S5_OCT, SSM_GROUP, SSM_STATE)).reshape(S5_OCTETS, LANES, SSM_STATE))

    ls = jnp.broadcast_to(log_step.reshape(S5_OCTETS, S5_OCT, 1, 1), (S5_OCTETS, S5_OCT, SSM_GROUP, LANES)).reshape(S5_OCTETS, LANES, LANES)
    bt = lambda b: _doubled(b.transpose(0, 2, 1).reshape(S5_OCTETS, LANES, SSM_STATE))
    ct = lambda c: _doubled(c.reshape(S5_OCTETS, LANES, SSM_STATE))
    return [per_group(a_re), per_group(a_im), ls, bt(b_re), bt(b_im), ct(c_re), ct(c_im)]


def _s5_param_grads(dtiles):
    dar, dai, dls, dbr, dbi, dcr, dci = dtiles
    halves = lambda d: d[..., :SSM_STATE] + d[..., SSM_STATE:]
    per_group = lambda d: halves(d).reshape(SSM_GROUPS, SSM_GROUP, SSM_STATE).sum(axis=1)
    per_row = lambda d: halves(d).reshape(SSM_GROUPS, SSM_GROUP, SSM_STATE)
    return (per_group(dar), per_group(dai), dls.reshape(SSM_GROUPS, SSM_GROUP * LANES).sum(axis=1),
            per_row(dbr).transpose(0, 2, 1), per_row(dbi).transpose(0, 2, 1), per_row(dcr), per_row(dci))


def _group_power_rows(tile):
    return tile[:, ::SSM_GROUP, :SSM_STATE].reshape(1, S5_STATES)


def _group_power_tiles(row):
    t = jnp.pad(row.reshape(S5_OCTETS, S5_OCT, 1, SSM_STATE), ((0, 0), (0, 0), (0, SSM_GROUP - 1), (0, LANES - SSM_STATE)))
    return t.reshape(S5_OCTETS, LANES, LANES)


def _rope_tables(t):
    pos = jnp.arange(t, dtype=F32)
    inv_freq = ROPE_THETA ** (-jnp.arange(0, HEAD_DIM, 2, dtype=F32) / HEAD_DIM)
    ang = pos[:, None] * inv_freq[None, :]
    cos = jnp.cos(ang)
    sin = jnp.sin(ang)
    cos64 = jnp.concatenate([cos, cos], axis=1)
    sin64 = jnp.concatenate([-sin, sin], axis=1)
    return jnp.concatenate([cos64, cos64], axis=1), jnp.concatenate([sin64, sin64], axis=1)


def _row(v):
    return v.reshape(1, -1)


def _s5_matrices(w, token=None):
    tiles = _s5_param_tiles(w["a_re"], w["a_im"], w["log_step"], w["b_re"], w["b_im"], w["c_re"], w["c_im"])
    kd, ws_re, ws_im, wo_re, wo_im, p_re, p_im = s5_param_fwd(tiles, token)
    return tiles, dict(kd=kd, ws_re=ws_re, ws_im=ws_im, wo_re=wo_re, wo_im=wo_im, a_re=_group_power_rows(p_re), a_im=_group_power_rows(p_im))


def _ssm_forward(x, w):
    tiles, mats = w["s5"] if "s5" in w else _s5_matrices(w)
    u, gate = ssm_proj_fwd(x, _row(w["norm"]), w["w_in"])
    s_re, s_im = s5_chunk_states(u, mats["ws_re"], mats["ws_im"])
    h_re, h_im = s5_scan_fwd(s_re, s_im, mats["a_re"], mats["a_im"])
    y_scan = s5_outputs(u, h_re, h_im, mats["kd"], mats["wo_re"], mats["wo_im"])
    y, g2, x_new = ssm_mix_fwd(x, u, gate, y_scan, _row(w["d"]), w["w_glu"], _row(w["b_glu"]), w["w_out"])
    saved = dict(x=x, u=u, gate=gate, y=y, g2=g2, h_re=h_re, h_im=h_im, mats=mats, tiles=tiles)
    return x_new, saved


def _ssm_backward(dxo, w, s, token=None, early=None):
    dy, dgate, dw_out, dw_glu, db_glu, dd = ssm_mix_bwd(dxo, s["u"], s["gate"], s["y"], s["g2"], w["w_glu"], w["w_out"], token)
    mats = s["mats"]
    started = early(dict(w_glu=dw_glu, w_out=dw_out)) if early else None
    dh_re, dh_im = s5_state_grads(dy, mats["wo_re"], mats["wo_im"], started)
    ds_re, ds_im, da_re, da_im = s5_scan_bwd(dh_re, dh_im, s["h_re"], s["h_im"], mats["a_re"], mats["a_im"])
    du_scan = s5_input_grads(dy, ds_re, ds_im, mats["kd"], mats["ws_re"], mats["ws_im"])
    dkd, dws_re, dws_im, dwo_re, dwo_im = s5_weight_grads(s["u"], dy, s["h_re"], s["h_im"], ds_re, ds_im)
    dparams = _s5_param_grads(s5_param_bwd(s["tiles"], dkd, dws_re, dws_im, dwo_re, dwo_im,
                                           _group_power_tiles(da_re), _group_power_tiles(da_im)))
    dx, dw_in, dnorm = ssm_proj_bwd(s["x"], _row(w["norm"]), dxo, dy, du_scan, dgate, _row(w["d"]), w["w_in"])
    grads = dict(norm=dnorm, w_in=dw_in, d=dd, w_glu=dw_glu, b_glu=db_glu, w_out=dw_out)
    for name, val in zip(("a_re", "a_im", "log_step", "b_re", "b_im", "c_re", "c_im"), dparams):
        grads[name] = val
    return dx, grads


def _attn_forward(x, w, cos2, sin2, loss_head=None):
    q, k, v, gate = attn_proj_fwd(x, _row(w["norm"]), w["w_in"], cos2, sin2)
    o, lse = attn_fwd(q, k, v, w["sinks"])
    if loss_head is None:
        result = attn_out_fwd(x, o, gate, w["w_out"])
    else:
        result = attn_out_loss(x, o, gate, w["w_out"], _row(loss_head[0]), loss_head[1])
    return result, dict(x=x, q=q, k=k, v=v, gate=gate, o=o, lse=lse)


def _attn_backward(dxo, w, s, cos2, sin2, token=None):
    do, dgate, dw_out = attn_out_bwd(dxo, s["o"], s["gate"], w["w_out"], token)
    dq, dk, dv, dsinks = attn_bwd(s["q"], s["k"], s["v"], w["sinks"], s["o"], s["lse"], do)
    dx, dw_in, dnorm = attn_proj_bwd(s["x"], _row(w["norm"]), dxo, dq, dk, dv, dgate, cos2, sin2, w["w_in"])
    return dx, dict(norm=dnorm, w_in=dw_in, sinks=dsinks[0, :N_Q_HEADS], w_out=dw_out)


class _NoExchanges:
    def __init__(self, layers):
        self.layers = layers

    def layer(self, i, x):
        return self.layers[i]

    def early_grads(self, i, grads):
        return None

    def layer_done(self, i, grads, dx):
        return None


def _sequence_step(x, target, final_norm, hooks, depth=4):
    cos2, sin2 = _rope_tables(x.shape[0])
    saved, layers = [], []
    for i in range(depth):
        w = hooks.layer(i, x)
        layers.append(w)
        if i % 2 == 0:
            x, s = _ssm_forward(x, w)
        else:
            x, s = _attn_forward(x, w, cos2, sin2, (final_norm, target) if i == depth - 1 else None)
        saved.append(s)
    loss, dx, dfinal = x
    grads = {"final_norm": dfinal}
    token = None
    for i in reversed(range(depth)):
        if i % 2 == 0:
            dx, g = _ssm_backward(dx, layers[i], saved[i], token, functools.partial(hooks.early_grads, i))
        else:
            dx, g = _attn_backward(dx, layers[i], saved[i], cos2, sin2, token)
        g = {"l%d_%s" % (i, name): val for name, val in g.items()}
        grads.update(g)
        token = hooks.layer_done(i, g, dx)
    return loss[0, 0], dx, grads


ANY = pl.BlockSpec(memory_space=pl.ANY)


def _place():
    return lax.axis_index("x"), lax.axis_index("y"), lax.axis_index("c")


def _other_chips(x, y):
    return [(1 - x, y), (x, 1 - y), (1 - x, 1 - y)]


class _StagedCopies:
    def __init__(self, bufs, load_sems, store_sems):
        self.bufs, self.load_sems, self.store_sems = bufs, load_sems, store_sems
        self.loads, self.stores = [], []

    def load(self, i, src):
        cp = pltpu.make_async_copy(src, self.bufs[i], self.load_sems.at[i])
        cp.start()
        self.loads.append(cp)

    def store(self, i, dst):
        self.loads[i].wait()
        cp = pltpu.make_async_copy(self.bufs[i], dst, self.store_sems.at[i])
        cp.start()
        self.stores.append(cp)

    def finish(self):
        for cp in self.stores:
            cp.wait()


def _staging(blocks):
    n = len(blocks)
    return [pltpu.VMEM(b.shape, b.dtype) for b in blocks] + [pltpu.SemaphoreType.DMA((n,)), pltpu.SemaphoreType.DMA((n,))]


def exchange_halves_with_sibling(grads):
    n = len(grads)

    def body(*refs):
        ins, outs = refs[:n], refs[n:2 * n]
        send_sems, recv_sems = refs[2 * n:]
        x, y, c = _place()
        copies = []
        for i in range(n):
            half = ins[i].shape[1] // 2
            src = ins[i].at[:, pl.ds((1 - c) * half, half), :]
            cp = pltpu.make_async_remote_copy(src_ref=src, dst_ref=outs[i], send_sem=send_sems.at[i], recv_sem=recv_sems.at[i],
                                              device_id=(x, y, 1 - c), device_id_type=MESH)
            cp.start()
            copies.append(cp)
        for cp in copies:
            cp.wait()

    return pl.pallas_call(
        body, name="exchange_halves_with_sibling",
        in_specs=[ANY] * n, out_specs=[ANY] * n,
        out_shape=[_sds((g.shape[0], g.shape[1] // 2, g.shape[2])) for g in grads],
        scratch_shapes=[pltpu.SemaphoreType.DMA((n,)), pltpu.SemaphoreType.DMA((n,))],
    )(*grads)


def swap_halves_with_sibling(pieces):
    n = len(pieces)

    def body(*refs):
        ins, outs = refs[:n], refs[n:2 * n]
        send_sems, recv_sems = refs[2 * n:2 * n + 2]
        own = _StagedCopies(refs[2 * n + 2:3 * n + 2], *refs[3 * n + 2:])
        x, y, c = _place()
        for i in range(n):
            own.load(i, ins[i])
        swaps = []
        for i in range(n):
            cp = pltpu.make_async_remote_copy(src_ref=ins[i], dst_ref=outs[i].at[c], send_sem=send_sems.at[i], recv_sem=recv_sems.at[i],
                                              device_id=(x, y, 1 - c), device_id_type=MESH)
            cp.start()
            swaps.append(cp)
        for i in range(n):
            own.store(i, outs[i].at[c])
        for i in range(n):
            pltpu.make_async_remote_copy(src_ref=ins[i], dst_ref=outs[i].at[1 - c], send_sem=send_sems.at[i], recv_sem=recv_sems.at[i],
                                         device_id=(x, y, 1 - c), device_id_type=MESH).wait_recv()
        for cp in swaps:
            cp.wait_send()
        own.finish()

    return pl.pallas_call(
        body, name="swap_halves_with_sibling",
        in_specs=[ANY] * n, out_specs=[ANY] * n,
        out_shape=[_sds((2,) + p.shape) for p in pieces],
        scratch_shapes=[pltpu.SemaphoreType.DMA((n,)), pltpu.SemaphoreType.DMA((n,))] + _staging(pieces),
        compiler_params=_params(),
    )(*pieces)


def pass_halves_to_sibling(stacks):
    n = len(stacks)

    def body(*refs):
        outs = refs[n:2 * n]
        send_sems, recv_sems = refs[2 * n:]
        x, y, c = _place()
        sends = []
        for i in range(n):
            for k, (tx, ty) in enumerate(_other_chips(x, y)):
                mine = _rows_of_core(outs[i].at[2 * tx + ty], c, True)
                cp = pltpu.make_async_remote_copy(src_ref=mine, dst_ref=mine, send_sem=send_sems.at[i, k], recv_sem=recv_sems.at[i, k],
                                                  device_id=(x, y, 1 - c), device_id_type=MESH)
                cp.start()
                sends.append(cp)
        for i in range(n):
            for k, (tx, ty) in enumerate(_other_chips(x, y)):
                missing = _rows_of_core(outs[i].at[2 * tx + ty], 1 - c, True)
                pltpu.make_async_remote_copy(src_ref=missing, dst_ref=missing, send_sem=send_sems.at[i, k], recv_sem=recv_sems.at[i, k],
                                             device_id=(x, y, 1 - c), device_id_type=MESH).wait_recv()
        for cp in sends:
            cp.wait_send()

    sems = pltpu.SemaphoreType.DMA((n, 3))
    return pl.pallas_call(
        body, name="pass_halves_to_sibling", in_specs=[ANY] * n, out_specs=[ANY] * n,
        out_shape=[_sds(s.shape, s.dtype) for s in stacks], input_output_aliases={i: i for i in range(n)},
        scratch_shapes=[sems, sems],
    )(*stacks)


IN_HBM = pl.BlockSpec(memory_space=pltpu.HBM)
SEMAPHORES = pl.BlockSpec(memory_space=pltpu.SEMAPHORE)
DATAFLOW = pltpu.SideEffectType.DATAFLOW_SIDE_EFFECTING


def _hbm(a):
    return pltpu.with_memory_space_constraint(a, pltpu.HBM)


def place_own_blocks(shards):
    n = len(shards)

    def body(*refs):
        ins, outs = refs[:n], refs[n:2 * n]
        own = _StagedCopies(refs[2 * n:3 * n], *refs[3 * n:])
        x, y, _ = _place()
        for i in range(n):
            own.load(i, ins[i])
        for i in range(n):
            own.store(i, outs[i].at[2 * x + y])
        own.finish()

    return pl.pallas_call(
        body, name="place_own_blocks", in_specs=[ANY] * n, out_specs=[ANY] * n,
        out_shape=[_sds((4,) + s.shape, s.dtype) for s in shards],
        scratch_shapes=_staging(shards), compiler_params=_params(),
    )(*shards)


def _block_to_send(ref, chip, per_target):
    if not per_target:
        return ref
    return ref.at[chip] if ref.shape[0] == 4 else ref.at[0]


def _rows_of_core(ref, c, core_half):
    if not core_half:
        return ref
    rows = ref.shape[0] // 2
    return ref.at[pl.ds(c * rows, rows), :]


def start_sends_to_chips(name, sources, landings, per_target, after, core_half=False):
    n = len(sources)
    n_sems = 2 * 3 * n

    def body(*refs):
        srcs = refs[:n]
        sems = refs[2 * n + 1:2 * n + 1 + n_sems]
        lands = refs[2 * n + 1 + n_sems:3 * n + 1 + n_sems]
        token = refs[3 * n + 1 + n_sems]
        x, y, c = _place()
        me = 2 * x + y
        for i in range(n):
            for k, (tx, ty) in enumerate(_other_chips(x, y)):
                src = _rows_of_core(_block_to_send(srcs[i], 2 * tx + ty, per_target), c, core_half)
                dst = _rows_of_core(lands[i].at[me], c, core_half)
                pltpu.make_async_remote_copy(src_ref=src, dst_ref=dst, send_sem=sems[2 * (3 * i + k)], recv_sem=sems[2 * (3 * i + k) + 1],
                                             device_id=(tx, ty, c), device_id_type=MESH).start()
        token[...] = jnp.zeros_like(token)

    outs = pl.pallas_call(
        body, name=name,
        in_specs=[IN_HBM] * (2 * n) + [ANY],
        out_specs=[SEMAPHORES] * n_sems + [IN_HBM] * n + [pl.BlockSpec(memory_space=pltpu.VMEM)],
        out_shape=[pltpu.SemaphoreType.DMA(())] * n_sems + [pltpu.HBM(l.shape, l.dtype) for l in landings] + [_sds(TOKEN_SHAPE)],
        input_output_aliases={n + i: n_sems + i for i in range(n)},
        compiler_params=pltpu.CompilerParams(has_side_effects=DATAFLOW),
    )(*[_hbm(s) for s in sources], *[_hbm(l) for l in landings], after)
    return list(outs[:n_sems]), list(outs[n_sems:n_sems + n]), outs[n_sems + n]


def wait_sends_to_chips(name, sources, landings, sems, per_target, after, core_half=False):
    n = len(sources)
    n_sems = len(sems)

    def body(*refs):
        srcs = refs[:n]
        sem_refs = refs[2 * n:2 * n + n_sems]
        lands = refs[2 * n + n_sems + 1:]
        x, y, c = _place()
        me = 2 * x + y
        for i in range(n):
            for k, (tx, ty) in enumerate(_other_chips(x, y)):
                src = _rows_of_core(_block_to_send(srcs[i], me, per_target), c, core_half)
                dst = _rows_of_core(lands[i].at[2 * tx + ty], c, core_half)
                cp = pltpu.make_async_remote_copy(src_ref=src, dst_ref=dst, send_sem=sem_refs[2 * (3 * i + k)],
                                                  recv_sem=sem_refs[2 * (3 * i + k) + 1], device_id=(tx, ty, c), device_id_type=MESH)
                cp.wait_send()
                cp.wait_recv()

    return pl.pallas_call(
        body, name=name,
        in_specs=[IN_HBM] * (2 * n) + [SEMAPHORES] * n_sems + [ANY],
        out_specs=[IN_HBM] * n,
        out_shape=[pltpu.HBM(l.shape, l.dtype) for l in landings],
        input_output_aliases={n + i: i for i in range(n)},
        compiler_params=pltpu.CompilerParams(has_side_effects=DATAFLOW),
    )(*[_hbm(s) for s in sources], *landings, *sems, after)


def _row_tile(rows, cols):
    tm = rows
    while tm * cols * 4 > (2 << 20) and tm % 16 == 0:
        tm //= 2
    return tm


def add_pairs(half, a_list, b_list, out_dtypes, copies=1):
    n = len(a_list)
    nb = a_list[0].shape[0]

    def body(half_ref, *refs):
        for i in range(n):
            total = (refs[i][...] + refs[n + i][...]).astype(out_dtypes[i])
            for o_ref in refs[2 * n + i * copies:2 * n + (i + 1) * copies]:
                o_ref[...] = total

    halves = [pl.BlockSpec((None,) + b.shape[1:], lambda j, h: (j, h[0], 0)) for b in b_list]
    whole = [pl.BlockSpec((None,) + b.shape[1:], lambda j, h: (j, 0, 0)) for b in b_list]
    outs = pl.pallas_call(
        body, name="add_pairs",
        grid_spec=pltpu.PrefetchScalarGridSpec(num_scalar_prefetch=1, grid=(nb,), in_specs=halves + whole,
                                               out_specs=[s for s in whole for _ in range(copies)]),
        out_shape=[_sds(b.shape, dt) for b, dt in zip(b_list, out_dtypes) for _ in range(copies)],
        compiler_params=_params(("parallel",)),
    )(half, *a_list, *b_list)
    return [tuple(outs[i * copies:(i + 1) * copies]) for i in range(n)]


def sum_fours(arrays, token=None):
    n = len(arrays)
    extra, extra_specs = _after(token)
    steps = 2 if all(a.shape[1] % 32 == 0 for a in arrays) else 1

    def body(*refs):
        outs = refs[-n:]
        for a_ref, o_ref in zip(refs[:n], outs):
            o_ref[...] = ((a_ref[0].astype(F32) + a_ref[1].astype(F32)) + a_ref[2].astype(F32)) + a_ref[3].astype(F32)

    return pl.pallas_call(
        body, name="sum_fours", grid=(steps,),
        in_specs=[pl.BlockSpec((4, a.shape[1] // steps, a.shape[2]), lambda i: (0, i, 0)) for a in arrays] + extra_specs,
        out_specs=[pl.BlockSpec((a.shape[1] // steps, a.shape[2]), lambda i: (i, 0)) for a in arrays],
        out_shape=[_sds(a.shape[1:]) for a in arrays], compiler_params=_params(("parallel",)),
    )(*arrays, *extra)


def _adamw_update(w_ref, g_ref, m_ref, v_ref, d_ref, nm_ref, nv_ref):
    g = g_ref[...]
    nm = ADAM_B1 * m_ref[...] + (1.0 - ADAM_B1) * g
    nv = ADAM_B2 * v_ref[...] + (1.0 - ADAM_B2) * (g * g)
    d_ref[...] = -ADAM_LR * ((nm / (1.0 - ADAM_B1 ** ADAM_STEP)) / (jnp.sqrt(nv / (1.0 - ADAM_B2 ** ADAM_STEP)) + ADAM_EPS) + ADAM_WD * w_ref[...])
    nm_ref[...] = nm
    nv_ref[...] = nv


def adamw(w, g, m, v):
    rows, cols = w.shape
    tm = _row_tile(rows, cols)

    def body(*refs):
        _adamw_update(*refs)

    spec = pl.BlockSpec((tm, cols), lambda i: (i, 0))
    return pl.pallas_call(
        body, name="adamw", grid=(rows // tm,), in_specs=[spec] * 4, out_specs=[spec] * 3,
        out_shape=[_sds(w.shape)] * 3, compiler_params=_params(("parallel",)),
    )(w, g, m, v)


def adamw_small(ws, gs, ms, vs, slabs=None):
    n = len(ws)

    def body(*refs):
        for i in range(n):
            _adamw_update(refs[i], refs[n + i], refs[2 * n + i], refs[3 * n + i], refs[4 * n + i], refs[5 * n + i], refs[6 * n + i])

    if slabs is None:
        grid = ()
        specs = [pl.BlockSpec(memory_space=pltpu.VMEM)] * n
    else:
        grid = (slabs,)
        specs = [pl.BlockSpec((w.shape[0] // slabs,) + w.shape[1:], lambda i: (i, 0, 0)) for w in ws]
    outs = pl.pallas_call(
        body, name="adamw_small", grid=grid, in_specs=specs * 4, out_specs=specs * 3,
        out_shape=[_sds(w.shape) for w in ws] * 3, compiler_params=_params(("parallel",) if slabs else None),
    )(*ws, *gs, *ms, *vs)
    return outs[:n], outs[n:2 * n], outs[2 * n:]


PACK_TILE = 8 * LANES
PACK_PIECES = 8
PACK_ALIGN = PACK_PIECES * 16


def _pack_small(values, scalar=None):
    parts = []
    for name in PACK_NAMES:
        flat = values[name].reshape(-1)
        pad = (-flat.shape[0]) % PACK_TILE
        if pad:
            flat = jnp.concatenate([flat, jnp.zeros((pad,), F32)])
        parts.append(flat.reshape(-1, LANES))
    rows = sum(p.shape[0] for p in parts) + 8
    parts.append(jnp.zeros(((-rows) % PACK_ALIGN, LANES), F32))
    last = jnp.zeros((8, LANES), F32)
    parts.append(last if scalar is None else jnp.broadcast_to(scalar.astype(F32), (8, LANES)))
    return jnp.concatenate(parts, axis=0)


def _pack_row_of(name, like):
    row = 0
    for other in PACK_NAMES:
        if other == name:
            return row
        row += -(-math.prod(like[other].shape) // PACK_TILE) * 8
    raise KeyError(name)


def _unpack_small(pack, like):
    out = {}
    row = 0
    for name in PACK_NAMES:
        size = math.prod(like[name].shape)
        rows = -(-size // PACK_TILE) * 8
        out[name] = pack[row:row + rows].reshape(-1)[:size].reshape(like[name].shape)
        row += rows
    return out


def _travels_transposed(name, shard):
    return name.endswith("w_in") and shard.shape[-1] % LANES != 0


def _to_blocks(name, full):
    if full.ndim == 3:
        return full
    return full.reshape(4, full.shape[0] // 4, full.shape[1])


def _from_blocks(name, stacked):
    if name.endswith("w_in") and stacked.shape[2] % LANES == 0 and stacked.shape[1] == D_MODEL:
        return stacked
    return stacked.reshape(4 * stacked.shape[1], stacked.shape[2])


def _layer_big_names(i):
    return [n for n in BIG_NAMES if n.startswith("l%d_" % i)]


class _OverlappedExchanges:
    def __init__(self, weights):
        self.weights = weights
        self.c = lax.axis_index("c")
        self.first = _layer_big_names(0)
        self.later = [n for n in BIG_NAMES if n not in self.first]
        shards = [weights[n].astype(MXU_DTYPE) for n in self.first + self.later]
        shards = [s.T if _travels_transposed(n, s) else s for n, s in zip(self.first + self.later, shards)]
        placed = place_own_blocks(shards)
        k = len(self.first)
        sems, stacks, token = start_sends_to_chips("gather_first_start", shards[:k], placed[:k], False, shards[0], core_half=True)
        self.gather_first = (shards[:k], sems, stacks)
        sems, stacks, token = start_sends_to_chips("gather_later_start", shards[k:], placed[k:], False, token)
        self.gather_later = (shards[k:], sems, stacks)
        self.s5 = {}
        for i in (0, 2):
            self.s5[i] = _s5_matrices({n: weights["l%d_%s" % (i, n)] for n in SSM_NAMES if "l%d_%s" % (i, n) in SMALL_NAMES}, token)
            token = self.s5[i][1]["kd"]
        self.full = {}
        self.in_flight = {}
        self.contributions = {}

    def layer(self, i, x):
        if i == 0:
            shards, sems, stacks = self.gather_first
            stacks = wait_sends_to_chips("gather_first_wait", shards, stacks, sems, False, self.s5[2][1]["kd"], core_half=True)
            stacks = pass_halves_to_sibling(stacks)
            self.full.update({n: _from_blocks(n, g) for n, g in zip(self.first, stacks)})
        if i == 1:
            shards, sems, stacks = self.gather_later
            stacks = wait_sends_to_chips("gather_later_wait", shards, stacks, sems, False, x)
            self.full.update({n: _from_blocks(n, g) for n, g in zip(self.later, stacks)})
        names = SSM_NAMES if i % 2 == 0 else ATTN_NAMES
        w = {n: self.full.get("l%d_%s" % (i, n), self.weights.get("l%d_%s" % (i, n))) for n in names}
        if i in self.s5:
            w["s5"] = self.s5[i]
        return w

    def chip_sums(self, names, grads, extra_blocks=(), extra_dtypes=(), copies=1):
        blocks = [_to_blocks(n, grads[n]) for n in names] + list(extra_blocks)
        from_sibling = exchange_halves_with_sibling(blocks)
        k = len(names)
        half = self.c.reshape(1).astype(jnp.int32)
        sums = add_pairs(half, blocks[:k], from_sibling[:k], [WIRE_DTYPE] * k, copies)
        if extra_blocks:
            sums += add_pairs(half, blocks[k:], from_sibling[k:], list(extra_dtypes), copies)
        return sums

    def start_scatter(self, tag, names, grads):
        pairs = self.chip_sums(names, grads, copies=2)
        sums = [p[0] for p in pairs]
        sems, landings, token = start_sends_to_chips("scatter_start_" + tag, sums, [p[1] for p in pairs], True, sums[0])
        self.in_flight[tag] = (names, sums, sems, landings)
        return token

    def wait_scatter(self, tag, after):
        if tag in self.in_flight:
            names, sums, sems, landings = self.in_flight.pop(tag)
            done = wait_sends_to_chips("scatter_wait_" + tag, sums, landings, sems, True, after)
            self.contributions.update(zip(names, done))

    def early_grads(self, i, grads):
        if i != 0:
            return None
        return self.start_scatter("l0_early", ["l0_w_glu", "l0_w_out"], {"l0_" + n: g for n, g in grads.items()})

    def layer_done(self, i, grads, dx):
        self.wait_scatter("l%d" % (i + 1), dx)
        if i == 0:
            self.wait_scatter("l0_early", dx)
            return None
        return self.start_scatter("l%d" % i, _layer_big_names(i), grads)


def _train_step(x, loss_target, weights, moments_m, moments_v):
    hooks = _OverlappedExchanges(weights)
    loss, dx, grads = _sequence_step(x[0], loss_target[0], weights["final_norm"], hooks)
    small_pack = _pack_small({n: grads[n] for n in SMALL_NAMES}, scalar=loss)
    tail = small_pack.shape[0] - _pack_row_of(EXACT_NAMES[0], grads)
    small = [small_pack[None, :-tail], small_pack[None, -tail:]]
    last = [n for n in _layer_big_names(0) if n not in hooks.contributions]
    pairs = hooks.chip_sums(last, grads, extra_blocks=small, extra_dtypes=[WIRE_DTYPE, F32], copies=2)
    sums = [p[0] for p in pairs]
    landings = [p[1] for p in pairs[:-2]] + [jnp.broadcast_to(s, (4,) + s.shape[1:]) for s in sums[-2:]]
    sems, landings, token = start_sends_to_chips("scatter_start_l0", sums, landings, True, sums[0])
    out_grad, out_delta, out_m, out_v = {}, {}, {}, {}

    def finish(names, arrays, token=None):
        shared = swap_halves_with_sibling(sum_fours(arrays, token))
        rest = []
        for n, s in zip(names, shared):
            if n not in weights:
                rest.append(s.reshape(-1, LANES))
                continue
            out_grad[n] = s.reshape(2 * s.shape[1], s.shape[2])
            if _travels_transposed(n, weights[n]):
                out_grad[n] = out_grad[n].T
            out_delta[n], out_m[n], out_v[n] = adamw(weights[n], out_grad[n], moments_m[n], moments_v[n])
        return rest

    others = [n for n in BIG_NAMES if n not in last]
    finish(others, [hooks.contributions[n] for n in others], token)
    arrived = wait_sends_to_chips("scatter_wait_l0", sums, landings, sems, True, out_v[others[-1]])
    small_grad_pack = jnp.concatenate(finish(last + ["small", "small tail"], arrived), axis=0)
    loss = small_grad_pack[-8, 0]
    out_grad.update(_unpack_small(small_grad_pack, {n: weights[n] for n in SMALL_NAMES}))
    cubes = [n for n in SMALL_NAMES if weights[n].ndim == 3]
    for names, slabs in ((cubes, 8), ([n for n in SMALL_NAMES if n not in cubes], None)):
        deltas, new_ms, new_vs = adamw_small(*[[group[n] for n in names] for group in (weights, out_grad, moments_m, moments_v)], slabs=slabs)
        out_delta.update(zip(names, deltas))
        out_m.update(zip(names, new_ms))
        out_v.update(zip(names, new_vs))
    outs = [loss, dx[None]]
    for group in (out_grad, out_delta, out_m, out_v):
        outs.extend(group[n] for n in WEIGHT_NAMES)
    return tuple(outs)


def kernel(x, l0_norm, l0_w_in, l0_a_re, l0_a_im, l0_log_step, l0_b_re, l0_b_im, l0_c_re, l0_c_im, l0_d, l0_w_glu, l0_b_glu, l0_w_out, l1_norm, l1_w_in, l1_sinks, l1_w_out, l2_norm, l2_w_in, l2_a_re, l2_a_im, l2_log_step, l2_b_re, l2_b_im, l2_c_re, l2_c_im, l2_d, l2_w_glu, l2_b_glu, l2_w_out, l3_norm, l3_w_in, l3_sinks, l3_w_out, final_norm, loss_target, m_l0_norm, m_l0_w_in, m_l0_a_re, m_l0_a_im, m_l0_log_step, m_l0_b_re, m_l0_b_im, m_l0_c_re, m_l0_c_im, m_l0_d, m_l0_w_glu, m_l0_b_glu, m_l0_w_out, m_l1_norm, m_l1_w_in, m_l1_sinks, m_l1_w_out, m_l2_norm, m_l2_w_in, m_l2_a_re, m_l2_a_im, m_l2_log_step, m_l2_b_re, m_l2_b_im, m_l2_c_re, m_l2_c_im, m_l2_d, m_l2_w_glu, m_l2_b_glu, m_l2_w_out, m_l3_norm, m_l3_w_in, m_l3_sinks, m_l3_w_out, m_final_norm, v_l0_norm, v_l0_w_in, v_l0_a_re, v_l0_a_im, v_l0_log_step, v_l0_b_re, v_l0_b_im, v_l0_c_re, v_l0_c_im, v_l0_d, v_l0_w_glu, v_l0_b_glu, v_l0_w_out, v_l1_norm, v_l1_w_in, v_l1_sinks, v_l1_w_out, v_l2_norm, v_l2_w_in, v_l2_a_re, v_l2_a_im, v_l2_log_step, v_l2_b_re, v_l2_b_im, v_l2_c_re, v_l2_c_im, v_l2_d, v_l2_w_glu, v_l2_b_glu, v_l2_w_out, v_l3_norm, v_l3_w_in, v_l3_sinks, v_l3_w_out, v_final_norm):
    args = locals()
    weights = {n: args[n] for n in WEIGHT_NAMES}
    moments_m = {n: args["m_" + n] for n in WEIGHT_NAMES}
    moments_v = {n: args["v_" + n] for n in WEIGHT_NAMES}
    return _train_step(x, loss_target, weights, moments_m, moments_v)
```

```python
import functools
import math

import jax
import jax.numpy as jnp
from jax import lax
from jax.experimental import pallas as pl
from jax.experimental.pallas import tpu as pltpu

F32 = jnp.float32
MXU_DTYPE = jnp.bfloat16
WIRE_DTYPE = jnp.bfloat16
MESH = pl.DeviceIdType.MESH

D_MODEL = 1024
BRANCH = 1024
NORM_EPS = 1e-5
SSM_GROUPS = 64
SSM_GROUP = 16
SSM_STATE = 64
S5_CHUNK = 16
LANES = 128
S5_OCT = LANES // SSM_GROUP
S5_OCTETS = SSM_GROUPS // S5_OCT
S5_OCT_IN = S5_CHUNK * LANES
S5_OCT_STATE = S5_OCT * SSM_STATE
S5_STATES = SSM_GROUPS * SSM_STATE
HEAD_DIM = 64
N_Q_HEADS = 16
N_KV_HEADS = 2
GQA_GROUP = N_Q_HEADS // N_KV_HEADS
ATTN_BLOCK = 128
Q_DIM = N_Q_HEADS * HEAD_DIM
KV_DIM = N_KV_HEADS * HEAD_DIM
ROPE_THETA = 10000.0
NEG_INF = -1e30
ADAM_LR = 0.001
ADAM_B1 = 0.9
ADAM_B2 = 0.999
ADAM_EPS = 1e-08
ADAM_WD = 0.01
ADAM_STEP = 10

VMEM_LIMIT_V7X = 56 * 1024 * 1024
ROW_TILE_FWD = 512
ROW_TILE_BWD = 512

SSM_NAMES = ("norm", "w_in", "a_re", "a_im", "log_step", "b_re", "b_im", "c_re", "c_im", "d", "w_glu", "b_glu", "w_out")
ATTN_NAMES = ("norm", "w_in", "sinks", "w_out")


def _weight_names():
    names = []
    for i in range(4):
        for n in (SSM_NAMES if i % 2 == 0 else ATTN_NAMES):
            names.append("l%d_%s" % (i, n))
    names.append("final_norm")
    return names


WEIGHT_NAMES = _weight_names()
BIG_NAMES = [n for n in WEIGHT_NAMES if n.endswith(("w_in", "w_glu", "w_out"))]
SMALL_NAMES = [n for n in WEIGHT_NAMES if n not in BIG_NAMES]
EXACT_NAMES = [n for n in SMALL_NAMES if n.endswith(("log_step", "sinks")) or n == "final_norm"]
PACK_NAMES = [n for n in SMALL_NAMES if n not in EXACT_NAMES] + EXACT_NAMES


def _params(semantics=None):
    return pltpu.CompilerParams(dimension_semantics=semantics, vmem_limit_bytes=VMEM_LIMIT_V7X)


def _rows(tm, n):
    return pl.BlockSpec((tm, n), lambda i: (i, 0))


def _whole(shape):
    return pl.BlockSpec(shape, lambda i: (0,) * len(shape), pipeline_mode=pl.Buffered(1))


def _sds(shape, dtype=F32):
    return jax.ShapeDtypeStruct(shape, dtype)


def _mm(a, b):
    return jnp.dot(a.astype(MXU_DTYPE), b.astype(MXU_DTYPE), preferred_element_type=F32)


def _mm_tn(a, b):
    return lax.dot_general(a.astype(MXU_DTYPE), b.astype(MXU_DTYPE), (((0,), (0,)), ((), ())), preferred_element_type=F32)


def _mm_nt(a, b):
    return lax.dot_general(a.astype(MXU_DTYPE), b.astype(MXU_DTYPE), (((1,), (1,)), ((), ())), preferred_element_type=F32)


def _sigmoid(x):
    return 0.5 + 0.5 * jnp.tanh(0.5 * x)


def _silu(x):
    return x * _sigmoid(x)


def _silu_and_grad(x):
    s = _sigmoid(x)
    return x * s, s * (1.0 + x * (1.0 - s))


GELU_C0 = math.sqrt(2.0 / math.pi)
GELU_C1 = 0.044715


def _gelu(x):
    return 0.5 * x * (1.0 + jnp.tanh(GELU_C0 * (x + GELU_C1 * x * x * x)))


def _gelu_and_grad(x):
    x2 = x * x
    th = jnp.tanh(GELU_C0 * x * (1.0 + GELU_C1 * x2))
    half = 0.5 + 0.5 * th
    return x * half, half + 0.5 * x * (1.0 - th * th) * (GELU_C0 + 3.0 * GELU_C0 * GELU_C1 * x2)


def _rms(x, g):
    r = lax.rsqrt(jnp.mean(x * x, axis=-1, keepdims=True) + NORM_EPS)
    xhat = x * r
    return r, xhat, xhat * g


def _rms_bwd(dh, g, r, xhat):
    dxhat = dh * g
    dx = r * (dxhat - xhat * jnp.mean(dxhat * xhat, axis=-1, keepdims=True))
    return dx, jnp.sum(dh * xhat, axis=0, keepdims=True)


def _swap_half_heads(x):
    n = x.shape[-1]
    lane = lax.broadcasted_iota(jnp.int32, x.shape, x.ndim - 1)
    first = (lane % HEAD_DIM) < (HEAD_DIM // 2)
    return jnp.where(first, pltpu.roll(x, n - HEAD_DIM // 2, x.ndim - 1), pltpu.roll(x, HEAD_DIM // 2, x.ndim - 1))


def _tile_lanes(t, reps):
    return jnp.concatenate([t] * reps, axis=1)


TOKEN_SHAPE = (8, LANES)


def _after(token):
    return ([], []) if token is None else ([token], [_whole(TOKEN_SHAPE)])


def ssm_proj_fwd(x, norm, w_in):
    t = x.shape[0]
    tm = min(ROW_TILE_FWD, t)

    def body(x_ref, g_ref, w_ref, u_ref, gate_ref):
        _, _, h = _rms(x_ref[...], g_ref[...])
        h = h.astype(MXU_DTYPE)
        half = BRANCH // 2
        for j in range(2):
            u_ref[:, j * half:(j + 1) * half] = _mm(h, w_ref[j])
            gate_ref[:, j * half:(j + 1) * half] = _mm(h, w_ref[2 + j])

    return pl.pallas_call(
        body, name="ssm_proj_fwd", grid=(t // tm,),
        in_specs=[_rows(tm, D_MODEL), _whole((1, D_MODEL)), _whole((4, D_MODEL, BRANCH // 2))],
        out_specs=[_rows(tm, BRANCH), _rows(tm, BRANCH)],
        out_shape=[_sds((t, BRANCH)), _sds((t, BRANCH))],
        compiler_params=_params(("parallel",)),
    )(x, norm, w_in)


def _chunk_rows(ref, nk, dtype=None):
    rows = jnp.concatenate([ref[pl.ds(s, nk, stride=S5_CHUNK), :] for s in range(S5_CHUNK)], axis=1)
    return rows.astype(MXU_DTYPE if dtype is None else dtype)


def _store_chunk_rows(ref, val, nk):
    for s in range(S5_CHUNK):
        ref[pl.ds(s, nk, stride=S5_CHUNK), :] = val[:, s * LANES:(s + 1) * LANES]


def _own_group_mask():
    row = lax.broadcasted_iota(jnp.int32, (S5_OCT_IN, S5_OCT_STATE), 0)
    col = lax.broadcasted_iota(jnp.int32, (S5_OCT_IN, S5_OCT_STATE), 1)
    return ((row % LANES) // SSM_GROUP) == (col // SSM_STATE)


def _spread_groups(w):
    return jnp.where(_own_group_mask(), jnp.concatenate([w] * (S5_OCT_STATE // LANES), axis=1), 0.0).astype(MXU_DTYPE)


def _fold_groups(p):
    p = jnp.where(_own_group_mask(), p, 0.0)
    return sum(p[:, q * LANES:(q + 1) * LANES] for q in range(S5_OCT_STATE // LANES))


def _fill_toeplitz(win_ref, kd_ref):
    win_ref[...] = jnp.zeros_like(win_ref)
    for s in range(S5_CHUNK):
        for t in range(s, S5_CHUNK):
            win_ref[s * LANES:(s + 1) * LANES, t * LANES:(t + 1) * LANES] = kd_ref[t - s].astype(MXU_DTYPE)


TOEPLITZ_BLOCK = 512
_TOEPLITZ_BLOCKS = [(lo, lo + TOEPLITZ_BLOCK) for lo in range(0, S5_OCT_IN, TOEPLITZ_BLOCK)]


def _strip(t):
    return pl.BlockSpec((t, LANES), lambda b: (0, b))


def _oct_states(nk):
    return pl.BlockSpec((nk, S5_OCT_STATE), lambda b: (0, b))


OCT_W = pl.BlockSpec((None, S5_OCT_IN, LANES), lambda b: (b, 0, 0))
OCT_KD = pl.BlockSpec((None, S5_CHUNK, LANES, LANES), lambda b: (b, 0, 0, 0))


def s5_chunk_states(u, ws_re, ws_im):
    t = u.shape[0]
    nk = t // S5_CHUNK

    def body(u_ref, wr_ref, wi_ref, re_ref, im_ref):
        uc = _chunk_rows(u_ref, nk)
        re_ref[...] = _mm(uc, _spread_groups(wr_ref[...]))
        im_ref[...] = _mm(uc, _spread_groups(wi_ref[...]))

    return pl.pallas_call(
        body, name="s5_chunk_states", grid=(S5_OCTETS,),
        in_specs=[_strip(t), OCT_W, OCT_W], out_specs=[_oct_states(nk), _oct_states(nk)],
        out_shape=[_sds((nk, S5_STATES)), _sds((nk, S5_STATES))],
        compiler_params=_params(("parallel",)),
    )(u, ws_re, ws_im)


def s5_scan_fwd(s_re, s_im, a_re, a_im):
    nk = s_re.shape[0]

    def body(sre_ref, sim_ref, ar_ref, ai_ref, hre_ref, him_ref):
        ar = ar_ref[...]
        ai = ai_ref[...]

        def step(k, carry):
            hr, hi = carry
            hre_ref[pl.ds(k, 1), :] = hr
            him_ref[pl.ds(k, 1), :] = hi
            sr = sre_ref[pl.ds(k, 1), :]
            si = sim_ref[pl.ds(k, 1), :]
            return ar * hr - ai * hi + sr, ai * hr + ar * hi + si

        zero = jnp.zeros((1, S5_STATES), F32)
        lax.fori_loop(0, nk, step, (zero, zero))

    vm = pl.BlockSpec(memory_space=pltpu.VMEM)
    return pl.pallas_call(
        body, name="s5_scan_fwd", in_specs=[vm, vm, vm, vm], out_specs=[vm, vm],
        out_shape=[_sds((nk, S5_STATES)), _sds((nk, S5_STATES))],
        compiler_params=_params(),
    )(s_re, s_im, a_re, a_im)


def s5_outputs(u, h_re, h_im, kd, wo_re, wo_im):
    t = u.shape[0]
    nk = t // S5_CHUNK

    def body(u_ref, hre_ref, him_ref, kd_ref, wor_ref, woi_ref, y_ref, win_ref):
        _fill_toeplitz(win_ref, kd_ref)
        uc = _chunk_rows(u_ref, nk)
        y = jnp.concatenate([_mm(uc[:, :hi], win_ref[:hi, lo:hi]) for lo, hi in _TOEPLITZ_BLOCKS], axis=1)
        y = y + _mm_nt(hre_ref[...], _spread_groups(wor_ref[...])) + _mm_nt(him_ref[...], _spread_groups(woi_ref[...]))
        _store_chunk_rows(y_ref, y, nk)

    return pl.pallas_call(
        body, name="s5_outputs", grid=(S5_OCTETS,),
        in_specs=[_strip(t), _oct_states(nk), _oct_states(nk), OCT_KD, OCT_W, OCT_W],
        out_specs=_strip(t), out_shape=_sds((t, BRANCH)),
        scratch_shapes=[pltpu.VMEM((S5_OCT_IN, S5_OCT_IN), MXU_DTYPE)],
        compiler_params=_params(("parallel",)),
    )(u, h_re, h_im, kd, wo_re, wo_im)


def s5_state_grads(dy, wo_re, wo_im, token=None):
    t = dy.shape[0]
    nk = t // S5_CHUNK
    extra, extra_specs = _after(token)

    def body(dy_ref, wor_ref, woi_ref, *rest):
        re_ref, im_ref = rest[-2:]
        dyc = _chunk_rows(dy_ref, nk)
        re_ref[...] = _mm(dyc, _spread_groups(wor_ref[...]))
        im_ref[...] = _mm(dyc, _spread_groups(woi_ref[...]))

    return pl.pallas_call(
        body, name="s5_state_grads", grid=(S5_OCTETS,),
        in_specs=[_strip(t), OCT_W, OCT_W] + extra_specs, out_specs=[_oct_states(nk), _oct_states(nk)],
        out_shape=[_sds((nk, S5_STATES)), _sds((nk, S5_STATES))],
        compiler_params=_params(("parallel",)),
    )(dy, wo_re, wo_im, *extra)


def s5_scan_bwd(dh_re, dh_im, h_re, h_im, a_re, a_im):
    nk = dh_re.shape[0]

    def body(dhr_ref, dhi_ref, hr_ref, hi_ref, ar_ref, ai_ref, dsr_ref, dsi_ref, dar_ref, dai_ref):
        ar = ar_ref[...]
        ai = ai_ref[...]

        def step(i, carry):
            gr, gi = carry
            k = nk - 1 - i
            dhr = dhr_ref[pl.ds(k, 1), :]
            dhi = dhi_ref[pl.ds(k, 1), :]
            dsr_ref[pl.ds(k, 1), :] = gr
            dsi_ref[pl.ds(k, 1), :] = gi
            return dhr + ar * gr + ai * gi, dhi - ai * gr + ar * gi

        zero = jnp.zeros((1, S5_STATES), F32)
        lax.fori_loop(0, nk, step, (zero, zero))
        dsr, dsi, hr, hi = dsr_ref[...], dsi_ref[...], hr_ref[...], hi_ref[...]
        dar_ref[...] = jnp.sum(dsr * hr + dsi * hi, axis=0, keepdims=True)
        dai_ref[...] = jnp.sum(dsi * hr - dsr * hi, axis=0, keepdims=True)

    vm = pl.BlockSpec(memory_space=pltpu.VMEM)
    return pl.pallas_call(
        body, name="s5_scan_bwd", in_specs=[vm] * 6, out_specs=[vm] * 4,
        out_shape=[_sds((nk, S5_STATES)), _sds((nk, S5_STATES)), _sds((1, S5_STATES)), _sds((1, S5_STATES))],
        input_output_aliases={0: 0, 1: 1}, compiler_params=_params(),
    )(dh_re, dh_im, h_re, h_im, a_re, a_im)


def s5_input_grads(dy, ds_re, ds_im, kd, ws_re, ws_im):
    t = dy.shape[0]
    nk = t // S5_CHUNK

    def body(dy_ref, dsr_ref, dsi_ref, kd_ref, wsr_ref, wsi_ref, du_ref, win_ref):
        _fill_toeplitz(win_ref, kd_ref)
        dyc = _chunk_rows(dy_ref, nk)
        du = jnp.concatenate([_mm_nt(dyc[:, lo:], win_ref[lo:hi, lo:]) for lo, hi in _TOEPLITZ_BLOCKS], axis=1)
        du = du + _mm_nt(dsr_ref[...], _spread_groups(wsr_ref[...])) + _mm_nt(dsi_ref[...], _spread_groups(wsi_ref[...]))
        _store_chunk_rows(du_ref, du, nk)

    return pl.pallas_call(
        body, name="s5_input_grads", grid=(S5_OCTETS,),
        in_specs=[_strip(t), _oct_states(nk), _oct_states(nk), OCT_KD, OCT_W, OCT_W],
        out_specs=_strip(t), out_shape=_sds((t, BRANCH)),
        scratch_shapes=[pltpu.VMEM((S5_OCT_IN, S5_OCT_IN), MXU_DTYPE)],
        compiler_params=_params(("parallel",)),
    )(dy, ds_re, ds_im, kd, ws_re, ws_im)


def s5_weight_grads(u, dy, h_re, h_im, ds_re, ds_im):
    t = u.shape[0]
    nk = t // S5_CHUNK

    def body(u_ref, dy_ref, hre_ref, him_ref, dsr_ref, dsi_ref, dkd_ref, dwsr_ref, dwsi_ref, dwor_ref, dwoi_ref):
        dyc = _chunk_rows(dy_ref, nk, F32)
        uct = _chunk_rows(u_ref, nk, F32).T.astype(MXU_DTYPE)
        dyct = dyc.T.astype(MXU_DTYPE)
        dyc = dyc.astype(MXU_DTYPE)
        dwsr_ref[...] = _fold_groups(_mm(uct, dsr_ref[...]))
        dwsi_ref[...] = _fold_groups(_mm(uct, dsi_ref[...]))
        dwor_ref[...] = _fold_groups(_mm(dyct, hre_ref[...]))
        dwoi_ref[...] = _fold_groups(_mm(dyct, him_ref[...]))
        dkd_ref[...] = jnp.zeros_like(dkd_ref)
        for tt in range(0, S5_CHUNK, 2):
            p = _mm(uct[:(tt + 2) * LANES], dyc[:, tt * LANES:(tt + 2) * LANES])
            for s in range(tt + 2):
                rows = p[s * LANES:(s + 1) * LANES]
                if s <= tt:
                    dkd_ref[tt - s] += rows[:, :LANES]
                dkd_ref[tt + 1 - s] += rows[:, LANES:]

    return pl.pallas_call(
        body, name="s5_weight_grads", grid=(S5_OCTETS,),
        in_specs=[_strip(t), _strip(t)] + [_oct_states(nk)] * 4,
        out_specs=[OCT_KD, OCT_W, OCT_W, OCT_W, OCT_W],
        out_shape=[_sds((S5_OCTETS, S5_CHUNK, LANES, LANES))] + [_sds((S5_OCTETS, S5_OCT_IN, LANES))] * 4,
        compiler_params=_params(("parallel",)),
    )(u, dy, h_re, h_im, ds_re, ds_im)


def ssm_mix_fwd(x, u, gate, y_scan, d, w_glu, b_glu, w_out):
    t = x.shape[0]
    tm = min(ROW_TILE_FWD, t)

    def body(x_ref, u_ref, gate_ref, ys_ref, d_ref, wg_ref, bg_ref, wo_ref, y_ref, g2_ref, xo_ref):
        y = ys_ref[...] + d_ref[...] * u_ref[...]
        z0 = _gelu(y)
        g2 = _mm(z0, wg_ref[...]) + bg_ref[...]
        a = z0 * _sigmoid(g2) * _silu(gate_ref[...])
        y_ref[...] = y
        g2_ref[...] = g2
        xo_ref[...] = x_ref[...] + _mm(a, wo_ref[...])

    row = _rows(tm, BRANCH)
    vec = _whole((1, BRANCH))
    mat = _whole((BRANCH, BRANCH))
    return pl.pallas_call(
        body, name="ssm_mix_fwd", grid=(t // tm,),
        in_specs=[row, row, row, row, vec, mat, vec, mat],
        out_specs=[row, row, row],
        out_shape=[_sds((t, BRANCH))] * 3,
        compiler_params=_params(("parallel",)),
    )(x, u, gate, y_scan, d, w_glu, b_glu, w_out)


def ssm_mix_bwd(dxo, u, gate, y, g2, w_glu, w_out, token=None):
    t = dxo.shape[0]
    tm = min(ROW_TILE_BWD, t)
    extra, extra_specs = _after(token)

    def body(dxo_ref, u_ref, gate_ref, y_ref, g2_ref, wgt_ref, wot_ref, *rest):
        dy_ref, dgate_ref, dwo_ref, dwg_ref, dbg_ref, dd_ref = rest[-6:]

        @pl.when(pl.program_id(0) == 0)
        def _():
            dwo_ref[...] = jnp.zeros_like(dwo_ref)
            dwg_ref[...] = jnp.zeros_like(dwg_ref)
            dbg_ref[...] = jnp.zeros_like(dbg_ref)
            dd_ref[...] = jnp.zeros_like(dd_ref)

        dxo = dxo_ref[...]
        gate = gate_ref[...]
        y = y_ref[...]
        z0, z0_grad = _gelu_and_grad(y)
        sg = _sigmoid(g2_ref[...])
        z = z0 * sg
        sgate, sgate_grad = _silu_and_grad(gate)
        da = _mm_nt(dxo, wot_ref[...])
        dwo_ref[...] += _mm_tn(z * sgate, dxo)
        dz = da * sgate
        dgate_ref[...] = da * z * sgate_grad
        dg2 = dz * z0 * sg * (1.0 - sg)
        dbg_ref[...] += jnp.sum(dg2, axis=0, keepdims=True)
        dwg_ref[...] += _mm_tn(z0, dg2)
        dz0 = dz * sg + _mm_nt(dg2, wgt_ref[...])
        dy = dz0 * z0_grad
        dd_ref[...] += jnp.sum(dy * u_ref[...], axis=0, keepdims=True)
        dy_ref[...] = dy

    row = _rows(tm, BRANCH)
    vec = _whole((1, BRANCH))
    mat = _whole((BRANCH, BRANCH))
    return pl.pallas_call(
        body, name="ssm_mix_bwd", grid=(t // tm,),
        in_specs=[row, row, row, row, row, mat, mat] + extra_specs,
        out_specs=[row, row, mat, mat, vec, vec],
        out_shape=[_sds((t, BRANCH)), _sds((t, BRANCH)), _sds((BRANCH, D_MODEL)), _sds((BRANCH, BRANCH)),
                   _sds((1, BRANCH)), _sds((1, BRANCH))],
        compiler_params=_params(("arbitrary",)),
    )(dxo, u, gate, y, g2, w_glu, w_out, *extra)


def ssm_proj_bwd(x, norm, dxo, dy, du_scan, dgate, d, w_in):
    t = x.shape[0]
    tm = min(ROW_TILE_BWD, t)
    n = 2 * BRANCH

    def body(x_ref, g_ref, dxo_ref, dy_ref, dus_ref, dgate_ref, d_ref, wt_ref, dx_ref, dw_ref, dg_ref):
        @pl.when(pl.program_id(0) == 0)
        def _():
            dw_ref[...] = jnp.zeros_like(dw_ref)
            dg_ref[...] = jnp.zeros_like(dg_ref)

        g = g_ref[...]
        r, xhat, h = _rms(x_ref[...], g)
        h = h.astype(MXU_DTYPE)
        du = dus_ref[...] + d_ref[...] * dy_ref[...]
        dproj = jnp.concatenate([du, dgate_ref[...]], axis=1).astype(MXU_DTYPE)
        dh = jnp.zeros((tm, D_MODEL), F32)
        for j in range(4):
            cols = dproj[:, j * (n // 4):(j + 1) * (n // 4)]
            dh = dh + _mm_nt(cols, wt_ref[j])
            dw_ref[j] += _mm_tn(h, cols)
        dx, dg = _rms_bwd(dh, g, r, xhat)
        dg_ref[...] += dg
        dx_ref[...] = dxo_ref[...] + dx

    row = _rows(tm, D_MODEL)
    vec = _whole((1, D_MODEL))
    blocks = _whole((4, D_MODEL, n // 4))
    return pl.pallas_call(
        body, name="ssm_proj_bwd", grid=(t // tm,),
        in_specs=[row, vec, row, row, row, row, vec, blocks],
        out_specs=[row, blocks, vec],
        out_shape=[_sds((t, D_MODEL)), _sds((4, D_MODEL, n // 4)), _sds((1, D_MODEL))],
        compiler_params=_params(("arbitrary",)),
    )(x, norm, dxo, dy, du_scan, dgate, d, w_in)


ATTN_N = Q_DIM + 2 * KV_DIM + BRANCH


def attn_proj_fwd(x, norm, w_in_t, cos2, sin2):
    t = x.shape[0]
    tm = min(ROW_TILE_FWD, t)

    def body(x_ref, g_ref, w_ref, cos_ref, sin_ref, q_ref, k_ref, v_ref, gate_ref):
        _, _, h = _rms(x_ref[...], g_ref[...])
        p = _mm_nt(h, w_ref[...])
        cs = cos_ref[...]
        sn = sin_ref[...]
        q = p[:, :Q_DIM]
        k = p[:, Q_DIM:Q_DIM + KV_DIM]
        q_ref[...] = q * _tile_lanes(cs, Q_DIM // LANES) + _swap_half_heads(q) * _tile_lanes(sn, Q_DIM // LANES)
        k_ref[...] = k * cs + _swap_half_heads(k) * sn
        v_ref[...] = p[:, Q_DIM + KV_DIM:Q_DIM + 2 * KV_DIM]
        gate_ref[...] = p[:, Q_DIM + 2 * KV_DIM:]

    return pl.pallas_call(
        body, name="attn_proj_fwd", grid=(t // tm,),
        in_specs=[_rows(tm, D_MODEL), _whole((1, D_MODEL)), _whole((ATTN_N, D_MODEL)), _rows(tm, LANES), _rows(tm, LANES)],
        out_specs=[_rows(tm, Q_DIM), _rows(tm, KV_DIM), _rows(tm, KV_DIM), _rows(tm, BRANCH)],
        out_shape=[_sds((t, Q_DIM)), _sds((t, KV_DIM)), _sds((t, KV_DIM)), _sds((t, BRANCH))],
        compiler_params=_params(("parallel",)),
    )(x, norm, w_in_t, cos2, sin2)


GQA_LANES = GQA_GROUP * ATTN_BLOCK


def _window_masks(first_block):
    kj = lax.broadcasted_iota(jnp.int32, (ATTN_BLOCK, GQA_LANES), 0)
    qi = lax.broadcasted_iota(jnp.int32, (ATTN_BLOCK, GQA_LANES), 1) % ATTN_BLOCK
    return kj > qi, kj > jnp.where(first_block, qi, ATTN_BLOCK)


def _fold(upper, both):
    return jnp.where(upper, both[:ATTN_BLOCK], both[ATTN_BLOCK:])


def _unfold(upper, tile):
    return jnp.concatenate([jnp.where(upper, tile, 0.0), jnp.where(upper, 0.0, tile)], axis=0).astype(MXU_DTYPE)


def _stack_heads(ref, group):
    return jnp.concatenate([ref[:, h * HEAD_DIM:(h + 1) * HEAD_DIM] for h in range(group * GQA_GROUP, (group + 1) * GQA_GROUP)], axis=0)


def _unstack_heads(ref, group, stacked):
    for n in range(GQA_GROUP):
        h = group * GQA_GROUP + n
        ref[:, h * HEAD_DIM:(h + 1) * HEAD_DIM] = stacked[n * ATTN_BLOCK:(n + 1) * ATTN_BLOCK]


def _sink_row(sink_ref, group):
    return jnp.concatenate([jnp.full((1, ATTN_BLOCK), sink_ref[group * GQA_GROUP + n], F32) for n in range(GQA_GROUP)], axis=1)


def _lane_is(h):
    return lax.broadcasted_iota(jnp.int32, (1, LANES), 1) == h


def attn_fwd(q, k, v, sinks):
    t = q.shape[0]
    nb = t // ATTN_BLOCK
    scale = HEAD_DIM ** -0.5

    def body(sink_ref, q_ref, kc_ref, kp_ref, vc_ref, vp_ref, o_ref, lse_ref):
        keys = jnp.concatenate([kp_ref[...], kc_ref[...]], axis=0).astype(MXU_DTYPE)
        vals = jnp.concatenate([vp_ref[...], vc_ref[...]], axis=0).astype(MXU_DTYPE)
        upper, dead = _window_masks(pl.program_id(0) == 0)
        for g in range(N_KV_HEADS):
            kv = slice(g * HEAD_DIM, (g + 1) * HEAD_DIM)
            qs = _stack_heads(q_ref, g) * scale
            s = jnp.where(dead, NEG_INF, _fold(upper, _mm_nt(keys[:, kv], qs)))
            sink = _sink_row(sink_ref, g)
            m = jnp.maximum(jnp.max(s, axis=0, keepdims=True), sink)
            p = jnp.exp(s - m)
            den = jnp.sum(p, axis=0, keepdims=True) + jnp.exp(sink - m)
            _unstack_heads(o_ref, g, _mm_tn(_unfold(upper, p * (1.0 / den)), vals[:, kv]))
            lse = m + jnp.log(den)
            for n in range(GQA_GROUP):
                lse_ref[pl.ds(g * GQA_GROUP + n, 1), :] = lse[:, n * ATTN_BLOCK:(n + 1) * ATTN_BLOCK]

    cur = lambda n: pl.BlockSpec((ATTN_BLOCK, n), lambda i: (i, 0))
    prev = lambda n: pl.BlockSpec((ATTN_BLOCK, n), lambda i: (jnp.maximum(i - 1, 0), 0))
    return pl.pallas_call(
        body, name="attn_fwd", grid=(nb,),
        in_specs=[pl.BlockSpec(memory_space=pltpu.SMEM), cur(Q_DIM), cur(KV_DIM), prev(KV_DIM), cur(KV_DIM), prev(KV_DIM)],
        out_specs=[cur(Q_DIM), pl.BlockSpec((N_Q_HEADS, ATTN_BLOCK), lambda i: (0, i))],
        out_shape=[_sds((t, Q_DIM)), _sds((N_Q_HEADS, t))],
        compiler_params=_params(("parallel",)),
    )(sinks, q, k, k, v, v)


def attn_bwd(q, k, v, sinks, o, lse, do):
    t = q.shape[0]
    nb = t // ATTN_BLOCK
    scale = HEAD_DIM ** -0.5

    def body(sink_ref, q_ref, kc_ref, kp_ref, vc_ref, vp_ref, o_ref, lse_ref, do_ref,
             dq_ref, dk_ref, dv_ref, dsink_ref, dk_carry, dv_carry):
        i = pl.program_id(0)

        @pl.when(i == 0)
        def _():
            dsink_ref[...] = jnp.zeros_like(dsink_ref)
            dk_carry[...] = jnp.zeros_like(dk_carry)
            dv_carry[...] = jnp.zeros_like(dv_carry)

        @pl.when(i < nb)
        def _():
            keys = jnp.concatenate([kp_ref[...], kc_ref[...]], axis=0).astype(MXU_DTYPE)
            vals = jnp.concatenate([vp_ref[...], vc_ref[...]], axis=0).astype(MXU_DTYPE)
            upper, dead = _window_masks(i == 0)
            dsink = jnp.zeros((1, LANES), F32)
            dk_heads = []
            dv_heads = []
            for g in range(N_KV_HEADS):
                kv = slice(g * HEAD_DIM, (g + 1) * HEAD_DIM)
                qs = (_stack_heads(q_ref, g) * scale).astype(MXU_DTYPE)
                dos = _stack_heads(do_ref, g)
                lse = jnp.concatenate([lse_ref[pl.ds(g * GQA_GROUP + n, 1), :] for n in range(GQA_GROUP)], axis=1)
                s = jnp.where(dead, NEG_INF, _fold(upper, _mm_nt(keys[:, kv], qs)))
                p = jnp.exp(s - lse)
                delta = _mm_f32(jnp.ones((8, HEAD_DIM), F32), dos * _stack_heads(o_ref, g), ((1,), (1,)))[:1]
                dos = dos.astype(MXU_DTYPE)
                ds = _unfold(upper, p * (_fold(upper, _mm_nt(vals[:, kv], dos)) - delta))
                _unstack_heads(dq_ref, g, _mm_tn(ds, keys[:, kv]) * scale)
                dk_heads.append(_mm(ds, qs))
                dv_heads.append(_mm(_unfold(upper, p), dos))
                at_sink = jnp.exp(_sink_row(sink_ref, g) - lse) * delta
                for n in range(GQA_GROUP):
                    dsink = dsink + jnp.where(_lane_is(g * GQA_GROUP + n), -jnp.sum(at_sink[:, n * ATTN_BLOCK:(n + 1) * ATTN_BLOCK]), 0.0)
            dkk = jnp.concatenate(dk_heads, axis=1)
            dvv = jnp.concatenate(dv_heads, axis=1)
            dsink_ref[...] += dsink
            dk_ref[...] = dk_carry[...] + dkk[:ATTN_BLOCK]
            dv_ref[...] = dv_carry[...] + dvv[:ATTN_BLOCK]
            dk_carry[...] = dkk[ATTN_BLOCK:]
            dv_carry[...] = dvv[ATTN_BLOCK:]

        @pl.when(i == nb)
        def _():
            dk_ref[...] = dk_carry[...]
            dv_ref[...] = dv_carry[...]

    last = nb - 1
    cur = lambda n: pl.BlockSpec((ATTN_BLOCK, n), lambda i: (jnp.minimum(i, last), 0))
    prev = lambda n: pl.BlockSpec((ATTN_BLOCK, n), lambda i: (jnp.clip(i - 1, 0, last), 0))
    late = lambda n: pl.BlockSpec((ATTN_BLOCK, n), lambda i: (i, 0))
    dq, dk_late, dv_late, dsinks = pl.pallas_call(
        body, name="attn_bwd", grid=(nb + 1,),
        in_specs=[pl.BlockSpec(memory_space=pltpu.SMEM), cur(Q_DIM), cur(KV_DIM), prev(KV_DIM), cur(KV_DIM), prev(KV_DIM),
                  cur(Q_DIM), pl.BlockSpec((N_Q_HEADS, ATTN_BLOCK), lambda i: (0, jnp.minimum(i, last))), cur(Q_DIM)],
        out_specs=[cur(Q_DIM), late(KV_DIM), late(KV_DIM), _whole((1, LANES))],
        out_shape=[_sds((t, Q_DIM)), _sds((t + ATTN_BLOCK, KV_DIM)), _sds((t + ATTN_BLOCK, KV_DIM)), _sds((1, LANES))],
        scratch_shapes=[pltpu.VMEM((ATTN_BLOCK, KV_DIM), F32), pltpu.VMEM((ATTN_BLOCK, KV_DIM), F32)],
        compiler_params=_params(("arbitrary",)),
    )(sinks, q, k, k, v, v, o, lse, do)
    return dq, dk_late[ATTN_BLOCK:], dv_late[ATTN_BLOCK:], dsinks


def attn_out_fwd(x, o, gate, w_out):
    t = x.shape[0]
    tm = min(ROW_TILE_FWD, t)

    def body(x_ref, o_ref, gate_ref, w_ref, xo_ref):
        xo_ref[...] = x_ref[...] + _mm(o_ref[...] * _silu(gate_ref[...]), w_ref[...])

    row = _rows(tm, D_MODEL)
    return pl.pallas_call(
        body, name="attn_out_fwd", grid=(t // tm,),
        in_specs=[row, row, row, _whole((Q_DIM, D_MODEL))], out_specs=row, out_shape=_sds((t, D_MODEL)),
        compiler_params=_params(("parallel",)),
    )(x, o, gate, w_out)


def attn_out_bwd(dxo, o, gate, w_out, token=None):
    t = dxo.shape[0]
    tm = min(ROW_TILE_BWD, t)
    extra, extra_specs = _after(token)

    def body(dxo_ref, o_ref, gate_ref, wt_ref, *rest):
        do_ref, dgate_ref, dw_ref = rest[-3:]

        @pl.when(pl.program_id(0) == 0)
        def _():
            dw_ref[...] = jnp.zeros_like(dw_ref)

        dxo = dxo_ref[...]
        o = o_ref[...]
        gate = gate_ref[...]
        sgate, sgate_grad = _silu_and_grad(gate)
        da = _mm_nt(dxo, wt_ref[...])
        dw_ref[...] += _mm_tn(o * sgate, dxo)
        do_ref[...] = da * sgate
        dgate_ref[...] = da * o * sgate_grad

    row = _rows(tm, D_MODEL)
    mat = _whole((Q_DIM, D_MODEL))
    return pl.pallas_call(
        body, name="attn_out_bwd", grid=(t // tm,),
        in_specs=[row, row, row, mat] + extra_specs, out_specs=[row, row, mat],
        out_shape=[_sds((t, Q_DIM)), _sds((t, BRANCH)), _sds((Q_DIM, D_MODEL))],
        compiler_params=_params(("arbitrary",)),
    )(dxo, o, gate, w_out, *extra)


def attn_proj_bwd(x, norm, dxo, dq, dk, dv, dgate, cos2, sin2, w_in_t):
    t = x.shape[0]
    tm = min(ROW_TILE_BWD, t)

    def body(x_ref, g_ref, dxo_ref, dq_ref, dk_ref, dv_ref, dgate_ref, cos_ref, sin_ref, wt_ref, dx_ref, dw_ref, dg_ref):
        @pl.when(pl.program_id(0) == 0)
        def _():
            dw_ref[...] = jnp.zeros_like(dw_ref)
            dg_ref[...] = jnp.zeros_like(dg_ref)

        g = g_ref[...]
        r, xhat, h = _rms(x_ref[...], g)
        cs = cos_ref[...]
        sn = sin_ref[...]
        dqr = dq_ref[...]
        dkr = dk_ref[...]
        dq = dqr * _tile_lanes(cs, Q_DIM // LANES) + _swap_half_heads(dqr * _tile_lanes(sn, Q_DIM // LANES))
        dk = dkr * cs + _swap_half_heads(dkr * sn)
        dproj = jnp.concatenate([dq, dk, dv_ref[...], dgate_ref[...]], axis=1)
        dh = _mm(dproj, wt_ref[...])
        dw_ref[...] += _mm_tn(dproj, h)
        dx, dg = _rms_bwd(dh, g, r, xhat)
        dg_ref[...] += dg
        dx_ref[...] = dxo_ref[...] + dx

    row = _rows(tm, D_MODEL)
    vec = _whole((1, D_MODEL))
    return pl.pallas_call(
        body, name="attn_proj_bwd", grid=(t // tm,),
        in_specs=[row, vec, row, _rows(tm, Q_DIM), _rows(tm, KV_DIM), _rows(tm, KV_DIM), _rows(tm, BRANCH),
                  _rows(tm, LANES), _rows(tm, LANES), _whole((ATTN_N, D_MODEL))],
        out_specs=[row, _whole((ATTN_N, D_MODEL)), vec],
        out_shape=[_sds((t, D_MODEL)), _sds((ATTN_N, D_MODEL)), _sds((1, D_MODEL))],
        compiler_params=_params(("arbitrary",)),
    )(x, norm, dxo, dq, dk, dv, dgate, cos2, sin2, w_in_t)


def attn_out_loss(x, o, gate, w_out, norm, target):
    t = x.shape[0]
    tm = min(ROW_TILE_FWD, t)

    def body(x_ref, o_ref, gate_ref, w_ref, g_ref, tgt_ref, loss_ref, dx_ref, dg_ref):
        @pl.when(pl.program_id(0) == 0)
        def _():
            loss_ref[...] = jnp.zeros_like(loss_ref)
            dg_ref[...] = jnp.zeros_like(dg_ref)

        out = x_ref[...] + _mm(o_ref[...] * _silu(gate_ref[...]), w_ref[...])
        g = g_ref[...]
        r, xhat, y = _rms(out, g)
        err = y - tgt_ref[...]
        loss_ref[...] += 0.5 * jnp.sum(jnp.mean(err * err, axis=-1, keepdims=True), axis=0, keepdims=True)
        dx, dg = _rms_bwd(err * (1.0 / D_MODEL), g, r, xhat)
        dg_ref[...] += dg
        dx_ref[...] = dx

    row = _rows(tm, D_MODEL)
    vec = _whole((1, D_MODEL))
    return pl.pallas_call(
        body, name="attn_out_loss", grid=(t // tm,),
        in_specs=[row, row, row, _whole((Q_DIM, D_MODEL)), vec, row], out_specs=[_whole((1, 1)), row, vec],
        out_shape=[_sds((1, 1)), _sds((t, D_MODEL)), _sds((1, D_MODEL))],
        compiler_params=_params(("arbitrary",)),
    )(x, o, gate, w_out, norm, target)


OCT_TILE = pl.BlockSpec((None, LANES, LANES), lambda b: (b, 0, 0))
N_LAGS = S5_CHUNK + 1


def _cmul(ar, ai, br, bi):
    return ar * br - ai * bi, ar * bi + ai * br


def _cmul_conj(ar, ai, br, bi):
    return ar * br + ai * bi, ar * bi - ai * br


def _mm_f32(a, b, dims):
    return lax.dot_general(a, b, (dims, ((), ())), precision=lax.Precision.HIGH, preferred_element_type=F32)


def _s5_discretise(ar, ai, ls, br, bi):
    dt = jnp.exp(ls)
    xr = ar * dt
    xi = ai * dt
    mag = jnp.exp(xr)
    first = (mag * jnp.cos(xi), mag * jnp.sin(xi))
    powers = [(jnp.ones_like(xr), jnp.zeros_like(xr)), first]
    for _ in range(2, N_LAGS):
        powers.append(_cmul(*powers[-1], *first))
    den = ar * ar + ai * ai
    nr = powers[1][0] - 1.0
    ni = powers[1][1]
    fr = (nr * ar + ni * ai) / den
    fi = (ni * ar - nr * ai) / den
    bbr, bbi = _cmul(fr, fi, br, bi)
    return dt, powers, (fr, fi), (bbr, bbi), den


def _same_group_tile():
    row = lax.broadcasted_iota(jnp.int32, (LANES, LANES), 0)
    col = lax.broadcasted_iota(jnp.int32, (LANES, LANES), 1)
    return (row // SSM_GROUP) == (col // SSM_GROUP)


def _first_copy_lanes():
    return lax.broadcasted_iota(jnp.int32, (LANES, LANES), 1) < SSM_STATE


def s5_param_fwd(tiles, token=None):
    extra, extra_specs = _after(token)

    def body(ar_ref, ai_ref, ls_ref, br_ref, bi_ref, cr_ref, ci_ref, *rest):
        kd_ref, wsr_ref, wsi_ref, wor_ref, woi_ref, pr_ref, pi_ref = rest[-7:]
        cr = cr_ref[...]
        ci = ci_ref[...]
        _, powers, _, (bbr, bbi), _ = _s5_discretise(ar_ref[...], ai_ref[...], ls_ref[...], br_ref[...], bi_ref[...])
        once = _first_copy_lanes()
        crm = jnp.where(once, cr, 0.0)
        cim = jnp.where(once, ci, 0.0)
        same = _same_group_tile()
        for lag in range(S5_CHUNK):
            er, ei = powers[lag]
            xr, xi = _cmul(er, ei, bbr, bbi)
            rows = pl.ds((S5_CHUNK - 1 - lag) * LANES, LANES)
            wsr_ref[rows, :] = xr
            wsi_ref[rows, :] = xi
        k = _mm_f32(wsr_ref[...], crm, ((1,), (1,))) - _mm_f32(wsi_ref[...], cim, ((1,), (1,)))
        for lag in range(S5_CHUNK):
            kd_ref[lag] = jnp.where(same, k[(S5_CHUNK - 1 - lag) * LANES:(S5_CHUNK - lag) * LANES], 0.0)
        for t in range(S5_CHUNK):
            er, ei = powers[t + 1]
            zr, zi = _cmul(er, ei, cr, ci)
            wor_ref[pl.ds(t * LANES, LANES), :] = zr
            woi_ref[pl.ds(t * LANES, LANES), :] = -zi
        pr_ref[...] = powers[S5_CHUNK][0]
        pi_ref[...] = powers[S5_CHUNK][1]

    return pl.pallas_call(
        body, name="s5_param_fwd", grid=(S5_OCTETS,),
        in_specs=[OCT_TILE] * 7 + [ANY] * len(extra),
        out_specs=[OCT_KD, OCT_W, OCT_W, OCT_W, OCT_W, OCT_TILE, OCT_TILE],
        out_shape=[_sds((S5_OCTETS, S5_CHUNK, LANES, LANES))] + [_sds((S5_OCTETS, S5_OCT_IN, LANES))] * 4
                  + [_sds((S5_OCTETS, LANES, LANES))] * 2,
        compiler_params=_params(("parallel",)),
    )(*tiles, *extra)


def s5_param_bwd(tiles, dkd, dws_re, dws_im, dwo_re, dwo_im, dp_re, dp_im):
    def body(ar_ref, ai_ref, ls_ref, br_ref, bi_ref, cr_ref, ci_ref, dkd_ref, dwsr_ref, dwsi_ref, dwor_ref, dwoi_ref, dpr_ref, dpi_ref,
             dar_ref, dai_ref, dls_ref, dbr_ref, dbi_ref, dcr_ref, dci_ref):
        ar = ar_ref[...]
        ai = ai_ref[...]
        br = br_ref[...]
        bi = bi_ref[...]
        cr = cr_ref[...]
        ci = ci_ref[...]
        dt, powers, (fr, fi), (bbr, bbi), den = _s5_discretise(ar, ai, ls_ref[...], br, bi)
        once = _first_copy_lanes()
        crm = jnp.where(once, cr, 0.0)
        cim = jnp.where(once, ci, 0.0)
        same = _same_group_tile()
        zero = jnp.zeros((LANES, LANES), F32)
        dpow = [[zero, zero] for _ in range(N_LAGS)]
        dbbr, dbbi = zero, zero
        by_step = [S5_CHUNK - 1 - s for s in range(S5_CHUNK)]
        x_all = [_cmul(*powers[lag], bbr, bbi) for lag in by_step]
        xr_all = jnp.concatenate([x[0] for x in x_all], axis=0)
        xi_all = jnp.concatenate([x[1] for x in x_all], axis=0)
        g_all = jnp.concatenate([jnp.where(same, dkd_ref[lag], 0.0) for lag in by_step], axis=0)
        dxr_all = dwsr_ref[...] + _mm_f32(g_all, crm, ((1,), (0,)))
        dxi_all = dwsi_ref[...] - _mm_f32(g_all, cim, ((1,), (0,)))
        dcr = jnp.where(once, _mm_f32(g_all, xr_all, ((0,), (0,))), 0.0)
        dci = -jnp.where(once, _mm_f32(g_all, xi_all, ((0,), (0,))), 0.0)
        for lag in range(S5_CHUNK):
            er, ei = powers[lag]
            rows = slice((S5_CHUNK - 1 - lag) * LANES, (S5_CHUNK - lag) * LANES)
            dxr = dxr_all[rows]
            dxi = dxi_all[rows]
            a, b = _cmul_conj(bbr, bbi, dxr, dxi)
            dpow[lag][0] = dpow[lag][0] + a
            dpow[lag][1] = dpow[lag][1] + b
            a, b = _cmul_conj(er, ei, dxr, dxi)
            dbbr = dbbr + a
            dbbi = dbbi + b
        for t in range(S5_CHUNK):
            er, ei = powers[t + 1]
            dzr = dwor_ref[pl.ds(t * LANES, LANES), :]
            dzi = -dwoi_ref[pl.ds(t * LANES, LANES), :]
            a, b = _cmul_conj(cr, ci, dzr, dzi)
            dpow[t + 1][0] = dpow[t + 1][0] + a
            dpow[t + 1][1] = dpow[t + 1][1] + b
            a, b = _cmul_conj(er, ei, dzr, dzi)
            dcr = dcr + a
            dci = dci + b
        dpow[S5_CHUNK][0] = dpow[S5_CHUNK][0] + dpr_ref[...]
        dpow[S5_CHUNK][1] = dpow[S5_CHUNK][1] + dpi_ref[...]
        dfr, dfi = _cmul_conj(br, bi, dbbr, dbbi)
        dbr, dbi = _cmul_conj(fr, fi, dbbr, dbbi)
        dnr, dni = _cmul(ar / den, ai / den, dfr, dfi)
        qr = (fr * ar + fi * ai) / den
        qi = (fi * ar - fr * ai) / den
        dlr, dli = _cmul(-qr, qi, dfr, dfi)
        dpow[1][0] = dpow[1][0] + dnr
        dpow[1][1] = dpow[1][1] + dni
        dxr, dxi = zero, zero
        for lag in range(1, N_LAGS):
            a, b = _cmul_conj(powers[lag][0], powers[lag][1], dpow[lag][0], dpow[lag][1])
            dxr = dxr + lag * a
            dxi = dxi + lag * b
        dar_ref[...] = dlr + dt * dxr
        dai_ref[...] = dli + dt * dxi
        dls_ref[...] = dt * (ar * dxr + ai * dxi)
        dbr_ref[...] = dbr
        dbi_ref[...] = dbi
        dcr_ref[...] = dcr
        dci_ref[...] = dci

    return pl.pallas_call(
        body, name="s5_param_bwd", grid=(S5_OCTETS,),
        in_specs=[OCT_TILE] * 7 + [OCT_KD, OCT_W, OCT_W, OCT_W, OCT_W, OCT_TILE, OCT_TILE], out_specs=[OCT_TILE] * 7,
        out_shape=[_sds((S5_OCTETS, LANES, LANES))] * 7,
        compiler_params=_params(("parallel",)),
    )(*tiles, dkd, dws_re, dws_im, dwo_re, dwo_im, dp_re, dp_im)


def _doubled(v):
    return jnp.concatenate([v, v], axis=-1)


def _s5_param_tiles(a_re, a_im, log_step, b_re, b_im, c_re, c_im):
    def per_group(a):
        return _doubled(jnp.broadcast_to(a.reshape(S5_OCTETS, S5_OCT, 1, SSM_STATE),
                                         (S5_OCTETS, S5_OCT, SSM_GROUP, SSM_STATE)).reshape(S5_OCTETS, LANES, SSM_STATE))

    ls = jnp.broadcast_to(log_step.reshape(S5_OCTETS, S5_OCT, 1, 1), (S5_OCTETS, S5_OCT, SSM_GROUP, LANES)).reshape(S5_OCTETS, LANES, LANES)
    bt = lambda b: _doubled(b.transpose(0, 2, 1).reshape(S5_OCTETS, LANES, SSM_STATE))
    ct = lambda c: _doubled(c.reshape(S5_OCTETS, LANES, SSM_STATE))
    return [per_group(a_re), per_group(a_im), ls, bt(b_re), bt(b_im), ct(c_re), ct(c_im)]


def _s5_param_grads(dtiles):
    dar, dai, dls, dbr, dbi, dcr, dci = dtiles
    halves = lambda d: d[..., :SSM_STATE] + d[..., SSM_STATE:]
    per_group = lambda d: halves(d).reshape(SSM_GROUPS, SSM_GROUP, SSM_STATE).sum(axis=1)
    per_row = lambda d: halves(d).reshape(SSM_GROUPS, SSM_GROUP, SSM_STATE)
    return (per_group(dar), per_group(dai), dls.reshape(SSM_GROUPS, SSM_GROUP * LANES).sum(axis=1),
            per_row(dbr).transpose(0, 2, 1), per_row(dbi).transpose(0, 2, 1), per_row(dcr), per_row(dci))


def _group_power_rows(tile):
    return tile[:, ::SSM_GROUP, :SSM_STATE].reshape(1, S5_STATES)


def _group_power_tiles(row):
    t = jnp.pad(row.reshape(S5_OCTETS, S5_OCT, 1, SSM_STATE), ((0, 0), (0, 0), (0, SSM_GROUP - 1), (0, LANES - SSM_STATE)))
    return t.reshape(S5_OCTETS, LANES, LANES)


def _rope_tables(t):
    pos = jnp.arange(t, dtype=F32)
    inv_freq = ROPE_THETA ** (-jnp.arange(0, HEAD_DIM, 2, dtype=F32) / HEAD_DIM)
    ang = pos[:, None] * inv_freq[None, :]
    cos = jnp.cos(ang)
    sin = jnp.sin(ang)
    cos64 = jnp.concatenate([cos, cos], axis=1)
    sin64 = jnp.concatenate([-sin, sin], axis=1)
    return jnp.concatenate([cos64, cos64], axis=1), jnp.concatenate([sin64, sin64], axis=1)


def _row(v):
    return v.reshape(1, -1)


def _s5_matrices(w, token=None):
    tiles = _s5_param_tiles(w["a_re"], w["a_im"], w["log_step"], w["b_re"], w["b_im"], w["c_re"], w["c_im"])
    kd, ws_re, ws_im, wo_re, wo_im, p_re, p_im = s5_param_fwd(tiles, token)
    return tiles, dict(kd=kd, ws_re=ws_re, ws_im=ws_im, wo_re=wo_re, wo_im=wo_im, a_re=_group_power_rows(p_re), a_im=_group_power_rows(p_im))


def _ssm_forward(x, w):
    tiles, mats = w["s5"] if "s5" in w else _s5_matrices(w)
    u, gate = ssm_proj_fwd(x, _row(w["norm"]), w["w_in"])
    s_re, s_im = s5_chunk_states(u, mats["ws_re"], mats["ws_im"])
    h_re, h_im = s5_scan_fwd(s_re, s_im, mats["a_re"], mats["a_im"])
    y_scan = s5_outputs(u, h_re, h_im, mats["kd"], mats["wo_re"], mats["wo_im"])
    y, g2, x_new = ssm_mix_fwd(x, u, gate, y_scan, _row(w["d"]), w["w_glu"], _row(w["b_glu"]), w["w_out"])
    saved = dict(x=x, u=u, gate=gate, y=y, g2=g2, h_re=h_re, h_im=h_im, mats=mats, tiles=tiles)
    return x_new, saved


def _ssm_backward(dxo, w, s, token=None, early=None):
    dy, dgate, dw_out, dw_glu, db_glu, dd = ssm_mix_bwd(dxo, s["u"], s["gate"], s["y"], s["g2"], w["w_glu"], w["w_out"], token)
    mats = s["mats"]
    started = early(dict(w_glu=dw_glu, w_out=dw_out)) if early else None
    dh_re, dh_im = s5_state_grads(dy, mats["wo_re"], mats["wo_im"], started)
    ds_re, ds_im, da_re, da_im = s5_scan_bwd(dh_re, dh_im, s["h_re"], s["h_im"], mats["a_re"], mats["a_im"])
    du_scan = s5_input_grads(dy, ds_re, ds_im, mats["kd"], mats["ws_re"], mats["ws_im"])
    dkd, dws_re, dws_im, dwo_re, dwo_im = s5_weight_grads(s["u"], dy, s["h_re"], s["h_im"], ds_re, ds_im)
    dparams = _s5_param_grads(s5_param_bwd(s["tiles"], dkd, dws_re, dws_im, dwo_re, dwo_im,
                                           _group_power_tiles(da_re), _group_power_tiles(da_im)))
    dx, dw_in, dnorm = ssm_proj_bwd(s["x"], _row(w["norm"]), dxo, dy, du_scan, dgate, _row(w["d"]), w["w_in"])
    grads = dict(norm=dnorm, w_in=dw_in, d=dd, w_glu=dw_glu, b_glu=db_glu, w_out=dw_out)
    for name, val in zip(("a_re", "a_im", "log_step", "b_re", "b_im", "c_re", "c_im"), dparams):
        grads[name] = val
    return dx, grads


def _attn_forward(x, w, cos2, sin2, loss_head=None):
    q, k, v, gate = attn_proj_fwd(x, _row(w["norm"]), w["w_in"], cos2, sin2)
    o, lse = attn_fwd(q, k, v, w["sinks"])
    if loss_head is None:
        result = attn_out_fwd(x, o, gate, w["w_out"])
    else:
        result = attn_out_loss(x, o, gate, w["w_out"], _row(loss_head[0]), loss_head[1])
    return result, dict(x=x, q=q, k=k, v=v, gate=gate, o=o, lse=lse)


def _attn_backward(dxo, w, s, cos2, sin2, token=None):
    do, dgate, dw_out = attn_out_bwd(dxo, s["o"], s["gate"], w["w_out"], token)
    dq, dk, dv, dsinks = attn_bwd(s["q"], s["k"], s["v"], w["sinks"], s["o"], s["lse"], do)
    dx, dw_in, dnorm = attn_proj_bwd(s["x"], _row(w["norm"]), dxo, dq, dk, dv, dgate, cos2, sin2, w["w_in"])
    return dx, dict(norm=dnorm, w_in=dw_in, sinks=dsinks[0, :N_Q_HEADS], w_out=dw_out)


class _NoExchanges:
    def __init__(self, layers):
        self.layers = layers

    def layer(self, i, x):
        return self.layers[i]

    def early_grads(self, i, grads):
        return None

    def layer_done(self, i, grads, dx):
        return None


def _sequence_step(x, target, final_norm, hooks, depth=4):
    cos2, sin2 = _rope_tables(x.shape[0])
    saved, layers = [], []
    for i in range(depth):
        w = hooks.layer(i, x)
        layers.append(w)
        if i % 2 == 0:
            x, s = _ssm_forward(x, w)
        else:
            x, s = _attn_forward(x, w, cos2, sin2, (final_norm, target) if i == depth - 1 else None)
        saved.append(s)
    loss, dx, dfinal = x
    grads = {"final_norm": dfinal}
    token = None
    for i in reversed(range(depth)):
        if i % 2 == 0:
            dx, g = _ssm_backward(dx, layers[i], saved[i], token, functools.partial(hooks.early_grads, i))
        else:
            dx, g = _attn_backward(dx, layers[i], saved[i], cos2, sin2, token)
        g = {"l%d_%s" % (i, name): val for name, val in g.items()}
        grads.update(g)
        token = hooks.layer_done(i, g, dx)
    return loss[0, 0], dx, grads


ANY = pl.BlockSpec(memory_space=pl.ANY)


def _place():
    return lax.axis_index("x"), lax.axis_index("y"), lax.axis_index("c")


def _other_chips(x, y):
    return [(1 - x, y), (x, 1 - y), (1 - x, 1 - y)]


class _StagedCopies:
    def __init__(self, bufs, load_sems, store_sems):
        self.bufs, self.load_sems, self.store_sems = bufs, load_sems, store_sems
        self.loads, self.stores = [], []

    def load(self, i, src):
        cp = pltpu.make_async_copy(src, self.bufs[i], self.load_sems.at[i])
        cp.start()
        self.loads.append(cp)

    def store(self, i, dst):
        self.loads[i].wait()
        cp = pltpu.make_async_copy(self.bufs[i], dst, self.store_sems.at[i])
        cp.start()
        self.stores.append(cp)

    def finish(self):
        for cp in self.stores:
            cp.wait()


def _staging(blocks):
    n = len(blocks)
    return [pltpu.VMEM(b.shape, b.dtype) for b in blocks] + [pltpu.SemaphoreType.DMA((n,)), pltpu.SemaphoreType.DMA((n,))]


def exchange_halves_with_sibling(grads):
    n = len(grads)

    def body(*refs):
        ins, outs = refs[:n], refs[n:2 * n]
        send_sems, recv_sems = refs[2 * n:]
        x, y, c = _place()
        copies = []
        for i in range(n):
            half = ins[i].shape[1] // 2
            src = ins[i].at[:, pl.ds((1 - c) * half, half), :]
            cp = pltpu.make_async_remote_copy(src_ref=src, dst_ref=outs[i], send_sem=send_sems.at[i], recv_sem=recv_sems.at[i],
                                              device_id=(x, y, 1 - c), device_id_type=MESH)
            cp.start()
            copies.append(cp)
        for cp in copies:
            cp.wait()

    return pl.pallas_call(
        body, name="exchange_halves_with_sibling",
        in_specs=[ANY] * n, out_specs=[ANY] * n,
        out_shape=[_sds((g.shape[0], g.shape[1] // 2, g.shape[2])) for g in grads],
        scratch_shapes=[pltpu.SemaphoreType.DMA((n,)), pltpu.SemaphoreType.DMA((n,))],
    )(*grads)


def swap_halves_with_sibling(pieces):
    n = len(pieces)

    def body(*refs):
        ins, outs = refs[:n], refs[n:2 * n]
        send_sems, recv_sems = refs[2 * n:2 * n + 2]
        own = _StagedCopies(refs[2 * n + 2:3 * n + 2], *refs[3 * n + 2:])
        x, y, c = _place()
        for i in range(n):
            own.load(i, ins[i])
        swaps = []
        for i in range(n):
            cp = pltpu.make_async_remote_copy(src_ref=ins[i], dst_ref=outs[i].at[c], send_sem=send_sems.at[i], recv_sem=recv_sems.at[i],
                                              device_id=(x, y, 1 - c), device_id_type=MESH)
            cp.start()
            swaps.append(cp)
        for i in range(n):
            own.store(i, outs[i].at[c])
        for i in range(n):
            pltpu.make_async_remote_copy(src_ref=ins[i], dst_ref=outs[i].at[1 - c], send_sem=send_sems.at[i], recv_sem=recv_sems.at[i],
                                         device_id=(x, y, 1 - c), device_id_type=MESH).wait_recv()
        for cp in swaps:
            cp.wait_send()
        own.finish()

    return pl.pallas_call(
        body, name="swap_halves_with_sibling",
        in_specs=[ANY] * n, out_specs=[ANY] * n,
        out_shape=[_sds((2,) + p.shape) for p in pieces],
        scratch_shapes=[pltpu.SemaphoreType.DMA((n,)), pltpu.SemaphoreType.DMA((n,))] + _staging(pieces),
        compiler_params=_params(),
    )(*pieces)


def pass_halves_to_sibling(stacks):
    n = len(stacks)

    def body(*refs):
        outs = refs[n:2 * n]
        send_sems, recv_sems = refs[2 * n:]
        x, y, c = _place()
        sends = []
        for i in range(n):
            for k, (tx, ty) in enumerate(_other_chips(x, y)):
                mine = _rows_of_core(outs[i].at[2 * tx + ty], c, True)
                cp = pltpu.make_async_remote_copy(src_ref=mine, dst_ref=mine, send_sem=send_sems.at[i, k], recv_sem=recv_sems.at[i, k],
                                                  device_id=(x, y, 1 - c), device_id_type=MESH)
                cp.start()
                sends.append(cp)
        for i in range(n):
            for k, (tx, ty) in enumerate(_other_chips(x, y)):
                missing = _rows_of_core(outs[i].at[2 * tx + ty], 1 - c, True)
                pltpu.make_async_remote_copy(src_ref=missing, dst_ref=missing, send_sem=send_sems.at[i, k], recv_sem=recv_sems.at[i, k],
                                             device_id=(x, y, 1 - c), device_id_type=MESH).wait_recv()
        for cp in sends:
            cp.wait_send()

    sems = pltpu.SemaphoreType.DMA((n, 3))
    return pl.pallas_call(
        body, name="pass_halves_to_sibling", in_specs=[ANY] * n, out_specs=[ANY] * n,
        out_shape=[_sds(s.shape, s.dtype) for s in stacks], input_output_aliases={i: i for i in range(n)},
        scratch_shapes=[sems, sems],
    )(*stacks)


IN_HBM = pl.BlockSpec(memory_space=pltpu.HBM)
SEMAPHORES = pl.BlockSpec(memory_space=pltpu.SEMAPHORE)
DATAFLOW = pltpu.SideEffectType.DATAFLOW_SIDE_EFFECTING


def _hbm(a):
    return pltpu.with_memory_space_constraint(a, pltpu.HBM)


def place_own_blocks(shards):
    n = len(shards)

    def body(*refs):
        ins, outs = refs[:n], refs[n:2 * n]
        own = _StagedCopies(refs[2 * n:3 * n], *refs[3 * n:])
        x, y, _ = _place()
        for i in range(n):
            own.load(i, ins[i])
        for i in range(n):
            own.store(i, outs[i].at[2 * x + y])
        own.finish()

    return pl.pallas_call(
        body, name="place_own_blocks", in_specs=[ANY] * n, out_specs=[ANY] * n,
        out_shape=[_sds((4,) + s.shape, s.dtype) for s in shards],
        scratch_shapes=_staging(shards), compiler_params=_params(),
    )(*shards)


def _block_to_send(ref, chip, per_target):
    if not per_target:
        return ref
    return ref.at[chip] if ref.shape[0] == 4 else ref.at[0]


def _rows_of_core(ref, c, core_half):
    if not core_half:
        return ref
    rows = ref.shape[0] // 2
    return ref.at[pl.ds(c * rows, rows), :]


def start_sends_to_chips(name, sources, landings, per_target, after, core_half=False):
    n = len(sources)
    n_sems = 2 * 3 * n

    def body(*refs):
        srcs = refs[:n]
        sems = refs[2 * n + 1:2 * n + 1 + n_sems]
        lands = refs[2 * n + 1 + n_sems:3 * n + 1 + n_sems]
        token = refs[3 * n + 1 + n_sems]
        x, y, c = _place()
        me = 2 * x + y
        for i in range(n):
            for k, (tx, ty) in enumerate(_other_chips(x, y)):
                src = _rows_of_core(_block_to_send(srcs[i], 2 * tx + ty, per_target), c, core_half)
                dst = _rows_of_core(lands[i].at[me], c, core_half)
                pltpu.make_async_remote_copy(src_ref=src, dst_ref=dst, send_sem=sems[2 * (3 * i + k)], recv_sem=sems[2 * (3 * i + k) + 1],
                                             device_id=(tx, ty, c), device_id_type=MESH).start()
        token[...] = jnp.zeros_like(token)

    outs = pl.pallas_call(
        body, name=name,
        in_specs=[IN_HBM] * (2 * n) + [ANY],
        out_specs=[SEMAPHORES] * n_sems + [IN_HBM] * n + [pl.BlockSpec(memory_space=pltpu.VMEM)],
        out_shape=[pltpu.SemaphoreType.DMA(())] * n_sems + [pltpu.HBM(l.shape, l.dtype) for l in landings] + [_sds(TOKEN_SHAPE)],
        input_output_aliases={n + i: n_sems + i for i in range(n)},
        compiler_params=pltpu.CompilerParams(has_side_effects=DATAFLOW),
    )(*[_hbm(s) for s in sources], *[_hbm(l) for l in landings], after)
    return list(outs[:n_sems]), list(outs[n_sems:n_sems + n]), outs[n_sems + n]


def wait_sends_to_chips(name, sources, landings, sems, per_target, after, core_half=False):
    n = len(sources)
    n_sems = len(sems)

    def body(*refs):
        srcs = refs[:n]
        sem_refs = refs[2 * n:2 * n + n_sems]
        lands = refs[2 * n + n_sems + 1:]
        x, y, c = _place()
        me = 2 * x + y
        for i in range(n):
            for k, (tx, ty) in enumerate(_other_chips(x, y)):
                src = _rows_of_core(_block_to_send(srcs[i], me, per_target), c, core_half)
                dst = _rows_of_core(lands[i].at[2 * tx + ty], c, core_half)
                cp = pltpu.make_async_remote_copy(src_ref=src, dst_ref=dst, send_sem=sem_refs[2 * (3 * i + k)],
                                                  recv_sem=sem_refs[2 * (3 * i + k) + 1], device_id=(tx, ty, c), device_id_type=MESH)
                cp.wait_send()
                cp.wait_recv()

    return pl.pallas_call(
        body, name=name,
        in_specs=[IN_HBM] * (2 * n) + [SEMAPHORES] * n_sems + [ANY],
        out_specs=[IN_HBM] * n,
        out_shape=[pltpu.HBM(l.shape, l.dtype) for l in landings],
        input_output_aliases={n + i: i for i in range(n)},
        compiler_params=pltpu.CompilerParams(has_side_effects=DATAFLOW),
    )(*[_hbm(s) for s in sources], *landings, *sems, after)


def _row_tile(rows, cols):
    tm = rows
    while tm * cols * 4 > (2 << 20) and tm % 16 == 0:
        tm //= 2
    return tm


def add_pairs(half, a_list, b_list, out_dtypes, copies=1):
    n = len(a_list)
    nb = a_list[0].shape[0]

    def body(half_ref, *refs):
        for i in range(n):
            total = (refs[i][...] + refs[n + i][...]).astype(out_dtypes[i])
            for o_ref in refs[2 * n + i * copies:2 * n + (i + 1) * copies]:
                o_ref[...] = total

    halves = [pl.BlockSpec((None,) + b.shape[1:], lambda j, h: (j, h[0], 0)) for b in b_list]
    whole = [pl.BlockSpec((None,) + b.shape[1:], lambda j, h: (j, 0, 0)) for b in b_list]
    outs = pl.pallas_call(
        body, name="add_pairs",
        grid_spec=pltpu.PrefetchScalarGridSpec(num_scalar_prefetch=1, grid=(nb,), in_specs=halves + whole,
                                               out_specs=[s for s in whole for _ in range(copies)]),
        out_shape=[_sds(b.shape, dt) for b, dt in zip(b_list, out_dtypes) for _ in range(copies)],
        compiler_params=_params(("parallel",)),
    )(half, *a_list, *b_list)
    return [tuple(outs[i * copies:(i + 1) * copies]) for i in range(n)]


def sum_fours(arrays, token=None):
    n = len(arrays)
    extra, extra_specs = _after(token)
    steps = 2 if all(a.shape[1] % 32 == 0 for a in arrays) else 1

    def body(*refs):
        outs = refs[-n:]
        for a_ref, o_ref in zip(refs[:n], outs):
            o_ref[...] = ((a_ref[0].astype(F32) + a_ref[1].astype(F32)) + a_ref[2].astype(F32)) + a_ref[3].astype(F32)

    return pl.pallas_call(
        body, name="sum_fours", grid=(steps,),
        in_specs=[pl.BlockSpec((4, a.shape[1] // steps, a.shape[2]), lambda i: (0, i, 0)) for a in arrays] + extra_specs,
        out_specs=[pl.BlockSpec((a.shape[1] // steps, a.shape[2]), lambda i: (i, 0)) for a in arrays],
        out_shape=[_sds(a.shape[1:]) for a in arrays], compiler_params=_params(("parallel",)),
    )(*arrays, *extra)


def _adamw_update(w_ref, g_ref, m_ref, v_ref, d_ref, nm_ref, nv_ref):
    g = g_ref[...]
    nm = ADAM_B1 * m_ref[...] + (1.0 - ADAM_B1) * g
    nv = ADAM_B2 * v_ref[...] + (1.0 - ADAM_B2) * (g * g)
    d_ref[...] = -ADAM_LR * ((nm / (1.0 - ADAM_B1 ** ADAM_STEP)) / (jnp.sqrt(nv / (1.0 - ADAM_B2 ** ADAM_STEP)) + ADAM_EPS) + ADAM_WD * w_ref[...])
    nm_ref[...] = nm
    nv_ref[...] = nv


def adamw(w, g, m, v):
    rows, cols = w.shape
    tm = _row_tile(rows, cols)

    def body(*refs):
        _adamw_update(*refs)

    spec = pl.BlockSpec((tm, cols), lambda i: (i, 0))
    return pl.pallas_call(
        body, name="adamw", grid=(rows // tm,), in_specs=[spec] * 4, out_specs=[spec] * 3,
        out_shape=[_sds(w.shape)] * 3, compiler_params=_params(("parallel",)),
    )(w, g, m, v)


def adamw_small(ws, gs, ms, vs, slabs=None):
    n = len(ws)

    def body(*refs):
        for i in range(n):
            _adamw_update(refs[i], refs[n + i], refs[2 * n + i], refs[3 * n + i], refs[4 * n + i], refs[5 * n + i], refs[6 * n + i])

    if slabs is None:
        grid = ()
        specs = [pl.BlockSpec(memory_space=pltpu.VMEM)] * n
    else:
        grid = (slabs,)
        specs = [pl.BlockSpec((w.shape[0] // slabs,) + w.shape[1:], lambda i: (i, 0, 0)) for w in ws]
    outs = pl.pallas_call(
        body, name="adamw_small", grid=grid, in_specs=specs * 4, out_specs=specs * 3,
        out_shape=[_sds(w.shape) for w in ws] * 3, compiler_params=_params(("parallel",) if slabs else None),
    )(*ws, *gs, *ms, *vs)
    return outs[:n], outs[n:2 * n], outs[2 * n:]


PACK_TILE = 8 * LANES
PACK_PIECES = 8
PACK_ALIGN = PACK_PIECES * 16


def _pack_small(values, scalar=None):
    parts = []
    for name in PACK_NAMES:
        flat = values[name].reshape(-1)
        pad = (-flat.shape[0]) % PACK_TILE
        if pad:
            flat = jnp.concatenate([flat, jnp.zeros((pad,), F32)])
        parts.append(flat.reshape(-1, LANES))
    rows = sum(p.shape[0] for p in parts) + 8
    parts.append(jnp.zeros(((-rows) % PACK_ALIGN, LANES), F32))
    last = jnp.zeros((8, LANES), F32)
    parts.append(last if scalar is None else jnp.broadcast_to(scalar.astype(F32), (8, LANES)))
    return jnp.concatenate(parts, axis=0)


def _pack_row_of(name, like):
    row = 0
    for other in PACK_NAMES:
        if other == name:
            return row
        row += -(-math.prod(like[other].shape) // PACK_TILE) * 8
    raise KeyError(name)


def _unpack_small(pack, like):
    out = {}
    row = 0
    for name in PACK_NAMES:
        size = math.prod(like[name].shape)
        rows = -(-size // PACK_TILE) * 8
        out[name] = pack[row:row + rows].reshape(-1)[:size].reshape(like[name].shape)
        row += rows
    return out


def _travels_transposed(name, shard):
    return name.endswith("w_in") and shard.shape[-1] % LANES != 0


def _to_blocks(name, full):
    if full.ndim == 3:
        return full
    return full.reshape(4, full.shape[0] // 4, full.shape[1])


def _from_blocks(name, stacked):
    if name.endswith("w_in") and stacked.shape[2] % LANES == 0 and stacked.shape[1] == D_MODEL:
        return stacked
    return stacked.reshape(4 * stacked.shape[1], stacked.shape[2])


def _layer_big_names(i):
    return [n for n in BIG_NAMES if n.startswith("l%d_" % i)]


class _OverlappedExchanges:
    def __init__(self, weights):
        self.weights = weights
        self.c = lax.axis_index("c")
        self.first = _layer_big_names(0)
        self.later = [n for n in BIG_NAMES if n not in self.first]
        shards = [weights[n].astype(MXU_DTYPE) for n in self.first + self.later]
        shards = [s.T if _travels_transposed(n, s) else s for n, s in zip(self.first + self.later, shards)]
        placed = place_own_blocks(shards)
        k = len(self.first)
        sems, stacks, token = start_sends_to_chips("gather_first_start", shards[:k], placed[:k], False, shards[0], core_half=True)
        self.gather_first = (shards[:k], sems, stacks)
        sems, stacks, token = start_sends_to_chips("gather_later_start", shards[k:], placed[k:], False, token)
        self.gather_later = (shards[k:], sems, stacks)
        self.s5 = {}
        for i in (0, 2):
            self.s5[i] = _s5_matrices({n: weights["l%d_%s" % (i, n)] for n in SSM_NAMES if "l%d_%s" % (i, n) in SMALL_NAMES}, token)
            token = self.s5[i][1]["kd"]
        self.full = {}
        self.held = {}
        self.in_flight = {}
        self.contributions = {}

    def layer(self, i, x):
        if i == 0:
            shards, sems, stacks = self.gather_first
            stacks = wait_sends_to_chips("gather_first_wait", shards, stacks, sems, False, self.s5[2][1]["kd"], core_half=True)
            stacks = pass_halves_to_sibling(stacks)
            self.full.update({n: _from_blocks(n, g) for n, g in zip(self.first, stacks)})
        if i == 1:
            shards, sems, stacks = self.gather_later
            stacks = wait_sends_to_chips("gather_later_wait", shards, stacks, sems, False, x)
            self.full.update({n: _from_blocks(n, g) for n, g in zip(self.later, stacks)})
        names = SSM_NAMES if i % 2 == 0 else ATTN_NAMES
        w = {n: self.full.get("l%d_%s" % (i, n), self.weights.get("l%d_%s" % (i, n))) for n in names}
        if i in self.s5:
            w["s5"] = self.s5[i]
        return w

    def chip_sums(self, names, grads, extra_blocks=(), extra_dtypes=(), copies=1):
        blocks = [_to_blocks(n, grads[n]) for n in names] + list(extra_blocks)
        from_sibling = exchange_halves_with_sibling(blocks)
        k = len(names)
        half = self.c.reshape(1).astype(jnp.int32)
        sums = add_pairs(half, blocks[:k], from_sibling[:k], [WIRE_DTYPE] * k, copies)
        if extra_blocks:
            sums += add_pairs(half, blocks[k:], from_sibling[k:], list(extra_dtypes), copies)
        return sums

    def start_scatter(self, tag, names, grads):
        pairs = self.chip_sums(names, grads, copies=2)
        sums = [p[0] for p in pairs]
        sems, landings, token = start_sends_to_chips("scatter_start_" + tag, sums, [p[1] for p in pairs], True, sums[0])
        self.in_flight[tag] = (names, sums, sems, landings)
        return token

    def wait_scatter(self, tag, after):
        if tag in self.in_flight:
            names, sums, sems, landings = self.in_flight.pop(tag)
            done = wait_sends_to_chips("scatter_wait_" + tag, sums, landings, sems, True, after)
            self.contributions.update(zip(names, done))


    def early_grads(self, i, grads):
        if i != 0:
            return None
        self.held.update({"l0_" + n: g for n, g in grads.items()})
        names = _layer_big_names(1) + ["l0_w_glu", "l0_w_out"]
        return self.start_scatter("l1_l0", names, self.held)

    def layer_done(self, i, grads, dx):
        self.held.update(grads)
        if i == 2:
            return self.start_scatter("l3_l2", _layer_big_names(3) + _layer_big_names(2), self.held)
        if i == 1:
            self.wait_scatter("l3_l2", dx)
        if i == 0:
            self.wait_scatter("l1_l0", dx)
        return None


def _train_step(x, loss_target, weights, moments_m, moments_v):
    hooks = _OverlappedExchanges(weights)
    loss, dx, grads = _sequence_step(x[0], loss_target[0], weights["final_norm"], hooks)
    small_pack = _pack_small({n: grads[n] for n in SMALL_NAMES}, scalar=loss)
    tail = small_pack.shape[0] - _pack_row_of(EXACT_NAMES[0], grads)
    small = [small_pack[None, :-tail], small_pack[None, -tail:]]
    last = [n for n in _layer_big_names(0) if n not in hooks.contributions]
    pairs = hooks.chip_sums(last, grads, extra_blocks=small, extra_dtypes=[WIRE_DTYPE, F32], copies=2)
    sums = [p[0] for p in pairs]
    landings = [p[1] for p in pairs[:-2]] + [jnp.broadcast_to(s, (4,) + s.shape[1:]) for s in sums[-2:]]
    sems, landings, token = start_sends_to_chips("scatter_start_l0", sums, landings, True, sums[0])
    out_grad, out_delta, out_m, out_v = {}, {}, {}, {}

    def finish(names, arrays, token=None):
        shared = swap_halves_with_sibling(sum_fours(arrays, token))
        rest = []
        for n, s in zip(names, shared):
            if n not in weights:
                rest.append(s.reshape(-1, LANES))
                continue
            out_grad[n] = s.reshape(2 * s.shape[1], s.shape[2])
            if _travels_transposed(n, weights[n]):
                out_grad[n] = out_grad[n].T
            out_delta[n], out_m[n], out_v[n] = adamw(weights[n], out_grad[n], moments_m[n], moments_v[n])
        return rest

    others = [n for n in BIG_NAMES if n not in last]
    finish(others, [hooks.contributions[n] for n in others], token)
    arrived = wait_sends_to_chips("scatter_wait_l0", sums, landings, sems, True, out_v[others[-1]])
    small_grad_pack = jnp.concatenate(finish(last + ["small", "small tail"], arrived), axis=0)
    loss = small_grad_pack[-8, 0]
    out_grad.update(_unpack_small(small_grad_pack, {n: weights[n] for n in SMALL_NAMES}))
    cubes = [n for n in SMALL_NAMES if weights[n].ndim == 3]
    for names, slabs in ((cubes, 8), ([n for n in SMALL_NAMES if n not in cubes], None)):
        deltas, new_ms, new_vs = adamw_small(*[[group[n] for n in names] for group in (weights, out_grad, moments_m, moments_v)], slabs=slabs)
        out_delta.update(zip(names, deltas))
        out_m.update(zip(names, new_ms))
        out_v.update(zip(names, new_vs))
    outs = [loss, dx[None]]
    for group in (out_grad, out_delta, out_m, out_v):
        outs.extend(group[n] for n in WEIGHT_NAMES)
    return tuple(outs)


def kernel(x, l0_norm, l0_w_in, l0_a_re, l0_a_im, l0_log_step, l0_b_re, l0_b_im, l0_c_re, l0_c_im, l0_d, l0_w_glu, l0_b_glu, l0_w_out, l1_norm, l1_w_in, l1_sinks, l1_w_out, l2_norm, l2_w_in, l2_a_re, l2_a_im, l2_log_step, l2_b_re, l2_b_im, l2_c_re, l2_c_im, l2_d, l2_w_glu, l2_b_glu, l2_w_out, l3_norm, l3_w_in, l3_sinks, l3_w_out, final_norm, loss_target, m_l0_norm, m_l0_w_in, m_l0_a_re, m_l0_a_im, m_l0_log_step, m_l0_b_re, m_l0_b_im, m_l0_c_re, m_l0_c_im, m_l0_d, m_l0_w_glu, m_l0_b_glu, m_l0_w_out, m_l1_norm, m_l1_w_in, m_l1_sinks, m_l1_w_out, m_l2_norm, m_l2_w_in, m_l2_a_re, m_l2_a_im, m_l2_log_step, m_l2_b_re, m_l2_b_im, m_l2_c_re, m_l2_c_im, m_l2_d, m_l2_w_glu, m_l2_b_glu, m_l2_w_out, m_l3_norm, m_l3_w_in, m_l3_sinks, m_l3_w_out, m_final_norm, v_l0_norm, v_l0_w_in, v_l0_a_re, v_l0_a_im, v_l0_log_step, v_l0_b_re, v_l0_b_im, v_l0_c_re, v_l0_c_im, v_l0_d, v_l0_w_glu, v_l0_b_glu, v_l0_w_out, v_l1_norm, v_l1_w_in, v_l1_sinks, v_l1_w_out, v_l2_norm, v_l2_w_in, v_l2_a_re, v_l2_a_im, v_l2_log_step, v_l2_b_re, v_l2_b_im, v_l2_c_re, v_l2_c_im, v_l2_d, v_l2_w_glu, v_l2_b_glu, v_l2_w_out, v_l3_norm, v_l3_w_in, v_l3_sinks, v_l3_w_out, v_final_norm):
    args = locals()
    weights = {n: args[n] for n in WEIGHT_NAMES}
    moments_m = {n: args["m_" + n] for n in WEIGHT_NAMES}
    moments_v = {n: args["v_" + n] for n in WEIGHT_NAMES}
    return _train_step(x, loss_target, weights, moments_m, moments_v)
```

```python
import functools
import math

import jax
import jax.numpy as jnp
from jax import lax
from jax.experimental import pallas as pl
from jax.experimental.pallas import tpu as pltpu

F32 = jnp.float32
MXU_DTYPE = jnp.bfloat16
WIRE_DTYPE = jnp.bfloat16
MESH = pl.DeviceIdType.MESH

D_MODEL = 1024
BRANCH = 1024
NORM_EPS = 1e-5
SSM_GROUPS = 64
SSM_GROUP = 16
SSM_STATE = 64
S5_CHUNK = 16
LANES = 128
S5_OCT = LANES // SSM_GROUP
S5_OCTETS = SSM_GROUPS // S5_OCT
S5_OCT_IN = S5_CHUNK * LANES
S5_OCT_STATE = S5_OCT * SSM_STATE
S5_STATES = SSM_GROUPS * SSM_STATE
HEAD_DIM = 64
N_Q_HEADS = 16
N_KV_HEADS = 2
GQA_GROUP = N_Q_HEADS // N_KV_HEADS
ATTN_BLOCK = 128
Q_DIM = N_Q_HEADS * HEAD_DIM
KV_DIM = N_KV_HEADS * HEAD_DIM
ROPE_THETA = 10000.0
NEG_INF = -1e30
ADAM_LR = 0.001
ADAM_B1 = 0.9
ADAM_B2 = 0.999
ADAM_EPS = 1e-08
ADAM_WD = 0.01
ADAM_STEP = 10

VMEM_LIMIT_V7X = 56 * 1024 * 1024
ROW_TILE_FWD = 512
ROW_TILE_BWD = 512

SSM_NAMES = ("norm", "w_in", "a_re", "a_im", "log_step", "b_re", "b_im", "c_re", "c_im", "d", "w_glu", "b_glu", "w_out")
ATTN_NAMES = ("norm", "w_in", "sinks", "w_out")


def _weight_names():
    names = []
    for i in range(4):
        for n in (SSM_NAMES if i % 2 == 0 else ATTN_NAMES):
            names.append("l%d_%s" % (i, n))
    names.append("final_norm")
    return names


WEIGHT_NAMES = _weight_names()
BIG_NAMES = [n for n in WEIGHT_NAMES if n.endswith(("w_in", "w_glu", "w_out"))]
SMALL_NAMES = [n for n in WEIGHT_NAMES if n not in BIG_NAMES]
EXACT_NAMES = [n for n in SMALL_NAMES if n.endswith(("log_step", "sinks")) or n == "final_norm"]
PACK_NAMES = [n for n in SMALL_NAMES if n not in EXACT_NAMES] + EXACT_NAMES


def _params(semantics=None):
    return pltpu.CompilerParams(dimension_semantics=semantics, vmem_limit_bytes=VMEM_LIMIT_V7X)


def _rows(tm, n):
    return pl.BlockSpec((tm, n), lambda i: (i, 0))


def _whole(shape):
    return pl.BlockSpec(shape, lambda i: (0,) * len(shape), pipeline_mode=pl.Buffered(1))


def _sds(shape, dtype=F32):
    return jax.ShapeDtypeStruct(shape, dtype)


def _mm(a, b):
    return jnp.dot(a.astype(MXU_DTYPE), b.astype(MXU_DTYPE), preferred_element_type=F32)


def _mm_tn(a, b):
    return lax.dot_general(a.astype(MXU_DTYPE), b.astype(MXU_DTYPE), (((0,), (0,)), ((), ())), preferred_element_type=F32)


def _mm_nt(a, b):
    return lax.dot_general(a.astype(MXU_DTYPE), b.astype(MXU_DTYPE), (((1,), (1,)), ((), ())), preferred_element_type=F32)


def _sigmoid(x):
    return 0.5 + 0.5 * jnp.tanh(0.5 * x)


def _silu(x):
    return x * _sigmoid(x)


def _silu_and_grad(x):
    s = _sigmoid(x)
    return x * s, s * (1.0 + x * (1.0 - s))


GELU_C0 = math.sqrt(2.0 / math.pi)
GELU_C1 = 0.044715


def _gelu(x):
    return 0.5 * x * (1.0 + jnp.tanh(GELU_C0 * (x + GELU_C1 * x * x * x)))


def _gelu_and_grad(x):
    x2 = x * x
    th = jnp.tanh(GELU_C0 * x * (1.0 + GELU_C1 * x2))
    half = 0.5 + 0.5 * th
    return x * half, half + 0.5 * x * (1.0 - th * th) * (GELU_C0 + 3.0 * GELU_C0 * GELU_C1 * x2)


def _rms(x, g):
    r = lax.rsqrt(jnp.mean(x * x, axis=-1, keepdims=True) + NORM_EPS)
    xhat = x * r
    return r, xhat, xhat * g


def _rms_bwd(dh, g, r, xhat):
    dxhat = dh * g
    dx = r * (dxhat - xhat * jnp.mean(dxhat * xhat, axis=-1, keepdims=True))
    return dx, jnp.sum(dh * xhat, axis=0, keepdims=True)


def _swap_half_heads(x):
    n = x.shape[-1]
    lane = lax.broadcasted_iota(jnp.int32, x.shape, x.ndim - 1)
    first = (lane % HEAD_DIM) < (HEAD_DIM // 2)
    return jnp.where(first, pltpu.roll(x, n - HEAD_DIM // 2, x.ndim - 1), pltpu.roll(x, HEAD_DIM // 2, x.ndim - 1))


def _tile_lanes(t, reps):
    return jnp.concatenate([t] * reps, axis=1)


TOKEN_SHAPE = (8, LANES)


def _after(token):
    return ([], []) if token is None else ([token], [_whole(TOKEN_SHAPE)])


def ssm_proj_fwd(x, norm, w_in):
    t = x.shape[0]
    tm = min(ROW_TILE_FWD, t)

    def body(x_ref, g_ref, w_ref, u_ref, gate_ref):
        _, _, h = _rms(x_ref[...], g_ref[...])
        h = h.astype(MXU_DTYPE)
        half = BRANCH // 2
        for j in range(2):
            u_ref[:, j * half:(j + 1) * half] = _mm(h, w_ref[j])
            gate_ref[:, j * half:(j + 1) * half] = _mm(h, w_ref[2 + j])

    return pl.pallas_call(
        body, name="ssm_proj_fwd", grid=(t // tm,),
        in_specs=[_rows(tm, D_MODEL), _whole((1, D_MODEL)), _whole((4, D_MODEL, BRANCH // 2))],
        out_specs=[_rows(tm, BRANCH), _rows(tm, BRANCH)],
        out_shape=[_sds((t, BRANCH)), _sds((t, BRANCH))],
        compiler_params=_params(("parallel",)),
    )(x, norm, w_in)


def _chunk_rows(ref, nk, dtype=None):
    rows = jnp.concatenate([ref[pl.ds(s, nk, stride=S5_CHUNK), :] for s in range(S5_CHUNK)], axis=1)
    return rows.astype(MXU_DTYPE if dtype is None else dtype)


def _store_chunk_rows(ref, val, nk):
    for s in range(S5_CHUNK):
        ref[pl.ds(s, nk, stride=S5_CHUNK), :] = val[:, s * LANES:(s + 1) * LANES]


def _own_group_mask():
    row = lax.broadcasted_iota(jnp.int32, (S5_OCT_IN, S5_OCT_STATE), 0)
    col = lax.broadcasted_iota(jnp.int32, (S5_OCT_IN, S5_OCT_STATE), 1)
    return ((row % LANES) // SSM_GROUP) == (col // SSM_STATE)


def _spread_groups(w):
    return jnp.where(_own_group_mask(), jnp.concatenate([w] * (S5_OCT_STATE // LANES), axis=1), 0.0).astype(MXU_DTYPE)


def _fold_groups(p):
    p = jnp.where(_own_group_mask(), p, 0.0)
    return sum(p[:, q * LANES:(q + 1) * LANES] for q in range(S5_OCT_STATE // LANES))


def _fill_toeplitz(win_ref, kd_ref):
    win_ref[...] = jnp.zeros_like(win_ref)
    for s in range(S5_CHUNK):
        for t in range(s, S5_CHUNK):
            win_ref[s * LANES:(s + 1) * LANES, t * LANES:(t + 1) * LANES] = kd_ref[t - s].astype(MXU_DTYPE)


TOEPLITZ_BLOCK = 512
_TOEPLITZ_BLOCKS = [(lo, lo + TOEPLITZ_BLOCK) for lo in range(0, S5_OCT_IN, TOEPLITZ_BLOCK)]


def _strip(t):
    return pl.BlockSpec((t, LANES), lambda b: (0, b))


def _oct_states(nk):
    return pl.BlockSpec((nk, S5_OCT_STATE), lambda b: (0, b))


OCT_W = pl.BlockSpec((None, S5_OCT_IN, LANES), lambda b: (b, 0, 0))
OCT_KD = pl.BlockSpec((None, S5_CHUNK, LANES, LANES), lambda b: (b, 0, 0, 0))


def s5_chunk_states(u, ws_re, ws_im):
    t = u.shape[0]
    nk = t // S5_CHUNK

    def body(u_ref, wr_ref, wi_ref, re_ref, im_ref):
        uc = _chunk_rows(u_ref, nk)
        re_ref[...] = _mm(uc, _spread_groups(wr_ref[...]))
        im_ref[...] = _mm(uc, _spread_groups(wi_ref[...]))

    return pl.pallas_call(
        body, name="s5_chunk_states", grid=(S5_OCTETS,),
        in_specs=[_strip(t), OCT_W, OCT_W], out_specs=[_oct_states(nk), _oct_states(nk)],
        out_shape=[_sds((nk, S5_STATES)), _sds((nk, S5_STATES))],
        compiler_params=_params(("parallel",)),
    )(u, ws_re, ws_im)


def s5_scan_fwd(s_re, s_im, a_re, a_im):
    nk = s_re.shape[0]

    def body(sre_ref, sim_ref, ar_ref, ai_ref, hre_ref, him_ref):
        ar = ar_ref[...]
        ai = ai_ref[...]

        def step(k, carry):
            hr, hi = carry
            hre_ref[pl.ds(k, 1), :] = hr
            him_ref[pl.ds(k, 1), :] = hi
            sr = sre_ref[pl.ds(k, 1), :]
            si = sim_ref[pl.ds(k, 1), :]
            return ar * hr - ai * hi + sr, ai * hr + ar * hi + si

        zero = jnp.zeros((1, S5_STATES), F32)
        lax.fori_loop(0, nk, step, (zero, zero))

    vm = pl.BlockSpec(memory_space=pltpu.VMEM)
    return pl.pallas_call(
        body, name="s5_scan_fwd", in_specs=[vm, vm, vm, vm], out_specs=[vm, vm],
        out_shape=[_sds((nk, S5_STATES)), _sds((nk, S5_STATES))],
        compiler_params=_params(),
    )(s_re, s_im, a_re, a_im)


def s5_outputs(u, h_re, h_im, kd, wo_re, wo_im):
    t = u.shape[0]
    nk = t // S5_CHUNK

    def body(u_ref, hre_ref, him_ref, kd_ref, wor_ref, woi_ref, y_ref, win_ref):
        _fill_toeplitz(win_ref, kd_ref)
        uc = _chunk_rows(u_ref, nk)
        y = jnp.concatenate([_mm(uc[:, :hi], win_ref[:hi, lo:hi]) for lo, hi in _TOEPLITZ_BLOCKS], axis=1)
        y = y + _mm_nt(hre_ref[...], _spread_groups(wor_ref[...])) + _mm_nt(him_ref[...], _spread_groups(woi_ref[...]))
        _store_chunk_rows(y_ref, y, nk)

    return pl.pallas_call(
        body, name="s5_outputs", grid=(S5_OCTETS,),
        in_specs=[_strip(t), _oct_states(nk), _oct_states(nk), OCT_KD, OCT_W, OCT_W],
        out_specs=_strip(t), out_shape=_sds((t, BRANCH)),
        scratch_shapes=[pltpu.VMEM((S5_OCT_IN, S5_OCT_IN), MXU_DTYPE)],
        compiler_params=_params(("parallel",)),
    )(u, h_re, h_im, kd, wo_re, wo_im)


def s5_state_grads(dy, wo_re, wo_im, token=None):
    t = dy.shape[0]
    nk = t // S5_CHUNK
    extra, extra_specs = _after(token)

    def body(dy_ref, wor_ref, woi_ref, *rest):
        re_ref, im_ref = rest[-2:]
        dyc = _chunk_rows(dy_ref, nk)
        re_ref[...] = _mm(dyc, _spread_groups(wor_ref[...]))
        im_ref[...] = _mm(dyc, _spread_groups(woi_ref[...]))

    return pl.pallas_call(
        body, name="s5_state_grads", grid=(S5_OCTETS,),
        in_specs=[_strip(t), OCT_W, OCT_W] + extra_specs, out_specs=[_oct_states(nk), _oct_states(nk)],
        out_shape=[_sds((nk, S5_STATES)), _sds((nk, S5_STATES))],
        compiler_params=_params(("parallel",)),
    )(dy, wo_re, wo_im, *extra)


def s5_scan_bwd(dh_re, dh_im, h_re, h_im, a_re, a_im):
    nk = dh_re.shape[0]

    def body(dhr_ref, dhi_ref, hr_ref, hi_ref, ar_ref, ai_ref, dsr_ref, dsi_ref, dar_ref, dai_ref):
        ar = ar_ref[...]
        ai = ai_ref[...]

        def step(i, carry):
            gr, gi = carry
            k = nk - 1 - i
            dhr = dhr_ref[pl.ds(k, 1), :]
            dhi = dhi_ref[pl.ds(k, 1), :]
            dsr_ref[pl.ds(k, 1), :] = gr
            dsi_ref[pl.ds(k, 1), :] = gi
            return dhr + ar * gr + ai * gi, dhi - ai * gr + ar * gi

        zero = jnp.zeros((1, S5_STATES), F32)
        lax.fori_loop(0, nk, step, (zero, zero))
        dsr, dsi, hr, hi = dsr_ref[...], dsi_ref[...], hr_ref[...], hi_ref[...]
        dar_ref[...] = jnp.sum(dsr * hr + dsi * hi, axis=0, keepdims=True)
        dai_ref[...] = jnp.sum(dsi * hr - dsr * hi, axis=0, keepdims=True)

    vm = pl.BlockSpec(memory_space=pltpu.VMEM)
    return pl.pallas_call(
        body, name="s5_scan_bwd", in_specs=[vm] * 6, out_specs=[vm] * 4,
        out_shape=[_sds((nk, S5_STATES)), _sds((nk, S5_STATES)), _sds((1, S5_STATES)), _sds((1, S5_STATES))],
        input_output_aliases={0: 0, 1: 1}, compiler_params=_params(),
    )(dh_re, dh_im, h_re, h_im, a_re, a_im)


def s5_input_grads(dy, ds_re, ds_im, kd, ws_re, ws_im):
    t = dy.shape[0]
    nk = t // S5_CHUNK

    def body(dy_ref, dsr_ref, dsi_ref, kd_ref, wsr_ref, wsi_ref, du_ref, win_ref):
        _fill_toeplitz(win_ref, kd_ref)
        dyc = _chunk_rows(dy_ref, nk)
        du = jnp.concatenate([_mm_nt(dyc[:, lo:], win_ref[lo:hi, lo:]) for lo, hi in _TOEPLITZ_BLOCKS], axis=1)
        du = du + _mm_nt(dsr_ref[...], _spread_groups(wsr_ref[...])) + _mm_nt(dsi_ref[...], _spread_groups(wsi_ref[...]))
        _store_chunk_rows(du_ref, du, nk)

    return pl.pallas_call(
        body, name="s5_input_grads", grid=(S5_OCTETS,),
        in_specs=[_strip(t), _oct_states(nk), _oct_states(nk), OCT_KD, OCT_W, OCT_W],
        out_specs=_strip(t), out_shape=_sds((t, BRANCH)),
        scratch_shapes=[pltpu.VMEM((S5_OCT_IN, S5_OCT_IN), MXU_DTYPE)],
        compiler_params=_params(("parallel",)),
    )(dy, ds_re, ds_im, kd, ws_re, ws_im)


def s5_weight_grads(u, dy, h_re, h_im, ds_re, ds_im):
    t = u.shape[0]
    nk = t // S5_CHUNK

    def body(u_ref, dy_ref, hre_ref, him_ref, dsr_ref, dsi_ref, dkd_ref, dwsr_ref, dwsi_ref, dwor_ref, dwoi_ref):
        dyc = _chunk_rows(dy_ref, nk, F32)
        uct = _chunk_rows(u_ref, nk, F32).T.astype(MXU_DTYPE)
        dyct = dyc.T.astype(MXU_DTYPE)
        dyc = dyc.astype(MXU_DTYPE)
        dwsr_ref[...] = _fold_groups(_mm(uct, dsr_ref[...]))
        dwsi_ref[...] = _fold_groups(_mm(uct, dsi_ref[...]))
        dwor_ref[...] = _fold_groups(_mm(dyct, hre_ref[...]))
        dwoi_ref[...] = _fold_groups(_mm(dyct, him_ref[...]))
        dkd_ref[...] = jnp.zeros_like(dkd_ref)
        for tt in range(0, S5_CHUNK, 2):
            p = _mm(uct[:(tt + 2) * LANES], dyc[:, tt * LANES:(tt + 2) * LANES])
            for s in range(tt + 2):
                rows = p[s * LANES:(s + 1) * LANES]
                if s <= tt:
                    dkd_ref[tt - s] += rows[:, :LANES]
                dkd_ref[tt + 1 - s] += rows[:, LANES:]

    return pl.pallas_call(
        body, name="s5_weight_grads", grid=(S5_OCTETS,),
        in_specs=[_strip(t), _strip(t)] + [_oct_states(nk)] * 4,
        out_specs=[OCT_KD, OCT_W, OCT_W, OCT_W, OCT_W],
        out_shape=[_sds((S5_OCTETS, S5_CHUNK, LANES, LANES))] + [_sds((S5_OCTETS, S5_OCT_IN, LANES))] * 4,
        compiler_params=_params(("parallel",)),
    )(u, dy, h_re, h_im, ds_re, ds_im)


def ssm_mix_fwd(x, u, gate, y_scan, d, w_glu, b_glu, w_out):
    t = x.shape[0]
    tm = min(ROW_TILE_FWD, t)

    def body(x_ref, u_ref, gate_ref, ys_ref, d_ref, wg_ref, bg_ref, wo_ref, y_ref, g2_ref, xo_ref):
        y = ys_ref[...] + d_ref[...] * u_ref[...]
        z0 = _gelu(y)
        g2 = _mm(z0, wg_ref[...]) + bg_ref[...]
        a = z0 * _sigmoid(g2) * _silu(gate_ref[...])
        y_ref[...] = y
        g2_ref[...] = g2
        xo_ref[...] = x_ref[...] + _mm(a, wo_ref[...])

    row = _rows(tm, BRANCH)
    vec = _whole((1, BRANCH))
    mat = _whole((BRANCH, BRANCH))
    return pl.pallas_call(
        body, name="ssm_mix_fwd", grid=(t // tm,),
        in_specs=[row, row, row, row, vec, mat, vec, mat],
        out_specs=[row, row, row],
        out_shape=[_sds((t, BRANCH))] * 3,
        compiler_params=_params(("parallel",)),
    )(x, u, gate, y_scan, d, w_glu, b_glu, w_out)


def ssm_mix_bwd(dxo, u, gate, y, g2, w_glu, w_out, token=None):
    t = dxo.shape[0]
    tm = min(ROW_TILE_BWD, t)
    extra, extra_specs = _after(token)

    def body(dxo_ref, u_ref, gate_ref, y_ref, g2_ref, wgt_ref, wot_ref, *rest):
        dy_ref, dgate_ref, dwo_ref, dwg_ref, dbg_ref, dd_ref = rest[-6:]

        @pl.when(pl.program_id(0) == 0)
        def _():
            dwo_ref[...] = jnp.zeros_like(dwo_ref)
            dwg_ref[...] = jnp.zeros_like(dwg_ref)
            dbg_ref[...] = jnp.zeros_like(dbg_ref)
            dd_ref[...] = jnp.zeros_like(dd_ref)

        dxo = dxo_ref[...]
        gate = gate_ref[...]
        y = y_ref[...]
        z0, z0_grad = _gelu_and_grad(y)
        sg = _sigmoid(g2_ref[...])
        z = z0 * sg
        sgate, sgate_grad = _silu_and_grad(gate)
        da = _mm_nt(dxo, wot_ref[...])
        dwo_ref[...] += _mm_tn(z * sgate, dxo)
        dz = da * sgate
        dgate_ref[...] = da * z * sgate_grad
        dg2 = dz * z0 * sg * (1.0 - sg)
        dbg_ref[...] += jnp.sum(dg2, axis=0, keepdims=True)
        dwg_ref[...] += _mm_tn(z0, dg2)
        dz0 = dz * sg + _mm_nt(dg2, wgt_ref[...])
        dy = dz0 * z0_grad
        dd_ref[...] += jnp.sum(dy * u_ref[...], axis=0, keepdims=True)
        dy_ref[...] = dy

    row = _rows(tm, BRANCH)
    vec = _whole((1, BRANCH))
    mat = _whole((BRANCH, BRANCH))
    return pl.pallas_call(
        body, name="ssm_mix_bwd", grid=(t // tm,),
        in_specs=[row, row, row, row, row, mat, mat] + extra_specs,
        out_specs=[row, row, mat, mat, vec, vec],
        out_shape=[_sds((t, BRANCH)), _sds((t, BRANCH)), _sds((BRANCH, D_MODEL)), _sds((BRANCH, BRANCH)),
                   _sds((1, BRANCH)), _sds((1, BRANCH))],
        compiler_params=_params(("arbitrary",)),
    )(dxo, u, gate, y, g2, w_glu, w_out, *extra)


def ssm_proj_bwd(x, norm, dxo, dy, du_scan, dgate, d, w_in):
    t = x.shape[0]
    tm = min(ROW_TILE_BWD, t)
    n = 2 * BRANCH

    def body(x_ref, g_ref, dxo_ref, dy_ref, dus_ref, dgate_ref, d_ref, wt_ref, dx_ref, dw_ref, dg_ref):
        @pl.when(pl.program_id(0) == 0)
        def _():
            dw_ref[...] = jnp.zeros_like(dw_ref)
            dg_ref[...] = jnp.zeros_like(dg_ref)

        g = g_ref[...]
        r, xhat, h = _rms(x_ref[...], g)
        h = h.astype(MXU_DTYPE)
        du = dus_ref[...] + d_ref[...] * dy_ref[...]
        dproj = jnp.concatenate([du, dgate_ref[...]], axis=1).astype(MXU_DTYPE)
        dh = jnp.zeros((tm, D_MODEL), F32)
        for j in range(4):
            cols = dproj[:, j * (n // 4):(j + 1) * (n // 4)]
            dh = dh + _mm_nt(cols, wt_ref[j])
            dw_ref[j] += _mm_tn(h, cols)
        dx, dg = _rms_bwd(dh, g, r, xhat)
        dg_ref[...] += dg
        dx_ref[...] = dxo_ref[...] + dx

    row = _rows(tm, D_MODEL)
    vec = _whole((1, D_MODEL))
    blocks = _whole((4, D_MODEL, n // 4))
    return pl.pallas_call(
        body, name="ssm_proj_bwd", grid=(t // tm,),
        in_specs=[row, vec, row, row, row, row, vec, blocks],
        out_specs=[row, blocks, vec],
        out_shape=[_sds((t, D_MODEL)), _sds((4, D_MODEL, n // 4)), _sds((1, D_MODEL))],
        compiler_params=_params(("arbitrary",)),
    )(x, norm, dxo, dy, du_scan, dgate, d, w_in)


ATTN_N = Q_DIM + 2 * KV_DIM + BRANCH


def attn_proj_fwd(x, norm, w_in_t, cos2, sin2):
    t = x.shape[0]
    tm = min(ROW_TILE_FWD, t)

    def body(x_ref, g_ref, w_ref, cos_ref, sin_ref, q_ref, k_ref, v_ref, gate_ref):
        _, _, h = _rms(x_ref[...], g_ref[...])
        p = _mm_nt(h, w_ref[...])
        cs = cos_ref[...]
        sn = sin_ref[...]
        q = p[:, :Q_DIM]
        k = p[:, Q_DIM:Q_DIM + KV_DIM]
        q_ref[...] = q * _tile_lanes(cs, Q_DIM // LANES) + _swap_half_heads(q) * _tile_lanes(sn, Q_DIM // LANES)
        k_ref[...] = k * cs + _swap_half_heads(k) * sn
        v_ref[...] = p[:, Q_DIM + KV_DIM:Q_DIM + 2 * KV_DIM]
        gate_ref[...] = p[:, Q_DIM + 2 * KV_DIM:]

    return pl.pallas_call(
        body, name="attn_proj_fwd", grid=(t // tm,),
        in_specs=[_rows(tm, D_MODEL), _whole((1, D_MODEL)), _whole((ATTN_N, D_MODEL)), _rows(tm, LANES), _rows(tm, LANES)],
        out_specs=[_rows(tm, Q_DIM), _rows(tm, KV_DIM), _rows(tm, KV_DIM), _rows(tm, BRANCH)],
        out_shape=[_sds((t, Q_DIM)), _sds((t, KV_DIM)), _sds((t, KV_DIM)), _sds((t, BRANCH))],
        compiler_params=_params(("parallel",)),
    )(x, norm, w_in_t, cos2, sin2)


GQA_LANES = GQA_GROUP * ATTN_BLOCK


def _window_masks(first_block):
    kj = lax.broadcasted_iota(jnp.int32, (ATTN_BLOCK, GQA_LANES), 0)
    qi = lax.broadcasted_iota(jnp.int32, (ATTN_BLOCK, GQA_LANES), 1) % ATTN_BLOCK
    return kj > qi, kj > jnp.where(first_block, qi, ATTN_BLOCK)


def _fold(upper, both):
    return jnp.where(upper, both[:ATTN_BLOCK], both[ATTN_BLOCK:])


def _unfold(upper, tile):
    return jnp.concatenate([jnp.where(upper, tile, 0.0), jnp.where(upper, 0.0, tile)], axis=0).astype(MXU_DTYPE)


def _stack_heads(ref, group):
    return jnp.concatenate([ref[:, h * HEAD_DIM:(h + 1) * HEAD_DIM] for h in range(group * GQA_GROUP, (group + 1) * GQA_GROUP)], axis=0)


def _unstack_heads(ref, group, stacked):
    for n in range(GQA_GROUP):
        h = group * GQA_GROUP + n
        ref[:, h * HEAD_DIM:(h + 1) * HEAD_DIM] = stacked[n * ATTN_BLOCK:(n + 1) * ATTN_BLOCK]


def _sink_row(sink_ref, group):
    return jnp.concatenate([jnp.full((1, ATTN_BLOCK), sink_ref[group * GQA_GROUP + n], F32) for n in range(GQA_GROUP)], axis=1)


def _lane_is(h):
    return lax.broadcasted_iota(jnp.int32, (1, LANES), 1) == h


def attn_fwd(q, k, v, sinks):
    t = q.shape[0]
    nb = t // ATTN_BLOCK
    scale = HEAD_DIM ** -0.5

    def body(sink_ref, q_ref, kc_ref, kp_ref, vc_ref, vp_ref, o_ref, lse_ref):
        keys = jnp.concatenate([kp_ref[...], kc_ref[...]], axis=0).astype(MXU_DTYPE)
        vals = jnp.concatenate([vp_ref[...], vc_ref[...]], axis=0).astype(MXU_DTYPE)
        upper, dead = _window_masks(pl.program_id(0) == 0)
        for g in range(N_KV_HEADS):
            kv = slice(g * HEAD_DIM, (g + 1) * HEAD_DIM)
            qs = _stack_heads(q_ref, g) * scale
            s = jnp.where(dead, NEG_INF, _fold(upper, _mm_nt(keys[:, kv], qs)))
            sink = _sink_row(sink_ref, g)
            m = jnp.maximum(jnp.max(s, axis=0, keepdims=True), sink)
            p = jnp.exp(s - m)
            den = jnp.sum(p, axis=0, keepdims=True) + jnp.exp(sink - m)
            _unstack_heads(o_ref, g, _mm_tn(_unfold(upper, p * (1.0 / den)), vals[:, kv]))
            lse = m + jnp.log(den)
            for n in range(GQA_GROUP):
                lse_ref[pl.ds(g * GQA_GROUP + n, 1), :] = lse[:, n * ATTN_BLOCK:(n + 1) * ATTN_BLOCK]

    cur = lambda n: pl.BlockSpec((ATTN_BLOCK, n), lambda i: (i, 0))
    prev = lambda n: pl.BlockSpec((ATTN_BLOCK, n), lambda i: (jnp.maximum(i - 1, 0), 0))
    return pl.pallas_call(
        body, name="attn_fwd", grid=(nb,),
        in_specs=[pl.BlockSpec(memory_space=pltpu.SMEM), cur(Q_DIM), cur(KV_DIM), prev(KV_DIM), cur(KV_DIM), prev(KV_DIM)],
        out_specs=[cur(Q_DIM), pl.BlockSpec((N_Q_HEADS, ATTN_BLOCK), lambda i: (0, i))],
        out_shape=[_sds((t, Q_DIM)), _sds((N_Q_HEADS, t))],
        compiler_params=_params(("parallel",)),
    )(sinks, q, k, k, v, v)


def attn_bwd(q, k, v, sinks, o, lse, do):
    t = q.shape[0]
    nb = t // ATTN_BLOCK
    scale = HEAD_DIM ** -0.5

    def body(sink_ref, q_ref, kc_ref, kp_ref, vc_ref, vp_ref, o_ref, lse_ref, do_ref,
             dq_ref, dk_ref, dv_ref, dsink_ref, dk_carry, dv_carry):
        i = pl.program_id(0)

        @pl.when(i == 0)
        def _():
            dsink_ref[...] = jnp.zeros_like(dsink_ref)
            dk_carry[...] = jnp.zeros_like(dk_carry)
            dv_carry[...] = jnp.zeros_like(dv_carry)

        @pl.when(i < nb)
        def _():
            keys = jnp.concatenate([kp_ref[...], kc_ref[...]], axis=0).astype(MXU_DTYPE)
            vals = jnp.concatenate([vp_ref[...], vc_ref[...]], axis=0).astype(MXU_DTYPE)
            upper, dead = _window_masks(i == 0)
            dsink = jnp.zeros((1, LANES), F32)
            dk_heads = []
            dv_heads = []
            for g in range(N_KV_HEADS):
                kv = slice(g * HEAD_DIM, (g + 1) * HEAD_DIM)
                qs = (_stack_heads(q_ref, g) * scale).astype(MXU_DTYPE)
                dos = _stack_heads(do_ref, g)
                lse = jnp.concatenate([lse_ref[pl.ds(g * GQA_GROUP + n, 1), :] for n in range(GQA_GROUP)], axis=1)
                s = jnp.where(dead, NEG_INF, _fold(upper, _mm_nt(keys[:, kv], qs)))
                p = jnp.exp(s - lse)
                prod = dos * _stack_heads(o_ref, g)
                head = prod.astype(MXU_DTYPE)
                ones = jnp.ones((8, HEAD_DIM), MXU_DTYPE)
                delta = (_mm_nt(ones, head) + _mm_nt(ones, prod - head.astype(F32)))[:1]
                dos = dos.astype(MXU_DTYPE)
                ds = _unfold(upper, p * (_fold(upper, _mm_nt(vals[:, kv], dos)) - delta))
                _unstack_heads(dq_ref, g, _mm_tn(ds, keys[:, kv]) * scale)
                dk_heads.append(_mm(ds, qs))
                dv_heads.append(_mm(_unfold(upper, p), dos))
                at_sink = jnp.exp(_sink_row(sink_ref, g) - lse) * delta
                for n in range(GQA_GROUP):
                    dsink = dsink + jnp.where(_lane_is(g * GQA_GROUP + n), -jnp.sum(at_sink[:, n * ATTN_BLOCK:(n + 1) * ATTN_BLOCK]), 0.0)
            dkk = jnp.concatenate(dk_heads, axis=1)
            dvv = jnp.concatenate(dv_heads, axis=1)
            dsink_ref[...] += dsink
            dk_ref[...] = dk_carry[...] + dkk[:ATTN_BLOCK]
            dv_ref[...] = dv_carry[...] + dvv[:ATTN_BLOCK]
            dk_carry[...] = dkk[ATTN_BLOCK:]
            dv_carry[...] = dvv[ATTN_BLOCK:]

        @pl.when(i == nb)
        def _():
            dk_ref[...] = dk_carry[...]
            dv_ref[...] = dv_carry[...]

    last = nb - 1
    cur = lambda n: pl.BlockSpec((ATTN_BLOCK, n), lambda i: (jnp.minimum(i, last), 0))
    prev = lambda n: pl.BlockSpec((ATTN_BLOCK, n), lambda i: (jnp.clip(i - 1, 0, last), 0))
    late = lambda n: pl.BlockSpec((ATTN_BLOCK, n), lambda i: (i, 0))
    dq, dk_late, dv_late, dsinks = pl.pallas_call(
        body, name="attn_bwd", grid=(nb + 1,),
        in_specs=[pl.BlockSpec(memory_space=pltpu.SMEM), cur(Q_DIM), cur(KV_DIM), prev(KV_DIM), cur(KV_DIM), prev(KV_DIM),
                  cur(Q_DIM), pl.BlockSpec((N_Q_HEADS, ATTN_BLOCK), lambda i: (0, jnp.minimum(i, last))), cur(Q_DIM)],
        out_specs=[cur(Q_DIM), late(KV_DIM), late(KV_DIM), _whole((1, LANES))],
        out_shape=[_sds((t, Q_DIM)), _sds((t + ATTN_BLOCK, KV_DIM)), _sds((t + ATTN_BLOCK, KV_DIM)), _sds((1, LANES))],
        scratch_shapes=[pltpu.VMEM((ATTN_BLOCK, KV_DIM), F32), pltpu.VMEM((ATTN_BLOCK, KV_DIM), F32)],
        compiler_params=_params(("arbitrary",)),
    )(sinks, q, k, k, v, v, o, lse, do)
    return dq, dk_late[ATTN_BLOCK:], dv_late[ATTN_BLOCK:], dsinks


def attn_out_fwd(x, o, gate, w_out):
    t = x.shape[0]
    tm = min(ROW_TILE_FWD, t)

    def body(x_ref, o_ref, gate_ref, w_ref, xo_ref):
        xo_ref[...] = x_ref[...] + _mm(o_ref[...] * _silu(gate_ref[...]), w_ref[...])

    row = _rows(tm, D_MODEL)
    return pl.pallas_call(
        body, name="attn_out_fwd", grid=(t // tm,),
        in_specs=[row, row, row, _whole((Q_DIM, D_MODEL))], out_specs=row, out_shape=_sds((t, D_MODEL)),
        compiler_params=_params(("parallel",)),
    )(x, o, gate, w_out)


def attn_out_bwd(dxo, o, gate, w_out, token=None):
    t = dxo.shape[0]
    tm = min(ROW_TILE_BWD, t)
    extra, extra_specs = _after(token)

    def body(dxo_ref, o_ref, gate_ref, wt_ref, *rest):
        do_ref, dgate_ref, dw_ref = rest[-3:]

        @pl.when(pl.program_id(0) == 0)
        def _():
            dw_ref[...] = jnp.zeros_like(dw_ref)

        dxo = dxo_ref[...]
        o = o_ref[...]
        gate = gate_ref[...]
        sgate, sgate_grad = _silu_and_grad(gate)
        da = _mm_nt(dxo, wt_ref[...])
        dw_ref[...] += _mm_tn(o * sgate, dxo)
        do_ref[...] = da * sgate
        dgate_ref[...] = da * o * sgate_grad

    row = _rows(tm, D_MODEL)
    mat = _whole((Q_DIM, D_MODEL))
    return pl.pallas_call(
        body, name="attn_out_bwd", grid=(t // tm,),
        in_specs=[row, row, row, mat] + extra_specs, out_specs=[row, row, mat],
        out_shape=[_sds((t, Q_DIM)), _sds((t, BRANCH)), _sds((Q_DIM, D_MODEL))],
        compiler_params=_params(("arbitrary",)),
    )(dxo, o, gate, w_out, *extra)


def attn_proj_bwd(x, norm, dxo, dq, dk, dv, dgate, cos2, sin2, w_in_t):
    t = x.shape[0]
    tm = min(ROW_TILE_BWD, t)

    def body(x_ref, g_ref, dxo_ref, dq_ref, dk_ref, dv_ref, dgate_ref, cos_ref, sin_ref, wt_ref, dx_ref, dw_ref, dg_ref):
        @pl.when(pl.program_id(0) == 0)
        def _():
            dw_ref[...] = jnp.zeros_like(dw_ref)
            dg_ref[...] = jnp.zeros_like(dg_ref)

        g = g_ref[...]
        r, xhat, h = _rms(x_ref[...], g)
        cs = cos_ref[...]
        sn = sin_ref[...]
        dqr = dq_ref[...]
        dkr = dk_ref[...]
        dq = dqr * _tile_lanes(cs, Q_DIM // LANES) + _swap_half_heads(dqr * _tile_lanes(sn, Q_DIM // LANES))
        dk = dkr * cs + _swap_half_heads(dkr * sn)
        dproj = jnp.concatenate([dq, dk, dv_ref[...], dgate_ref[...]], axis=1)
        dh = _mm(dproj, wt_ref[...])
        dw_ref[...] += _mm_tn(dproj, h)
        dx, dg = _rms_bwd(dh, g, r, xhat)
        dg_ref[...] += dg
        dx_ref[...] = dxo_ref[...] + dx

    row = _rows(tm, D_MODEL)
    vec = _whole((1, D_MODEL))
    return pl.pallas_call(
        body, name="attn_proj_bwd", grid=(t // tm,),
        in_specs=[row, vec, row, _rows(tm, Q_DIM), _rows(tm, KV_DIM), _rows(tm, KV_DIM), _rows(tm, BRANCH),
                  _rows(tm, LANES), _rows(tm, LANES), _whole((ATTN_N, D_MODEL))],
        out_specs=[row, _whole((ATTN_N, D_MODEL)), vec],
        out_shape=[_sds((t, D_MODEL)), _sds((ATTN_N, D_MODEL)), _sds((1, D_MODEL))],
        compiler_params=_params(("arbitrary",)),
    )(x, norm, dxo, dq, dk, dv, dgate, cos2, sin2, w_in_t)


def attn_out_loss(x, o, gate, w_out, norm, target):
    t = x.shape[0]
    tm = min(ROW_TILE_FWD, t)

    def body(x_ref, o_ref, gate_ref, w_ref, g_ref, tgt_ref, loss_ref, dx_ref, dg_ref):
        @pl.when(pl.program_id(0) == 0)
        def _():
            loss_ref[...] = jnp.zeros_like(loss_ref)
            dg_ref[...] = jnp.zeros_like(dg_ref)

        out = x_ref[...] + _mm(o_ref[...] * _silu(gate_ref[...]), w_ref[...])
        g = g_ref[...]
        r, xhat, y = _rms(out, g)
        err = y - tgt_ref[...]
        loss_ref[...] += 0.5 * jnp.sum(jnp.mean(err * err, axis=-1, keepdims=True), axis=0, keepdims=True)
        dx, dg = _rms_bwd(err * (1.0 / D_MODEL), g, r, xhat)
        dg_ref[...] += dg
        dx_ref[...] = dx

    row = _rows(tm, D_MODEL)
    vec = _whole((1, D_MODEL))
    return pl.pallas_call(
        body, name="attn_out_loss", grid=(t // tm,),
        in_specs=[row, row, row, _whole((Q_DIM, D_MODEL)), vec, row], out_specs=[_whole((1, 1)), row, vec],
        out_shape=[_sds((1, 1)), _sds((t, D_MODEL)), _sds((1, D_MODEL))],
        compiler_params=_params(("arbitrary",)),
    )(x, o, gate, w_out, norm, target)


OCT_TILE = pl.BlockSpec((None, LANES, LANES), lambda b: (b, 0, 0))
N_LAGS = S5_CHUNK + 1


def _cmul(ar, ai, br, bi):
    return ar * br - ai * bi, ar * bi + ai * br


def _cmul_conj(ar, ai, br, bi):
    return ar * br + ai * bi, ar * bi - ai * br


def _mm_f32(a, b, dims):
    return lax.dot_general(a, b, (dims, ((), ())), precision=lax.Precision.HIGH, preferred_element_type=F32)


def _s5_discretise(ar, ai, ls, br, bi):
    dt = jnp.exp(ls)
    xr = ar * dt
    xi = ai * dt
    mag = jnp.exp(xr)
    first = (mag * jnp.cos(xi), mag * jnp.sin(xi))
    powers = [(jnp.ones_like(xr), jnp.zeros_like(xr)), first]
    for _ in range(2, N_LAGS):
        powers.append(_cmul(*powers[-1], *first))
    den = ar * ar + ai * ai
    nr = powers[1][0] - 1.0
    ni = powers[1][1]
    fr = (nr * ar + ni * ai) / den
    fi = (ni * ar - nr * ai) / den
    bbr, bbi = _cmul(fr, fi, br, bi)
    return dt, powers, (fr, fi), (bbr, bbi), den


def _same_group_tile():
    row = lax.broadcasted_iota(jnp.int32, (LANES, LANES), 0)
    col = lax.broadcasted_iota(jnp.int32, (LANES, LANES), 1)
    return (row // SSM_GROUP) == (col // SSM_GROUP)


def _first_copy_lanes():
    return lax.broadcasted_iota(jnp.int32, (LANES, LANES), 1) < SSM_STATE


def s5_param_fwd(tiles, token=None):
    extra, extra_specs = _after(token)

    def body(ar_ref, ai_ref, ls_ref, br_ref, bi_ref, cr_ref, ci_ref, *rest):
        kd_ref, wsr_ref, wsi_ref, wor_ref, woi_ref, pr_ref, pi_ref = rest[-7:]
        cr = cr_ref[...]
        ci = ci_ref[...]
        _, powers, _, (bbr, bbi), _ = _s5_discretise(ar_ref[...], ai_ref[...], ls_ref[...], br_ref[...], bi_ref[...])
        once = _first_copy_lanes()
        crm = jnp.where(once, cr, 0.0)
        cim = jnp.where(once, ci, 0.0)
        same = _same_group_tile()
        for lag in range(S5_CHUNK):
            er, ei = powers[lag]
            xr, xi = _cmul(er, ei, bbr, bbi)
            rows = pl.ds((S5_CHUNK - 1 - lag) * LANES, LANES)
            wsr_ref[rows, :] = xr
            wsi_ref[rows, :] = xi
        k = _mm_f32(wsr_ref[...], crm, ((1,), (1,))) - _mm_f32(wsi_ref[...], cim, ((1,), (1,)))
        for lag in range(S5_CHUNK):
            kd_ref[lag] = jnp.where(same, k[(S5_CHUNK - 1 - lag) * LANES:(S5_CHUNK - lag) * LANES], 0.0)
        for t in range(S5_CHUNK):
            er, ei = powers[t + 1]
            zr, zi = _cmul(er, ei, cr, ci)
            wor_ref[pl.ds(t * LANES, LANES), :] = zr
            woi_ref[pl.ds(t * LANES, LANES), :] = -zi
        pr_ref[...] = powers[S5_CHUNK][0]
        pi_ref[...] = powers[S5_CHUNK][1]

    return pl.pallas_call(
        body, name="s5_param_fwd", grid=(S5_OCTETS,),
        in_specs=[OCT_TILE] * 7 + [ANY] * len(extra),
        out_specs=[OCT_KD, OCT_W, OCT_W, OCT_W, OCT_W, OCT_TILE, OCT_TILE],
        out_shape=[_sds((S5_OCTETS, S5_CHUNK, LANES, LANES))] + [_sds((S5_OCTETS, S5_OCT_IN, LANES))] * 4
                  + [_sds((S5_OCTETS, LANES, LANES))] * 2,
        compiler_params=_params(("parallel",)),
    )(*tiles, *extra)


def s5_param_bwd(tiles, dkd, dws_re, dws_im, dwo_re, dwo_im, dp_re, dp_im):
    def body(ar_ref, ai_ref, ls_ref, br_ref, bi_ref, cr_ref, ci_ref, dkd_ref, dwsr_ref, dwsi_ref, dwor_ref, dwoi_ref, dpr_ref, dpi_ref,
             dar_ref, dai_ref, dls_ref, dbr_ref, dbi_ref, dcr_ref, dci_ref):
        ar = ar_ref[...]
        ai = ai_ref[...]
        br = br_ref[...]
        bi = bi_ref[...]
        cr = cr_ref[...]
        ci = ci_ref[...]
        dt, powers, (fr, fi), (bbr, bbi), den = _s5_discretise(ar, ai, ls_ref[...], br, bi)
        once = _first_copy_lanes()
        crm = jnp.where(once, cr, 0.0)
        cim = jnp.where(once, ci, 0.0)
        same = _same_group_tile()
        zero = jnp.zeros((LANES, LANES), F32)
        dpow = [[zero, zero] for _ in range(N_LAGS)]
        dbbr, dbbi = zero, zero
        by_step = [S5_CHUNK - 1 - s for s in range(S5_CHUNK)]
        x_all = [_cmul(*powers[lag], bbr, bbi) for lag in by_step]
        xr_all = jnp.concatenate([x[0] for x in x_all], axis=0)
        xi_all = jnp.concatenate([x[1] for x in x_all], axis=0)
        g_all = jnp.concatenate([jnp.where(same, dkd_ref[lag], 0.0) for lag in by_step], axis=0)
        dxr_all = dwsr_ref[...] + _mm_f32(g_all, crm, ((1,), (0,)))
        dxi_all = dwsi_ref[...] - _mm_f32(g_all, cim, ((1,), (0,)))
        dcr = jnp.where(once, _mm_f32(g_all, xr_all, ((0,), (0,))), 0.0)
        dci = -jnp.where(once, _mm_f32(g_all, xi_all, ((0,), (0,))), 0.0)
        for lag in range(S5_CHUNK):
            er, ei = powers[lag]
            rows = slice((S5_CHUNK - 1 - lag) * LANES, (S5_CHUNK - lag) * LANES)
            dxr = dxr_all[rows]
            dxi = dxi_all[rows]
            a, b = _cmul_conj(bbr, bbi, dxr, dxi)
            dpow[lag][0] = dpow[lag][0] + a
            dpow[lag][1] = dpow[lag][1] + b
            a, b = _cmul_conj(er, ei, dxr, dxi)
            dbbr = dbbr + a
            dbbi = dbbi + b
        for t in range(S5_CHUNK):
            er, ei = powers[t + 1]
            dzr = dwor_ref[pl.ds(t * LANES, LANES), :]
            dzi = -dwoi_ref[pl.ds(t * LANES, LANES), :]
            a, b = _cmul_conj(cr, ci, dzr, dzi)
            dpow[t + 1][0] = dpow[t + 1][0] + a
            dpow[t + 1][1] = dpow[t + 1][1] + b
            a, b = _cmul_conj(er, ei, dzr, dzi)
            dcr = dcr + a
            dci = dci + b
        dpow[S5_CHUNK][0] = dpow[S5_CHUNK][0] + dpr_ref[...]
        dpow[S5_CHUNK][1] = dpow[S5_CHUNK][1] + dpi_ref[...]
        dfr, dfi = _cmul_conj(br, bi, dbbr, dbbi)
        dbr, dbi = _cmul_conj(fr, fi, dbbr, dbbi)
        dnr, dni = _cmul(ar / den, ai / den, dfr, dfi)
        qr = (fr * ar + fi * ai) / den
        qi = (fi * ar - fr * ai) / den
        dlr, dli = _cmul(-qr, qi, dfr, dfi)
        dpow[1][0] = dpow[1][0] + dnr
        dpow[1][1] = dpow[1][1] + dni
        dxr, dxi = zero, zero
        for lag in range(1, N_LAGS):
            a, b = _cmul_conj(powers[lag][0], powers[lag][1], dpow[lag][0], dpow[lag][1])
            dxr = dxr + lag * a
            dxi = dxi + lag * b
        dar_ref[...] = dlr + dt * dxr
        dai_ref[...] = dli + dt * dxi
        dls_ref[...] = dt * (ar * dxr + ai * dxi)
        dbr_ref[...] = dbr
        dbi_ref[...] = dbi
        dcr_ref[...] = dcr
        dci_ref[...] = dci

    return pl.pallas_call(
        body, name="s5_param_bwd", grid=(S5_OCTETS,),
        in_specs=[OCT_TILE] * 7 + [OCT_KD, OCT_W, OCT_W, OCT_W, OCT_W, OCT_TILE, OCT_TILE], out_specs=[OCT_TILE] * 7,
        out_shape=[_sds((S5_OCTETS, LANES, LANES))] * 7,
        compiler_params=_params(("parallel",)),
    )(*tiles, dkd, dws_re, dws_im, dwo_re, dwo_im, dp_re, dp_im)


def _doubled(v):
    return jnp.concatenate([v, v], axis=-1)


def _s5_param_tiles(a_re, a_im, log_step, b_re, b_im, c_re, c_im):
    def per_group(a):
        return _doubled(jnp.broadcast_to(a.reshape(S5_OCTETS, S5_OCT, 1, SSM_STATE),
                                         (S5_OCTETS, S5_OCT, SSM_GROUP, SSM_STATE)).reshape(S5_OCTETS, LANES, SSM_STATE))

    ls = jnp.broadcast_to(log_step.reshape(S5_OCTETS, S5_OCT, 1, 1), (S5_OCTETS, S5_OCT, SSM_GROUP, LANES)).reshape(S5_OCTETS, LANES, LANES)
    bt = lambda b: _doubled(b.transpose(0, 2, 1).reshape(S5_OCTETS, LANES, SSM_STATE))
    ct = lambda c: _doubled(c.reshape(S5_OCTETS, LANES, SSM_STATE))
    return [per_group(a_re), per_group(a_im), ls, bt(b_re), bt(b_im), ct(c_re), ct(c_im)]


def _s5_param_grads(dtiles):
    dar, dai, dls, dbr, dbi, dcr, dci = dtiles
    halves = lambda d: d[..., :SSM_STATE] + d[..., SSM_STATE:]
    per_group = lambda d: halves(d).reshape(SSM_GROUPS, SSM_GROUP, SSM_STATE).sum(axis=1)
    per_row = lambda d: halves(d).reshape(SSM_GROUPS, SSM_GROUP, SSM_STATE)
    return (per_group(dar), per_group(dai), dls.reshape(SSM_GROUPS, SSM_GROUP * LANES).sum(axis=1),
            per_row(dbr).transpose(0, 2, 1), per_row(dbi).transpose(0, 2, 1), per_row(dcr), per_row(dci))


def _group_power_rows(tile):
    return tile[:, ::SSM_GROUP, :SSM_STATE].reshape(1, S5_STATES)


def _group_power_tiles(row):
    t = jnp.pad(row.reshape(S5_OCTETS, S5_OCT, 1, SSM_STATE), ((0, 0), (0, 0), (0, SSM_GROUP - 1), (0, LANES - SSM_STATE)))
    return t.reshape(S5_OCTETS, LANES, LANES)


def _rope_tables(t):
    pos = jnp.arange(t, dtype=F32)
    inv_freq = ROPE_THETA ** (-jnp.arange(0, HEAD_DIM, 2, dtype=F32) / HEAD_DIM)
    ang = pos[:, None] * inv_freq[None, :]
    cos = jnp.cos(ang)
    sin = jnp.sin(ang)
    cos64 = jnp.concatenate([cos, cos], axis=1)
    sin64 = jnp.concatenate([-sin, sin], axis=1)
    return jnp.concatenate([cos64, cos64], axis=1), jnp.concatenate([sin64, sin64], axis=1)


def _row(v):
    return v.reshape(1, -1)


def _s5_matrices(w, token=None):
    tiles = _s5_param_tiles(w["a_re"], w["a_im"], w["log_step"], w["b_re"], w["b_im"], w["c_re"], w["c_im"])
    kd, ws_re, ws_im, wo_re, wo_im, p_re, p_im = s5_param_fwd(tiles, token)
    return tiles, dict(kd=kd, ws_re=ws_re, ws_im=ws_im, wo_re=wo_re, wo_im=wo_im, a_re=_group_power_rows(p_re), a_im=_group_power_rows(p_im))


def _ssm_forward(x, w):
    tiles, mats = w["s5"] if "s5" in w else _s5_matrices(w)
    u, gate = ssm_proj_fwd(x, _row(w["norm"]), w["w_in"])
    s_re, s_im = s5_chunk_states(u, mats["ws_re"], mats["ws_im"])
    h_re, h_im = s5_scan_fwd(s_re, s_im, mats["a_re"], mats["a_im"])
    y_scan = s5_outputs(u, h_re, h_im, mats["kd"], mats["wo_re"], mats["wo_im"])
    y, g2, x_new = ssm_mix_fwd(x, u, gate, y_scan, _row(w["d"]), w["w_glu"], _row(w["b_glu"]), w["w_out"])
    saved = dict(x=x, u=u, gate=gate, y=y, g2=g2, h_re=h_re, h_im=h_im, mats=mats, tiles=tiles)
    return x_new, saved


def _ssm_backward(dxo, w, s, token=None, early=None):
    dy, dgate, dw_out, dw_glu, db_glu, dd = ssm_mix_bwd(dxo, s["u"], s["gate"], s["y"], s["g2"], w["w_glu"], w["w_out"], token)
    mats = s["mats"]
    started = early(dict(w_glu=dw_glu, w_out=dw_out)) if early else None
    dh_re, dh_im = s5_state_grads(dy, mats["wo_re"], mats["wo_im"], started)
    ds_re, ds_im, da_re, da_im = s5_scan_bwd(dh_re, dh_im, s["h_re"], s["h_im"], mats["a_re"], mats["a_im"])
    du_scan = s5_input_grads(dy, ds_re, ds_im, mats["kd"], mats["ws_re"], mats["ws_im"])
    dkd, dws_re, dws_im, dwo_re, dwo_im = s5_weight_grads(s["u"], dy, s["h_re"], s["h_im"], ds_re, ds_im)
    dparams = _s5_param_grads(s5_param_bwd(s["tiles"], dkd, dws_re, dws_im, dwo_re, dwo_im,
                                           _group_power_tiles(da_re), _group_power_tiles(da_im)))
    dx, dw_in, dnorm = ssm_proj_bwd(s["x"], _row(w["norm"]), dxo, dy, du_scan, dgate, _row(w["d"]), w["w_in"])
    grads = dict(norm=dnorm, w_in=dw_in, d=dd, w_glu=dw_glu, b_glu=db_glu, w_out=dw_out)
    for name, val in zip(("a_re", "a_im", "log_step", "b_re", "b_im", "c_re", "c_im"), dparams):
        grads[name] = val
    return dx, grads


def _attn_forward(x, w, cos2, sin2, loss_head=None):
    q, k, v, gate = attn_proj_fwd(x, _row(w["norm"]), w["w_in"], cos2, sin2)
    o, lse = attn_fwd(q, k, v, w["sinks"])
    if loss_head is None:
        result = attn_out_fwd(x, o, gate, w["w_out"])
    else:
        result = attn_out_loss(x, o, gate, w["w_out"], _row(loss_head[0]), loss_head[1])
    return result, dict(x=x, q=q, k=k, v=v, gate=gate, o=o, lse=lse)


def _attn_backward(dxo, w, s, cos2, sin2, token=None):
    do, dgate, dw_out = attn_out_bwd(dxo, s["o"], s["gate"], w["w_out"], token)
    dq, dk, dv, dsinks = attn_bwd(s["q"], s["k"], s["v"], w["sinks"], s["o"], s["lse"], do)
    dx, dw_in, dnorm = attn_proj_bwd(s["x"], _row(w["norm"]), dxo, dq, dk, dv, dgate, cos2, sin2, w["w_in"])
    return dx, dict(norm=dnorm, w_in=dw_in, sinks=dsinks[0, :N_Q_HEADS], w_out=dw_out)


class _NoExchanges:
    def __init__(self, layers):
        self.layers = layers

    def layer(self, i, x):
        return self.layers[i]

    def early_grads(self, i, grads):
        return None

    def layer_done(self, i, grads, dx):
        return None


def _sequence_step(x, target, final_norm, hooks, depth=4):
    cos2, sin2 = _rope_tables(x.shape[0])
    saved, layers = [], []
    for i in range(depth):
        w = hooks.layer(i, x)
        layers.append(w)
        if i % 2 == 0:
            x, s = _ssm_forward(x, w)
        else:
            x, s = _attn_forward(x, w, cos2, sin2, (final_norm, target) if i == depth - 1 else None)
        saved.append(s)
    loss, dx, dfinal = x
    grads = {"final_norm": dfinal}
    token = None
    for i in reversed(range(depth)):
        if i % 2 == 0:
            dx, g = _ssm_backward(dx, layers[i], saved[i], token, functools.partial(hooks.early_grads, i))
        else:
            dx, g = _attn_backward(dx, layers[i], saved[i], cos2, sin2, token)
        g = {"l%d_%s" % (i, name): val for name, val in g.items()}
        grads.update(g)
        token = hooks.layer_done(i, g, dx)
    return loss[0, 0], dx, grads


ANY = pl.BlockSpec(memory_space=pl.ANY)


def _place():
    return lax.axis_index("x"), lax.axis_index("y"), lax.axis_index("c")


def _other_chips(x, y):
    return [(1 - x, y), (x, 1 - y), (1 - x, 1 - y)]


class _StagedCopies:
    def __init__(self, bufs, load_sems, store_sems):
        self.bufs, self.load_sems, self.store_sems = bufs, load_sems, store_sems
        self.loads, self.stores = [], []

    def load(self, i, src):
        cp = pltpu.make_async_copy(src, self.bufs[i], self.load_sems.at[i])
        cp.start()
        self.loads.append(cp)

    def store(self, i, dst):
        self.loads[i].wait()
        cp = pltpu.make_async_copy(self.bufs[i], dst, self.store_sems.at[i])
        cp.start()
        self.stores.append(cp)

    def finish(self):
        for cp in self.stores:
            cp.wait()


def _staging(blocks):
    n = len(blocks)
    return [pltpu.VMEM(b.shape, b.dtype) for b in blocks] + [pltpu.SemaphoreType.DMA((n,)), pltpu.SemaphoreType.DMA((n,))]


def exchange_halves_with_sibling(grads):
    n = len(grads)

    def body(*refs):
        ins, outs = refs[:n], refs[n:2 * n]
        send_sems, recv_sems = refs[2 * n:]
        x, y, c = _place()
        copies = []
        for i in range(n):
            half = ins[i].shape[1] // 2
            src = ins[i].at[:, pl.ds((1 - c) * half, half), :]
            cp = pltpu.make_async_remote_copy(src_ref=src, dst_ref=outs[i], send_sem=send_sems.at[i], recv_sem=recv_sems.at[i],
                                              device_id=(x, y, 1 - c), device_id_type=MESH)
            cp.start()
            copies.append(cp)
        for cp in copies:
            cp.wait()

    return pl.pallas_call(
        body, name="exchange_halves_with_sibling",
        in_specs=[ANY] * n, out_specs=[ANY] * n,
        out_shape=[_sds((g.shape[0], g.shape[1] // 2, g.shape[2])) for g in grads],
        scratch_shapes=[pltpu.SemaphoreType.DMA((n,)), pltpu.SemaphoreType.DMA((n,))],
    )(*grads)


def swap_halves_with_sibling(pieces):
    n = len(pieces)

    def body(*refs):
        ins, outs = refs[:n], refs[n:2 * n]
        send_sems, recv_sems = refs[2 * n:2 * n + 2]
        own = _StagedCopies(refs[2 * n + 2:3 * n + 2], *refs[3 * n + 2:])
        x, y, c = _place()
        for i in range(n):
            own.load(i, ins[i])
        swaps = []
        for i in range(n):
            cp = pltpu.make_async_remote_copy(src_ref=ins[i], dst_ref=outs[i].at[c], send_sem=send_sems.at[i], recv_sem=recv_sems.at[i],
                                              device_id=(x, y, 1 - c), device_id_type=MESH)
            cp.start()
            swaps.append(cp)
        for i in range(n):
            own.store(i, outs[i].at[c])
        for i in range(n):
            pltpu.make_async_remote_copy(src_ref=ins[i], dst_ref=outs[i].at[1 - c], send_sem=send_sems.at[i], recv_sem=recv_sems.at[i],
                                         device_id=(x, y, 1 - c), device_id_type=MESH).wait_recv()
        for cp in swaps:
            cp.wait_send()
        own.finish()

    return pl.pallas_call(
        body, name="swap_halves_with_sibling",
        in_specs=[ANY] * n, out_specs=[ANY] * n,
        out_shape=[_sds((2,) + p.shape) for p in pieces],
        scratch_shapes=[pltpu.SemaphoreType.DMA((n,)), pltpu.SemaphoreType.DMA((n,))] + _staging(pieces),
        compiler_params=_params(),
    )(*pieces)


def pass_halves_to_sibling(stacks):
    n = len(stacks)

    def body(*refs):
        outs = refs[n:2 * n]
        send_sems, recv_sems = refs[2 * n:]
        x, y, c = _place()
        sends = []
        for i in range(n):
            for k, (tx, ty) in enumerate(_other_chips(x, y)):
                mine = _rows_of_core(outs[i].at[2 * tx + ty], c, True)
                cp = pltpu.make_async_remote_copy(src_ref=mine, dst_ref=mine, send_sem=send_sems.at[i, k], recv_sem=recv_sems.at[i, k],
                                                  device_id=(x, y, 1 - c), device_id_type=MESH)
                cp.start()
                sends.append(cp)
        for i in range(n):
            for k, (tx, ty) in enumerate(_other_chips(x, y)):
                missing = _rows_of_core(outs[i].at[2 * tx + ty], 1 - c, True)
                pltpu.make_async_remote_copy(src_ref=missing, dst_ref=missing, send_sem=send_sems.at[i, k], recv_sem=recv_sems.at[i, k],
                                             device_id=(x, y, 1 - c), device_id_type=MESH).wait_recv()
        for cp in sends:
            cp.wait_send()

    sems = pltpu.SemaphoreType.DMA((n, 3))
    return pl.pallas_call(
        body, name="pass_halves_to_sibling", in_specs=[ANY] * n, out_specs=[ANY] * n,
        out_shape=[_sds(s.shape, s.dtype) for s in stacks], input_output_aliases={i: i for i in range(n)},
        scratch_shapes=[sems, sems],
    )(*stacks)


IN_HBM = pl.BlockSpec(memory_space=pltpu.HBM)
SEMAPHORES = pl.BlockSpec(memory_space=pltpu.SEMAPHORE)
DATAFLOW = pltpu.SideEffectType.DATAFLOW_SIDE_EFFECTING


def _hbm(a):
    return pltpu.with_memory_space_constraint(a, pltpu.HBM)


def place_own_blocks(shards):
    n = len(shards)

    def body(*refs):
        ins, outs = refs[:n], refs[n:2 * n]
        own = _StagedCopies(refs[2 * n:3 * n], *refs[3 * n:])
        x, y, _ = _place()
        for i in range(n):
            own.load(i, ins[i])
        for i in range(n):
            own.store(i, outs[i].at[2 * x + y])
        own.finish()

    return pl.pallas_call(
        body, name="place_own_blocks", in_specs=[ANY] * n, out_specs=[ANY] * n,
        out_shape=[_sds((4,) + s.shape, s.dtype) for s in shards],
        scratch_shapes=_staging(shards), compiler_params=_params(),
    )(*shards)


def _block_to_send(ref, chip, per_target):
    if not per_target:
        return ref
    return ref.at[chip] if ref.shape[0] == 4 else ref.at[0]


def _rows_of_core(ref, c, core_half):
    if not core_half:
        return ref
    rows = ref.shape[0] // 2
    return ref.at[pl.ds(c * rows, rows), :]


def start_sends_to_chips(name, sources, landings, per_target, after, core_half=False):
    n = len(sources)
    n_sems = 2 * 3 * n

    def body(*refs):
        srcs = refs[:n]
        sems = refs[2 * n + 1:2 * n + 1 + n_sems]
        lands = refs[2 * n + 1 + n_sems:3 * n + 1 + n_sems]
        token = refs[3 * n + 1 + n_sems]
        x, y, c = _place()
        me = 2 * x + y
        for i in range(n):
            for k, (tx, ty) in enumerate(_other_chips(x, y)):
                src = _rows_of_core(_block_to_send(srcs[i], 2 * tx + ty, per_target), c, core_half)
                dst = _rows_of_core(lands[i].at[me], c, core_half)
                pltpu.make_async_remote_copy(src_ref=src, dst_ref=dst, send_sem=sems[2 * (3 * i + k)], recv_sem=sems[2 * (3 * i + k) + 1],
                                             device_id=(tx, ty, c), device_id_type=MESH).start()
        token[...] = jnp.zeros_like(token)

    outs = pl.pallas_call(
        body, name=name,
        in_specs=[IN_HBM] * (2 * n) + [ANY],
        out_specs=[SEMAPHORES] * n_sems + [IN_HBM] * n + [pl.BlockSpec(memory_space=pltpu.VMEM)],
        out_shape=[pltpu.SemaphoreType.DMA(())] * n_sems + [pltpu.HBM(l.shape, l.dtype) for l in landings] + [_sds(TOKEN_SHAPE)],
        input_output_aliases={n + i: n_sems + i for i in range(n)},
        compiler_params=pltpu.CompilerParams(has_side_effects=DATAFLOW),
    )(*[_hbm(s) for s in sources], *[_hbm(l) for l in landings], after)
    return list(outs[:n_sems]), list(outs[n_sems:n_sems + n]), outs[n_sems + n]


def wait_sends_to_chips(name, sources, landings, sems, per_target, after, core_half=False):
    n = len(sources)
    n_sems = len(sems)

    def body(*refs):
        srcs = refs[:n]
        sem_refs = refs[2 * n:2 * n + n_sems]
        lands = refs[2 * n + n_sems + 1:]
        x, y, c = _place()
        me = 2 * x + y
        for i in range(n):
            for k, (tx, ty) in enumerate(_other_chips(x, y)):
                src = _rows_of_core(_block_to_send(srcs[i], me, per_target), c, core_half)
                dst = _rows_of_core(lands[i].at[2 * tx + ty], c, core_half)
                cp = pltpu.make_async_remote_copy(src_ref=src, dst_ref=dst, send_sem=sem_refs[2 * (3 * i + k)],
                                                  recv_sem=sem_refs[2 * (3 * i + k) + 1], device_id=(tx, ty, c), device_id_type=MESH)
                cp.wait_send()
                cp.wait_recv()

    return pl.pallas_call(
        body, name=name,
        in_specs=[IN_HBM] * (2 * n) + [SEMAPHORES] * n_sems + [ANY],
        out_specs=[IN_HBM] * n,
        out_shape=[pltpu.HBM(l.shape, l.dtype) for l in landings],
        input_output_aliases={n + i: i for i in range(n)},
        compiler_params=pltpu.CompilerParams(has_side_effects=DATAFLOW),
    )(*[_hbm(s) for s in sources], *landings, *sems, after)


def _row_tile(rows, cols):
    tm = rows
    while tm * cols * 4 > (2 << 20) and tm % 16 == 0:
        tm //= 2
    return tm


def add_pairs(half, a_list, b_list, out_dtypes, copies=1):
    n = len(a_list)
    nb = a_list[0].shape[0]

    def body(half_ref, *refs):
        for i in range(n):
            total = (refs[i][...] + refs[n + i][...]).astype(out_dtypes[i])
            for o_ref in refs[2 * n + i * copies:2 * n + (i + 1) * copies]:
                o_ref[...] = total

    halves = [pl.BlockSpec((None,) + b.shape[1:], lambda j, h: (j, h[0], 0)) for b in b_list]
    whole = [pl.BlockSpec((None,) + b.shape[1:], lambda j, h: (j, 0, 0)) for b in b_list]
    outs = pl.pallas_call(
        body, name="add_pairs",
        grid_spec=pltpu.PrefetchScalarGridSpec(num_scalar_prefetch=1, grid=(nb,), in_specs=halves + whole,
                                               out_specs=[s for s in whole for _ in range(copies)]),
        out_shape=[_sds(b.shape, dt) for b, dt in zip(b_list, out_dtypes) for _ in range(copies)],
        compiler_params=_params(("parallel",)),
    )(half, *a_list, *b_list)
    return [tuple(outs[i * copies:(i + 1) * copies]) for i in range(n)]


def sum_fours(arrays, token=None):
    n = len(arrays)
    extra, extra_specs = _after(token)
    steps = 2 if all(a.shape[1] % 32 == 0 for a in arrays) else 1

    def body(*refs):
        outs = refs[-n:]
        for a_ref, o_ref in zip(refs[:n], outs):
            o_ref[...] = ((a_ref[0].astype(F32) + a_ref[1].astype(F32)) + a_ref[2].astype(F32)) + a_ref[3].astype(F32)

    return pl.pallas_call(
        body, name="sum_fours", grid=(steps,),
        in_specs=[pl.BlockSpec((4, a.shape[1] // steps, a.shape[2]), lambda i: (0, i, 0)) for a in arrays] + extra_specs,
        out_specs=[pl.BlockSpec((a.shape[1] // steps, a.shape[2]), lambda i: (i, 0)) for a in arrays],
        out_shape=[_sds(a.shape[1:]) for a in arrays], compiler_params=_params(("parallel",)),
    )(*arrays, *extra)


def _adamw_update(w_ref, g_ref, m_ref, v_ref, d_ref, nm_ref, nv_ref):
    g = g_ref[...]
    nm = ADAM_B1 * m_ref[...] + (1.0 - ADAM_B1) * g
    nv = ADAM_B2 * v_ref[...] + (1.0 - ADAM_B2) * (g * g)
    d_ref[...] = -ADAM_LR * ((nm / (1.0 - ADAM_B1 ** ADAM_STEP)) / (jnp.sqrt(nv / (1.0 - ADAM_B2 ** ADAM_STEP)) + ADAM_EPS) + ADAM_WD * w_ref[...])
    nm_ref[...] = nm
    nv_ref[...] = nv


def adamw(w, g, m, v):
    rows, cols = w.shape
    tm = _row_tile(rows, cols)

    def body(*refs):
        _adamw_update(*refs)

    spec = pl.BlockSpec((tm, cols), lambda i: (i, 0))
    return pl.pallas_call(
        body, name="adamw", grid=(rows // tm,), in_specs=[spec] * 4, out_specs=[spec] * 3,
        out_shape=[_sds(w.shape)] * 3, compiler_params=_params(("parallel",)),
    )(w, g, m, v)


def adamw_small(ws, gs, ms, vs, slabs=None):
    n = len(ws)

    def body(*refs):
        for i in range(n):
            _adamw_update(refs[i], refs[n + i], refs[2 * n + i], refs[3 * n + i], refs[4 * n + i], refs[5 * n + i], refs[6 * n + i])

    if slabs is None:
        grid = ()
        specs = [pl.BlockSpec(memory_space=pltpu.VMEM)] * n
    else:
        grid = (slabs,)
        specs = [pl.BlockSpec((w.shape[0] // slabs,) + w.shape[1:], lambda i: (i, 0, 0)) for w in ws]
    outs = pl.pallas_call(
        body, name="adamw_small", grid=grid, in_specs=specs * 4, out_specs=specs * 3,
        out_shape=[_sds(w.shape) for w in ws] * 3, compiler_params=_params(("parallel",) if slabs else None),
    )(*ws, *gs, *ms, *vs)
    return outs[:n], outs[n:2 * n], outs[2 * n:]


PACK_TILE = 8 * LANES
PACK_PIECES = 8
PACK_ALIGN = PACK_PIECES * 16


def _pack_small(values, scalar=None):
    parts = []
    for name in PACK_NAMES:
        flat = values[name].reshape(-1)
        pad = (-flat.shape[0]) % PACK_TILE
        if pad:
            flat = jnp.concatenate([flat, jnp.zeros((pad,), F32)])
        parts.append(flat.reshape(-1, LANES))
    rows = sum(p.shape[0] for p in parts) + 8
    parts.append(jnp.zeros(((-rows) % PACK_ALIGN, LANES), F32))
    last = jnp.zeros((8, LANES), F32)
    parts.append(last if scalar is None else jnp.broadcast_to(scalar.astype(F32), (8, LANES)))
    return jnp.concatenate(parts, axis=0)


def _pack_row_of(name, like):
    row = 0
    for other in PACK_NAMES:
        if other == name:
            return row
        row += -(-math.prod(like[other].shape) // PACK_TILE) * 8
    raise KeyError(name)


def _unpack_small(pack, like):
    out = {}
    row = 0
    for name in PACK_NAMES:
        size = math.prod(like[name].shape)
        rows = -(-size // PACK_TILE) * 8
        out[name] = pack[row:row + rows].reshape(-1)[:size].reshape(like[name].shape)
        row += rows
    return out


def _travels_transposed(name, shard):
    return name.endswith("w_in") and shard.shape[-1] % LANES != 0


def _to_blocks(name, full):
    if full.ndim == 3:
        return full
    return full.reshape(4, full.shape[0] // 4, full.shape[1])


def _from_blocks(name, stacked):
    if name.endswith("w_in") and stacked.shape[2] % LANES == 0 and stacked.shape[1] == D_MODEL:
        return stacked
    return stacked.reshape(4 * stacked.shape[1], stacked.shape[2])


def _layer_big_names(i):
    return [n for n in BIG_NAMES if n.startswith("l%d_" % i)]


class _OverlappedExchanges:
    def __init__(self, weights):
        self.weights = weights
        self.c = lax.axis_index("c")
        self.first = _layer_big_names(0)
        self.later = [n for n in BIG_NAMES if n not in self.first]
        shards = [weights[n].astype(MXU_DTYPE) for n in self.first + self.later]
        shards = [s.T if _travels_transposed(n, s) else s for n, s in zip(self.first + self.later, shards)]
        placed = place_own_blocks(shards)
        k = len(self.first)
        sems, stacks, token = start_sends_to_chips("gather_first_start", shards[:k], placed[:k], False, shards[0], core_half=True)
        self.gather_first = (shards[:k], sems, stacks)
        sems, stacks, token = start_sends_to_chips("gather_later_start", shards[k:], placed[k:], False, token)
        self.gather_later = (shards[k:], sems, stacks)
        self.s5 = {}
        for i in (0, 2):
            self.s5[i] = _s5_matrices({n: weights["l%d_%s" % (i, n)] for n in SSM_NAMES if "l%d_%s" % (i, n) in SMALL_NAMES}, token)
            token = self.s5[i][1]["kd"]
        self.full = {}
        self.held = {}
        self.in_flight = {}
        self.contributions = {}

    def layer(self, i, x):
        if i == 0:
            shards, sems, stacks = self.gather_first
            stacks = wait_sends_to_chips("gather_first_wait", shards, stacks, sems, False, self.s5[2][1]["kd"], core_half=True)
            stacks = pass_halves_to_sibling(stacks)
            self.full.update({n: _from_blocks(n, g) for n, g in zip(self.first, stacks)})
        if i == 1:
            shards, sems, stacks = self.gather_later
            stacks = wait_sends_to_chips("gather_later_wait", shards, stacks, sems, False, x)
            self.full.update({n: _from_blocks(n, g) for n, g in zip(self.later, stacks)})
        names = SSM_NAMES if i % 2 == 0 else ATTN_NAMES
        w = {n: self.full.get("l%d_%s" % (i, n), self.weights.get("l%d_%s" % (i, n))) for n in names}
        if i in self.s5:
            w["s5"] = self.s5[i]
        return w

    def chip_sums(self, names, grads, extra_blocks=(), extra_dtypes=(), copies=1):
        blocks = [_to_blocks(n, grads[n]) for n in names] + list(extra_blocks)
        from_sibling = exchange_halves_with_sibling(blocks)
        k = len(names)
        half = self.c.reshape(1).astype(jnp.int32)
        sums = add_pairs(half, blocks[:k], from_sibling[:k], [WIRE_DTYPE] * k, copies)
        if extra_blocks:
            sums += add_pairs(half, blocks[k:], from_sibling[k:], list(extra_dtypes), copies)
        return sums

    def start_scatter(self, tag, names, grads):
        pairs = self.chip_sums(names, grads, copies=2)
        sums = [p[0] for p in pairs]
        sems, landings, token = start_sends_to_chips("scatter_start_" + tag, sums, [p[1] for p in pairs], True, sums[0])
        self.in_flight[tag] = (names, sums, sems, landings)
        return token

    def wait_scatter(self, tag, after):
        if tag in self.in_flight:
            names, sums, sems, landings = self.in_flight.pop(tag)
            done = wait_sends_to_chips("scatter_wait_" + tag, sums, landings, sems, True, after)
            self.contributions.update(zip(names, done))


    def early_grads(self, i, grads):
        if i != 0:
            return None
        self.held.update({"l0_" + n: g for n, g in grads.items()})
        names = _layer_big_names(1) + ["l0_w_glu", "l0_w_out"]
        return self.start_scatter("l1_l0", names, self.held)

    def layer_done(self, i, grads, dx):
        self.held.update(grads)
        if i == 2:
            return self.start_scatter("l3_l2", _layer_big_names(3) + _layer_big_names(2), self.held)
        if i == 1:
            self.wait_scatter("l3_l2", dx)
        if i == 0:
            self.wait_scatter("l1_l0", dx)
        return None


def _train_step(x, loss_target, weights, moments_m, moments_v):
    hooks = _OverlappedExchanges(weights)
    loss, dx, grads = _sequence_step(x[0], loss_target[0], weights["final_norm"], hooks)
    small_pack = _pack_small({n: grads[n] for n in SMALL_NAMES}, scalar=loss)
    tail = small_pack.shape[0] - _pack_row_of(EXACT_NAMES[0], grads)
    small = [small_pack[None, :-tail], small_pack[None, -tail:]]
    last = [n for n in _layer_big_names(0) if n not in hooks.contributions]
    pairs = hooks.chip_sums(last, grads, extra_blocks=small, extra_dtypes=[WIRE_DTYPE, F32], copies=2)
    sums = [p[0] for p in pairs]
    landings = [p[1] for p in pairs[:-2]] + [jnp.broadcast_to(s, (4,) + s.shape[1:]) for s in sums[-2:]]
    sems, landings, token = start_sends_to_chips("scatter_start_l0", sums, landings, True, sums[0])
    out_grad, out_delta, out_m, out_v = {}, {}, {}, {}

    def finish(names, arrays, token=None):
        shared = swap_halves_with_sibling(sum_fours(arrays, token))
        rest = []
        for n, s in zip(names, shared):
            if n not in weights:
                rest.append(s.reshape(-1, LANES))
                continue
            out_grad[n] = s.reshape(2 * s.shape[1], s.shape[2])
            if _travels_transposed(n, weights[n]):
                out_grad[n] = out_grad[n].T
            out_delta[n], out_m[n], out_v[n] = adamw(weights[n], out_grad[n], moments_m[n], moments_v[n])
        return rest

    others = [n for n in BIG_NAMES if n not in last]
    finish(others, [hooks.contributions[n] for n in others], token)
    arrived = wait_sends_to_chips("scatter_wait_l0", sums, landings, sems, True, out_v[others[-1]])
    small_grad_pack = jnp.concatenate(finish(last + ["small", "small tail"], arrived), axis=0)
    loss = small_grad_pack[-8, 0]
    out_grad.update(_unpack_small(small_grad_pack, {n: weights[n] for n in SMALL_NAMES}))
    cubes = [n for n in SMALL_NAMES if weights[n].ndim == 3]
    for names, slabs in ((cubes, 8), ([n for n in SMALL_NAMES if n not in cubes], None)):
        deltas, new_ms, new_vs = adamw_small(*[[group[n] for n in names] for group in (weights, out_grad, moments_m, moments_v)], slabs=slabs)
        out_delta.update(zip(names, deltas))
        out_m.update(zip(names, new_ms))
        out_v.update(zip(names, new_vs))
    outs = [loss, dx[None]]
    for group in (out_grad, out_delta, out_m, out_v):
        outs.extend(group[n] for n in WEIGHT_NAMES)
    return tuple(outs)


def kernel(x, l0_norm, l0_w_in, l0_a_re, l0_a_im, l0_log_step, l0_b_re, l0_b_im, l0_c_re, l0_c_im, l0_d, l0_w_glu, l0_b_glu, l0_w_out, l1_norm, l1_w_in, l1_sinks, l1_w_out, l2_norm, l2_w_in, l2_a_re, l2_a_im, l2_log_step, l2_b_re, l2_b_im, l2_c_re, l2_c_im, l2_d, l2_w_glu, l2_b_glu, l2_w_out, l3_norm, l3_w_in, l3_sinks, l3_w_out, final_norm, loss_target, m_l0_norm, m_l0_w_in, m_l0_a_re, m_l0_a_im, m_l0_log_step, m_l0_b_re, m_l0_b_im, m_l0_c_re, m_l0_c_im, m_l0_d, m_l0_w_glu, m_l0_b_glu, m_l0_w_out, m_l1_norm, m_l1_w_in, m_l1_sinks, m_l1_w_out, m_l2_norm, m_l2_w_in, m_l2_a_re, m_l2_a_im, m_l2_log_step, m_l2_b_re, m_l2_b_im, m_l2_c_re, m_l2_c_im, m_l2_d, m_l2_w_glu, m_l2_b_glu, m_l2_w_out, m_l3_norm, m_l3_w_in, m_l3_sinks, m_l3_w_out, m_final_norm, v_l0_norm, v_l0_w_in, v_l0_a_re, v_l0_a_im, v_l0_log_step, v_l0_b_re, v_l0_b_im, v_l0_c_re, v_l0_c_im, v_l0_d, v_l0_w_glu, v_l0_b_glu, v_l0_w_out, v_l1_norm, v_l1_w_in, v_l1_sinks, v_l1_w_out, v_l2_norm, v_l2_w_in, v_l2_a_re, v_l2_a_im, v_l2_log_step, v_l2_b_re, v_l2_b_im, v_l2_c_re, v_l2_c_im, v_l2_d, v_l2_w_glu, v_l2_b_glu, v_l2_w_out, v_l3_norm, v_l3_w_in, v_l3_sinks, v_l3_w_out, v_final_norm):
    args = locals()
    weights = {n: args[n] for n in WEIGHT_NAMES}
    moments_m = {n: args["m_" + n] for n in WEIGHT_NAMES}
    moments_v = {n: args["v_" + n] for n in WEIGHT_NAMES}
    return _train_step(x, loss_target, weights, moments_m, moments_v)
```

```python
import functools
import math

import jax
import jax.numpy as jnp
from jax import lax
from jax.experimental import pallas as pl
from jax.experimental.pallas import tpu as pltpu

F32 = jnp.float32
MXU_DTYPE = jnp.bfloat16
WIRE_DTYPE = jnp.bfloat16
MESH = pl.DeviceIdType.MESH

D_MODEL = 1024
BRANCH = 1024
NORM_EPS = 1e-5
SSM_GROUPS = 64
SSM_GROUP = 16
SSM_STATE = 64
S5_CHUNK = 16
LANES = 128
S5_OCT = LANES // SSM_GROUP
S5_OCTETS = SSM_GROUPS // S5_OCT
S5_OCT_IN = S5_CHUNK * LANES
S5_OCT_STATE = S5_OCT * SSM_STATE
S5_STATES = SSM_GROUPS * SSM_STATE
HEAD_DIM = 64
N_Q_HEADS = 16
N_KV_HEADS = 2
GQA_GROUP = N_Q_HEADS // N_KV_HEADS
ATTN_BLOCK = 128
Q_DIM = N_Q_HEADS * HEAD_DIM
KV_DIM = N_KV_HEADS * HEAD_DIM
ROPE_THETA = 10000.0
NEG_INF = -1e30
ADAM_LR = 0.001
ADAM_B1 = 0.9
ADAM_B2 = 0.999
ADAM_EPS = 1e-08
ADAM_WD = 0.01
ADAM_STEP = 10

VMEM_LIMIT_V7X = 56 * 1024 * 1024
ROW_TILE_FWD = 512
ROW_TILE_BWD = 512

SSM_NAMES = ("norm", "w_in", "a_re", "a_im", "log_step", "b_re", "b_im", "c_re", "c_im", "d", "w_glu", "b_glu", "w_out")
ATTN_NAMES = ("norm", "w_in", "sinks", "w_out")


def _weight_names():
    names = []
    for i in range(4):
        for n in (SSM_NAMES if i % 2 == 0 else ATTN_NAMES):
            names.append("l%d_%s" % (i, n))
    names.append("final_norm")
    return names


WEIGHT_NAMES = _weight_names()
BIG_NAMES = [n for n in WEIGHT_NAMES if n.endswith(("w_in", "w_glu", "w_out"))]
SMALL_NAMES = [n for n in WEIGHT_NAMES if n not in BIG_NAMES]
EXACT_NAMES = [n for n in SMALL_NAMES if n.endswith(("log_step", "sinks")) or n == "final_norm"]
PACK_NAMES = [n for n in SMALL_NAMES if n not in EXACT_NAMES] + EXACT_NAMES


def _params(semantics=None):
    return pltpu.CompilerParams(dimension_semantics=semantics, vmem_limit_bytes=VMEM_LIMIT_V7X)


def _rows(tm, n):
    return pl.BlockSpec((tm, n), lambda i: (i, 0))


def _whole(shape):
    return pl.BlockSpec(shape, lambda i: (0,) * len(shape), pipeline_mode=pl.Buffered(1))


def _sds(shape, dtype=F32):
    return jax.ShapeDtypeStruct(shape, dtype)


def _mm(a, b):
    return jnp.dot(a.astype(MXU_DTYPE), b.astype(MXU_DTYPE), preferred_element_type=F32)


def _mm_tn(a, b):
    return lax.dot_general(a.astype(MXU_DTYPE), b.astype(MXU_DTYPE), (((0,), (0,)), ((), ())), preferred_element_type=F32)


def _mm_nt(a, b):
    return lax.dot_general(a.astype(MXU_DTYPE), b.astype(MXU_DTYPE), (((1,), (1,)), ((), ())), preferred_element_type=F32)


def _sigmoid(x):
    return 0.5 + 0.5 * jnp.tanh(0.5 * x)


def _silu(x):
    return x * _sigmoid(x)


def _silu_and_grad(x):
    s = _sigmoid(x)
    return x * s, s * (1.0 + x * (1.0 - s))


GELU_C0 = math.sqrt(2.0 / math.pi)
GELU_C1 = 0.044715


def _gelu(x):
    return 0.5 * x * (1.0 + jnp.tanh(GELU_C0 * (x + GELU_C1 * x * x * x)))


def _gelu_and_grad(x):
    x2 = x * x
    th = jnp.tanh(GELU_C0 * x * (1.0 + GELU_C1 * x2))
    half = 0.5 + 0.5 * th
    return x * half, half + 0.5 * x * (1.0 - th * th) * (GELU_C0 + 3.0 * GELU_C0 * GELU_C1 * x2)


def _rms(x, g):
    r = lax.rsqrt(jnp.mean(x * x, axis=-1, keepdims=True) + NORM_EPS)
    xhat = x * r
    return r, xhat, xhat * g


def _rms_bwd(dh, g, r, xhat):
    dxhat = dh * g
    dx = r * (dxhat - xhat * jnp.mean(dxhat * xhat, axis=-1, keepdims=True))
    return dx, jnp.sum(dh * xhat, axis=0, keepdims=True)


def _swap_half_heads(x):
    n = x.shape[-1]
    lane = lax.broadcasted_iota(jnp.int32, x.shape, x.ndim - 1)
    first = (lane % HEAD_DIM) < (HEAD_DIM // 2)
    return jnp.where(first, pltpu.roll(x, n - HEAD_DIM // 2, x.ndim - 1), pltpu.roll(x, HEAD_DIM // 2, x.ndim - 1))


def _tile_lanes(t, reps):
    return jnp.concatenate([t] * reps, axis=1)


TOKEN_SHAPE = (8, LANES)


def _after(token):
    return ([], []) if token is None else ([token], [_whole(TOKEN_SHAPE)])


def ssm_proj_fwd(x, norm, w_in):
    t = x.shape[0]
    tm = min(ROW_TILE_FWD, t)

    def body(x_ref, g_ref, w_ref, u_ref, gate_ref):
        _, _, h = _rms(x_ref[...], g_ref[...])
        h = h.astype(MXU_DTYPE)
        half = BRANCH // 2
        for j in range(2):
            u_ref[:, j * half:(j + 1) * half] = _mm(h, w_ref[j])
            gate_ref[:, j * half:(j + 1) * half] = _mm(h, w_ref[2 + j])

    return pl.pallas_call(
        body, name="ssm_proj_fwd", grid=(t // tm,),
        in_specs=[_rows(tm, D_MODEL), _whole((1, D_MODEL)), _whole((4, D_MODEL, BRANCH // 2))],
        out_specs=[_rows(tm, BRANCH), _rows(tm, BRANCH)],
        out_shape=[_sds((t, BRANCH)), _sds((t, BRANCH))],
        compiler_params=_params(("parallel",)),
    )(x, norm, w_in)


def _chunk_rows(ref, nk, dtype=None):
    rows = jnp.concatenate([ref[pl.ds(s, nk, stride=S5_CHUNK), :] for s in range(S5_CHUNK)], axis=1)
    return rows.astype(MXU_DTYPE if dtype is None else dtype)


def _store_chunk_rows(ref, val, nk):
    for s in range(S5_CHUNK):
        ref[pl.ds(s, nk, stride=S5_CHUNK), :] = val[:, s * LANES:(s + 1) * LANES]


def _own_group_mask():
    row = lax.broadcasted_iota(jnp.int32, (S5_OCT_IN, S5_OCT_STATE), 0)
    col = lax.broadcasted_iota(jnp.int32, (S5_OCT_IN, S5_OCT_STATE), 1)
    return ((row % LANES) // SSM_GROUP) == (col // SSM_STATE)


def _spread_groups(w):
    return jnp.where(_own_group_mask(), jnp.concatenate([w] * (S5_OCT_STATE // LANES), axis=1), 0.0).astype(MXU_DTYPE)


def _fold_groups(p):
    p = jnp.where(_own_group_mask(), p, 0.0)
    return sum(p[:, q * LANES:(q + 1) * LANES] for q in range(S5_OCT_STATE // LANES))


def _fill_toeplitz(win_ref, kd_ref):
    win_ref[...] = jnp.zeros_like(win_ref)
    for s in range(S5_CHUNK):
        for t in range(s, S5_CHUNK):
            win_ref[s * LANES:(s + 1) * LANES, t * LANES:(t + 1) * LANES] = kd_ref[t - s].astype(MXU_DTYPE)


TOEPLITZ_BLOCK = 512
_TOEPLITZ_BLOCKS = [(lo, lo + TOEPLITZ_BLOCK) for lo in range(0, S5_OCT_IN, TOEPLITZ_BLOCK)]


def _strip(t):
    return pl.BlockSpec((t, LANES), lambda b: (0, b))


def _oct_states(nk):
    return pl.BlockSpec((nk, S5_OCT_STATE), lambda b: (0, b))


OCT_W = pl.BlockSpec((None, S5_OCT_IN, LANES), lambda b: (b, 0, 0))
OCT_KD = pl.BlockSpec((None, S5_CHUNK, LANES, LANES), lambda b: (b, 0, 0, 0))


def s5_chunk_states(u, ws_re, ws_im):
    t = u.shape[0]
    nk = t // S5_CHUNK

    def body(u_ref, wr_ref, wi_ref, re_ref, im_ref):
        uc = _chunk_rows(u_ref, nk)
        re_ref[...] = _mm(uc, _spread_groups(wr_ref[...]))
        im_ref[...] = _mm(uc, _spread_groups(wi_ref[...]))

    return pl.pallas_call(
        body, name="s5_chunk_states", grid=(S5_OCTETS,),
        in_specs=[_strip(t), OCT_W, OCT_W], out_specs=[_oct_states(nk), _oct_states(nk)],
        out_shape=[_sds((nk, S5_STATES)), _sds((nk, S5_STATES))],
        compiler_params=_params(("parallel",)),
    )(u, ws_re, ws_im)


def s5_scan_fwd(s_re, s_im, a_re, a_im):
    nk = s_re.shape[0]

    def body(sre_ref, sim_ref, ar_ref, ai_ref, hre_ref, him_ref):
        ar = ar_ref[...]
        ai = ai_ref[...]

        def step(k, carry):
            hr, hi = carry
            hre_ref[pl.ds(k, 1), :] = hr
            him_ref[pl.ds(k, 1), :] = hi
            sr = sre_ref[pl.ds(k, 1), :]
            si = sim_ref[pl.ds(k, 1), :]
            return ar * hr - ai * hi + sr, ai * hr + ar * hi + si

        zero = jnp.zeros((1, S5_STATES), F32)
        lax.fori_loop(0, nk, step, (zero, zero))

    vm = pl.BlockSpec(memory_space=pltpu.VMEM)
    return pl.pallas_call(
        body, name="s5_scan_fwd", in_specs=[vm, vm, vm, vm], out_specs=[vm, vm],
        out_shape=[_sds((nk, S5_STATES)), _sds((nk, S5_STATES))],
        compiler_params=_params(),
    )(s_re, s_im, a_re, a_im)


def s5_outputs(u, h_re, h_im, kd, wo_re, wo_im):
    t = u.shape[0]
    nk = t // S5_CHUNK

    def body(u_ref, hre_ref, him_ref, kd_ref, wor_ref, woi_ref, y_ref, win_ref):
        _fill_toeplitz(win_ref, kd_ref)
        uc = _chunk_rows(u_ref, nk)
        y = jnp.concatenate([_mm(uc[:, :hi], win_ref[:hi, lo:hi]) for lo, hi in _TOEPLITZ_BLOCKS], axis=1)
        y = y + _mm_nt(hre_ref[...], _spread_groups(wor_ref[...])) + _mm_nt(him_ref[...], _spread_groups(woi_ref[...]))
        _store_chunk_rows(y_ref, y, nk)

    return pl.pallas_call(
        body, name="s5_outputs", grid=(S5_OCTETS,),
        in_specs=[_strip(t), _oct_states(nk), _oct_states(nk), OCT_KD, OCT_W, OCT_W],
        out_specs=_strip(t), out_shape=_sds((t, BRANCH)),
        scratch_shapes=[pltpu.VMEM((S5_OCT_IN, S5_OCT_IN), MXU_DTYPE)],
        compiler_params=_params(("parallel",)),
    )(u, h_re, h_im, kd, wo_re, wo_im)


def s5_state_grads(dy, wo_re, wo_im, token=None):
    t = dy.shape[0]
    nk = t // S5_CHUNK
    extra, extra_specs = _after(token)

    def body(dy_ref, wor_ref, woi_ref, *rest):
        re_ref, im_ref = rest[-2:]
        dyc = _chunk_rows(dy_ref, nk)
        re_ref[...] = _mm(dyc, _spread_groups(wor_ref[...]))
        im_ref[...] = _mm(dyc, _spread_groups(woi_ref[...]))

    return pl.pallas_call(
        body, name="s5_state_grads", grid=(S5_OCTETS,),
        in_specs=[_strip(t), OCT_W, OCT_W] + extra_specs, out_specs=[_oct_states(nk), _oct_states(nk)],
        out_shape=[_sds((nk, S5_STATES)), _sds((nk, S5_STATES))],
        compiler_params=_params(("parallel",)),
    )(dy, wo_re, wo_im, *extra)


def s5_scan_bwd(dh_re, dh_im, h_re, h_im, a_re, a_im):
    nk = dh_re.shape[0]

    def body(dhr_ref, dhi_ref, hr_ref, hi_ref, ar_ref, ai_ref, dsr_ref, dsi_ref, dar_ref, dai_ref):
        ar = ar_ref[...]
        ai = ai_ref[...]

        def step(i, carry):
            gr, gi = carry
            k = nk - 1 - i
            dhr = dhr_ref[pl.ds(k, 1), :]
            dhi = dhi_ref[pl.ds(k, 1), :]
            dsr_ref[pl.ds(k, 1), :] = gr
            dsi_ref[pl.ds(k, 1), :] = gi
            return dhr + ar * gr + ai * gi, dhi - ai * gr + ar * gi

        zero = jnp.zeros((1, S5_STATES), F32)
        lax.fori_loop(0, nk, step, (zero, zero))
        dsr, dsi, hr, hi = dsr_ref[...], dsi_ref[...], hr_ref[...], hi_ref[...]
        dar_ref[...] = jnp.sum(dsr * hr + dsi * hi, axis=0, keepdims=True)
        dai_ref[...] = jnp.sum(dsi * hr - dsr * hi, axis=0, keepdims=True)

    vm = pl.BlockSpec(memory_space=pltpu.VMEM)
    return pl.pallas_call(
        body, name="s5_scan_bwd", in_specs=[vm] * 6, out_specs=[vm] * 4,
        out_shape=[_sds((nk, S5_STATES)), _sds((nk, S5_STATES)), _sds((1, S5_STATES)), _sds((1, S5_STATES))],
        input_output_aliases={0: 0, 1: 1}, compiler_params=_params(),
    )(dh_re, dh_im, h_re, h_im, a_re, a_im)


def s5_input_grads(dy, ds_re, ds_im, kd, ws_re, ws_im):
    t = dy.shape[0]
    nk = t // S5_CHUNK

    def body(dy_ref, dsr_ref, dsi_ref, kd_ref, wsr_ref, wsi_ref, du_ref, win_ref):
        _fill_toeplitz(win_ref, kd_ref)
        dyc = _chunk_rows(dy_ref, nk)
        du = jnp.concatenate([_mm_nt(dyc[:, lo:], win_ref[lo:hi, lo:]) for lo, hi in _TOEPLITZ_BLOCKS], axis=1)
        du = du + _mm_nt(dsr_ref[...], _spread_groups(wsr_ref[...])) + _mm_nt(dsi_ref[...], _spread_groups(wsi_ref[...]))
        _store_chunk_rows(du_ref, du, nk)

    return pl.pallas_call(
        body, name="s5_input_grads", grid=(S5_OCTETS,),
        in_specs=[_strip(t), _oct_states(nk), _oct_states(nk), OCT_KD, OCT_W, OCT_W],
        out_specs=_strip(t), out_shape=_sds((t, BRANCH)),
        scratch_shapes=[pltpu.VMEM((S5_OCT_IN, S5_OCT_IN), MXU_DTYPE)],
        compiler_params=_params(("parallel",)),
    )(dy, ds_re, ds_im, kd, ws_re, ws_im)


def s5_weight_grads(u, dy, h_re, h_im, ds_re, ds_im):
    t = u.shape[0]
    nk = t // S5_CHUNK

    def body(u_ref, dy_ref, hre_ref, him_ref, dsr_ref, dsi_ref, dkd_ref, dwsr_ref, dwsi_ref, dwor_ref, dwoi_ref):
        dyc = _chunk_rows(dy_ref, nk, F32)
        uct = _chunk_rows(u_ref, nk, F32).T.astype(MXU_DTYPE)
        dyct = dyc.T.astype(MXU_DTYPE)
        dyc = dyc.astype(MXU_DTYPE)
        dwsr_ref[...] = _fold_groups(_mm(uct, dsr_ref[...]))
        dwsi_ref[...] = _fold_groups(_mm(uct, dsi_ref[...]))
        dwor_ref[...] = _fold_groups(_mm(dyct, hre_ref[...]))
        dwoi_ref[...] = _fold_groups(_mm(dyct, him_ref[...]))
        dkd_ref[...] = jnp.zeros_like(dkd_ref)
        for tt in range(0, S5_CHUNK, 2):
            p = _mm(uct[:(tt + 2) * LANES], dyc[:, tt * LANES:(tt + 2) * LANES])
            for s in range(tt + 2):
                rows = p[s * LANES:(s + 1) * LANES]
                if s <= tt:
                    dkd_ref[tt - s] += rows[:, :LANES]
                dkd_ref[tt + 1 - s] += rows[:, LANES:]

    return pl.pallas_call(
        body, name="s5_weight_grads", grid=(S5_OCTETS,),
        in_specs=[_strip(t), _strip(t)] + [_oct_states(nk)] * 4,
        out_specs=[OCT_KD, OCT_W, OCT_W, OCT_W, OCT_W],
        out_shape=[_sds((S5_OCTETS, S5_CHUNK, LANES, LANES))] + [_sds((S5_OCTETS, S5_OCT_IN, LANES))] * 4,
        compiler_params=_params(("parallel",)),
    )(u, dy, h_re, h_im, ds_re, ds_im)


def ssm_mix_fwd(x, u, gate, y_scan, d, w_glu, b_glu, w_out):
    t = x.shape[0]
    tm = min(ROW_TILE_FWD, t)

    def body(x_ref, u_ref, gate_ref, ys_ref, d_ref, wg_ref, bg_ref, wo_ref, y_ref, g2_ref, xo_ref):
        y = ys_ref[...] + d_ref[...] * u_ref[...]
        z0 = _gelu(y)
        g2 = _mm(z0, wg_ref[...]) + bg_ref[...]
        a = z0 * _sigmoid(g2) * _silu(gate_ref[...])
        y_ref[...] = y
        g2_ref[...] = g2
        xo_ref[...] = x_ref[...] + _mm(a, wo_ref[...])

    row = _rows(tm, BRANCH)
    vec = _whole((1, BRANCH))
    mat = _whole((BRANCH, BRANCH))
    return pl.pallas_call(
        body, name="ssm_mix_fwd", grid=(t // tm,),
        in_specs=[row, row, row, row, vec, mat, vec, mat],
        out_specs=[row, row, row],
        out_shape=[_sds((t, BRANCH))] * 3,
        compiler_params=_params(("parallel",)),
    )(x, u, gate, y_scan, d, w_glu, b_glu, w_out)


def ssm_mix_bwd(dxo, u, gate, y, g2, w_glu, w_out, token=None):
    t = dxo.shape[0]
    tm = min(ROW_TILE_BWD, t)
    extra, extra_specs = _after(token)

    def body(dxo_ref, u_ref, gate_ref, y_ref, g2_ref, wgt_ref, wot_ref, *rest):
        dy_ref, dgate_ref, dwo_ref, dwg_ref, dbg_ref, dd_ref = rest[-6:]

        @pl.when(pl.program_id(0) == 0)
        def _():
            dwo_ref[...] = jnp.zeros_like(dwo_ref)
            dwg_ref[...] = jnp.zeros_like(dwg_ref)
            dbg_ref[...] = jnp.zeros_like(dbg_ref)
            dd_ref[...] = jnp.zeros_like(dd_ref)

        dxo = dxo_ref[...]
        gate = gate_ref[...]
        y = y_ref[...]
        z0, z0_grad = _gelu_and_grad(y)
        sg = _sigmoid(g2_ref[...])
        z = z0 * sg
        sgate, sgate_grad = _silu_and_grad(gate)
        da = _mm_nt(dxo, wot_ref[...])
        dwo_ref[...] += _mm_tn(z * sgate, dxo)
        dz = da * sgate
        dgate_ref[...] = da * z * sgate_grad
        dg2 = dz * z0 * sg * (1.0 - sg)
        dbg_ref[...] += jnp.sum(dg2, axis=0, keepdims=True)
        dwg_ref[...] += _mm_tn(z0, dg2)
        dz0 = dz * sg + _mm_nt(dg2, wgt_ref[...])
        dy = dz0 * z0_grad
        dd_ref[...] += jnp.sum(dy * u_ref[...], axis=0, keepdims=True)
        dy_ref[...] = dy

    row = _rows(tm, BRANCH)
    vec = _whole((1, BRANCH))
    mat = _whole((BRANCH, BRANCH))
    return pl.pallas_call(
        body, name="ssm_mix_bwd", grid=(t // tm,),
        in_specs=[row, row, row, row, row, mat, mat] + extra_specs,
        out_specs=[row, row, mat, mat, vec, vec],
        out_shape=[_sds((t, BRANCH)), _sds((t, BRANCH)), _sds((BRANCH, D_MODEL)), _sds((BRANCH, BRANCH)),
                   _sds((1, BRANCH)), _sds((1, BRANCH))],
        compiler_params=_params(("arbitrary",)),
    )(dxo, u, gate, y, g2, w_glu, w_out, *extra)


def ssm_proj_bwd(x, norm, dxo, dy, du_scan, dgate, d, w_in):
    t = x.shape[0]
    tm = min(ROW_TILE_BWD, t)
    n = 2 * BRANCH

    def body(x_ref, g_ref, dxo_ref, dy_ref, dus_ref, dgate_ref, d_ref, wt_ref, dx_ref, dw_ref, dg_ref):
        @pl.when(pl.program_id(0) == 0)
        def _():
            dw_ref[...] = jnp.zeros_like(dw_ref)
            dg_ref[...] = jnp.zeros_like(dg_ref)

        g = g_ref[...]
        r, xhat, h = _rms(x_ref[...], g)
        h = h.astype(MXU_DTYPE)
        du = dus_ref[...] + d_ref[...] * dy_ref[...]
        dproj = jnp.concatenate([du, dgate_ref[...]], axis=1).astype(MXU_DTYPE)
        dh = jnp.zeros((tm, D_MODEL), F32)
        for j in range(4):
            cols = dproj[:, j * (n // 4):(j + 1) * (n // 4)]
            dh = dh + _mm_nt(cols, wt_ref[j])
            dw_ref[j] += _mm_tn(h, cols)
        dx, dg = _rms_bwd(dh, g, r, xhat)
        dg_ref[...] += dg
        dx_ref[...] = dxo_ref[...] + dx

    row = _rows(tm, D_MODEL)
    vec = _whole((1, D_MODEL))
    blocks = _whole((4, D_MODEL, n // 4))
    return pl.pallas_call(
        body, name="ssm_proj_bwd", grid=(t // tm,),
        in_specs=[row, vec, row, row, row, row, vec, blocks],
        out_specs=[row, blocks, vec],
        out_shape=[_sds((t, D_MODEL)), _sds((4, D_MODEL, n // 4)), _sds((1, D_MODEL))],
        compiler_params=_params(("arbitrary",)),
    )(x, norm, dxo, dy, du_scan, dgate, d, w_in)


ATTN_N = Q_DIM + 2 * KV_DIM + BRANCH


def attn_proj_fwd(x, norm, w_in_t, cos2, sin2):
    t = x.shape[0]
    tm = min(ROW_TILE_FWD, t)

    def body(x_ref, g_ref, w_ref, cos_ref, sin_ref, q_ref, k_ref, v_ref, gate_ref):
        _, _, h = _rms(x_ref[...], g_ref[...])
        p = _mm_nt(h, w_ref[...])
        cs = cos_ref[...]
        sn = sin_ref[...]
        q = p[:, :Q_DIM]
        k = p[:, Q_DIM:Q_DIM + KV_DIM]
        q_ref[...] = (q * _tile_lanes(cs, Q_DIM // LANES) + _swap_half_heads(q) * _tile_lanes(sn, Q_DIM // LANES)).astype(MXU_DTYPE)
        k_ref[...] = (k * cs + _swap_half_heads(k) * sn).astype(MXU_DTYPE)
        v_ref[...] = p[:, Q_DIM + KV_DIM:Q_DIM + 2 * KV_DIM].astype(MXU_DTYPE)
        gate_ref[...] = p[:, Q_DIM + 2 * KV_DIM:]

    return pl.pallas_call(
        body, name="attn_proj_fwd", grid=(t // tm,),
        in_specs=[_rows(tm, D_MODEL), _whole((1, D_MODEL)), _whole((ATTN_N, D_MODEL)), _rows(tm, LANES), _rows(tm, LANES)],
        out_specs=[_rows(tm, Q_DIM), _rows(tm, KV_DIM), _rows(tm, KV_DIM), _rows(tm, BRANCH)],
        out_shape=[_sds((t, Q_DIM), MXU_DTYPE), _sds((t, KV_DIM), MXU_DTYPE), _sds((t, KV_DIM), MXU_DTYPE), _sds((t, BRANCH))],
        compiler_params=_params(("parallel",)),
    )(x, norm, w_in_t, cos2, sin2)


GQA_LANES = GQA_GROUP * ATTN_BLOCK


def _window_masks(first_block):
    kj = lax.broadcasted_iota(jnp.int32, (ATTN_BLOCK, GQA_LANES), 0)
    qi = lax.broadcasted_iota(jnp.int32, (ATTN_BLOCK, GQA_LANES), 1) % ATTN_BLOCK
    return kj > qi, kj > jnp.where(first_block, qi, ATTN_BLOCK)


def _fold(upper, both):
    return jnp.where(upper, both[:ATTN_BLOCK], both[ATTN_BLOCK:])


def _unfold(upper, tile):
    return jnp.concatenate([jnp.where(upper, tile, 0.0), jnp.where(upper, 0.0, tile)], axis=0).astype(MXU_DTYPE)


def _stack_heads(ref, group):
    return jnp.concatenate([ref[:, h * HEAD_DIM:(h + 1) * HEAD_DIM] for h in range(group * GQA_GROUP, (group + 1) * GQA_GROUP)], axis=0)


def _unstack_heads(ref, group, stacked):
    for n in range(GQA_GROUP):
        h = group * GQA_GROUP + n
        ref[:, h * HEAD_DIM:(h + 1) * HEAD_DIM] = stacked[n * ATTN_BLOCK:(n + 1) * ATTN_BLOCK]


def _sink_row(sink_ref, group):
    return jnp.concatenate([jnp.full((1, ATTN_BLOCK), sink_ref[group * GQA_GROUP + n], F32) for n in range(GQA_GROUP)], axis=1)


def _lane_is(h):
    return lax.broadcasted_iota(jnp.int32, (1, LANES), 1) == h


def attn_fwd(q, k, v, sinks):
    t = q.shape[0]
    nb = t // ATTN_BLOCK
    scale = HEAD_DIM ** -0.5

    def body(sink_ref, q_ref, kc_ref, kp_ref, vc_ref, vp_ref, o_ref, lse_ref):
        keys = jnp.concatenate([kp_ref[...], kc_ref[...]], axis=0).astype(MXU_DTYPE)
        vals = jnp.concatenate([vp_ref[...], vc_ref[...]], axis=0).astype(MXU_DTYPE)
        upper, dead = _window_masks(pl.program_id(0) == 0)
        for g in range(N_KV_HEADS):
            kv = slice(g * HEAD_DIM, (g + 1) * HEAD_DIM)
            qs = _stack_heads(q_ref, g) * scale
            s = jnp.where(dead, NEG_INF, _fold(upper, _mm_nt(keys[:, kv], qs)))
            sink = _sink_row(sink_ref, g)
            m = jnp.maximum(jnp.max(s, axis=0, keepdims=True), sink)
            p = jnp.exp(s - m)
            den = jnp.sum(p, axis=0, keepdims=True) + jnp.exp(sink - m)
            _unstack_heads(o_ref, g, _mm_tn(_unfold(upper, p * (1.0 / den)), vals[:, kv]))
            lse = m + jnp.log(den)
            for n in range(GQA_GROUP):
                lse_ref[pl.ds(g * GQA_GROUP + n, 1), :] = lse[:, n * ATTN_BLOCK:(n + 1) * ATTN_BLOCK]

    cur = lambda n: pl.BlockSpec((ATTN_BLOCK, n), lambda i: (i, 0))
    prev = lambda n: pl.BlockSpec((ATTN_BLOCK, n), lambda i: (jnp.maximum(i - 1, 0), 0))
    return pl.pallas_call(
        body, name="attn_fwd", grid=(nb,),
        in_specs=[pl.BlockSpec(memory_space=pltpu.SMEM), cur(Q_DIM), cur(KV_DIM), prev(KV_DIM), cur(KV_DIM), prev(KV_DIM)],
        out_specs=[cur(Q_DIM), pl.BlockSpec((N_Q_HEADS, ATTN_BLOCK), lambda i: (0, i))],
        out_shape=[_sds((t, Q_DIM)), _sds((N_Q_HEADS, t))],
        compiler_params=_params(("parallel",)),
    )(sinks, q, k, k, v, v)


def attn_bwd(q, k, v, sinks, o, lse, do):
    t = q.shape[0]
    nb = t // ATTN_BLOCK
    scale = HEAD_DIM ** -0.5

    def body(sink_ref, q_ref, kc_ref, kp_ref, vc_ref, vp_ref, o_ref, lse_ref, do_ref,
             dq_ref, dk_ref, dv_ref, dsink_ref, dk_carry, dv_carry):
        i = pl.program_id(0)

        @pl.when(i == 0)
        def _():
            dsink_ref[...] = jnp.zeros_like(dsink_ref)
            dk_carry[...] = jnp.zeros_like(dk_carry)
            dv_carry[...] = jnp.zeros_like(dv_carry)

        @pl.when(i < nb)
        def _():
            keys = jnp.concatenate([kp_ref[...], kc_ref[...]], axis=0).astype(MXU_DTYPE)
            vals = jnp.concatenate([vp_ref[...], vc_ref[...]], axis=0).astype(MXU_DTYPE)
            upper, dead = _window_masks(i == 0)
            dsink = jnp.zeros((1, LANES), F32)
            dk_heads = []
            dv_heads = []
            for g in range(N_KV_HEADS):
                kv = slice(g * HEAD_DIM, (g + 1) * HEAD_DIM)
                qs = (_stack_heads(q_ref, g) * scale).astype(MXU_DTYPE)
                dos = _stack_heads(do_ref, g)
                lse = jnp.concatenate([lse_ref[pl.ds(g * GQA_GROUP + n, 1), :] for n in range(GQA_GROUP)], axis=1)
                s = jnp.where(dead, NEG_INF, _fold(upper, _mm_nt(keys[:, kv], qs)))
                p = jnp.exp(s - lse)
                prod = dos * _stack_heads(o_ref, g)
                head = prod.astype(MXU_DTYPE)
                ones = jnp.ones((8, HEAD_DIM), MXU_DTYPE)
                delta = (_mm_nt(ones, head) + _mm_nt(ones, prod - head.astype(F32)))[:1]
                dos = dos.astype(MXU_DTYPE)
                ds = _unfold(upper, p * (_fold(upper, _mm_nt(vals[:, kv], dos)) - delta))
                _unstack_heads(dq_ref, g, _mm_tn(ds, keys[:, kv]) * scale)
                dk_heads.append(_mm(ds, qs))
                dv_heads.append(_mm(_unfold(upper, p), dos))
                at_sink = jnp.exp(_sink_row(sink_ref, g) - lse) * delta
                for n in range(GQA_GROUP):
                    dsink = dsink + jnp.where(_lane_is(g * GQA_GROUP + n), -jnp.sum(at_sink[:, n * ATTN_BLOCK:(n + 1) * ATTN_BLOCK]), 0.0)
            dkk = jnp.concatenate(dk_heads, axis=1)
            dvv = jnp.concatenate(dv_heads, axis=1)
            dsink_ref[...] += dsink
            dk_ref[...] = dk_carry[...] + dkk[:ATTN_BLOCK]
            dv_ref[...] = dv_carry[...] + dvv[:ATTN_BLOCK]
            dk_carry[...] = dkk[ATTN_BLOCK:]
            dv_carry[...] = dvv[ATTN_BLOCK:]

        @pl.when(i == nb)
        def _():
            dk_ref[...] = dk_carry[...]
            dv_ref[...] = dv_carry[...]

    last = nb - 1
    cur = lambda n: pl.BlockSpec((ATTN_BLOCK, n), lambda i: (jnp.minimum(i, last), 0))
    prev = lambda n: pl.BlockSpec((ATTN_BLOCK, n), lambda i: (jnp.clip(i - 1, 0, last), 0))
    late = lambda n: pl.BlockSpec((ATTN_BLOCK, n), lambda i: (i, 0))
    dq, dk_late, dv_late, dsinks = pl.pallas_call(
        body, name="attn_bwd", grid=(nb + 1,),
        in_specs=[pl.BlockSpec(memory_space=pltpu.SMEM), cur(Q_DIM), cur(KV_DIM), prev(KV_DIM), cur(KV_DIM), prev(KV_DIM),
                  cur(Q_DIM), pl.BlockSpec((N_Q_HEADS, ATTN_BLOCK), lambda i: (0, jnp.minimum(i, last))), cur(Q_DIM)],
        out_specs=[cur(Q_DIM), late(KV_DIM), late(KV_DIM), _whole((1, LANES))],
        out_shape=[_sds((t, Q_DIM)), _sds((t + ATTN_BLOCK, KV_DIM)), _sds((t + ATTN_BLOCK, KV_DIM)), _sds((1, LANES))],
        scratch_shapes=[pltpu.VMEM((ATTN_BLOCK, KV_DIM), F32), pltpu.VMEM((ATTN_BLOCK, KV_DIM), F32)],
        compiler_params=_params(("arbitrary",)),
    )(sinks, q, k, k, v, v, o, lse, do)
    return dq, dk_late[ATTN_BLOCK:], dv_late[ATTN_BLOCK:], dsinks


def attn_out_fwd(x, o, gate, w_out):
    t = x.shape[0]
    tm = min(ROW_TILE_FWD, t)

    def body(x_ref, o_ref, gate_ref, w_ref, xo_ref):
        xo_ref[...] = x_ref[...] + _mm(o_ref[...] * _silu(gate_ref[...]), w_ref[...])

    row = _rows(tm, D_MODEL)
    return pl.pallas_call(
        body, name="attn_out_fwd", grid=(t // tm,),
        in_specs=[row, row, row, _whole((Q_DIM, D_MODEL))], out_specs=row, out_shape=_sds((t, D_MODEL)),
        compiler_params=_params(("parallel",)),
    )(x, o, gate, w_out)


def attn_out_bwd(dxo, o, gate, w_out, token=None):
    t = dxo.shape[0]
    tm = min(ROW_TILE_BWD, t)
    extra, extra_specs = _after(token)

    def body(dxo_ref, o_ref, gate_ref, wt_ref, *rest):
        do_ref, dgate_ref, dw_ref = rest[-3:]

        @pl.when(pl.program_id(0) == 0)
        def _():
            dw_ref[...] = jnp.zeros_like(dw_ref)

        dxo = dxo_ref[...]
        o = o_ref[...]
        gate = gate_ref[...]
        sgate, sgate_grad = _silu_and_grad(gate)
        da = _mm_nt(dxo, wt_ref[...])
        dw_ref[...] += _mm_tn(o * sgate, dxo)
        do_ref[...] = da * sgate
        dgate_ref[...] = da * o * sgate_grad

    row = _rows(tm, D_MODEL)
    mat = _whole((Q_DIM, D_MODEL))
    return pl.pallas_call(
        body, name="attn_out_bwd", grid=(t // tm,),
        in_specs=[row, row, row, mat] + extra_specs, out_specs=[row, row, mat],
        out_shape=[_sds((t, Q_DIM)), _sds((t, BRANCH)), _sds((Q_DIM, D_MODEL))],
        compiler_params=_params(("arbitrary",)),
    )(dxo, o, gate, w_out, *extra)


def attn_proj_bwd(x, norm, dxo, dq, dk, dv, dgate, cos2, sin2, w_in_t):
    t = x.shape[0]
    tm = min(ROW_TILE_BWD, t)

    def body(x_ref, g_ref, dxo_ref, dq_ref, dk_ref, dv_ref, dgate_ref, cos_ref, sin_ref, wt_ref, dx_ref, dw_ref, dg_ref):
        @pl.when(pl.program_id(0) == 0)
        def _():
            dw_ref[...] = jnp.zeros_like(dw_ref)
            dg_ref[...] = jnp.zeros_like(dg_ref)

        g = g_ref[...]
        r, xhat, h = _rms(x_ref[...], g)
        cs = cos_ref[...]
        sn = sin_ref[...]
        dqr = dq_ref[...]
        dkr = dk_ref[...]
        dq = dqr * _tile_lanes(cs, Q_DIM // LANES) + _swap_half_heads(dqr * _tile_lanes(sn, Q_DIM // LANES))
        dk = dkr * cs + _swap_half_heads(dkr * sn)
        dproj = jnp.concatenate([dq, dk, dv_ref[...], dgate_ref[...]], axis=1)
        dh = _mm(dproj, wt_ref[...])
        dw_ref[...] += _mm_tn(dproj, h)
        dx, dg = _rms_bwd(dh, g, r, xhat)
        dg_ref[...] += dg
        dx_ref[...] = dxo_ref[...] + dx

    row = _rows(tm, D_MODEL)
    vec = _whole((1, D_MODEL))
    return pl.pallas_call(
        body, name="attn_proj_bwd", grid=(t // tm,),
        in_specs=[row, vec, row, _rows(tm, Q_DIM), _rows(tm, KV_DIM), _rows(tm, KV_DIM), _rows(tm, BRANCH),
                  _rows(tm, LANES), _rows(tm, LANES), _whole((ATTN_N, D_MODEL))],
        out_specs=[row, _whole((ATTN_N, D_MODEL)), vec],
        out_shape=[_sds((t, D_MODEL)), _sds((ATTN_N, D_MODEL)), _sds((1, D_MODEL))],
        compiler_params=_params(("arbitrary",)),
    )(x, norm, dxo, dq, dk, dv, dgate, cos2, sin2, w_in_t)


def attn_out_loss(x, o, gate, w_out, norm, target):
    t = x.shape[0]
    tm = min(ROW_TILE_FWD, t)

    def body(x_ref, o_ref, gate_ref, w_ref, g_ref, tgt_ref, loss_ref, dx_ref, dg_ref):
        @pl.when(pl.program_id(0) == 0)
        def _():
            loss_ref[...] = jnp.zeros_like(loss_ref)
            dg_ref[...] = jnp.zeros_like(dg_ref)

        out = x_ref[...] + _mm(o_ref[...] * _silu(gate_ref[...]), w_ref[...])
        g = g_ref[...]
        r, xhat, y = _rms(out, g)
        err = y - tgt_ref[...]
        loss_ref[...] += 0.5 * jnp.sum(jnp.mean(err * err, axis=-1, keepdims=True), axis=0, keepdims=True)
        dx, dg = _rms_bwd(err * (1.0 / D_MODEL), g, r, xhat)
        dg_ref[...] += dg
        dx_ref[...] = dx

    row = _rows(tm, D_MODEL)
    vec = _whole((1, D_MODEL))
    return pl.pallas_call(
        body, name="attn_out_loss", grid=(t // tm,),
        in_specs=[row, row, row, _whole((Q_DIM, D_MODEL)), vec, row], out_specs=[_whole((1, 1)), row, vec],
        out_shape=[_sds((1, 1)), _sds((t, D_MODEL)), _sds((1, D_MODEL))],
        compiler_params=_params(("arbitrary",)),
    )(x, o, gate, w_out, norm, target)


OCT_TILE = pl.BlockSpec((None, LANES, LANES), lambda b: (b, 0, 0))
N_LAGS = S5_CHUNK + 1


def _cmul(ar, ai, br, bi):
    return ar * br - ai * bi, ar * bi + ai * br


def _cmul_conj(ar, ai, br, bi):
    return ar * br + ai * bi, ar * bi - ai * br


def _mm_f32(a, b, dims):
    return lax.dot_general(a, b, (dims, ((), ())), precision=lax.Precision.HIGH, preferred_element_type=F32)


def _s5_discretise(ar, ai, ls, br, bi):
    dt = jnp.exp(ls)
    xr = ar * dt
    xi = ai * dt
    mag = jnp.exp(xr)
    first = (mag * jnp.cos(xi), mag * jnp.sin(xi))
    powers = [(jnp.ones_like(xr), jnp.zeros_like(xr)), first]
    for _ in range(2, N_LAGS):
        powers.append(_cmul(*powers[-1], *first))
    den = ar * ar + ai * ai
    nr = powers[1][0] - 1.0
    ni = powers[1][1]
    fr = (nr * ar + ni * ai) / den
    fi = (ni * ar - nr * ai) / den
    bbr, bbi = _cmul(fr, fi, br, bi)
    return dt, powers, (fr, fi), (bbr, bbi), den


def _same_group_tile():
    row = lax.broadcasted_iota(jnp.int32, (LANES, LANES), 0)
    col = lax.broadcasted_iota(jnp.int32, (LANES, LANES), 1)
    return (row // SSM_GROUP) == (col // SSM_GROUP)


def _first_copy_lanes():
    return lax.broadcasted_iota(jnp.int32, (LANES, LANES), 1) < SSM_STATE


def s5_param_fwd(tiles, token=None):
    extra, extra_specs = _after(token)

    def body(ar_ref, ai_ref, ls_ref, br_ref, bi_ref, cr_ref, ci_ref, *rest):
        kd_ref, wsr_ref, wsi_ref, wor_ref, woi_ref, pr_ref, pi_ref = rest[-7:]
        cr = cr_ref[...]
        ci = ci_ref[...]
        _, powers, _, (bbr, bbi), _ = _s5_discretise(ar_ref[...], ai_ref[...], ls_ref[...], br_ref[...], bi_ref[...])
        once = _first_copy_lanes()
        crm = jnp.where(once, cr, 0.0)
        cim = jnp.where(once, ci, 0.0)
        same = _same_group_tile()
        for lag in range(S5_CHUNK):
            er, ei = powers[lag]
            xr, xi = _cmul(er, ei, bbr, bbi)
            rows = pl.ds((S5_CHUNK - 1 - lag) * LANES, LANES)
            wsr_ref[rows, :] = xr
            wsi_ref[rows, :] = xi
        k = _mm_f32(wsr_ref[...], crm, ((1,), (1,))) - _mm_f32(wsi_ref[...], cim, ((1,), (1,)))
        for lag in range(S5_CHUNK):
            kd_ref[lag] = jnp.where(same, k[(S5_CHUNK - 1 - lag) * LANES:(S5_CHUNK - lag) * LANES], 0.0)
        for t in range(S5_CHUNK):
            er, ei = powers[t + 1]
            zr, zi = _cmul(er, ei, cr, ci)
            wor_ref[pl.ds(t * LANES, LANES), :] = zr
            woi_ref[pl.ds(t * LANES, LANES), :] = -zi
        pr_ref[...] = powers[S5_CHUNK][0]
        pi_ref[...] = powers[S5_CHUNK][1]

    return pl.pallas_call(
        body, name="s5_param_fwd", grid=(S5_OCTETS,),
        in_specs=[OCT_TILE] * 7 + [ANY] * len(extra),
        out_specs=[OCT_KD, OCT_W, OCT_W, OCT_W, OCT_W, OCT_TILE, OCT_TILE],
        out_shape=[_sds((S5_OCTETS, S5_CHUNK, LANES, LANES))] + [_sds((S5_OCTETS, S5_OCT_IN, LANES))] * 4
                  + [_sds((S5_OCTETS, LANES, LANES))] * 2,
        compiler_params=_params(("parallel",)),
    )(*tiles, *extra)


def s5_param_bwd(tiles, dkd, dws_re, dws_im, dwo_re, dwo_im, dp_re, dp_im):
    def body(ar_ref, ai_ref, ls_ref, br_ref, bi_ref, cr_ref, ci_ref, dkd_ref, dwsr_ref, dwsi_ref, dwor_ref, dwoi_ref, dpr_ref, dpi_ref,
             dar_ref, dai_ref, dls_ref, dbr_ref, dbi_ref, dcr_ref, dci_ref):
        ar = ar_ref[...]
        ai = ai_ref[...]
        br = br_ref[...]
        bi = bi_ref[...]
        cr = cr_ref[...]
        ci = ci_ref[...]
        dt, powers, (fr, fi), (bbr, bbi), den = _s5_discretise(ar, ai, ls_ref[...], br, bi)
        once = _first_copy_lanes()
        crm = jnp.where(once, cr, 0.0)
        cim = jnp.where(once, ci, 0.0)
        same = _same_group_tile()
        zero = jnp.zeros((LANES, LANES), F32)
        dpow = [[zero, zero] for _ in range(N_LAGS)]
        dbbr, dbbi = zero, zero
        by_step = [S5_CHUNK - 1 - s for s in range(S5_CHUNK)]
        x_all = [_cmul(*powers[lag], bbr, bbi) for lag in by_step]
        xr_all = jnp.concatenate([x[0] for x in x_all], axis=0)
        xi_all = jnp.concatenate([x[1] for x in x_all], axis=0)
        g_all = jnp.concatenate([jnp.where(same, dkd_ref[lag], 0.0) for lag in by_step], axis=0)
        dxr_all = dwsr_ref[...] + _mm_f32(g_all, crm, ((1,), (0,)))
        dxi_all = dwsi_ref[...] - _mm_f32(g_all, cim, ((1,), (0,)))
        dcr = jnp.where(once, _mm_f32(g_all, xr_all, ((0,), (0,))), 0.0)
        dci = -jnp.where(once, _mm_f32(g_all, xi_all, ((0,), (0,))), 0.0)
        for lag in range(S5_CHUNK):
            er, ei = powers[lag]
            rows = slice((S5_CHUNK - 1 - lag) * LANES, (S5_CHUNK - lag) * LANES)
            dxr = dxr_all[rows]
            dxi = dxi_all[rows]
            a, b = _cmul_conj(bbr, bbi, dxr, dxi)
            dpow[lag][0] = dpow[lag][0] + a
            dpow[lag][1] = dpow[lag][1] + b
            a, b = _cmul_conj(er, ei, dxr, dxi)
            dbbr = dbbr + a
            dbbi = dbbi + b
        for t in range(S5_CHUNK):
            er, ei = powers[t + 1]
            dzr = dwor_ref[pl.ds(t * LANES, LANES), :]
            dzi = -dwoi_ref[pl.ds(t * LANES, LANES), :]
            a, b = _cmul_conj(cr, ci, dzr, dzi)
            dpow[t + 1][0] = dpow[t + 1][0] + a
            dpow[t + 1][1] = dpow[t + 1][1] + b
            a, b = _cmul_conj(er, ei, dzr, dzi)
            dcr = dcr + a
            dci = dci + b
        dpow[S5_CHUNK][0] = dpow[S5_CHUNK][0] + dpr_ref[...]
        dpow[S5_CHUNK][1] = dpow[S5_CHUNK][1] + dpi_ref[...]
        dfr, dfi = _cmul_conj(br, bi, dbbr, dbbi)
        dbr, dbi = _cmul_conj(fr, fi, dbbr, dbbi)
        dnr, dni = _cmul(ar / den, ai / den, dfr, dfi)
        qr = (fr * ar + fi * ai) / den
        qi = (fi * ar - fr * ai) / den
        dlr, dli = _cmul(-qr, qi, dfr, dfi)
        dpow[1][0] = dpow[1][0] + dnr
        dpow[1][1] = dpow[1][1] + dni
        dxr, dxi = zero, zero
        for lag in range(1, N_LAGS):
            a, b = _cmul_conj(powers[lag][0], powers[lag][1], dpow[lag][0], dpow[lag][1])
            dxr = dxr + lag * a
            dxi = dxi + lag * b
        dar_ref[...] = dlr + dt * dxr
        dai_ref[...] = dli + dt * dxi
        dls_ref[...] = dt * (ar * dxr + ai * dxi)
        dbr_ref[...] = dbr
        dbi_ref[...] = dbi
        dcr_ref[...] = dcr
        dci_ref[...] = dci

    return pl.pallas_call(
        body, name="s5_param_bwd", grid=(S5_OCTETS,),
        in_specs=[OCT_TILE] * 7 + [OCT_KD, OCT_W, OCT_W, OCT_W, OCT_W, OCT_TILE, OCT_TILE], out_specs=[OCT_TILE] * 7,
        out_shape=[_sds((S5_OCTETS, LANES, LANES))] * 7,
        compiler_params=_params(("parallel",)),
    )(*tiles, dkd, dws_re, dws_im, dwo_re, dwo_im, dp_re, dp_im)


def _doubled(v):
    return jnp.concatenate([v, v], axis=-1)


def _s5_param_tiles(a_re, a_im, log_step, b_re, b_im, c_re, c_im):
    def per_group(a):
        return _doubled(jnp.broadcast_to(a.reshape(S5_OCTETS, S5_OCT, 1, SSM_STATE),
                                         (S5_OCTETS, S5_OCT, SSM_GROUP, SSM_STATE)).reshape(S5_OCTETS, LANES, SSM_STATE))

    ls = jnp.broadcast_to(log_step.reshape(S5_OCTETS, S5_OCT, 1, 1), (S5_OCTETS, S5_OCT, SSM_GROUP, LANES)).reshape(S5_OCTETS, LANES, LANES)
    bt = lambda b: _doubled(b.transpose(0, 2, 1).reshape(S5_OCTETS, LANES, SSM_STATE))
    ct = lambda c: _doubled(c.reshape(S5_OCTETS, LANES, SSM_STATE))
    return [per_group(a_re), per_group(a_im), ls, bt(b_re), bt(b_im), ct(c_re), ct(c_im)]


def _s5_param_grads(dtiles):
    dar, dai, dls, dbr, dbi, dcr, dci = dtiles
    halves = lambda d: d[..., :SSM_STATE] + d[..., SSM_STATE:]
    per_group = lambda d: halves(d).reshape(SSM_GROUPS, SSM_GROUP, SSM_STATE).sum(axis=1)
    per_row = lambda d: halves(d).reshape(SSM_GROUPS, SSM_GROUP, SSM_STATE)
    return (per_group(dar), per_group(dai), dls.reshape(SSM_GROUPS, SSM_GROUP * LANES).sum(axis=1),
            per_row(dbr).transpose(0, 2, 1), per_row(dbi).transpose(0, 2, 1), per_row(dcr), per_row(dci))


def _group_power_rows(tile):
    return tile[:, ::SSM_GROUP, :SSM_STATE].reshape(1, S5_STATES)


def _group_power_tiles(row):
    t = jnp.pad(row.reshape(S5_OCTETS, S5_OCT, 1, SSM_STATE), ((0, 0), (0, 0), (0, SSM_GROUP - 1), (0, LANES - SSM_STATE)))
    return t.reshape(S5_OCTETS, LANES, LANES)


def _rope_tables(t):
    pos = jnp.arange(t, dtype=F32)
    inv_freq = ROPE_THETA ** (-jnp.arange(0, HEAD_DIM, 2, dtype=F32) / HEAD_DIM)
    ang = pos[:, None] * inv_freq[None, :]
    cos = jnp.cos(ang)
    sin = jnp.sin(ang)
    cos64 = jnp.concatenate([cos, cos], axis=1)
    sin64 = jnp.concatenate([-sin, sin], axis=1)
    return jnp.concatenate([cos64, cos64], axis=1), jnp.concatenate([sin64, sin64], axis=1)


def _row(v):
    return v.reshape(1, -1)


def _s5_matrices(w, token=None):
    tiles = _s5_param_tiles(w["a_re"], w["a_im"], w["log_step"], w["b_re"], w["b_im"], w["c_re"], w["c_im"])
    kd, ws_re, ws_im, wo_re, wo_im, p_re, p_im = s5_param_fwd(tiles, token)
    return tiles, dict(kd=kd, ws_re=ws_re, ws_im=ws_im, wo_re=wo_re, wo_im=wo_im, a_re=_group_power_rows(p_re), a_im=_group_power_rows(p_im))


def _ssm_forward(x, w):
    tiles, mats = w["s5"] if "s5" in w else _s5_matrices(w)
    u, gate = ssm_proj_fwd(x, _row(w["norm"]), w["w_in"])
    s_re, s_im = s5_chunk_states(u, mats["ws_re"], mats["ws_im"])
    h_re, h_im = s5_scan_fwd(s_re, s_im, mats["a_re"], mats["a_im"])
    y_scan = s5_outputs(u, h_re, h_im, mats["kd"], mats["wo_re"], mats["wo_im"])
    y, g2, x_new = ssm_mix_fwd(x, u, gate, y_scan, _row(w["d"]), w["w_glu"], _row(w["b_glu"]), w["w_out"])
    saved = dict(x=x, u=u, gate=gate, y=y, g2=g2, h_re=h_re, h_im=h_im, mats=mats, tiles=tiles)
    return x_new, saved


def _ssm_backward(dxo, w, s, token=None, early=None):
    dy, dgate, dw_out, dw_glu, db_glu, dd = ssm_mix_bwd(dxo, s["u"], s["gate"], s["y"], s["g2"], w["w_glu"], w["w_out"], token)
    mats = s["mats"]
    started = early(dict(w_glu=dw_glu, w_out=dw_out)) if early else None
    dh_re, dh_im = s5_state_grads(dy, mats["wo_re"], mats["wo_im"], started)
    ds_re, ds_im, da_re, da_im = s5_scan_bwd(dh_re, dh_im, s["h_re"], s["h_im"], mats["a_re"], mats["a_im"])
    du_scan = s5_input_grads(dy, ds_re, ds_im, mats["kd"], mats["ws_re"], mats["ws_im"])
    dkd, dws_re, dws_im, dwo_re, dwo_im = s5_weight_grads(s["u"], dy, s["h_re"], s["h_im"], ds_re, ds_im)
    dparams = _s5_param_grads(s5_param_bwd(s["tiles"], dkd, dws_re, dws_im, dwo_re, dwo_im,
                                           _group_power_tiles(da_re), _group_power_tiles(da_im)))
    dx, dw_in, dnorm = ssm_proj_bwd(s["x"], _row(w["norm"]), dxo, dy, du_scan, dgate, _row(w["d"]), w["w_in"])
    grads = dict(norm=dnorm, w_in=dw_in, d=dd, w_glu=dw_glu, b_glu=db_glu, w_out=dw_out)
    for name, val in zip(("a_re", "a_im", "log_step", "b_re", "b_im", "c_re", "c_im"), dparams):
        grads[name] = val
    return dx, grads


def _attn_forward(x, w, cos2, sin2, loss_head=None):
    q, k, v, gate = attn_proj_fwd(x, _row(w["norm"]), w["w_in"], cos2, sin2)
    o, lse = attn_fwd(q, k, v, w["sinks"])
    if loss_head is None:
        result = attn_out_fwd(x, o, gate, w["w_out"])
    else:
        result = attn_out_loss(x, o, gate, w["w_out"], _row(loss_head[0]), loss_head[1])
    return result, dict(x=x, q=q, k=k, v=v, gate=gate, o=o, lse=lse)


def _attn_backward(dxo, w, s, cos2, sin2, token=None):
    do, dgate, dw_out = attn_out_bwd(dxo, s["o"], s["gate"], w["w_out"], token)
    dq, dk, dv, dsinks = attn_bwd(s["q"], s["k"], s["v"], w["sinks"], s["o"], s["lse"], do)
    dx, dw_in, dnorm = attn_proj_bwd(s["x"], _row(w["norm"]), dxo, dq, dk, dv, dgate, cos2, sin2, w["w_in"])
    return dx, dict(norm=dnorm, w_in=dw_in, sinks=dsinks[0, :N_Q_HEADS], w_out=dw_out)


class _NoExchanges:
    def __init__(self, layers):
        self.layers = layers

    def layer(self, i, x):
        return self.layers[i]

    def early_grads(self, i, grads):
        return None

    def layer_done(self, i, grads, dx):
        return None


def _sequence_step(x, target, final_norm, hooks, depth=4):
    cos2, sin2 = _rope_tables(x.shape[0])
    saved, layers = [], []
    for i in range(depth):
        w = hooks.layer(i, x)
        layers.append(w)
        if i % 2 == 0:
            x, s = _ssm_forward(x, w)
        else:
            x, s = _attn_forward(x, w, cos2, sin2, (final_norm, target) if i == depth - 1 else None)
        saved.append(s)
    loss, dx, dfinal = x
    grads = {"final_norm": dfinal}
    token = None
    for i in reversed(range(depth)):
        if i % 2 == 0:
            dx, g = _ssm_backward(dx, layers[i], saved[i], token, functools.partial(hooks.early_grads, i))
        else:
            dx, g = _attn_backward(dx, layers[i], saved[i], cos2, sin2, token)
        g = {"l%d_%s" % (i, name): val for name, val in g.items()}
        grads.update(g)
        token = hooks.layer_done(i, g, dx)
    return loss[0, 0], dx, grads


ANY = pl.BlockSpec(memory_space=pl.ANY)


def _place():
    return lax.axis_index("x"), lax.axis_index("y"), lax.axis_index("c")


def _other_chips(x, y):
    return [(1 - x, y), (x, 1 - y), (1 - x, 1 - y)]


class _StagedCopies:
    def __init__(self, bufs, load_sems, store_sems):
        self.bufs, self.load_sems, self.store_sems = bufs, load_sems, store_sems
        self.loads, self.stores = [], []

    def load(self, i, src):
        cp = pltpu.make_async_copy(src, self.bufs[i], self.load_sems.at[i])
        cp.start()
        self.loads.append(cp)

    def store(self, i, dst):
        self.loads[i].wait()
        cp = pltpu.make_async_copy(self.bufs[i], dst, self.store_sems.at[i])
        cp.start()
        self.stores.append(cp)

    def finish(self):
        for cp in self.stores:
            cp.wait()


def _staging(blocks):
    n = len(blocks)
    return [pltpu.VMEM(b.shape, b.dtype) for b in blocks] + [pltpu.SemaphoreType.DMA((n,)), pltpu.SemaphoreType.DMA((n,))]


def exchange_halves_with_sibling(grads):
    n = len(grads)

    def body(*refs):
        ins, outs = refs[:n], refs[n:2 * n]
        send_sems, recv_sems = refs[2 * n:]
        x, y, c = _place()
        copies = []
        for i in range(n):
            half = ins[i].shape[1] // 2
            src = ins[i].at[:, pl.ds((1 - c) * half, half), :]
            cp = pltpu.make_async_remote_copy(src_ref=src, dst_ref=outs[i], send_sem=send_sems.at[i], recv_sem=recv_sems.at[i],
                                              device_id=(x, y, 1 - c), device_id_type=MESH)
            cp.start()
            copies.append(cp)
        for cp in copies:
            cp.wait()

    return pl.pallas_call(
        body, name="exchange_halves_with_sibling",
        in_specs=[ANY] * n, out_specs=[ANY] * n,
        out_shape=[_sds((g.shape[0], g.shape[1] // 2, g.shape[2])) for g in grads],
        scratch_shapes=[pltpu.SemaphoreType.DMA((n,)), pltpu.SemaphoreType.DMA((n,))],
    )(*grads)


def swap_halves_with_sibling(pieces):
    n = len(pieces)

    def body(*refs):
        ins, outs = refs[:n], refs[n:2 * n]
        send_sems, recv_sems = refs[2 * n:2 * n + 2]
        own = _StagedCopies(refs[2 * n + 2:3 * n + 2], *refs[3 * n + 2:])
        x, y, c = _place()
        for i in range(n):
            own.load(i, ins[i])
        swaps = []
        for i in range(n):
            cp = pltpu.make_async_remote_copy(src_ref=ins[i], dst_ref=outs[i].at[c], send_sem=send_sems.at[i], recv_sem=recv_sems.at[i],
                                              device_id=(x, y, 1 - c), device_id_type=MESH)
            cp.start()
            swaps.append(cp)
        for i in range(n):
            own.store(i, outs[i].at[c])
        for i in range(n):
            pltpu.make_async_remote_copy(src_ref=ins[i], dst_ref=outs[i].at[1 - c], send_sem=send_sems.at[i], recv_sem=recv_sems.at[i],
                                         device_id=(x, y, 1 - c), device_id_type=MESH).wait_recv()
        for cp in swaps:
            cp.wait_send()
        own.finish()

    return pl.pallas_call(
        body, name="swap_halves_with_sibling",
        in_specs=[ANY] * n, out_specs=[ANY] * n,
        out_shape=[_sds((2,) + p.shape) for p in pieces],
        scratch_shapes=[pltpu.SemaphoreType.DMA((n,)), pltpu.SemaphoreType.DMA((n,))] + _staging(pieces),
        compiler_params=_params(),
    )(*pieces)


def pass_halves_to_sibling(stacks):
    n = len(stacks)

    def body(*refs):
        outs = refs[n:2 * n]
        send_sems, recv_sems = refs[2 * n:]
        x, y, c = _place()
        sends = []
        for i in range(n):
            for k, (tx, ty) in enumerate(_other_chips(x, y)):
                mine = _rows_of_core(outs[i].at[2 * tx + ty], c, True)
                cp = pltpu.make_async_remote_copy(src_ref=mine, dst_ref=mine, send_sem=send_sems.at[i, k], recv_sem=recv_sems.at[i, k],
                                                  device_id=(x, y, 1 - c), device_id_type=MESH)
                cp.start()
                sends.append(cp)
        for i in range(n):
            for k, (tx, ty) in enumerate(_other_chips(x, y)):
                missing = _rows_of_core(outs[i].at[2 * tx + ty], 1 - c, True)
                pltpu.make_async_remote_copy(src_ref=missing, dst_ref=missing, send_sem=send_sems.at[i, k], recv_sem=recv_sems.at[i, k],
                                             device_id=(x, y, 1 - c), device_id_type=MESH).wait_recv()
        for cp in sends:
            cp.wait_send()

    sems = pltpu.SemaphoreType.DMA((n, 3))
    return pl.pallas_call(
        body, name="pass_halves_to_sibling", in_specs=[ANY] * n, out_specs=[ANY] * n,
        out_shape=[_sds(s.shape, s.dtype) for s in stacks], input_output_aliases={i: i for i in range(n)},
        scratch_shapes=[sems, sems],
    )(*stacks)


IN_HBM = pl.BlockSpec(memory_space=pltpu.HBM)
SEMAPHORES = pl.BlockSpec(memory_space=pltpu.SEMAPHORE)
DATAFLOW = pltpu.SideEffectType.DATAFLOW_SIDE_EFFECTING


def _hbm(a):
    return pltpu.with_memory_space_constraint(a, pltpu.HBM)


def place_own_blocks(shards):
    n = len(shards)

    def body(*refs):
        ins, outs = refs[:n], refs[n:2 * n]
        own = _StagedCopies(refs[2 * n:3 * n], *refs[3 * n:])
        x, y, _ = _place()
        for i in range(n):
            own.load(i, ins[i])
        for i in range(n):
            own.store(i, outs[i].at[2 * x + y])
        own.finish()

    return pl.pallas_call(
        body, name="place_own_blocks", in_specs=[ANY] * n, out_specs=[ANY] * n,
        out_shape=[_sds((4,) + s.shape, s.dtype) for s in shards],
        scratch_shapes=_staging(shards), compiler_params=_params(),
    )(*shards)


def _block_to_send(ref, chip, per_target):
    if not per_target:
        return ref
    return ref.at[chip] if ref.shape[0] == 4 else ref.at[0]


def _rows_of_core(ref, c, core_half):
    if not core_half:
        return ref
    rows = ref.shape[0] // 2
    return ref.at[pl.ds(c * rows, rows), :]


def start_sends_to_chips(name, sources, landings, per_target, after, core_half=False):
    n = len(sources)
    n_sems = 2 * 3 * n

    def body(*refs):
        srcs = refs[:n]
        sems = refs[2 * n + 1:2 * n + 1 + n_sems]
        lands = refs[2 * n + 1 + n_sems:3 * n + 1 + n_sems]
        token = refs[3 * n + 1 + n_sems]
        x, y, c = _place()
        me = 2 * x + y
        for i in range(n):
            for k, (tx, ty) in enumerate(_other_chips(x, y)):
                src = _rows_of_core(_block_to_send(srcs[i], 2 * tx + ty, per_target), c, core_half)
                dst = _rows_of_core(lands[i].at[me], c, core_half)
                pltpu.make_async_remote_copy(src_ref=src, dst_ref=dst, send_sem=sems[2 * (3 * i + k)], recv_sem=sems[2 * (3 * i + k) + 1],
                                             device_id=(tx, ty, c), device_id_type=MESH).start()
        token[...] = jnp.zeros_like(token)

    outs = pl.pallas_call(
        body, name=name,
        in_specs=[IN_HBM] * (2 * n) + [ANY],
        out_specs=[SEMAPHORES] * n_sems + [IN_HBM] * n + [pl.BlockSpec(memory_space=pltpu.VMEM)],
        out_shape=[pltpu.SemaphoreType.DMA(())] * n_sems + [pltpu.HBM(l.shape, l.dtype) for l in landings] + [_sds(TOKEN_SHAPE)],
        input_output_aliases={n + i: n_sems + i for i in range(n)},
        compiler_params=pltpu.CompilerParams(has_side_effects=DATAFLOW),
    )(*[_hbm(s) for s in sources], *[_hbm(l) for l in landings], after)
    return list(outs[:n_sems]), list(outs[n_sems:n_sems + n]), outs[n_sems + n]


def wait_sends_to_chips(name, sources, landings, sems, per_target, after, core_half=False):
    n = len(sources)
    n_sems = len(sems)

    def body(*refs):
        srcs = refs[:n]
        sem_refs = refs[2 * n:2 * n + n_sems]
        lands = refs[2 * n + n_sems + 1:]
        x, y, c = _place()
        me = 2 * x + y
        for i in range(n):
            for k, (tx, ty) in enumerate(_other_chips(x, y)):
                src = _rows_of_core(_block_to_send(srcs[i], me, per_target), c, core_half)
                dst = _rows_of_core(lands[i].at[2 * tx + ty], c, core_half)
                cp = pltpu.make_async_remote_copy(src_ref=src, dst_ref=dst, send_sem=sem_refs[2 * (3 * i + k)],
                                                  recv_sem=sem_refs[2 * (3 * i + k) + 1], device_id=(tx, ty, c), device_id_type=MESH)
                cp.wait_send()
                cp.wait_recv()

    return pl.pallas_call(
        body, name=name,
        in_specs=[IN_HBM] * (2 * n) + [SEMAPHORES] * n_sems + [ANY],
        out_specs=[IN_HBM] * n,
        out_shape=[pltpu.HBM(l.shape, l.dtype) for l in landings],
        input_output_aliases={n + i: i for i in range(n)},
        compiler_params=pltpu.CompilerParams(has_side_effects=DATAFLOW),
    )(*[_hbm(s) for s in sources], *landings, *sems, after)


def _row_tile(rows, cols):
    tm = rows
    while tm * cols * 4 > (2 << 20) and tm % 16 == 0:
        tm //= 2
    return tm


def add_pairs(half, a_list, b_list, out_dtypes, copies=1):
    n = len(a_list)
    nb = a_list[0].shape[0]

    def body(half_ref, *refs):
        for i in range(n):
            total = (refs[i][...] + refs[n + i][...]).astype(out_dtypes[i])
            for o_ref in refs[2 * n + i * copies:2 * n + (i + 1) * copies]:
                o_ref[...] = total

    halves = [pl.BlockSpec((None,) + b.shape[1:], lambda j, h: (j, h[0], 0)) for b in b_list]
    whole = [pl.BlockSpec((None,) + b.shape[1:], lambda j, h: (j, 0, 0)) for b in b_list]
    outs = pl.pallas_call(
        body, name="add_pairs",
        grid_spec=pltpu.PrefetchScalarGridSpec(num_scalar_prefetch=1, grid=(nb,), in_specs=halves + whole,
                                               out_specs=[s for s in whole for _ in range(copies)]),
        out_shape=[_sds(b.shape, dt) for b, dt in zip(b_list, out_dtypes) for _ in range(copies)],
        compiler_params=_params(("parallel",)),
    )(half, *a_list, *b_list)
    return [tuple(outs[i * copies:(i + 1) * copies]) for i in range(n)]


def sum_fours(arrays, token=None):
    n = len(arrays)
    extra, extra_specs = _after(token)
    steps = 2 if all(a.shape[1] % 32 == 0 for a in arrays) else 1

    def body(*refs):
        outs = refs[-n:]
        for a_ref, o_ref in zip(refs[:n], outs):
            o_ref[...] = ((a_ref[0].astype(F32) + a_ref[1].astype(F32)) + a_ref[2].astype(F32)) + a_ref[3].astype(F32)

    return pl.pallas_call(
        body, name="sum_fours", grid=(steps,),
        in_specs=[pl.BlockSpec((4, a.shape[1] // steps, a.shape[2]), lambda i: (0, i, 0)) for a in arrays] + extra_specs,
        out_specs=[pl.BlockSpec((a.shape[1] // steps, a.shape[2]), lambda i: (i, 0)) for a in arrays],
        out_shape=[_sds(a.shape[1:]) for a in arrays], compiler_params=_params(("parallel",)),
    )(*arrays, *extra)


def _adamw_update(w_ref, g_ref, m_ref, v_ref, d_ref, nm_ref, nv_ref):
    g = g_ref[...]
    nm = ADAM_B1 * m_ref[...] + (1.0 - ADAM_B1) * g
    nv = ADAM_B2 * v_ref[...] + (1.0 - ADAM_B2) * (g * g)
    d_ref[...] = -ADAM_LR * ((nm / (1.0 - ADAM_B1 ** ADAM_STEP)) / (jnp.sqrt(nv / (1.0 - ADAM_B2 ** ADAM_STEP)) + ADAM_EPS) + ADAM_WD * w_ref[...])
    nm_ref[...] = nm
    nv_ref[...] = nv


def adamw(w, g, m, v):
    rows, cols = w.shape
    tm = _row_tile(rows, cols)

    def body(*refs):
        _adamw_update(*refs)

    spec = pl.BlockSpec((tm, cols), lambda i: (i, 0))
    return pl.pallas_call(
        body, name="adamw", grid=(rows // tm,), in_specs=[spec] * 4, out_specs=[spec] * 3,
        out_shape=[_sds(w.shape)] * 3, compiler_params=_params(("parallel",)),
    )(w, g, m, v)


def adamw_small(ws, gs, ms, vs, slabs=None):
    n = len(ws)

    def body(*refs):
        for i in range(n):
            _adamw_update(refs[i], refs[n + i], refs[2 * n + i], refs[3 * n + i], refs[4 * n + i], refs[5 * n + i], refs[6 * n + i])

    if slabs is None:
        grid = ()
        specs = [pl.BlockSpec(memory_space=pltpu.VMEM)] * n
    else:
        grid = (slabs,)
        specs = [pl.BlockSpec((w.shape[0] // slabs,) + w.shape[1:], lambda i: (i, 0, 0)) for w in ws]
    outs = pl.pallas_call(
        body, name="adamw_small", grid=grid, in_specs=specs * 4, out_specs=specs * 3,
        out_shape=[_sds(w.shape) for w in ws] * 3, compiler_params=_params(("parallel",) if slabs else None),
    )(*ws, *gs, *ms, *vs)
    return outs[:n], outs[n:2 * n], outs[2 * n:]


PACK_TILE = 8 * LANES
PACK_PIECES = 8
PACK_ALIGN = PACK_PIECES * 16


def _pack_small(values, scalar=None):
    parts = []
    for name in PACK_NAMES:
        flat = values[name].reshape(-1)
        pad = (-flat.shape[0]) % PACK_TILE
        if pad:
            flat = jnp.concatenate([flat, jnp.zeros((pad,), F32)])
        parts.append(flat.reshape(-1, LANES))
    rows = sum(p.shape[0] for p in parts) + 8
    parts.append(jnp.zeros(((-rows) % PACK_ALIGN, LANES), F32))
    last = jnp.zeros((8, LANES), F32)
    parts.append(last if scalar is None else jnp.broadcast_to(scalar.astype(F32), (8, LANES)))
    return jnp.concatenate(parts, axis=0)


def _pack_row_of(name, like):
    row = 0
    for other in PACK_NAMES:
        if other == name:
            return row
        row += -(-math.prod(like[other].shape) // PACK_TILE) * 8
    raise KeyError(name)


def _unpack_small(pack, like):
    out = {}
    row = 0
    for name in PACK_NAMES:
        size = math.prod(like[name].shape)
        rows = -(-size // PACK_TILE) * 8
        out[name] = pack[row:row + rows].reshape(-1)[:size].reshape(like[name].shape)
        row += rows
    return out


def _travels_transposed(name, shard):
    return name.endswith("w_in") and shard.shape[-1] % LANES != 0


def _to_blocks(name, full):
    if full.ndim == 3:
        return full
    return full.reshape(4, full.shape[0] // 4, full.shape[1])


def _from_blocks(name, stacked):
    if name.endswith("w_in") and stacked.shape[2] % LANES == 0 and stacked.shape[1] == D_MODEL:
        return stacked
    return stacked.reshape(4 * stacked.shape[1], stacked.shape[2])


def _layer_big_names(i):
    return [n for n in BIG_NAMES if n.startswith("l%d_" % i)]


class _OverlappedExchanges:
    def __init__(self, weights):
        self.weights = weights
        self.c = lax.axis_index("c")
        self.first = _layer_big_names(0)
        self.later = [n for n in BIG_NAMES if n not in self.first]
        shards = [weights[n].astype(MXU_DTYPE) for n in self.first + self.later]
        shards = [s.T if _travels_transposed(n, s) else s for n, s in zip(self.first + self.later, shards)]
        placed = place_own_blocks(shards)
        k = len(self.first)
        sems, stacks, token = start_sends_to_chips("gather_first_start", shards[:k], placed[:k], False, shards[0], core_half=True)
        self.gather_first = (shards[:k], sems, stacks)
        sems, stacks, token = start_sends_to_chips("gather_later_start", shards[k:], placed[k:], False, token)
        self.gather_later = (shards[k:], sems, stacks)
        self.s5 = {}
        for i in (0, 2):
            self.s5[i] = _s5_matrices({n: weights["l%d_%s" % (i, n)] for n in SSM_NAMES if "l%d_%s" % (i, n) in SMALL_NAMES}, token)
            token = self.s5[i][1]["kd"]
        self.full = {}
        self.held = {}
        self.in_flight = {}
        self.contributions = {}

    def layer(self, i, x):
        if i == 0:
            shards, sems, stacks = self.gather_first
            stacks = wait_sends_to_chips("gather_first_wait", shards, stacks, sems, False, self.s5[2][1]["kd"], core_half=True)
            stacks = pass_halves_to_sibling(stacks)
            self.full.update({n: _from_blocks(n, g) for n, g in zip(self.first, stacks)})
        if i == 1:
            shards, sems, stacks = self.gather_later
            stacks = wait_sends_to_chips("gather_later_wait", shards, stacks, sems, False, x)
            self.full.update({n: _from_blocks(n, g) for n, g in zip(self.later, stacks)})
        names = SSM_NAMES if i % 2 == 0 else ATTN_NAMES
        w = {n: self.full.get("l%d_%s" % (i, n), self.weights.get("l%d_%s" % (i, n))) for n in names}
        if i in self.s5:
            w["s5"] = self.s5[i]
        return w

    def chip_sums(self, names, grads, extra_blocks=(), extra_dtypes=(), copies=1):
        blocks = [_to_blocks(n, grads[n]) for n in names] + list(extra_blocks)
        from_sibling = exchange_halves_with_sibling(blocks)
        k = len(names)
        half = self.c.reshape(1).astype(jnp.int32)
        sums = add_pairs(half, blocks[:k], from_sibling[:k], [WIRE_DTYPE] * k, copies)
        if extra_blocks:
            sums += add_pairs(half, blocks[k:], from_sibling[k:], list(extra_dtypes), copies)
        return sums

    def start_scatter(self, tag, names, grads):
        pairs = self.chip_sums(names, grads, copies=2)
        sums = [p[0] for p in pairs]
        sems, landings, token = start_sends_to_chips("scatter_start_" + tag, sums, [p[1] for p in pairs], True, sums[0])
        self.in_flight[tag] = (names, sums, sems, landings)
        return token

    def wait_scatter(self, tag, after):
        if tag in self.in_flight:
            names, sums, sems, landings = self.in_flight.pop(tag)
            done = wait_sends_to_chips("scatter_wait_" + tag, sums, landings, sems, True, after)
            self.contributions.update(zip(names, done))


    def early_grads(self, i, grads):
        if i != 0:
            return None
        self.held.update({"l0_" + n: g for n, g in grads.items()})
        names = _layer_big_names(1) + ["l0_w_glu", "l0_w_out"]
        return self.start_scatter("l1_l0", names, self.held)

    def layer_done(self, i, grads, dx):
        self.held.update(grads)
        if i == 2:
            return self.start_scatter("l3_l2", _layer_big_names(3) + _layer_big_names(2), self.held)
        if i == 1:
            self.wait_scatter("l3_l2", dx)
        if i == 0:
            self.wait_scatter("l1_l0", dx)
        return None


def _train_step(x, loss_target, weights, moments_m, moments_v):
    hooks = _OverlappedExchanges(weights)
    loss, dx, grads = _sequence_step(x[0], loss_target[0], weights["final_norm"], hooks)
    small_pack = _pack_small({n: grads[n] for n in SMALL_NAMES}, scalar=loss)
    tail = small_pack.shape[0] - _pack_row_of(EXACT_NAMES[0], grads)
    small = [small_pack[None, :-tail], small_pack[None, -tail:]]
    last = [n for n in _layer_big_names(0) if n not in hooks.contributions]
    pairs = hooks.chip_sums(last, grads, extra_blocks=small, extra_dtypes=[WIRE_DTYPE, F32], copies=2)
    sums = [p[0] for p in pairs]
    landings = [p[1] for p in pairs[:-2]] + [jnp.broadcast_to(s, (4,) + s.shape[1:]) for s in sums[-2:]]
    sems, landings, token = start_sends_to_chips("scatter_start_l0", sums, landings, True, sums[0])
    out_grad, out_delta, out_m, out_v = {}, {}, {}, {}

    def finish(names, arrays, token=None):
        shared = swap_halves_with_sibling(sum_fours(arrays, token))
        rest = []
        for n, s in zip(names, shared):
            if n not in weights:
                rest.append(s.reshape(-1, LANES))
                continue
            out_grad[n] = s.reshape(2 * s.shape[1], s.shape[2])
            if _travels_transposed(n, weights[n]):
                out_grad[n] = out_grad[n].T
            out_delta[n], out_m[n], out_v[n] = adamw(weights[n], out_grad[n], moments_m[n], moments_v[n])
        return rest

    others = [n for n in BIG_NAMES if n not in last]
    finish(others, [hooks.contributions[n] for n in others], token)
    arrived = wait_sends_to_chips("scatter_wait_l0", sums, landings, sems, True, out_v[others[-1]])
    small_grad_pack = jnp.concatenate(finish(last + ["small", "small tail"], arrived), axis=0)
    loss = small_grad_pack[-8, 0]
    out_grad.update(_unpack_small(small_grad_pack, {n: weights[n] for n in SMALL_NAMES}))
    cubes = [n for n in SMALL_NAMES if weights[n].ndim == 3]
    for names, slabs in ((cubes, 8), ([n for n in SMALL_NAMES if n not in cubes], None)):
        deltas, new_ms, new_vs = adamw_small(*[[group[n] for n in names] for group in (weights, out_grad, moments_m, moments_v)], slabs=slabs)
        out_delta.update(zip(names, deltas))
        out_m.update(zip(names, new_ms))
        out_v.update(zip(names, new_vs))
    outs = [loss, dx[None]]
    for group in (out_grad, out_delta, out_m, out_v):
        outs.extend(group[n] for n in WEIGHT_NAMES)
    return tuple(outs)


def kernel(x, l0_norm, l0_w_in, l0_a_re, l0_a_im, l0_log_step, l0_b_re, l0_b_im, l0_c_re, l0_c_im, l0_d, l0_w_glu, l0_b_glu, l0_w_out, l1_norm, l1_w_in, l1_sinks, l1_w_out, l2_norm, l2_w_in, l2_a_re, l2_a_im, l2_log_step, l2_b_re, l2_b_im, l2_c_re, l2_c_im, l2_d, l2_w_glu, l2_b_glu, l2_w_out, l3_norm, l3_w_in, l3_sinks, l3_w_out, final_norm, loss_target, m_l0_norm, m_l0_w_in, m_l0_a_re, m_l0_a_im, m_l0_log_step, m_l0_b_re, m_l0_b_im, m_l0_c_re, m_l0_c_im, m_l0_d, m_l0_w_glu, m_l0_b_glu, m_l0_w_out, m_l1_norm, m_l1_w_in, m_l1_sinks, m_l1_w_out, m_l2_norm, m_l2_w_in, m_l2_a_re, m_l2_a_im, m_l2_log_step, m_l2_b_re, m_l2_b_im, m_l2_c_re, m_l2_c_im, m_l2_d, m_l2_w_glu, m_l2_b_glu, m_l2_w_out, m_l3_norm, m_l3_w_in, m_l3_sinks, m_l3_w_out, m_final_norm, v_l0_norm, v_l0_w_in, v_l0_a_re, v_l0_a_im, v_l0_log_step, v_l0_b_re, v_l0_b_im, v_l0_c_re, v_l0_c_im, v_l0_d, v_l0_w_glu, v_l0_b_glu, v_l0_w_out, v_l1_norm, v_l1_w_in, v_l1_sinks, v_l1_w_out, v_l2_norm, v_l2_w_in, v_l2_a_re, v_l2_a_im, v_l2_log_step, v_l2_b_re, v_l2_b_im, v_l2_c_re, v_l2_c_im, v_l2_d, v_l2_w_glu, v_l2_b_glu, v_l2_w_out, v_l3_norm, v_l3_w_in, v_l3_sinks, v_l3_w_out, v_final_norm):
    args = locals()
    weights = {n: args[n] for n in WEIGHT_NAMES}
    moments_m = {n: args["m_" + n] for n in WEIGHT_NAMES}
    moments_v = {n: args["v_" + n] for n in WEIGHT_NAMES}
    return _train_step(x, loss_target, weights, moments_m, moments_v)
```
